```python
import jax, jax.numpy as jnp
from jax import lax
import numpy as np

D_MODEL = 1024
BATCH = 8
SEQ = 4096
DEPTH = 1

HEAD_DIM = 64
RWKV_HEADS = D_MODEL // HEAD_DIM
RWKV_WIDTH = RWKV_HEADS * HEAD_DIM
DECAY_LORA = 64
AAA_LORA = 64
GATE_LORA = 160
RWKV_COLS = 3 * RWKV_WIDTH + DECAY_LORA + AAA_LORA + GATE_LORA
GN_EPS = 64e-5
ATTN_PAIRS = ((128, 1), (512, 4), (2048, 16))
ATTN_GROUPS = 3
ATTN_HEADS_PER_GROUP = 4
ATTN_HEADS = ATTN_GROUPS * ATTN_HEADS_PER_GROUP
ATTN_WIDTH = ATTN_HEADS * HEAD_DIM
N_BRANCHES = 2
IN_COLS = RWKV_COLS + 3 * ATTN_WIDTH + N_BRANCHES * D_MODEL
D_FF = 2816
RMS_EPS = 1e-6
NEG_INF = -1e30

kernel_name = 'hybrid_rwkv7_dilated_attn_macaron'


def _split_last(t, sizes):
    out, start = [], 0
    for n in sizes:
        out.append(t[..., start:start + n])
        start += n
    return out


def rms_norm(x, g):
    xf = x.astype(jnp.float32)
    y = xf * lax.rsqrt(jnp.mean(xf * xf, axis=-1, keepdims=True) + RMS_EPS)
    return y.astype(x.dtype) * g


def swiglu(h, w_in, w_out):
    gate, up = _split_last(h @ w_in, (D_FF, D_FF))
    return (jax.nn.silu(gate) * up) @ w_out


def token_shift(p, mu):
    prev = jnp.pad(p, ((0, 0), (1, 0), (0, 0)))[:, :-1]
    return p + (prev - p) * mu


def wkv7_scan(r, w, k, v, a, b):
    B, S, H, N = r.shape
    to_time = lambda t: jnp.moveaxis(t, 1, 0)

    def step(state, inp):
        r_t, w_t, k_t, v_t, a_t, b_t = inp
        sa = jnp.einsum('bhvk,bhk->bhv', state, a_t)
        state = (state * w_t[:, :, None, :] + sa[..., None] * b_t[:, :, None, :]
                 + v_t[..., None] * k_t[:, :, None, :])
        return state, jnp.einsum('bhvk,bhk->bhv', state, r_t)

    state0 = jnp.zeros((B, H, N, N), jnp.float32)
    _, out = lax.scan(step, state0, (to_time(r), to_time(w), to_time(k),
                                      to_time(v), to_time(a), to_time(b)))
    return jnp.moveaxis(out, 0, 1)


def rwkv7_time_mix(p, mu, w0, w2, a0, a2, g2, k_k, k_a, r_k, ln_w, ln_b):
    B, S, _ = p.shape
    H, N = RWKV_HEADS, HEAD_DIM
    p = token_shift(p.astype(jnp.float32), mu)
    r, k, v, wd, ad, gd = _split_last(
        p, (RWKV_WIDTH, RWKV_WIDTH, RWKV_WIDTH, DECAY_LORA, AAA_LORA, GATE_LORA))
    w = -jax.nn.softplus(-(w0 + jnp.tanh(wd) @ w2)) - 0.5
    decay = jnp.exp(-jnp.exp(w))
    a = jax.nn.sigmoid(a0 + ad @ a2)
    g = jax.nn.sigmoid(gd) @ g2
    heads = lambda t: t.reshape(B, S, H, N)
    kk = heads(k * k_k)
    kk = kk / jnp.maximum(jnp.sqrt(jnp.sum(kk * kk, axis=-1, keepdims=True)), 1e-12)
    k = k * (1.0 + (a - 1.0) * k_a)
    rh, kh, vh, ah = heads(r), heads(k), heads(v), heads(a)
    wkv = wkv7_scan(rh, heads(decay), kh, vh, -kk, kk * ah)
    mean = jnp.mean(wkv, axis=-1, keepdims=True)
    var = jnp.mean(jnp.square(wkv - mean), axis=-1, keepdims=True)
    y = ((wkv - mean) * lax.rsqrt(var + GN_EPS)).reshape(B, S, RWKV_WIDTH) * ln_w + ln_b
    bonus = jnp.sum(rh * kh * r_k, axis=-1, keepdims=True) * vh
    return (y + bonus.reshape(B, S, RWKV_WIDTH)) * g


def dilated_group_attention(q, k, v, window, dilation):
    B, S, Hg, E = q.shape
    span = window // dilation
    blk = span
    sub = S // dilation
    nb = -(-sub // blk)
    pad = nb * blk - sub

    def to_sub(t):
        t = t.reshape(B, sub, dilation, Hg, E).transpose(0, 2, 1, 3, 4)
        t = jnp.pad(t, ((0, 0), (0, 0), (0, pad), (0, 0), (0, 0)))
        return t.reshape(B, dilation, nb, blk, Hg, E)

    def with_prev(t):
        tp = jnp.pad(t, ((0, 0), (0, 0), (1, 0), (0, 0), (0, 0), (0, 0)))
        return jnp.concatenate([tp[:, :, :-1], tp[:, :, 1:]], axis=3)

    qb = to_sub(q)
    kw, vw = with_prev(to_sub(k)), with_prev(to_sub(v))
    s = jnp.einsum('bdnqhe,bdnkhe->bdnhqk', qb, kw)
    qi = jnp.arange(blk)[:, None]
    kj = jnp.arange(2 * blk)[None, :]
    dist = qi + blk - kj
    bidx = jnp.arange(nb)[:, None, None]
    valid = (dist >= 0) & (dist <= span) & (bidx * blk + kj - blk >= 0)
    s = jnp.where(valid[None, None, :, None], s, NEG_INF)
    m = jnp.max(s, axis=-1, keepdims=True)
    pexp = jnp.exp(s - m)
    den = jnp.sum(pexp, axis=-1, keepdims=True)
    o = jnp.einsum('bdnhqk,bdnkhe->bdnqhe', pexp / den, vw)
    lse = (m + jnp.log(den))[..., 0].transpose(0, 1, 2, 4, 3)
    o = o.reshape(B, dilation, nb * blk, Hg, E)[:, :, :sub]
    o = o.transpose(0, 2, 1, 3, 4).reshape(B, S, Hg, E)
    lse = lse.reshape(B, dilation, nb * blk, Hg)[:, :, :sub]
    lse = lse.transpose(0, 2, 1, 3).reshape(B, S, Hg)
    return o, lse


def dilated_attention(pq, pk, pv, q_gain, k_gain):
    B, S, _ = pq.shape
    heads = lambda t: t.astype(jnp.float32).reshape(B, S, ATTN_HEADS, HEAD_DIM)
    q = rms_norm(heads(pq), q_gain) * (HEAD_DIM ** -0.5)
    k = rms_norm(heads(pk), k_gain)
    v = heads(pv)
    outs, lses = [], []
    for gi, (window, dilation) in enumerate(ATTN_PAIRS):
        sl = slice(gi * ATTN_HEADS_PER_GROUP, (gi + 1) * ATTN_HEADS_PER_GROUP)
        o, lse = dilated_group_attention(q[:, :, sl], k[:, :, sl], v[:, :, sl], window, dilation)
        outs.append(o)
        lses.append(lse)
    o = jnp.stack(outs, axis=2)
    alpha = jax.nn.softmax(jnp.stack(lses, axis=2), axis=2)
    return (o * alpha[..., None]).reshape(B, S, ATTN_WIDTH)


def _fwd_setup_inputs(seed: int = 0) -> dict:
    key = jax.random.key(seed)
    ks = jax.random.split(key, 27)
    f32 = jnp.float32
    L, D = DEPTH, D_MODEL

    def nrm(k, shape, scale):
        return scale * jax.random.normal(k, shape, f32)

    def gain(k, shape):
        return 1.0 + 0.1 * jax.random.normal(k, shape, f32)

    return {
        'x': jax.random.normal(ks[0], (BATCH, SEQ, D), f32),
        'ffn1_norm': gain(ks[1], (L, D)),
        'ffn1_w_in': nrm(ks[2], (L, D, 2 * D_FF), D ** -0.5),
        'ffn1_w_out': nrm(ks[3], (L, D_FF, D), D_FF ** -0.5),
        'mix_norm': gain(ks[4], (L, D)),
        'w_in': nrm(ks[5], (L, D, IN_COLS), D ** -0.5),
        'b_gate': nrm(ks[6], (L, N_BRANCHES * D), 0.1),
        'rwkv_mu': jax.random.uniform(ks[7], (L, RWKV_COLS), f32),
        'rwkv_w0': jax.random.uniform(ks[8], (L, RWKV_WIDTH), f32, -6.0, 0.0),
        'rwkv_w2': nrm(ks[9], (L, DECAY_LORA, RWKV_WIDTH), 0.1 * DECAY_LORA ** -0.5),
        'rwkv_a0': nrm(ks[10], (L, RWKV_WIDTH), 0.1),
        'rwkv_a2': nrm(ks[11], (L, AAA_LORA, RWKV_WIDTH), AAA_LORA ** -0.5),
        'rwkv_g2': nrm(ks[12], (L, GATE_LORA, RWKV_WIDTH), GATE_LORA ** -0.5),
        'rwkv_k_k': gain(ks[13], (L, RWKV_WIDTH)),
        'rwkv_k_a': gain(ks[14], (L, RWKV_WIDTH)),
        'rwkv_r_k': nrm(ks[15], (L, RWKV_HEADS, HEAD_DIM), 0.1),
        'rwkv_ln_w': gain(ks[16], (L, RWKV_WIDTH)),
        'rwkv_ln_b': nrm(ks[17], (L, RWKV_WIDTH), 0.01),
        'attn_q_norm': gain(ks[18], (L, HEAD_DIM)),
        'attn_k_norm': gain(ks[19], (L, HEAD_DIM)),
        'w_proj_rwkv': nrm(ks[20], (L, RWKV_WIDTH, D), RWKV_WIDTH ** -0.5),
        'w_proj_attn': nrm(ks[21], (L, ATTN_WIDTH, D), ATTN_WIDTH ** -0.5),
        'w_out': nrm(ks[22], (L, D, D), D ** -0.5),
        'ffn2_norm': gain(ks[23], (L, D)),
        'ffn2_w_in': nrm(ks[24], (L, D, 2 * D_FF), D ** -0.5),
        'ffn2_w_out': nrm(ks[25], (L, D_FF, D), D_FF ** -0.5),
    }


def _fwd_reference(x, ffn1_norm, ffn1_w_in, ffn1_w_out, mix_norm, w_in, b_gate, rwkv_mu,
              rwkv_w0, rwkv_w2, rwkv_a0, rwkv_a2, rwkv_g2, rwkv_k_k, rwkv_k_a, rwkv_r_k,
              rwkv_ln_w, rwkv_ln_b, attn_q_norm, attn_k_norm, w_proj_rwkv, w_proj_attn,
              w_out, ffn2_norm, ffn2_w_in, ffn2_w_out):
    for l in range(DEPTH):
        x = x + 0.5 * swiglu(rms_norm(x, ffn1_norm[l]), ffn1_w_in[l], ffn1_w_out[l])
        h = rms_norm(x, mix_norm[l])
        p_rwkv, p_q, p_k, p_v, p_gate = _split_last(
            h @ w_in[l], (RWKV_COLS, ATTN_WIDTH, ATTN_WIDTH, ATTN_WIDTH, N_BRANCHES * D_MODEL))
        y_a = rwkv7_time_mix(p_rwkv, rwkv_mu[l], rwkv_w0[l], rwkv_w2[l], rwkv_a0[l],
                             rwkv_a2[l], rwkv_g2[l], rwkv_k_k[l], rwkv_k_a[l], rwkv_r_k[l],
                             rwkv_ln_w[l], rwkv_ln_b[l]).astype(x.dtype)
        y_b = dilated_attention(p_q, p_k, p_v, attn_q_norm[l], attn_k_norm[l]).astype(x.dtype)
        g_a, g_b = _split_last(jax.nn.sigmoid(p_gate + b_gate[l]), (D_MODEL, D_MODEL))
        merged = g_a * (y_a @ w_proj_rwkv[l]) + g_b * (y_b @ w_proj_attn[l])
        x = x + merged @ w_out[l]
        x = x + 0.5 * swiglu(rms_norm(x, ffn2_norm[l]), ffn2_w_in[l], ffn2_w_out[l])
    return x


import jax as _jax
import jax.numpy as _jnp

TWIN_FORMAT = 'train_step'
FWD_PARAMS = ['x', 'ffn1_norm', 'ffn1_w_in', 'ffn1_w_out', 'mix_norm', 'w_in', 'b_gate', 'rwkv_mu', 'rwkv_w0', 'rwkv_w2', 'rwkv_a0', 'rwkv_a2', 'rwkv_g2', 'rwkv_k_k', 'rwkv_k_a', 'rwkv_r_k', 'rwkv_ln_w', 'rwkv_ln_b', 'attn_q_norm', 'attn_k_norm', 'w_proj_rwkv', 'w_proj_attn', 'w_out', 'ffn2_norm', 'ffn2_w_in', 'ffn2_w_out']
TWIN_WEIGHTS = ['ffn1_norm', 'ffn1_w_in', 'ffn1_w_out', 'mix_norm', 'w_in', 'b_gate', 'rwkv_mu', 'rwkv_w0', 'rwkv_w2', 'rwkv_a0', 'rwkv_a2', 'rwkv_g2', 'rwkv_k_k', 'rwkv_k_a', 'rwkv_r_k', 'rwkv_ln_w', 'rwkv_ln_b', 'attn_q_norm', 'attn_k_norm', 'w_proj_rwkv', 'w_proj_attn', 'w_out', 'ffn2_norm', 'ffn2_w_in', 'ffn2_w_out']
TWIN_DIFF_INPUT = 'x'
TWIN_INPUTS = ['x', 'ffn1_norm', 'ffn1_w_in', 'ffn1_w_out', 'mix_norm', 'w_in', 'b_gate', 'rwkv_mu', 'rwkv_w0', 'rwkv_w2', 'rwkv_a0', 'rwkv_a2', 'rwkv_g2', 'rwkv_k_k', 'rwkv_k_a', 'rwkv_r_k', 'rwkv_ln_w', 'rwkv_ln_b', 'attn_q_norm', 'attn_k_norm', 'w_proj_rwkv', 'w_proj_attn', 'w_out', 'ffn2_norm', 'ffn2_w_in', 'ffn2_w_out', 'loss_target', 'm_ffn1_norm', 'm_ffn1_w_in', 'm_ffn1_w_out', 'm_mix_norm', 'm_w_in', 'm_b_gate', 'm_rwkv_mu', 'm_rwkv_w0', 'm_rwkv_w2', 'm_rwkv_a0', 'm_rwkv_a2', 'm_rwkv_g2', 'm_rwkv_k_k', 'm_rwkv_k_a', 'm_rwkv_r_k', 'm_rwkv_ln_w', 'm_rwkv_ln_b', 'm_attn_q_norm', 'm_attn_k_norm', 'm_w_proj_rwkv', 'm_w_proj_attn', 'm_w_out', 'm_ffn2_norm', 'm_ffn2_w_in', 'm_ffn2_w_out', 'v_ffn1_norm', 'v_ffn1_w_in', 'v_ffn1_w_out', 'v_mix_norm', 'v_w_in', 'v_b_gate', 'v_rwkv_mu', 'v_rwkv_w0', 'v_rwkv_w2', 'v_rwkv_a0', 'v_rwkv_a2', 'v_rwkv_g2', 'v_rwkv_k_k', 'v_rwkv_k_a', 'v_rwkv_r_k', 'v_rwkv_ln_w', 'v_rwkv_ln_b', 'v_attn_q_norm', 'v_attn_k_norm', 'v_w_proj_rwkv', 'v_w_proj_attn', 'v_w_out', 'v_ffn2_norm', 'v_ffn2_w_in', 'v_ffn2_w_out']
TWIN_OUTPUTS = ['loss', 'grad_x', 'grad_ffn1_norm', 'grad_ffn1_w_in', 'grad_ffn1_w_out', 'grad_mix_norm', 'grad_w_in', 'grad_b_gate', 'grad_rwkv_mu', 'grad_rwkv_w0', 'grad_rwkv_w2', 'grad_rwkv_a0', 'grad_rwkv_a2', 'grad_rwkv_g2', 'grad_rwkv_k_k', 'grad_rwkv_k_a', 'grad_rwkv_r_k', 'grad_rwkv_ln_w', 'grad_rwkv_ln_b', 'grad_attn_q_norm', 'grad_attn_k_norm', 'grad_w_proj_rwkv', 'grad_w_proj_attn', 'grad_w_out', 'grad_ffn2_norm', 'grad_ffn2_w_in', 'grad_ffn2_w_out', 'delta_ffn1_norm', 'delta_ffn1_w_in', 'delta_ffn1_w_out', 'delta_mix_norm', 'delta_w_in', 'delta_b_gate', 'delta_rwkv_mu', 'delta_rwkv_w0', 'delta_rwkv_w2', 'delta_rwkv_a0', 'delta_rwkv_a2', 'delta_rwkv_g2', 'delta_rwkv_k_k', 'delta_rwkv_k_a', 'delta_rwkv_r_k', 'delta_rwkv_ln_w', 'delta_rwkv_ln_b', 'delta_attn_q_norm', 'delta_attn_k_norm', 'delta_w_proj_rwkv', 'delta_w_proj_attn', 'delta_w_out', 'delta_ffn2_norm', 'delta_ffn2_w_in', 'delta_ffn2_w_out', 'new_m_ffn1_norm', 'new_m_ffn1_w_in', 'new_m_ffn1_w_out', 'new_m_mix_norm', 'new_m_w_in', 'new_m_b_gate', 'new_m_rwkv_mu', 'new_m_rwkv_w0', 'new_m_rwkv_w2', 'new_m_rwkv_a0', 'new_m_rwkv_a2', 'new_m_rwkv_g2', 'new_m_rwkv_k_k', 'new_m_rwkv_k_a', 'new_m_rwkv_r_k', 'new_m_rwkv_ln_w', 'new_m_rwkv_ln_b', 'new_m_attn_q_norm', 'new_m_attn_k_norm', 'new_m_w_proj_rwkv', 'new_m_w_proj_attn', 'new_m_w_out', 'new_m_ffn2_norm', 'new_m_ffn2_w_in', 'new_m_ffn2_w_out', 'new_v_ffn1_norm', 'new_v_ffn1_w_in', 'new_v_ffn1_w_out', 'new_v_mix_norm', 'new_v_w_in', 'new_v_b_gate', 'new_v_rwkv_mu', 'new_v_rwkv_w0', 'new_v_rwkv_w2', 'new_v_rwkv_a0', 'new_v_rwkv_a2', 'new_v_rwkv_g2', 'new_v_rwkv_k_k', 'new_v_rwkv_k_a', 'new_v_rwkv_r_k', 'new_v_rwkv_ln_w', 'new_v_rwkv_ln_b', 'new_v_attn_q_norm', 'new_v_attn_k_norm', 'new_v_w_proj_rwkv', 'new_v_w_proj_attn', 'new_v_w_out', 'new_v_ffn2_norm', 'new_v_ffn2_w_in', 'new_v_ffn2_w_out']
TWIN_LEAF_KINDS = {'loss': 'loss', 'grad_x': 'grad_x', 'grad_ffn1_norm': 'grad_w', 'grad_ffn1_w_in': 'grad_w', 'grad_ffn1_w_out': 'grad_w', 'grad_mix_norm': 'grad_w', 'grad_w_in': 'grad_w', 'grad_b_gate': 'grad_w', 'grad_rwkv_mu': 'grad_w', 'grad_rwkv_w0': 'grad_w', 'grad_rwkv_w2': 'grad_w', 'grad_rwkv_a0': 'grad_w', 'grad_rwkv_a2': 'grad_w', 'grad_rwkv_g2': 'grad_w', 'grad_rwkv_k_k': 'grad_w', 'grad_rwkv_k_a': 'grad_w', 'grad_rwkv_r_k': 'grad_w', 'grad_rwkv_ln_w': 'grad_w', 'grad_rwkv_ln_b': 'grad_w', 'grad_attn_q_norm': 'grad_w', 'grad_attn_k_norm': 'grad_w', 'grad_w_proj_rwkv': 'grad_w', 'grad_w_proj_attn': 'grad_w', 'grad_w_out': 'grad_w', 'grad_ffn2_norm': 'grad_w', 'grad_ffn2_w_in': 'grad_w', 'grad_ffn2_w_out': 'grad_w', 'delta_ffn1_norm': 'delta_w', 'delta_ffn1_w_in': 'delta_w', 'delta_ffn1_w_out': 'delta_w', 'delta_mix_norm': 'delta_w', 'delta_w_in': 'delta_w', 'delta_b_gate': 'delta_w', 'delta_rwkv_mu': 'delta_w', 'delta_rwkv_w0': 'delta_w', 'delta_rwkv_w2': 'delta_w', 'delta_rwkv_a0': 'delta_w', 'delta_rwkv_a2': 'delta_w', 'delta_rwkv_g2': 'delta_w', 'delta_rwkv_k_k': 'delta_w', 'delta_rwkv_k_a': 'delta_w', 'delta_rwkv_r_k': 'delta_w', 'delta_rwkv_ln_w': 'delta_w', 'delta_rwkv_ln_b': 'delta_w', 'delta_attn_q_norm': 'delta_w', 'delta_attn_k_norm': 'delta_w', 'delta_w_proj_rwkv': 'delta_w', 'delta_w_proj_attn': 'delta_w', 'delta_w_out': 'delta_w', 'delta_ffn2_norm': 'delta_w', 'delta_ffn2_w_in': 'delta_w', 'delta_ffn2_w_out': 'delta_w', 'new_m_ffn1_norm': 'new_m', 'new_m_ffn1_w_in': 'new_m', 'new_m_ffn1_w_out': 'new_m', 'new_m_mix_norm': 'new_m', 'new_m_w_in': 'new_m', 'new_m_b_gate': 'new_m', 'new_m_rwkv_mu': 'new_m', 'new_m_rwkv_w0': 'new_m', 'new_m_rwkv_w2': 'new_m', 'new_m_rwkv_a0': 'new_m', 'new_m_rwkv_a2': 'new_m', 'new_m_rwkv_g2': 'new_m', 'new_m_rwkv_k_k': 'new_m', 'new_m_rwkv_k_a': 'new_m', 'new_m_rwkv_r_k': 'new_m', 'new_m_rwkv_ln_w': 'new_m', 'new_m_rwkv_ln_b': 'new_m', 'new_m_attn_q_norm': 'new_m', 'new_m_attn_k_norm': 'new_m', 'new_m_w_proj_rwkv': 'new_m', 'new_m_w_proj_attn': 'new_m', 'new_m_w_out': 'new_m', 'new_m_ffn2_norm': 'new_m', 'new_m_ffn2_w_in': 'new_m', 'new_m_ffn2_w_out': 'new_m', 'new_v_ffn1_norm': 'new_v', 'new_v_ffn1_w_in': 'new_v', 'new_v_ffn1_w_out': 'new_v', 'new_v_mix_norm': 'new_v', 'new_v_w_in': 'new_v', 'new_v_b_gate': 'new_v', 'new_v_rwkv_mu': 'new_v', 'new_v_rwkv_w0': 'new_v', 'new_v_rwkv_w2': 'new_v', 'new_v_rwkv_a0': 'new_v', 'new_v_rwkv_a2': 'new_v', 'new_v_rwkv_g2': 'new_v', 'new_v_rwkv_k_k': 'new_v', 'new_v_rwkv_k_a': 'new_v', 'new_v_rwkv_r_k': 'new_v', 'new_v_rwkv_ln_w': 'new_v', 'new_v_rwkv_ln_b': 'new_v', 'new_v_attn_q_norm': 'new_v', 'new_v_attn_k_norm': 'new_v', 'new_v_w_proj_rwkv': 'new_v', 'new_v_w_proj_attn': 'new_v', 'new_v_w_out': 'new_v', 'new_v_ffn2_norm': 'new_v', 'new_v_ffn2_w_in': 'new_v', 'new_v_ffn2_w_out': 'new_v'}


def _forward(args):
    return _fwd_reference(*[args[k] for k in FWD_PARAMS])


def _output_shape():
    out = _jax.eval_shape(lambda: _forward(_fwd_setup_inputs(0)))
    return out.shape, out.dtype

N_MICROBATCH = 1
ADAM_LR = 0.001
ADAM_B1 = 0.9
ADAM_B2 = 0.999
ADAM_EPS = 1e-08
ADAM_WD = 0.01
ADAM_STEP = 10
PER_EXAMPLE_BATCH_AXIS = {'x': 0, 'loss_target': 0}
SHARED_INPUTS = []
_WEIGHT_DTYPES = {'ffn1_norm': _jnp.float32, 'ffn1_w_in': _jnp.float32, 'ffn1_w_out': _jnp.float32, 'mix_norm': _jnp.float32, 'w_in': _jnp.float32, 'b_gate': _jnp.float32, 'rwkv_mu': _jnp.float32, 'rwkv_w0': _jnp.float32, 'rwkv_w2': _jnp.float32, 'rwkv_a0': _jnp.float32, 'rwkv_a2': _jnp.float32, 'rwkv_g2': _jnp.float32, 'rwkv_k_k': _jnp.float32, 'rwkv_k_a': _jnp.float32, 'rwkv_r_k': _jnp.float32, 'rwkv_ln_w': _jnp.float32, 'rwkv_ln_b': _jnp.float32, 'attn_q_norm': _jnp.float32, 'attn_k_norm': _jnp.float32, 'w_proj_rwkv': _jnp.float32, 'w_proj_attn': _jnp.float32, 'w_out': _jnp.float32, 'ffn2_norm': _jnp.float32, 'ffn2_w_in': _jnp.float32, 'ffn2_w_out': _jnp.float32}
MOMENT_SCALE = {'ffn1_norm': 6.165289e+00, 'ffn1_w_in': 7.249191e-02, 'ffn1_w_out': 1.220761e-01, 'mix_norm': 8.356165e-01, 'w_in': 7.694370e-02, 'b_gate': 7.325829e-01, 'rwkv_mu': 1.288542e+00, 'rwkv_w0': 5.011520e-02, 'rwkv_w2': 4.841952e-03, 'rwkv_a0': 1.146497e-01, 'rwkv_a2': 3.601885e-02, 'rwkv_g2': 2.646710e+00, 'rwkv_k_k': 1.917741e-01, 'rwkv_k_a': 2.749157e-01, 'rwkv_r_k': 1.909811e+00, 'rwkv_ln_w': 4.536674e+00, 'rwkv_ln_b': 2.439361e-01, 'attn_q_norm': 2.512059e-01, 'attn_k_norm': 2.617892e-01, 'w_proj_rwkv': 1.577162e-01, 'w_proj_attn': 1.846671e-02, 'w_out': 1.332634e-01, 'ffn2_norm': 6.163397e+00, 'ffn2_w_in': 6.167965e-02, 'ffn2_w_out': 1.069560e-01}


def _to_microbatches(a, axis):
    t = _jnp.moveaxis(a, axis, 0)
    t = t.reshape((N_MICROBATCH, t.shape[0] // N_MICROBATCH) + t.shape[1:])
    return _jnp.moveaxis(t, 1, axis + 1)


def setup_inputs(seed: int = 0) -> dict:
    inp = _fwd_setup_inputs(seed)
    key = _jax.random.fold_in(_jax.random.key(seed), 7919)
    shape, _ = _output_shape()
    out = dict(inp)
    out["loss_target"] = _jax.random.normal(_jax.random.fold_in(key, 0), shape, _jnp.float32)
    for i, name in enumerate(TWIN_WEIGHTS):
        w = inp[name].astype(_jnp.float32)
        if MOMENT_SCALE is None:
            s = _jnp.sqrt(_jnp.mean(_jnp.square(w)) + 1e-30)
        else:
            s = MOMENT_SCALE[name]
        km, kv = _jax.random.split(_jax.random.fold_in(key, i + 1))
        out[name] = w
        out["m_" + name] = s * _jax.random.normal(km, w.shape, _jnp.float32)
        out["v_" + name] = (s * s) * _jax.random.uniform(kv, w.shape, _jnp.float32, 0.5, 1.5)
    if N_MICROBATCH > 1:
        for name, axis in PER_EXAMPLE_BATCH_AXIS.items():
            out[name] = _to_microbatches(out[name], axis)
    return {'x': out['x'], 'ffn1_norm': out['ffn1_norm'], 'ffn1_w_in': out['ffn1_w_in'], 'ffn1_w_out': out['ffn1_w_out'], 'mix_norm': out['mix_norm'], 'w_in': out['w_in'], 'b_gate': out['b_gate'], 'rwkv_mu': out['rwkv_mu'], 'rwkv_w0': out['rwkv_w0'], 'rwkv_w2': out['rwkv_w2'], 'rwkv_a0': out['rwkv_a0'], 'rwkv_a2': out['rwkv_a2'], 'rwkv_g2': out['rwkv_g2'], 'rwkv_k_k': out['rwkv_k_k'], 'rwkv_k_a': out['rwkv_k_a'], 'rwkv_r_k': out['rwkv_r_k'], 'rwkv_ln_w': out['rwkv_ln_w'], 'rwkv_ln_b': out['rwkv_ln_b'], 'attn_q_norm': out['attn_q_norm'], 'attn_k_norm': out['attn_k_norm'], 'w_proj_rwkv': out['w_proj_rwkv'], 'w_proj_attn': out['w_proj_attn'], 'w_out': out['w_out'], 'ffn2_norm': out['ffn2_norm'], 'ffn2_w_in': out['ffn2_w_in'], 'ffn2_w_out': out['ffn2_w_out'], 'loss_target': out['loss_target'], 'm_ffn1_norm': out['m_ffn1_norm'], 'm_ffn1_w_in': out['m_ffn1_w_in'], 'm_ffn1_w_out': out['m_ffn1_w_out'], 'm_mix_norm': out['m_mix_norm'], 'm_w_in': out['m_w_in'], 'm_b_gate': out['m_b_gate'], 'm_rwkv_mu': out['m_rwkv_mu'], 'm_rwkv_w0': out['m_rwkv_w0'], 'm_rwkv_w2': out['m_rwkv_w2'], 'm_rwkv_a0': out['m_rwkv_a0'], 'm_rwkv_a2': out['m_rwkv_a2'], 'm_rwkv_g2': out['m_rwkv_g2'], 'm_rwkv_k_k': out['m_rwkv_k_k'], 'm_rwkv_k_a': out['m_rwkv_k_a'], 'm_rwkv_r_k': out['m_rwkv_r_k'], 'm_rwkv_ln_w': out['m_rwkv_ln_w'], 'm_rwkv_ln_b': out['m_rwkv_ln_b'], 'm_attn_q_norm': out['m_attn_q_norm'], 'm_attn_k_norm': out['m_attn_k_norm'], 'm_w_proj_rwkv': out['m_w_proj_rwkv'], 'm_w_proj_attn': out['m_w_proj_attn'], 'm_w_out': out['m_w_out'], 'm_ffn2_norm': out['m_ffn2_norm'], 'm_ffn2_w_in': out['m_ffn2_w_in'], 'm_ffn2_w_out': out['m_ffn2_w_out'], 'v_ffn1_norm': out['v_ffn1_norm'], 'v_ffn1_w_in': out['v_ffn1_w_in'], 'v_ffn1_w_out': out['v_ffn1_w_out'], 'v_mix_norm': out['v_mix_norm'], 'v_w_in': out['v_w_in'], 'v_b_gate': out['v_b_gate'], 'v_rwkv_mu': out['v_rwkv_mu'], 'v_rwkv_w0': out['v_rwkv_w0'], 'v_rwkv_w2': out['v_rwkv_w2'], 'v_rwkv_a0': out['v_rwkv_a0'], 'v_rwkv_a2': out['v_rwkv_a2'], 'v_rwkv_g2': out['v_rwkv_g2'], 'v_rwkv_k_k': out['v_rwkv_k_k'], 'v_rwkv_k_a': out['v_rwkv_k_a'], 'v_rwkv_r_k': out['v_rwkv_r_k'], 'v_rwkv_ln_w': out['v_rwkv_ln_w'], 'v_rwkv_ln_b': out['v_rwkv_ln_b'], 'v_attn_q_norm': out['v_attn_q_norm'], 'v_attn_k_norm': out['v_attn_k_norm'], 'v_w_proj_rwkv': out['v_w_proj_rwkv'], 'v_w_proj_attn': out['v_w_proj_attn'], 'v_w_out': out['v_w_out'], 'v_ffn2_norm': out['v_ffn2_norm'], 'v_ffn2_w_in': out['v_ffn2_w_in'], 'v_ffn2_w_out': out['v_ffn2_w_out']}


def _loss(weights, diff, rest, loss_target):
    with _jax.named_scope("forward"):
        args = {**rest, TWIN_DIFF_INPUT: diff, **{k: w.astype(_WEIGHT_DTYPES[k]) for k, w in weights.items()}}
        y = _forward(args)
    with _jax.named_scope("loss_head"):
        err = _jnp.square(y.astype(_jnp.float32) - loss_target)
        return 0.5 * _jnp.sum(_jnp.mean(err, axis=-1)) if err.ndim else 0.5 * err


def _adamw(w, g, m, v):
    m = ADAM_B1 * m + (1.0 - ADAM_B1) * g
    v = ADAM_B2 * v + (1.0 - ADAM_B2) * _jnp.square(g)
    m_hat = m / (1.0 - ADAM_B1 ** ADAM_STEP)
    v_hat = v / (1.0 - ADAM_B2 ** ADAM_STEP)
    delta = -ADAM_LR * (m_hat / (_jnp.sqrt(v_hat) + ADAM_EPS) + ADAM_WD * w)
    return delta, m, v


def reference(x, ffn1_norm, ffn1_w_in, ffn1_w_out, mix_norm, w_in, b_gate, rwkv_mu, rwkv_w0, rwkv_w2, rwkv_a0, rwkv_a2, rwkv_g2, rwkv_k_k, rwkv_k_a, rwkv_r_k, rwkv_ln_w, rwkv_ln_b, attn_q_norm, attn_k_norm, w_proj_rwkv, w_proj_attn, w_out, ffn2_norm, ffn2_w_in, ffn2_w_out, loss_target, m_ffn1_norm, m_ffn1_w_in, m_ffn1_w_out, m_mix_norm, m_w_in, m_b_gate, m_rwkv_mu, m_rwkv_w0, m_rwkv_w2, m_rwkv_a0, m_rwkv_a2, m_rwkv_g2, m_rwkv_k_k, m_rwkv_k_a, m_rwkv_r_k, m_rwkv_ln_w, m_rwkv_ln_b, m_attn_q_norm, m_attn_k_norm, m_w_proj_rwkv, m_w_proj_attn, m_w_out, m_ffn2_norm, m_ffn2_w_in, m_ffn2_w_out, v_ffn1_norm, v_ffn1_w_in, v_ffn1_w_out, v_mix_norm, v_w_in, v_b_gate, v_rwkv_mu, v_rwkv_w0, v_rwkv_w2, v_rwkv_a0, v_rwkv_a2, v_rwkv_g2, v_rwkv_k_k, v_rwkv_k_a, v_rwkv_r_k, v_rwkv_ln_w, v_rwkv_ln_b, v_attn_q_norm, v_attn_k_norm, v_w_proj_rwkv, v_w_proj_attn, v_w_out, v_ffn2_norm, v_ffn2_w_in, v_ffn2_w_out):
    given = dict(x=x, ffn1_norm=ffn1_norm, ffn1_w_in=ffn1_w_in, ffn1_w_out=ffn1_w_out, mix_norm=mix_norm, w_in=w_in, b_gate=b_gate, rwkv_mu=rwkv_mu, rwkv_w0=rwkv_w0, rwkv_w2=rwkv_w2, rwkv_a0=rwkv_a0, rwkv_a2=rwkv_a2, rwkv_g2=rwkv_g2, rwkv_k_k=rwkv_k_k, rwkv_k_a=rwkv_k_a, rwkv_r_k=rwkv_r_k, rwkv_ln_w=rwkv_ln_w, rwkv_ln_b=rwkv_ln_b, attn_q_norm=attn_q_norm, attn_k_norm=attn_k_norm, w_proj_rwkv=w_proj_rwkv, w_proj_attn=w_proj_attn, w_out=w_out, ffn2_norm=ffn2_norm, ffn2_w_in=ffn2_w_in, ffn2_w_out=ffn2_w_out, loss_target=loss_target, m_ffn1_norm=m_ffn1_norm, m_ffn1_w_in=m_ffn1_w_in, m_ffn1_w_out=m_ffn1_w_out, m_mix_norm=m_mix_norm, m_w_in=m_w_in, m_b_gate=m_b_gate, m_rwkv_mu=m_rwkv_mu, m_rwkv_w0=m_rwkv_w0, m_rwkv_w2=m_rwkv_w2, m_rwkv_a0=m_rwkv_a0, m_rwkv_a2=m_rwkv_a2, m_rwkv_g2=m_rwkv_g2, m_rwkv_k_k=m_rwkv_k_k, m_rwkv_k_a=m_rwkv_k_a, m_rwkv_r_k=m_rwkv_r_k, m_rwkv_ln_w=m_rwkv_ln_w, m_rwkv_ln_b=m_rwkv_ln_b, m_attn_q_norm=m_attn_q_norm, m_attn_k_norm=m_attn_k_norm, m_w_proj_rwkv=m_w_proj_rwkv, m_w_proj_attn=m_w_proj_attn, m_w_out=m_w_out, m_ffn2_norm=m_ffn2_norm, m_ffn2_w_in=m_ffn2_w_in, m_ffn2_w_out=m_ffn2_w_out, v_ffn1_norm=v_ffn1_norm, v_ffn1_w_in=v_ffn1_w_in, v_ffn1_w_out=v_ffn1_w_out, v_mix_norm=v_mix_norm, v_w_in=v_w_in, v_b_gate=v_b_gate, v_rwkv_mu=v_rwkv_mu, v_rwkv_w0=v_rwkv_w0, v_rwkv_w2=v_rwkv_w2, v_rwkv_a0=v_rwkv_a0, v_rwkv_a2=v_rwkv_a2, v_rwkv_g2=v_rwkv_g2, v_rwkv_k_k=v_rwkv_k_k, v_rwkv_k_a=v_rwkv_k_a, v_rwkv_r_k=v_rwkv_r_k, v_rwkv_ln_w=v_rwkv_ln_w, v_rwkv_ln_b=v_rwkv_ln_b, v_attn_q_norm=v_attn_q_norm, v_attn_k_norm=v_attn_k_norm, v_w_proj_rwkv=v_w_proj_rwkv, v_w_proj_attn=v_w_proj_attn, v_w_out=v_w_out, v_ffn2_norm=v_ffn2_norm, v_ffn2_w_in=v_ffn2_w_in, v_ffn2_w_out=v_ffn2_w_out)
    weights = {n: given[n] for n in TWIN_WEIGHTS}
    shared = {n: given[n] for n in SHARED_INPUTS}
    per_example = {n: given[n] for n in ['x']}
    grad_fn = _jax.value_and_grad(_loss, argnums=(0, 1))

    def one_microbatch(ex, loss_target):
        ex = dict(ex)
        diff = ex.pop(TWIN_DIFF_INPUT)
        return grad_fn(weights, diff, {**shared, **ex}, loss_target)

    if N_MICROBATCH == 1:
        loss, (grad_w, grad_x) = one_microbatch(per_example, given["loss_target"])
    else:
        def body(carry, xs):
            loss_sum, grad_sum = carry
            l_k, (gw_k, gx_k) = one_microbatch(xs[0], xs[1])
            with _jax.named_scope("update"):
                return (loss_sum + l_k, _jax.tree.map(_jnp.add, grad_sum, gw_k)), gx_k

        init = (_jnp.zeros((), _jnp.float32), _jax.tree.map(_jnp.zeros_like, weights))
        (loss, grad_w), grad_x = _jax.lax.scan(body, init, (per_example, given["loss_target"]))
    with _jax.named_scope("update"):
        delta_w, new_m, new_v = {}, {}, {}
        for n in TWIN_WEIGHTS:
            delta_w[n], new_m[n], new_v[n] = _adamw(weights[n], grad_w[n], given["m_" + n], given["v_" + n])
    return (loss, grad_x, *[grad_w[n] for n in TWIN_WEIGHTS], *[delta_w[n] for n in TWIN_WEIGHTS],
            *[new_m[n] for n in TWIN_WEIGHTS], *[new_v[n] for n in TWIN_WEIGHTS])
```

```python
import functools

import jax
import jax.numpy as jnp
from jax import lax
from jax.experimental import pallas as pl
from jax.experimental.pallas import tpu as pltpu

F32 = jnp.float32
BF16 = jnp.bfloat16
HI = lax.Precision.HIGHEST
MESH = pl.DeviceIdType.MESH

D_MODEL = 1024
HEAD_DIM = 64
RWKV_HEADS = 16
LORA_W, LORA_A, LORA_G = 64, 64, 160
LORA = LORA_W + LORA_A + LORA_G
RKV = 3 * D_MODEL
ATTN_PAIRS = ((128, 1), (512, 4), (2048, 16))
ATTN_BLK = 128
ATTN_HPG = 4
ATTN_WIDTH = 768
GROUP_W = ATTN_HPG * HEAD_DIM
D_FF = 2816
GN_EPS = 64e-5
RMS_EPS = 1e-6
NEG_INF = -1e30
WKV_CHUNK = 64

ADAM_LR, ADAM_B1, ADAM_B2, ADAM_EPS, ADAM_WD, ADAM_STEP = 0.001, 0.9, 0.999, 1e-08, 0.01, 10

V7X_VMEM_BYTES = 64 << 20
VMEM_TEMP_ALLOWANCE = 20 << 20

PACK_W = 1024
PACK_ROW_BLOCK = 512


def _cparams(sem, block_bytes):
    limit = min(2 * block_bytes + VMEM_TEMP_ALLOWANCE, V7X_VMEM_BYTES - (6 << 20))
    return pltpu.CompilerParams(dimension_semantics=sem, vmem_limit_bytes=int(limit))


def _nbytes(shape, dtype):
    n = 1
    for s in shape:
        n *= s
    return n * jnp.dtype(dtype).itemsize


def _make_dots(prec):
    def raw(a, b, ca, cb):
        return lax.dot_general(a, b, (((ca,), (cb,)), ((), ())), precision=prec,
                               preferred_element_type=F32)

    @jax.custom_vjp
    def nn(a, b):
        return raw(a, b, 1, 0)

    @jax.custom_vjp
    def nt(a, b):
        return raw(a, b, 1, 1)

    @jax.custom_vjp
    def tn(a, b):
        return raw(a, b, 0, 0)

    nn.defvjp(lambda a, b: (raw(a, b, 1, 0), (a, b)),
              lambda res, g: (raw(g, res[1], 1, 1), raw(res[0], g, 0, 0)))
    nt.defvjp(lambda a, b: (raw(a, b, 1, 1), (a, b)),
              lambda res, g: (raw(g, res[1], 1, 0), raw(g, res[0], 0, 0)))
    tn.defvjp(lambda a, b: (raw(a, b, 0, 0), (a, b)),
              lambda res, g: (raw(res[1], g, 1, 1), raw(res[0], g, 1, 0)))
    return nn, nt, tn


NN, NT, TN = _make_dots(None)
NN_HI, NT_HI, TN_HI = _make_dots(HI)


def _pick(n, cap):
    best = None
    for t in range(128, min(n, cap) + 1, 128):
        if n % t == 0:
            best = t
    return best or n


def matmul(a, b, mode, name, *, add=None, scale=1.0, out_dtype=F32):
    if mode == "nn":
        (M, K), (K2, N) = a.shape, b.shape
    elif mode == "nt":
        (M, K), (N, K2) = a.shape, b.shape
    else:
        (K, M), (K2, N) = a.shape, b.shape
    assert K == K2, (name, a.shape, b.shape)
    tm, tn, tk = _pick(M, 512), _pick(N, 512), _pick(K, 1024)
    nk = K // tk
    ca, cb = {"nn": (1, 0), "nt": (1, 1), "tn": (0, 0)}[mode]

    def body(*refs):
        if add is None:
            a_ref, b_ref, o_ref, acc_ref = refs
        else:
            a_ref, b_ref, add_ref, o_ref, acc_ref = refs
        k = pl.program_id(2)

        @pl.when(k == 0)
        def _():
            acc_ref[...] = jnp.zeros_like(acc_ref)

        acc_ref[...] += lax.dot_general(a_ref[...].astype(BF16), b_ref[...].astype(BF16),
                                        (((ca,), (cb,)), ((), ())), preferred_element_type=F32)

        @pl.when(k == nk - 1)
        def _():
            r = acc_ref[...] * scale
            if add is not None:
                r = add_ref[...] + r
            o_ref[...] = r.astype(o_ref.dtype)

    a_spec = (pl.BlockSpec((tk, tm), lambda i, j, k: (k, i)) if mode == "tn"
              else pl.BlockSpec((tm, tk), lambda i, j, k: (i, k)))
    b_spec = (pl.BlockSpec((tn, tk), lambda i, j, k: (j, k)) if mode == "nt"
              else pl.BlockSpec((tk, tn), lambda i, j, k: (k, j)))
    in_specs, args = [a_spec, b_spec], [a, b]
    blk = tm * tk * a.dtype.itemsize + tk * tn * b.dtype.itemsize + tm * tn * 8
    if add is not None:
        in_specs.append(pl.BlockSpec((tm, tn), lambda i, j, k: (i, j)))
        args.append(add)
        blk += tm * tn * 4
    return pl.pallas_call(
        body, name=name, grid=(M // tm, N // tn, nk),
        in_specs=in_specs, out_specs=pl.BlockSpec((tm, tn), lambda i, j, k: (i, j)),
        out_shape=jax.ShapeDtypeStruct((M, N), out_dtype),
        scratch_shapes=[pltpu.VMEM((tm, tn), F32)],
        compiler_params=_cparams(("parallel", "parallel", "arbitrary"), blk),
    )(*args)


def rowmap(f, rows, params, outs, accs=(), *, tb, name):
    rows = [r if isinstance(r, tuple) else (r, r.shape[1], 0) for r in rows]
    S = rows[0][0].shape[0]
    assert S % tb == 0, (name, S, tb)
    n_in, n_out = len(rows) + len(params), len(outs)

    def body(*refs):
        res = f(*[r[...] for r in refs[:n_in]])
        res = res if isinstance(res, (tuple, list)) else (res,)
        o_refs, a_refs = refs[n_in:n_in + n_out], refs[n_in + n_out:]
        for ref, val in zip(o_refs, res[:n_out]):
            ref[...] = val.astype(ref.dtype)
        if a_refs:
            @pl.when(pl.program_id(0) == 0)
            def _():
                for ref in a_refs:
                    ref[...] = jnp.zeros_like(ref)

            for ref, val in zip(a_refs, res[n_out:]):
                ref[...] += val.astype(F32)

    in_specs = [pl.BlockSpec((tb, w), functools.partial(lambda cb, i: (i, cb), cb)) for _, w, cb in rows]
    in_specs += [pl.BlockSpec(p.shape, lambda i: (0, 0)) for p in params]
    out_specs = [pl.BlockSpec((tb, w), lambda i: (i, 0)) for w, _ in outs]
    out_specs += [pl.BlockSpec(tuple(s), lambda i: (0, 0)) for s in accs]
    out_shape = [jax.ShapeDtypeStruct((S, w), dt) for w, dt in outs]
    out_shape += [jax.ShapeDtypeStruct(tuple(s), F32) for s in accs]
    blk = sum(tb * w * a.dtype.itemsize for a, w, _ in rows) + sum(_nbytes(p.shape, p.dtype) for p in params)
    blk += sum(_nbytes((tb, w), dt) for w, dt in outs) + sum(_nbytes(s, F32) for s in accs)
    res = pl.pallas_call(
        body, name=name, grid=(S // tb,), in_specs=in_specs, out_specs=out_specs, out_shape=out_shape,
        compiler_params=_cparams(("arbitrary",) if accs else ("parallel",), blk),
    )(*[r[0] for r in rows], *params)
    return res


def _rms(x, g):
    return x * lax.rsqrt(jnp.mean(x * x, axis=-1, keepdims=True) + RMS_EPS) * g


def _softplus(z):
    return jnp.maximum(z, 0.0) + jnp.log(1.0 + jnp.exp(-jnp.abs(z)))


def _swiglu_act(gu):
    gate, up = gu[:, :D_FF], gu[:, D_FF:]
    return gate * jax.nn.sigmoid(gate) * up


def _rwkv_pre(xrk, xlo, w0, w2p, a0, a2p, g2p, k_k, k_a, seg, seg_t):
    k = xrk[:, D_MODEL:2 * D_MODEL]
    w = -_softplus(-(w0 + NN(jnp.tanh(xlo), w2p))) - 0.5
    log_decay = -jnp.exp(w)
    a = jax.nn.sigmoid(a0 + NN(xlo, a2p))
    g = NN(jax.nn.sigmoid(xlo), g2p)
    kk = k * k_k
    norm = jnp.maximum(jnp.sqrt(NN_HI(kk * kk, seg)), 1e-12)
    kk = kk / NN_HI(norm, seg_t)
    k_mod = k * (1.0 + (a - 1.0) * k_a)
    return log_decay, k_mod, -kk, kk * a, g


def _rwkv_post(wkv, r, k_mod, v, g, r_k, ln_w, ln_b, seg, seg_t):
    inv_n = 1.0 / HEAD_DIM
    mean = NN_HI(wkv, seg) * inv_n
    cen = wkv - NN_HI(mean, seg_t)
    var = NN_HI(cen * cen, seg) * inv_n
    y = cen * NN_HI(lax.rsqrt(var + GN_EPS), seg_t) * ln_w + ln_b
    bonus = NN_HI(NN_HI(r * k_mod * r_k, seg), seg_t) * v
    return (y + bonus) * g


def _gate_merge(pgate, pa, pb, b_gate):
    sg = jax.nn.sigmoid(pgate + b_gate)
    return sg[:, :D_MODEL] * pa + sg[:, D_MODEL:] * pb


def _group_combine(o, lse):
    ls = [lse[:, GROUP_W * i:GROUP_W * (i + 1)] for i in range(3)]
    m = jnp.maximum(jnp.maximum(ls[0], ls[1]), ls[2])
    es = [jnp.exp(l - m) for l in ls]
    den = es[0] + es[1] + es[2]
    return jnp.concatenate([o[:, GROUP_W * i:GROUP_W * (i + 1)] * (es[i] / den) for i in range(3)], axis=1)


def _attn_block(q, kc, kp, vc, vp, q_gain, k_gain, first):
    qn = _rms(q, q_gain) * (HEAD_DIM ** -0.5)
    kcn, kpn = _rms(kc, k_gain), _rms(kp, k_gain)
    qi = lax.broadcasted_iota(jnp.int32, (ATTN_BLK, ATTN_BLK), 0)
    kj = lax.broadcasted_iota(jnp.int32, (ATTN_BLK, ATTN_BLK), 1)
    s_c = jnp.where(kj <= qi, NT(qn, kcn), NEG_INF)
    s_p = jnp.where((kj >= qi) & (first < 0.5), NT(qn, kpn), NEG_INF)
    m = jnp.maximum(jnp.max(s_c, axis=-1, keepdims=True), jnp.max(s_p, axis=-1, keepdims=True))
    e_c, e_p = jnp.exp(s_c - m), jnp.exp(s_p - m)
    den = jnp.sum(e_c, axis=-1, keepdims=True) + jnp.sum(e_p, axis=-1, keepdims=True)
    o = NN(e_c / den, vc) + NN(e_p / den, vp)
    lse = m + jnp.log(den)
    return o, jnp.broadcast_to(lse, (ATTN_BLK, HEAD_DIM))


def _tri_inverse(n):
    c = n.shape[0]
    eye = (lax.broadcasted_iota(jnp.int32, (c, c), 0) == lax.broadcasted_iota(jnp.int32, (c, c), 1)).astype(F32)
    t, p, span = eye + n, n, 2
    while span < c:
        p = NN_HI(p, p)
        t = t + NN_HI(t, p)
        span *= 2
    return t


@jax.custom_vjp
def _tri_solve(n, rhs):
    return NN_HI(_tri_inverse(n), rhs)


def _tri_solve_fwd(n, rhs):
    t = _tri_inverse(n)
    x = NN_HI(t, rhs)
    return x, (t, x)


def _tri_solve_bwd(res, dx):
    t, x = res
    drhs = TN_HI(t, dx)
    return NT_HI(drhs, x), drhs


_tri_solve.defvjp(_tri_solve_fwd, _tri_solve_bwd)


def _wkv_chunk(s0, r, lw, k, v, a, b):
    c = r.shape[0]
    row = lax.broadcasted_iota(jnp.int32, (c, c), 0)
    col = lax.broadcasted_iota(jnp.int32, (c, c), 1)
    cum = NN_HI((row >= col).astype(F32), lw)
    e_pos, e_neg = jnp.exp(cum), jnp.exp(-cum)
    a_t = a * jnp.exp(cum - lw)
    b_t, k_t, r_t = b * e_neg, k * e_neg, r * e_pos
    strict, incl = row > col, row >= col
    n_ab = jnp.where(strict, NT_HI(a_t, b_t), 0.0)
    n_ak = jnp.where(strict, NT_HI(a_t, k_t), 0.0)
    m_rb = jnp.where(incl, NT_HI(r_t, b_t), 0.0)
    m_rk = jnp.where(incl, NT_HI(r_t, k_t), 0.0)
    u = _tri_solve(n_ab, NT_HI(a_t, s0) + NN_HI(n_ak, v))
    y = NT_HI(r_t, s0) + NN_HI(m_rb, u) + NN_HI(m_rk, v)
    g_end = jnp.exp(jnp.sum(lw, axis=0, keepdims=True))
    s1 = s0 * g_end + TN_HI(u, b_t * g_end) + TN_HI(v, k_t * g_end)
    return y, s1


def _adamw(w, g, m, v):
    m = ADAM_B1 * m + (1.0 - ADAM_B1) * g
    v = ADAM_B2 * v + (1.0 - ADAM_B2) * jnp.square(g)
    m_hat = m / (1.0 - ADAM_B1 ** ADAM_STEP)
    v_hat = v / (1.0 - ADAM_B2 ** ADAM_STEP)
    delta = -ADAM_LR * (m_hat / (jnp.sqrt(v_hat) + ADAM_EPS) + ADAM_WD * w)
    return delta, m, v


def token_shift_fwd(p, mu, *, tb, name):
    S, W = p.shape
    hb = tb // 8

    def body(p_ref, halo_ref, mu_ref, o_ref):
        i = pl.program_id(0)
        x = p_ref[...]
        before = halo_ref[7:8, :] * (i > 0).astype(F32)
        row = lax.broadcasted_iota(jnp.int32, (tb, W), 0)
        prev = jnp.where(row == 0, before, pltpu.roll(x, 1, 0))
        o_ref[...] = x + (prev - x) * mu_ref[...]

    blk = (2 * tb + 8) * W * 4
    return pl.pallas_call(
        body, name=name, grid=(S // tb,),
        in_specs=[pl.BlockSpec((tb, W), lambda i: (i, 0)),
                  pl.BlockSpec((8, W), lambda i: (jnp.maximum(i * hb - 1, 0), 0)),
                  pl.BlockSpec((1, W), lambda i: (0, 0))],
        out_specs=pl.BlockSpec((tb, W), lambda i: (i, 0)),
        out_shape=jax.ShapeDtypeStruct((S, W), F32),
        compiler_params=_cparams(("parallel",), blk),
    )(p, p, mu)


def token_shift_bwd(dxs, p, mu, *, tb, name):
    S, W = p.shape
    hb, nb = tb // 8, S // tb

    def body(d_ref, dnext_ref, p_ref, halo_ref, mu_ref, dp_ref, dmu_ref):
        i = pl.program_id(0)
        d, x, mu_v = d_ref[...], p_ref[...], mu_ref[...]
        row = lax.broadcasted_iota(jnp.int32, (tb, W), 0)
        before = halo_ref[7:8, :] * (i > 0).astype(F32)
        prev = jnp.where(row == 0, before, pltpu.roll(x, 1, 0))
        t = d * mu_v
        after = dnext_ref[0:1, :] * mu_v * (i < nb - 1).astype(F32)
        nxt = jnp.where(row == tb - 1, after, pltpu.roll(t, tb - 1, 0))
        dp_ref[...] = (d - t + nxt).astype(dp_ref.dtype)

        @pl.when(i == 0)
        def _():
            dmu_ref[...] = jnp.zeros_like(dmu_ref)

        dmu_ref[...] += jnp.sum(d * (prev - x), axis=0, keepdims=True)

    blk = (3 * tb + 16) * W * 4
    return pl.pallas_call(
        body, name=name, grid=(nb,),
        in_specs=[pl.BlockSpec((tb, W), lambda i: (i, 0)),
                  pl.BlockSpec((8, W), lambda i: (jnp.minimum((i + 1) * hb, S // 8 - 1), 0)),
                  pl.BlockSpec((tb, W), lambda i: (i, 0)),
                  pl.BlockSpec((8, W), lambda i: (jnp.maximum(i * hb - 1, 0), 0)),
                  pl.BlockSpec((1, W), lambda i: (0, 0))],
        out_specs=[pl.BlockSpec((tb, W), lambda i: (i, 0)), pl.BlockSpec((1, W), lambda i: (0, 0))],
        out_shape=[jax.ShapeDtypeStruct((S, W), BF16), jax.ShapeDtypeStruct((1, W), F32)],
        compiler_params=_cparams(("arbitrary",), blk),
    )(dxs, dxs, p, p, mu)


def wkv_fwd(r, lw, k, v, a, b):
    H, S, N = r.shape
    C, nc = WKV_CHUNK, S // WKV_CHUNK

    def body(r_ref, lw_ref, k_ref, v_ref, a_ref, b_ref, y_ref, st_ref, state):
        @pl.when(pl.program_id(1) == 0)
        def _():
            state[...] = jnp.zeros_like(state)

        s0 = state[...]
        st_ref[...] = s0
        y, s1 = _wkv_chunk(s0, r_ref[...], lw_ref[...], k_ref[...], v_ref[...], a_ref[...], b_ref[...])
        y_ref[...] = y
        state[...] = s1

    seq = pl.BlockSpec((None, C, N), lambda h, c: (h, c, 0))
    return pl.pallas_call(
        body, name="wkv_fwd", grid=(H, nc), in_specs=[seq] * 6,
        out_specs=[seq, pl.BlockSpec((None, None, N, N), lambda h, c: (h, c, 0, 0))],
        out_shape=[jax.ShapeDtypeStruct((H, S, N), F32), jax.ShapeDtypeStruct((H, nc, N, N), F32)],
        scratch_shapes=[pltpu.VMEM((N, N), F32)],
        compiler_params=_cparams(("parallel", "arbitrary"), 8 * C * N * 4),
    )(r, lw, k, v, a, b)


def wkv_bwd(r, lw, k, v, a, b, states, dy):
    H, S, N = r.shape
    C, nc = WKV_CHUNK, S // WKV_CHUNK

    def body(r_ref, lw_ref, k_ref, v_ref, a_ref, b_ref, st_ref, dy_ref,
             dr_ref, dlw_ref, dk_ref, dv_ref, da_ref, db_ref, dstate):
        @pl.when(pl.program_id(1) == 0)
        def _():
            dstate[...] = jnp.zeros_like(dstate)

        _, pull = jax.vjp(_wkv_chunk, st_ref[...], r_ref[...], lw_ref[...], k_ref[...], v_ref[...],
                          a_ref[...], b_ref[...])
        ds0, dr, dlw, dk, dv, da, db = pull((dy_ref[...], dstate[...]))
        dstate[...] = ds0
        dr_ref[...], dlw_ref[...], dk_ref[...] = dr, dlw, dk
        dv_ref[...], da_ref[...], db_ref[...] = dv, da, db

    seq = pl.BlockSpec((None, C, N), lambda h, c: (h, nc - 1 - c, 0))
    st = pl.BlockSpec((None, None, N, N), lambda h, c: (h, nc - 1 - c, 0, 0))
    return pl.pallas_call(
        body, name="wkv_bwd", grid=(H, nc), in_specs=[seq] * 6 + [st, seq],
        out_specs=[seq] * 6, out_shape=[jax.ShapeDtypeStruct((H, S, N), F32)] * 6,
        scratch_shapes=[pltpu.VMEM((N, N), F32)],
        compiler_params=_cparams(("parallel", "arbitrary"), 16 * C * N * 4),
    )(r, lw, k, v, a, b, states, dy)


def _first_flag(i, seq_len):
    per_group = ATTN_HPG * seq_len // ATTN_BLK
    g = i // per_group
    per_seq = [seq_len // d // ATTN_BLK for _, d in ATTN_PAIRS]
    n = jnp.where(g == 0, per_seq[0], jnp.where(g == 1, per_seq[1], per_seq[2]))
    return (lax.rem(i, n) == 0).astype(F32)


def attn_fwd(q, k, v, q_gain, k_gain, seq_len):
    R, N = q.shape
    nb = R // ATTN_BLK

    def body(q_ref, kc_ref, kp_ref, vc_ref, vp_ref, qg_ref, kg_ref, o_ref, lse_ref):
        first = _first_flag(pl.program_id(0), seq_len)
        o, lse = _attn_block(q_ref[...], kc_ref[...], kp_ref[...], vc_ref[...], vp_ref[...],
                             qg_ref[...], kg_ref[...], first)
        o_ref[...] = o
        lse_ref[...] = lse

    cur = pl.BlockSpec((ATTN_BLK, N), lambda i: (i, 0))
    prv = pl.BlockSpec((ATTN_BLK, N), lambda i: (jnp.maximum(i - 1, 0), 0))
    gain = pl.BlockSpec((1, N), lambda i: (0, 0))
    return pl.pallas_call(
        body, name="attn_fwd", grid=(nb,), in_specs=[cur, cur, prv, cur, prv, gain, gain],
        out_specs=[cur, cur], out_shape=[jax.ShapeDtypeStruct((R, N), F32)] * 2,
        compiler_params=_cparams(("parallel",), 8 * ATTN_BLK * 128 * 4),
    )(q, k, k, v, v, q_gain, k_gain)


def attn_bwd(q, k, v, q_gain, k_gain, do, dlse, seq_len):
    R, N = q.shape
    nb = R // ATTN_BLK

    def body(q_ref, kc_ref, kp_ref, vc_ref, vp_ref, qg_ref, kg_ref, do_ref, dl_ref,
             dq_ref, dk_ref, dv_ref, dqg_ref, dkg_ref, carry_k, carry_v):
        step = pl.program_id(0)
        first = _first_flag(nb - 1 - step, seq_len)

        @pl.when(step == 0)
        def _():
            carry_k[...] = jnp.zeros_like(carry_k)
            carry_v[...] = jnp.zeros_like(carry_v)
            dqg_ref[...] = jnp.zeros_like(dqg_ref)
            dkg_ref[...] = jnp.zeros_like(dkg_ref)

        _, pull = jax.vjp(functools.partial(_attn_block, first=first), q_ref[...], kc_ref[...], kp_ref[...],
                          vc_ref[...], vp_ref[...], qg_ref[...], kg_ref[...])
        dq, dkc, dkp, dvc, dvp, dqg, dkg = pull((do_ref[...], dl_ref[...]))
        dq_ref[...] = dq
        dk_ref[...] = dkc + carry_k[...]
        dv_ref[...] = dvc + carry_v[...]
        carry_k[...] = dkp
        carry_v[...] = dvp
        dqg_ref[...] += dqg
        dkg_ref[...] += dkg

    cur = pl.BlockSpec((ATTN_BLK, N), lambda i: (nb - 1 - i, 0))
    prv = pl.BlockSpec((ATTN_BLK, N), lambda i: (jnp.maximum(nb - 2 - i, 0), 0))
    gain = pl.BlockSpec((1, N), lambda i: (0, 0))
    return pl.pallas_call(
        body, name="attn_bwd", grid=(nb,), in_specs=[cur, cur, prv, cur, prv, gain, gain, cur, cur],
        out_specs=[cur, cur, cur, gain, gain],
        out_shape=[jax.ShapeDtypeStruct((R, N), F32)] * 3 + [jax.ShapeDtypeStruct((1, N), F32)] * 2,
        scratch_shapes=[pltpu.VMEM((ATTN_BLK, N), F32)] * 2,
        compiler_params=_cparams(("arbitrary",), 16 * ATTN_BLK * 128 * 4),
    )(q, k, k, v, v, q_gain, k_gain, do, dlse)


def to_subsequences(t):
    S = t.shape[0]
    parts = []
    for gi, (_, d) in enumerate(ATTN_PAIRS):
        tg = t[:, GROUP_W * gi:GROUP_W * (gi + 1)].reshape(S // d, d, ATTN_HPG, HEAD_DIM)
        parts.append(tg.transpose(1, 2, 0, 3).reshape(ATTN_HPG * S, HEAD_DIM))
    return jnp.concatenate(parts, axis=0)


def from_subsequences(u, S):
    parts = []
    for gi, (_, d) in enumerate(ATTN_PAIRS):
        ug = u[ATTN_HPG * S * gi:ATTN_HPG * S * (gi + 1)].reshape(d, ATTN_HPG, S // d, HEAD_DIM)
        parts.append(ug.transpose(2, 0, 1, 3).reshape(S, GROUP_W))
    return jnp.concatenate(parts, axis=1)


def to_heads(t):
    return t.reshape(t.shape[0], RWKV_HEADS, HEAD_DIM).transpose(1, 0, 2)


def from_heads(t):
    return t.transpose(1, 0, 2).reshape(t.shape[1], RWKV_HEADS * HEAD_DIM)


def _ffn_fwd(x, norm, w_in, w_out, tag):
    h = rowmap(_rms, [x], [norm], [(D_MODEL, BF16)], tb=512, name=tag + "_norm")[0]
    gu = matmul(h, w_in, "nn", tag + "_in")
    act = rowmap(_swiglu_act, [gu], [], [(D_FF, BF16)], tb=256, name=tag + "_act")[0]
    y = matmul(act, w_out, "nn", tag + "_out", add=x, scale=0.5)
    return y, (x, h, gu, act)


def _ffn_bwd(dy, saved, norm, w_in, w_out, tag):
    x, h, gu, act = saved
    dact = matmul(dy, w_out, "nt", tag + "_dact", scale=0.5)
    dw_out = matmul(act, dy, "tn", tag + "_dwout", scale=0.5)

    def act_bwd(gu_b, dact_b):
        return jax.vjp(_swiglu_act, gu_b)[1](dact_b)[0]

    dgu = rowmap(act_bwd, [gu, dact], [], [(2 * D_FF, BF16)], tb=256, name=tag + "_dgu")[0]
    dh = matmul(dgu, w_in, "nt", tag + "_dh")
    dw_in = matmul(h, dgu, "tn", tag + "_dwin")

    def norm_bwd(x_b, dh_b, dy_b, g):
        dx, dg = jax.vjp(_rms, x_b, g)[1](dh_b)
        return dy_b + dx, dg

    dx, dnorm = rowmap(norm_bwd, [x, dh, dy], [norm], [(D_MODEL, F32)], [(1, D_MODEL)], tb=256,
                       name=tag + "_dnorm")
    return dx, dnorm, dw_in, dw_out


def layer_step(x, tgt, W, P):
    S = x.shape[0]
    seg = (jnp.arange(D_MODEL)[:, None] // HEAD_DIM == jnp.arange(RWKV_HEADS)[None, :]).astype(F32)
    seg_t = seg.T
    w_rkv, w_lora = W["w_in"][:, :RKV], W["w_in"][:, RKV:RKV + LORA]
    w_qkv = W["w_in"][:, RKV + LORA:RKV + LORA + 3 * ATTN_WIDTH]
    w_gate = W["w_in"][:, RKV + LORA + 3 * ATTN_WIDTH:]
    mu_rk, mu_lo = P["rwkv_mu"][:, :RKV], P["rwkv_mu"][:, RKV:]
    zeros = lambda n: jnp.zeros((n, D_MODEL), F32)
    w2p = jnp.concatenate([W["rwkv_w2"], zeros(LORA - LORA_W)], axis=0)
    a2p = jnp.concatenate([zeros(LORA_W), W["rwkv_a2"], zeros(LORA_G)], axis=0)
    g2p = jnp.concatenate([zeros(LORA_W + LORA_A), W["rwkv_g2"]], axis=0)
    pre_params = [P["rwkv_w0"], w2p, P["rwkv_a0"], a2p, g2p, P["rwkv_k_k"], P["rwkv_k_a"], seg, seg_t]
    post_params = [P["rwkv_r_k"], P["rwkv_ln_w"], P["rwkv_ln_b"], seg, seg_t]
    col = lambda arr, j: (arr, D_MODEL, j)

    x1, ffn1_saved = _ffn_fwd(x, P["ffn1_norm"], W["ffn1_w_in"], W["ffn1_w_out"], "ffn1")
    h = rowmap(_rms, [x1], [P["mix_norm"]], [(D_MODEL, BF16)], tb=512, name="mix_norm")[0]
    p_rk = matmul(h, w_rkv, "nn", "proj_rkv")
    p_lo = matmul(h, w_lora, "nn", "proj_lora")
    p_qkv = matmul(h, w_qkv, "nn", "proj_qkv")
    p_gate = matmul(h, w_gate, "nn", "proj_gate")
    xs_rk = token_shift_fwd(p_rk, mu_rk, tb=256, name="shift_rk")
    xs_lo = token_shift_fwd(p_lo, mu_lo, tb=256, name="shift_lora")
    lw, k_mod, a_neg, b_kk, g = rowmap(
        _rwkv_pre, [xs_rk, xs_lo], pre_params, [(D_MODEL, F32)] * 5, tb=256, name="rwkv_pre")
    r_h, v_h = to_heads(xs_rk[:, :D_MODEL]), to_heads(xs_rk[:, 2 * D_MODEL:])
    lw_h, k_h, a_h, b_h = to_heads(lw), to_heads(k_mod), to_heads(a_neg), to_heads(b_kk)
    wkv_h, states = wkv_fwd(r_h, lw_h, k_h, v_h, a_h, b_h)
    wkv = from_heads(wkv_h)
    post_rows = [wkv, col(xs_rk, 0), k_mod, col(xs_rk, 2), g]
    y_a = rowmap(_rwkv_post, post_rows, post_params, [(D_MODEL, BF16)], tb=256, name="rwkv_post")[0]

    q_s = to_subsequences(p_qkv[:, :ATTN_WIDTH])
    k_s = to_subsequences(p_qkv[:, ATTN_WIDTH:2 * ATTN_WIDTH])
    v_s = to_subsequences(p_qkv[:, 2 * ATTN_WIDTH:])
    o_s, lse_s = attn_fwd(q_s, k_s, v_s, P["attn_q_norm"], P["attn_k_norm"], S)
    o, lse = from_subsequences(o_s, S), from_subsequences(lse_s, S)
    y_b = rowmap(_group_combine, [o, lse], [], [(ATTN_WIDTH, BF16)], tb=512, name="attn_combine")[0]

    pa = matmul(y_a, W["w_proj_rwkv"], "nn", "proj_a")
    pb = matmul(y_b, W["w_proj_attn"], "nn", "proj_b")
    merged = rowmap(_gate_merge, [p_gate, pa, pb], [P["b_gate"]], [(D_MODEL, BF16)], tb=256, name="merge")[0]
    x2 = matmul(merged, W["w_out"], "nn", "mix_out", add=x1)
    x3, ffn2_saved = _ffn_fwd(x2, P["ffn2_norm"], W["ffn2_w_in"], W["ffn2_w_out"], "ffn2")

    def loss_head(y_b_, t_b):
        err = y_b_ - t_b
        return err * (1.0 / D_MODEL), (0.5 / D_MODEL) * jnp.sum(err * err, axis=0, keepdims=True)

    dx3, loss_cols = rowmap(loss_head, [x3, tgt], [], [(D_MODEL, F32)], [(1, D_MODEL)], tb=512, name="loss")

    gW, gP = {}, {}
    dx2, gP["ffn2_norm"], gW["ffn2_w_in"], gW["ffn2_w_out"] = _ffn_bwd(
        dx3, ffn2_saved, P["ffn2_norm"], W["ffn2_w_in"], W["ffn2_w_out"], "ffn2")

    dmerged = matmul(dx2, W["w_out"], "nt", "d_merged")
    gW["w_out"] = matmul(merged, dx2, "tn", "dw_out")

    def merge_bwd(pg, pa_b, pb_b, dm, bg):
        return jax.vjp(_gate_merge, pg, pa_b, pb_b, bg)[1](dm)

    dp_gate, dpa, dpb, gP["b_gate"] = rowmap(
        merge_bwd, [p_gate, pa, pb, dmerged], [P["b_gate"]],
        [(2 * D_MODEL, BF16), (D_MODEL, BF16), (D_MODEL, BF16)], [(1, 2 * D_MODEL)], tb=256, name="merge_bwd")
    dy_a = matmul(dpa, W["w_proj_rwkv"], "nt", "d_ya")
    gW["w_proj_rwkv"] = matmul(y_a, dpa, "tn", "dw_proj_a")
    dy_b = matmul(dpb, W["w_proj_attn"], "nt", "d_yb")
    gW["w_proj_attn"] = matmul(y_b, dpb, "tn", "dw_proj_b")

    def combine_bwd(o_b, l_b, d_b):
        return jax.vjp(_group_combine, o_b, l_b)[1](d_b)

    do, dlse = rowmap(combine_bwd, [o, lse, dy_b], [], [(ATTN_WIDTH, F32)] * 2, tb=256, name="attn_combine_bwd")
    dq_s, dk_s, dv_s, gP["attn_q_norm"], gP["attn_k_norm"] = attn_bwd(
        q_s, k_s, v_s, P["attn_q_norm"], P["attn_k_norm"], to_subsequences(do), to_subsequences(dlse), S)
    dp_qkv = jnp.concatenate([from_subsequences(t, S) for t in (dq_s, dk_s, dv_s)], axis=1).astype(BF16)

    def post_bwd(wkv_b, r_b, k_b, v_b, g_b, d_b, r_k, ln_w, ln_b, sg, sgt):
        f = lambda *a: _rwkv_post(*a, sg, sgt)
        return jax.vjp(f, wkv_b, r_b, k_b, v_b, g_b, r_k, ln_w, ln_b)[1](d_b)

    dwkv, dr_p, dk_p, dv_p, dg, gP["rwkv_r_k"], gP["rwkv_ln_w"], gP["rwkv_ln_b"] = rowmap(
        post_bwd, post_rows + [dy_a], post_params, [(D_MODEL, F32)] * 5, [(1, D_MODEL)] * 3, tb=128,
        name="rwkv_post_bwd")
    dr_h, dlw_h, dk_h, dv_h, da_h, db_h = wkv_bwd(r_h, lw_h, k_h, v_h, a_h, b_h, states, to_heads(dwkv))
    dr_w, dlw, dk_w, dv_w, da_neg, db_kk = [from_heads(t) for t in (dr_h, dlw_h, dk_h, dv_h, da_h, db_h)]

    def pre_bwd(xrk_b, xlo_b, dlw_b, dkw_b, dkp_b, da_b, db_b, dg_b, drp_b, drw_b, dvp_b, dvw_b,
                w0, w2, a0, a2, g2, k_k, k_a, sg, sgt):
        f = lambda *a: _rwkv_pre(*a, sg, sgt)
        pull = jax.vjp(f, xrk_b, xlo_b, w0, w2, a0, a2, g2, k_k, k_a)[1]
        dxrk, dxlo, *dpar = pull((dlw_b, dkw_b + dkp_b, da_b, db_b, dg_b))
        direct = jnp.concatenate([drp_b + drw_b, jnp.zeros_like(drp_b), dvp_b + dvw_b], axis=1)
        return (dxrk + direct, dxlo, *dpar)

    pre_rows = [xs_rk, xs_lo, dlw, dk_w, dk_p, da_neg, db_kk, dg, dr_p, dr_w, dv_p, dv_w]
    dxs_rk, dxs_lo, gP["rwkv_w0"], dw2p, gP["rwkv_a0"], da2p, dg2p, gP["rwkv_k_k"], gP["rwkv_k_a"] = rowmap(
        pre_bwd, pre_rows, pre_params, [(RKV, F32), (LORA, F32)],
        [(1, D_MODEL), (LORA, D_MODEL), (1, D_MODEL), (LORA, D_MODEL), (LORA, D_MODEL), (1, D_MODEL), (1, D_MODEL)],
        tb=128, name="rwkv_pre_bwd")
    gW["rwkv_w2"] = dw2p[:LORA_W]
    gW["rwkv_a2"] = da2p[LORA_W:LORA_W + LORA_A]
    gW["rwkv_g2"] = dg2p[LORA_W + LORA_A:]
    dp_rk, dmu_rk = token_shift_bwd(dxs_rk, p_rk, mu_rk, tb=256, name="shift_rk_bwd")
    dp_lo, dmu_lo = token_shift_bwd(dxs_lo, p_lo, mu_lo, tb=256, name="shift_lora_bwd")
    gP["rwkv_mu"] = jnp.concatenate([dmu_rk, dmu_lo], axis=1)

    dh = matmul(dp_rk, w_rkv, "nt", "dh_rkv")
    dh = matmul(dp_lo, w_lora, "nt", "dh_lora", add=dh)
    dh = matmul(dp_qkv, w_qkv, "nt", "dh_qkv", add=dh)
    dh = matmul(dp_gate, w_gate, "nt", "dh_gate", add=dh)
    gW["w_in"] = jnp.concatenate([
        matmul(h, dp_rk, "tn", "dw_rkv"), matmul(h, dp_lo, "tn", "dw_lora"),
        matmul(h, dp_qkv, "tn", "dw_qkv"), matmul(h, dp_gate, "tn", "dw_gate")], axis=1)

    def norm_bwd(x_b, dh_b, dy_b, gn):
        dx, dgn = jax.vjp(_rms, x_b, gn)[1](dh_b)
        return dy_b + dx, dgn

    dx1, gP["mix_norm"] = rowmap(norm_bwd, [x1, dh, dx2], [P["mix_norm"]], [(D_MODEL, F32)], [(1, D_MODEL)],
                                 tb=256, name="mix_norm_bwd")
    dx, gP["ffn1_norm"], gW["ffn1_w_in"], gW["ffn1_w_out"] = _ffn_bwd(
        dx1, ffn1_saved, P["ffn1_norm"], W["ffn1_w_in"], W["ffn1_w_out"], "ffn1")
    return loss_cols, dx, gW, gP


N_SHARDS = 4
BIG = (("ffn1_w_in", (D_MODEL, 2 * D_FF), 1), ("ffn1_w_out", (D_FF, D_MODEL), 0),
       ("w_in", (D_MODEL, 7712), 1), ("rwkv_w2", (LORA_W, D_MODEL), 1), ("rwkv_a2", (LORA_A, D_MODEL), 1),
       ("rwkv_g2", (LORA_G, D_MODEL), 1), ("w_proj_rwkv", (D_MODEL, D_MODEL), 0),
       ("w_proj_attn", (ATTN_WIDTH, D_MODEL), 1), ("w_out", (D_MODEL, D_MODEL), 0),
       ("ffn2_w_in", (D_MODEL, 2 * D_FF), 1), ("ffn2_w_out", (D_FF, D_MODEL), 0))
SMALL = (("ffn1_norm", 1024), ("mix_norm", 1024), ("b_gate", 2048), ("rwkv_mu", 3360), ("rwkv_w0", 1024),
         ("rwkv_a0", 1024), ("rwkv_k_k", 1024), ("rwkv_k_a", 1024), ("rwkv_r_k", 1024), ("rwkv_ln_w", 1024),
         ("rwkv_ln_b", 1024), ("attn_q_norm", 64), ("attn_k_norm", 64), ("ffn2_norm", 1024))
WEIGHT_ORDER = ("ffn1_norm", "ffn1_w_in", "ffn1_w_out", "mix_norm", "w_in", "b_gate", "rwkv_mu", "rwkv_w0",
                "rwkv_w2", "rwkv_a0", "rwkv_a2", "rwkv_g2", "rwkv_k_k", "rwkv_k_a", "rwkv_r_k", "rwkv_ln_w",
                "rwkv_ln_b", "attn_q_norm", "attn_k_norm", "w_proj_rwkv", "w_proj_attn", "w_out", "ffn2_norm",
                "ffn2_w_in", "ffn2_w_out")


def _shard_shape(shape, axis):
    return tuple(s // N_SHARDS if i == axis else s for i, s in enumerate(shape))


def _pack_rows():
    return [(_shard_shape(shape, axis)[0] * _shard_shape(shape, axis)[1]) // PACK_W for _, shape, axis in BIG]


PACK_USED = sum(_pack_rows())
PACK_ROWS = -(-PACK_USED // PACK_ROW_BLOCK) * PACK_ROW_BLOCK
SMALL_USED = D_MODEL + sum(n for _, n in SMALL)
SMALL_W = -(-SMALL_USED // 128) * 128


def pack_shards(shards, dtype):
    parts = [shards[name].astype(dtype).reshape(-1, PACK_W) for name, _, _ in BIG]
    parts.append(jnp.zeros((PACK_ROWS - PACK_USED, PACK_W), dtype))
    return jnp.concatenate(parts, axis=0)


def unpack_shards(packed):
    out, row = {}, 0
    for (name, shape, axis), n in zip(BIG, _pack_rows()):
        out[name] = packed[row:row + n].reshape(_shard_shape(shape, axis))
        row += n
    return out


def unpack_full(gathered):
    out, row = {}, 0
    for (name, shape, axis), n in zip(BIG, _pack_rows()):
        blocks = gathered[:, row:row + n].reshape((N_SHARDS,) + _shard_shape(shape, axis))
        if axis == 0:
            out[name] = blocks.reshape(shape)
        else:
            out[name] = blocks.transpose(1, 0, 2).reshape(shape)
        row += n
    return out


def pack_full(full):
    parts = []
    for (name, shape, axis), n in zip(BIG, _pack_rows()):
        t = full[name]
        if axis == 0:
            blocks = t.reshape((N_SHARDS,) + _shard_shape(shape, axis))
        else:
            blocks = t.reshape(shape[0], N_SHARDS, shape[1] // N_SHARDS).transpose(1, 0, 2)
        parts.append(blocks.reshape(N_SHARDS, n, PACK_W))
    parts.append(jnp.zeros((N_SHARDS, PACK_ROWS - PACK_USED, PACK_W), F32))
    return jnp.concatenate(parts, axis=1)


def pack_small(vals, head):
    parts = [head] + [vals[name].reshape(1, n) for name, n in SMALL]
    parts.append(jnp.zeros((1, SMALL_W - SMALL_USED), F32))
    return jnp.concatenate(parts, axis=1)


def unpack_small(vec, shapes):
    out, off = {}, D_MODEL
    for name, n in SMALL:
        out[name] = vec[:, off:off + n].reshape(shapes[name])
        off += n
    return out


def _place():
    return lax.axis_index("x"), lax.axis_index("y"), lax.axis_index("c")


def _other_chips(x, y):
    return [(1 - x, y), (x, 1 - y), (1 - x, 1 - y)]


def gather_weights(packed):
    R, Wd = packed.shape

    def body(w_ref, out_ref, send_sems, recv_sems, local_sem):
        x, y, c = _place()
        mine = pltpu.make_async_copy(w_ref, out_ref.at[2 * x + y], local_sem)
        mine.start()
        sends = [pltpu.make_async_remote_copy(
            src_ref=w_ref, dst_ref=out_ref.at[2 * x + y], send_sem=send_sems.at[j], recv_sem=recv_sems.at[j],
            device_id=(px, py, c), device_id_type=MESH) for j, (px, py) in enumerate(_other_chips(x, y))]
        for cp in sends:
            cp.start()
        for j, (px, py) in enumerate(_other_chips(x, y)):
            pltpu.make_async_remote_copy(
                src_ref=w_ref, dst_ref=out_ref.at[2 * px + py], send_sem=send_sems.at[j],
                recv_sem=recv_sems.at[j], device_id=(px, py, c), device_id_type=MESH).wait_recv()
        for cp in sends:
            cp.wait_send()
        mine.wait()

    return pl.pallas_call(
        body, name="gather_weights",
        in_specs=[pl.BlockSpec(memory_space=pl.ANY)], out_specs=pl.BlockSpec(memory_space=pl.ANY),
        out_shape=jax.ShapeDtypeStruct((N_SHARDS, R, Wd), packed.dtype),
        scratch_shapes=[pltpu.SemaphoreType.DMA((3,)), pltpu.SemaphoreType.DMA((3,)), pltpu.SemaphoreType.DMA],
        compiler_params=pltpu.CompilerParams(has_side_effects=True),
    )(packed)


def scatter_grads(grads):
    _, R, Wd = grads.shape

    def body(g_ref, land_ref, send_sems, recv_sems, local_sem):
        x, y, c = _place()
        mine = pltpu.make_async_copy(g_ref.at[2 * x + y], land_ref.at[3], local_sem)
        mine.start()
        sends = [pltpu.make_async_remote_copy(
            src_ref=g_ref.at[2 * px + py], dst_ref=land_ref.at[j], send_sem=send_sems.at[j],
            recv_sem=recv_sems.at[j], device_id=(px, py, c), device_id_type=MESH)
            for j, (px, py) in enumerate(_other_chips(x, y))]
        for cp in sends:
            cp.start()
        for cp in sends:
            cp.wait_recv()
        for cp in sends:
            cp.wait_send()
        mine.wait()

    return pl.pallas_call(
        body, name="scatter_grads",
        in_specs=[pl.BlockSpec(memory_space=pl.ANY)], out_specs=pl.BlockSpec(memory_space=pl.ANY),
        out_shape=jax.ShapeDtypeStruct((4, R, Wd), grads.dtype),
        scratch_shapes=[pltpu.SemaphoreType.DMA((3,)), pltpu.SemaphoreType.DMA((3,)), pltpu.SemaphoreType.DMA],
        compiler_params=pltpu.CompilerParams(has_side_effects=True),
    )(grads)


def sum_landed(landed):
    _, R, Wd = landed.shape
    tb = PACK_ROW_BLOCK

    def body(a_ref, b_ref, c_ref, d_ref, o_ref):
        o_ref[...] = ((a_ref[...] + b_ref[...]) + c_ref[...]) + d_ref[...]

    specs = [pl.BlockSpec((None, tb, Wd), functools.partial(lambda k, i: (k, i, 0), k)) for k in range(4)]
    return pl.pallas_call(
        body, name="sum_landed", grid=(R // tb,), in_specs=specs,
        out_specs=pl.BlockSpec((tb, Wd), lambda i: (i, 0)), out_shape=jax.ShapeDtypeStruct((R, Wd), F32),
        compiler_params=_cparams(("parallel",), 5 * tb * Wd * 4),
    )(landed, landed, landed, landed)


def swap_with_sibling(part):
    def body(p_ref, got_ref, send_sem, recv_sem):
        x, y, c = _place()
        cp = pltpu.make_async_remote_copy(src_ref=p_ref, dst_ref=got_ref, send_sem=send_sem, recv_sem=recv_sem,
                                          device_id=(x, y, 1 - c), device_id_type=MESH)
        cp.start()
        cp.wait()

    return pl.pallas_call(
        body, name="swap_with_sibling",
        in_specs=[pl.BlockSpec(memory_space=pl.ANY)], out_specs=pl.BlockSpec(memory_space=pl.ANY),
        out_shape=jax.ShapeDtypeStruct(part.shape, part.dtype),
        scratch_shapes=[pltpu.SemaphoreType.DMA, pltpu.SemaphoreType.DMA],
        compiler_params=pltpu.CompilerParams(has_side_effects=True),
    )(part)


def adamw_packed(w, g_own, g_sib, m, v):
    def f(w_b, ga, gb, m_b, v_b):
        g = ga + gb
        delta, m2, v2 = _adamw(w_b, g, m_b, v_b)
        return g, delta, m2, v2

    return rowmap(f, [w, g_own, g_sib, m, v], [], [(PACK_W, F32)] * 4, tb=PACK_ROW_BLOCK, name="adamw_packed")


def reduce_small(vec, w, m, v):
    n_dev = 8

    def body(vec_ref, w_ref, m_ref, v_ref, loss_ref, g_ref, d_ref, m2_ref, v2_ref, slots, send_sems, recv_sems):
        x, y, c = _place()
        me = 4 * x + 2 * y + c
        slots[me] = vec_ref[...]
        flips = [(fx, fy, fc) for fx in (0, 1) for fy in (0, 1) for fc in (0, 1)][1:]
        peers = [(1 - x if fx else x, 1 - y if fy else y, 1 - c if fc else c) for fx, fy, fc in flips]
        sends = [pltpu.make_async_remote_copy(
            src_ref=vec_ref, dst_ref=slots.at[me], send_sem=send_sems.at[j], recv_sem=recv_sems.at[j],
            device_id=peer, device_id_type=MESH) for j, peer in enumerate(peers)]
        for cp in sends:
            cp.start()
        for j, (px, py, pc) in enumerate(peers):
            pltpu.make_async_remote_copy(
                src_ref=vec_ref, dst_ref=slots.at[4 * px + 2 * py + pc], send_sem=send_sems.at[j],
                recv_sem=recv_sems.at[j], device_id=(px, py, pc), device_id_type=MESH).wait_recv()
        for cp in sends:
            cp.wait_send()
        g = slots[0]
        for d in range(1, n_dev):
            g = g + slots[d]
        loss_ref[...] = jnp.sum(g[:, :D_MODEL], axis=1, keepdims=True)
        delta, m2, v2 = _adamw(w_ref[...], g, m_ref[...], v_ref[...])
        g_ref[...], d_ref[...], m2_ref[...], v2_ref[...] = g, delta, m2, v2

    vm = pl.BlockSpec(memory_space=pltpu.VMEM)
    vec_t = jax.ShapeDtypeStruct(vec.shape, F32)
    return pl.pallas_call(
        body, name="reduce_small", in_specs=[vm] * 4, out_specs=[vm] * 5,
        out_shape=[jax.ShapeDtypeStruct((1, 1), F32)] + [vec_t] * 4,
        scratch_shapes=[pltpu.VMEM((n_dev,) + vec.shape, F32), pltpu.SemaphoreType.DMA((n_dev - 1,)),
                        pltpu.SemaphoreType.DMA((n_dev - 1,))],
        compiler_params=pltpu.CompilerParams(has_side_effects=True),
    )(vec, w, m, v)


def kernel(x, ffn1_norm, ffn1_w_in, ffn1_w_out, mix_norm, w_in, b_gate, rwkv_mu, rwkv_w0, rwkv_w2, rwkv_a0, rwkv_a2, rwkv_g2, rwkv_k_k, rwkv_k_a, rwkv_r_k, rwkv_ln_w, rwkv_ln_b, attn_q_norm, attn_k_norm, w_proj_rwkv, w_proj_attn, w_out, ffn2_norm, ffn2_w_in, ffn2_w_out, loss_target, m_ffn1_norm, m_ffn1_w_in, m_ffn1_w_out, m_mix_norm, m_w_in, m_b_gate, m_rwkv_mu, m_rwkv_w0, m_rwkv_w2, m_rwkv_a0, m_rwkv_a2, m_rwkv_g2, m_rwkv_k_k, m_rwkv_k_a, m_rwkv_r_k, m_rwkv_ln_w, m_rwkv_ln_b, m_attn_q_norm, m_attn_k_norm, m_w_proj_rwkv, m_w_proj_attn, m_w_out, m_ffn2_norm, m_ffn2_w_in, m_ffn2_w_out, v_ffn1_norm, v_ffn1_w_in, v_ffn1_w_out, v_mix_norm, v_w_in, v_b_gate, v_rwkv_mu, v_rwkv_w0, v_rwkv_w2, v_rwkv_a0, v_rwkv_a2, v_rwkv_g2, v_rwkv_k_k, v_rwkv_k_a, v_rwkv_r_k, v_rwkv_ln_w, v_rwkv_ln_b, v_attn_q_norm, v_attn_k_norm, v_w_proj_rwkv, v_w_proj_attn, v_w_out, v_ffn2_norm, v_ffn2_w_in, v_ffn2_w_out):
    given = dict(locals())
    weights = {n: given[n] for n in WEIGHT_ORDER}
    mom_m = {n: given["m_" + n] for n in WEIGHT_ORDER}
    mom_v = {n: given["v_" + n] for n in WEIGHT_ORDER}
    big = [name for name, _, _ in BIG]
    shapes = {n: weights[n].shape for n in WEIGHT_ORDER}
    flat2d = lambda t: t.reshape(t.shape[-2], t.shape[-1]) if t.ndim == 3 else t

    w_packed = pack_shards({n: flat2d(weights[n]) for n in big}, F32)
    gathered = gather_weights(w_packed.astype(BF16))
    W = unpack_full(gathered)
    P = {n: weights[n].reshape(1, -1) for n, _ in SMALL}

    loss_cols, dx, gW, gP = layer_step(x[0], loss_target[0], W, P)

    landed = scatter_grads(pack_full(gW))
    part = sum_landed(landed)
    sib = swap_with_sibling(part)
    m_packed = pack_shards({n: flat2d(mom_m[n]) for n in big}, F32)
    v_packed = pack_shards({n: flat2d(mom_v[n]) for n in big}, F32)
    g_p, d_p, m_p, v_p = adamw_packed(w_packed, part, sib, m_packed, v_packed)
    out_g, out_d, out_m, out_v = [
        {n: t.reshape(shapes[n]) for n, t in unpack_shards(p).items()} for p in (g_p, d_p, m_p, v_p)]

    zero_head = jnp.zeros((1, D_MODEL), F32)
    vec = pack_small(gP, loss_cols)
    loss, g_s, d_s, m_s, v_s = reduce_small(
        vec, pack_small({n: weights[n] for n, _ in SMALL}, zero_head),
        pack_small({n: mom_m[n] for n, _ in SMALL}, zero_head),
        pack_small({n: mom_v[n] for n, _ in SMALL}, zero_head))
    for dst, src in ((out_g, g_s), (out_d, d_s), (out_m, m_s), (out_v, v_s)):
        dst.update(unpack_small(src, shapes))

    return (loss[0, 0], dx[None], *[out_g[n] for n in WEIGHT_ORDER], *[out_d[n] for n in WEIGHT_ORDER],
            *[out_m[n] for n in WEIGHT_ORDER], *[out_v[n] for n in WEIGHT_ORDER])
```

```python
import functools

import jax
import jax.numpy as jnp
from jax import lax
from jax.experimental import pallas as pl
from jax.experimental.pallas import tpu as pltpu

F32 = jnp.float32
BF16 = jnp.bfloat16
HI = lax.Precision.HIGHEST
MESH = pl.DeviceIdType.MESH

D_MODEL = 1024
HEAD_DIM = 64
RWKV_HEADS = 16
LORA_W, LORA_A, LORA_G = 64, 64, 160
LORA = LORA_W + LORA_A + LORA_G
RKV = 3 * D_MODEL
ATTN_PAIRS = ((128, 1), (512, 4), (2048, 16))
ATTN_BLK = 128
ATTN_HPG = 4
ATTN_WIDTH = 768
GROUP_W = ATTN_HPG * HEAD_DIM
D_FF = 2816
GN_EPS = 64e-5
RMS_EPS = 1e-6
NEG_INF = -1e30
WKV_CHUNK = 64
WKV_HEADS_PER_STEP = 4

ADAM_LR, ADAM_B1, ADAM_B2, ADAM_EPS, ADAM_WD, ADAM_STEP = 0.001, 0.9, 0.999, 1e-08, 0.01, 10

V7X_VMEM_BYTES = 64 << 20
VMEM_TEMP_ALLOWANCE = 20 << 20

PACK_W = 1024
PACK_ROW_BLOCK = 512


def _cparams(sem, block_bytes):
    limit = min(2 * block_bytes + VMEM_TEMP_ALLOWANCE, V7X_VMEM_BYTES - (6 << 20))
    return pltpu.CompilerParams(dimension_semantics=sem, vmem_limit_bytes=int(limit))


def _nbytes(shape, dtype):
    n = 1
    for s in shape:
        n *= s
    return n * jnp.dtype(dtype).itemsize


def _split_bf16(a):
    hi = a.astype(BF16)
    return hi, (a - hi.astype(F32)).astype(BF16)


def _make_dots(prec):
    def one(a, b, ca, cb):
        return lax.dot_general(a, b, (((ca,), (cb,)), ((), ())), precision=None if prec == "x3" else prec,
                               preferred_element_type=F32)

    def raw(a, b, ca, cb):
        if prec != "x3":
            return one(a, b, ca, cb)
        (ah, al), (bh, bl) = _split_bf16(a), _split_bf16(b)
        return one(ah, bh, ca, cb) + (one(al, bh, ca, cb) + one(ah, bl, ca, cb))

    @jax.custom_vjp
    def nn(a, b):
        return raw(a, b, 1, 0)

    @jax.custom_vjp
    def nt(a, b):
        return raw(a, b, 1, 1)

    @jax.custom_vjp
    def tn(a, b):
        return raw(a, b, 0, 0)

    nn.defvjp(lambda a, b: (raw(a, b, 1, 0), (a, b)),
              lambda res, g: (raw(g, res[1], 1, 1), raw(res[0], g, 0, 0)))
    nt.defvjp(lambda a, b: (raw(a, b, 1, 1), (a, b)),
              lambda res, g: (raw(g, res[1], 1, 0), raw(g, res[0], 0, 0)))
    tn.defvjp(lambda a, b: (raw(a, b, 0, 0), (a, b)),
              lambda res, g: (raw(res[1], g, 1, 1), raw(res[0], g, 1, 0)))
    return nn, nt, tn


NN, NT, TN = _make_dots(None)
NN_HI, NT_HI, TN_HI = _make_dots(HI)
NN_X3, NT_X3, TN_X3 = _make_dots("x3")


def _pick(n, cap):
    best = None
    for t in range(128, min(n, cap) + 1, 128):
        if n % t == 0:
            best = t
    return best or n


def matmul(a, b, mode, name, *, add=None, scale=1.0, out_dtype=F32):
    if mode == "nn":
        (M, K), (K2, N) = a.shape, b.shape
    elif mode == "nt":
        (M, K), (N, K2) = a.shape, b.shape
    else:
        (K, M), (K2, N) = a.shape, b.shape
    assert K == K2, (name, a.shape, b.shape)
    tm, tn, tk = _pick(M, 512), _pick(N, 512), _pick(K, 1024)
    nk = K // tk
    ca, cb = {"nn": (1, 0), "nt": (1, 1), "tn": (0, 0)}[mode]

    def body(*refs):
        if add is None:
            a_ref, b_ref, o_ref, acc_ref = refs
        else:
            a_ref, b_ref, add_ref, o_ref, acc_ref = refs
        k = pl.program_id(2)

        @pl.when(k == 0)
        def _():
            acc_ref[...] = jnp.zeros_like(acc_ref)

        acc_ref[...] += lax.dot_general(a_ref[...].astype(BF16), b_ref[...].astype(BF16),
                                        (((ca,), (cb,)), ((), ())), preferred_element_type=F32)

        @pl.when(k == nk - 1)
        def _():
            r = acc_ref[...] * scale
            if add is not None:
                r = add_ref[...] + r
            o_ref[...] = r.astype(o_ref.dtype)

    a_spec = (pl.BlockSpec((tk, tm), lambda i, j, k: (k, i)) if mode == "tn"
              else pl.BlockSpec((tm, tk), lambda i, j, k: (i, k)))
    b_spec = (pl.BlockSpec((tn, tk), lambda i, j, k: (j, k)) if mode == "nt"
              else pl.BlockSpec((tk, tn), lambda i, j, k: (k, j)))
    in_specs, args = [a_spec, b_spec], [a, b]
    blk = tm * tk * a.dtype.itemsize + tk * tn * b.dtype.itemsize + tm * tn * 8
    if add is not None:
        in_specs.append(pl.BlockSpec((tm, tn), lambda i, j, k: (i, j)))
        args.append(add)
        blk += tm * tn * 4
    return pl.pallas_call(
        body, name=name, grid=(M // tm, N // tn, nk),
        in_specs=in_specs, out_specs=pl.BlockSpec((tm, tn), lambda i, j, k: (i, j)),
        out_shape=jax.ShapeDtypeStruct((M, N), out_dtype),
        scratch_shapes=[pltpu.VMEM((tm, tn), F32)],
        compiler_params=_cparams(("parallel", "parallel", "arbitrary"), blk),
    )(*args)


def rowmap(f, rows, params, outs, accs=(), *, tb, name):
    rows = [r if isinstance(r, tuple) else (r, r.shape[1], 0) for r in rows]
    S = rows[0][0].shape[0]
    assert S % tb == 0, (name, S, tb)
    n_in, n_out = len(rows) + len(params), len(outs)

    def body(*refs):
        res = f(*[r[...] for r in refs[:n_in]])
        res = res if isinstance(res, (tuple, list)) else (res,)
        o_refs, a_refs = refs[n_in:n_in + n_out], refs[n_in + n_out:]
        for ref, val in zip(o_refs, res[:n_out]):
            ref[...] = val.astype(ref.dtype)
        if a_refs:
            @pl.when(pl.program_id(0) == 0)
            def _():
                for ref in a_refs:
                    ref[...] = jnp.zeros_like(ref)

            for ref, val in zip(a_refs, res[n_out:]):
                ref[...] += val.astype(F32)

    in_specs = [pl.BlockSpec((tb, w), functools.partial(lambda cb, i: (i, cb), cb)) for _, w, cb in rows]
    in_specs += [pl.BlockSpec(p.shape, lambda i: (0, 0)) for p in params]
    out_specs = [pl.BlockSpec((tb, w), lambda i: (i, 0)) for w, _ in outs]
    out_specs += [pl.BlockSpec(tuple(s), lambda i: (0, 0)) for s in accs]
    out_shape = [jax.ShapeDtypeStruct((S, w), dt) for w, dt in outs]
    out_shape += [jax.ShapeDtypeStruct(tuple(s), F32) for s in accs]
    blk = sum(tb * w * a.dtype.itemsize for a, w, _ in rows) + sum(_nbytes(p.shape, p.dtype) for p in params)
    blk += sum(_nbytes((tb, w), dt) for w, dt in outs) + sum(_nbytes(s, F32) for s in accs)
    res = pl.pallas_call(
        body, name=name, grid=(S // tb,), in_specs=in_specs, out_specs=out_specs, out_shape=out_shape,
        compiler_params=_cparams(("arbitrary",) if accs else ("parallel",), blk),
    )(*[r[0] for r in rows], *params)
    return res


def _rms(x, g):
    return x * lax.rsqrt(jnp.mean(x * x, axis=-1, keepdims=True) + RMS_EPS) * g


def _softplus(z):
    return jnp.maximum(z, 0.0) + jnp.log(1.0 + jnp.exp(-jnp.abs(z)))


def _swiglu_act(gu):
    gate, up = gu[:, :D_FF], gu[:, D_FF:]
    return gate * jax.nn.sigmoid(gate) * up


def _rwkv_pre(xrk, xlo, w0, w2p, a0, a2p, g2p, k_k, k_a, seg, seg_t):
    k = xrk[:, D_MODEL:2 * D_MODEL]
    w = -_softplus(-(w0 + NN(jnp.tanh(xlo), w2p))) - 0.5
    log_decay = -jnp.exp(w)
    a = jax.nn.sigmoid(a0 + NN(xlo, a2p))
    g = NN(jax.nn.sigmoid(xlo), g2p)
    kk = k * k_k
    norm = jnp.maximum(jnp.sqrt(NN_HI(kk * kk, seg)), 1e-12)
    kk = kk / NN_HI(norm, seg_t)
    k_mod = k * (1.0 + (a - 1.0) * k_a)
    return log_decay, k_mod, -kk, kk * a, g


def _rwkv_post(wkv, r, k_mod, v, g, r_k, ln_w, ln_b, seg, seg_t):
    inv_n = 1.0 / HEAD_DIM
    mean = NN_HI(wkv, seg) * inv_n
    cen = wkv - NN_HI(mean, seg_t)
    var = NN_HI(cen * cen, seg) * inv_n
    y = cen * NN_HI(lax.rsqrt(var + GN_EPS), seg_t) * ln_w + ln_b
    bonus = NN_HI(NN_HI(r * k_mod * r_k, seg), seg_t) * v
    return (y + bonus) * g


def _gate_merge(pgate, pa, pb, b_gate):
    sg = jax.nn.sigmoid(pgate + b_gate)
    return sg[:, :D_MODEL] * pa + sg[:, D_MODEL:] * pb


def _group_combine(o, lse):
    ls = [lse[:, GROUP_W * i:GROUP_W * (i + 1)] for i in range(3)]
    m = jnp.maximum(jnp.maximum(ls[0], ls[1]), ls[2])
    es = [jnp.exp(l - m) for l in ls]
    den = es[0] + es[1] + es[2]
    return jnp.concatenate([o[:, GROUP_W * i:GROUP_W * (i + 1)] * (es[i] / den) for i in range(3)], axis=1)


def _attn_block(q, kc, kp, vc, vp, q_gain, k_gain, first):
    qn = _rms(q, q_gain) * (HEAD_DIM ** -0.5)
    kcn, kpn = _rms(kc, k_gain), _rms(kp, k_gain)
    qi = lax.broadcasted_iota(jnp.int32, (ATTN_BLK, ATTN_BLK), 0)
    kj = lax.broadcasted_iota(jnp.int32, (ATTN_BLK, ATTN_BLK), 1)
    s_c = jnp.where(kj <= qi, NT(qn, kcn), NEG_INF)
    s_p = jnp.where((kj >= qi) & (first < 0.5), NT(qn, kpn), NEG_INF)
    m = jnp.maximum(jnp.max(s_c, axis=-1, keepdims=True), jnp.max(s_p, axis=-1, keepdims=True))
    e_c, e_p = jnp.exp(s_c - m), jnp.exp(s_p - m)
    den = jnp.sum(e_c, axis=-1, keepdims=True) + jnp.sum(e_p, axis=-1, keepdims=True)
    o = NN(e_c / den, vc) + NN(e_p / den, vp)
    lse = m + jnp.log(den)
    return o, jnp.broadcast_to(lse, (ATTN_BLK, HEAD_DIM))


def _each(f, *xs):
    return tuple(f(*args) for args in zip(*xs))


def _tri_inverse(n):
    c = n[0].shape[0]
    eye = (lax.broadcasted_iota(jnp.int32, (c, c), 0) == lax.broadcasted_iota(jnp.int32, (c, c), 1)).astype(F32)
    t, p, span = _each(lambda m: eye + m, n), n, 2
    while span < c:
        p = _each(NN_X3, p, p)
        t = _each(lambda t_, p_: t_ + NN_X3(t_, p_), t, p)
        span *= 2
    return t


@jax.custom_vjp
def _tri_solve(n, rhs):
    return _each(NN_X3, _tri_inverse(n), rhs)


def _tri_solve_fwd(n, rhs):
    t = _tri_inverse(n)
    x = _each(NN_X3, t, rhs)
    return x, (t, x)


def _tri_solve_bwd(res, dx):
    t, x = res
    drhs = _each(TN_X3, t, dx)
    return _each(NT_X3, drhs, x), drhs


_tri_solve.defvjp(_tri_solve_fwd, _tri_solve_bwd)


def _lower_ones(c):
    row = lax.broadcasted_iota(jnp.int32, (c, c), 0)
    col = lax.broadcasted_iota(jnp.int32, (c, c), 1)
    return (row >= col).astype(BF16)


def _ones_dot(ones, x, contract):
    hi, lo = _split_bf16(x)
    dims = (((contract,), (0,)), ((), ()))
    return (lax.dot_general(ones, hi, dims, preferred_element_type=F32)
            + lax.dot_general(ones, lo, dims, preferred_element_type=F32))


@jax.custom_vjp
def _cumsum_rows(x):
    return _ones_dot(_lower_ones(x.shape[0]), x, 1)


_cumsum_rows.defvjp(lambda x: (_ones_dot(_lower_ones(x.shape[0]), x, 1), None),
                    lambda _, g: (_ones_dot(_lower_ones(g.shape[0]), g, 0),))


def _wkv_chunk(s0, r, lw, k, v, a, b):
    c = r[0].shape[0]
    row = lax.broadcasted_iota(jnp.int32, (c, c), 0)
    col = lax.broadcasted_iota(jnp.int32, (c, c), 1)
    strict, incl = row > col, row >= col
    cat = lambda p, q: jnp.concatenate([p, q], axis=0)
    cum = _each(_cumsum_rows, lw)
    e_neg = _each(lambda c_: jnp.exp(-c_), cum)
    ar = _each(lambda a_, r_, c_, l_: cat(a_ * jnp.exp(c_ - l_), r_ * jnp.exp(c_)), a, r, cum, lw)
    b_t, k_t = _each(jnp.multiply, b, e_neg), _each(jnp.multiply, k, e_neg)
    p_b, p_k, p_s = _each(NT_X3, ar, b_t), _each(NT_X3, ar, k_t), _each(NT_X3, ar, s0)
    n_ab = _each(lambda p: jnp.where(strict, p[:c], 0.0), p_b)
    m_rb = _each(lambda p: jnp.where(incl, p[c:], 0.0), p_b)
    n_ak = _each(lambda p: jnp.where(strict, p[:c], 0.0), p_k)
    m_rk = _each(lambda p: jnp.where(incl, p[c:], 0.0), p_k)
    u = _tri_solve(n_ab, _each(lambda p, n_, v_: p[:c] + NN_X3(n_, v_), p_s, n_ak, v))
    y = _each(lambda p, mb, u_, mk, v_: p[c:] + NN_X3(mb, u_) + NN_X3(mk, v_), p_s, m_rb, u, m_rk, v)
    g_end = _each(lambda l_: jnp.exp(jnp.sum(l_, axis=0, keepdims=True)), lw)
    s1 = _each(lambda s_, g_, u_, v_, b_, k_: s_ * g_ + TN_X3(cat(u_, v_), cat(b_, k_) * g_),
               s0, g_end, u, v, b_t, k_t)
    return y, s1


def _adamw(w, g, m, v):
    m = ADAM_B1 * m + (1.0 - ADAM_B1) * g
    v = ADAM_B2 * v + (1.0 - ADAM_B2) * jnp.square(g)
    m_hat = m / (1.0 - ADAM_B1 ** ADAM_STEP)
    v_hat = v / (1.0 - ADAM_B2 ** ADAM_STEP)
    delta = -ADAM_LR * (m_hat / (jnp.sqrt(v_hat) + ADAM_EPS) + ADAM_WD * w)
    return delta, m, v


def token_shift_fwd(p, mu, *, tb, name):
    S, W = p.shape
    hb = tb // 8

    def body(p_ref, halo_ref, mu_ref, o_ref):
        i = pl.program_id(0)
        x = p_ref[...]
        before = halo_ref[7:8, :] * (i > 0).astype(F32)
        row = lax.broadcasted_iota(jnp.int32, (tb, W), 0)
        prev = jnp.where(row == 0, before, pltpu.roll(x, 1, 0))
        o_ref[...] = x + (prev - x) * mu_ref[...]

    blk = (2 * tb + 8) * W * 4
    return pl.pallas_call(
        body, name=name, grid=(S // tb,),
        in_specs=[pl.BlockSpec((tb, W), lambda i: (i, 0)),
                  pl.BlockSpec((8, W), lambda i: (jnp.maximum(i * hb - 1, 0), 0)),
                  pl.BlockSpec((1, W), lambda i: (0, 0))],
        out_specs=pl.BlockSpec((tb, W), lambda i: (i, 0)),
        out_shape=jax.ShapeDtypeStruct((S, W), F32),
        compiler_params=_cparams(("parallel",), blk),
    )(p, p, mu)


def token_shift_bwd(dxs, p, mu, *, tb, name):
    S, W = p.shape
    hb, nb = tb // 8, S // tb

    def body(d_ref, dnext_ref, p_ref, halo_ref, mu_ref, dp_ref, dmu_ref):
        i = pl.program_id(0)
        d, x, mu_v = d_ref[...], p_ref[...], mu_ref[...]
        row = lax.broadcasted_iota(jnp.int32, (tb, W), 0)
        before = halo_ref[7:8, :] * (i > 0).astype(F32)
        prev = jnp.where(row == 0, before, pltpu.roll(x, 1, 0))
        t = d * mu_v
        after = dnext_ref[0:1, :] * mu_v * (i < nb - 1).astype(F32)
        nxt = jnp.where(row == tb - 1, after, pltpu.roll(t, tb - 1, 0))
        dp_ref[...] = (d - t + nxt).astype(dp_ref.dtype)

        @pl.when(i == 0)
        def _():
            dmu_ref[...] = jnp.zeros_like(dmu_ref)

        dmu_ref[...] += jnp.sum(d * (prev - x), axis=0, keepdims=True)

    blk = (3 * tb + 16) * W * 4
    return pl.pallas_call(
        body, name=name, grid=(nb,),
        in_specs=[pl.BlockSpec((tb, W), lambda i: (i, 0)),
                  pl.BlockSpec((8, W), lambda i: (jnp.minimum((i + 1) * hb, S // 8 - 1), 0)),
                  pl.BlockSpec((tb, W), lambda i: (i, 0)),
                  pl.BlockSpec((8, W), lambda i: (jnp.maximum(i * hb - 1, 0), 0)),
                  pl.BlockSpec((1, W), lambda i: (0, 0))],
        out_specs=[pl.BlockSpec((tb, W), lambda i: (i, 0)), pl.BlockSpec((1, W), lambda i: (0, 0))],
        out_shape=[jax.ShapeDtypeStruct((S, W), BF16), jax.ShapeDtypeStruct((1, W), F32)],
        compiler_params=_cparams(("arbitrary",), blk),
    )(dxs, dxs, p, p, mu)


def _head_cols(h):
    return pl.ds(h * HEAD_DIM, HEAD_DIM)


def wkv_fwd(xs_rk, lw, k, a, b):
    S = lw.shape[0]
    C, nc, G, N = WKV_CHUNK, S // WKV_CHUNK, WKV_HEADS_PER_STEP, HEAD_DIM

    def body(r_ref, lw_ref, k_ref, v_ref, a_ref, b_ref, y_ref, st_ref, state):
        @pl.when(pl.program_id(1) == 0)
        def _():
            state[...] = jnp.zeros_like(state)

        heads = lambda ref: tuple(ref[:, _head_cols(h)] for h in range(G))
        s0 = tuple(state[h] for h in range(G))
        y, s1 = _wkv_chunk(s0, heads(r_ref), heads(lw_ref), heads(k_ref), heads(v_ref), heads(a_ref),
                           heads(b_ref))
        for h in range(G):
            st_ref[h] = s0[h]
            y_ref[:, _head_cols(h)] = y[h]
            state[h] = s1[h]

    W = G * N
    seq = lambda j: pl.BlockSpec((C, W), functools.partial(lambda j, g, c: (c, j + g), j))
    per = D_MODEL // W
    return pl.pallas_call(
        body, name="wkv_fwd", grid=(RWKV_HEADS // G, nc),
        in_specs=[seq(0), seq(0), seq(0), seq(2 * per), seq(0), seq(0)],
        out_specs=[seq(0), pl.BlockSpec((None, G, N, N), lambda g, c: (c, g, 0, 0))],
        out_shape=[jax.ShapeDtypeStruct((S, D_MODEL), F32), jax.ShapeDtypeStruct((nc, RWKV_HEADS, N, N), F32)],
        scratch_shapes=[pltpu.VMEM((G, N, N), F32)],
        compiler_params=_cparams(("parallel", "arbitrary"), 8 * C * W * 4 + 2 * G * N * N * 4),
    )(xs_rk, lw, k, xs_rk, a, b)


def wkv_bwd(xs_rk, lw, k, a, b, states, dy):
    S = lw.shape[0]
    C, nc, G, N = WKV_CHUNK, S // WKV_CHUNK, WKV_HEADS_PER_STEP, HEAD_DIM

    def body(r_ref, lw_ref, k_ref, v_ref, a_ref, b_ref, st_ref, dy_ref,
             dr_ref, dlw_ref, dk_ref, dv_ref, da_ref, db_ref, dstate):
        @pl.when(pl.program_id(1) == 0)
        def _():
            dstate[...] = jnp.zeros_like(dstate)

        heads = lambda ref: tuple(ref[:, _head_cols(h)] for h in range(G))
        _, pull = jax.vjp(_wkv_chunk, tuple(st_ref[h] for h in range(G)), heads(r_ref), heads(lw_ref),
                          heads(k_ref), heads(v_ref), heads(a_ref), heads(b_ref))
        ds0, *grads = pull((heads(dy_ref), tuple(dstate[h] for h in range(G))))
        for h in range(G):
            dstate[h] = ds0[h]
            for ref, grad in zip((dr_ref, dlw_ref, dk_ref, dv_ref, da_ref, db_ref), grads):
                ref[:, _head_cols(h)] = grad[h]

    W = G * N
    seq = lambda j: pl.BlockSpec((C, W), functools.partial(lambda j, g, c: (nc - 1 - c, j + g), j))
    per = D_MODEL // W
    st = pl.BlockSpec((None, G, N, N), lambda g, c: (nc - 1 - c, g, 0, 0))
    return pl.pallas_call(
        body, name="wkv_bwd", grid=(RWKV_HEADS // G, nc),
        in_specs=[seq(0), seq(0), seq(0), seq(2 * per), seq(0), seq(0), st, seq(0)],
        out_specs=[seq(0)] * 6, out_shape=[jax.ShapeDtypeStruct((S, D_MODEL), F32)] * 6,
        scratch_shapes=[pltpu.VMEM((G, N, N), F32)],
        compiler_params=_cparams(("parallel", "arbitrary"), 14 * C * W * 4 + 2 * G * N * N * 4),
    )(xs_rk, lw, k, xs_rk, a, b, states, dy)


def _first_flag(i, seq_len):
    per_group = ATTN_HPG * seq_len // ATTN_BLK
    g = i // per_group
    per_seq = [seq_len // d // ATTN_BLK for _, d in ATTN_PAIRS]
    n = jnp.where(g == 0, per_seq[0], jnp.where(g == 1, per_seq[1], per_seq[2]))
    return (lax.rem(i, n) == 0).astype(F32)


def attn_fwd(q, k, v, q_gain, k_gain, seq_len):
    R, N = q.shape
    nb = R // ATTN_BLK

    def body(q_ref, kc_ref, kp_ref, vc_ref, vp_ref, qg_ref, kg_ref, o_ref, lse_ref):
        first = _first_flag(pl.program_id(0), seq_len)
        o, lse = _attn_block(q_ref[...], kc_ref[...], kp_ref[...], vc_ref[...], vp_ref[...],
                             qg_ref[...], kg_ref[...], first)
        o_ref[...] = o
        lse_ref[...] = lse

    cur = pl.BlockSpec((ATTN_BLK, N), lambda i: (i, 0))
    prv = pl.BlockSpec((ATTN_BLK, N), lambda i: (jnp.maximum(i - 1, 0), 0))
    gain = pl.BlockSpec((1, N), lambda i: (0, 0))
    return pl.pallas_call(
        body, name="attn_fwd", grid=(nb,), in_specs=[cur, cur, prv, cur, prv, gain, gain],
        out_specs=[cur, cur], out_shape=[jax.ShapeDtypeStruct((R, N), F32)] * 2,
        compiler_params=_cparams(("parallel",), 8 * ATTN_BLK * 128 * 4),
    )(q, k, k, v, v, q_gain, k_gain)


def attn_bwd(q, k, v, q_gain, k_gain, do, dlse, seq_len):
    R, N = q.shape
    nb = R // ATTN_BLK

    def body(q_ref, kc_ref, kp_ref, vc_ref, vp_ref, qg_ref, kg_ref, do_ref, dl_ref,
             dq_ref, dk_ref, dv_ref, dqg_ref, dkg_ref, carry_k, carry_v):
        step = pl.program_id(0)
        first = _first_flag(nb - 1 - step, seq_len)

        @pl.when(step == 0)
        def _():
            carry_k[...] = jnp.zeros_like(carry_k)
            carry_v[...] = jnp.zeros_like(carry_v)
            dqg_ref[...] = jnp.zeros_like(dqg_ref)
            dkg_ref[...] = jnp.zeros_like(dkg_ref)

        _, pull = jax.vjp(functools.partial(_attn_block, first=first), q_ref[...], kc_ref[...], kp_ref[...],
                          vc_ref[...], vp_ref[...], qg_ref[...], kg_ref[...])
        dq, dkc, dkp, dvc, dvp, dqg, dkg = pull((do_ref[...], dl_ref[...]))
        dq_ref[...] = dq
        dk_ref[...] = dkc + carry_k[...]
        dv_ref[...] = dvc + carry_v[...]
        carry_k[...] = dkp
        carry_v[...] = dvp
        dqg_ref[...] += dqg
        dkg_ref[...] += dkg

    cur = pl.BlockSpec((ATTN_BLK, N), lambda i: (nb - 1 - i, 0))
    prv = pl.BlockSpec((ATTN_BLK, N), lambda i: (jnp.maximum(nb - 2 - i, 0), 0))
    gain = pl.BlockSpec((1, N), lambda i: (0, 0))
    return pl.pallas_call(
        body, name="attn_bwd", grid=(nb,), in_specs=[cur, cur, prv, cur, prv, gain, gain, cur, cur],
        out_specs=[cur, cur, cur, gain, gain],
        out_shape=[jax.ShapeDtypeStruct((R, N), F32)] * 3 + [jax.ShapeDtypeStruct((1, N), F32)] * 2,
        scratch_shapes=[pltpu.VMEM((ATTN_BLK, N), F32)] * 2,
        compiler_params=_cparams(("arbitrary",), 16 * ATTN_BLK * 128 * 4),
    )(q, k, k, v, v, q_gain, k_gain, do, dlse)


def to_subsequences(t):
    S = t.shape[0]
    parts = []
    for gi, (_, d) in enumerate(ATTN_PAIRS):
        tg = t[:, GROUP_W * gi:GROUP_W * (gi + 1)].reshape(S // d, d, ATTN_HPG, HEAD_DIM)
        parts.append(tg.transpose(1, 2, 0, 3).reshape(ATTN_HPG * S, HEAD_DIM))
    return jnp.concatenate(parts, axis=0)


def from_subsequences(u, S):
    parts = []
    for gi, (_, d) in enumerate(ATTN_PAIRS):
        ug = u[ATTN_HPG * S * gi:ATTN_HPG * S * (gi + 1)].reshape(d, ATTN_HPG, S // d, HEAD_DIM)
        parts.append(ug.transpose(2, 0, 1, 3).reshape(S, GROUP_W))
    return jnp.concatenate(parts, axis=1)


def _ffn_fwd(x, norm, w_in, w_out, tag):
    h = rowmap(_rms, [x], [norm], [(D_MODEL, BF16)], tb=512, name=tag + "_norm")[0]
    gu = matmul(h, w_in, "nn", tag + "_in")
    act = rowmap(_swiglu_act, [gu], [], [(D_FF, BF16)], tb=256, name=tag + "_act")[0]
    y = matmul(act, w_out, "nn", tag + "_out", add=x, scale=0.5)
    return y, (x, h, gu, act)


def _ffn_bwd(dy, saved, norm, w_in, w_out, tag):
    x, h, gu, act = saved
    dact = matmul(dy, w_out, "nt", tag + "_dact", scale=0.5)
    dw_out = matmul(act, dy, "tn", tag + "_dwout", scale=0.5)

    def act_bwd(gu_b, dact_b):
        return jax.vjp(_swiglu_act, gu_b)[1](dact_b)[0]

    dgu = rowmap(act_bwd, [gu, dact], [], [(2 * D_FF, BF16)], tb=256, name=tag + "_dgu")[0]
    dh = matmul(dgu, w_in, "nt", tag + "_dh")
    dw_in = matmul(h, dgu, "tn", tag + "_dwin")

    def norm_bwd(x_b, dh_b, dy_b, g):
        dx, dg = jax.vjp(_rms, x_b, g)[1](dh_b)
        return dy_b + dx, dg

    dx, dnorm = rowmap(norm_bwd, [x, dh, dy], [norm], [(D_MODEL, F32)], [(1, D_MODEL)], tb=256,
                       name=tag + "_dnorm")
    return dx, dnorm, dw_in, dw_out


def layer_step(x, tgt, W, P):
    S = x.shape[0]
    seg = (jnp.arange(D_MODEL)[:, None] // HEAD_DIM == jnp.arange(RWKV_HEADS)[None, :]).astype(F32)
    seg_t = seg.T
    w_rkv, w_lora = W["w_in"][:, :RKV], W["w_in"][:, RKV:RKV + LORA]
    w_qkv = W["w_in"][:, RKV + LORA:RKV + LORA + 3 * ATTN_WIDTH]
    w_gate = W["w_in"][:, RKV + LORA + 3 * ATTN_WIDTH:]
    mu_rk, mu_lo = P["rwkv_mu"][:, :RKV], P["rwkv_mu"][:, RKV:]
    zeros = lambda n: jnp.zeros((n, D_MODEL), F32)
    w2p = jnp.concatenate([W["rwkv_w2"], zeros(LORA - LORA_W)], axis=0)
    a2p = jnp.concatenate([zeros(LORA_W), W["rwkv_a2"], zeros(LORA_G)], axis=0)
    g2p = jnp.concatenate([zeros(LORA_W + LORA_A), W["rwkv_g2"]], axis=0)
    pre_params = [P["rwkv_w0"], w2p, P["rwkv_a0"], a2p, g2p, P["rwkv_k_k"], P["rwkv_k_a"], seg, seg_t]
    post_params = [P["rwkv_r_k"], P["rwkv_ln_w"], P["rwkv_ln_b"], seg, seg_t]
    col = lambda arr, j: (arr, D_MODEL, j)

    x1, ffn1_saved = _ffn_fwd(x, P["ffn1_norm"], W["ffn1_w_in"], W["ffn1_w_out"], "ffn1")
    h = rowmap(_rms, [x1], [P["mix_norm"]], [(D_MODEL, BF16)], tb=512, name="mix_norm")[0]
    p_rk = matmul(h, w_rkv, "nn", "proj_rkv")
    p_lo = matmul(h, w_lora, "nn", "proj_lora")
    p_qkv = matmul(h, w_qkv, "nn", "proj_qkv")
    p_gate = matmul(h, w_gate, "nn", "proj_gate")
    xs_rk = token_shift_fwd(p_rk, mu_rk, tb=256, name="shift_rk")
    xs_lo = token_shift_fwd(p_lo, mu_lo, tb=256, name="shift_lora")
    lw, k_mod, a_neg, b_kk, g = rowmap(
        _rwkv_pre, [xs_rk, xs_lo], pre_params, [(D_MODEL, F32)] * 5, tb=256, name="rwkv_pre")
    wkv, states = wkv_fwd(xs_rk, lw, k_mod, a_neg, b_kk)
    post_rows = [wkv, col(xs_rk, 0), k_mod, col(xs_rk, 2), g]
    y_a = rowmap(_rwkv_post, post_rows, post_params, [(D_MODEL, BF16)], tb=256, name="rwkv_post")[0]

    q_s = to_subsequences(p_qkv[:, :ATTN_WIDTH])
    k_s = to_subsequences(p_qkv[:, ATTN_WIDTH:2 * ATTN_WIDTH])
    v_s = to_subsequences(p_qkv[:, 2 * ATTN_WIDTH:])
    o_s, lse_s = attn_fwd(q_s, k_s, v_s, P["attn_q_norm"], P["attn_k_norm"], S)
    o, lse = from_subsequences(o_s, S), from_subsequences(lse_s, S)
    y_b = rowmap(_group_combine, [o, lse], [], [(ATTN_WIDTH, BF16)], tb=512, name="attn_combine")[0]

    pa = matmul(y_a, W["w_proj_rwkv"], "nn", "proj_a")
    pb = matmul(y_b, W["w_proj_attn"], "nn", "proj_b")
    merged = rowmap(_gate_merge, [p_gate, pa, pb], [P["b_gate"]], [(D_MODEL, BF16)], tb=256, name="merge")[0]
    x2 = matmul(merged, W["w_out"], "nn", "mix_out", add=x1)
    x3, ffn2_saved = _ffn_fwd(x2, P["ffn2_norm"], W["ffn2_w_in"], W["ffn2_w_out"], "ffn2")

    def loss_head(y_b_, t_b):
        err = y_b_ - t_b
        return err * (1.0 / D_MODEL), (0.5 / D_MODEL) * jnp.sum(err * err, axis=0, keepdims=True)

    dx3, loss_cols = rowmap(loss_head, [x3, tgt], [], [(D_MODEL, F32)], [(1, D_MODEL)], tb=512, name="loss")

    gW, gP = {}, {}
    dx2, gP["ffn2_norm"], gW["ffn2_w_in"], gW["ffn2_w_out"] = _ffn_bwd(
        dx3, ffn2_saved, P["ffn2_norm"], W["ffn2_w_in"], W["ffn2_w_out"], "ffn2")

    dmerged = matmul(dx2, W["w_out"], "nt", "d_merged")
    gW["w_out"] = matmul(merged, dx2, "tn", "dw_out")

    def merge_bwd(pg, pa_b, pb_b, dm, bg):
        return jax.vjp(_gate_merge, pg, pa_b, pb_b, bg)[1](dm)

    dp_gate, dpa, dpb, gP["b_gate"] = rowmap(
        merge_bwd, [p_gate, pa, pb, dmerged], [P["b_gate"]],
        [(2 * D_MODEL, BF16), (D_MODEL, BF16), (D_MODEL, BF16)], [(1, 2 * D_MODEL)], tb=256, name="merge_bwd")
    dy_a = matmul(dpa, W["w_proj_rwkv"], "nt", "d_ya")
    gW["w_proj_rwkv"] = matmul(y_a, dpa, "tn", "dw_proj_a")
    dy_b = matmul(dpb, W["w_proj_attn"], "nt", "d_yb")
    gW["w_proj_attn"] = matmul(y_b, dpb, "tn", "dw_proj_b")

    def combine_bwd(o_b, l_b, d_b):
        return jax.vjp(_group_combine, o_b, l_b)[1](d_b)

    do, dlse = rowmap(combine_bwd, [o, lse, dy_b], [], [(ATTN_WIDTH, F32)] * 2, tb=256, name="attn_combine_bwd")
    dq_s, dk_s, dv_s, gP["attn_q_norm"], gP["attn_k_norm"] = attn_bwd(
        q_s, k_s, v_s, P["attn_q_norm"], P["attn_k_norm"], to_subsequences(do), to_subsequences(dlse), S)
    dp_qkv = jnp.concatenate([from_subsequences(t, S) for t in (dq_s, dk_s, dv_s)], axis=1).astype(BF16)

    def post_bwd(wkv_b, r_b, k_b, v_b, g_b, d_b, r_k, ln_w, ln_b, sg, sgt):
        f = lambda *a: _rwkv_post(*a, sg, sgt)
        return jax.vjp(f, wkv_b, r_b, k_b, v_b, g_b, r_k, ln_w, ln_b)[1](d_b)

    dwkv, dr_p, dk_p, dv_p, dg, gP["rwkv_r_k"], gP["rwkv_ln_w"], gP["rwkv_ln_b"] = rowmap(
        post_bwd, post_rows + [dy_a], post_params, [(D_MODEL, F32)] * 5, [(1, D_MODEL)] * 3, tb=128,
        name="rwkv_post_bwd")
    dr_w, dlw, dk_w, dv_w, da_neg, db_kk = wkv_bwd(xs_rk, lw, k_mod, a_neg, b_kk, states, dwkv)

    def pre_bwd(xrk_b, xlo_b, dlw_b, dkw_b, dkp_b, da_b, db_b, dg_b, drp_b, drw_b, dvp_b, dvw_b,
                w0, w2, a0, a2, g2, k_k, k_a, sg, sgt):
        f = lambda *a: _rwkv_pre(*a, sg, sgt)
        pull = jax.vjp(f, xrk_b, xlo_b, w0, w2, a0, a2, g2, k_k, k_a)[1]
        dxrk, dxlo, *dpar = pull((dlw_b, dkw_b + dkp_b, da_b, db_b, dg_b))
        direct = jnp.concatenate([drp_b + drw_b, jnp.zeros_like(drp_b), dvp_b + dvw_b], axis=1)
        return (dxrk + direct, dxlo, *dpar)

    pre_rows = [xs_rk, xs_lo, dlw, dk_w, dk_p, da_neg, db_kk, dg, dr_p, dr_w, dv_p, dv_w]
    dxs_rk, dxs_lo, gP["rwkv_w0"], dw2p, gP["rwkv_a0"], da2p, dg2p, gP["rwkv_k_k"], gP["rwkv_k_a"] = rowmap(
        pre_bwd, pre_rows, pre_params, [(RKV, F32), (LORA, F32)],
        [(1, D_MODEL), (LORA, D_MODEL), (1, D_MODEL), (LORA, D_MODEL), (LORA, D_MODEL), (1, D_MODEL), (1, D_MODEL)],
        tb=128, name="rwkv_pre_bwd")
    gW["rwkv_w2"] = dw2p[:LORA_W]
    gW["rwkv_a2"] = da2p[LORA_W:LORA_W + LORA_A]
    gW["rwkv_g2"] = dg2p[LORA_W + LORA_A:]
    dp_rk, dmu_rk = token_shift_bwd(dxs_rk, p_rk, mu_rk, tb=256, name="shift_rk_bwd")
    dp_lo, dmu_lo = token_shift_bwd(dxs_lo, p_lo, mu_lo, tb=256, name="shift_lora_bwd")
    gP["rwkv_mu"] = jnp.concatenate([dmu_rk, dmu_lo], axis=1)

    dh = matmul(dp_rk, w_rkv, "nt", "dh_rkv")
    dh = matmul(dp_lo, w_lora, "nt", "dh_lora", add=dh)
    dh = matmul(dp_qkv, w_qkv, "nt", "dh_qkv", add=dh)
    dh = matmul(dp_gate, w_gate, "nt", "dh_gate", add=dh)
    gW["w_in"] = jnp.concatenate([
        matmul(h, dp_rk, "tn", "dw_rkv"), matmul(h, dp_lo, "tn", "dw_lora"),
        matmul(h, dp_qkv, "tn", "dw_qkv"), matmul(h, dp_gate, "tn", "dw_gate")], axis=1)

    def norm_bwd(x_b, dh_b, dy_b, gn):
        dx, dgn = jax.vjp(_rms, x_b, gn)[1](dh_b)
        return dy_b + dx, dgn

    dx1, gP["mix_norm"] = rowmap(norm_bwd, [x1, dh, dx2], [P["mix_norm"]], [(D_MODEL, F32)], [(1, D_MODEL)],
                                 tb=256, name="mix_norm_bwd")
    dx, gP["ffn1_norm"], gW["ffn1_w_in"], gW["ffn1_w_out"] = _ffn_bwd(
        dx1, ffn1_saved, P["ffn1_norm"], W["ffn1_w_in"], W["ffn1_w_out"], "ffn1")
    return loss_cols, dx, gW, gP


N_SHARDS = 4
BIG = (("ffn1_w_in", (D_MODEL, 2 * D_FF), 1), ("ffn1_w_out", (D_FF, D_MODEL), 0),
       ("w_in", (D_MODEL, 7712), 1), ("rwkv_w2", (LORA_W, D_MODEL), 1), ("rwkv_a2", (LORA_A, D_MODEL), 1),
       ("rwkv_g2", (LORA_G, D_MODEL), 1), ("w_proj_rwkv", (D_MODEL, D_MODEL), 0),
       ("w_proj_attn", (ATTN_WIDTH, D_MODEL), 1), ("w_out", (D_MODEL, D_MODEL), 0),
       ("ffn2_w_in", (D_MODEL, 2 * D_FF), 1), ("ffn2_w_out", (D_FF, D_MODEL), 0))
SMALL = (("ffn1_norm", 1024), ("mix_norm", 1024), ("b_gate", 2048), ("rwkv_mu", 3360), ("rwkv_w0", 1024),
         ("rwkv_a0", 1024), ("rwkv_k_k", 1024), ("rwkv_k_a", 1024), ("rwkv_r_k", 1024), ("rwkv_ln_w", 1024),
         ("rwkv_ln_b", 1024), ("attn_q_norm", 64), ("attn_k_norm", 64), ("ffn2_norm", 1024))
WEIGHT_ORDER = ("ffn1_norm", "ffn1_w_in", "ffn1_w_out", "mix_norm", "w_in", "b_gate", "rwkv_mu", "rwkv_w0",
                "rwkv_w2", "rwkv_a0", "rwkv_a2", "rwkv_g2", "rwkv_k_k", "rwkv_k_a", "rwkv_r_k", "rwkv_ln_w",
                "rwkv_ln_b", "attn_q_norm", "attn_k_norm", "w_proj_rwkv", "w_proj_attn", "w_out", "ffn2_norm",
                "ffn2_w_in", "ffn2_w_out")


def _shard_shape(shape, axis):
    return tuple(s // N_SHARDS if i == axis else s for i, s in enumerate(shape))


def _pack_rows():
    return [(_shard_shape(shape, axis)[0] * _shard_shape(shape, axis)[1]) // PACK_W for _, shape, axis in BIG]


PACK_USED = sum(_pack_rows())
PACK_ROWS = -(-PACK_USED // PACK_ROW_BLOCK) * PACK_ROW_BLOCK
SMALL_USED = D_MODEL + sum(n for _, n in SMALL)
SMALL_W = -(-SMALL_USED // 128) * 128


def pack_shards(shards, dtype):
    parts = [shards[name].astype(dtype).reshape(-1, PACK_W) for name, _, _ in BIG]
    parts.append(jnp.zeros((PACK_ROWS - PACK_USED, PACK_W), dtype))
    return jnp.concatenate(parts, axis=0)


def unpack_shards(packed):
    out, row = {}, 0
    for (name, shape, axis), n in zip(BIG, _pack_rows()):
        out[name] = packed[row:row + n].reshape(_shard_shape(shape, axis))
        row += n
    return out


def unpack_full(gathered):
    out, row = {}, 0
    for (name, shape, axis), n in zip(BIG, _pack_rows()):
        blocks = gathered[:, row:row + n].reshape((N_SHARDS,) + _shard_shape(shape, axis))
        if axis == 0:
            out[name] = blocks.reshape(shape)
        else:
            out[name] = blocks.transpose(1, 0, 2).reshape(shape)
        row += n
    return out


def pack_full(full):
    parts = []
    for (name, shape, axis), n in zip(BIG, _pack_rows()):
        t = full[name]
        if axis == 0:
            blocks = t.reshape((N_SHARDS,) + _shard_shape(shape, axis))
        else:
            blocks = t.reshape(shape[0], N_SHARDS, shape[1] // N_SHARDS).transpose(1, 0, 2)
        parts.append(blocks.reshape(N_SHARDS, n, PACK_W))
    parts.append(jnp.zeros((N_SHARDS, PACK_ROWS - PACK_USED, PACK_W), F32))
    return jnp.concatenate(parts, axis=1)


def pack_small(vals, head):
    parts = [head] + [vals[name].reshape(1, n) for name, n in SMALL]
    parts.append(jnp.zeros((1, SMALL_W - SMALL_USED), F32))
    return jnp.concatenate(parts, axis=1)


def unpack_small(vec, shapes):
    out, off = {}, D_MODEL
    for name, n in SMALL:
        out[name] = vec[:, off:off + n].reshape(shapes[name])
        off += n
    return out


def _place():
    return lax.axis_index("x"), lax.axis_index("y"), lax.axis_index("c")


def _other_chips(x, y):
    return [(1 - x, y), (x, 1 - y), (1 - x, 1 - y)]


def gather_weights(packed):
    R, Wd = packed.shape

    def body(w_ref, out_ref, send_sems, recv_sems, local_sem):
        x, y, c = _place()
        mine = pltpu.make_async_copy(w_ref, out_ref.at[2 * x + y], local_sem)
        mine.start()
        sends = [pltpu.make_async_remote_copy(
            src_ref=w_ref, dst_ref=out_ref.at[2 * x + y], send_sem=send_sems.at[j], recv_sem=recv_sems.at[j],
            device_id=(px, py, c), device_id_type=MESH) for j, (px, py) in enumerate(_other_chips(x, y))]
        for cp in sends:
            cp.start()
        for j, (px, py) in enumerate(_other_chips(x, y)):
            pltpu.make_async_remote_copy(
                src_ref=w_ref, dst_ref=out_ref.at[2 * px + py], send_sem=send_sems.at[j],
                recv_sem=recv_sems.at[j], device_id=(px, py, c), device_id_type=MESH).wait_recv()
        for cp in sends:
            cp.wait_send()
        mine.wait()

    return pl.pallas_call(
        body, name="gather_weights",
        in_specs=[pl.BlockSpec(memory_space=pl.ANY)], out_specs=pl.BlockSpec(memory_space=pl.ANY),
        out_shape=jax.ShapeDtypeStruct((N_SHARDS, R, Wd), packed.dtype),
        scratch_shapes=[pltpu.SemaphoreType.DMA((3,)), pltpu.SemaphoreType.DMA((3,)), pltpu.SemaphoreType.DMA],
        compiler_params=pltpu.CompilerParams(has_side_effects=True),
    )(packed)


def scatter_grads(grads):
    _, R, Wd = grads.shape

    def body(g_ref, land_ref, send_sems, recv_sems, local_sem):
        x, y, c = _place()
        mine = pltpu.make_async_copy(g_ref.at[2 * x + y], land_ref.at[3], local_sem)
        mine.start()
        sends = [pltpu.make_async_remote_copy(
            src_ref=g_ref.at[2 * px + py], dst_ref=land_ref.at[j], send_sem=send_sems.at[j],
            recv_sem=recv_sems.at[j], device_id=(px, py, c), device_id_type=MESH)
            for j, (px, py) in enumerate(_other_chips(x, y))]
        for cp in sends:
            cp.start()
        for cp in sends:
            cp.wait_recv()
        for cp in sends:
            cp.wait_send()
        mine.wait()

    return pl.pallas_call(
        body, name="scatter_grads",
        in_specs=[pl.BlockSpec(memory_space=pl.ANY)], out_specs=pl.BlockSpec(memory_space=pl.ANY),
        out_shape=jax.ShapeDtypeStruct((4, R, Wd), grads.dtype),
        scratch_shapes=[pltpu.SemaphoreType.DMA((3,)), pltpu.SemaphoreType.DMA((3,)), pltpu.SemaphoreType.DMA],
        compiler_params=pltpu.CompilerParams(has_side_effects=True),
    )(grads)


def sum_landed(landed):
    _, R, Wd = landed.shape
    tb = PACK_ROW_BLOCK

    def body(a_ref, b_ref, c_ref, d_ref, o_ref):
        o_ref[...] = ((a_ref[...] + b_ref[...]) + c_ref[...]) + d_ref[...]

    specs = [pl.BlockSpec((None, tb, Wd), functools.partial(lambda k, i: (k, i, 0), k)) for k in range(4)]
    return pl.pallas_call(
        body, name="sum_landed", grid=(R // tb,), in_specs=specs,
        out_specs=pl.BlockSpec((tb, Wd), lambda i: (i, 0)), out_shape=jax.ShapeDtypeStruct((R, Wd), F32),
        compiler_params=_cparams(("parallel",), 5 * tb * Wd * 4),
    )(landed, landed, landed, landed)


def swap_with_sibling(part):
    def body(p_ref, got_ref, send_sem, recv_sem):
        x, y, c = _place()
        cp = pltpu.make_async_remote_copy(src_ref=p_ref, dst_ref=got_ref, send_sem=send_sem, recv_sem=recv_sem,
                                          device_id=(x, y, 1 - c), device_id_type=MESH)
        cp.start()
        cp.wait()

    return pl.pallas_call(
        body, name="swap_with_sibling",
        in_specs=[pl.BlockSpec(memory_space=pl.ANY)], out_specs=pl.BlockSpec(memory_space=pl.ANY),
        out_shape=jax.ShapeDtypeStruct(part.shape, part.dtype),
        scratch_shapes=[pltpu.SemaphoreType.DMA, pltpu.SemaphoreType.DMA],
        compiler_params=pltpu.CompilerParams(has_side_effects=True),
    )(part)


def adamw_packed(w, g_own, g_sib, m, v):
    def f(w_b, ga, gb, m_b, v_b):
        g = ga + gb
        delta, m2, v2 = _adamw(w_b, g, m_b, v_b)
        return g, delta, m2, v2

    return rowmap(f, [w, g_own, g_sib, m, v], [], [(PACK_W, F32)] * 4, tb=PACK_ROW_BLOCK, name="adamw_packed")


def reduce_small(vec, w, m, v):
    n_dev = 8

    def body(vec_ref, w_ref, m_ref, v_ref, loss_ref, g_ref, d_ref, m2_ref, v2_ref, slots, send_sems, recv_sems):
        x, y, c = _place()
        me = 4 * x + 2 * y + c
        slots[me] = vec_ref[...]
        flips = [(fx, fy, fc) for fx in (0, 1) for fy in (0, 1) for fc in (0, 1)][1:]
        peers = [(1 - x if fx else x, 1 - y if fy else y, 1 - c if fc else c) for fx, fy, fc in flips]
        sends = [pltpu.make_async_remote_copy(
            src_ref=vec_ref, dst_ref=slots.at[me], send_sem=send_sems.at[j], recv_sem=recv_sems.at[j],
            device_id=peer, device_id_type=MESH) for j, peer in enumerate(peers)]
        for cp in sends:
            cp.start()
        for j, (px, py, pc) in enumerate(peers):
            pltpu.make_async_remote_copy(
                src_ref=vec_ref, dst_ref=slots.at[4 * px + 2 * py + pc], send_sem=send_sems.at[j],
                recv_sem=recv_sems.at[j], device_id=(px, py, pc), device_id_type=MESH).wait_recv()
        for cp in sends:
            cp.wait_send()
        g = slots[0]
        for d in range(1, n_dev):
            g = g + slots[d]
        loss_ref[...] = jnp.sum(g[:, :D_MODEL], axis=1, keepdims=True)
        delta, m2, v2 = _adamw(w_ref[...], g, m_ref[...], v_ref[...])
        g_ref[...], d_ref[...], m2_ref[...], v2_ref[...] = g, delta, m2, v2

    vm = pl.BlockSpec(memory_space=pltpu.VMEM)
    vec_t = jax.ShapeDtypeStruct(vec.shape, F32)
    return pl.pallas_call(
        body, name="reduce_small", in_specs=[vm] * 4, out_specs=[vm] * 5,
        out_shape=[jax.ShapeDtypeStruct((1, 1), F32)] + [vec_t] * 4,
        scratch_shapes=[pltpu.VMEM((n_dev,) + vec.shape, F32), pltpu.SemaphoreType.DMA((n_dev - 1,)),
                        pltpu.SemaphoreType.DMA((n_dev - 1,))],
        compiler_params=pltpu.CompilerParams(has_side_effects=True),
    )(vec, w, m, v)


def kernel(x, ffn1_norm, ffn1_w_in, ffn1_w_out, mix_norm, w_in, b_gate, rwkv_mu, rwkv_w0, rwkv_w2, rwkv_a0, rwkv_a2, rwkv_g2, rwkv_k_k, rwkv_k_a, rwkv_r_k, rwkv_ln_w, rwkv_ln_b, attn_q_norm, attn_k_norm, w_proj_rwkv, w_proj_attn, w_out, ffn2_norm, ffn2_w_in, ffn2_w_out, loss_target, m_ffn1_norm, m_ffn1_w_in, m_ffn1_w_out, m_mix_norm, m_w_in, m_b_gate, m_rwkv_mu, m_rwkv_w0, m_rwkv_w2, m_rwkv_a0, m_rwkv_a2, m_rwkv_g2, m_rwkv_k_k, m_rwkv_k_a, m_rwkv_r_k, m_rwkv_ln_w, m_rwkv_ln_b, m_attn_q_norm, m_attn_k_norm, m_w_proj_rwkv, m_w_proj_attn, m_w_out, m_ffn2_norm, m_ffn2_w_in, m_ffn2_w_out, v_ffn1_norm, v_ffn1_w_in, v_ffn1_w_out, v_mix_norm, v_w_in, v_b_gate, v_rwkv_mu, v_rwkv_w0, v_rwkv_w2, v_rwkv_a0, v_rwkv_a2, v_rwkv_g2, v_rwkv_k_k, v_rwkv_k_a, v_rwkv_r_k, v_rwkv_ln_w, v_rwkv_ln_b, v_attn_q_norm, v_attn_k_norm, v_w_proj_rwkv, v_w_proj_attn, v_w_out, v_ffn2_norm, v_ffn2_w_in, v_ffn2_w_out):
    given = dict(locals())
    weights = {n: given[n] for n in WEIGHT_ORDER}
    mom_m = {n: given["m_" + n] for n in WEIGHT_ORDER}
    mom_v = {n: given["v_" + n] for n in WEIGHT_ORDER}
    big = [name for name, _, _ in BIG]
    shapes = {n: weights[n].shape for n in WEIGHT_ORDER}
    flat2d = lambda t: t.reshape(t.shape[-2], t.shape[-1]) if t.ndim == 3 else t

    w_packed = pack_shards({n: flat2d(weights[n]) for n in big}, F32)
    gathered = gather_weights(w_packed.astype(BF16))
    W = unpack_full(gathered)
    P = {n: weights[n].reshape(1, -1) for n, _ in SMALL}

    loss_cols, dx, gW, gP = layer_step(x[0], loss_target[0], W, P)

    landed = scatter_grads(pack_full(gW))
    part = sum_landed(landed)
    sib = swap_with_sibling(part)
    m_packed = pack_shards({n: flat2d(mom_m[n]) for n in big}, F32)
    v_packed = pack_shards({n: flat2d(mom_v[n]) for n in big}, F32)
    g_p, d_p, m_p, v_p = adamw_packed(w_packed, part, sib, m_packed, v_packed)
    out_g, out_d, out_m, out_v = [
        {n: t.reshape(shapes[n]) for n, t in unpack_shards(p).items()} for p in (g_p, d_p, m_p, v_p)]

    zero_head = jnp.zeros((1, D_MODEL), F32)
    vec = pack_small(gP, loss_cols)
    loss, g_s, d_s, m_s, v_s = reduce_small(
        vec, pack_small({n: weights[n] for n, _ in SMALL}, zero_head),
        pack_small({n: mom_m[n] for n, _ in SMALL}, zero_head),
        pack_small({n: mom_v[n] for n, _ in SMALL}, zero_head))
    for dst, src in ((out_g, g_s), (out_d, d_s), (out_m, m_s), (out_v, v_s)):
        dst.update(unpack_small(src, shapes))

    return (loss[0, 0], dx[None], *[out_g[n] for n in WEIGHT_ORDER], *[out_d[n] for n in WEIGHT_ORDER],
            *[out_m[n] for n in WEIGHT_ORDER], *[out_v[n] for n in WEIGHT_ORDER])
```

```python
import functools

import jax
import jax.numpy as jnp
from jax import lax
from jax.experimental import pallas as pl
from jax.experimental.pallas import tpu as pltpu

F32 = jnp.float32
BF16 = jnp.bfloat16
HI = lax.Precision.HIGHEST
MESH = pl.DeviceIdType.MESH

D_MODEL = 1024
HEAD_DIM = 64
RWKV_HEADS = 16
LORA_W, LORA_A, LORA_G = 64, 64, 160
LORA = LORA_W + LORA_A + LORA_G
RKV = 3 * D_MODEL
ATTN_PAIRS = ((128, 1), (512, 4), (2048, 16))
ATTN_BLK = 128
ATTN_HPG = 4
ATTN_WIDTH = 768
GROUP_W = ATTN_HPG * HEAD_DIM
D_FF = 2816
GN_EPS = 64e-5
RMS_EPS = 1e-6
NEG_INF = -1e30
WKV_CHUNK = 64
WKV_HEADS_PER_STEP = 4

ADAM_LR, ADAM_B1, ADAM_B2, ADAM_EPS, ADAM_WD, ADAM_STEP = 0.001, 0.9, 0.999, 1e-08, 0.01, 10

V7X_VMEM_BYTES = 64 << 20
VMEM_TEMP_ALLOWANCE = 20 << 20


def _cparams(sem, block_bytes):
    limit = min(2 * block_bytes + VMEM_TEMP_ALLOWANCE, V7X_VMEM_BYTES - (6 << 20))
    return pltpu.CompilerParams(dimension_semantics=sem, vmem_limit_bytes=int(limit))


def _nbytes(shape, dtype):
    n = 1
    for s in shape:
        n *= s
    return n * jnp.dtype(dtype).itemsize


def _split_bf16(a):
    hi = a.astype(BF16)
    return hi, (a - hi.astype(F32)).astype(BF16)


def _make_dots(prec):
    def one(a, b, ca, cb):
        return lax.dot_general(a, b, (((ca,), (cb,)), ((), ())), precision=None if prec == "x3" else prec,
                               preferred_element_type=F32)

    def raw(a, b, ca, cb):
        if prec != "x3":
            return one(a, b, ca, cb)
        (ah, al), (bh, bl) = _split_bf16(a), _split_bf16(b)
        return one(ah, bh, ca, cb) + (one(al, bh, ca, cb) + one(ah, bl, ca, cb))

    @jax.custom_vjp
    def nn(a, b):
        return raw(a, b, 1, 0)

    @jax.custom_vjp
    def nt(a, b):
        return raw(a, b, 1, 1)

    @jax.custom_vjp
    def tn(a, b):
        return raw(a, b, 0, 0)

    nn.defvjp(lambda a, b: (raw(a, b, 1, 0), (a, b)),
              lambda res, g: (raw(g, res[1], 1, 1), raw(res[0], g, 0, 0)))
    nt.defvjp(lambda a, b: (raw(a, b, 1, 1), (a, b)),
              lambda res, g: (raw(g, res[1], 1, 0), raw(g, res[0], 0, 0)))
    tn.defvjp(lambda a, b: (raw(a, b, 0, 0), (a, b)),
              lambda res, g: (raw(res[1], g, 1, 1), raw(res[0], g, 1, 0)))
    return nn, nt, tn


NN, NT, TN = _make_dots(None)
NN_HI, NT_HI, TN_HI = _make_dots(HI)
NN_X3, NT_X3, TN_X3 = _make_dots("x3")


def _pick(n, cap):
    best = None
    for t in range(128, min(n, cap) + 1, 128):
        if n % t == 0:
            best = t
    return best or n


def matmul(a, b, mode, name, *, add=None, scale=1.0, out_dtype=F32):
    if mode == "nn":
        (M, K), (K2, N) = a.shape, b.shape
    elif mode == "nt":
        (M, K), (N, K2) = a.shape, b.shape
    else:
        (K, M), (K2, N) = a.shape, b.shape
    assert K == K2, (name, a.shape, b.shape)
    tm, tn, tk = _pick(M, 512), _pick(N, 512), _pick(K, 1024)
    nk = K // tk
    ca, cb = {"nn": (1, 0), "nt": (1, 1), "tn": (0, 0)}[mode]

    def body(*refs):
        if add is None:
            a_ref, b_ref, o_ref, acc_ref = refs
        else:
            a_ref, b_ref, add_ref, o_ref, acc_ref = refs
        k = pl.program_id(2)

        @pl.when(k == 0)
        def _():
            acc_ref[...] = jnp.zeros_like(acc_ref)

        acc_ref[...] += lax.dot_general(a_ref[...].astype(BF16), b_ref[...].astype(BF16),
                                        (((ca,), (cb,)), ((), ())), preferred_element_type=F32)

        @pl.when(k == nk - 1)
        def _():
            r = acc_ref[...] * scale
            if add is not None:
                r = add_ref[...] + r
            o_ref[...] = r.astype(o_ref.dtype)

    a_spec = (pl.BlockSpec((tk, tm), lambda i, j, k: (k, i)) if mode == "tn"
              else pl.BlockSpec((tm, tk), lambda i, j, k: (i, k)))
    b_spec = (pl.BlockSpec((tn, tk), lambda i, j, k: (j, k)) if mode == "nt"
              else pl.BlockSpec((tk, tn), lambda i, j, k: (k, j)))
    in_specs, args = [a_spec, b_spec], [a, b]
    blk = tm * tk * a.dtype.itemsize + tk * tn * b.dtype.itemsize + tm * tn * 8
    if add is not None:
        in_specs.append(pl.BlockSpec((tm, tn), lambda i, j, k: (i, j)))
        args.append(add)
        blk += tm * tn * 4
    return pl.pallas_call(
        body, name=name, grid=(M // tm, N // tn, nk),
        in_specs=in_specs, out_specs=pl.BlockSpec((tm, tn), lambda i, j, k: (i, j)),
        out_shape=jax.ShapeDtypeStruct((M, N), out_dtype),
        scratch_shapes=[pltpu.VMEM((tm, tn), F32)],
        compiler_params=_cparams(("parallel", "parallel", "arbitrary"), blk),
    )(*args)


def matmul_cs(a, w, mode, name, *, scale=1.0, out_dtype=F32):
    n_blk = N_SHARDS
    if mode == "tn":
        (K, R), Cs = a.shape, w.shape[1] // n_blk
        tm, tk = _pick(R, 512), _pick(K, 1024)
        grid = (R // tm, n_blk, K // tk)
        a_spec = pl.BlockSpec((tk, tm), lambda i, j, k: (k, i))
        w_spec = pl.BlockSpec((tk, Cs), lambda i, j, k: (k, j))
        o_spec = pl.BlockSpec((None, tm, Cs), lambda i, j, k: (j, i, 0))
        out_shape, acc_shape, dims = (n_blk, R, Cs), (tm, Cs), (0, 0)
        blk = tk * tm * a.dtype.itemsize + tk * Cs * w.dtype.itemsize + tm * Cs * 8
    elif mode == "nn":
        (M, R), Cs = a.shape, w.shape[2]
        tm, tk = _pick(M, 512), _pick(R, 1024)
        grid = (M // tm, n_blk, R // tk)
        a_spec = pl.BlockSpec((tm, tk), lambda i, j, k: (i, k))
        w_spec = pl.BlockSpec((None, tk, Cs), lambda i, j, k: (j, k, 0))
        o_spec = pl.BlockSpec((tm, Cs), lambda i, j, k: (i, j))
        out_shape, acc_shape, dims = (M, n_blk * Cs), (tm, Cs), (1, 0)
        blk = tm * tk * a.dtype.itemsize + tk * Cs * w.dtype.itemsize + tm * Cs * 8
    else:
        M, (_, R, Cs) = a.shape[0], w.shape
        tm, tn = _pick(M, 512), _pick(R, 512)
        grid = (M // tm, R // tn, n_blk)
        a_spec = pl.BlockSpec((tm, Cs), lambda i, j, k: (i, k))
        w_spec = pl.BlockSpec((None, tn, Cs), lambda i, j, k: (k, j, 0))
        o_spec = pl.BlockSpec((tm, tn), lambda i, j, k: (i, j))
        out_shape, acc_shape, dims = (M, R), (tm, tn), (1, 1)
        blk = tm * Cs * a.dtype.itemsize + tn * Cs * w.dtype.itemsize + tm * tn * 8
    nk = grid[2]

    def body(a_ref, w_ref, o_ref, acc_ref):
        k = pl.program_id(2)

        @pl.when(k == 0)
        def _():
            acc_ref[...] = jnp.zeros_like(acc_ref)

        acc_ref[...] += lax.dot_general(a_ref[...].astype(BF16), w_ref[...].astype(BF16),
                                        (((dims[0],), (dims[1],)), ((), ())), preferred_element_type=F32)

        @pl.when(k == nk - 1)
        def _():
            o_ref[...] = (acc_ref[...] * scale).astype(o_ref.dtype)

    return pl.pallas_call(
        body, name=name, grid=grid, in_specs=[a_spec, w_spec], out_specs=o_spec,
        out_shape=jax.ShapeDtypeStruct(out_shape, out_dtype), scratch_shapes=[pltpu.VMEM(acc_shape, F32)],
        compiler_params=_cparams(("parallel", "parallel", "arbitrary"), blk),
    )(a, w)


def _row_block(n, width, n_arrays):
    cap = (V7X_VMEM_BYTES // 4) // (2 * 4 * width * n_arrays)
    best = None
    for t in range(16, min(n, cap) + 1, 16):
        if n % t == 0:
            best = t
    return best or n


def rowmap(f, rows, params, outs, accs=(), *, tb, name, n_rows=None):
    rows = [r if isinstance(r, tuple) else (r, r.shape[1], 0) for r in rows]
    rows = [r if len(r) == 4 else (*r, 0) for r in rows]
    S = n_rows or rows[0][0].shape[0]
    assert S % tb == 0, (name, S, tb)
    n_in, n_out = len(rows) + len(params), len(outs)

    def body(*refs):
        res = f(*[r[...] for r in refs[:n_in]])
        res = res if isinstance(res, (tuple, list)) else (res,)
        o_refs, a_refs = refs[n_in:n_in + n_out], refs[n_in + n_out:]
        for ref, val in zip(o_refs, res[:n_out]):
            ref[...] = val.astype(ref.dtype)
        if a_refs:
            @pl.when(pl.program_id(0) == 0)
            def _():
                for ref in a_refs:
                    ref[...] = jnp.zeros_like(ref)

            for ref, val in zip(a_refs, res[n_out:]):
                ref[...] += val.astype(F32)

    in_specs = [pl.BlockSpec((tb, w), functools.partial(lambda cb, rb, i: (i + rb, cb), cb, rb))
                for _, w, cb, rb in rows]
    in_specs += [pl.BlockSpec(p.shape, lambda i: (0, 0)) for p in params]
    out_specs = [pl.BlockSpec((tb, w), lambda i: (i, 0)) for w, _ in outs]
    out_specs += [pl.BlockSpec(tuple(s), lambda i: (0, 0)) for s in accs]
    out_shape = [jax.ShapeDtypeStruct((S, w), dt) for w, dt in outs]
    out_shape += [jax.ShapeDtypeStruct(tuple(s), F32) for s in accs]
    blk = sum(tb * w * a.dtype.itemsize for a, w, _, _ in rows) + sum(_nbytes(p.shape, p.dtype) for p in params)
    blk += sum(_nbytes((tb, w), dt) for w, dt in outs) + sum(_nbytes(s, F32) for s in accs)
    res = pl.pallas_call(
        body, name=name, grid=(S // tb,), in_specs=in_specs, out_specs=out_specs, out_shape=out_shape,
        compiler_params=_cparams(("arbitrary",) if accs else ("parallel",), blk),
    )(*[r[0] for r in rows], *params)
    return res


def _rms(x, g):
    return x * lax.rsqrt(jnp.mean(x * x, axis=-1, keepdims=True) + RMS_EPS) * g


def _softplus(z):
    return jnp.maximum(z, 0.0) + jnp.log(1.0 + jnp.exp(-jnp.abs(z)))


def _swiglu_act(gu):
    gate, up = gu[:, :D_FF], gu[:, D_FF:]
    return gate * jax.nn.sigmoid(gate) * up


def _rwkv_pre(xrk, xlo, w0, w2p, a0, a2p, g2p, k_k, k_a, seg, seg_t):
    k = xrk[:, D_MODEL:2 * D_MODEL]
    w = -_softplus(-(w0 + NN(jnp.tanh(xlo), w2p))) - 0.5
    log_decay = -jnp.exp(w)
    a = jax.nn.sigmoid(a0 + NN(xlo, a2p))
    g = NN(jax.nn.sigmoid(xlo), g2p)
    kk = k * k_k
    norm = jnp.maximum(jnp.sqrt(NN_HI(kk * kk, seg)), 1e-12)
    kk = kk / NN_HI(norm, seg_t)
    k_mod = k * (1.0 + (a - 1.0) * k_a)
    return log_decay, k_mod, -kk, kk * a, g


def _rwkv_post(wkv, r, k_mod, v, g, r_k, ln_w, ln_b, seg, seg_t):
    inv_n = 1.0 / HEAD_DIM
    mean = NN_HI(wkv, seg) * inv_n
    cen = wkv - NN_HI(mean, seg_t)
    var = NN_HI(cen * cen, seg) * inv_n
    y = cen * NN_HI(lax.rsqrt(var + GN_EPS), seg_t) * ln_w + ln_b
    bonus = NN_HI(NN_HI(r * k_mod * r_k, seg), seg_t) * v
    return (y + bonus) * g


def _gate_merge(pgate, pa, pb, b_gate):
    sg = jax.nn.sigmoid(pgate + b_gate)
    return sg[:, :D_MODEL] * pa + sg[:, D_MODEL:] * pb


def _group_combine(o, lse):
    ls = [lse[:, GROUP_W * i:GROUP_W * (i + 1)] for i in range(3)]
    m = jnp.maximum(jnp.maximum(ls[0], ls[1]), ls[2])
    es = [jnp.exp(l - m) for l in ls]
    den = es[0] + es[1] + es[2]
    return jnp.concatenate([o[:, GROUP_W * i:GROUP_W * (i + 1)] * (es[i] / den) for i in range(3)], axis=1)


def _attn_block(q, kc, kp, vc, vp, q_gain, k_gain, first):
    qn = _rms(q, q_gain) * (HEAD_DIM ** -0.5)
    kcn, kpn = _rms(kc, k_gain), _rms(kp, k_gain)
    qi = lax.broadcasted_iota(jnp.int32, (ATTN_BLK, ATTN_BLK), 0)
    kj = lax.broadcasted_iota(jnp.int32, (ATTN_BLK, ATTN_BLK), 1)
    s_c = jnp.where(kj <= qi, NT(qn, kcn), NEG_INF)
    s_p = jnp.where((kj >= qi) & (first < 0.5), NT(qn, kpn), NEG_INF)
    m = jnp.maximum(jnp.max(s_c, axis=-1, keepdims=True), jnp.max(s_p, axis=-1, keepdims=True))
    e_c, e_p = jnp.exp(s_c - m), jnp.exp(s_p - m)
    den = jnp.sum(e_c, axis=-1, keepdims=True) + jnp.sum(e_p, axis=-1, keepdims=True)
    o = NN(e_c / den, vc) + NN(e_p / den, vp)
    lse = m + jnp.log(den)
    return o, jnp.broadcast_to(lse, (ATTN_BLK, HEAD_DIM))


def _each(f, *xs):
    return tuple(f(*args) for args in zip(*xs))


def _tri_inverse(n):
    c = n[0].shape[0]
    eye = (lax.broadcasted_iota(jnp.int32, (c, c), 0) == lax.broadcasted_iota(jnp.int32, (c, c), 1)).astype(F32)
    t, p, span = _each(lambda m: eye + m, n), n, 2
    while span < c:
        p = _each(NN_X3, p, p)
        t = _each(lambda t_, p_: t_ + NN_X3(t_, p_), t, p)
        span *= 2
    return t


@jax.custom_vjp
def _tri_solve(n, rhs):
    return _each(NN_X3, _tri_inverse(n), rhs)


def _tri_solve_fwd(n, rhs):
    t = _tri_inverse(n)
    x = _each(NN_X3, t, rhs)
    return x, (t, x)


def _tri_solve_bwd(res, dx):
    t, x = res
    drhs = _each(TN_X3, t, dx)
    return _each(NT_X3, drhs, x), drhs


_tri_solve.defvjp(_tri_solve_fwd, _tri_solve_bwd)


def _lower_ones(c):
    row = lax.broadcasted_iota(jnp.int32, (c, c), 0)
    col = lax.broadcasted_iota(jnp.int32, (c, c), 1)
    return (row >= col).astype(BF16)


def _ones_dot(ones, x, contract):
    hi, lo = _split_bf16(x)
    dims = (((contract,), (0,)), ((), ()))
    return (lax.dot_general(ones, hi, dims, preferred_element_type=F32)
            + lax.dot_general(ones, lo, dims, preferred_element_type=F32))


@jax.custom_vjp
def _cumsum_rows(x):
    return _ones_dot(_lower_ones(x.shape[0]), x, 1)


_cumsum_rows.defvjp(lambda x: (_ones_dot(_lower_ones(x.shape[0]), x, 1), None),
                    lambda _, g: (_ones_dot(_lower_ones(g.shape[0]), g, 0),))


def _wkv_chunk(s0, r, lw, k, v, a, b):
    c = r[0].shape[0]
    row = lax.broadcasted_iota(jnp.int32, (c, c), 0)
    col = lax.broadcasted_iota(jnp.int32, (c, c), 1)
    strict, incl = row > col, row >= col
    cat = lambda p, q: jnp.concatenate([p, q], axis=0)
    cum = _each(_cumsum_rows, lw)
    e_neg = _each(lambda c_: jnp.exp(-c_), cum)
    ar = _each(lambda a_, r_, c_, l_: cat(a_ * jnp.exp(c_ - l_), r_ * jnp.exp(c_)), a, r, cum, lw)
    b_t, k_t = _each(jnp.multiply, b, e_neg), _each(jnp.multiply, k, e_neg)
    p_b, p_k, p_s = _each(NT_X3, ar, b_t), _each(NT_X3, ar, k_t), _each(NT_X3, ar, s0)
    n_ab = _each(lambda p: jnp.where(strict, p[:c], 0.0), p_b)
    m_rb = _each(lambda p: jnp.where(incl, p[c:], 0.0), p_b)
    n_ak = _each(lambda p: jnp.where(strict, p[:c], 0.0), p_k)
    m_rk = _each(lambda p: jnp.where(incl, p[c:], 0.0), p_k)
    u = _tri_solve(n_ab, _each(lambda p, n_, v_: p[:c] + NN_X3(n_, v_), p_s, n_ak, v))
    y = _each(lambda p, mb, u_, mk, v_: p[c:] + NN_X3(mb, u_) + NN_X3(mk, v_), p_s, m_rb, u, m_rk, v)
    g_end = _each(lambda l_: jnp.exp(jnp.sum(l_, axis=0, keepdims=True)), lw)
    s1 = _each(lambda s_, g_, u_, v_, b_, k_: s_ * g_ + TN_X3(cat(u_, v_), cat(b_, k_) * g_),
               s0, g_end, u, v, b_t, k_t)
    return y, s1


def _adamw(w, g, m, v):
    m = ADAM_B1 * m + (1.0 - ADAM_B1) * g
    v = ADAM_B2 * v + (1.0 - ADAM_B2) * jnp.square(g)
    m_hat = m / (1.0 - ADAM_B1 ** ADAM_STEP)
    v_hat = v / (1.0 - ADAM_B2 ** ADAM_STEP)
    delta = -ADAM_LR * (m_hat / (jnp.sqrt(v_hat) + ADAM_EPS) + ADAM_WD * w)
    return delta, m, v


def token_shift_fwd(p, mu, *, tb, name):
    S, W = p.shape
    hb = tb // 8

    def body(p_ref, halo_ref, mu_ref, o_ref):
        i = pl.program_id(0)
        x = p_ref[...]
        before = halo_ref[7:8, :] * (i > 0).astype(F32)
        row = lax.broadcasted_iota(jnp.int32, (tb, W), 0)
        prev = jnp.where(row == 0, before, pltpu.roll(x, 1, 0))
        o_ref[...] = x + (prev - x) * mu_ref[...]

    blk = (2 * tb + 8) * W * 4
    return pl.pallas_call(
        body, name=name, grid=(S // tb,),
        in_specs=[pl.BlockSpec((tb, W), lambda i: (i, 0)),
                  pl.BlockSpec((8, W), lambda i: (jnp.maximum(i * hb - 1, 0), 0)),
                  pl.BlockSpec((1, W), lambda i: (0, 0))],
        out_specs=pl.BlockSpec((tb, W), lambda i: (i, 0)),
        out_shape=jax.ShapeDtypeStruct((S, W), F32),
        compiler_params=_cparams(("parallel",), blk),
    )(p, p, mu)


def token_shift_bwd(dxs, p, mu, *, tb, name):
    S, W = p.shape
    hb, nb = tb // 8, S // tb

    def body(d_ref, dnext_ref, p_ref, halo_ref, mu_ref, dp_ref, dmu_ref):
        i = pl.program_id(0)
        d, x, mu_v = d_ref[...], p_ref[...], mu_ref[...]
        row = lax.broadcasted_iota(jnp.int32, (tb, W), 0)
        before = halo_ref[7:8, :] * (i > 0).astype(F32)
        prev = jnp.where(row == 0, before, pltpu.roll(x, 1, 0))
        t = d * mu_v
        after = dnext_ref[0:1, :] * mu_v * (i < nb - 1).astype(F32)
        nxt = jnp.where(row == tb - 1, after, pltpu.roll(t, tb - 1, 0))
        dp_ref[...] = (d - t + nxt).astype(dp_ref.dtype)

        @pl.when(i == 0)
        def _():
            dmu_ref[...] = jnp.zeros_like(dmu_ref)

        dmu_ref[...] += jnp.sum(d * (prev - x), axis=0, keepdims=True)

    blk = (3 * tb + 16) * W * 4
    return pl.pallas_call(
        body, name=name, grid=(nb,),
        in_specs=[pl.BlockSpec((tb, W), lambda i: (i, 0)),
                  pl.BlockSpec((8, W), lambda i: (jnp.minimum((i + 1) * hb, S // 8 - 1), 0)),
                  pl.BlockSpec((tb, W), lambda i: (i, 0)),
                  pl.BlockSpec((8, W), lambda i: (jnp.maximum(i * hb - 1, 0), 0)),
                  pl.BlockSpec((1, W), lambda i: (0, 0))],
        out_specs=[pl.BlockSpec((tb, W), lambda i: (i, 0)), pl.BlockSpec((1, W), lambda i: (0, 0))],
        out_shape=[jax.ShapeDtypeStruct((S, W), BF16), jax.ShapeDtypeStruct((1, W), F32)],
        compiler_params=_cparams(("arbitrary",), blk),
    )(dxs, dxs, p, p, mu)


def _head_cols(h):
    return pl.ds(h * HEAD_DIM, HEAD_DIM)


def wkv_fwd(xs_rk, lw, k, a, b):
    S = lw.shape[0]
    C, nc, G, N = WKV_CHUNK, S // WKV_CHUNK, WKV_HEADS_PER_STEP, HEAD_DIM

    def body(r_ref, lw_ref, k_ref, v_ref, a_ref, b_ref, y_ref, st_ref, state):
        @pl.when(pl.program_id(1) == 0)
        def _():
            state[...] = jnp.zeros_like(state)

        heads = lambda ref: tuple(ref[:, _head_cols(h)] for h in range(G))
        s0 = tuple(state[h] for h in range(G))
        y, s1 = _wkv_chunk(s0, heads(r_ref), heads(lw_ref), heads(k_ref), heads(v_ref), heads(a_ref),
                           heads(b_ref))
        for h in range(G):
            st_ref[h] = s0[h]
            y_ref[:, _head_cols(h)] = y[h]
            state[h] = s1[h]

    W = G * N
    seq = lambda j: pl.BlockSpec((C, W), functools.partial(lambda j, g, c: (c, j + g), j))
    per = D_MODEL // W
    return pl.pallas_call(
        body, name="wkv_fwd", grid=(RWKV_HEADS // G, nc),
        in_specs=[seq(0), seq(0), seq(0), seq(2 * per), seq(0), seq(0)],
        out_specs=[seq(0), pl.BlockSpec((None, G, N, N), lambda g, c: (c, g, 0, 0))],
        out_shape=[jax.ShapeDtypeStruct((S, D_MODEL), F32), jax.ShapeDtypeStruct((nc, RWKV_HEADS, N, N), F32)],
        scratch_shapes=[pltpu.VMEM((G, N, N), F32)],
        compiler_params=_cparams(("parallel", "arbitrary"), 8 * C * W * 4 + 2 * G * N * N * 4),
    )(xs_rk, lw, k, xs_rk, a, b)


def wkv_bwd(xs_rk, lw, k, a, b, states, dy):
    S = lw.shape[0]
    C, nc, G, N = WKV_CHUNK, S // WKV_CHUNK, WKV_HEADS_PER_STEP, HEAD_DIM

    def body(r_ref, lw_ref, k_ref, v_ref, a_ref, b_ref, st_ref, dy_ref,
             dr_ref, dlw_ref, dk_ref, dv_ref, da_ref, db_ref, dstate):
        @pl.when(pl.program_id(1) == 0)
        def _():
            dstate[...] = jnp.zeros_like(dstate)

        heads = lambda ref: tuple(ref[:, _head_cols(h)] for h in range(G))
        _, pull = jax.vjp(_wkv_chunk, tuple(st_ref[h] for h in range(G)), heads(r_ref), heads(lw_ref),
                          heads(k_ref), heads(v_ref), heads(a_ref), heads(b_ref))
        ds0, *grads = pull((heads(dy_ref), tuple(dstate[h] for h in range(G))))
        for h in range(G):
            dstate[h] = ds0[h]
            for ref, grad in zip((dr_ref, dlw_ref, dk_ref, dv_ref, da_ref, db_ref), grads):
                ref[:, _head_cols(h)] = grad[h]

    W = G * N
    seq = lambda j: pl.BlockSpec((C, W), functools.partial(lambda j, g, c: (nc - 1 - c, j + g), j))
    per = D_MODEL // W
    st = pl.BlockSpec((None, G, N, N), lambda g, c: (nc - 1 - c, g, 0, 0))
    return pl.pallas_call(
        body, name="wkv_bwd", grid=(RWKV_HEADS // G, nc),
        in_specs=[seq(0), seq(0), seq(0), seq(2 * per), seq(0), seq(0), st, seq(0)],
        out_specs=[seq(0)] * 6, out_shape=[jax.ShapeDtypeStruct((S, D_MODEL), F32)] * 6,
        scratch_shapes=[pltpu.VMEM((G, N, N), F32)],
        compiler_params=_cparams(("parallel", "arbitrary"), 14 * C * W * 4 + 2 * G * N * N * 4),
    )(xs_rk, lw, k, xs_rk, a, b, states, dy)


def _first_flag(i, seq_len):
    per_group = ATTN_HPG * seq_len // ATTN_BLK
    g = i // per_group
    per_seq = [seq_len // d // ATTN_BLK for _, d in ATTN_PAIRS]
    n = jnp.where(g == 0, per_seq[0], jnp.where(g == 1, per_seq[1], per_seq[2]))
    return (lax.rem(i, n) == 0).astype(F32)


def attn_fwd(q, k, v, q_gain, k_gain, seq_len):
    R, N = q.shape
    nb = R // ATTN_BLK

    def body(q_ref, kc_ref, kp_ref, vc_ref, vp_ref, qg_ref, kg_ref, o_ref, lse_ref):
        first = _first_flag(pl.program_id(0), seq_len)
        o, lse = _attn_block(q_ref[...], kc_ref[...], kp_ref[...], vc_ref[...], vp_ref[...],
                             qg_ref[...], kg_ref[...], first)
        o_ref[...] = o
        lse_ref[...] = lse

    cur = pl.BlockSpec((ATTN_BLK, N), lambda i: (i, 0))
    prv = pl.BlockSpec((ATTN_BLK, N), lambda i: (jnp.maximum(i - 1, 0), 0))
    gain = pl.BlockSpec((1, N), lambda i: (0, 0))
    return pl.pallas_call(
        body, name="attn_fwd", grid=(nb,), in_specs=[cur, cur, prv, cur, prv, gain, gain],
        out_specs=[cur, cur], out_shape=[jax.ShapeDtypeStruct((R, N), F32)] * 2,
        compiler_params=_cparams(("parallel",), 8 * ATTN_BLK * 128 * 4),
    )(q, k, k, v, v, q_gain, k_gain)


def attn_bwd(q, k, v, q_gain, k_gain, do, dlse, seq_len):
    R, N = q.shape
    nb = R // ATTN_BLK

    def body(q_ref, kc_ref, kp_ref, vc_ref, vp_ref, qg_ref, kg_ref, do_ref, dl_ref,
             dq_ref, dk_ref, dv_ref, dqg_ref, dkg_ref, carry_k, carry_v):
        step = pl.program_id(0)
        first = _first_flag(nb - 1 - step, seq_len)

        @pl.when(step == 0)
        def _():
            carry_k[...] = jnp.zeros_like(carry_k)
            carry_v[...] = jnp.zeros_like(carry_v)
            dqg_ref[...] = jnp.zeros_like(dqg_ref)
            dkg_ref[...] = jnp.zeros_like(dkg_ref)

        _, pull = jax.vjp(functools.partial(_attn_block, first=first), q_ref[...], kc_ref[...], kp_ref[...],
                          vc_ref[...], vp_ref[...], qg_ref[...], kg_ref[...])
        dq, dkc, dkp, dvc, dvp, dqg, dkg = pull((do_ref[...], dl_ref[...]))
        dq_ref[...] = dq
        dk_ref[...] = dkc + carry_k[...]
        dv_ref[...] = dvc + carry_v[...]
        carry_k[...] = dkp
        carry_v[...] = dvp
        dqg_ref[...] += dqg
        dkg_ref[...] += dkg

    cur = pl.BlockSpec((ATTN_BLK, N), lambda i: (nb - 1 - i, 0))
    prv = pl.BlockSpec((ATTN_BLK, N), lambda i: (jnp.maximum(nb - 2 - i, 0), 0))
    gain = pl.BlockSpec((1, N), lambda i: (0, 0))
    return pl.pallas_call(
        body, name="attn_bwd", grid=(nb,), in_specs=[cur, cur, prv, cur, prv, gain, gain, cur, cur],
        out_specs=[cur, cur, cur, gain, gain],
        out_shape=[jax.ShapeDtypeStruct((R, N), F32)] * 3 + [jax.ShapeDtypeStruct((1, N), F32)] * 2,
        scratch_shapes=[pltpu.VMEM((ATTN_BLK, N), F32)] * 2,
        compiler_params=_cparams(("arbitrary",), 16 * ATTN_BLK * 128 * 4),
    )(q, k, k, v, v, q_gain, k_gain, do, dlse)


def to_subsequences(t):
    S = t.shape[0]
    parts = []
    for gi, (_, d) in enumerate(ATTN_PAIRS):
        tg = t[:, GROUP_W * gi:GROUP_W * (gi + 1)].reshape(S // d, d, ATTN_HPG, HEAD_DIM)
        parts.append(tg.transpose(1, 2, 0, 3).reshape(ATTN_HPG * S, HEAD_DIM))
    return jnp.concatenate(parts, axis=0)


def from_subsequences(u, S):
    parts = []
    for gi, (_, d) in enumerate(ATTN_PAIRS):
        ug = u[ATTN_HPG * S * gi:ATTN_HPG * S * (gi + 1)].reshape(d, ATTN_HPG, S // d, HEAD_DIM)
        parts.append(ug.transpose(2, 0, 1, 3).reshape(S, GROUP_W))
    return jnp.concatenate(parts, axis=1)


def _ffn_fwd(x, norm, w_in, w_out, tag):
    h = rowmap(_rms, [x], [norm], [(D_MODEL, BF16)], tb=512, name=tag + "_norm")[0]
    gu = matmul_cs(h, w_in, "nn", tag + "_in")
    act = rowmap(_swiglu_act, [gu], [], [(D_FF, BF16)], tb=256, name=tag + "_act")[0]
    y = matmul(act, w_out, "nn", tag + "_out", add=x, scale=0.5)
    return y, (x, h, gu, act)


def _ffn_bwd(dy, saved, norm, w_in, w_out, tag):
    x, h, gu, act = saved
    dact = matmul(dy, w_out, "nt", tag + "_dact", scale=0.5)
    dw_out = matmul(act, dy, "tn", tag + "_dwout", scale=0.5)

    def act_bwd(gu_b, dact_b):
        return jax.vjp(_swiglu_act, gu_b)[1](dact_b)[0]

    dgu = rowmap(act_bwd, [gu, dact], [], [(2 * D_FF, BF16)], tb=256, name=tag + "_dgu")[0]
    dh = matmul_cs(dgu, w_in, "nt", tag + "_dh")
    dw_in = matmul_cs(h, dgu, "tn", tag + "_dwin")

    def norm_bwd(x_b, dh_b, dy_b, g):
        dx, dg = jax.vjp(_rms, x_b, g)[1](dh_b)
        return dy_b + dx, dg

    dx, dnorm = rowmap(norm_bwd, [x, dh, dy], [norm], [(D_MODEL, F32)], [(1, D_MODEL)], tb=256,
                       name=tag + "_dnorm")
    return dx, dnorm, dw_in, dw_out


def layer_step(x, tgt, W, P):
    S = x.shape[0]
    seg = (jnp.arange(D_MODEL)[:, None] // HEAD_DIM == jnp.arange(RWKV_HEADS)[None, :]).astype(F32)
    seg_t = seg.T
    w_rkv, w_lora = W["w_in"][:, :RKV], W["w_in"][:, RKV:RKV + LORA]
    w_qkv = W["w_in"][:, RKV + LORA:RKV + LORA + 3 * ATTN_WIDTH]
    w_gate = W["w_in"][:, RKV + LORA + 3 * ATTN_WIDTH:]
    mu_rk, mu_lo = P["rwkv_mu"][:, :RKV], P["rwkv_mu"][:, RKV:]
    zeros = lambda n: jnp.zeros((n, D_MODEL), F32)
    w2p = jnp.concatenate([W["rwkv_w2"], zeros(LORA - LORA_W)], axis=0)
    a2p = jnp.concatenate([zeros(LORA_W), W["rwkv_a2"], zeros(LORA_G)], axis=0)
    g2p = jnp.concatenate([zeros(LORA_W + LORA_A), W["rwkv_g2"]], axis=0)
    pre_params = [P["rwkv_w0"], w2p, P["rwkv_a0"], a2p, g2p, P["rwkv_k_k"], P["rwkv_k_a"], seg, seg_t]
    post_params = [P["rwkv_r_k"], P["rwkv_ln_w"], P["rwkv_ln_b"], seg, seg_t]
    col = lambda arr, j: (arr, D_MODEL, j)

    x1, ffn1_saved = _ffn_fwd(x, P["ffn1_norm"], W["ffn1_w_in"], W["ffn1_w_out"], "ffn1")
    h = rowmap(_rms, [x1], [P["mix_norm"]], [(D_MODEL, BF16)], tb=512, name="mix_norm")[0]
    p_rk = matmul(h, w_rkv, "nn", "proj_rkv")
    p_lo = matmul(h, w_lora, "nn", "proj_lora")
    p_qkv = matmul(h, w_qkv, "nn", "proj_qkv")
    p_gate = matmul(h, w_gate, "nn", "proj_gate")
    xs_rk = token_shift_fwd(p_rk, mu_rk, tb=256, name="shift_rk")
    xs_lo = token_shift_fwd(p_lo, mu_lo, tb=256, name="shift_lora")
    lw, k_mod, a_neg, b_kk, g = rowmap(
        _rwkv_pre, [xs_rk, xs_lo], pre_params, [(D_MODEL, F32)] * 5, tb=256, name="rwkv_pre")
    wkv, states = wkv_fwd(xs_rk, lw, k_mod, a_neg, b_kk)
    post_rows = [wkv, col(xs_rk, 0), k_mod, col(xs_rk, 2), g]
    y_a = rowmap(_rwkv_post, post_rows, post_params, [(D_MODEL, BF16)], tb=256, name="rwkv_post")[0]

    q_s = to_subsequences(p_qkv[:, :ATTN_WIDTH])
    k_s = to_subsequences(p_qkv[:, ATTN_WIDTH:2 * ATTN_WIDTH])
    v_s = to_subsequences(p_qkv[:, 2 * ATTN_WIDTH:])
    o_s, lse_s = attn_fwd(q_s, k_s, v_s, P["attn_q_norm"], P["attn_k_norm"], S)
    o, lse = from_subsequences(o_s, S), from_subsequences(lse_s, S)
    y_b = rowmap(_group_combine, [o, lse], [], [(ATTN_WIDTH, BF16)], tb=512, name="attn_combine")[0]

    pa = matmul(y_a, W["w_proj_rwkv"], "nn", "proj_a")
    pb = matmul(y_b, W["w_proj_attn"], "nn", "proj_b")
    merged = rowmap(_gate_merge, [p_gate, pa, pb], [P["b_gate"]], [(D_MODEL, BF16)], tb=256, name="merge")[0]
    x2 = matmul(merged, W["w_out"], "nn", "mix_out", add=x1)
    x3, ffn2_saved = _ffn_fwd(x2, P["ffn2_norm"], W["ffn2_w_in"], W["ffn2_w_out"], "ffn2")

    def loss_head(y_b_, t_b):
        err = y_b_ - t_b
        return err * (1.0 / D_MODEL), (0.5 / D_MODEL) * jnp.sum(err * err, axis=0, keepdims=True)

    dx3, loss_cols = rowmap(loss_head, [x3, tgt], [], [(D_MODEL, F32)], [(1, D_MODEL)], tb=512, name="loss")

    gW, gP = {}, {}
    dx2, gP["ffn2_norm"], gW["ffn2_w_in"], gW["ffn2_w_out"] = _ffn_bwd(
        dx3, ffn2_saved, P["ffn2_norm"], W["ffn2_w_in"], W["ffn2_w_out"], "ffn2")

    dmerged = matmul(dx2, W["w_out"], "nt", "d_merged")
    gW["w_out"] = matmul(merged, dx2, "tn", "dw_out")

    def merge_bwd(pg, pa_b, pb_b, dm, bg):
        return jax.vjp(_gate_merge, pg, pa_b, pb_b, bg)[1](dm)

    dp_gate, dpa, dpb, gP["b_gate"] = rowmap(
        merge_bwd, [p_gate, pa, pb, dmerged], [P["b_gate"]],
        [(2 * D_MODEL, BF16), (D_MODEL, BF16), (D_MODEL, BF16)], [(1, 2 * D_MODEL)], tb=256, name="merge_bwd")
    dy_a = matmul(dpa, W["w_proj_rwkv"], "nt", "d_ya")
    gW["w_proj_rwkv"] = matmul(y_a, dpa, "tn", "dw_proj_a")
    dy_b = matmul(dpb, W["w_proj_attn"], "nt", "d_yb")
    gW["w_proj_attn"] = matmul(y_b, dpb, "tn", "dw_proj_b")

    def combine_bwd(o_b, l_b, d_b):
        return jax.vjp(_group_combine, o_b, l_b)[1](d_b)

    do, dlse = rowmap(combine_bwd, [o, lse, dy_b], [], [(ATTN_WIDTH, F32)] * 2, tb=256, name="attn_combine_bwd")
    dq_s, dk_s, dv_s, gP["attn_q_norm"], gP["attn_k_norm"] = attn_bwd(
        q_s, k_s, v_s, P["attn_q_norm"], P["attn_k_norm"], to_subsequences(do), to_subsequences(dlse), S)
    dp_qkv = jnp.concatenate([from_subsequences(t, S) for t in (dq_s, dk_s, dv_s)], axis=1).astype(BF16)

    def post_bwd(wkv_b, r_b, k_b, v_b, g_b, d_b, r_k, ln_w, ln_b, sg, sgt):
        f = lambda *a: _rwkv_post(*a, sg, sgt)
        return jax.vjp(f, wkv_b, r_b, k_b, v_b, g_b, r_k, ln_w, ln_b)[1](d_b)

    dwkv, dr_p, dk_p, dv_p, dg, gP["rwkv_r_k"], gP["rwkv_ln_w"], gP["rwkv_ln_b"] = rowmap(
        post_bwd, post_rows + [dy_a], post_params, [(D_MODEL, F32)] * 5, [(1, D_MODEL)] * 3, tb=128,
        name="rwkv_post_bwd")
    dr_w, dlw, dk_w, dv_w, da_neg, db_kk = wkv_bwd(xs_rk, lw, k_mod, a_neg, b_kk, states, dwkv)

    def pre_bwd(xrk_b, xlo_b, dlw_b, dkw_b, dkp_b, da_b, db_b, dg_b, drp_b, drw_b, dvp_b, dvw_b,
                w0, w2, a0, a2, g2, k_k, k_a, sg, sgt):
        f = lambda *a: _rwkv_pre(*a, sg, sgt)
        pull = jax.vjp(f, xrk_b, xlo_b, w0, w2, a0, a2, g2, k_k, k_a)[1]
        dxrk, dxlo, *dpar = pull((dlw_b, dkw_b + dkp_b, da_b, db_b, dg_b))
        direct = jnp.concatenate([drp_b + drw_b, jnp.zeros_like(drp_b), dvp_b + dvw_b], axis=1)
        return (dxrk + direct, dxlo, *dpar)

    pre_rows = [xs_rk, xs_lo, dlw, dk_w, dk_p, da_neg, db_kk, dg, dr_p, dr_w, dv_p, dv_w]
    dxs_rk, dxs_lo, gP["rwkv_w0"], dw2p, gP["rwkv_a0"], da2p, dg2p, gP["rwkv_k_k"], gP["rwkv_k_a"] = rowmap(
        pre_bwd, pre_rows, pre_params, [(RKV, F32), (LORA, F32)],
        [(1, D_MODEL), (LORA, D_MODEL), (1, D_MODEL), (LORA, D_MODEL), (LORA, D_MODEL), (1, D_MODEL), (1, D_MODEL)],
        tb=128, name="rwkv_pre_bwd")
    gW["rwkv_w2"] = dw2p[:LORA_W]
    gW["rwkv_a2"] = da2p[LORA_W:LORA_W + LORA_A]
    gW["rwkv_g2"] = dg2p[LORA_W + LORA_A:]
    dp_rk, dmu_rk = token_shift_bwd(dxs_rk, p_rk, mu_rk, tb=256, name="shift_rk_bwd")
    dp_lo, dmu_lo = token_shift_bwd(dxs_lo, p_lo, mu_lo, tb=256, name="shift_lora_bwd")
    gP["rwkv_mu"] = jnp.concatenate([dmu_rk, dmu_lo], axis=1)

    dh = matmul(dp_rk, w_rkv, "nt", "dh_rkv")
    dh = matmul(dp_lo, w_lora, "nt", "dh_lora", add=dh)
    dh = matmul(dp_qkv, w_qkv, "nt", "dh_qkv", add=dh)
    dh = matmul(dp_gate, w_gate, "nt", "dh_gate", add=dh)
    gW["w_in"] = jnp.concatenate([
        matmul(h, dp_rk, "tn", "dw_rkv"), matmul(h, dp_lo, "tn", "dw_lora"),
        matmul(h, dp_qkv, "tn", "dw_qkv"), matmul(h, dp_gate, "tn", "dw_gate")], axis=1)

    def norm_bwd(x_b, dh_b, dy_b, gn):
        dx, dgn = jax.vjp(_rms, x_b, gn)[1](dh_b)
        return dy_b + dx, dgn

    dx1, gP["mix_norm"] = rowmap(norm_bwd, [x1, dh, dx2], [P["mix_norm"]], [(D_MODEL, F32)], [(1, D_MODEL)],
                                 tb=256, name="mix_norm_bwd")
    dx, gP["ffn1_norm"], gW["ffn1_w_in"], gW["ffn1_w_out"] = _ffn_bwd(
        dx1, ffn1_saved, P["ffn1_norm"], W["ffn1_w_in"], W["ffn1_w_out"], "ffn1")
    return loss_cols, dx, gW, gP


N_SHARDS = 4
BIG = (("ffn1_w_in", (D_MODEL, 2 * D_FF), 1), ("ffn1_w_out", (D_FF, D_MODEL), 0),
       ("w_in", (D_MODEL, 7712), 1), ("rwkv_w2", (LORA_W, D_MODEL), 1), ("rwkv_a2", (LORA_A, D_MODEL), 1),
       ("rwkv_g2", (LORA_G, D_MODEL), 1), ("w_proj_rwkv", (D_MODEL, D_MODEL), 0),
       ("w_proj_attn", (ATTN_WIDTH, D_MODEL), 1), ("w_out", (D_MODEL, D_MODEL), 0),
       ("ffn2_w_in", (D_MODEL, 2 * D_FF), 1), ("ffn2_w_out", (D_FF, D_MODEL), 0))
SMALL = (("ffn1_norm", 1024), ("mix_norm", 1024), ("b_gate", 2048), ("rwkv_mu", 3360), ("rwkv_w0", 1024),
         ("rwkv_a0", 1024), ("rwkv_k_k", 1024), ("rwkv_k_a", 1024), ("rwkv_r_k", 1024), ("rwkv_ln_w", 1024),
         ("rwkv_ln_b", 1024), ("attn_q_norm", 64), ("attn_k_norm", 64), ("ffn2_norm", 1024))
WEIGHT_ORDER = ("ffn1_norm", "ffn1_w_in", "ffn1_w_out", "mix_norm", "w_in", "b_gate", "rwkv_mu", "rwkv_w0",
                "rwkv_w2", "rwkv_a0", "rwkv_a2", "rwkv_g2", "rwkv_k_k", "rwkv_k_a", "rwkv_r_k", "rwkv_ln_w",
                "rwkv_ln_b", "attn_q_norm", "attn_k_norm", "w_proj_rwkv", "w_proj_attn", "w_out", "ffn2_norm",
                "ffn2_w_in", "ffn2_w_out")


LORA_PARTS = ("rwkv_w2", "rwkv_a2", "rwkv_g2")
BLOCK_MAJOR = ("ffn1_w_in", "ffn2_w_in")
SMALL_USED = D_MODEL + sum(n for _, n in SMALL)
SMALL_W = -(-SMALL_USED // 128) * 128


def _travel():
    out = {}
    for name, shape, axis in BIG:
        if name == LORA_PARTS[0]:
            out["lora"] = ((LORA, D_MODEL), 1)
        elif name not in LORA_PARTS:
            out[name] = (shape, axis)
    return out


def local_blocks(vals):
    out = {n: vals[n] for n in _travel() if n != "lora"}
    out["lora"] = jnp.concatenate([vals[n] for n in LORA_PARTS], axis=0)
    return out


def split_lora(t):
    return {"rwkv_w2": t[:LORA_W], "rwkv_a2": t[LORA_W:LORA_W + LORA_A], "rwkv_g2": t[LORA_W + LORA_A:]}


def blocks_to_full(name, blocks):
    shape, axis = _travel()[name]
    if name in BLOCK_MAJOR:
        return blocks
    if axis == 0:
        return blocks.reshape(shape)
    return blocks.transpose(1, 0, 2).reshape(shape)


def full_to_blocks(name, full):
    shape, axis = _travel()[name]
    if name in BLOCK_MAJOR:
        return full
    if axis == 0:
        return full.reshape(N_SHARDS, shape[0] // N_SHARDS, shape[1])
    return full.reshape(shape[0], N_SHARDS, shape[1] // N_SHARDS).transpose(1, 0, 2)


def pack_small(vals, head):
    parts = [head] + [vals[name].reshape(1, n) for name, n in SMALL]
    parts.append(jnp.zeros((1, SMALL_W - SMALL_USED), F32))
    return jnp.concatenate(parts, axis=1)


def unpack_small(vec, shapes):
    out, off = {}, D_MODEL
    for name, n in SMALL:
        out[name] = vec[:, off:off + n].reshape(shapes[name])
        off += n
    return out


def _place():
    return lax.axis_index("x"), lax.axis_index("y"), lax.axis_index("c")


def _other_chips(x, y):
    return [(1 - x, y), (x, 1 - y), (1 - x, 1 - y)]


def _remote(src, dst, send_sem, recv_sem, device):
    return pltpu.make_async_remote_copy(src_ref=src, dst_ref=dst, send_sem=send_sem, recv_sem=recv_sem,
                                        device_id=device, device_id_type=MESH)


def _half(ref, who):
    hr = ref.shape[-2] // 2
    rows = pl.ds(pl.multiple_of(who * hr, 8), hr)
    return ref.at[rows] if len(ref.shape) == 2 else ref.at[:, rows]


HBM_REF = pl.BlockSpec(memory_space=pl.ANY)
COMM_PARAMS = dict(compiler_params=pltpu.CompilerParams(has_side_effects=True))


def gather_weights(blocks):
    n = len(blocks)

    def body(*refs):
        ins, outs = refs[:n], refs[n:2 * n]
        ici_send, ici_recv, d2d_send, d2d_recv, local_sems = refs[2 * n:]
        x, y, c = _place()
        me, sibling, chips = 2 * x + y, (x, y, 1 - c), _other_chips(x, y)
        own = [pltpu.make_async_copy(ins[t], outs[t].at[me], local_sems.at[t]) for t in range(n)]
        for cp in own:
            cp.start()
        first = [_remote(_half(ins[t], c), _half(outs[t].at[me], c), ici_send.at[k, t], ici_recv.at[k, t],
                         (px, py, c)) for k, (px, py) in enumerate(chips) for t in range(n)]
        for cp in first:
            cp.start()
        passed = []
        for k, (px, py) in enumerate(chips):
            for t in range(n):
                landed = _half(outs[t].at[2 * px + py], c)
                _remote(landed, landed, ici_send.at[k, t], ici_recv.at[k, t], (px, py, c)).wait_recv()
                cp = _remote(landed, landed, d2d_send.at[k, t], d2d_recv.at[k, t], sibling)
                cp.start()
                passed.append(cp)
        for k, (px, py) in enumerate(chips):
            for t in range(n):
                other = _half(outs[t].at[2 * px + py], 1 - c)
                _remote(other, other, d2d_send.at[k, t], d2d_recv.at[k, t], sibling).wait_recv()
        for cp in first + passed:
            cp.wait_send()
        for cp in own:
            cp.wait()

    sems = [pltpu.SemaphoreType.DMA((3, n))] * 4 + [pltpu.SemaphoreType.DMA((n,))]
    return pl.pallas_call(
        body, name="gather_weights", in_specs=[HBM_REF] * n, out_specs=[HBM_REF] * n,
        out_shape=[jax.ShapeDtypeStruct((N_SHARDS,) + b.shape, b.dtype) for b in blocks],
        scratch_shapes=sems, **COMM_PARAMS)(*blocks)


def swap_halves(grads):
    n = len(grads)

    def body(*refs):
        ins, own, got = refs[:n], refs[n:2 * n], refs[2 * n:3 * n]
        send_sems, recv_sems, local_sems = refs[3 * n:]
        x, y, c = _place()
        keep = [pltpu.make_async_copy(_half(ins[t], c), own[t], local_sems.at[t]) for t in range(n)]
        give = [_remote(_half(ins[t], 1 - c), got[t], send_sems.at[t], recv_sems.at[t], (x, y, 1 - c))
                for t in range(n)]
        for cp in keep + give:
            cp.start()
        for cp in give:
            cp.wait_recv()
        for cp in give:
            cp.wait_send()
        for cp in keep:
            cp.wait()

    halves = [jax.ShapeDtypeStruct((g.shape[0], g.shape[1] // 2, g.shape[2]), g.dtype) for g in grads]
    res = pl.pallas_call(
        body, name="swap_halves", in_specs=[HBM_REF] * n, out_specs=[HBM_REF] * (2 * n),
        out_shape=halves + halves, scratch_shapes=[pltpu.SemaphoreType.DMA((n,))] * 3, **COMM_PARAMS)(*grads)
    return res[:n], res[n:]


def scatter_partials(partials, own, got):
    n = len(partials)

    def body(*refs):
        parts, owns, gots = refs[:n], refs[n:2 * n], refs[2 * n:3 * n]
        landed, mine_a, mine_b = refs[3 * n:4 * n], refs[4 * n:5 * n], refs[5 * n:6 * n]
        send_sems, recv_sems, local_sems = refs[6 * n:]
        x, y, c = _place()
        me, chips = 2 * x + y, _other_chips(x, y)
        local = [pltpu.make_async_copy(owns[t].at[me], mine_a[t], local_sems.at[0, t]) for t in range(n)]
        local += [pltpu.make_async_copy(gots[t].at[me], mine_b[t], local_sems.at[1, t]) for t in range(n)]
        sends = [_remote(parts[t].at[2 * px + py], landed[t].at[k], send_sems.at[k, t], recv_sems.at[k, t],
                         (px, py, c)) for k, (px, py) in enumerate(chips) for t in range(n)]
        for cp in local + sends:
            cp.start()
        for cp in sends:
            cp.wait_recv()
        for cp in sends:
            cp.wait_send()
        for cp in local:
            cp.wait()

    land_t = [jax.ShapeDtypeStruct((3,) + p.shape[1:], p.dtype) for p in partials]
    mine_t = [jax.ShapeDtypeStruct(o.shape[1:], o.dtype) for o in own]
    sems = [pltpu.SemaphoreType.DMA((3, n))] * 2 + [pltpu.SemaphoreType.DMA((2, n))]
    res = pl.pallas_call(
        body, name="scatter_partials", in_specs=[HBM_REF] * (3 * n), out_specs=[HBM_REF] * (3 * n),
        out_shape=land_t + mine_t + mine_t, scratch_shapes=sems, **COMM_PARAMS)(*partials, *own, *got)
    return res[:n], res[n:2 * n], res[2 * n:]


def join_halves(halves):
    n = len(halves)

    def body(*refs):
        ins, outs = refs[:n], refs[n:2 * n]
        send_sems, recv_sems, local_sems = refs[2 * n:]
        x, y, c = _place()
        keep = [pltpu.make_async_copy(ins[t], _half(outs[t], c), local_sems.at[t]) for t in range(n)]
        give = [_remote(ins[t], _half(outs[t], c), send_sems.at[t], recv_sems.at[t], (x, y, 1 - c))
                for t in range(n)]
        for cp in keep + give:
            cp.start()
        for t in range(n):
            arriving = _half(outs[t], 1 - c)
            _remote(ins[t], arriving, send_sems.at[t], recv_sems.at[t], (x, y, 1 - c)).wait_recv()
        for cp in give:
            cp.wait_send()
        for cp in keep:
            cp.wait()

    return pl.pallas_call(
        body, name="join_halves", in_specs=[HBM_REF] * n, out_specs=[HBM_REF] * n,
        out_shape=[jax.ShapeDtypeStruct((2 * h.shape[0], h.shape[1]), h.dtype) for h in halves],
        scratch_shapes=[pltpu.SemaphoreType.DMA((n,))] * 3, **COMM_PARAMS)(*halves)


def reduce_block_grads(grads):
    names = list(grads)
    own, got = swap_halves([grads[n] for n in names])
    partials = []
    for name, o, g in zip(names, own, got):
        rows, width = o.shape[0] * o.shape[1], o.shape[2]
        p = rowmap(jnp.add, [o.reshape(rows, width), g.reshape(rows, width)], [], [(width, BF16)],
                   tb=_row_block(rows, width, 3), name="chip_sum_" + name)[0]
        partials.append(p.reshape(o.shape))
    landed, mine_a, mine_b = scatter_partials(partials, own, got)
    halves = []
    for name, l, a, b in zip(names, landed, mine_a, mine_b):
        rows, width = a.shape
        tb = _row_block(rows, width, 6)
        l2 = l.reshape(3 * rows, width)
        f = lambda a_, b_, l0, l1, l2_: (((a_ + b_) + l0.astype(F32)) + l1.astype(F32)) + l2_.astype(F32)
        views = [(l2, width, 0, k * (rows // tb)) for k in range(3)]
        halves.append(rowmap(f, [a, b, *views], [], [(width, F32)], tb=tb, name="owner_sum_" + name,
                             n_rows=rows)[0])
    return dict(zip(names, join_halves(halves)))


def adamw_block(name, w, g, m, v):
    rows, width = w.shape
    return rowmap(_adamw, [w, g, m, v], [], [(width, F32)] * 3, tb=_row_block(rows, width, 7),
                  name="adamw_" + name)


def reduce_small(vec, w, m, v):
    n_dev = 8

    def body(vec_ref, w_ref, m_ref, v_ref, loss_ref, g_ref, d_ref, m2_ref, v2_ref, slots, send_sems, recv_sems):
        x, y, c = _place()
        me = 4 * x + 2 * y + c
        slots[me] = vec_ref[...]
        flips = [(fx, fy, fc) for fx in (0, 1) for fy in (0, 1) for fc in (0, 1)][1:]
        peers = [(1 - x if fx else x, 1 - y if fy else y, 1 - c if fc else c) for fx, fy, fc in flips]
        sends = [pltpu.make_async_remote_copy(
            src_ref=vec_ref, dst_ref=slots.at[me], send_sem=send_sems.at[j], recv_sem=recv_sems.at[j],
            device_id=peer, device_id_type=MESH) for j, peer in enumerate(peers)]
        for cp in sends:
            cp.start()
        for j, (px, py, pc) in enumerate(peers):
            pltpu.make_async_remote_copy(
                src_ref=vec_ref, dst_ref=slots.at[4 * px + 2 * py + pc], send_sem=send_sems.at[j],
                recv_sem=recv_sems.at[j], device_id=(px, py, pc), device_id_type=MESH).wait_recv()
        for cp in sends:
            cp.wait_send()
        g = slots[0]
        for d in range(1, n_dev):
            g = g + slots[d]
        loss_ref[...] = jnp.sum(g[:, :D_MODEL], axis=1, keepdims=True)
        delta, m2, v2 = _adamw(w_ref[...], g, m_ref[...], v_ref[...])
        g_ref[...], d_ref[...], m2_ref[...], v2_ref[...] = g, delta, m2, v2

    vm = pl.BlockSpec(memory_space=pltpu.VMEM)
    vec_t = jax.ShapeDtypeStruct(vec.shape, F32)
    return pl.pallas_call(
        body, name="reduce_small", in_specs=[vm] * 4, out_specs=[vm] * 5,
        out_shape=[jax.ShapeDtypeStruct((1, 1), F32)] + [vec_t] * 4,
        scratch_shapes=[pltpu.VMEM((n_dev,) + vec.shape, F32), pltpu.SemaphoreType.DMA((n_dev - 1,)),
                        pltpu.SemaphoreType.DMA((n_dev - 1,))],
        compiler_params=pltpu.CompilerParams(has_side_effects=True),
    )(vec, w, m, v)


def kernel(x, ffn1_norm, ffn1_w_in, ffn1_w_out, mix_norm, w_in, b_gate, rwkv_mu, rwkv_w0, rwkv_w2, rwkv_a0, rwkv_a2, rwkv_g2, rwkv_k_k, rwkv_k_a, rwkv_r_k, rwkv_ln_w, rwkv_ln_b, attn_q_norm, attn_k_norm, w_proj_rwkv, w_proj_attn, w_out, ffn2_norm, ffn2_w_in, ffn2_w_out, loss_target, m_ffn1_norm, m_ffn1_w_in, m_ffn1_w_out, m_mix_norm, m_w_in, m_b_gate, m_rwkv_mu, m_rwkv_w0, m_rwkv_w2, m_rwkv_a0, m_rwkv_a2, m_rwkv_g2, m_rwkv_k_k, m_rwkv_k_a, m_rwkv_r_k, m_rwkv_ln_w, m_rwkv_ln_b, m_attn_q_norm, m_attn_k_norm, m_w_proj_rwkv, m_w_proj_attn, m_w_out, m_ffn2_norm, m_ffn2_w_in, m_ffn2_w_out, v_ffn1_norm, v_ffn1_w_in, v_ffn1_w_out, v_mix_norm, v_w_in, v_b_gate, v_rwkv_mu, v_rwkv_w0, v_rwkv_w2, v_rwkv_a0, v_rwkv_a2, v_rwkv_g2, v_rwkv_k_k, v_rwkv_k_a, v_rwkv_r_k, v_rwkv_ln_w, v_rwkv_ln_b, v_attn_q_norm, v_attn_k_norm, v_w_proj_rwkv, v_w_proj_attn, v_w_out, v_ffn2_norm, v_ffn2_w_in, v_ffn2_w_out):
    given = dict(locals())
    weights = {n: given[n] for n in WEIGHT_ORDER}
    mom_m = {n: given["m_" + n] for n in WEIGHT_ORDER}
    mom_v = {n: given["v_" + n] for n in WEIGHT_ORDER}
    big = [name for name, _, _ in BIG]
    shapes = {n: weights[n].shape for n in WEIGHT_ORDER}
    blocks_of = lambda d: local_blocks({n: d[n][0] for n in big})
    w_blk, m_blk, v_blk = blocks_of(weights), blocks_of(mom_m), blocks_of(mom_v)
    names = list(w_blk)

    gathered = gather_weights([w_blk[n].astype(BF16) for n in names])
    W = {n: blocks_to_full(n, g) for n, g in zip(names, gathered)}
    W.update(split_lora(W.pop("lora")))
    P = {n: weights[n].reshape(1, -1) for n, _ in SMALL}

    loss_cols, dx, gW, gP = layer_step(x[0], loss_target[0], W, P)

    gW["lora"] = jnp.concatenate([gW.pop(n) for n in LORA_PARTS], axis=0)
    g_blk = reduce_block_grads({n: full_to_blocks(n, gW[n]) for n in names})
    out_g, out_d, out_m, out_v = {}, {}, {}, {}
    for n in names:
        res = (g_blk[n], *adamw_block(n, w_blk[n], g_blk[n], m_blk[n], v_blk[n]))
        for dst, t in zip((out_g, out_d, out_m, out_v), res):
            for part, val in (split_lora(t) if n == "lora" else {n: t}).items():
                dst[part] = val.reshape(shapes[part])

    zero_head = jnp.zeros((1, D_MODEL), F32)
    vec = pack_small(gP, loss_cols)
    loss, g_s, d_s, m_s, v_s = reduce_small(
        vec, pack_small({n: weights[n] for n, _ in SMALL}, zero_head),
        pack_small({n: mom_m[n] for n, _ in SMALL}, zero_head),
        pack_small({n: mom_v[n] for n, _ in SMALL}, zero_head))
    for dst, src in ((out_g, g_s), (out_d, d_s), (out_m, m_s), (out_v, v_s)):
        dst.update(unpack_small(src, shapes))

    return (loss[0, 0], dx[None], *[out_g[n] for n in WEIGHT_ORDER], *[out_d[n] for n in WEIGHT_ORDER],
            *[out_m[n] for n in WEIGHT_ORDER], *[out_v[n] for n in WEIGHT_ORDER])
```

```python
import functools

import jax
import jax.numpy as jnp
from jax import lax
from jax.experimental import pallas as pl
from jax.experimental.pallas import tpu as pltpu

F32 = jnp.float32
BF16 = jnp.bfloat16
HI = lax.Precision.HIGHEST
MESH = pl.DeviceIdType.MESH

D_MODEL = 1024
HEAD_DIM = 64
RWKV_HEADS = 16
LORA_W, LORA_A, LORA_G = 64, 64, 160
LORA = LORA_W + LORA_A + LORA_G
RKV = 3 * D_MODEL
ATTN_PAIRS = ((128, 1), (512, 4), (2048, 16))
ATTN_BLK = 128
ATTN_HPG = 4
ATTN_WIDTH = 768
GROUP_W = ATTN_HPG * HEAD_DIM
D_FF = 2816
GN_EPS = 64e-5
RMS_EPS = 1e-6
NEG_INF = -1e30
WKV_CHUNK = 64
WKV_HEADS_PER_STEP = 4

ADAM_LR, ADAM_B1, ADAM_B2, ADAM_EPS, ADAM_WD, ADAM_STEP = 0.001, 0.9, 0.999, 1e-08, 0.01, 10

V7X_VMEM_BYTES = 64 << 20
VMEM_TEMP_ALLOWANCE = 20 << 20


def _cparams(sem, block_bytes):
    limit = min(2 * block_bytes + VMEM_TEMP_ALLOWANCE, V7X_VMEM_BYTES - (6 << 20))
    return pltpu.CompilerParams(dimension_semantics=sem, vmem_limit_bytes=int(limit))


def _nbytes(shape, dtype):
    n = 1
    for s in shape:
        n *= s
    return n * jnp.dtype(dtype).itemsize


def _split_bf16(a):
    hi = a.astype(BF16)
    return hi, (a - hi.astype(F32)).astype(BF16)


def _make_dots(prec):
    def one(a, b, ca, cb):
        return lax.dot_general(a, b, (((ca,), (cb,)), ((), ())), precision=None if prec == "x3" else prec,
                               preferred_element_type=F32)

    def raw(a, b, ca, cb):
        if prec != "x3":
            return one(a, b, ca, cb)
        (ah, al), (bh, bl) = _split_bf16(a), _split_bf16(b)
        return one(ah, bh, ca, cb) + (one(al, bh, ca, cb) + one(ah, bl, ca, cb))

    @jax.custom_vjp
    def nn(a, b):
        return raw(a, b, 1, 0)

    @jax.custom_vjp
    def nt(a, b):
        return raw(a, b, 1, 1)

    @jax.custom_vjp
    def tn(a, b):
        return raw(a, b, 0, 0)

    nn.defvjp(lambda a, b: (raw(a, b, 1, 0), (a, b)),
              lambda res, g: (raw(g, res[1], 1, 1), raw(res[0], g, 0, 0)))
    nt.defvjp(lambda a, b: (raw(a, b, 1, 1), (a, b)),
              lambda res, g: (raw(g, res[1], 1, 0), raw(g, res[0], 0, 0)))
    tn.defvjp(lambda a, b: (raw(a, b, 0, 0), (a, b)),
              lambda res, g: (raw(res[1], g, 1, 1), raw(res[0], g, 1, 0)))
    return nn, nt, tn


NN, NT, TN = _make_dots(None)
NN_HI, NT_HI, TN_HI = _make_dots(HI)
NN_X3, NT_X3, TN_X3 = _make_dots("x3")


def _pick(n, cap):
    best = None
    for t in range(128, min(n, cap) + 1, 128):
        if n % t == 0:
            best = t
    return best or n


def matmul(a, b, mode, name, *, add=None, scale=1.0, out_dtype=F32):
    if mode == "nn":
        (M, K), (K2, N) = a.shape, b.shape
    elif mode == "nt":
        (M, K), (N, K2) = a.shape, b.shape
    else:
        (K, M), (K2, N) = a.shape, b.shape
    assert K == K2, (name, a.shape, b.shape)
    tm, tn, tk = _pick(M, 512), _pick(N, 512), _pick(K, 1024)
    nk = K // tk
    ca, cb = {"nn": (1, 0), "nt": (1, 1), "tn": (0, 0)}[mode]

    def body(*refs):
        if add is None:
            a_ref, b_ref, o_ref, acc_ref = refs
        else:
            a_ref, b_ref, add_ref, o_ref, acc_ref = refs
        k = pl.program_id(2)

        @pl.when(k == 0)
        def _():
            acc_ref[...] = jnp.zeros_like(acc_ref)

        acc_ref[...] += lax.dot_general(a_ref[...].astype(BF16), b_ref[...].astype(BF16),
                                        (((ca,), (cb,)), ((), ())), preferred_element_type=F32)

        @pl.when(k == nk - 1)
        def _():
            r = acc_ref[...] * scale
            if add is not None:
                r = add_ref[...] + r
            o_ref[...] = r.astype(o_ref.dtype)

    a_spec = (pl.BlockSpec((tk, tm), lambda i, j, k: (k, i)) if mode == "tn"
              else pl.BlockSpec((tm, tk), lambda i, j, k: (i, k)))
    b_spec = (pl.BlockSpec((tn, tk), lambda i, j, k: (j, k)) if mode == "nt"
              else pl.BlockSpec((tk, tn), lambda i, j, k: (k, j)))
    in_specs, args = [a_spec, b_spec], [a, b]
    blk = tm * tk * a.dtype.itemsize + tk * tn * b.dtype.itemsize + tm * tn * 8
    if add is not None:
        in_specs.append(pl.BlockSpec((tm, tn), lambda i, j, k: (i, j)))
        args.append(add)
        blk += tm * tn * 4
    return pl.pallas_call(
        body, name=name, grid=(M // tm, N // tn, nk),
        in_specs=in_specs, out_specs=pl.BlockSpec((tm, tn), lambda i, j, k: (i, j)),
        out_shape=jax.ShapeDtypeStruct((M, N), out_dtype),
        scratch_shapes=[pltpu.VMEM((tm, tn), F32)],
        compiler_params=_cparams(("parallel", "parallel", "arbitrary"), blk),
    )(*args)


def matmul_cs(a, w, mode, name, *, scale=1.0, out_dtype=F32):
    n_blk = N_SHARDS
    if mode == "tn":
        (K, R), Cs = a.shape, w.shape[1] // n_blk
        tm, tk = _pick(R, 512), _pick(K, 1024)
        grid = (R // tm, n_blk, K // tk)
        a_spec = pl.BlockSpec((tk, tm), lambda i, j, k: (k, i))
        w_spec = pl.BlockSpec((tk, Cs), lambda i, j, k: (k, j))
        o_spec = pl.BlockSpec((None, tm, Cs), lambda i, j, k: (j, i, 0))
        out_shape, acc_shape, dims = (n_blk, R, Cs), (tm, Cs), (0, 0)
        blk = tk * tm * a.dtype.itemsize + tk * Cs * w.dtype.itemsize + tm * Cs * 8
    elif mode == "nn":
        (M, R), Cs = a.shape, w.shape[2]
        tm, tk = _pick(M, 512), _pick(R, 1024)
        grid = (M // tm, n_blk, R // tk)
        a_spec = pl.BlockSpec((tm, tk), lambda i, j, k: (i, k))
        w_spec = pl.BlockSpec((None, tk, Cs), lambda i, j, k: (j, k, 0))
        o_spec = pl.BlockSpec((tm, Cs), lambda i, j, k: (i, j))
        out_shape, acc_shape, dims = (M, n_blk * Cs), (tm, Cs), (1, 0)
        blk = tm * tk * a.dtype.itemsize + tk * Cs * w.dtype.itemsize + tm * Cs * 8
    else:
        M, (_, R, Cs) = a.shape[0], w.shape
        tm, tn = _pick(M, 512), _pick(R, 512)
        grid = (M // tm, R // tn, n_blk)
        a_spec = pl.BlockSpec((tm, Cs), lambda i, j, k: (i, k))
        w_spec = pl.BlockSpec((None, tn, Cs), lambda i, j, k: (k, j, 0))
        o_spec = pl.BlockSpec((tm, tn), lambda i, j, k: (i, j))
        out_shape, acc_shape, dims = (M, R), (tm, tn), (1, 1)
        blk = tm * Cs * a.dtype.itemsize + tn * Cs * w.dtype.itemsize + tm * tn * 8
    nk = grid[2]

    def body(a_ref, w_ref, o_ref, acc_ref):
        k = pl.program_id(2)

        @pl.when(k == 0)
        def _():
            acc_ref[...] = jnp.zeros_like(acc_ref)

        acc_ref[...] += lax.dot_general(a_ref[...].astype(BF16), w_ref[...].astype(BF16),
                                        (((dims[0],), (dims[1],)), ((), ())), preferred_element_type=F32)

        @pl.when(k == nk - 1)
        def _():
            o_ref[...] = (acc_ref[...] * scale).astype(o_ref.dtype)

    return pl.pallas_call(
        body, name=name, grid=grid, in_specs=[a_spec, w_spec], out_specs=o_spec,
        out_shape=jax.ShapeDtypeStruct(out_shape, out_dtype), scratch_shapes=[pltpu.VMEM(acc_shape, F32)],
        compiler_params=_cparams(("parallel", "parallel", "arbitrary"), blk),
    )(a, w)


def _row_block(n, width, n_arrays):
    cap = (V7X_VMEM_BYTES // 4) // (2 * 4 * width * n_arrays)
    best = None
    for t in range(16, min(n, cap) + 1, 16):
        if n % t == 0:
            best = t
    return best or n


def placed_map(f, ins, out, *, n_blocks, tb, name):
    def body(*refs):
        refs[-1][...] = f(*[r[...] for r in refs[:-1]]).astype(refs[-1].dtype)

    def spec(fn):
        def index(i):
            x, y, c = _place()
            return fn(i, (c, 2 * x + y)), 0
        return pl.BlockSpec((tb, width), index)

    o_rows, width, o_dtype, o_fn = out
    blk = (sum(a.dtype.itemsize for a, _ in ins) + jnp.dtype(o_dtype).itemsize) * tb * width
    return pl.pallas_call(
        body, name=name, grid=(n_blocks,), in_specs=[spec(fn) for _, fn in ins], out_specs=spec(o_fn),
        out_shape=jax.ShapeDtypeStruct((o_rows, width), o_dtype),
        compiler_params=_cparams(("parallel",), blk),
    )(*[a for a, _ in ins])


def rowmap(f, rows, params, outs, accs=(), *, tb, name):
    rows = [r if isinstance(r, tuple) else (r, r.shape[1], 0) for r in rows]
    S = rows[0][0].shape[0]
    assert S % tb == 0, (name, S, tb)
    n_in, n_out = len(rows) + len(params), len(outs)

    def body(*refs):
        res = f(*[r[...] for r in refs[:n_in]])
        res = res if isinstance(res, (tuple, list)) else (res,)
        o_refs, a_refs = refs[n_in:n_in + n_out], refs[n_in + n_out:]
        for ref, val in zip(o_refs, res[:n_out]):
            ref[...] = val.astype(ref.dtype)
        if a_refs:
            @pl.when(pl.program_id(0) == 0)
            def _():
                for ref in a_refs:
                    ref[...] = jnp.zeros_like(ref)

            for ref, val in zip(a_refs, res[n_out:]):
                ref[...] += val.astype(F32)

    in_specs = [pl.BlockSpec((tb, w), functools.partial(lambda cb, i: (i, cb), cb)) for _, w, cb in rows]
    in_specs += [pl.BlockSpec(p.shape, lambda i: (0, 0)) for p in params]
    out_specs = [pl.BlockSpec((tb, w), lambda i: (i, 0)) for w, _ in outs]
    out_specs += [pl.BlockSpec(tuple(s), lambda i: (0, 0)) for s in accs]
    out_shape = [jax.ShapeDtypeStruct((S, w), dt) for w, dt in outs]
    out_shape += [jax.ShapeDtypeStruct(tuple(s), F32) for s in accs]
    blk = sum(tb * w * a.dtype.itemsize for a, w, _ in rows) + sum(_nbytes(p.shape, p.dtype) for p in params)
    blk += sum(_nbytes((tb, w), dt) for w, dt in outs) + sum(_nbytes(s, F32) for s in accs)
    res = pl.pallas_call(
        body, name=name, grid=(S // tb,), in_specs=in_specs, out_specs=out_specs, out_shape=out_shape,
        compiler_params=_cparams(("arbitrary",) if accs else ("parallel",), blk),
    )(*[r[0] for r in rows], *params)
    return res


def _rms(x, g):
    return x * lax.rsqrt(jnp.mean(x * x, axis=-1, keepdims=True) + RMS_EPS) * g


def _softplus(z):
    return jnp.maximum(z, 0.0) + jnp.log(1.0 + jnp.exp(-jnp.abs(z)))


def _swiglu_act(gu):
    gate, up = gu[:, :D_FF], gu[:, D_FF:]
    return gate * jax.nn.sigmoid(gate) * up


def _rwkv_pre(xrk, xlo, w0, w2p, a0, a2p, g2p, k_k, k_a, seg, seg_t):
    k = xrk[:, D_MODEL:2 * D_MODEL]
    w = -_softplus(-(w0 + NN(jnp.tanh(xlo), w2p))) - 0.5
    log_decay = -jnp.exp(w)
    a = jax.nn.sigmoid(a0 + NN(xlo, a2p))
    g = NN(jax.nn.sigmoid(xlo), g2p)
    kk = k * k_k
    norm = jnp.maximum(jnp.sqrt(NN_HI(kk * kk, seg)), 1e-12)
    kk = kk / NN_HI(norm, seg_t)
    k_mod = k * (1.0 + (a - 1.0) * k_a)
    return log_decay, k_mod, -kk, kk * a, g


def _rwkv_post(wkv, r, k_mod, v, g, r_k, ln_w, ln_b, seg, seg_t):
    inv_n = 1.0 / HEAD_DIM
    mean = NN_HI(wkv, seg) * inv_n
    cen = wkv - NN_HI(mean, seg_t)
    var = NN_HI(cen * cen, seg) * inv_n
    y = cen * NN_HI(lax.rsqrt(var + GN_EPS), seg_t) * ln_w + ln_b
    bonus = NN_HI(NN_HI(r * k_mod * r_k, seg), seg_t) * v
    return (y + bonus) * g


def _gate_merge(pgate, pa, pb, b_gate):
    sg = jax.nn.sigmoid(pgate + b_gate)
    return sg[:, :D_MODEL] * pa + sg[:, D_MODEL:] * pb


def _group_combine(o, lse):
    ls = [lse[:, GROUP_W * i:GROUP_W * (i + 1)] for i in range(3)]
    m = jnp.maximum(jnp.maximum(ls[0], ls[1]), ls[2])
    es = [jnp.exp(l - m) for l in ls]
    den = es[0] + es[1] + es[2]
    return jnp.concatenate([o[:, GROUP_W * i:GROUP_W * (i + 1)] * (es[i] / den) for i in range(3)], axis=1)


def _attn_block(q, kc, kp, vc, vp, q_gain, k_gain, first):
    qn = _rms(q, q_gain) * (HEAD_DIM ** -0.5)
    kcn, kpn = _rms(kc, k_gain), _rms(kp, k_gain)
    qi = lax.broadcasted_iota(jnp.int32, (ATTN_BLK, ATTN_BLK), 0)
    kj = lax.broadcasted_iota(jnp.int32, (ATTN_BLK, ATTN_BLK), 1)
    s_c = jnp.where(kj <= qi, NT(qn, kcn), NEG_INF)
    s_p = jnp.where((kj >= qi) & (first < 0.5), NT(qn, kpn), NEG_INF)
    m = jnp.maximum(jnp.max(s_c, axis=-1, keepdims=True), jnp.max(s_p, axis=-1, keepdims=True))
    e_c, e_p = jnp.exp(s_c - m), jnp.exp(s_p - m)
    den = jnp.sum(e_c, axis=-1, keepdims=True) + jnp.sum(e_p, axis=-1, keepdims=True)
    o = NN(e_c / den, vc) + NN(e_p / den, vp)
    lse = m + jnp.log(den)
    return o, jnp.broadcast_to(lse, (ATTN_BLK, HEAD_DIM))


def _each(f, *xs):
    return tuple(f(*args) for args in zip(*xs))


def _tri_inverse(n):
    c = n[0].shape[0]
    eye = (lax.broadcasted_iota(jnp.int32, (c, c), 0) == lax.broadcasted_iota(jnp.int32, (c, c), 1)).astype(F32)
    t, p, span = _each(lambda m: eye + m, n), n, 2
    while span < c:
        p = _each(NN_X3, p, p)
        t = _each(lambda t_, p_: t_ + NN_X3(t_, p_), t, p)
        span *= 2
    return t


@jax.custom_vjp
def _tri_solve(n, rhs):
    return _each(NN_X3, _tri_inverse(n), rhs)


def _tri_solve_fwd(n, rhs):
    t = _tri_inverse(n)
    x = _each(NN_X3, t, rhs)
    return x, (t, x)


def _tri_solve_bwd(res, dx):
    t, x = res
    drhs = _each(TN_X3, t, dx)
    return _each(NT_X3, drhs, x), drhs


_tri_solve.defvjp(_tri_solve_fwd, _tri_solve_bwd)


def _lower_ones(c):
    row = lax.broadcasted_iota(jnp.int32, (c, c), 0)
    col = lax.broadcasted_iota(jnp.int32, (c, c), 1)
    return (row >= col).astype(BF16)


def _ones_dot(ones, x, contract):
    hi, lo = _split_bf16(x)
    dims = (((contract,), (0,)), ((), ()))
    return (lax.dot_general(ones, hi, dims, preferred_element_type=F32)
            + lax.dot_general(ones, lo, dims, preferred_element_type=F32))


@jax.custom_vjp
def _cumsum_rows(x):
    return _ones_dot(_lower_ones(x.shape[0]), x, 1)


_cumsum_rows.defvjp(lambda x: (_ones_dot(_lower_ones(x.shape[0]), x, 1), None),
                    lambda _, g: (_ones_dot(_lower_ones(g.shape[0]), g, 0),))


def _wkv_chunk(s0, r, lw, k, v, a, b):
    c = r[0].shape[0]
    row = lax.broadcasted_iota(jnp.int32, (c, c), 0)
    col = lax.broadcasted_iota(jnp.int32, (c, c), 1)
    strict, incl = row > col, row >= col
    cat = lambda p, q: jnp.concatenate([p, q], axis=0)
    cum = _each(_cumsum_rows, lw)
    e_neg = _each(lambda c_: jnp.exp(-c_), cum)
    ar = _each(lambda a_, r_, c_, l_: cat(a_ * jnp.exp(c_ - l_), r_ * jnp.exp(c_)), a, r, cum, lw)
    b_t, k_t = _each(jnp.multiply, b, e_neg), _each(jnp.multiply, k, e_neg)
    p_b, p_k, p_s = _each(NT_X3, ar, b_t), _each(NT_X3, ar, k_t), _each(NT_X3, ar, s0)
    n_ab = _each(lambda p: jnp.where(strict, p[:c], 0.0), p_b)
    m_rb = _each(lambda p: jnp.where(incl, p[c:], 0.0), p_b)
    n_ak = _each(lambda p: jnp.where(strict, p[:c], 0.0), p_k)
    m_rk = _each(lambda p: jnp.where(incl, p[c:], 0.0), p_k)
    u = _tri_solve(n_ab, _each(lambda p, n_, v_: p[:c] + NN_X3(n_, v_), p_s, n_ak, v))
    y = _each(lambda p, mb, u_, mk, v_: p[c:] + NN_X3(mb, u_) + NN_X3(mk, v_), p_s, m_rb, u, m_rk, v)
    g_end = _each(lambda l_: jnp.exp(jnp.sum(l_, axis=0, keepdims=True)), lw)
    s1 = _each(lambda s_, g_, u_, v_, b_, k_: s_ * g_ + TN_X3(cat(u_, v_), cat(b_, k_) * g_),
               s0, g_end, u, v, b_t, k_t)
    return y, s1


def _adamw(w, g, m, v):
    m = ADAM_B1 * m + (1.0 - ADAM_B1) * g
    v = ADAM_B2 * v + (1.0 - ADAM_B2) * jnp.square(g)
    m_hat = m / (1.0 - ADAM_B1 ** ADAM_STEP)
    v_hat = v / (1.0 - ADAM_B2 ** ADAM_STEP)
    delta = -ADAM_LR * (m_hat / (jnp.sqrt(v_hat) + ADAM_EPS) + ADAM_WD * w)
    return delta, m, v


def token_shift_fwd(p, mu, *, tb, name):
    S, W = p.shape
    hb = tb // 8

    def body(p_ref, halo_ref, mu_ref, o_ref):
        i = pl.program_id(0)
        x = p_ref[...]
        before = halo_ref[7:8, :] * (i > 0).astype(F32)
        row = lax.broadcasted_iota(jnp.int32, (tb, W), 0)
        prev = jnp.where(row == 0, before, pltpu.roll(x, 1, 0))
        o_ref[...] = x + (prev - x) * mu_ref[...]

    blk = (2 * tb + 8) * W * 4
    return pl.pallas_call(
        body, name=name, grid=(S // tb,),
        in_specs=[pl.BlockSpec((tb, W), lambda i: (i, 0)),
                  pl.BlockSpec((8, W), lambda i: (jnp.maximum(i * hb - 1, 0), 0)),
                  pl.BlockSpec((1, W), lambda i: (0, 0))],
        out_specs=pl.BlockSpec((tb, W), lambda i: (i, 0)),
        out_shape=jax.ShapeDtypeStruct((S, W), F32),
        compiler_params=_cparams(("parallel",), blk),
    )(p, p, mu)


def token_shift_bwd(dxs, p, mu, *, tb, name):
    S, W = p.shape
    hb, nb = tb // 8, S // tb

    def body(d_ref, dnext_ref, p_ref, halo_ref, mu_ref, dp_ref, dmu_ref):
        i = pl.program_id(0)
        d, x, mu_v = d_ref[...], p_ref[...], mu_ref[...]
        row = lax.broadcasted_iota(jnp.int32, (tb, W), 0)
        before = halo_ref[7:8, :] * (i > 0).astype(F32)
        prev = jnp.where(row == 0, before, pltpu.roll(x, 1, 0))
        t = d * mu_v
        after = dnext_ref[0:1, :] * mu_v * (i < nb - 1).astype(F32)
        nxt = jnp.where(row == tb - 1, after, pltpu.roll(t, tb - 1, 0))
        dp_ref[...] = (d - t + nxt).astype(dp_ref.dtype)

        @pl.when(i == 0)
        def _():
            dmu_ref[...] = jnp.zeros_like(dmu_ref)

        dmu_ref[...] += jnp.sum(d * (prev - x), axis=0, keepdims=True)

    blk = (3 * tb + 16) * W * 4
    return pl.pallas_call(
        body, name=name, grid=(nb,),
        in_specs=[pl.BlockSpec((tb, W), lambda i: (i, 0)),
                  pl.BlockSpec((8, W), lambda i: (jnp.minimum((i + 1) * hb, S // 8 - 1), 0)),
                  pl.BlockSpec((tb, W), lambda i: (i, 0)),
                  pl.BlockSpec((8, W), lambda i: (jnp.maximum(i * hb - 1, 0), 0)),
                  pl.BlockSpec((1, W), lambda i: (0, 0))],
        out_specs=[pl.BlockSpec((tb, W), lambda i: (i, 0)), pl.BlockSpec((1, W), lambda i: (0, 0))],
        out_shape=[jax.ShapeDtypeStruct((S, W), BF16), jax.ShapeDtypeStruct((1, W), F32)],
        compiler_params=_cparams(("arbitrary",), blk),
    )(dxs, dxs, p, p, mu)


def _head_cols(h):
    return pl.ds(h * HEAD_DIM, HEAD_DIM)


def wkv_fwd(xs_rk, lw, k, a, b):
    S = lw.shape[0]
    C, nc, G, N = WKV_CHUNK, S // WKV_CHUNK, WKV_HEADS_PER_STEP, HEAD_DIM

    def body(r_ref, lw_ref, k_ref, v_ref, a_ref, b_ref, y_ref, st_ref, state):
        @pl.when(pl.program_id(1) == 0)
        def _():
            state[...] = jnp.zeros_like(state)

        heads = lambda ref: tuple(ref[:, _head_cols(h)] for h in range(G))
        s0 = tuple(state[h] for h in range(G))
        y, s1 = _wkv_chunk(s0, heads(r_ref), heads(lw_ref), heads(k_ref), heads(v_ref), heads(a_ref),
                           heads(b_ref))
        for h in range(G):
            st_ref[h] = s0[h]
            y_ref[:, _head_cols(h)] = y[h]
            state[h] = s1[h]

    W = G * N
    seq = lambda j: pl.BlockSpec((C, W), functools.partial(lambda j, g, c: (c, j + g), j))
    per = D_MODEL // W
    return pl.pallas_call(
        body, name="wkv_fwd", grid=(RWKV_HEADS // G, nc),
        in_specs=[seq(0), seq(0), seq(0), seq(2 * per), seq(0), seq(0)],
        out_specs=[seq(0), pl.BlockSpec((None, G, N, N), lambda g, c: (c, g, 0, 0))],
        out_shape=[jax.ShapeDtypeStruct((S, D_MODEL), F32), jax.ShapeDtypeStruct((nc, RWKV_HEADS, N, N), F32)],
        scratch_shapes=[pltpu.VMEM((G, N, N), F32)],
        compiler_params=_cparams(("parallel", "arbitrary"), 8 * C * W * 4 + 2 * G * N * N * 4),
    )(xs_rk, lw, k, xs_rk, a, b)


def wkv_bwd(xs_rk, lw, k, a, b, states, dy):
    S = lw.shape[0]
    C, nc, G, N = WKV_CHUNK, S // WKV_CHUNK, WKV_HEADS_PER_STEP, HEAD_DIM

    def body(r_ref, lw_ref, k_ref, v_ref, a_ref, b_ref, st_ref, dy_ref,
             dr_ref, dlw_ref, dk_ref, dv_ref, da_ref, db_ref, dstate):
        @pl.when(pl.program_id(1) == 0)
        def _():
            dstate[...] = jnp.zeros_like(dstate)

        heads = lambda ref: tuple(ref[:, _head_cols(h)] for h in range(G))
        _, pull = jax.vjp(_wkv_chunk, tuple(st_ref[h] for h in range(G)), heads(r_ref), heads(lw_ref),
                          heads(k_ref), heads(v_ref), heads(a_ref), heads(b_ref))
        ds0, *grads = pull((heads(dy_ref), tuple(dstate[h] for h in range(G))))
        for h in range(G):
            dstate[h] = ds0[h]
            for ref, grad in zip((dr_ref, dlw_ref, dk_ref, dv_ref, da_ref, db_ref), grads):
                ref[:, _head_cols(h)] = grad[h]

    W = G * N
    seq = lambda j: pl.BlockSpec((C, W), functools.partial(lambda j, g, c: (nc - 1 - c, j + g), j))
    per = D_MODEL // W
    st = pl.BlockSpec((None, G, N, N), lambda g, c: (nc - 1 - c, g, 0, 0))
    return pl.pallas_call(
        body, name="wkv_bwd", grid=(RWKV_HEADS // G, nc),
        in_specs=[seq(0), seq(0), seq(0), seq(2 * per), seq(0), seq(0), st, seq(0)],
        out_specs=[seq(0)] * 6, out_shape=[jax.ShapeDtypeStruct((S, D_MODEL), F32)] * 6,
        scratch_shapes=[pltpu.VMEM((G, N, N), F32)],
        compiler_params=_cparams(("parallel", "arbitrary"), 14 * C * W * 4 + 2 * G * N * N * 4),
    )(xs_rk, lw, k, xs_rk, a, b, states, dy)


def _first_flag(i, seq_len):
    per_group = ATTN_HPG * seq_len // ATTN_BLK
    g = i // per_group
    per_seq = [seq_len // d // ATTN_BLK for _, d in ATTN_PAIRS]
    n = jnp.where(g == 0, per_seq[0], jnp.where(g == 1, per_seq[1], per_seq[2]))
    return (lax.rem(i, n) == 0).astype(F32)


def attn_fwd(q, k, v, q_gain, k_gain, seq_len):
    R, N = q.shape
    nb = R // ATTN_BLK

    def body(q_ref, kc_ref, kp_ref, vc_ref, vp_ref, qg_ref, kg_ref, o_ref, lse_ref):
        first = _first_flag(pl.program_id(0), seq_len)
        o, lse = _attn_block(q_ref[...], kc_ref[...], kp_ref[...], vc_ref[...], vp_ref[...],
                             qg_ref[...], kg_ref[...], first)
        o_ref[...] = o
        lse_ref[...] = lse

    cur = pl.BlockSpec((ATTN_BLK, N), lambda i: (i, 0))
    prv = pl.BlockSpec((ATTN_BLK, N), lambda i: (jnp.maximum(i - 1, 0), 0))
    gain = pl.BlockSpec((1, N), lambda i: (0, 0))
    return pl.pallas_call(
        body, name="attn_fwd", grid=(nb,), in_specs=[cur, cur, prv, cur, prv, gain, gain],
        out_specs=[cur, cur], out_shape=[jax.ShapeDtypeStruct((R, N), F32)] * 2,
        compiler_params=_cparams(("parallel",), 8 * ATTN_BLK * 128 * 4),
    )(q, k, k, v, v, q_gain, k_gain)


def attn_bwd(q, k, v, q_gain, k_gain, do, dlse, seq_len):
    R, N = q.shape
    nb = R // ATTN_BLK

    def body(q_ref, kc_ref, kp_ref, vc_ref, vp_ref, qg_ref, kg_ref, do_ref, dl_ref,
             dq_ref, dk_ref, dv_ref, dqg_ref, dkg_ref, carry_k, carry_v):
        step = pl.program_id(0)
        first = _first_flag(nb - 1 - step, seq_len)

        @pl.when(step == 0)
        def _():
            carry_k[...] = jnp.zeros_like(carry_k)
            carry_v[...] = jnp.zeros_like(carry_v)
            dqg_ref[...] = jnp.zeros_like(dqg_ref)
            dkg_ref[...] = jnp.zeros_like(dkg_ref)

        _, pull = jax.vjp(functools.partial(_attn_block, first=first), q_ref[...], kc_ref[...], kp_ref[...],
                          vc_ref[...], vp_ref[...], qg_ref[...], kg_ref[...])
        dq, dkc, dkp, dvc, dvp, dqg, dkg = pull((do_ref[...], dl_ref[...]))
        dq_ref[...] = dq
        dk_ref[...] = dkc + carry_k[...]
        dv_ref[...] = dvc + carry_v[...]
        carry_k[...] = dkp
        carry_v[...] = dvp
        dqg_ref[...] += dqg
        dkg_ref[...] += dkg

    cur = pl.BlockSpec((ATTN_BLK, N), lambda i: (nb - 1 - i, 0))
    prv = pl.BlockSpec((ATTN_BLK, N), lambda i: (jnp.maximum(nb - 2 - i, 0), 0))
    gain = pl.BlockSpec((1, N), lambda i: (0, 0))
    return pl.pallas_call(
        body, name="attn_bwd", grid=(nb,), in_specs=[cur, cur, prv, cur, prv, gain, gain, cur, cur],
        out_specs=[cur, cur, cur, gain, gain],
        out_shape=[jax.ShapeDtypeStruct((R, N), F32)] * 3 + [jax.ShapeDtypeStruct((1, N), F32)] * 2,
        scratch_shapes=[pltpu.VMEM((ATTN_BLK, N), F32)] * 2,
        compiler_params=_cparams(("arbitrary",), 16 * ATTN_BLK * 128 * 4),
    )(q, k, k, v, v, q_gain, k_gain, do, dlse)


def to_subsequences(t):
    S = t.shape[0]
    parts = []
    for gi, (_, d) in enumerate(ATTN_PAIRS):
        tg = t[:, GROUP_W * gi:GROUP_W * (gi + 1)].reshape(S // d, d, ATTN_HPG, HEAD_DIM)
        parts.append(tg.transpose(1, 2, 0, 3).reshape(ATTN_HPG * S, HEAD_DIM))
    return jnp.concatenate(parts, axis=0)


def from_subsequences(u, S):
    parts = []
    for gi, (_, d) in enumerate(ATTN_PAIRS):
        ug = u[ATTN_HPG * S * gi:ATTN_HPG * S * (gi + 1)].reshape(d, ATTN_HPG, S // d, HEAD_DIM)
        parts.append(ug.transpose(2, 0, 1, 3).reshape(S, GROUP_W))
    return jnp.concatenate(parts, axis=1)


def _ffn_fwd(x, norm, w_in, w_out, tag):
    h = rowmap(_rms, [x], [norm], [(D_MODEL, BF16)], tb=512, name=tag + "_norm")[0]
    gu = matmul_cs(h, w_in, "nn", tag + "_in")
    act = rowmap(_swiglu_act, [gu], [], [(D_FF, BF16)], tb=256, name=tag + "_act")[0]
    y = matmul(act, w_out, "nn", tag + "_out", add=x, scale=0.5)
    return y, (x, h, gu, act)


def _ffn_bwd(dy, saved, norm, w_in, w_out, tag):
    x, h, gu, act = saved
    dact = matmul(dy, w_out, "nt", tag + "_dact", scale=0.5)
    dw_out = matmul(act, dy, "tn", tag + "_dwout", scale=0.5)

    def act_bwd(gu_b, dact_b):
        return jax.vjp(_swiglu_act, gu_b)[1](dact_b)[0]

    dgu = rowmap(act_bwd, [gu, dact], [], [(2 * D_FF, BF16)], tb=256, name=tag + "_dgu")[0]
    dh = matmul_cs(dgu, w_in, "nt", tag + "_dh")
    dw_in = matmul_cs(h, dgu, "tn", tag + "_dwin")

    def norm_bwd(x_b, dh_b, dy_b, g):
        dx, dg = jax.vjp(_rms, x_b, g)[1](dh_b)
        return dy_b + dx, dg

    dx, dnorm = rowmap(norm_bwd, [x, dh, dy], [norm], [(D_MODEL, F32)], [(1, D_MODEL)], tb=256,
                       name=tag + "_dnorm")
    return dx, dnorm, dw_in, dw_out


def layer_step(x, tgt, W, P):
    S = x.shape[0]
    seg = (jnp.arange(D_MODEL)[:, None] // HEAD_DIM == jnp.arange(RWKV_HEADS)[None, :]).astype(F32)
    seg_t = seg.T
    w_rkv, w_lora = W["w_in"][:, :RKV], W["w_in"][:, RKV:RKV + LORA]
    w_qkv = W["w_in"][:, RKV + LORA:RKV + LORA + 3 * ATTN_WIDTH]
    w_gate = W["w_in"][:, RKV + LORA + 3 * ATTN_WIDTH:]
    mu_rk, mu_lo = P["rwkv_mu"][:, :RKV], P["rwkv_mu"][:, RKV:]
    zeros = lambda n: jnp.zeros((n, D_MODEL), F32)
    w2p = jnp.concatenate([W["rwkv_w2"], zeros(LORA - LORA_W)], axis=0)
    a2p = jnp.concatenate([zeros(LORA_W), W["rwkv_a2"], zeros(LORA_G)], axis=0)
    g2p = jnp.concatenate([zeros(LORA_W + LORA_A), W["rwkv_g2"]], axis=0)
    pre_params = [P["rwkv_w0"], w2p, P["rwkv_a0"], a2p, g2p, P["rwkv_k_k"], P["rwkv_k_a"], seg, seg_t]
    post_params = [P["rwkv_r_k"], P["rwkv_ln_w"], P["rwkv_ln_b"], seg, seg_t]
    col = lambda arr, j: (arr, D_MODEL, j)

    x1, ffn1_saved = _ffn_fwd(x, P["ffn1_norm"], W["ffn1_w_in"], W["ffn1_w_out"], "ffn1")
    h = rowmap(_rms, [x1], [P["mix_norm"]], [(D_MODEL, BF16)], tb=512, name="mix_norm")[0]
    p_rk = matmul(h, w_rkv, "nn", "proj_rkv")
    p_lo = matmul(h, w_lora, "nn", "proj_lora")
    p_qkv = matmul(h, w_qkv, "nn", "proj_qkv")
    p_gate = matmul(h, w_gate, "nn", "proj_gate")
    xs_rk = token_shift_fwd(p_rk, mu_rk, tb=256, name="shift_rk")
    xs_lo = token_shift_fwd(p_lo, mu_lo, tb=256, name="shift_lora")
    lw, k_mod, a_neg, b_kk, g = rowmap(
        _rwkv_pre, [xs_rk, xs_lo], pre_params, [(D_MODEL, F32)] * 5, tb=256, name="rwkv_pre")
    wkv, states = wkv_fwd(xs_rk, lw, k_mod, a_neg, b_kk)
    post_rows = [wkv, col(xs_rk, 0), k_mod, col(xs_rk, 2), g]
    y_a = rowmap(_rwkv_post, post_rows, post_params, [(D_MODEL, BF16)], tb=256, name="rwkv_post")[0]

    q_s = to_subsequences(p_qkv[:, :ATTN_WIDTH])
    k_s = to_subsequences(p_qkv[:, ATTN_WIDTH:2 * ATTN_WIDTH])
    v_s = to_subsequences(p_qkv[:, 2 * ATTN_WIDTH:])
    o_s, lse_s = attn_fwd(q_s, k_s, v_s, P["attn_q_norm"], P["attn_k_norm"], S)
    o, lse = from_subsequences(o_s, S), from_subsequences(lse_s, S)
    y_b = rowmap(_group_combine, [o, lse], [], [(ATTN_WIDTH, BF16)], tb=512, name="attn_combine")[0]

    pa = matmul(y_a, W["w_proj_rwkv"], "nn", "proj_a")
    pb = matmul(y_b, W["w_proj_attn"], "nn", "proj_b")
    merged = rowmap(_gate_merge, [p_gate, pa, pb], [P["b_gate"]], [(D_MODEL, BF16)], tb=256, name="merge")[0]
    x2 = matmul(merged, W["w_out"], "nn", "mix_out", add=x1)
    x3, ffn2_saved = _ffn_fwd(x2, P["ffn2_norm"], W["ffn2_w_in"], W["ffn2_w_out"], "ffn2")

    def loss_head(y_b_, t_b):
        err = y_b_ - t_b
        return err * (1.0 / D_MODEL), (0.5 / D_MODEL) * jnp.sum(err * err, axis=0, keepdims=True)

    dx3, loss_cols = rowmap(loss_head, [x3, tgt], [], [(D_MODEL, F32)], [(1, D_MODEL)], tb=512, name="loss")

    gW, gP = {}, {}
    dx2, gP["ffn2_norm"], gW["ffn2_w_in"], gW["ffn2_w_out"] = _ffn_bwd(
        dx3, ffn2_saved, P["ffn2_norm"], W["ffn2_w_in"], W["ffn2_w_out"], "ffn2")

    dmerged = matmul(dx2, W["w_out"], "nt", "d_merged")
    gW["w_out"] = matmul(merged, dx2, "tn", "dw_out")

    def merge_bwd(pg, pa_b, pb_b, dm, bg):
        return jax.vjp(_gate_merge, pg, pa_b, pb_b, bg)[1](dm)

    dp_gate, dpa, dpb, gP["b_gate"] = rowmap(
        merge_bwd, [p_gate, pa, pb, dmerged], [P["b_gate"]],
        [(2 * D_MODEL, BF16), (D_MODEL, BF16), (D_MODEL, BF16)], [(1, 2 * D_MODEL)], tb=256, name="merge_bwd")
    dy_a = matmul(dpa, W["w_proj_rwkv"], "nt", "d_ya")
    gW["w_proj_rwkv"] = matmul(y_a, dpa, "tn", "dw_proj_a")
    dy_b = matmul(dpb, W["w_proj_attn"], "nt", "d_yb")
    gW["w_proj_attn"] = matmul(y_b, dpb, "tn", "dw_proj_b")

    def combine_bwd(o_b, l_b, d_b):
        return jax.vjp(_group_combine, o_b, l_b)[1](d_b)

    do, dlse = rowmap(combine_bwd, [o, lse, dy_b], [], [(ATTN_WIDTH, F32)] * 2, tb=256, name="attn_combine_bwd")
    dq_s, dk_s, dv_s, gP["attn_q_norm"], gP["attn_k_norm"] = attn_bwd(
        q_s, k_s, v_s, P["attn_q_norm"], P["attn_k_norm"], to_subsequences(do), to_subsequences(dlse), S)
    dp_qkv = jnp.concatenate([from_subsequences(t, S) for t in (dq_s, dk_s, dv_s)], axis=1).astype(BF16)

    def post_bwd(wkv_b, r_b, k_b, v_b, g_b, d_b, r_k, ln_w, ln_b, sg, sgt):
        f = lambda *a: _rwkv_post(*a, sg, sgt)
        return jax.vjp(f, wkv_b, r_b, k_b, v_b, g_b, r_k, ln_w, ln_b)[1](d_b)

    dwkv, dr_p, dk_p, dv_p, dg, gP["rwkv_r_k"], gP["rwkv_ln_w"], gP["rwkv_ln_b"] = rowmap(
        post_bwd, post_rows + [dy_a], post_params, [(D_MODEL, F32)] * 5, [(1, D_MODEL)] * 3, tb=128,
        name="rwkv_post_bwd")
    dr_w, dlw, dk_w, dv_w, da_neg, db_kk = wkv_bwd(xs_rk, lw, k_mod, a_neg, b_kk, states, dwkv)

    def pre_bwd(xrk_b, xlo_b, dlw_b, dkw_b, dkp_b, da_b, db_b, dg_b, drp_b, drw_b, dvp_b, dvw_b,
                w0, w2, a0, a2, g2, k_k, k_a, sg, sgt):
        f = lambda *a: _rwkv_pre(*a, sg, sgt)
        pull = jax.vjp(f, xrk_b, xlo_b, w0, w2, a0, a2, g2, k_k, k_a)[1]
        dxrk, dxlo, *dpar = pull((dlw_b, dkw_b + dkp_b, da_b, db_b, dg_b))
        direct = jnp.concatenate([drp_b + drw_b, jnp.zeros_like(drp_b), dvp_b + dvw_b], axis=1)
        return (dxrk + direct, dxlo, *dpar)

    pre_rows = [xs_rk, xs_lo, dlw, dk_w, dk_p, da_neg, db_kk, dg, dr_p, dr_w, dv_p, dv_w]
    dxs_rk, dxs_lo, gP["rwkv_w0"], dw2p, gP["rwkv_a0"], da2p, dg2p, gP["rwkv_k_k"], gP["rwkv_k_a"] = rowmap(
        pre_bwd, pre_rows, pre_params, [(RKV, F32), (LORA, F32)],
        [(1, D_MODEL), (LORA, D_MODEL), (1, D_MODEL), (LORA, D_MODEL), (LORA, D_MODEL), (1, D_MODEL), (1, D_MODEL)],
        tb=128, name="rwkv_pre_bwd")
    gW["rwkv_w2"] = dw2p[:LORA_W]
    gW["rwkv_a2"] = da2p[LORA_W:LORA_W + LORA_A]
    gW["rwkv_g2"] = dg2p[LORA_W + LORA_A:]
    dp_rk, dmu_rk = token_shift_bwd(dxs_rk, p_rk, mu_rk, tb=256, name="shift_rk_bwd")
    dp_lo, dmu_lo = token_shift_bwd(dxs_lo, p_lo, mu_lo, tb=256, name="shift_lora_bwd")
    gP["rwkv_mu"] = jnp.concatenate([dmu_rk, dmu_lo], axis=1)

    dh = matmul(dp_rk, w_rkv, "nt", "dh_rkv")
    dh = matmul(dp_lo, w_lora, "nt", "dh_lora", add=dh)
    dh = matmul(dp_qkv, w_qkv, "nt", "dh_qkv", add=dh)
    dh = matmul(dp_gate, w_gate, "nt", "dh_gate", add=dh)
    gW["w_in"] = jnp.concatenate([
        matmul(h, dp_rk, "tn", "dw_rkv"), matmul(h, dp_lo, "tn", "dw_lora"),
        matmul(h, dp_qkv, "tn", "dw_qkv"), matmul(h, dp_gate, "tn", "dw_gate")], axis=1)

    def norm_bwd(x_b, dh_b, dy_b, gn):
        dx, dgn = jax.vjp(_rms, x_b, gn)[1](dh_b)
        return dy_b + dx, dgn

    dx1, gP["mix_norm"] = rowmap(norm_bwd, [x1, dh, dx2], [P["mix_norm"]], [(D_MODEL, F32)], [(1, D_MODEL)],
                                 tb=256, name="mix_norm_bwd")
    dx, gP["ffn1_norm"], gW["ffn1_w_in"], gW["ffn1_w_out"] = _ffn_bwd(
        dx1, ffn1_saved, P["ffn1_norm"], W["ffn1_w_in"], W["ffn1_w_out"], "ffn1")
    return loss_cols, dx, gW, gP


N_SHARDS = 4
BIG = (("ffn1_w_in", (D_MODEL, 2 * D_FF), 1), ("ffn1_w_out", (D_FF, D_MODEL), 0),
       ("w_in", (D_MODEL, 7712), 1), ("rwkv_w2", (LORA_W, D_MODEL), 1), ("rwkv_a2", (LORA_A, D_MODEL), 1),
       ("rwkv_g2", (LORA_G, D_MODEL), 1), ("w_proj_rwkv", (D_MODEL, D_MODEL), 0),
       ("w_proj_attn", (ATTN_WIDTH, D_MODEL), 1), ("w_out", (D_MODEL, D_MODEL), 0),
       ("ffn2_w_in", (D_MODEL, 2 * D_FF), 1), ("ffn2_w_out", (D_FF, D_MODEL), 0))
SMALL = (("ffn1_norm", 1024), ("mix_norm", 1024), ("b_gate", 2048), ("rwkv_mu", 3360), ("rwkv_w0", 1024),
         ("rwkv_a0", 1024), ("rwkv_k_k", 1024), ("rwkv_k_a", 1024), ("rwkv_r_k", 1024), ("rwkv_ln_w", 1024),
         ("rwkv_ln_b", 1024), ("attn_q_norm", 64), ("attn_k_norm", 64), ("ffn2_norm", 1024))
WEIGHT_ORDER = ("ffn1_norm", "ffn1_w_in", "ffn1_w_out", "mix_norm", "w_in", "b_gate", "rwkv_mu", "rwkv_w0",
                "rwkv_w2", "rwkv_a0", "rwkv_a2", "rwkv_g2", "rwkv_k_k", "rwkv_k_a", "rwkv_r_k", "rwkv_ln_w",
                "rwkv_ln_b", "attn_q_norm", "attn_k_norm", "w_proj_rwkv", "w_proj_attn", "w_out", "ffn2_norm",
                "ffn2_w_in", "ffn2_w_out")


LORA_PARTS = ("rwkv_w2", "rwkv_a2", "rwkv_g2")
BLOCK_MAJOR = ("ffn1_w_in", "ffn2_w_in")
SMALL_USED = D_MODEL + sum(n for _, n in SMALL)
SMALL_W = -(-SMALL_USED // 128) * 128


def _travel():
    out = {}
    for name, shape, axis in BIG:
        if name == LORA_PARTS[0]:
            out["lora"] = ((LORA, D_MODEL), 1)
        elif name not in LORA_PARTS:
            out[name] = (shape, axis)
    return out


def local_blocks(vals):
    out = {n: vals[n] for n in _travel() if n != "lora"}
    out["lora"] = jnp.concatenate([vals[n] for n in LORA_PARTS], axis=0)
    return out


def split_lora(t):
    return {"rwkv_w2": t[:LORA_W], "rwkv_a2": t[LORA_W:LORA_W + LORA_A], "rwkv_g2": t[LORA_W + LORA_A:]}


def blocks_to_full(name, blocks):
    shape, axis = _travel()[name]
    if name in BLOCK_MAJOR:
        return blocks
    if axis == 0:
        return blocks.reshape(shape)
    return blocks.transpose(1, 0, 2).reshape(shape)


def full_to_blocks(name, full):
    shape, axis = _travel()[name]
    if name in BLOCK_MAJOR:
        return full
    if axis == 0:
        return full.reshape(N_SHARDS, shape[0] // N_SHARDS, shape[1])
    return full.reshape(shape[0], N_SHARDS, shape[1] // N_SHARDS).transpose(1, 0, 2)


def pack_small(vals, head):
    parts = [head] + [vals[name].reshape(1, n) for name, n in SMALL]
    parts.append(jnp.zeros((1, SMALL_W - SMALL_USED), F32))
    return jnp.concatenate(parts, axis=1)


def unpack_small(vec, shapes):
    out, off = {}, D_MODEL
    for name, n in SMALL:
        out[name] = vec[:, off:off + n].reshape(shapes[name])
        off += n
    return out


def _place():
    return lax.axis_index("x"), lax.axis_index("y"), lax.axis_index("c")


def _other_chips(x, y):
    return [(1 - x, y), (x, 1 - y), (1 - x, 1 - y)]


def _remote(src, dst, send_sem, recv_sem, device):
    return pltpu.make_async_remote_copy(src_ref=src, dst_ref=dst, send_sem=send_sem, recv_sem=recv_sem,
                                        device_id=device, device_id_type=MESH)


def _half(ref, who):
    hr = ref.shape[-2] // 2
    rows = pl.ds(pl.multiple_of(who * hr, 8), hr)
    return ref.at[rows] if len(ref.shape) == 2 else ref.at[:, rows]


HBM_REF = pl.BlockSpec(memory_space=pl.ANY)
COMM_PARAMS = dict(compiler_params=pltpu.CompilerParams(has_side_effects=True))


def gather_weights(blocks):
    n = len(blocks)

    def body(*refs):
        ins, outs = refs[:n], refs[n:2 * n]
        ici_send, ici_recv, d2d_send, d2d_recv = refs[2 * n:]
        x, y, c = _place()
        me, sibling, chips = 2 * x + y, (x, y, 1 - c), _other_chips(x, y)
        first = [_remote(_half(ins[t], c), _half(outs[t].at[me], c), ici_send.at[k, t], ici_recv.at[k, t],
                         (px, py, c)) for k, (px, py) in enumerate(chips) for t in range(n)]
        for cp in first:
            cp.start()
        passed = []
        for k, (px, py) in enumerate(chips):
            for t in range(n):
                landed = _half(outs[t].at[2 * px + py], c)
                _remote(landed, landed, ici_send.at[k, t], ici_recv.at[k, t], (px, py, c)).wait_recv()
                cp = _remote(landed, landed, d2d_send.at[k, t], d2d_recv.at[k, t], sibling)
                cp.start()
                passed.append(cp)
        for k, (px, py) in enumerate(chips):
            for t in range(n):
                other = _half(outs[t].at[2 * px + py], 1 - c)
                _remote(other, other, d2d_send.at[k, t], d2d_recv.at[k, t], sibling).wait_recv()
        for cp in first + passed:
            cp.wait_send()

    res = pl.pallas_call(
        body, name="gather_weights", in_specs=[HBM_REF] * n, out_specs=[HBM_REF] * n,
        out_shape=[jax.ShapeDtypeStruct((N_SHARDS,) + b.shape, b.dtype) for b in blocks],
        scratch_shapes=[pltpu.SemaphoreType.DMA((3, n))] * 4, **COMM_PARAMS)(*blocks)
    me = 2 * lax.axis_index("x") + lax.axis_index("y")
    return [lax.dynamic_update_slice(g, b[None], (me, 0, 0)) for g, b in zip(res, blocks)]


def swap_halves(grads):
    n = len(grads)

    def body(*refs):
        ins, got = refs[:n], refs[n:2 * n]
        send_sems, recv_sems = refs[2 * n:]
        x, y, c = _place()
        give = [_remote(_half(ins[t], 1 - c), got[t], send_sems.at[t], recv_sems.at[t], (x, y, 1 - c))
                for t in range(n)]
        for cp in give:
            cp.start()
        for cp in give:
            cp.wait_recv()
        for cp in give:
            cp.wait_send()

    return pl.pallas_call(
        body, name="swap_halves", in_specs=[HBM_REF] * n, out_specs=[HBM_REF] * n,
        out_shape=[jax.ShapeDtypeStruct((g.shape[0], g.shape[1] // 2, g.shape[2]), g.dtype) for g in grads],
        scratch_shapes=[pltpu.SemaphoreType.DMA((n,))] * 2, **COMM_PARAMS)(*grads)


def scatter_partials(partials):
    n = len(partials)

    def body(*refs):
        parts, landed = refs[:n], refs[n:2 * n]
        send_sems, recv_sems = refs[2 * n:]
        x, y, c = _place()
        sends = [_remote(parts[t].at[2 * px + py], landed[t].at[k], send_sems.at[k, t], recv_sems.at[k, t],
                         (px, py, c)) for k, (px, py) in enumerate(_other_chips(x, y)) for t in range(n)]
        for cp in sends:
            cp.start()
        for cp in sends:
            cp.wait_recv()
        for cp in sends:
            cp.wait_send()

    return pl.pallas_call(
        body, name="scatter_partials", in_specs=[HBM_REF] * n, out_specs=[HBM_REF] * n,
        out_shape=[jax.ShapeDtypeStruct((3,) + p.shape[1:], p.dtype) for p in partials],
        scratch_shapes=[pltpu.SemaphoreType.DMA((3, n))] * 2, **COMM_PARAMS)(*partials)


def join_halves(blocks):
    n = len(blocks)

    def body(*refs):
        outs = refs[n:2 * n]
        send_sems, recv_sems = refs[2 * n:]
        x, y, c = _place()
        give = [_remote(_half(outs[t], c), _half(outs[t], c), send_sems.at[t], recv_sems.at[t], (x, y, 1 - c))
                for t in range(n)]
        for cp in give:
            cp.start()
        for t in range(n):
            arriving = _half(outs[t], 1 - c)
            _remote(arriving, arriving, send_sems.at[t], recv_sems.at[t], (x, y, 1 - c)).wait_recv()
        for cp in give:
            cp.wait_send()

    return pl.pallas_call(
        body, name="join_halves", in_specs=[HBM_REF] * n, out_specs=[HBM_REF] * n,
        out_shape=[jax.ShapeDtypeStruct(b.shape, b.dtype) for b in blocks],
        input_output_aliases={t: t for t in range(n)},
        scratch_shapes=[pltpu.SemaphoreType.DMA((n,))] * 2, **COMM_PARAMS)(*blocks)


def reduce_block_grads(grads):
    names = list(grads)
    got = swap_halves([grads[n] for n in names])
    partials = []
    for name, theirs in zip(names, got):
        n_slot, hr, width = theirs.shape
        tb = _row_block(hr, width, 6)
        per_half = hr // tb
        mine = lambda i, s, per_half=per_half: (i // per_half) * 2 * per_half + s[0] * per_half + i % per_half
        p = placed_map(
            jnp.add,
            [(grads[name].reshape(2 * n_slot * hr, width), mine), (theirs.reshape(n_slot * hr, width), lambda i, s: i)],
            (n_slot * hr, width, BF16, lambda i, s: i), n_blocks=n_slot * per_half, tb=tb, name="chip_sum_" + name)
        partials.append(p.reshape(theirs.shape))
    landed = scatter_partials(partials)
    blocks = []
    for name, theirs, arrived in zip(names, got, landed):
        n_slot, hr, width = theirs.shape
        tb = _row_block(hr, width, 6)
        per_half = hr // tb
        views = [(grads[name].reshape(2 * n_slot * hr, width),
                  lambda i, s, per_half=per_half: s[1] * 2 * per_half + s[0] * per_half + i),
                 (theirs.reshape(n_slot * hr, width), lambda i, s, per_half=per_half: s[1] * per_half + i)]
        views += [(arrived.reshape(3 * hr, width), functools.partial(lambda k, per_half, i, s: k * per_half + i,
                                                                     k, per_half)) for k in range(3)]
        f = lambda a, b, l0, l1, l2: (((a + b) + l0.astype(F32)) + l1.astype(F32)) + l2.astype(F32)
        blocks.append(placed_map(
            f, views,(2 * hr, width, F32, lambda i, s, per_half=per_half: s[0] * per_half + i),
            n_blocks=per_half, tb=tb, name="owner_sum_" + name))
    return dict(zip(names, join_halves(blocks)))


def adamw_block(name, w, g, m, v):
    rows, width = w.shape
    return rowmap(_adamw, [w, g, m, v], [], [(width, F32)] * 3, tb=_row_block(rows, width, 7),
                  name="adamw_" + name)


def reduce_small(vec, w, m, v):
    n_dev = 8

    def body(vec_ref, w_ref, m_ref, v_ref, loss_ref, g_ref, d_ref, m2_ref, v2_ref, slots, send_sems, recv_sems):
        x, y, c = _place()
        me = 4 * x + 2 * y + c
        slots[me] = vec_ref[...]
        flips = [(fx, fy, fc) for fx in (0, 1) for fy in (0, 1) for fc in (0, 1)][1:]
        peers = [(1 - x if fx else x, 1 - y if fy else y, 1 - c if fc else c) for fx, fy, fc in flips]
        sends = [pltpu.make_async_remote_copy(
            src_ref=vec_ref, dst_ref=slots.at[me], send_sem=send_sems.at[j], recv_sem=recv_sems.at[j],
            device_id=peer, device_id_type=MESH) for j, peer in enumerate(peers)]
        for cp in sends:
            cp.start()
        for j, (px, py, pc) in enumerate(peers):
            pltpu.make_async_remote_copy(
                src_ref=vec_ref, dst_ref=slots.at[4 * px + 2 * py + pc], send_sem=send_sems.at[j],
                recv_sem=recv_sems.at[j], device_id=(px, py, pc), device_id_type=MESH).wait_recv()
        for cp in sends:
            cp.wait_send()
        g = slots[0]
        for d in range(1, n_dev):
            g = g + slots[d]
        loss_ref[...] = jnp.sum(g[:, :D_MODEL], axis=1, keepdims=True)
        delta, m2, v2 = _adamw(w_ref[...], g, m_ref[...], v_ref[...])
        g_ref[...], d_ref[...], m2_ref[...], v2_ref[...] = g, delta, m2, v2

    vm = pl.BlockSpec(memory_space=pltpu.VMEM)
    vec_t = jax.ShapeDtypeStruct(vec.shape, F32)
    return pl.pallas_call(
        body, name="reduce_small", in_specs=[vm] * 4, out_specs=[vm] * 5,
        out_shape=[jax.ShapeDtypeStruct((1, 1), F32)] + [vec_t] * 4,
        scratch_shapes=[pltpu.VMEM((n_dev,) + vec.shape, F32), pltpu.SemaphoreType.DMA((n_dev - 1,)),
                        pltpu.SemaphoreType.DMA((n_dev - 1,))],
        compiler_params=pltpu.CompilerParams(has_side_effects=True),
    )(vec, w, m, v)


def kernel(x, ffn1_norm, ffn1_w_in, ffn1_w_out, mix_norm, w_in, b_gate, rwkv_mu, rwkv_w0, rwkv_w2, rwkv_a0, rwkv_a2, rwkv_g2, rwkv_k_k, rwkv_k_a, rwkv_r_k, rwkv_ln_w, rwkv_ln_b, attn_q_norm, attn_k_norm, w_proj_rwkv, w_proj_attn, w_out, ffn2_norm, ffn2_w_in, ffn2_w_out, loss_target, m_ffn1_norm, m_ffn1_w_in, m_ffn1_w_out, m_mix_norm, m_w_in, m_b_gate, m_rwkv_mu, m_rwkv_w0, m_rwkv_w2, m_rwkv_a0, m_rwkv_a2, m_rwkv_g2, m_rwkv_k_k, m_rwkv_k_a, m_rwkv_r_k, m_rwkv_ln_w, m_rwkv_ln_b, m_attn_q_norm, m_attn_k_norm, m_w_proj_rwkv, m_w_proj_attn, m_w_out, m_ffn2_norm, m_ffn2_w_in, m_ffn2_w_out, v_ffn1_norm, v_ffn1_w_in, v_ffn1_w_out, v_mix_norm, v_w_in, v_b_gate, v_rwkv_mu, v_rwkv_w0, v_rwkv_w2, v_rwkv_a0, v_rwkv_a2, v_rwkv_g2, v_rwkv_k_k, v_rwkv_k_a, v_rwkv_r_k, v_rwkv_ln_w, v_rwkv_ln_b, v_attn_q_norm, v_attn_k_norm, v_w_proj_rwkv, v_w_proj_attn, v_w_out, v_ffn2_norm, v_ffn2_w_in, v_ffn2_w_out):
    given = dict(locals())
    weights = {n: given[n] for n in WEIGHT_ORDER}
    mom_m = {n: given["m_" + n] for n in WEIGHT_ORDER}
    mom_v = {n: given["v_" + n] for n in WEIGHT_ORDER}
    big = [name for name, _, _ in BIG]
    shapes = {n: weights[n].shape for n in WEIGHT_ORDER}
    blocks_of = lambda d: local_blocks({n: d[n][0] for n in big})
    w_blk, m_blk, v_blk = blocks_of(weights), blocks_of(mom_m), blocks_of(mom_v)
    names = list(w_blk)

    gathered = gather_weights([w_blk[n].astype(BF16) for n in names])
    W = {n: blocks_to_full(n, g) for n, g in zip(names, gathered)}
    W.update(split_lora(W.pop("lora")))
    P = {n: weights[n].reshape(1, -1) for n, _ in SMALL}

    loss_cols, dx, gW, gP = layer_step(x[0], loss_target[0], W, P)

    gW["lora"] = jnp.concatenate([gW.pop(n) for n in LORA_PARTS], axis=0)
    g_blk = reduce_block_grads({n: full_to_blocks(n, gW[n]) for n in names})
    out_g, out_d, out_m, out_v = {}, {}, {}, {}
    for n in names:
        res = (g_blk[n], *adamw_block(n, w_blk[n], g_blk[n], m_blk[n], v_blk[n]))
        for dst, t in zip((out_g, out_d, out_m, out_v), res):
            for part, val in (split_lora(t) if n == "lora" else {n: t}).items():
                dst[part] = val.reshape(shapes[part])

    zero_head = jnp.zeros((1, D_MODEL), F32)
    vec = pack_small(gP, loss_cols)
    loss, g_s, d_s, m_s, v_s = reduce_small(
        vec, pack_small({n: weights[n] for n, _ in SMALL}, zero_head),
        pack_small({n: mom_m[n] for n, _ in SMALL}, zero_head),
        pack_small({n: mom_v[n] for n, _ in SMALL}, zero_head))
    for dst, src in ((out_g, g_s), (out_d, d_s), (out_m, m_s), (out_v, v_s)):
        dst.update(unpack_small(src, shapes))

    return (loss[0, 0], dx[None], *[out_g[n] for n in WEIGHT_ORDER], *[out_d[n] for n in WEIGHT_ORDER],
            *[out_m[n] for n in WEIGHT_ORDER], *[out_v[n] for n in WEIGHT_ORDER])
```

```python
import functools

import jax
import jax.numpy as jnp
from jax import lax
from jax.experimental import pallas as pl
from jax.experimental.pallas import tpu as pltpu

F32 = jnp.float32
BF16 = jnp.bfloat16
HI = lax.Precision.HIGHEST
MESH = pl.DeviceIdType.MESH

D_MODEL = 1024
HEAD_DIM = 64
RWKV_HEADS = 16
LORA_W, LORA_A, LORA_G = 64, 64, 160
LORA = LORA_W + LORA_A + LORA_G
RKV = 3 * D_MODEL
ATTN_PAIRS = ((128, 1), (512, 4), (2048, 16))
ATTN_BLK = 128
ATTN_HPG = 4
ATTN_WIDTH = 768
GROUP_W = ATTN_HPG * HEAD_DIM
D_FF = 2816
GN_EPS = 64e-5
RMS_EPS = 1e-6
NEG_INF = -1e30
WKV_CHUNK = 64
WKV_HEADS_PER_STEP = 8

ADAM_LR, ADAM_B1, ADAM_B2, ADAM_EPS, ADAM_WD, ADAM_STEP = 0.001, 0.9, 0.999, 1e-08, 0.01, 10

V7X_VMEM_BYTES = 64 << 20
VMEM_TEMP_ALLOWANCE = 20 << 20


def _cparams(sem, block_bytes):
    limit = min(2 * block_bytes + VMEM_TEMP_ALLOWANCE, V7X_VMEM_BYTES - (6 << 20))
    return pltpu.CompilerParams(dimension_semantics=sem, vmem_limit_bytes=int(limit))


def _nbytes(shape, dtype):
    n = 1
    for s in shape:
        n *= s
    return n * jnp.dtype(dtype).itemsize


def _split_bf16(a):
    hi = a.astype(BF16)
    return hi, (a - hi.astype(F32)).astype(BF16)


def _make_dots(prec):
    def one(a, b, ca, cb):
        return lax.dot_general(a, b, (((ca,), (cb,)), ((), ())), precision=None if prec == "x3" else prec,
                               preferred_element_type=F32)

    def raw(a, b, ca, cb):
        if prec != "x3":
            return one(a, b, ca, cb)
        (ah, al), (bh, bl) = _split_bf16(a), _split_bf16(b)
        return one(ah, bh, ca, cb) + (one(al, bh, ca, cb) + one(ah, bl, ca, cb))

    @jax.custom_vjp
    def nn(a, b):
        return raw(a, b, 1, 0)

    @jax.custom_vjp
    def nt(a, b):
        return raw(a, b, 1, 1)

    @jax.custom_vjp
    def tn(a, b):
        return raw(a, b, 0, 0)

    nn.defvjp(lambda a, b: (raw(a, b, 1, 0), (a, b)),
              lambda res, g: (raw(g, res[1], 1, 1), raw(res[0], g, 0, 0)))
    nt.defvjp(lambda a, b: (raw(a, b, 1, 1), (a, b)),
              lambda res, g: (raw(g, res[1], 1, 0), raw(g, res[0], 0, 0)))
    tn.defvjp(lambda a, b: (raw(a, b, 0, 0), (a, b)),
              lambda res, g: (raw(res[1], g, 1, 1), raw(res[0], g, 1, 0)))
    return nn, nt, tn


NN, NT, TN = _make_dots(None)
NN_HI, NT_HI, TN_HI = _make_dots(HI)
NN_X3, NT_X3, TN_X3 = _make_dots("x3")


def _pick(n, cap):
    best = None
    for t in range(128, min(n, cap) + 1, 128):
        if n % t == 0:
            best = t
    return best or n


def matmul(a, b, mode, name, *, add=None, scale=1.0, out_dtype=F32):
    if mode == "nn":
        (M, K), (K2, N) = a.shape, b.shape
    elif mode == "nt":
        (M, K), (N, K2) = a.shape, b.shape
    else:
        (K, M), (K2, N) = a.shape, b.shape
    assert K == K2, (name, a.shape, b.shape)
    tm, tn, tk = _pick(M, 512), _pick(N, 512), _pick(K, 1024)
    nk = K // tk
    ca, cb = {"nn": (1, 0), "nt": (1, 1), "tn": (0, 0)}[mode]

    def body(*refs):
        if add is None:
            a_ref, b_ref, o_ref, acc_ref = refs
        else:
            a_ref, b_ref, add_ref, o_ref, acc_ref = refs
        k = pl.program_id(2)

        @pl.when(k == 0)
        def _():
            acc_ref[...] = jnp.zeros_like(acc_ref)

        acc_ref[...] += lax.dot_general(a_ref[...].astype(BF16), b_ref[...].astype(BF16),
                                        (((ca,), (cb,)), ((), ())), preferred_element_type=F32)

        @pl.when(k == nk - 1)
        def _():
            r = acc_ref[...] * scale
            if add is not None:
                r = add_ref[...] + r
            o_ref[...] = r.astype(o_ref.dtype)

    a_spec = (pl.BlockSpec((tk, tm), lambda i, j, k: (k, i)) if mode == "tn"
              else pl.BlockSpec((tm, tk), lambda i, j, k: (i, k)))
    b_spec = (pl.BlockSpec((tn, tk), lambda i, j, k: (j, k)) if mode == "nt"
              else pl.BlockSpec((tk, tn), lambda i, j, k: (k, j)))
    in_specs, args = [a_spec, b_spec], [a, b]
    blk = tm * tk * a.dtype.itemsize + tk * tn * b.dtype.itemsize + tm * tn * 8
    if add is not None:
        in_specs.append(pl.BlockSpec((tm, tn), lambda i, j, k: (i, j)))
        args.append(add)
        blk += tm * tn * 4
    return pl.pallas_call(
        body, name=name, grid=(M // tm, N // tn, nk),
        in_specs=in_specs, out_specs=pl.BlockSpec((tm, tn), lambda i, j, k: (i, j)),
        out_shape=jax.ShapeDtypeStruct((M, N), out_dtype),
        scratch_shapes=[pltpu.VMEM((tm, tn), F32)],
        compiler_params=_cparams(("parallel", "parallel", "arbitrary"), blk),
    )(*args)


def matmul_cs(a, w, mode, name, *, scale=1.0, out_dtype=F32):
    n_blk = N_SHARDS
    if mode == "tn":
        (K, R), Cs = a.shape, w.shape[1] // n_blk
        tm, tk = _pick(R, 512), _pick(K, 1024)
        grid = (R // tm, n_blk, K // tk)
        a_spec = pl.BlockSpec((tk, tm), lambda i, j, k: (k, i))
        w_spec = pl.BlockSpec((tk, Cs), lambda i, j, k: (k, j))
        o_spec = pl.BlockSpec((None, tm, Cs), lambda i, j, k: (j, i, 0))
        out_shape, acc_shape, dims = (n_blk, R, Cs), (tm, Cs), (0, 0)
        blk = tk * tm * a.dtype.itemsize + tk * Cs * w.dtype.itemsize + tm * Cs * 8
    elif mode == "nn":
        (M, R), Cs = a.shape, w.shape[2]
        tm, tk = _pick(M, 512), _pick(R, 1024)
        grid = (M // tm, n_blk, R // tk)
        a_spec = pl.BlockSpec((tm, tk), lambda i, j, k: (i, k))
        w_spec = pl.BlockSpec((None, tk, Cs), lambda i, j, k: (j, k, 0))
        o_spec = pl.BlockSpec((tm, Cs), lambda i, j, k: (i, j))
        out_shape, acc_shape, dims = (M, n_blk * Cs), (tm, Cs), (1, 0)
        blk = tm * tk * a.dtype.itemsize + tk * Cs * w.dtype.itemsize + tm * Cs * 8
    else:
        M, (_, R, Cs) = a.shape[0], w.shape
        tm, tn = _pick(M, 512), _pick(R, 512)
        grid = (M // tm, R // tn, n_blk)
        a_spec = pl.BlockSpec((tm, Cs), lambda i, j, k: (i, k))
        w_spec = pl.BlockSpec((None, tn, Cs), lambda i, j, k: (k, j, 0))
        o_spec = pl.BlockSpec((tm, tn), lambda i, j, k: (i, j))
        out_shape, acc_shape, dims = (M, R), (tm, tn), (1, 1)
        blk = tm * Cs * a.dtype.itemsize + tn * Cs * w.dtype.itemsize + tm * tn * 8
    nk = grid[2]

    def body(a_ref, w_ref, o_ref, acc_ref):
        k = pl.program_id(2)

        @pl.when(k == 0)
        def _():
            acc_ref[...] = jnp.zeros_like(acc_ref)

        acc_ref[...] += lax.dot_general(a_ref[...].astype(BF16), w_ref[...].astype(BF16),
                                        (((dims[0],), (dims[1],)), ((), ())), preferred_element_type=F32)

        @pl.when(k == nk - 1)
        def _():
            o_ref[...] = (acc_ref[...] * scale).astype(o_ref.dtype)

    return pl.pallas_call(
        body, name=name, grid=grid, in_specs=[a_spec, w_spec], out_specs=o_spec,
        out_shape=jax.ShapeDtypeStruct(out_shape, out_dtype), scratch_shapes=[pltpu.VMEM(acc_shape, F32)],
        compiler_params=_cparams(("parallel", "parallel", "arbitrary"), blk),
    )(a, w)


def _row_block(n, width, n_arrays):
    cap = (V7X_VMEM_BYTES // 4) // (2 * 4 * width * n_arrays)
    best = None
    for t in range(16, min(n, cap) + 1, 16):
        if n % t == 0:
            best = t
    return best or n


def placed_map(f, ins, out, *, n_blocks, tb, name):
    def body(*refs):
        refs[-1][...] = f(*[r[...] for r in refs[:-1]]).astype(refs[-1].dtype)

    def spec(fn):
        def index(i):
            x, y, c = _place()
            return fn(i, (c, 2 * x + y)), 0
        return pl.BlockSpec((tb, width), index)

    o_rows, width, o_dtype, o_fn = out
    blk = (sum(a.dtype.itemsize for a, _ in ins) + jnp.dtype(o_dtype).itemsize) * tb * width
    return pl.pallas_call(
        body, name=name, grid=(n_blocks,), in_specs=[spec(fn) for _, fn in ins], out_specs=spec(o_fn),
        out_shape=jax.ShapeDtypeStruct((o_rows, width), o_dtype),
        compiler_params=_cparams(("parallel",), blk),
    )(*[a for a, _ in ins])


def rowmap(f, rows, params, outs, accs=(), *, tb, name):
    rows = [r if isinstance(r, tuple) else (r, r.shape[1], 0) for r in rows]
    S = rows[0][0].shape[0]
    assert S % tb == 0, (name, S, tb)
    n_in, n_out = len(rows) + len(params), len(outs)

    def body(*refs):
        res = f(*[r[...] for r in refs[:n_in]])
        res = res if isinstance(res, (tuple, list)) else (res,)
        o_refs, a_refs = refs[n_in:n_in + n_out], refs[n_in + n_out:]
        for ref, val in zip(o_refs, res[:n_out]):
            ref[...] = val.astype(ref.dtype)
        if a_refs:
            @pl.when(pl.program_id(0) == 0)
            def _():
                for ref in a_refs:
                    ref[...] = jnp.zeros_like(ref)

            for ref, val in zip(a_refs, res[n_out:]):
                ref[...] += val.astype(F32)

    in_specs = [pl.BlockSpec((tb, w), functools.partial(lambda cb, i: (i, cb), cb)) for _, w, cb in rows]
    in_specs += [pl.BlockSpec(p.shape, lambda i: (0, 0)) for p in params]
    out_specs = [pl.BlockSpec((tb, w), lambda i: (i, 0)) for w, _ in outs]
    out_specs += [pl.BlockSpec(tuple(s), lambda i: (0, 0)) for s in accs]
    out_shape = [jax.ShapeDtypeStruct((S, w), dt) for w, dt in outs]
    out_shape += [jax.ShapeDtypeStruct(tuple(s), F32) for s in accs]
    blk = sum(tb * w * a.dtype.itemsize for a, w, _ in rows) + sum(_nbytes(p.shape, p.dtype) for p in params)
    blk += sum(_nbytes((tb, w), dt) for w, dt in outs) + sum(_nbytes(s, F32) for s in accs)
    res = pl.pallas_call(
        body, name=name, grid=(S // tb,), in_specs=in_specs, out_specs=out_specs, out_shape=out_shape,
        compiler_params=_cparams(("arbitrary",) if accs else ("parallel",), blk),
    )(*[r[0] for r in rows], *params)
    return res


def _rms(x, g):
    return x * lax.rsqrt(jnp.mean(x * x, axis=-1, keepdims=True) + RMS_EPS) * g


def _softplus(z):
    return jnp.maximum(z, 0.0) + jnp.log(1.0 + jnp.exp(-jnp.abs(z)))


def _swiglu_act(gu):
    gate, up = gu[:, :D_FF], gu[:, D_FF:]
    return gate * jax.nn.sigmoid(gate) * up


def _rwkv_pre(xrk, xlo, w0, w2p, a0, a2p, g2p, k_k, k_a, seg, seg_t):
    k = xrk[:, D_MODEL:2 * D_MODEL]
    w = -_softplus(-(w0 + NN(jnp.tanh(xlo), w2p))) - 0.5
    log_decay = -jnp.exp(w)
    a = jax.nn.sigmoid(a0 + NN(xlo, a2p))
    g = NN(jax.nn.sigmoid(xlo), g2p)
    kk = k * k_k
    norm = jnp.maximum(jnp.sqrt(NN_HI(kk * kk, seg)), 1e-12)
    kk = kk / NN_HI(norm, seg_t)
    k_mod = k * (1.0 + (a - 1.0) * k_a)
    return log_decay, k_mod, -kk, kk * a, g


def _rwkv_post(wkv, r, k_mod, v, g, r_k, ln_w, ln_b, seg, seg_t):
    inv_n = 1.0 / HEAD_DIM
    mean = NN_HI(wkv, seg) * inv_n
    cen = wkv - NN_HI(mean, seg_t)
    var = NN_HI(cen * cen, seg) * inv_n
    y = cen * NN_HI(lax.rsqrt(var + GN_EPS), seg_t) * ln_w + ln_b
    bonus = NN_HI(NN_HI(r * k_mod * r_k, seg), seg_t) * v
    return (y + bonus) * g


def _gate_merge(pgate, pa, pb, b_gate):
    sg = jax.nn.sigmoid(pgate + b_gate)
    return sg[:, :D_MODEL] * pa + sg[:, D_MODEL:] * pb


def _group_combine(o, lse):
    ls = [lse[:, GROUP_W * i:GROUP_W * (i + 1)] for i in range(3)]
    m = jnp.maximum(jnp.maximum(ls[0], ls[1]), ls[2])
    es = [jnp.exp(l - m) for l in ls]
    den = es[0] + es[1] + es[2]
    return jnp.concatenate([o[:, GROUP_W * i:GROUP_W * (i + 1)] * (es[i] / den) for i in range(3)], axis=1)


def _each(f, *xs):
    return tuple(f(*args) for args in zip(*xs))


def _attn_block(q, kc, kp, vc, vp, q_gain, k_gain, first):
    qn = _each(lambda t: _rms(t, q_gain) * (HEAD_DIM ** -0.5), q)
    kcn, kpn = _each(lambda t: _rms(t, k_gain), kc), _each(lambda t: _rms(t, k_gain), kp)
    qi = lax.broadcasted_iota(jnp.int32, (ATTN_BLK, ATTN_BLK), 0)
    kj = lax.broadcasted_iota(jnp.int32, (ATTN_BLK, ATTN_BLK), 1)
    own, before = kj <= qi, (kj >= qi) & (first < 0.5)
    s_c = _each(lambda a, b: jnp.where(own, NT(a, b), NEG_INF), qn, kcn)
    s_p = _each(lambda a, b: jnp.where(before, NT(a, b), NEG_INF), qn, kpn)
    row_max = lambda s: jnp.max(s, axis=-1, keepdims=True)
    row_sum = lambda s: jnp.sum(s, axis=-1, keepdims=True)
    m = _each(lambda c_, p_: jnp.maximum(row_max(c_), row_max(p_)), s_c, s_p)
    e_c, e_p = _each(lambda s, m_: jnp.exp(s - m_), s_c, m), _each(lambda s, m_: jnp.exp(s - m_), s_p, m)
    den = _each(lambda c_, p_: row_sum(c_) + row_sum(p_), e_c, e_p)
    o = _each(lambda ec, ep, d_, vc_, vp_: NN(ec / d_, vc_) + NN(ep / d_, vp_), e_c, e_p, den, vc, vp)
    lse = _each(lambda m_, d_: jnp.broadcast_to(m_ + jnp.log(d_), (ATTN_BLK, HEAD_DIM)), m, den)
    return o, lse


def _tri_inverse(n):
    c = n[0].shape[0]
    eye = (lax.broadcasted_iota(jnp.int32, (c, c), 0) == lax.broadcasted_iota(jnp.int32, (c, c), 1)).astype(F32)
    t, p, span = _each(lambda m: eye + m, n), n, 2
    while span < c:
        p = _each(NN_X3, p, p)
        t = _each(lambda t_, p_: t_ + NN_X3(t_, p_), t, p)
        span *= 2
    return t


@jax.custom_vjp
def _tri_solve(n, rhs):
    return _each(NN_X3, _tri_inverse(n), rhs)


def _tri_solve_fwd(n, rhs):
    t = _tri_inverse(n)
    x = _each(NN_X3, t, rhs)
    return x, (t, x)


def _tri_solve_bwd(res, dx):
    t, x = res
    drhs = _each(TN_X3, t, dx)
    return _each(NT_X3, drhs, x), drhs


_tri_solve.defvjp(_tri_solve_fwd, _tri_solve_bwd)


def _lower_ones(c):
    row = lax.broadcasted_iota(jnp.int32, (c, c), 0)
    col = lax.broadcasted_iota(jnp.int32, (c, c), 1)
    return (row >= col).astype(BF16)


def _ones_dot(ones, x, contract):
    hi, lo = _split_bf16(x)
    dims = (((contract,), (0,)), ((), ()))
    return (lax.dot_general(ones, hi, dims, preferred_element_type=F32)
            + lax.dot_general(ones, lo, dims, preferred_element_type=F32))


@jax.custom_vjp
def _cumsum_rows(x):
    return _ones_dot(_lower_ones(x.shape[0]), x, 1)


_cumsum_rows.defvjp(lambda x: (_ones_dot(_lower_ones(x.shape[0]), x, 1), None),
                    lambda _, g: (_ones_dot(_lower_ones(g.shape[0]), g, 0),))


def _wkv_chunk(s0, r, lw, k, v, a, b):
    c = r[0].shape[0]
    row = lax.broadcasted_iota(jnp.int32, (c, c), 0)
    col = lax.broadcasted_iota(jnp.int32, (c, c), 1)
    strict, incl = row > col, row >= col
    cat = lambda p, q: jnp.concatenate([p, q], axis=0)
    cum = _each(_cumsum_rows, lw)
    e_neg = _each(lambda c_: jnp.exp(-c_), cum)
    ar = _each(lambda a_, r_, c_, l_: cat(a_ * jnp.exp(c_ - l_), r_ * jnp.exp(c_)), a, r, cum, lw)
    b_t, k_t = _each(jnp.multiply, b, e_neg), _each(jnp.multiply, k, e_neg)
    p_b, p_k, p_s = _each(NT_X3, ar, b_t), _each(NT_X3, ar, k_t), _each(NT_X3, ar, s0)
    n_ab = _each(lambda p: jnp.where(strict, p[:c], 0.0), p_b)
    m_rb = _each(lambda p: jnp.where(incl, p[c:], 0.0), p_b)
    n_ak = _each(lambda p: jnp.where(strict, p[:c], 0.0), p_k)
    m_rk = _each(lambda p: jnp.where(incl, p[c:], 0.0), p_k)
    u = _tri_solve(n_ab, _each(lambda p, n_, v_: p[:c] + NN_X3(n_, v_), p_s, n_ak, v))
    y = _each(lambda p, mb, u_, mk, v_: p[c:] + NN_X3(mb, u_) + NN_X3(mk, v_), p_s, m_rb, u, m_rk, v)
    g_end = _each(lambda l_: jnp.exp(jnp.sum(l_, axis=0, keepdims=True)), lw)
    s1 = _each(lambda s_, g_, u_, v_, b_, k_: s_ * g_ + TN_X3(cat(u_, v_), cat(b_, k_) * g_),
               s0, g_end, u, v, b_t, k_t)
    return y, s1


def _adamw(w, g, m, v):
    m = ADAM_B1 * m + (1.0 - ADAM_B1) * g
    v = ADAM_B2 * v + (1.0 - ADAM_B2) * jnp.square(g)
    m_hat = m / (1.0 - ADAM_B1 ** ADAM_STEP)
    v_hat = v / (1.0 - ADAM_B2 ** ADAM_STEP)
    delta = -ADAM_LR * (m_hat / (jnp.sqrt(v_hat) + ADAM_EPS) + ADAM_WD * w)
    return delta, m, v


def token_shift_fwd(p, mu, *, tb, name):
    S, W = p.shape
    hb = tb // 8

    def body(p_ref, halo_ref, mu_ref, o_ref):
        i = pl.program_id(0)
        x = p_ref[...]
        before = halo_ref[7:8, :] * (i > 0).astype(F32)
        row = lax.broadcasted_iota(jnp.int32, (tb, W), 0)
        prev = jnp.where(row == 0, before, pltpu.roll(x, 1, 0))
        o_ref[...] = x + (prev - x) * mu_ref[...]

    blk = (2 * tb + 8) * W * 4
    return pl.pallas_call(
        body, name=name, grid=(S // tb,),
        in_specs=[pl.BlockSpec((tb, W), lambda i: (i, 0)),
                  pl.BlockSpec((8, W), lambda i: (jnp.maximum(i * hb - 1, 0), 0)),
                  pl.BlockSpec((1, W), lambda i: (0, 0))],
        out_specs=pl.BlockSpec((tb, W), lambda i: (i, 0)),
        out_shape=jax.ShapeDtypeStruct((S, W), F32),
        compiler_params=_cparams(("parallel",), blk),
    )(p, p, mu)


def token_shift_bwd(dxs, p, mu, *, tb, name):
    S, W = p.shape
    hb, nb = tb // 8, S // tb

    def body(d_ref, dnext_ref, p_ref, halo_ref, mu_ref, dp_ref, dmu_ref):
        i = pl.program_id(0)
        d, x, mu_v = d_ref[...], p_ref[...], mu_ref[...]
        row = lax.broadcasted_iota(jnp.int32, (tb, W), 0)
        before = halo_ref[7:8, :] * (i > 0).astype(F32)
        prev = jnp.where(row == 0, before, pltpu.roll(x, 1, 0))
        t = d * mu_v
        after = dnext_ref[0:1, :] * mu_v * (i < nb - 1).astype(F32)
        nxt = jnp.where(row == tb - 1, after, pltpu.roll(t, tb - 1, 0))
        dp_ref[...] = (d - t + nxt).astype(dp_ref.dtype)

        @pl.when(i == 0)
        def _():
            dmu_ref[...] = jnp.zeros_like(dmu_ref)

        dmu_ref[...] += jnp.sum(d * (prev - x), axis=0, keepdims=True)

    blk = (3 * tb + 16) * W * 4
    return pl.pallas_call(
        body, name=name, grid=(nb,),
        in_specs=[pl.BlockSpec((tb, W), lambda i: (i, 0)),
                  pl.BlockSpec((8, W), lambda i: (jnp.minimum((i + 1) * hb, S // 8 - 1), 0)),
                  pl.BlockSpec((tb, W), lambda i: (i, 0)),
                  pl.BlockSpec((8, W), lambda i: (jnp.maximum(i * hb - 1, 0), 0)),
                  pl.BlockSpec((1, W), lambda i: (0, 0))],
        out_specs=[pl.BlockSpec((tb, W), lambda i: (i, 0)), pl.BlockSpec((1, W), lambda i: (0, 0))],
        out_shape=[jax.ShapeDtypeStruct((S, W), BF16), jax.ShapeDtypeStruct((1, W), F32)],
        compiler_params=_cparams(("arbitrary",), blk),
    )(dxs, dxs, p, p, mu)


def _head_cols(h):
    return pl.ds(h * HEAD_DIM, HEAD_DIM)


def wkv_fwd(xs_rk, lw, k, a, b):
    S = lw.shape[0]
    C, nc, G, N = WKV_CHUNK, S // WKV_CHUNK, WKV_HEADS_PER_STEP, HEAD_DIM

    def body(r_ref, lw_ref, k_ref, v_ref, a_ref, b_ref, y_ref, st_ref, state):
        @pl.when(pl.program_id(1) == 0)
        def _():
            state[...] = jnp.zeros_like(state)

        heads = lambda ref: tuple(ref[:, _head_cols(h)] for h in range(G))
        s0 = tuple(state[h] for h in range(G))
        y, s1 = _wkv_chunk(s0, heads(r_ref), heads(lw_ref), heads(k_ref), heads(v_ref), heads(a_ref),
                           heads(b_ref))
        for h in range(G):
            st_ref[h] = s0[h]
            y_ref[:, _head_cols(h)] = y[h]
            state[h] = s1[h]

    W = G * N
    seq = lambda j: pl.BlockSpec((C, W), functools.partial(lambda j, g, c: (c, j + g), j))
    per = D_MODEL // W
    return pl.pallas_call(
        body, name="wkv_fwd", grid=(RWKV_HEADS // G, nc),
        in_specs=[seq(0), seq(0), seq(0), seq(2 * per), seq(0), seq(0)],
        out_specs=[seq(0), pl.BlockSpec((None, G, N, N), lambda g, c: (c, g, 0, 0))],
        out_shape=[jax.ShapeDtypeStruct((S, D_MODEL), F32), jax.ShapeDtypeStruct((nc, RWKV_HEADS, N, N), F32)],
        scratch_shapes=[pltpu.VMEM((G, N, N), F32)],
        compiler_params=_cparams(("parallel", "arbitrary"), 8 * C * W * 4 + 2 * G * N * N * 4),
    )(xs_rk, lw, k, xs_rk, a, b)


def wkv_bwd(xs_rk, lw, k, a, b, states, dy):
    S = lw.shape[0]
    C, nc, G, N = WKV_CHUNK, S // WKV_CHUNK, WKV_HEADS_PER_STEP, HEAD_DIM

    def body(r_ref, lw_ref, k_ref, v_ref, a_ref, b_ref, st_ref, dy_ref,
             dr_ref, dlw_ref, dk_ref, dv_ref, da_ref, db_ref, dstate):
        @pl.when(pl.program_id(1) == 0)
        def _():
            dstate[...] = jnp.zeros_like(dstate)

        heads = lambda ref: tuple(ref[:, _head_cols(h)] for h in range(G))
        _, pull = jax.vjp(_wkv_chunk, tuple(st_ref[h] for h in range(G)), heads(r_ref), heads(lw_ref),
                          heads(k_ref), heads(v_ref), heads(a_ref), heads(b_ref))
        ds0, *grads = pull((heads(dy_ref), tuple(dstate[h] for h in range(G))))
        for h in range(G):
            dstate[h] = ds0[h]
            for ref, grad in zip((dr_ref, dlw_ref, dk_ref, dv_ref, da_ref, db_ref), grads):
                ref[:, _head_cols(h)] = grad[h]

    W = G * N
    seq = lambda j: pl.BlockSpec((C, W), functools.partial(lambda j, g, c: (nc - 1 - c, j + g), j))
    per = D_MODEL // W
    st = pl.BlockSpec((None, G, N, N), lambda g, c: (nc - 1 - c, g, 0, 0))
    return pl.pallas_call(
        body, name="wkv_bwd", grid=(RWKV_HEADS // G, nc),
        in_specs=[seq(0), seq(0), seq(0), seq(2 * per), seq(0), seq(0), st, seq(0)],
        out_specs=[seq(0)] * 6, out_shape=[jax.ShapeDtypeStruct((S, D_MODEL), F32)] * 6,
        scratch_shapes=[pltpu.VMEM((G, N, N), F32)],
        compiler_params=_cparams(("parallel", "arbitrary"), 14 * C * W * 4 + 2 * G * N * N * 4),
    )(xs_rk, lw, k, xs_rk, a, b, states, dy)


def _first_flag(i, seq_len):
    per_group = seq_len // ATTN_BLK
    g = i // per_group
    per_seq = [seq_len // d // ATTN_BLK for _, d in ATTN_PAIRS]
    n = jnp.where(g == 0, per_seq[0], jnp.where(g == 1, per_seq[1], per_seq[2]))
    return (lax.rem(i, n) == 0).astype(F32)


def attn_fwd(q, k, v, q_gain, k_gain, seq_len):
    R, N = q.shape
    nb = R // ATTN_BLK

    def body(q_ref, kc_ref, kp_ref, vc_ref, vp_ref, qg_ref, kg_ref, o_ref, lse_ref):
        first = _first_flag(pl.program_id(0), seq_len)
        heads = lambda ref: tuple(ref[:, _head_cols(h)] for h in range(ATTN_HPG))
        o, lse = _attn_block(heads(q_ref), heads(kc_ref), heads(kp_ref), heads(vc_ref), heads(vp_ref),
                             qg_ref[...], kg_ref[...], first)
        for h in range(ATTN_HPG):
            o_ref[:, _head_cols(h)] = o[h]
            lse_ref[:, _head_cols(h)] = lse[h]

    cur = pl.BlockSpec((ATTN_BLK, N), lambda i: (i, 0))
    prv = pl.BlockSpec((ATTN_BLK, N), lambda i: (jnp.maximum(i - 1, 0), 0))
    gain = pl.BlockSpec((1, HEAD_DIM), lambda i: (0, 0))
    return pl.pallas_call(
        body, name="attn_fwd", grid=(nb,), in_specs=[cur, cur, prv, cur, prv, gain, gain],
        out_specs=[cur, cur], out_shape=[jax.ShapeDtypeStruct((R, N), F32)] * 2,
        compiler_params=_cparams(("parallel",), 7 * ATTN_BLK * N * 4),
    )(q, k, k, v, v, q_gain, k_gain)


def attn_bwd(q, k, v, q_gain, k_gain, do, dlse, seq_len):
    R, N = q.shape
    nb = R // ATTN_BLK

    def body(q_ref, kc_ref, kp_ref, vc_ref, vp_ref, qg_ref, kg_ref, do_ref, dl_ref,
             dq_ref, dk_ref, dv_ref, dqg_ref, dkg_ref, carry_k, carry_v):
        step = pl.program_id(0)
        first = _first_flag(nb - 1 - step, seq_len)

        @pl.when(step == 0)
        def _():
            carry_k[...] = jnp.zeros_like(carry_k)
            carry_v[...] = jnp.zeros_like(carry_v)
            dqg_ref[...] = jnp.zeros_like(dqg_ref)
            dkg_ref[...] = jnp.zeros_like(dkg_ref)

        heads = lambda ref: tuple(ref[:, _head_cols(h)] for h in range(ATTN_HPG))
        _, pull = jax.vjp(functools.partial(_attn_block, first=first), heads(q_ref), heads(kc_ref), heads(kp_ref),
                          heads(vc_ref), heads(vp_ref), qg_ref[...], kg_ref[...])
        dq, dkc, dkp, dvc, dvp, dqg, dkg = pull((heads(do_ref), heads(dl_ref)))
        old_k, old_v = heads(carry_k), heads(carry_v)
        for h in range(ATTN_HPG):
            cols = _head_cols(h)
            dq_ref[:, cols] = dq[h]
            dk_ref[:, cols] = dkc[h] + old_k[h]
            dv_ref[:, cols] = dvc[h] + old_v[h]
            carry_k[:, cols] = dkp[h]
            carry_v[:, cols] = dvp[h]
        dqg_ref[...] += dqg
        dkg_ref[...] += dkg

    cur = pl.BlockSpec((ATTN_BLK, N), lambda i: (nb - 1 - i, 0))
    prv = pl.BlockSpec((ATTN_BLK, N), lambda i: (jnp.maximum(nb - 2 - i, 0), 0))
    gain = pl.BlockSpec((1, HEAD_DIM), lambda i: (0, 0))
    return pl.pallas_call(
        body, name="attn_bwd", grid=(nb,), in_specs=[cur, cur, prv, cur, prv, gain, gain, cur, cur],
        out_specs=[cur, cur, cur, gain, gain],
        out_shape=[jax.ShapeDtypeStruct((R, N), F32)] * 3 + [jax.ShapeDtypeStruct((1, HEAD_DIM), F32)] * 2,
        scratch_shapes=[pltpu.VMEM((ATTN_BLK, N), F32)] * 2,
        compiler_params=_cparams(("arbitrary",), 12 * ATTN_BLK * N * 4),
    )(q, k, k, v, v, q_gain, k_gain, do, dlse)


def to_subsequences(t):
    S = t.shape[0]
    parts = []
    for gi, (_, d) in enumerate(ATTN_PAIRS):
        tg = t[:, GROUP_W * gi:GROUP_W * (gi + 1)].reshape(S // d, d, GROUP_W)
        parts.append(tg.transpose(1, 0, 2).reshape(S, GROUP_W))
    return jnp.concatenate(parts, axis=0)


def from_subsequences(u, S):
    parts = []
    for gi, (_, d) in enumerate(ATTN_PAIRS):
        ug = u[S * gi:S * (gi + 1)].reshape(d, S // d, GROUP_W)
        parts.append(ug.transpose(1, 0, 2).reshape(S, GROUP_W))
    return jnp.concatenate(parts, axis=1)


def _ffn_fwd(x, norm, w_in, w_out, tag):
    h = rowmap(_rms, [x], [norm], [(D_MODEL, BF16)], tb=512, name=tag + "_norm")[0]
    gu = matmul_cs(h, w_in, "nn", tag + "_in")
    act = rowmap(_swiglu_act, [gu], [], [(D_FF, BF16)], tb=256, name=tag + "_act")[0]
    y = matmul(act, w_out, "nn", tag + "_out", add=x, scale=0.5)
    return y, (x, h, gu, act)


def _ffn_bwd(dy, saved, norm, w_in, w_out, tag):
    x, h, gu, act = saved
    dact = matmul(dy, w_out, "nt", tag + "_dact", scale=0.5)
    dw_out = matmul(act, dy, "tn", tag + "_dwout", scale=0.5)

    def act_bwd(gu_b, dact_b):
        return jax.vjp(_swiglu_act, gu_b)[1](dact_b)[0]

    dgu = rowmap(act_bwd, [gu, dact], [], [(2 * D_FF, BF16)], tb=256, name=tag + "_dgu")[0]
    dh = matmul_cs(dgu, w_in, "nt", tag + "_dh")
    dw_in = matmul_cs(h, dgu, "tn", tag + "_dwin")

    def norm_bwd(x_b, dh_b, dy_b, g):
        dx, dg = jax.vjp(_rms, x_b, g)[1](dh_b)
        return dy_b + dx, dg

    dx, dnorm = rowmap(norm_bwd, [x, dh, dy], [norm], [(D_MODEL, F32)], [(1, D_MODEL)], tb=256,
                       name=tag + "_dnorm")
    return dx, dnorm, dw_in, dw_out


def layer_step(x, tgt, W, P):
    S = x.shape[0]
    seg = (jnp.arange(D_MODEL)[:, None] // HEAD_DIM == jnp.arange(RWKV_HEADS)[None, :]).astype(F32)
    seg_t = seg.T
    w_rkv, w_lora = W["w_in"][:, :RKV], W["w_in"][:, RKV:RKV + LORA]
    w_qkv = W["w_in"][:, RKV + LORA:RKV + LORA + 3 * ATTN_WIDTH]
    w_gate = W["w_in"][:, RKV + LORA + 3 * ATTN_WIDTH:]
    mu_rk, mu_lo = P["rwkv_mu"][:, :RKV], P["rwkv_mu"][:, RKV:]
    zeros = lambda n: jnp.zeros((n, D_MODEL), F32)
    w2p = jnp.concatenate([W["rwkv_w2"], zeros(LORA - LORA_W)], axis=0)
    a2p = jnp.concatenate([zeros(LORA_W), W["rwkv_a2"], zeros(LORA_G)], axis=0)
    g2p = jnp.concatenate([zeros(LORA_W + LORA_A), W["rwkv_g2"]], axis=0)
    pre_params = [P["rwkv_w0"], w2p, P["rwkv_a0"], a2p, g2p, P["rwkv_k_k"], P["rwkv_k_a"], seg, seg_t]
    post_params = [P["rwkv_r_k"], P["rwkv_ln_w"], P["rwkv_ln_b"], seg, seg_t]
    col = lambda arr, j: (arr, D_MODEL, j)

    x1, ffn1_saved = _ffn_fwd(x, P["ffn1_norm"], W["ffn1_w_in"], W["ffn1_w_out"], "ffn1")
    h = rowmap(_rms, [x1], [P["mix_norm"]], [(D_MODEL, BF16)], tb=512, name="mix_norm")[0]
    p_rk = matmul(h, w_rkv, "nn", "proj_rkv")
    p_lo = matmul(h, w_lora, "nn", "proj_lora")
    p_qkv = matmul(h, w_qkv, "nn", "proj_qkv")
    p_gate = matmul(h, w_gate, "nn", "proj_gate")
    xs_rk = token_shift_fwd(p_rk, mu_rk, tb=256, name="shift_rk")
    xs_lo = token_shift_fwd(p_lo, mu_lo, tb=256, name="shift_lora")
    lw, k_mod, a_neg, b_kk, g = rowmap(
        _rwkv_pre, [xs_rk, xs_lo], pre_params, [(D_MODEL, F32)] * 5, tb=256, name="rwkv_pre")
    wkv, states = wkv_fwd(xs_rk, lw, k_mod, a_neg, b_kk)
    post_rows = [wkv, col(xs_rk, 0), k_mod, col(xs_rk, 2), g]
    y_a = rowmap(_rwkv_post, post_rows, post_params, [(D_MODEL, BF16)], tb=256, name="rwkv_post")[0]

    q_s = to_subsequences(p_qkv[:, :ATTN_WIDTH])
    k_s = to_subsequences(p_qkv[:, ATTN_WIDTH:2 * ATTN_WIDTH])
    v_s = to_subsequences(p_qkv[:, 2 * ATTN_WIDTH:])
    o_s, lse_s = attn_fwd(q_s, k_s, v_s, P["attn_q_norm"], P["attn_k_norm"], S)
    o, lse = from_subsequences(o_s, S), from_subsequences(lse_s, S)
    y_b = rowmap(_group_combine, [o, lse], [], [(ATTN_WIDTH, BF16)], tb=512, name="attn_combine")[0]

    pa = matmul(y_a, W["w_proj_rwkv"], "nn", "proj_a")
    pb = matmul(y_b, W["w_proj_attn"], "nn", "proj_b")
    merged = rowmap(_gate_merge, [p_gate, pa, pb], [P["b_gate"]], [(D_MODEL, BF16)], tb=256, name="merge")[0]
    x2 = matmul(merged, W["w_out"], "nn", "mix_out", add=x1)
    x3, ffn2_saved = _ffn_fwd(x2, P["ffn2_norm"], W["ffn2_w_in"], W["ffn2_w_out"], "ffn2")

    def loss_head(y_b_, t_b):
        err = y_b_ - t_b
        return err * (1.0 / D_MODEL), (0.5 / D_MODEL) * jnp.sum(err * err, axis=0, keepdims=True)

    dx3, loss_cols = rowmap(loss_head, [x3, tgt], [], [(D_MODEL, F32)], [(1, D_MODEL)], tb=512, name="loss")

    gW, gP = {}, {}
    dx2, gP["ffn2_norm"], gW["ffn2_w_in"], gW["ffn2_w_out"] = _ffn_bwd(
        dx3, ffn2_saved, P["ffn2_norm"], W["ffn2_w_in"], W["ffn2_w_out"], "ffn2")

    dmerged = matmul(dx2, W["w_out"], "nt", "d_merged")
    gW["w_out"] = matmul(merged, dx2, "tn", "dw_out")

    def merge_bwd(pg, pa_b, pb_b, dm, bg):
        return jax.vjp(_gate_merge, pg, pa_b, pb_b, bg)[1](dm)

    dp_gate, dpa, dpb, gP["b_gate"] = rowmap(
        merge_bwd, [p_gate, pa, pb, dmerged], [P["b_gate"]],
        [(2 * D_MODEL, BF16), (D_MODEL, BF16), (D_MODEL, BF16)], [(1, 2 * D_MODEL)], tb=256, name="merge_bwd")
    dy_a = matmul(dpa, W["w_proj_rwkv"], "nt", "d_ya")
    gW["w_proj_rwkv"] = matmul(y_a, dpa, "tn", "dw_proj_a")
    dy_b = matmul(dpb, W["w_proj_attn"], "nt", "d_yb")
    gW["w_proj_attn"] = matmul(y_b, dpb, "tn", "dw_proj_b")

    def combine_bwd(o_b, l_b, d_b):
        return jax.vjp(_group_combine, o_b, l_b)[1](d_b)

    do, dlse = rowmap(combine_bwd, [o, lse, dy_b], [], [(ATTN_WIDTH, F32)] * 2, tb=256, name="attn_combine_bwd")
    dq_s, dk_s, dv_s, gP["attn_q_norm"], gP["attn_k_norm"] = attn_bwd(
        q_s, k_s, v_s, P["attn_q_norm"], P["attn_k_norm"], to_subsequences(do), to_subsequences(dlse), S)
    dp_qkv = jnp.concatenate([from_subsequences(t, S) for t in (dq_s, dk_s, dv_s)], axis=1).astype(BF16)

    def post_bwd(wkv_b, r_b, k_b, v_b, g_b, d_b, r_k, ln_w, ln_b, sg, sgt):
        f = lambda *a: _rwkv_post(*a, sg, sgt)
        return jax.vjp(f, wkv_b, r_b, k_b, v_b, g_b, r_k, ln_w, ln_b)[1](d_b)

    dwkv, dr_p, dk_p, dv_p, dg, gP["rwkv_r_k"], gP["rwkv_ln_w"], gP["rwkv_ln_b"] = rowmap(
        post_bwd, post_rows + [dy_a], post_params, [(D_MODEL, F32)] * 5, [(1, D_MODEL)] * 3, tb=128,
        name="rwkv_post_bwd")
    dr_w, dlw, dk_w, dv_w, da_neg, db_kk = wkv_bwd(xs_rk, lw, k_mod, a_neg, b_kk, states, dwkv)

    def pre_bwd(xrk_b, xlo_b, dlw_b, dkw_b, dkp_b, da_b, db_b, dg_b, drp_b, drw_b, dvp_b, dvw_b,
                w0, w2, a0, a2, g2, k_k, k_a, sg, sgt):
        f = lambda *a: _rwkv_pre(*a, sg, sgt)
        pull = jax.vjp(f, xrk_b, xlo_b, w0, w2, a0, a2, g2, k_k, k_a)[1]
        dxrk, dxlo, *dpar = pull((dlw_b, dkw_b + dkp_b, da_b, db_b, dg_b))
        direct = jnp.concatenate([drp_b + drw_b, jnp.zeros_like(drp_b), dvp_b + dvw_b], axis=1)
        return (dxrk + direct, dxlo, *dpar)

    pre_rows = [xs_rk, xs_lo, dlw, dk_w, dk_p, da_neg, db_kk, dg, dr_p, dr_w, dv_p, dv_w]
    dxs_rk, dxs_lo, gP["rwkv_w0"], dw2p, gP["rwkv_a0"], da2p, dg2p, gP["rwkv_k_k"], gP["rwkv_k_a"] = rowmap(
        pre_bwd, pre_rows, pre_params, [(RKV, F32), (LORA, F32)],
        [(1, D_MODEL), (LORA, D_MODEL), (1, D_MODEL), (LORA, D_MODEL), (LORA, D_MODEL), (1, D_MODEL), (1, D_MODEL)],
        tb=128, name="rwkv_pre_bwd")
    gW["rwkv_w2"] = dw2p[:LORA_W]
    gW["rwkv_a2"] = da2p[LORA_W:LORA_W + LORA_A]
    gW["rwkv_g2"] = dg2p[LORA_W + LORA_A:]
    dp_rk, dmu_rk = token_shift_bwd(dxs_rk, p_rk, mu_rk, tb=256, name="shift_rk_bwd")
    dp_lo, dmu_lo = token_shift_bwd(dxs_lo, p_lo, mu_lo, tb=256, name="shift_lora_bwd")
    gP["rwkv_mu"] = jnp.concatenate([dmu_rk, dmu_lo], axis=1)

    dh = matmul(dp_rk, w_rkv, "nt", "dh_rkv")
    dh = matmul(dp_lo, w_lora, "nt", "dh_lora", add=dh)
    dh = matmul(dp_qkv, w_qkv, "nt", "dh_qkv", add=dh)
    dh = matmul(dp_gate, w_gate, "nt", "dh_gate", add=dh)
    gW["w_in"] = jnp.concatenate([
        matmul(h, dp_rk, "tn", "dw_rkv"), matmul(h, dp_lo, "tn", "dw_lora"),
        matmul(h, dp_qkv, "tn", "dw_qkv"), matmul(h, dp_gate, "tn", "dw_gate")], axis=1)

    def norm_bwd(x_b, dh_b, dy_b, gn):
        dx, dgn = jax.vjp(_rms, x_b, gn)[1](dh_b)
        return dy_b + dx, dgn

    dx1, gP["mix_norm"] = rowmap(norm_bwd, [x1, dh, dx2], [P["mix_norm"]], [(D_MODEL, F32)], [(1, D_MODEL)],
                                 tb=256, name="mix_norm_bwd")
    dx, gP["ffn1_norm"], gW["ffn1_w_in"], gW["ffn1_w_out"] = _ffn_bwd(
        dx1, ffn1_saved, P["ffn1_norm"], W["ffn1_w_in"], W["ffn1_w_out"], "ffn1")
    return loss_cols, dx, gW, gP


N_SHARDS = 4
BIG = (("ffn1_w_in", (D_MODEL, 2 * D_FF), 1), ("ffn1_w_out", (D_FF, D_MODEL), 0),
       ("w_in", (D_MODEL, 7712), 1), ("rwkv_w2", (LORA_W, D_MODEL), 1), ("rwkv_a2", (LORA_A, D_MODEL), 1),
       ("rwkv_g2", (LORA_G, D_MODEL), 1), ("w_proj_rwkv", (D_MODEL, D_MODEL), 0),
       ("w_proj_attn", (ATTN_WIDTH, D_MODEL), 1), ("w_out", (D_MODEL, D_MODEL), 0),
       ("ffn2_w_in", (D_MODEL, 2 * D_FF), 1), ("ffn2_w_out", (D_FF, D_MODEL), 0))
SMALL = (("ffn1_norm", 1024), ("mix_norm", 1024), ("b_gate", 2048), ("rwkv_mu", 3360), ("rwkv_w0", 1024),
         ("rwkv_a0", 1024), ("rwkv_k_k", 1024), ("rwkv_k_a", 1024), ("rwkv_r_k", 1024), ("rwkv_ln_w", 1024),
         ("rwkv_ln_b", 1024), ("attn_q_norm", 64), ("attn_k_norm", 64), ("ffn2_norm", 1024))
WEIGHT_ORDER = ("ffn1_norm", "ffn1_w_in", "ffn1_w_out", "mix_norm", "w_in", "b_gate", "rwkv_mu", "rwkv_w0",
                "rwkv_w2", "rwkv_a0", "rwkv_a2", "rwkv_g2", "rwkv_k_k", "rwkv_k_a", "rwkv_r_k", "rwkv_ln_w",
                "rwkv_ln_b", "attn_q_norm", "attn_k_norm", "w_proj_rwkv", "w_proj_attn", "w_out", "ffn2_norm",
                "ffn2_w_in", "ffn2_w_out")


LORA_PARTS = ("rwkv_w2", "rwkv_a2", "rwkv_g2")
BLOCK_MAJOR = ("ffn1_w_in", "ffn2_w_in")
SMALL_USED = D_MODEL + sum(n for _, n in SMALL)
SMALL_W = -(-SMALL_USED // 128) * 128


def _travel():
    out = {}
    for name, shape, axis in BIG:
        if name == LORA_PARTS[0]:
            out["lora"] = ((LORA, D_MODEL), 1)
        elif name not in LORA_PARTS:
            out[name] = (shape, axis)
    return out


def local_blocks(vals):
    out = {n: vals[n] for n in _travel() if n != "lora"}
    out["lora"] = jnp.concatenate([vals[n] for n in LORA_PARTS], axis=0)
    return out


def split_lora(t):
    return {"rwkv_w2": t[:LORA_W], "rwkv_a2": t[LORA_W:LORA_W + LORA_A], "rwkv_g2": t[LORA_W + LORA_A:]}


def blocks_to_full(name, blocks):
    shape, axis = _travel()[name]
    if name in BLOCK_MAJOR:
        return blocks
    if axis == 0:
        return blocks.reshape(shape)
    return blocks.transpose(1, 0, 2).reshape(shape)


def full_to_blocks(name, full):
    shape, axis = _travel()[name]
    if name in BLOCK_MAJOR:
        return full
    if axis == 0:
        return full.reshape(N_SHARDS, shape[0] // N_SHARDS, shape[1])
    return full.reshape(shape[0], N_SHARDS, shape[1] // N_SHARDS).transpose(1, 0, 2)


def pack_small(vals, head):
    parts = [head] + [vals[name].reshape(1, n) for name, n in SMALL]
    parts.append(jnp.zeros((1, SMALL_W - SMALL_USED), F32))
    return jnp.concatenate(parts, axis=1)


def unpack_small(vec, shapes):
    out, off = {}, D_MODEL
    for name, n in SMALL:
        out[name] = vec[:, off:off + n].reshape(shapes[name])
        off += n
    return out


def _place():
    return lax.axis_index("x"), lax.axis_index("y"), lax.axis_index("c")


def _other_chips(x, y):
    return [(1 - x, y), (x, 1 - y), (1 - x, 1 - y)]


def _remote(src, dst, send_sem, recv_sem, device):
    return pltpu.make_async_remote_copy(src_ref=src, dst_ref=dst, send_sem=send_sem, recv_sem=recv_sem,
                                        device_id=device, device_id_type=MESH)


def _half(ref, who):
    hr = ref.shape[-2] // 2
    rows = pl.ds(pl.multiple_of(who * hr, 8), hr)
    return ref.at[rows] if len(ref.shape) == 2 else ref.at[:, rows]


HBM_REF = pl.BlockSpec(memory_space=pl.ANY)
COMM_PARAMS = dict(compiler_params=pltpu.CompilerParams(has_side_effects=True))


def gather_weights(blocks):
    n = len(blocks)

    def body(*refs):
        ins, outs = refs[:n], refs[n:2 * n]
        ici_send, ici_recv, d2d_send, d2d_recv = refs[2 * n:]
        x, y, c = _place()
        me, sibling, chips = 2 * x + y, (x, y, 1 - c), _other_chips(x, y)
        first = [_remote(_half(ins[t], c), _half(outs[t].at[me], c), ici_send.at[k, t], ici_recv.at[k, t],
                         (px, py, c)) for k, (px, py) in enumerate(chips) for t in range(n)]
        for cp in first:
            cp.start()
        passed = []
        for k, (px, py) in enumerate(chips):
            for t in range(n):
                landed = _half(outs[t].at[2 * px + py], c)
                _remote(landed, landed, ici_send.at[k, t], ici_recv.at[k, t], (px, py, c)).wait_recv()
                cp = _remote(landed, landed, d2d_send.at[k, t], d2d_recv.at[k, t], sibling)
                cp.start()
                passed.append(cp)
        for k, (px, py) in enumerate(chips):
            for t in range(n):
                other = _half(outs[t].at[2 * px + py], 1 - c)
                _remote(other, other, d2d_send.at[k, t], d2d_recv.at[k, t], sibling).wait_recv()
        for cp in first + passed:
            cp.wait_send()

    res = pl.pallas_call(
        body, name="gather_weights", in_specs=[HBM_REF] * n, out_specs=[HBM_REF] * n,
        out_shape=[jax.ShapeDtypeStruct((N_SHARDS,) + b.shape, b.dtype) for b in blocks],
        scratch_shapes=[pltpu.SemaphoreType.DMA((3, n))] * 4, **COMM_PARAMS)(*blocks)
    me = 2 * lax.axis_index("x") + lax.axis_index("y")
    return [lax.dynamic_update_slice(g, b[None], (me, 0, 0)) for g, b in zip(res, blocks)]


def swap_halves(grads):
    n = len(grads)

    def body(*refs):
        ins, got = refs[:n], refs[n:2 * n]
        send_sems, recv_sems = refs[2 * n:]
        x, y, c = _place()
        give = [_remote(_half(ins[t], 1 - c), got[t], send_sems.at[t], recv_sems.at[t], (x, y, 1 - c))
                for t in range(n)]
        for cp in give:
            cp.start()
        for cp in give:
            cp.wait_recv()
        for cp in give:
            cp.wait_send()

    return pl.pallas_call(
        body, name="swap_halves", in_specs=[HBM_REF] * n, out_specs=[HBM_REF] * n,
        out_shape=[jax.ShapeDtypeStruct((g.shape[0], g.shape[1] // 2, g.shape[2]), g.dtype) for g in grads],
        scratch_shapes=[pltpu.SemaphoreType.DMA((n,))] * 2, **COMM_PARAMS)(*grads)


def scatter_partials(partials):
    n = len(partials)

    def body(*refs):
        parts, landed = refs[:n], refs[n:2 * n]
        send_sems, recv_sems = refs[2 * n:]
        x, y, c = _place()
        sends = [_remote(parts[t].at[2 * px + py], landed[t].at[k], send_sems.at[k, t], recv_sems.at[k, t],
                         (px, py, c)) for k, (px, py) in enumerate(_other_chips(x, y)) for t in range(n)]
        for cp in sends:
            cp.start()
        for cp in sends:
            cp.wait_recv()
        for cp in sends:
            cp.wait_send()

    return pl.pallas_call(
        body, name="scatter_partials", in_specs=[HBM_REF] * n, out_specs=[HBM_REF] * n,
        out_shape=[jax.ShapeDtypeStruct((3,) + p.shape[1:], p.dtype) for p in partials],
        scratch_shapes=[pltpu.SemaphoreType.DMA((3, n))] * 2, **COMM_PARAMS)(*partials)


def join_halves(blocks):
    n = len(blocks)

    def body(*refs):
        outs = refs[n:2 * n]
        send_sems, recv_sems = refs[2 * n:]
        x, y, c = _place()
        give = [_remote(_half(outs[t], c), _half(outs[t], c), send_sems.at[t], recv_sems.at[t], (x, y, 1 - c))
                for t in range(n)]
        for cp in give:
            cp.start()
        for t in range(n):
            arriving = _half(outs[t], 1 - c)
            _remote(arriving, arriving, send_sems.at[t], recv_sems.at[t], (x, y, 1 - c)).wait_recv()
        for cp in give:
            cp.wait_send()

    return pl.pallas_call(
        body, name="join_halves", in_specs=[HBM_REF] * n, out_specs=[HBM_REF] * n,
        out_shape=[jax.ShapeDtypeStruct(b.shape, b.dtype) for b in blocks],
        input_output_aliases={t: t for t in range(n)},
        scratch_shapes=[pltpu.SemaphoreType.DMA((n,))] * 2, **COMM_PARAMS)(*blocks)


def reduce_block_grads(grads):
    names = list(grads)
    got = swap_halves([grads[n] for n in names])
    partials = []
    for name, theirs in zip(names, got):
        n_slot, hr, width = theirs.shape
        tb = _row_block(hr, width, 6)
        per_half = hr // tb
        mine = lambda i, s, per_half=per_half: (i // per_half) * 2 * per_half + s[0] * per_half + i % per_half
        p = placed_map(
            jnp.add,
            [(grads[name].reshape(2 * n_slot * hr, width), mine), (theirs.reshape(n_slot * hr, width), lambda i, s: i)],
            (n_slot * hr, width, BF16, lambda i, s: i), n_blocks=n_slot * per_half, tb=tb, name="chip_sum_" + name)
        partials.append(p.reshape(theirs.shape))
    landed = scatter_partials(partials)
    blocks = []
    for name, theirs, arrived in zip(names, got, landed):
        n_slot, hr, width = theirs.shape
        tb = _row_block(hr, width, 6)
        per_half = hr // tb
        views = [(grads[name].reshape(2 * n_slot * hr, width),
                  lambda i, s, per_half=per_half: s[1] * 2 * per_half + s[0] * per_half + i),
                 (theirs.reshape(n_slot * hr, width), lambda i, s, per_half=per_half: s[1] * per_half + i)]
        views += [(arrived.reshape(3 * hr, width), functools.partial(lambda k, per_half, i, s: k * per_half + i,
                                                                     k, per_half)) for k in range(3)]
        f = lambda a, b, l0, l1, l2: (((a + b) + l0.astype(F32)) + l1.astype(F32)) + l2.astype(F32)
        blocks.append(placed_map(
            f, views,(2 * hr, width, F32, lambda i, s, per_half=per_half: s[0] * per_half + i),
            n_blocks=per_half, tb=tb, name="owner_sum_" + name))
    return dict(zip(names, join_halves(blocks)))


def adamw_block(name, w, g, m, v):
    rows, width = w.shape
    return rowmap(_adamw, [w, g, m, v], [], [(width, F32)] * 3, tb=_row_block(rows, width, 7),
                  name="adamw_" + name)


def reduce_small(vec, w, m, v):
    n_dev = 8

    def body(vec_ref, w_ref, m_ref, v_ref, loss_ref, g_ref, d_ref, m2_ref, v2_ref, slots, send_sems, recv_sems):
        x, y, c = _place()
        me = 4 * x + 2 * y + c
        slots[me] = vec_ref[...]
        flips = [(fx, fy, fc) for fx in (0, 1) for fy in (0, 1) for fc in (0, 1)][1:]
        peers = [(1 - x if fx else x, 1 - y if fy else y, 1 - c if fc else c) for fx, fy, fc in flips]
        sends = [pltpu.make_async_remote_copy(
            src_ref=vec_ref, dst_ref=slots.at[me], send_sem=send_sems.at[j], recv_sem=recv_sems.at[j],
            device_id=peer, device_id_type=MESH) for j, peer in enumerate(peers)]
        for cp in sends:
            cp.start()
        for j, (px, py, pc) in enumerate(peers):
            pltpu.make_async_remote_copy(
                src_ref=vec_ref, dst_ref=slots.at[4 * px + 2 * py + pc], send_sem=send_sems.at[j],
                recv_sem=recv_sems.at[j], device_id=(px, py, pc), device_id_type=MESH).wait_recv()
        for cp in sends:
            cp.wait_send()
        g = slots[0]
        for d in range(1, n_dev):
            g = g + slots[d]
        loss_ref[...] = jnp.sum(g[:, :D_MODEL], axis=1, keepdims=True)
        delta, m2, v2 = _adamw(w_ref[...], g, m_ref[...], v_ref[...])
        g_ref[...], d_ref[...], m2_ref[...], v2_ref[...] = g, delta, m2, v2

    vm = pl.BlockSpec(memory_space=pltpu.VMEM)
    vec_t = jax.ShapeDtypeStruct(vec.shape, F32)
    return pl.pallas_call(
        body, name="reduce_small", in_specs=[vm] * 4, out_specs=[vm] * 5,
        out_shape=[jax.ShapeDtypeStruct((1, 1), F32)] + [vec_t] * 4,
        scratch_shapes=[pltpu.VMEM((n_dev,) + vec.shape, F32), pltpu.SemaphoreType.DMA((n_dev - 1,)),
                        pltpu.SemaphoreType.DMA((n_dev - 1,))],
        compiler_params=pltpu.CompilerParams(has_side_effects=True),
    )(vec, w, m, v)


def kernel(x, ffn1_norm, ffn1_w_in, ffn1_w_out, mix_norm, w_in, b_gate, rwkv_mu, rwkv_w0, rwkv_w2, rwkv_a0, rwkv_a2, rwkv_g2, rwkv_k_k, rwkv_k_a, rwkv_r_k, rwkv_ln_w, rwkv_ln_b, attn_q_norm, attn_k_norm, w_proj_rwkv, w_proj_attn, w_out, ffn2_norm, ffn2_w_in, ffn2_w_out, loss_target, m_ffn1_norm, m_ffn1_w_in, m_ffn1_w_out, m_mix_norm, m_w_in, m_b_gate, m_rwkv_mu, m_rwkv_w0, m_rwkv_w2, m_rwkv_a0, m_rwkv_a2, m_rwkv_g2, m_rwkv_k_k, m_rwkv_k_a, m_rwkv_r_k, m_rwkv_ln_w, m_rwkv_ln_b, m_attn_q_norm, m_attn_k_norm, m_w_proj_rwkv, m_w_proj_attn, m_w_out, m_ffn2_norm, m_ffn2_w_in, m_ffn2_w_out, v_ffn1_norm, v_ffn1_w_in, v_ffn1_w_out, v_mix_norm, v_w_in, v_b_gate, v_rwkv_mu, v_rwkv_w0, v_rwkv_w2, v_rwkv_a0, v_rwkv_a2, v_rwkv_g2, v_rwkv_k_k, v_rwkv_k_a, v_rwkv_r_k, v_rwkv_ln_w, v_rwkv_ln_b, v_attn_q_norm, v_attn_k_norm, v_w_proj_rwkv, v_w_proj_attn, v_w_out, v_ffn2_norm, v_ffn2_w_in, v_ffn2_w_out):
    given = dict(locals())
    weights = {n: given[n] for n in WEIGHT_ORDER}
    mom_m = {n: given["m_" + n] for n in WEIGHT_ORDER}
    mom_v = {n: given["v_" + n] for n in WEIGHT_ORDER}
    big = [name for name, _, _ in BIG]
    shapes = {n: weights[n].shape for n in WEIGHT_ORDER}
    blocks_of = lambda d: local_blocks({n: d[n][0] for n in big})
    w_blk, m_blk, v_blk = blocks_of(weights), blocks_of(mom_m), blocks_of(mom_v)
    names = list(w_blk)

    gathered = gather_weights([w_blk[n].astype(BF16) for n in names])
    W = {n: blocks_to_full(n, g) for n, g in zip(names, gathered)}
    W.update(split_lora(W.pop("lora")))
    P = {n: weights[n].reshape(1, -1) for n, _ in SMALL}

    loss_cols, dx, gW, gP = layer_step(x[0], loss_target[0], W, P)

    gW["lora"] = jnp.concatenate([gW.pop(n) for n in LORA_PARTS], axis=0)
    g_blk = reduce_block_grads({n: full_to_blocks(n, gW[n]) for n in names})
    out_g, out_d, out_m, out_v = {}, {}, {}, {}
    for n in names:
        res = (g_blk[n], *adamw_block(n, w_blk[n], g_blk[n], m_blk[n], v_blk[n]))
        for dst, t in zip((out_g, out_d, out_m, out_v), res):
            for part, val in (split_lora(t) if n == "lora" else {n: t}).items():
                dst[part] = val.reshape(shapes[part])

    zero_head = jnp.zeros((1, D_MODEL), F32)
    vec = pack_small(gP, loss_cols)
    loss, g_s, d_s, m_s, v_s = reduce_small(
        vec, pack_small({n: weights[n] for n, _ in SMALL}, zero_head),
        pack_small({n: mom_m[n] for n, _ in SMALL}, zero_head),
        pack_small({n: mom_v[n] for n, _ in SMALL}, zero_head))
    for dst, src in ((out_g, g_s), (out_d, d_s), (out_m, m_s), (out_v, v_s)):
        dst.update(unpack_small(src, shapes))

    return (loss[0, 0], dx[None], *[out_g[n] for n in WEIGHT_ORDER], *[out_d[n] for n in WEIGHT_ORDER],
            *[out_m[n] for n in WEIGHT_ORDER], *[out_v[n] for n in WEIGHT_ORDER])
```

```python
import functools

import jax
import jax.numpy as jnp
from jax import lax
from jax.experimental import pallas as pl
from jax.experimental.pallas import tpu as pltpu

F32 = jnp.float32
BF16 = jnp.bfloat16
HI = lax.Precision.HIGHEST
MESH = pl.DeviceIdType.MESH

D_MODEL = 1024
HEAD_DIM = 64
RWKV_HEADS = 16
LORA_W, LORA_A, LORA_G = 64, 64, 160
LORA = LORA_W + LORA_A + LORA_G
RKV = 3 * D_MODEL
ATTN_PAIRS = ((128, 1), (512, 4), (2048, 16))
ATTN_BLK = 128
ATTN_HPG = 4
ATTN_WIDTH = 768
GROUP_W = ATTN_HPG * HEAD_DIM
D_FF = 2816
GN_EPS = 64e-5
RMS_EPS = 1e-6
NEG_INF = -1e30
WKV_CHUNK = 64
WKV_HEADS_PER_STEP = 8

ADAM_LR, ADAM_B1, ADAM_B2, ADAM_EPS, ADAM_WD, ADAM_STEP = 0.001, 0.9, 0.999, 1e-08, 0.01, 10

V7X_VMEM_BYTES = 64 << 20
VMEM_TEMP_ALLOWANCE = 20 << 20


def _cparams(sem, block_bytes):
    limit = min(2 * block_bytes + VMEM_TEMP_ALLOWANCE, V7X_VMEM_BYTES - (6 << 20))
    return pltpu.CompilerParams(dimension_semantics=sem, vmem_limit_bytes=int(limit))


def _nbytes(shape, dtype):
    n = 1
    for s in shape:
        n *= s
    return n * jnp.dtype(dtype).itemsize


def _split_bf16(a):
    hi = a.astype(BF16)
    return hi, (a - hi.astype(F32)).astype(BF16)


def _make_dots(prec):
    def one(a, b, ca, cb):
        return lax.dot_general(a, b, (((ca,), (cb,)), ((), ())), precision=None if prec == "x3" else prec,
                               preferred_element_type=F32)

    def raw(a, b, ca, cb):
        if prec != "x3":
            return one(a, b, ca, cb)
        (ah, al), (bh, bl) = _split_bf16(a), _split_bf16(b)
        return one(ah, bh, ca, cb) + (one(al, bh, ca, cb) + one(ah, bl, ca, cb))

    @jax.custom_vjp
    def nn(a, b):
        return raw(a, b, 1, 0)

    @jax.custom_vjp
    def nt(a, b):
        return raw(a, b, 1, 1)

    @jax.custom_vjp
    def tn(a, b):
        return raw(a, b, 0, 0)

    nn.defvjp(lambda a, b: (raw(a, b, 1, 0), (a, b)),
              lambda res, g: (raw(g, res[1], 1, 1), raw(res[0], g, 0, 0)))
    nt.defvjp(lambda a, b: (raw(a, b, 1, 1), (a, b)),
              lambda res, g: (raw(g, res[1], 1, 0), raw(g, res[0], 0, 0)))
    tn.defvjp(lambda a, b: (raw(a, b, 0, 0), (a, b)),
              lambda res, g: (raw(res[1], g, 1, 1), raw(res[0], g, 1, 0)))
    return nn, nt, tn


def _exact_rhs_dot(x, ones, cx, co):
    hi, lo = _split_bf16(x)
    dims = (((cx,), (co,)), ((), ()))
    return (lax.dot_general(hi, ones, dims, preferred_element_type=F32)
            + lax.dot_general(lo, ones, dims, preferred_element_type=F32))


@jax.custom_vjp
def SEG(x, ones):
    return _exact_rhs_dot(x, ones, 1, 0)


SEG.defvjp(lambda x, ones: (_exact_rhs_dot(x, ones, 1, 0), ones),
           lambda ones, g: (_exact_rhs_dot(g, ones, 1, 1), jnp.zeros_like(ones)))

NN, NT, TN = _make_dots(None)
NN_HI, NT_HI, TN_HI = _make_dots(HI)
NN_X3, NT_X3, TN_X3 = _make_dots("x3")


MM_TILE_M, MM_TILE_N, MM_TILE_K = 1408, 1408, 1536


def _pick(n, cap):
    best = None
    for t in range(128, min(n, cap) + 1, 128):
        if n % t == 0:
            best = t
    return best or n


def matmul(a, b, mode, name, *, add=None, scale=1.0, out_dtype=F32):
    if mode == "nn":
        (M, K), (K2, N) = a.shape, b.shape
    elif mode == "nt":
        (M, K), (N, K2) = a.shape, b.shape
    else:
        (K, M), (K2, N) = a.shape, b.shape
    assert K == K2, (name, a.shape, b.shape)
    tm, tn, tk = _pick(M, MM_TILE_M), _pick(N, MM_TILE_N), _pick(K, MM_TILE_K)
    nk = K // tk
    ca, cb = {"nn": (1, 0), "nt": (1, 1), "tn": (0, 0)}[mode]

    def body(*refs):
        if add is None:
            a_ref, b_ref, o_ref, acc_ref = refs
        else:
            a_ref, b_ref, add_ref, o_ref, acc_ref = refs
        k = pl.program_id(2)

        @pl.when(k == 0)
        def _():
            acc_ref[...] = jnp.zeros_like(acc_ref)

        acc_ref[...] += lax.dot_general(a_ref[...].astype(BF16), b_ref[...].astype(BF16),
                                        (((ca,), (cb,)), ((), ())), preferred_element_type=F32)

        @pl.when(k == nk - 1)
        def _():
            r = acc_ref[...] * scale
            if add is not None:
                r = add_ref[...] + r
            o_ref[...] = r.astype(o_ref.dtype)

    a_spec = (pl.BlockSpec((tk, tm), lambda i, j, k: (k, i)) if mode == "tn"
              else pl.BlockSpec((tm, tk), lambda i, j, k: (i, k)))
    b_spec = (pl.BlockSpec((tn, tk), lambda i, j, k: (j, k)) if mode == "nt"
              else pl.BlockSpec((tk, tn), lambda i, j, k: (k, j)))
    in_specs, args = [a_spec, b_spec], [a, b]
    blk = tm * tk * a.dtype.itemsize + tk * tn * b.dtype.itemsize + tm * tn * 8
    if add is not None:
        in_specs.append(pl.BlockSpec((tm, tn), lambda i, j, k: (i, j)))
        args.append(add)
        blk += tm * tn * 4
    return pl.pallas_call(
        body, name=name, grid=(M // tm, N // tn, nk),
        in_specs=in_specs, out_specs=pl.BlockSpec((tm, tn), lambda i, j, k: (i, j)),
        out_shape=jax.ShapeDtypeStruct((M, N), out_dtype),
        scratch_shapes=[pltpu.VMEM((tm, tn), F32)],
        compiler_params=_cparams(("parallel", "parallel", "arbitrary"), blk),
    )(*args)


def matmul_cs(a, w, mode, name, *, scale=1.0, out_dtype=F32):
    n_blk = N_SHARDS
    if mode == "tn":
        (K, R), Cs = a.shape, w.shape[1] // n_blk
        tm, tk = _pick(R, 512), _pick(K, 1024)
        grid = (R // tm, n_blk, K // tk)
        a_spec = pl.BlockSpec((tk, tm), lambda i, j, k: (k, i))
        w_spec = pl.BlockSpec((tk, Cs), lambda i, j, k: (k, j))
        o_spec = pl.BlockSpec((None, tm, Cs), lambda i, j, k: (j, i, 0))
        out_shape, acc_shape, dims = (n_blk, R, Cs), (tm, Cs), (0, 0)
        blk = tk * tm * a.dtype.itemsize + tk * Cs * w.dtype.itemsize + tm * Cs * 8
    elif mode == "nn":
        (M, R), Cs = a.shape, w.shape[2]
        tm, tk = _pick(M, 512), _pick(R, 1024)
        grid = (M // tm, n_blk, R // tk)
        a_spec = pl.BlockSpec((tm, tk), lambda i, j, k: (i, k))
        w_spec = pl.BlockSpec((None, tk, Cs), lambda i, j, k: (j, k, 0))
        o_spec = pl.BlockSpec((tm, Cs), lambda i, j, k: (i, j))
        out_shape, acc_shape, dims = (M, n_blk * Cs), (tm, Cs), (1, 0)
        blk = tm * tk * a.dtype.itemsize + tk * Cs * w.dtype.itemsize + tm * Cs * 8
    else:
        M, (_, R, Cs) = a.shape[0], w.shape
        tm, tn = _pick(M, 512), _pick(R, 512)
        grid = (M // tm, R // tn, n_blk)
        a_spec = pl.BlockSpec((tm, Cs), lambda i, j, k: (i, k))
        w_spec = pl.BlockSpec((None, tn, Cs), lambda i, j, k: (k, j, 0))
        o_spec = pl.BlockSpec((tm, tn), lambda i, j, k: (i, j))
        out_shape, acc_shape, dims = (M, R), (tm, tn), (1, 1)
        blk = tm * Cs * a.dtype.itemsize + tn * Cs * w.dtype.itemsize + tm * tn * 8
    nk = grid[2]

    def body(a_ref, w_ref, o_ref, acc_ref):
        k = pl.program_id(2)

        @pl.when(k == 0)
        def _():
            acc_ref[...] = jnp.zeros_like(acc_ref)

        acc_ref[...] += lax.dot_general(a_ref[...].astype(BF16), w_ref[...].astype(BF16),
                                        (((dims[0],), (dims[1],)), ((), ())), preferred_element_type=F32)

        @pl.when(k == nk - 1)
        def _():
            o_ref[...] = (acc_ref[...] * scale).astype(o_ref.dtype)

    return pl.pallas_call(
        body, name=name, grid=grid, in_specs=[a_spec, w_spec], out_specs=o_spec,
        out_shape=jax.ShapeDtypeStruct(out_shape, out_dtype), scratch_shapes=[pltpu.VMEM(acc_shape, F32)],
        compiler_params=_cparams(("parallel", "parallel", "arbitrary"), blk),
    )(a, w)


def _row_block(n, width, n_arrays):
    cap = (V7X_VMEM_BYTES // 4) // (2 * 4 * width * n_arrays)
    best = None
    for t in range(16, min(n, cap) + 1, 16):
        if n % t == 0:
            best = t
    return best or n


def placed_map(f, ins, out, *, n_blocks, tb, name):
    def body(*refs):
        refs[-1][...] = f(*[r[...] for r in refs[:-1]]).astype(refs[-1].dtype)

    def spec(fn):
        def index(i):
            x, y, c = _place()
            return fn(i, (c, 2 * x + y)), 0
        return pl.BlockSpec((tb, width), index)

    o_rows, width, o_dtype, o_fn = out
    blk = (sum(a.dtype.itemsize for a, _ in ins) + jnp.dtype(o_dtype).itemsize) * tb * width
    return pl.pallas_call(
        body, name=name, grid=(n_blocks,), in_specs=[spec(fn) for _, fn in ins], out_specs=spec(o_fn),
        out_shape=jax.ShapeDtypeStruct((o_rows, width), o_dtype),
        compiler_params=_cparams(("parallel",), blk),
    )(*[a for a, _ in ins])


def rowmap(f, rows, params, outs, accs=(), *, tb, name):
    rows = [r if isinstance(r, tuple) else (r, r.shape[1], 0) for r in rows]
    S = rows[0][0].shape[0]
    assert S % tb == 0, (name, S, tb)
    n_in, n_out = len(rows) + len(params), len(outs)

    def body(*refs):
        res = f(*[r[...] for r in refs[:n_in]])
        res = res if isinstance(res, (tuple, list)) else (res,)
        o_refs, a_refs = refs[n_in:n_in + n_out], refs[n_in + n_out:]
        for ref, val in zip(o_refs, res[:n_out]):
            ref[...] = val.astype(ref.dtype)
        if a_refs:
            @pl.when(pl.program_id(0) == 0)
            def _():
                for ref in a_refs:
                    ref[...] = jnp.zeros_like(ref)

            for ref, val in zip(a_refs, res[n_out:]):
                ref[...] += val.astype(F32)

    in_specs = [pl.BlockSpec((tb, w), functools.partial(lambda cb, i: (i, cb), cb)) for _, w, cb in rows]
    in_specs += [pl.BlockSpec(p.shape, lambda i: (0, 0)) for p in params]
    out_specs = [pl.BlockSpec((tb, w), lambda i: (i, 0)) for w, _ in outs]
    out_specs += [pl.BlockSpec(tuple(s), lambda i: (0, 0)) for s in accs]
    out_shape = [jax.ShapeDtypeStruct((S, w), dt) for w, dt in outs]
    out_shape += [jax.ShapeDtypeStruct(tuple(s), F32) for s in accs]
    blk = sum(tb * w * a.dtype.itemsize for a, w, _ in rows) + sum(_nbytes(p.shape, p.dtype) for p in params)
    blk += sum(_nbytes((tb, w), dt) for w, dt in outs) + sum(_nbytes(s, F32) for s in accs)
    res = pl.pallas_call(
        body, name=name, grid=(S // tb,), in_specs=in_specs, out_specs=out_specs, out_shape=out_shape,
        compiler_params=_cparams(("arbitrary",) if accs else ("parallel",), blk),
    )(*[r[0] for r in rows], *params)
    return res


def _rms(x, g):
    return x * lax.rsqrt(jnp.mean(x * x, axis=-1, keepdims=True) + RMS_EPS) * g


def _softplus(z):
    return jnp.maximum(z, 0.0) + jnp.log(1.0 + jnp.exp(-jnp.abs(z)))


def _swiglu_act(gu):
    gate, up = gu[:, :D_FF], gu[:, D_FF:]
    return gate * jax.nn.sigmoid(gate) * up


def _rwkv_pre(xrk, xlo, w0, w2p, a0, a2p, g2p, k_k, k_a, seg, seg_t):
    k = xrk[:, D_MODEL:2 * D_MODEL]
    w = -_softplus(-(w0 + NN(jnp.tanh(xlo), w2p))) - 0.5
    log_decay = -jnp.exp(w)
    a = jax.nn.sigmoid(a0 + NN(xlo, a2p))
    g = NN(jax.nn.sigmoid(xlo), g2p)
    kk = k * k_k
    norm = jnp.maximum(jnp.sqrt(SEG(kk * kk, seg)), 1e-12)
    kk = kk * SEG(1.0 / norm, seg_t)
    k_mod = k * (1.0 + (a - 1.0) * k_a)
    return log_decay, k_mod, -kk, kk * a, g


def _rwkv_post(wkv, r, k_mod, v, g, r_k, ln_w, ln_b, seg, seg_t):
    inv_n = 1.0 / HEAD_DIM
    mean = SEG(wkv, seg) * inv_n
    cen = wkv - SEG(mean, seg_t)
    var = SEG(cen * cen, seg) * inv_n
    y = cen * SEG(lax.rsqrt(var + GN_EPS), seg_t) * ln_w + ln_b
    bonus = SEG(SEG(r * k_mod * r_k, seg), seg_t) * v
    return (y + bonus) * g


def _qk_norm(q, k, q_gain, k_gain, seg, seg_t, tile_t):
    def norm(x, gain):
        mean_sq = SEG(x * x, seg) * (1.0 / HEAD_DIM)
        return x * SEG(lax.rsqrt(mean_sq + RMS_EPS), seg_t) * SEG(gain, tile_t)

    return norm(q, q_gain) * (HEAD_DIM ** -0.5), norm(k, k_gain)


def _gate_merge(pgate, pa, pb, b_gate):
    sg = jax.nn.sigmoid(pgate + b_gate)
    return sg[:, :D_MODEL] * pa + sg[:, D_MODEL:] * pb


def _group_combine(o, lse):
    ls = [lse[:, GROUP_W * i:GROUP_W * (i + 1)] for i in range(3)]
    m = jnp.maximum(jnp.maximum(ls[0], ls[1]), ls[2])
    es = [jnp.exp(l - m) for l in ls]
    den = es[0] + es[1] + es[2]
    return jnp.concatenate([o[:, GROUP_W * i:GROUP_W * (i + 1)] * (es[i] / den) for i in range(3)], axis=1)


def _each(f, *xs):
    return tuple(f(*args) for args in zip(*xs))


def _attn_block(q, kc, kp, vc, vp, first):
    qi = lax.broadcasted_iota(jnp.int32, (ATTN_BLK, ATTN_BLK), 0)
    kj = lax.broadcasted_iota(jnp.int32, (ATTN_BLK, ATTN_BLK), 1)
    own, before = kj <= qi, (kj >= qi) & (first < 0.5)
    s_c = _each(lambda a, b: jnp.where(own, NT(a, b), NEG_INF), q, kc)
    s_p = _each(lambda a, b: jnp.where(before, NT(a, b), NEG_INF), q, kp)
    row_max = lambda s: jnp.max(s, axis=-1, keepdims=True)
    row_sum = lambda s: jnp.sum(s, axis=-1, keepdims=True)
    m = _each(lambda c_, p_: jnp.maximum(row_max(c_), row_max(p_)), s_c, s_p)
    e_c, e_p = _each(lambda s, m_: jnp.exp(s - m_), s_c, m), _each(lambda s, m_: jnp.exp(s - m_), s_p, m)
    den = _each(lambda c_, p_: row_sum(c_) + row_sum(p_), e_c, e_p)
    inv = _each(lambda d_: 1.0 / d_, den)
    o = _each(lambda ec, ep, i_, vc_, vp_: (NN(ec, vc_) + NN(ep, vp_)) * i_, e_c, e_p, inv, vc, vp)
    lse = _each(lambda m_, d_: jnp.broadcast_to(m_ + jnp.log(d_), (ATTN_BLK, HEAD_DIM)), m, den)
    return o, lse


def _tri_inverse(n):
    c = n[0].shape[0]
    eye = (lax.broadcasted_iota(jnp.int32, (c, c), 0) == lax.broadcasted_iota(jnp.int32, (c, c), 1)).astype(F32)
    t, p, span = _each(lambda m: eye + m, n), n, 2
    while span < c:
        p = _each(NN_X3, p, p)
        t = _each(lambda t_, p_: t_ + NN_X3(t_, p_), t, p)
        span *= 2
    return t


@jax.custom_vjp
def _tri_solve(n, rhs):
    return _each(NN_X3, _tri_inverse(n), rhs)


def _tri_solve_fwd(n, rhs):
    t = _tri_inverse(n)
    x = _each(NN_X3, t, rhs)
    return x, (t, x)


def _tri_solve_bwd(res, dx):
    t, x = res
    drhs = _each(TN_X3, t, dx)
    return _each(NT_X3, drhs, x), drhs


_tri_solve.defvjp(_tri_solve_fwd, _tri_solve_bwd)


def _lower_ones(c):
    row = lax.broadcasted_iota(jnp.int32, (c, c), 0)
    col = lax.broadcasted_iota(jnp.int32, (c, c), 1)
    return (row >= col).astype(BF16)


def _ones_dot(ones, x, contract):
    hi, lo = _split_bf16(x)
    dims = (((contract,), (0,)), ((), ()))
    return (lax.dot_general(ones, hi, dims, preferred_element_type=F32)
            + lax.dot_general(ones, lo, dims, preferred_element_type=F32))


@jax.custom_vjp
def _cumsum_rows(x):
    return _ones_dot(_lower_ones(x.shape[0]), x, 1)


_cumsum_rows.defvjp(lambda x: (_ones_dot(_lower_ones(x.shape[0]), x, 1), None),
                    lambda _, g: (_ones_dot(_lower_ones(g.shape[0]), g, 0),))


def _wkv_chunk(s0, r, lw, k, v, a, b):
    c = r[0].shape[0]
    row = lax.broadcasted_iota(jnp.int32, (c, c), 0)
    col = lax.broadcasted_iota(jnp.int32, (c, c), 1)
    strict, incl = row > col, row >= col
    cat = lambda p, q: jnp.concatenate([p, q], axis=0)
    cum = _each(_cumsum_rows, lw)
    e_neg = _each(lambda c_: jnp.exp(-c_), cum)
    ar = _each(lambda a_, r_, c_, l_: cat(a_ * jnp.exp(c_ - l_), r_ * jnp.exp(c_)), a, r, cum, lw)
    b_t, k_t = _each(jnp.multiply, b, e_neg), _each(jnp.multiply, k, e_neg)
    p_b, p_k, p_s = _each(NT_X3, ar, b_t), _each(NT_X3, ar, k_t), _each(NT_X3, ar, s0)
    n_ab = _each(lambda p: jnp.where(strict, p[:c], 0.0), p_b)
    m_rb = _each(lambda p: jnp.where(incl, p[c:], 0.0), p_b)
    n_ak = _each(lambda p: jnp.where(strict, p[:c], 0.0), p_k)
    m_rk = _each(lambda p: jnp.where(incl, p[c:], 0.0), p_k)
    u = _tri_solve(n_ab, _each(lambda p, n_, v_: p[:c] + NN_X3(n_, v_), p_s, n_ak, v))
    y = _each(lambda p, mb, u_, mk, v_: p[c:] + NN_X3(mb, u_) + NN_X3(mk, v_), p_s, m_rb, u, m_rk, v)
    g_end = _each(lambda l_: jnp.exp(jnp.sum(l_, axis=0, keepdims=True)), lw)
    s1 = _each(lambda s_, g_, u_, v_, b_, k_: s_ * g_ + TN_X3(cat(u_, v_), cat(b_, k_) * g_),
               s0, g_end, u, v, b_t, k_t)
    return y, s1


def _adamw(w, g, m, v):
    m = ADAM_B1 * m + (1.0 - ADAM_B1) * g
    v = ADAM_B2 * v + (1.0 - ADAM_B2) * jnp.square(g)
    m_hat = m / (1.0 - ADAM_B1 ** ADAM_STEP)
    v_hat = v / (1.0 - ADAM_B2 ** ADAM_STEP)
    delta = -ADAM_LR * (m_hat / (jnp.sqrt(v_hat) + ADAM_EPS) + ADAM_WD * w)
    return delta, m, v


def token_shift_fwd(p, mu, *, tb, name):
    S, W = p.shape
    hb = tb // 8

    def body(p_ref, halo_ref, mu_ref, o_ref):
        i = pl.program_id(0)
        x = p_ref[...]
        before = halo_ref[7:8, :] * (i > 0).astype(F32)
        row = lax.broadcasted_iota(jnp.int32, (tb, W), 0)
        prev = jnp.where(row == 0, before, pltpu.roll(x, 1, 0))
        o_ref[...] = x + (prev - x) * mu_ref[...]

    blk = (2 * tb + 8) * W * 4
    return pl.pallas_call(
        body, name=name, grid=(S // tb,),
        in_specs=[pl.BlockSpec((tb, W), lambda i: (i, 0)),
                  pl.BlockSpec((8, W), lambda i: (jnp.maximum(i * hb - 1, 0), 0)),
                  pl.BlockSpec((1, W), lambda i: (0, 0))],
        out_specs=pl.BlockSpec((tb, W), lambda i: (i, 0)),
        out_shape=jax.ShapeDtypeStruct((S, W), F32),
        compiler_params=_cparams(("parallel",), blk),
    )(p, p, mu)


def token_shift_bwd(dxs, p, mu, *, tb, name):
    S, W = p.shape
    hb, nb = tb // 8, S // tb

    def body(d_ref, dnext_ref, p_ref, halo_ref, mu_ref, dp_ref, dmu_ref):
        i = pl.program_id(0)
        d, x, mu_v = d_ref[...], p_ref[...], mu_ref[...]
        row = lax.broadcasted_iota(jnp.int32, (tb, W), 0)
        before = halo_ref[7:8, :] * (i > 0).astype(F32)
        prev = jnp.where(row == 0, before, pltpu.roll(x, 1, 0))
        t = d * mu_v
        after = dnext_ref[0:1, :] * mu_v * (i < nb - 1).astype(F32)
        nxt = jnp.where(row == tb - 1, after, pltpu.roll(t, tb - 1, 0))
        dp_ref[...] = (d - t + nxt).astype(dp_ref.dtype)

        @pl.when(i == 0)
        def _():
            dmu_ref[...] = jnp.zeros_like(dmu_ref)

        dmu_ref[...] += jnp.sum(d * (prev - x), axis=0, keepdims=True)

    blk = (3 * tb + 16) * W * 4
    return pl.pallas_call(
        body, name=name, grid=(nb,),
        in_specs=[pl.BlockSpec((tb, W), lambda i: (i, 0)),
                  pl.BlockSpec((8, W), lambda i: (jnp.minimum((i + 1) * hb, S // 8 - 1), 0)),
                  pl.BlockSpec((tb, W), lambda i: (i, 0)),
                  pl.BlockSpec((8, W), lambda i: (jnp.maximum(i * hb - 1, 0), 0)),
                  pl.BlockSpec((1, W), lambda i: (0, 0))],
        out_specs=[pl.BlockSpec((tb, W), lambda i: (i, 0)), pl.BlockSpec((1, W), lambda i: (0, 0))],
        out_shape=[jax.ShapeDtypeStruct((S, W), BF16), jax.ShapeDtypeStruct((1, W), F32)],
        compiler_params=_cparams(("arbitrary",), blk),
    )(dxs, dxs, p, p, mu)


def _head_cols(h):
    return pl.ds(h * HEAD_DIM, HEAD_DIM)


def wkv_fwd(xs_rk, lw, k, a, b):
    S = lw.shape[0]
    C, nc, G, N = WKV_CHUNK, S // WKV_CHUNK, WKV_HEADS_PER_STEP, HEAD_DIM

    def body(r_ref, lw_ref, k_ref, v_ref, a_ref, b_ref, y_ref, st_ref, state):
        @pl.when(pl.program_id(1) == 0)
        def _():
            state[...] = jnp.zeros_like(state)

        heads = lambda ref: tuple(ref[:, _head_cols(h)] for h in range(G))
        s0 = tuple(state[h] for h in range(G))
        y, s1 = _wkv_chunk(s0, heads(r_ref), heads(lw_ref), heads(k_ref), heads(v_ref), heads(a_ref),
                           heads(b_ref))
        for h in range(G):
            st_ref[h] = s0[h]
            y_ref[:, _head_cols(h)] = y[h]
            state[h] = s1[h]

    W = G * N
    seq = lambda j: pl.BlockSpec((C, W), functools.partial(lambda j, g, c: (c, j + g), j))
    per = D_MODEL // W
    return pl.pallas_call(
        body, name="wkv_fwd", grid=(RWKV_HEADS // G, nc),
        in_specs=[seq(0), seq(0), seq(0), seq(2 * per), seq(0), seq(0)],
        out_specs=[seq(0), pl.BlockSpec((None, G, N, N), lambda g, c: (c, g, 0, 0))],
        out_shape=[jax.ShapeDtypeStruct((S, D_MODEL), F32), jax.ShapeDtypeStruct((nc, RWKV_HEADS, N, N), F32)],
        scratch_shapes=[pltpu.VMEM((G, N, N), F32)],
        compiler_params=_cparams(("parallel", "arbitrary"), 8 * C * W * 4 + 2 * G * N * N * 4),
    )(xs_rk, lw, k, xs_rk, a, b)


def wkv_bwd(xs_rk, lw, k, a, b, states, dy):
    S = lw.shape[0]
    C, nc, G, N = WKV_CHUNK, S // WKV_CHUNK, WKV_HEADS_PER_STEP, HEAD_DIM

    def body(r_ref, lw_ref, k_ref, v_ref, a_ref, b_ref, st_ref, dy_ref,
             dr_ref, dlw_ref, dk_ref, dv_ref, da_ref, db_ref, dstate):
        @pl.when(pl.program_id(1) == 0)
        def _():
            dstate[...] = jnp.zeros_like(dstate)

        heads = lambda ref: tuple(ref[:, _head_cols(h)] for h in range(G))
        _, pull = jax.vjp(_wkv_chunk, tuple(st_ref[h] for h in range(G)), heads(r_ref), heads(lw_ref),
                          heads(k_ref), heads(v_ref), heads(a_ref), heads(b_ref))
        ds0, *grads = pull((heads(dy_ref), tuple(dstate[h] for h in range(G))))
        for h in range(G):
            dstate[h] = ds0[h]
            for ref, grad in zip((dr_ref, dlw_ref, dk_ref, dv_ref, da_ref, db_ref), grads):
                ref[:, _head_cols(h)] = grad[h]

    W = G * N
    seq = lambda j: pl.BlockSpec((C, W), functools.partial(lambda j, g, c: (nc - 1 - c, j + g), j))
    per = D_MODEL // W
    st = pl.BlockSpec((None, G, N, N), lambda g, c: (nc - 1 - c, g, 0, 0))
    return pl.pallas_call(
        body, name="wkv_bwd", grid=(RWKV_HEADS // G, nc),
        in_specs=[seq(0), seq(0), seq(0), seq(2 * per), seq(0), seq(0), st, seq(0)],
        out_specs=[seq(0)] * 6, out_shape=[jax.ShapeDtypeStruct((S, D_MODEL), F32)] * 6,
        scratch_shapes=[pltpu.VMEM((G, N, N), F32)],
        compiler_params=_cparams(("parallel", "arbitrary"), 14 * C * W * 4 + 2 * G * N * N * 4),
    )(xs_rk, lw, k, xs_rk, a, b, states, dy)


def _first_flag(i, seq_len):
    per_group = seq_len // ATTN_BLK
    g = i // per_group
    per_seq = [seq_len // d // ATTN_BLK for _, d in ATTN_PAIRS]
    n = jnp.where(g == 0, per_seq[0], jnp.where(g == 1, per_seq[1], per_seq[2]))
    return (lax.rem(i, n) == 0).astype(F32)


def attn_fwd(q, k, v, seq_len):
    R, N = q.shape
    nb = R // ATTN_BLK

    def body(q_ref, kc_ref, kp_ref, vc_ref, vp_ref, o_ref, lse_ref):
        first = _first_flag(pl.program_id(0), seq_len)
        heads = lambda ref: tuple(ref[:, _head_cols(h)] for h in range(ATTN_HPG))
        o, lse = _attn_block(heads(q_ref), heads(kc_ref), heads(kp_ref), heads(vc_ref), heads(vp_ref), first)
        for h in range(ATTN_HPG):
            o_ref[:, _head_cols(h)] = o[h]
            lse_ref[:, _head_cols(h)] = lse[h]

    cur = pl.BlockSpec((ATTN_BLK, N), lambda i: (i, 0))
    prv = pl.BlockSpec((ATTN_BLK, N), lambda i: (jnp.maximum(i - 1, 0), 0))
    return pl.pallas_call(
        body, name="attn_fwd", grid=(nb,), in_specs=[cur, cur, prv, cur, prv],
        out_specs=[cur, cur], out_shape=[jax.ShapeDtypeStruct((R, N), F32)] * 2,
        compiler_params=_cparams(("parallel",), 7 * ATTN_BLK * N * 4),
    )(q, k, k, v, v)


def attn_bwd(q, k, v, do, dlse, seq_len):
    R, N = q.shape
    nb = R // ATTN_BLK

    def body(q_ref, kc_ref, kp_ref, vc_ref, vp_ref, do_ref, dl_ref, dq_ref, dk_ref, dv_ref, carry_k, carry_v):
        step = pl.program_id(0)
        first = _first_flag(nb - 1 - step, seq_len)

        @pl.when(step == 0)
        def _():
            carry_k[...] = jnp.zeros_like(carry_k)
            carry_v[...] = jnp.zeros_like(carry_v)

        heads = lambda ref: tuple(ref[:, _head_cols(h)] for h in range(ATTN_HPG))
        _, pull = jax.vjp(functools.partial(_attn_block, first=first), heads(q_ref), heads(kc_ref), heads(kp_ref),
                          heads(vc_ref), heads(vp_ref))
        dq, dkc, dkp, dvc, dvp = pull((heads(do_ref), heads(dl_ref)))
        old_k, old_v = heads(carry_k), heads(carry_v)
        for h in range(ATTN_HPG):
            cols = _head_cols(h)
            dq_ref[:, cols] = dq[h]
            dk_ref[:, cols] = dkc[h] + old_k[h]
            dv_ref[:, cols] = dvc[h] + old_v[h]
            carry_k[:, cols] = dkp[h]
            carry_v[:, cols] = dvp[h]

    cur = pl.BlockSpec((ATTN_BLK, N), lambda i: (nb - 1 - i, 0))
    prv = pl.BlockSpec((ATTN_BLK, N), lambda i: (jnp.maximum(nb - 2 - i, 0), 0))
    return pl.pallas_call(
        body, name="attn_bwd", grid=(nb,), in_specs=[cur, cur, prv, cur, prv, cur, cur],
        out_specs=[cur, cur, cur], out_shape=[jax.ShapeDtypeStruct((R, N), F32)] * 3,
        scratch_shapes=[pltpu.VMEM((ATTN_BLK, N), F32)] * 2,
        compiler_params=_cparams(("arbitrary",), 12 * ATTN_BLK * N * 4),
    )(q, k, k, v, v, do, dlse)


def to_subsequences(t):
    S = t.shape[0]
    parts = []
    for gi, (_, d) in enumerate(ATTN_PAIRS):
        tg = t[:, GROUP_W * gi:GROUP_W * (gi + 1)].reshape(S // d, d, GROUP_W)
        parts.append(tg.transpose(1, 0, 2).reshape(S, GROUP_W))
    return jnp.concatenate(parts, axis=0)


def from_subsequences(u, S):
    parts = []
    for gi, (_, d) in enumerate(ATTN_PAIRS):
        ug = u[S * gi:S * (gi + 1)].reshape(d, S // d, GROUP_W)
        parts.append(ug.transpose(1, 0, 2).reshape(S, GROUP_W))
    return jnp.concatenate(parts, axis=1)


def _ffn_fwd(x, norm, w_in, w_out, tag):
    h = rowmap(_rms, [x], [norm], [(D_MODEL, BF16)], tb=512, name=tag + "_norm")[0]
    gu = matmul_cs(h, w_in, "nn", tag + "_in")
    act = rowmap(_swiglu_act, [gu], [], [(D_FF, BF16)], tb=256, name=tag + "_act")[0]
    y = matmul(act, w_out, "nn", tag + "_out", add=x, scale=0.5)
    return y, (x, h, gu, act)


def _ffn_bwd(dy, saved, norm, w_in, w_out, tag):
    x, h, gu, act = saved
    dact = matmul(dy, w_out, "nt", tag + "_dact", scale=0.5)
    dw_out = matmul(act, dy, "tn", tag + "_dwout", scale=0.5)

    def act_bwd(gu_b, dact_b):
        return jax.vjp(_swiglu_act, gu_b)[1](dact_b)[0]

    dgu = rowmap(act_bwd, [gu, dact], [], [(2 * D_FF, BF16)], tb=256, name=tag + "_dgu")[0]
    dh = matmul_cs(dgu, w_in, "nt", tag + "_dh")
    dw_in = matmul_cs(h, dgu, "tn", tag + "_dwin")

    def norm_bwd(x_b, dh_b, dy_b, g):
        dx, dg = jax.vjp(_rms, x_b, g)[1](dh_b)
        return dy_b + dx, dg

    dx, dnorm = rowmap(norm_bwd, [x, dh, dy], [norm], [(D_MODEL, F32)], [(1, D_MODEL)], tb=256,
                       name=tag + "_dnorm")
    return dx, dnorm, dw_in, dw_out


def layer_step(x, tgt, W, P):
    S = x.shape[0]
    head_of = lambda n: jnp.arange(n)[:, None] // HEAD_DIM == jnp.arange(n // HEAD_DIM)[None, :]
    seg, seg_a = head_of(D_MODEL).astype(BF16), head_of(ATTN_WIDTH).astype(BF16)
    seg_t, seg_a_t = seg.T, seg_a.T
    tile_t = (jnp.arange(HEAD_DIM)[:, None] == jnp.arange(ATTN_WIDTH)[None, :] % HEAD_DIM).astype(BF16)
    qk_params = [P["attn_q_norm"], P["attn_k_norm"], seg_a, seg_a_t, tile_t]
    w_rkv, w_lora = W["w_in"][:, :RKV], W["w_in"][:, RKV:RKV + LORA]
    w_qkv = W["w_in"][:, RKV + LORA:RKV + LORA + 3 * ATTN_WIDTH]
    w_gate = W["w_in"][:, RKV + LORA + 3 * ATTN_WIDTH:]
    mu_rk, mu_lo = P["rwkv_mu"][:, :RKV], P["rwkv_mu"][:, RKV:]
    zeros = lambda n: jnp.zeros((n, D_MODEL), F32)
    w2p = jnp.concatenate([W["rwkv_w2"], zeros(LORA - LORA_W)], axis=0)
    a2p = jnp.concatenate([zeros(LORA_W), W["rwkv_a2"], zeros(LORA_G)], axis=0)
    g2p = jnp.concatenate([zeros(LORA_W + LORA_A), W["rwkv_g2"]], axis=0)
    pre_params = [P["rwkv_w0"], w2p, P["rwkv_a0"], a2p, g2p, P["rwkv_k_k"], P["rwkv_k_a"], seg, seg_t]
    post_params = [P["rwkv_r_k"], P["rwkv_ln_w"], P["rwkv_ln_b"], seg, seg_t]
    col = lambda arr, j: (arr, D_MODEL, j)

    x1, ffn1_saved = _ffn_fwd(x, P["ffn1_norm"], W["ffn1_w_in"], W["ffn1_w_out"], "ffn1")
    h = rowmap(_rms, [x1], [P["mix_norm"]], [(D_MODEL, BF16)], tb=512, name="mix_norm")[0]
    p_rk = matmul(h, w_rkv, "nn", "proj_rkv")
    p_lo = matmul(h, w_lora, "nn", "proj_lora")
    p_qkv = matmul(h, w_qkv, "nn", "proj_qkv")
    p_gate = matmul(h, w_gate, "nn", "proj_gate")
    xs_rk = token_shift_fwd(p_rk, mu_rk, tb=256, name="shift_rk")
    xs_lo = token_shift_fwd(p_lo, mu_lo, tb=256, name="shift_lora")
    lw, k_mod, a_neg, b_kk, g = rowmap(
        _rwkv_pre, [xs_rk, xs_lo], pre_params, [(D_MODEL, F32)] * 5, tb=256, name="rwkv_pre")
    wkv, states = wkv_fwd(xs_rk, lw, k_mod, a_neg, b_kk)
    post_rows = [wkv, col(xs_rk, 0), k_mod, col(xs_rk, 2), g]
    y_a = rowmap(_rwkv_post, post_rows, post_params, [(D_MODEL, BF16)], tb=256, name="rwkv_post")[0]

    qk_rows = [(p_qkv, ATTN_WIDTH, 0), (p_qkv, ATTN_WIDTH, 1)]
    qn, kn = rowmap(_qk_norm, qk_rows, qk_params, [(ATTN_WIDTH, F32)] * 2, tb=256, name="qk_norm")
    q_s, k_s, v_s = to_subsequences(qn), to_subsequences(kn), to_subsequences(p_qkv[:, 2 * ATTN_WIDTH:])
    o_s, lse_s = attn_fwd(q_s, k_s, v_s, S)
    o, lse = from_subsequences(o_s, S), from_subsequences(lse_s, S)
    y_b = rowmap(_group_combine, [o, lse], [], [(ATTN_WIDTH, BF16)], tb=512, name="attn_combine")[0]

    pa = matmul(y_a, W["w_proj_rwkv"], "nn", "proj_a")
    pb = matmul(y_b, W["w_proj_attn"], "nn", "proj_b")
    merged = rowmap(_gate_merge, [p_gate, pa, pb], [P["b_gate"]], [(D_MODEL, BF16)], tb=256, name="merge")[0]
    x2 = matmul(merged, W["w_out"], "nn", "mix_out", add=x1)
    x3, ffn2_saved = _ffn_fwd(x2, P["ffn2_norm"], W["ffn2_w_in"], W["ffn2_w_out"], "ffn2")

    def loss_head(y_b_, t_b):
        err = y_b_ - t_b
        return err * (1.0 / D_MODEL), (0.5 / D_MODEL) * jnp.sum(err * err, axis=0, keepdims=True)

    dx3, loss_cols = rowmap(loss_head, [x3, tgt], [], [(D_MODEL, F32)], [(1, D_MODEL)], tb=512, name="loss")

    gW, gP = {}, {}
    dx2, gP["ffn2_norm"], gW["ffn2_w_in"], gW["ffn2_w_out"] = _ffn_bwd(
        dx3, ffn2_saved, P["ffn2_norm"], W["ffn2_w_in"], W["ffn2_w_out"], "ffn2")

    dmerged = matmul(dx2, W["w_out"], "nt", "d_merged")
    gW["w_out"] = matmul(merged, dx2, "tn", "dw_out")

    def merge_bwd(pg, pa_b, pb_b, dm, bg):
        return jax.vjp(_gate_merge, pg, pa_b, pb_b, bg)[1](dm)

    dp_gate, dpa, dpb, gP["b_gate"] = rowmap(
        merge_bwd, [p_gate, pa, pb, dmerged], [P["b_gate"]],
        [(2 * D_MODEL, BF16), (D_MODEL, BF16), (D_MODEL, BF16)], [(1, 2 * D_MODEL)], tb=256, name="merge_bwd")
    dy_a = matmul(dpa, W["w_proj_rwkv"], "nt", "d_ya")
    gW["w_proj_rwkv"] = matmul(y_a, dpa, "tn", "dw_proj_a")
    dy_b = matmul(dpb, W["w_proj_attn"], "nt", "d_yb")
    gW["w_proj_attn"] = matmul(y_b, dpb, "tn", "dw_proj_b")

    def combine_bwd(o_b, l_b, d_b):
        return jax.vjp(_group_combine, o_b, l_b)[1](d_b)

    do, dlse = rowmap(combine_bwd, [o, lse, dy_b], [], [(ATTN_WIDTH, F32)] * 2, tb=256, name="attn_combine_bwd")
    dq_s, dk_s, dv_s = attn_bwd(q_s, k_s, v_s, to_subsequences(do), to_subsequences(dlse), S)

    def qk_norm_bwd(q_b, k_b, dqn_b, dkn_b, dv_b, qg, kg, sg, sgt, tl):
        f = lambda *a: _qk_norm(*a, sg, sgt, tl)
        dq, dk, dqg, dkg = jax.vjp(f, q_b, k_b, qg, kg)[1]((dqn_b, dkn_b))
        return jnp.concatenate([dq, dk, dv_b], axis=1), dqg, dkg

    dp_qkv, gP["attn_q_norm"], gP["attn_k_norm"] = rowmap(
        qk_norm_bwd, qk_rows + [from_subsequences(t, S) for t in (dq_s, dk_s, dv_s)], qk_params,
        [(3 * ATTN_WIDTH, BF16)], [(1, HEAD_DIM)] * 2, tb=256, name="qk_norm_bwd")

    def post_bwd(wkv_b, r_b, k_b, v_b, g_b, d_b, r_k, ln_w, ln_b, sg, sgt):
        f = lambda *a: _rwkv_post(*a, sg, sgt)
        return jax.vjp(f, wkv_b, r_b, k_b, v_b, g_b, r_k, ln_w, ln_b)[1](d_b)

    dwkv, dr_p, dk_p, dv_p, dg, gP["rwkv_r_k"], gP["rwkv_ln_w"], gP["rwkv_ln_b"] = rowmap(
        post_bwd, post_rows + [dy_a], post_params, [(D_MODEL, F32)] * 5, [(1, D_MODEL)] * 3, tb=128,
        name="rwkv_post_bwd")
    dr_w, dlw, dk_w, dv_w, da_neg, db_kk = wkv_bwd(xs_rk, lw, k_mod, a_neg, b_kk, states, dwkv)

    def pre_bwd(xrk_b, xlo_b, dlw_b, dkw_b, dkp_b, da_b, db_b, dg_b, drp_b, drw_b, dvp_b, dvw_b,
                w0, w2, a0, a2, g2, k_k, k_a, sg, sgt):
        f = lambda *a: _rwkv_pre(*a, sg, sgt)
        pull = jax.vjp(f, xrk_b, xlo_b, w0, w2, a0, a2, g2, k_k, k_a)[1]
        dxrk, dxlo, *dpar = pull((dlw_b, dkw_b + dkp_b, da_b, db_b, dg_b))
        direct = jnp.concatenate([drp_b + drw_b, jnp.zeros_like(drp_b), dvp_b + dvw_b], axis=1)
        return (dxrk + direct, dxlo, *dpar)

    pre_rows = [xs_rk, xs_lo, dlw, dk_w, dk_p, da_neg, db_kk, dg, dr_p, dr_w, dv_p, dv_w]
    dxs_rk, dxs_lo, gP["rwkv_w0"], dw2p, gP["rwkv_a0"], da2p, dg2p, gP["rwkv_k_k"], gP["rwkv_k_a"] = rowmap(
        pre_bwd, pre_rows, pre_params, [(RKV, F32), (LORA, F32)],
        [(1, D_MODEL), (LORA, D_MODEL), (1, D_MODEL), (LORA, D_MODEL), (LORA, D_MODEL), (1, D_MODEL), (1, D_MODEL)],
        tb=128, name="rwkv_pre_bwd")
    gW["rwkv_w2"] = dw2p[:LORA_W]
    gW["rwkv_a2"] = da2p[LORA_W:LORA_W + LORA_A]
    gW["rwkv_g2"] = dg2p[LORA_W + LORA_A:]
    dp_rk, dmu_rk = token_shift_bwd(dxs_rk, p_rk, mu_rk, tb=256, name="shift_rk_bwd")
    dp_lo, dmu_lo = token_shift_bwd(dxs_lo, p_lo, mu_lo, tb=256, name="shift_lora_bwd")
    gP["rwkv_mu"] = jnp.concatenate([dmu_rk, dmu_lo], axis=1)

    dh = matmul(dp_rk, w_rkv, "nt", "dh_rkv")
    dh = matmul(dp_lo, w_lora, "nt", "dh_lora", add=dh)
    dh = matmul(dp_qkv, w_qkv, "nt", "dh_qkv", add=dh)
    dh = matmul(dp_gate, w_gate, "nt", "dh_gate", add=dh)
    gW["w_in"] = jnp.concatenate([
        matmul(h, dp_rk, "tn", "dw_rkv"), matmul(h, dp_lo, "tn", "dw_lora"),
        matmul(h, dp_qkv, "tn", "dw_qkv"), matmul(h, dp_gate, "tn", "dw_gate")], axis=1)

    def norm_bwd(x_b, dh_b, dy_b, gn):
        dx, dgn = jax.vjp(_rms, x_b, gn)[1](dh_b)
        return dy_b + dx, dgn

    dx1, gP["mix_norm"] = rowmap(norm_bwd, [x1, dh, dx2], [P["mix_norm"]], [(D_MODEL, F32)], [(1, D_MODEL)],
                                 tb=256, name="mix_norm_bwd")
    dx, gP["ffn1_norm"], gW["ffn1_w_in"], gW["ffn1_w_out"] = _ffn_bwd(
        dx1, ffn1_saved, P["ffn1_norm"], W["ffn1_w_in"], W["ffn1_w_out"], "ffn1")
    return loss_cols, dx, gW, gP


N_SHARDS = 4
BIG = (("ffn1_w_in", (D_MODEL, 2 * D_FF), 1), ("ffn1_w_out", (D_FF, D_MODEL), 0),
       ("w_in", (D_MODEL, 7712), 1), ("rwkv_w2", (LORA_W, D_MODEL), 1), ("rwkv_a2", (LORA_A, D_MODEL), 1),
       ("rwkv_g2", (LORA_G, D_MODEL), 1), ("w_proj_rwkv", (D_MODEL, D_MODEL), 0),
       ("w_proj_attn", (ATTN_WIDTH, D_MODEL), 1), ("w_out", (D_MODEL, D_MODEL), 0),
       ("ffn2_w_in", (D_MODEL, 2 * D_FF), 1), ("ffn2_w_out", (D_FF, D_MODEL), 0))
SMALL = (("ffn1_norm", 1024), ("mix_norm", 1024), ("b_gate", 2048), ("rwkv_mu", 3360), ("rwkv_w0", 1024),
         ("rwkv_a0", 1024), ("rwkv_k_k", 1024), ("rwkv_k_a", 1024), ("rwkv_r_k", 1024), ("rwkv_ln_w", 1024),
         ("rwkv_ln_b", 1024), ("attn_q_norm", 64), ("attn_k_norm", 64), ("ffn2_norm", 1024))
WEIGHT_ORDER = ("ffn1_norm", "ffn1_w_in", "ffn1_w_out", "mix_norm", "w_in", "b_gate", "rwkv_mu", "rwkv_w0",
                "rwkv_w2", "rwkv_a0", "rwkv_a2", "rwkv_g2", "rwkv_k_k", "rwkv_k_a", "rwkv_r_k", "rwkv_ln_w",
                "rwkv_ln_b", "attn_q_norm", "attn_k_norm", "w_proj_rwkv", "w_proj_attn", "w_out", "ffn2_norm",
                "ffn2_w_in", "ffn2_w_out")


LORA_PARTS = ("rwkv_w2", "rwkv_a2", "rwkv_g2")
BLOCK_MAJOR = ("ffn1_w_in", "ffn2_w_in")
SMALL_USED = D_MODEL + sum(n for _, n in SMALL)
SMALL_W = -(-SMALL_USED // 128) * 128


def _travel():
    out = {}
    for name, shape, axis in BIG:
        if name == LORA_PARTS[0]:
            out["lora"] = ((LORA, D_MODEL), 1)
        elif name not in LORA_PARTS:
            out[name] = (shape, axis)
    return out


def local_blocks(vals):
    out = {n: vals[n] for n in _travel() if n != "lora"}
    out["lora"] = jnp.concatenate([vals[n] for n in LORA_PARTS], axis=0)
    return out


def split_lora(t):
    return {"rwkv_w2": t[:LORA_W], "rwkv_a2": t[LORA_W:LORA_W + LORA_A], "rwkv_g2": t[LORA_W + LORA_A:]}


def blocks_to_full(name, blocks):
    shape, axis = _travel()[name]
    if name in BLOCK_MAJOR:
        return blocks
    if axis == 0:
        return blocks.reshape(shape)
    return blocks.transpose(1, 0, 2).reshape(shape)


def full_to_blocks(name, full):
    shape, axis = _travel()[name]
    if name in BLOCK_MAJOR:
        return full
    if axis == 0:
        return full.reshape(N_SHARDS, shape[0] // N_SHARDS, shape[1])
    return full.reshape(shape[0], N_SHARDS, shape[1] // N_SHARDS).transpose(1, 0, 2)


def pack_small(vals, head):
    parts = [head] + [vals[name].reshape(1, n) for name, n in SMALL]
    parts.append(jnp.zeros((1, SMALL_W - SMALL_USED), F32))
    return jnp.concatenate(parts, axis=1)


def unpack_small(vec, shapes):
    out, off = {}, D_MODEL
    for name, n in SMALL:
        out[name] = vec[:, off:off + n].reshape(shapes[name])
        off += n
    return out


def _place():
    return lax.axis_index("x"), lax.axis_index("y"), lax.axis_index("c")


def _other_chips(x, y):
    return [(1 - x, y), (x, 1 - y), (1 - x, 1 - y)]


def _remote(src, dst, send_sem, recv_sem, device):
    return pltpu.make_async_remote_copy(src_ref=src, dst_ref=dst, send_sem=send_sem, recv_sem=recv_sem,
                                        device_id=device, device_id_type=MESH)


def _half(ref, who):
    hr = ref.shape[-2] // 2
    rows = pl.ds(pl.multiple_of(who * hr, 8), hr)
    return ref.at[rows] if len(ref.shape) == 2 else ref.at[:, rows]


HBM_REF = pl.BlockSpec(memory_space=pl.ANY)
COMM_PARAMS = dict(compiler_params=pltpu.CompilerParams(has_side_effects=True))


def gather_weights(blocks):
    n = len(blocks)

    def body(*refs):
        ins, outs = refs[:n], refs[n:2 * n]
        ici_send, ici_recv, d2d_send, d2d_recv = refs[2 * n:]
        x, y, c = _place()
        me, sibling, chips = 2 * x + y, (x, y, 1 - c), _other_chips(x, y)
        first = [_remote(_half(ins[t], c), _half(outs[t].at[me], c), ici_send.at[k, t], ici_recv.at[k, t],
                         (px, py, c)) for k, (px, py) in enumerate(chips) for t in range(n)]
        for cp in first:
            cp.start()
        passed = []
        for k, (px, py) in enumerate(chips):
            for t in range(n):
                landed = _half(outs[t].at[2 * px + py], c)
                _remote(landed, landed, ici_send.at[k, t], ici_recv.at[k, t], (px, py, c)).wait_recv()
                cp = _remote(landed, landed, d2d_send.at[k, t], d2d_recv.at[k, t], sibling)
                cp.start()
                passed.append(cp)
        for k, (px, py) in enumerate(chips):
            for t in range(n):
                other = _half(outs[t].at[2 * px + py], 1 - c)
                _remote(other, other, d2d_send.at[k, t], d2d_recv.at[k, t], sibling).wait_recv()
        for cp in first + passed:
            cp.wait_send()

    res = pl.pallas_call(
        body, name="gather_weights", in_specs=[HBM_REF] * n, out_specs=[HBM_REF] * n,
        out_shape=[jax.ShapeDtypeStruct((N_SHARDS,) + b.shape, b.dtype) for b in blocks],
        scratch_shapes=[pltpu.SemaphoreType.DMA((3, n))] * 4, **COMM_PARAMS)(*blocks)
    me = 2 * lax.axis_index("x") + lax.axis_index("y")
    return [lax.dynamic_update_slice(g, b[None], (me, 0, 0)) for g, b in zip(res, blocks)]


def swap_halves(grads):
    n = len(grads)

    def body(*refs):
        ins, got = refs[:n], refs[n:2 * n]
        send_sems, recv_sems = refs[2 * n:]
        x, y, c = _place()
        give = [_remote(_half(ins[t], 1 - c), got[t], send_sems.at[t], recv_sems.at[t], (x, y, 1 - c))
                for t in range(n)]
        for cp in give:
            cp.start()
        for cp in give:
            cp.wait_recv()
        for cp in give:
            cp.wait_send()

    return pl.pallas_call(
        body, name="swap_halves", in_specs=[HBM_REF] * n, out_specs=[HBM_REF] * n,
        out_shape=[jax.ShapeDtypeStruct((g.shape[0], g.shape[1] // 2, g.shape[2]), g.dtype) for g in grads],
        scratch_shapes=[pltpu.SemaphoreType.DMA((n,))] * 2, **COMM_PARAMS)(*grads)


def scatter_partials(partials):
    n = len(partials)

    def body(*refs):
        parts, landed = refs[:n], refs[n:2 * n]
        send_sems, recv_sems = refs[2 * n:]
        x, y, c = _place()
        sends = [_remote(parts[t].at[2 * px + py], landed[t].at[k], send_sems.at[k, t], recv_sems.at[k, t],
                         (px, py, c)) for k, (px, py) in enumerate(_other_chips(x, y)) for t in range(n)]
        for cp in sends:
            cp.start()
        for cp in sends:
            cp.wait_recv()
        for cp in sends:
            cp.wait_send()

    return pl.pallas_call(
        body, name="scatter_partials", in_specs=[HBM_REF] * n, out_specs=[HBM_REF] * n,
        out_shape=[jax.ShapeDtypeStruct((3,) + p.shape[1:], p.dtype) for p in partials],
        scratch_shapes=[pltpu.SemaphoreType.DMA((3, n))] * 2, **COMM_PARAMS)(*partials)


def join_halves(blocks):
    n = len(blocks)

    def body(*refs):
        outs = refs[n:2 * n]
        send_sems, recv_sems = refs[2 * n:]
        x, y, c = _place()
        give = [_remote(_half(outs[t], c), _half(outs[t], c), send_sems.at[t], recv_sems.at[t], (x, y, 1 - c))
                for t in range(n)]
        for cp in give:
            cp.start()
        for t in range(n):
            arriving = _half(outs[t], 1 - c)
            _remote(arriving, arriving, send_sems.at[t], recv_sems.at[t], (x, y, 1 - c)).wait_recv()
        for cp in give:
            cp.wait_send()

    return pl.pallas_call(
        body, name="join_halves", in_specs=[HBM_REF] * n, out_specs=[HBM_REF] * n,
        out_shape=[jax.ShapeDtypeStruct(b.shape, b.dtype) for b in blocks],
        input_output_aliases={t: t for t in range(n)},
        scratch_shapes=[pltpu.SemaphoreType.DMA((n,))] * 2, **COMM_PARAMS)(*blocks)


def reduce_block_grads(grads):
    names = list(grads)
    got = swap_halves([grads[n] for n in names])
    partials = []
    for name, theirs in zip(names, got):
        n_slot, hr, width = theirs.shape
        tb = _row_block(hr, width, 6)
        per_half = hr // tb
        mine = lambda i, s, per_half=per_half: (i // per_half) * 2 * per_half + s[0] * per_half + i % per_half
        p = placed_map(
            jnp.add,
            [(grads[name].reshape(2 * n_slot * hr, width), mine), (theirs.reshape(n_slot * hr, width), lambda i, s: i)],
            (n_slot * hr, width, BF16, lambda i, s: i), n_blocks=n_slot * per_half, tb=tb, name="chip_sum_" + name)
        partials.append(p.reshape(theirs.shape))
    landed = scatter_partials(partials)
    blocks = []
    for name, theirs, arrived in zip(names, got, landed):
        n_slot, hr, width = theirs.shape
        tb = _row_block(hr, width, 6)
        per_half = hr // tb
        views = [(grads[name].reshape(2 * n_slot * hr, width),
                  lambda i, s, per_half=per_half: s[1] * 2 * per_half + s[0] * per_half + i),
                 (theirs.reshape(n_slot * hr, width), lambda i, s, per_half=per_half: s[1] * per_half + i)]
        views += [(arrived.reshape(3 * hr, width), functools.partial(lambda k, per_half, i, s: k * per_half + i,
                                                                     k, per_half)) for k in range(3)]
        f = lambda a, b, l0, l1, l2: (((a + b) + l0.astype(F32)) + l1.astype(F32)) + l2.astype(F32)
        blocks.append(placed_map(
            f, views,(2 * hr, width, F32, lambda i, s, per_half=per_half: s[0] * per_half + i),
            n_blocks=per_half, tb=tb, name="owner_sum_" + name))
    return dict(zip(names, join_halves(blocks)))


def adamw_block(name, w, g, m, v):
    rows, width = w.shape
    return rowmap(_adamw, [w, g, m, v], [], [(width, F32)] * 3, tb=_row_block(rows, width, 7),
                  name="adamw_" + name)


def reduce_small(vec, w, m, v):
    n_dev = 8

    def body(vec_ref, w_ref, m_ref, v_ref, loss_ref, g_ref, d_ref, m2_ref, v2_ref, slots, send_sems, recv_sems):
        x, y, c = _place()
        me = 4 * x + 2 * y + c
        slots[me] = vec_ref[...]
        flips = [(fx, fy, fc) for fx in (0, 1) for fy in (0, 1) for fc in (0, 1)][1:]
        peers = [(1 - x if fx else x, 1 - y if fy else y, 1 - c if fc else c) for fx, fy, fc in flips]
        sends = [pltpu.make_async_remote_copy(
            src_ref=vec_ref, dst_ref=slots.at[me], send_sem=send_sems.at[j], recv_sem=recv_sems.at[j],
            device_id=peer, device_id_type=MESH) for j, peer in enumerate(peers)]
        for cp in sends:
            cp.start()
        for j, (px, py, pc) in enumerate(peers):
            pltpu.make_async_remote_copy(
                src_ref=vec_ref, dst_ref=slots.at[4 * px + 2 * py + pc], send_sem=send_sems.at[j],
                recv_sem=recv_sems.at[j], device_id=(px, py, pc), device_id_type=MESH).wait_recv()
        for cp in sends:
            cp.wait_send()
        g = slots[0]
        for d in range(1, n_dev):
            g = g + slots[d]
        loss_ref[...] = jnp.sum(g[:, :D_MODEL], axis=1, keepdims=True)
        delta, m2, v2 = _adamw(w_ref[...], g, m_ref[...], v_ref[...])
        g_ref[...], d_ref[...], m2_ref[...], v2_ref[...] = g, delta, m2, v2

    vm = pl.BlockSpec(memory_space=pltpu.VMEM)
    vec_t = jax.ShapeDtypeStruct(vec.shape, F32)
    return pl.pallas_call(
        body, name="reduce_small", in_specs=[vm] * 4, out_specs=[vm] * 5,
        out_shape=[jax.ShapeDtypeStruct((1, 1), F32)] + [vec_t] * 4,
        scratch_shapes=[pltpu.VMEM((n_dev,) + vec.shape, F32), pltpu.SemaphoreType.DMA((n_dev - 1,)),
                        pltpu.SemaphoreType.DMA((n_dev - 1,))],
        compiler_params=pltpu.CompilerParams(has_side_effects=True),
    )(vec, w, m, v)


def kernel(x, ffn1_norm, ffn1_w_in, ffn1_w_out, mix_norm, w_in, b_gate, rwkv_mu, rwkv_w0, rwkv_w2, rwkv_a0, rwkv_a2, rwkv_g2, rwkv_k_k, rwkv_k_a, rwkv_r_k, rwkv_ln_w, rwkv_ln_b, attn_q_norm, attn_k_norm, w_proj_rwkv, w_proj_attn, w_out, ffn2_norm, ffn2_w_in, ffn2_w_out, loss_target, m_ffn1_norm, m_ffn1_w_in, m_ffn1_w_out, m_mix_norm, m_w_in, m_b_gate, m_rwkv_mu, m_rwkv_w0, m_rwkv_w2, m_rwkv_a0, m_rwkv_a2, m_rwkv_g2, m_rwkv_k_k, m_rwkv_k_a, m_rwkv_r_k, m_rwkv_ln_w, m_rwkv_ln_b, m_attn_q_norm, m_attn_k_norm, m_w_proj_rwkv, m_w_proj_attn, m_w_out, m_ffn2_norm, m_ffn2_w_in, m_ffn2_w_out, v_ffn1_norm, v_ffn1_w_in, v_ffn1_w_out, v_mix_norm, v_w_in, v_b_gate, v_rwkv_mu, v_rwkv_w0, v_rwkv_w2, v_rwkv_a0, v_rwkv_a2, v_rwkv_g2, v_rwkv_k_k, v_rwkv_k_a, v_rwkv_r_k, v_rwkv_ln_w, v_rwkv_ln_b, v_attn_q_norm, v_attn_k_norm, v_w_proj_rwkv, v_w_proj_attn, v_w_out, v_ffn2_norm, v_ffn2_w_in, v_ffn2_w_out):
    given = dict(locals())
    weights = {n: given[n] for n in WEIGHT_ORDER}
    mom_m = {n: given["m_" + n] for n in WEIGHT_ORDER}
    mom_v = {n: given["v_" + n] for n in WEIGHT_ORDER}
    big = [name for name, _, _ in BIG]
    shapes = {n: weights[n].shape for n in WEIGHT_ORDER}
    blocks_of = lambda d: local_blocks({n: d[n][0] for n in big})
    w_blk, m_blk, v_blk = blocks_of(weights), blocks_of(mom_m), blocks_of(mom_v)
    names = list(w_blk)

    gathered = gather_weights([w_blk[n].astype(BF16) for n in names])
    W = {n: blocks_to_full(n, g) for n, g in zip(names, gathered)}
    W.update(split_lora(W.pop("lora")))
    P = {n: weights[n].reshape(1, -1) for n, _ in SMALL}

    loss_cols, dx, gW, gP = layer_step(x[0], loss_target[0], W, P)

    gW["lora"] = jnp.concatenate([gW.pop(n) for n in LORA_PARTS], axis=0)
    g_blk = reduce_block_grads({n: full_to_blocks(n, gW[n]) for n in names})
    out_g, out_d, out_m, out_v = {}, {}, {}, {}
    for n in names:
        res = (g_blk[n], *adamw_block(n, w_blk[n], g_blk[n], m_blk[n], v_blk[n]))
        for dst, t in zip((out_g, out_d, out_m, out_v), res):
            for part, val in (split_lora(t) if n == "lora" else {n: t}).items():
                dst[part] = val.reshape(shapes[part])

    zero_head = jnp.zeros((1, D_MODEL), F32)
    vec = pack_small(gP, loss_cols)
    loss, g_s, d_s, m_s, v_s = reduce_small(
        vec, pack_small({n: weights[n] for n, _ in SMALL}, zero_head),
        pack_small({n: mom_m[n] for n, _ in SMALL}, zero_head),
        pack_small({n: mom_v[n] for n, _ in SMALL}, zero_head))
    for dst, src in ((out_g, g_s), (out_d, d_s), (out_m, m_s), (out_v, v_s)):
        dst.update(unpack_small(src, shapes))

    return (loss[0, 0], dx[None], *[out_g[n] for n in WEIGHT_ORDER], *[out_d[n] for n in WEIGHT_ORDER],
            *[out_m[n] for n in WEIGHT_ORDER], *[out_v[n] for n in WEIGHT_ORDER])
```

```python
import functools

import jax
import jax.numpy as jnp
from jax import lax
from jax.experimental import pallas as pl
from jax.experimental.pallas import tpu as pltpu

F32 = jnp.float32
BF16 = jnp.bfloat16
HI = lax.Precision.HIGHEST
MESH = pl.DeviceIdType.MESH

D_MODEL = 1024
HEAD_DIM = 64
RWKV_HEADS = 16
LORA_W, LORA_A, LORA_G = 64, 64, 160
LORA = LORA_W + LORA_A + LORA_G
RKV = 3 * D_MODEL
ATTN_PAIRS = ((128, 1), (512, 4), (2048, 16))
ATTN_BLK = 128
ATTN_HPG = 4
ATTN_WIDTH = 768
GROUP_W = ATTN_HPG * HEAD_DIM
D_FF = 2816
GN_EPS = 64e-5
RMS_EPS = 1e-6
NEG_INF = -1e30
WKV_CHUNK = 64
WKV_HEADS_PER_STEP = 8

ADAM_LR, ADAM_B1, ADAM_B2, ADAM_EPS, ADAM_WD, ADAM_STEP = 0.001, 0.9, 0.999, 1e-08, 0.01, 10

V7X_VMEM_BYTES = 64 << 20
VMEM_TEMP_ALLOWANCE = 20 << 20


def _cparams(sem, block_bytes):
    limit = min(2 * block_bytes + VMEM_TEMP_ALLOWANCE, V7X_VMEM_BYTES - (6 << 20))
    return pltpu.CompilerParams(dimension_semantics=sem, vmem_limit_bytes=int(limit))


def _nbytes(shape, dtype):
    n = 1
    for s in shape:
        n *= s
    return n * jnp.dtype(dtype).itemsize


def _split_bf16(a):
    hi = a.astype(BF16)
    return hi, (a - hi.astype(F32)).astype(BF16)


def _make_dots(prec):
    def one(a, b, ca, cb):
        if prec is not HI:
            a, b = a.astype(BF16), b.astype(BF16)
        return lax.dot_general(a, b, (((ca,), (cb,)), ((), ())), precision=None if prec == "x3" else prec,
                               preferred_element_type=F32)

    def raw(a, b, ca, cb):
        if prec != "x3":
            return one(a, b, ca, cb)
        (ah, al), (bh, bl) = _split_bf16(a), _split_bf16(b)
        return one(ah, bh, ca, cb) + (one(al, bh, ca, cb) + one(ah, bl, ca, cb))

    @jax.custom_vjp
    def nn(a, b):
        return raw(a, b, 1, 0)

    @jax.custom_vjp
    def nt(a, b):
        return raw(a, b, 1, 1)

    @jax.custom_vjp
    def tn(a, b):
        return raw(a, b, 0, 0)

    nn.defvjp(lambda a, b: (raw(a, b, 1, 0), (a, b)),
              lambda res, g: (raw(g, res[1], 1, 1), raw(res[0], g, 0, 0)))
    nt.defvjp(lambda a, b: (raw(a, b, 1, 1), (a, b)),
              lambda res, g: (raw(g, res[1], 1, 0), raw(g, res[0], 0, 0)))
    tn.defvjp(lambda a, b: (raw(a, b, 0, 0), (a, b)),
              lambda res, g: (raw(res[1], g, 1, 1), raw(res[0], g, 1, 0)))
    return nn, nt, tn


def _exact_rhs_dot(x, ones, cx, co):
    hi, lo = _split_bf16(x)
    dims = (((cx,), (co,)), ((), ()))
    return (lax.dot_general(hi, ones, dims, preferred_element_type=F32)
            + lax.dot_general(lo, ones, dims, preferred_element_type=F32))


@jax.custom_vjp
def SEG(x, ones):
    return _exact_rhs_dot(x, ones, 1, 0)


SEG.defvjp(lambda x, ones: (_exact_rhs_dot(x, ones, 1, 0), ones),
           lambda ones, g: (_exact_rhs_dot(g, ones, 1, 1), jnp.zeros_like(ones)))

NN, NT, TN = _make_dots(None)
NN_HI, NT_HI, TN_HI = _make_dots(HI)
NN_X3, NT_X3, TN_X3 = _make_dots("x3")


MM_TILE_M, MM_TILE_N, MM_TILE_K = 1408, 1408, 1536


def _pick(n, cap):
    best = None
    for t in range(128, min(n, cap) + 1, 128):
        if n % t == 0:
            best = t
    return best or n


def matmul(a, b, mode, name, *, add=None, scale=1.0, out_dtype=F32):
    if mode == "nn":
        (M, K), (K2, N) = a.shape, b.shape
    elif mode == "nt":
        (M, K), (N, K2) = a.shape, b.shape
    else:
        (K, M), (K2, N) = a.shape, b.shape
    assert K == K2, (name, a.shape, b.shape)
    tm, tn, tk = _pick(M, MM_TILE_M), _pick(N, MM_TILE_N), _pick(K, MM_TILE_K)
    nk = K // tk
    ca, cb = {"nn": (1, 0), "nt": (1, 1), "tn": (0, 0)}[mode]

    def body(*refs):
        if add is None:
            a_ref, b_ref, o_ref, acc_ref = refs
        else:
            a_ref, b_ref, add_ref, o_ref, acc_ref = refs
        k = pl.program_id(2)

        @pl.when(k == 0)
        def _():
            acc_ref[...] = jnp.zeros_like(acc_ref)

        acc_ref[...] += lax.dot_general(a_ref[...].astype(BF16), b_ref[...].astype(BF16),
                                        (((ca,), (cb,)), ((), ())), preferred_element_type=F32)

        @pl.when(k == nk - 1)
        def _():
            r = acc_ref[...] * scale
            if add is not None:
                r = add_ref[...] + r
            o_ref[...] = r.astype(o_ref.dtype)

    a_spec = (pl.BlockSpec((tk, tm), lambda i, j, k: (k, i)) if mode == "tn"
              else pl.BlockSpec((tm, tk), lambda i, j, k: (i, k)))
    b_spec = (pl.BlockSpec((tn, tk), lambda i, j, k: (j, k)) if mode == "nt"
              else pl.BlockSpec((tk, tn), lambda i, j, k: (k, j)))
    in_specs, args = [a_spec, b_spec], [a, b]
    blk = tm * tk * a.dtype.itemsize + tk * tn * b.dtype.itemsize + tm * tn * 8
    if add is not None:
        in_specs.append(pl.BlockSpec((tm, tn), lambda i, j, k: (i, j)))
        args.append(add)
        blk += tm * tn * 4
    return pl.pallas_call(
        body, name=name, grid=(M // tm, N // tn, nk),
        in_specs=in_specs, out_specs=pl.BlockSpec((tm, tn), lambda i, j, k: (i, j)),
        out_shape=jax.ShapeDtypeStruct((M, N), out_dtype),
        scratch_shapes=[pltpu.VMEM((tm, tn), F32)],
        compiler_params=_cparams(("parallel", "parallel", "arbitrary"), blk),
    )(*args)


def matmul_cs(a, w, mode, name, *, scale=1.0, out_dtype=F32):
    n_blk = N_SHARDS
    if mode == "tn":
        (K, R), Cs = a.shape, w.shape[1] // n_blk
        tm, tk = _pick(R, MM_TILE_M), _pick(K, 1024)
        grid = (R // tm, n_blk, K // tk)
        a_spec = pl.BlockSpec((tk, tm), lambda i, j, k: (k, i))
        w_spec = pl.BlockSpec((tk, Cs), lambda i, j, k: (k, j))
        o_spec = pl.BlockSpec((None, tm, Cs), lambda i, j, k: (j, i, 0))
        out_shape, acc_shape, dims = (n_blk, R, Cs), (tm, Cs), (0, 0)
        blk = tk * tm * a.dtype.itemsize + tk * Cs * w.dtype.itemsize + tm * Cs * 8
    elif mode == "nn":
        (M, R), Cs = a.shape, w.shape[2]
        tm, tk = _pick(M, MM_TILE_M), _pick(R, 1024)
        grid = (M // tm, n_blk, R // tk)
        a_spec = pl.BlockSpec((tm, tk), lambda i, j, k: (i, k))
        w_spec = pl.BlockSpec((None, tk, Cs), lambda i, j, k: (j, k, 0))
        o_spec = pl.BlockSpec((tm, Cs), lambda i, j, k: (i, j))
        out_shape, acc_shape, dims = (M, n_blk * Cs), (tm, Cs), (1, 0)
        blk = tm * tk * a.dtype.itemsize + tk * Cs * w.dtype.itemsize + tm * Cs * 8
    else:
        M, (_, R, Cs) = a.shape[0], w.shape
        tm, tn = _pick(M, MM_TILE_M), _pick(R, MM_TILE_N)
        grid = (M // tm, R // tn, n_blk)
        a_spec = pl.BlockSpec((tm, Cs), lambda i, j, k: (i, k))
        w_spec = pl.BlockSpec((None, tn, Cs), lambda i, j, k: (k, j, 0))
        o_spec = pl.BlockSpec((tm, tn), lambda i, j, k: (i, j))
        out_shape, acc_shape, dims = (M, R), (tm, tn), (1, 1)
        blk = tm * Cs * a.dtype.itemsize + tn * Cs * w.dtype.itemsize + tm * tn * 8
    nk = grid[2]

    def body(a_ref, w_ref, o_ref, acc_ref):
        k = pl.program_id(2)

        @pl.when(k == 0)
        def _():
            acc_ref[...] = jnp.zeros_like(acc_ref)

        acc_ref[...] += lax.dot_general(a_ref[...].astype(BF16), w_ref[...].astype(BF16),
                                        (((dims[0],), (dims[1],)), ((), ())), preferred_element_type=F32)

        @pl.when(k == nk - 1)
        def _():
            o_ref[...] = (acc_ref[...] * scale).astype(o_ref.dtype)

    return pl.pallas_call(
        body, name=name, grid=grid, in_specs=[a_spec, w_spec], out_specs=o_spec,
        out_shape=jax.ShapeDtypeStruct(out_shape, out_dtype), scratch_shapes=[pltpu.VMEM(acc_shape, F32)],
        compiler_params=_cparams(("parallel", "parallel", "arbitrary"), blk),
    )(a, w)


def _row_block(n, width, n_arrays):
    cap = (V7X_VMEM_BYTES // 4) // (2 * 4 * width * n_arrays)
    best = None
    for t in range(16, min(n, cap) + 1, 16):
        if n % t == 0:
            best = t
    return best or n


def placed_map(f, ins, out, *, n_blocks, tb, name):
    def body(*refs):
        refs[-1][...] = f(*[r[...] for r in refs[:-1]]).astype(refs[-1].dtype)

    def spec(fn):
        def index(i):
            x, y, c = _place()
            return fn(i, (c, 2 * x + y)), 0
        return pl.BlockSpec((tb, width), index)

    o_rows, width, o_dtype, o_fn = out
    blk = (sum(a.dtype.itemsize for a, _ in ins) + jnp.dtype(o_dtype).itemsize) * tb * width
    return pl.pallas_call(
        body, name=name, grid=(n_blocks,), in_specs=[spec(fn) for _, fn in ins], out_specs=spec(o_fn),
        out_shape=jax.ShapeDtypeStruct((o_rows, width), o_dtype),
        compiler_params=_cparams(("parallel",), blk),
    )(*[a for a, _ in ins])


def rowmap(f, rows, params, outs, accs=(), *, tb, name):
    rows = [r if isinstance(r, tuple) else (r, r.shape[1], 0) for r in rows]
    S = rows[0][0].shape[0]
    assert S % tb == 0, (name, S, tb)
    n_in, n_out = len(rows) + len(params), len(outs)

    def body(*refs):
        res = f(*[r[...] for r in refs[:n_in]])
        res = res if isinstance(res, (tuple, list)) else (res,)
        o_refs, a_refs = refs[n_in:n_in + n_out], refs[n_in + n_out:]
        for ref, val in zip(o_refs, res[:n_out]):
            ref[...] = val.astype(ref.dtype)
        if a_refs:
            @pl.when(pl.program_id(0) == 0)
            def _():
                for ref in a_refs:
                    ref[...] = jnp.zeros_like(ref)

            for ref, val in zip(a_refs, res[n_out:]):
                ref[...] += val.astype(F32)

    in_specs = [pl.BlockSpec((tb, w), functools.partial(lambda cb, i: (i, cb), cb)) for _, w, cb in rows]
    in_specs += [pl.BlockSpec(p.shape, lambda i: (0, 0)) for p in params]
    out_specs = [pl.BlockSpec((tb, w), lambda i: (i, 0)) for w, _ in outs]
    out_specs += [pl.BlockSpec(tuple(s), lambda i: (0, 0)) for s in accs]
    out_shape = [jax.ShapeDtypeStruct((S, w), dt) for w, dt in outs]
    out_shape += [jax.ShapeDtypeStruct(tuple(s), F32) for s in accs]
    blk = sum(tb * w * a.dtype.itemsize for a, w, _ in rows) + sum(_nbytes(p.shape, p.dtype) for p in params)
    blk += sum(_nbytes((tb, w), dt) for w, dt in outs) + sum(_nbytes(s, F32) for s in accs)
    res = pl.pallas_call(
        body, name=name, grid=(S // tb,), in_specs=in_specs, out_specs=out_specs, out_shape=out_shape,
        compiler_params=_cparams(("arbitrary",) if accs else ("parallel",), blk),
    )(*[r[0] for r in rows], *params)
    return res


def _rms(x, g):
    return x * lax.rsqrt(jnp.mean(x * x, axis=-1, keepdims=True) + RMS_EPS) * g


def _softplus(z):
    return jnp.maximum(z, 0.0) + jnp.log(1.0 + jnp.exp(-jnp.abs(z)))


def _swiglu_act(gu):
    gate, up = gu[:, :D_FF], gu[:, D_FF:]
    return gate * jax.nn.sigmoid(gate) * up


def _rwkv_pre(xrk, xlo, w0, w2p, a0, a2p, g2p, k_k, k_a, seg, seg_t):
    k = xrk[:, D_MODEL:2 * D_MODEL]
    w = -_softplus(-(w0 + NN(jnp.tanh(xlo), w2p))) - 0.5
    log_decay = -jnp.exp(w)
    a = jax.nn.sigmoid(a0 + NN(xlo, a2p))
    g = NN(jax.nn.sigmoid(xlo), g2p)
    kk = k * k_k
    norm = jnp.maximum(jnp.sqrt(SEG(kk * kk, seg)), 1e-12)
    kk = kk * SEG(1.0 / norm, seg_t)
    k_mod = k * (1.0 + (a - 1.0) * k_a)
    return log_decay, k_mod, -kk, kk * a, g


def _rwkv_post(wkv, r, k_mod, v, g, r_k, ln_w, ln_b, seg, seg_t):
    inv_n = 1.0 / HEAD_DIM
    mean = SEG(wkv, seg) * inv_n
    cen = wkv - SEG(mean, seg_t)
    var = SEG(cen * cen, seg) * inv_n
    y = cen * SEG(lax.rsqrt(var + GN_EPS), seg_t) * ln_w + ln_b
    bonus = SEG(SEG(r * k_mod * r_k, seg), seg_t) * v
    return (y + bonus) * g


def _qk_norm(q, k, q_gain, k_gain, seg, seg_t, tile_t):
    def norm(x, gain):
        mean_sq = SEG(x * x, seg) * (1.0 / HEAD_DIM)
        return x * SEG(lax.rsqrt(mean_sq + RMS_EPS), seg_t) * SEG(gain, tile_t)

    return norm(q, q_gain) * (HEAD_DIM ** -0.5), norm(k, k_gain)


def _gate_merge(pgate, pa, pb, b_gate):
    sg = jax.nn.sigmoid(pgate + b_gate)
    return sg[:, :D_MODEL] * pa + sg[:, D_MODEL:] * pb


def _group_combine(o, lse):
    ls = [lse[:, GROUP_W * i:GROUP_W * (i + 1)] for i in range(3)]
    m = jnp.maximum(jnp.maximum(ls[0], ls[1]), ls[2])
    es = [jnp.exp(l - m) for l in ls]
    den = es[0] + es[1] + es[2]
    return jnp.concatenate([o[:, GROUP_W * i:GROUP_W * (i + 1)] * (es[i] / den) for i in range(3)], axis=1)


def _each(f, *xs):
    return tuple(f(*args) for args in zip(*xs))


def _attn_block(q, kc, kp, vc, vp, first):
    qi = lax.broadcasted_iota(jnp.int32, (ATTN_BLK, ATTN_BLK), 0)
    kj = lax.broadcasted_iota(jnp.int32, (ATTN_BLK, ATTN_BLK), 1)
    own, before = kj <= qi, (kj >= qi) & (first < 0.5)
    s_c = _each(lambda a, b: jnp.where(own, NT(a, b), NEG_INF), q, kc)
    s_p = _each(lambda a, b: jnp.where(before, NT(a, b), NEG_INF), q, kp)
    row_max = lambda s: jnp.max(s, axis=-1, keepdims=True)
    row_sum = lambda s: jnp.sum(s, axis=-1, keepdims=True)
    m = _each(lambda c_, p_: jnp.maximum(row_max(c_), row_max(p_)), s_c, s_p)
    e_c, e_p = _each(lambda s, m_: jnp.exp(s - m_), s_c, m), _each(lambda s, m_: jnp.exp(s - m_), s_p, m)
    den = _each(lambda c_, p_: row_sum(c_) + row_sum(p_), e_c, e_p)
    inv = _each(lambda d_: 1.0 / d_, den)
    o = _each(lambda ec, ep, i_, vc_, vp_: (NN(ec, vc_) + NN(ep, vp_)) * i_, e_c, e_p, inv, vc, vp)
    lse = _each(lambda m_, d_: jnp.broadcast_to(m_ + jnp.log(d_), (ATTN_BLK, HEAD_DIM)), m, den)
    return o, lse


TRI_SEED = 8


def _tri_inverse(n):
    c = n[0].shape[0]
    row = lax.broadcasted_iota(jnp.int32, (c, c), 0)
    col = lax.broadcasted_iota(jnp.int32, (c, c), 1)
    same_block = lambda size: (row >> (size.bit_length() - 1)) == (col >> (size.bit_length() - 1))
    seed = same_block(TRI_SEED)
    p = _each(lambda m: jnp.where(seed, m, 0.0), n)
    t, span = _each(lambda m: (row == col).astype(F32) + m, p), 2
    while span < TRI_SEED:
        p = _each(NN_X3, p, p)
        t = _each(lambda t_, p_: t_ + NN_X3(t_, p_), t, p)
        span *= 2
    size = TRI_SEED
    while size < c:
        joins = same_block(2 * size) & jnp.logical_not(same_block(size))
        t = _each(lambda t_, m: t_ + NN_X3(NN_X3(t_, jnp.where(joins, m, 0.0)), t_), t, n)
        size *= 2
    return t


@jax.custom_vjp
def _tri_solve(n, rhs):
    return _each(NN, _tri_inverse(n), rhs)


def _tri_solve_fwd(n, rhs):
    t = _tri_inverse(n)
    x = _each(NN, t, rhs)
    return x, (t, x)


def _tri_solve_bwd(res, dx):
    t, x = res
    drhs = _each(TN, t, dx)
    return _each(NT, drhs, x), drhs


_tri_solve.defvjp(_tri_solve_fwd, _tri_solve_bwd)


def _lower_ones(c):
    row = lax.broadcasted_iota(jnp.int32, (c, c), 0)
    col = lax.broadcasted_iota(jnp.int32, (c, c), 1)
    return (row >= col).astype(BF16)


def _ones_dot(ones, x, contract):
    hi, lo = _split_bf16(x)
    dims = (((contract,), (0,)), ((), ()))
    return (lax.dot_general(ones, hi, dims, preferred_element_type=F32)
            + lax.dot_general(ones, lo, dims, preferred_element_type=F32))


@jax.custom_vjp
def _cumsum_rows(x):
    return _ones_dot(_lower_ones(x.shape[0]), x, 1)


_cumsum_rows.defvjp(lambda x: (_ones_dot(_lower_ones(x.shape[0]), x, 1), None),
                    lambda _, g: (_ones_dot(_lower_ones(g.shape[0]), g, 0),))


def _wkv_chunk(s0, r, lw, k, v, a, b):
    c = r[0].shape[0]
    row = lax.broadcasted_iota(jnp.int32, (c, c), 0)
    col = lax.broadcasted_iota(jnp.int32, (c, c), 1)
    strict, incl = row > col, row >= col
    cat = lambda p, q: jnp.concatenate([p, q], axis=0)
    cum = _each(_cumsum_rows, lw)
    e_neg = _each(lambda c_: jnp.exp(-c_), cum)
    ar = _each(lambda a_, r_, c_, l_: cat(a_ * jnp.exp(c_ - l_), r_ * jnp.exp(c_)), a, r, cum, lw)
    b_t, k_t = _each(jnp.multiply, b, e_neg), _each(jnp.multiply, k, e_neg)
    p_b, p_k, p_s = _each(NT, ar, b_t), _each(NT, ar, k_t), _each(NT, ar, s0)
    n_ab = _each(lambda p: jnp.where(strict, p[:c], 0.0), p_b)
    m_rb = _each(lambda p: jnp.where(incl, p[c:], 0.0), p_b)
    n_ak = _each(lambda p: jnp.where(strict, p[:c], 0.0), p_k)
    m_rk = _each(lambda p: jnp.where(incl, p[c:], 0.0), p_k)
    u = _tri_solve(n_ab, _each(lambda p, n_, v_: p[:c] + NN(n_, v_), p_s, n_ak, v))
    y = _each(lambda p, mb, u_, mk, v_: p[c:] + NN(mb, u_) + NN(mk, v_), p_s, m_rb, u, m_rk, v)
    g_end = _each(lambda l_: jnp.exp(jnp.sum(l_, axis=0, keepdims=True)), lw)
    s1 = _each(lambda s_, g_, u_, v_, b_, k_: s_ * g_ + TN(cat(u_, v_), cat(b_, k_) * g_),
               s0, g_end, u, v, b_t, k_t)
    return y, s1


def _adamw(w, g, m, v):
    m = ADAM_B1 * m + (1.0 - ADAM_B1) * g
    v = ADAM_B2 * v + (1.0 - ADAM_B2) * jnp.square(g)
    m_hat = m / (1.0 - ADAM_B1 ** ADAM_STEP)
    v_hat = v / (1.0 - ADAM_B2 ** ADAM_STEP)
    delta = -ADAM_LR * (m_hat / (jnp.sqrt(v_hat) + ADAM_EPS) + ADAM_WD * w)
    return delta, m, v


def token_shift_fwd(p, mu, *, tb, name):
    S, W = p.shape
    hb = tb // 8

    def body(p_ref, halo_ref, mu_ref, o_ref):
        i = pl.program_id(0)
        x = p_ref[...]
        before = halo_ref[7:8, :] * (i > 0).astype(F32)
        row = lax.broadcasted_iota(jnp.int32, (tb, W), 0)
        prev = jnp.where(row == 0, before, pltpu.roll(x, 1, 0))
        o_ref[...] = x + (prev - x) * mu_ref[...]

    blk = (2 * tb + 8) * W * 4
    return pl.pallas_call(
        body, name=name, grid=(S // tb,),
        in_specs=[pl.BlockSpec((tb, W), lambda i: (i, 0)),
                  pl.BlockSpec((8, W), lambda i: (jnp.maximum(i * hb - 1, 0), 0)),
                  pl.BlockSpec((1, W), lambda i: (0, 0))],
        out_specs=pl.BlockSpec((tb, W), lambda i: (i, 0)),
        out_shape=jax.ShapeDtypeStruct((S, W), F32),
        compiler_params=_cparams(("parallel",), blk),
    )(p, p, mu)


def token_shift_bwd(dxs, p, mu, *, tb, name):
    S, W = p.shape
    hb, nb = tb // 8, S // tb

    def body(d_ref, dnext_ref, p_ref, halo_ref, mu_ref, dp_ref, dmu_ref):
        i = pl.program_id(0)
        d, x, mu_v = d_ref[...], p_ref[...], mu_ref[...]
        row = lax.broadcasted_iota(jnp.int32, (tb, W), 0)
        before = halo_ref[7:8, :] * (i > 0).astype(F32)
        prev = jnp.where(row == 0, before, pltpu.roll(x, 1, 0))
        t = d * mu_v
        after = dnext_ref[0:1, :] * mu_v * (i < nb - 1).astype(F32)
        nxt = jnp.where(row == tb - 1, after, pltpu.roll(t, tb - 1, 0))
        dp_ref[...] = (d - t + nxt).astype(dp_ref.dtype)

        @pl.when(i == 0)
        def _():
            dmu_ref[...] = jnp.zeros_like(dmu_ref)

        dmu_ref[...] += jnp.sum(d * (prev - x), axis=0, keepdims=True)

    blk = (3 * tb + 16) * W * 4
    return pl.pallas_call(
        body, name=name, grid=(nb,),
        in_specs=[pl.BlockSpec((tb, W), lambda i: (i, 0)),
                  pl.BlockSpec((8, W), lambda i: (jnp.minimum((i + 1) * hb, S // 8 - 1), 0)),
                  pl.BlockSpec((tb, W), lambda i: (i, 0)),
                  pl.BlockSpec((8, W), lambda i: (jnp.maximum(i * hb - 1, 0), 0)),
                  pl.BlockSpec((1, W), lambda i: (0, 0))],
        out_specs=[pl.BlockSpec((tb, W), lambda i: (i, 0)), pl.BlockSpec((1, W), lambda i: (0, 0))],
        out_shape=[jax.ShapeDtypeStruct((S, W), BF16), jax.ShapeDtypeStruct((1, W), F32)],
        compiler_params=_cparams(("arbitrary",), blk),
    )(dxs, dxs, p, p, mu)


def _head_cols(h):
    return pl.ds(h * HEAD_DIM, HEAD_DIM)


def wkv_fwd(xs_rk, lw, k, a, b):
    S = lw.shape[0]
    C, nc, G, N = WKV_CHUNK, S // WKV_CHUNK, WKV_HEADS_PER_STEP, HEAD_DIM

    def body(r_ref, lw_ref, k_ref, v_ref, a_ref, b_ref, y_ref, st_ref, state):
        @pl.when(pl.program_id(1) == 0)
        def _():
            state[...] = jnp.zeros_like(state)

        heads = lambda ref: tuple(ref[:, _head_cols(h)] for h in range(G))
        s0 = tuple(state[h] for h in range(G))
        y, s1 = _wkv_chunk(s0, heads(r_ref), heads(lw_ref), heads(k_ref), heads(v_ref), heads(a_ref),
                           heads(b_ref))
        for h in range(G):
            st_ref[h] = s0[h]
            y_ref[:, _head_cols(h)] = y[h]
            state[h] = s1[h]

    W = G * N
    seq = lambda j: pl.BlockSpec((C, W), functools.partial(lambda j, g, c: (c, j + g), j))
    per = D_MODEL // W
    return pl.pallas_call(
        body, name="wkv_fwd", grid=(RWKV_HEADS // G, nc),
        in_specs=[seq(0), seq(0), seq(0), seq(2 * per), seq(0), seq(0)],
        out_specs=[seq(0), pl.BlockSpec((None, G, N, N), lambda g, c: (c, g, 0, 0))],
        out_shape=[jax.ShapeDtypeStruct((S, D_MODEL), F32), jax.ShapeDtypeStruct((nc, RWKV_HEADS, N, N), F32)],
        scratch_shapes=[pltpu.VMEM((G, N, N), F32)],
        compiler_params=_cparams(("parallel", "arbitrary"), 8 * C * W * 4 + 2 * G * N * N * 4),
    )(xs_rk, lw, k, xs_rk, a, b)


def wkv_bwd(xs_rk, lw, k, a, b, states, dy):
    S = lw.shape[0]
    C, nc, G, N = WKV_CHUNK, S // WKV_CHUNK, WKV_HEADS_PER_STEP, HEAD_DIM

    def body(r_ref, lw_ref, k_ref, v_ref, a_ref, b_ref, st_ref, dy_ref,
             dr_ref, dlw_ref, dk_ref, dv_ref, da_ref, db_ref, dstate):
        @pl.when(pl.program_id(1) == 0)
        def _():
            dstate[...] = jnp.zeros_like(dstate)

        heads = lambda ref: tuple(ref[:, _head_cols(h)] for h in range(G))
        _, pull = jax.vjp(_wkv_chunk, tuple(st_ref[h] for h in range(G)), heads(r_ref), heads(lw_ref),
                          heads(k_ref), heads(v_ref), heads(a_ref), heads(b_ref))
        ds0, *grads = pull((heads(dy_ref), tuple(dstate[h] for h in range(G))))
        for h in range(G):
            dstate[h] = ds0[h]
            for ref, grad in zip((dr_ref, dlw_ref, dk_ref, dv_ref, da_ref, db_ref), grads):
                ref[:, _head_cols(h)] = grad[h]

    W = G * N
    seq = lambda j: pl.BlockSpec((C, W), functools.partial(lambda j, g, c: (nc - 1 - c, j + g), j))
    per = D_MODEL // W
    st = pl.BlockSpec((None, G, N, N), lambda g, c: (nc - 1 - c, g, 0, 0))
    return pl.pallas_call(
        body, name="wkv_bwd", grid=(RWKV_HEADS // G, nc),
        in_specs=[seq(0), seq(0), seq(0), seq(2 * per), seq(0), seq(0), st, seq(0)],
        out_specs=[seq(0)] * 6, out_shape=[jax.ShapeDtypeStruct((S, D_MODEL), F32)] * 6,
        scratch_shapes=[pltpu.VMEM((G, N, N), F32)],
        compiler_params=_cparams(("parallel", "arbitrary"), 14 * C * W * 4 + 2 * G * N * N * 4),
    )(xs_rk, lw, k, xs_rk, a, b, states, dy)


def _first_flag(i, seq_len):
    per_group = seq_len // ATTN_BLK
    g = i // per_group
    per_seq = [seq_len // d // ATTN_BLK for _, d in ATTN_PAIRS]
    n = jnp.where(g == 0, per_seq[0], jnp.where(g == 1, per_seq[1], per_seq[2]))
    return (lax.rem(i, n) == 0).astype(F32)


def attn_fwd(q, k, v, seq_len):
    R, N = q.shape
    nb = R // ATTN_BLK

    def body(q_ref, kc_ref, kp_ref, vc_ref, vp_ref, o_ref, lse_ref):
        first = _first_flag(pl.program_id(0), seq_len)
        heads = lambda ref: tuple(ref[:, _head_cols(h)] for h in range(ATTN_HPG))
        o, lse = _attn_block(heads(q_ref), heads(kc_ref), heads(kp_ref), heads(vc_ref), heads(vp_ref), first)
        for h in range(ATTN_HPG):
            o_ref[:, _head_cols(h)] = o[h]
            lse_ref[:, _head_cols(h)] = lse[h]

    cur = pl.BlockSpec((ATTN_BLK, N), lambda i: (i, 0))
    prv = pl.BlockSpec((ATTN_BLK, N), lambda i: (jnp.maximum(i - 1, 0), 0))
    return pl.pallas_call(
        body, name="attn_fwd", grid=(nb,), in_specs=[cur, cur, prv, cur, prv],
        out_specs=[cur, cur], out_shape=[jax.ShapeDtypeStruct((R, N), F32)] * 2,
        compiler_params=_cparams(("parallel",), 7 * ATTN_BLK * N * 4),
    )(q, k, k, v, v)


def attn_bwd(q, k, v, do, dlse, seq_len):
    R, N = q.shape
    nb = R // ATTN_BLK

    def body(q_ref, kc_ref, kp_ref, vc_ref, vp_ref, do_ref, dl_ref, dq_ref, dk_ref, dv_ref, carry_k, carry_v):
        step = pl.program_id(0)
        first = _first_flag(nb - 1 - step, seq_len)

        @pl.when(step == 0)
        def _():
            carry_k[...] = jnp.zeros_like(carry_k)
            carry_v[...] = jnp.zeros_like(carry_v)

        heads = lambda ref: tuple(ref[:, _head_cols(h)] for h in range(ATTN_HPG))
        _, pull = jax.vjp(functools.partial(_attn_block, first=first), heads(q_ref), heads(kc_ref), heads(kp_ref),
                          heads(vc_ref), heads(vp_ref))
        dq, dkc, dkp, dvc, dvp = pull((heads(do_ref), heads(dl_ref)))
        old_k, old_v = heads(carry_k), heads(carry_v)
        for h in range(ATTN_HPG):
            cols = _head_cols(h)
            dq_ref[:, cols] = dq[h]
            dk_ref[:, cols] = dkc[h] + old_k[h]
            dv_ref[:, cols] = dvc[h] + old_v[h]
            carry_k[:, cols] = dkp[h]
            carry_v[:, cols] = dvp[h]

    cur = pl.BlockSpec((ATTN_BLK, N), lambda i: (nb - 1 - i, 0))
    prv = pl.BlockSpec((ATTN_BLK, N), lambda i: (jnp.maximum(nb - 2 - i, 0), 0))
    return pl.pallas_call(
        body, name="attn_bwd", grid=(nb,), in_specs=[cur, cur, prv, cur, prv, cur, cur],
        out_specs=[cur, cur, cur], out_shape=[jax.ShapeDtypeStruct((R, N), F32)] * 3,
        scratch_shapes=[pltpu.VMEM((ATTN_BLK, N), F32)] * 2,
        compiler_params=_cparams(("arbitrary",), 12 * ATTN_BLK * N * 4),
    )(q, k, k, v, v, do, dlse)


def to_subsequences(t):
    S = t.shape[0]
    parts = []
    for gi, (_, d) in enumerate(ATTN_PAIRS):
        tg = t[:, GROUP_W * gi:GROUP_W * (gi + 1)].reshape(S // d, d, GROUP_W)
        parts.append(tg.transpose(1, 0, 2).reshape(S, GROUP_W))
    return jnp.concatenate(parts, axis=0)


def from_subsequences(u, S):
    parts = []
    for gi, (_, d) in enumerate(ATTN_PAIRS):
        ug = u[S * gi:S * (gi + 1)].reshape(d, S // d, GROUP_W)
        parts.append(ug.transpose(1, 0, 2).reshape(S, GROUP_W))
    return jnp.concatenate(parts, axis=1)


def _ffn_fwd(x, norm, w_in, w_out, tag):
    h = rowmap(_rms, [x], [norm], [(D_MODEL, BF16)], tb=512, name=tag + "_norm")[0]
    gu = matmul_cs(h, w_in, "nn", tag + "_in")
    act = rowmap(_swiglu_act, [gu], [], [(D_FF, BF16)], tb=256, name=tag + "_act")[0]
    y = matmul(act, w_out, "nn", tag + "_out", add=x, scale=0.5)
    return y, (x, h, gu, act)


def _ffn_bwd(dy, saved, norm, w_in, w_out, tag):
    x, h, gu, act = saved
    dact = matmul(dy, w_out, "nt", tag + "_dact", scale=0.5)
    dw_out = matmul(act, dy, "tn", tag + "_dwout", scale=0.5)

    def act_bwd(gu_b, dact_b):
        return jax.vjp(_swiglu_act, gu_b)[1](dact_b)[0]

    dgu = rowmap(act_bwd, [gu, dact], [], [(2 * D_FF, BF16)], tb=256, name=tag + "_dgu")[0]
    dh = matmul_cs(dgu, w_in, "nt", tag + "_dh")
    dw_in = matmul_cs(h, dgu, "tn", tag + "_dwin")

    def norm_bwd(x_b, dh_b, dy_b, g):
        dx, dg = jax.vjp(_rms, x_b, g)[1](dh_b)
        return dy_b + dx, dg

    dx, dnorm = rowmap(norm_bwd, [x, dh, dy], [norm], [(D_MODEL, F32)], [(1, D_MODEL)], tb=256,
                       name=tag + "_dnorm")
    return dx, dnorm, dw_in, dw_out


def layer_step(x, tgt, W, P):
    S = x.shape[0]
    head_of = lambda n: jnp.arange(n)[:, None] // HEAD_DIM == jnp.arange(n // HEAD_DIM)[None, :]
    seg, seg_a = head_of(D_MODEL).astype(BF16), head_of(ATTN_WIDTH).astype(BF16)
    seg_t, seg_a_t = seg.T, seg_a.T
    tile_t = (jnp.arange(HEAD_DIM)[:, None] == jnp.arange(ATTN_WIDTH)[None, :] % HEAD_DIM).astype(BF16)
    qk_params = [P["attn_q_norm"], P["attn_k_norm"], seg_a, seg_a_t, tile_t]
    w_rkv, w_lora = W["w_in"][:, :RKV], W["w_in"][:, RKV:RKV + LORA]
    w_qkv = W["w_in"][:, RKV + LORA:RKV + LORA + 3 * ATTN_WIDTH]
    w_gate = W["w_in"][:, RKV + LORA + 3 * ATTN_WIDTH:]
    mu_rk, mu_lo = P["rwkv_mu"][:, :RKV], P["rwkv_mu"][:, RKV:]
    zeros = lambda n: jnp.zeros((n, D_MODEL), F32)
    w2p = jnp.concatenate([W["rwkv_w2"], zeros(LORA - LORA_W)], axis=0)
    a2p = jnp.concatenate([zeros(LORA_W), W["rwkv_a2"], zeros(LORA_G)], axis=0)
    g2p = jnp.concatenate([zeros(LORA_W + LORA_A), W["rwkv_g2"]], axis=0)
    pre_params = [P["rwkv_w0"], w2p, P["rwkv_a0"], a2p, g2p, P["rwkv_k_k"], P["rwkv_k_a"], seg, seg_t]
    post_params = [P["rwkv_r_k"], P["rwkv_ln_w"], P["rwkv_ln_b"], seg, seg_t]
    col = lambda arr, j: (arr, D_MODEL, j)

    x1, ffn1_saved = _ffn_fwd(x, P["ffn1_norm"], W["ffn1_w_in"], W["ffn1_w_out"], "ffn1")
    h = rowmap(_rms, [x1], [P["mix_norm"]], [(D_MODEL, BF16)], tb=512, name="mix_norm")[0]
    p_rk = matmul(h, w_rkv, "nn", "proj_rkv")
    p_lo = matmul(h, w_lora, "nn", "proj_lora")
    p_qkv = matmul(h, w_qkv, "nn", "proj_qkv")
    p_gate = matmul(h, w_gate, "nn", "proj_gate")
    xs_rk = token_shift_fwd(p_rk, mu_rk, tb=256, name="shift_rk")
    xs_lo = token_shift_fwd(p_lo, mu_lo, tb=256, name="shift_lora")
    lw, k_mod, a_neg, b_kk, g = rowmap(
        _rwkv_pre, [xs_rk, xs_lo], pre_params, [(D_MODEL, F32)] * 5, tb=256, name="rwkv_pre")
    wkv, states = wkv_fwd(xs_rk, lw, k_mod, a_neg, b_kk)
    post_rows = [wkv, col(xs_rk, 0), k_mod, col(xs_rk, 2), g]
    y_a = rowmap(_rwkv_post, post_rows, post_params, [(D_MODEL, BF16)], tb=256, name="rwkv_post")[0]

    qk_rows = [(p_qkv, ATTN_WIDTH, 0), (p_qkv, ATTN_WIDTH, 1)]
    qn, kn = rowmap(_qk_norm, qk_rows, qk_params, [(ATTN_WIDTH, F32)] * 2, tb=256, name="qk_norm")
    q_s, k_s, v_s = to_subsequences(qn), to_subsequences(kn), to_subsequences(p_qkv[:, 2 * ATTN_WIDTH:])
    o_s, lse_s = attn_fwd(q_s, k_s, v_s, S)
    o, lse = from_subsequences(o_s, S), from_subsequences(lse_s, S)
    y_b = rowmap(_group_combine, [o, lse], [], [(ATTN_WIDTH, BF16)], tb=512, name="attn_combine")[0]

    pa = matmul(y_a, W["w_proj_rwkv"], "nn", "proj_a")
    pb = matmul(y_b, W["w_proj_attn"], "nn", "proj_b")
    merged = rowmap(_gate_merge, [p_gate, pa, pb], [P["b_gate"]], [(D_MODEL, BF16)], tb=256, name="merge")[0]
    x2 = matmul(merged, W["w_out"], "nn", "mix_out", add=x1)
    x3, ffn2_saved = _ffn_fwd(x2, P["ffn2_norm"], W["ffn2_w_in"], W["ffn2_w_out"], "ffn2")

    def loss_head(y_b_, t_b):
        err = y_b_ - t_b
        return err * (1.0 / D_MODEL), (0.5 / D_MODEL) * jnp.sum(err * err, axis=0, keepdims=True)

    dx3, loss_cols = rowmap(loss_head, [x3, tgt], [], [(D_MODEL, F32)], [(1, D_MODEL)], tb=512, name="loss")

    gW, gP = {}, {}
    dx2, gP["ffn2_norm"], gW["ffn2_w_in"], gW["ffn2_w_out"] = _ffn_bwd(
        dx3, ffn2_saved, P["ffn2_norm"], W["ffn2_w_in"], W["ffn2_w_out"], "ffn2")

    dmerged = matmul(dx2, W["w_out"], "nt", "d_merged")
    gW["w_out"] = matmul(merged, dx2, "tn", "dw_out")

    def merge_bwd(pg, pa_b, pb_b, dm, bg):
        return jax.vjp(_gate_merge, pg, pa_b, pb_b, bg)[1](dm)

    dp_gate, dpa, dpb, gP["b_gate"] = rowmap(
        merge_bwd, [p_gate, pa, pb, dmerged], [P["b_gate"]],
        [(2 * D_MODEL, BF16), (D_MODEL, BF16), (D_MODEL, BF16)], [(1, 2 * D_MODEL)], tb=256, name="merge_bwd")
    dy_a = matmul(dpa, W["w_proj_rwkv"], "nt", "d_ya")
    gW["w_proj_rwkv"] = matmul(y_a, dpa, "tn", "dw_proj_a")
    dy_b = matmul(dpb, W["w_proj_attn"], "nt", "d_yb")
    gW["w_proj_attn"] = matmul(y_b, dpb, "tn", "dw_proj_b")

    def combine_bwd(o_b, l_b, d_b):
        return jax.vjp(_group_combine, o_b, l_b)[1](d_b)

    do, dlse = rowmap(combine_bwd, [o, lse, dy_b], [], [(ATTN_WIDTH, F32)] * 2, tb=256, name="attn_combine_bwd")
    dq_s, dk_s, dv_s = attn_bwd(q_s, k_s, v_s, to_subsequences(do), to_subsequences(dlse), S)

    def qk_norm_bwd(q_b, k_b, dqn_b, dkn_b, dv_b, qg, kg, sg, sgt, tl):
        f = lambda *a: _qk_norm(*a, sg, sgt, tl)
        dq, dk, dqg, dkg = jax.vjp(f, q_b, k_b, qg, kg)[1]((dqn_b, dkn_b))
        return jnp.concatenate([dq, dk, dv_b], axis=1), dqg, dkg

    dp_qkv, gP["attn_q_norm"], gP["attn_k_norm"] = rowmap(
        qk_norm_bwd, qk_rows + [from_subsequences(t, S) for t in (dq_s, dk_s, dv_s)], qk_params,
        [(3 * ATTN_WIDTH, BF16)], [(1, HEAD_DIM)] * 2, tb=256, name="qk_norm_bwd")

    def post_bwd(wkv_b, r_b, k_b, v_b, g_b, d_b, r_k, ln_w, ln_b, sg, sgt):
        f = lambda *a: _rwkv_post(*a, sg, sgt)
        return jax.vjp(f, wkv_b, r_b, k_b, v_b, g_b, r_k, ln_w, ln_b)[1](d_b)

    dwkv, dr_p, dk_p, dv_p, dg, gP["rwkv_r_k"], gP["rwkv_ln_w"], gP["rwkv_ln_b"] = rowmap(
        post_bwd, post_rows + [dy_a], post_params, [(D_MODEL, F32)] * 5, [(1, D_MODEL)] * 3, tb=128,
        name="rwkv_post_bwd")
    dr_w, dlw, dk_w, dv_w, da_neg, db_kk = wkv_bwd(xs_rk, lw, k_mod, a_neg, b_kk, states, dwkv)

    def pre_bwd(xrk_b, xlo_b, dlw_b, dkw_b, dkp_b, da_b, db_b, dg_b, drp_b, drw_b, dvp_b, dvw_b,
                w0, w2, a0, a2, g2, k_k, k_a, sg, sgt):
        f = lambda *a: _rwkv_pre(*a, sg, sgt)
        pull = jax.vjp(f, xrk_b, xlo_b, w0, w2, a0, a2, g2, k_k, k_a)[1]
        dxrk, dxlo, *dpar = pull((dlw_b, dkw_b + dkp_b, da_b, db_b, dg_b))
        direct = jnp.concatenate([drp_b + drw_b, jnp.zeros_like(drp_b), dvp_b + dvw_b], axis=1)
        return (dxrk + direct, dxlo, *dpar)

    pre_rows = [xs_rk, xs_lo, dlw, dk_w, dk_p, da_neg, db_kk, dg, dr_p, dr_w, dv_p, dv_w]
    dxs_rk, dxs_lo, gP["rwkv_w0"], dw2p, gP["rwkv_a0"], da2p, dg2p, gP["rwkv_k_k"], gP["rwkv_k_a"] = rowmap(
        pre_bwd, pre_rows, pre_params, [(RKV, F32), (LORA, F32)],
        [(1, D_MODEL), (LORA, D_MODEL), (1, D_MODEL), (LORA, D_MODEL), (LORA, D_MODEL), (1, D_MODEL), (1, D_MODEL)],
        tb=128, name="rwkv_pre_bwd")
    gW["rwkv_w2"] = dw2p[:LORA_W]
    gW["rwkv_a2"] = da2p[LORA_W:LORA_W + LORA_A]
    gW["rwkv_g2"] = dg2p[LORA_W + LORA_A:]
    dp_rk, dmu_rk = token_shift_bwd(dxs_rk, p_rk, mu_rk, tb=256, name="shift_rk_bwd")
    dp_lo, dmu_lo = token_shift_bwd(dxs_lo, p_lo, mu_lo, tb=256, name="shift_lora_bwd")
    gP["rwkv_mu"] = jnp.concatenate([dmu_rk, dmu_lo], axis=1)

    dh = matmul(dp_rk, w_rkv, "nt", "dh_rkv")
    dh = matmul(dp_lo, w_lora, "nt", "dh_lora", add=dh)
    dh = matmul(dp_qkv, w_qkv, "nt", "dh_qkv", add=dh)
    dh = matmul(dp_gate, w_gate, "nt", "dh_gate", add=dh)
    gW["w_in"] = jnp.concatenate([
        matmul(h, dp_rk, "tn", "dw_rkv"), matmul(h, dp_lo, "tn", "dw_lora"),
        matmul(h, dp_qkv, "tn", "dw_qkv"), matmul(h, dp_gate, "tn", "dw_gate")], axis=1)

    def norm_bwd(x_b, dh_b, dy_b, gn):
        dx, dgn = jax.vjp(_rms, x_b, gn)[1](dh_b)
        return dy_b + dx, dgn

    dx1, gP["mix_norm"] = rowmap(norm_bwd, [x1, dh, dx2], [P["mix_norm"]], [(D_MODEL, F32)], [(1, D_MODEL)],
                                 tb=256, name="mix_norm_bwd")
    dx, gP["ffn1_norm"], gW["ffn1_w_in"], gW["ffn1_w_out"] = _ffn_bwd(
        dx1, ffn1_saved, P["ffn1_norm"], W["ffn1_w_in"], W["ffn1_w_out"], "ffn1")
    return loss_cols, dx, gW, gP


N_SHARDS = 4
BIG = (("ffn1_w_in", (D_MODEL, 2 * D_FF), 1), ("ffn1_w_out", (D_FF, D_MODEL), 0),
       ("w_in", (D_MODEL, 7712), 1), ("rwkv_w2", (LORA_W, D_MODEL), 1), ("rwkv_a2", (LORA_A, D_MODEL), 1),
       ("rwkv_g2", (LORA_G, D_MODEL), 1), ("w_proj_rwkv", (D_MODEL, D_MODEL), 0),
       ("w_proj_attn", (ATTN_WIDTH, D_MODEL), 1), ("w_out", (D_MODEL, D_MODEL), 0),
       ("ffn2_w_in", (D_MODEL, 2 * D_FF), 1), ("ffn2_w_out", (D_FF, D_MODEL), 0))
SMALL = (("ffn1_norm", 1024), ("mix_norm", 1024), ("b_gate", 2048), ("rwkv_mu", 3360), ("rwkv_w0", 1024),
         ("rwkv_a0", 1024), ("rwkv_k_k", 1024), ("rwkv_k_a", 1024), ("rwkv_r_k", 1024), ("rwkv_ln_w", 1024),
         ("rwkv_ln_b", 1024), ("attn_q_norm", 64), ("attn_k_norm", 64), ("ffn2_norm", 1024))
WEIGHT_ORDER = ("ffn1_norm", "ffn1_w_in", "ffn1_w_out", "mix_norm", "w_in", "b_gate", "rwkv_mu", "rwkv_w0",
                "rwkv_w2", "rwkv_a0", "rwkv_a2", "rwkv_g2", "rwkv_k_k", "rwkv_k_a", "rwkv_r_k", "rwkv_ln_w",
                "rwkv_ln_b", "attn_q_norm", "attn_k_norm", "w_proj_rwkv", "w_proj_attn", "w_out", "ffn2_norm",
                "ffn2_w_in", "ffn2_w_out")


LORA_PARTS = ("rwkv_w2", "rwkv_a2", "rwkv_g2")
BLOCK_MAJOR = ("ffn1_w_in", "ffn2_w_in")
SMALL_USED = D_MODEL + sum(n for _, n in SMALL)
SMALL_W = -(-SMALL_USED // 128) * 128


def _travel():
    out = {}
    for name, shape, axis in BIG:
        if name == LORA_PARTS[0]:
            out["lora"] = ((LORA, D_MODEL), 1)
        elif name not in LORA_PARTS:
            out[name] = (shape, axis)
    return out


def local_blocks(vals):
    out = {n: vals[n] for n in _travel() if n != "lora"}
    out["lora"] = jnp.concatenate([vals[n] for n in LORA_PARTS], axis=0)
    return out


def split_lora(t):
    return {"rwkv_w2": t[:LORA_W], "rwkv_a2": t[LORA_W:LORA_W + LORA_A], "rwkv_g2": t[LORA_W + LORA_A:]}


def blocks_to_full(name, blocks):
    shape, axis = _travel()[name]
    if name in BLOCK_MAJOR:
        return blocks
    if axis == 0:
        return blocks.reshape(shape)
    return blocks.transpose(1, 0, 2).reshape(shape)


def full_to_blocks(name, full):
    shape, axis = _travel()[name]
    if name in BLOCK_MAJOR:
        return full
    if axis == 0:
        return full.reshape(N_SHARDS, shape[0] // N_SHARDS, shape[1])
    return full.reshape(shape[0], N_SHARDS, shape[1] // N_SHARDS).transpose(1, 0, 2)


def pack_small(vals, head):
    parts = [head] + [vals[name].reshape(1, n) for name, n in SMALL]
    parts.append(jnp.zeros((1, SMALL_W - SMALL_USED), F32))
    return jnp.concatenate(parts, axis=1)


def unpack_small(vec, shapes):
    out, off = {}, D_MODEL
    for name, n in SMALL:
        out[name] = vec[:, off:off + n].reshape(shapes[name])
        off += n
    return out


def _place():
    return lax.axis_index("x"), lax.axis_index("y"), lax.axis_index("c")


def _other_chips(x, y):
    return [(1 - x, y), (x, 1 - y), (1 - x, 1 - y)]


def _remote(src, dst, send_sem, recv_sem, device):
    return pltpu.make_async_remote_copy(src_ref=src, dst_ref=dst, send_sem=send_sem, recv_sem=recv_sem,
                                        device_id=device, device_id_type=MESH)


def _half(ref, who):
    hr = ref.shape[-2] // 2
    rows = pl.ds(pl.multiple_of(who * hr, 8), hr)
    return ref.at[rows] if len(ref.shape) == 2 else ref.at[:, rows]


HBM_REF = pl.BlockSpec(memory_space=pl.ANY)
COMM_PARAMS = dict(compiler_params=pltpu.CompilerParams(has_side_effects=True))


def gather_weights(blocks):
    n = len(blocks)

    def body(*refs):
        ins, outs = refs[:n], refs[n:2 * n]
        ici_send, ici_recv, d2d_send, d2d_recv = refs[2 * n:]
        x, y, c = _place()
        me, sibling, chips = 2 * x + y, (x, y, 1 - c), _other_chips(x, y)
        first = [_remote(_half(ins[t], c), _half(outs[t].at[me], c), ici_send.at[k, t], ici_recv.at[k, t],
                         (px, py, c)) for k, (px, py) in enumerate(chips) for t in range(n)]
        for cp in first:
            cp.start()
        passed = []
        for k, (px, py) in enumerate(chips):
            for t in range(n):
                landed = _half(outs[t].at[2 * px + py], c)
                _remote(landed, landed, ici_send.at[k, t], ici_recv.at[k, t], (px, py, c)).wait_recv()
                cp = _remote(landed, landed, d2d_send.at[k, t], d2d_recv.at[k, t], sibling)
                cp.start()
                passed.append(cp)
        for k, (px, py) in enumerate(chips):
            for t in range(n):
                other = _half(outs[t].at[2 * px + py], 1 - c)
                _remote(other, other, d2d_send.at[k, t], d2d_recv.at[k, t], sibling).wait_recv()
        for cp in first + passed:
            cp.wait_send()

    res = pl.pallas_call(
        body, name="gather_weights", in_specs=[HBM_REF] * n, out_specs=[HBM_REF] * n,
        out_shape=[jax.ShapeDtypeStruct((N_SHARDS,) + b.shape, b.dtype) for b in blocks],
        scratch_shapes=[pltpu.SemaphoreType.DMA((3, n))] * 4, **COMM_PARAMS)(*blocks)
    me = 2 * lax.axis_index("x") + lax.axis_index("y")
    return [lax.dynamic_update_slice(g, b[None], (me, 0, 0)) for g, b in zip(res, blocks)]


def swap_halves(grads):
    n = len(grads)

    def body(*refs):
        ins, got = refs[:n], refs[n:2 * n]
        send_sems, recv_sems = refs[2 * n:]
        x, y, c = _place()
        give = [_remote(_half(ins[t], 1 - c), got[t], send_sems.at[t], recv_sems.at[t], (x, y, 1 - c))
                for t in range(n)]
        for cp in give:
            cp.start()
        for cp in give:
            cp.wait_recv()
        for cp in give:
            cp.wait_send()

    return pl.pallas_call(
        body, name="swap_halves", in_specs=[HBM_REF] * n, out_specs=[HBM_REF] * n,
        out_shape=[jax.ShapeDtypeStruct((g.shape[0], g.shape[1] // 2, g.shape[2]), g.dtype) for g in grads],
        scratch_shapes=[pltpu.SemaphoreType.DMA((n,))] * 2, **COMM_PARAMS)(*grads)


def scatter_partials(partials):
    n = len(partials)

    def body(*refs):
        parts, landed = refs[:n], refs[n:2 * n]
        send_sems, recv_sems = refs[2 * n:]
        x, y, c = _place()
        sends = [_remote(parts[t].at[2 * px + py], landed[t].at[k], send_sems.at[k, t], recv_sems.at[k, t],
                         (px, py, c)) for k, (px, py) in enumerate(_other_chips(x, y)) for t in range(n)]
        for cp in sends:
            cp.start()
        for cp in sends:
            cp.wait_recv()
        for cp in sends:
            cp.wait_send()

    return pl.pallas_call(
        body, name="scatter_partials", in_specs=[HBM_REF] * n, out_specs=[HBM_REF] * n,
        out_shape=[jax.ShapeDtypeStruct((3,) + p.shape[1:], p.dtype) for p in partials],
        scratch_shapes=[pltpu.SemaphoreType.DMA((3, n))] * 2, **COMM_PARAMS)(*partials)


def join_halves(blocks):
    n = len(blocks)

    def body(*refs):
        outs = refs[n:2 * n]
        send_sems, recv_sems = refs[2 * n:]
        x, y, c = _place()
        give = [_remote(_half(outs[t], c), _half(outs[t], c), send_sems.at[t], recv_sems.at[t], (x, y, 1 - c))
                for t in range(n)]
        for cp in give:
            cp.start()
        for t in range(n):
            arriving = _half(outs[t], 1 - c)
            _remote(arriving, arriving, send_sems.at[t], recv_sems.at[t], (x, y, 1 - c)).wait_recv()
        for cp in give:
            cp.wait_send()

    return pl.pallas_call(
        body, name="join_halves", in_specs=[HBM_REF] * n, out_specs=[HBM_REF] * n,
        out_shape=[jax.ShapeDtypeStruct(b.shape, b.dtype) for b in blocks],
        input_output_aliases={t: t for t in range(n)},
        scratch_shapes=[pltpu.SemaphoreType.DMA((n,))] * 2, **COMM_PARAMS)(*blocks)


def reduce_block_grads(grads):
    names = list(grads)
    got = swap_halves([grads[n] for n in names])
    partials = []
    for name, theirs in zip(names, got):
        n_slot, hr, width = theirs.shape
        tb = _row_block(hr, width, 6)
        per_half = hr // tb
        mine = lambda i, s, per_half=per_half: (i // per_half) * 2 * per_half + s[0] * per_half + i % per_half
        p = placed_map(
            jnp.add,
            [(grads[name].reshape(2 * n_slot * hr, width), mine), (theirs.reshape(n_slot * hr, width), lambda i, s: i)],
            (n_slot * hr, width, BF16, lambda i, s: i), n_blocks=n_slot * per_half, tb=tb, name="chip_sum_" + name)
        partials.append(p.reshape(theirs.shape))
    landed = scatter_partials(partials)
    blocks = []
    for name, theirs, arrived in zip(names, got, landed):
        n_slot, hr, width = theirs.shape
        tb = _row_block(hr, width, 6)
        per_half = hr // tb
        views = [(grads[name].reshape(2 * n_slot * hr, width),
                  lambda i, s, per_half=per_half: s[1] * 2 * per_half + s[0] * per_half + i),
                 (theirs.reshape(n_slot * hr, width), lambda i, s, per_half=per_half: s[1] * per_half + i)]
        views += [(arrived.reshape(3 * hr, width), functools.partial(lambda k, per_half, i, s: k * per_half + i,
                                                                     k, per_half)) for k in range(3)]
        f = lambda a, b, l0, l1, l2: (((a + b) + l0.astype(F32)) + l1.astype(F32)) + l2.astype(F32)
        blocks.append(placed_map(
            f, views,(2 * hr, width, F32, lambda i, s, per_half=per_half: s[0] * per_half + i),
            n_blocks=per_half, tb=tb, name="owner_sum_" + name))
    return dict(zip(names, join_halves(blocks)))


def adamw_block(name, w, g, m, v):
    rows, width = w.shape
    return rowmap(_adamw, [w, g, m, v], [], [(width, F32)] * 3, tb=_row_block(rows, width, 7),
                  name="adamw_" + name)


def reduce_small(vec, w, m, v):
    n_dev = 8

    def body(vec_ref, w_ref, m_ref, v_ref, loss_ref, g_ref, d_ref, m2_ref, v2_ref, slots, send_sems, recv_sems):
        x, y, c = _place()
        me = 4 * x + 2 * y + c
        slots[me] = vec_ref[...]
        flips = [(fx, fy, fc) for fx in (0, 1) for fy in (0, 1) for fc in (0, 1)][1:]
        peers = [(1 - x if fx else x, 1 - y if fy else y, 1 - c if fc else c) for fx, fy, fc in flips]
        sends = [pltpu.make_async_remote_copy(
            src_ref=vec_ref, dst_ref=slots.at[me], send_sem=send_sems.at[j], recv_sem=recv_sems.at[j],
            device_id=peer, device_id_type=MESH) for j, peer in enumerate(peers)]
        for cp in sends:
            cp.start()
        for j, (px, py, pc) in enumerate(peers):
            pltpu.make_async_remote_copy(
                src_ref=vec_ref, dst_ref=slots.at[4 * px + 2 * py + pc], send_sem=send_sems.at[j],
                recv_sem=recv_sems.at[j], device_id=(px, py, pc), device_id_type=MESH).wait_recv()
        for cp in sends:
            cp.wait_send()
        g = slots[0]
        for d in range(1, n_dev):
            g = g + slots[d]
        loss_ref[...] = jnp.sum(g[:, :D_MODEL], axis=1, keepdims=True)
        delta, m2, v2 = _adamw(w_ref[...], g, m_ref[...], v_ref[...])
        g_ref[...], d_ref[...], m2_ref[...], v2_ref[...] = g, delta, m2, v2

    vm = pl.BlockSpec(memory_space=pltpu.VMEM)
    vec_t = jax.ShapeDtypeStruct(vec.shape, F32)
    return pl.pallas_call(
        body, name="reduce_small", in_specs=[vm] * 4, out_specs=[vm] * 5,
        out_shape=[jax.ShapeDtypeStruct((1, 1), F32)] + [vec_t] * 4,
        scratch_shapes=[pltpu.VMEM((n_dev,) + vec.shape, F32), pltpu.SemaphoreType.DMA((n_dev - 1,)),
                        pltpu.SemaphoreType.DMA((n_dev - 1,))],
        compiler_params=pltpu.CompilerParams(has_side_effects=True),
    )(vec, w, m, v)


def kernel(x, ffn1_norm, ffn1_w_in, ffn1_w_out, mix_norm, w_in, b_gate, rwkv_mu, rwkv_w0, rwkv_w2, rwkv_a0, rwkv_a2, rwkv_g2, rwkv_k_k, rwkv_k_a, rwkv_r_k, rwkv_ln_w, rwkv_ln_b, attn_q_norm, attn_k_norm, w_proj_rwkv, w_proj_attn, w_out, ffn2_norm, ffn2_w_in, ffn2_w_out, loss_target, m_ffn1_norm, m_ffn1_w_in, m_ffn1_w_out, m_mix_norm, m_w_in, m_b_gate, m_rwkv_mu, m_rwkv_w0, m_rwkv_w2, m_rwkv_a0, m_rwkv_a2, m_rwkv_g2, m_rwkv_k_k, m_rwkv_k_a, m_rwkv_r_k, m_rwkv_ln_w, m_rwkv_ln_b, m_attn_q_norm, m_attn_k_norm, m_w_proj_rwkv, m_w_proj_attn, m_w_out, m_ffn2_norm, m_ffn2_w_in, m_ffn2_w_out, v_ffn1_norm, v_ffn1_w_in, v_ffn1_w_out, v_mix_norm, v_w_in, v_b_gate, v_rwkv_mu, v_rwkv_w0, v_rwkv_w2, v_rwkv_a0, v_rwkv_a2, v_rwkv_g2, v_rwkv_k_k, v_rwkv_k_a, v_rwkv_r_k, v_rwkv_ln_w, v_rwkv_ln_b, v_attn_q_norm, v_attn_k_norm, v_w_proj_rwkv, v_w_proj_attn, v_w_out, v_ffn2_norm, v_ffn2_w_in, v_ffn2_w_out):
    given = dict(locals())
    weights = {n: given[n] for n in WEIGHT_ORDER}
    mom_m = {n: given["m_" + n] for n in WEIGHT_ORDER}
    mom_v = {n: given["v_" + n] for n in WEIGHT_ORDER}
    big = [name for name, _, _ in BIG]
    shapes = {n: weights[n].shape for n in WEIGHT_ORDER}
    blocks_of = lambda d: local_blocks({n: d[n][0] for n in big})
    w_blk, m_blk, v_blk = blocks_of(weights), blocks_of(mom_m), blocks_of(mom_v)
    names = list(w_blk)

    gathered = gather_weights([w_blk[n].astype(BF16) for n in names])
    W = {n: blocks_to_full(n, g) for n, g in zip(names, gathered)}
    W.update(split_lora(W.pop("lora")))
    P = {n: weights[n].reshape(1, -1) for n, _ in SMALL}

    loss_cols, dx, gW, gP = layer_step(x[0], loss_target[0], W, P)

    gW["lora"] = jnp.concatenate([gW.pop(n) for n in LORA_PARTS], axis=0)
    g_blk = reduce_block_grads({n: full_to_blocks(n, gW[n]) for n in names})
    out_g, out_d, out_m, out_v = {}, {}, {}, {}
    for n in names:
        res = (g_blk[n], *adamw_block(n, w_blk[n], g_blk[n], m_blk[n], v_blk[n]))
        for dst, t in zip((out_g, out_d, out_m, out_v), res):
            for part, val in (split_lora(t) if n == "lora" else {n: t}).items():
                dst[part] = val.reshape(shapes[part])

    zero_head = jnp.zeros((1, D_MODEL), F32)
    vec = pack_small(gP, loss_cols)
    loss, g_s, d_s, m_s, v_s = reduce_small(
        vec, pack_small({n: weights[n] for n, _ in SMALL}, zero_head),
        pack_small({n: mom_m[n] for n, _ in SMALL}, zero_head),
        pack_small({n: mom_v[n] for n, _ in SMALL}, zero_head))
    for dst, src in ((out_g, g_s), (out_d, d_s), (out_m, m_s), (out_v, v_s)):
        dst.update(unpack_small(src, shapes))

    return (loss[0, 0], dx[None], *[out_g[n] for n in WEIGHT_ORDER], *[out_d[n] for n in WEIGHT_ORDER],
            *[out_m[n] for n in WEIGHT_ORDER], *[out_v[n] for n in WEIGHT_ORDER])
```

```python
import functools

import jax
import jax.numpy as jnp
from jax import lax
from jax.experimental import pallas as pl
from jax.experimental.pallas import tpu as pltpu

F32 = jnp.float32
BF16 = jnp.bfloat16
HI = lax.Precision.HIGHEST
MESH = pl.DeviceIdType.MESH

D_MODEL = 1024
HEAD_DIM = 64
RWKV_HEADS = 16
LORA_W, LORA_A, LORA_G = 64, 64, 160
LORA = LORA_W + LORA_A + LORA_G
RKV = 3 * D_MODEL
ATTN_PAIRS = ((128, 1), (512, 4), (2048, 16))
ATTN_BLK = 128
ATTN_HPG = 4
ATTN_WIDTH = 768
GROUP_W = ATTN_HPG * HEAD_DIM
D_FF = 2816
GN_EPS = 64e-5
RMS_EPS = 1e-6
NEG_INF = -1e30
WKV_CHUNK = 64
WKV_HEADS_PER_STEP = 8

ADAM_LR, ADAM_B1, ADAM_B2, ADAM_EPS, ADAM_WD, ADAM_STEP = 0.001, 0.9, 0.999, 1e-08, 0.01, 10

V7X_VMEM_BYTES = 64 << 20
VMEM_TEMP_ALLOWANCE = 20 << 20


def _cparams(sem, block_bytes):
    limit = min(2 * block_bytes + VMEM_TEMP_ALLOWANCE, V7X_VMEM_BYTES - (6 << 20))
    return pltpu.CompilerParams(dimension_semantics=sem, vmem_limit_bytes=int(limit))


def _nbytes(shape, dtype):
    n = 1
    for s in shape:
        n *= s
    return n * jnp.dtype(dtype).itemsize


def _split_bf16(a):
    hi = a.astype(BF16)
    return hi, (a - hi.astype(F32)).astype(BF16)


def _make_dots(prec):
    def one(a, b, ca, cb):
        if prec is not HI:
            a, b = a.astype(BF16), b.astype(BF16)
        return lax.dot_general(a, b, (((ca,), (cb,)), ((), ())), precision=None if prec == "x3" else prec,
                               preferred_element_type=F32)

    def raw(a, b, ca, cb):
        if prec != "x3":
            return one(a, b, ca, cb)
        (ah, al), (bh, bl) = _split_bf16(a), _split_bf16(b)
        return one(ah, bh, ca, cb) + (one(al, bh, ca, cb) + one(ah, bl, ca, cb))

    @jax.custom_vjp
    def nn(a, b):
        return raw(a, b, 1, 0)

    @jax.custom_vjp
    def nt(a, b):
        return raw(a, b, 1, 1)

    @jax.custom_vjp
    def tn(a, b):
        return raw(a, b, 0, 0)

    nn.defvjp(lambda a, b: (raw(a, b, 1, 0), (a, b)),
              lambda res, g: (raw(g, res[1], 1, 1), raw(res[0], g, 0, 0)))
    nt.defvjp(lambda a, b: (raw(a, b, 1, 1), (a, b)),
              lambda res, g: (raw(g, res[1], 1, 0), raw(g, res[0], 0, 0)))
    tn.defvjp(lambda a, b: (raw(a, b, 0, 0), (a, b)),
              lambda res, g: (raw(res[1], g, 1, 1), raw(res[0], g, 1, 0)))
    return nn, nt, tn


def _exact_rhs_dot(x, ones, cx, co):
    hi, lo = _split_bf16(x)
    dims = (((cx,), (co,)), ((), ()))
    return (lax.dot_general(hi, ones, dims, preferred_element_type=F32)
            + lax.dot_general(lo, ones, dims, preferred_element_type=F32))


@jax.custom_vjp
def SEG(x, ones):
    return _exact_rhs_dot(x, ones, 1, 0)


SEG.defvjp(lambda x, ones: (_exact_rhs_dot(x, ones, 1, 0), ones),
           lambda ones, g: (_exact_rhs_dot(g, ones, 1, 1), jnp.zeros_like(ones)))

NN, NT, TN = _make_dots(None)
NN_HI, NT_HI, TN_HI = _make_dots(HI)
NN_X3, NT_X3, TN_X3 = _make_dots("x3")


MM_TILE_M, MM_TILE_N, MM_TILE_K = 1408, 1408, 1536


def _pick(n, cap):
    best = None
    for t in range(128, min(n, cap) + 1, 128):
        if n % t == 0:
            best = t
    return best or n


def matmul(a, b, mode, name, *, add=None, scale=1.0, out_dtype=F32):
    if mode == "nn":
        (M, K), (K2, N) = a.shape, b.shape
    elif mode == "nt":
        (M, K), (N, K2) = a.shape, b.shape
    else:
        (K, M), (K2, N) = a.shape, b.shape
    assert K == K2, (name, a.shape, b.shape)
    tm, tn, tk = _pick(M, MM_TILE_M), _pick(N, MM_TILE_N), _pick(K, MM_TILE_K)
    nk = K // tk
    ca, cb = {"nn": (1, 0), "nt": (1, 1), "tn": (0, 0)}[mode]

    def body(*refs):
        if add is None:
            a_ref, b_ref, o_ref, acc_ref = refs
        else:
            a_ref, b_ref, add_ref, o_ref, acc_ref = refs
        k = pl.program_id(2)

        @pl.when(k == 0)
        def _():
            acc_ref[...] = jnp.zeros_like(acc_ref)

        acc_ref[...] += lax.dot_general(a_ref[...].astype(BF16), b_ref[...].astype(BF16),
                                        (((ca,), (cb,)), ((), ())), preferred_element_type=F32)

        @pl.when(k == nk - 1)
        def _():
            r = acc_ref[...] * scale
            if add is not None:
                r = add_ref[...] + r
            o_ref[...] = r.astype(o_ref.dtype)

    a_spec = (pl.BlockSpec((tk, tm), lambda i, j, k: (k, i)) if mode == "tn"
              else pl.BlockSpec((tm, tk), lambda i, j, k: (i, k)))
    b_spec = (pl.BlockSpec((tn, tk), lambda i, j, k: (j, k)) if mode == "nt"
              else pl.BlockSpec((tk, tn), lambda i, j, k: (k, j)))
    in_specs, args = [a_spec, b_spec], [a, b]
    blk = tm * tk * a.dtype.itemsize + tk * tn * b.dtype.itemsize + tm * tn * 8
    if add is not None:
        in_specs.append(pl.BlockSpec((tm, tn), lambda i, j, k: (i, j)))
        args.append(add)
        blk += tm * tn * 4
    return pl.pallas_call(
        body, name=name, grid=(M // tm, N // tn, nk),
        in_specs=in_specs, out_specs=pl.BlockSpec((tm, tn), lambda i, j, k: (i, j)),
        out_shape=jax.ShapeDtypeStruct((M, N), out_dtype),
        scratch_shapes=[pltpu.VMEM((tm, tn), F32)],
        compiler_params=_cparams(("parallel", "parallel", "arbitrary"), blk),
    )(*args)


def matmul_cs(a, w, mode, name, *, scale=1.0, out_dtype=F32):
    n_blk = N_SHARDS
    if mode == "tn":
        (K, R), Cs = a.shape, w.shape[1] // n_blk
        tm, tk = _pick(R, MM_TILE_M), _pick(K, 1024)
        grid = (R // tm, n_blk, K // tk)
        a_spec = pl.BlockSpec((tk, tm), lambda i, j, k: (k, i))
        w_spec = pl.BlockSpec((tk, Cs), lambda i, j, k: (k, j))
        o_spec = pl.BlockSpec((None, tm, Cs), lambda i, j, k: (j, i, 0))
        out_shape, acc_shape, dims = (n_blk, R, Cs), (tm, Cs), (0, 0)
        blk = tk * tm * a.dtype.itemsize + tk * Cs * w.dtype.itemsize + tm * Cs * 8
    elif mode == "nn":
        (M, R), Cs = a.shape, w.shape[2]
        tm, tk = _pick(M, MM_TILE_M), _pick(R, 1024)
        grid = (M // tm, n_blk, R // tk)
        a_spec = pl.BlockSpec((tm, tk), lambda i, j, k: (i, k))
        w_spec = pl.BlockSpec((None, tk, Cs), lambda i, j, k: (j, k, 0))
        o_spec = pl.BlockSpec((tm, Cs), lambda i, j, k: (i, j))
        out_shape, acc_shape, dims = (M, n_blk * Cs), (tm, Cs), (1, 0)
        blk = tm * tk * a.dtype.itemsize + tk * Cs * w.dtype.itemsize + tm * Cs * 8
    else:
        M, (_, R, Cs) = a.shape[0], w.shape
        tm, tn = _pick(M, MM_TILE_M), _pick(R, MM_TILE_N)
        grid = (M // tm, R // tn, n_blk)
        a_spec = pl.BlockSpec((tm, Cs), lambda i, j, k: (i, k))
        w_spec = pl.BlockSpec((None, tn, Cs), lambda i, j, k: (k, j, 0))
        o_spec = pl.BlockSpec((tm, tn), lambda i, j, k: (i, j))
        out_shape, acc_shape, dims = (M, R), (tm, tn), (1, 1)
        blk = tm * Cs * a.dtype.itemsize + tn * Cs * w.dtype.itemsize + tm * tn * 8
    nk = grid[2]

    def body(a_ref, w_ref, o_ref, acc_ref):
        k = pl.program_id(2)

        @pl.when(k == 0)
        def _():
            acc_ref[...] = jnp.zeros_like(acc_ref)

        acc_ref[...] += lax.dot_general(a_ref[...].astype(BF16), w_ref[...].astype(BF16),
                                        (((dims[0],), (dims[1],)), ((), ())), preferred_element_type=F32)

        @pl.when(k == nk - 1)
        def _():
            o_ref[...] = (acc_ref[...] * scale).astype(o_ref.dtype)

    return pl.pallas_call(
        body, name=name, grid=grid, in_specs=[a_spec, w_spec], out_specs=o_spec,
        out_shape=jax.ShapeDtypeStruct(out_shape, out_dtype), scratch_shapes=[pltpu.VMEM(acc_shape, F32)],
        compiler_params=_cparams(("parallel", "parallel", "arbitrary"), blk),
    )(a, w)


def _row_block(n, width, n_arrays):
    cap = (V7X_VMEM_BYTES // 4) // (2 * 4 * width * n_arrays)
    best = None
    for t in range(16, min(n, cap) + 1, 16):
        if n % t == 0:
            best = t
    return best or n


def placed_map(f, ins, out, *, n_blocks, tb, name):
    def body(*refs):
        refs[-1][...] = f(*[r[...] for r in refs[:-1]]).astype(refs[-1].dtype)

    def spec(fn):
        def index(i):
            x, y, c = _place()
            return fn(i, (c, 2 * x + y)), 0
        return pl.BlockSpec((tb, width), index)

    o_rows, width, o_dtype, o_fn = out
    blk = (sum(a.dtype.itemsize for a, _ in ins) + jnp.dtype(o_dtype).itemsize) * tb * width
    return pl.pallas_call(
        body, name=name, grid=(n_blocks,), in_specs=[spec(fn) for _, fn in ins], out_specs=spec(o_fn),
        out_shape=jax.ShapeDtypeStruct((o_rows, width), o_dtype),
        compiler_params=_cparams(("parallel",), blk),
    )(*[a for a, _ in ins])


def rowmap(f, rows, params, outs, accs=(), *, tb, name):
    rows = [r if isinstance(r, tuple) else (r, r.shape[1], 0) for r in rows]
    S = rows[0][0].shape[0]
    assert S % tb == 0, (name, S, tb)
    n_in, n_out = len(rows) + len(params), len(outs)

    def body(*refs):
        res = f(*[r[...] for r in refs[:n_in]])
        res = res if isinstance(res, (tuple, list)) else (res,)
        o_refs, a_refs = refs[n_in:n_in + n_out], refs[n_in + n_out:]
        for ref, val in zip(o_refs, res[:n_out]):
            ref[...] = val.astype(ref.dtype)
        if a_refs:
            @pl.when(pl.program_id(0) == 0)
            def _():
                for ref in a_refs:
                    ref[...] = jnp.zeros_like(ref)

            for ref, val in zip(a_refs, res[n_out:]):
                ref[...] += val.astype(F32)

    in_specs = [pl.BlockSpec((tb, w), functools.partial(lambda cb, i: (i, cb), cb)) for _, w, cb in rows]
    in_specs += [pl.BlockSpec(p.shape, lambda i: (0, 0)) for p in params]
    out_specs = [pl.BlockSpec((tb, w), lambda i: (i, 0)) for w, _ in outs]
    out_specs += [pl.BlockSpec(tuple(s), lambda i: (0, 0)) for s in accs]
    out_shape = [jax.ShapeDtypeStruct((S, w), dt) for w, dt in outs]
    out_shape += [jax.ShapeDtypeStruct(tuple(s), F32) for s in accs]
    blk = sum(tb * w * a.dtype.itemsize for a, w, _ in rows) + sum(_nbytes(p.shape, p.dtype) for p in params)
    blk += sum(_nbytes((tb, w), dt) for w, dt in outs) + sum(_nbytes(s, F32) for s in accs)
    res = pl.pallas_call(
        body, name=name, grid=(S // tb,), in_specs=in_specs, out_specs=out_specs, out_shape=out_shape,
        compiler_params=_cparams(("arbitrary",) if accs else ("parallel",), blk),
    )(*[r[0] for r in rows], *params)
    return res


def _rms(x, g):
    return x * lax.rsqrt(jnp.mean(x * x, axis=-1, keepdims=True) + RMS_EPS) * g


def _softplus(z):
    return jnp.maximum(z, 0.0) + jnp.log(1.0 + jnp.exp(-jnp.abs(z)))


def _swiglu_act(gu):
    gate, up = gu[:, :D_FF], gu[:, D_FF:]
    return gate * jax.nn.sigmoid(gate) * up


def _rwkv_pre(xrk, xlo, w0, w2p, a0, a2p, g2p, k_k, k_a, seg, seg_t):
    k = xrk[:, D_MODEL:2 * D_MODEL]
    w = -_softplus(-(w0 + NN(jnp.tanh(xlo), w2p))) - 0.5
    log_decay = -jnp.exp(w)
    a = jax.nn.sigmoid(a0 + NN(xlo, a2p))
    g = NN(jax.nn.sigmoid(xlo), g2p)
    kk = k * k_k
    norm = jnp.maximum(jnp.sqrt(SEG(kk * kk, seg)), 1e-12)
    kk = kk * SEG(1.0 / norm, seg_t)
    k_mod = k * (1.0 + (a - 1.0) * k_a)
    return log_decay, k_mod, -kk, kk * a, g


def _rwkv_post(wkv, r, k_mod, v, g, r_k, ln_w, ln_b, seg, seg_t):
    inv_n = 1.0 / HEAD_DIM
    mean = SEG(wkv, seg) * inv_n
    cen = wkv - SEG(mean, seg_t)
    var = SEG(cen * cen, seg) * inv_n
    y = cen * SEG(lax.rsqrt(var + GN_EPS), seg_t) * ln_w + ln_b
    bonus = SEG(SEG(r * k_mod * r_k, seg), seg_t) * v
    return (y + bonus) * g


def _qk_norm(q, k, q_gain, k_gain, seg, seg_t, tile_t):
    def norm(x, gain):
        mean_sq = SEG(x * x, seg) * (1.0 / HEAD_DIM)
        return x * SEG(lax.rsqrt(mean_sq + RMS_EPS), seg_t) * SEG(gain, tile_t)

    return norm(q, q_gain) * (HEAD_DIM ** -0.5), norm(k, k_gain)


def _gate_merge(pgate, pa, pb, b_gate):
    sg = jax.nn.sigmoid(pgate + b_gate)
    return sg[:, :D_MODEL] * pa + sg[:, D_MODEL:] * pb


def _group_combine(o, lse):
    ls = [lse[:, GROUP_W * i:GROUP_W * (i + 1)] for i in range(3)]
    m = jnp.maximum(jnp.maximum(ls[0], ls[1]), ls[2])
    es = [jnp.exp(l - m) for l in ls]
    den = es[0] + es[1] + es[2]
    return jnp.concatenate([o[:, GROUP_W * i:GROUP_W * (i + 1)] * (es[i] / den) for i in range(3)], axis=1)


def _each(f, *xs):
    return tuple(f(*args) for args in zip(*xs))


def _attn_block(q, kc, kp, vc, vp, first):
    qi = lax.broadcasted_iota(jnp.int32, (ATTN_BLK, ATTN_BLK), 0)
    kj = lax.broadcasted_iota(jnp.int32, (ATTN_BLK, ATTN_BLK), 1)
    own, before = kj <= qi, (kj >= qi) & (first < 0.5)
    s_c = _each(lambda a, b: jnp.where(own, NT(a, b), NEG_INF), q, kc)
    s_p = _each(lambda a, b: jnp.where(before, NT(a, b), NEG_INF), q, kp)
    row_max = lambda s: jnp.max(s, axis=-1, keepdims=True)
    row_sum = lambda s: jnp.sum(s, axis=-1, keepdims=True)
    m = _each(lambda c_, p_: jnp.maximum(row_max(c_), row_max(p_)), s_c, s_p)
    e_c, e_p = _each(lambda s, m_: jnp.exp(s - m_), s_c, m), _each(lambda s, m_: jnp.exp(s - m_), s_p, m)
    den = _each(lambda c_, p_: row_sum(c_) + row_sum(p_), e_c, e_p)
    inv = _each(lambda d_: 1.0 / d_, den)
    o = _each(lambda ec, ep, i_, vc_, vp_: (NN(ec, vc_) + NN(ep, vp_)) * i_, e_c, e_p, inv, vc, vp)
    lse = _each(lambda m_, d_: jnp.broadcast_to(m_ + jnp.log(d_), (ATTN_BLK, HEAD_DIM)), m, den)
    return o, lse


TRI_SEED = 8


def _tri_inverse(n):
    c = n[0].shape[0]
    row = lax.broadcasted_iota(jnp.int32, (c, c), 0)
    col = lax.broadcasted_iota(jnp.int32, (c, c), 1)
    same_block = lambda size: (row >> (size.bit_length() - 1)) == (col >> (size.bit_length() - 1))
    seed = same_block(TRI_SEED)
    p = _each(lambda m: jnp.where(seed, m, 0.0), n)
    t, span = _each(lambda m: (row == col).astype(F32) + m, p), 2
    while span < TRI_SEED:
        p = _each(NN_X3, p, p)
        t = _each(lambda t_, p_: t_ + NN_X3(t_, p_), t, p)
        span *= 2
    size = TRI_SEED
    while size < c:
        joins = same_block(2 * size) & jnp.logical_not(same_block(size))
        t = _each(lambda t_, m: t_ + NN_X3(NN_X3(t_, jnp.where(joins, m, 0.0)), t_), t, n)
        size *= 2
    return t


@jax.custom_vjp
def _tri_solve(n, rhs, t):
    return _each(NN, t, rhs)


def _tri_solve_fwd(n, rhs, t):
    x = _each(NN, t, rhs)
    return x, (t, x)


def _tri_solve_bwd(res, dx):
    t, x = res
    drhs = _each(TN, t, dx)
    return _each(NT, drhs, x), drhs, _each(jnp.zeros_like, t)


_tri_solve.defvjp(_tri_solve_fwd, _tri_solve_bwd)


def _lower_ones(c):
    row = lax.broadcasted_iota(jnp.int32, (c, c), 0)
    col = lax.broadcasted_iota(jnp.int32, (c, c), 1)
    return (row >= col).astype(BF16)


def _ones_dot(ones, x, contract):
    hi, lo = _split_bf16(x)
    dims = (((contract,), (0,)), ((), ()))
    return (lax.dot_general(ones, hi, dims, preferred_element_type=F32)
            + lax.dot_general(ones, lo, dims, preferred_element_type=F32))


@jax.custom_vjp
def _cumsum_rows(x):
    return _ones_dot(_lower_ones(x.shape[0]), x, 1)


_cumsum_rows.defvjp(lambda x: (_ones_dot(_lower_ones(x.shape[0]), x, 1), None),
                    lambda _, g: (_ones_dot(_lower_ones(g.shape[0]), g, 0),))


def _wkv_chunk(s0, r, lw, k, v, a, b, t_inv=None):
    c = r[0].shape[0]
    row = lax.broadcasted_iota(jnp.int32, (c, c), 0)
    col = lax.broadcasted_iota(jnp.int32, (c, c), 1)
    strict, incl = row > col, row >= col
    cat = lambda p, q: jnp.concatenate([p, q], axis=0)
    cum = _each(_cumsum_rows, lw)
    e_neg = _each(lambda c_: jnp.exp(-c_), cum)
    ar = _each(lambda a_, r_, c_, l_: cat(a_ * jnp.exp(c_ - l_), r_ * jnp.exp(c_)), a, r, cum, lw)
    b_t, k_t = _each(jnp.multiply, b, e_neg), _each(jnp.multiply, k, e_neg)
    p_b, p_k, p_s = _each(NT, ar, b_t), _each(NT, ar, k_t), _each(NT, ar, s0)
    n_ab = _each(lambda p: jnp.where(strict, p[:c], 0.0), p_b)
    m_rb = _each(lambda p: jnp.where(incl, p[c:], 0.0), p_b)
    n_ak = _each(lambda p: jnp.where(strict, p[:c], 0.0), p_k)
    m_rk = _each(lambda p: jnp.where(incl, p[c:], 0.0), p_k)
    if t_inv is None:
        t_inv = _tri_inverse(n_ab)
    u = _tri_solve(n_ab, _each(lambda p, n_, v_: p[:c] + NN(n_, v_), p_s, n_ak, v), t_inv)
    y = _each(lambda p, mb, u_, mk, v_: p[c:] + NN(mb, u_) + NN(mk, v_), p_s, m_rb, u, m_rk, v)
    g_end = _each(lambda l_: jnp.exp(jnp.sum(l_, axis=0, keepdims=True)), lw)
    s1 = _each(lambda s_, g_, u_, v_, b_, k_: s_ * g_ + TN(cat(u_, v_), cat(b_, k_) * g_),
               s0, g_end, u, v, b_t, k_t)
    return y, s1, t_inv


def _adamw(w, g, m, v):
    m = ADAM_B1 * m + (1.0 - ADAM_B1) * g
    v = ADAM_B2 * v + (1.0 - ADAM_B2) * jnp.square(g)
    m_hat = m / (1.0 - ADAM_B1 ** ADAM_STEP)
    v_hat = v / (1.0 - ADAM_B2 ** ADAM_STEP)
    delta = -ADAM_LR * (m_hat / (jnp.sqrt(v_hat) + ADAM_EPS) + ADAM_WD * w)
    return delta, m, v


def token_shift_fwd(p, mu, *, tb, name):
    S, W = p.shape
    hb = tb // 8

    def body(p_ref, halo_ref, mu_ref, o_ref):
        i = pl.program_id(0)
        x = p_ref[...]
        before = halo_ref[7:8, :] * (i > 0).astype(F32)
        row = lax.broadcasted_iota(jnp.int32, (tb, W), 0)
        prev = jnp.where(row == 0, before, pltpu.roll(x, 1, 0))
        o_ref[...] = x + (prev - x) * mu_ref[...]

    blk = (2 * tb + 8) * W * 4
    return pl.pallas_call(
        body, name=name, grid=(S // tb,),
        in_specs=[pl.BlockSpec((tb, W), lambda i: (i, 0)),
                  pl.BlockSpec((8, W), lambda i: (jnp.maximum(i * hb - 1, 0), 0)),
                  pl.BlockSpec((1, W), lambda i: (0, 0))],
        out_specs=pl.BlockSpec((tb, W), lambda i: (i, 0)),
        out_shape=jax.ShapeDtypeStruct((S, W), F32),
        compiler_params=_cparams(("parallel",), blk),
    )(p, p, mu)


def token_shift_bwd(dxs, p, mu, *, tb, name):
    S, W = p.shape
    hb, nb = tb // 8, S // tb

    def body(d_ref, dnext_ref, p_ref, halo_ref, mu_ref, dp_ref, dmu_ref):
        i = pl.program_id(0)
        d, x, mu_v = d_ref[...], p_ref[...], mu_ref[...]
        row = lax.broadcasted_iota(jnp.int32, (tb, W), 0)
        before = halo_ref[7:8, :] * (i > 0).astype(F32)
        prev = jnp.where(row == 0, before, pltpu.roll(x, 1, 0))
        t = d * mu_v
        after = dnext_ref[0:1, :] * mu_v * (i < nb - 1).astype(F32)
        nxt = jnp.where(row == tb - 1, after, pltpu.roll(t, tb - 1, 0))
        dp_ref[...] = (d - t + nxt).astype(dp_ref.dtype)

        @pl.when(i == 0)
        def _():
            dmu_ref[...] = jnp.zeros_like(dmu_ref)

        dmu_ref[...] += jnp.sum(d * (prev - x), axis=0, keepdims=True)

    blk = (3 * tb + 16) * W * 4
    return pl.pallas_call(
        body, name=name, grid=(nb,),
        in_specs=[pl.BlockSpec((tb, W), lambda i: (i, 0)),
                  pl.BlockSpec((8, W), lambda i: (jnp.minimum((i + 1) * hb, S // 8 - 1), 0)),
                  pl.BlockSpec((tb, W), lambda i: (i, 0)),
                  pl.BlockSpec((8, W), lambda i: (jnp.maximum(i * hb - 1, 0), 0)),
                  pl.BlockSpec((1, W), lambda i: (0, 0))],
        out_specs=[pl.BlockSpec((tb, W), lambda i: (i, 0)), pl.BlockSpec((1, W), lambda i: (0, 0))],
        out_shape=[jax.ShapeDtypeStruct((S, W), BF16), jax.ShapeDtypeStruct((1, W), F32)],
        compiler_params=_cparams(("arbitrary",), blk),
    )(dxs, dxs, p, p, mu)


def _head_cols(h):
    return pl.ds(h * HEAD_DIM, HEAD_DIM)


def wkv_fwd(xs_rk, lw, k, a, b):
    S = lw.shape[0]
    C, nc, G, N = WKV_CHUNK, S // WKV_CHUNK, WKV_HEADS_PER_STEP, HEAD_DIM

    def body(r_ref, lw_ref, k_ref, v_ref, a_ref, b_ref, y_ref, st_ref, ti_ref, state):
        @pl.when(pl.program_id(1) == 0)
        def _():
            state[...] = jnp.zeros_like(state)

        heads = lambda ref: tuple(ref[:, _head_cols(h)] for h in range(G))
        s0 = tuple(state[h] for h in range(G))
        y, s1, t_inv = _wkv_chunk(s0, heads(r_ref), heads(lw_ref), heads(k_ref), heads(v_ref), heads(a_ref),
                                  heads(b_ref))
        for h in range(G):
            st_ref[h] = s0[h]
            ti_ref[h] = t_inv[h]
            y_ref[:, _head_cols(h)] = y[h]
            state[h] = s1[h]

    W = G * N
    seq = lambda j: pl.BlockSpec((C, W), functools.partial(lambda j, g, c: (c, j + g), j))
    per = D_MODEL // W
    per_chunk = pl.BlockSpec((None, G, N, N), lambda g, c: (c, g, 0, 0))
    return pl.pallas_call(
        body, name="wkv_fwd", grid=(RWKV_HEADS // G, nc),
        in_specs=[seq(0), seq(0), seq(0), seq(2 * per), seq(0), seq(0)],
        out_specs=[seq(0), per_chunk, per_chunk],
        out_shape=[jax.ShapeDtypeStruct((S, D_MODEL), F32)] + [jax.ShapeDtypeStruct((nc, RWKV_HEADS, N, N), F32)] * 2,
        scratch_shapes=[pltpu.VMEM((G, N, N), F32)],
        compiler_params=_cparams(("parallel", "arbitrary"), 8 * C * W * 4 + 3 * G * N * N * 4),
    )(xs_rk, lw, k, xs_rk, a, b)


def wkv_bwd(xs_rk, lw, k, a, b, states, t_invs, dy):
    S = lw.shape[0]
    C, nc, G, N = WKV_CHUNK, S // WKV_CHUNK, WKV_HEADS_PER_STEP, HEAD_DIM

    def body(r_ref, lw_ref, k_ref, v_ref, a_ref, b_ref, st_ref, ti_ref, dy_ref,
             dr_ref, dlw_ref, dk_ref, dv_ref, da_ref, db_ref, dstate):
        @pl.when(pl.program_id(1) == 0)
        def _():
            dstate[...] = jnp.zeros_like(dstate)

        heads = lambda ref: tuple(ref[:, _head_cols(h)] for h in range(G))
        t_inv = tuple(ti_ref[h] for h in range(G))
        chunk = lambda *args: _wkv_chunk(*args, t_inv)[:2]
        _, pull = jax.vjp(chunk, tuple(st_ref[h] for h in range(G)), heads(r_ref), heads(lw_ref),
                          heads(k_ref), heads(v_ref), heads(a_ref), heads(b_ref))
        ds0, *grads = pull((heads(dy_ref), tuple(dstate[h] for h in range(G))))
        for h in range(G):
            dstate[h] = ds0[h]
            for ref, grad in zip((dr_ref, dlw_ref, dk_ref, dv_ref, da_ref, db_ref), grads):
                ref[:, _head_cols(h)] = grad[h]

    W = G * N
    seq = lambda j: pl.BlockSpec((C, W), functools.partial(lambda j, g, c: (nc - 1 - c, j + g), j))
    per = D_MODEL // W
    st = pl.BlockSpec((None, G, N, N), lambda g, c: (nc - 1 - c, g, 0, 0))
    return pl.pallas_call(
        body, name="wkv_bwd", grid=(RWKV_HEADS // G, nc),
        in_specs=[seq(0), seq(0), seq(0), seq(2 * per), seq(0), seq(0), st, st, seq(0)],
        out_specs=[seq(0)] * 6, out_shape=[jax.ShapeDtypeStruct((S, D_MODEL), F32)] * 6,
        scratch_shapes=[pltpu.VMEM((G, N, N), F32)],
        compiler_params=_cparams(("parallel", "arbitrary"), 14 * C * W * 4 + 3 * G * N * N * 4),
    )(xs_rk, lw, k, xs_rk, a, b, states, t_invs, dy)


def _first_flag(i, seq_len):
    per_group = seq_len // ATTN_BLK
    g = i // per_group
    per_seq = [seq_len // d // ATTN_BLK for _, d in ATTN_PAIRS]
    n = jnp.where(g == 0, per_seq[0], jnp.where(g == 1, per_seq[1], per_seq[2]))
    return (lax.rem(i, n) == 0).astype(F32)


def attn_fwd(q, k, v, seq_len):
    R, N = q.shape
    nb = R // ATTN_BLK

    def body(q_ref, kc_ref, kp_ref, vc_ref, vp_ref, o_ref, lse_ref):
        first = _first_flag(pl.program_id(0), seq_len)
        heads = lambda ref: tuple(ref[:, _head_cols(h)] for h in range(ATTN_HPG))
        o, lse = _attn_block(heads(q_ref), heads(kc_ref), heads(kp_ref), heads(vc_ref), heads(vp_ref), first)
        for h in range(ATTN_HPG):
            o_ref[:, _head_cols(h)] = o[h]
            lse_ref[:, _head_cols(h)] = lse[h]

    cur = pl.BlockSpec((ATTN_BLK, N), lambda i: (i, 0))
    prv = pl.BlockSpec((ATTN_BLK, N), lambda i: (jnp.maximum(i - 1, 0), 0))
    return pl.pallas_call(
        body, name="attn_fwd", grid=(nb,), in_specs=[cur, cur, prv, cur, prv],
        out_specs=[cur, cur], out_shape=[jax.ShapeDtypeStruct((R, N), F32)] * 2,
        compiler_params=_cparams(("parallel",), 7 * ATTN_BLK * N * 4),
    )(q, k, k, v, v)


def attn_bwd(q, k, v, do, dlse, seq_len):
    R, N = q.shape
    nb = R // ATTN_BLK

    def body(q_ref, kc_ref, kp_ref, vc_ref, vp_ref, do_ref, dl_ref, dq_ref, dk_ref, dv_ref, carry_k, carry_v):
        step = pl.program_id(0)
        first = _first_flag(nb - 1 - step, seq_len)

        @pl.when(step == 0)
        def _():
            carry_k[...] = jnp.zeros_like(carry_k)
            carry_v[...] = jnp.zeros_like(carry_v)

        heads = lambda ref: tuple(ref[:, _head_cols(h)] for h in range(ATTN_HPG))
        _, pull = jax.vjp(functools.partial(_attn_block, first=first), heads(q_ref), heads(kc_ref), heads(kp_ref),
                          heads(vc_ref), heads(vp_ref))
        dq, dkc, dkp, dvc, dvp = pull((heads(do_ref), heads(dl_ref)))
        old_k, old_v = heads(carry_k), heads(carry_v)
        for h in range(ATTN_HPG):
            cols = _head_cols(h)
            dq_ref[:, cols] = dq[h]
            dk_ref[:, cols] = dkc[h] + old_k[h]
            dv_ref[:, cols] = dvc[h] + old_v[h]
            carry_k[:, cols] = dkp[h]
            carry_v[:, cols] = dvp[h]

    cur = pl.BlockSpec((ATTN_BLK, N), lambda i: (nb - 1 - i, 0))
    prv = pl.BlockSpec((ATTN_BLK, N), lambda i: (jnp.maximum(nb - 2 - i, 0), 0))
    return pl.pallas_call(
        body, name="attn_bwd", grid=(nb,), in_specs=[cur, cur, prv, cur, prv, cur, cur],
        out_specs=[cur, cur, cur], out_shape=[jax.ShapeDtypeStruct((R, N), F32)] * 3,
        scratch_shapes=[pltpu.VMEM((ATTN_BLK, N), F32)] * 2,
        compiler_params=_cparams(("arbitrary",), 12 * ATTN_BLK * N * 4),
    )(q, k, k, v, v, do, dlse)


def to_subsequences(t):
    S = t.shape[0]
    parts = []
    for gi, (_, d) in enumerate(ATTN_PAIRS):
        tg = t[:, GROUP_W * gi:GROUP_W * (gi + 1)].reshape(S // d, d, GROUP_W)
        parts.append(tg.transpose(1, 0, 2).reshape(S, GROUP_W))
    return jnp.concatenate(parts, axis=0)


def from_subsequences(u, S):
    parts = []
    for gi, (_, d) in enumerate(ATTN_PAIRS):
        ug = u[S * gi:S * (gi + 1)].reshape(d, S // d, GROUP_W)
        parts.append(ug.transpose(1, 0, 2).reshape(S, GROUP_W))
    return jnp.concatenate(parts, axis=1)


def _ffn_fwd(x, norm, w_in, w_out, tag):
    h = rowmap(_rms, [x], [norm], [(D_MODEL, BF16)], tb=512, name=tag + "_norm")[0]
    gu = matmul_cs(h, w_in, "nn", tag + "_in")
    act = rowmap(_swiglu_act, [gu], [], [(D_FF, BF16)], tb=256, name=tag + "_act")[0]
    y = matmul(act, w_out, "nn", tag + "_out", add=x, scale=0.5)
    return y, (x, h, gu, act)


def _ffn_bwd(dy, saved, norm, w_in, w_out, tag):
    x, h, gu, act = saved
    dact = matmul(dy, w_out, "nt", tag + "_dact", scale=0.5)
    dw_out = matmul(act, dy, "tn", tag + "_dwout", scale=0.5)

    def act_bwd(gu_b, dact_b):
        return jax.vjp(_swiglu_act, gu_b)[1](dact_b)[0]

    dgu = rowmap(act_bwd, [gu, dact], [], [(2 * D_FF, BF16)], tb=256, name=tag + "_dgu")[0]
    dh = matmul_cs(dgu, w_in, "nt", tag + "_dh")
    dw_in = matmul_cs(h, dgu, "tn", tag + "_dwin")

    def norm_bwd(x_b, dh_b, dy_b, g):
        dx, dg = jax.vjp(_rms, x_b, g)[1](dh_b)
        return dy_b + dx, dg

    dx, dnorm = rowmap(norm_bwd, [x, dh, dy], [norm], [(D_MODEL, F32)], [(1, D_MODEL)], tb=256,
                       name=tag + "_dnorm")
    return dx, dnorm, dw_in, dw_out


def layer_step(x, tgt, W, P):
    S = x.shape[0]
    head_of = lambda n: jnp.arange(n)[:, None] // HEAD_DIM == jnp.arange(n // HEAD_DIM)[None, :]
    seg, seg_a = head_of(D_MODEL).astype(BF16), head_of(ATTN_WIDTH).astype(BF16)
    seg_t, seg_a_t = seg.T, seg_a.T
    tile_t = (jnp.arange(HEAD_DIM)[:, None] == jnp.arange(ATTN_WIDTH)[None, :] % HEAD_DIM).astype(BF16)
    qk_params = [P["attn_q_norm"], P["attn_k_norm"], seg_a, seg_a_t, tile_t]
    w_rkv, w_lora = W["w_in"][:, :RKV], W["w_in"][:, RKV:RKV + LORA]
    w_qkv = W["w_in"][:, RKV + LORA:RKV + LORA + 3 * ATTN_WIDTH]
    w_gate = W["w_in"][:, RKV + LORA + 3 * ATTN_WIDTH:]
    mu_rk, mu_lo = P["rwkv_mu"][:, :RKV], P["rwkv_mu"][:, RKV:]
    zeros = lambda n: jnp.zeros((n, D_MODEL), F32)
    w2p = jnp.concatenate([W["rwkv_w2"], zeros(LORA - LORA_W)], axis=0)
    a2p = jnp.concatenate([zeros(LORA_W), W["rwkv_a2"], zeros(LORA_G)], axis=0)
    g2p = jnp.concatenate([zeros(LORA_W + LORA_A), W["rwkv_g2"]], axis=0)
    pre_params = [P["rwkv_w0"], w2p, P["rwkv_a0"], a2p, g2p, P["rwkv_k_k"], P["rwkv_k_a"], seg, seg_t]
    post_params = [P["rwkv_r_k"], P["rwkv_ln_w"], P["rwkv_ln_b"], seg, seg_t]
    col = lambda arr, j: (arr, D_MODEL, j)

    x1, ffn1_saved = _ffn_fwd(x, P["ffn1_norm"], W["ffn1_w_in"], W["ffn1_w_out"], "ffn1")
    h = rowmap(_rms, [x1], [P["mix_norm"]], [(D_MODEL, BF16)], tb=512, name="mix_norm")[0]
    p_rk = matmul(h, w_rkv, "nn", "proj_rkv")
    p_lo = matmul(h, w_lora, "nn", "proj_lora")
    p_qkv = matmul(h, w_qkv, "nn", "proj_qkv")
    p_gate = matmul(h, w_gate, "nn", "proj_gate")
    xs_rk = token_shift_fwd(p_rk, mu_rk, tb=256, name="shift_rk")
    xs_lo = token_shift_fwd(p_lo, mu_lo, tb=256, name="shift_lora")
    lw, k_mod, a_neg, b_kk, g = rowmap(
        _rwkv_pre, [xs_rk, xs_lo], pre_params, [(D_MODEL, F32)] * 5, tb=256, name="rwkv_pre")
    wkv, states, t_invs = wkv_fwd(xs_rk, lw, k_mod, a_neg, b_kk)
    post_rows = [wkv, col(xs_rk, 0), k_mod, col(xs_rk, 2), g]
    y_a = rowmap(_rwkv_post, post_rows, post_params, [(D_MODEL, BF16)], tb=256, name="rwkv_post")[0]

    qk_rows = [(p_qkv, ATTN_WIDTH, 0), (p_qkv, ATTN_WIDTH, 1)]
    qn, kn = rowmap(_qk_norm, qk_rows, qk_params, [(ATTN_WIDTH, F32)] * 2, tb=256, name="qk_norm")
    q_s, k_s, v_s = to_subsequences(qn), to_subsequences(kn), to_subsequences(p_qkv[:, 2 * ATTN_WIDTH:])
    o_s, lse_s = attn_fwd(q_s, k_s, v_s, S)
    o, lse = from_subsequences(o_s, S), from_subsequences(lse_s, S)
    y_b = rowmap(_group_combine, [o, lse], [], [(ATTN_WIDTH, BF16)], tb=512, name="attn_combine")[0]

    pa = matmul(y_a, W["w_proj_rwkv"], "nn", "proj_a")
    pb = matmul(y_b, W["w_proj_attn"], "nn", "proj_b")
    merged = rowmap(_gate_merge, [p_gate, pa, pb], [P["b_gate"]], [(D_MODEL, BF16)], tb=256, name="merge")[0]
    x2 = matmul(merged, W["w_out"], "nn", "mix_out", add=x1)
    x3, ffn2_saved = _ffn_fwd(x2, P["ffn2_norm"], W["ffn2_w_in"], W["ffn2_w_out"], "ffn2")

    def loss_head(y_b_, t_b):
        err = y_b_ - t_b
        return err * (1.0 / D_MODEL), (0.5 / D_MODEL) * jnp.sum(err * err, axis=0, keepdims=True)

    dx3, loss_cols = rowmap(loss_head, [x3, tgt], [], [(D_MODEL, F32)], [(1, D_MODEL)], tb=512, name="loss")

    gW, gP = {}, {}
    dx2, gP["ffn2_norm"], gW["ffn2_w_in"], gW["ffn2_w_out"] = _ffn_bwd(
        dx3, ffn2_saved, P["ffn2_norm"], W["ffn2_w_in"], W["ffn2_w_out"], "ffn2")

    dmerged = matmul(dx2, W["w_out"], "nt", "d_merged")
    gW["w_out"] = matmul(merged, dx2, "tn", "dw_out")

    def merge_bwd(pg, pa_b, pb_b, dm, bg):
        return jax.vjp(_gate_merge, pg, pa_b, pb_b, bg)[1](dm)

    dp_gate, dpa, dpb, gP["b_gate"] = rowmap(
        merge_bwd, [p_gate, pa, pb, dmerged], [P["b_gate"]],
        [(2 * D_MODEL, BF16), (D_MODEL, BF16), (D_MODEL, BF16)], [(1, 2 * D_MODEL)], tb=256, name="merge_bwd")
    dy_a = matmul(dpa, W["w_proj_rwkv"], "nt", "d_ya")
    gW["w_proj_rwkv"] = matmul(y_a, dpa, "tn", "dw_proj_a")
    dy_b = matmul(dpb, W["w_proj_attn"], "nt", "d_yb")
    gW["w_proj_attn"] = matmul(y_b, dpb, "tn", "dw_proj_b")

    def combine_bwd(o_b, l_b, d_b):
        return jax.vjp(_group_combine, o_b, l_b)[1](d_b)

    do, dlse = rowmap(combine_bwd, [o, lse, dy_b], [], [(ATTN_WIDTH, F32)] * 2, tb=256, name="attn_combine_bwd")
    dq_s, dk_s, dv_s = attn_bwd(q_s, k_s, v_s, to_subsequences(do), to_subsequences(dlse), S)

    def qk_norm_bwd(q_b, k_b, dqn_b, dkn_b, dv_b, qg, kg, sg, sgt, tl):
        f = lambda *a: _qk_norm(*a, sg, sgt, tl)
        dq, dk, dqg, dkg = jax.vjp(f, q_b, k_b, qg, kg)[1]((dqn_b, dkn_b))
        return jnp.concatenate([dq, dk, dv_b], axis=1), dqg, dkg

    dp_qkv, gP["attn_q_norm"], gP["attn_k_norm"] = rowmap(
        qk_norm_bwd, qk_rows + [from_subsequences(t, S) for t in (dq_s, dk_s, dv_s)], qk_params,
        [(3 * ATTN_WIDTH, BF16)], [(1, HEAD_DIM)] * 2, tb=256, name="qk_norm_bwd")

    def post_bwd(wkv_b, r_b, k_b, v_b, g_b, d_b, r_k, ln_w, ln_b, sg, sgt):
        f = lambda *a: _rwkv_post(*a, sg, sgt)
        return jax.vjp(f, wkv_b, r_b, k_b, v_b, g_b, r_k, ln_w, ln_b)[1](d_b)

    dwkv, dr_p, dk_p, dv_p, dg, gP["rwkv_r_k"], gP["rwkv_ln_w"], gP["rwkv_ln_b"] = rowmap(
        post_bwd, post_rows + [dy_a], post_params, [(D_MODEL, F32)] * 5, [(1, D_MODEL)] * 3, tb=128,
        name="rwkv_post_bwd")
    dr_w, dlw, dk_w, dv_w, da_neg, db_kk = wkv_bwd(xs_rk, lw, k_mod, a_neg, b_kk, states, t_invs, dwkv)

    def pre_bwd(xrk_b, xlo_b, dlw_b, dkw_b, dkp_b, da_b, db_b, dg_b, drp_b, drw_b, dvp_b, dvw_b,
                w0, w2, a0, a2, g2, k_k, k_a, sg, sgt):
        f = lambda *a: _rwkv_pre(*a, sg, sgt)
        pull = jax.vjp(f, xrk_b, xlo_b, w0, w2, a0, a2, g2, k_k, k_a)[1]
        dxrk, dxlo, *dpar = pull((dlw_b, dkw_b + dkp_b, da_b, db_b, dg_b))
        direct = jnp.concatenate([drp_b + drw_b, jnp.zeros_like(drp_b), dvp_b + dvw_b], axis=1)
        return (dxrk + direct, dxlo, *dpar)

    pre_rows = [xs_rk, xs_lo, dlw, dk_w, dk_p, da_neg, db_kk, dg, dr_p, dr_w, dv_p, dv_w]
    dxs_rk, dxs_lo, gP["rwkv_w0"], dw2p, gP["rwkv_a0"], da2p, dg2p, gP["rwkv_k_k"], gP["rwkv_k_a"] = rowmap(
        pre_bwd, pre_rows, pre_params, [(RKV, F32), (LORA, F32)],
        [(1, D_MODEL), (LORA, D_MODEL), (1, D_MODEL), (LORA, D_MODEL), (LORA, D_MODEL), (1, D_MODEL), (1, D_MODEL)],
        tb=128, name="rwkv_pre_bwd")
    gW["rwkv_w2"] = dw2p[:LORA_W]
    gW["rwkv_a2"] = da2p[LORA_W:LORA_W + LORA_A]
    gW["rwkv_g2"] = dg2p[LORA_W + LORA_A:]
    dp_rk, dmu_rk = token_shift_bwd(dxs_rk, p_rk, mu_rk, tb=256, name="shift_rk_bwd")
    dp_lo, dmu_lo = token_shift_bwd(dxs_lo, p_lo, mu_lo, tb=256, name="shift_lora_bwd")
    gP["rwkv_mu"] = jnp.concatenate([dmu_rk, dmu_lo], axis=1)

    dh = matmul(dp_rk, w_rkv, "nt", "dh_rkv")
    dh = matmul(dp_lo, w_lora, "nt", "dh_lora", add=dh)
    dh = matmul(dp_qkv, w_qkv, "nt", "dh_qkv", add=dh)
    dh = matmul(dp_gate, w_gate, "nt", "dh_gate", add=dh)
    gW["w_in"] = jnp.concatenate([
        matmul(h, dp_rk, "tn", "dw_rkv"), matmul(h, dp_lo, "tn", "dw_lora"),
        matmul(h, dp_qkv, "tn", "dw_qkv"), matmul(h, dp_gate, "tn", "dw_gate")], axis=1)

    def norm_bwd(x_b, dh_b, dy_b, gn):
        dx, dgn = jax.vjp(_rms, x_b, gn)[1](dh_b)
        return dy_b + dx, dgn

    dx1, gP["mix_norm"] = rowmap(norm_bwd, [x1, dh, dx2], [P["mix_norm"]], [(D_MODEL, F32)], [(1, D_MODEL)],
                                 tb=256, name="mix_norm_bwd")
    dx, gP["ffn1_norm"], gW["ffn1_w_in"], gW["ffn1_w_out"] = _ffn_bwd(
        dx1, ffn1_saved, P["ffn1_norm"], W["ffn1_w_in"], W["ffn1_w_out"], "ffn1")
    return loss_cols, dx, gW, gP


N_SHARDS = 4
BIG = (("ffn1_w_in", (D_MODEL, 2 * D_FF), 1), ("ffn1_w_out", (D_FF, D_MODEL), 0),
       ("w_in", (D_MODEL, 7712), 1), ("rwkv_w2", (LORA_W, D_MODEL), 1), ("rwkv_a2", (LORA_A, D_MODEL), 1),
       ("rwkv_g2", (LORA_G, D_MODEL), 1), ("w_proj_rwkv", (D_MODEL, D_MODEL), 0),
       ("w_proj_attn", (ATTN_WIDTH, D_MODEL), 1), ("w_out", (D_MODEL, D_MODEL), 0),
       ("ffn2_w_in", (D_MODEL, 2 * D_FF), 1), ("ffn2_w_out", (D_FF, D_MODEL), 0))
SMALL = (("ffn1_norm", 1024), ("mix_norm", 1024), ("b_gate", 2048), ("rwkv_mu", 3360), ("rwkv_w0", 1024),
         ("rwkv_a0", 1024), ("rwkv_k_k", 1024), ("rwkv_k_a", 1024), ("rwkv_r_k", 1024), ("rwkv_ln_w", 1024),
         ("rwkv_ln_b", 1024), ("attn_q_norm", 64), ("attn_k_norm", 64), ("ffn2_norm", 1024))
WEIGHT_ORDER = ("ffn1_norm", "ffn1_w_in", "ffn1_w_out", "mix_norm", "w_in", "b_gate", "rwkv_mu", "rwkv_w0",
                "rwkv_w2", "rwkv_a0", "rwkv_a2", "rwkv_g2", "rwkv_k_k", "rwkv_k_a", "rwkv_r_k", "rwkv_ln_w",
                "rwkv_ln_b", "attn_q_norm", "attn_k_norm", "w_proj_rwkv", "w_proj_attn", "w_out", "ffn2_norm",
                "ffn2_w_in", "ffn2_w_out")


LORA_PARTS = ("rwkv_w2", "rwkv_a2", "rwkv_g2")
BLOCK_MAJOR = ("ffn1_w_in", "ffn2_w_in")
SMALL_USED = D_MODEL + sum(n for _, n in SMALL)
SMALL_W = -(-SMALL_USED // 128) * 128


def _travel():
    out = {}
    for name, shape, axis in BIG:
        if name == LORA_PARTS[0]:
            out["lora"] = ((LORA, D_MODEL), 1)
        elif name not in LORA_PARTS:
            out[name] = (shape, axis)
    return out


def local_blocks(vals):
    out = {n: vals[n] for n in _travel() if n != "lora"}
    out["lora"] = jnp.concatenate([vals[n] for n in LORA_PARTS], axis=0)
    return out


def split_lora(t):
    return {"rwkv_w2": t[:LORA_W], "rwkv_a2": t[LORA_W:LORA_W + LORA_A], "rwkv_g2": t[LORA_W + LORA_A:]}


def blocks_to_full(name, blocks):
    shape, axis = _travel()[name]
    if name in BLOCK_MAJOR:
        return blocks
    if axis == 0:
        return blocks.reshape(shape)
    return blocks.transpose(1, 0, 2).reshape(shape)


def full_to_blocks(name, full):
    shape, axis = _travel()[name]
    if name in BLOCK_MAJOR:
        return full
    if axis == 0:
        return full.reshape(N_SHARDS, shape[0] // N_SHARDS, shape[1])
    return full.reshape(shape[0], N_SHARDS, shape[1] // N_SHARDS).transpose(1, 0, 2)


def pack_small(vals, head):
    parts = [head] + [vals[name].reshape(1, n) for name, n in SMALL]
    parts.append(jnp.zeros((1, SMALL_W - SMALL_USED), F32))
    return jnp.concatenate(parts, axis=1)


def unpack_small(vec, shapes):
    out, off = {}, D_MODEL
    for name, n in SMALL:
        out[name] = vec[:, off:off + n].reshape(shapes[name])
        off += n
    return out


def _place():
    return lax.axis_index("x"), lax.axis_index("y"), lax.axis_index("c")


def _other_chips(x, y):
    return [(1 - x, y), (x, 1 - y), (1 - x, 1 - y)]


def _remote(src, dst, send_sem, recv_sem, device):
    return pltpu.make_async_remote_copy(src_ref=src, dst_ref=dst, send_sem=send_sem, recv_sem=recv_sem,
                                        device_id=device, device_id_type=MESH)


def _half(ref, who):
    hr = ref.shape[-2] // 2
    rows = pl.ds(pl.multiple_of(who * hr, 8), hr)
    return ref.at[rows] if len(ref.shape) == 2 else ref.at[:, rows]


HBM_REF = pl.BlockSpec(memory_space=pl.ANY)
COMM_PARAMS = dict(compiler_params=pltpu.CompilerParams(has_side_effects=True))


def gather_weights(blocks):
    n = len(blocks)

    def body(*refs):
        ins, outs = refs[:n], refs[n:2 * n]
        ici_send, ici_recv, d2d_send, d2d_recv = refs[2 * n:]
        x, y, c = _place()
        me, sibling, chips = 2 * x + y, (x, y, 1 - c), _other_chips(x, y)
        first = [_remote(_half(ins[t], c), _half(outs[t].at[me], c), ici_send.at[k, t], ici_recv.at[k, t],
                         (px, py, c)) for k, (px, py) in enumerate(chips) for t in range(n)]
        for cp in first:
            cp.start()
        passed = []
        for k, (px, py) in enumerate(chips):
            for t in range(n):
                landed = _half(outs[t].at[2 * px + py], c)
                _remote(landed, landed, ici_send.at[k, t], ici_recv.at[k, t], (px, py, c)).wait_recv()
                cp = _remote(landed, landed, d2d_send.at[k, t], d2d_recv.at[k, t], sibling)
                cp.start()
                passed.append(cp)
        for k, (px, py) in enumerate(chips):
            for t in range(n):
                other = _half(outs[t].at[2 * px + py], 1 - c)
                _remote(other, other, d2d_send.at[k, t], d2d_recv.at[k, t], sibling).wait_recv()
        for cp in first + passed:
            cp.wait_send()

    res = pl.pallas_call(
        body, name="gather_weights", in_specs=[HBM_REF] * n, out_specs=[HBM_REF] * n,
        out_shape=[jax.ShapeDtypeStruct((N_SHARDS,) + b.shape, b.dtype) for b in blocks],
        scratch_shapes=[pltpu.SemaphoreType.DMA((3, n))] * 4, **COMM_PARAMS)(*blocks)
    me = 2 * lax.axis_index("x") + lax.axis_index("y")
    return [lax.dynamic_update_slice(g, b[None], (me, 0, 0)) for g, b in zip(res, blocks)]


def swap_halves(grads):
    n = len(grads)

    def body(*refs):
        ins, got = refs[:n], refs[n:2 * n]
        send_sems, recv_sems = refs[2 * n:]
        x, y, c = _place()
        give = [_remote(_half(ins[t], 1 - c), got[t], send_sems.at[t], recv_sems.at[t], (x, y, 1 - c))
                for t in range(n)]
        for cp in give:
            cp.start()
        for cp in give:
            cp.wait_recv()
        for cp in give:
            cp.wait_send()

    return pl.pallas_call(
        body, name="swap_halves", in_specs=[HBM_REF] * n, out_specs=[HBM_REF] * n,
        out_shape=[jax.ShapeDtypeStruct((g.shape[0], g.shape[1] // 2, g.shape[2]), g.dtype) for g in grads],
        scratch_shapes=[pltpu.SemaphoreType.DMA((n,))] * 2, **COMM_PARAMS)(*grads)


def scatter_partials(partials):
    n = len(partials)

    def body(*refs):
        parts, landed = refs[:n], refs[n:2 * n]
        send_sems, recv_sems = refs[2 * n:]
        x, y, c = _place()
        sends = [_remote(parts[t].at[2 * px + py], landed[t].at[k], send_sems.at[k, t], recv_sems.at[k, t],
                         (px, py, c)) for k, (px, py) in enumerate(_other_chips(x, y)) for t in range(n)]
        for cp in sends:
            cp.start()
        for cp in sends:
            cp.wait_recv()
        for cp in sends:
            cp.wait_send()

    return pl.pallas_call(
        body, name="scatter_partials", in_specs=[HBM_REF] * n, out_specs=[HBM_REF] * n,
        out_shape=[jax.ShapeDtypeStruct((3,) + p.shape[1:], p.dtype) for p in partials],
        scratch_shapes=[pltpu.SemaphoreType.DMA((3, n))] * 2, **COMM_PARAMS)(*partials)


def join_halves(blocks):
    n = len(blocks)

    def body(*refs):
        outs = refs[n:2 * n]
        send_sems, recv_sems = refs[2 * n:]
        x, y, c = _place()
        give = [_remote(_half(outs[t], c), _half(outs[t], c), send_sems.at[t], recv_sems.at[t], (x, y, 1 - c))
                for t in range(n)]
        for cp in give:
            cp.start()
        for t in range(n):
            arriving = _half(outs[t], 1 - c)
            _remote(arriving, arriving, send_sems.at[t], recv_sems.at[t], (x, y, 1 - c)).wait_recv()
        for cp in give:
            cp.wait_send()

    return pl.pallas_call(
        body, name="join_halves", in_specs=[HBM_REF] * n, out_specs=[HBM_REF] * n,
        out_shape=[jax.ShapeDtypeStruct(b.shape, b.dtype) for b in blocks],
        input_output_aliases={t: t for t in range(n)},
        scratch_shapes=[pltpu.SemaphoreType.DMA((n,))] * 2, **COMM_PARAMS)(*blocks)


def reduce_block_grads(grads):
    names = list(grads)
    got = swap_halves([grads[n] for n in names])
    partials = []
    for name, theirs in zip(names, got):
        n_slot, hr, width = theirs.shape
        tb = _row_block(hr, width, 6)
        per_half = hr // tb
        mine = lambda i, s, per_half=per_half: (i // per_half) * 2 * per_half + s[0] * per_half + i % per_half
        p = placed_map(
            jnp.add,
            [(grads[name].reshape(2 * n_slot * hr, width), mine), (theirs.reshape(n_slot * hr, width), lambda i, s: i)],
            (n_slot * hr, width, BF16, lambda i, s: i), n_blocks=n_slot * per_half, tb=tb, name="chip_sum_" + name)
        partials.append(p.reshape(theirs.shape))
    landed = scatter_partials(partials)
    blocks = []
    for name, theirs, arrived in zip(names, got, landed):
        n_slot, hr, width = theirs.shape
        tb = _row_block(hr, width, 6)
        per_half = hr // tb
        views = [(grads[name].reshape(2 * n_slot * hr, width),
                  lambda i, s, per_half=per_half: s[1] * 2 * per_half + s[0] * per_half + i),
                 (theirs.reshape(n_slot * hr, width), lambda i, s, per_half=per_half: s[1] * per_half + i)]
        views += [(arrived.reshape(3 * hr, width), functools.partial(lambda k, per_half, i, s: k * per_half + i,
                                                                     k, per_half)) for k in range(3)]
        f = lambda a, b, l0, l1, l2: (((a + b) + l0.astype(F32)) + l1.astype(F32)) + l2.astype(F32)
        blocks.append(placed_map(
            f, views,(2 * hr, width, F32, lambda i, s, per_half=per_half: s[0] * per_half + i),
            n_blocks=per_half, tb=tb, name="owner_sum_" + name))
    return dict(zip(names, join_halves(blocks)))


def adamw_block(name, w, g, m, v):
    rows, width = w.shape
    return rowmap(_adamw, [w, g, m, v], [], [(width, F32)] * 3, tb=_row_block(rows, width, 7),
                  name="adamw_" + name)


def reduce_small(vec, w, m, v):
    n_dev = 8

    def body(vec_ref, w_ref, m_ref, v_ref, loss_ref, g_ref, d_ref, m2_ref, v2_ref, slots, send_sems, recv_sems):
        x, y, c = _place()
        me = 4 * x + 2 * y + c
        slots[me] = vec_ref[...]
        flips = [(fx, fy, fc) for fx in (0, 1) for fy in (0, 1) for fc in (0, 1)][1:]
        peers = [(1 - x if fx else x, 1 - y if fy else y, 1 - c if fc else c) for fx, fy, fc in flips]
        sends = [pltpu.make_async_remote_copy(
            src_ref=vec_ref, dst_ref=slots.at[me], send_sem=send_sems.at[j], recv_sem=recv_sems.at[j],
            device_id=peer, device_id_type=MESH) for j, peer in enumerate(peers)]
        for cp in sends:
            cp.start()
        for j, (px, py, pc) in enumerate(peers):
            pltpu.make_async_remote_copy(
                src_ref=vec_ref, dst_ref=slots.at[4 * px + 2 * py + pc], send_sem=send_sems.at[j],
                recv_sem=recv_sems.at[j], device_id=(px, py, pc), device_id_type=MESH).wait_recv()
        for cp in sends:
            cp.wait_send()
        g = slots[0]
        for d in range(1, n_dev):
            g = g + slots[d]
        loss_ref[...] = jnp.sum(g[:, :D_MODEL], axis=1, keepdims=True)
        delta, m2, v2 = _adamw(w_ref[...], g, m_ref[...], v_ref[...])
        g_ref[...], d_ref[...], m2_ref[...], v2_ref[...] = g, delta, m2, v2

    vm = pl.BlockSpec(memory_space=pltpu.VMEM)
    vec_t = jax.ShapeDtypeStruct(vec.shape, F32)
    return pl.pallas_call(
        body, name="reduce_small", in_specs=[vm] * 4, out_specs=[vm] * 5,
        out_shape=[jax.ShapeDtypeStruct((1, 1), F32)] + [vec_t] * 4,
        scratch_shapes=[pltpu.VMEM((n_dev,) + vec.shape, F32), pltpu.SemaphoreType.DMA((n_dev - 1,)),
                        pltpu.SemaphoreType.DMA((n_dev - 1,))],
        compiler_params=pltpu.CompilerParams(has_side_effects=True),
    )(vec, w, m, v)


def kernel(x, ffn1_norm, ffn1_w_in, ffn1_w_out, mix_norm, w_in, b_gate, rwkv_mu, rwkv_w0, rwkv_w2, rwkv_a0, rwkv_a2, rwkv_g2, rwkv_k_k, rwkv_k_a, rwkv_r_k, rwkv_ln_w, rwkv_ln_b, attn_q_norm, attn_k_norm, w_proj_rwkv, w_proj_attn, w_out, ffn2_norm, ffn2_w_in, ffn2_w_out, loss_target, m_ffn1_norm, m_ffn1_w_in, m_ffn1_w_out, m_mix_norm, m_w_in, m_b_gate, m_rwkv_mu, m_rwkv_w0, m_rwkv_w2, m_rwkv_a0, m_rwkv_a2, m_rwkv_g2, m_rwkv_k_k, m_rwkv_k_a, m_rwkv_r_k, m_rwkv_ln_w, m_rwkv_ln_b, m_attn_q_norm, m_attn_k_norm, m_w_proj_rwkv, m_w_proj_attn, m_w_out, m_ffn2_norm, m_ffn2_w_in, m_ffn2_w_out, v_ffn1_norm, v_ffn1_w_in, v_ffn1_w_out, v_mix_norm, v_w_in, v_b_gate, v_rwkv_mu, v_rwkv_w0, v_rwkv_w2, v_rwkv_a0, v_rwkv_a2, v_rwkv_g2, v_rwkv_k_k, v_rwkv_k_a, v_rwkv_r_k, v_rwkv_ln_w, v_rwkv_ln_b, v_attn_q_norm, v_attn_k_norm, v_w_proj_rwkv, v_w_proj_attn, v_w_out, v_ffn2_norm, v_ffn2_w_in, v_ffn2_w_out):
    given = dict(locals())
    weights = {n: given[n] for n in WEIGHT_ORDER}
    mom_m = {n: given["m_" + n] for n in WEIGHT_ORDER}
    mom_v = {n: given["v_" + n] for n in WEIGHT_ORDER}
    big = [name for name, _, _ in BIG]
    shapes = {n: weights[n].shape for n in WEIGHT_ORDER}
    blocks_of = lambda d: local_blocks({n: d[n][0] for n in big})
    w_blk, m_blk, v_blk = blocks_of(weights), blocks_of(mom_m), blocks_of(mom_v)
    names = list(w_blk)

    gathered = gather_weights([w_blk[n].astype(BF16) for n in names])
    W = {n: blocks_to_full(n, g) for n, g in zip(names, gathered)}
    W.update(split_lora(W.pop("lora")))
    P = {n: weights[n].reshape(1, -1) for n, _ in SMALL}

    loss_cols, dx, gW, gP = layer_step(x[0], loss_target[0], W, P)

    gW["lora"] = jnp.concatenate([gW.pop(n) for n in LORA_PARTS], axis=0)
    g_blk = reduce_block_grads({n: full_to_blocks(n, gW[n]) for n in names})
    out_g, out_d, out_m, out_v = {}, {}, {}, {}
    for n in names:
        res = (g_blk[n], *adamw_block(n, w_blk[n], g_blk[n], m_blk[n], v_blk[n]))
        for dst, t in zip((out_g, out_d, out_m, out_v), res):
            for part, val in (split_lora(t) if n == "lora" else {n: t}).items():
                dst[part] = val.reshape(shapes[part])

    zero_head = jnp.zeros((1, D_MODEL), F32)
    vec = pack_small(gP, loss_cols)
    loss, g_s, d_s, m_s, v_s = reduce_small(
        vec, pack_small({n: weights[n] for n, _ in SMALL}, zero_head),
        pack_small({n: mom_m[n] for n, _ in SMALL}, zero_head),
        pack_small({n: mom_v[n] for n, _ in SMALL}, zero_head))
    for dst, src in ((out_g, g_s), (out_d, d_s), (out_m, m_s), (out_v, v_s)):
        dst.update(unpack_small(src, shapes))

    return (loss[0, 0], dx[None], *[out_g[n] for n in WEIGHT_ORDER], *[out_d[n] for n in WEIGHT_ORDER],
            *[out_m[n] for n in WEIGHT_ORDER], *[out_v[n] for n in WEIGHT_ORDER])
```

```python
import functools

import jax
import jax.numpy as jnp
from jax import lax
from jax.experimental import pallas as pl
from jax.experimental.pallas import tpu as pltpu

F32 = jnp.float32
BF16 = jnp.bfloat16
MESH = pl.DeviceIdType.MESH

D_MODEL = 1024
HEAD_DIM = 64
RWKV_HEADS = 16
LORA_W, LORA_A, LORA_G = 64, 64, 160
LORA = LORA_W + LORA_A + LORA_G
RKV = 3 * D_MODEL
ATTN_PAIRS = ((128, 1), (512, 4), (2048, 16))
ATTN_BLK = 128
ATTN_HPG = 4
ATTN_WIDTH = 768
GROUP_W = ATTN_HPG * HEAD_DIM
D_FF = 2816
GN_EPS = 64e-5
RMS_EPS = 1e-6
NEG_INF = -1e30
WKV_CHUNK = 64
WKV_HEADS_PER_STEP = 8

ADAM_LR, ADAM_B1, ADAM_B2, ADAM_EPS, ADAM_WD, ADAM_STEP = 0.001, 0.9, 0.999, 1e-08, 0.01, 10

V7X_VMEM_BYTES = 64 << 20
VMEM_TEMP_ALLOWANCE = 20 << 20


def _cparams(sem, block_bytes):
    limit = min(2 * block_bytes + VMEM_TEMP_ALLOWANCE, V7X_VMEM_BYTES - (6 << 20))
    return pltpu.CompilerParams(dimension_semantics=sem, vmem_limit_bytes=int(limit))


def _nbytes(shape, dtype):
    n = 1
    for s in shape:
        n *= s
    return n * jnp.dtype(dtype).itemsize


def _split_bf16(a):
    hi = a.astype(BF16)
    return hi, (a - hi.astype(F32)).astype(BF16)


def _make_dots():
    def raw(a, b, ca, cb):
        return lax.dot_general(a.astype(BF16), b.astype(BF16), (((ca,), (cb,)), ((), ())),
                               preferred_element_type=F32)

    @jax.custom_vjp
    def nn(a, b):
        return raw(a, b, 1, 0)

    @jax.custom_vjp
    def nt(a, b):
        return raw(a, b, 1, 1)

    @jax.custom_vjp
    def tn(a, b):
        return raw(a, b, 0, 0)

    nn.defvjp(lambda a, b: (raw(a, b, 1, 0), (a, b)),
              lambda res, g: (raw(g, res[1], 1, 1), raw(res[0], g, 0, 0)))
    nt.defvjp(lambda a, b: (raw(a, b, 1, 1), (a, b)),
              lambda res, g: (raw(g, res[1], 1, 0), raw(g, res[0], 0, 0)))
    tn.defvjp(lambda a, b: (raw(a, b, 0, 0), (a, b)),
              lambda res, g: (raw(res[1], g, 1, 1), raw(res[0], g, 1, 0)))
    return nn, nt, tn


def _exact_rhs_dot(x, ones, cx, co):
    hi, lo = _split_bf16(x)
    dims = (((cx,), (co,)), ((), ()))
    return (lax.dot_general(hi, ones, dims, preferred_element_type=F32)
            + lax.dot_general(lo, ones, dims, preferred_element_type=F32))


@jax.custom_vjp
def SEG(x, ones):
    return _exact_rhs_dot(x, ones, 1, 0)


SEG.defvjp(lambda x, ones: (_exact_rhs_dot(x, ones, 1, 0), ones),
           lambda ones, g: (_exact_rhs_dot(g, ones, 1, 1), jnp.zeros_like(ones)))

NN, NT, TN = _make_dots()


MM_TILE_M, MM_TILE_N, MM_TILE_K = 1408, 1408, 1536


def _pick(n, cap):
    best = None
    for t in range(128, min(n, cap) + 1, 128):
        if n % t == 0:
            best = t
    return best or n


def matmul(a, b, mode, name, *, add=None, scale=1.0, out_dtype=F32):
    if mode == "nn":
        (M, K), (K2, N) = a.shape, b.shape
    elif mode == "nt":
        (M, K), (N, K2) = a.shape, b.shape
    else:
        (K, M), (K2, N) = a.shape, b.shape
    assert K == K2, (name, a.shape, b.shape)
    tm, tn, tk = _pick(M, MM_TILE_M), _pick(N, MM_TILE_N), _pick(K, MM_TILE_K)
    nk = K // tk
    ca, cb = {"nn": (1, 0), "nt": (1, 1), "tn": (0, 0)}[mode]

    def body(*refs):
        if add is None:
            a_ref, b_ref, o_ref, acc_ref = refs
        else:
            a_ref, b_ref, add_ref, o_ref, acc_ref = refs
        k = pl.program_id(2)

        @pl.when(k == 0)
        def _():
            acc_ref[...] = jnp.zeros_like(acc_ref)

        acc_ref[...] += lax.dot_general(a_ref[...].astype(BF16), b_ref[...].astype(BF16),
                                        (((ca,), (cb,)), ((), ())), preferred_element_type=F32)

        @pl.when(k == nk - 1)
        def _():
            r = acc_ref[...] * scale
            if add is not None:
                r = add_ref[...] + r
            o_ref[...] = r.astype(o_ref.dtype)

    a_spec = (pl.BlockSpec((tk, tm), lambda i, j, k: (k, i)) if mode == "tn"
              else pl.BlockSpec((tm, tk), lambda i, j, k: (i, k)))
    b_spec = (pl.BlockSpec((tn, tk), lambda i, j, k: (j, k)) if mode == "nt"
              else pl.BlockSpec((tk, tn), lambda i, j, k: (k, j)))
    in_specs, args = [a_spec, b_spec], [a, b]
    blk = tm * tk * a.dtype.itemsize + tk * tn * b.dtype.itemsize + tm * tn * 8
    if add is not None:
        in_specs.append(pl.BlockSpec((tm, tn), lambda i, j, k: (i, j)))
        args.append(add)
        blk += tm * tn * 4
    return pl.pallas_call(
        body, name=name, grid=(M // tm, N // tn, nk),
        in_specs=in_specs, out_specs=pl.BlockSpec((tm, tn), lambda i, j, k: (i, j)),
        out_shape=jax.ShapeDtypeStruct((M, N), out_dtype),
        scratch_shapes=[pltpu.VMEM((tm, tn), F32)],
        compiler_params=_cparams(("parallel", "parallel", "arbitrary"), blk),
    )(*args)


def matmul_cs(a, w, mode, name, *, scale=1.0, out_dtype=F32):
    n_blk = N_SHARDS
    if mode == "tn":
        (K, R), Cs = a.shape, w.shape[1] // n_blk
        tm, tk = _pick(R, MM_TILE_M), _pick(K, 1024)
        grid = (R // tm, n_blk, K // tk)
        a_spec = pl.BlockSpec((tk, tm), lambda i, j, k: (k, i))
        w_spec = pl.BlockSpec((tk, Cs), lambda i, j, k: (k, j))
        o_spec = pl.BlockSpec((None, tm, Cs), lambda i, j, k: (j, i, 0))
        out_shape, acc_shape, dims = (n_blk, R, Cs), (tm, Cs), (0, 0)
        blk = tk * tm * a.dtype.itemsize + tk * Cs * w.dtype.itemsize + tm * Cs * 8
    elif mode == "nn":
        (M, R), Cs = a.shape, w.shape[2]
        tm, tk = _pick(M, MM_TILE_M), _pick(R, 1024)
        grid = (M // tm, n_blk, R // tk)
        a_spec = pl.BlockSpec((tm, tk), lambda i, j, k: (i, k))
        w_spec = pl.BlockSpec((None, tk, Cs), lambda i, j, k: (j, k, 0))
        o_spec = pl.BlockSpec((tm, Cs), lambda i, j, k: (i, j))
        out_shape, acc_shape, dims = (M, n_blk * Cs), (tm, Cs), (1, 0)
        blk = tm * tk * a.dtype.itemsize + tk * Cs * w.dtype.itemsize + tm * Cs * 8
    else:
        M, (_, R, Cs) = a.shape[0], w.shape
        tm, tn = _pick(M, MM_TILE_M), _pick(R, MM_TILE_N)
        grid = (M // tm, R // tn, n_blk)
        a_spec = pl.BlockSpec((tm, Cs), lambda i, j, k: (i, k))
        w_spec = pl.BlockSpec((None, tn, Cs), lambda i, j, k: (k, j, 0))
        o_spec = pl.BlockSpec((tm, tn), lambda i, j, k: (i, j))
        out_shape, acc_shape, dims = (M, R), (tm, tn), (1, 1)
        blk = tm * Cs * a.dtype.itemsize + tn * Cs * w.dtype.itemsize + tm * tn * 8
    nk = grid[2]

    def body(a_ref, w_ref, o_ref, acc_ref):
        k = pl.program_id(2)

        @pl.when(k == 0)
        def _():
            acc_ref[...] = jnp.zeros_like(acc_ref)

        acc_ref[...] += lax.dot_general(a_ref[...].astype(BF16), w_ref[...].astype(BF16),
                                        (((dims[0],), (dims[1],)), ((), ())), preferred_element_type=F32)

        @pl.when(k == nk - 1)
        def _():
            o_ref[...] = (acc_ref[...] * scale).astype(o_ref.dtype)

    return pl.pallas_call(
        body, name=name, grid=grid, in_specs=[a_spec, w_spec], out_specs=o_spec,
        out_shape=jax.ShapeDtypeStruct(out_shape, out_dtype), scratch_shapes=[pltpu.VMEM(acc_shape, F32)],
        compiler_params=_cparams(("parallel", "parallel", "arbitrary"), blk),
    )(a, w)


def _row_block(n, width, n_arrays):
    cap = (V7X_VMEM_BYTES // 4) // (2 * 4 * width * n_arrays)
    best = None
    for t in range(16, min(n, cap) + 1, 16):
        if n % t == 0:
            best = t
    return best or n


def placed_map(f, ins, out, *, n_blocks, tb, name):
    def body(*refs):
        refs[-1][...] = f(*[r[...] for r in refs[:-1]]).astype(refs[-1].dtype)

    def spec(fn):
        def index(i):
            x, y, c = _place()
            return fn(i, (c, 2 * x + y)), 0
        return pl.BlockSpec((tb, width), index)

    o_rows, width, o_dtype, o_fn = out
    blk = (sum(a.dtype.itemsize for a, _ in ins) + jnp.dtype(o_dtype).itemsize) * tb * width
    return pl.pallas_call(
        body, name=name, grid=(n_blocks,), in_specs=[spec(fn) for _, fn in ins], out_specs=spec(o_fn),
        out_shape=jax.ShapeDtypeStruct((o_rows, width), o_dtype),
        compiler_params=_cparams(("parallel",), blk),
    )(*[a for a, _ in ins])


def rowmap(f, rows, params, outs, accs=(), *, tb, name):
    rows = [r if isinstance(r, tuple) else (r, r.shape[1], 0) for r in rows]
    S = rows[0][0].shape[0]
    assert S % tb == 0, (name, S, tb)
    n_in, n_out = len(rows) + len(params), len(outs)

    def body(*refs):
        res = f(*[r[...] for r in refs[:n_in]])
        res = res if isinstance(res, (tuple, list)) else (res,)
        o_refs, a_refs = refs[n_in:n_in + n_out], refs[n_in + n_out:]
        for ref, val in zip(o_refs, res[:n_out]):
            ref[...] = val.astype(ref.dtype)
        if a_refs:
            @pl.when(pl.program_id(0) == 0)
            def _():
                for ref in a_refs:
                    ref[...] = jnp.zeros_like(ref)

            for ref, val in zip(a_refs, res[n_out:]):
                ref[...] += val.astype(F32)

    in_specs = [pl.BlockSpec((tb, w), functools.partial(lambda cb, i: (i, cb), cb)) for _, w, cb in rows]
    in_specs += [pl.BlockSpec(p.shape, lambda i: (0, 0)) for p in params]
    out_specs = [pl.BlockSpec((tb, w), lambda i: (i, 0)) for w, _ in outs]
    out_specs += [pl.BlockSpec(tuple(s), lambda i: (0, 0)) for s in accs]
    out_shape = [jax.ShapeDtypeStruct((S, w), dt) for w, dt in outs]
    out_shape += [jax.ShapeDtypeStruct(tuple(s), F32) for s in accs]
    blk = sum(tb * w * a.dtype.itemsize for a, w, _ in rows) + sum(_nbytes(p.shape, p.dtype) for p in params)
    blk += sum(_nbytes((tb, w), dt) for w, dt in outs) + sum(_nbytes(s, F32) for s in accs)
    res = pl.pallas_call(
        body, name=name, grid=(S // tb,), in_specs=in_specs, out_specs=out_specs, out_shape=out_shape,
        compiler_params=_cparams(("arbitrary",) if accs else ("parallel",), blk),
    )(*[r[0] for r in rows], *[pltpu.with_memory_space_constraint(p, pltpu.HBM) for p in params])
    return res


def _rms(x, g):
    return x * lax.rsqrt(jnp.mean(x * x, axis=-1, keepdims=True) + RMS_EPS) * g


def _softplus(z):
    return jnp.maximum(z, 0.0) + jnp.log(1.0 + jnp.exp(-jnp.abs(z)))


def _swiglu_act(gu):
    gate, up = gu[:, :D_FF], gu[:, D_FF:]
    return gate * jax.nn.sigmoid(gate) * up


def _rwkv_pre(xrk, xlo, w0, w2p, a0, a2p, g2p, k_k, k_a, seg, seg_t):
    k = xrk[:, D_MODEL:2 * D_MODEL]
    w = -_softplus(-(w0 + NN(jnp.tanh(xlo), w2p))) - 0.5
    log_decay = -jnp.exp(w)
    a = jax.nn.sigmoid(a0 + NN(xlo, a2p))
    g = NN(jax.nn.sigmoid(xlo), g2p)
    kk = k * k_k
    norm = jnp.maximum(jnp.sqrt(SEG(kk * kk, seg)), 1e-12)
    kk = kk * SEG(1.0 / norm, seg_t)
    k_mod = k * (1.0 + (a - 1.0) * k_a)
    return log_decay, k_mod, -kk, kk * a, g


def _rwkv_post(wkv, r, k_mod, v, g, r_k, ln_w, ln_b, seg, seg_t):
    inv_n = 1.0 / HEAD_DIM
    mean = SEG(wkv, seg) * inv_n
    cen = wkv - SEG(mean, seg_t)
    var = SEG(cen * cen, seg) * inv_n
    y = cen * SEG(lax.rsqrt(var + GN_EPS), seg_t) * ln_w + ln_b
    bonus = SEG(SEG(r * k_mod * r_k, seg), seg_t) * v
    return (y + bonus) * g


def _qk_norm(q, k, q_gain, k_gain, seg, seg_t, tile_t):
    def norm(x, gain):
        mean_sq = SEG(x * x, seg) * (1.0 / HEAD_DIM)
        return x * SEG(lax.rsqrt(mean_sq + RMS_EPS), seg_t) * SEG(gain, tile_t)

    return norm(q, q_gain) * (HEAD_DIM ** -0.5), norm(k, k_gain)


def _gate_merge(pgate, pa, pb, b_gate):
    sg = jax.nn.sigmoid(pgate + b_gate)
    return sg[:, :D_MODEL] * pa + sg[:, D_MODEL:] * pb


def _group_combine(o, lse):
    ls = [lse[:, GROUP_W * i:GROUP_W * (i + 1)] for i in range(3)]
    m = jnp.maximum(jnp.maximum(ls[0], ls[1]), ls[2])
    es = [jnp.exp(l - m) for l in ls]
    den = es[0] + es[1] + es[2]
    return jnp.concatenate([o[:, GROUP_W * i:GROUP_W * (i + 1)] * (es[i] / den) for i in range(3)], axis=1)


def _each(f, *xs):
    return tuple(f(*args) for args in zip(*xs))


def _attn_block(q, kc, kp, vc, vp, first):
    qi = lax.broadcasted_iota(jnp.int32, (ATTN_BLK, ATTN_BLK), 0)
    kj = lax.broadcasted_iota(jnp.int32, (ATTN_BLK, ATTN_BLK), 1)
    own, before = kj <= qi, (kj >= qi) & (first < 0.5)
    s_c = _each(lambda a, b: jnp.where(own, NT(a, b), NEG_INF), q, kc)
    s_p = _each(lambda a, b: jnp.where(before, NT(a, b), NEG_INF), q, kp)
    row_max = lambda s: jnp.max(s, axis=-1, keepdims=True)
    row_sum = lambda s: jnp.sum(s, axis=-1, keepdims=True)
    m = _each(lambda c_, p_: jnp.maximum(row_max(c_), row_max(p_)), s_c, s_p)
    e_c, e_p = _each(lambda s, m_: jnp.exp(s - m_), s_c, m), _each(lambda s, m_: jnp.exp(s - m_), s_p, m)
    den = _each(lambda c_, p_: row_sum(c_) + row_sum(p_), e_c, e_p)
    inv = _each(lambda d_: 1.0 / d_, den)
    o = _each(lambda ec, ep, i_, vc_, vp_: (NN(ec, vc_) + NN(ep, vp_)) * i_, e_c, e_p, inv, vc, vp)
    lse = _each(lambda m_, d_: jnp.broadcast_to(m_ + jnp.log(d_), (ATTN_BLK, HEAD_DIM)), m, den)
    return o, lse


TRI_SEED = 8


def _tri_inverse(n):
    c = n[0].shape[0]
    row = lax.broadcasted_iota(jnp.int32, (c, c), 0)
    col = lax.broadcasted_iota(jnp.int32, (c, c), 1)
    same_block = lambda size: (row >> (size.bit_length() - 1)) == (col >> (size.bit_length() - 1))
    seed = same_block(TRI_SEED)
    p = _each(lambda m: jnp.where(seed, m, 0.0), n)
    t, span = _each(lambda m: (row == col).astype(F32) + m, p), 2
    while span < TRI_SEED:
        p = _each(NN, p, p)
        t = _each(lambda t_, p_: t_ + NN(t_, p_), t, p)
        span *= 2
    size = TRI_SEED
    while size < c:
        joins = same_block(2 * size) & jnp.logical_not(same_block(size))
        t = _each(lambda t_, m: t_ + NN(NN(t_, jnp.where(joins, m, 0.0)), t_), t, n)
        size *= 2
    return t


@jax.custom_vjp
def _tri_solve(n, rhs, t):
    return _each(NN, t, rhs)


def _tri_solve_fwd(n, rhs, t):
    x = _each(NN, t, rhs)
    return x, (t, x)


def _tri_solve_bwd(res, dx):
    t, x = res
    drhs = _each(TN, t, dx)
    return _each(NT, drhs, x), drhs, _each(jnp.zeros_like, t)


_tri_solve.defvjp(_tri_solve_fwd, _tri_solve_bwd)


def _lower_ones(c):
    row = lax.broadcasted_iota(jnp.int32, (c, c), 0)
    col = lax.broadcasted_iota(jnp.int32, (c, c), 1)
    return (row >= col).astype(BF16)


def _ones_dot(ones, x, contract):
    hi, lo = _split_bf16(x)
    dims = (((contract,), (0,)), ((), ()))
    return (lax.dot_general(ones, hi, dims, preferred_element_type=F32)
            + lax.dot_general(ones, lo, dims, preferred_element_type=F32))


@jax.custom_vjp
def _cumsum_rows(x):
    return _ones_dot(_lower_ones(x.shape[0]), x, 1)


_cumsum_rows.defvjp(lambda x: (_ones_dot(_lower_ones(x.shape[0]), x, 1), None),
                    lambda _, g: (_ones_dot(_lower_ones(g.shape[0]), g, 0),))


def _wkv_chunk(s0, r, lw, k, v, a, b, t_inv=None):
    c = r[0].shape[0]
    row = lax.broadcasted_iota(jnp.int32, (c, c), 0)
    col = lax.broadcasted_iota(jnp.int32, (c, c), 1)
    strict, incl = row > col, row >= col
    cat = lambda p, q: jnp.concatenate([p, q], axis=0)
    cum = _each(_cumsum_rows, lw)
    e_neg = _each(lambda c_: jnp.exp(-c_), cum)
    ar = _each(lambda a_, r_, c_, l_: cat(a_ * jnp.exp(c_ - l_), r_ * jnp.exp(c_)), a, r, cum, lw)
    b_t, k_t = _each(jnp.multiply, b, e_neg), _each(jnp.multiply, k, e_neg)
    p_b, p_k, p_s = _each(NT, ar, b_t), _each(NT, ar, k_t), _each(NT, ar, s0)
    n_ab = _each(lambda p: jnp.where(strict, p[:c], 0.0), p_b)
    m_rb = _each(lambda p: jnp.where(incl, p[c:], 0.0), p_b)
    n_ak = _each(lambda p: jnp.where(strict, p[:c], 0.0), p_k)
    m_rk = _each(lambda p: jnp.where(incl, p[c:], 0.0), p_k)
    if t_inv is None:
        t_inv = _tri_inverse(n_ab)
    u = _tri_solve(n_ab, _each(lambda p, n_, v_: p[:c] + NN(n_, v_), p_s, n_ak, v), t_inv)
    y = _each(lambda p, mb, u_, mk, v_: p[c:] + NN(mb, u_) + NN(mk, v_), p_s, m_rb, u, m_rk, v)
    g_end = _each(lambda l_: jnp.exp(jnp.sum(l_, axis=0, keepdims=True)), lw)
    s1 = _each(lambda s_, g_, u_, v_, b_, k_: s_ * g_ + TN(cat(u_, v_), cat(b_, k_) * g_),
               s0, g_end, u, v, b_t, k_t)
    return y, s1, t_inv


def _adamw(w, g, m, v):
    m = ADAM_B1 * m + (1.0 - ADAM_B1) * g
    v = ADAM_B2 * v + (1.0 - ADAM_B2) * jnp.square(g)
    m_hat = m / (1.0 - ADAM_B1 ** ADAM_STEP)
    v_hat = v / (1.0 - ADAM_B2 ** ADAM_STEP)
    delta = -ADAM_LR * (m_hat / (jnp.sqrt(v_hat) + ADAM_EPS) + ADAM_WD * w)
    return delta, m, v


def token_shift_fwd(p, mu, *, tb, name):
    S, W = p.shape
    hb = tb // 8

    def body(p_ref, halo_ref, mu_ref, o_ref):
        i = pl.program_id(0)
        x = p_ref[...]
        before = halo_ref[7:8, :] * (i > 0).astype(F32)
        row = lax.broadcasted_iota(jnp.int32, (tb, W), 0)
        prev = jnp.where(row == 0, before, pltpu.roll(x, 1, 0))
        o_ref[...] = x + (prev - x) * mu_ref[...]

    blk = (2 * tb + 8) * W * 4
    return pl.pallas_call(
        body, name=name, grid=(S // tb,),
        in_specs=[pl.BlockSpec((tb, W), lambda i: (i, 0)),
                  pl.BlockSpec((8, W), lambda i: (jnp.maximum(i * hb - 1, 0), 0)),
                  pl.BlockSpec((1, W), lambda i: (0, 0))],
        out_specs=pl.BlockSpec((tb, W), lambda i: (i, 0)),
        out_shape=jax.ShapeDtypeStruct((S, W), F32),
        compiler_params=_cparams(("parallel",), blk),
    )(p, p, mu)


def token_shift_bwd(dxs, p, mu, *, tb, name):
    S, W = p.shape
    hb, nb = tb // 8, S // tb

    def body(d_ref, dnext_ref, p_ref, halo_ref, mu_ref, dp_ref, dmu_ref):
        i = pl.program_id(0)
        d, x, mu_v = d_ref[...], p_ref[...], mu_ref[...]
        row = lax.broadcasted_iota(jnp.int32, (tb, W), 0)
        before = halo_ref[7:8, :] * (i > 0).astype(F32)
        prev = jnp.where(row == 0, before, pltpu.roll(x, 1, 0))
        t = d * mu_v
        after = dnext_ref[0:1, :] * mu_v * (i < nb - 1).astype(F32)
        nxt = jnp.where(row == tb - 1, after, pltpu.roll(t, tb - 1, 0))
        dp_ref[...] = (d - t + nxt).astype(dp_ref.dtype)

        @pl.when(i == 0)
        def _():
            dmu_ref[...] = jnp.zeros_like(dmu_ref)

        dmu_ref[...] += jnp.sum(d * (prev - x), axis=0, keepdims=True)

    blk = (3 * tb + 16) * W * 4
    return pl.pallas_call(
        body, name=name, grid=(nb,),
        in_specs=[pl.BlockSpec((tb, W), lambda i: (i, 0)),
                  pl.BlockSpec((8, W), lambda i: (jnp.minimum((i + 1) * hb, S // 8 - 1), 0)),
                  pl.BlockSpec((tb, W), lambda i: (i, 0)),
                  pl.BlockSpec((8, W), lambda i: (jnp.maximum(i * hb - 1, 0), 0)),
                  pl.BlockSpec((1, W), lambda i: (0, 0))],
        out_specs=[pl.BlockSpec((tb, W), lambda i: (i, 0)), pl.BlockSpec((1, W), lambda i: (0, 0))],
        out_shape=[jax.ShapeDtypeStruct((S, W), BF16), jax.ShapeDtypeStruct((1, W), F32)],
        compiler_params=_cparams(("arbitrary",), blk),
    )(dxs, dxs, p, p, mu)


def _head_cols(h):
    return pl.ds(h * HEAD_DIM, HEAD_DIM)


def wkv_fwd(xs_rk, lw, k, a, b):
    S = lw.shape[0]
    C, nc, G, N = WKV_CHUNK, S // WKV_CHUNK, WKV_HEADS_PER_STEP, HEAD_DIM

    def body(r_ref, lw_ref, k_ref, v_ref, a_ref, b_ref, y_ref, st_ref, ti_ref, state):
        @pl.when(pl.program_id(1) == 0)
        def _():
            state[...] = jnp.zeros_like(state)

        heads = lambda ref: tuple(ref[:, _head_cols(h)] for h in range(G))
        s0 = tuple(state[h] for h in range(G))
        y, s1, t_inv = _wkv_chunk(s0, heads(r_ref), heads(lw_ref), heads(k_ref), heads(v_ref), heads(a_ref),
                                  heads(b_ref))
        for h in range(G):
            st_ref[h] = s0[h]
            ti_ref[h] = t_inv[h]
            y_ref[:, _head_cols(h)] = y[h]
            state[h] = s1[h]

    W = G * N
    seq = lambda j: pl.BlockSpec((C, W), functools.partial(lambda j, g, c: (c, j + g), j))
    per = D_MODEL // W
    per_chunk = pl.BlockSpec((None, G, N, N), lambda g, c: (c, g, 0, 0))
    return pl.pallas_call(
        body, name="wkv_fwd", grid=(RWKV_HEADS // G, nc),
        in_specs=[seq(0), seq(0), seq(0), seq(2 * per), seq(0), seq(0)],
        out_specs=[seq(0), per_chunk, per_chunk],
        out_shape=[jax.ShapeDtypeStruct((S, D_MODEL), F32)] + [jax.ShapeDtypeStruct((nc, RWKV_HEADS, N, N), F32)] * 2,
        scratch_shapes=[pltpu.VMEM((G, N, N), F32)],
        compiler_params=_cparams(("parallel", "arbitrary"), 8 * C * W * 4 + 3 * G * N * N * 4),
    )(xs_rk, lw, k, xs_rk, a, b)


def wkv_bwd(xs_rk, lw, k, a, b, states, t_invs, dy):
    S = lw.shape[0]
    C, nc, G, N = WKV_CHUNK, S // WKV_CHUNK, WKV_HEADS_PER_STEP, HEAD_DIM

    def body(r_ref, lw_ref, k_ref, v_ref, a_ref, b_ref, st_ref, ti_ref, dy_ref,
             dr_ref, dlw_ref, dk_ref, dv_ref, da_ref, db_ref, dstate):
        @pl.when(pl.program_id(1) == 0)
        def _():
            dstate[...] = jnp.zeros_like(dstate)

        heads = lambda ref: tuple(ref[:, _head_cols(h)] for h in range(G))
        t_inv = tuple(ti_ref[h] for h in range(G))
        chunk = lambda *args: _wkv_chunk(*args, t_inv)[:2]
        _, pull = jax.vjp(chunk, tuple(st_ref[h] for h in range(G)), heads(r_ref), heads(lw_ref),
                          heads(k_ref), heads(v_ref), heads(a_ref), heads(b_ref))
        ds0, *grads = pull((heads(dy_ref), tuple(dstate[h] for h in range(G))))
        for h in range(G):
            dstate[h] = ds0[h]
            for ref, grad in zip((dr_ref, dlw_ref, dk_ref, dv_ref, da_ref, db_ref), grads):
                ref[:, _head_cols(h)] = grad[h]

    W = G * N
    seq = lambda j: pl.BlockSpec((C, W), functools.partial(lambda j, g, c: (nc - 1 - c, j + g), j))
    per = D_MODEL // W
    st = pl.BlockSpec((None, G, N, N), lambda g, c: (nc - 1 - c, g, 0, 0))
    return pl.pallas_call(
        body, name="wkv_bwd", grid=(RWKV_HEADS // G, nc),
        in_specs=[seq(0), seq(0), seq(0), seq(2 * per), seq(0), seq(0), st, st, seq(0)],
        out_specs=[seq(0)] * 6, out_shape=[jax.ShapeDtypeStruct((S, D_MODEL), F32)] * 6,
        scratch_shapes=[pltpu.VMEM((G, N, N), F32)],
        compiler_params=_cparams(("parallel", "arbitrary"), 14 * C * W * 4 + 3 * G * N * N * 4),
    )(xs_rk, lw, k, xs_rk, a, b, states, t_invs, dy)


def _first_flag(i, seq_len):
    per_group = seq_len // ATTN_BLK
    g = i // per_group
    per_seq = [seq_len // d // ATTN_BLK for _, d in ATTN_PAIRS]
    n = jnp.where(g == 0, per_seq[0], jnp.where(g == 1, per_seq[1], per_seq[2]))
    return (lax.rem(i, n) == 0).astype(F32)


def attn_fwd(q, k, v, seq_len):
    R, N = q.shape
    nb = R // ATTN_BLK

    def body(q_ref, kc_ref, kp_ref, vc_ref, vp_ref, o_ref, lse_ref):
        first = _first_flag(pl.program_id(0), seq_len)
        heads = lambda ref: tuple(ref[:, _head_cols(h)] for h in range(ATTN_HPG))
        o, lse = _attn_block(heads(q_ref), heads(kc_ref), heads(kp_ref), heads(vc_ref), heads(vp_ref), first)
        for h in range(ATTN_HPG):
            o_ref[:, _head_cols(h)] = o[h]
            lse_ref[:, _head_cols(h)] = lse[h]

    cur = pl.BlockSpec((ATTN_BLK, N), lambda i: (i, 0))
    prv = pl.BlockSpec((ATTN_BLK, N), lambda i: (jnp.maximum(i - 1, 0), 0))
    return pl.pallas_call(
        body, name="attn_fwd", grid=(nb,), in_specs=[cur, cur, prv, cur, prv],
        out_specs=[cur, cur], out_shape=[jax.ShapeDtypeStruct((R, N), F32)] * 2,
        compiler_params=_cparams(("parallel",), 7 * ATTN_BLK * N * 4),
    )(q, k, k, v, v)


def attn_bwd(q, k, v, do, dlse, seq_len):
    R, N = q.shape
    nb = R // ATTN_BLK

    def body(q_ref, kc_ref, kp_ref, vc_ref, vp_ref, do_ref, dl_ref, dq_ref, dk_ref, dv_ref, carry_k, carry_v):
        step = pl.program_id(0)
        first = _first_flag(nb - 1 - step, seq_len)

        @pl.when(step == 0)
        def _():
            carry_k[...] = jnp.zeros_like(carry_k)
            carry_v[...] = jnp.zeros_like(carry_v)

        heads = lambda ref: tuple(ref[:, _head_cols(h)] for h in range(ATTN_HPG))
        _, pull = jax.vjp(functools.partial(_attn_block, first=first), heads(q_ref), heads(kc_ref), heads(kp_ref),
                          heads(vc_ref), heads(vp_ref))
        dq, dkc, dkp, dvc, dvp = pull((heads(do_ref), heads(dl_ref)))
        old_k, old_v = heads(carry_k), heads(carry_v)
        for h in range(ATTN_HPG):
            cols = _head_cols(h)
            dq_ref[:, cols] = dq[h]
            dk_ref[:, cols] = dkc[h] + old_k[h]
            dv_ref[:, cols] = dvc[h] + old_v[h]
            carry_k[:, cols] = dkp[h]
            carry_v[:, cols] = dvp[h]

    cur = pl.BlockSpec((ATTN_BLK, N), lambda i: (nb - 1 - i, 0))
    prv = pl.BlockSpec((ATTN_BLK, N), lambda i: (jnp.maximum(nb - 2 - i, 0), 0))
    return pl.pallas_call(
        body, name="attn_bwd", grid=(nb,), in_specs=[cur, cur, prv, cur, prv, cur, cur],
        out_specs=[cur, cur, cur], out_shape=[jax.ShapeDtypeStruct((R, N), F32)] * 3,
        scratch_shapes=[pltpu.VMEM((ATTN_BLK, N), F32)] * 2,
        compiler_params=_cparams(("arbitrary",), 12 * ATTN_BLK * N * 4),
    )(q, k, k, v, v, do, dlse)


def to_subsequences(t):
    S = t.shape[0]
    parts = []
    for gi, (_, d) in enumerate(ATTN_PAIRS):
        tg = t[:, GROUP_W * gi:GROUP_W * (gi + 1)].reshape(S // d, d, GROUP_W)
        parts.append(tg.transpose(1, 0, 2).reshape(S, GROUP_W))
    return jnp.concatenate(parts, axis=0)


def from_subsequences(u, S):
    parts = []
    for gi, (_, d) in enumerate(ATTN_PAIRS):
        ug = u[S * gi:S * (gi + 1)].reshape(d, S // d, GROUP_W)
        parts.append(ug.transpose(1, 0, 2).reshape(S, GROUP_W))
    return jnp.concatenate(parts, axis=1)


def _ffn_fwd(x, norm, w_in, w_out, tag):
    h = rowmap(_rms, [x], [norm], [(D_MODEL, BF16)], tb=512, name=tag + "_norm")[0]
    gu = matmul_cs(h, w_in, "nn", tag + "_in")
    act = rowmap(_swiglu_act, [gu], [], [(D_FF, BF16)], tb=256, name=tag + "_act")[0]
    y = matmul(act, w_out, "nn", tag + "_out", add=x, scale=0.5)
    return y, (x, h, gu, act)


def _ffn_bwd(dy, saved, norm, w_in, w_out, tag):
    x, h, gu, act = saved
    dact = matmul(dy, w_out, "nt", tag + "_dact", scale=0.5)
    dw_out = matmul(act, dy, "tn", tag + "_dwout", scale=0.5)

    def act_bwd(gu_b, dact_b):
        return jax.vjp(_swiglu_act, gu_b)[1](dact_b)[0]

    dgu = rowmap(act_bwd, [gu, dact], [], [(2 * D_FF, BF16)], tb=256, name=tag + "_dgu")[0]
    dh = matmul_cs(dgu, w_in, "nt", tag + "_dh")
    dw_in = matmul_cs(h, dgu, "tn", tag + "_dwin")

    def norm_bwd(x_b, dh_b, dy_b, g):
        dx, dg = jax.vjp(_rms, x_b, g)[1](dh_b)
        return dy_b + dx, dg

    dx, dnorm = rowmap(norm_bwd, [x, dh, dy], [norm], [(D_MODEL, F32)], [(1, D_MODEL)], tb=256,
                       name=tag + "_dnorm")
    return dx, dnorm, dw_in, dw_out


def layer_step(x, tgt, W, P):
    S = x.shape[0]
    head_of = lambda n: jnp.arange(n)[:, None] // HEAD_DIM == jnp.arange(n // HEAD_DIM)[None, :]
    seg, seg_a = head_of(D_MODEL).astype(BF16), head_of(ATTN_WIDTH).astype(BF16)
    seg_t, seg_a_t = seg.T, seg_a.T
    tile_t = (jnp.arange(HEAD_DIM)[:, None] == jnp.arange(ATTN_WIDTH)[None, :] % HEAD_DIM).astype(BF16)
    qk_params = [P["attn_q_norm"], P["attn_k_norm"], seg_a, seg_a_t, tile_t]
    w_rkv, w_lora = W["w_in"][:, :RKV], W["w_in"][:, RKV:RKV + LORA]
    w_qkv = W["w_in"][:, RKV + LORA:RKV + LORA + 3 * ATTN_WIDTH]
    w_gate = W["w_in"][:, RKV + LORA + 3 * ATTN_WIDTH:]
    mu_rk, mu_lo = P["rwkv_mu"][:, :RKV], P["rwkv_mu"][:, RKV:]
    zeros = lambda n: jnp.zeros((n, D_MODEL), F32)
    w2p = jnp.concatenate([W["rwkv_w2"], zeros(LORA - LORA_W)], axis=0)
    a2p = jnp.concatenate([zeros(LORA_W), W["rwkv_a2"], zeros(LORA_G)], axis=0)
    g2p = jnp.concatenate([zeros(LORA_W + LORA_A), W["rwkv_g2"]], axis=0)
    pre_params = [P["rwkv_w0"], w2p, P["rwkv_a0"], a2p, g2p, P["rwkv_k_k"], P["rwkv_k_a"], seg, seg_t]
    post_params = [P["rwkv_r_k"], P["rwkv_ln_w"], P["rwkv_ln_b"], seg, seg_t]
    col = lambda arr, j: (arr, D_MODEL, j)

    x1, ffn1_saved = _ffn_fwd(x, P["ffn1_norm"], W["ffn1_w_in"], W["ffn1_w_out"], "ffn1")
    h = rowmap(_rms, [x1], [P["mix_norm"]], [(D_MODEL, BF16)], tb=512, name="mix_norm")[0]
    p_rk = matmul(h, w_rkv, "nn", "proj_rkv")
    p_lo = matmul(h, w_lora, "nn", "proj_lora")
    p_qkv = matmul(h, w_qkv, "nn", "proj_qkv")
    p_gate = matmul(h, w_gate, "nn", "proj_gate")
    xs_rk = token_shift_fwd(p_rk, mu_rk, tb=256, name="shift_rk")
    xs_lo = token_shift_fwd(p_lo, mu_lo, tb=256, name="shift_lora")
    lw, k_mod, a_neg, b_kk, g = rowmap(
        _rwkv_pre, [xs_rk, xs_lo], pre_params, [(D_MODEL, F32)] * 5, tb=256, name="rwkv_pre")
    wkv, states, t_invs = wkv_fwd(xs_rk, lw, k_mod, a_neg, b_kk)
    post_rows = [wkv, col(xs_rk, 0), k_mod, col(xs_rk, 2), g]
    y_a = rowmap(_rwkv_post, post_rows, post_params, [(D_MODEL, BF16)], tb=256, name="rwkv_post")[0]

    qk_rows = [(p_qkv, ATTN_WIDTH, 0), (p_qkv, ATTN_WIDTH, 1)]
    qn, kn = rowmap(_qk_norm, qk_rows, qk_params, [(ATTN_WIDTH, F32)] * 2, tb=256, name="qk_norm")
    q_s, k_s, v_s = to_subsequences(qn), to_subsequences(kn), to_subsequences(p_qkv[:, 2 * ATTN_WIDTH:])
    o_s, lse_s = attn_fwd(q_s, k_s, v_s, S)
    o, lse = from_subsequences(o_s, S), from_subsequences(lse_s, S)
    y_b = rowmap(_group_combine, [o, lse], [], [(ATTN_WIDTH, BF16)], tb=512, name="attn_combine")[0]

    pa = matmul(y_a, W["w_proj_rwkv"], "nn", "proj_a")
    pb = matmul(y_b, W["w_proj_attn"], "nn", "proj_b")
    merged = rowmap(_gate_merge, [p_gate, pa, pb], [P["b_gate"]], [(D_MODEL, BF16)], tb=256, name="merge")[0]
    x2 = matmul(merged, W["w_out"], "nn", "mix_out", add=x1)
    x3, ffn2_saved = _ffn_fwd(x2, P["ffn2_norm"], W["ffn2_w_in"], W["ffn2_w_out"], "ffn2")

    def loss_head(y_b_, t_b):
        err = y_b_ - t_b
        return err * (1.0 / D_MODEL), (0.5 / D_MODEL) * jnp.sum(err * err, axis=0, keepdims=True)

    dx3, loss_cols = rowmap(loss_head, [x3, tgt], [], [(D_MODEL, F32)], [(1, D_MODEL)], tb=512, name="loss")

    gW, gP = {}, {}
    dx2, gP["ffn2_norm"], gW["ffn2_w_in"], gW["ffn2_w_out"] = _ffn_bwd(
        dx3, ffn2_saved, P["ffn2_norm"], W["ffn2_w_in"], W["ffn2_w_out"], "ffn2")

    dmerged = matmul(dx2, W["w_out"], "nt", "d_merged")
    gW["w_out"] = matmul(merged, dx2, "tn", "dw_out")

    def merge_bwd(pg, pa_b, pb_b, dm, bg):
        return jax.vjp(_gate_merge, pg, pa_b, pb_b, bg)[1](dm)

    dp_gate, dpa, dpb, gP["b_gate"] = rowmap(
        merge_bwd, [p_gate, pa, pb, dmerged], [P["b_gate"]],
        [(2 * D_MODEL, BF16), (D_MODEL, BF16), (D_MODEL, BF16)], [(1, 2 * D_MODEL)], tb=256, name="merge_bwd")
    dy_a = matmul(dpa, W["w_proj_rwkv"], "nt", "d_ya")
    gW["w_proj_rwkv"] = matmul(y_a, dpa, "tn", "dw_proj_a")
    dy_b = matmul(dpb, W["w_proj_attn"], "nt", "d_yb")
    gW["w_proj_attn"] = matmul(y_b, dpb, "tn", "dw_proj_b")

    def combine_bwd(o_b, l_b, d_b):
        return jax.vjp(_group_combine, o_b, l_b)[1](d_b)

    do, dlse = rowmap(combine_bwd, [o, lse, dy_b], [], [(ATTN_WIDTH, F32)] * 2, tb=256, name="attn_combine_bwd")
    dq_s, dk_s, dv_s = attn_bwd(q_s, k_s, v_s, to_subsequences(do), to_subsequences(dlse), S)

    def qk_norm_bwd(q_b, k_b, dqn_b, dkn_b, dv_b, qg, kg, sg, sgt, tl):
        f = lambda *a: _qk_norm(*a, sg, sgt, tl)
        dq, dk, dqg, dkg = jax.vjp(f, q_b, k_b, qg, kg)[1]((dqn_b, dkn_b))
        return jnp.concatenate([dq, dk, dv_b], axis=1), dqg, dkg

    dp_qkv, gP["attn_q_norm"], gP["attn_k_norm"] = rowmap(
        qk_norm_bwd, qk_rows + [from_subsequences(t, S) for t in (dq_s, dk_s, dv_s)], qk_params,
        [(3 * ATTN_WIDTH, BF16)], [(1, HEAD_DIM)] * 2, tb=256, name="qk_norm_bwd")

    def post_bwd(wkv_b, r_b, k_b, v_b, g_b, d_b, r_k, ln_w, ln_b, sg, sgt):
        f = lambda *a: _rwkv_post(*a, sg, sgt)
        return jax.vjp(f, wkv_b, r_b, k_b, v_b, g_b, r_k, ln_w, ln_b)[1](d_b)

    dwkv, dr_p, dk_p, dv_p, dg, gP["rwkv_r_k"], gP["rwkv_ln_w"], gP["rwkv_ln_b"] = rowmap(
        post_bwd, post_rows + [dy_a], post_params, [(D_MODEL, F32)] * 5, [(1, D_MODEL)] * 3, tb=128,
        name="rwkv_post_bwd")
    dr_w, dlw, dk_w, dv_w, da_neg, db_kk = wkv_bwd(xs_rk, lw, k_mod, a_neg, b_kk, states, t_invs, dwkv)

    def pre_bwd(xrk_b, xlo_b, dlw_b, dkw_b, dkp_b, da_b, db_b, dg_b, drp_b, drw_b, dvp_b, dvw_b,
                w0, w2, a0, a2, g2, k_k, k_a, sg, sgt):
        f = lambda *a: _rwkv_pre(*a, sg, sgt)
        pull = jax.vjp(f, xrk_b, xlo_b, w0, w2, a0, a2, g2, k_k, k_a)[1]
        dxrk, dxlo, *dpar = pull((dlw_b, dkw_b + dkp_b, da_b, db_b, dg_b))
        direct = jnp.concatenate([drp_b + drw_b, jnp.zeros_like(drp_b), dvp_b + dvw_b], axis=1)
        return (dxrk + direct, dxlo, *dpar)

    pre_rows = [xs_rk, xs_lo, dlw, dk_w, dk_p, da_neg, db_kk, dg, dr_p, dr_w, dv_p, dv_w]
    dxs_rk, dxs_lo, gP["rwkv_w0"], dw2p, gP["rwkv_a0"], da2p, dg2p, gP["rwkv_k_k"], gP["rwkv_k_a"] = rowmap(
        pre_bwd, pre_rows, pre_params, [(RKV, F32), (LORA, F32)],
        [(1, D_MODEL), (LORA, D_MODEL), (1, D_MODEL), (LORA, D_MODEL), (LORA, D_MODEL), (1, D_MODEL), (1, D_MODEL)],
        tb=128, name="rwkv_pre_bwd")
    gW["rwkv_w2"] = dw2p[:LORA_W]
    gW["rwkv_a2"] = da2p[LORA_W:LORA_W + LORA_A]
    gW["rwkv_g2"] = dg2p[LORA_W + LORA_A:]
    dp_rk, dmu_rk = token_shift_bwd(dxs_rk, p_rk, mu_rk, tb=256, name="shift_rk_bwd")
    dp_lo, dmu_lo = token_shift_bwd(dxs_lo, p_lo, mu_lo, tb=256, name="shift_lora_bwd")
    gP["rwkv_mu"] = jnp.concatenate([dmu_rk, dmu_lo], axis=1)

    dh = matmul(dp_rk, w_rkv, "nt", "dh_rkv")
    dh = matmul(dp_lo, w_lora, "nt", "dh_lora", add=dh)
    dh = matmul(dp_qkv, w_qkv, "nt", "dh_qkv", add=dh)
    dh = matmul(dp_gate, w_gate, "nt", "dh_gate", add=dh)
    gW["w_in"] = jnp.concatenate([
        matmul(h, dp_rk, "tn", "dw_rkv"), matmul(h, dp_lo, "tn", "dw_lora"),
        matmul(h, dp_qkv, "tn", "dw_qkv"), matmul(h, dp_gate, "tn", "dw_gate")], axis=1)

    def norm_bwd(x_b, dh_b, dy_b, gn):
        dx, dgn = jax.vjp(_rms, x_b, gn)[1](dh_b)
        return dy_b + dx, dgn

    dx1, gP["mix_norm"] = rowmap(norm_bwd, [x1, dh, dx2], [P["mix_norm"]], [(D_MODEL, F32)], [(1, D_MODEL)],
                                 tb=256, name="mix_norm_bwd")
    dx, gP["ffn1_norm"], gW["ffn1_w_in"], gW["ffn1_w_out"] = _ffn_bwd(
        dx1, ffn1_saved, P["ffn1_norm"], W["ffn1_w_in"], W["ffn1_w_out"], "ffn1")
    return loss_cols, dx, gW, gP


N_SHARDS = 4
BIG = (("ffn1_w_in", (D_MODEL, 2 * D_FF), 1), ("ffn1_w_out", (D_FF, D_MODEL), 0),
       ("w_in", (D_MODEL, 7712), 1), ("rwkv_w2", (LORA_W, D_MODEL), 1), ("rwkv_a2", (LORA_A, D_MODEL), 1),
       ("rwkv_g2", (LORA_G, D_MODEL), 1), ("w_proj_rwkv", (D_MODEL, D_MODEL), 0),
       ("w_proj_attn", (ATTN_WIDTH, D_MODEL), 1), ("w_out", (D_MODEL, D_MODEL), 0),
       ("ffn2_w_in", (D_MODEL, 2 * D_FF), 1), ("ffn2_w_out", (D_FF, D_MODEL), 0))
SMALL = (("ffn1_norm", 1024), ("mix_norm", 1024), ("b_gate", 2048), ("rwkv_mu", 3360), ("rwkv_w0", 1024),
         ("rwkv_a0", 1024), ("rwkv_k_k", 1024), ("rwkv_k_a", 1024), ("rwkv_r_k", 1024), ("rwkv_ln_w", 1024),
         ("rwkv_ln_b", 1024), ("attn_q_norm", 64), ("attn_k_norm", 64), ("ffn2_norm", 1024))
WEIGHT_ORDER = ("ffn1_norm", "ffn1_w_in", "ffn1_w_out", "mix_norm", "w_in", "b_gate", "rwkv_mu", "rwkv_w0",
                "rwkv_w2", "rwkv_a0", "rwkv_a2", "rwkv_g2", "rwkv_k_k", "rwkv_k_a", "rwkv_r_k", "rwkv_ln_w",
                "rwkv_ln_b", "attn_q_norm", "attn_k_norm", "w_proj_rwkv", "w_proj_attn", "w_out", "ffn2_norm",
                "ffn2_w_in", "ffn2_w_out")


LORA_PARTS = ("rwkv_w2", "rwkv_a2", "rwkv_g2")
BLOCK_MAJOR = ("ffn1_w_in", "ffn2_w_in")
SMALL_USED = D_MODEL + sum(n for _, n in SMALL)
SMALL_W = -(-SMALL_USED // 128) * 128


def _travel():
    out = {}
    for name, shape, axis in BIG:
        if name == LORA_PARTS[0]:
            out["lora"] = ((LORA, D_MODEL), 1)
        elif name not in LORA_PARTS:
            out[name] = (shape, axis)
    return out


def local_blocks(vals):
    out = {n: vals[n] for n in _travel() if n != "lora"}
    out["lora"] = jnp.concatenate([vals[n] for n in LORA_PARTS], axis=0)
    return out


def split_lora(t):
    return {"rwkv_w2": t[:LORA_W], "rwkv_a2": t[LORA_W:LORA_W + LORA_A], "rwkv_g2": t[LORA_W + LORA_A:]}


def blocks_to_full(name, blocks):
    shape, axis = _travel()[name]
    if name in BLOCK_MAJOR:
        return blocks
    if axis == 0:
        return blocks.reshape(shape)
    return blocks.transpose(1, 0, 2).reshape(shape)


def full_to_blocks(name, full):
    shape, axis = _travel()[name]
    if name in BLOCK_MAJOR:
        return full
    if axis == 0:
        return full.reshape(N_SHARDS, shape[0] // N_SHARDS, shape[1])
    return full.reshape(shape[0], N_SHARDS, shape[1] // N_SHARDS).transpose(1, 0, 2)


def pack_small(vals, head):
    parts = [head] + [vals[name].reshape(1, n) for name, n in SMALL]
    parts.append(jnp.zeros((1, SMALL_W - SMALL_USED), F32))
    return jnp.concatenate(parts, axis=1)


def unpack_small(vec, shapes):
    out, off = {}, D_MODEL
    for name, n in SMALL:
        out[name] = vec[:, off:off + n].reshape(shapes[name])
        off += n
    return out


def _place():
    return lax.axis_index("x"), lax.axis_index("y"), lax.axis_index("c")


def _other_chips(x, y):
    return [(1 - x, y), (x, 1 - y), (1 - x, 1 - y)]


def _remote(src, dst, send_sem, recv_sem, device):
    return pltpu.make_async_remote_copy(src_ref=src, dst_ref=dst, send_sem=send_sem, recv_sem=recv_sem,
                                        device_id=device, device_id_type=MESH)


def _half(ref, who):
    hr = ref.shape[-2] // 2
    rows = pl.ds(pl.multiple_of(who * hr, 8), hr)
    return ref.at[rows] if len(ref.shape) == 2 else ref.at[:, rows]


HBM_REF = pl.BlockSpec(memory_space=pl.ANY)
COMM_PARAMS = dict(compiler_params=pltpu.CompilerParams(has_side_effects=True))


def gather_weights(blocks):
    n = len(blocks)

    def body(*refs):
        ins, outs = refs[:n], refs[n:2 * n]
        ici_send, ici_recv, d2d_send, d2d_recv = refs[2 * n:]
        x, y, c = _place()
        me, sibling, chips = 2 * x + y, (x, y, 1 - c), _other_chips(x, y)
        first = [_remote(_half(ins[t], c), _half(outs[t].at[me], c), ici_send.at[k, t], ici_recv.at[k, t],
                         (px, py, c)) for k, (px, py) in enumerate(chips) for t in range(n)]
        for cp in first:
            cp.start()
        passed = []
        for k, (px, py) in enumerate(chips):
            for t in range(n):
                landed = _half(outs[t].at[2 * px + py], c)
                _remote(landed, landed, ici_send.at[k, t], ici_recv.at[k, t], (px, py, c)).wait_recv()
                cp = _remote(landed, landed, d2d_send.at[k, t], d2d_recv.at[k, t], sibling)
                cp.start()
                passed.append(cp)
        for k, (px, py) in enumerate(chips):
            for t in range(n):
                other = _half(outs[t].at[2 * px + py], 1 - c)
                _remote(other, other, d2d_send.at[k, t], d2d_recv.at[k, t], sibling).wait_recv()
        for cp in first + passed:
            cp.wait_send()

    res = pl.pallas_call(
        body, name="gather_weights", in_specs=[HBM_REF] * n, out_specs=[HBM_REF] * n,
        out_shape=[jax.ShapeDtypeStruct((N_SHARDS,) + b.shape, b.dtype) for b in blocks],
        scratch_shapes=[pltpu.SemaphoreType.DMA((3, n))] * 4, **COMM_PARAMS)(*blocks)
    me = 2 * lax.axis_index("x") + lax.axis_index("y")
    return [lax.dynamic_update_slice(g, b[None], (me, 0, 0)) for g, b in zip(res, blocks)]


def swap_halves(grads):
    n = len(grads)

    def body(*refs):
        ins, got = refs[:n], refs[n:2 * n]
        send_sems, recv_sems = refs[2 * n:]
        x, y, c = _place()
        give = [_remote(_half(ins[t], 1 - c), got[t], send_sems.at[t], recv_sems.at[t], (x, y, 1 - c))
                for t in range(n)]
        for cp in give:
            cp.start()
        for cp in give:
            cp.wait_recv()
        for cp in give:
            cp.wait_send()

    return pl.pallas_call(
        body, name="swap_halves", in_specs=[HBM_REF] * n, out_specs=[HBM_REF] * n,
        out_shape=[jax.ShapeDtypeStruct((g.shape[0], g.shape[1] // 2, g.shape[2]), g.dtype) for g in grads],
        scratch_shapes=[pltpu.SemaphoreType.DMA((n,))] * 2, **COMM_PARAMS)(*grads)


def scatter_partials(partials):
    n = len(partials)

    def body(*refs):
        parts, landed = refs[:n], refs[n:2 * n]
        send_sems, recv_sems = refs[2 * n:]
        x, y, c = _place()
        sends = [_remote(parts[t].at[2 * px + py], landed[t].at[k], send_sems.at[k, t], recv_sems.at[k, t],
                         (px, py, c)) for k, (px, py) in enumerate(_other_chips(x, y)) for t in range(n)]
        for cp in sends:
            cp.start()
        for cp in sends:
            cp.wait_recv()
        for cp in sends:
            cp.wait_send()

    return pl.pallas_call(
        body, name="scatter_partials", in_specs=[HBM_REF] * n, out_specs=[HBM_REF] * n,
        out_shape=[jax.ShapeDtypeStruct((3,) + p.shape[1:], p.dtype) for p in partials],
        scratch_shapes=[pltpu.SemaphoreType.DMA((3, n))] * 2, **COMM_PARAMS)(*partials)


def join_halves(blocks):
    n = len(blocks)

    def body(*refs):
        outs = refs[n:2 * n]
        send_sems, recv_sems = refs[2 * n:]
        x, y, c = _place()
        give = [_remote(_half(outs[t], c), _half(outs[t], c), send_sems.at[t], recv_sems.at[t], (x, y, 1 - c))
                for t in range(n)]
        for cp in give:
            cp.start()
        for t in range(n):
            arriving = _half(outs[t], 1 - c)
            _remote(arriving, arriving, send_sems.at[t], recv_sems.at[t], (x, y, 1 - c)).wait_recv()
        for cp in give:
            cp.wait_send()

    return pl.pallas_call(
        body, name="join_halves", in_specs=[HBM_REF] * n, out_specs=[HBM_REF] * n,
        out_shape=[jax.ShapeDtypeStruct(b.shape, b.dtype) for b in blocks],
        input_output_aliases={t: t for t in range(n)},
        scratch_shapes=[pltpu.SemaphoreType.DMA((n,))] * 2, **COMM_PARAMS)(*blocks)


def reduce_block_grads(grads):
    names = list(grads)
    got = swap_halves([grads[n] for n in names])
    partials = []
    for name, theirs in zip(names, got):
        n_slot, hr, width = theirs.shape
        tb = _row_block(hr, width, 6)
        per_half = hr // tb
        mine = lambda i, s, per_half=per_half: (i // per_half) * 2 * per_half + s[0] * per_half + i % per_half
        p = placed_map(
            jnp.add,
            [(grads[name].reshape(2 * n_slot * hr, width), mine), (theirs.reshape(n_slot * hr, width), lambda i, s: i)],
            (n_slot * hr, width, BF16, lambda i, s: i), n_blocks=n_slot * per_half, tb=tb, name="chip_sum_" + name)
        partials.append(p.reshape(theirs.shape))
    landed = scatter_partials(partials)
    blocks = []
    for name, theirs, arrived in zip(names, got, landed):
        n_slot, hr, width = theirs.shape
        tb = _row_block(hr, width, 6)
        per_half = hr // tb
        views = [(grads[name].reshape(2 * n_slot * hr, width),
                  lambda i, s, per_half=per_half: s[1] * 2 * per_half + s[0] * per_half + i),
                 (theirs.reshape(n_slot * hr, width), lambda i, s, per_half=per_half: s[1] * per_half + i)]
        views += [(arrived.reshape(3 * hr, width), functools.partial(lambda k, per_half, i, s: k * per_half + i,
                                                                     k, per_half)) for k in range(3)]
        f = lambda a, b, l0, l1, l2: (((a + b) + l0.astype(F32)) + l1.astype(F32)) + l2.astype(F32)
        blocks.append(placed_map(
            f, views,(2 * hr, width, F32, lambda i, s, per_half=per_half: s[0] * per_half + i),
            n_blocks=per_half, tb=tb, name="owner_sum_" + name))
    return dict(zip(names, join_halves(blocks)))


def adamw_block(name, w, g, m, v):
    rows, width = w.shape
    return rowmap(_adamw, [w, g, m, v], [], [(width, F32)] * 3, tb=_row_block(rows, width, 7),
                  name="adamw_" + name)


def reduce_small(vec, w, m, v):
    n_dev = 8

    def body(vec_ref, w_ref, m_ref, v_ref, loss_ref, g_ref, d_ref, m2_ref, v2_ref, slots, send_sems, recv_sems):
        x, y, c = _place()
        me = 4 * x + 2 * y + c
        slots[me] = vec_ref[...]
        flips = [(fx, fy, fc) for fx in (0, 1) for fy in (0, 1) for fc in (0, 1)][1:]
        peers = [(1 - x if fx else x, 1 - y if fy else y, 1 - c if fc else c) for fx, fy, fc in flips]
        sends = [pltpu.make_async_remote_copy(
            src_ref=vec_ref, dst_ref=slots.at[me], send_sem=send_sems.at[j], recv_sem=recv_sems.at[j],
            device_id=peer, device_id_type=MESH) for j, peer in enumerate(peers)]
        for cp in sends:
            cp.start()
        for j, (px, py, pc) in enumerate(peers):
            pltpu.make_async_remote_copy(
                src_ref=vec_ref, dst_ref=slots.at[4 * px + 2 * py + pc], send_sem=send_sems.at[j],
                recv_sem=recv_sems.at[j], device_id=(px, py, pc), device_id_type=MESH).wait_recv()
        for cp in sends:
            cp.wait_send()
        g = slots[0]
        for d in range(1, n_dev):
            g = g + slots[d]
        loss_ref[...] = jnp.sum(g[:, :D_MODEL], axis=1, keepdims=True)
        delta, m2, v2 = _adamw(w_ref[...], g, m_ref[...], v_ref[...])
        g_ref[...], d_ref[...], m2_ref[...], v2_ref[...] = g, delta, m2, v2

    vm = pl.BlockSpec(memory_space=pltpu.VMEM)
    vec_t = jax.ShapeDtypeStruct(vec.shape, F32)
    return pl.pallas_call(
        body, name="reduce_small", in_specs=[vm] * 4, out_specs=[vm] * 5,
        out_shape=[jax.ShapeDtypeStruct((1, 1), F32)] + [vec_t] * 4,
        scratch_shapes=[pltpu.VMEM((n_dev,) + vec.shape, F32), pltpu.SemaphoreType.DMA((n_dev - 1,)),
                        pltpu.SemaphoreType.DMA((n_dev - 1,))],
        compiler_params=pltpu.CompilerParams(has_side_effects=True),
    )(vec, w, m, v)


def kernel(x, ffn1_norm, ffn1_w_in, ffn1_w_out, mix_norm, w_in, b_gate, rwkv_mu, rwkv_w0, rwkv_w2, rwkv_a0, rwkv_a2, rwkv_g2, rwkv_k_k, rwkv_k_a, rwkv_r_k, rwkv_ln_w, rwkv_ln_b, attn_q_norm, attn_k_norm, w_proj_rwkv, w_proj_attn, w_out, ffn2_norm, ffn2_w_in, ffn2_w_out, loss_target, m_ffn1_norm, m_ffn1_w_in, m_ffn1_w_out, m_mix_norm, m_w_in, m_b_gate, m_rwkv_mu, m_rwkv_w0, m_rwkv_w2, m_rwkv_a0, m_rwkv_a2, m_rwkv_g2, m_rwkv_k_k, m_rwkv_k_a, m_rwkv_r_k, m_rwkv_ln_w, m_rwkv_ln_b, m_attn_q_norm, m_attn_k_norm, m_w_proj_rwkv, m_w_proj_attn, m_w_out, m_ffn2_norm, m_ffn2_w_in, m_ffn2_w_out, v_ffn1_norm, v_ffn1_w_in, v_ffn1_w_out, v_mix_norm, v_w_in, v_b_gate, v_rwkv_mu, v_rwkv_w0, v_rwkv_w2, v_rwkv_a0, v_rwkv_a2, v_rwkv_g2, v_rwkv_k_k, v_rwkv_k_a, v_rwkv_r_k, v_rwkv_ln_w, v_rwkv_ln_b, v_attn_q_norm, v_attn_k_norm, v_w_proj_rwkv, v_w_proj_attn, v_w_out, v_ffn2_norm, v_ffn2_w_in, v_ffn2_w_out):
    given = dict(locals())
    weights = {n: given[n] for n in WEIGHT_ORDER}
    mom_m = {n: given["m_" + n] for n in WEIGHT_ORDER}
    mom_v = {n: given["v_" + n] for n in WEIGHT_ORDER}
    big = [name for name, _, _ in BIG]
    shapes = {n: weights[n].shape for n in WEIGHT_ORDER}
    blocks_of = lambda d: local_blocks({n: d[n][0] for n in big})
    w_blk, m_blk, v_blk = blocks_of(weights), blocks_of(mom_m), blocks_of(mom_v)
    names = list(w_blk)

    gathered = gather_weights([w_blk[n].astype(BF16) for n in names])
    W = {n: blocks_to_full(n, g) for n, g in zip(names, gathered)}
    W.update(split_lora(W.pop("lora")))
    P = {n: weights[n].reshape(1, -1) for n, _ in SMALL}

    loss_cols, dx, gW, gP = layer_step(x[0], loss_target[0], W, P)

    gW["lora"] = jnp.concatenate([gW.pop(n) for n in LORA_PARTS], axis=0)
    g_blk = reduce_block_grads({n: full_to_blocks(n, gW[n]) for n in names})
    out_g, out_d, out_m, out_v = {}, {}, {}, {}
    for n in names:
        res = (g_blk[n], *adamw_block(n, w_blk[n], g_blk[n], m_blk[n], v_blk[n]))
        for dst, t in zip((out_g, out_d, out_m, out_v), res):
            for part, val in (split_lora(t) if n == "lora" else {n: t}).items():
                dst[part] = val.reshape(shapes[part])

    zero_head = jnp.zeros((1, D_MODEL), F32)
    vec = pack_small(gP, loss_cols)
    loss, g_s, d_s, m_s, v_s = reduce_small(
        vec, pack_small({n: weights[n] for n, _ in SMALL}, zero_head),
        pack_small({n: mom_m[n] for n, _ in SMALL}, zero_head),
        pack_small({n: mom_v[n] for n, _ in SMALL}, zero_head))
    for dst, src in ((out_g, g_s), (out_d, d_s), (out_m, m_s), (out_v, v_s)):
        dst.update(unpack_small(src, shapes))

    return (loss[0, 0], dx[None], *[out_g[n] for n in WEIGHT_ORDER], *[out_d[n] for n in WEIGHT_ORDER],
            *[out_m[n] for n in WEIGHT_ORDER], *[out_v[n] for n in WEIGHT_ORDER])
```

```python
import functools

import jax
import jax.numpy as jnp
from jax import lax
from jax.experimental import pallas as pl
from jax.experimental.pallas import tpu as pltpu

F32 = jnp.float32
BF16 = jnp.bfloat16
MESH = pl.DeviceIdType.MESH

D_MODEL = 1024
HEAD_DIM = 64
RWKV_HEADS = 16
LORA_W, LORA_A, LORA_G = 64, 64, 160
LORA = LORA_W + LORA_A + LORA_G
RKV = 3 * D_MODEL
ATTN_PAIRS = ((128, 1), (512, 4), (2048, 16))
ATTN_BLK = 128
ATTN_HPG = 4
ATTN_WIDTH = 768
GROUP_W = ATTN_HPG * HEAD_DIM
D_FF = 2816
GN_EPS = 64e-5
RMS_EPS = 1e-6
NEG_INF = -1e30
WKV_CHUNK = 64
WKV_HEADS_PER_STEP = 16

ADAM_LR, ADAM_B1, ADAM_B2, ADAM_EPS, ADAM_WD, ADAM_STEP = 0.001, 0.9, 0.999, 1e-08, 0.01, 10

V7X_VMEM_BYTES = 64 << 20
VMEM_TEMP_ALLOWANCE = 20 << 20


def _cparams(sem, block_bytes):
    limit = min(2 * block_bytes + VMEM_TEMP_ALLOWANCE, V7X_VMEM_BYTES - (6 << 20))
    return pltpu.CompilerParams(dimension_semantics=sem, vmem_limit_bytes=int(limit))


def _nbytes(shape, dtype):
    n = 1
    for s in shape:
        n *= s
    return n * jnp.dtype(dtype).itemsize


def _split_bf16(a):
    hi = a.astype(BF16)
    return hi, (a - hi.astype(F32)).astype(BF16)


def _make_dots():
    def raw(a, b, ca, cb):
        return lax.dot_general(a.astype(BF16), b.astype(BF16), (((ca,), (cb,)), ((), ())),
                               preferred_element_type=F32)

    @jax.custom_vjp
    def nn(a, b):
        return raw(a, b, 1, 0)

    @jax.custom_vjp
    def nt(a, b):
        return raw(a, b, 1, 1)

    @jax.custom_vjp
    def tn(a, b):
        return raw(a, b, 0, 0)

    nn.defvjp(lambda a, b: (raw(a, b, 1, 0), (a, b)),
              lambda res, g: (raw(g, res[1], 1, 1), raw(res[0], g, 0, 0)))
    nt.defvjp(lambda a, b: (raw(a, b, 1, 1), (a, b)),
              lambda res, g: (raw(g, res[1], 1, 0), raw(g, res[0], 0, 0)))
    tn.defvjp(lambda a, b: (raw(a, b, 0, 0), (a, b)),
              lambda res, g: (raw(res[1], g, 1, 1), raw(res[0], g, 1, 0)))
    return nn, nt, tn


def _exact_rhs_dot(x, ones, cx, co):
    hi, lo = _split_bf16(x)
    dims = (((cx,), (co,)), ((), ()))
    return (lax.dot_general(hi, ones, dims, preferred_element_type=F32)
            + lax.dot_general(lo, ones, dims, preferred_element_type=F32))


@jax.custom_vjp
def SEG(x, ones):
    return _exact_rhs_dot(x, ones, 1, 0)


SEG.defvjp(lambda x, ones: (_exact_rhs_dot(x, ones, 1, 0), ones),
           lambda ones, g: (_exact_rhs_dot(g, ones, 1, 1), jnp.zeros_like(ones)))

NN, NT, TN = _make_dots()


MM_TILE_M, MM_TILE_N, MM_TILE_K = 1408, 1408, 1536


def _pick(n, cap):
    best = None
    for t in range(128, min(n, cap) + 1, 128):
        if n % t == 0:
            best = t
    return best or n


def matmul(a, b, mode, name, *, add=None, scale=1.0, out_dtype=F32):
    if mode == "nn":
        (M, K), (K2, N) = a.shape, b.shape
    elif mode == "nt":
        (M, K), (N, K2) = a.shape, b.shape
    else:
        (K, M), (K2, N) = a.shape, b.shape
    assert K == K2, (name, a.shape, b.shape)
    tm, tn, tk = _pick(M, MM_TILE_M), _pick(N, MM_TILE_N), _pick(K, MM_TILE_K)
    nk = K // tk
    ca, cb = {"nn": (1, 0), "nt": (1, 1), "tn": (0, 0)}[mode]

    def body(*refs):
        if add is None:
            a_ref, b_ref, o_ref, acc_ref = refs
        else:
            a_ref, b_ref, add_ref, o_ref, acc_ref = refs
        k = pl.program_id(2)

        @pl.when(k == 0)
        def _():
            acc_ref[...] = jnp.zeros_like(acc_ref)

        acc_ref[...] += lax.dot_general(a_ref[...].astype(BF16), b_ref[...].astype(BF16),
                                        (((ca,), (cb,)), ((), ())), preferred_element_type=F32)

        @pl.when(k == nk - 1)
        def _():
            r = acc_ref[...] * scale
            if add is not None:
                r = add_ref[...] + r
            o_ref[...] = r.astype(o_ref.dtype)

    a_spec = (pl.BlockSpec((tk, tm), lambda i, j, k: (k, i)) if mode == "tn"
              else pl.BlockSpec((tm, tk), lambda i, j, k: (i, k)))
    b_spec = (pl.BlockSpec((tn, tk), lambda i, j, k: (j, k)) if mode == "nt"
              else pl.BlockSpec((tk, tn), lambda i, j, k: (k, j)))
    in_specs, args = [a_spec, b_spec], [a, b]
    blk = tm * tk * a.dtype.itemsize + tk * tn * b.dtype.itemsize + tm * tn * 8
    if add is not None:
        in_specs.append(pl.BlockSpec((tm, tn), lambda i, j, k: (i, j)))
        args.append(add)
        blk += tm * tn * 4
    return pl.pallas_call(
        body, name=name, grid=(M // tm, N // tn, nk),
        in_specs=in_specs, out_specs=pl.BlockSpec((tm, tn), lambda i, j, k: (i, j)),
        out_shape=jax.ShapeDtypeStruct((M, N), out_dtype),
        scratch_shapes=[pltpu.VMEM((tm, tn), F32)],
        compiler_params=_cparams(("parallel", "parallel", "arbitrary"), blk),
    )(*args)


def matmul_cs(a, w, mode, name, *, scale=1.0, out_dtype=F32):
    n_blk = N_SHARDS
    if mode == "tn":
        (K, R), Cs = a.shape, w.shape[1] // n_blk
        tm, tk = _pick(R, MM_TILE_M), _pick(K, 1024)
        grid = (R // tm, n_blk, K // tk)
        a_spec = pl.BlockSpec((tk, tm), lambda i, j, k: (k, i))
        w_spec = pl.BlockSpec((tk, Cs), lambda i, j, k: (k, j))
        o_spec = pl.BlockSpec((None, tm, Cs), lambda i, j, k: (j, i, 0))
        out_shape, acc_shape, dims = (n_blk, R, Cs), (tm, Cs), (0, 0)
        blk = tk * tm * a.dtype.itemsize + tk * Cs * w.dtype.itemsize + tm * Cs * 8
    elif mode == "nn":
        (M, R), Cs = a.shape, w.shape[2]
        tm, tk = _pick(M, MM_TILE_M), _pick(R, 1024)
        grid = (M // tm, n_blk, R // tk)
        a_spec = pl.BlockSpec((tm, tk), lambda i, j, k: (i, k))
        w_spec = pl.BlockSpec((None, tk, Cs), lambda i, j, k: (j, k, 0))
        o_spec = pl.BlockSpec((tm, Cs), lambda i, j, k: (i, j))
        out_shape, acc_shape, dims = (M, n_blk * Cs), (tm, Cs), (1, 0)
        blk = tm * tk * a.dtype.itemsize + tk * Cs * w.dtype.itemsize + tm * Cs * 8
    else:
        M, (_, R, Cs) = a.shape[0], w.shape
        tm, tn = _pick(M, MM_TILE_M), _pick(R, MM_TILE_N)
        grid = (M // tm, R // tn, n_blk)
        a_spec = pl.BlockSpec((tm, Cs), lambda i, j, k: (i, k))
        w_spec = pl.BlockSpec((None, tn, Cs), lambda i, j, k: (k, j, 0))
        o_spec = pl.BlockSpec((tm, tn), lambda i, j, k: (i, j))
        out_shape, acc_shape, dims = (M, R), (tm, tn), (1, 1)
        blk = tm * Cs * a.dtype.itemsize + tn * Cs * w.dtype.itemsize + tm * tn * 8
    nk = grid[2]

    def body(a_ref, w_ref, o_ref, acc_ref):
        k = pl.program_id(2)

        @pl.when(k == 0)
        def _():
            acc_ref[...] = jnp.zeros_like(acc_ref)

        acc_ref[...] += lax.dot_general(a_ref[...].astype(BF16), w_ref[...].astype(BF16),
                                        (((dims[0],), (dims[1],)), ((), ())), preferred_element_type=F32)

        @pl.when(k == nk - 1)
        def _():
            o_ref[...] = (acc_ref[...] * scale).astype(o_ref.dtype)

    return pl.pallas_call(
        body, name=name, grid=grid, in_specs=[a_spec, w_spec], out_specs=o_spec,
        out_shape=jax.ShapeDtypeStruct(out_shape, out_dtype), scratch_shapes=[pltpu.VMEM(acc_shape, F32)],
        compiler_params=_cparams(("parallel", "parallel", "arbitrary"), blk),
    )(a, w)


def _row_block(n, width, n_arrays):
    cap = (V7X_VMEM_BYTES // 4) // (2 * 4 * width * n_arrays)
    best = None
    for t in range(16, min(n, cap) + 1, 16):
        if n % t == 0:
            best = t
    return best or n


def placed_map(f, ins, out, *, n_blocks, tb, name):
    def body(*refs):
        refs[-1][...] = f(*[r[...] for r in refs[:-1]]).astype(refs[-1].dtype)

    def spec(fn):
        def index(i):
            x, y, c = _place()
            return fn(i, (c, 2 * x + y)), 0
        return pl.BlockSpec((tb, width), index)

    o_rows, width, o_dtype, o_fn = out
    blk = (sum(a.dtype.itemsize for a, _ in ins) + jnp.dtype(o_dtype).itemsize) * tb * width
    return pl.pallas_call(
        body, name=name, grid=(n_blocks,), in_specs=[spec(fn) for _, fn in ins], out_specs=spec(o_fn),
        out_shape=jax.ShapeDtypeStruct((o_rows, width), o_dtype),
        compiler_params=_cparams(("parallel",), blk),
    )(*[a for a, _ in ins])


def rowmap(f, rows, params, outs, accs=(), *, tb, name):
    rows = [r if isinstance(r, tuple) else (r, r.shape[1], 0) for r in rows]
    S = rows[0][0].shape[0]
    assert S % tb == 0, (name, S, tb)
    n_in, n_out = len(rows) + len(params), len(outs)

    def body(*refs):
        res = f(*[r[...] for r in refs[:n_in]])
        res = res if isinstance(res, (tuple, list)) else (res,)
        o_refs, a_refs = refs[n_in:n_in + n_out], refs[n_in + n_out:]
        for ref, val in zip(o_refs, res[:n_out]):
            ref[...] = val.astype(ref.dtype)
        if a_refs:
            @pl.when(pl.program_id(0) == 0)
            def _():
                for ref in a_refs:
                    ref[...] = jnp.zeros_like(ref)

            for ref, val in zip(a_refs, res[n_out:]):
                ref[...] += val.astype(F32)

    in_specs = [pl.BlockSpec((tb, w), functools.partial(lambda cb, i: (i, cb), cb)) for _, w, cb in rows]
    in_specs += [pl.BlockSpec(p.shape, lambda i: (0, 0)) for p in params]
    out_specs = [pl.BlockSpec((tb, w), lambda i: (i, 0)) for w, _ in outs]
    out_specs += [pl.BlockSpec(tuple(s), lambda i: (0, 0)) for s in accs]
    out_shape = [jax.ShapeDtypeStruct((S, w), dt) for w, dt in outs]
    out_shape += [jax.ShapeDtypeStruct(tuple(s), F32) for s in accs]
    blk = sum(tb * w * a.dtype.itemsize for a, w, _ in rows) + sum(_nbytes(p.shape, p.dtype) for p in params)
    blk += sum(_nbytes((tb, w), dt) for w, dt in outs) + sum(_nbytes(s, F32) for s in accs)
    res = pl.pallas_call(
        body, name=name, grid=(S // tb,), in_specs=in_specs, out_specs=out_specs, out_shape=out_shape,
        compiler_params=_cparams(("arbitrary",) if accs else ("parallel",), blk),
    )(*[r[0] for r in rows], *[pltpu.with_memory_space_constraint(p, pltpu.HBM) for p in params])
    return res


def _rms(x, g):
    return x * lax.rsqrt(jnp.mean(x * x, axis=-1, keepdims=True) + RMS_EPS) * g


def _softplus(z):
    return jnp.maximum(z, 0.0) + jnp.log(1.0 + jnp.exp(-jnp.abs(z)))


def _swiglu_act(gu):
    gate, up = gu[:, :D_FF], gu[:, D_FF:]
    return gate * jax.nn.sigmoid(gate) * up


def _rwkv_pre(xrk, xlo, w0, w2p, a0, a2p, g2p, k_k, k_a, seg, seg_t):
    k = xrk[:, D_MODEL:2 * D_MODEL]
    w = -_softplus(-(w0 + NN(jnp.tanh(xlo), w2p))) - 0.5
    log_decay = -jnp.exp(w)
    a = jax.nn.sigmoid(a0 + NN(xlo, a2p))
    g = NN(jax.nn.sigmoid(xlo), g2p)
    kk = k * k_k
    norm = jnp.maximum(jnp.sqrt(SEG(kk * kk, seg)), 1e-12)
    kk = kk * SEG(1.0 / norm, seg_t)
    k_mod = k * (1.0 + (a - 1.0) * k_a)
    return log_decay, k_mod, -kk, kk * a, g


def _rwkv_post(wkv, r, k_mod, v, g, r_k, ln_w, ln_b, seg, seg_t):
    inv_n = 1.0 / HEAD_DIM
    mean = SEG(wkv, seg) * inv_n
    cen = wkv - SEG(mean, seg_t)
    var = SEG(cen * cen, seg) * inv_n
    y = cen * SEG(lax.rsqrt(var + GN_EPS), seg_t) * ln_w + ln_b
    bonus = SEG(SEG(r * k_mod * r_k, seg), seg_t) * v
    return (y + bonus) * g


def _qk_norm(q, k, q_gain, k_gain, seg, seg_t, tile_t):
    def norm(x, gain):
        mean_sq = SEG(x * x, seg) * (1.0 / HEAD_DIM)
        return x * SEG(lax.rsqrt(mean_sq + RMS_EPS), seg_t) * SEG(gain, tile_t)

    return norm(q, q_gain) * (HEAD_DIM ** -0.5), norm(k, k_gain)


def _gate_merge(pgate, pa, pb, b_gate):
    sg = jax.nn.sigmoid(pgate + b_gate)
    return sg[:, :D_MODEL] * pa + sg[:, D_MODEL:] * pb


def _group_combine(o, lse):
    ls = [lse[:, GROUP_W * i:GROUP_W * (i + 1)] for i in range(3)]
    m = jnp.maximum(jnp.maximum(ls[0], ls[1]), ls[2])
    es = [jnp.exp(l - m) for l in ls]
    den = es[0] + es[1] + es[2]
    return jnp.concatenate([o[:, GROUP_W * i:GROUP_W * (i + 1)] * (es[i] / den) for i in range(3)], axis=1)


def _each(f, *xs):
    return tuple(f(*args) for args in zip(*xs))


def _attn_block(q, kc, kp, vc, vp, first):
    qi = lax.broadcasted_iota(jnp.int32, (ATTN_BLK, ATTN_BLK), 0)
    kj = lax.broadcasted_iota(jnp.int32, (ATTN_BLK, ATTN_BLK), 1)
    own, before = kj <= qi, (kj >= qi) & (first < 0.5)
    s_c = _each(lambda a, b: jnp.where(own, NT(a, b), NEG_INF), q, kc)
    s_p = _each(lambda a, b: jnp.where(before, NT(a, b), NEG_INF), q, kp)
    row_max = lambda s: jnp.max(s, axis=-1, keepdims=True)
    row_sum = lambda s: jnp.sum(s, axis=-1, keepdims=True)
    m = _each(lambda c_, p_: jnp.maximum(row_max(c_), row_max(p_)), s_c, s_p)
    e_c, e_p = _each(lambda s, m_: jnp.exp(s - m_), s_c, m), _each(lambda s, m_: jnp.exp(s - m_), s_p, m)
    den = _each(lambda c_, p_: row_sum(c_) + row_sum(p_), e_c, e_p)
    inv = _each(lambda d_: 1.0 / d_, den)
    o = _each(lambda ec, ep, i_, vc_, vp_: (NN(ec, vc_) + NN(ep, vp_)) * i_, e_c, e_p, inv, vc, vp)
    lse = _each(lambda m_, d_: jnp.broadcast_to(m_ + jnp.log(d_), (ATTN_BLK, HEAD_DIM)), m, den)
    return o, lse


TRI_SEED = 8


def _tri_inverse(n):
    c = n[0].shape[0]
    row = lax.broadcasted_iota(jnp.int32, (c, c), 0)
    col = lax.broadcasted_iota(jnp.int32, (c, c), 1)
    same_block = lambda size: (row >> (size.bit_length() - 1)) == (col >> (size.bit_length() - 1))
    seed = same_block(TRI_SEED)
    p = _each(lambda m: jnp.where(seed, m, 0.0), n)
    t, span = _each(lambda m: (row == col).astype(F32) + m, p), 2
    while span < TRI_SEED:
        p = _each(NN, p, p)
        t = _each(lambda t_, p_: t_ + NN(t_, p_), t, p)
        span *= 2
    size = TRI_SEED
    while size < c:
        joins = same_block(2 * size) & jnp.logical_not(same_block(size))
        t = _each(lambda t_, m: t_ + NN(NN(t_, jnp.where(joins, m, 0.0)), t_), t, n)
        size *= 2
    return t


@jax.custom_vjp
def _tri_solve(n, rhs, t):
    return _each(NN, t, rhs)


def _tri_solve_fwd(n, rhs, t):
    x = _each(NN, t, rhs)
    return x, (t, x)


def _tri_solve_bwd(res, dx):
    t, x = res
    drhs = _each(TN, t, dx)
    return _each(NT, drhs, x), drhs, _each(jnp.zeros_like, t)


_tri_solve.defvjp(_tri_solve_fwd, _tri_solve_bwd)


def _lower_ones(c):
    row = lax.broadcasted_iota(jnp.int32, (c, c), 0)
    col = lax.broadcasted_iota(jnp.int32, (c, c), 1)
    return (row >= col).astype(BF16)


def _ones_dot(ones, x, contract):
    hi, lo = _split_bf16(x)
    dims = (((contract,), (0,)), ((), ()))
    return (lax.dot_general(ones, hi, dims, preferred_element_type=F32)
            + lax.dot_general(ones, lo, dims, preferred_element_type=F32))


@jax.custom_vjp
def _cumsum_rows(x):
    return _ones_dot(_lower_ones(x.shape[0]), x, 1)


_cumsum_rows.defvjp(lambda x: (_ones_dot(_lower_ones(x.shape[0]), x, 1), None),
                    lambda _, g: (_ones_dot(_lower_ones(g.shape[0]), g, 0),))


def _wkv_chunk(s0, r, lw, k, v, a, b, t_inv=None):
    c = r[0].shape[0]
    row = lax.broadcasted_iota(jnp.int32, (c, c), 0)
    col = lax.broadcasted_iota(jnp.int32, (c, c), 1)
    strict, incl = row > col, row >= col
    cat = lambda p, q: jnp.concatenate([p, q], axis=0)
    cum = _each(_cumsum_rows, lw)
    e_neg = _each(lambda c_: jnp.exp(-c_), cum)
    ar = _each(lambda a_, r_, c_, l_: cat(a_ * jnp.exp(c_ - l_), r_ * jnp.exp(c_)), a, r, cum, lw)
    b_t, k_t = _each(jnp.multiply, b, e_neg), _each(jnp.multiply, k, e_neg)
    p_b, p_k, p_s = _each(NT, ar, b_t), _each(NT, ar, k_t), _each(NT, ar, s0)
    n_ab = _each(lambda p: jnp.where(strict, p[:c], 0.0), p_b)
    m_rb = _each(lambda p: jnp.where(incl, p[c:], 0.0), p_b)
    n_ak = _each(lambda p: jnp.where(strict, p[:c], 0.0), p_k)
    m_rk = _each(lambda p: jnp.where(incl, p[c:], 0.0), p_k)
    if t_inv is None:
        t_inv = _tri_inverse(n_ab)
    u = _tri_solve(n_ab, _each(lambda p, n_, v_: p[:c] + NN(n_, v_), p_s, n_ak, v), t_inv)
    y = _each(lambda p, mb, u_, mk, v_: p[c:] + NN(mb, u_) + NN(mk, v_), p_s, m_rb, u, m_rk, v)
    g_end = _each(lambda l_: jnp.exp(jnp.sum(l_, axis=0, keepdims=True)), lw)
    s1 = _each(lambda s_, g_, u_, v_, b_, k_: s_ * g_ + TN(cat(u_, v_), cat(b_, k_) * g_),
               s0, g_end, u, v, b_t, k_t)
    return y, s1, t_inv


def _adamw(w, g, m, v):
    m = ADAM_B1 * m + (1.0 - ADAM_B1) * g
    v = ADAM_B2 * v + (1.0 - ADAM_B2) * jnp.square(g)
    m_hat = m / (1.0 - ADAM_B1 ** ADAM_STEP)
    v_hat = v / (1.0 - ADAM_B2 ** ADAM_STEP)
    delta = -ADAM_LR * (m_hat / (jnp.sqrt(v_hat) + ADAM_EPS) + ADAM_WD * w)
    return delta, m, v


def token_shift_fwd(p, mu, *, tb, name):
    S, W = p.shape
    hb = tb // 8

    def body(p_ref, halo_ref, mu_ref, o_ref):
        i = pl.program_id(0)
        x = p_ref[...]
        before = halo_ref[7:8, :] * (i > 0).astype(F32)
        row = lax.broadcasted_iota(jnp.int32, (tb, W), 0)
        prev = jnp.where(row == 0, before, pltpu.roll(x, 1, 0))
        o_ref[...] = x + (prev - x) * mu_ref[...]

    blk = (2 * tb + 8) * W * 4
    return pl.pallas_call(
        body, name=name, grid=(S // tb,),
        in_specs=[pl.BlockSpec((tb, W), lambda i: (i, 0)),
                  pl.BlockSpec((8, W), lambda i: (jnp.maximum(i * hb - 1, 0), 0)),
                  pl.BlockSpec((1, W), lambda i: (0, 0))],
        out_specs=pl.BlockSpec((tb, W), lambda i: (i, 0)),
        out_shape=jax.ShapeDtypeStruct((S, W), F32),
        compiler_params=_cparams(("parallel",), blk),
    )(p, p, mu)


def token_shift_bwd(dxs, p, mu, *, tb, name):
    S, W = p.shape
    hb, nb = tb // 8, S // tb

    def body(d_ref, dnext_ref, p_ref, halo_ref, mu_ref, dp_ref, dmu_ref):
        i = pl.program_id(0)
        d, x, mu_v = d_ref[...], p_ref[...], mu_ref[...]
        row = lax.broadcasted_iota(jnp.int32, (tb, W), 0)
        before = halo_ref[7:8, :] * (i > 0).astype(F32)
        prev = jnp.where(row == 0, before, pltpu.roll(x, 1, 0))
        t = d * mu_v
        after = dnext_ref[0:1, :] * mu_v * (i < nb - 1).astype(F32)
        nxt = jnp.where(row == tb - 1, after, pltpu.roll(t, tb - 1, 0))
        dp_ref[...] = (d - t + nxt).astype(dp_ref.dtype)

        @pl.when(i == 0)
        def _():
            dmu_ref[...] = jnp.zeros_like(dmu_ref)

        dmu_ref[...] += jnp.sum(d * (prev - x), axis=0, keepdims=True)

    blk = (3 * tb + 16) * W * 4
    return pl.pallas_call(
        body, name=name, grid=(nb,),
        in_specs=[pl.BlockSpec((tb, W), lambda i: (i, 0)),
                  pl.BlockSpec((8, W), lambda i: (jnp.minimum((i + 1) * hb, S // 8 - 1), 0)),
                  pl.BlockSpec((tb, W), lambda i: (i, 0)),
                  pl.BlockSpec((8, W), lambda i: (jnp.maximum(i * hb - 1, 0), 0)),
                  pl.BlockSpec((1, W), lambda i: (0, 0))],
        out_specs=[pl.BlockSpec((tb, W), lambda i: (i, 0)), pl.BlockSpec((1, W), lambda i: (0, 0))],
        out_shape=[jax.ShapeDtypeStruct((S, W), BF16), jax.ShapeDtypeStruct((1, W), F32)],
        compiler_params=_cparams(("arbitrary",), blk),
    )(dxs, dxs, p, p, mu)


def _head_cols(h):
    return pl.ds(h * HEAD_DIM, HEAD_DIM)


def wkv_fwd(xs_rk, lw, k, a, b):
    S = lw.shape[0]
    C, nc, G, N = WKV_CHUNK, S // WKV_CHUNK, WKV_HEADS_PER_STEP, HEAD_DIM

    def body(r_ref, lw_ref, k_ref, v_ref, a_ref, b_ref, y_ref, st_ref, ti_ref, state):
        @pl.when(pl.program_id(1) == 0)
        def _():
            state[...] = jnp.zeros_like(state)

        heads = lambda ref: tuple(ref[:, _head_cols(h)] for h in range(G))
        s0 = tuple(state[h] for h in range(G))
        y, s1, t_inv = _wkv_chunk(s0, heads(r_ref), heads(lw_ref), heads(k_ref), heads(v_ref), heads(a_ref),
                                  heads(b_ref))
        for h in range(G):
            st_ref[h] = s0[h]
            ti_ref[h] = t_inv[h]
            y_ref[:, _head_cols(h)] = y[h]
            state[h] = s1[h]

    W = G * N
    seq = lambda j: pl.BlockSpec((C, W), functools.partial(lambda j, g, c: (c, j + g), j))
    per = D_MODEL // W
    per_chunk = pl.BlockSpec((None, G, N, N), lambda g, c: (c, g, 0, 0))
    return pl.pallas_call(
        body, name="wkv_fwd", grid=(RWKV_HEADS // G, nc),
        in_specs=[seq(0), seq(0), seq(0), seq(2 * per), seq(0), seq(0)],
        out_specs=[seq(0), per_chunk, per_chunk],
        out_shape=[jax.ShapeDtypeStruct((S, D_MODEL), F32)] + [jax.ShapeDtypeStruct((nc, RWKV_HEADS, N, N), F32)] * 2,
        scratch_shapes=[pltpu.VMEM((G, N, N), F32)],
        compiler_params=_cparams(("parallel", "arbitrary"), 8 * C * W * 4 + 3 * G * N * N * 4),
    )(xs_rk, lw, k, xs_rk, a, b)


def wkv_bwd(xs_rk, lw, k, a, b, states, t_invs, dy):
    S = lw.shape[0]
    C, nc, G, N = WKV_CHUNK, S // WKV_CHUNK, WKV_HEADS_PER_STEP, HEAD_DIM

    def body(r_ref, lw_ref, k_ref, v_ref, a_ref, b_ref, st_ref, ti_ref, dy_ref,
             dr_ref, dlw_ref, dk_ref, dv_ref, da_ref, db_ref, dstate):
        @pl.when(pl.program_id(1) == 0)
        def _():
            dstate[...] = jnp.zeros_like(dstate)

        heads = lambda ref: tuple(ref[:, _head_cols(h)] for h in range(G))
        t_inv = tuple(ti_ref[h] for h in range(G))
        chunk = lambda *args: _wkv_chunk(*args, t_inv)[:2]
        _, pull = jax.vjp(chunk, tuple(st_ref[h] for h in range(G)), heads(r_ref), heads(lw_ref),
                          heads(k_ref), heads(v_ref), heads(a_ref), heads(b_ref))
        ds0, *grads = pull((heads(dy_ref), tuple(dstate[h] for h in range(G))))
        for h in range(G):
            dstate[h] = ds0[h]
            for ref, grad in zip((dr_ref, dlw_ref, dk_ref, dv_ref, da_ref, db_ref), grads):
                ref[:, _head_cols(h)] = grad[h]

    W = G * N
    seq = lambda j: pl.BlockSpec((C, W), functools.partial(lambda j, g, c: (nc - 1 - c, j + g), j))
    per = D_MODEL // W
    st = pl.BlockSpec((None, G, N, N), lambda g, c: (nc - 1 - c, g, 0, 0))
    return pl.pallas_call(
        body, name="wkv_bwd", grid=(RWKV_HEADS // G, nc),
        in_specs=[seq(0), seq(0), seq(0), seq(2 * per), seq(0), seq(0), st, st, seq(0)],
        out_specs=[seq(0)] * 6, out_shape=[jax.ShapeDtypeStruct((S, D_MODEL), F32)] * 6,
        scratch_shapes=[pltpu.VMEM((G, N, N), F32)],
        compiler_params=_cparams(("parallel", "arbitrary"), 14 * C * W * 4 + 3 * G * N * N * 4),
    )(xs_rk, lw, k, xs_rk, a, b, states, t_invs, dy)


def _first_flag(i, seq_len):
    per_group = seq_len // ATTN_BLK
    g = i // per_group
    per_seq = [seq_len // d // ATTN_BLK for _, d in ATTN_PAIRS]
    n = jnp.where(g == 0, per_seq[0], jnp.where(g == 1, per_seq[1], per_seq[2]))
    return (lax.rem(i, n) == 0).astype(F32)


def attn_fwd(q, k, v, seq_len):
    R, N = q.shape
    nb = R // ATTN_BLK

    def body(q_ref, kc_ref, kp_ref, vc_ref, vp_ref, o_ref, lse_ref):
        first = _first_flag(pl.program_id(0), seq_len)
        heads = lambda ref: tuple(ref[:, _head_cols(h)] for h in range(ATTN_HPG))
        o, lse = _attn_block(heads(q_ref), heads(kc_ref), heads(kp_ref), heads(vc_ref), heads(vp_ref), first)
        for h in range(ATTN_HPG):
            o_ref[:, _head_cols(h)] = o[h]
            lse_ref[:, _head_cols(h)] = lse[h]

    cur = pl.BlockSpec((ATTN_BLK, N), lambda i: (i, 0))
    prv = pl.BlockSpec((ATTN_BLK, N), lambda i: (jnp.maximum(i - 1, 0), 0))
    return pl.pallas_call(
        body, name="attn_fwd", grid=(nb,), in_specs=[cur, cur, prv, cur, prv],
        out_specs=[cur, cur], out_shape=[jax.ShapeDtypeStruct((R, N), F32)] * 2,
        compiler_params=_cparams(("parallel",), 7 * ATTN_BLK * N * 4),
    )(q, k, k, v, v)


def attn_bwd(q, k, v, do, dlse, seq_len):
    R, N = q.shape
    nb = R // ATTN_BLK

    def body(q_ref, kc_ref, kp_ref, vc_ref, vp_ref, do_ref, dl_ref, dq_ref, dk_ref, dv_ref, carry_k, carry_v):
        step = pl.program_id(0)
        first = _first_flag(nb - 1 - step, seq_len)

        @pl.when(step == 0)
        def _():
            carry_k[...] = jnp.zeros_like(carry_k)
            carry_v[...] = jnp.zeros_like(carry_v)

        heads = lambda ref: tuple(ref[:, _head_cols(h)] for h in range(ATTN_HPG))
        _, pull = jax.vjp(functools.partial(_attn_block, first=first), heads(q_ref), heads(kc_ref), heads(kp_ref),
                          heads(vc_ref), heads(vp_ref))
        dq, dkc, dkp, dvc, dvp = pull((heads(do_ref), heads(dl_ref)))
        old_k, old_v = heads(carry_k), heads(carry_v)
        for h in range(ATTN_HPG):
            cols = _head_cols(h)
            dq_ref[:, cols] = dq[h]
            dk_ref[:, cols] = dkc[h] + old_k[h]
            dv_ref[:, cols] = dvc[h] + old_v[h]
            carry_k[:, cols] = dkp[h]
            carry_v[:, cols] = dvp[h]

    cur = pl.BlockSpec((ATTN_BLK, N), lambda i: (nb - 1 - i, 0))
    prv = pl.BlockSpec((ATTN_BLK, N), lambda i: (jnp.maximum(nb - 2 - i, 0), 0))
    return pl.pallas_call(
        body, name="attn_bwd", grid=(nb,), in_specs=[cur, cur, prv, cur, prv, cur, cur],
        out_specs=[cur, cur, cur], out_shape=[jax.ShapeDtypeStruct((R, N), F32)] * 3,
        scratch_shapes=[pltpu.VMEM((ATTN_BLK, N), F32)] * 2,
        compiler_params=_cparams(("arbitrary",), 12 * ATTN_BLK * N * 4),
    )(q, k, k, v, v, do, dlse)


def to_subsequences(t):
    S = t.shape[0]
    parts = []
    for gi, (_, d) in enumerate(ATTN_PAIRS):
        tg = t[:, GROUP_W * gi:GROUP_W * (gi + 1)].reshape(S // d, d, GROUP_W)
        parts.append(tg.transpose(1, 0, 2).reshape(S, GROUP_W))
    return jnp.concatenate(parts, axis=0)


def from_subsequences(u, S):
    parts = []
    for gi, (_, d) in enumerate(ATTN_PAIRS):
        ug = u[S * gi:S * (gi + 1)].reshape(d, S // d, GROUP_W)
        parts.append(ug.transpose(1, 0, 2).reshape(S, GROUP_W))
    return jnp.concatenate(parts, axis=1)


def _ffn_fwd(x, norm, w_in, w_out, tag):
    h = rowmap(_rms, [x], [norm], [(D_MODEL, BF16)], tb=512, name=tag + "_norm")[0]
    gu = matmul_cs(h, w_in, "nn", tag + "_in")
    act = rowmap(_swiglu_act, [gu], [], [(D_FF, BF16)], tb=256, name=tag + "_act")[0]
    y = matmul(act, w_out, "nn", tag + "_out", add=x, scale=0.5)
    return y, (x, h, gu, act)


def _ffn_bwd(dy, saved, norm, w_in, w_out, tag):
    x, h, gu, act = saved
    dact = matmul(dy, w_out, "nt", tag + "_dact", scale=0.5)
    dw_out = matmul(act, dy, "tn", tag + "_dwout", scale=0.5)

    def act_bwd(gu_b, dact_b):
        return jax.vjp(_swiglu_act, gu_b)[1](dact_b)[0]

    dgu = rowmap(act_bwd, [gu, dact], [], [(2 * D_FF, BF16)], tb=256, name=tag + "_dgu")[0]
    dh = matmul_cs(dgu, w_in, "nt", tag + "_dh")
    dw_in = matmul_cs(h, dgu, "tn", tag + "_dwin")

    def norm_bwd(x_b, dh_b, dy_b, g):
        dx, dg = jax.vjp(_rms, x_b, g)[1](dh_b)
        return dy_b + dx, dg

    dx, dnorm = rowmap(norm_bwd, [x, dh, dy], [norm], [(D_MODEL, F32)], [(1, D_MODEL)], tb=256,
                       name=tag + "_dnorm")
    return dx, dnorm, dw_in, dw_out


def layer_step(x, tgt, W, P):
    S = x.shape[0]
    head_of = lambda n: jnp.arange(n)[:, None] // HEAD_DIM == jnp.arange(n // HEAD_DIM)[None, :]
    seg, seg_a = head_of(D_MODEL).astype(BF16), head_of(ATTN_WIDTH).astype(BF16)
    seg_t, seg_a_t = seg.T, seg_a.T
    tile_t = (jnp.arange(HEAD_DIM)[:, None] == jnp.arange(ATTN_WIDTH)[None, :] % HEAD_DIM).astype(BF16)
    qk_params = [P["attn_q_norm"], P["attn_k_norm"], seg_a, seg_a_t, tile_t]
    w_rkv, w_lora = W["w_in"][:, :RKV], W["w_in"][:, RKV:RKV + LORA]
    w_qkv = W["w_in"][:, RKV + LORA:RKV + LORA + 3 * ATTN_WIDTH]
    w_gate = W["w_in"][:, RKV + LORA + 3 * ATTN_WIDTH:]
    mu_rk, mu_lo = P["rwkv_mu"][:, :RKV], P["rwkv_mu"][:, RKV:]
    zeros = lambda n: jnp.zeros((n, D_MODEL), F32)
    w2p = jnp.concatenate([W["rwkv_w2"], zeros(LORA - LORA_W)], axis=0)
    a2p = jnp.concatenate([zeros(LORA_W), W["rwkv_a2"], zeros(LORA_G)], axis=0)
    g2p = jnp.concatenate([zeros(LORA_W + LORA_A), W["rwkv_g2"]], axis=0)
    pre_params = [P["rwkv_w0"], w2p, P["rwkv_a0"], a2p, g2p, P["rwkv_k_k"], P["rwkv_k_a"], seg, seg_t]
    post_params = [P["rwkv_r_k"], P["rwkv_ln_w"], P["rwkv_ln_b"], seg, seg_t]
    col = lambda arr, j: (arr, D_MODEL, j)

    x1, ffn1_saved = _ffn_fwd(x, P["ffn1_norm"], W["ffn1_w_in"], W["ffn1_w_out"], "ffn1")
    h = rowmap(_rms, [x1], [P["mix_norm"]], [(D_MODEL, BF16)], tb=512, name="mix_norm")[0]
    p_rk = matmul(h, w_rkv, "nn", "proj_rkv")
    p_lo = matmul(h, w_lora, "nn", "proj_lora")
    p_qkv = matmul(h, w_qkv, "nn", "proj_qkv")
    p_gate = matmul(h, w_gate, "nn", "proj_gate")
    xs_rk = token_shift_fwd(p_rk, mu_rk, tb=256, name="shift_rk")
    xs_lo = token_shift_fwd(p_lo, mu_lo, tb=256, name="shift_lora")
    lw, k_mod, a_neg, b_kk, g = rowmap(
        _rwkv_pre, [xs_rk, xs_lo], pre_params, [(D_MODEL, F32)] * 5, tb=256, name="rwkv_pre")
    wkv, states, t_invs = wkv_fwd(xs_rk, lw, k_mod, a_neg, b_kk)
    post_rows = [wkv, col(xs_rk, 0), k_mod, col(xs_rk, 2), g]
    y_a = rowmap(_rwkv_post, post_rows, post_params, [(D_MODEL, BF16)], tb=256, name="rwkv_post")[0]

    qk_rows = [(p_qkv, ATTN_WIDTH, 0), (p_qkv, ATTN_WIDTH, 1)]
    qn, kn = rowmap(_qk_norm, qk_rows, qk_params, [(ATTN_WIDTH, F32)] * 2, tb=256, name="qk_norm")
    q_s, k_s, v_s = to_subsequences(qn), to_subsequences(kn), to_subsequences(p_qkv[:, 2 * ATTN_WIDTH:])
    o_s, lse_s = attn_fwd(q_s, k_s, v_s, S)
    o, lse = from_subsequences(o_s, S), from_subsequences(lse_s, S)
    y_b = rowmap(_group_combine, [o, lse], [], [(ATTN_WIDTH, BF16)], tb=512, name="attn_combine")[0]

    pa = matmul(y_a, W["w_proj_rwkv"], "nn", "proj_a")
    pb = matmul(y_b, W["w_proj_attn"], "nn", "proj_b")
    merged = rowmap(_gate_merge, [p_gate, pa, pb], [P["b_gate"]], [(D_MODEL, BF16)], tb=256, name="merge")[0]
    x2 = matmul(merged, W["w_out"], "nn", "mix_out", add=x1)
    x3, ffn2_saved = _ffn_fwd(x2, P["ffn2_norm"], W["ffn2_w_in"], W["ffn2_w_out"], "ffn2")

    def loss_head(y_b_, t_b):
        err = y_b_ - t_b
        return err * (1.0 / D_MODEL), (0.5 / D_MODEL) * jnp.sum(err * err, axis=0, keepdims=True)

    dx3, loss_cols = rowmap(loss_head, [x3, tgt], [], [(D_MODEL, F32)], [(1, D_MODEL)], tb=512, name="loss")

    gW, gP = {}, {}
    dx2, gP["ffn2_norm"], gW["ffn2_w_in"], gW["ffn2_w_out"] = _ffn_bwd(
        dx3, ffn2_saved, P["ffn2_norm"], W["ffn2_w_in"], W["ffn2_w_out"], "ffn2")

    dmerged = matmul(dx2, W["w_out"], "nt", "d_merged")
    gW["w_out"] = matmul(merged, dx2, "tn", "dw_out")

    def merge_bwd(pg, pa_b, pb_b, dm, bg):
        return jax.vjp(_gate_merge, pg, pa_b, pb_b, bg)[1](dm)

    dp_gate, dpa, dpb, gP["b_gate"] = rowmap(
        merge_bwd, [p_gate, pa, pb, dmerged], [P["b_gate"]],
        [(2 * D_MODEL, BF16), (D_MODEL, BF16), (D_MODEL, BF16)], [(1, 2 * D_MODEL)], tb=256, name="merge_bwd")
    dy_a = matmul(dpa, W["w_proj_rwkv"], "nt", "d_ya")
    gW["w_proj_rwkv"] = matmul(y_a, dpa, "tn", "dw_proj_a")
    dy_b = matmul(dpb, W["w_proj_attn"], "nt", "d_yb")
    gW["w_proj_attn"] = matmul(y_b, dpb, "tn", "dw_proj_b")

    def combine_bwd(o_b, l_b, d_b):
        return jax.vjp(_group_combine, o_b, l_b)[1](d_b)

    do, dlse = rowmap(combine_bwd, [o, lse, dy_b], [], [(ATTN_WIDTH, F32)] * 2, tb=256, name="attn_combine_bwd")
    dq_s, dk_s, dv_s = attn_bwd(q_s, k_s, v_s, to_subsequences(do), to_subsequences(dlse), S)

    def qk_norm_bwd(q_b, k_b, dqn_b, dkn_b, dv_b, qg, kg, sg, sgt, tl):
        f = lambda *a: _qk_norm(*a, sg, sgt, tl)
        dq, dk, dqg, dkg = jax.vjp(f, q_b, k_b, qg, kg)[1]((dqn_b, dkn_b))
        return jnp.concatenate([dq, dk, dv_b], axis=1), dqg, dkg

    dp_qkv, gP["attn_q_norm"], gP["attn_k_norm"] = rowmap(
        qk_norm_bwd, qk_rows + [from_subsequences(t, S) for t in (dq_s, dk_s, dv_s)], qk_params,
        [(3 * ATTN_WIDTH, BF16)], [(1, HEAD_DIM)] * 2, tb=256, name="qk_norm_bwd")

    def post_bwd(wkv_b, r_b, k_b, v_b, g_b, d_b, r_k, ln_w, ln_b, sg, sgt):
        f = lambda *a: _rwkv_post(*a, sg, sgt)
        return jax.vjp(f, wkv_b, r_b, k_b, v_b, g_b, r_k, ln_w, ln_b)[1](d_b)

    dwkv, dr_p, dk_p, dv_p, dg, gP["rwkv_r_k"], gP["rwkv_ln_w"], gP["rwkv_ln_b"] = rowmap(
        post_bwd, post_rows + [dy_a], post_params, [(D_MODEL, F32)] * 5, [(1, D_MODEL)] * 3, tb=128,
        name="rwkv_post_bwd")
    dr_w, dlw, dk_w, dv_w, da_neg, db_kk = wkv_bwd(xs_rk, lw, k_mod, a_neg, b_kk, states, t_invs, dwkv)

    def pre_bwd(xrk_b, xlo_b, dlw_b, dkw_b, dkp_b, da_b, db_b, dg_b, drp_b, drw_b, dvp_b, dvw_b,
                w0, w2, a0, a2, g2, k_k, k_a, sg, sgt):
        f = lambda *a: _rwkv_pre(*a, sg, sgt)
        pull = jax.vjp(f, xrk_b, xlo_b, w0, w2, a0, a2, g2, k_k, k_a)[1]
        dxrk, dxlo, *dpar = pull((dlw_b, dkw_b + dkp_b, da_b, db_b, dg_b))
        direct = jnp.concatenate([drp_b + drw_b, jnp.zeros_like(drp_b), dvp_b + dvw_b], axis=1)
        return (dxrk + direct, dxlo, *dpar)

    pre_rows = [xs_rk, xs_lo, dlw, dk_w, dk_p, da_neg, db_kk, dg, dr_p, dr_w, dv_p, dv_w]
    dxs_rk, dxs_lo, gP["rwkv_w0"], dw2p, gP["rwkv_a0"], da2p, dg2p, gP["rwkv_k_k"], gP["rwkv_k_a"] = rowmap(
        pre_bwd, pre_rows, pre_params, [(RKV, F32), (LORA, F32)],
        [(1, D_MODEL), (LORA, D_MODEL), (1, D_MODEL), (LORA, D_MODEL), (LORA, D_MODEL), (1, D_MODEL), (1, D_MODEL)],
        tb=128, name="rwkv_pre_bwd")
    gW["rwkv_w2"] = dw2p[:LORA_W]
    gW["rwkv_a2"] = da2p[LORA_W:LORA_W + LORA_A]
    gW["rwkv_g2"] = dg2p[LORA_W + LORA_A:]
    dp_rk, dmu_rk = token_shift_bwd(dxs_rk, p_rk, mu_rk, tb=256, name="shift_rk_bwd")
    dp_lo, dmu_lo = token_shift_bwd(dxs_lo, p_lo, mu_lo, tb=256, name="shift_lora_bwd")
    gP["rwkv_mu"] = jnp.concatenate([dmu_rk, dmu_lo], axis=1)

    dh = matmul(dp_rk, w_rkv, "nt", "dh_rkv")
    dh = matmul(dp_lo, w_lora, "nt", "dh_lora", add=dh)
    dh = matmul(dp_qkv, w_qkv, "nt", "dh_qkv", add=dh)
    dh = matmul(dp_gate, w_gate, "nt", "dh_gate", add=dh)
    gW["w_in"] = jnp.concatenate([
        matmul(h, dp_rk, "tn", "dw_rkv"), matmul(h, dp_lo, "tn", "dw_lora"),
        matmul(h, dp_qkv, "tn", "dw_qkv"), matmul(h, dp_gate, "tn", "dw_gate")], axis=1)

    def norm_bwd(x_b, dh_b, dy_b, gn):
        dx, dgn = jax.vjp(_rms, x_b, gn)[1](dh_b)
        return dy_b + dx, dgn

    dx1, gP["mix_norm"] = rowmap(norm_bwd, [x1, dh, dx2], [P["mix_norm"]], [(D_MODEL, F32)], [(1, D_MODEL)],
                                 tb=256, name="mix_norm_bwd")
    dx, gP["ffn1_norm"], gW["ffn1_w_in"], gW["ffn1_w_out"] = _ffn_bwd(
        dx1, ffn1_saved, P["ffn1_norm"], W["ffn1_w_in"], W["ffn1_w_out"], "ffn1")
    return loss_cols, dx, gW, gP


N_SHARDS = 4
BIG = (("ffn1_w_in", (D_MODEL, 2 * D_FF), 1), ("ffn1_w_out", (D_FF, D_MODEL), 0),
       ("w_in", (D_MODEL, 7712), 1), ("rwkv_w2", (LORA_W, D_MODEL), 1), ("rwkv_a2", (LORA_A, D_MODEL), 1),
       ("rwkv_g2", (LORA_G, D_MODEL), 1), ("w_proj_rwkv", (D_MODEL, D_MODEL), 0),
       ("w_proj_attn", (ATTN_WIDTH, D_MODEL), 1), ("w_out", (D_MODEL, D_MODEL), 0),
       ("ffn2_w_in", (D_MODEL, 2 * D_FF), 1), ("ffn2_w_out", (D_FF, D_MODEL), 0))
SMALL = (("ffn1_norm", 1024), ("mix_norm", 1024), ("b_gate", 2048), ("rwkv_mu", 3360), ("rwkv_w0", 1024),
         ("rwkv_a0", 1024), ("rwkv_k_k", 1024), ("rwkv_k_a", 1024), ("rwkv_r_k", 1024), ("rwkv_ln_w", 1024),
         ("rwkv_ln_b", 1024), ("attn_q_norm", 64), ("attn_k_norm", 64), ("ffn2_norm", 1024))
WEIGHT_ORDER = ("ffn1_norm", "ffn1_w_in", "ffn1_w_out", "mix_norm", "w_in", "b_gate", "rwkv_mu", "rwkv_w0",
                "rwkv_w2", "rwkv_a0", "rwkv_a2", "rwkv_g2", "rwkv_k_k", "rwkv_k_a", "rwkv_r_k", "rwkv_ln_w",
                "rwkv_ln_b", "attn_q_norm", "attn_k_norm", "w_proj_rwkv", "w_proj_attn", "w_out", "ffn2_norm",
                "ffn2_w_in", "ffn2_w_out")


LORA_PARTS = ("rwkv_w2", "rwkv_a2", "rwkv_g2")
BLOCK_MAJOR = ("ffn1_w_in", "ffn2_w_in")
SMALL_USED = D_MODEL + sum(n for _, n in SMALL)
SMALL_W = -(-SMALL_USED // 128) * 128


def _travel():
    out = {}
    for name, shape, axis in BIG:
        if name == LORA_PARTS[0]:
            out["lora"] = ((LORA, D_MODEL), 1)
        elif name not in LORA_PARTS:
            out[name] = (shape, axis)
    return out


def local_blocks(vals):
    out = {n: vals[n] for n in _travel() if n != "lora"}
    out["lora"] = jnp.concatenate([vals[n] for n in LORA_PARTS], axis=0)
    return out


def split_lora(t):
    return {"rwkv_w2": t[:LORA_W], "rwkv_a2": t[LORA_W:LORA_W + LORA_A], "rwkv_g2": t[LORA_W + LORA_A:]}


def blocks_to_full(name, blocks):
    shape, axis = _travel()[name]
    if name in BLOCK_MAJOR:
        return blocks
    if axis == 0:
        return blocks.reshape(shape)
    return blocks.transpose(1, 0, 2).reshape(shape)


def full_to_blocks(name, full):
    shape, axis = _travel()[name]
    if name in BLOCK_MAJOR:
        return full
    if axis == 0:
        return full.reshape(N_SHARDS, shape[0] // N_SHARDS, shape[1])
    return full.reshape(shape[0], N_SHARDS, shape[1] // N_SHARDS).transpose(1, 0, 2)


def pack_small(vals, head):
    parts = [head] + [vals[name].reshape(1, n) for name, n in SMALL]
    parts.append(jnp.zeros((1, SMALL_W - SMALL_USED), F32))
    return jnp.concatenate(parts, axis=1)


def unpack_small(vec, shapes):
    out, off = {}, D_MODEL
    for name, n in SMALL:
        out[name] = vec[:, off:off + n].reshape(shapes[name])
        off += n
    return out


def _place():
    return lax.axis_index("x"), lax.axis_index("y"), lax.axis_index("c")


def _other_chips(x, y):
    return [(1 - x, y), (x, 1 - y), (1 - x, 1 - y)]


def _remote(src, dst, send_sem, recv_sem, device):
    return pltpu.make_async_remote_copy(src_ref=src, dst_ref=dst, send_sem=send_sem, recv_sem=recv_sem,
                                        device_id=device, device_id_type=MESH)


def _half(ref, who):
    hr = ref.shape[-2] // 2
    rows = pl.ds(pl.multiple_of(who * hr, 8), hr)
    return ref.at[rows] if len(ref.shape) == 2 else ref.at[:, rows]


HBM_REF = pl.BlockSpec(memory_space=pl.ANY)
COMM_PARAMS = dict(compiler_params=pltpu.CompilerParams(has_side_effects=True))


def gather_weights(blocks):
    n = len(blocks)

    def body(*refs):
        ins, outs = refs[:n], refs[n:2 * n]
        ici_send, ici_recv, d2d_send, d2d_recv = refs[2 * n:]
        x, y, c = _place()
        me, sibling, chips = 2 * x + y, (x, y, 1 - c), _other_chips(x, y)
        first = [_remote(_half(ins[t], c), _half(outs[t].at[me], c), ici_send.at[k, t], ici_recv.at[k, t],
                         (px, py, c)) for k, (px, py) in enumerate(chips) for t in range(n)]
        for cp in first:
            cp.start()
        passed = []
        for k, (px, py) in enumerate(chips):
            for t in range(n):
                landed = _half(outs[t].at[2 * px + py], c)
                _remote(landed, landed, ici_send.at[k, t], ici_recv.at[k, t], (px, py, c)).wait_recv()
                cp = _remote(landed, landed, d2d_send.at[k, t], d2d_recv.at[k, t], sibling)
                cp.start()
                passed.append(cp)
        for k, (px, py) in enumerate(chips):
            for t in range(n):
                other = _half(outs[t].at[2 * px + py], 1 - c)
                _remote(other, other, d2d_send.at[k, t], d2d_recv.at[k, t], sibling).wait_recv()
        for cp in first + passed:
            cp.wait_send()

    res = pl.pallas_call(
        body, name="gather_weights", in_specs=[HBM_REF] * n, out_specs=[HBM_REF] * n,
        out_shape=[jax.ShapeDtypeStruct((N_SHARDS,) + b.shape, b.dtype) for b in blocks],
        scratch_shapes=[pltpu.SemaphoreType.DMA((3, n))] * 4, **COMM_PARAMS)(*blocks)
    me = 2 * lax.axis_index("x") + lax.axis_index("y")
    return [lax.dynamic_update_slice(g, b[None], (me, 0, 0)) for g, b in zip(res, blocks)]


def swap_halves(grads):
    n = len(grads)

    def body(*refs):
        ins, got = refs[:n], refs[n:2 * n]
        send_sems, recv_sems = refs[2 * n:]
        x, y, c = _place()
        give = [_remote(_half(ins[t], 1 - c), got[t], send_sems.at[t], recv_sems.at[t], (x, y, 1 - c))
                for t in range(n)]
        for cp in give:
            cp.start()
        for cp in give:
            cp.wait_recv()
        for cp in give:
            cp.wait_send()

    return pl.pallas_call(
        body, name="swap_halves", in_specs=[HBM_REF] * n, out_specs=[HBM_REF] * n,
        out_shape=[jax.ShapeDtypeStruct((g.shape[0], g.shape[1] // 2, g.shape[2]), g.dtype) for g in grads],
        scratch_shapes=[pltpu.SemaphoreType.DMA((n,))] * 2, **COMM_PARAMS)(*grads)


def scatter_partials(partials):
    n = len(partials)

    def body(*refs):
        parts, landed = refs[:n], refs[n:2 * n]
        send_sems, recv_sems = refs[2 * n:]
        x, y, c = _place()
        sends = [_remote(parts[t].at[2 * px + py], landed[t].at[k], send_sems.at[k, t], recv_sems.at[k, t],
                         (px, py, c)) for k, (px, py) in enumerate(_other_chips(x, y)) for t in range(n)]
        for cp in sends:
            cp.start()
        for cp in sends:
            cp.wait_recv()
        for cp in sends:
            cp.wait_send()

    return pl.pallas_call(
        body, name="scatter_partials", in_specs=[HBM_REF] * n, out_specs=[HBM_REF] * n,
        out_shape=[jax.ShapeDtypeStruct((3,) + p.shape[1:], p.dtype) for p in partials],
        scratch_shapes=[pltpu.SemaphoreType.DMA((3, n))] * 2, **COMM_PARAMS)(*partials)


def join_halves(blocks):
    n = len(blocks)

    def body(*refs):
        outs = refs[n:2 * n]
        send_sems, recv_sems = refs[2 * n:]
        x, y, c = _place()
        give = [_remote(_half(outs[t], c), _half(outs[t], c), send_sems.at[t], recv_sems.at[t], (x, y, 1 - c))
                for t in range(n)]
        for cp in give:
            cp.start()
        for t in range(n):
            arriving = _half(outs[t], 1 - c)
            _remote(arriving, arriving, send_sems.at[t], recv_sems.at[t], (x, y, 1 - c)).wait_recv()
        for cp in give:
            cp.wait_send()

    return pl.pallas_call(
        body, name="join_halves", in_specs=[HBM_REF] * n, out_specs=[HBM_REF] * n,
        out_shape=[jax.ShapeDtypeStruct(b.shape, b.dtype) for b in blocks],
        input_output_aliases={t: t for t in range(n)},
        scratch_shapes=[pltpu.SemaphoreType.DMA((n,))] * 2, **COMM_PARAMS)(*blocks)


def reduce_block_grads(grads):
    names = list(grads)
    got = swap_halves([grads[n] for n in names])
    partials = []
    for name, theirs in zip(names, got):
        n_slot, hr, width = theirs.shape
        tb = _row_block(hr, width, 6)
        per_half = hr // tb
        mine = lambda i, s, per_half=per_half: (i // per_half) * 2 * per_half + s[0] * per_half + i % per_half
        p = placed_map(
            jnp.add,
            [(grads[name].reshape(2 * n_slot * hr, width), mine), (theirs.reshape(n_slot * hr, width), lambda i, s: i)],
            (n_slot * hr, width, BF16, lambda i, s: i), n_blocks=n_slot * per_half, tb=tb, name="chip_sum_" + name)
        partials.append(p.reshape(theirs.shape))
    landed = scatter_partials(partials)
    blocks = []
    for name, theirs, arrived in zip(names, got, landed):
        n_slot, hr, width = theirs.shape
        tb = _row_block(hr, width, 6)
        per_half = hr // tb
        views = [(grads[name].reshape(2 * n_slot * hr, width),
                  lambda i, s, per_half=per_half: s[1] * 2 * per_half + s[0] * per_half + i),
                 (theirs.reshape(n_slot * hr, width), lambda i, s, per_half=per_half: s[1] * per_half + i)]
        views += [(arrived.reshape(3 * hr, width), functools.partial(lambda k, per_half, i, s: k * per_half + i,
                                                                     k, per_half)) for k in range(3)]
        f = lambda a, b, l0, l1, l2: (((a + b) + l0.astype(F32)) + l1.astype(F32)) + l2.astype(F32)
        blocks.append(placed_map(
            f, views,(2 * hr, width, F32, lambda i, s, per_half=per_half: s[0] * per_half + i),
            n_blocks=per_half, tb=tb, name="owner_sum_" + name))
    return dict(zip(names, join_halves(blocks)))


def adamw_block(name, w, g, m, v):
    rows, width = w.shape
    return rowmap(_adamw, [w, g, m, v], [], [(width, F32)] * 3, tb=_row_block(rows, width, 7),
                  name="adamw_" + name)


def reduce_small(vec, w, m, v):
    n_dev = 8

    def body(vec_ref, w_ref, m_ref, v_ref, loss_ref, g_ref, d_ref, m2_ref, v2_ref, slots, send_sems, recv_sems):
        x, y, c = _place()
        me = 4 * x + 2 * y + c
        slots[me] = vec_ref[...]
        flips = [(fx, fy, fc) for fx in (0, 1) for fy in (0, 1) for fc in (0, 1)][1:]
        peers = [(1 - x if fx else x, 1 - y if fy else y, 1 - c if fc else c) for fx, fy, fc in flips]
        sends = [pltpu.make_async_remote_copy(
            src_ref=vec_ref, dst_ref=slots.at[me], send_sem=send_sems.at[j], recv_sem=recv_sems.at[j],
            device_id=peer, device_id_type=MESH) for j, peer in enumerate(peers)]
        for cp in sends:
            cp.start()
        for j, (px, py, pc) in enumerate(peers):
            pltpu.make_async_remote_copy(
                src_ref=vec_ref, dst_ref=slots.at[4 * px + 2 * py + pc], send_sem=send_sems.at[j],
                recv_sem=recv_sems.at[j], device_id=(px, py, pc), device_id_type=MESH).wait_recv()
        for cp in sends:
            cp.wait_send()
        g = slots[0]
        for d in range(1, n_dev):
            g = g + slots[d]
        loss_ref[...] = jnp.sum(g[:, :D_MODEL], axis=1, keepdims=True)
        delta, m2, v2 = _adamw(w_ref[...], g, m_ref[...], v_ref[...])
        g_ref[...], d_ref[...], m2_ref[...], v2_ref[...] = g, delta, m2, v2

    vm = pl.BlockSpec(memory_space=pltpu.VMEM)
    vec_t = jax.ShapeDtypeStruct(vec.shape, F32)
    return pl.pallas_call(
        body, name="reduce_small", in_specs=[vm] * 4, out_specs=[vm] * 5,
        out_shape=[jax.ShapeDtypeStruct((1, 1), F32)] + [vec_t] * 4,
        scratch_shapes=[pltpu.VMEM((n_dev,) + vec.shape, F32), pltpu.SemaphoreType.DMA((n_dev - 1,)),
                        pltpu.SemaphoreType.DMA((n_dev - 1,))],
        compiler_params=pltpu.CompilerParams(has_side_effects=True),
    )(vec, w, m, v)


def kernel(x, ffn1_norm, ffn1_w_in, ffn1_w_out, mix_norm, w_in, b_gate, rwkv_mu, rwkv_w0, rwkv_w2, rwkv_a0, rwkv_a2, rwkv_g2, rwkv_k_k, rwkv_k_a, rwkv_r_k, rwkv_ln_w, rwkv_ln_b, attn_q_norm, attn_k_norm, w_proj_rwkv, w_proj_attn, w_out, ffn2_norm, ffn2_w_in, ffn2_w_out, loss_target, m_ffn1_norm, m_ffn1_w_in, m_ffn1_w_out, m_mix_norm, m_w_in, m_b_gate, m_rwkv_mu, m_rwkv_w0, m_rwkv_w2, m_rwkv_a0, m_rwkv_a2, m_rwkv_g2, m_rwkv_k_k, m_rwkv_k_a, m_rwkv_r_k, m_rwkv_ln_w, m_rwkv_ln_b, m_attn_q_norm, m_attn_k_norm, m_w_proj_rwkv, m_w_proj_attn, m_w_out, m_ffn2_norm, m_ffn2_w_in, m_ffn2_w_out, v_ffn1_norm, v_ffn1_w_in, v_ffn1_w_out, v_mix_norm, v_w_in, v_b_gate, v_rwkv_mu, v_rwkv_w0, v_rwkv_w2, v_rwkv_a0, v_rwkv_a2, v_rwkv_g2, v_rwkv_k_k, v_rwkv_k_a, v_rwkv_r_k, v_rwkv_ln_w, v_rwkv_ln_b, v_attn_q_norm, v_attn_k_norm, v_w_proj_rwkv, v_w_proj_attn, v_w_out, v_ffn2_norm, v_ffn2_w_in, v_ffn2_w_out):
    given = dict(locals())
    weights = {n: given[n] for n in WEIGHT_ORDER}
    mom_m = {n: given["m_" + n] for n in WEIGHT_ORDER}
    mom_v = {n: given["v_" + n] for n in WEIGHT_ORDER}
    big = [name for name, _, _ in BIG]
    shapes = {n: weights[n].shape for n in WEIGHT_ORDER}
    blocks_of = lambda d: local_blocks({n: d[n][0] for n in big})
    w_blk, m_blk, v_blk = blocks_of(weights), blocks_of(mom_m), blocks_of(mom_v)
    names = list(w_blk)

    gathered = gather_weights([w_blk[n].astype(BF16) for n in names])
    W = {n: blocks_to_full(n, g) for n, g in zip(names, gathered)}
    W.update(split_lora(W.pop("lora")))
    P = {n: weights[n].reshape(1, -1) for n, _ in SMALL}

    loss_cols, dx, gW, gP = layer_step(x[0], loss_target[0], W, P)

    gW["lora"] = jnp.concatenate([gW.pop(n) for n in LORA_PARTS], axis=0)
    g_blk = reduce_block_grads({n: full_to_blocks(n, gW[n]) for n in names})
    out_g, out_d, out_m, out_v = {}, {}, {}, {}
    for n in names:
        res = (g_blk[n], *adamw_block(n, w_blk[n], g_blk[n], m_blk[n], v_blk[n]))
        for dst, t in zip((out_g, out_d, out_m, out_v), res):
            for part, val in (split_lora(t) if n == "lora" else {n: t}).items():
                dst[part] = val.reshape(shapes[part])

    zero_head = jnp.zeros((1, D_MODEL), F32)
    vec = pack_small(gP, loss_cols)
    loss, g_s, d_s, m_s, v_s = reduce_small(
        vec, pack_small({n: weights[n] for n, _ in SMALL}, zero_head),
        pack_small({n: mom_m[n] for n, _ in SMALL}, zero_head),
        pack_small({n: mom_v[n] for n, _ in SMALL}, zero_head))
    for dst, src in ((out_g, g_s), (out_d, d_s), (out_m, m_s), (out_v, v_s)):
        dst.update(unpack_small(src, shapes))

    return (loss[0, 0], dx[None], *[out_g[n] for n in WEIGHT_ORDER], *[out_d[n] for n in WEIGHT_ORDER],
            *[out_m[n] for n in WEIGHT_ORDER], *[out_v[n] for n in WEIGHT_ORDER])
```

```python
import functools

import jax
import jax.numpy as jnp
from jax import lax
from jax.experimental import pallas as pl
from jax.experimental.pallas import tpu as pltpu

F32 = jnp.float32
BF16 = jnp.bfloat16
MESH = pl.DeviceIdType.MESH

D_MODEL = 1024
HEAD_DIM = 64
RWKV_HEADS = 16
LORA_W, LORA_A, LORA_G = 64, 64, 160
LORA = LORA_W + LORA_A + LORA_G
RKV = 3 * D_MODEL
ATTN_PAIRS = ((128, 1), (512, 4), (2048, 16))
ATTN_BLK = 128
ATTN_HPG = 4
ATTN_WIDTH = 768
GROUP_W = ATTN_HPG * HEAD_DIM
D_FF = 2816
GN_EPS = 64e-5
RMS_EPS = 1e-6
NEG_INF = -1e30
WKV_CHUNK = 64
WKV_HEADS_PER_STEP = 16

ADAM_LR, ADAM_B1, ADAM_B2, ADAM_EPS, ADAM_WD, ADAM_STEP = 0.001, 0.9, 0.999, 1e-08, 0.01, 10

V7X_VMEM_BYTES = 64 << 20
VMEM_TEMP_ALLOWANCE = 20 << 20


def _cparams(sem, block_bytes):
    limit = min(2 * block_bytes + VMEM_TEMP_ALLOWANCE, V7X_VMEM_BYTES - (6 << 20))
    return pltpu.CompilerParams(dimension_semantics=sem, vmem_limit_bytes=int(limit))


def _nbytes(shape, dtype):
    n = 1
    for s in shape:
        n *= s
    return n * jnp.dtype(dtype).itemsize


def _split_bf16(a):
    hi = a.astype(BF16)
    return hi, (a - hi.astype(F32)).astype(BF16)


def _make_dots():
    def raw(a, b, ca, cb):
        return lax.dot_general(a.astype(BF16), b.astype(BF16), (((ca,), (cb,)), ((), ())),
                               preferred_element_type=F32)

    @jax.custom_vjp
    def nn(a, b):
        return raw(a, b, 1, 0)

    @jax.custom_vjp
    def nt(a, b):
        return raw(a, b, 1, 1)

    @jax.custom_vjp
    def tn(a, b):
        return raw(a, b, 0, 0)

    nn.defvjp(lambda a, b: (raw(a, b, 1, 0), (a, b)),
              lambda res, g: (raw(g, res[1], 1, 1), raw(res[0], g, 0, 0)))
    nt.defvjp(lambda a, b: (raw(a, b, 1, 1), (a, b)),
              lambda res, g: (raw(g, res[1], 1, 0), raw(g, res[0], 0, 0)))
    tn.defvjp(lambda a, b: (raw(a, b, 0, 0), (a, b)),
              lambda res, g: (raw(res[1], g, 1, 1), raw(res[0], g, 1, 0)))
    return nn, nt, tn


def _exact_rhs_dot(x, ones, cx, co):
    hi, lo = _split_bf16(x)
    dims = (((cx,), (co,)), ((), ()))
    return (lax.dot_general(hi, ones, dims, preferred_element_type=F32)
            + lax.dot_general(lo, ones, dims, preferred_element_type=F32))


@jax.custom_vjp
def SEG(x, ones):
    return _exact_rhs_dot(x, ones, 1, 0)


SEG.defvjp(lambda x, ones: (_exact_rhs_dot(x, ones, 1, 0), ones),
           lambda ones, g: (_exact_rhs_dot(g, ones, 1, 1), jnp.zeros_like(ones)))

NN, NT, TN = _make_dots()


MM_TILE_M, MM_TILE_N, MM_TILE_K = 1408, 1408, 1536


def _pick(n, cap):
    best = None
    for t in range(128, min(n, cap) + 1, 128):
        if n % t == 0:
            best = t
    return best or n


def matmul(a, b, mode, name, *, add=None, scale=1.0, out_dtype=F32):
    if mode == "nn":
        (M, K), (K2, N) = a.shape, b.shape
    elif mode == "nt":
        (M, K), (N, K2) = a.shape, b.shape
    else:
        (K, M), (K2, N) = a.shape, b.shape
    assert K == K2, (name, a.shape, b.shape)
    tm, tn, tk = _pick(M, MM_TILE_M), _pick(N, MM_TILE_N), _pick(K, MM_TILE_K)
    nk = K // tk
    ca, cb = {"nn": (1, 0), "nt": (1, 1), "tn": (0, 0)}[mode]

    def body(*refs):
        if add is None:
            a_ref, b_ref, o_ref, acc_ref = refs
        else:
            a_ref, b_ref, add_ref, o_ref, acc_ref = refs
        k = pl.program_id(2)

        @pl.when(k == 0)
        def _():
            acc_ref[...] = jnp.zeros_like(acc_ref)

        acc_ref[...] += lax.dot_general(a_ref[...].astype(BF16), b_ref[...].astype(BF16),
                                        (((ca,), (cb,)), ((), ())), preferred_element_type=F32)

        @pl.when(k == nk - 1)
        def _():
            r = acc_ref[...] * scale
            if add is not None:
                r = add_ref[...] + r
            o_ref[...] = r.astype(o_ref.dtype)

    a_spec = (pl.BlockSpec((tk, tm), lambda i, j, k: (k, i)) if mode == "tn"
              else pl.BlockSpec((tm, tk), lambda i, j, k: (i, k)))
    b_spec = (pl.BlockSpec((tn, tk), lambda i, j, k: (j, k)) if mode == "nt"
              else pl.BlockSpec((tk, tn), lambda i, j, k: (k, j)))
    in_specs, args = [a_spec, b_spec], [a, b]
    blk = tm * tk * a.dtype.itemsize + tk * tn * b.dtype.itemsize + tm * tn * 8
    if add is not None:
        in_specs.append(pl.BlockSpec((tm, tn), lambda i, j, k: (i, j)))
        args.append(add)
        blk += tm * tn * 4
    return pl.pallas_call(
        body, name=name, grid=(M // tm, N // tn, nk),
        in_specs=in_specs, out_specs=pl.BlockSpec((tm, tn), lambda i, j, k: (i, j)),
        out_shape=jax.ShapeDtypeStruct((M, N), out_dtype),
        scratch_shapes=[pltpu.VMEM((tm, tn), F32)],
        compiler_params=_cparams(("parallel", "parallel", "arbitrary"), blk),
    )(*args)


def matmul_cs(a, w, mode, name, *, scale=1.0, out_dtype=F32):
    n_blk = N_SHARDS
    if mode == "tn":
        (K, R), Cs = a.shape, w.shape[2] // 2
        tm, tk = _pick(R, MM_TILE_M), _pick(K, 1024)
        grid = (R // tm, n_blk, K // tk)
        a_spec = pl.BlockSpec((tk, tm), lambda i, j, k: (k, i))
        w_spec = pl.BlockSpec((None, tk, Cs), lambda i, j, k: (j // 2, k, j % 2))
        o_spec = pl.BlockSpec((None, tm, Cs), lambda i, j, k: (j, i, 0))
        out_shape, acc_shape, dims = (n_blk, R, Cs), (tm, Cs), (0, 0)
        blk = tk * tm * a.dtype.itemsize + tk * Cs * w.dtype.itemsize + tm * Cs * 8
    else:
        M, (_, R, Cs) = a.shape[1], w.shape
        tm, tn = _pick(M, MM_TILE_M), _pick(R, MM_TILE_N)
        grid = (M // tm, R // tn, n_blk)
        a_spec = pl.BlockSpec((None, tm, Cs), lambda i, j, k: (k // 2, i, k % 2))
        w_spec = pl.BlockSpec((None, tn, Cs), lambda i, j, k: (k, j, 0))
        o_spec = pl.BlockSpec((tm, tn), lambda i, j, k: (i, j))
        out_shape, acc_shape, dims = (M, R), (tm, tn), (1, 1)
        blk = tm * Cs * a.dtype.itemsize + tn * Cs * w.dtype.itemsize + tm * tn * 8
    nk = grid[2]

    def body(a_ref, w_ref, o_ref, acc_ref):
        k = pl.program_id(2)

        @pl.when(k == 0)
        def _():
            acc_ref[...] = jnp.zeros_like(acc_ref)

        acc_ref[...] += lax.dot_general(a_ref[...].astype(BF16), w_ref[...].astype(BF16),
                                        (((dims[0],), (dims[1],)), ((), ())), preferred_element_type=F32)

        @pl.when(k == nk - 1)
        def _():
            o_ref[...] = (acc_ref[...] * scale).astype(o_ref.dtype)

    return pl.pallas_call(
        body, name=name, grid=grid, in_specs=[a_spec, w_spec], out_specs=o_spec,
        out_shape=jax.ShapeDtypeStruct(out_shape, out_dtype), scratch_shapes=[pltpu.VMEM(acc_shape, F32)],
        compiler_params=_cparams(("parallel", "parallel", "arbitrary"), blk),
    )(a, w)


FFN_TILE_M = 512


def _swiglu(gate, up):
    return gate * jax.nn.sigmoid(gate) * up


def ffn_in_act(h, w, name):
    (M, R), Cs, half = h.shape, w.shape[2], N_SHARDS // 2
    tm, tk = _pick(M, FFN_TILE_M), _pick(R, 1024)
    nk = R // tk

    def body(h_ref, wg_ref, wu_ref, gu_ref, act_ref, acc_ref):
        k = pl.program_id(2)

        @pl.when(k == 0)
        def _():
            acc_ref[...] = jnp.zeros_like(acc_ref)

        hb = h_ref[...].astype(BF16)
        for part, w_ref in enumerate((wg_ref, wu_ref)):
            acc_ref[part] += jnp.dot(hb, w_ref[...].astype(BF16), preferred_element_type=F32)

        @pl.when(k == nk - 1)
        def _():
            gu_ref[...] = acc_ref[...]
            act_ref[...] = _swiglu(acc_ref[0], acc_ref[1]).astype(act_ref.dtype)

    w_spec = lambda off: pl.BlockSpec((None, tk, Cs), functools.partial(lambda off, i, j, k: (j + off, k, 0), off))
    blk = tm * tk * h.dtype.itemsize + 2 * tk * Cs * w.dtype.itemsize + tm * Cs * (16 + 2)
    return pl.pallas_call(
        body, name=name, grid=(M // tm, half, nk),
        in_specs=[pl.BlockSpec((tm, tk), lambda i, j, k: (i, k)), w_spec(0), w_spec(half)],
        out_specs=[pl.BlockSpec((2, tm, Cs), lambda i, j, k: (0, i, j)), pl.BlockSpec((tm, Cs), lambda i, j, k: (i, j))],
        out_shape=[jax.ShapeDtypeStruct((2, M, half * Cs), F32), jax.ShapeDtypeStruct((M, half * Cs), BF16)],
        scratch_shapes=[pltpu.VMEM((2, tm, Cs), F32)],
        compiler_params=_cparams(("parallel", "parallel", "arbitrary"), blk),
    )(h, w, w)


def ffn_dact_dgu(dy, w_out, gu, scale, name):
    (M, D), F = dy.shape, w_out.shape[0]
    tm, tn = _pick(M, FFN_TILE_M), F // 2

    def body(dy_ref, w_ref, gu_ref, dgu_ref):
        dact = scale * lax.dot_general(dy_ref[...].astype(BF16), w_ref[...].astype(BF16),
                                       (((1,), (1,)), ((), ())), preferred_element_type=F32)
        dgate, dup = jax.vjp(_swiglu, gu_ref[0], gu_ref[1])[1](dact)
        dgu_ref[0] = dgate.astype(dgu_ref.dtype)
        dgu_ref[1] = dup.astype(dgu_ref.dtype)

    pair = pl.BlockSpec((2, tm, tn), lambda i, j: (0, i, j))
    blk = tm * D * dy.dtype.itemsize + tn * D * w_out.dtype.itemsize + 2 * tm * tn * (4 + 2)
    return pl.pallas_call(
        body, name=name, grid=(M // tm, F // tn),
        in_specs=[pl.BlockSpec((tm, D), lambda i, j: (i, 0)), pl.BlockSpec((tn, D), lambda i, j: (j, 0)), pair],
        out_specs=pair, out_shape=jax.ShapeDtypeStruct((2, M, F), BF16),
        compiler_params=_cparams(("parallel", "parallel"), blk),
    )(dy, w_out, gu)


def _row_block(n, width, n_arrays):
    cap = (V7X_VMEM_BYTES // 4) // (2 * 4 * width * n_arrays)
    best = None
    for t in range(16, min(n, cap) + 1, 16):
        if n % t == 0:
            best = t
    return best or n


def placed_map(f, ins, out, *, n_blocks, tb, name):
    def body(*refs):
        refs[-1][...] = f(*[r[...] for r in refs[:-1]]).astype(refs[-1].dtype)

    def spec(fn):
        def index(i):
            x, y, c = _place()
            return fn(i, (c, 2 * x + y)), 0
        return pl.BlockSpec((tb, width), index)

    o_rows, width, o_dtype, o_fn = out
    blk = (sum(a.dtype.itemsize for a, _ in ins) + jnp.dtype(o_dtype).itemsize) * tb * width
    return pl.pallas_call(
        body, name=name, grid=(n_blocks,), in_specs=[spec(fn) for _, fn in ins], out_specs=spec(o_fn),
        out_shape=jax.ShapeDtypeStruct((o_rows, width), o_dtype),
        compiler_params=_cparams(("parallel",), blk),
    )(*[a for a, _ in ins])


def rowmap(f, rows, params, outs, accs=(), *, tb, name):
    rows = [r if isinstance(r, tuple) else (r, r.shape[1], 0) for r in rows]
    S = rows[0][0].shape[0]
    assert S % tb == 0, (name, S, tb)
    n_in, n_out = len(rows) + len(params), len(outs)

    def body(*refs):
        res = f(*[r[...] for r in refs[:n_in]])
        res = res if isinstance(res, (tuple, list)) else (res,)
        o_refs, a_refs = refs[n_in:n_in + n_out], refs[n_in + n_out:]
        for ref, val in zip(o_refs, res[:n_out]):
            ref[...] = val.astype(ref.dtype)
        if a_refs:
            @pl.when(pl.program_id(0) == 0)
            def _():
                for ref in a_refs:
                    ref[...] = jnp.zeros_like(ref)

            for ref, val in zip(a_refs, res[n_out:]):
                ref[...] += val.astype(F32)

    in_specs = [pl.BlockSpec((tb, w), functools.partial(lambda cb, i: (i, cb), cb)) for _, w, cb in rows]
    in_specs += [pl.BlockSpec(p.shape, lambda i: (0, 0)) for p in params]
    out_specs = [pl.BlockSpec((tb, w), lambda i: (i, 0)) for w, _ in outs]
    out_specs += [pl.BlockSpec(tuple(s), lambda i: (0, 0)) for s in accs]
    out_shape = [jax.ShapeDtypeStruct((S, w), dt) for w, dt in outs]
    out_shape += [jax.ShapeDtypeStruct(tuple(s), F32) for s in accs]
    blk = sum(tb * w * a.dtype.itemsize for a, w, _ in rows) + sum(_nbytes(p.shape, p.dtype) for p in params)
    blk += sum(_nbytes((tb, w), dt) for w, dt in outs) + sum(_nbytes(s, F32) for s in accs)
    res = pl.pallas_call(
        body, name=name, grid=(S // tb,), in_specs=in_specs, out_specs=out_specs, out_shape=out_shape,
        compiler_params=_cparams(("arbitrary",) if accs else ("parallel",), blk),
    )(*[r[0] for r in rows], *[pltpu.with_memory_space_constraint(p, pltpu.HBM) for p in params])
    return res


def _rms(x, g):
    return x * lax.rsqrt(jnp.mean(x * x, axis=-1, keepdims=True) + RMS_EPS) * g


def _softplus(z):
    return jnp.maximum(z, 0.0) + jnp.log(1.0 + jnp.exp(-jnp.abs(z)))


def _rwkv_pre(xrk, xlo, w0, w2p, a0, a2p, g2p, k_k, k_a, seg, seg_t):
    k = xrk[:, D_MODEL:2 * D_MODEL]
    w = -_softplus(-(w0 + NN(jnp.tanh(xlo), w2p))) - 0.5
    log_decay = -jnp.exp(w)
    a = jax.nn.sigmoid(a0 + NN(xlo, a2p))
    g = NN(jax.nn.sigmoid(xlo), g2p)
    kk = k * k_k
    norm = jnp.maximum(jnp.sqrt(SEG(kk * kk, seg)), 1e-12)
    kk = kk * SEG(1.0 / norm, seg_t)
    k_mod = k * (1.0 + (a - 1.0) * k_a)
    return log_decay, k_mod, -kk, kk * a, g


def _rwkv_post(wkv, r, k_mod, v, g, r_k, ln_w, ln_b, seg, seg_t):
    inv_n = 1.0 / HEAD_DIM
    mean = SEG(wkv, seg) * inv_n
    cen = wkv - SEG(mean, seg_t)
    var = SEG(cen * cen, seg) * inv_n
    y = cen * SEG(lax.rsqrt(var + GN_EPS), seg_t) * ln_w + ln_b
    bonus = SEG(SEG(r * k_mod * r_k, seg), seg_t) * v
    return (y + bonus) * g


def _qk_norm(q, k, q_gain, k_gain, seg, seg_t, tile_t):
    def norm(x, gain):
        mean_sq = SEG(x * x, seg) * (1.0 / HEAD_DIM)
        return x * SEG(lax.rsqrt(mean_sq + RMS_EPS), seg_t) * SEG(gain, tile_t)

    return norm(q, q_gain) * (HEAD_DIM ** -0.5), norm(k, k_gain)


def _gate_merge(pgate, pa, pb, b_gate):
    sg = jax.nn.sigmoid(pgate + b_gate)
    return sg[:, :D_MODEL] * pa + sg[:, D_MODEL:] * pb


def _group_combine(o, lse):
    ls = [lse[:, GROUP_W * i:GROUP_W * (i + 1)] for i in range(3)]
    m = jnp.maximum(jnp.maximum(ls[0], ls[1]), ls[2])
    es = [jnp.exp(l - m) for l in ls]
    den = es[0] + es[1] + es[2]
    return jnp.concatenate([o[:, GROUP_W * i:GROUP_W * (i + 1)] * (es[i] / den) for i in range(3)], axis=1)


def _each(f, *xs):
    return tuple(f(*args) for args in zip(*xs))


def _attn_block(q, kc, kp, vc, vp, first):
    qi = lax.broadcasted_iota(jnp.int32, (ATTN_BLK, ATTN_BLK), 0)
    kj = lax.broadcasted_iota(jnp.int32, (ATTN_BLK, ATTN_BLK), 1)
    own = kj <= qi
    s_c = _each(lambda a, b: jnp.where(own, NT(a, b), NEG_INF), q, kc)
    s_p = _each(lambda a, b, f: jnp.where((kj >= qi) & (f < 0.5), NT(a, b), NEG_INF), q, kp, first)
    row_max = lambda s: jnp.max(s, axis=-1, keepdims=True)
    row_sum = lambda s: jnp.sum(s, axis=-1, keepdims=True)
    m = _each(lambda c_, p_: jnp.maximum(row_max(c_), row_max(p_)), s_c, s_p)
    e_c, e_p = _each(lambda s, m_: jnp.exp(s - m_), s_c, m), _each(lambda s, m_: jnp.exp(s - m_), s_p, m)
    den = _each(lambda c_, p_: row_sum(c_) + row_sum(p_), e_c, e_p)
    inv = _each(lambda d_: 1.0 / d_, den)
    o = _each(lambda ec, ep, i_, vc_, vp_: (NN(ec, vc_) + NN(ep, vp_)) * i_, e_c, e_p, inv, vc, vp)
    lse = _each(lambda m_, d_: jnp.broadcast_to(m_ + jnp.log(d_), (ATTN_BLK, HEAD_DIM)), m, den)
    return o, lse


def _attn_pair(q, k, k_before, v, v_before, first):
    n = len(q[0])
    o, lse = _attn_block(q[0] + q[1], k[0] + k[1], k_before + k[0], v[0] + v[1], v_before + v[0],
                         (first[0],) * n + (first[1],) * n)
    return (o[:n], o[n:]), (lse[:n], lse[n:])


TRI_SEED = 8


def _tri_inverse(n):
    c = n[0].shape[0]
    row = lax.broadcasted_iota(jnp.int32, (c, c), 0)
    col = lax.broadcasted_iota(jnp.int32, (c, c), 1)
    same_block = lambda size: (row >> (size.bit_length() - 1)) == (col >> (size.bit_length() - 1))
    seed = same_block(TRI_SEED)
    p = _each(lambda m: jnp.where(seed, m, 0.0), n)
    t, span = _each(lambda m: (row == col).astype(F32) + m, p), 2
    while span < TRI_SEED:
        p = _each(NN, p, p)
        t = _each(lambda t_, p_: t_ + NN(t_, p_), t, p)
        span *= 2
    size = TRI_SEED
    while size < c:
        joins = same_block(2 * size) & jnp.logical_not(same_block(size))
        t = _each(lambda t_, m: t_ + NN(NN(t_, jnp.where(joins, m, 0.0)), t_), t, n)
        size *= 2
    return t


@jax.custom_vjp
def _tri_solve(n, rhs, t):
    return _each(NN, t, rhs)


def _tri_solve_fwd(n, rhs, t):
    x = _each(NN, t, rhs)
    return x, (t, x)


def _tri_solve_bwd(res, dx):
    t, x = res
    drhs = _each(TN, t, dx)
    return _each(NT, drhs, x), drhs, _each(jnp.zeros_like, t)


_tri_solve.defvjp(_tri_solve_fwd, _tri_solve_bwd)


def _lower_ones(c):
    row = lax.broadcasted_iota(jnp.int32, (c, c), 0)
    col = lax.broadcasted_iota(jnp.int32, (c, c), 1)
    return (row >= col).astype(BF16)


def _ones_dot(ones, x, contract):
    hi, lo = _split_bf16(x)
    dims = (((contract,), (0,)), ((), ()))
    return (lax.dot_general(ones, hi, dims, preferred_element_type=F32)
            + lax.dot_general(ones, lo, dims, preferred_element_type=F32))


@jax.custom_vjp
def _cumsum_rows(x):
    return _ones_dot(_lower_ones(x.shape[0]), x, 1)


_cumsum_rows.defvjp(lambda x: (_ones_dot(_lower_ones(x.shape[0]), x, 1), None),
                    lambda _, g: (_ones_dot(_lower_ones(g.shape[0]), g, 0),))


def _wkv_chunk(s0, r, lw, k, v, a, b, t_inv=None):
    c = r[0].shape[0]
    row = lax.broadcasted_iota(jnp.int32, (c, c), 0)
    col = lax.broadcasted_iota(jnp.int32, (c, c), 1)
    strict, incl = row > col, row >= col
    cat = lambda p, q: jnp.concatenate([p, q], axis=0)
    cum = _each(_cumsum_rows, lw)
    e_neg = _each(lambda c_: jnp.exp(-c_), cum)
    ar = _each(lambda a_, r_, c_, l_: cat(a_ * jnp.exp(c_ - l_), r_ * jnp.exp(c_)), a, r, cum, lw)
    b_t, k_t = _each(jnp.multiply, b, e_neg), _each(jnp.multiply, k, e_neg)
    p_b, p_k, p_s = _each(NT, ar, b_t), _each(NT, ar, k_t), _each(NT, ar, s0)
    n_ab = _each(lambda p: jnp.where(strict, p[:c], 0.0), p_b)
    m_rb = _each(lambda p: jnp.where(incl, p[c:], 0.0), p_b)
    n_ak = _each(lambda p: jnp.where(strict, p[:c], 0.0), p_k)
    m_rk = _each(lambda p: jnp.where(incl, p[c:], 0.0), p_k)
    if t_inv is None:
        t_inv = _tri_inverse(n_ab)
    u = _tri_solve(n_ab, _each(lambda p, n_, v_: p[:c] + NN(n_, v_), p_s, n_ak, v), t_inv)
    y = _each(lambda p, mb, u_, mk, v_: p[c:] + NN(mb, u_) + NN(mk, v_), p_s, m_rb, u, m_rk, v)
    g_end = _each(lambda l_: jnp.exp(jnp.sum(l_, axis=0, keepdims=True)), lw)
    s1 = _each(lambda s_, g_, u_, v_, b_, k_: s_ * g_ + TN(cat(u_, v_), cat(b_, k_) * g_),
               s0, g_end, u, v, b_t, k_t)
    return y, s1, t_inv


def _adamw(w, g, m, v):
    m = ADAM_B1 * m + (1.0 - ADAM_B1) * g
    v = ADAM_B2 * v + (1.0 - ADAM_B2) * jnp.square(g)
    m_hat = m / (1.0 - ADAM_B1 ** ADAM_STEP)
    v_hat = v / (1.0 - ADAM_B2 ** ADAM_STEP)
    delta = -ADAM_LR * (m_hat / (jnp.sqrt(v_hat) + ADAM_EPS) + ADAM_WD * w)
    return delta, m, v


def token_shift_fwd(p, mu, *, tb, name):
    S, W = p.shape
    hb = tb // 8

    def body(p_ref, halo_ref, mu_ref, o_ref):
        i = pl.program_id(0)
        x = p_ref[...]
        before = halo_ref[7:8, :] * (i > 0).astype(F32)
        row = lax.broadcasted_iota(jnp.int32, (tb, W), 0)
        prev = jnp.where(row == 0, before, pltpu.roll(x, 1, 0))
        o_ref[...] = x + (prev - x) * mu_ref[...]

    blk = (2 * tb + 8) * W * 4
    return pl.pallas_call(
        body, name=name, grid=(S // tb,),
        in_specs=[pl.BlockSpec((tb, W), lambda i: (i, 0)),
                  pl.BlockSpec((8, W), lambda i: (jnp.maximum(i * hb - 1, 0), 0)),
                  pl.BlockSpec((1, W), lambda i: (0, 0))],
        out_specs=pl.BlockSpec((tb, W), lambda i: (i, 0)),
        out_shape=jax.ShapeDtypeStruct((S, W), F32),
        compiler_params=_cparams(("parallel",), blk),
    )(p, p, mu)


def token_shift_bwd(dxs, p, mu, *, tb, name):
    S, W = p.shape
    hb, nb = tb // 8, S // tb

    def body(d_ref, dnext_ref, p_ref, halo_ref, mu_ref, dp_ref, dmu_ref):
        i = pl.program_id(0)
        d, x, mu_v = d_ref[...], p_ref[...], mu_ref[...]
        row = lax.broadcasted_iota(jnp.int32, (tb, W), 0)
        before = halo_ref[7:8, :] * (i > 0).astype(F32)
        prev = jnp.where(row == 0, before, pltpu.roll(x, 1, 0))
        t = d * mu_v
        after = dnext_ref[0:1, :] * mu_v * (i < nb - 1).astype(F32)
        nxt = jnp.where(row == tb - 1, after, pltpu.roll(t, tb - 1, 0))
        dp_ref[...] = (d - t + nxt).astype(dp_ref.dtype)

        @pl.when(i == 0)
        def _():
            dmu_ref[...] = jnp.zeros_like(dmu_ref)

        dmu_ref[...] += jnp.sum(d * (prev - x), axis=0, keepdims=True)

    blk = (3 * tb + 16) * W * 4
    return pl.pallas_call(
        body, name=name, grid=(nb,),
        in_specs=[pl.BlockSpec((tb, W), lambda i: (i, 0)),
                  pl.BlockSpec((8, W), lambda i: (jnp.minimum((i + 1) * hb, S // 8 - 1), 0)),
                  pl.BlockSpec((tb, W), lambda i: (i, 0)),
                  pl.BlockSpec((8, W), lambda i: (jnp.maximum(i * hb - 1, 0), 0)),
                  pl.BlockSpec((1, W), lambda i: (0, 0))],
        out_specs=[pl.BlockSpec((tb, W), lambda i: (i, 0)), pl.BlockSpec((1, W), lambda i: (0, 0))],
        out_shape=[jax.ShapeDtypeStruct((S, W), BF16), jax.ShapeDtypeStruct((1, W), F32)],
        compiler_params=_cparams(("arbitrary",), blk),
    )(dxs, dxs, p, p, mu)


def _head_cols(h):
    return pl.ds(h * HEAD_DIM, HEAD_DIM)


def wkv_fwd(xs_rk, lw, k, a, b):
    S = lw.shape[0]
    C, nc, G, N = WKV_CHUNK, S // WKV_CHUNK, WKV_HEADS_PER_STEP, HEAD_DIM

    def body(r_ref, lw_ref, k_ref, v_ref, a_ref, b_ref, y_ref, st_ref, ti_ref, state):
        @pl.when(pl.program_id(1) == 0)
        def _():
            state[...] = jnp.zeros_like(state)

        heads = lambda ref: tuple(ref[:, _head_cols(h)] for h in range(G))
        s0 = tuple(state[h] for h in range(G))
        y, s1, t_inv = _wkv_chunk(s0, heads(r_ref), heads(lw_ref), heads(k_ref), heads(v_ref), heads(a_ref),
                                  heads(b_ref))
        for h in range(G):
            st_ref[h] = s0[h]
            ti_ref[h] = t_inv[h]
            y_ref[:, _head_cols(h)] = y[h]
            state[h] = s1[h]

    W = G * N
    seq = lambda j: pl.BlockSpec((C, W), functools.partial(lambda j, g, c: (c, j + g), j))
    per = D_MODEL // W
    per_chunk = pl.BlockSpec((None, G, N, N), lambda g, c: (c, g, 0, 0))
    return pl.pallas_call(
        body, name="wkv_fwd", grid=(RWKV_HEADS // G, nc),
        in_specs=[seq(0), seq(0), seq(0), seq(2 * per), seq(0), seq(0)],
        out_specs=[seq(0), per_chunk, per_chunk],
        out_shape=[jax.ShapeDtypeStruct((S, D_MODEL), F32)] + [jax.ShapeDtypeStruct((nc, RWKV_HEADS, N, N), F32)] * 2,
        scratch_shapes=[pltpu.VMEM((G, N, N), F32)],
        compiler_params=_cparams(("parallel", "arbitrary"), 8 * C * W * 4 + 3 * G * N * N * 4),
    )(xs_rk, lw, k, xs_rk, a, b)


def wkv_bwd(xs_rk, lw, k, a, b, states, t_invs, dy):
    S = lw.shape[0]
    C, nc, G, N = WKV_CHUNK, S // WKV_CHUNK, WKV_HEADS_PER_STEP, HEAD_DIM

    def body(r_ref, lw_ref, k_ref, v_ref, a_ref, b_ref, st_ref, ti_ref, dy_ref,
             dr_ref, dlw_ref, dk_ref, dv_ref, da_ref, db_ref, dstate):
        @pl.when(pl.program_id(1) == 0)
        def _():
            dstate[...] = jnp.zeros_like(dstate)

        heads = lambda ref: tuple(ref[:, _head_cols(h)] for h in range(G))
        t_inv = tuple(ti_ref[h] for h in range(G))
        chunk = lambda *args: _wkv_chunk(*args, t_inv)[:2]
        _, pull = jax.vjp(chunk, tuple(st_ref[h] for h in range(G)), heads(r_ref), heads(lw_ref),
                          heads(k_ref), heads(v_ref), heads(a_ref), heads(b_ref))
        ds0, *grads = pull((heads(dy_ref), tuple(dstate[h] for h in range(G))))
        for h in range(G):
            dstate[h] = ds0[h]
            for ref, grad in zip((dr_ref, dlw_ref, dk_ref, dv_ref, da_ref, db_ref), grads):
                ref[:, _head_cols(h)] = grad[h]

    W = G * N
    seq = lambda j: pl.BlockSpec((C, W), functools.partial(lambda j, g, c: (nc - 1 - c, j + g), j))
    per = D_MODEL // W
    st = pl.BlockSpec((None, G, N, N), lambda g, c: (nc - 1 - c, g, 0, 0))
    return pl.pallas_call(
        body, name="wkv_bwd", grid=(RWKV_HEADS // G, nc),
        in_specs=[seq(0), seq(0), seq(0), seq(2 * per), seq(0), seq(0), st, st, seq(0)],
        out_specs=[seq(0)] * 6, out_shape=[jax.ShapeDtypeStruct((S, D_MODEL), F32)] * 6,
        scratch_shapes=[pltpu.VMEM((G, N, N), F32)],
        compiler_params=_cparams(("parallel", "arbitrary"), 14 * C * W * 4 + 3 * G * N * N * 4),
    )(xs_rk, lw, k, xs_rk, a, b, states, t_invs, dy)


def _first_flag(i, seq_len):
    per_group = seq_len // ATTN_BLK
    g = i // per_group
    per_seq = [seq_len // d // ATTN_BLK for _, d in ATTN_PAIRS]
    n = jnp.where(g == 0, per_seq[0], jnp.where(g == 1, per_seq[1], per_seq[2]))
    return (lax.rem(i, n) == 0).astype(F32)


def _block_rows(half):
    return pl.ds(half * ATTN_BLK, ATTN_BLK)


def _block_heads(ref, half):
    return tuple(ref[_block_rows(half), _head_cols(h)] for h in range(ATTN_HPG))


def _pair_heads(ref):
    return _block_heads(ref, 0), _block_heads(ref, 1)


def attn_fwd(q, k, v, seq_len):
    R, N = q.shape
    n_pairs = R // (2 * ATTN_BLK)

    def body(q_ref, k_ref, kb_ref, v_ref, vb_ref, o_ref, lse_ref):
        pair = pl.program_id(0)
        first = (_first_flag(2 * pair, seq_len), _first_flag(2 * pair + 1, seq_len))
        o, lse = _attn_pair(_pair_heads(q_ref), _pair_heads(k_ref), _block_heads(kb_ref, 0), _pair_heads(v_ref),
                            _block_heads(vb_ref, 0), first)
        for half in range(2):
            for h in range(ATTN_HPG):
                o_ref[_block_rows(half), _head_cols(h)] = o[half][h]
                lse_ref[_block_rows(half), _head_cols(h)] = lse[half][h]

    cur = pl.BlockSpec((2 * ATTN_BLK, N), lambda i: (i, 0))
    prv = pl.BlockSpec((ATTN_BLK, N), lambda i: (jnp.maximum(2 * i - 1, 0), 0))
    return pl.pallas_call(
        body, name="attn_fwd", grid=(n_pairs,), in_specs=[cur, cur, prv, cur, prv],
        out_specs=[cur, cur], out_shape=[jax.ShapeDtypeStruct((R, N), F32)] * 2,
        compiler_params=_cparams(("parallel",), 12 * ATTN_BLK * N * 4),
    )(q, k, k, v, v)


def attn_bwd(q, k, v, do, dlse, seq_len):
    R, N = q.shape
    n_pairs = R // (2 * ATTN_BLK)

    def body(q_ref, k_ref, kb_ref, v_ref, vb_ref, do_ref, dl_ref, dq_ref, dk_ref, dv_ref, carry_k, carry_v):
        step = pl.program_id(0)
        pair = n_pairs - 1 - step
        first = (_first_flag(2 * pair, seq_len), _first_flag(2 * pair + 1, seq_len))

        @pl.when(step == 0)
        def _():
            carry_k[...] = jnp.zeros_like(carry_k)
            carry_v[...] = jnp.zeros_like(carry_v)

        _, pull = jax.vjp(functools.partial(_attn_pair, first=first), _pair_heads(q_ref), _pair_heads(k_ref),
                          _block_heads(kb_ref, 0), _pair_heads(v_ref), _block_heads(vb_ref, 0))
        dq, dk, dk_before, dv, dv_before = pull((_pair_heads(do_ref), _pair_heads(dl_ref)))
        old_k, old_v = _block_heads(carry_k, 0), _block_heads(carry_v, 0)
        for h in range(ATTN_HPG):
            cols = _head_cols(h)
            for half in range(2):
                dq_ref[_block_rows(half), cols] = dq[half][h]
            dk_ref[_block_rows(0), cols] = dk[0][h]
            dv_ref[_block_rows(0), cols] = dv[0][h]
            dk_ref[_block_rows(1), cols] = dk[1][h] + old_k[h]
            dv_ref[_block_rows(1), cols] = dv[1][h] + old_v[h]
            carry_k[:, cols] = dk_before[h]
            carry_v[:, cols] = dv_before[h]

    cur = pl.BlockSpec((2 * ATTN_BLK, N), lambda i: (n_pairs - 1 - i, 0))
    prv = pl.BlockSpec((ATTN_BLK, N), lambda i: (jnp.maximum(2 * (n_pairs - 1 - i) - 1, 0), 0))
    return pl.pallas_call(
        body, name="attn_bwd", grid=(n_pairs,), in_specs=[cur, cur, prv, cur, prv, cur, cur],
        out_specs=[cur, cur, cur], out_shape=[jax.ShapeDtypeStruct((R, N), F32)] * 3,
        scratch_shapes=[pltpu.VMEM((ATTN_BLK, N), F32)] * 2,
        compiler_params=_cparams(("arbitrary",), 22 * ATTN_BLK * N * 4),
    )(q, k, k, v, v, do, dlse)


def to_subsequences(t):
    S = t.shape[0]
    parts = []
    for gi, (_, d) in enumerate(ATTN_PAIRS):
        tg = t[:, GROUP_W * gi:GROUP_W * (gi + 1)].reshape(S // d, d, GROUP_W)
        parts.append(tg.transpose(1, 0, 2).reshape(S, GROUP_W))
    return jnp.concatenate(parts, axis=0)


def from_subsequences(u, S):
    parts = []
    for gi, (_, d) in enumerate(ATTN_PAIRS):
        ug = u[S * gi:S * (gi + 1)].reshape(d, S // d, GROUP_W)
        parts.append(ug.transpose(1, 0, 2).reshape(S, GROUP_W))
    return jnp.concatenate(parts, axis=1)


def _ffn_fwd(x, norm, w_in, w_out, tag):
    h = rowmap(_rms, [x], [norm], [(D_MODEL, BF16)], tb=512, name=tag + "_norm")[0]
    gu, act = ffn_in_act(h, w_in, tag + "_in")
    y = matmul(act, w_out, "nn", tag + "_out", add=x, scale=0.5)
    return y, (x, h, gu, act)


def _ffn_bwd(dy, saved, norm, w_in, w_out, tag):
    x, h, gu, act = saved
    dw_out = matmul(act, dy, "tn", tag + "_dwout", scale=0.5)
    dgu = ffn_dact_dgu(dy, w_out, gu, 0.5, tag + "_dgu")
    dh = matmul_cs(dgu, w_in, "nt", tag + "_dh")
    dw_in = matmul_cs(h, dgu, "tn", tag + "_dwin")

    def norm_bwd(x_b, dh_b, dy_b, g):
        dx, dg = jax.vjp(_rms, x_b, g)[1](dh_b)
        return dy_b + dx, dg

    dx, dnorm = rowmap(norm_bwd, [x, dh, dy], [norm], [(D_MODEL, F32)], [(1, D_MODEL)], tb=256,
                       name=tag + "_dnorm")
    return dx, dnorm, dw_in, dw_out


def layer_step(x, tgt, W, P):
    S = x.shape[0]
    head_of = lambda n: jnp.arange(n)[:, None] // HEAD_DIM == jnp.arange(n // HEAD_DIM)[None, :]
    seg, seg_a = head_of(D_MODEL).astype(BF16), head_of(ATTN_WIDTH).astype(BF16)
    seg_t, seg_a_t = seg.T, seg_a.T
    tile_t = (jnp.arange(HEAD_DIM)[:, None] == jnp.arange(ATTN_WIDTH)[None, :] % HEAD_DIM).astype(BF16)
    qk_params = [P["attn_q_norm"], P["attn_k_norm"], seg_a, seg_a_t, tile_t]
    w_rkv, w_lora = W["w_in"][:, :RKV], W["w_in"][:, RKV:RKV + LORA]
    w_qkv = W["w_in"][:, RKV + LORA:RKV + LORA + 3 * ATTN_WIDTH]
    w_gate = W["w_in"][:, RKV + LORA + 3 * ATTN_WIDTH:]
    mu_rk, mu_lo = P["rwkv_mu"][:, :RKV], P["rwkv_mu"][:, RKV:]
    zeros = lambda n: jnp.zeros((n, D_MODEL), F32)
    w2p = jnp.concatenate([W["rwkv_w2"], zeros(LORA - LORA_W)], axis=0)
    a2p = jnp.concatenate([zeros(LORA_W), W["rwkv_a2"], zeros(LORA_G)], axis=0)
    g2p = jnp.concatenate([zeros(LORA_W + LORA_A), W["rwkv_g2"]], axis=0)
    pre_params = [P["rwkv_w0"], w2p, P["rwkv_a0"], a2p, g2p, P["rwkv_k_k"], P["rwkv_k_a"], seg, seg_t]
    post_params = [P["rwkv_r_k"], P["rwkv_ln_w"], P["rwkv_ln_b"], seg, seg_t]
    col = lambda arr, j: (arr, D_MODEL, j)

    x1, ffn1_saved = _ffn_fwd(x, P["ffn1_norm"], W["ffn1_w_in"], W["ffn1_w_out"], "ffn1")
    h = rowmap(_rms, [x1], [P["mix_norm"]], [(D_MODEL, BF16)], tb=512, name="mix_norm")[0]
    p_rk = matmul(h, w_rkv, "nn", "proj_rkv")
    p_lo = matmul(h, w_lora, "nn", "proj_lora")
    p_qkv = matmul(h, w_qkv, "nn", "proj_qkv")
    p_gate = matmul(h, w_gate, "nn", "proj_gate")
    xs_rk = token_shift_fwd(p_rk, mu_rk, tb=256, name="shift_rk")
    xs_lo = token_shift_fwd(p_lo, mu_lo, tb=256, name="shift_lora")
    lw, k_mod, a_neg, b_kk, g = rowmap(
        _rwkv_pre, [xs_rk, xs_lo], pre_params, [(D_MODEL, F32)] * 5, tb=256, name="rwkv_pre")
    wkv, states, t_invs = wkv_fwd(xs_rk, lw, k_mod, a_neg, b_kk)
    post_rows = [wkv, col(xs_rk, 0), k_mod, col(xs_rk, 2), g]
    y_a = rowmap(_rwkv_post, post_rows, post_params, [(D_MODEL, BF16)], tb=256, name="rwkv_post")[0]

    qk_rows = [(p_qkv, ATTN_WIDTH, 0), (p_qkv, ATTN_WIDTH, 1)]
    qn, kn = rowmap(_qk_norm, qk_rows, qk_params, [(ATTN_WIDTH, F32)] * 2, tb=256, name="qk_norm")
    q_s, k_s, v_s = to_subsequences(qn), to_subsequences(kn), to_subsequences(p_qkv[:, 2 * ATTN_WIDTH:])
    o_s, lse_s = attn_fwd(q_s, k_s, v_s, S)
    o, lse = from_subsequences(o_s, S), from_subsequences(lse_s, S)
    y_b = rowmap(_group_combine, [o, lse], [], [(ATTN_WIDTH, BF16)], tb=512, name="attn_combine")[0]

    pa = matmul(y_a, W["w_proj_rwkv"], "nn", "proj_a")
    pb = matmul(y_b, W["w_proj_attn"], "nn", "proj_b")
    merged = rowmap(_gate_merge, [p_gate, pa, pb], [P["b_gate"]], [(D_MODEL, BF16)], tb=256, name="merge")[0]
    x2 = matmul(merged, W["w_out"], "nn", "mix_out", add=x1)
    x3, ffn2_saved = _ffn_fwd(x2, P["ffn2_norm"], W["ffn2_w_in"], W["ffn2_w_out"], "ffn2")

    def loss_head(y_b_, t_b):
        err = y_b_ - t_b
        return err * (1.0 / D_MODEL), (0.5 / D_MODEL) * jnp.sum(err * err, axis=0, keepdims=True)

    dx3, loss_cols = rowmap(loss_head, [x3, tgt], [], [(D_MODEL, F32)], [(1, D_MODEL)], tb=512, name="loss")

    gW, gP = {}, {}
    dx2, gP["ffn2_norm"], gW["ffn2_w_in"], gW["ffn2_w_out"] = _ffn_bwd(
        dx3, ffn2_saved, P["ffn2_norm"], W["ffn2_w_in"], W["ffn2_w_out"], "ffn2")

    dmerged = matmul(dx2, W["w_out"], "nt", "d_merged")
    gW["w_out"] = matmul(merged, dx2, "tn", "dw_out")

    def merge_bwd(pg, pa_b, pb_b, dm, bg):
        return jax.vjp(_gate_merge, pg, pa_b, pb_b, bg)[1](dm)

    dp_gate, dpa, dpb, gP["b_gate"] = rowmap(
        merge_bwd, [p_gate, pa, pb, dmerged], [P["b_gate"]],
        [(2 * D_MODEL, BF16), (D_MODEL, BF16), (D_MODEL, BF16)], [(1, 2 * D_MODEL)], tb=256, name="merge_bwd")
    dy_a = matmul(dpa, W["w_proj_rwkv"], "nt", "d_ya")
    gW["w_proj_rwkv"] = matmul(y_a, dpa, "tn", "dw_proj_a")
    dy_b = matmul(dpb, W["w_proj_attn"], "nt", "d_yb")
    gW["w_proj_attn"] = matmul(y_b, dpb, "tn", "dw_proj_b")

    def combine_bwd(o_b, l_b, d_b):
        return jax.vjp(_group_combine, o_b, l_b)[1](d_b)

    do, dlse = rowmap(combine_bwd, [o, lse, dy_b], [], [(ATTN_WIDTH, F32)] * 2, tb=256, name="attn_combine_bwd")
    dq_s, dk_s, dv_s = attn_bwd(q_s, k_s, v_s, to_subsequences(do), to_subsequences(dlse), S)

    def qk_norm_bwd(q_b, k_b, dqn_b, dkn_b, dv_b, qg, kg, sg, sgt, tl):
        f = lambda *a: _qk_norm(*a, sg, sgt, tl)
        dq, dk, dqg, dkg = jax.vjp(f, q_b, k_b, qg, kg)[1]((dqn_b, dkn_b))
        return jnp.concatenate([dq, dk, dv_b], axis=1), dqg, dkg

    dp_qkv, gP["attn_q_norm"], gP["attn_k_norm"] = rowmap(
        qk_norm_bwd, qk_rows + [from_subsequences(t, S) for t in (dq_s, dk_s, dv_s)], qk_params,
        [(3 * ATTN_WIDTH, BF16)], [(1, HEAD_DIM)] * 2, tb=256, name="qk_norm_bwd")

    def post_bwd(wkv_b, r_b, k_b, v_b, g_b, d_b, r_k, ln_w, ln_b, sg, sgt):
        f = lambda *a: _rwkv_post(*a, sg, sgt)
        return jax.vjp(f, wkv_b, r_b, k_b, v_b, g_b, r_k, ln_w, ln_b)[1](d_b)

    dwkv, dr_p, dk_p, dv_p, dg, gP["rwkv_r_k"], gP["rwkv_ln_w"], gP["rwkv_ln_b"] = rowmap(
        post_bwd, post_rows + [dy_a], post_params, [(D_MODEL, F32)] * 5, [(1, D_MODEL)] * 3, tb=128,
        name="rwkv_post_bwd")
    dr_w, dlw, dk_w, dv_w, da_neg, db_kk = wkv_bwd(xs_rk, lw, k_mod, a_neg, b_kk, states, t_invs, dwkv)

    def pre_bwd(xrk_b, xlo_b, dlw_b, dkw_b, dkp_b, da_b, db_b, dg_b, drp_b, drw_b, dvp_b, dvw_b,
                w0, w2, a0, a2, g2, k_k, k_a, sg, sgt):
        f = lambda *a: _rwkv_pre(*a, sg, sgt)
        pull = jax.vjp(f, xrk_b, xlo_b, w0, w2, a0, a2, g2, k_k, k_a)[1]
        dxrk, dxlo, *dpar = pull((dlw_b, dkw_b + dkp_b, da_b, db_b, dg_b))
        direct = jnp.concatenate([drp_b + drw_b, jnp.zeros_like(drp_b), dvp_b + dvw_b], axis=1)
        return (dxrk + direct, dxlo, *dpar)

    pre_rows = [xs_rk, xs_lo, dlw, dk_w, dk_p, da_neg, db_kk, dg, dr_p, dr_w, dv_p, dv_w]
    dxs_rk, dxs_lo, gP["rwkv_w0"], dw2p, gP["rwkv_a0"], da2p, dg2p, gP["rwkv_k_k"], gP["rwkv_k_a"] = rowmap(
        pre_bwd, pre_rows, pre_params, [(RKV, F32), (LORA, F32)],
        [(1, D_MODEL), (LORA, D_MODEL), (1, D_MODEL), (LORA, D_MODEL), (LORA, D_MODEL), (1, D_MODEL), (1, D_MODEL)],
        tb=128, name="rwkv_pre_bwd")
    gW["rwkv_w2"] = dw2p[:LORA_W]
    gW["rwkv_a2"] = da2p[LORA_W:LORA_W + LORA_A]
    gW["rwkv_g2"] = dg2p[LORA_W + LORA_A:]
    dp_rk, dmu_rk = token_shift_bwd(dxs_rk, p_rk, mu_rk, tb=256, name="shift_rk_bwd")
    dp_lo, dmu_lo = token_shift_bwd(dxs_lo, p_lo, mu_lo, tb=256, name="shift_lora_bwd")
    gP["rwkv_mu"] = jnp.concatenate([dmu_rk, dmu_lo], axis=1)

    dh = matmul(dp_rk, w_rkv, "nt", "dh_rkv")
    dh = matmul(dp_lo, w_lora, "nt", "dh_lora", add=dh)
    dh = matmul(dp_qkv, w_qkv, "nt", "dh_qkv", add=dh)
    dh = matmul(dp_gate, w_gate, "nt", "dh_gate", add=dh)
    gW["w_in"] = jnp.concatenate([
        matmul(h, dp_rk, "tn", "dw_rkv"), matmul(h, dp_lo, "tn", "dw_lora"),
        matmul(h, dp_qkv, "tn", "dw_qkv"), matmul(h, dp_gate, "tn", "dw_gate")], axis=1)

    def norm_bwd(x_b, dh_b, dy_b, gn):
        dx, dgn = jax.vjp(_rms, x_b, gn)[1](dh_b)
        return dy_b + dx, dgn

    dx1, gP["mix_norm"] = rowmap(norm_bwd, [x1, dh, dx2], [P["mix_norm"]], [(D_MODEL, F32)], [(1, D_MODEL)],
                                 tb=256, name="mix_norm_bwd")
    dx, gP["ffn1_norm"], gW["ffn1_w_in"], gW["ffn1_w_out"] = _ffn_bwd(
        dx1, ffn1_saved, P["ffn1_norm"], W["ffn1_w_in"], W["ffn1_w_out"], "ffn1")
    return loss_cols, dx, gW, gP


N_SHARDS = 4
BIG = (("ffn1_w_in", (D_MODEL, 2 * D_FF), 1), ("ffn1_w_out", (D_FF, D_MODEL), 0),
       ("w_in", (D_MODEL, 7712), 1), ("rwkv_w2", (LORA_W, D_MODEL), 1), ("rwkv_a2", (LORA_A, D_MODEL), 1),
       ("rwkv_g2", (LORA_G, D_MODEL), 1), ("w_proj_rwkv", (D_MODEL, D_MODEL), 0),
       ("w_proj_attn", (ATTN_WIDTH, D_MODEL), 1), ("w_out", (D_MODEL, D_MODEL), 0),
       ("ffn2_w_in", (D_MODEL, 2 * D_FF), 1), ("ffn2_w_out", (D_FF, D_MODEL), 0))
SMALL = (("ffn1_norm", 1024), ("mix_norm", 1024), ("b_gate", 2048), ("rwkv_mu", 3360), ("rwkv_w0", 1024),
         ("rwkv_a0", 1024), ("rwkv_k_k", 1024), ("rwkv_k_a", 1024), ("rwkv_r_k", 1024), ("rwkv_ln_w", 1024),
         ("rwkv_ln_b", 1024), ("attn_q_norm", 64), ("attn_k_norm", 64), ("ffn2_norm", 1024))
WEIGHT_ORDER = ("ffn1_norm", "ffn1_w_in", "ffn1_w_out", "mix_norm", "w_in", "b_gate", "rwkv_mu", "rwkv_w0",
                "rwkv_w2", "rwkv_a0", "rwkv_a2", "rwkv_g2", "rwkv_k_k", "rwkv_k_a", "rwkv_r_k", "rwkv_ln_w",
                "rwkv_ln_b", "attn_q_norm", "attn_k_norm", "w_proj_rwkv", "w_proj_attn", "w_out", "ffn2_norm",
                "ffn2_w_in", "ffn2_w_out")


LORA_PARTS = ("rwkv_w2", "rwkv_a2", "rwkv_g2")
BLOCK_MAJOR = ("ffn1_w_in", "ffn2_w_in")
SMALL_USED = D_MODEL + sum(n for _, n in SMALL)
SMALL_W = -(-SMALL_USED // 128) * 128


def _travel():
    out = {}
    for name, shape, axis in BIG:
        if name == LORA_PARTS[0]:
            out["lora"] = ((LORA, D_MODEL), 1)
        elif name not in LORA_PARTS:
            out[name] = (shape, axis)
    return out


def local_blocks(vals):
    out = {n: vals[n] for n in _travel() if n != "lora"}
    out["lora"] = jnp.concatenate([vals[n] for n in LORA_PARTS], axis=0)
    return out


def split_lora(t):
    return {"rwkv_w2": t[:LORA_W], "rwkv_a2": t[LORA_W:LORA_W + LORA_A], "rwkv_g2": t[LORA_W + LORA_A:]}


def blocks_to_full(name, blocks):
    shape, axis = _travel()[name]
    if name in BLOCK_MAJOR:
        return blocks
    if axis == 0:
        return blocks.reshape(shape)
    return blocks.transpose(1, 0, 2).reshape(shape)


def full_to_blocks(name, full):
    shape, axis = _travel()[name]
    if name in BLOCK_MAJOR:
        return full
    if axis == 0:
        return full.reshape(N_SHARDS, shape[0] // N_SHARDS, shape[1])
    return full.reshape(shape[0], N_SHARDS, shape[1] // N_SHARDS).transpose(1, 0, 2)


def pack_small(vals, head):
    parts = [head] + [vals[name].reshape(1, n) for name, n in SMALL]
    parts.append(jnp.zeros((1, SMALL_W - SMALL_USED), F32))
    return jnp.concatenate(parts, axis=1)


def unpack_small(vec, shapes):
    out, off = {}, D_MODEL
    for name, n in SMALL:
        out[name] = vec[:, off:off + n].reshape(shapes[name])
        off += n
    return out


def _place():
    return lax.axis_index("x"), lax.axis_index("y"), lax.axis_index("c")


def _other_chips(x, y):
    return [(1 - x, y), (x, 1 - y), (1 - x, 1 - y)]


def _remote(src, dst, send_sem, recv_sem, device):
    return pltpu.make_async_remote_copy(src_ref=src, dst_ref=dst, send_sem=send_sem, recv_sem=recv_sem,
                                        device_id=device, device_id_type=MESH)


def _half(ref, who):
    hr = ref.shape[-2] // 2
    rows = pl.ds(pl.multiple_of(who * hr, 8), hr)
    return ref.at[rows] if len(ref.shape) == 2 else ref.at[:, rows]


HBM_REF = pl.BlockSpec(memory_space=pl.ANY)
COMM_PARAMS = dict(compiler_params=pltpu.CompilerParams(has_side_effects=True))


def gather_weights(blocks):
    n = len(blocks)

    def body(*refs):
        ins, outs = refs[:n], refs[n:2 * n]
        ici_send, ici_recv, d2d_send, d2d_recv = refs[2 * n:]
        x, y, c = _place()
        me, sibling, chips = 2 * x + y, (x, y, 1 - c), _other_chips(x, y)
        first = [_remote(_half(ins[t], c), _half(outs[t].at[me], c), ici_send.at[k, t], ici_recv.at[k, t],
                         (px, py, c)) for k, (px, py) in enumerate(chips) for t in range(n)]
        for cp in first:
            cp.start()
        passed = []
        for k, (px, py) in enumerate(chips):
            for t in range(n):
                landed = _half(outs[t].at[2 * px + py], c)
                _remote(landed, landed, ici_send.at[k, t], ici_recv.at[k, t], (px, py, c)).wait_recv()
                cp = _remote(landed, landed, d2d_send.at[k, t], d2d_recv.at[k, t], sibling)
                cp.start()
                passed.append(cp)
        for k, (px, py) in enumerate(chips):
            for t in range(n):
                other = _half(outs[t].at[2 * px + py], 1 - c)
                _remote(other, other, d2d_send.at[k, t], d2d_recv.at[k, t], sibling).wait_recv()
        for cp in first + passed:
            cp.wait_send()

    res = pl.pallas_call(
        body, name="gather_weights", in_specs=[HBM_REF] * n, out_specs=[HBM_REF] * n,
        out_shape=[jax.ShapeDtypeStruct((N_SHARDS,) + b.shape, b.dtype) for b in blocks],
        scratch_shapes=[pltpu.SemaphoreType.DMA((3, n))] * 4, **COMM_PARAMS)(*blocks)
    me = 2 * lax.axis_index("x") + lax.axis_index("y")
    return [lax.dynamic_update_slice(g, b[None], (me, 0, 0)) for g, b in zip(res, blocks)]


def swap_halves(grads):
    n = len(grads)

    def body(*refs):
        ins, got = refs[:n], refs[n:2 * n]
        send_sems, recv_sems = refs[2 * n:]
        x, y, c = _place()
        give = [_remote(_half(ins[t], 1 - c), got[t], send_sems.at[t], recv_sems.at[t], (x, y, 1 - c))
                for t in range(n)]
        for cp in give:
            cp.start()
        for cp in give:
            cp.wait_recv()
        for cp in give:
            cp.wait_send()

    return pl.pallas_call(
        body, name="swap_halves", in_specs=[HBM_REF] * n, out_specs=[HBM_REF] * n,
        out_shape=[jax.ShapeDtypeStruct((g.shape[0], g.shape[1] // 2, g.shape[2]), g.dtype) for g in grads],
        scratch_shapes=[pltpu.SemaphoreType.DMA((n,))] * 2, **COMM_PARAMS)(*grads)


def scatter_partials(partials):
    n = len(partials)

    def body(*refs):
        parts, landed = refs[:n], refs[n:2 * n]
        send_sems, recv_sems = refs[2 * n:]
        x, y, c = _place()
        sends = [_remote(parts[t].at[2 * px + py], landed[t].at[k], send_sems.at[k, t], recv_sems.at[k, t],
                         (px, py, c)) for k, (px, py) in enumerate(_other_chips(x, y)) for t in range(n)]
        for cp in sends:
            cp.start()
        for cp in sends:
            cp.wait_recv()
        for cp in sends:
            cp.wait_send()

    return pl.pallas_call(
        body, name="scatter_partials", in_specs=[HBM_REF] * n, out_specs=[HBM_REF] * n,
        out_shape=[jax.ShapeDtypeStruct((3,) + p.shape[1:], p.dtype) for p in partials],
        scratch_shapes=[pltpu.SemaphoreType.DMA((3, n))] * 2, **COMM_PARAMS)(*partials)


def join_halves(blocks):
    n = len(blocks)

    def body(*refs):
        outs = refs[n:2 * n]
        send_sems, recv_sems = refs[2 * n:]
        x, y, c = _place()
        give = [_remote(_half(outs[t], c), _half(outs[t], c), send_sems.at[t], recv_sems.at[t], (x, y, 1 - c))
                for t in range(n)]
        for cp in give:
            cp.start()
        for t in range(n):
            arriving = _half(outs[t], 1 - c)
            _remote(arriving, arriving, send_sems.at[t], recv_sems.at[t], (x, y, 1 - c)).wait_recv()
        for cp in give:
            cp.wait_send()

    return pl.pallas_call(
        body, name="join_halves", in_specs=[HBM_REF] * n, out_specs=[HBM_REF] * n,
        out_shape=[jax.ShapeDtypeStruct(b.shape, b.dtype) for b in blocks],
        input_output_aliases={t: t for t in range(n)},
        scratch_shapes=[pltpu.SemaphoreType.DMA((n,))] * 2, **COMM_PARAMS)(*blocks)


def reduce_block_grads(grads):
    names = list(grads)
    got = swap_halves([grads[n] for n in names])
    partials = []
    for name, theirs in zip(names, got):
        n_slot, hr, width = theirs.shape
        tb = _row_block(hr, width, 6)
        per_half = hr // tb
        mine = lambda i, s, per_half=per_half: (i // per_half) * 2 * per_half + s[0] * per_half + i % per_half
        p = placed_map(
            jnp.add,
            [(grads[name].reshape(2 * n_slot * hr, width), mine), (theirs.reshape(n_slot * hr, width), lambda i, s: i)],
            (n_slot * hr, width, BF16, lambda i, s: i), n_blocks=n_slot * per_half, tb=tb, name="chip_sum_" + name)
        partials.append(p.reshape(theirs.shape))
    landed = scatter_partials(partials)
    blocks = []
    for name, theirs, arrived in zip(names, got, landed):
        n_slot, hr, width = theirs.shape
        tb = _row_block(hr, width, 6)
        per_half = hr // tb
        views = [(grads[name].reshape(2 * n_slot * hr, width),
                  lambda i, s, per_half=per_half: s[1] * 2 * per_half + s[0] * per_half + i),
                 (theirs.reshape(n_slot * hr, width), lambda i, s, per_half=per_half: s[1] * per_half + i)]
        views += [(arrived.reshape(3 * hr, width), functools.partial(lambda k, per_half, i, s: k * per_half + i,
                                                                     k, per_half)) for k in range(3)]
        f = lambda a, b, l0, l1, l2: (((a + b) + l0.astype(F32)) + l1.astype(F32)) + l2.astype(F32)
        blocks.append(placed_map(
            f, views,(2 * hr, width, F32, lambda i, s, per_half=per_half: s[0] * per_half + i),
            n_blocks=per_half, tb=tb, name="owner_sum_" + name))
    return dict(zip(names, join_halves(blocks)))


def adamw_block(name, w, g, m, v):
    rows, width = w.shape
    return rowmap(_adamw, [w, g, m, v], [], [(width, F32)] * 3, tb=_row_block(rows, width, 7),
                  name="adamw_" + name)


def reduce_small(vec, w, m, v):
    n_dev = 8

    def body(vec_ref, w_ref, m_ref, v_ref, loss_ref, g_ref, d_ref, m2_ref, v2_ref, slots, send_sems, recv_sems):
        x, y, c = _place()
        me = 4 * x + 2 * y + c
        slots[me] = vec_ref[...]
        flips = [(fx, fy, fc) for fx in (0, 1) for fy in (0, 1) for fc in (0, 1)][1:]
        peers = [(1 - x if fx else x, 1 - y if fy else y, 1 - c if fc else c) for fx, fy, fc in flips]
        sends = [pltpu.make_async_remote_copy(
            src_ref=vec_ref, dst_ref=slots.at[me], send_sem=send_sems.at[j], recv_sem=recv_sems.at[j],
            device_id=peer, device_id_type=MESH) for j, peer in enumerate(peers)]
        for cp in sends:
            cp.start()
        for j, (px, py, pc) in enumerate(peers):
            pltpu.make_async_remote_copy(
                src_ref=vec_ref, dst_ref=slots.at[4 * px + 2 * py + pc], send_sem=send_sems.at[j],
                recv_sem=recv_sems.at[j], device_id=(px, py, pc), device_id_type=MESH).wait_recv()
        for cp in sends:
            cp.wait_send()
        g = slots[0]
        for d in range(1, n_dev):
            g = g + slots[d]
        loss_ref[...] = jnp.sum(g[:, :D_MODEL], axis=1, keepdims=True)
        delta, m2, v2 = _adamw(w_ref[...], g, m_ref[...], v_ref[...])
        g_ref[...], d_ref[...], m2_ref[...], v2_ref[...] = g, delta, m2, v2

    vm = pl.BlockSpec(memory_space=pltpu.VMEM)
    vec_t = jax.ShapeDtypeStruct(vec.shape, F32)
    return pl.pallas_call(
        body, name="reduce_small", in_specs=[vm] * 4, out_specs=[vm] * 5,
        out_shape=[jax.ShapeDtypeStruct((1, 1), F32)] + [vec_t] * 4,
        scratch_shapes=[pltpu.VMEM((n_dev,) + vec.shape, F32), pltpu.SemaphoreType.DMA((n_dev - 1,)),
                        pltpu.SemaphoreType.DMA((n_dev - 1,))],
        compiler_params=pltpu.CompilerParams(has_side_effects=True),
    )(vec, w, m, v)


def kernel(x, ffn1_norm, ffn1_w_in, ffn1_w_out, mix_norm, w_in, b_gate, rwkv_mu, rwkv_w0, rwkv_w2, rwkv_a0, rwkv_a2, rwkv_g2, rwkv_k_k, rwkv_k_a, rwkv_r_k, rwkv_ln_w, rwkv_ln_b, attn_q_norm, attn_k_norm, w_proj_rwkv, w_proj_attn, w_out, ffn2_norm, ffn2_w_in, ffn2_w_out, loss_target, m_ffn1_norm, m_ffn1_w_in, m_ffn1_w_out, m_mix_norm, m_w_in, m_b_gate, m_rwkv_mu, m_rwkv_w0, m_rwkv_w2, m_rwkv_a0, m_rwkv_a2, m_rwkv_g2, m_rwkv_k_k, m_rwkv_k_a, m_rwkv_r_k, m_rwkv_ln_w, m_rwkv_ln_b, m_attn_q_norm, m_attn_k_norm, m_w_proj_rwkv, m_w_proj_attn, m_w_out, m_ffn2_norm, m_ffn2_w_in, m_ffn2_w_out, v_ffn1_norm, v_ffn1_w_in, v_ffn1_w_out, v_mix_norm, v_w_in, v_b_gate, v_rwkv_mu, v_rwkv_w0, v_rwkv_w2, v_rwkv_a0, v_rwkv_a2, v_rwkv_g2, v_rwkv_k_k, v_rwkv_k_a, v_rwkv_r_k, v_rwkv_ln_w, v_rwkv_ln_b, v_attn_q_norm, v_attn_k_norm, v_w_proj_rwkv, v_w_proj_attn, v_w_out, v_ffn2_norm, v_ffn2_w_in, v_ffn2_w_out):
    given = dict(locals())
    weights = {n: given[n] for n in WEIGHT_ORDER}
    mom_m = {n: given["m_" + n] for n in WEIGHT_ORDER}
    mom_v = {n: given["v_" + n] for n in WEIGHT_ORDER}
    big = [name for name, _, _ in BIG]
    shapes = {n: weights[n].shape for n in WEIGHT_ORDER}
    blocks_of = lambda d: local_blocks({n: d[n][0] for n in big})
    w_blk, m_blk, v_blk = blocks_of(weights), blocks_of(mom_m), blocks_of(mom_v)
    names = list(w_blk)

    gathered = gather_weights([w_blk[n].astype(BF16) for n in names])
    W = {n: blocks_to_full(n, g) for n, g in zip(names, gathered)}
    W.update(split_lora(W.pop("lora")))
    P = {n: weights[n].reshape(1, -1) for n, _ in SMALL}

    loss_cols, dx, gW, gP = layer_step(x[0], loss_target[0], W, P)

    gW["lora"] = jnp.concatenate([gW.pop(n) for n in LORA_PARTS], axis=0)
    g_blk = reduce_block_grads({n: full_to_blocks(n, gW[n]) for n in names})
    out_g, out_d, out_m, out_v = {}, {}, {}, {}
    for n in names:
        res = (g_blk[n], *adamw_block(n, w_blk[n], g_blk[n], m_blk[n], v_blk[n]))
        for dst, t in zip((out_g, out_d, out_m, out_v), res):
            for part, val in (split_lora(t) if n == "lora" else {n: t}).items():
                dst[part] = val.reshape(shapes[part])

    zero_head = jnp.zeros((1, D_MODEL), F32)
    vec = pack_small(gP, loss_cols)
    loss, g_s, d_s, m_s, v_s = reduce_small(
        vec, pack_small({n: weights[n] for n, _ in SMALL}, zero_head),
        pack_small({n: mom_m[n] for n, _ in SMALL}, zero_head),
        pack_small({n: mom_v[n] for n, _ in SMALL}, zero_head))
    for dst, src in ((out_g, g_s), (out_d, d_s), (out_m, m_s), (out_v, v_s)):
        dst.update(unpack_small(src, shapes))

    return (loss[0, 0], dx[None], *[out_g[n] for n in WEIGHT_ORDER], *[out_d[n] for n in WEIGHT_ORDER],
            *[out_m[n] for n in WEIGHT_ORDER], *[out_v[n] for n in WEIGHT_ORDER])
```

```python
import functools

import jax
import jax.numpy as jnp
from jax import lax
from jax.experimental import pallas as pl
from jax.experimental.pallas import tpu as pltpu

F32 = jnp.float32
BF16 = jnp.bfloat16
MESH = pl.DeviceIdType.MESH

D_MODEL = 1024
HEAD_DIM = 64
RWKV_HEADS = 16
LORA_W, LORA_A, LORA_G = 64, 64, 160
LORA = LORA_W + LORA_A + LORA_G
RKV = 3 * D_MODEL
ATTN_PAIRS = ((128, 1), (512, 4), (2048, 16))
ATTN_BLK = 128
ATTN_HPG = 4
ATTN_WIDTH = 768
GROUP_W = ATTN_HPG * HEAD_DIM
D_FF = 2816
GN_EPS = 64e-5
RMS_EPS = 1e-6
NEG_INF = -1e30
WKV_CHUNK = 64
WKV_HEADS_PER_STEP = 16

ADAM_LR, ADAM_B1, ADAM_B2, ADAM_EPS, ADAM_WD, ADAM_STEP = 0.001, 0.9, 0.999, 1e-08, 0.01, 10

V7X_VMEM_BYTES = 64 << 20
VMEM_TEMP_ALLOWANCE = 20 << 20


def _cparams(sem, block_bytes):
    limit = min(2 * block_bytes + VMEM_TEMP_ALLOWANCE, V7X_VMEM_BYTES - (6 << 20))
    return pltpu.CompilerParams(dimension_semantics=sem, vmem_limit_bytes=int(limit))


def _nbytes(shape, dtype):
    n = 1
    for s in shape:
        n *= s
    return n * jnp.dtype(dtype).itemsize


def _split_bf16(a):
    hi = a.astype(BF16)
    return hi, (a - hi.astype(F32)).astype(BF16)


def _make_dots():
    def raw(a, b, ca, cb):
        return lax.dot_general(a.astype(BF16), b.astype(BF16), (((ca,), (cb,)), ((), ())),
                               preferred_element_type=F32)

    @jax.custom_vjp
    def nn(a, b):
        return raw(a, b, 1, 0)

    @jax.custom_vjp
    def nt(a, b):
        return raw(a, b, 1, 1)

    @jax.custom_vjp
    def tn(a, b):
        return raw(a, b, 0, 0)

    nn.defvjp(lambda a, b: (raw(a, b, 1, 0), (a, b)),
              lambda res, g: (raw(g, res[1], 1, 1), raw(res[0], g, 0, 0)))
    nt.defvjp(lambda a, b: (raw(a, b, 1, 1), (a, b)),
              lambda res, g: (raw(g, res[1], 1, 0), raw(g, res[0], 0, 0)))
    tn.defvjp(lambda a, b: (raw(a, b, 0, 0), (a, b)),
              lambda res, g: (raw(res[1], g, 1, 1), raw(res[0], g, 1, 0)))
    return nn, nt, tn


def _exact_rhs_dot(x, ones, cx, co):
    hi, lo = _split_bf16(x)
    dims = (((cx,), (co,)), ((), ()))
    return (lax.dot_general(hi, ones, dims, preferred_element_type=F32)
            + lax.dot_general(lo, ones, dims, preferred_element_type=F32))


@jax.custom_vjp
def SEG(x, ones):
    return _exact_rhs_dot(x, ones, 1, 0)


SEG.defvjp(lambda x, ones: (_exact_rhs_dot(x, ones, 1, 0), ones),
           lambda ones, g: (_exact_rhs_dot(g, ones, 1, 1), jnp.zeros_like(ones)))

NN, NT, TN = _make_dots()


MM_TILE_M, MM_TILE_N, MM_TILE_K = 1408, 1408, 1536


def _pick(n, cap):
    best = None
    for t in range(128, min(n, cap) + 1, 128):
        if n % t == 0:
            best = t
    return best or n


def matmul(a, b, mode, name, *, add=None, scale=1.0, out_dtype=F32):
    if mode == "nn":
        (M, K), (K2, N) = a.shape, b.shape
    elif mode == "nt":
        (M, K), (N, K2) = a.shape, b.shape
    else:
        (K, M), (K2, N) = a.shape, b.shape
    assert K == K2, (name, a.shape, b.shape)
    tm, tn, tk = _pick(M, MM_TILE_M), _pick(N, MM_TILE_N), _pick(K, MM_TILE_K)
    nk = K // tk
    ca, cb = {"nn": (1, 0), "nt": (1, 1), "tn": (0, 0)}[mode]

    def body(*refs):
        if add is None:
            a_ref, b_ref, o_ref, acc_ref = refs
        else:
            a_ref, b_ref, add_ref, o_ref, acc_ref = refs
        k = pl.program_id(2)

        @pl.when(k == 0)
        def _():
            acc_ref[...] = jnp.zeros_like(acc_ref)

        acc_ref[...] += lax.dot_general(a_ref[...].astype(BF16), b_ref[...].astype(BF16),
                                        (((ca,), (cb,)), ((), ())), preferred_element_type=F32)

        @pl.when(k == nk - 1)
        def _():
            r = acc_ref[...] * scale
            if add is not None:
                r = add_ref[...] + r
            o_ref[...] = r.astype(o_ref.dtype)

    a_spec = (pl.BlockSpec((tk, tm), lambda i, j, k: (k, i)) if mode == "tn"
              else pl.BlockSpec((tm, tk), lambda i, j, k: (i, k)))
    b_spec = (pl.BlockSpec((tn, tk), lambda i, j, k: (j, k)) if mode == "nt"
              else pl.BlockSpec((tk, tn), lambda i, j, k: (k, j)))
    in_specs, args = [a_spec, b_spec], [a, b]
    blk = tm * tk * a.dtype.itemsize + tk * tn * b.dtype.itemsize + tm * tn * 8
    if add is not None:
        in_specs.append(pl.BlockSpec((tm, tn), lambda i, j, k: (i, j)))
        args.append(add)
        blk += tm * tn * 4
    return pl.pallas_call(
        body, name=name, grid=(M // tm, N // tn, nk),
        in_specs=in_specs, out_specs=pl.BlockSpec((tm, tn), lambda i, j, k: (i, j)),
        out_shape=jax.ShapeDtypeStruct((M, N), out_dtype),
        scratch_shapes=[pltpu.VMEM((tm, tn), F32)],
        compiler_params=_cparams(("parallel", "parallel", "arbitrary"), blk),
    )(*args)


def matmul_cs(a, w, mode, name, *, scale=1.0, out_dtype=F32):
    n_blk = N_SHARDS
    if mode == "tn":
        (K, R), Cs = a.shape, w.shape[2] // 2
        tm, tk = _pick(R, MM_TILE_M), _pick(K, 1024)
        grid = (R // tm, n_blk, K // tk)
        a_spec = pl.BlockSpec((tk, tm), lambda i, j, k: (k, i))
        w_spec = pl.BlockSpec((None, tk, Cs), lambda i, j, k: (j // 2, k, j % 2))
        o_spec = pl.BlockSpec((None, tm, Cs), lambda i, j, k: (j, i, 0))
        out_shape, acc_shape, dims = (n_blk, R, Cs), (tm, Cs), (0, 0)
        blk = tk * tm * a.dtype.itemsize + tk * Cs * w.dtype.itemsize + tm * Cs * 8
    else:
        M, (_, R, Cs) = a.shape[1], w.shape
        tm, tn = _pick(M, MM_TILE_M), _pick(R, MM_TILE_N)
        grid = (M // tm, R // tn, n_blk)
        a_spec = pl.BlockSpec((None, tm, Cs), lambda i, j, k: (k // 2, i, k % 2))
        w_spec = pl.BlockSpec((None, tn, Cs), lambda i, j, k: (k, j, 0))
        o_spec = pl.BlockSpec((tm, tn), lambda i, j, k: (i, j))
        out_shape, acc_shape, dims = (M, R), (tm, tn), (1, 1)
        blk = tm * Cs * a.dtype.itemsize + tn * Cs * w.dtype.itemsize + tm * tn * 8
    nk = grid[2]

    def body(a_ref, w_ref, o_ref, acc_ref):
        k = pl.program_id(2)

        @pl.when(k == 0)
        def _():
            acc_ref[...] = jnp.zeros_like(acc_ref)

        acc_ref[...] += lax.dot_general(a_ref[...].astype(BF16), w_ref[...].astype(BF16),
                                        (((dims[0],), (dims[1],)), ((), ())), preferred_element_type=F32)

        @pl.when(k == nk - 1)
        def _():
            o_ref[...] = (acc_ref[...] * scale).astype(o_ref.dtype)

    return pl.pallas_call(
        body, name=name, grid=grid, in_specs=[a_spec, w_spec], out_specs=o_spec,
        out_shape=jax.ShapeDtypeStruct(out_shape, out_dtype), scratch_shapes=[pltpu.VMEM(acc_shape, F32)],
        compiler_params=_cparams(("parallel", "parallel", "arbitrary"), blk),
    )(a, w)


FFN_TILE_M = 512


def _swiglu(gate, up):
    return gate * jax.nn.sigmoid(gate) * up


def ffn_in_act(h, w, name):
    (M, R), Cs, half = h.shape, w.shape[2], N_SHARDS // 2
    tm, tk = _pick(M, FFN_TILE_M), _pick(R, 1024)
    nk = R // tk

    def body(h_ref, wg_ref, wu_ref, gu_ref, act_ref, acc_ref):
        k = pl.program_id(2)

        @pl.when(k == 0)
        def _():
            acc_ref[...] = jnp.zeros_like(acc_ref)

        hb = h_ref[...].astype(BF16)
        for part, w_ref in enumerate((wg_ref, wu_ref)):
            acc_ref[part] += jnp.dot(hb, w_ref[...].astype(BF16), preferred_element_type=F32)

        @pl.when(k == nk - 1)
        def _():
            gu_ref[...] = acc_ref[...]
            act_ref[...] = _swiglu(acc_ref[0], acc_ref[1]).astype(act_ref.dtype)

    w_spec = lambda off: pl.BlockSpec((None, tk, Cs), functools.partial(lambda off, i, j, k: (j + off, k, 0), off))
    blk = tm * tk * h.dtype.itemsize + 2 * tk * Cs * w.dtype.itemsize + tm * Cs * (16 + 2)
    return pl.pallas_call(
        body, name=name, grid=(M // tm, half, nk),
        in_specs=[pl.BlockSpec((tm, tk), lambda i, j, k: (i, k)), w_spec(0), w_spec(half)],
        out_specs=[pl.BlockSpec((2, tm, Cs), lambda i, j, k: (0, i, j)), pl.BlockSpec((tm, Cs), lambda i, j, k: (i, j))],
        out_shape=[jax.ShapeDtypeStruct((2, M, half * Cs), F32), jax.ShapeDtypeStruct((M, half * Cs), BF16)],
        scratch_shapes=[pltpu.VMEM((2, tm, Cs), F32)],
        compiler_params=_cparams(("parallel", "parallel", "arbitrary"), blk),
    )(h, w, w)


def ffn_dact_dgu(dy, w_out, gu, scale, name):
    (M, D), F = dy.shape, w_out.shape[0]
    tm, tn = _pick(M, FFN_TILE_M), F // 2

    def body(dy_ref, w_ref, gu_ref, dgu_ref):
        dact = scale * lax.dot_general(dy_ref[...].astype(BF16), w_ref[...].astype(BF16),
                                       (((1,), (1,)), ((), ())), preferred_element_type=F32)
        dgate, dup = jax.vjp(_swiglu, gu_ref[0], gu_ref[1])[1](dact)
        dgu_ref[0] = dgate.astype(dgu_ref.dtype)
        dgu_ref[1] = dup.astype(dgu_ref.dtype)

    pair = pl.BlockSpec((2, tm, tn), lambda i, j: (0, i, j))
    blk = tm * D * dy.dtype.itemsize + tn * D * w_out.dtype.itemsize + 2 * tm * tn * (4 + 2)
    return pl.pallas_call(
        body, name=name, grid=(M // tm, F // tn),
        in_specs=[pl.BlockSpec((tm, D), lambda i, j: (i, 0)), pl.BlockSpec((tn, D), lambda i, j: (j, 0)), pair],
        out_specs=pair, out_shape=jax.ShapeDtypeStruct((2, M, F), BF16),
        compiler_params=_cparams(("parallel", "parallel"), blk),
    )(dy, w_out, gu)


def _row_block(n, width, n_arrays):
    cap = (V7X_VMEM_BYTES // 4) // (2 * 4 * width * n_arrays)
    best = None
    for t in range(16, min(n, cap) + 1, 16):
        if n % t == 0:
            best = t
    return best or n


def placed_map(f, ins, out, *, n_blocks, tb, name):
    def body(*refs):
        refs[-1][...] = f(*[r[...] for r in refs[:-1]]).astype(refs[-1].dtype)

    def spec(fn):
        def index(i):
            x, y, c = _place()
            return fn(i, (c, 2 * x + y)), 0
        return pl.BlockSpec((tb, width), index)

    o_rows, width, o_dtype, o_fn = out
    blk = (sum(a.dtype.itemsize for a, _ in ins) + jnp.dtype(o_dtype).itemsize) * tb * width
    return pl.pallas_call(
        body, name=name, grid=(n_blocks,), in_specs=[spec(fn) for _, fn in ins], out_specs=spec(o_fn),
        out_shape=jax.ShapeDtypeStruct((o_rows, width), o_dtype),
        compiler_params=_cparams(("parallel",), blk),
    )(*[a for a, _ in ins])


def rowmap(f, rows, params, outs, accs=(), *, tb, name):
    rows = [r if isinstance(r, tuple) else (r, r.shape[1], 0) for r in rows]
    S = rows[0][0].shape[0]
    assert S % tb == 0, (name, S, tb)
    n_in, n_out = len(rows) + len(params), len(outs)

    def body(*refs):
        res = f(*[r[...] for r in refs[:n_in]])
        res = res if isinstance(res, (tuple, list)) else (res,)
        o_refs, a_refs = refs[n_in:n_in + n_out], refs[n_in + n_out:]
        for ref, val in zip(o_refs, res[:n_out]):
            ref[...] = val.astype(ref.dtype)
        if a_refs:
            @pl.when(pl.program_id(0) == 0)
            def _():
                for ref in a_refs:
                    ref[...] = jnp.zeros_like(ref)

            for ref, val in zip(a_refs, res[n_out:]):
                ref[...] += val.astype(F32)

    in_specs = [pl.BlockSpec((tb, w), functools.partial(lambda cb, i: (i, cb), cb)) for _, w, cb in rows]
    in_specs += [pl.BlockSpec(p.shape, lambda i: (0, 0)) for p in params]
    out_specs = [pl.BlockSpec((tb, w), lambda i: (i, 0)) for w, _ in outs]
    out_specs += [pl.BlockSpec(tuple(s), lambda i: (0, 0)) for s in accs]
    out_shape = [jax.ShapeDtypeStruct((S, w), dt) for w, dt in outs]
    out_shape += [jax.ShapeDtypeStruct(tuple(s), F32) for s in accs]
    blk = sum(tb * w * a.dtype.itemsize for a, w, _ in rows) + sum(_nbytes(p.shape, p.dtype) for p in params)
    blk += sum(_nbytes((tb, w), dt) for w, dt in outs) + sum(_nbytes(s, F32) for s in accs)
    res = pl.pallas_call(
        body, name=name, grid=(S // tb,), in_specs=in_specs, out_specs=out_specs, out_shape=out_shape,
        compiler_params=_cparams(("arbitrary",) if accs else ("parallel",), blk),
    )(*[r[0] for r in rows], *[pltpu.with_memory_space_constraint(p, pltpu.HBM) for p in params])
    return res


def _rms(x, g):
    return x * lax.rsqrt(jnp.mean(x * x, axis=-1, keepdims=True) + RMS_EPS) * g


def _softplus(z):
    return jnp.maximum(z, 0.0) + jnp.log(1.0 + jnp.exp(-jnp.abs(z)))


def _rwkv_pre(xrk, xlo, w0, w2p, a0, a2p, g2p, k_k, k_a, seg, seg_t):
    k = xrk[:, D_MODEL:2 * D_MODEL]
    w = -_softplus(-(w0 + NN(jnp.tanh(xlo), w2p))) - 0.5
    log_decay = -jnp.exp(w)
    a = jax.nn.sigmoid(a0 + NN(xlo, a2p))
    g = NN(jax.nn.sigmoid(xlo), g2p)
    kk = k * k_k
    norm = jnp.maximum(jnp.sqrt(SEG(kk * kk, seg)), 1e-12)
    kk = kk * SEG(1.0 / norm, seg_t)
    k_mod = k * (1.0 + (a - 1.0) * k_a)
    return log_decay, k_mod, -kk, kk * a, g


def _rwkv_post(wkv, r, k_mod, v, g, r_k, ln_w, ln_b, seg, seg_t):
    inv_n = 1.0 / HEAD_DIM
    mean = SEG(wkv, seg) * inv_n
    cen = wkv - SEG(mean, seg_t)
    var = SEG(cen * cen, seg) * inv_n
    y = cen * SEG(lax.rsqrt(var + GN_EPS), seg_t) * ln_w + ln_b
    bonus = SEG(SEG(r * k_mod * r_k, seg), seg_t) * v
    return (y + bonus) * g


def _qk_norm(q, k, q_gain, k_gain, seg, seg_t, tile_t):
    def norm(x, gain):
        mean_sq = SEG(x * x, seg) * (1.0 / HEAD_DIM)
        return x * SEG(lax.rsqrt(mean_sq + RMS_EPS), seg_t) * SEG(gain, tile_t)

    return norm(q, q_gain) * (HEAD_DIM ** -0.5), norm(k, k_gain)


def _gate_merge(pgate, pa, pb, b_gate):
    sg = jax.nn.sigmoid(pgate + b_gate)
    return sg[:, :D_MODEL] * pa + sg[:, D_MODEL:] * pb


def _group_combine(o, lse):
    ls = [lse[:, GROUP_W * i:GROUP_W * (i + 1)] for i in range(3)]
    m = jnp.maximum(jnp.maximum(ls[0], ls[1]), ls[2])
    es = [jnp.exp(l - m) for l in ls]
    den = es[0] + es[1] + es[2]
    return jnp.concatenate([o[:, GROUP_W * i:GROUP_W * (i + 1)] * (es[i] / den) for i in range(3)], axis=1)


def _each(f, *xs):
    return tuple(f(*args) for args in zip(*xs))


def _attn_block(q, kc, kp, vc, vp, first):
    qi = lax.broadcasted_iota(jnp.int32, (ATTN_BLK, ATTN_BLK), 0)
    kj = lax.broadcasted_iota(jnp.int32, (ATTN_BLK, ATTN_BLK), 1)
    own = kj <= qi
    s_c = _each(lambda a, b: jnp.where(own, NT(a, b), NEG_INF), q, kc)
    s_p = _each(lambda a, b, f: jnp.where((kj >= qi) & (f < 0.5), NT(a, b), NEG_INF), q, kp, first)
    row_max = lambda s: jnp.max(s, axis=-1, keepdims=True)
    row_sum = lambda s: jnp.sum(s, axis=-1, keepdims=True)
    m = _each(lambda c_, p_: jnp.maximum(row_max(c_), row_max(p_)), s_c, s_p)
    e_c, e_p = _each(lambda s, m_: jnp.exp(s - m_), s_c, m), _each(lambda s, m_: jnp.exp(s - m_), s_p, m)
    den = _each(lambda c_, p_: row_sum(c_) + row_sum(p_), e_c, e_p)
    inv = _each(lambda d_: 1.0 / d_, den)
    o = _each(lambda ec, ep, i_, vc_, vp_: (NN(ec, vc_) + NN(ep, vp_)) * i_, e_c, e_p, inv, vc, vp)
    lse = _each(lambda m_, d_: jnp.broadcast_to(m_ + jnp.log(d_), (ATTN_BLK, HEAD_DIM)), m, den)
    return o, lse


def _attn_pair(q, k, k_before, v, v_before, first):
    n = len(q[0])
    o, lse = _attn_block(q[0] + q[1], k[0] + k[1], k_before + k[0], v[0] + v[1], v_before + v[0],
                         (first[0],) * n + (first[1],) * n)
    return (o[:n], o[n:]), (lse[:n], lse[n:])


TRI_SEED = 8


def _tri_inverse(n):
    c = n[0].shape[0]
    row = lax.broadcasted_iota(jnp.int32, (c, c), 0)
    col = lax.broadcasted_iota(jnp.int32, (c, c), 1)
    same_block = lambda size: (row >> (size.bit_length() - 1)) == (col >> (size.bit_length() - 1))
    seed = same_block(TRI_SEED)
    p = _each(lambda m: jnp.where(seed, m, 0.0), n)
    t, span = _each(lambda m: (row == col).astype(F32) + m, p), 2
    while span < TRI_SEED:
        p = _each(NN, p, p)
        t = _each(lambda t_, p_: t_ + NN(t_, p_), t, p)
        span *= 2
    size = TRI_SEED
    while size < c:
        joins = same_block(2 * size) & jnp.logical_not(same_block(size))
        t = _each(lambda t_, m: t_ + NN(NN(t_, jnp.where(joins, m, 0.0)), t_), t, n)
        size *= 2
    return t


@jax.custom_vjp
def _tri_solve(n, rhs, t):
    return _each(NN, t, rhs)


def _tri_solve_fwd(n, rhs, t):
    x = _each(NN, t, rhs)
    return x, (t, x)


def _tri_solve_bwd(res, dx):
    t, x = res
    drhs = _each(TN, t, dx)
    return _each(NT, drhs, x), drhs, _each(jnp.zeros_like, t)


_tri_solve.defvjp(_tri_solve_fwd, _tri_solve_bwd)


def _lower_ones(c):
    row = lax.broadcasted_iota(jnp.int32, (c, c), 0)
    col = lax.broadcasted_iota(jnp.int32, (c, c), 1)
    return (row >= col).astype(BF16)


def _ones_dot(ones, x, contract):
    hi, lo = _split_bf16(x)
    dims = (((contract,), (0,)), ((), ()))
    return (lax.dot_general(ones, hi, dims, preferred_element_type=F32)
            + lax.dot_general(ones, lo, dims, preferred_element_type=F32))


@jax.custom_vjp
def _cumsum_rows(x):
    return _ones_dot(_lower_ones(x.shape[0]), x, 1)


_cumsum_rows.defvjp(lambda x: (_ones_dot(_lower_ones(x.shape[0]), x, 1), None),
                    lambda _, g: (_ones_dot(_lower_ones(g.shape[0]), g, 0),))


def _wkv_chunk(s0, r, lw, k, v, a, b, t_inv=None):
    c = r[0].shape[0]
    row = lax.broadcasted_iota(jnp.int32, (c, c), 0)
    col = lax.broadcasted_iota(jnp.int32, (c, c), 1)
    strict, incl = row > col, row >= col
    cat = lambda p, q: jnp.concatenate([p, q], axis=0)
    cum = _each(_cumsum_rows, lw)
    e_neg = _each(lambda c_: jnp.exp(-c_), cum)
    ar = _each(lambda a_, r_, c_, l_: cat(a_ * jnp.exp(c_ - l_), r_ * jnp.exp(c_)), a, r, cum, lw)
    b_t, k_t = _each(jnp.multiply, b, e_neg), _each(jnp.multiply, k, e_neg)
    p_b, p_k, p_s = _each(NT, ar, b_t), _each(NT, ar, k_t), _each(NT, ar, s0)
    n_ab = _each(lambda p: jnp.where(strict, p[:c], 0.0), p_b)
    m_rb = _each(lambda p: jnp.where(incl, p[c:], 0.0), p_b)
    n_ak = _each(lambda p: jnp.where(strict, p[:c], 0.0), p_k)
    m_rk = _each(lambda p: jnp.where(incl, p[c:], 0.0), p_k)
    if t_inv is None:
        t_inv = _tri_inverse(n_ab)
    u = _tri_solve(n_ab, _each(lambda p, n_, v_: p[:c] + NN(n_, v_), p_s, n_ak, v), t_inv)
    y = _each(lambda p, mb, u_, mk, v_: p[c:] + NN(mb, u_) + NN(mk, v_), p_s, m_rb, u, m_rk, v)
    g_end = _each(lambda l_: jnp.exp(jnp.sum(l_, axis=0, keepdims=True)), lw)
    s1 = _each(lambda s_, g_, u_, v_, b_, k_: s_ * g_ + TN(cat(u_, v_), cat(b_, k_) * g_),
               s0, g_end, u, v, b_t, k_t)
    return y, s1, t_inv


def _adamw(w, g, m, v):
    m = ADAM_B1 * m + (1.0 - ADAM_B1) * g
    v = ADAM_B2 * v + (1.0 - ADAM_B2) * jnp.square(g)
    m_hat = m / (1.0 - ADAM_B1 ** ADAM_STEP)
    v_hat = v / (1.0 - ADAM_B2 ** ADAM_STEP)
    delta = -ADAM_LR * (m_hat / (jnp.sqrt(v_hat) + ADAM_EPS) + ADAM_WD * w)
    return delta, m, v


def token_shift_fwd(p, mu, *, tb, name):
    S, W = p.shape
    hb = tb // 8

    def body(p_ref, halo_ref, mu_ref, o_ref):
        i = pl.program_id(0)
        x = p_ref[...]
        before = halo_ref[7:8, :] * (i > 0).astype(F32)
        row = lax.broadcasted_iota(jnp.int32, (tb, W), 0)
        prev = jnp.where(row == 0, before, pltpu.roll(x, 1, 0))
        o_ref[...] = x + (prev - x) * mu_ref[...]

    blk = (2 * tb + 8) * W * 4
    return pl.pallas_call(
        body, name=name, grid=(S // tb,),
        in_specs=[pl.BlockSpec((tb, W), lambda i: (i, 0)),
                  pl.BlockSpec((8, W), lambda i: (jnp.maximum(i * hb - 1, 0), 0)),
                  pl.BlockSpec((1, W), lambda i: (0, 0))],
        out_specs=pl.BlockSpec((tb, W), lambda i: (i, 0)),
        out_shape=jax.ShapeDtypeStruct((S, W), F32),
        compiler_params=_cparams(("parallel",), blk),
    )(p, p, mu)


def token_shift_bwd(dxs, p, mu, *, tb, name):
    S, W = p.shape
    hb, nb = tb // 8, S // tb

    def body(d_ref, dnext_ref, p_ref, halo_ref, mu_ref, dp_ref, dmu_ref):
        i = pl.program_id(0)
        d, x, mu_v = d_ref[...], p_ref[...], mu_ref[...]
        row = lax.broadcasted_iota(jnp.int32, (tb, W), 0)
        before = halo_ref[7:8, :] * (i > 0).astype(F32)
        prev = jnp.where(row == 0, before, pltpu.roll(x, 1, 0))
        t = d * mu_v
        after = dnext_ref[0:1, :] * mu_v * (i < nb - 1).astype(F32)
        nxt = jnp.where(row == tb - 1, after, pltpu.roll(t, tb - 1, 0))
        dp_ref[...] = (d - t + nxt).astype(dp_ref.dtype)

        @pl.when(i == 0)
        def _():
            dmu_ref[...] = jnp.zeros_like(dmu_ref)

        dmu_ref[...] += jnp.sum(d * (prev - x), axis=0, keepdims=True)

    blk = (3 * tb + 16) * W * 4
    return pl.pallas_call(
        body, name=name, grid=(nb,),
        in_specs=[pl.BlockSpec((tb, W), lambda i: (i, 0)),
                  pl.BlockSpec((8, W), lambda i: (jnp.minimum((i + 1) * hb, S // 8 - 1), 0)),
                  pl.BlockSpec((tb, W), lambda i: (i, 0)),
                  pl.BlockSpec((8, W), lambda i: (jnp.maximum(i * hb - 1, 0), 0)),
                  pl.BlockSpec((1, W), lambda i: (0, 0))],
        out_specs=[pl.BlockSpec((tb, W), lambda i: (i, 0)), pl.BlockSpec((1, W), lambda i: (0, 0))],
        out_shape=[jax.ShapeDtypeStruct((S, W), BF16), jax.ShapeDtypeStruct((1, W), F32)],
        compiler_params=_cparams(("arbitrary",), blk),
    )(dxs, dxs, p, p, mu)


def _head_cols(h):
    return pl.ds(h * HEAD_DIM, HEAD_DIM)


def wkv_fwd(xs_rk, lw, k, a, b):
    S = lw.shape[0]
    C, nc, G, N = WKV_CHUNK, S // WKV_CHUNK, WKV_HEADS_PER_STEP, HEAD_DIM

    def body(r_ref, lw_ref, k_ref, v_ref, a_ref, b_ref, y_ref, st_ref, ti_ref, state):
        @pl.when(pl.program_id(1) == 0)
        def _():
            state[...] = jnp.zeros_like(state)

        heads = lambda ref: tuple(ref[:, _head_cols(h)] for h in range(G))
        s0 = tuple(state[h] for h in range(G))
        y, s1, t_inv = _wkv_chunk(s0, heads(r_ref), heads(lw_ref), heads(k_ref), heads(v_ref), heads(a_ref),
                                  heads(b_ref))
        for h in range(G):
            st_ref[h] = s0[h]
            ti_ref[h] = t_inv[h]
            y_ref[:, _head_cols(h)] = y[h]
            state[h] = s1[h]

    W = G * N
    seq = lambda j: pl.BlockSpec((C, W), functools.partial(lambda j, g, c: (c, j + g), j))
    per = D_MODEL // W
    per_chunk = pl.BlockSpec((None, G, N, N), lambda g, c: (c, g, 0, 0))
    return pl.pallas_call(
        body, name="wkv_fwd", grid=(RWKV_HEADS // G, nc),
        in_specs=[seq(0), seq(0), seq(0), seq(2 * per), seq(0), seq(0)],
        out_specs=[seq(0), per_chunk, per_chunk],
        out_shape=[jax.ShapeDtypeStruct((S, D_MODEL), F32)] + [jax.ShapeDtypeStruct((nc, RWKV_HEADS, N, N), F32)] * 2,
        scratch_shapes=[pltpu.VMEM((G, N, N), F32)],
        compiler_params=_cparams(("parallel", "arbitrary"), 8 * C * W * 4 + 3 * G * N * N * 4),
    )(xs_rk, lw, k, xs_rk, a, b)


def wkv_bwd(xs_rk, lw, k, a, b, states, t_invs, dy):
    S = lw.shape[0]
    C, nc, G, N = WKV_CHUNK, S // WKV_CHUNK, WKV_HEADS_PER_STEP, HEAD_DIM

    def body(r_ref, lw_ref, k_ref, v_ref, a_ref, b_ref, st_ref, ti_ref, dy_ref,
             dr_ref, dlw_ref, dk_ref, dv_ref, da_ref, db_ref, dstate):
        @pl.when(pl.program_id(1) == 0)
        def _():
            dstate[...] = jnp.zeros_like(dstate)

        heads = lambda ref: tuple(ref[:, _head_cols(h)] for h in range(G))
        t_inv = tuple(ti_ref[h] for h in range(G))
        chunk = lambda *args: _wkv_chunk(*args, t_inv)[:2]
        _, pull = jax.vjp(chunk, tuple(st_ref[h] for h in range(G)), heads(r_ref), heads(lw_ref),
                          heads(k_ref), heads(v_ref), heads(a_ref), heads(b_ref))
        ds0, *grads = pull((heads(dy_ref), tuple(dstate[h] for h in range(G))))
        for h in range(G):
            dstate[h] = ds0[h]
            for ref, grad in zip((dr_ref, dlw_ref, dk_ref, dv_ref, da_ref, db_ref), grads):
                ref[:, _head_cols(h)] = grad[h]

    W = G * N
    seq = lambda j: pl.BlockSpec((C, W), functools.partial(lambda j, g, c: (nc - 1 - c, j + g), j))
    per = D_MODEL // W
    st = pl.BlockSpec((None, G, N, N), lambda g, c: (nc - 1 - c, g, 0, 0))
    return pl.pallas_call(
        body, name="wkv_bwd", grid=(RWKV_HEADS // G, nc),
        in_specs=[seq(0), seq(0), seq(0), seq(2 * per), seq(0), seq(0), st, st, seq(0)],
        out_specs=[seq(0)] * 6, out_shape=[jax.ShapeDtypeStruct((S, D_MODEL), F32)] * 6,
        scratch_shapes=[pltpu.VMEM((G, N, N), F32)],
        compiler_params=_cparams(("parallel", "arbitrary"), 14 * C * W * 4 + 3 * G * N * N * 4),
    )(xs_rk, lw, k, xs_rk, a, b, states, t_invs, dy)


def _first_flag(i, seq_len):
    per_group = seq_len // ATTN_BLK
    g = i // per_group
    per_seq = [seq_len // d // ATTN_BLK for _, d in ATTN_PAIRS]
    n = jnp.where(g == 0, per_seq[0], jnp.where(g == 1, per_seq[1], per_seq[2]))
    return (lax.rem(i, n) == 0).astype(F32)


def _block_rows(half):
    return pl.ds(half * ATTN_BLK, ATTN_BLK)


def _block_heads(ref, half):
    return tuple(ref[_block_rows(half), _head_cols(h)] for h in range(ATTN_HPG))


def _pair_heads(ref):
    return _block_heads(ref, 0), _block_heads(ref, 1)


def attn_fwd(q, k, v, seq_len):
    R, N = q.shape
    n_pairs = R // (2 * ATTN_BLK)

    def body(q_ref, k_ref, kb_ref, v_ref, vb_ref, o_ref, lse_ref):
        pair = pl.program_id(0)
        first = (_first_flag(2 * pair, seq_len), _first_flag(2 * pair + 1, seq_len))
        o, lse = _attn_pair(_pair_heads(q_ref), _pair_heads(k_ref), _block_heads(kb_ref, 0), _pair_heads(v_ref),
                            _block_heads(vb_ref, 0), first)
        for half in range(2):
            for h in range(ATTN_HPG):
                o_ref[_block_rows(half), _head_cols(h)] = o[half][h]
                lse_ref[_block_rows(half), _head_cols(h)] = lse[half][h]

    cur = pl.BlockSpec((2 * ATTN_BLK, N), lambda i: (i, 0))
    prv = pl.BlockSpec((ATTN_BLK, N), lambda i: (jnp.maximum(2 * i - 1, 0), 0))
    return pl.pallas_call(
        body, name="attn_fwd", grid=(n_pairs,), in_specs=[cur, cur, prv, cur, prv],
        out_specs=[cur, cur], out_shape=[jax.ShapeDtypeStruct((R, N), F32)] * 2,
        compiler_params=_cparams(("parallel",), 12 * ATTN_BLK * N * 4),
    )(q, k, k, v, v)


def attn_bwd(q, k, v, do, dlse, seq_len):
    R, N = q.shape
    n_pairs = R // (2 * ATTN_BLK)

    def body(q_ref, k_ref, kb_ref, v_ref, vb_ref, do_ref, dl_ref, dq_ref, dk_ref, dv_ref, carry_k, carry_v):
        step = pl.program_id(0)
        pair = n_pairs - 1 - step
        first = (_first_flag(2 * pair, seq_len), _first_flag(2 * pair + 1, seq_len))

        @pl.when(step == 0)
        def _():
            carry_k[...] = jnp.zeros_like(carry_k)
            carry_v[...] = jnp.zeros_like(carry_v)

        _, pull = jax.vjp(functools.partial(_attn_pair, first=first), _pair_heads(q_ref), _pair_heads(k_ref),
                          _block_heads(kb_ref, 0), _pair_heads(v_ref), _block_heads(vb_ref, 0))
        dq, dk, dk_before, dv, dv_before = pull((_pair_heads(do_ref), _pair_heads(dl_ref)))
        old_k, old_v = _block_heads(carry_k, 0), _block_heads(carry_v, 0)
        for h in range(ATTN_HPG):
            cols = _head_cols(h)
            for half in range(2):
                dq_ref[_block_rows(half), cols] = dq[half][h]
            dk_ref[_block_rows(0), cols] = dk[0][h]
            dv_ref[_block_rows(0), cols] = dv[0][h]
            dk_ref[_block_rows(1), cols] = dk[1][h] + old_k[h]
            dv_ref[_block_rows(1), cols] = dv[1][h] + old_v[h]
            carry_k[:, cols] = dk_before[h]
            carry_v[:, cols] = dv_before[h]

    cur = pl.BlockSpec((2 * ATTN_BLK, N), lambda i: (n_pairs - 1 - i, 0))
    prv = pl.BlockSpec((ATTN_BLK, N), lambda i: (jnp.maximum(2 * (n_pairs - 1 - i) - 1, 0), 0))
    return pl.pallas_call(
        body, name="attn_bwd", grid=(n_pairs,), in_specs=[cur, cur, prv, cur, prv, cur, cur],
        out_specs=[cur, cur, cur], out_shape=[jax.ShapeDtypeStruct((R, N), F32)] * 3,
        scratch_shapes=[pltpu.VMEM((ATTN_BLK, N), F32)] * 2,
        compiler_params=_cparams(("arbitrary",), 22 * ATTN_BLK * N * 4),
    )(q, k, k, v, v, do, dlse)


def to_subsequences(t):
    S = t.shape[0]
    parts = []
    for gi, (_, d) in enumerate(ATTN_PAIRS):
        tg = t[:, GROUP_W * gi:GROUP_W * (gi + 1)].reshape(S // d, d, GROUP_W)
        parts.append(tg.transpose(1, 0, 2).reshape(S, GROUP_W))
    return jnp.concatenate(parts, axis=0)


def from_subsequences(u, S):
    parts = []
    for gi, (_, d) in enumerate(ATTN_PAIRS):
        ug = u[S * gi:S * (gi + 1)].reshape(d, S // d, GROUP_W)
        parts.append(ug.transpose(1, 0, 2).reshape(S, GROUP_W))
    return jnp.concatenate(parts, axis=1)


def _ffn_fwd(x, norm, w_in, w_out, tag):
    h = rowmap(_rms, [x], [norm], [(D_MODEL, BF16)], tb=512, name=tag + "_norm")[0]
    gu, act = ffn_in_act(h, w_in, tag + "_in")
    y = matmul(act, w_out, "nn", tag + "_out", add=x, scale=0.5)
    return y, (x, h, gu, act)


def _ffn_bwd(dy, saved, norm, w_in, w_out, tag):
    x, h, gu, act = saved
    dw_out = matmul(act, dy, "tn", tag + "_dwout", scale=0.5)
    dgu = ffn_dact_dgu(dy, w_out, gu, 0.5, tag + "_dgu")
    dh = matmul_cs(dgu, w_in, "nt", tag + "_dh")
    dw_in = matmul_cs(h, dgu, "tn", tag + "_dwin")

    def norm_bwd(x_b, dh_b, dy_b, g):
        dx, dg = jax.vjp(_rms, x_b, g)[1](dh_b)
        return dy_b + dx, dg

    dx, dnorm = rowmap(norm_bwd, [x, dh, dy], [norm], [(D_MODEL, F32)], [(1, D_MODEL)], tb=256,
                       name=tag + "_dnorm")
    return dx, dnorm, dw_in, dw_out


def layer_step(x, tgt, W, P, on_mixer_grads):
    S = x.shape[0]
    head_of = lambda n: jnp.arange(n)[:, None] // HEAD_DIM == jnp.arange(n // HEAD_DIM)[None, :]
    seg, seg_a = head_of(D_MODEL).astype(BF16), head_of(ATTN_WIDTH).astype(BF16)
    seg_t, seg_a_t = seg.T, seg_a.T
    tile_t = (jnp.arange(HEAD_DIM)[:, None] == jnp.arange(ATTN_WIDTH)[None, :] % HEAD_DIM).astype(BF16)
    qk_params = [P["attn_q_norm"], P["attn_k_norm"], seg_a, seg_a_t, tile_t]
    w_rkv, w_lora = W["w_in"][:, :RKV], W["w_in"][:, RKV:RKV + LORA]
    w_qkv = W["w_in"][:, RKV + LORA:RKV + LORA + 3 * ATTN_WIDTH]
    w_gate = W["w_in"][:, RKV + LORA + 3 * ATTN_WIDTH:]
    mu_rk, mu_lo = P["rwkv_mu"][:, :RKV], P["rwkv_mu"][:, RKV:]
    zeros = lambda n: jnp.zeros((n, D_MODEL), F32)
    w2p = jnp.concatenate([W["rwkv_w2"], zeros(LORA - LORA_W)], axis=0)
    a2p = jnp.concatenate([zeros(LORA_W), W["rwkv_a2"], zeros(LORA_G)], axis=0)
    g2p = jnp.concatenate([zeros(LORA_W + LORA_A), W["rwkv_g2"]], axis=0)
    pre_params = [P["rwkv_w0"], w2p, P["rwkv_a0"], a2p, g2p, P["rwkv_k_k"], P["rwkv_k_a"], seg, seg_t]
    post_params = [P["rwkv_r_k"], P["rwkv_ln_w"], P["rwkv_ln_b"], seg, seg_t]
    col = lambda arr, j: (arr, D_MODEL, j)

    x1, ffn1_saved = _ffn_fwd(x, P["ffn1_norm"], W["ffn1_w_in"], W["ffn1_w_out"], "ffn1")
    h = rowmap(_rms, [x1], [P["mix_norm"]], [(D_MODEL, BF16)], tb=512, name="mix_norm")[0]
    p_rk = matmul(h, w_rkv, "nn", "proj_rkv")
    p_lo = matmul(h, w_lora, "nn", "proj_lora")
    p_qkv = matmul(h, w_qkv, "nn", "proj_qkv")
    p_gate = matmul(h, w_gate, "nn", "proj_gate")
    xs_rk = token_shift_fwd(p_rk, mu_rk, tb=256, name="shift_rk")
    xs_lo = token_shift_fwd(p_lo, mu_lo, tb=256, name="shift_lora")
    lw, k_mod, a_neg, b_kk, g = rowmap(
        _rwkv_pre, [xs_rk, xs_lo], pre_params, [(D_MODEL, F32)] * 5, tb=256, name="rwkv_pre")
    wkv, states, t_invs = wkv_fwd(xs_rk, lw, k_mod, a_neg, b_kk)
    post_rows = [wkv, col(xs_rk, 0), k_mod, col(xs_rk, 2), g]
    y_a = rowmap(_rwkv_post, post_rows, post_params, [(D_MODEL, BF16)], tb=256, name="rwkv_post")[0]

    qk_rows = [(p_qkv, ATTN_WIDTH, 0), (p_qkv, ATTN_WIDTH, 1)]
    qn, kn = rowmap(_qk_norm, qk_rows, qk_params, [(ATTN_WIDTH, F32)] * 2, tb=256, name="qk_norm")
    q_s, k_s, v_s = to_subsequences(qn), to_subsequences(kn), to_subsequences(p_qkv[:, 2 * ATTN_WIDTH:])
    o_s, lse_s = attn_fwd(q_s, k_s, v_s, S)
    o, lse = from_subsequences(o_s, S), from_subsequences(lse_s, S)
    y_b = rowmap(_group_combine, [o, lse], [], [(ATTN_WIDTH, BF16)], tb=512, name="attn_combine")[0]

    pa = matmul(y_a, W["w_proj_rwkv"], "nn", "proj_a")
    pb = matmul(y_b, W["w_proj_attn"], "nn", "proj_b")
    merged = rowmap(_gate_merge, [p_gate, pa, pb], [P["b_gate"]], [(D_MODEL, BF16)], tb=256, name="merge")[0]
    x2 = matmul(merged, W["w_out"], "nn", "mix_out", add=x1)
    x3, ffn2_saved = _ffn_fwd(x2, P["ffn2_norm"], W["ffn2_w_in"], W["ffn2_w_out"], "ffn2")

    def loss_head(y_b_, t_b):
        err = y_b_ - t_b
        return err * (1.0 / D_MODEL), (0.5 / D_MODEL) * jnp.sum(err * err, axis=0, keepdims=True)

    dx3, loss_cols = rowmap(loss_head, [x3, tgt], [], [(D_MODEL, F32)], [(1, D_MODEL)], tb=512, name="loss")

    gW, gP = {}, {}
    dx2, gP["ffn2_norm"], gW["ffn2_w_in"], gW["ffn2_w_out"] = _ffn_bwd(
        dx3, ffn2_saved, P["ffn2_norm"], W["ffn2_w_in"], W["ffn2_w_out"], "ffn2")

    dmerged = matmul(dx2, W["w_out"], "nt", "d_merged")
    gW["w_out"] = matmul(merged, dx2, "tn", "dw_out")

    def merge_bwd(pg, pa_b, pb_b, dm, bg):
        return jax.vjp(_gate_merge, pg, pa_b, pb_b, bg)[1](dm)

    dp_gate, dpa, dpb, gP["b_gate"] = rowmap(
        merge_bwd, [p_gate, pa, pb, dmerged], [P["b_gate"]],
        [(2 * D_MODEL, BF16), (D_MODEL, BF16), (D_MODEL, BF16)], [(1, 2 * D_MODEL)], tb=256, name="merge_bwd")
    dy_a = matmul(dpa, W["w_proj_rwkv"], "nt", "d_ya")
    gW["w_proj_rwkv"] = matmul(y_a, dpa, "tn", "dw_proj_a")
    dy_b = matmul(dpb, W["w_proj_attn"], "nt", "d_yb")
    gW["w_proj_attn"] = matmul(y_b, dpb, "tn", "dw_proj_b")

    def combine_bwd(o_b, l_b, d_b):
        return jax.vjp(_group_combine, o_b, l_b)[1](d_b)

    do, dlse = rowmap(combine_bwd, [o, lse, dy_b], [], [(ATTN_WIDTH, F32)] * 2, tb=256, name="attn_combine_bwd")
    dq_s, dk_s, dv_s = attn_bwd(q_s, k_s, v_s, to_subsequences(do), to_subsequences(dlse), S)

    def qk_norm_bwd(q_b, k_b, dqn_b, dkn_b, dv_b, qg, kg, sg, sgt, tl):
        f = lambda *a: _qk_norm(*a, sg, sgt, tl)
        dq, dk, dqg, dkg = jax.vjp(f, q_b, k_b, qg, kg)[1]((dqn_b, dkn_b))
        return jnp.concatenate([dq, dk, dv_b], axis=1), dqg, dkg

    dp_qkv, gP["attn_q_norm"], gP["attn_k_norm"] = rowmap(
        qk_norm_bwd, qk_rows + [from_subsequences(t, S) for t in (dq_s, dk_s, dv_s)], qk_params,
        [(3 * ATTN_WIDTH, BF16)], [(1, HEAD_DIM)] * 2, tb=256, name="qk_norm_bwd")

    def post_bwd(wkv_b, r_b, k_b, v_b, g_b, d_b, r_k, ln_w, ln_b, sg, sgt):
        f = lambda *a: _rwkv_post(*a, sg, sgt)
        return jax.vjp(f, wkv_b, r_b, k_b, v_b, g_b, r_k, ln_w, ln_b)[1](d_b)

    dwkv, dr_p, dk_p, dv_p, dg, gP["rwkv_r_k"], gP["rwkv_ln_w"], gP["rwkv_ln_b"] = rowmap(
        post_bwd, post_rows + [dy_a], post_params, [(D_MODEL, F32)] * 5, [(1, D_MODEL)] * 3, tb=128,
        name="rwkv_post_bwd")
    dr_w, dlw, dk_w, dv_w, da_neg, db_kk = wkv_bwd(xs_rk, lw, k_mod, a_neg, b_kk, states, t_invs, dwkv)

    def pre_bwd(xrk_b, xlo_b, dlw_b, dkw_b, dkp_b, da_b, db_b, dg_b, drp_b, drw_b, dvp_b, dvw_b,
                w0, w2, a0, a2, g2, k_k, k_a, sg, sgt):
        f = lambda *a: _rwkv_pre(*a, sg, sgt)
        pull = jax.vjp(f, xrk_b, xlo_b, w0, w2, a0, a2, g2, k_k, k_a)[1]
        dxrk, dxlo, *dpar = pull((dlw_b, dkw_b + dkp_b, da_b, db_b, dg_b))
        direct = jnp.concatenate([drp_b + drw_b, jnp.zeros_like(drp_b), dvp_b + dvw_b], axis=1)
        return (dxrk + direct, dxlo, *dpar)

    pre_rows = [xs_rk, xs_lo, dlw, dk_w, dk_p, da_neg, db_kk, dg, dr_p, dr_w, dv_p, dv_w]
    dxs_rk, dxs_lo, gP["rwkv_w0"], dw2p, gP["rwkv_a0"], da2p, dg2p, gP["rwkv_k_k"], gP["rwkv_k_a"] = rowmap(
        pre_bwd, pre_rows, pre_params, [(RKV, F32), (LORA, F32)],
        [(1, D_MODEL), (LORA, D_MODEL), (1, D_MODEL), (LORA, D_MODEL), (LORA, D_MODEL), (1, D_MODEL), (1, D_MODEL)],
        tb=128, name="rwkv_pre_bwd")
    gW["rwkv_w2"] = dw2p[:LORA_W]
    gW["rwkv_a2"] = da2p[LORA_W:LORA_W + LORA_A]
    gW["rwkv_g2"] = dg2p[LORA_W + LORA_A:]
    dp_rk, dmu_rk = token_shift_bwd(dxs_rk, p_rk, mu_rk, tb=256, name="shift_rk_bwd")
    dp_lo, dmu_lo = token_shift_bwd(dxs_lo, p_lo, mu_lo, tb=256, name="shift_lora_bwd")
    gP["rwkv_mu"] = jnp.concatenate([dmu_rk, dmu_lo], axis=1)

    dh = matmul(dp_rk, w_rkv, "nt", "dh_rkv")
    dh = matmul(dp_lo, w_lora, "nt", "dh_lora", add=dh)
    dh = matmul(dp_qkv, w_qkv, "nt", "dh_qkv", add=dh)
    dh = matmul(dp_gate, w_gate, "nt", "dh_gate", add=dh)
    gW["w_in"] = jnp.concatenate([
        matmul(h, dp_rk, "tn", "dw_rkv"), matmul(h, dp_lo, "tn", "dw_lora"),
        matmul(h, dp_qkv, "tn", "dw_qkv"), matmul(h, dp_gate, "tn", "dw_gate")], axis=1)

    token = on_mixer_grads(gW)

    def norm_bwd(x_b, dh_b, dy_b, gn, tok):
        dx, dgn = jax.vjp(_rms, x_b, gn)[1](dh_b)
        return dy_b + dx + tok[0:1, 0:1], dgn

    dx1, gP["mix_norm"] = rowmap(norm_bwd, [x1, dh, dx2], [P["mix_norm"], token], [(D_MODEL, F32)],
                                 [(1, D_MODEL)], tb=256, name="mix_norm_bwd")
    dx, gP["ffn1_norm"], gW["ffn1_w_in"], gW["ffn1_w_out"] = _ffn_bwd(
        dx1, ffn1_saved, P["ffn1_norm"], W["ffn1_w_in"], W["ffn1_w_out"], "ffn1")
    return loss_cols, dx, gW, gP


N_SHARDS = 4
BIG = (("ffn1_w_in", (D_MODEL, 2 * D_FF), 1), ("ffn1_w_out", (D_FF, D_MODEL), 0),
       ("w_in", (D_MODEL, 7712), 1), ("rwkv_w2", (LORA_W, D_MODEL), 1), ("rwkv_a2", (LORA_A, D_MODEL), 1),
       ("rwkv_g2", (LORA_G, D_MODEL), 1), ("w_proj_rwkv", (D_MODEL, D_MODEL), 0),
       ("w_proj_attn", (ATTN_WIDTH, D_MODEL), 1), ("w_out", (D_MODEL, D_MODEL), 0),
       ("ffn2_w_in", (D_MODEL, 2 * D_FF), 1), ("ffn2_w_out", (D_FF, D_MODEL), 0))
SMALL = (("ffn1_norm", 1024), ("mix_norm", 1024), ("b_gate", 2048), ("rwkv_mu", 3360), ("rwkv_w0", 1024),
         ("rwkv_a0", 1024), ("rwkv_k_k", 1024), ("rwkv_k_a", 1024), ("rwkv_r_k", 1024), ("rwkv_ln_w", 1024),
         ("rwkv_ln_b", 1024), ("attn_q_norm", 64), ("attn_k_norm", 64), ("ffn2_norm", 1024))
WEIGHT_ORDER = ("ffn1_norm", "ffn1_w_in", "ffn1_w_out", "mix_norm", "w_in", "b_gate", "rwkv_mu", "rwkv_w0",
                "rwkv_w2", "rwkv_a0", "rwkv_a2", "rwkv_g2", "rwkv_k_k", "rwkv_k_a", "rwkv_r_k", "rwkv_ln_w",
                "rwkv_ln_b", "attn_q_norm", "attn_k_norm", "w_proj_rwkv", "w_proj_attn", "w_out", "ffn2_norm",
                "ffn2_w_in", "ffn2_w_out")


LORA_PARTS = ("rwkv_w2", "rwkv_a2", "rwkv_g2")
BLOCK_MAJOR = ("ffn1_w_in", "ffn2_w_in")
FIRST_FFN = ("ffn1_w_in", "ffn1_w_out")
SMALL_USED = D_MODEL + sum(n for _, n in SMALL)
SMALL_W = -(-SMALL_USED // 128) * 128


def _travel():
    out = {}
    for name, shape, axis in BIG:
        if name == LORA_PARTS[0]:
            out["lora"] = ((LORA, D_MODEL), 1)
        elif name not in LORA_PARTS:
            out[name] = (shape, axis)
    return out


def local_blocks(vals):
    out = {n: vals[n] for n in _travel() if n != "lora"}
    out["lora"] = jnp.concatenate([vals[n] for n in LORA_PARTS], axis=0)
    return out


def split_lora(t):
    return {"rwkv_w2": t[:LORA_W], "rwkv_a2": t[LORA_W:LORA_W + LORA_A], "rwkv_g2": t[LORA_W + LORA_A:]}


def blocks_to_full(name, blocks):
    shape, axis = _travel()[name]
    if name in BLOCK_MAJOR:
        return blocks
    if axis == 0:
        return blocks.reshape(shape)
    return blocks.transpose(1, 0, 2).reshape(shape)


def full_to_blocks(name, full):
    shape, axis = _travel()[name]
    if name in BLOCK_MAJOR:
        return full
    if axis == 0:
        return full.reshape(N_SHARDS, shape[0] // N_SHARDS, shape[1])
    return full.reshape(shape[0], N_SHARDS, shape[1] // N_SHARDS).transpose(1, 0, 2)


def pack_small(vals, head):
    parts = [head] + [vals[name].reshape(1, n) for name, n in SMALL]
    parts.append(jnp.zeros((1, SMALL_W - SMALL_USED), F32))
    return jnp.concatenate(parts, axis=1)


def unpack_small(vec, shapes):
    out, off = {}, D_MODEL
    for name, n in SMALL:
        out[name] = vec[:, off:off + n].reshape(shapes[name])
        off += n
    return out


def _place():
    return lax.axis_index("x"), lax.axis_index("y"), lax.axis_index("c")


def _other_chips(x, y):
    return [(1 - x, y), (x, 1 - y), (1 - x, 1 - y)]


def _remote(src, dst, send_sem, recv_sem, device):
    return pltpu.make_async_remote_copy(src_ref=src, dst_ref=dst, send_sem=send_sem, recv_sem=recv_sem,
                                        device_id=device, device_id_type=MESH)


def _half(ref, who):
    hr = ref.shape[-2] // 2
    rows = pl.ds(pl.multiple_of(who * hr, 8), hr)
    return ref.at[rows] if len(ref.shape) == 2 else ref.at[:, rows]


HBM_REF = pl.BlockSpec(memory_space=pl.ANY)
COMM_PARAMS = dict(compiler_params=pltpu.CompilerParams(has_side_effects=True))


def gather_weights(blocks):
    n = len(blocks)

    def body(*refs):
        ins, outs = refs[:n], refs[n:2 * n]
        ici_send, ici_recv, d2d_send, d2d_recv = refs[2 * n:]
        x, y, c = _place()
        me, sibling, chips = 2 * x + y, (x, y, 1 - c), _other_chips(x, y)
        first = [_remote(_half(ins[t], c), _half(outs[t].at[me], c), ici_send.at[k, t], ici_recv.at[k, t],
                         (px, py, c)) for k, (px, py) in enumerate(chips) for t in range(n)]
        for cp in first:
            cp.start()
        passed = []
        for k, (px, py) in enumerate(chips):
            for t in range(n):
                landed = _half(outs[t].at[2 * px + py], c)
                _remote(landed, landed, ici_send.at[k, t], ici_recv.at[k, t], (px, py, c)).wait_recv()
                cp = _remote(landed, landed, d2d_send.at[k, t], d2d_recv.at[k, t], sibling)
                cp.start()
                passed.append(cp)
        for k, (px, py) in enumerate(chips):
            for t in range(n):
                other = _half(outs[t].at[2 * px + py], 1 - c)
                _remote(other, other, d2d_send.at[k, t], d2d_recv.at[k, t], sibling).wait_recv()
        for cp in first + passed:
            cp.wait_send()

    res = pl.pallas_call(
        body, name="gather_weights", in_specs=[HBM_REF] * n, out_specs=[HBM_REF] * n,
        out_shape=[jax.ShapeDtypeStruct((N_SHARDS,) + b.shape, b.dtype) for b in blocks],
        scratch_shapes=[pltpu.SemaphoreType.DMA((3, n))] * 4, **COMM_PARAMS)(*blocks)
    me = 2 * lax.axis_index("x") + lax.axis_index("y")
    return [lax.dynamic_update_slice(g, b[None], (me, 0, 0)) for g, b in zip(res, blocks)]


def swap_halves(grads):
    n = len(grads)

    def body(*refs):
        ins, got = refs[:n], refs[n:2 * n]
        send_sems, recv_sems = refs[2 * n:]
        x, y, c = _place()
        give = [_remote(_half(ins[t], 1 - c), got[t], send_sems.at[t], recv_sems.at[t], (x, y, 1 - c))
                for t in range(n)]
        for cp in give:
            cp.start()
        for cp in give:
            cp.wait_recv()
        for cp in give:
            cp.wait_send()

    return pl.pallas_call(
        body, name="swap_halves", in_specs=[HBM_REF] * n, out_specs=[HBM_REF] * n,
        out_shape=[jax.ShapeDtypeStruct((g.shape[0], g.shape[1] // 2, g.shape[2]), g.dtype) for g in grads],
        scratch_shapes=[pltpu.SemaphoreType.DMA((n,))] * 2, **COMM_PARAMS)(*grads)


def scatter_partials(partials):
    n = len(partials)

    def body(*refs):
        parts, landed = refs[:n], refs[n:2 * n]
        send_sems, recv_sems = refs[2 * n:]
        sends = _scatter_copies(parts, landed, lambda k, t: send_sems.at[k, t], lambda k, t: recv_sems.at[k, t])
        for cp in sends:
            cp.start()
        for cp in sends:
            cp.wait_recv()
        for cp in sends:
            cp.wait_send()

    return pl.pallas_call(
        body, name="scatter_partials", in_specs=[HBM_REF] * n, out_specs=[HBM_REF] * n,
        out_shape=[jax.ShapeDtypeStruct((3,) + p.shape[1:], p.dtype) for p in partials],
        scratch_shapes=[pltpu.SemaphoreType.DMA((3, n))] * 2, **COMM_PARAMS)(*partials)


def join_halves(blocks):
    n = len(blocks)

    def body(*refs):
        outs = refs[n:2 * n]
        send_sems, recv_sems = refs[2 * n:]
        x, y, c = _place()
        give = [_remote(_half(outs[t], c), _half(outs[t], c), send_sems.at[t], recv_sems.at[t], (x, y, 1 - c))
                for t in range(n)]
        for cp in give:
            cp.start()
        for t in range(n):
            arriving = _half(outs[t], 1 - c)
            _remote(arriving, arriving, send_sems.at[t], recv_sems.at[t], (x, y, 1 - c)).wait_recv()
        for cp in give:
            cp.wait_send()

    return pl.pallas_call(
        body, name="join_halves", in_specs=[HBM_REF] * n, out_specs=[HBM_REF] * n,
        out_shape=[jax.ShapeDtypeStruct(b.shape, b.dtype) for b in blocks],
        input_output_aliases={t: t for t in range(n)},
        scratch_shapes=[pltpu.SemaphoreType.DMA((n,))] * 2, **COMM_PARAMS)(*blocks)


SPLIT_HBM = pl.BlockSpec(memory_space=pltpu.HBM)
SPLIT_SEM = pl.BlockSpec(memory_space=pltpu.SEMAPHORE)
SPLIT_PARAMS = dict(compiler_params=pltpu.CompilerParams(has_side_effects=pltpu.SideEffectType.DATAFLOW_SIDE_EFFECTING))


def _scatter_copies(parts, landed, send_sem, recv_sem):
    x, y, c = _place()
    return [_remote(parts[t].at[2 * px + py], landed[t].at[k], send_sem(k, t), recv_sem(k, t), (px, py, c))
            for k, (px, py) in enumerate(_other_chips(x, y)) for t in range(len(parts))]


def scatter_start(partials):
    n = len(partials)
    n_cp = 3 * n

    def body(*refs):
        parts, landed = refs[:n], refs[n:2 * n]
        sems, token = refs[2 * n:2 * n + 2 * n_cp], refs[-1]
        for cp in _scatter_copies(parts, landed, lambda k, t: sems[k * n + t], lambda k, t: sems[n_cp + k * n + t]):
            cp.start()
        token[...] = jnp.zeros_like(token)

    hbm = lambda a: pltpu.with_memory_space_constraint(a, pltpu.HBM)
    landing = [lax.empty((3,) + p.shape[1:], p.dtype) for p in partials]
    res = pl.pallas_call(
        body, name="scatter_start",
        out_shape=(*[pltpu.SemaphoreType.DMA(())] * (2 * n_cp),
                   *[pltpu.HBM(a.shape, a.dtype) for a in partials + landing], jax.ShapeDtypeStruct((8, 128), F32)),
        in_specs=[SPLIT_HBM] * (2 * n),
        out_specs=(*[SPLIT_SEM] * (2 * n_cp), *[SPLIT_HBM] * (2 * n), pl.BlockSpec(memory_space=pltpu.VMEM)),
        input_output_aliases={t: 2 * n_cp + t for t in range(2 * n)}, **SPLIT_PARAMS,
    )(*[hbm(a) for a in partials + landing])
    return (n, res[:-1]), res[-1]


def scatter_wait(handles, after):
    n, held = handles
    n_cp = 3 * n
    sems, thru = held[:2 * n_cp], held[2 * n_cp:]

    def body(*refs):
        parts, landed = refs[:n], refs[n:2 * n]
        sem_refs = refs[2 * n:2 * n + 2 * n_cp]
        for cp in _scatter_copies(parts, landed, lambda k, t: sem_refs[k * n + t],
                                  lambda k, t: sem_refs[n_cp + k * n + t]):
            cp.wait_send()
            cp.wait_recv()

    res = pl.pallas_call(
        body, name="scatter_wait", out_shape=tuple(pltpu.HBM(a.shape, a.dtype) for a in thru),
        in_specs=[SPLIT_HBM] * (2 * n) + [SPLIT_SEM] * (2 * n_cp) + [pl.BlockSpec(memory_space=pl.ANY)],
        out_specs=tuple([SPLIT_HBM] * (2 * n)), input_output_aliases={t: t for t in range(2 * n)}, **SPLIT_PARAMS,
    )(*thru, *sems, after)
    return list(res[n:])


def chip_sums(grads):
    names = list(grads)
    got = swap_halves([grads[n] for n in names])
    partials = []
    for name, theirs in zip(names, got):
        n_slot, hr, width = theirs.shape
        tb = _row_block(hr, width, 6)
        per_half = hr // tb
        mine = lambda i, s, per_half=per_half: (i // per_half) * 2 * per_half + s[0] * per_half + i % per_half
        p = placed_map(
            jnp.add,
            [(grads[name].reshape(2 * n_slot * hr, width), mine), (theirs.reshape(n_slot * hr, width), lambda i, s: i)],
            (n_slot * hr, width, BF16, lambda i, s: i), n_blocks=n_slot * per_half, tb=tb, name="chip_sum_" + name)
        partials.append(p.reshape(theirs.shape))
    return got, partials


def owner_sums(grads, got, landed):
    names = list(grads)
    blocks = []
    for name, theirs, arrived in zip(names, got, landed):
        n_slot, hr, width = theirs.shape
        tb = _row_block(hr, width, 6)
        per_half = hr // tb
        views = [(grads[name].reshape(2 * n_slot * hr, width),
                  lambda i, s, per_half=per_half: s[1] * 2 * per_half + s[0] * per_half + i),
                 (theirs.reshape(n_slot * hr, width), lambda i, s, per_half=per_half: s[1] * per_half + i)]
        views += [(arrived.reshape(3 * hr, width), functools.partial(lambda k, per_half, i, s: k * per_half + i,
                                                                     k, per_half)) for k in range(3)]
        f = lambda a, b, l0, l1, l2: (((a + b) + l0.astype(F32)) + l1.astype(F32)) + l2.astype(F32)
        blocks.append(placed_map(
            f, views,(2 * hr, width, F32, lambda i, s, per_half=per_half: s[0] * per_half + i),
            n_blocks=per_half, tb=tb, name="owner_sum_" + name))
    return dict(zip(names, join_halves(blocks)))


def adamw_block(name, w, g, m, v):
    rows, width = w.shape
    return rowmap(_adamw, [w, g, m, v], [], [(width, F32)] * 3, tb=_row_block(rows, width, 7),
                  name="adamw_" + name)


def reduce_small(vec, w, m, v):
    n_dev = 8

    def body(vec_ref, w_ref, m_ref, v_ref, loss_ref, g_ref, d_ref, m2_ref, v2_ref, slots, send_sems, recv_sems):
        x, y, c = _place()
        me = 4 * x + 2 * y + c
        slots[me] = vec_ref[...]
        flips = [(fx, fy, fc) for fx in (0, 1) for fy in (0, 1) for fc in (0, 1)][1:]
        peers = [(1 - x if fx else x, 1 - y if fy else y, 1 - c if fc else c) for fx, fy, fc in flips]
        sends = [pltpu.make_async_remote_copy(
            src_ref=vec_ref, dst_ref=slots.at[me], send_sem=send_sems.at[j], recv_sem=recv_sems.at[j],
            device_id=peer, device_id_type=MESH) for j, peer in enumerate(peers)]
        for cp in sends:
            cp.start()
        for j, (px, py, pc) in enumerate(peers):
            pltpu.make_async_remote_copy(
                src_ref=vec_ref, dst_ref=slots.at[4 * px + 2 * py + pc], send_sem=send_sems.at[j],
                recv_sem=recv_sems.at[j], device_id=(px, py, pc), device_id_type=MESH).wait_recv()
        for cp in sends:
            cp.wait_send()
        g = slots[0]
        for d in range(1, n_dev):
            g = g + slots[d]
        loss_ref[...] = jnp.sum(g[:, :D_MODEL], axis=1, keepdims=True)
        delta, m2, v2 = _adamw(w_ref[...], g, m_ref[...], v_ref[...])
        g_ref[...], d_ref[...], m2_ref[...], v2_ref[...] = g, delta, m2, v2

    vm = pl.BlockSpec(memory_space=pltpu.VMEM)
    vec_t = jax.ShapeDtypeStruct(vec.shape, F32)
    return pl.pallas_call(
        body, name="reduce_small", in_specs=[vm] * 4, out_specs=[vm] * 5,
        out_shape=[jax.ShapeDtypeStruct((1, 1), F32)] + [vec_t] * 4,
        scratch_shapes=[pltpu.VMEM((n_dev,) + vec.shape, F32), pltpu.SemaphoreType.DMA((n_dev - 1,)),
                        pltpu.SemaphoreType.DMA((n_dev - 1,))],
        compiler_params=pltpu.CompilerParams(has_side_effects=True),
    )(vec, w, m, v)


def kernel(x, ffn1_norm, ffn1_w_in, ffn1_w_out, mix_norm, w_in, b_gate, rwkv_mu, rwkv_w0, rwkv_w2, rwkv_a0, rwkv_a2, rwkv_g2, rwkv_k_k, rwkv_k_a, rwkv_r_k, rwkv_ln_w, rwkv_ln_b, attn_q_norm, attn_k_norm, w_proj_rwkv, w_proj_attn, w_out, ffn2_norm, ffn2_w_in, ffn2_w_out, loss_target, m_ffn1_norm, m_ffn1_w_in, m_ffn1_w_out, m_mix_norm, m_w_in, m_b_gate, m_rwkv_mu, m_rwkv_w0, m_rwkv_w2, m_rwkv_a0, m_rwkv_a2, m_rwkv_g2, m_rwkv_k_k, m_rwkv_k_a, m_rwkv_r_k, m_rwkv_ln_w, m_rwkv_ln_b, m_attn_q_norm, m_attn_k_norm, m_w_proj_rwkv, m_w_proj_attn, m_w_out, m_ffn2_norm, m_ffn2_w_in, m_ffn2_w_out, v_ffn1_norm, v_ffn1_w_in, v_ffn1_w_out, v_mix_norm, v_w_in, v_b_gate, v_rwkv_mu, v_rwkv_w0, v_rwkv_w2, v_rwkv_a0, v_rwkv_a2, v_rwkv_g2, v_rwkv_k_k, v_rwkv_k_a, v_rwkv_r_k, v_rwkv_ln_w, v_rwkv_ln_b, v_attn_q_norm, v_attn_k_norm, v_w_proj_rwkv, v_w_proj_attn, v_w_out, v_ffn2_norm, v_ffn2_w_in, v_ffn2_w_out):
    given = dict(locals())
    weights = {n: given[n] for n in WEIGHT_ORDER}
    mom_m = {n: given["m_" + n] for n in WEIGHT_ORDER}
    mom_v = {n: given["v_" + n] for n in WEIGHT_ORDER}
    big = [name for name, _, _ in BIG]
    shapes = {n: weights[n].shape for n in WEIGHT_ORDER}
    blocks_of = lambda d: local_blocks({n: d[n][0] for n in big})
    w_blk, m_blk, v_blk = blocks_of(weights), blocks_of(mom_m), blocks_of(mom_v)
    names = list(w_blk)

    gathered = gather_weights([w_blk[n].astype(BF16) for n in names])
    W = {n: blocks_to_full(n, g) for n, g in zip(names, gathered)}
    W.update(split_lora(W.pop("lora")))
    P = {n: weights[n].reshape(1, -1) for n, _ in SMALL}

    early = [n for n in names if n not in FIRST_FFN]
    sent = {}

    def send_early(gw):
        lora = jnp.concatenate([gw[n] for n in LORA_PARTS], axis=0)
        sent["grads"] = {n: full_to_blocks(n, lora if n == "lora" else gw[n]) for n in early}
        sent["got"], partials = chip_sums(sent["grads"])
        sent["handles"], token = scatter_start(partials)
        return token

    loss_cols, dx, gW, gP = layer_step(x[0], loss_target[0], W, P, send_early)
    landed = scatter_wait(sent["handles"], gP["ffn1_norm"])
    late = {n: full_to_blocks(n, gW[n]) for n in FIRST_FFN}
    late_got, late_partials = chip_sums(late)
    g_blk = owner_sums({**sent["grads"], **late}, list(sent["got"]) + list(late_got),
                       landed + list(scatter_partials(late_partials)))
    out_g, out_d, out_m, out_v = {}, {}, {}, {}
    for n in names:
        res = (g_blk[n], *adamw_block(n, w_blk[n], g_blk[n], m_blk[n], v_blk[n]))
        for dst, t in zip((out_g, out_d, out_m, out_v), res):
            for part, val in (split_lora(t) if n == "lora" else {n: t}).items():
                dst[part] = val.reshape(shapes[part])

    zero_head = jnp.zeros((1, D_MODEL), F32)
    vec = pack_small(gP, loss_cols)
    loss, g_s, d_s, m_s, v_s = reduce_small(
        vec, pack_small({n: weights[n] for n, _ in SMALL}, zero_head),
        pack_small({n: mom_m[n] for n, _ in SMALL}, zero_head),
        pack_small({n: mom_v[n] for n, _ in SMALL}, zero_head))
    for dst, src in ((out_g, g_s), (out_d, d_s), (out_m, m_s), (out_v, v_s)):
        dst.update(unpack_small(src, shapes))

    return (loss[0, 0], dx[None], *[out_g[n] for n in WEIGHT_ORDER], *[out_d[n] for n in WEIGHT_ORDER],
            *[out_m[n] for n in WEIGHT_ORDER], *[out_v[n] for n in WEIGHT_ORDER])
```

```python
import functools

import jax
import jax.numpy as jnp
from jax import lax
from jax.experimental import pallas as pl
from jax.experimental.pallas import tpu as pltpu

F32 = jnp.float32
BF16 = jnp.bfloat16
MESH = pl.DeviceIdType.MESH

D_MODEL = 1024
HEAD_DIM = 64
RWKV_HEADS = 16
LORA_W, LORA_A, LORA_G = 64, 64, 160
LORA = LORA_W + LORA_A + LORA_G
RKV = 3 * D_MODEL
ATTN_PAIRS = ((128, 1), (512, 4), (2048, 16))
ATTN_BLK = 128
ATTN_HPG = 4
ATTN_WIDTH = 768
GROUP_W = ATTN_HPG * HEAD_DIM
D_FF = 2816
GN_EPS = 64e-5
RMS_EPS = 1e-6
NEG_INF = -1e30
WKV_CHUNK = 64
WKV_HEADS_PER_STEP = 16

ADAM_LR, ADAM_B1, ADAM_B2, ADAM_EPS, ADAM_WD, ADAM_STEP = 0.001, 0.9, 0.999, 1e-08, 0.01, 10

V7X_VMEM_BYTES = 64 << 20
VMEM_TEMP_ALLOWANCE = 20 << 20


def _cparams(sem, block_bytes):
    limit = min(2 * block_bytes + VMEM_TEMP_ALLOWANCE, V7X_VMEM_BYTES - (6 << 20))
    return pltpu.CompilerParams(dimension_semantics=sem, vmem_limit_bytes=int(limit))


def _nbytes(shape, dtype):
    n = 1
    for s in shape:
        n *= s
    return n * jnp.dtype(dtype).itemsize


def _split_bf16(a):
    hi = a.astype(BF16)
    return hi, (a - hi.astype(F32)).astype(BF16)


def _make_dots():
    def raw(a, b, ca, cb):
        return lax.dot_general(a.astype(BF16), b.astype(BF16), (((ca,), (cb,)), ((), ())),
                               preferred_element_type=F32)

    @jax.custom_vjp
    def nn(a, b):
        return raw(a, b, 1, 0)

    @jax.custom_vjp
    def nt(a, b):
        return raw(a, b, 1, 1)

    @jax.custom_vjp
    def tn(a, b):
        return raw(a, b, 0, 0)

    nn.defvjp(lambda a, b: (raw(a, b, 1, 0), (a, b)),
              lambda res, g: (raw(g, res[1], 1, 1), raw(res[0], g, 0, 0)))
    nt.defvjp(lambda a, b: (raw(a, b, 1, 1), (a, b)),
              lambda res, g: (raw(g, res[1], 1, 0), raw(g, res[0], 0, 0)))
    tn.defvjp(lambda a, b: (raw(a, b, 0, 0), (a, b)),
              lambda res, g: (raw(res[1], g, 1, 1), raw(res[0], g, 1, 0)))
    return nn, nt, tn


def _exact_rhs_dot(x, ones, cx, co):
    hi, lo = _split_bf16(x)
    dims = (((cx,), (co,)), ((), ()))
    return (lax.dot_general(hi, ones, dims, preferred_element_type=F32)
            + lax.dot_general(lo, ones, dims, preferred_element_type=F32))


@jax.custom_vjp
def SEG(x, ones):
    return _exact_rhs_dot(x, ones, 1, 0)


SEG.defvjp(lambda x, ones: (_exact_rhs_dot(x, ones, 1, 0), ones),
           lambda ones, g: (_exact_rhs_dot(g, ones, 1, 1), jnp.zeros_like(ones)))

NN, NT, TN = _make_dots()


MM_TILE_M, MM_TILE_N, MM_TILE_K = 1408, 1408, 1536


def _pick(n, cap):
    best = None
    for t in range(128, min(n, cap) + 1, 128):
        if n % t == 0:
            best = t
    return best or n


def matmul(a, b, mode, name, *, add=None, scale=1.0, out_dtype=F32):
    if mode == "nn":
        (M, K), (K2, N) = a.shape, b.shape
    elif mode == "nt":
        (M, K), (N, K2) = a.shape, b.shape
    else:
        (K, M), (K2, N) = a.shape, b.shape
    assert K == K2, (name, a.shape, b.shape)
    tm, tn, tk = _pick(M, MM_TILE_M), _pick(N, MM_TILE_N), _pick(K, MM_TILE_K)
    nk = K // tk
    ca, cb = {"nn": (1, 0), "nt": (1, 1), "tn": (0, 0)}[mode]

    def body(*refs):
        if add is None:
            a_ref, b_ref, o_ref, acc_ref = refs
        else:
            a_ref, b_ref, add_ref, o_ref, acc_ref = refs
        k = pl.program_id(2)

        @pl.when(k == 0)
        def _():
            acc_ref[...] = jnp.zeros_like(acc_ref)

        acc_ref[...] += lax.dot_general(a_ref[...].astype(BF16), b_ref[...].astype(BF16),
                                        (((ca,), (cb,)), ((), ())), preferred_element_type=F32)

        @pl.when(k == nk - 1)
        def _():
            r = acc_ref[...] * scale
            if add is not None:
                r = add_ref[...] + r
            o_ref[...] = r.astype(o_ref.dtype)

    a_spec = (pl.BlockSpec((tk, tm), lambda i, j, k: (k, i)) if mode == "tn"
              else pl.BlockSpec((tm, tk), lambda i, j, k: (i, k)))
    b_spec = (pl.BlockSpec((tn, tk), lambda i, j, k: (j, k)) if mode == "nt"
              else pl.BlockSpec((tk, tn), lambda i, j, k: (k, j)))
    in_specs, args = [a_spec, b_spec], [a, b]
    blk = tm * tk * a.dtype.itemsize + tk * tn * b.dtype.itemsize + tm * tn * 8
    if add is not None:
        in_specs.append(pl.BlockSpec((tm, tn), lambda i, j, k: (i, j)))
        args.append(add)
        blk += tm * tn * 4
    return pl.pallas_call(
        body, name=name, grid=(M // tm, N // tn, nk),
        in_specs=in_specs, out_specs=pl.BlockSpec((tm, tn), lambda i, j, k: (i, j)),
        out_shape=jax.ShapeDtypeStruct((M, N), out_dtype),
        scratch_shapes=[pltpu.VMEM((tm, tn), F32)],
        compiler_params=_cparams(("parallel", "parallel", "arbitrary"), blk),
    )(*args)


def matmul_cs(a, w, mode, name, *, scale=1.0, out_dtype=F32):
    n_blk = N_SHARDS
    if mode == "tn":
        (K, R), Cs = a.shape, w.shape[2] // 2
        tm, tk = _pick(R, MM_TILE_M), _pick(K, 1024)
        grid = (R // tm, n_blk, K // tk)
        a_spec = pl.BlockSpec((tk, tm), lambda i, j, k: (k, i))
        w_spec = pl.BlockSpec((None, tk, Cs), lambda i, j, k: (j // 2, k, j % 2))
        o_spec = pl.BlockSpec((None, tm, Cs), lambda i, j, k: (j, i, 0))
        out_shape, acc_shape, dims = (n_blk, R, Cs), (tm, Cs), (0, 0)
        blk = tk * tm * a.dtype.itemsize + tk * Cs * w.dtype.itemsize + tm * Cs * 8
    else:
        M, (_, R, Cs) = a.shape[1], w.shape
        tm, tn = _pick(M, MM_TILE_M), _pick(R, MM_TILE_N)
        grid = (M // tm, R // tn, n_blk)
        a_spec = pl.BlockSpec((None, tm, Cs), lambda i, j, k: (k // 2, i, k % 2))
        w_spec = pl.BlockSpec((None, tn, Cs), lambda i, j, k: (k, j, 0))
        o_spec = pl.BlockSpec((tm, tn), lambda i, j, k: (i, j))
        out_shape, acc_shape, dims = (M, R), (tm, tn), (1, 1)
        blk = tm * Cs * a.dtype.itemsize + tn * Cs * w.dtype.itemsize + tm * tn * 8
    nk = grid[2]

    def body(a_ref, w_ref, o_ref, acc_ref):
        k = pl.program_id(2)

        @pl.when(k == 0)
        def _():
            acc_ref[...] = jnp.zeros_like(acc_ref)

        acc_ref[...] += lax.dot_general(a_ref[...].astype(BF16), w_ref[...].astype(BF16),
                                        (((dims[0],), (dims[1],)), ((), ())), preferred_element_type=F32)

        @pl.when(k == nk - 1)
        def _():
            o_ref[...] = (acc_ref[...] * scale).astype(o_ref.dtype)

    return pl.pallas_call(
        body, name=name, grid=grid, in_specs=[a_spec, w_spec], out_specs=o_spec,
        out_shape=jax.ShapeDtypeStruct(out_shape, out_dtype), scratch_shapes=[pltpu.VMEM(acc_shape, F32)],
        compiler_params=_cparams(("parallel", "parallel", "arbitrary"), blk),
    )(a, w)


FFN_TILE_M = 512


def _swiglu(gate, up):
    return gate * jax.nn.sigmoid(gate) * up


def ffn_in_act(h, w, name):
    (M, R), Cs, half = h.shape, w.shape[2], N_SHARDS // 2
    tm, tk = _pick(M, FFN_TILE_M), _pick(R, 1024)
    nk = R // tk

    def body(h_ref, wg_ref, wu_ref, gu_ref, act_ref, acc_ref):
        k = pl.program_id(2)

        @pl.when(k == 0)
        def _():
            acc_ref[...] = jnp.zeros_like(acc_ref)

        hb = h_ref[...].astype(BF16)
        for part, w_ref in enumerate((wg_ref, wu_ref)):
            acc_ref[part] += jnp.dot(hb, w_ref[...].astype(BF16), preferred_element_type=F32)

        @pl.when(k == nk - 1)
        def _():
            gu_ref[...] = acc_ref[...]
            act_ref[...] = _swiglu(acc_ref[0], acc_ref[1]).astype(act_ref.dtype)

    w_spec = lambda off: pl.BlockSpec((None, tk, Cs), functools.partial(lambda off, i, j, k: (j + off, k, 0), off))
    blk = tm * tk * h.dtype.itemsize + 2 * tk * Cs * w.dtype.itemsize + tm * Cs * (16 + 2)
    return pl.pallas_call(
        body, name=name, grid=(M // tm, half, nk),
        in_specs=[pl.BlockSpec((tm, tk), lambda i, j, k: (i, k)), w_spec(0), w_spec(half)],
        out_specs=[pl.BlockSpec((2, tm, Cs), lambda i, j, k: (0, i, j)), pl.BlockSpec((tm, Cs), lambda i, j, k: (i, j))],
        out_shape=[jax.ShapeDtypeStruct((2, M, half * Cs), F32), jax.ShapeDtypeStruct((M, half * Cs), BF16)],
        scratch_shapes=[pltpu.VMEM((2, tm, Cs), F32)],
        compiler_params=_cparams(("parallel", "parallel", "arbitrary"), blk),
    )(h, w, w)


def ffn_dact_dgu(dy, w_out, gu, scale, name):
    (M, D), F = dy.shape, w_out.shape[0]
    tm, tn = _pick(M, FFN_TILE_M), F // 2

    def body(dy_ref, w_ref, gu_ref, dgu_ref):
        dact = scale * lax.dot_general(dy_ref[...].astype(BF16), w_ref[...].astype(BF16),
                                       (((1,), (1,)), ((), ())), preferred_element_type=F32)
        dgate, dup = jax.vjp(_swiglu, gu_ref[0], gu_ref[1])[1](dact)
        dgu_ref[0] = dgate.astype(dgu_ref.dtype)
        dgu_ref[1] = dup.astype(dgu_ref.dtype)

    pair = pl.BlockSpec((2, tm, tn), lambda i, j: (0, i, j))
    blk = tm * D * dy.dtype.itemsize + tn * D * w_out.dtype.itemsize + 2 * tm * tn * (4 + 2)
    return pl.pallas_call(
        body, name=name, grid=(M // tm, F // tn),
        in_specs=[pl.BlockSpec((tm, D), lambda i, j: (i, 0)), pl.BlockSpec((tn, D), lambda i, j: (j, 0)), pair],
        out_specs=pair, out_shape=jax.ShapeDtypeStruct((2, M, F), BF16),
        compiler_params=_cparams(("parallel", "parallel"), blk),
    )(dy, w_out, gu)


def _row_block(n, width, n_arrays):
    cap = (V7X_VMEM_BYTES // 4) // (2 * 4 * width * n_arrays)
    best = None
    for t in range(16, min(n, cap) + 1, 16):
        if n % t == 0:
            best = t
    return best or n


def placed_map(f, ins, out, *, n_blocks, tb, name):
    def body(*refs):
        refs[-1][...] = f(*[r[...] for r in refs[:-1]]).astype(refs[-1].dtype)

    def spec(fn):
        def index(i):
            x, y, c = _place()
            return fn(i, (c, 2 * x + y)), 0
        return pl.BlockSpec((tb, width), index)

    o_rows, width, o_dtype, o_fn = out
    blk = (sum(a.dtype.itemsize for a, _ in ins) + jnp.dtype(o_dtype).itemsize) * tb * width
    return pl.pallas_call(
        body, name=name, grid=(n_blocks,), in_specs=[spec(fn) for _, fn in ins], out_specs=spec(o_fn),
        out_shape=jax.ShapeDtypeStruct((o_rows, width), o_dtype),
        compiler_params=_cparams(("parallel",), blk),
    )(*[a for a, _ in ins])


def rowmap(f, rows, params, outs, accs=(), *, tb, name):
    rows = [r if isinstance(r, tuple) else (r, r.shape[1], 0) for r in rows]
    S = rows[0][0].shape[0]
    assert S % tb == 0, (name, S, tb)
    n_in, n_out = len(rows) + len(params), len(outs)

    def body(*refs):
        res = f(*[r[...] for r in refs[:n_in]])
        res = res if isinstance(res, (tuple, list)) else (res,)
        o_refs, a_refs = refs[n_in:n_in + n_out], refs[n_in + n_out:]
        for ref, val in zip(o_refs, res[:n_out]):
            ref[...] = val.astype(ref.dtype)
        if a_refs:
            @pl.when(pl.program_id(0) == 0)
            def _():
                for ref in a_refs:
                    ref[...] = jnp.zeros_like(ref)

            for ref, val in zip(a_refs, res[n_out:]):
                ref[...] += val.astype(F32)

    in_specs = [pl.BlockSpec((tb, w), functools.partial(lambda cb, i: (i, cb), cb)) for _, w, cb in rows]
    in_specs += [pl.BlockSpec(p.shape, lambda i: (0, 0)) for p in params]
    out_specs = [pl.BlockSpec((tb, w), lambda i: (i, 0)) for w, _ in outs]
    out_specs += [pl.BlockSpec(tuple(s), lambda i: (0, 0)) for s in accs]
    out_shape = [jax.ShapeDtypeStruct((S, w), dt) for w, dt in outs]
    out_shape += [jax.ShapeDtypeStruct(tuple(s), F32) for s in accs]
    blk = sum(tb * w * a.dtype.itemsize for a, w, _ in rows) + sum(_nbytes(p.shape, p.dtype) for p in params)
    blk += sum(_nbytes((tb, w), dt) for w, dt in outs) + sum(_nbytes(s, F32) for s in accs)
    res = pl.pallas_call(
        body, name=name, grid=(S // tb,), in_specs=in_specs, out_specs=out_specs, out_shape=out_shape,
        compiler_params=_cparams(("arbitrary",) if accs else ("parallel",), blk),
    )(*[r[0] for r in rows], *[pltpu.with_memory_space_constraint(p, pltpu.HBM) for p in params])
    return res


def _rms(x, g):
    return x * lax.rsqrt(jnp.mean(x * x, axis=-1, keepdims=True) + RMS_EPS) * g


def _softplus(z):
    return jnp.maximum(z, 0.0) + jnp.log(1.0 + jnp.exp(-jnp.abs(z)))


def _rwkv_pre(xrk, xlo, w0, w2p, a0, a2p, g2p, k_k, k_a, seg, seg_t):
    k = xrk[:, D_MODEL:2 * D_MODEL]
    w = -_softplus(-(w0 + NN(jnp.tanh(xlo), w2p))) - 0.5
    log_decay = -jnp.exp(w)
    a = jax.nn.sigmoid(a0 + NN(xlo, a2p))
    g = NN(jax.nn.sigmoid(xlo), g2p)
    kk = k * k_k
    norm = jnp.maximum(jnp.sqrt(SEG(kk * kk, seg)), 1e-12)
    kk = kk * SEG(1.0 / norm, seg_t)
    k_mod = k * (1.0 + (a - 1.0) * k_a)
    return log_decay, k_mod, -kk, kk * a, g


def _rwkv_post(wkv, r, k_mod, v, g, r_k, ln_w, ln_b, seg, seg_t):
    inv_n = 1.0 / HEAD_DIM
    mean = SEG(wkv, seg) * inv_n
    cen = wkv - SEG(mean, seg_t)
    var = SEG(cen * cen, seg) * inv_n
    y = cen * SEG(lax.rsqrt(var + GN_EPS), seg_t) * ln_w + ln_b
    bonus = SEG(SEG(r * k_mod * r_k, seg), seg_t) * v
    return (y + bonus) * g


def _qk_norm(q, k, q_gain, k_gain, seg, seg_t, tile_t):
    def norm(x, gain):
        mean_sq = SEG(x * x, seg) * (1.0 / HEAD_DIM)
        return x * SEG(lax.rsqrt(mean_sq + RMS_EPS), seg_t) * SEG(gain, tile_t)

    return norm(q, q_gain) * (HEAD_DIM ** -0.5), norm(k, k_gain)


def _gate_merge(pgate, pa, pb, b_gate):
    sg = jax.nn.sigmoid(pgate + b_gate)
    return sg[:, :D_MODEL] * pa + sg[:, D_MODEL:] * pb


def _group_combine(o, lse):
    ls = [lse[:, GROUP_W * i:GROUP_W * (i + 1)] for i in range(3)]
    m = jnp.maximum(jnp.maximum(ls[0], ls[1]), ls[2])
    es = [jnp.exp(l - m) for l in ls]
    den = es[0] + es[1] + es[2]
    return jnp.concatenate([o[:, GROUP_W * i:GROUP_W * (i + 1)] * (es[i] / den) for i in range(3)], axis=1)


def _each(f, *xs):
    return tuple(f(*args) for args in zip(*xs))


def _attn_block(q, kc, kp, vc, vp, first):
    qi = lax.broadcasted_iota(jnp.int32, (ATTN_BLK, ATTN_BLK), 0)
    kj = lax.broadcasted_iota(jnp.int32, (ATTN_BLK, ATTN_BLK), 1)
    own = kj <= qi
    s_c = _each(lambda a, b: jnp.where(own, NT(a, b), NEG_INF), q, kc)
    s_p = _each(lambda a, b, f: jnp.where((kj >= qi) & (f < 0.5), NT(a, b), NEG_INF), q, kp, first)
    row_max = lambda s: jnp.max(s, axis=-1, keepdims=True)
    row_sum = lambda s: jnp.sum(s, axis=-1, keepdims=True)
    m = _each(lambda c_, p_: jnp.maximum(row_max(c_), row_max(p_)), s_c, s_p)
    e_c, e_p = _each(lambda s, m_: jnp.exp(s - m_), s_c, m), _each(lambda s, m_: jnp.exp(s - m_), s_p, m)
    den = _each(lambda c_, p_: row_sum(c_) + row_sum(p_), e_c, e_p)
    inv = _each(lambda d_: 1.0 / d_, den)
    o = _each(lambda ec, ep, i_, vc_, vp_: (NN(ec, vc_) + NN(ep, vp_)) * i_, e_c, e_p, inv, vc, vp)
    lse = _each(lambda m_, d_: jnp.broadcast_to(m_ + jnp.log(d_), (ATTN_BLK, HEAD_DIM)), m, den)
    return o, lse


def _attn_pair(q, k, k_before, v, v_before, first):
    n = len(q[0])
    o, lse = _attn_block(q[0] + q[1], k[0] + k[1], k_before + k[0], v[0] + v[1], v_before + v[0],
                         (first[0],) * n + (first[1],) * n)
    return (o[:n], o[n:]), (lse[:n], lse[n:])


TRI_SEED = 8


def _tri_inverse(n):
    c = n[0].shape[0]
    row = lax.broadcasted_iota(jnp.int32, (c, c), 0)
    col = lax.broadcasted_iota(jnp.int32, (c, c), 1)
    same_block = lambda size: (row >> (size.bit_length() - 1)) == (col >> (size.bit_length() - 1))
    seed = same_block(TRI_SEED)
    p = _each(lambda m: jnp.where(seed, m, 0.0), n)
    t, span = _each(lambda m: (row == col).astype(F32) + m, p), 2
    while span < TRI_SEED:
        p = _each(NN, p, p)
        t = _each(lambda t_, p_: t_ + NN(t_, p_), t, p)
        span *= 2
    size = TRI_SEED
    while size < c:
        joins = same_block(2 * size) & jnp.logical_not(same_block(size))
        t = _each(lambda t_, m: t_ + NN(NN(t_, jnp.where(joins, m, 0.0)), t_), t, n)
        size *= 2
    return t


@jax.custom_vjp
def _tri_solve(n, rhs, t):
    return _each(NN, t, rhs)


def _tri_solve_fwd(n, rhs, t):
    x = _each(NN, t, rhs)
    return x, (t, x)


def _tri_solve_bwd(res, dx):
    t, x = res
    drhs = _each(TN, t, dx)
    return _each(NT, drhs, x), drhs, _each(jnp.zeros_like, t)


_tri_solve.defvjp(_tri_solve_fwd, _tri_solve_bwd)


def _lower_ones(c):
    row = lax.broadcasted_iota(jnp.int32, (c, c), 0)
    col = lax.broadcasted_iota(jnp.int32, (c, c), 1)
    return (row >= col).astype(BF16)


def _ones_dot(ones, x, contract):
    hi, lo = _split_bf16(x)
    dims = (((contract,), (0,)), ((), ()))
    return (lax.dot_general(ones, hi, dims, preferred_element_type=F32)
            + lax.dot_general(ones, lo, dims, preferred_element_type=F32))


@jax.custom_vjp
def _cumsum_rows(x):
    return _ones_dot(_lower_ones(x.shape[0]), x, 1)


_cumsum_rows.defvjp(lambda x: (_ones_dot(_lower_ones(x.shape[0]), x, 1), None),
                    lambda _, g: (_ones_dot(_lower_ones(g.shape[0]), g, 0),))


def _wkv_chunk(s0, r, lw, k, v, a, b, t_inv=None):
    c = r[0].shape[0]
    row = lax.broadcasted_iota(jnp.int32, (c, c), 0)
    col = lax.broadcasted_iota(jnp.int32, (c, c), 1)
    strict, incl = row > col, row >= col
    cat = lambda p, q: jnp.concatenate([p, q], axis=0)
    cum = _each(_cumsum_rows, lw)
    e_neg = _each(lambda c_: jnp.exp(-c_), cum)
    ar = _each(lambda a_, r_, c_, l_: cat(a_ * jnp.exp(c_ - l_), r_ * jnp.exp(c_)), a, r, cum, lw)
    b_t, k_t = _each(jnp.multiply, b, e_neg), _each(jnp.multiply, k, e_neg)
    p_b, p_k, p_s = _each(NT, ar, b_t), _each(NT, ar, k_t), _each(NT, ar, s0)
    n_ab = _each(lambda p: jnp.where(strict, p[:c], 0.0), p_b)
    m_rb = _each(lambda p: jnp.where(incl, p[c:], 0.0), p_b)
    n_ak = _each(lambda p: jnp.where(strict, p[:c], 0.0), p_k)
    m_rk = _each(lambda p: jnp.where(incl, p[c:], 0.0), p_k)
    if t_inv is None:
        t_inv = _tri_inverse(n_ab)
    u = _tri_solve(n_ab, _each(lambda p, n_, v_: p[:c] + NN(n_, v_), p_s, n_ak, v), t_inv)
    y = _each(lambda p, mb, u_, mk, v_: p[c:] + NN(mb, u_) + NN(mk, v_), p_s, m_rb, u, m_rk, v)
    g_end = _each(lambda l_: jnp.exp(jnp.sum(l_, axis=0, keepdims=True)), lw)
    s1 = _each(lambda s_, g_, u_, v_, b_, k_: s_ * g_ + TN(cat(u_, v_), cat(b_, k_) * g_),
               s0, g_end, u, v, b_t, k_t)
    return y, s1, t_inv


def _adamw(w, g, m, v):
    m = ADAM_B1 * m + (1.0 - ADAM_B1) * g
    v = ADAM_B2 * v + (1.0 - ADAM_B2) * jnp.square(g)
    m_hat = m / (1.0 - ADAM_B1 ** ADAM_STEP)
    v_hat = v / (1.0 - ADAM_B2 ** ADAM_STEP)
    delta = -ADAM_LR * (m_hat / (jnp.sqrt(v_hat) + ADAM_EPS) + ADAM_WD * w)
    return delta, m, v


def token_shift_fwd(p, mu, *, tb, name):
    S, W = p.shape
    hb = tb // 8

    def body(p_ref, halo_ref, mu_ref, o_ref):
        i = pl.program_id(0)
        x = p_ref[...]
        before = halo_ref[7:8, :] * (i > 0).astype(F32)
        row = lax.broadcasted_iota(jnp.int32, (tb, W), 0)
        prev = jnp.where(row == 0, before, pltpu.roll(x, 1, 0))
        o_ref[...] = x + (prev - x) * mu_ref[...]

    blk = (2 * tb + 8) * W * 4
    return pl.pallas_call(
        body, name=name, grid=(S // tb,),
        in_specs=[pl.BlockSpec((tb, W), lambda i: (i, 0)),
                  pl.BlockSpec((8, W), lambda i: (jnp.maximum(i * hb - 1, 0), 0)),
                  pl.BlockSpec((1, W), lambda i: (0, 0))],
        out_specs=pl.BlockSpec((tb, W), lambda i: (i, 0)),
        out_shape=jax.ShapeDtypeStruct((S, W), F32),
        compiler_params=_cparams(("parallel",), blk),
    )(p, p, mu)


def token_shift_bwd(dxs, p, mu, *, tb, name):
    S, W = p.shape
    hb, nb = tb // 8, S // tb

    def body(d_ref, dnext_ref, p_ref, halo_ref, mu_ref, dp_ref, dmu_ref):
        i = pl.program_id(0)
        d, x, mu_v = d_ref[...], p_ref[...], mu_ref[...]
        row = lax.broadcasted_iota(jnp.int32, (tb, W), 0)
        before = halo_ref[7:8, :] * (i > 0).astype(F32)
        prev = jnp.where(row == 0, before, pltpu.roll(x, 1, 0))
        t = d * mu_v
        after = dnext_ref[0:1, :] * mu_v * (i < nb - 1).astype(F32)
        nxt = jnp.where(row == tb - 1, after, pltpu.roll(t, tb - 1, 0))
        dp_ref[...] = (d - t + nxt).astype(dp_ref.dtype)

        @pl.when(i == 0)
        def _():
            dmu_ref[...] = jnp.zeros_like(dmu_ref)

        dmu_ref[...] += jnp.sum(d * (prev - x), axis=0, keepdims=True)

    blk = (3 * tb + 16) * W * 4
    return pl.pallas_call(
        body, name=name, grid=(nb,),
        in_specs=[pl.BlockSpec((tb, W), lambda i: (i, 0)),
                  pl.BlockSpec((8, W), lambda i: (jnp.minimum((i + 1) * hb, S // 8 - 1), 0)),
                  pl.BlockSpec((tb, W), lambda i: (i, 0)),
                  pl.BlockSpec((8, W), lambda i: (jnp.maximum(i * hb - 1, 0), 0)),
                  pl.BlockSpec((1, W), lambda i: (0, 0))],
        out_specs=[pl.BlockSpec((tb, W), lambda i: (i, 0)), pl.BlockSpec((1, W), lambda i: (0, 0))],
        out_shape=[jax.ShapeDtypeStruct((S, W), BF16), jax.ShapeDtypeStruct((1, W), F32)],
        compiler_params=_cparams(("arbitrary",), blk),
    )(dxs, dxs, p, p, mu)


def _head_cols(h):
    return pl.ds(h * HEAD_DIM, HEAD_DIM)


def wkv_fwd(xs_rk, lw, k, a, b):
    S = lw.shape[0]
    C, nc, G, N = WKV_CHUNK, S // WKV_CHUNK, WKV_HEADS_PER_STEP, HEAD_DIM

    def body(r_ref, lw_ref, k_ref, v_ref, a_ref, b_ref, y_ref, st_ref, ti_ref, state):
        @pl.when(pl.program_id(1) == 0)
        def _():
            state[...] = jnp.zeros_like(state)

        heads = lambda ref: tuple(ref[:, _head_cols(h)] for h in range(G))
        s0 = tuple(state[h] for h in range(G))
        y, s1, t_inv = _wkv_chunk(s0, heads(r_ref), heads(lw_ref), heads(k_ref), heads(v_ref), heads(a_ref),
                                  heads(b_ref))
        for h in range(G):
            st_ref[h] = s0[h]
            ti_ref[h] = t_inv[h]
            y_ref[:, _head_cols(h)] = y[h]
            state[h] = s1[h]

    W = G * N
    seq = lambda j: pl.BlockSpec((C, W), functools.partial(lambda j, g, c: (c, j + g), j))
    per = D_MODEL // W
    per_chunk = pl.BlockSpec((None, G, N, N), lambda g, c: (c, g, 0, 0))
    return pl.pallas_call(
        body, name="wkv_fwd", grid=(RWKV_HEADS // G, nc),
        in_specs=[seq(0), seq(0), seq(0), seq(2 * per), seq(0), seq(0)],
        out_specs=[seq(0), per_chunk, per_chunk],
        out_shape=[jax.ShapeDtypeStruct((S, D_MODEL), F32)] + [jax.ShapeDtypeStruct((nc, RWKV_HEADS, N, N), F32)] * 2,
        scratch_shapes=[pltpu.VMEM((G, N, N), F32)],
        compiler_params=_cparams(("parallel", "arbitrary"), 8 * C * W * 4 + 3 * G * N * N * 4),
    )(xs_rk, lw, k, xs_rk, a, b)


def wkv_bwd(xs_rk, lw, k, a, b, states, t_invs, dy):
    S = lw.shape[0]
    C, nc, G, N = WKV_CHUNK, S // WKV_CHUNK, WKV_HEADS_PER_STEP, HEAD_DIM

    def body(r_ref, lw_ref, k_ref, v_ref, a_ref, b_ref, st_ref, ti_ref, dy_ref,
             dr_ref, dlw_ref, dk_ref, dv_ref, da_ref, db_ref, dstate):
        @pl.when(pl.program_id(1) == 0)
        def _():
            dstate[...] = jnp.zeros_like(dstate)

        heads = lambda ref: tuple(ref[:, _head_cols(h)] for h in range(G))
        t_inv = tuple(ti_ref[h] for h in range(G))
        chunk = lambda *args: _wkv_chunk(*args, t_inv)[:2]
        _, pull = jax.vjp(chunk, tuple(st_ref[h] for h in range(G)), heads(r_ref), heads(lw_ref),
                          heads(k_ref), heads(v_ref), heads(a_ref), heads(b_ref))
        ds0, *grads = pull((heads(dy_ref), tuple(dstate[h] for h in range(G))))
        for h in range(G):
            dstate[h] = ds0[h]
            for ref, grad in zip((dr_ref, dlw_ref, dk_ref, dv_ref, da_ref, db_ref), grads):
                ref[:, _head_cols(h)] = grad[h]

    W = G * N
    seq = lambda j: pl.BlockSpec((C, W), functools.partial(lambda j, g, c: (nc - 1 - c, j + g), j))
    per = D_MODEL // W
    st = pl.BlockSpec((None, G, N, N), lambda g, c: (nc - 1 - c, g, 0, 0))
    return pl.pallas_call(
        body, name="wkv_bwd", grid=(RWKV_HEADS // G, nc),
        in_specs=[seq(0), seq(0), seq(0), seq(2 * per), seq(0), seq(0), st, st, seq(0)],
        out_specs=[seq(0)] * 6, out_shape=[jax.ShapeDtypeStruct((S, D_MODEL), F32)] * 6,
        scratch_shapes=[pltpu.VMEM((G, N, N), F32)],
        compiler_params=_cparams(("parallel", "arbitrary"), 14 * C * W * 4 + 3 * G * N * N * 4),
    )(xs_rk, lw, k, xs_rk, a, b, states, t_invs, dy)


def _first_flag(i, seq_len):
    per_group = seq_len // ATTN_BLK
    g = i // per_group
    per_seq = [seq_len // d // ATTN_BLK for _, d in ATTN_PAIRS]
    n = jnp.where(g == 0, per_seq[0], jnp.where(g == 1, per_seq[1], per_seq[2]))
    return (lax.rem(i, n) == 0).astype(F32)


def _block_rows(half):
    return pl.ds(half * ATTN_BLK, ATTN_BLK)


def _block_heads(ref, half):
    return tuple(ref[_block_rows(half), _head_cols(h)] for h in range(ATTN_HPG))


def _pair_heads(ref):
    return _block_heads(ref, 0), _block_heads(ref, 1)


def attn_fwd(q, k, v, seq_len):
    R, N = q.shape
    n_pairs = R // (2 * ATTN_BLK)

    def body(q_ref, k_ref, kb_ref, v_ref, vb_ref, o_ref, lse_ref):
        pair = pl.program_id(0)
        first = (_first_flag(2 * pair, seq_len), _first_flag(2 * pair + 1, seq_len))
        o, lse = _attn_pair(_pair_heads(q_ref), _pair_heads(k_ref), _block_heads(kb_ref, 0), _pair_heads(v_ref),
                            _block_heads(vb_ref, 0), first)
        for half in range(2):
            for h in range(ATTN_HPG):
                o_ref[_block_rows(half), _head_cols(h)] = o[half][h]
                lse_ref[_block_rows(half), _head_cols(h)] = lse[half][h]

    cur = pl.BlockSpec((2 * ATTN_BLK, N), lambda i: (i, 0))
    prv = pl.BlockSpec((ATTN_BLK, N), lambda i: (jnp.maximum(2 * i - 1, 0), 0))
    return pl.pallas_call(
        body, name="attn_fwd", grid=(n_pairs,), in_specs=[cur, cur, prv, cur, prv],
        out_specs=[cur, cur], out_shape=[jax.ShapeDtypeStruct((R, N), F32)] * 2,
        compiler_params=_cparams(("parallel",), 12 * ATTN_BLK * N * 4),
    )(q, k, k, v, v)


def attn_bwd(q, k, v, do, dlse, seq_len):
    R, N = q.shape
    n_pairs = R // (2 * ATTN_BLK)

    def body(q_ref, k_ref, kb_ref, v_ref, vb_ref, do_ref, dl_ref, dq_ref, dk_ref, dv_ref, carry_k, carry_v):
        step = pl.program_id(0)
        pair = n_pairs - 1 - step
        first = (_first_flag(2 * pair, seq_len), _first_flag(2 * pair + 1, seq_len))

        @pl.when(step == 0)
        def _():
            carry_k[...] = jnp.zeros_like(carry_k)
            carry_v[...] = jnp.zeros_like(carry_v)

        _, pull = jax.vjp(functools.partial(_attn_pair, first=first), _pair_heads(q_ref), _pair_heads(k_ref),
                          _block_heads(kb_ref, 0), _pair_heads(v_ref), _block_heads(vb_ref, 0))
        dq, dk, dk_before, dv, dv_before = pull((_pair_heads(do_ref), _pair_heads(dl_ref)))
        old_k, old_v = _block_heads(carry_k, 0), _block_heads(carry_v, 0)
        for h in range(ATTN_HPG):
            cols = _head_cols(h)
            for half in range(2):
                dq_ref[_block_rows(half), cols] = dq[half][h]
            dk_ref[_block_rows(0), cols] = dk[0][h]
            dv_ref[_block_rows(0), cols] = dv[0][h]
            dk_ref[_block_rows(1), cols] = dk[1][h] + old_k[h]
            dv_ref[_block_rows(1), cols] = dv[1][h] + old_v[h]
            carry_k[:, cols] = dk_before[h]
            carry_v[:, cols] = dv_before[h]

    cur = pl.BlockSpec((2 * ATTN_BLK, N), lambda i: (n_pairs - 1 - i, 0))
    prv = pl.BlockSpec((ATTN_BLK, N), lambda i: (jnp.maximum(2 * (n_pairs - 1 - i) - 1, 0), 0))
    return pl.pallas_call(
        body, name="attn_bwd", grid=(n_pairs,), in_specs=[cur, cur, prv, cur, prv, cur, cur],
        out_specs=[cur, cur, cur], out_shape=[jax.ShapeDtypeStruct((R, N), F32)] * 3,
        scratch_shapes=[pltpu.VMEM((ATTN_BLK, N), F32)] * 2,
        compiler_params=_cparams(("arbitrary",), 22 * ATTN_BLK * N * 4),
    )(q, k, k, v, v, do, dlse)


def to_subsequences(t):
    S = t.shape[0]
    parts = []
    for gi, (_, d) in enumerate(ATTN_PAIRS):
        tg = t[:, GROUP_W * gi:GROUP_W * (gi + 1)].reshape(S // d, d, GROUP_W)
        parts.append(tg.transpose(1, 0, 2).reshape(S, GROUP_W))
    return jnp.concatenate(parts, axis=0)


def from_subsequences(u, S):
    parts = []
    for gi, (_, d) in enumerate(ATTN_PAIRS):
        ug = u[S * gi:S * (gi + 1)].reshape(d, S // d, GROUP_W)
        parts.append(ug.transpose(1, 0, 2).reshape(S, GROUP_W))
    return jnp.concatenate(parts, axis=1)


def _ffn_fwd(x, norm, w_in, w_out, tag):
    h = rowmap(_rms, [x], [norm], [(D_MODEL, BF16)], tb=512, name=tag + "_norm")[0]
    gu, act = ffn_in_act(h, w_in, tag + "_in")
    y = matmul(act, w_out, "nn", tag + "_out", add=x, scale=0.5)
    return y, (x, h, gu, act)


def _ffn_bwd(dy, saved, norm, w_in, w_out, tag):
    x, h, gu, act = saved
    dw_out = matmul(act, dy, "tn", tag + "_dwout", scale=0.5)
    dgu = ffn_dact_dgu(dy, w_out, gu, 0.5, tag + "_dgu")
    dh = matmul_cs(dgu, w_in, "nt", tag + "_dh")
    dw_in = matmul_cs(h, dgu, "tn", tag + "_dwin")

    def norm_bwd(x_b, dh_b, dy_b, g):
        dx, dg = jax.vjp(_rms, x_b, g)[1](dh_b)
        return dy_b + dx, dg

    dx, dnorm = rowmap(norm_bwd, [x, dh, dy], [norm], [(D_MODEL, F32)], [(1, D_MODEL)], tb=256,
                       name=tag + "_dnorm")
    return dx, dnorm, dw_in, dw_out


def layer_step(x, tgt, W, P, on_mixer_grads):
    S = x.shape[0]
    head_of = lambda n: jnp.arange(n)[:, None] // HEAD_DIM == jnp.arange(n // HEAD_DIM)[None, :]
    seg, seg_a = head_of(D_MODEL).astype(BF16), head_of(ATTN_WIDTH).astype(BF16)
    seg_t, seg_a_t = seg.T, seg_a.T
    tile_t = (jnp.arange(HEAD_DIM)[:, None] == jnp.arange(ATTN_WIDTH)[None, :] % HEAD_DIM).astype(BF16)
    qk_params = [P["attn_q_norm"], P["attn_k_norm"], seg_a, seg_a_t, tile_t]
    w_rkv, w_lora = W["w_in"][:, :RKV], W["w_in"][:, RKV:RKV + LORA]
    w_qkv = W["w_in"][:, RKV + LORA:RKV + LORA + 3 * ATTN_WIDTH]
    w_gate = W["w_in"][:, RKV + LORA + 3 * ATTN_WIDTH:]
    mu_rk, mu_lo = P["rwkv_mu"][:, :RKV], P["rwkv_mu"][:, RKV:]
    zeros = lambda n: jnp.zeros((n, D_MODEL), F32)
    w2p = jnp.concatenate([W["rwkv_w2"], zeros(LORA - LORA_W)], axis=0)
    a2p = jnp.concatenate([zeros(LORA_W), W["rwkv_a2"], zeros(LORA_G)], axis=0)
    g2p = jnp.concatenate([zeros(LORA_W + LORA_A), W["rwkv_g2"]], axis=0)
    pre_params = [P["rwkv_w0"], w2p, P["rwkv_a0"], a2p, g2p, P["rwkv_k_k"], P["rwkv_k_a"], seg, seg_t]
    post_params = [P["rwkv_r_k"], P["rwkv_ln_w"], P["rwkv_ln_b"], seg, seg_t]
    col = lambda arr, j: (arr, D_MODEL, j)

    x1, ffn1_saved = _ffn_fwd(x, P["ffn1_norm"], W["ffn1_w_in"], W["ffn1_w_out"], "ffn1")
    h = rowmap(_rms, [x1], [P["mix_norm"]], [(D_MODEL, BF16)], tb=512, name="mix_norm")[0]
    p_rk = matmul(h, w_rkv, "nn", "proj_rkv")
    p_lo = matmul(h, w_lora, "nn", "proj_lora")
    p_qkv = matmul(h, w_qkv, "nn", "proj_qkv")
    p_gate = matmul(h, w_gate, "nn", "proj_gate")
    xs_rk = token_shift_fwd(p_rk, mu_rk, tb=256, name="shift_rk")
    xs_lo = token_shift_fwd(p_lo, mu_lo, tb=256, name="shift_lora")
    lw, k_mod, a_neg, b_kk, g = rowmap(
        _rwkv_pre, [xs_rk, xs_lo], pre_params, [(D_MODEL, F32)] * 5, tb=256, name="rwkv_pre")
    wkv, states, t_invs = wkv_fwd(xs_rk, lw, k_mod, a_neg, b_kk)
    post_rows = [wkv, col(xs_rk, 0), k_mod, col(xs_rk, 2), g]
    y_a = rowmap(_rwkv_post, post_rows, post_params, [(D_MODEL, BF16)], tb=256, name="rwkv_post")[0]

    qk_rows = [(p_qkv, ATTN_WIDTH, 0), (p_qkv, ATTN_WIDTH, 1)]
    qn, kn = rowmap(_qk_norm, qk_rows, qk_params, [(ATTN_WIDTH, F32)] * 2, tb=256, name="qk_norm")
    q_s, k_s, v_s = to_subsequences(qn), to_subsequences(kn), to_subsequences(p_qkv[:, 2 * ATTN_WIDTH:])
    o_s, lse_s = attn_fwd(q_s, k_s, v_s, S)
    o, lse = from_subsequences(o_s, S), from_subsequences(lse_s, S)
    y_b = rowmap(_group_combine, [o, lse], [], [(ATTN_WIDTH, BF16)], tb=512, name="attn_combine")[0]

    pa = matmul(y_a, W["w_proj_rwkv"], "nn", "proj_a")
    pb = matmul(y_b, W["w_proj_attn"], "nn", "proj_b")
    merged = rowmap(_gate_merge, [p_gate, pa, pb], [P["b_gate"]], [(D_MODEL, BF16)], tb=256, name="merge")[0]
    x2 = matmul(merged, W["w_out"], "nn", "mix_out", add=x1)
    x3, ffn2_saved = _ffn_fwd(x2, P["ffn2_norm"], W["ffn2_w_in"], W["ffn2_w_out"], "ffn2")

    def loss_head(y_b_, t_b):
        err = y_b_ - t_b
        return err * (1.0 / D_MODEL), (0.5 / D_MODEL) * jnp.sum(err * err, axis=0, keepdims=True)

    dx3, loss_cols = rowmap(loss_head, [x3, tgt], [], [(D_MODEL, F32)], [(1, D_MODEL)], tb=512, name="loss")

    gW, gP = {}, {}
    dx2, gP["ffn2_norm"], gW["ffn2_w_in"], gW["ffn2_w_out"] = _ffn_bwd(
        dx3, ffn2_saved, P["ffn2_norm"], W["ffn2_w_in"], W["ffn2_w_out"], "ffn2")

    dmerged = matmul(dx2, W["w_out"], "nt", "d_merged")
    gW["w_out"] = matmul(merged, dx2, "tn", "dw_out")

    def merge_bwd(pg, pa_b, pb_b, dm, bg):
        return jax.vjp(_gate_merge, pg, pa_b, pb_b, bg)[1](dm)

    dp_gate, dpa, dpb, gP["b_gate"] = rowmap(
        merge_bwd, [p_gate, pa, pb, dmerged], [P["b_gate"]],
        [(2 * D_MODEL, BF16), (D_MODEL, BF16), (D_MODEL, BF16)], [(1, 2 * D_MODEL)], tb=256, name="merge_bwd")
    dy_a = matmul(dpa, W["w_proj_rwkv"], "nt", "d_ya")
    gW["w_proj_rwkv"] = matmul(y_a, dpa, "tn", "dw_proj_a")
    dy_b = matmul(dpb, W["w_proj_attn"], "nt", "d_yb")
    gW["w_proj_attn"] = matmul(y_b, dpb, "tn", "dw_proj_b")

    def combine_bwd(o_b, l_b, d_b):
        return jax.vjp(_group_combine, o_b, l_b)[1](d_b)

    do, dlse = rowmap(combine_bwd, [o, lse, dy_b], [], [(ATTN_WIDTH, F32)] * 2, tb=256, name="attn_combine_bwd")
    dq_s, dk_s, dv_s = attn_bwd(q_s, k_s, v_s, to_subsequences(do), to_subsequences(dlse), S)

    def qk_norm_bwd(q_b, k_b, dqn_b, dkn_b, dv_b, qg, kg, sg, sgt, tl):
        f = lambda *a: _qk_norm(*a, sg, sgt, tl)
        dq, dk, dqg, dkg = jax.vjp(f, q_b, k_b, qg, kg)[1]((dqn_b, dkn_b))
        return jnp.concatenate([dq, dk, dv_b], axis=1), dqg, dkg

    dp_qkv, gP["attn_q_norm"], gP["attn_k_norm"] = rowmap(
        qk_norm_bwd, qk_rows + [from_subsequences(t, S) for t in (dq_s, dk_s, dv_s)], qk_params,
        [(3 * ATTN_WIDTH, BF16)], [(1, HEAD_DIM)] * 2, tb=256, name="qk_norm_bwd")

    def post_bwd(wkv_b, r_b, k_b, v_b, g_b, d_b, r_k, ln_w, ln_b, sg, sgt):
        f = lambda *a: _rwkv_post(*a, sg, sgt)
        return jax.vjp(f, wkv_b, r_b, k_b, v_b, g_b, r_k, ln_w, ln_b)[1](d_b)

    dwkv, dr_p, dk_p, dv_p, dg, gP["rwkv_r_k"], gP["rwkv_ln_w"], gP["rwkv_ln_b"] = rowmap(
        post_bwd, post_rows + [dy_a], post_params, [(D_MODEL, F32)] * 5, [(1, D_MODEL)] * 3, tb=128,
        name="rwkv_post_bwd")
    dr_w, dlw, dk_w, dv_w, da_neg, db_kk = wkv_bwd(xs_rk, lw, k_mod, a_neg, b_kk, states, t_invs, dwkv)

    def pre_bwd(xrk_b, xlo_b, dlw_b, dkw_b, dkp_b, da_b, db_b, dg_b, drp_b, drw_b, dvp_b, dvw_b,
                w0, w2, a0, a2, g2, k_k, k_a, sg, sgt):
        f = lambda *a: _rwkv_pre(*a, sg, sgt)
        pull = jax.vjp(f, xrk_b, xlo_b, w0, w2, a0, a2, g2, k_k, k_a)[1]
        dxrk, dxlo, *dpar = pull((dlw_b, dkw_b + dkp_b, da_b, db_b, dg_b))
        direct = jnp.concatenate([drp_b + drw_b, jnp.zeros_like(drp_b), dvp_b + dvw_b], axis=1)
        return (dxrk + direct, dxlo, *dpar)

    pre_rows = [xs_rk, xs_lo, dlw, dk_w, dk_p, da_neg, db_kk, dg, dr_p, dr_w, dv_p, dv_w]
    dxs_rk, dxs_lo, gP["rwkv_w0"], dw2p, gP["rwkv_a0"], da2p, dg2p, gP["rwkv_k_k"], gP["rwkv_k_a"] = rowmap(
        pre_bwd, pre_rows, pre_params, [(RKV, F32), (LORA, F32)],
        [(1, D_MODEL), (LORA, D_MODEL), (1, D_MODEL), (LORA, D_MODEL), (LORA, D_MODEL), (1, D_MODEL), (1, D_MODEL)],
        tb=128, name="rwkv_pre_bwd")
    gW["rwkv_w2"] = dw2p[:LORA_W]
    gW["rwkv_a2"] = da2p[LORA_W:LORA_W + LORA_A]
    gW["rwkv_g2"] = dg2p[LORA_W + LORA_A:]
    dp_rk, dmu_rk = token_shift_bwd(dxs_rk, p_rk, mu_rk, tb=256, name="shift_rk_bwd")
    dp_lo, dmu_lo = token_shift_bwd(dxs_lo, p_lo, mu_lo, tb=256, name="shift_lora_bwd")
    gP["rwkv_mu"] = jnp.concatenate([dmu_rk, dmu_lo], axis=1)

    dh = matmul(dp_rk, w_rkv, "nt", "dh_rkv")
    dh = matmul(dp_lo, w_lora, "nt", "dh_lora", add=dh)
    dh = matmul(dp_qkv, w_qkv, "nt", "dh_qkv", add=dh)
    dh = matmul(dp_gate, w_gate, "nt", "dh_gate", add=dh)
    gW["w_in"] = jnp.concatenate([
        matmul(h, dp_rk, "tn", "dw_rkv"), matmul(h, dp_lo, "tn", "dw_lora"),
        matmul(h, dp_qkv, "tn", "dw_qkv"), matmul(h, dp_gate, "tn", "dw_gate")], axis=1)

    token = on_mixer_grads(gW)

    def norm_bwd(x_b, dh_b, dy_b, gn, tok):
        dx, dgn = jax.vjp(_rms, x_b, gn)[1](dh_b)
        return dy_b + dx + tok[0:1, 0:1], dgn

    dx1, gP["mix_norm"] = rowmap(norm_bwd, [x1, dh, dx2], [P["mix_norm"], token], [(D_MODEL, F32)],
                                 [(1, D_MODEL)], tb=256, name="mix_norm_bwd")
    dx, gP["ffn1_norm"], gW["ffn1_w_in"], gW["ffn1_w_out"] = _ffn_bwd(
        dx1, ffn1_saved, P["ffn1_norm"], W["ffn1_w_in"], W["ffn1_w_out"], "ffn1")
    return loss_cols, dx, gW, gP


N_SHARDS = 4
BIG = (("ffn1_w_in", (D_MODEL, 2 * D_FF), 1), ("ffn1_w_out", (D_FF, D_MODEL), 0),
       ("w_in", (D_MODEL, 7712), 1), ("rwkv_w2", (LORA_W, D_MODEL), 1), ("rwkv_a2", (LORA_A, D_MODEL), 1),
       ("rwkv_g2", (LORA_G, D_MODEL), 1), ("w_proj_rwkv", (D_MODEL, D_MODEL), 0),
       ("w_proj_attn", (ATTN_WIDTH, D_MODEL), 1), ("w_out", (D_MODEL, D_MODEL), 0),
       ("ffn2_w_in", (D_MODEL, 2 * D_FF), 1), ("ffn2_w_out", (D_FF, D_MODEL), 0))
SMALL = (("ffn1_norm", 1024), ("mix_norm", 1024), ("b_gate", 2048), ("rwkv_mu", 3360), ("rwkv_w0", 1024),
         ("rwkv_a0", 1024), ("rwkv_k_k", 1024), ("rwkv_k_a", 1024), ("rwkv_r_k", 1024), ("rwkv_ln_w", 1024),
         ("rwkv_ln_b", 1024), ("attn_q_norm", 64), ("attn_k_norm", 64), ("ffn2_norm", 1024))
WEIGHT_ORDER = ("ffn1_norm", "ffn1_w_in", "ffn1_w_out", "mix_norm", "w_in", "b_gate", "rwkv_mu", "rwkv_w0",
                "rwkv_w2", "rwkv_a0", "rwkv_a2", "rwkv_g2", "rwkv_k_k", "rwkv_k_a", "rwkv_r_k", "rwkv_ln_w",
                "rwkv_ln_b", "attn_q_norm", "attn_k_norm", "w_proj_rwkv", "w_proj_attn", "w_out", "ffn2_norm",
                "ffn2_w_in", "ffn2_w_out")


LORA_PARTS = ("rwkv_w2", "rwkv_a2", "rwkv_g2")
BLOCK_MAJOR = ("ffn1_w_in", "ffn2_w_in")
FIRST_FFN = ("ffn1_w_in", "ffn1_w_out")
SMALL_USED = D_MODEL + sum(n for _, n in SMALL)
SMALL_W = -(-SMALL_USED // 128) * 128


def _travel():
    out = {}
    for name, shape, axis in BIG:
        if name == LORA_PARTS[0]:
            out["lora"] = ((LORA, D_MODEL), 1)
        elif name not in LORA_PARTS:
            out[name] = (shape, axis)
    return out


def local_blocks(vals):
    out = {n: vals[n] for n in _travel() if n != "lora"}
    out["lora"] = jnp.concatenate([vals[n] for n in LORA_PARTS], axis=0)
    return out


def split_lora(t):
    return {"rwkv_w2": t[:LORA_W], "rwkv_a2": t[LORA_W:LORA_W + LORA_A], "rwkv_g2": t[LORA_W + LORA_A:]}


def blocks_to_full(name, blocks):
    shape, axis = _travel()[name]
    if name in BLOCK_MAJOR:
        return blocks
    if axis == 0:
        return blocks.reshape(shape)
    return blocks.transpose(1, 0, 2).reshape(shape)


def full_to_blocks(name, full):
    shape, axis = _travel()[name]
    if name in BLOCK_MAJOR:
        return full
    if axis == 0:
        return full.reshape(N_SHARDS, shape[0] // N_SHARDS, shape[1])
    return full.reshape(shape[0], N_SHARDS, shape[1] // N_SHARDS).transpose(1, 0, 2)


def pack_small(vals, head):
    parts = [head] + [vals[name].reshape(1, n) for name, n in SMALL]
    parts.append(jnp.zeros((1, SMALL_W - SMALL_USED), F32))
    return jnp.concatenate(parts, axis=1)


def unpack_small(vec, shapes):
    out, off = {}, D_MODEL
    for name, n in SMALL:
        out[name] = vec[:, off:off + n].reshape(shapes[name])
        off += n
    return out


def _place():
    return lax.axis_index("x"), lax.axis_index("y"), lax.axis_index("c")


def _other_chips(x, y):
    return [(1 - x, y), (x, 1 - y), (1 - x, 1 - y)]


def _remote(src, dst, send_sem, recv_sem, device):
    return pltpu.make_async_remote_copy(src_ref=src, dst_ref=dst, send_sem=send_sem, recv_sem=recv_sem,
                                        device_id=device, device_id_type=MESH)


def _half(ref, who):
    hr = ref.shape[-2] // 2
    rows = pl.ds(pl.multiple_of(who * hr, 8), hr)
    return ref.at[rows] if len(ref.shape) == 2 else ref.at[:, rows]


HBM_REF = pl.BlockSpec(memory_space=pl.ANY)
COMM_PARAMS = dict(compiler_params=pltpu.CompilerParams(has_side_effects=True))


def gather_weights(blocks):
    n = len(blocks)

    def body(*refs):
        ins, outs = refs[:n], refs[n:2 * n]
        ici_send, ici_recv, d2d_send, d2d_recv = refs[2 * n:]
        x, y, c = _place()
        me, sibling, chips = 2 * x + y, (x, y, 1 - c), _other_chips(x, y)
        first = [_remote(_half(ins[t], c), _half(outs[t].at[me], c), ici_send.at[k, t], ici_recv.at[k, t],
                         (px, py, c)) for k, (px, py) in enumerate(chips) for t in range(n)]
        for cp in first:
            cp.start()
        passed = []
        for k, (px, py) in enumerate(chips):
            for t in range(n):
                landed = _half(outs[t].at[2 * px + py], c)
                _remote(landed, landed, ici_send.at[k, t], ici_recv.at[k, t], (px, py, c)).wait_recv()
                cp = _remote(landed, landed, d2d_send.at[k, t], d2d_recv.at[k, t], sibling)
                cp.start()
                passed.append(cp)
        for k, (px, py) in enumerate(chips):
            for t in range(n):
                other = _half(outs[t].at[2 * px + py], 1 - c)
                _remote(other, other, d2d_send.at[k, t], d2d_recv.at[k, t], sibling).wait_recv()
        for cp in first + passed:
            cp.wait_send()

    res = pl.pallas_call(
        body, name="gather_weights", in_specs=[HBM_REF] * n, out_specs=[HBM_REF] * n,
        out_shape=[jax.ShapeDtypeStruct((N_SHARDS,) + b.shape, b.dtype) for b in blocks],
        scratch_shapes=[pltpu.SemaphoreType.DMA((3, n))] * 4, **COMM_PARAMS)(*blocks)
    me = 2 * lax.axis_index("x") + lax.axis_index("y")
    return [lax.dynamic_update_slice(g, b[None], (me, 0, 0)) for g, b in zip(res, blocks)]


def swap_halves(grads):
    n = len(grads)

    def body(*refs):
        ins, got = refs[:n], refs[n:2 * n]
        send_sems, recv_sems = refs[2 * n:]
        x, y, c = _place()
        give = [_remote(_half(ins[t], 1 - c), got[t], send_sems.at[t], recv_sems.at[t], (x, y, 1 - c))
                for t in range(n)]
        for cp in give:
            cp.start()
        for cp in give:
            cp.wait_recv()
        for cp in give:
            cp.wait_send()

    return pl.pallas_call(
        body, name="swap_halves", in_specs=[HBM_REF] * n, out_specs=[HBM_REF] * n,
        out_shape=[jax.ShapeDtypeStruct((g.shape[0], g.shape[1] // 2, g.shape[2]), g.dtype) for g in grads],
        scratch_shapes=[pltpu.SemaphoreType.DMA((n,))] * 2, **COMM_PARAMS)(*grads)


def join_halves(blocks):
    n = len(blocks)

    def body(*refs):
        outs = refs[n:2 * n]
        send_sems, recv_sems = refs[2 * n:]
        x, y, c = _place()
        give = [_remote(_half(outs[t], c), _half(outs[t], c), send_sems.at[t], recv_sems.at[t], (x, y, 1 - c))
                for t in range(n)]
        for cp in give:
            cp.start()
        for t in range(n):
            arriving = _half(outs[t], 1 - c)
            _remote(arriving, arriving, send_sems.at[t], recv_sems.at[t], (x, y, 1 - c)).wait_recv()
        for cp in give:
            cp.wait_send()

    return pl.pallas_call(
        body, name="join_halves", in_specs=[HBM_REF] * n, out_specs=[HBM_REF] * n,
        out_shape=[jax.ShapeDtypeStruct(b.shape, b.dtype) for b in blocks],
        input_output_aliases={t: t for t in range(n)},
        scratch_shapes=[pltpu.SemaphoreType.DMA((n,))] * 2, **COMM_PARAMS)(*blocks)


SPLIT_HBM = pl.BlockSpec(memory_space=pltpu.HBM)
SPLIT_SEM = pl.BlockSpec(memory_space=pltpu.SEMAPHORE)
SPLIT_PARAMS = dict(compiler_params=pltpu.CompilerParams(has_side_effects=pltpu.SideEffectType.DATAFLOW_SIDE_EFFECTING))


def _scatter_copies(parts, landed, send_sem, recv_sem):
    x, y, c = _place()
    return [_remote(parts[t].at[2 * px + py], landed[t].at[k], send_sem(k, t), recv_sem(k, t), (px, py, c))
            for k, (px, py) in enumerate(_other_chips(x, y)) for t in range(len(parts))]


def scatter_start(partials, name):
    n = len(partials)
    n_cp = 3 * n

    def body(*refs):
        parts, landed = refs[:n], refs[n:2 * n]
        sems, token = refs[2 * n:2 * n + 2 * n_cp], refs[-1]
        for cp in _scatter_copies(parts, landed, lambda k, t: sems[k * n + t], lambda k, t: sems[n_cp + k * n + t]):
            cp.start()
        token[...] = jnp.zeros_like(token)

    hbm = lambda a: pltpu.with_memory_space_constraint(a, pltpu.HBM)
    landing = [lax.empty((3,) + p.shape[1:], p.dtype) for p in partials]
    res = pl.pallas_call(
        body, name=name,
        out_shape=(*[pltpu.SemaphoreType.DMA(())] * (2 * n_cp),
                   *[pltpu.HBM(a.shape, a.dtype) for a in partials + landing], jax.ShapeDtypeStruct((8, 128), F32)),
        in_specs=[SPLIT_HBM] * (2 * n),
        out_specs=(*[SPLIT_SEM] * (2 * n_cp), *[SPLIT_HBM] * (2 * n), pl.BlockSpec(memory_space=pltpu.VMEM)),
        input_output_aliases={t: 2 * n_cp + t for t in range(2 * n)}, **SPLIT_PARAMS,
    )(*[hbm(a) for a in partials + landing])
    return (n, res[:-1]), res[-1]


def scatter_wait(handles, after, name):
    n, held = handles
    n_cp = 3 * n
    sems, thru = held[:2 * n_cp], held[2 * n_cp:]

    def body(*refs):
        parts, landed = refs[:n], refs[n:2 * n]
        sem_refs = refs[2 * n:2 * n + 2 * n_cp]
        for cp in _scatter_copies(parts, landed, lambda k, t: sem_refs[k * n + t],
                                  lambda k, t: sem_refs[n_cp + k * n + t]):
            cp.wait_send()
            cp.wait_recv()

    res = pl.pallas_call(
        body, name=name, out_shape=tuple(pltpu.HBM(a.shape, a.dtype) for a in thru),
        in_specs=[SPLIT_HBM] * (2 * n) + [SPLIT_SEM] * (2 * n_cp) + [pl.BlockSpec(memory_space=pl.ANY)],
        out_specs=tuple([SPLIT_HBM] * (2 * n)), input_output_aliases={t: t for t in range(2 * n)}, **SPLIT_PARAMS,
    )(*thru, *sems, after)
    return list(res[n:])


def chip_sums(grads):
    names = list(grads)
    got = swap_halves([grads[n] for n in names])
    partials = []
    for name, theirs in zip(names, got):
        n_slot, hr, width = theirs.shape
        tb = _row_block(hr, width, 6)
        per_half = hr // tb
        mine = lambda i, s, per_half=per_half: (i // per_half) * 2 * per_half + s[0] * per_half + i % per_half
        p = placed_map(
            jnp.add,
            [(grads[name].reshape(2 * n_slot * hr, width), mine), (theirs.reshape(n_slot * hr, width), lambda i, s: i)],
            (n_slot * hr, width, BF16, lambda i, s: i), n_blocks=n_slot * per_half, tb=tb, name="chip_sum_" + name)
        partials.append(p.reshape(theirs.shape))
    return got, partials


def owner_sums(grads, got, landed):
    names = list(grads)
    blocks = []
    for name, theirs, arrived in zip(names, got, landed):
        n_slot, hr, width = theirs.shape
        tb = _row_block(hr, width, 6)
        per_half = hr // tb
        views = [(grads[name].reshape(2 * n_slot * hr, width),
                  lambda i, s, per_half=per_half: s[1] * 2 * per_half + s[0] * per_half + i),
                 (theirs.reshape(n_slot * hr, width), lambda i, s, per_half=per_half: s[1] * per_half + i)]
        views += [(arrived.reshape(3 * hr, width), functools.partial(lambda k, per_half, i, s: k * per_half + i,
                                                                     k, per_half)) for k in range(3)]
        f = lambda a, b, l0, l1, l2: (((a + b) + l0.astype(F32)) + l1.astype(F32)) + l2.astype(F32)
        blocks.append(placed_map(
            f, views,(2 * hr, width, F32, lambda i, s, per_half=per_half: s[0] * per_half + i),
            n_blocks=per_half, tb=tb, name="owner_sum_" + name))
    return dict(zip(names, join_halves(blocks)))


def adamw_block(name, w, g, m, v):
    rows, width = w.shape
    return rowmap(_adamw, [w, g, m, v], [], [(width, F32)] * 3, tb=_row_block(rows, width, 7),
                  name="adamw_" + name)


def reduce_small(vec, w, m, v):
    n_dev = 8

    def body(vec_ref, w_ref, m_ref, v_ref, loss_ref, g_ref, d_ref, m2_ref, v2_ref, slots, send_sems, recv_sems):
        x, y, c = _place()
        me = 4 * x + 2 * y + c
        slots[me] = vec_ref[...]
        flips = [(fx, fy, fc) for fx in (0, 1) for fy in (0, 1) for fc in (0, 1)][1:]
        peers = [(1 - x if fx else x, 1 - y if fy else y, 1 - c if fc else c) for fx, fy, fc in flips]
        sends = [pltpu.make_async_remote_copy(
            src_ref=vec_ref, dst_ref=slots.at[me], send_sem=send_sems.at[j], recv_sem=recv_sems.at[j],
            device_id=peer, device_id_type=MESH) for j, peer in enumerate(peers)]
        for cp in sends:
            cp.start()
        for j, (px, py, pc) in enumerate(peers):
            pltpu.make_async_remote_copy(
                src_ref=vec_ref, dst_ref=slots.at[4 * px + 2 * py + pc], send_sem=send_sems.at[j],
                recv_sem=recv_sems.at[j], device_id=(px, py, pc), device_id_type=MESH).wait_recv()
        for cp in sends:
            cp.wait_send()
        g = slots[0]
        for d in range(1, n_dev):
            g = g + slots[d]
        loss_ref[...] = jnp.sum(g[:, :D_MODEL], axis=1, keepdims=True)
        delta, m2, v2 = _adamw(w_ref[...], g, m_ref[...], v_ref[...])
        g_ref[...], d_ref[...], m2_ref[...], v2_ref[...] = g, delta, m2, v2

    vm = pl.BlockSpec(memory_space=pltpu.VMEM)
    vec_t = jax.ShapeDtypeStruct(vec.shape, F32)
    return pl.pallas_call(
        body, name="reduce_small", in_specs=[vm] * 4, out_specs=[vm] * 5,
        out_shape=[jax.ShapeDtypeStruct((1, 1), F32)] + [vec_t] * 4,
        scratch_shapes=[pltpu.VMEM((n_dev,) + vec.shape, F32), pltpu.SemaphoreType.DMA((n_dev - 1,)),
                        pltpu.SemaphoreType.DMA((n_dev - 1,))],
        compiler_params=pltpu.CompilerParams(has_side_effects=True),
    )(vec, w, m, v)


def kernel(x, ffn1_norm, ffn1_w_in, ffn1_w_out, mix_norm, w_in, b_gate, rwkv_mu, rwkv_w0, rwkv_w2, rwkv_a0, rwkv_a2, rwkv_g2, rwkv_k_k, rwkv_k_a, rwkv_r_k, rwkv_ln_w, rwkv_ln_b, attn_q_norm, attn_k_norm, w_proj_rwkv, w_proj_attn, w_out, ffn2_norm, ffn2_w_in, ffn2_w_out, loss_target, m_ffn1_norm, m_ffn1_w_in, m_ffn1_w_out, m_mix_norm, m_w_in, m_b_gate, m_rwkv_mu, m_rwkv_w0, m_rwkv_w2, m_rwkv_a0, m_rwkv_a2, m_rwkv_g2, m_rwkv_k_k, m_rwkv_k_a, m_rwkv_r_k, m_rwkv_ln_w, m_rwkv_ln_b, m_attn_q_norm, m_attn_k_norm, m_w_proj_rwkv, m_w_proj_attn, m_w_out, m_ffn2_norm, m_ffn2_w_in, m_ffn2_w_out, v_ffn1_norm, v_ffn1_w_in, v_ffn1_w_out, v_mix_norm, v_w_in, v_b_gate, v_rwkv_mu, v_rwkv_w0, v_rwkv_w2, v_rwkv_a0, v_rwkv_a2, v_rwkv_g2, v_rwkv_k_k, v_rwkv_k_a, v_rwkv_r_k, v_rwkv_ln_w, v_rwkv_ln_b, v_attn_q_norm, v_attn_k_norm, v_w_proj_rwkv, v_w_proj_attn, v_w_out, v_ffn2_norm, v_ffn2_w_in, v_ffn2_w_out):
    given = dict(locals())
    weights = {n: given[n] for n in WEIGHT_ORDER}
    mom_m = {n: given["m_" + n] for n in WEIGHT_ORDER}
    mom_v = {n: given["v_" + n] for n in WEIGHT_ORDER}
    big = [name for name, _, _ in BIG]
    shapes = {n: weights[n].shape for n in WEIGHT_ORDER}
    blocks_of = lambda d: local_blocks({n: d[n][0] for n in big})
    w_blk, m_blk, v_blk = blocks_of(weights), blocks_of(mom_m), blocks_of(mom_v)
    names = list(w_blk)

    gathered = gather_weights([w_blk[n].astype(BF16) for n in names])
    W = {n: blocks_to_full(n, g) for n, g in zip(names, gathered)}
    W.update(split_lora(W.pop("lora")))
    P = {n: weights[n].reshape(1, -1) for n, _ in SMALL}

    early = [n for n in names if n not in FIRST_FFN]
    sent = {}

    def send_early(gw):
        lora = jnp.concatenate([gw[n] for n in LORA_PARTS], axis=0)
        sent["grads"] = {n: full_to_blocks(n, lora if n == "lora" else gw[n]) for n in early}
        sent["got"], partials = chip_sums(sent["grads"])
        sent["handles"], token = scatter_start(partials, "scatter_start")
        return token

    loss_cols, dx, gW, gP = layer_step(x[0], loss_target[0], W, P, send_early)
    landed = scatter_wait(sent["handles"], gP["ffn1_norm"], "scatter_wait")

    late = {n: full_to_blocks(n, gW[n]) for n in FIRST_FFN}
    late_got, late_partials = chip_sums(late)
    late_handles, late_token = scatter_start(late_partials, "scatter_start_ffn1")
    landed[-1] = landed[-1] + late_token[0, 0].astype(landed[-1].dtype)
    out_g, out_d, out_m, out_v = {}, {}, {}, {}

    def apply(g_blk):
        for n in g_blk:
            res = (g_blk[n], *adamw_block(n, w_blk[n], g_blk[n], m_blk[n], v_blk[n]))
            for dst, t in zip((out_g, out_d, out_m, out_v), res):
                for part, val in (split_lora(t) if n == "lora" else {n: t}).items():
                    dst[part] = val.reshape(shapes[part])

    apply(owner_sums(sent["grads"], sent["got"], landed))
    apply(owner_sums(late, late_got, scatter_wait(late_handles, list(out_d.values())[-1], "scatter_wait_ffn1")))

    zero_head = jnp.zeros((1, D_MODEL), F32)
    vec = pack_small(gP, loss_cols)
    loss, g_s, d_s, m_s, v_s = reduce_small(
        vec, pack_small({n: weights[n] for n, _ in SMALL}, zero_head),
        pack_small({n: mom_m[n] for n, _ in SMALL}, zero_head),
        pack_small({n: mom_v[n] for n, _ in SMALL}, zero_head))
    for dst, src in ((out_g, g_s), (out_d, d_s), (out_m, m_s), (out_v, v_s)):
        dst.update(unpack_small(src, shapes))

    return (loss[0, 0], dx[None], *[out_g[n] for n in WEIGHT_ORDER], *[out_d[n] for n in WEIGHT_ORDER],
            *[out_m[n] for n in WEIGHT_ORDER], *[out_v[n] for n in WEIGHT_ORDER])
```

```python
import functools

import jax
import jax.numpy as jnp
from jax import lax
from jax.experimental import pallas as pl
from jax.experimental.pallas import tpu as pltpu

F32 = jnp.float32
BF16 = jnp.bfloat16
MESH = pl.DeviceIdType.MESH

D_MODEL = 1024
HEAD_DIM = 64
RWKV_HEADS = 16
LORA_W, LORA_A, LORA_G = 64, 64, 160
LORA = LORA_W + LORA_A + LORA_G
RKV = 3 * D_MODEL
ATTN_PAIRS = ((128, 1), (512, 4), (2048, 16))
ATTN_BLK = 128
ATTN_HPG = 4
ATTN_WIDTH = 768
GROUP_W = ATTN_HPG * HEAD_DIM
D_FF = 2816
GN_EPS = 64e-5
RMS_EPS = 1e-6
NEG_INF = -1e30
WKV_CHUNK = 64
WKV_HEADS_PER_STEP = 16

ADAM_LR, ADAM_B1, ADAM_B2, ADAM_EPS, ADAM_WD, ADAM_STEP = 0.001, 0.9, 0.999, 1e-08, 0.01, 10

V7X_VMEM_BYTES = 64 << 20
VMEM_TEMP_ALLOWANCE = 20 << 20


def _cparams(sem, block_bytes):
    limit = min(2 * block_bytes + VMEM_TEMP_ALLOWANCE, V7X_VMEM_BYTES - (6 << 20))
    return pltpu.CompilerParams(dimension_semantics=sem, vmem_limit_bytes=int(limit))


def _nbytes(shape, dtype):
    n = 1
    for s in shape:
        n *= s
    return n * jnp.dtype(dtype).itemsize


def _split_bf16(a):
    hi = a.astype(BF16)
    return hi, (a - hi.astype(F32)).astype(BF16)


def _make_dots():
    def raw(a, b, ca, cb):
        return lax.dot_general(a.astype(BF16), b.astype(BF16), (((ca,), (cb,)), ((), ())),
                               preferred_element_type=F32)

    @jax.custom_vjp
    def nn(a, b):
        return raw(a, b, 1, 0)

    @jax.custom_vjp
    def nt(a, b):
        return raw(a, b, 1, 1)

    @jax.custom_vjp
    def tn(a, b):
        return raw(a, b, 0, 0)

    nn.defvjp(lambda a, b: (raw(a, b, 1, 0), (a, b)),
              lambda res, g: (raw(g, res[1], 1, 1), raw(res[0], g, 0, 0)))
    nt.defvjp(lambda a, b: (raw(a, b, 1, 1), (a, b)),
              lambda res, g: (raw(g, res[1], 1, 0), raw(g, res[0], 0, 0)))
    tn.defvjp(lambda a, b: (raw(a, b, 0, 0), (a, b)),
              lambda res, g: (raw(res[1], g, 1, 1), raw(res[0], g, 1, 0)))
    return nn, nt, tn


def _exact_rhs_dot(x, ones, cx, co):
    hi, lo = _split_bf16(x)
    dims = (((cx,), (co,)), ((), ()))
    return (lax.dot_general(hi, ones, dims, preferred_element_type=F32)
            + lax.dot_general(lo, ones, dims, preferred_element_type=F32))


@jax.custom_vjp
def SEG(x, ones):
    return _exact_rhs_dot(x, ones, 1, 0)


SEG.defvjp(lambda x, ones: (_exact_rhs_dot(x, ones, 1, 0), ones),
           lambda ones, g: (_exact_rhs_dot(g, ones, 1, 1), jnp.zeros_like(ones)))

NN, NT, TN = _make_dots()


MM_TILE_M, MM_TILE_N, MM_TILE_K = 1408, 1408, 1536


def _pick(n, cap):
    best = None
    for t in range(128, min(n, cap) + 1, 128):
        if n % t == 0:
            best = t
    return best or n


def matmul(a, b, mode, name, *, add=None, scale=1.0, out_dtype=F32):
    if mode == "nn":
        (M, K), (K2, N) = a.shape, b.shape
    elif mode == "nt":
        (M, K), (N, K2) = a.shape, b.shape
    else:
        (K, M), (K2, N) = a.shape, b.shape
    assert K == K2, (name, a.shape, b.shape)
    tm, tn, tk = _pick(M, MM_TILE_M), _pick(N, MM_TILE_N), _pick(K, MM_TILE_K)
    nk = K // tk
    ca, cb = {"nn": (1, 0), "nt": (1, 1), "tn": (0, 0)}[mode]

    def body(*refs):
        if add is None:
            a_ref, b_ref, o_ref, acc_ref = refs
        else:
            a_ref, b_ref, add_ref, o_ref, acc_ref = refs
        k = pl.program_id(2)

        @pl.when(k == 0)
        def _():
            acc_ref[...] = jnp.zeros_like(acc_ref)

        acc_ref[...] += lax.dot_general(a_ref[...].astype(BF16), b_ref[...].astype(BF16),
                                        (((ca,), (cb,)), ((), ())), preferred_element_type=F32)

        @pl.when(k == nk - 1)
        def _():
            r = acc_ref[...] * scale
            if add is not None:
                r = add_ref[...] + r
            o_ref[...] = r.astype(o_ref.dtype)

    a_spec = (pl.BlockSpec((tk, tm), lambda i, j, k: (k, i)) if mode == "tn"
              else pl.BlockSpec((tm, tk), lambda i, j, k: (i, k)))
    b_spec = (pl.BlockSpec((tn, tk), lambda i, j, k: (j, k)) if mode == "nt"
              else pl.BlockSpec((tk, tn), lambda i, j, k: (k, j)))
    in_specs, args = [a_spec, b_spec], [a, b]
    blk = tm * tk * a.dtype.itemsize + tk * tn * b.dtype.itemsize + tm * tn * 8
    if add is not None:
        in_specs.append(pl.BlockSpec((tm, tn), lambda i, j, k: (i, j)))
        args.append(add)
        blk += tm * tn * 4
    return pl.pallas_call(
        body, name=name, grid=(M // tm, N // tn, nk),
        in_specs=in_specs, out_specs=pl.BlockSpec((tm, tn), lambda i, j, k: (i, j)),
        out_shape=jax.ShapeDtypeStruct((M, N), out_dtype),
        scratch_shapes=[pltpu.VMEM((tm, tn), F32)],
        compiler_params=_cparams(("parallel", "parallel", "arbitrary"), blk),
    )(*args)


def matmul_cs(a, w, mode, name, *, scale=1.0, out_dtype=F32):
    n_blk = N_SHARDS
    if mode == "tn":
        (K, R), Cs = a.shape, w.shape[2] // 2
        tm, tk = _pick(R, MM_TILE_M), _pick(K, 1024)
        grid = (R // tm, n_blk, K // tk)
        a_spec = pl.BlockSpec((tk, tm), lambda i, j, k: (k, i))
        w_spec = pl.BlockSpec((None, tk, Cs), lambda i, j, k: (j // 2, k, j % 2))
        o_spec = pl.BlockSpec((None, tm, Cs), lambda i, j, k: (j, i, 0))
        out_shape, acc_shape, dims = (n_blk, R, Cs), (tm, Cs), (0, 0)
        blk = tk * tm * a.dtype.itemsize + tk * Cs * w.dtype.itemsize + tm * Cs * 8
    else:
        M, (_, R, Cs) = a.shape[1], w.shape
        tm, tn = _pick(M, MM_TILE_M), _pick(R, MM_TILE_N)
        grid = (M // tm, R // tn, n_blk)
        a_spec = pl.BlockSpec((None, tm, Cs), lambda i, j, k: (k // 2, i, k % 2))
        w_spec = pl.BlockSpec((None, tn, Cs), lambda i, j, k: (k, j, 0))
        o_spec = pl.BlockSpec((tm, tn), lambda i, j, k: (i, j))
        out_shape, acc_shape, dims = (M, R), (tm, tn), (1, 1)
        blk = tm * Cs * a.dtype.itemsize + tn * Cs * w.dtype.itemsize + tm * tn * 8
    nk = grid[2]

    def body(a_ref, w_ref, o_ref, acc_ref):
        k = pl.program_id(2)

        @pl.when(k == 0)
        def _():
            acc_ref[...] = jnp.zeros_like(acc_ref)

        acc_ref[...] += lax.dot_general(a_ref[...].astype(BF16), w_ref[...].astype(BF16),
                                        (((dims[0],), (dims[1],)), ((), ())), preferred_element_type=F32)

        @pl.when(k == nk - 1)
        def _():
            o_ref[...] = (acc_ref[...] * scale).astype(o_ref.dtype)

    return pl.pallas_call(
        body, name=name, grid=grid, in_specs=[a_spec, w_spec], out_specs=o_spec,
        out_shape=jax.ShapeDtypeStruct(out_shape, out_dtype), scratch_shapes=[pltpu.VMEM(acc_shape, F32)],
        compiler_params=_cparams(("parallel", "parallel", "arbitrary"), blk),
    )(a, w)


FFN_TILE_M = 512


def _swiglu(gate, up):
    return gate * jax.nn.sigmoid(gate) * up


def ffn_in_act(h, w, name):
    (M, R), Cs, half = h.shape, w.shape[2], N_SHARDS // 2
    tm, tk = _pick(M, FFN_TILE_M), _pick(R, 1024)
    nk = R // tk

    def body(h_ref, wg_ref, wu_ref, gu_ref, act_ref, acc_ref):
        k = pl.program_id(2)

        @pl.when(k == 0)
        def _():
            acc_ref[...] = jnp.zeros_like(acc_ref)

        hb = h_ref[...].astype(BF16)
        for part, w_ref in enumerate((wg_ref, wu_ref)):
            acc_ref[part] += jnp.dot(hb, w_ref[...].astype(BF16), preferred_element_type=F32)

        @pl.when(k == nk - 1)
        def _():
            gu_ref[...] = acc_ref[...]
            act_ref[...] = _swiglu(acc_ref[0], acc_ref[1]).astype(act_ref.dtype)

    w_spec = lambda off: pl.BlockSpec((None, tk, Cs), functools.partial(lambda off, i, j, k: (j + off, k, 0), off))
    blk = tm * tk * h.dtype.itemsize + 2 * tk * Cs * w.dtype.itemsize + tm * Cs * (16 + 2)
    return pl.pallas_call(
        body, name=name, grid=(M // tm, half, nk),
        in_specs=[pl.BlockSpec((tm, tk), lambda i, j, k: (i, k)), w_spec(0), w_spec(half)],
        out_specs=[pl.BlockSpec((2, tm, Cs), lambda i, j, k: (0, i, j)), pl.BlockSpec((tm, Cs), lambda i, j, k: (i, j))],
        out_shape=[jax.ShapeDtypeStruct((2, M, half * Cs), F32), jax.ShapeDtypeStruct((M, half * Cs), BF16)],
        scratch_shapes=[pltpu.VMEM((2, tm, Cs), F32)],
        compiler_params=_cparams(("parallel", "parallel", "arbitrary"), blk),
    )(h, w, w)


def ffn_dact_dgu(dy, w_out, gu, scale, name):
    (M, D), F = dy.shape, w_out.shape[0]
    tm, tn = _pick(M, FFN_TILE_M), F // 2

    def body(dy_ref, w_ref, gu_ref, dgu_ref):
        dact = scale * lax.dot_general(dy_ref[...].astype(BF16), w_ref[...].astype(BF16),
                                       (((1,), (1,)), ((), ())), preferred_element_type=F32)
        dgate, dup = jax.vjp(_swiglu, gu_ref[0], gu_ref[1])[1](dact)
        dgu_ref[0] = dgate.astype(dgu_ref.dtype)
        dgu_ref[1] = dup.astype(dgu_ref.dtype)

    pair = pl.BlockSpec((2, tm, tn), lambda i, j: (0, i, j))
    blk = tm * D * dy.dtype.itemsize + tn * D * w_out.dtype.itemsize + 2 * tm * tn * (4 + 2)
    return pl.pallas_call(
        body, name=name, grid=(M // tm, F // tn),
        in_specs=[pl.BlockSpec((tm, D), lambda i, j: (i, 0)), pl.BlockSpec((tn, D), lambda i, j: (j, 0)), pair],
        out_specs=pair, out_shape=jax.ShapeDtypeStruct((2, M, F), BF16),
        compiler_params=_cparams(("parallel", "parallel"), blk),
    )(dy, w_out, gu)


def _row_block(n, width, n_arrays):
    cap = (V7X_VMEM_BYTES // 4) // (2 * 4 * width * n_arrays)
    best = None
    for t in range(16, min(n, cap) + 1, 16):
        if n % t == 0:
            best = t
    return best or n


def placed_map(f, ins, out, *, n_blocks, tb, name):
    def body(*refs):
        refs[-1][...] = f(*[r[...] for r in refs[:-1]]).astype(refs[-1].dtype)

    def spec(fn):
        def index(i):
            x, y, c = _place()
            return fn(i, (c, 2 * x + y)), 0
        return pl.BlockSpec((tb, width), index)

    o_rows, width, o_dtype, o_fn = out
    blk = (sum(a.dtype.itemsize for a, _ in ins) + jnp.dtype(o_dtype).itemsize) * tb * width
    return pl.pallas_call(
        body, name=name, grid=(n_blocks,), in_specs=[spec(fn) for _, fn in ins], out_specs=spec(o_fn),
        out_shape=jax.ShapeDtypeStruct((o_rows, width), o_dtype),
        compiler_params=_cparams(("parallel",), blk),
    )(*[a for a, _ in ins])


def rowmap(f, rows, params, outs, accs=(), *, tb, name):
    rows = [r if isinstance(r, tuple) else (r, r.shape[1], 0) for r in rows]
    S = rows[0][0].shape[0]
    assert S % tb == 0, (name, S, tb)
    n_in, n_out = len(rows) + len(params), len(outs)

    def body(*refs):
        res = f(*[r[...] for r in refs[:n_in]])
        res = res if isinstance(res, (tuple, list)) else (res,)
        o_refs, a_refs = refs[n_in:n_in + n_out], refs[n_in + n_out:]
        for ref, val in zip(o_refs, res[:n_out]):
            ref[...] = val.astype(ref.dtype)
        if a_refs:
            @pl.when(pl.program_id(0) == 0)
            def _():
                for ref in a_refs:
                    ref[...] = jnp.zeros_like(ref)

            for ref, val in zip(a_refs, res[n_out:]):
                ref[...] += val.astype(F32)

    in_specs = [pl.BlockSpec((tb, w), functools.partial(lambda cb, i: (i, cb), cb)) for _, w, cb in rows]
    in_specs += [pl.BlockSpec(p.shape, lambda i: (0, 0)) for p in params]
    out_specs = [pl.BlockSpec((tb, w), lambda i: (i, 0)) for w, _ in outs]
    out_specs += [pl.BlockSpec(tuple(s), lambda i: (0, 0)) for s in accs]
    out_shape = [jax.ShapeDtypeStruct((S, w), dt) for w, dt in outs]
    out_shape += [jax.ShapeDtypeStruct(tuple(s), F32) for s in accs]
    blk = sum(tb * w * a.dtype.itemsize for a, w, _ in rows) + sum(_nbytes(p.shape, p.dtype) for p in params)
    blk += sum(_nbytes((tb, w), dt) for w, dt in outs) + sum(_nbytes(s, F32) for s in accs)
    res = pl.pallas_call(
        body, name=name, grid=(S // tb,), in_specs=in_specs, out_specs=out_specs, out_shape=out_shape,
        compiler_params=_cparams(("arbitrary",) if accs else ("parallel",), blk),
    )(*[r[0] for r in rows], *[pltpu.with_memory_space_constraint(p, pltpu.HBM) for p in params])
    return res


def _rms(x, g):
    return x * lax.rsqrt(jnp.mean(x * x, axis=-1, keepdims=True) + RMS_EPS) * g


def _softplus(z):
    return jnp.maximum(z, 0.0) + jnp.log(1.0 + jnp.exp(-jnp.abs(z)))


def _rwkv_pre(xrk, xlo, w0, w2p, a0, a2p, g2p, k_k, k_a, seg, seg_t):
    k = xrk[:, D_MODEL:2 * D_MODEL]
    w = -_softplus(-(w0 + NN(jnp.tanh(xlo), w2p))) - 0.5
    log_decay = -jnp.exp(w)
    a = jax.nn.sigmoid(a0 + NN(xlo, a2p))
    g = NN(jax.nn.sigmoid(xlo), g2p)
    kk = k * k_k
    norm = jnp.maximum(jnp.sqrt(SEG(kk * kk, seg)), 1e-12)
    kk = kk * SEG(1.0 / norm, seg_t)
    k_mod = k * (1.0 + (a - 1.0) * k_a)
    return log_decay, k_mod, -kk, kk * a, g


def _rwkv_post(wkv, r, k_mod, v, g, r_k, ln_w, ln_b, seg, seg_t):
    inv_n = 1.0 / HEAD_DIM
    mean = SEG(wkv, seg) * inv_n
    cen = wkv - SEG(mean, seg_t)
    var = SEG(cen * cen, seg) * inv_n
    y = cen * SEG(lax.rsqrt(var + GN_EPS), seg_t) * ln_w + ln_b
    bonus = SEG(SEG(r * k_mod * r_k, seg), seg_t) * v
    return (y + bonus) * g


def _qk_norm(q, k, q_gain, k_gain, seg, seg_t, tile_t):
    def norm(x, gain):
        mean_sq = SEG(x * x, seg) * (1.0 / HEAD_DIM)
        return x * SEG(lax.rsqrt(mean_sq + RMS_EPS), seg_t) * SEG(gain, tile_t)

    return norm(q, q_gain) * (HEAD_DIM ** -0.5), norm(k, k_gain)


def _gate_merge(pgate, pa, pb, b_gate):
    sg = jax.nn.sigmoid(pgate + b_gate)
    return sg[:, :D_MODEL] * pa + sg[:, D_MODEL:] * pb


def _group_combine(o, lse):
    ls = [lse[:, GROUP_W * i:GROUP_W * (i + 1)] for i in range(3)]
    m = jnp.maximum(jnp.maximum(ls[0], ls[1]), ls[2])
    es = [jnp.exp(l - m) for l in ls]
    den = es[0] + es[1] + es[2]
    return jnp.concatenate([o[:, GROUP_W * i:GROUP_W * (i + 1)] * (es[i] / den) for i in range(3)], axis=1)


def _each(f, *xs):
    return tuple(f(*args) for args in zip(*xs))


def _attn_block(q, kc, kp, vc, vp, first):
    qi = lax.broadcasted_iota(jnp.int32, (ATTN_BLK, ATTN_BLK), 0)
    kj = lax.broadcasted_iota(jnp.int32, (ATTN_BLK, ATTN_BLK), 1)
    own = kj <= qi
    s_c = _each(lambda a, b: jnp.where(own, NT(a, b), NEG_INF), q, kc)
    s_p = _each(lambda a, b, f: jnp.where((kj >= qi) & (f < 0.5), NT(a, b), NEG_INF), q, kp, first)
    row_max = lambda s: jnp.max(s, axis=-1, keepdims=True)
    row_sum = lambda s: jnp.sum(s, axis=-1, keepdims=True)
    m = _each(lambda c_, p_: jnp.maximum(row_max(c_), row_max(p_)), s_c, s_p)
    e_c, e_p = _each(lambda s, m_: jnp.exp(s - m_), s_c, m), _each(lambda s, m_: jnp.exp(s - m_), s_p, m)
    den = _each(lambda c_, p_: row_sum(c_) + row_sum(p_), e_c, e_p)
    inv = _each(lambda d_: 1.0 / d_, den)
    o = _each(lambda ec, ep, i_, vc_, vp_: (NN(ec, vc_) + NN(ep, vp_)) * i_, e_c, e_p, inv, vc, vp)
    lse = _each(lambda m_, d_: jnp.broadcast_to(m_ + jnp.log(d_), (ATTN_BLK, HEAD_DIM)), m, den)
    return o, lse


def _attn_pair(q, k, k_before, v, v_before, first):
    n = len(q[0])
    o, lse = _attn_block(q[0] + q[1], k[0] + k[1], k_before + k[0], v[0] + v[1], v_before + v[0],
                         (first[0],) * n + (first[1],) * n)
    return (o[:n], o[n:]), (lse[:n], lse[n:])


TRI_SEED = 8


def _tri_inverse(n):
    c = n[0].shape[0]
    row = lax.broadcasted_iota(jnp.int32, (c, c), 0)
    col = lax.broadcasted_iota(jnp.int32, (c, c), 1)
    same_block = lambda size: (row >> (size.bit_length() - 1)) == (col >> (size.bit_length() - 1))
    seed = same_block(TRI_SEED)
    p = _each(lambda m: jnp.where(seed, m, 0.0), n)
    t, span = _each(lambda m: (row == col).astype(F32) + m, p), 2
    while span < TRI_SEED:
        p = _each(NN, p, p)
        t = _each(lambda t_, p_: t_ + NN(t_, p_), t, p)
        span *= 2
    size = TRI_SEED
    while size < c:
        joins = same_block(2 * size) & jnp.logical_not(same_block(size))
        t = _each(lambda t_, m: t_ + NN(NN(t_, jnp.where(joins, m, 0.0)), t_), t, n)
        size *= 2
    return t


@jax.custom_vjp
def _tri_solve(n, rhs, t):
    return _each(NN, t, rhs)


def _tri_solve_fwd(n, rhs, t):
    x = _each(NN, t, rhs)
    return x, (t, x)


def _tri_solve_bwd(res, dx):
    t, x = res
    drhs = _each(TN, t, dx)
    return _each(NT, drhs, x), drhs, _each(jnp.zeros_like, t)


_tri_solve.defvjp(_tri_solve_fwd, _tri_solve_bwd)


def _lower_ones(c):
    row = lax.broadcasted_iota(jnp.int32, (c, c), 0)
    col = lax.broadcasted_iota(jnp.int32, (c, c), 1)
    return (row >= col).astype(BF16)


def _ones_dot(ones, x, contract):
    hi, lo = _split_bf16(x)
    dims = (((contract,), (0,)), ((), ()))
    return (lax.dot_general(ones, hi, dims, preferred_element_type=F32)
            + lax.dot_general(ones, lo, dims, preferred_element_type=F32))


@jax.custom_vjp
def _cumsum_rows(x):
    return _ones_dot(_lower_ones(x.shape[0]), x, 1)


_cumsum_rows.defvjp(lambda x: (_ones_dot(_lower_ones(x.shape[0]), x, 1), None),
                    lambda _, g: (_ones_dot(_lower_ones(g.shape[0]), g, 0),))


def _wkv_chunk(s0, r, lw, k, v, a, b, t_inv=None):
    c = r[0].shape[0]
    row = lax.broadcasted_iota(jnp.int32, (c, c), 0)
    col = lax.broadcasted_iota(jnp.int32, (c, c), 1)
    strict, incl = row > col, row >= col
    cat = lambda p, q: jnp.concatenate([p, q], axis=0)
    cum = _each(_cumsum_rows, lw)
    e_neg = _each(lambda c_: jnp.exp(-c_), cum)
    ar = _each(lambda a_, r_, c_, l_: cat(a_ * jnp.exp(c_ - l_), r_ * jnp.exp(c_)), a, r, cum, lw)
    b_t, k_t = _each(jnp.multiply, b, e_neg), _each(jnp.multiply, k, e_neg)
    p_b, p_k, p_s = _each(NT, ar, b_t), _each(NT, ar, k_t), _each(NT, ar, s0)
    n_ab = _each(lambda p: jnp.where(strict, p[:c], 0.0), p_b)
    m_rb = _each(lambda p: jnp.where(incl, p[c:], 0.0), p_b)
    n_ak = _each(lambda p: jnp.where(strict, p[:c], 0.0), p_k)
    m_rk = _each(lambda p: jnp.where(incl, p[c:], 0.0), p_k)
    if t_inv is None:
        t_inv = _tri_inverse(n_ab)
    u = _tri_solve(n_ab, _each(lambda p, n_, v_: p[:c] + NN(n_, v_), p_s, n_ak, v), t_inv)
    y = _each(lambda p, mb, u_, mk, v_: p[c:] + NN(mb, u_) + NN(mk, v_), p_s, m_rb, u, m_rk, v)
    g_end = _each(lambda l_: jnp.exp(jnp.sum(l_, axis=0, keepdims=True)), lw)
    s1 = _each(lambda s_, g_, u_, v_, b_, k_: s_ * g_ + TN(cat(u_, v_), cat(b_, k_) * g_),
               s0, g_end, u, v, b_t, k_t)
    return y, s1, t_inv


def _adamw(w, g, m, v):
    m = ADAM_B1 * m + (1.0 - ADAM_B1) * g
    v = ADAM_B2 * v + (1.0 - ADAM_B2) * jnp.square(g)
    m_hat = m / (1.0 - ADAM_B1 ** ADAM_STEP)
    v_hat = v / (1.0 - ADAM_B2 ** ADAM_STEP)
    delta = -ADAM_LR * (m_hat / (jnp.sqrt(v_hat) + ADAM_EPS) + ADAM_WD * w)
    return delta, m, v


def token_shift_fwd(p, mu, *, tb, name):
    S, W = p.shape
    hb = tb // 8

    def body(p_ref, halo_ref, mu_ref, o_ref):
        i = pl.program_id(0)
        x = p_ref[...]
        before = halo_ref[7:8, :] * (i > 0).astype(F32)
        row = lax.broadcasted_iota(jnp.int32, (tb, W), 0)
        prev = jnp.where(row == 0, before, pltpu.roll(x, 1, 0))
        o_ref[...] = x + (prev - x) * mu_ref[...]

    blk = (2 * tb + 8) * W * 4
    return pl.pallas_call(
        body, name=name, grid=(S // tb,),
        in_specs=[pl.BlockSpec((tb, W), lambda i: (i, 0)),
                  pl.BlockSpec((8, W), lambda i: (jnp.maximum(i * hb - 1, 0), 0)),
                  pl.BlockSpec((1, W), lambda i: (0, 0))],
        out_specs=pl.BlockSpec((tb, W), lambda i: (i, 0)),
        out_shape=jax.ShapeDtypeStruct((S, W), F32),
        compiler_params=_cparams(("parallel",), blk),
    )(p, p, mu)


def token_shift_bwd(dxs, p, mu, *, tb, name):
    S, W = p.shape
    hb, nb = tb // 8, S // tb

    def body(d_ref, dnext_ref, p_ref, halo_ref, mu_ref, dp_ref, dmu_ref):
        i = pl.program_id(0)
        d, x, mu_v = d_ref[...], p_ref[...], mu_ref[...]
        row = lax.broadcasted_iota(jnp.int32, (tb, W), 0)
        before = halo_ref[7:8, :] * (i > 0).astype(F32)
        prev = jnp.where(row == 0, before, pltpu.roll(x, 1, 0))
        t = d * mu_v
        after = dnext_ref[0:1, :] * mu_v * (i < nb - 1).astype(F32)
        nxt = jnp.where(row == tb - 1, after, pltpu.roll(t, tb - 1, 0))
        dp_ref[...] = (d - t + nxt).astype(dp_ref.dtype)

        @pl.when(i == 0)
        def _():
            dmu_ref[...] = jnp.zeros_like(dmu_ref)

        dmu_ref[...] += jnp.sum(d * (prev - x), axis=0, keepdims=True)

    blk = (3 * tb + 16) * W * 4
    return pl.pallas_call(
        body, name=name, grid=(nb,),
        in_specs=[pl.BlockSpec((tb, W), lambda i: (i, 0)),
                  pl.BlockSpec((8, W), lambda i: (jnp.minimum((i + 1) * hb, S // 8 - 1), 0)),
                  pl.BlockSpec((tb, W), lambda i: (i, 0)),
                  pl.BlockSpec((8, W), lambda i: (jnp.maximum(i * hb - 1, 0), 0)),
                  pl.BlockSpec((1, W), lambda i: (0, 0))],
        out_specs=[pl.BlockSpec((tb, W), lambda i: (i, 0)), pl.BlockSpec((1, W), lambda i: (0, 0))],
        out_shape=[jax.ShapeDtypeStruct((S, W), BF16), jax.ShapeDtypeStruct((1, W), F32)],
        compiler_params=_cparams(("arbitrary",), blk),
    )(dxs, dxs, p, p, mu)


def _head_cols(h):
    return pl.ds(h * HEAD_DIM, HEAD_DIM)


def wkv_fwd(xs_rk, lw, k, a, b):
    S = lw.shape[0]
    C, nc, G, N = WKV_CHUNK, S // WKV_CHUNK, WKV_HEADS_PER_STEP, HEAD_DIM

    def body(r_ref, lw_ref, k_ref, v_ref, a_ref, b_ref, y_ref, st_ref, ti_ref, state):
        @pl.when(pl.program_id(1) == 0)
        def _():
            state[...] = jnp.zeros_like(state)

        heads = lambda ref: tuple(ref[:, _head_cols(h)] for h in range(G))
        s0 = tuple(state[h] for h in range(G))
        y, s1, t_inv = _wkv_chunk(s0, heads(r_ref), heads(lw_ref), heads(k_ref), heads(v_ref), heads(a_ref),
                                  heads(b_ref))
        for h in range(G):
            st_ref[h] = s0[h]
            ti_ref[h] = t_inv[h]
            y_ref[:, _head_cols(h)] = y[h]
            state[h] = s1[h]

    W = G * N
    seq = lambda j: pl.BlockSpec((C, W), functools.partial(lambda j, g, c: (c, j + g), j))
    per = D_MODEL // W
    per_chunk = pl.BlockSpec((None, G, N, N), lambda g, c: (c, g, 0, 0))
    return pl.pallas_call(
        body, name="wkv_fwd", grid=(RWKV_HEADS // G, nc),
        in_specs=[seq(0), seq(0), seq(0), seq(2 * per), seq(0), seq(0)],
        out_specs=[seq(0), per_chunk, per_chunk],
        out_shape=[jax.ShapeDtypeStruct((S, D_MODEL), F32)] + [jax.ShapeDtypeStruct((nc, RWKV_HEADS, N, N), F32)] * 2,
        scratch_shapes=[pltpu.VMEM((G, N, N), F32)],
        compiler_params=_cparams(("parallel", "arbitrary"), 8 * C * W * 4 + 3 * G * N * N * 4),
    )(xs_rk, lw, k, xs_rk, a, b)


def wkv_bwd(xs_rk, lw, k, a, b, states, t_invs, dy):
    S = lw.shape[0]
    C, nc, G, N = WKV_CHUNK, S // WKV_CHUNK, WKV_HEADS_PER_STEP, HEAD_DIM

    def body(r_ref, lw_ref, k_ref, v_ref, a_ref, b_ref, st_ref, ti_ref, dy_ref,
             dr_ref, dlw_ref, dk_ref, dv_ref, da_ref, db_ref, dstate):
        @pl.when(pl.program_id(1) == 0)
        def _():
            dstate[...] = jnp.zeros_like(dstate)

        heads = lambda ref: tuple(ref[:, _head_cols(h)] for h in range(G))
        t_inv = tuple(ti_ref[h] for h in range(G))
        chunk = lambda *args: _wkv_chunk(*args, t_inv)[:2]
        _, pull = jax.vjp(chunk, tuple(st_ref[h] for h in range(G)), heads(r_ref), heads(lw_ref),
                          heads(k_ref), heads(v_ref), heads(a_ref), heads(b_ref))
        ds0, *grads = pull((heads(dy_ref), tuple(dstate[h] for h in range(G))))
        for h in range(G):
            dstate[h] = ds0[h]
            for ref, grad in zip((dr_ref, dlw_ref, dk_ref, dv_ref, da_ref, db_ref), grads):
                ref[:, _head_cols(h)] = grad[h]

    W = G * N
    seq = lambda j: pl.BlockSpec((C, W), functools.partial(lambda j, g, c: (nc - 1 - c, j + g), j))
    per = D_MODEL // W
    st = pl.BlockSpec((None, G, N, N), lambda g, c: (nc - 1 - c, g, 0, 0))
    return pl.pallas_call(
        body, name="wkv_bwd", grid=(RWKV_HEADS // G, nc),
        in_specs=[seq(0), seq(0), seq(0), seq(2 * per), seq(0), seq(0), st, st, seq(0)],
        out_specs=[seq(0)] * 6, out_shape=[jax.ShapeDtypeStruct((S, D_MODEL), F32)] * 6,
        scratch_shapes=[pltpu.VMEM((G, N, N), F32)],
        compiler_params=_cparams(("parallel", "arbitrary"), 14 * C * W * 4 + 3 * G * N * N * 4),
    )(xs_rk, lw, k, xs_rk, a, b, states, t_invs, dy)


def _first_flag(i, seq_len):
    per_group = seq_len // ATTN_BLK
    g = i // per_group
    per_seq = [seq_len // d // ATTN_BLK for _, d in ATTN_PAIRS]
    n = jnp.where(g == 0, per_seq[0], jnp.where(g == 1, per_seq[1], per_seq[2]))
    return (lax.rem(i, n) == 0).astype(F32)


def _block_rows(half):
    return pl.ds(half * ATTN_BLK, ATTN_BLK)


def _block_heads(ref, half):
    return tuple(ref[_block_rows(half), _head_cols(h)] for h in range(ATTN_HPG))


def _pair_heads(ref):
    return _block_heads(ref, 0), _block_heads(ref, 1)


def attn_fwd(q, k, v, seq_len):
    R, N = q.shape
    n_pairs = R // (2 * ATTN_BLK)

    def body(q_ref, k_ref, kb_ref, v_ref, vb_ref, o_ref, lse_ref):
        pair = pl.program_id(0)
        first = (_first_flag(2 * pair, seq_len), _first_flag(2 * pair + 1, seq_len))
        o, lse = _attn_pair(_pair_heads(q_ref), _pair_heads(k_ref), _block_heads(kb_ref, 0), _pair_heads(v_ref),
                            _block_heads(vb_ref, 0), first)
        for half in range(2):
            for h in range(ATTN_HPG):
                o_ref[_block_rows(half), _head_cols(h)] = o[half][h]
                lse_ref[_block_rows(half), _head_cols(h)] = lse[half][h]

    cur = pl.BlockSpec((2 * ATTN_BLK, N), lambda i: (i, 0))
    prv = pl.BlockSpec((ATTN_BLK, N), lambda i: (jnp.maximum(2 * i - 1, 0), 0))
    return pl.pallas_call(
        body, name="attn_fwd", grid=(n_pairs,), in_specs=[cur, cur, prv, cur, prv],
        out_specs=[cur, cur], out_shape=[jax.ShapeDtypeStruct((R, N), F32)] * 2,
        compiler_params=_cparams(("parallel",), 12 * ATTN_BLK * N * 4),
    )(q, k, k, v, v)


def attn_bwd(q, k, v, do, dlse, seq_len):
    R, N = q.shape
    n_pairs = R // (2 * ATTN_BLK)

    def body(q_ref, k_ref, kb_ref, v_ref, vb_ref, do_ref, dl_ref, dq_ref, dk_ref, dv_ref, carry_k, carry_v):
        step = pl.program_id(0)
        pair = n_pairs - 1 - step
        first = (_first_flag(2 * pair, seq_len), _first_flag(2 * pair + 1, seq_len))

        @pl.when(step == 0)
        def _():
            carry_k[...] = jnp.zeros_like(carry_k)
            carry_v[...] = jnp.zeros_like(carry_v)

        _, pull = jax.vjp(functools.partial(_attn_pair, first=first), _pair_heads(q_ref), _pair_heads(k_ref),
                          _block_heads(kb_ref, 0), _pair_heads(v_ref), _block_heads(vb_ref, 0))
        dq, dk, dk_before, dv, dv_before = pull((_pair_heads(do_ref), _pair_heads(dl_ref)))
        old_k, old_v = _block_heads(carry_k, 0), _block_heads(carry_v, 0)
        for h in range(ATTN_HPG):
            cols = _head_cols(h)
            for half in range(2):
                dq_ref[_block_rows(half), cols] = dq[half][h]
            dk_ref[_block_rows(0), cols] = dk[0][h]
            dv_ref[_block_rows(0), cols] = dv[0][h]
            dk_ref[_block_rows(1), cols] = dk[1][h] + old_k[h]
            dv_ref[_block_rows(1), cols] = dv[1][h] + old_v[h]
            carry_k[:, cols] = dk_before[h]
            carry_v[:, cols] = dv_before[h]

    cur = pl.BlockSpec((2 * ATTN_BLK, N), lambda i: (n_pairs - 1 - i, 0))
    prv = pl.BlockSpec((ATTN_BLK, N), lambda i: (jnp.maximum(2 * (n_pairs - 1 - i) - 1, 0), 0))
    return pl.pallas_call(
        body, name="attn_bwd", grid=(n_pairs,), in_specs=[cur, cur, prv, cur, prv, cur, cur],
        out_specs=[cur, cur, cur], out_shape=[jax.ShapeDtypeStruct((R, N), F32)] * 3,
        scratch_shapes=[pltpu.VMEM((ATTN_BLK, N), F32)] * 2,
        compiler_params=_cparams(("arbitrary",), 22 * ATTN_BLK * N * 4),
    )(q, k, k, v, v, do, dlse)


def to_subsequences(t):
    S = t.shape[0]
    parts = []
    for gi, (_, d) in enumerate(ATTN_PAIRS):
        tg = t[:, GROUP_W * gi:GROUP_W * (gi + 1)].reshape(S // d, d, GROUP_W)
        parts.append(tg.transpose(1, 0, 2).reshape(S, GROUP_W))
    return jnp.concatenate(parts, axis=0)


def from_subsequences(u, S):
    parts = []
    for gi, (_, d) in enumerate(ATTN_PAIRS):
        ug = u[S * gi:S * (gi + 1)].reshape(d, S // d, GROUP_W)
        parts.append(ug.transpose(1, 0, 2).reshape(S, GROUP_W))
    return jnp.concatenate(parts, axis=1)


def _ffn_fwd(x, norm, w_in, w_out, tag, token):
    h = rowmap(lambda x_b, g, tok: _rms(x_b, g) + tok[0:1, 0:1], [x], [norm, token], [(D_MODEL, BF16)], tb=512,
               name=tag + "_norm")[0]
    gu, act = ffn_in_act(h, w_in, tag + "_in")
    y = matmul(act, w_out, "nn", tag + "_out", add=x, scale=0.5)
    return y, (x, h, gu, act)


def _ffn_bwd(dy, saved, norm, w_in, w_out, tag):
    x, h, gu, act = saved
    dw_out = matmul(act, dy, "tn", tag + "_dwout", scale=0.5)
    dgu = ffn_dact_dgu(dy, w_out, gu, 0.5, tag + "_dgu")
    dh = matmul_cs(dgu, w_in, "nt", tag + "_dh")
    dw_in = matmul_cs(h, dgu, "tn", tag + "_dwin")

    def norm_bwd(x_b, dh_b, dy_b, g):
        dx, dg = jax.vjp(_rms, x_b, g)[1](dh_b)
        return dy_b + dx, dg

    dx, dnorm = rowmap(norm_bwd, [x, dh, dy], [norm], [(D_MODEL, F32)], [(1, D_MODEL)], tb=256,
                       name=tag + "_dnorm")
    return dx, dnorm, dw_in, dw_out


def layer_step(x, tgt, W, P, start_token, rest_weights, on_mixer_grads):
    S = x.shape[0]
    x1, ffn1_saved = _ffn_fwd(x, P["ffn1_norm"], W["ffn1_w_in"], W["ffn1_w_out"], "ffn1", start_token)
    W = {**W, **rest_weights(x1)}
    head_of = lambda n: jnp.arange(n)[:, None] // HEAD_DIM == jnp.arange(n // HEAD_DIM)[None, :]
    seg, seg_a = head_of(D_MODEL).astype(BF16), head_of(ATTN_WIDTH).astype(BF16)
    seg_t, seg_a_t = seg.T, seg_a.T
    tile_t = (jnp.arange(HEAD_DIM)[:, None] == jnp.arange(ATTN_WIDTH)[None, :] % HEAD_DIM).astype(BF16)
    qk_params = [P["attn_q_norm"], P["attn_k_norm"], seg_a, seg_a_t, tile_t]
    w_rkv, w_lora = W["w_in"][:, :RKV], W["w_in"][:, RKV:RKV + LORA]
    w_qkv = W["w_in"][:, RKV + LORA:RKV + LORA + 3 * ATTN_WIDTH]
    w_gate = W["w_in"][:, RKV + LORA + 3 * ATTN_WIDTH:]
    mu_rk, mu_lo = P["rwkv_mu"][:, :RKV], P["rwkv_mu"][:, RKV:]
    zeros = lambda n: jnp.zeros((n, D_MODEL), F32)
    w2p = jnp.concatenate([W["rwkv_w2"], zeros(LORA - LORA_W)], axis=0)
    a2p = jnp.concatenate([zeros(LORA_W), W["rwkv_a2"], zeros(LORA_G)], axis=0)
    g2p = jnp.concatenate([zeros(LORA_W + LORA_A), W["rwkv_g2"]], axis=0)
    pre_params = [P["rwkv_w0"], w2p, P["rwkv_a0"], a2p, g2p, P["rwkv_k_k"], P["rwkv_k_a"], seg, seg_t]
    post_params = [P["rwkv_r_k"], P["rwkv_ln_w"], P["rwkv_ln_b"], seg, seg_t]
    col = lambda arr, j: (arr, D_MODEL, j)

    h = rowmap(_rms, [x1], [P["mix_norm"]], [(D_MODEL, BF16)], tb=512, name="mix_norm")[0]
    p_rk = matmul(h, w_rkv, "nn", "proj_rkv")
    p_lo = matmul(h, w_lora, "nn", "proj_lora")
    p_qkv = matmul(h, w_qkv, "nn", "proj_qkv")
    p_gate = matmul(h, w_gate, "nn", "proj_gate")
    xs_rk = token_shift_fwd(p_rk, mu_rk, tb=256, name="shift_rk")
    xs_lo = token_shift_fwd(p_lo, mu_lo, tb=256, name="shift_lora")
    lw, k_mod, a_neg, b_kk, g = rowmap(
        _rwkv_pre, [xs_rk, xs_lo], pre_params, [(D_MODEL, F32)] * 5, tb=256, name="rwkv_pre")
    wkv, states, t_invs = wkv_fwd(xs_rk, lw, k_mod, a_neg, b_kk)
    post_rows = [wkv, col(xs_rk, 0), k_mod, col(xs_rk, 2), g]
    y_a = rowmap(_rwkv_post, post_rows, post_params, [(D_MODEL, BF16)], tb=256, name="rwkv_post")[0]

    qk_rows = [(p_qkv, ATTN_WIDTH, 0), (p_qkv, ATTN_WIDTH, 1)]
    qn, kn = rowmap(_qk_norm, qk_rows, qk_params, [(ATTN_WIDTH, F32)] * 2, tb=256, name="qk_norm")
    q_s, k_s, v_s = to_subsequences(qn), to_subsequences(kn), to_subsequences(p_qkv[:, 2 * ATTN_WIDTH:])
    o_s, lse_s = attn_fwd(q_s, k_s, v_s, S)
    o, lse = from_subsequences(o_s, S), from_subsequences(lse_s, S)
    y_b = rowmap(_group_combine, [o, lse], [], [(ATTN_WIDTH, BF16)], tb=512, name="attn_combine")[0]

    pa = matmul(y_a, W["w_proj_rwkv"], "nn", "proj_a")
    pb = matmul(y_b, W["w_proj_attn"], "nn", "proj_b")
    merged = rowmap(_gate_merge, [p_gate, pa, pb], [P["b_gate"]], [(D_MODEL, BF16)], tb=256, name="merge")[0]
    x2 = matmul(merged, W["w_out"], "nn", "mix_out", add=x1)
    x3, ffn2_saved = _ffn_fwd(x2, P["ffn2_norm"], W["ffn2_w_in"], W["ffn2_w_out"], "ffn2",
                              jnp.zeros_like(start_token))

    def loss_head(y_b_, t_b):
        err = y_b_ - t_b
        return err * (1.0 / D_MODEL), (0.5 / D_MODEL) * jnp.sum(err * err, axis=0, keepdims=True)

    dx3, loss_cols = rowmap(loss_head, [x3, tgt], [], [(D_MODEL, F32)], [(1, D_MODEL)], tb=512, name="loss")

    gW, gP = {}, {}
    dx2, gP["ffn2_norm"], gW["ffn2_w_in"], gW["ffn2_w_out"] = _ffn_bwd(
        dx3, ffn2_saved, P["ffn2_norm"], W["ffn2_w_in"], W["ffn2_w_out"], "ffn2")

    dmerged = matmul(dx2, W["w_out"], "nt", "d_merged")
    gW["w_out"] = matmul(merged, dx2, "tn", "dw_out")

    def merge_bwd(pg, pa_b, pb_b, dm, bg):
        return jax.vjp(_gate_merge, pg, pa_b, pb_b, bg)[1](dm)

    dp_gate, dpa, dpb, gP["b_gate"] = rowmap(
        merge_bwd, [p_gate, pa, pb, dmerged], [P["b_gate"]],
        [(2 * D_MODEL, BF16), (D_MODEL, BF16), (D_MODEL, BF16)], [(1, 2 * D_MODEL)], tb=256, name="merge_bwd")
    dy_a = matmul(dpa, W["w_proj_rwkv"], "nt", "d_ya")
    gW["w_proj_rwkv"] = matmul(y_a, dpa, "tn", "dw_proj_a")
    dy_b = matmul(dpb, W["w_proj_attn"], "nt", "d_yb")
    gW["w_proj_attn"] = matmul(y_b, dpb, "tn", "dw_proj_b")

    def combine_bwd(o_b, l_b, d_b):
        return jax.vjp(_group_combine, o_b, l_b)[1](d_b)

    do, dlse = rowmap(combine_bwd, [o, lse, dy_b], [], [(ATTN_WIDTH, F32)] * 2, tb=256, name="attn_combine_bwd")
    dq_s, dk_s, dv_s = attn_bwd(q_s, k_s, v_s, to_subsequences(do), to_subsequences(dlse), S)

    def qk_norm_bwd(q_b, k_b, dqn_b, dkn_b, dv_b, qg, kg, sg, sgt, tl):
        f = lambda *a: _qk_norm(*a, sg, sgt, tl)
        dq, dk, dqg, dkg = jax.vjp(f, q_b, k_b, qg, kg)[1]((dqn_b, dkn_b))
        return jnp.concatenate([dq, dk, dv_b], axis=1), dqg, dkg

    dp_qkv, gP["attn_q_norm"], gP["attn_k_norm"] = rowmap(
        qk_norm_bwd, qk_rows + [from_subsequences(t, S) for t in (dq_s, dk_s, dv_s)], qk_params,
        [(3 * ATTN_WIDTH, BF16)], [(1, HEAD_DIM)] * 2, tb=256, name="qk_norm_bwd")

    def post_bwd(wkv_b, r_b, k_b, v_b, g_b, d_b, r_k, ln_w, ln_b, sg, sgt):
        f = lambda *a: _rwkv_post(*a, sg, sgt)
        return jax.vjp(f, wkv_b, r_b, k_b, v_b, g_b, r_k, ln_w, ln_b)[1](d_b)

    dwkv, dr_p, dk_p, dv_p, dg, gP["rwkv_r_k"], gP["rwkv_ln_w"], gP["rwkv_ln_b"] = rowmap(
        post_bwd, post_rows + [dy_a], post_params, [(D_MODEL, F32)] * 5, [(1, D_MODEL)] * 3, tb=128,
        name="rwkv_post_bwd")
    dr_w, dlw, dk_w, dv_w, da_neg, db_kk = wkv_bwd(xs_rk, lw, k_mod, a_neg, b_kk, states, t_invs, dwkv)

    def pre_bwd(xrk_b, xlo_b, dlw_b, dkw_b, dkp_b, da_b, db_b, dg_b, drp_b, drw_b, dvp_b, dvw_b,
                w0, w2, a0, a2, g2, k_k, k_a, sg, sgt):
        f = lambda *a: _rwkv_pre(*a, sg, sgt)
        pull = jax.vjp(f, xrk_b, xlo_b, w0, w2, a0, a2, g2, k_k, k_a)[1]
        dxrk, dxlo, *dpar = pull((dlw_b, dkw_b + dkp_b, da_b, db_b, dg_b))
        direct = jnp.concatenate([drp_b + drw_b, jnp.zeros_like(drp_b), dvp_b + dvw_b], axis=1)
        return (dxrk + direct, dxlo, *dpar)

    pre_rows = [xs_rk, xs_lo, dlw, dk_w, dk_p, da_neg, db_kk, dg, dr_p, dr_w, dv_p, dv_w]
    dxs_rk, dxs_lo, gP["rwkv_w0"], dw2p, gP["rwkv_a0"], da2p, dg2p, gP["rwkv_k_k"], gP["rwkv_k_a"] = rowmap(
        pre_bwd, pre_rows, pre_params, [(RKV, F32), (LORA, F32)],
        [(1, D_MODEL), (LORA, D_MODEL), (1, D_MODEL), (LORA, D_MODEL), (LORA, D_MODEL), (1, D_MODEL), (1, D_MODEL)],
        tb=128, name="rwkv_pre_bwd")
    gW["rwkv_w2"] = dw2p[:LORA_W]
    gW["rwkv_a2"] = da2p[LORA_W:LORA_W + LORA_A]
    gW["rwkv_g2"] = dg2p[LORA_W + LORA_A:]
    dp_rk, dmu_rk = token_shift_bwd(dxs_rk, p_rk, mu_rk, tb=256, name="shift_rk_bwd")
    dp_lo, dmu_lo = token_shift_bwd(dxs_lo, p_lo, mu_lo, tb=256, name="shift_lora_bwd")
    gP["rwkv_mu"] = jnp.concatenate([dmu_rk, dmu_lo], axis=1)

    dh = matmul(dp_rk, w_rkv, "nt", "dh_rkv")
    dh = matmul(dp_lo, w_lora, "nt", "dh_lora", add=dh)
    dh = matmul(dp_qkv, w_qkv, "nt", "dh_qkv", add=dh)
    dh = matmul(dp_gate, w_gate, "nt", "dh_gate", add=dh)
    gW["w_in"] = jnp.concatenate([
        matmul(h, dp_rk, "tn", "dw_rkv"), matmul(h, dp_lo, "tn", "dw_lora"),
        matmul(h, dp_qkv, "tn", "dw_qkv"), matmul(h, dp_gate, "tn", "dw_gate")], axis=1)

    token = on_mixer_grads(gW)

    def norm_bwd(x_b, dh_b, dy_b, gn, tok):
        dx, dgn = jax.vjp(_rms, x_b, gn)[1](dh_b)
        return dy_b + dx + tok[0:1, 0:1], dgn

    dx1, gP["mix_norm"] = rowmap(norm_bwd, [x1, dh, dx2], [P["mix_norm"], token], [(D_MODEL, F32)],
                                 [(1, D_MODEL)], tb=256, name="mix_norm_bwd")
    dx, gP["ffn1_norm"], gW["ffn1_w_in"], gW["ffn1_w_out"] = _ffn_bwd(
        dx1, ffn1_saved, P["ffn1_norm"], W["ffn1_w_in"], W["ffn1_w_out"], "ffn1")
    return loss_cols, dx, gW, gP


N_SHARDS = 4
BIG = (("ffn1_w_in", (D_MODEL, 2 * D_FF), 1), ("ffn1_w_out", (D_FF, D_MODEL), 0),
       ("w_in", (D_MODEL, 7712), 1), ("rwkv_w2", (LORA_W, D_MODEL), 1), ("rwkv_a2", (LORA_A, D_MODEL), 1),
       ("rwkv_g2", (LORA_G, D_MODEL), 1), ("w_proj_rwkv", (D_MODEL, D_MODEL), 0),
       ("w_proj_attn", (ATTN_WIDTH, D_MODEL), 1), ("w_out", (D_MODEL, D_MODEL), 0),
       ("ffn2_w_in", (D_MODEL, 2 * D_FF), 1), ("ffn2_w_out", (D_FF, D_MODEL), 0))
SMALL = (("ffn1_norm", 1024), ("mix_norm", 1024), ("b_gate", 2048), ("rwkv_mu", 3360), ("rwkv_w0", 1024),
         ("rwkv_a0", 1024), ("rwkv_k_k", 1024), ("rwkv_k_a", 1024), ("rwkv_r_k", 1024), ("rwkv_ln_w", 1024),
         ("rwkv_ln_b", 1024), ("attn_q_norm", 64), ("attn_k_norm", 64), ("ffn2_norm", 1024))
WEIGHT_ORDER = ("ffn1_norm", "ffn1_w_in", "ffn1_w_out", "mix_norm", "w_in", "b_gate", "rwkv_mu", "rwkv_w0",
                "rwkv_w2", "rwkv_a0", "rwkv_a2", "rwkv_g2", "rwkv_k_k", "rwkv_k_a", "rwkv_r_k", "rwkv_ln_w",
                "rwkv_ln_b", "attn_q_norm", "attn_k_norm", "w_proj_rwkv", "w_proj_attn", "w_out", "ffn2_norm",
                "ffn2_w_in", "ffn2_w_out")


LORA_PARTS = ("rwkv_w2", "rwkv_a2", "rwkv_g2")
BLOCK_MAJOR = ("ffn1_w_in", "ffn2_w_in")
FIRST_FFN = ("ffn1_w_in", "ffn1_w_out")
SMALL_USED = D_MODEL + sum(n for _, n in SMALL)
SMALL_W = -(-SMALL_USED // 128) * 128


def _travel():
    out = {}
    for name, shape, axis in BIG:
        if name == LORA_PARTS[0]:
            out["lora"] = ((LORA, D_MODEL), 1)
        elif name not in LORA_PARTS:
            out[name] = (shape, axis)
    return out


def local_blocks(vals):
    out = {n: vals[n] for n in _travel() if n != "lora"}
    out["lora"] = jnp.concatenate([vals[n] for n in LORA_PARTS], axis=0)
    return out


def split_lora(t):
    return {"rwkv_w2": t[:LORA_W], "rwkv_a2": t[LORA_W:LORA_W + LORA_A], "rwkv_g2": t[LORA_W + LORA_A:]}


def blocks_to_full(name, blocks):
    shape, axis = _travel()[name]
    if name in BLOCK_MAJOR:
        return blocks
    if axis == 0:
        return blocks.reshape(shape)
    return blocks.transpose(1, 0, 2).reshape(shape)


def full_to_blocks(name, full):
    shape, axis = _travel()[name]
    if name in BLOCK_MAJOR:
        return full
    if axis == 0:
        return full.reshape(N_SHARDS, shape[0] // N_SHARDS, shape[1])
    return full.reshape(shape[0], N_SHARDS, shape[1] // N_SHARDS).transpose(1, 0, 2)


def pack_small(vals, head):
    parts = [head] + [vals[name].reshape(1, n) for name, n in SMALL]
    parts.append(jnp.zeros((1, SMALL_W - SMALL_USED), F32))
    return jnp.concatenate(parts, axis=1)


def unpack_small(vec, shapes):
    out, off = {}, D_MODEL
    for name, n in SMALL:
        out[name] = vec[:, off:off + n].reshape(shapes[name])
        off += n
    return out


def _place():
    return lax.axis_index("x"), lax.axis_index("y"), lax.axis_index("c")


def _other_chips(x, y):
    return [(1 - x, y), (x, 1 - y), (1 - x, 1 - y)]


def _remote(src, dst, send_sem, recv_sem, device):
    return pltpu.make_async_remote_copy(src_ref=src, dst_ref=dst, send_sem=send_sem, recv_sem=recv_sem,
                                        device_id=device, device_id_type=MESH)


def _half(ref, who):
    hr = ref.shape[-2] // 2
    rows = pl.ds(pl.multiple_of(who * hr, 8), hr)
    return ref.at[rows] if len(ref.shape) == 2 else ref.at[:, rows]


HBM_REF = pl.BlockSpec(memory_space=pl.ANY)
COMM_PARAMS = dict(compiler_params=pltpu.CompilerParams(has_side_effects=True))


def gather_weights(blocks):
    n = len(blocks)

    def body(*refs):
        ins, outs = refs[:n], refs[n:2 * n]
        ici_send, ici_recv, d2d_send, d2d_recv = refs[2 * n:]
        x, y, c = _place()
        me, sibling, chips = 2 * x + y, (x, y, 1 - c), _other_chips(x, y)
        first = [_remote(_half(ins[t], c), _half(outs[t].at[me], c), ici_send.at[k, t], ici_recv.at[k, t],
                         (px, py, c)) for k, (px, py) in enumerate(chips) for t in range(n)]
        for cp in first:
            cp.start()
        passed = []
        for k, (px, py) in enumerate(chips):
            for t in range(n):
                landed = _half(outs[t].at[2 * px + py], c)
                _remote(landed, landed, ici_send.at[k, t], ici_recv.at[k, t], (px, py, c)).wait_recv()
                cp = _remote(landed, landed, d2d_send.at[k, t], d2d_recv.at[k, t], sibling)
                cp.start()
                passed.append(cp)
        for k, (px, py) in enumerate(chips):
            for t in range(n):
                other = _half(outs[t].at[2 * px + py], 1 - c)
                _remote(other, other, d2d_send.at[k, t], d2d_recv.at[k, t], sibling).wait_recv()
        for cp in first + passed:
            cp.wait_send()

    res = pl.pallas_call(
        body, name="gather_weights", in_specs=[HBM_REF] * n, out_specs=[HBM_REF] * n,
        out_shape=[jax.ShapeDtypeStruct((N_SHARDS,) + b.shape, b.dtype) for b in blocks],
        scratch_shapes=[pltpu.SemaphoreType.DMA((3, n))] * 4, **COMM_PARAMS)(*blocks)
    me = 2 * lax.axis_index("x") + lax.axis_index("y")
    return [lax.dynamic_update_slice(g, b[None], (me, 0, 0)) for g, b in zip(res, blocks)]


def _gather_copies(ins, outs, send_sem, recv_sem):
    x, y, c = _place()
    return [_remote(_half(ins[t], c), _half(outs[t].at[2 * x + y], c), send_sem(k, t), recv_sem(k, t), (px, py, c))
            for k, (px, py) in enumerate(_other_chips(x, y)) for t in range(len(ins))]


def gather_start(blocks, name):
    n = len(blocks)
    n_cp = 3 * n

    def body(*refs):
        ins, outs = refs[:n], refs[n:2 * n]
        sems, token = refs[2 * n:2 * n + 2 * n_cp], refs[-1]
        for cp in _gather_copies(ins, outs, lambda k, t: sems[k * n + t], lambda k, t: sems[n_cp + k * n + t]):
            cp.start()
        token[...] = jnp.zeros_like(token)

    hbm = lambda a: pltpu.with_memory_space_constraint(a, pltpu.HBM)
    landing = [lax.empty((N_SHARDS,) + b.shape, b.dtype) for b in blocks]
    res = pl.pallas_call(
        body, name=name,
        out_shape=(*[pltpu.SemaphoreType.DMA(())] * (2 * n_cp),
                   *[pltpu.HBM(a.shape, a.dtype) for a in list(blocks) + landing], jax.ShapeDtypeStruct((8, 128), F32)),
        in_specs=[SPLIT_HBM] * (2 * n),
        out_specs=(*[SPLIT_SEM] * (2 * n_cp), *[SPLIT_HBM] * (2 * n), pl.BlockSpec(memory_space=pltpu.VMEM)),
        input_output_aliases={t: 2 * n_cp + t for t in range(2 * n)}, **SPLIT_PARAMS,
    )(*[hbm(a) for a in list(blocks) + landing])
    return (n, res[:-1]), res[-1]


def gather_wait(handles, after, name):
    n, held = handles
    n_cp = 3 * n
    sems, thru = held[:2 * n_cp], held[2 * n_cp:]

    def body(*refs):
        ins, outs = refs[:n], refs[n:2 * n]
        sem_refs = refs[2 * n:2 * n + 2 * n_cp]
        for cp in _gather_copies(ins, outs, lambda k, t: sem_refs[k * n + t], lambda k, t: sem_refs[n_cp + k * n + t]):
            cp.wait_send()
            cp.wait_recv()

    res = pl.pallas_call(
        body, name=name, out_shape=tuple(pltpu.HBM(a.shape, a.dtype) for a in thru),
        in_specs=[SPLIT_HBM] * (2 * n) + [SPLIT_SEM] * (2 * n_cp) + [pl.BlockSpec(memory_space=pl.ANY)],
        out_specs=tuple([SPLIT_HBM] * (2 * n)), input_output_aliases={t: t for t in range(2 * n)}, **SPLIT_PARAMS,
    )(*thru, *sems, after)
    return list(res[n:])


def pass_halves(gathered, blocks):
    n = len(gathered)

    def body(*refs):
        outs = refs[n:2 * n]
        send_sems, recv_sems = refs[2 * n:]
        x, y, c = _place()
        slots = [2 * px + py for px, py in _other_chips(x, y)]
        give = [_remote(_half(outs[t].at[s], c), _half(outs[t].at[s], c), send_sems.at[k, t], recv_sems.at[k, t],
                        (x, y, 1 - c)) for k, s in enumerate(slots) for t in range(n)]
        for cp in give:
            cp.start()
        for k, s in enumerate(slots):
            for t in range(n):
                other = _half(outs[t].at[s], 1 - c)
                _remote(other, other, send_sems.at[k, t], recv_sems.at[k, t], (x, y, 1 - c)).wait_recv()
        for cp in give:
            cp.wait_send()

    res = pl.pallas_call(
        body, name="pass_halves", in_specs=[HBM_REF] * n, out_specs=[HBM_REF] * n,
        out_shape=[jax.ShapeDtypeStruct(g.shape, g.dtype) for g in gathered],
        input_output_aliases={t: t for t in range(n)},
        scratch_shapes=[pltpu.SemaphoreType.DMA((3, n))] * 2, **COMM_PARAMS)(*gathered)
    me = 2 * lax.axis_index("x") + lax.axis_index("y")
    return [lax.dynamic_update_slice(g, b[None], (me, 0, 0)) for g, b in zip(res, blocks)]


def swap_halves(grads):
    n = len(grads)

    def body(*refs):
        ins, got = refs[:n], refs[n:2 * n]
        send_sems, recv_sems = refs[2 * n:]
        x, y, c = _place()
        give = [_remote(_half(ins[t], 1 - c), got[t], send_sems.at[t], recv_sems.at[t], (x, y, 1 - c))
                for t in range(n)]
        for cp in give:
            cp.start()
        for cp in give:
            cp.wait_recv()
        for cp in give:
            cp.wait_send()

    return pl.pallas_call(
        body, name="swap_halves", in_specs=[HBM_REF] * n, out_specs=[HBM_REF] * n,
        out_shape=[jax.ShapeDtypeStruct((g.shape[0], g.shape[1] // 2, g.shape[2]), g.dtype) for g in grads],
        scratch_shapes=[pltpu.SemaphoreType.DMA((n,))] * 2, **COMM_PARAMS)(*grads)


def join_halves(blocks):
    n = len(blocks)

    def body(*refs):
        outs = refs[n:2 * n]
        send_sems, recv_sems = refs[2 * n:]
        x, y, c = _place()
        give = [_remote(_half(outs[t], c), _half(outs[t], c), send_sems.at[t], recv_sems.at[t], (x, y, 1 - c))
                for t in range(n)]
        for cp in give:
            cp.start()
        for t in range(n):
            arriving = _half(outs[t], 1 - c)
            _remote(arriving, arriving, send_sems.at[t], recv_sems.at[t], (x, y, 1 - c)).wait_recv()
        for cp in give:
            cp.wait_send()

    return pl.pallas_call(
        body, name="join_halves", in_specs=[HBM_REF] * n, out_specs=[HBM_REF] * n,
        out_shape=[jax.ShapeDtypeStruct(b.shape, b.dtype) for b in blocks],
        input_output_aliases={t: t for t in range(n)},
        scratch_shapes=[pltpu.SemaphoreType.DMA((n,))] * 2, **COMM_PARAMS)(*blocks)


SPLIT_HBM = pl.BlockSpec(memory_space=pltpu.HBM)
SPLIT_SEM = pl.BlockSpec(memory_space=pltpu.SEMAPHORE)
SPLIT_PARAMS = dict(compiler_params=pltpu.CompilerParams(has_side_effects=pltpu.SideEffectType.DATAFLOW_SIDE_EFFECTING))


def _scatter_copies(parts, landed, send_sem, recv_sem):
    x, y, c = _place()
    return [_remote(parts[t].at[2 * px + py], landed[t].at[k], send_sem(k, t), recv_sem(k, t), (px, py, c))
            for k, (px, py) in enumerate(_other_chips(x, y)) for t in range(len(parts))]


def scatter_start(partials, name):
    n = len(partials)
    n_cp = 3 * n

    def body(*refs):
        parts, landed = refs[:n], refs[n:2 * n]
        sems, token = refs[2 * n:2 * n + 2 * n_cp], refs[-1]
        for cp in _scatter_copies(parts, landed, lambda k, t: sems[k * n + t], lambda k, t: sems[n_cp + k * n + t]):
            cp.start()
        token[...] = jnp.zeros_like(token)

    hbm = lambda a: pltpu.with_memory_space_constraint(a, pltpu.HBM)
    landing = [lax.empty((3,) + p.shape[1:], p.dtype) for p in partials]
    res = pl.pallas_call(
        body, name=name,
        out_shape=(*[pltpu.SemaphoreType.DMA(())] * (2 * n_cp),
                   *[pltpu.HBM(a.shape, a.dtype) for a in partials + landing], jax.ShapeDtypeStruct((8, 128), F32)),
        in_specs=[SPLIT_HBM] * (2 * n),
        out_specs=(*[SPLIT_SEM] * (2 * n_cp), *[SPLIT_HBM] * (2 * n), pl.BlockSpec(memory_space=pltpu.VMEM)),
        input_output_aliases={t: 2 * n_cp + t for t in range(2 * n)}, **SPLIT_PARAMS,
    )(*[hbm(a) for a in partials + landing])
    return (n, res[:-1]), res[-1]


def scatter_wait(handles, after, name):
    n, held = handles
    n_cp = 3 * n
    sems, thru = held[:2 * n_cp], held[2 * n_cp:]

    def body(*refs):
        parts, landed = refs[:n], refs[n:2 * n]
        sem_refs = refs[2 * n:2 * n + 2 * n_cp]
        for cp in _scatter_copies(parts, landed, lambda k, t: sem_refs[k * n + t],
                                  lambda k, t: sem_refs[n_cp + k * n + t]):
            cp.wait_send()
            cp.wait_recv()

    res = pl.pallas_call(
        body, name=name, out_shape=tuple(pltpu.HBM(a.shape, a.dtype) for a in thru),
        in_specs=[SPLIT_HBM] * (2 * n) + [SPLIT_SEM] * (2 * n_cp) + [pl.BlockSpec(memory_space=pl.ANY)],
        out_specs=tuple([SPLIT_HBM] * (2 * n)), input_output_aliases={t: t for t in range(2 * n)}, **SPLIT_PARAMS,
    )(*thru, *sems, after)
    return list(res[n:])


def chip_sums(grads):
    names = list(grads)
    got = swap_halves([grads[n] for n in names])
    partials = []
    for name, theirs in zip(names, got):
        n_slot, hr, width = theirs.shape
        tb = _row_block(hr, width, 6)
        per_half = hr // tb
        mine = lambda i, s, per_half=per_half: (i // per_half) * 2 * per_half + s[0] * per_half + i % per_half
        p = placed_map(
            jnp.add,
            [(grads[name].reshape(2 * n_slot * hr, width), mine), (theirs.reshape(n_slot * hr, width), lambda i, s: i)],
            (n_slot * hr, width, BF16, lambda i, s: i), n_blocks=n_slot * per_half, tb=tb, name="chip_sum_" + name)
        partials.append(p.reshape(theirs.shape))
    return got, partials


def owner_sums(grads, got, landed):
    names = list(grads)
    blocks = []
    for name, theirs, arrived in zip(names, got, landed):
        n_slot, hr, width = theirs.shape
        tb = _row_block(hr, width, 6)
        per_half = hr // tb
        views = [(grads[name].reshape(2 * n_slot * hr, width),
                  lambda i, s, per_half=per_half: s[1] * 2 * per_half + s[0] * per_half + i),
                 (theirs.reshape(n_slot * hr, width), lambda i, s, per_half=per_half: s[1] * per_half + i)]
        views += [(arrived.reshape(3 * hr, width), functools.partial(lambda k, per_half, i, s: k * per_half + i,
                                                                     k, per_half)) for k in range(3)]
        f = lambda a, b, l0, l1, l2: (((a + b) + l0.astype(F32)) + l1.astype(F32)) + l2.astype(F32)
        blocks.append(placed_map(
            f, views,(2 * hr, width, F32, lambda i, s, per_half=per_half: s[0] * per_half + i),
            n_blocks=per_half, tb=tb, name="owner_sum_" + name))
    return dict(zip(names, join_halves(blocks)))


def adamw_block(name, w, g, m, v):
    rows, width = w.shape
    return rowmap(_adamw, [w, g, m, v], [], [(width, F32)] * 3, tb=_row_block(rows, width, 7),
                  name="adamw_" + name)


def reduce_small(vec, w, m, v):
    n_dev = 8

    def body(vec_ref, w_ref, m_ref, v_ref, loss_ref, g_ref, d_ref, m2_ref, v2_ref, slots, send_sems, recv_sems):
        x, y, c = _place()
        me = 4 * x + 2 * y + c
        slots[me] = vec_ref[...]
        flips = [(fx, fy, fc) for fx in (0, 1) for fy in (0, 1) for fc in (0, 1)][1:]
        peers = [(1 - x if fx else x, 1 - y if fy else y, 1 - c if fc else c) for fx, fy, fc in flips]
        sends = [pltpu.make_async_remote_copy(
            src_ref=vec_ref, dst_ref=slots.at[me], send_sem=send_sems.at[j], recv_sem=recv_sems.at[j],
            device_id=peer, device_id_type=MESH) for j, peer in enumerate(peers)]
        for cp in sends:
            cp.start()
        for j, (px, py, pc) in enumerate(peers):
            pltpu.make_async_remote_copy(
                src_ref=vec_ref, dst_ref=slots.at[4 * px + 2 * py + pc], send_sem=send_sems.at[j],
                recv_sem=recv_sems.at[j], device_id=(px, py, pc), device_id_type=MESH).wait_recv()
        for cp in sends:
            cp.wait_send()
        g = slots[0]
        for d in range(1, n_dev):
            g = g + slots[d]
        loss_ref[...] = jnp.sum(g[:, :D_MODEL], axis=1, keepdims=True)
        delta, m2, v2 = _adamw(w_ref[...], g, m_ref[...], v_ref[...])
        g_ref[...], d_ref[...], m2_ref[...], v2_ref[...] = g, delta, m2, v2

    vm = pl.BlockSpec(memory_space=pltpu.VMEM)
    vec_t = jax.ShapeDtypeStruct(vec.shape, F32)
    return pl.pallas_call(
        body, name="reduce_small", in_specs=[vm] * 4, out_specs=[vm] * 5,
        out_shape=[jax.ShapeDtypeStruct((1, 1), F32)] + [vec_t] * 4,
        scratch_shapes=[pltpu.VMEM((n_dev,) + vec.shape, F32), pltpu.SemaphoreType.DMA((n_dev - 1,)),
                        pltpu.SemaphoreType.DMA((n_dev - 1,))],
        compiler_params=pltpu.CompilerParams(has_side_effects=True),
    )(vec, w, m, v)


def kernel(x, ffn1_norm, ffn1_w_in, ffn1_w_out, mix_norm, w_in, b_gate, rwkv_mu, rwkv_w0, rwkv_w2, rwkv_a0, rwkv_a2, rwkv_g2, rwkv_k_k, rwkv_k_a, rwkv_r_k, rwkv_ln_w, rwkv_ln_b, attn_q_norm, attn_k_norm, w_proj_rwkv, w_proj_attn, w_out, ffn2_norm, ffn2_w_in, ffn2_w_out, loss_target, m_ffn1_norm, m_ffn1_w_in, m_ffn1_w_out, m_mix_norm, m_w_in, m_b_gate, m_rwkv_mu, m_rwkv_w0, m_rwkv_w2, m_rwkv_a0, m_rwkv_a2, m_rwkv_g2, m_rwkv_k_k, m_rwkv_k_a, m_rwkv_r_k, m_rwkv_ln_w, m_rwkv_ln_b, m_attn_q_norm, m_attn_k_norm, m_w_proj_rwkv, m_w_proj_attn, m_w_out, m_ffn2_norm, m_ffn2_w_in, m_ffn2_w_out, v_ffn1_norm, v_ffn1_w_in, v_ffn1_w_out, v_mix_norm, v_w_in, v_b_gate, v_rwkv_mu, v_rwkv_w0, v_rwkv_w2, v_rwkv_a0, v_rwkv_a2, v_rwkv_g2, v_rwkv_k_k, v_rwkv_k_a, v_rwkv_r_k, v_rwkv_ln_w, v_rwkv_ln_b, v_attn_q_norm, v_attn_k_norm, v_w_proj_rwkv, v_w_proj_attn, v_w_out, v_ffn2_norm, v_ffn2_w_in, v_ffn2_w_out):
    given = dict(locals())
    weights = {n: given[n] for n in WEIGHT_ORDER}
    mom_m = {n: given["m_" + n] for n in WEIGHT_ORDER}
    mom_v = {n: given["v_" + n] for n in WEIGHT_ORDER}
    big = [name for name, _, _ in BIG]
    shapes = {n: weights[n].shape for n in WEIGHT_ORDER}
    blocks_of = lambda d: local_blocks({n: d[n][0] for n in big})
    w_blk, m_blk, v_blk = blocks_of(weights), blocks_of(mom_m), blocks_of(mom_v)
    names = list(w_blk)

    early = [n for n in names if n not in FIRST_FFN]
    bf16_block = lambda n: w_blk[n].astype(BF16)
    W = {n: blocks_to_full(n, g) for n, g in zip(FIRST_FFN, gather_weights([bf16_block(n) for n in FIRST_FFN]))}
    rest_blocks = [bf16_block(n) for n in early]
    gather_handles, start_token = gather_start(rest_blocks, "gather_start")

    def rest_weights(x1):
        got = pass_halves(gather_wait(gather_handles, x1, "gather_wait"), rest_blocks)
        w_rest = {n: blocks_to_full(n, g) for n, g in zip(early, got)}
        w_rest.update(split_lora(w_rest.pop("lora")))
        return w_rest

    P = {n: weights[n].reshape(1, -1) for n, _ in SMALL}

    sent = {}

    def send_early(gw):
        lora = jnp.concatenate([gw[n] for n in LORA_PARTS], axis=0)
        sent["grads"] = {n: full_to_blocks(n, lora if n == "lora" else gw[n]) for n in early}
        sent["got"], partials = chip_sums(sent["grads"])
        sent["handles"], token = scatter_start(partials, "scatter_start")
        return token

    loss_cols, dx, gW, gP = layer_step(x[0], loss_target[0], W, P, start_token, rest_weights, send_early)
    landed = scatter_wait(sent["handles"], gP["ffn1_norm"], "scatter_wait")

    late = {n: full_to_blocks(n, gW[n]) for n in FIRST_FFN}
    late_got, late_partials = chip_sums(late)
    late_handles, late_token = scatter_start(late_partials, "scatter_start_ffn1")
    landed[-1] = landed[-1] + late_token[0, 0].astype(landed[-1].dtype)
    out_g, out_d, out_m, out_v = {}, {}, {}, {}

    def apply(g_blk):
        for n in g_blk:
            res = (g_blk[n], *adamw_block(n, w_blk[n], g_blk[n], m_blk[n], v_blk[n]))
            for dst, t in zip((out_g, out_d, out_m, out_v), res):
                for part, val in (split_lora(t) if n == "lora" else {n: t}).items():
                    dst[part] = val.reshape(shapes[part])

    apply(owner_sums(sent["grads"], sent["got"], landed))
    apply(owner_sums(late, late_got, scatter_wait(late_handles, list(out_d.values())[-1], "scatter_wait_ffn1")))

    zero_head = jnp.zeros((1, D_MODEL), F32)
    vec = pack_small(gP, loss_cols)
    loss, g_s, d_s, m_s, v_s = reduce_small(
        vec, pack_small({n: weights[n] for n, _ in SMALL}, zero_head),
        pack_small({n: mom_m[n] for n, _ in SMALL}, zero_head),
        pack_small({n: mom_v[n] for n, _ in SMALL}, zero_head))
    for dst, src in ((out_g, g_s), (out_d, d_s), (out_m, m_s), (out_v, v_s)):
        dst.update(unpack_small(src, shapes))

    return (loss[0, 0], dx[None], *[out_g[n] for n in WEIGHT_ORDER], *[out_d[n] for n in WEIGHT_ORDER],
            *[out_m[n] for n in WEIGHT_ORDER], *[out_v[n] for n in WEIGHT_ORDER])
```

```python
import functools

import jax
import jax.numpy as jnp
from jax import lax
from jax.experimental import pallas as pl
from jax.experimental.pallas import tpu as pltpu

F32 = jnp.float32
BF16 = jnp.bfloat16
MESH = pl.DeviceIdType.MESH

D_MODEL = 1024
HEAD_DIM = 64
RWKV_HEADS = 16
LORA_W, LORA_A, LORA_G = 64, 64, 160
LORA = LORA_W + LORA_A + LORA_G
RKV = 3 * D_MODEL
ATTN_PAIRS = ((128, 1), (512, 4), (2048, 16))
ATTN_BLK = 128
ATTN_HPG = 4
ATTN_WIDTH = 768
GROUP_W = ATTN_HPG * HEAD_DIM
D_FF = 2816
GN_EPS = 64e-5
RMS_EPS = 1e-6
NEG_INF = -1e30
WKV_CHUNK = 64
WKV_HEADS_PER_STEP = 16

ADAM_LR, ADAM_B1, ADAM_B2, ADAM_EPS, ADAM_WD, ADAM_STEP = 0.001, 0.9, 0.999, 1e-08, 0.01, 10

V7X_VMEM_BYTES = 64 << 20
VMEM_TEMP_ALLOWANCE = 20 << 20


def _cparams(sem, block_bytes):
    limit = min(2 * block_bytes + VMEM_TEMP_ALLOWANCE, V7X_VMEM_BYTES - (6 << 20))
    return pltpu.CompilerParams(dimension_semantics=sem, vmem_limit_bytes=int(limit))


def _nbytes(shape, dtype):
    n = 1
    for s in shape:
        n *= s
    return n * jnp.dtype(dtype).itemsize


def _split_bf16(a):
    hi = a.astype(BF16)
    return hi, (a - hi.astype(F32)).astype(BF16)


def _make_dots():
    def raw(a, b, ca, cb):
        return lax.dot_general(a.astype(BF16), b.astype(BF16), (((ca,), (cb,)), ((), ())),
                               preferred_element_type=F32)

    @jax.custom_vjp
    def nn(a, b):
        return raw(a, b, 1, 0)

    @jax.custom_vjp
    def nt(a, b):
        return raw(a, b, 1, 1)

    @jax.custom_vjp
    def tn(a, b):
        return raw(a, b, 0, 0)

    nn.defvjp(lambda a, b: (raw(a, b, 1, 0), (a, b)),
              lambda res, g: (raw(g, res[1], 1, 1), raw(res[0], g, 0, 0)))
    nt.defvjp(lambda a, b: (raw(a, b, 1, 1), (a, b)),
              lambda res, g: (raw(g, res[1], 1, 0), raw(g, res[0], 0, 0)))
    tn.defvjp(lambda a, b: (raw(a, b, 0, 0), (a, b)),
              lambda res, g: (raw(res[1], g, 1, 1), raw(res[0], g, 1, 0)))
    return nn, nt, tn


def _exact_rhs_dot(x, ones, cx, co):
    hi, lo = _split_bf16(x)
    dims = (((cx,), (co,)), ((), ()))
    return (lax.dot_general(hi, ones, dims, preferred_element_type=F32)
            + lax.dot_general(lo, ones, dims, preferred_element_type=F32))


@jax.custom_vjp
def SEG(x, ones):
    return _exact_rhs_dot(x, ones, 1, 0)


SEG.defvjp(lambda x, ones: (_exact_rhs_dot(x, ones, 1, 0), ones),
           lambda ones, g: (_exact_rhs_dot(g, ones, 1, 1), jnp.zeros_like(ones)))

NN, NT, TN = _make_dots()


MM_TILE_M, MM_TILE_N, MM_TILE_K = 1408, 1408, 1536


def _pick(n, cap):
    best = None
    for t in range(128, min(n, cap) + 1, 128):
        if n % t == 0:
            best = t
    return best or n


def matmul(a, b, mode, name, *, add=None, scale=1.0, out_dtype=F32):
    if mode == "nn":
        (M, K), (K2, N) = a.shape, b.shape
    elif mode == "nt":
        (M, K), (N, K2) = a.shape, b.shape
    else:
        (K, M), (K2, N) = a.shape, b.shape
    assert K == K2, (name, a.shape, b.shape)
    tm, tn, tk = _pick(M, MM_TILE_M), _pick(N, MM_TILE_N), _pick(K, MM_TILE_K)
    nk = K // tk
    ca, cb = {"nn": (1, 0), "nt": (1, 1), "tn": (0, 0)}[mode]

    def body(*refs):
        if add is None:
            a_ref, b_ref, o_ref, acc_ref = refs
        else:
            a_ref, b_ref, add_ref, o_ref, acc_ref = refs
        k = pl.program_id(2)

        @pl.when(k == 0)
        def _():
            acc_ref[...] = jnp.zeros_like(acc_ref)

        acc_ref[...] += lax.dot_general(a_ref[...].astype(BF16), b_ref[...].astype(BF16),
                                        (((ca,), (cb,)), ((), ())), preferred_element_type=F32)

        @pl.when(k == nk - 1)
        def _():
            r = acc_ref[...] * scale
            if add is not None:
                r = add_ref[...] + r
            o_ref[...] = r.astype(o_ref.dtype)

    a_spec = (pl.BlockSpec((tk, tm), lambda i, j, k: (k, i)) if mode == "tn"
              else pl.BlockSpec((tm, tk), lambda i, j, k: (i, k)))
    b_spec = (pl.BlockSpec((tn, tk), lambda i, j, k: (j, k)) if mode == "nt"
              else pl.BlockSpec((tk, tn), lambda i, j, k: (k, j)))
    in_specs, args = [a_spec, b_spec], [a, b]
    blk = tm * tk * a.dtype.itemsize + tk * tn * b.dtype.itemsize + tm * tn * 8
    if add is not None:
        in_specs.append(pl.BlockSpec((tm, tn), lambda i, j, k: (i, j)))
        args.append(add)
        blk += tm * tn * 4
    return pl.pallas_call(
        body, name=name, grid=(M // tm, N // tn, nk),
        in_specs=in_specs, out_specs=pl.BlockSpec((tm, tn), lambda i, j, k: (i, j)),
        out_shape=jax.ShapeDtypeStruct((M, N), out_dtype),
        scratch_shapes=[pltpu.VMEM((tm, tn), F32)],
        compiler_params=_cparams(("parallel", "parallel", "arbitrary"), blk),
    )(*args)


def matmul_cs(a, w, mode, name, *, scale=1.0, out_dtype=F32):
    n_blk = N_SHARDS
    if mode == "tn":
        (K, R), Cs = a.shape, w.shape[2] // 2
        tm, tk = _pick(R, MM_TILE_M), _pick(K, 1024)
        grid = (R // tm, n_blk, K // tk)
        a_spec = pl.BlockSpec((tk, tm), lambda i, j, k: (k, i))
        w_spec = pl.BlockSpec((None, tk, Cs), lambda i, j, k: (j // 2, k, j % 2))
        o_spec = pl.BlockSpec((None, tm, Cs), lambda i, j, k: (j, i, 0))
        out_shape, acc_shape, dims = (n_blk, R, Cs), (tm, Cs), (0, 0)
        blk = tk * tm * a.dtype.itemsize + tk * Cs * w.dtype.itemsize + tm * Cs * 8
    else:
        M, (_, R, Cs) = a.shape[1], w.shape
        tm, tn = _pick(M, MM_TILE_M), _pick(R, MM_TILE_N)
        grid = (M // tm, R // tn, n_blk)
        a_spec = pl.BlockSpec((None, tm, Cs), lambda i, j, k: (k // 2, i, k % 2))
        w_spec = pl.BlockSpec((None, tn, Cs), lambda i, j, k: (k, j, 0))
        o_spec = pl.BlockSpec((tm, tn), lambda i, j, k: (i, j))
        out_shape, acc_shape, dims = (M, R), (tm, tn), (1, 1)
        blk = tm * Cs * a.dtype.itemsize + tn * Cs * w.dtype.itemsize + tm * tn * 8
    nk = grid[2]

    def body(a_ref, w_ref, o_ref, acc_ref):
        k = pl.program_id(2)

        @pl.when(k == 0)
        def _():
            acc_ref[...] = jnp.zeros_like(acc_ref)

        acc_ref[...] += lax.dot_general(a_ref[...].astype(BF16), w_ref[...].astype(BF16),
                                        (((dims[0],), (dims[1],)), ((), ())), preferred_element_type=F32)

        @pl.when(k == nk - 1)
        def _():
            o_ref[...] = (acc_ref[...] * scale).astype(o_ref.dtype)

    return pl.pallas_call(
        body, name=name, grid=grid, in_specs=[a_spec, w_spec], out_specs=o_spec,
        out_shape=jax.ShapeDtypeStruct(out_shape, out_dtype), scratch_shapes=[pltpu.VMEM(acc_shape, F32)],
        compiler_params=_cparams(("parallel", "parallel", "arbitrary"), blk),
    )(a, w)


FFN_TILE_M = 512


def _swiglu(gate, up):
    return gate * jax.nn.sigmoid(gate) * up


def ffn_in_act(h, w, name):
    (M, R), Cs, half = h.shape, w.shape[2], N_SHARDS // 2
    tm, tk = _pick(M, FFN_TILE_M), _pick(R, 1024)
    nk = R // tk

    def body(h_ref, wg_ref, wu_ref, gu_ref, act_ref, acc_ref):
        k = pl.program_id(2)

        @pl.when(k == 0)
        def _():
            acc_ref[...] = jnp.zeros_like(acc_ref)

        hb = h_ref[...].astype(BF16)
        for part, w_ref in enumerate((wg_ref, wu_ref)):
            acc_ref[part] += jnp.dot(hb, w_ref[...].astype(BF16), preferred_element_type=F32)

        @pl.when(k == nk - 1)
        def _():
            gu_ref[...] = acc_ref[...]
            act_ref[...] = _swiglu(acc_ref[0], acc_ref[1]).astype(act_ref.dtype)

    w_spec = lambda off: pl.BlockSpec((None, tk, Cs), functools.partial(lambda off, i, j, k: (j + off, k, 0), off))
    blk = tm * tk * h.dtype.itemsize + 2 * tk * Cs * w.dtype.itemsize + tm * Cs * (16 + 2)
    return pl.pallas_call(
        body, name=name, grid=(M // tm, half, nk),
        in_specs=[pl.BlockSpec((tm, tk), lambda i, j, k: (i, k)), w_spec(0), w_spec(half)],
        out_specs=[pl.BlockSpec((2, tm, Cs), lambda i, j, k: (0, i, j)), pl.BlockSpec((tm, Cs), lambda i, j, k: (i, j))],
        out_shape=[jax.ShapeDtypeStruct((2, M, half * Cs), F32), jax.ShapeDtypeStruct((M, half * Cs), BF16)],
        scratch_shapes=[pltpu.VMEM((2, tm, Cs), F32)],
        compiler_params=_cparams(("parallel", "parallel", "arbitrary"), blk),
    )(h, w, w)


def ffn_dact_dgu(dy, w_out, gu, scale, name):
    (M, D), F = dy.shape, w_out.shape[0]
    tm, tn = _pick(M, FFN_TILE_M), F // 2

    def body(dy_ref, w_ref, gu_ref, dgu_ref):
        dact = scale * lax.dot_general(dy_ref[...].astype(BF16), w_ref[...].astype(BF16),
                                       (((1,), (1,)), ((), ())), preferred_element_type=F32)
        dgate, dup = jax.vjp(_swiglu, gu_ref[0], gu_ref[1])[1](dact)
        dgu_ref[0] = dgate.astype(dgu_ref.dtype)
        dgu_ref[1] = dup.astype(dgu_ref.dtype)

    pair = pl.BlockSpec((2, tm, tn), lambda i, j: (0, i, j))
    blk = tm * D * dy.dtype.itemsize + tn * D * w_out.dtype.itemsize + 2 * tm * tn * (4 + 2)
    return pl.pallas_call(
        body, name=name, grid=(M // tm, F // tn),
        in_specs=[pl.BlockSpec((tm, D), lambda i, j: (i, 0)), pl.BlockSpec((tn, D), lambda i, j: (j, 0)), pair],
        out_specs=pair, out_shape=jax.ShapeDtypeStruct((2, M, F), BF16),
        compiler_params=_cparams(("parallel", "parallel"), blk),
    )(dy, w_out, gu)


def _row_block(n, width, n_arrays):
    cap = (V7X_VMEM_BYTES // 4) // (2 * 4 * width * n_arrays)
    best = None
    for t in range(16, min(n, cap) + 1, 16):
        if n % t == 0:
            best = t
    return best or n


def placed_map(f, ins, out, *, n_blocks, tb, name):
    def body(*refs):
        refs[-1][...] = f(*[r[...] for r in refs[:-1]]).astype(refs[-1].dtype)

    def spec(fn):
        def index(i):
            x, y, c = _place()
            return fn(i, (c, 2 * x + y)), 0
        return pl.BlockSpec((tb, width), index)

    o_rows, width, o_dtype, o_fn = out
    blk = (sum(a.dtype.itemsize for a, _ in ins) + jnp.dtype(o_dtype).itemsize) * tb * width
    return pl.pallas_call(
        body, name=name, grid=(n_blocks,), in_specs=[spec(fn) for _, fn in ins], out_specs=spec(o_fn),
        out_shape=jax.ShapeDtypeStruct((o_rows, width), o_dtype),
        compiler_params=_cparams(("parallel",), blk),
    )(*[a for a, _ in ins])


def rowmap(f, rows, params, outs, accs=(), *, tb, name):
    rows = [r if isinstance(r, tuple) else (r, r.shape[1], 0) for r in rows]
    S = rows[0][0].shape[0]
    assert S % tb == 0, (name, S, tb)
    n_in, n_out = len(rows) + len(params), len(outs)

    def body(*refs):
        res = f(*[r[...] for r in refs[:n_in]])
        res = res if isinstance(res, (tuple, list)) else (res,)
        o_refs, a_refs = refs[n_in:n_in + n_out], refs[n_in + n_out:]
        for ref, val in zip(o_refs, res[:n_out]):
            ref[...] = val.astype(ref.dtype)
        if a_refs:
            @pl.when(pl.program_id(0) == 0)
            def _():
                for ref in a_refs:
                    ref[...] = jnp.zeros_like(ref)

            for ref, val in zip(a_refs, res[n_out:]):
                ref[...] += val.astype(F32)

    in_specs = [pl.BlockSpec((tb, w), functools.partial(lambda cb, i: (i, cb), cb)) for _, w, cb in rows]
    in_specs += [pl.BlockSpec(p.shape, lambda i: (0, 0)) for p in params]
    out_specs = [pl.BlockSpec((tb, w), lambda i: (i, 0)) for w, _ in outs]
    out_specs += [pl.BlockSpec(tuple(s), lambda i: (0, 0)) for s in accs]
    out_shape = [jax.ShapeDtypeStruct((S, w), dt) for w, dt in outs]
    out_shape += [jax.ShapeDtypeStruct(tuple(s), F32) for s in accs]
    blk = sum(tb * w * a.dtype.itemsize for a, w, _ in rows) + sum(_nbytes(p.shape, p.dtype) for p in params)
    blk += sum(_nbytes((tb, w), dt) for w, dt in outs) + sum(_nbytes(s, F32) for s in accs)
    res = pl.pallas_call(
        body, name=name, grid=(S // tb,), in_specs=in_specs, out_specs=out_specs, out_shape=out_shape,
        compiler_params=_cparams(("arbitrary",) if accs else ("parallel",), blk),
    )(*[r[0] for r in rows], *[pltpu.with_memory_space_constraint(p, pltpu.HBM) for p in params])
    return res


def _rms(x, g):
    return x * lax.rsqrt(jnp.mean(x * x, axis=-1, keepdims=True) + RMS_EPS) * g


def _softplus(z):
    return jnp.maximum(z, 0.0) + jnp.log(1.0 + jnp.exp(-jnp.abs(z)))


def _rwkv_pre(xrk, xlo, w0, w2p, a0, a2p, g2p, k_k, k_a, seg, seg_t):
    k = xrk[:, D_MODEL:2 * D_MODEL]
    w = -_softplus(-(w0 + NN(jnp.tanh(xlo), w2p))) - 0.5
    log_decay = -jnp.exp(w)
    a = jax.nn.sigmoid(a0 + NN(xlo, a2p))
    g = NN(jax.nn.sigmoid(xlo), g2p)
    kk = k * k_k
    norm = jnp.maximum(jnp.sqrt(SEG(kk * kk, seg)), 1e-12)
    kk = kk * SEG(1.0 / norm, seg_t)
    k_mod = k * (1.0 + (a - 1.0) * k_a)
    return log_decay, k_mod, -kk, kk * a, g


def _rwkv_post(wkv, r, k_mod, v, g, r_k, ln_w, ln_b, seg, seg_t):
    inv_n = 1.0 / HEAD_DIM
    mean = SEG(wkv, seg) * inv_n
    cen = wkv - SEG(mean, seg_t)
    var = SEG(cen * cen, seg) * inv_n
    y = cen * SEG(lax.rsqrt(var + GN_EPS), seg_t) * ln_w + ln_b
    bonus = SEG(SEG(r * k_mod * r_k, seg), seg_t) * v
    return (y + bonus) * g


def _qk_norm(q, k, q_gain, k_gain, seg, seg_t, tile_t):
    def norm(x, gain):
        mean_sq = SEG(x * x, seg) * (1.0 / HEAD_DIM)
        return x * SEG(lax.rsqrt(mean_sq + RMS_EPS), seg_t) * SEG(gain, tile_t)

    return norm(q, q_gain) * (HEAD_DIM ** -0.5), norm(k, k_gain)


def _gate_merge(pgate, pa, pb, b_gate):
    sg = jax.nn.sigmoid(pgate + b_gate)
    return sg[:, :D_MODEL] * pa + sg[:, D_MODEL:] * pb


def _group_combine(o, lse):
    ls = [lse[:, GROUP_W * i:GROUP_W * (i + 1)] for i in range(3)]
    m = jnp.maximum(jnp.maximum(ls[0], ls[1]), ls[2])
    es = [jnp.exp(l - m) for l in ls]
    den = es[0] + es[1] + es[2]
    return jnp.concatenate([o[:, GROUP_W * i:GROUP_W * (i + 1)] * (es[i] / den) for i in range(3)], axis=1)


def _each(f, *xs):
    return tuple(f(*args) for args in zip(*xs))


def _attn_block(q, kc, kp, vc, vp, first):
    qi = lax.broadcasted_iota(jnp.int32, (ATTN_BLK, ATTN_BLK), 0)
    kj = lax.broadcasted_iota(jnp.int32, (ATTN_BLK, ATTN_BLK), 1)
    own = kj <= qi
    s_c = _each(lambda a, b: jnp.where(own, NT(a, b), NEG_INF), q, kc)
    s_p = _each(lambda a, b, f: jnp.where((kj >= qi) & (f < 0.5), NT(a, b), NEG_INF), q, kp, first)
    row_max = lambda s: jnp.max(s, axis=-1, keepdims=True)
    row_sum = lambda s: jnp.sum(s, axis=-1, keepdims=True)
    m = _each(lambda c_, p_: jnp.maximum(row_max(c_), row_max(p_)), s_c, s_p)
    e_c, e_p = _each(lambda s, m_: jnp.exp(s - m_), s_c, m), _each(lambda s, m_: jnp.exp(s - m_), s_p, m)
    den = _each(lambda c_, p_: row_sum(c_) + row_sum(p_), e_c, e_p)
    inv = _each(lambda d_: 1.0 / d_, den)
    o = _each(lambda ec, ep, i_, vc_, vp_: (NN(ec, vc_) + NN(ep, vp_)) * i_, e_c, e_p, inv, vc, vp)
    lse = _each(lambda m_, d_: jnp.broadcast_to(m_ + jnp.log(d_), (ATTN_BLK, HEAD_DIM)), m, den)
    return o, lse


def _attn_pair(q, k, k_before, v, v_before, first):
    n = len(q[0])
    o, lse = _attn_block(q[0] + q[1], k[0] + k[1], k_before + k[0], v[0] + v[1], v_before + v[0],
                         (first[0],) * n + (first[1],) * n)
    return (o[:n], o[n:]), (lse[:n], lse[n:])


TRI_SEED = 8


def _tri_inverse(n):
    c = n[0].shape[0]
    row = lax.broadcasted_iota(jnp.int32, (c, c), 0)
    col = lax.broadcasted_iota(jnp.int32, (c, c), 1)
    same_block = lambda size: (row >> (size.bit_length() - 1)) == (col >> (size.bit_length() - 1))
    seed = same_block(TRI_SEED)
    p = _each(lambda m: jnp.where(seed, m, 0.0), n)
    t, span = _each(lambda m: (row == col).astype(F32) + m, p), 2
    while span < TRI_SEED:
        p = _each(NN, p, p)
        t = _each(lambda t_, p_: t_ + NN(t_, p_), t, p)
        span *= 2
    size = TRI_SEED
    while size < c:
        joins = same_block(2 * size) & jnp.logical_not(same_block(size))
        t = _each(lambda t_, m: t_ + NN(NN(t_, jnp.where(joins, m, 0.0)), t_), t, n)
        size *= 2
    return t


@jax.custom_vjp
def _tri_solve(n, rhs, t):
    return _each(NN, t, rhs)


def _tri_solve_fwd(n, rhs, t):
    x = _each(NN, t, rhs)
    return x, (t, x)


def _tri_solve_bwd(res, dx):
    t, x = res
    drhs = _each(TN, t, dx)
    return _each(NT, drhs, x), drhs, _each(jnp.zeros_like, t)


_tri_solve.defvjp(_tri_solve_fwd, _tri_solve_bwd)


def _lower_ones(c):
    row = lax.broadcasted_iota(jnp.int32, (c, c), 0)
    col = lax.broadcasted_iota(jnp.int32, (c, c), 1)
    return (row >= col).astype(BF16)


def _ones_dot(ones, x, contract):
    hi, lo = _split_bf16(x)
    dims = (((contract,), (0,)), ((), ()))
    return (lax.dot_general(ones, hi, dims, preferred_element_type=F32)
            + lax.dot_general(ones, lo, dims, preferred_element_type=F32))


@jax.custom_vjp
def _cumsum_rows(x):
    return _ones_dot(_lower_ones(x.shape[0]), x, 1)


_cumsum_rows.defvjp(lambda x: (_ones_dot(_lower_ones(x.shape[0]), x, 1), None),
                    lambda _, g: (_ones_dot(_lower_ones(g.shape[0]), g, 0),))


def _wkv_chunk(s0, r, lw, k, v, a, b, t_inv=None):
    c = r[0].shape[0]
    row = lax.broadcasted_iota(jnp.int32, (c, c), 0)
    col = lax.broadcasted_iota(jnp.int32, (c, c), 1)
    strict, incl = row > col, row >= col
    cat = lambda p, q: jnp.concatenate([p, q], axis=0)
    cum = _each(_cumsum_rows, lw)
    e_neg = _each(lambda c_: jnp.exp(-c_), cum)
    ar = _each(lambda a_, r_, c_, l_: cat(a_ * jnp.exp(c_ - l_), r_ * jnp.exp(c_)), a, r, cum, lw)
    b_t, k_t = _each(jnp.multiply, b, e_neg), _each(jnp.multiply, k, e_neg)
    p_b, p_k, p_s = _each(NT, ar, b_t), _each(NT, ar, k_t), _each(NT, ar, s0)
    n_ab = _each(lambda p: jnp.where(strict, p[:c], 0.0), p_b)
    m_rb = _each(lambda p: jnp.where(incl, p[c:], 0.0), p_b)
    n_ak = _each(lambda p: jnp.where(strict, p[:c], 0.0), p_k)
    m_rk = _each(lambda p: jnp.where(incl, p[c:], 0.0), p_k)
    if t_inv is None:
        t_inv = _tri_inverse(n_ab)
    u = _tri_solve(n_ab, _each(lambda p, n_, v_: p[:c] + NN(n_, v_), p_s, n_ak, v), t_inv)
    y = _each(lambda p, mb, u_, mk, v_: p[c:] + NN(mb, u_) + NN(mk, v_), p_s, m_rb, u, m_rk, v)
    g_end = _each(lambda l_: jnp.exp(jnp.sum(l_, axis=0, keepdims=True)), lw)
    s1 = _each(lambda s_, g_, u_, v_, b_, k_: s_ * g_ + TN(cat(u_, v_), cat(b_, k_) * g_),
               s0, g_end, u, v, b_t, k_t)
    return y, s1, t_inv


def _adamw(w, g, m, v):
    m = ADAM_B1 * m + (1.0 - ADAM_B1) * g
    v = ADAM_B2 * v + (1.0 - ADAM_B2) * jnp.square(g)
    m_hat = m / (1.0 - ADAM_B1 ** ADAM_STEP)
    v_hat = v / (1.0 - ADAM_B2 ** ADAM_STEP)
    delta = -ADAM_LR * (m_hat / (jnp.sqrt(v_hat) + ADAM_EPS) + ADAM_WD * w)
    return delta, m, v


def token_shift_fwd(p, mu, *, tb, name):
    S, W = p.shape
    hb = tb // 8

    def body(p_ref, halo_ref, mu_ref, o_ref):
        i = pl.program_id(0)
        x = p_ref[...]
        before = halo_ref[7:8, :] * (i > 0).astype(F32)
        row = lax.broadcasted_iota(jnp.int32, (tb, W), 0)
        prev = jnp.where(row == 0, before, pltpu.roll(x, 1, 0))
        o_ref[...] = x + (prev - x) * mu_ref[...]

    blk = (2 * tb + 8) * W * 4
    return pl.pallas_call(
        body, name=name, grid=(S // tb,),
        in_specs=[pl.BlockSpec((tb, W), lambda i: (i, 0)),
                  pl.BlockSpec((8, W), lambda i: (jnp.maximum(i * hb - 1, 0), 0)),
                  pl.BlockSpec((1, W), lambda i: (0, 0))],
        out_specs=pl.BlockSpec((tb, W), lambda i: (i, 0)),
        out_shape=jax.ShapeDtypeStruct((S, W), F32),
        compiler_params=_cparams(("parallel",), blk),
    )(p, p, mu)


def token_shift_bwd(dxs, p, mu, *, tb, name):
    S, W = p.shape
    hb, nb = tb // 8, S // tb

    def body(d_ref, dnext_ref, p_ref, halo_ref, mu_ref, dp_ref, dmu_ref):
        i = pl.program_id(0)
        d, x, mu_v = d_ref[...], p_ref[...], mu_ref[...]
        row = lax.broadcasted_iota(jnp.int32, (tb, W), 0)
        before = halo_ref[7:8, :] * (i > 0).astype(F32)
        prev = jnp.where(row == 0, before, pltpu.roll(x, 1, 0))
        t = d * mu_v
        after = dnext_ref[0:1, :] * mu_v * (i < nb - 1).astype(F32)
        nxt = jnp.where(row == tb - 1, after, pltpu.roll(t, tb - 1, 0))
        dp_ref[...] = (d - t + nxt).astype(dp_ref.dtype)

        @pl.when(i == 0)
        def _():
            dmu_ref[...] = jnp.zeros_like(dmu_ref)

        dmu_ref[...] += jnp.sum(d * (prev - x), axis=0, keepdims=True)

    blk = (3 * tb + 16) * W * 4
    return pl.pallas_call(
        body, name=name, grid=(nb,),
        in_specs=[pl.BlockSpec((tb, W), lambda i: (i, 0)),
                  pl.BlockSpec((8, W), lambda i: (jnp.minimum((i + 1) * hb, S // 8 - 1), 0)),
                  pl.BlockSpec((tb, W), lambda i: (i, 0)),
                  pl.BlockSpec((8, W), lambda i: (jnp.maximum(i * hb - 1, 0), 0)),
                  pl.BlockSpec((1, W), lambda i: (0, 0))],
        out_specs=[pl.BlockSpec((tb, W), lambda i: (i, 0)), pl.BlockSpec((1, W), lambda i: (0, 0))],
        out_shape=[jax.ShapeDtypeStruct((S, W), BF16), jax.ShapeDtypeStruct((1, W), F32)],
        compiler_params=_cparams(("arbitrary",), blk),
    )(dxs, dxs, p, p, mu)


def _head_cols(h):
    return pl.ds(h * HEAD_DIM, HEAD_DIM)


def wkv_fwd(xs_rk, lw, k, a, b):
    S = lw.shape[0]
    C, nc, G, N = WKV_CHUNK, S // WKV_CHUNK, WKV_HEADS_PER_STEP, HEAD_DIM

    def body(r_ref, lw_ref, k_ref, v_ref, a_ref, b_ref, y_ref, st_ref, ti_ref, state):
        @pl.when(pl.program_id(1) == 0)
        def _():
            state[...] = jnp.zeros_like(state)

        heads = lambda ref: tuple(ref[:, _head_cols(h)] for h in range(G))
        s0 = tuple(state[h] for h in range(G))
        y, s1, t_inv = _wkv_chunk(s0, heads(r_ref), heads(lw_ref), heads(k_ref), heads(v_ref), heads(a_ref),
                                  heads(b_ref))
        for h in range(G):
            st_ref[h] = s0[h]
            ti_ref[h] = t_inv[h]
            y_ref[:, _head_cols(h)] = y[h]
            state[h] = s1[h]

    W = G * N
    seq = lambda j: pl.BlockSpec((C, W), functools.partial(lambda j, g, c: (c, j + g), j))
    per = D_MODEL // W
    per_chunk = pl.BlockSpec((None, G, N, N), lambda g, c: (c, g, 0, 0))
    return pl.pallas_call(
        body, name="wkv_fwd", grid=(RWKV_HEADS // G, nc),
        in_specs=[seq(0), seq(0), seq(0), seq(2 * per), seq(0), seq(0)],
        out_specs=[seq(0), per_chunk, per_chunk],
        out_shape=[jax.ShapeDtypeStruct((S, D_MODEL), F32)] + [jax.ShapeDtypeStruct((nc, RWKV_HEADS, N, N), F32)] * 2,
        scratch_shapes=[pltpu.VMEM((G, N, N), F32)],
        compiler_params=_cparams(("parallel", "arbitrary"), 8 * C * W * 4 + 3 * G * N * N * 4),
    )(xs_rk, lw, k, xs_rk, a, b)


def wkv_bwd(xs_rk, lw, k, a, b, states, t_invs, dy):
    S = lw.shape[0]
    C, nc, G, N = WKV_CHUNK, S // WKV_CHUNK, WKV_HEADS_PER_STEP, HEAD_DIM

    def body(r_ref, lw_ref, k_ref, v_ref, a_ref, b_ref, st_ref, ti_ref, dy_ref,
             dr_ref, dlw_ref, dk_ref, dv_ref, da_ref, db_ref, dstate):
        @pl.when(pl.program_id(1) == 0)
        def _():
            dstate[...] = jnp.zeros_like(dstate)

        heads = lambda ref: tuple(ref[:, _head_cols(h)] for h in range(G))
        t_inv = tuple(ti_ref[h] for h in range(G))
        chunk = lambda *args: _wkv_chunk(*args, t_inv)[:2]
        _, pull = jax.vjp(chunk, tuple(st_ref[h] for h in range(G)), heads(r_ref), heads(lw_ref),
                          heads(k_ref), heads(v_ref), heads(a_ref), heads(b_ref))
        ds0, *grads = pull((heads(dy_ref), tuple(dstate[h] for h in range(G))))
        for h in range(G):
            dstate[h] = ds0[h]
            for ref, grad in zip((dr_ref, dlw_ref, dk_ref, dv_ref, da_ref, db_ref), grads):
                ref[:, _head_cols(h)] = grad[h]

    W = G * N
    seq = lambda j: pl.BlockSpec((C, W), functools.partial(lambda j, g, c: (nc - 1 - c, j + g), j))
    per = D_MODEL // W
    st = pl.BlockSpec((None, G, N, N), lambda g, c: (nc - 1 - c, g, 0, 0))
    return pl.pallas_call(
        body, name="wkv_bwd", grid=(RWKV_HEADS // G, nc),
        in_specs=[seq(0), seq(0), seq(0), seq(2 * per), seq(0), seq(0), st, st, seq(0)],
        out_specs=[seq(0)] * 6, out_shape=[jax.ShapeDtypeStruct((S, D_MODEL), F32)] * 6,
        scratch_shapes=[pltpu.VMEM((G, N, N), F32)],
        compiler_params=_cparams(("parallel", "arbitrary"), 14 * C * W * 4 + 3 * G * N * N * 4),
    )(xs_rk, lw, k, xs_rk, a, b, states, t_invs, dy)


def _first_flag(i, seq_len):
    per_group = seq_len // ATTN_BLK
    g = i // per_group
    per_seq = [seq_len // d // ATTN_BLK for _, d in ATTN_PAIRS]
    n = jnp.where(g == 0, per_seq[0], jnp.where(g == 1, per_seq[1], per_seq[2]))
    return (lax.rem(i, n) == 0).astype(F32)


def _block_rows(half):
    return pl.ds(half * ATTN_BLK, ATTN_BLK)


def _block_heads(ref, half):
    return tuple(ref[_block_rows(half), _head_cols(h)] for h in range(ATTN_HPG))


def _pair_heads(ref):
    return _block_heads(ref, 0), _block_heads(ref, 1)


def attn_fwd(q, k, v, seq_len):
    R, N = q.shape
    n_pairs = R // (2 * ATTN_BLK)

    def body(q_ref, k_ref, kb_ref, v_ref, vb_ref, o_ref, lse_ref):
        pair = pl.program_id(0)
        first = (_first_flag(2 * pair, seq_len), _first_flag(2 * pair + 1, seq_len))
        o, lse = _attn_pair(_pair_heads(q_ref), _pair_heads(k_ref), _block_heads(kb_ref, 0), _pair_heads(v_ref),
                            _block_heads(vb_ref, 0), first)
        for half in range(2):
            for h in range(ATTN_HPG):
                o_ref[_block_rows(half), _head_cols(h)] = o[half][h]
                lse_ref[_block_rows(half), _head_cols(h)] = lse[half][h]

    cur = pl.BlockSpec((2 * ATTN_BLK, N), lambda i: (i, 0))
    prv = pl.BlockSpec((ATTN_BLK, N), lambda i: (jnp.maximum(2 * i - 1, 0), 0))
    return pl.pallas_call(
        body, name="attn_fwd", grid=(n_pairs,), in_specs=[cur, cur, prv, cur, prv],
        out_specs=[cur, cur], out_shape=[jax.ShapeDtypeStruct((R, N), F32)] * 2,
        compiler_params=_cparams(("parallel",), 12 * ATTN_BLK * N * 4),
    )(q, k, k, v, v)


def attn_bwd(q, k, v, do, dlse, seq_len):
    R, N = q.shape
    n_pairs = R // (2 * ATTN_BLK)

    def body(q_ref, k_ref, kb_ref, v_ref, vb_ref, do_ref, dl_ref, dq_ref, dk_ref, dv_ref, carry_k, carry_v):
        step = pl.program_id(0)
        pair = n_pairs - 1 - step
        first = (_first_flag(2 * pair, seq_len), _first_flag(2 * pair + 1, seq_len))

        @pl.when(step == 0)
        def _():
            carry_k[...] = jnp.zeros_like(carry_k)
            carry_v[...] = jnp.zeros_like(carry_v)

        _, pull = jax.vjp(functools.partial(_attn_pair, first=first), _pair_heads(q_ref), _pair_heads(k_ref),
                          _block_heads(kb_ref, 0), _pair_heads(v_ref), _block_heads(vb_ref, 0))
        dq, dk, dk_before, dv, dv_before = pull((_pair_heads(do_ref), _pair_heads(dl_ref)))
        old_k, old_v = _block_heads(carry_k, 0), _block_heads(carry_v, 0)
        for h in range(ATTN_HPG):
            cols = _head_cols(h)
            for half in range(2):
                dq_ref[_block_rows(half), cols] = dq[half][h]
            dk_ref[_block_rows(0), cols] = dk[0][h]
            dv_ref[_block_rows(0), cols] = dv[0][h]
            dk_ref[_block_rows(1), cols] = dk[1][h] + old_k[h]
            dv_ref[_block_rows(1), cols] = dv[1][h] + old_v[h]
            carry_k[:, cols] = dk_before[h]
            carry_v[:, cols] = dv_before[h]

    cur = pl.BlockSpec((2 * ATTN_BLK, N), lambda i: (n_pairs - 1 - i, 0))
    prv = pl.BlockSpec((ATTN_BLK, N), lambda i: (jnp.maximum(2 * (n_pairs - 1 - i) - 1, 0), 0))
    return pl.pallas_call(
        body, name="attn_bwd", grid=(n_pairs,), in_specs=[cur, cur, prv, cur, prv, cur, cur],
        out_specs=[cur, cur, cur], out_shape=[jax.ShapeDtypeStruct((R, N), F32)] * 3,
        scratch_shapes=[pltpu.VMEM((ATTN_BLK, N), F32)] * 2,
        compiler_params=_cparams(("arbitrary",), 22 * ATTN_BLK * N * 4),
    )(q, k, k, v, v, do, dlse)


def to_subsequences(t):
    S = t.shape[0]
    parts = []
    for gi, (_, d) in enumerate(ATTN_PAIRS):
        tg = t[:, GROUP_W * gi:GROUP_W * (gi + 1)].reshape(S // d, d, GROUP_W)
        parts.append(tg.transpose(1, 0, 2).reshape(S, GROUP_W))
    return jnp.concatenate(parts, axis=0)


def from_subsequences(u, S):
    parts = []
    for gi, (_, d) in enumerate(ATTN_PAIRS):
        ug = u[S * gi:S * (gi + 1)].reshape(d, S // d, GROUP_W)
        parts.append(ug.transpose(1, 0, 2).reshape(S, GROUP_W))
    return jnp.concatenate(parts, axis=1)


def _ffn_fwd(x, norm, w_in, w_out, tag, token):
    h = rowmap(lambda x_b, g, tok: _rms(x_b, g) + tok[0:1, 0:1], [x], [norm, token], [(D_MODEL, BF16)], tb=512,
               name=tag + "_norm")[0]
    gu, act = ffn_in_act(h, w_in, tag + "_in")
    y = matmul(act, w_out, "nn", tag + "_out", add=x, scale=0.5)
    return y, (x, h, gu, act)


def _ffn_bwd(dy, saved, norm, w_in, w_out, tag):
    x, h, gu, act = saved
    dw_out = matmul(act, dy, "tn", tag + "_dwout", scale=0.5)
    dgu = ffn_dact_dgu(dy, w_out, gu, 0.5, tag + "_dgu")
    dh = matmul_cs(dgu, w_in, "nt", tag + "_dh")
    dw_in = matmul_cs(h, dgu, "tn", tag + "_dwin")

    def norm_bwd(x_b, dh_b, dy_b, g):
        dx, dg = jax.vjp(_rms, x_b, g)[1](dh_b)
        return dy_b + dx, dg

    dx, dnorm = rowmap(norm_bwd, [x, dh, dy], [norm], [(D_MODEL, F32)], [(1, D_MODEL)], tb=256,
                       name=tag + "_dnorm")
    return dx, dnorm, dw_in, dw_out


def layer_step(x, tgt, W, P, start_token, more_weights, on_mixer_grads):
    S = x.shape[0]
    x1, ffn1_saved = _ffn_fwd(x, P["ffn1_norm"], W["ffn1_w_in"], W["ffn1_w_out"], "ffn1", start_token)
    W = {**W, **more_weights("mixer", x1)}
    head_of = lambda n: jnp.arange(n)[:, None] // HEAD_DIM == jnp.arange(n // HEAD_DIM)[None, :]
    seg, seg_a = head_of(D_MODEL).astype(BF16), head_of(ATTN_WIDTH).astype(BF16)
    seg_t, seg_a_t = seg.T, seg_a.T
    tile_t = (jnp.arange(HEAD_DIM)[:, None] == jnp.arange(ATTN_WIDTH)[None, :] % HEAD_DIM).astype(BF16)
    qk_params = [P["attn_q_norm"], P["attn_k_norm"], seg_a, seg_a_t, tile_t]
    w_rkv, w_lora = W["w_in"][:, :RKV], W["w_in"][:, RKV:RKV + LORA]
    w_qkv = W["w_in"][:, RKV + LORA:RKV + LORA + 3 * ATTN_WIDTH]
    w_gate = W["w_in"][:, RKV + LORA + 3 * ATTN_WIDTH:]
    mu_rk, mu_lo = P["rwkv_mu"][:, :RKV], P["rwkv_mu"][:, RKV:]
    zeros = lambda n: jnp.zeros((n, D_MODEL), F32)
    w2p = jnp.concatenate([W["rwkv_w2"], zeros(LORA - LORA_W)], axis=0)
    a2p = jnp.concatenate([zeros(LORA_W), W["rwkv_a2"], zeros(LORA_G)], axis=0)
    g2p = jnp.concatenate([zeros(LORA_W + LORA_A), W["rwkv_g2"]], axis=0)
    pre_params = [P["rwkv_w0"], w2p, P["rwkv_a0"], a2p, g2p, P["rwkv_k_k"], P["rwkv_k_a"], seg, seg_t]
    post_params = [P["rwkv_r_k"], P["rwkv_ln_w"], P["rwkv_ln_b"], seg, seg_t]
    col = lambda arr, j: (arr, D_MODEL, j)

    h = rowmap(_rms, [x1], [P["mix_norm"]], [(D_MODEL, BF16)], tb=512, name="mix_norm")[0]
    p_rk = matmul(h, w_rkv, "nn", "proj_rkv")
    p_lo = matmul(h, w_lora, "nn", "proj_lora")
    p_qkv = matmul(h, w_qkv, "nn", "proj_qkv")
    p_gate = matmul(h, w_gate, "nn", "proj_gate")
    xs_rk = token_shift_fwd(p_rk, mu_rk, tb=256, name="shift_rk")
    xs_lo = token_shift_fwd(p_lo, mu_lo, tb=256, name="shift_lora")
    lw, k_mod, a_neg, b_kk, g = rowmap(
        _rwkv_pre, [xs_rk, xs_lo], pre_params, [(D_MODEL, F32)] * 5, tb=256, name="rwkv_pre")
    wkv, states, t_invs = wkv_fwd(xs_rk, lw, k_mod, a_neg, b_kk)
    post_rows = [wkv, col(xs_rk, 0), k_mod, col(xs_rk, 2), g]
    y_a = rowmap(_rwkv_post, post_rows, post_params, [(D_MODEL, BF16)], tb=256, name="rwkv_post")[0]

    qk_rows = [(p_qkv, ATTN_WIDTH, 0), (p_qkv, ATTN_WIDTH, 1)]
    qn, kn = rowmap(_qk_norm, qk_rows, qk_params, [(ATTN_WIDTH, F32)] * 2, tb=256, name="qk_norm")
    q_s, k_s, v_s = to_subsequences(qn), to_subsequences(kn), to_subsequences(p_qkv[:, 2 * ATTN_WIDTH:])
    o_s, lse_s = attn_fwd(q_s, k_s, v_s, S)
    o, lse = from_subsequences(o_s, S), from_subsequences(lse_s, S)
    y_b = rowmap(_group_combine, [o, lse], [], [(ATTN_WIDTH, BF16)], tb=512, name="attn_combine")[0]

    W = {**W, **more_weights("out", y_b)}
    pa = matmul(y_a, W["w_proj_rwkv"], "nn", "proj_a")
    pb = matmul(y_b, W["w_proj_attn"], "nn", "proj_b")
    merged = rowmap(_gate_merge, [p_gate, pa, pb], [P["b_gate"]], [(D_MODEL, BF16)], tb=256, name="merge")[0]
    x2 = matmul(merged, W["w_out"], "nn", "mix_out", add=x1)
    x3, ffn2_saved = _ffn_fwd(x2, P["ffn2_norm"], W["ffn2_w_in"], W["ffn2_w_out"], "ffn2",
                              jnp.zeros_like(start_token))

    def loss_head(y_b_, t_b):
        err = y_b_ - t_b
        return err * (1.0 / D_MODEL), (0.5 / D_MODEL) * jnp.sum(err * err, axis=0, keepdims=True)

    dx3, loss_cols = rowmap(loss_head, [x3, tgt], [], [(D_MODEL, F32)], [(1, D_MODEL)], tb=512, name="loss")

    gW, gP = {}, {}
    dx2, gP["ffn2_norm"], gW["ffn2_w_in"], gW["ffn2_w_out"] = _ffn_bwd(
        dx3, ffn2_saved, P["ffn2_norm"], W["ffn2_w_in"], W["ffn2_w_out"], "ffn2")

    dmerged = matmul(dx2, W["w_out"], "nt", "d_merged")
    gW["w_out"] = matmul(merged, dx2, "tn", "dw_out")

    def merge_bwd(pg, pa_b, pb_b, dm, bg):
        return jax.vjp(_gate_merge, pg, pa_b, pb_b, bg)[1](dm)

    dp_gate, dpa, dpb, gP["b_gate"] = rowmap(
        merge_bwd, [p_gate, pa, pb, dmerged], [P["b_gate"]],
        [(2 * D_MODEL, BF16), (D_MODEL, BF16), (D_MODEL, BF16)], [(1, 2 * D_MODEL)], tb=256, name="merge_bwd")
    dy_a = matmul(dpa, W["w_proj_rwkv"], "nt", "d_ya")
    gW["w_proj_rwkv"] = matmul(y_a, dpa, "tn", "dw_proj_a")
    dy_b = matmul(dpb, W["w_proj_attn"], "nt", "d_yb")
    gW["w_proj_attn"] = matmul(y_b, dpb, "tn", "dw_proj_b")

    def combine_bwd(o_b, l_b, d_b):
        return jax.vjp(_group_combine, o_b, l_b)[1](d_b)

    do, dlse = rowmap(combine_bwd, [o, lse, dy_b], [], [(ATTN_WIDTH, F32)] * 2, tb=256, name="attn_combine_bwd")
    dq_s, dk_s, dv_s = attn_bwd(q_s, k_s, v_s, to_subsequences(do), to_subsequences(dlse), S)

    def qk_norm_bwd(q_b, k_b, dqn_b, dkn_b, dv_b, qg, kg, sg, sgt, tl):
        f = lambda *a: _qk_norm(*a, sg, sgt, tl)
        dq, dk, dqg, dkg = jax.vjp(f, q_b, k_b, qg, kg)[1]((dqn_b, dkn_b))
        return jnp.concatenate([dq, dk, dv_b], axis=1), dqg, dkg

    dp_qkv, gP["attn_q_norm"], gP["attn_k_norm"] = rowmap(
        qk_norm_bwd, qk_rows + [from_subsequences(t, S) for t in (dq_s, dk_s, dv_s)], qk_params,
        [(3 * ATTN_WIDTH, BF16)], [(1, HEAD_DIM)] * 2, tb=256, name="qk_norm_bwd")

    def post_bwd(wkv_b, r_b, k_b, v_b, g_b, d_b, r_k, ln_w, ln_b, sg, sgt):
        f = lambda *a: _rwkv_post(*a, sg, sgt)
        return jax.vjp(f, wkv_b, r_b, k_b, v_b, g_b, r_k, ln_w, ln_b)[1](d_b)

    dwkv, dr_p, dk_p, dv_p, dg, gP["rwkv_r_k"], gP["rwkv_ln_w"], gP["rwkv_ln_b"] = rowmap(
        post_bwd, post_rows + [dy_a], post_params, [(D_MODEL, F32)] * 5, [(1, D_MODEL)] * 3, tb=128,
        name="rwkv_post_bwd")
    dr_w, dlw, dk_w, dv_w, da_neg, db_kk = wkv_bwd(xs_rk, lw, k_mod, a_neg, b_kk, states, t_invs, dwkv)

    def pre_bwd(xrk_b, xlo_b, dlw_b, dkw_b, dkp_b, da_b, db_b, dg_b, drp_b, drw_b, dvp_b, dvw_b,
                w0, w2, a0, a2, g2, k_k, k_a, sg, sgt):
        f = lambda *a: _rwkv_pre(*a, sg, sgt)
        pull = jax.vjp(f, xrk_b, xlo_b, w0, w2, a0, a2, g2, k_k, k_a)[1]
        dxrk, dxlo, *dpar = pull((dlw_b, dkw_b + dkp_b, da_b, db_b, dg_b))
        direct = jnp.concatenate([drp_b + drw_b, jnp.zeros_like(drp_b), dvp_b + dvw_b], axis=1)
        return (dxrk + direct, dxlo, *dpar)

    pre_rows = [xs_rk, xs_lo, dlw, dk_w, dk_p, da_neg, db_kk, dg, dr_p, dr_w, dv_p, dv_w]
    dxs_rk, dxs_lo, gP["rwkv_w0"], dw2p, gP["rwkv_a0"], da2p, dg2p, gP["rwkv_k_k"], gP["rwkv_k_a"] = rowmap(
        pre_bwd, pre_rows, pre_params, [(RKV, F32), (LORA, F32)],
        [(1, D_MODEL), (LORA, D_MODEL), (1, D_MODEL), (LORA, D_MODEL), (LORA, D_MODEL), (1, D_MODEL), (1, D_MODEL)],
        tb=128, name="rwkv_pre_bwd")
    gW["rwkv_w2"] = dw2p[:LORA_W]
    gW["rwkv_a2"] = da2p[LORA_W:LORA_W + LORA_A]
    gW["rwkv_g2"] = dg2p[LORA_W + LORA_A:]
    dp_rk, dmu_rk = token_shift_bwd(dxs_rk, p_rk, mu_rk, tb=256, name="shift_rk_bwd")
    dp_lo, dmu_lo = token_shift_bwd(dxs_lo, p_lo, mu_lo, tb=256, name="shift_lora_bwd")
    gP["rwkv_mu"] = jnp.concatenate([dmu_rk, dmu_lo], axis=1)

    dh = matmul(dp_rk, w_rkv, "nt", "dh_rkv")
    dh = matmul(dp_lo, w_lora, "nt", "dh_lora", add=dh)
    dh = matmul(dp_qkv, w_qkv, "nt", "dh_qkv", add=dh)
    dh = matmul(dp_gate, w_gate, "nt", "dh_gate", add=dh)
    gW["w_in"] = jnp.concatenate([
        matmul(h, dp_rk, "tn", "dw_rkv"), matmul(h, dp_lo, "tn", "dw_lora"),
        matmul(h, dp_qkv, "tn", "dw_qkv"), matmul(h, dp_gate, "tn", "dw_gate")], axis=1)

    token = on_mixer_grads(gW)

    def norm_bwd(x_b, dh_b, dy_b, gn, tok):
        dx, dgn = jax.vjp(_rms, x_b, gn)[1](dh_b)
        return dy_b + dx + tok[0:1, 0:1], dgn

    dx1, gP["mix_norm"] = rowmap(norm_bwd, [x1, dh, dx2], [P["mix_norm"], token], [(D_MODEL, F32)],
                                 [(1, D_MODEL)], tb=256, name="mix_norm_bwd")
    dx, gP["ffn1_norm"], gW["ffn1_w_in"], gW["ffn1_w_out"] = _ffn_bwd(
        dx1, ffn1_saved, P["ffn1_norm"], W["ffn1_w_in"], W["ffn1_w_out"], "ffn1")
    return loss_cols, dx, gW, gP


N_SHARDS = 4
BIG = (("ffn1_w_in", (D_MODEL, 2 * D_FF), 1), ("ffn1_w_out", (D_FF, D_MODEL), 0),
       ("w_in", (D_MODEL, 7712), 1), ("rwkv_w2", (LORA_W, D_MODEL), 1), ("rwkv_a2", (LORA_A, D_MODEL), 1),
       ("rwkv_g2", (LORA_G, D_MODEL), 1), ("w_proj_rwkv", (D_MODEL, D_MODEL), 0),
       ("w_proj_attn", (ATTN_WIDTH, D_MODEL), 1), ("w_out", (D_MODEL, D_MODEL), 0),
       ("ffn2_w_in", (D_MODEL, 2 * D_FF), 1), ("ffn2_w_out", (D_FF, D_MODEL), 0))
SMALL = (("ffn1_norm", 1024), ("mix_norm", 1024), ("b_gate", 2048), ("rwkv_mu", 3360), ("rwkv_w0", 1024),
         ("rwkv_a0", 1024), ("rwkv_k_k", 1024), ("rwkv_k_a", 1024), ("rwkv_r_k", 1024), ("rwkv_ln_w", 1024),
         ("rwkv_ln_b", 1024), ("attn_q_norm", 64), ("attn_k_norm", 64), ("ffn2_norm", 1024))
WEIGHT_ORDER = ("ffn1_norm", "ffn1_w_in", "ffn1_w_out", "mix_norm", "w_in", "b_gate", "rwkv_mu", "rwkv_w0",
                "rwkv_w2", "rwkv_a0", "rwkv_a2", "rwkv_g2", "rwkv_k_k", "rwkv_k_a", "rwkv_r_k", "rwkv_ln_w",
                "rwkv_ln_b", "attn_q_norm", "attn_k_norm", "w_proj_rwkv", "w_proj_attn", "w_out", "ffn2_norm",
                "ffn2_w_in", "ffn2_w_out")


LORA_PARTS = ("rwkv_w2", "rwkv_a2", "rwkv_g2")
BLOCK_MAJOR = ("ffn1_w_in", "ffn2_w_in")
FIRST_FFN = ("ffn1_w_in", "ffn1_w_out")
MIXER_IN = ("w_in", "lora")
SMALL_USED = D_MODEL + sum(n for _, n in SMALL)
SMALL_W = -(-SMALL_USED // 128) * 128


def _travel():
    out = {}
    for name, shape, axis in BIG:
        if name == LORA_PARTS[0]:
            out["lora"] = ((LORA, D_MODEL), 1)
        elif name not in LORA_PARTS:
            out[name] = (shape, axis)
    return out


def local_blocks(vals):
    out = {n: vals[n] for n in _travel() if n != "lora"}
    out["lora"] = jnp.concatenate([vals[n] for n in LORA_PARTS], axis=0)
    return out


def split_lora(t):
    return {"rwkv_w2": t[:LORA_W], "rwkv_a2": t[LORA_W:LORA_W + LORA_A], "rwkv_g2": t[LORA_W + LORA_A:]}


def blocks_to_full(name, blocks):
    shape, axis = _travel()[name]
    if name in BLOCK_MAJOR:
        return blocks
    if axis == 0:
        return blocks.reshape(shape)
    return blocks.transpose(1, 0, 2).reshape(shape)


def full_to_blocks(name, full):
    shape, axis = _travel()[name]
    if name in BLOCK_MAJOR:
        return full
    if axis == 0:
        return full.reshape(N_SHARDS, shape[0] // N_SHARDS, shape[1])
    return full.reshape(shape[0], N_SHARDS, shape[1] // N_SHARDS).transpose(1, 0, 2)


def pack_small(vals, head):
    parts = [head] + [vals[name].reshape(1, n) for name, n in SMALL]
    parts.append(jnp.zeros((1, SMALL_W - SMALL_USED), F32))
    return jnp.concatenate(parts, axis=1)


def unpack_small(vec, shapes):
    out, off = {}, D_MODEL
    for name, n in SMALL:
        out[name] = vec[:, off:off + n].reshape(shapes[name])
        off += n
    return out


def _place():
    return lax.axis_index("x"), lax.axis_index("y"), lax.axis_index("c")


def _other_chips(x, y):
    return [(1 - x, y), (x, 1 - y), (1 - x, 1 - y)]


def _remote(src, dst, send_sem, recv_sem, device):
    return pltpu.make_async_remote_copy(src_ref=src, dst_ref=dst, send_sem=send_sem, recv_sem=recv_sem,
                                        device_id=device, device_id_type=MESH)


def _half(ref, who):
    hr = ref.shape[-2] // 2
    rows = pl.ds(pl.multiple_of(who * hr, 8), hr)
    return ref.at[rows] if len(ref.shape) == 2 else ref.at[:, rows]


HBM_REF = pl.BlockSpec(memory_space=pl.ANY)
COMM_PARAMS = dict(compiler_params=pltpu.CompilerParams(has_side_effects=True))


def gather_weights(blocks):
    n = len(blocks)

    def body(*refs):
        ins, outs = refs[:n], refs[n:2 * n]
        ici_send, ici_recv, d2d_send, d2d_recv = refs[2 * n:]
        x, y, c = _place()
        me, sibling, chips = 2 * x + y, (x, y, 1 - c), _other_chips(x, y)
        first = [_remote(_half(ins[t], c), _half(outs[t].at[me], c), ici_send.at[k, t], ici_recv.at[k, t],
                         (px, py, c)) for k, (px, py) in enumerate(chips) for t in range(n)]
        for cp in first:
            cp.start()
        passed = []
        for k, (px, py) in enumerate(chips):
            for t in range(n):
                landed = _half(outs[t].at[2 * px + py], c)
                _remote(landed, landed, ici_send.at[k, t], ici_recv.at[k, t], (px, py, c)).wait_recv()
                cp = _remote(landed, landed, d2d_send.at[k, t], d2d_recv.at[k, t], sibling)
                cp.start()
                passed.append(cp)
        for k, (px, py) in enumerate(chips):
            for t in range(n):
                other = _half(outs[t].at[2 * px + py], 1 - c)
                _remote(other, other, d2d_send.at[k, t], d2d_recv.at[k, t], sibling).wait_recv()
        for cp in first + passed:
            cp.wait_send()

    res = pl.pallas_call(
        body, name="gather_weights", in_specs=[HBM_REF] * n, out_specs=[HBM_REF] * n,
        out_shape=[jax.ShapeDtypeStruct((N_SHARDS,) + b.shape, b.dtype) for b in blocks],
        scratch_shapes=[pltpu.SemaphoreType.DMA((3, n))] * 4, **COMM_PARAMS)(*blocks)
    me = 2 * lax.axis_index("x") + lax.axis_index("y")
    return [lax.dynamic_update_slice(g, b[None], (me, 0, 0)) for g, b in zip(res, blocks)]


def _gather_copies(ins, outs, send_sem, recv_sem):
    x, y, c = _place()
    return [_remote(_half(ins[t], c), _half(outs[t].at[2 * x + y], c), send_sem(k, t), recv_sem(k, t), (px, py, c))
            for k, (px, py) in enumerate(_other_chips(x, y)) for t in range(len(ins))]


def gather_start(blocks, name):
    n = len(blocks)
    n_cp = 3 * n

    def body(*refs):
        ins, outs = refs[:n], refs[n:2 * n]
        sems, token = refs[2 * n:2 * n + 2 * n_cp], refs[-1]
        for cp in _gather_copies(ins, outs, lambda k, t: sems[k * n + t], lambda k, t: sems[n_cp + k * n + t]):
            cp.start()
        token[...] = jnp.zeros_like(token)

    hbm = lambda a: pltpu.with_memory_space_constraint(a, pltpu.HBM)
    landing = [lax.empty((N_SHARDS,) + b.shape, b.dtype) for b in blocks]
    res = pl.pallas_call(
        body, name=name,
        out_shape=(*[pltpu.SemaphoreType.DMA(())] * (2 * n_cp),
                   *[pltpu.HBM(a.shape, a.dtype) for a in list(blocks) + landing], jax.ShapeDtypeStruct((8, 128), F32)),
        in_specs=[SPLIT_HBM] * (2 * n),
        out_specs=(*[SPLIT_SEM] * (2 * n_cp), *[SPLIT_HBM] * (2 * n), pl.BlockSpec(memory_space=pltpu.VMEM)),
        input_output_aliases={t: 2 * n_cp + t for t in range(2 * n)}, **SPLIT_PARAMS,
    )(*[hbm(a) for a in list(blocks) + landing])
    return (n, res[:-1]), res[-1]


def gather_wait(handles, after, name):
    n, held = handles
    n_cp = 3 * n
    sems, thru = held[:2 * n_cp], held[2 * n_cp:]

    def body(*refs):
        ins, outs = refs[:n], refs[n:2 * n]
        sem_refs = refs[2 * n:2 * n + 2 * n_cp]
        for cp in _gather_copies(ins, outs, lambda k, t: sem_refs[k * n + t], lambda k, t: sem_refs[n_cp + k * n + t]):
            cp.wait_send()
            cp.wait_recv()

    res = pl.pallas_call(
        body, name=name, out_shape=tuple(pltpu.HBM(a.shape, a.dtype) for a in thru),
        in_specs=[SPLIT_HBM] * (2 * n) + [SPLIT_SEM] * (2 * n_cp) + [pl.BlockSpec(memory_space=pl.ANY)],
        out_specs=tuple([SPLIT_HBM] * (2 * n)), input_output_aliases={t: t for t in range(2 * n)}, **SPLIT_PARAMS,
    )(*thru, *sems, after)
    return list(res[n:])


def pass_halves(gathered, blocks, name):
    n = len(gathered)

    def body(*refs):
        outs = refs[n:2 * n]
        send_sems, recv_sems = refs[2 * n:]
        x, y, c = _place()
        slots = [2 * px + py for px, py in _other_chips(x, y)]
        give = [_remote(_half(outs[t].at[s], c), _half(outs[t].at[s], c), send_sems.at[k, t], recv_sems.at[k, t],
                        (x, y, 1 - c)) for k, s in enumerate(slots) for t in range(n)]
        for cp in give:
            cp.start()
        for k, s in enumerate(slots):
            for t in range(n):
                other = _half(outs[t].at[s], 1 - c)
                _remote(other, other, send_sems.at[k, t], recv_sems.at[k, t], (x, y, 1 - c)).wait_recv()
        for cp in give:
            cp.wait_send()

    res = pl.pallas_call(
        body, name=name, in_specs=[HBM_REF] * n, out_specs=[HBM_REF] * n,
        out_shape=[jax.ShapeDtypeStruct(g.shape, g.dtype) for g in gathered],
        input_output_aliases={t: t for t in range(n)},
        scratch_shapes=[pltpu.SemaphoreType.DMA((3, n))] * 2, **COMM_PARAMS)(*gathered)
    me = 2 * lax.axis_index("x") + lax.axis_index("y")
    return [lax.dynamic_update_slice(g, b[None], (me, 0, 0)) for g, b in zip(res, blocks)]


def swap_halves(grads):
    n = len(grads)

    def body(*refs):
        ins, got = refs[:n], refs[n:2 * n]
        send_sems, recv_sems = refs[2 * n:]
        x, y, c = _place()
        give = [_remote(_half(ins[t], 1 - c), got[t], send_sems.at[t], recv_sems.at[t], (x, y, 1 - c))
                for t in range(n)]
        for cp in give:
            cp.start()
        for cp in give:
            cp.wait_recv()
        for cp in give:
            cp.wait_send()

    return pl.pallas_call(
        body, name="swap_halves", in_specs=[HBM_REF] * n, out_specs=[HBM_REF] * n,
        out_shape=[jax.ShapeDtypeStruct((g.shape[0], g.shape[1] // 2, g.shape[2]), g.dtype) for g in grads],
        scratch_shapes=[pltpu.SemaphoreType.DMA((n,))] * 2, **COMM_PARAMS)(*grads)


def join_halves(blocks):
    n = len(blocks)

    def body(*refs):
        outs = refs[n:2 * n]
        send_sems, recv_sems = refs[2 * n:]
        x, y, c = _place()
        give = [_remote(_half(outs[t], c), _half(outs[t], c), send_sems.at[t], recv_sems.at[t], (x, y, 1 - c))
                for t in range(n)]
        for cp in give:
            cp.start()
        for t in range(n):
            arriving = _half(outs[t], 1 - c)
            _remote(arriving, arriving, send_sems.at[t], recv_sems.at[t], (x, y, 1 - c)).wait_recv()
        for cp in give:
            cp.wait_send()

    return pl.pallas_call(
        body, name="join_halves", in_specs=[HBM_REF] * n, out_specs=[HBM_REF] * n,
        out_shape=[jax.ShapeDtypeStruct(b.shape, b.dtype) for b in blocks],
        input_output_aliases={t: t for t in range(n)},
        scratch_shapes=[pltpu.SemaphoreType.DMA((n,))] * 2, **COMM_PARAMS)(*blocks)


SPLIT_HBM = pl.BlockSpec(memory_space=pltpu.HBM)
SPLIT_SEM = pl.BlockSpec(memory_space=pltpu.SEMAPHORE)
SPLIT_PARAMS = dict(compiler_params=pltpu.CompilerParams(has_side_effects=pltpu.SideEffectType.DATAFLOW_SIDE_EFFECTING))


def _scatter_copies(parts, landed, send_sem, recv_sem):
    x, y, c = _place()
    return [_remote(parts[t].at[2 * px + py], landed[t].at[k], send_sem(k, t), recv_sem(k, t), (px, py, c))
            for k, (px, py) in enumerate(_other_chips(x, y)) for t in range(len(parts))]


def scatter_start(partials, name):
    n = len(partials)
    n_cp = 3 * n

    def body(*refs):
        parts, landed = refs[:n], refs[n:2 * n]
        sems, token = refs[2 * n:2 * n + 2 * n_cp], refs[-1]
        for cp in _scatter_copies(parts, landed, lambda k, t: sems[k * n + t], lambda k, t: sems[n_cp + k * n + t]):
            cp.start()
        token[...] = jnp.zeros_like(token)

    hbm = lambda a: pltpu.with_memory_space_constraint(a, pltpu.HBM)
    landing = [lax.empty((3,) + p.shape[1:], p.dtype) for p in partials]
    res = pl.pallas_call(
        body, name=name,
        out_shape=(*[pltpu.SemaphoreType.DMA(())] * (2 * n_cp),
                   *[pltpu.HBM(a.shape, a.dtype) for a in partials + landing], jax.ShapeDtypeStruct((8, 128), F32)),
        in_specs=[SPLIT_HBM] * (2 * n),
        out_specs=(*[SPLIT_SEM] * (2 * n_cp), *[SPLIT_HBM] * (2 * n), pl.BlockSpec(memory_space=pltpu.VMEM)),
        input_output_aliases={t: 2 * n_cp + t for t in range(2 * n)}, **SPLIT_PARAMS,
    )(*[hbm(a) for a in partials + landing])
    return (n, res[:-1]), res[-1]


def scatter_wait(handles, after, name):
    n, held = handles
    n_cp = 3 * n
    sems, thru = held[:2 * n_cp], held[2 * n_cp:]

    def body(*refs):
        parts, landed = refs[:n], refs[n:2 * n]
        sem_refs = refs[2 * n:2 * n + 2 * n_cp]
        for cp in _scatter_copies(parts, landed, lambda k, t: sem_refs[k * n + t],
                                  lambda k, t: sem_refs[n_cp + k * n + t]):
            cp.wait_send()
            cp.wait_recv()

    res = pl.pallas_call(
        body, name=name, out_shape=tuple(pltpu.HBM(a.shape, a.dtype) for a in thru),
        in_specs=[SPLIT_HBM] * (2 * n) + [SPLIT_SEM] * (2 * n_cp) + [pl.BlockSpec(memory_space=pl.ANY)],
        out_specs=tuple([SPLIT_HBM] * (2 * n)), input_output_aliases={t: t for t in range(2 * n)}, **SPLIT_PARAMS,
    )(*thru, *sems, after)
    return list(res[n:])


def chip_sums(grads):
    names = list(grads)
    got = swap_halves([grads[n] for n in names])
    partials = []
    for name, theirs in zip(names, got):
        n_slot, hr, width = theirs.shape
        tb = _row_block(hr, width, 6)
        per_half = hr // tb
        mine = lambda i, s, per_half=per_half: (i // per_half) * 2 * per_half + s[0] * per_half + i % per_half
        p = placed_map(
            jnp.add,
            [(grads[name].reshape(2 * n_slot * hr, width), mine), (theirs.reshape(n_slot * hr, width), lambda i, s: i)],
            (n_slot * hr, width, BF16, lambda i, s: i), n_blocks=n_slot * per_half, tb=tb, name="chip_sum_" + name)
        partials.append(p.reshape(theirs.shape))
    return got, partials


def owner_sums(grads, got, landed):
    names = list(grads)
    blocks = []
    for name, theirs, arrived in zip(names, got, landed):
        n_slot, hr, width = theirs.shape
        tb = _row_block(hr, width, 6)
        per_half = hr // tb
        views = [(grads[name].reshape(2 * n_slot * hr, width),
                  lambda i, s, per_half=per_half: s[1] * 2 * per_half + s[0] * per_half + i),
                 (theirs.reshape(n_slot * hr, width), lambda i, s, per_half=per_half: s[1] * per_half + i)]
        views += [(arrived.reshape(3 * hr, width), functools.partial(lambda k, per_half, i, s: k * per_half + i,
                                                                     k, per_half)) for k in range(3)]
        f = lambda a, b, l0, l1, l2: (((a + b) + l0.astype(F32)) + l1.astype(F32)) + l2.astype(F32)
        blocks.append(placed_map(
            f, views,(2 * hr, width, F32, lambda i, s, per_half=per_half: s[0] * per_half + i),
            n_blocks=per_half, tb=tb, name="owner_sum_" + name))
    return dict(zip(names, join_halves(blocks)))


def adamw_block(name, w, g, m, v):
    rows, width = w.shape
    return rowmap(_adamw, [w, g, m, v], [], [(width, F32)] * 3, tb=_row_block(rows, width, 7),
                  name="adamw_" + name)


def reduce_small(vec, w, m, v):
    n_dev = 8

    def body(vec_ref, w_ref, m_ref, v_ref, loss_ref, g_ref, d_ref, m2_ref, v2_ref, slots, send_sems, recv_sems):
        x, y, c = _place()
        me = 4 * x + 2 * y + c
        slots[me] = vec_ref[...]
        flips = [(fx, fy, fc) for fx in (0, 1) for fy in (0, 1) for fc in (0, 1)][1:]
        peers = [(1 - x if fx else x, 1 - y if fy else y, 1 - c if fc else c) for fx, fy, fc in flips]
        sends = [pltpu.make_async_remote_copy(
            src_ref=vec_ref, dst_ref=slots.at[me], send_sem=send_sems.at[j], recv_sem=recv_sems.at[j],
            device_id=peer, device_id_type=MESH) for j, peer in enumerate(peers)]
        for cp in sends:
            cp.start()
        for j, (px, py, pc) in enumerate(peers):
            pltpu.make_async_remote_copy(
                src_ref=vec_ref, dst_ref=slots.at[4 * px + 2 * py + pc], send_sem=send_sems.at[j],
                recv_sem=recv_sems.at[j], device_id=(px, py, pc), device_id_type=MESH).wait_recv()
        for cp in sends:
            cp.wait_send()
        g = slots[0]
        for d in range(1, n_dev):
            g = g + slots[d]
        loss_ref[...] = jnp.sum(g[:, :D_MODEL], axis=1, keepdims=True)
        delta, m2, v2 = _adamw(w_ref[...], g, m_ref[...], v_ref[...])
        g_ref[...], d_ref[...], m2_ref[...], v2_ref[...] = g, delta, m2, v2

    vm = pl.BlockSpec(memory_space=pltpu.VMEM)
    vec_t = jax.ShapeDtypeStruct(vec.shape, F32)
    return pl.pallas_call(
        body, name="reduce_small", in_specs=[vm] * 4, out_specs=[vm] * 5,
        out_shape=[jax.ShapeDtypeStruct((1, 1), F32)] + [vec_t] * 4,
        scratch_shapes=[pltpu.VMEM((n_dev,) + vec.shape, F32), pltpu.SemaphoreType.DMA((n_dev - 1,)),
                        pltpu.SemaphoreType.DMA((n_dev - 1,))],
        compiler_params=pltpu.CompilerParams(has_side_effects=True),
    )(vec, w, m, v)


def kernel(x, ffn1_norm, ffn1_w_in, ffn1_w_out, mix_norm, w_in, b_gate, rwkv_mu, rwkv_w0, rwkv_w2, rwkv_a0, rwkv_a2, rwkv_g2, rwkv_k_k, rwkv_k_a, rwkv_r_k, rwkv_ln_w, rwkv_ln_b, attn_q_norm, attn_k_norm, w_proj_rwkv, w_proj_attn, w_out, ffn2_norm, ffn2_w_in, ffn2_w_out, loss_target, m_ffn1_norm, m_ffn1_w_in, m_ffn1_w_out, m_mix_norm, m_w_in, m_b_gate, m_rwkv_mu, m_rwkv_w0, m_rwkv_w2, m_rwkv_a0, m_rwkv_a2, m_rwkv_g2, m_rwkv_k_k, m_rwkv_k_a, m_rwkv_r_k, m_rwkv_ln_w, m_rwkv_ln_b, m_attn_q_norm, m_attn_k_norm, m_w_proj_rwkv, m_w_proj_attn, m_w_out, m_ffn2_norm, m_ffn2_w_in, m_ffn2_w_out, v_ffn1_norm, v_ffn1_w_in, v_ffn1_w_out, v_mix_norm, v_w_in, v_b_gate, v_rwkv_mu, v_rwkv_w0, v_rwkv_w2, v_rwkv_a0, v_rwkv_a2, v_rwkv_g2, v_rwkv_k_k, v_rwkv_k_a, v_rwkv_r_k, v_rwkv_ln_w, v_rwkv_ln_b, v_attn_q_norm, v_attn_k_norm, v_w_proj_rwkv, v_w_proj_attn, v_w_out, v_ffn2_norm, v_ffn2_w_in, v_ffn2_w_out):
    given = dict(locals())
    weights = {n: given[n] for n in WEIGHT_ORDER}
    mom_m = {n: given["m_" + n] for n in WEIGHT_ORDER}
    mom_v = {n: given["v_" + n] for n in WEIGHT_ORDER}
    big = [name for name, _, _ in BIG]
    shapes = {n: weights[n].shape for n in WEIGHT_ORDER}
    blocks_of = lambda d: local_blocks({n: d[n][0] for n in big})
    w_blk, m_blk, v_blk = blocks_of(weights), blocks_of(mom_m), blocks_of(mom_v)
    names = list(w_blk)

    early = [n for n in names if n not in FIRST_FFN]
    bf16_block = lambda n: w_blk[n].astype(BF16)
    W = {n: blocks_to_full(n, g) for n, g in zip(FIRST_FFN, gather_weights([bf16_block(n) for n in FIRST_FFN]))}
    stages = {"mixer": [n for n in early if n in MIXER_IN], "out": [n for n in early if n not in MIXER_IN]}
    stage_blocks = {s: [bf16_block(n) for n in stages[s]] for s in stages}
    started = {s: gather_start(stage_blocks[s], "gather_start_" + s) for s in ("mixer", "out")}
    start_token = started["mixer"][1] + started["out"][1]

    def more_weights(stage, after):
        landed = gather_wait(started[stage][0], after, "gather_wait_" + stage)
        got = pass_halves(landed, stage_blocks[stage], "pass_halves_" + stage)
        more = {n: blocks_to_full(n, g) for n, g in zip(stages[stage], got)}
        if "lora" in more:
            more.update(split_lora(more.pop("lora")))
        return more

    P = {n: weights[n].reshape(1, -1) for n, _ in SMALL}

    sent = {}

    def send_early(gw):
        lora = jnp.concatenate([gw[n] for n in LORA_PARTS], axis=0)
        sent["grads"] = {n: full_to_blocks(n, lora if n == "lora" else gw[n]) for n in early}
        sent["got"], partials = chip_sums(sent["grads"])
        sent["handles"], token = scatter_start(partials, "scatter_start")
        return token

    loss_cols, dx, gW, gP = layer_step(x[0], loss_target[0], W, P, start_token, more_weights, send_early)
    landed = scatter_wait(sent["handles"], gP["ffn1_norm"], "scatter_wait")

    late = {n: full_to_blocks(n, gW[n]) for n in FIRST_FFN}
    late_got, late_partials = chip_sums(late)
    late_handles, late_token = scatter_start(late_partials, "scatter_start_ffn1")
    landed[-1] = landed[-1] + late_token[0, 0].astype(landed[-1].dtype)
    out_g, out_d, out_m, out_v = {}, {}, {}, {}

    def apply(g_blk):
        for n in g_blk:
            res = (g_blk[n], *adamw_block(n, w_blk[n], g_blk[n], m_blk[n], v_blk[n]))
            for dst, t in zip((out_g, out_d, out_m, out_v), res):
                for part, val in (split_lora(t) if n == "lora" else {n: t}).items():
                    dst[part] = val.reshape(shapes[part])

    apply(owner_sums(sent["grads"], sent["got"], landed))
    apply(owner_sums(late, late_got, scatter_wait(late_handles, list(out_d.values())[-1], "scatter_wait_ffn1")))

    zero_head = jnp.zeros((1, D_MODEL), F32)
    vec = pack_small(gP, loss_cols)
    loss, g_s, d_s, m_s, v_s = reduce_small(
        vec, pack_small({n: weights[n] for n, _ in SMALL}, zero_head),
        pack_small({n: mom_m[n] for n, _ in SMALL}, zero_head),
        pack_small({n: mom_v[n] for n, _ in SMALL}, zero_head))
    for dst, src in ((out_g, g_s), (out_d, d_s), (out_m, m_s), (out_v, v_s)):
        dst.update(unpack_small(src, shapes))

    return (loss[0, 0], dx[None], *[out_g[n] for n in WEIGHT_ORDER], *[out_d[n] for n in WEIGHT_ORDER],
            *[out_m[n] for n in WEIGHT_ORDER], *[out_v[n] for n in WEIGHT_ORDER])
```

```python
import functools

import jax
import jax.numpy as jnp
from jax import lax
from jax.experimental import pallas as pl
from jax.experimental.pallas import tpu as pltpu

F32 = jnp.float32
BF16 = jnp.bfloat16
MESH = pl.DeviceIdType.MESH

D_MODEL = 1024
HEAD_DIM = 64
RWKV_HEADS = 16
LORA_W, LORA_A, LORA_G = 64, 64, 160
LORA = LORA_W + LORA_A + LORA_G
RKV = 3 * D_MODEL
ATTN_PAIRS = ((128, 1), (512, 4), (2048, 16))
ATTN_BLK = 128
ATTN_HPG = 4
ATTN_WIDTH = 768
GROUP_W = ATTN_HPG * HEAD_DIM
D_FF = 2816
GN_EPS = 64e-5
RMS_EPS = 1e-6
NEG_INF = -1e30
WKV_CHUNK = 64
WKV_HEADS_PER_STEP = 16

ADAM_LR, ADAM_B1, ADAM_B2, ADAM_EPS, ADAM_WD, ADAM_STEP = 0.001, 0.9, 0.999, 1e-08, 0.01, 10

V7X_VMEM_BYTES = 64 << 20
VMEM_TEMP_ALLOWANCE = 20 << 20


def _cparams(sem, block_bytes):
    limit = min(2 * block_bytes + VMEM_TEMP_ALLOWANCE, V7X_VMEM_BYTES - (6 << 20))
    return pltpu.CompilerParams(dimension_semantics=sem, vmem_limit_bytes=int(limit))


def _nbytes(shape, dtype):
    n = 1
    for s in shape:
        n *= s
    return n * jnp.dtype(dtype).itemsize


def _split_bf16(a):
    hi = a.astype(BF16)
    return hi, (a - hi.astype(F32)).astype(BF16)


def _make_dots():
    def raw(a, b, ca, cb):
        return lax.dot_general(a.astype(BF16), b.astype(BF16), (((ca,), (cb,)), ((), ())),
                               preferred_element_type=F32)

    @jax.custom_vjp
    def nn(a, b):
        return raw(a, b, 1, 0)

    @jax.custom_vjp
    def nt(a, b):
        return raw(a, b, 1, 1)

    @jax.custom_vjp
    def tn(a, b):
        return raw(a, b, 0, 0)

    nn.defvjp(lambda a, b: (raw(a, b, 1, 0), (a, b)),
              lambda res, g: (raw(g, res[1], 1, 1), raw(res[0], g, 0, 0)))
    nt.defvjp(lambda a, b: (raw(a, b, 1, 1), (a, b)),
              lambda res, g: (raw(g, res[1], 1, 0), raw(g, res[0], 0, 0)))
    tn.defvjp(lambda a, b: (raw(a, b, 0, 0), (a, b)),
              lambda res, g: (raw(res[1], g, 1, 1), raw(res[0], g, 1, 0)))
    return nn, nt, tn


def _exact_rhs_dot(x, ones, cx, co):
    hi, lo = _split_bf16(x)
    dims = (((cx,), (co,)), ((), ()))
    return (lax.dot_general(hi, ones, dims, preferred_element_type=F32)
            + lax.dot_general(lo, ones, dims, preferred_element_type=F32))


@jax.custom_vjp
def SEG(x, ones):
    return _exact_rhs_dot(x, ones, 1, 0)


SEG.defvjp(lambda x, ones: (_exact_rhs_dot(x, ones, 1, 0), ones),
           lambda ones, g: (_exact_rhs_dot(g, ones, 1, 1), jnp.zeros_like(ones)))

NN, NT, TN = _make_dots()


MM_TILE_M, MM_TILE_N, MM_TILE_K = 1408, 1408, 1536


def _pick(n, cap):
    best = None
    for t in range(128, min(n, cap) + 1, 128):
        if n % t == 0:
            best = t
    return best or n


def matmul(a, b, mode, name, *, add=None, scale=1.0, out_dtype=F32):
    if mode == "nn":
        (M, K), (K2, N) = a.shape, b.shape
    elif mode == "nt":
        (M, K), (N, K2) = a.shape, b.shape
    else:
        (K, M), (K2, N) = a.shape, b.shape
    assert K == K2, (name, a.shape, b.shape)
    tm, tn, tk = _pick(M, MM_TILE_M), _pick(N, MM_TILE_N), _pick(K, MM_TILE_K)
    nk = K // tk
    ca, cb = {"nn": (1, 0), "nt": (1, 1), "tn": (0, 0)}[mode]

    def body(*refs):
        if add is None:
            a_ref, b_ref, o_ref, acc_ref = refs
        else:
            a_ref, b_ref, add_ref, o_ref, acc_ref = refs
        k = pl.program_id(2)

        @pl.when(k == 0)
        def _():
            acc_ref[...] = jnp.zeros_like(acc_ref)

        acc_ref[...] += lax.dot_general(a_ref[...].astype(BF16), b_ref[...].astype(BF16),
                                        (((ca,), (cb,)), ((), ())), preferred_element_type=F32)

        @pl.when(k == nk - 1)
        def _():
            r = acc_ref[...] * scale
            if add is not None:
                r = add_ref[...] + r
            o_ref[...] = r.astype(o_ref.dtype)

    a_spec = (pl.BlockSpec((tk, tm), lambda i, j, k: (k, i)) if mode == "tn"
              else pl.BlockSpec((tm, tk), lambda i, j, k: (i, k)))
    b_spec = (pl.BlockSpec((tn, tk), lambda i, j, k: (j, k)) if mode == "nt"
              else pl.BlockSpec((tk, tn), lambda i, j, k: (k, j)))
    in_specs, args = [a_spec, b_spec], [a, b]
    blk = tm * tk * a.dtype.itemsize + tk * tn * b.dtype.itemsize + tm * tn * 8
    if add is not None:
        in_specs.append(pl.BlockSpec((tm, tn), lambda i, j, k: (i, j)))
        args.append(add)
        blk += tm * tn * 4
    return pl.pallas_call(
        body, name=name, grid=(M // tm, N // tn, nk),
        in_specs=in_specs, out_specs=pl.BlockSpec((tm, tn), lambda i, j, k: (i, j)),
        out_shape=jax.ShapeDtypeStruct((M, N), out_dtype),
        scratch_shapes=[pltpu.VMEM((tm, tn), F32)],
        compiler_params=_cparams(("parallel", "parallel", "arbitrary"), blk),
    )(*args)


def matmul_cs(a, w, mode, name, *, scale=1.0, out_dtype=F32):
    n_blk = N_SHARDS
    if mode == "tn":
        (K, R), Cs = a.shape, w.shape[2] // 2
        tm, tk = _pick(R, MM_TILE_M), _pick(K, 1024)
        grid = (R // tm, n_blk, K // tk)
        a_spec = pl.BlockSpec((tk, tm), lambda i, j, k: (k, i))
        w_spec = pl.BlockSpec((None, tk, Cs), lambda i, j, k: (j // 2, k, j % 2))
        o_spec = pl.BlockSpec((None, tm, Cs), lambda i, j, k: (j, i, 0))
        out_shape, acc_shape, dims = (n_blk, R, Cs), (tm, Cs), (0, 0)
        blk = tk * tm * a.dtype.itemsize + tk * Cs * w.dtype.itemsize + tm * Cs * 8
    else:
        M, (_, R, Cs) = a.shape[1], w.shape
        tm, tn = _pick(M, MM_TILE_M), _pick(R, MM_TILE_N)
        grid = (M // tm, R // tn, n_blk)
        a_spec = pl.BlockSpec((None, tm, Cs), lambda i, j, k: (k // 2, i, k % 2))
        w_spec = pl.BlockSpec((None, tn, Cs), lambda i, j, k: (k, j, 0))
        o_spec = pl.BlockSpec((tm, tn), lambda i, j, k: (i, j))
        out_shape, acc_shape, dims = (M, R), (tm, tn), (1, 1)
        blk = tm * Cs * a.dtype.itemsize + tn * Cs * w.dtype.itemsize + tm * tn * 8
    nk = grid[2]

    def body(a_ref, w_ref, o_ref, acc_ref):
        k = pl.program_id(2)

        @pl.when(k == 0)
        def _():
            acc_ref[...] = jnp.zeros_like(acc_ref)

        acc_ref[...] += lax.dot_general(a_ref[...].astype(BF16), w_ref[...].astype(BF16),
                                        (((dims[0],), (dims[1],)), ((), ())), preferred_element_type=F32)

        @pl.when(k == nk - 1)
        def _():
            o_ref[...] = (acc_ref[...] * scale).astype(o_ref.dtype)

    return pl.pallas_call(
        body, name=name, grid=grid, in_specs=[a_spec, w_spec], out_specs=o_spec,
        out_shape=jax.ShapeDtypeStruct(out_shape, out_dtype), scratch_shapes=[pltpu.VMEM(acc_shape, F32)],
        compiler_params=_cparams(("parallel", "parallel", "arbitrary"), blk),
    )(a, w)


FFN_TILE_M = 512


def _swiglu(gate, up):
    return gate * jax.nn.sigmoid(gate) * up


def ffn_in_act(h, w, name):
    (M, R), Cs, half = h.shape, w.shape[2], N_SHARDS // 2
    tm, tk = _pick(M, FFN_TILE_M), _pick(R, 1024)
    nk = R // tk

    def body(h_ref, wg_ref, wu_ref, gu_ref, act_ref, acc_ref):
        k = pl.program_id(2)

        @pl.when(k == 0)
        def _():
            acc_ref[...] = jnp.zeros_like(acc_ref)

        hb = h_ref[...].astype(BF16)
        for part, w_ref in enumerate((wg_ref, wu_ref)):
            acc_ref[part] += jnp.dot(hb, w_ref[...].astype(BF16), preferred_element_type=F32)

        @pl.when(k == nk - 1)
        def _():
            gu_ref[...] = acc_ref[...]
            act_ref[...] = _swiglu(acc_ref[0], acc_ref[1]).astype(act_ref.dtype)

    w_spec = lambda off: pl.BlockSpec((None, tk, Cs), functools.partial(lambda off, i, j, k: (j + off, k, 0), off))
    blk = tm * tk * h.dtype.itemsize + 2 * tk * Cs * w.dtype.itemsize + tm * Cs * (16 + 2)
    return pl.pallas_call(
        body, name=name, grid=(M // tm, half, nk),
        in_specs=[pl.BlockSpec((tm, tk), lambda i, j, k: (i, k)), w_spec(0), w_spec(half)],
        out_specs=[pl.BlockSpec((2, tm, Cs), lambda i, j, k: (0, i, j)), pl.BlockSpec((tm, Cs), lambda i, j, k: (i, j))],
        out_shape=[jax.ShapeDtypeStruct((2, M, half * Cs), F32), jax.ShapeDtypeStruct((M, half * Cs), BF16)],
        scratch_shapes=[pltpu.VMEM((2, tm, Cs), F32)],
        compiler_params=_cparams(("parallel", "parallel", "arbitrary"), blk),
    )(h, w, w)


def ffn_dact_dgu(dy, w_out, gu, scale, name):
    (M, D), F = dy.shape, w_out.shape[0]
    tm, tn = _pick(M, FFN_TILE_M), F // 2

    def body(dy_ref, w_ref, gu_ref, dgu_ref):
        dact = scale * lax.dot_general(dy_ref[...].astype(BF16), w_ref[...].astype(BF16),
                                       (((1,), (1,)), ((), ())), preferred_element_type=F32)
        dgate, dup = jax.vjp(_swiglu, gu_ref[0], gu_ref[1])[1](dact)
        dgu_ref[0] = dgate.astype(dgu_ref.dtype)
        dgu_ref[1] = dup.astype(dgu_ref.dtype)

    pair = pl.BlockSpec((2, tm, tn), lambda i, j: (0, i, j))
    blk = tm * D * dy.dtype.itemsize + tn * D * w_out.dtype.itemsize + 2 * tm * tn * (4 + 2)
    return pl.pallas_call(
        body, name=name, grid=(M // tm, F // tn),
        in_specs=[pl.BlockSpec((tm, D), lambda i, j: (i, 0)), pl.BlockSpec((tn, D), lambda i, j: (j, 0)), pair],
        out_specs=pair, out_shape=jax.ShapeDtypeStruct((2, M, F), BF16),
        compiler_params=_cparams(("parallel", "parallel"), blk),
    )(dy, w_out, gu)


def _row_block(n, width, n_arrays):
    cap = (V7X_VMEM_BYTES // 4) // (2 * 4 * width * n_arrays)
    best = None
    for t in range(16, min(n, cap) + 1, 16):
        if n % t == 0:
            best = t
    return best or n


def placed_map(f, ins, out, *, n_blocks, tb, name):
    def body(*refs):
        refs[-1][...] = f(*[r[...] for r in refs[:-1]]).astype(refs[-1].dtype)

    def spec(fn):
        def index(i):
            x, y, c = _place()
            return fn(i, (c, 2 * x + y)), 0
        return pl.BlockSpec((tb, width), index)

    o_rows, width, o_dtype, o_fn = out
    blk = (sum(a.dtype.itemsize for a, _ in ins) + jnp.dtype(o_dtype).itemsize) * tb * width
    return pl.pallas_call(
        body, name=name, grid=(n_blocks,), in_specs=[spec(fn) for _, fn in ins], out_specs=spec(o_fn),
        out_shape=jax.ShapeDtypeStruct((o_rows, width), o_dtype),
        compiler_params=_cparams(("parallel",), blk),
    )(*[a for a, _ in ins])


def rowmap(f, rows, params, outs, accs=(), *, tb, name):
    rows = [r if isinstance(r, tuple) else (r, r.shape[1], 0) for r in rows]
    S = rows[0][0].shape[0]
    assert S % tb == 0, (name, S, tb)
    n_in, n_out = len(rows) + len(params), len(outs)

    def body(*refs):
        res = f(*[r[...] for r in refs[:n_in]])
        res = res if isinstance(res, (tuple, list)) else (res,)
        o_refs, a_refs = refs[n_in:n_in + n_out], refs[n_in + n_out:]
        for ref, val in zip(o_refs, res[:n_out]):
            ref[...] = val.astype(ref.dtype)
        if a_refs:
            @pl.when(pl.program_id(0) == 0)
            def _():
                for ref in a_refs:
                    ref[...] = jnp.zeros_like(ref)

            for ref, val in zip(a_refs, res[n_out:]):
                ref[...] += val.astype(F32)

    in_specs = [pl.BlockSpec((tb, w), functools.partial(lambda cb, i: (i, cb), cb)) for _, w, cb in rows]
    in_specs += [pl.BlockSpec(p.shape, lambda i: (0, 0)) for p in params]
    out_specs = [pl.BlockSpec((tb, w), lambda i: (i, 0)) for w, _ in outs]
    out_specs += [pl.BlockSpec(tuple(s), lambda i: (0, 0)) for s in accs]
    out_shape = [jax.ShapeDtypeStruct((S, w), dt) for w, dt in outs]
    out_shape += [jax.ShapeDtypeStruct(tuple(s), F32) for s in accs]
    blk = sum(tb * w * a.dtype.itemsize for a, w, _ in rows) + sum(_nbytes(p.shape, p.dtype) for p in params)
    blk += sum(_nbytes((tb, w), dt) for w, dt in outs) + sum(_nbytes(s, F32) for s in accs)
    res = pl.pallas_call(
        body, name=name, grid=(S // tb,), in_specs=in_specs, out_specs=out_specs, out_shape=out_shape,
        compiler_params=_cparams(("arbitrary",) if accs else ("parallel",), blk),
    )(*[r[0] for r in rows], *[pltpu.with_memory_space_constraint(p, pltpu.HBM) for p in params])
    return res


def _rms(x, g):
    return x * lax.rsqrt(jnp.mean(x * x, axis=-1, keepdims=True) + RMS_EPS) * g


def _softplus(z):
    return jnp.maximum(z, 0.0) + jnp.log(1.0 + jnp.exp(-jnp.abs(z)))


def _rwkv_pre(xrk, xlo, w0, w2p, a0, a2p, g2p, k_k, k_a, seg, seg_t):
    k = xrk[:, D_MODEL:2 * D_MODEL]
    w = -_softplus(-(w0 + NN(jnp.tanh(xlo), w2p))) - 0.5
    log_decay = -jnp.exp(w)
    a = jax.nn.sigmoid(a0 + NN(xlo, a2p))
    g = NN(jax.nn.sigmoid(xlo), g2p)
    kk = k * k_k
    norm = jnp.maximum(jnp.sqrt(SEG(kk * kk, seg)), 1e-12)
    kk = kk * SEG(1.0 / norm, seg_t)
    k_mod = k * (1.0 + (a - 1.0) * k_a)
    return log_decay, k_mod, -kk, kk * a, g


def _rwkv_post(wkv, r, k_mod, v, g, r_k, ln_w, ln_b, seg, seg_t):
    inv_n = 1.0 / HEAD_DIM
    mean = SEG(wkv, seg) * inv_n
    cen = wkv - SEG(mean, seg_t)
    var = SEG(cen * cen, seg) * inv_n
    y = cen * SEG(lax.rsqrt(var + GN_EPS), seg_t) * ln_w + ln_b
    bonus = SEG(SEG(r * k_mod * r_k, seg), seg_t) * v
    return (y + bonus) * g


def _qk_norm(q, k, q_gain, k_gain, seg, seg_t, tile_t):
    def norm(x, gain):
        mean_sq = SEG(x * x, seg) * (1.0 / HEAD_DIM)
        return x * SEG(lax.rsqrt(mean_sq + RMS_EPS), seg_t) * SEG(gain, tile_t)

    return norm(q, q_gain) * (HEAD_DIM ** -0.5), norm(k, k_gain)


def _gate_merge(pgate, pa, pb, b_gate):
    sg = jax.nn.sigmoid(pgate + b_gate)
    return sg[:, :D_MODEL] * pa + sg[:, D_MODEL:] * pb


def _group_combine(o0, o1, o2, l0, l1, l2):
    m = jnp.maximum(jnp.maximum(l0, l1), l2)
    es = [jnp.exp(l - m) for l in (l0, l1, l2)]
    den = es[0] + es[1] + es[2]
    return jnp.concatenate([o * (e / den) for o, e in zip((o0, o1, o2), es)], axis=1)


def _each(f, *xs):
    return tuple(f(*args) for args in zip(*xs))


def _attn_block(q, kc, kp, vc, vp, first):
    qi = lax.broadcasted_iota(jnp.int32, (ATTN_BLK, ATTN_BLK), 0)
    kj = lax.broadcasted_iota(jnp.int32, (ATTN_BLK, ATTN_BLK), 1)
    own = kj <= qi
    s_c = _each(lambda a, b: jnp.where(own, NT(a, b), NEG_INF), q, kc)
    s_p = _each(lambda a, b, f: jnp.where((kj >= qi) & (f < 0.5), NT(a, b), NEG_INF), q, kp, first)
    row_max = lambda s: jnp.max(s, axis=-1, keepdims=True)
    row_sum = lambda s: jnp.sum(s, axis=-1, keepdims=True)
    m = _each(lambda c_, p_: jnp.maximum(row_max(c_), row_max(p_)), s_c, s_p)
    e_c, e_p = _each(lambda s, m_: jnp.exp(s - m_), s_c, m), _each(lambda s, m_: jnp.exp(s - m_), s_p, m)
    den = _each(lambda c_, p_: row_sum(c_) + row_sum(p_), e_c, e_p)
    inv = _each(lambda d_: 1.0 / d_, den)
    o = _each(lambda ec, ep, i_, vc_, vp_: (NN(ec, vc_) + NN(ep, vp_)) * i_, e_c, e_p, inv, vc, vp)
    lse = _each(lambda m_, d_: jnp.broadcast_to(m_ + jnp.log(d_), (ATTN_BLK, HEAD_DIM)), m, den)
    return o, lse


def _attn_pair(q, k, k_before, v, v_before, first):
    n = len(q[0])
    o, lse = _attn_block(q[0] + q[1], k[0] + k[1], k_before + k[0], v[0] + v[1], v_before + v[0],
                         (first[0],) * n + (first[1],) * n)
    return (o[:n], o[n:]), (lse[:n], lse[n:])


TRI_SEED = 8


def _tri_inverse(n):
    c = n[0].shape[0]
    row = lax.broadcasted_iota(jnp.int32, (c, c), 0)
    col = lax.broadcasted_iota(jnp.int32, (c, c), 1)
    same_block = lambda size: (row >> (size.bit_length() - 1)) == (col >> (size.bit_length() - 1))
    seed = same_block(TRI_SEED)
    p = _each(lambda m: jnp.where(seed, m, 0.0), n)
    t, span = _each(lambda m: (row == col).astype(F32) + m, p), 2
    while span < TRI_SEED:
        p = _each(NN, p, p)
        t = _each(lambda t_, p_: t_ + NN(t_, p_), t, p)
        span *= 2
    size = TRI_SEED
    while size < c:
        joins = same_block(2 * size) & jnp.logical_not(same_block(size))
        t = _each(lambda t_, m: t_ + NN(NN(t_, jnp.where(joins, m, 0.0)), t_), t, n)
        size *= 2
    return t


@jax.custom_vjp
def _tri_solve(n, rhs, t):
    return _each(NN, t, rhs)


def _tri_solve_fwd(n, rhs, t):
    x = _each(NN, t, rhs)
    return x, (t, x)


def _tri_solve_bwd(res, dx):
    t, x = res
    drhs = _each(TN, t, dx)
    return _each(NT, drhs, x), drhs, _each(jnp.zeros_like, t)


_tri_solve.defvjp(_tri_solve_fwd, _tri_solve_bwd)


def _lower_ones(c):
    row = lax.broadcasted_iota(jnp.int32, (c, c), 0)
    col = lax.broadcasted_iota(jnp.int32, (c, c), 1)
    return (row >= col).astype(BF16)


def _ones_dot(ones, x, contract):
    hi, lo = _split_bf16(x)
    dims = (((contract,), (0,)), ((), ()))
    return (lax.dot_general(ones, hi, dims, preferred_element_type=F32)
            + lax.dot_general(ones, lo, dims, preferred_element_type=F32))


@jax.custom_vjp
def _cumsum_rows(x):
    return _ones_dot(_lower_ones(x.shape[0]), x, 1)


_cumsum_rows.defvjp(lambda x: (_ones_dot(_lower_ones(x.shape[0]), x, 1), None),
                    lambda _, g: (_ones_dot(_lower_ones(g.shape[0]), g, 0),))


def _wkv_chunk(s0, r, lw, k, v, a, b, t_inv=None):
    c = r[0].shape[0]
    row = lax.broadcasted_iota(jnp.int32, (c, c), 0)
    col = lax.broadcasted_iota(jnp.int32, (c, c), 1)
    strict, incl = row > col, row >= col
    cat = lambda p, q: jnp.concatenate([p, q], axis=0)
    cum = _each(_cumsum_rows, lw)
    e_neg = _each(lambda c_: jnp.exp(-c_), cum)
    ar = _each(lambda a_, r_, c_, l_: cat(a_ * jnp.exp(c_ - l_), r_ * jnp.exp(c_)), a, r, cum, lw)
    b_t, k_t = _each(jnp.multiply, b, e_neg), _each(jnp.multiply, k, e_neg)
    p_b, p_k, p_s = _each(NT, ar, b_t), _each(NT, ar, k_t), _each(NT, ar, s0)
    n_ab = _each(lambda p: jnp.where(strict, p[:c], 0.0), p_b)
    m_rb = _each(lambda p: jnp.where(incl, p[c:], 0.0), p_b)
    n_ak = _each(lambda p: jnp.where(strict, p[:c], 0.0), p_k)
    m_rk = _each(lambda p: jnp.where(incl, p[c:], 0.0), p_k)
    if t_inv is None:
        t_inv = _tri_inverse(n_ab)
    u = _tri_solve(n_ab, _each(lambda p, n_, v_: p[:c] + NN(n_, v_), p_s, n_ak, v), t_inv)
    y = _each(lambda p, mb, u_, mk, v_: p[c:] + NN(mb, u_) + NN(mk, v_), p_s, m_rb, u, m_rk, v)
    g_end = _each(lambda l_: jnp.exp(jnp.sum(l_, axis=0, keepdims=True)), lw)
    s1 = _each(lambda s_, g_, u_, v_, b_, k_: s_ * g_ + TN(cat(u_, v_), cat(b_, k_) * g_),
               s0, g_end, u, v, b_t, k_t)
    return y, s1, t_inv


def _adamw(w, g, m, v):
    m = ADAM_B1 * m + (1.0 - ADAM_B1) * g
    v = ADAM_B2 * v + (1.0 - ADAM_B2) * jnp.square(g)
    m_hat = m / (1.0 - ADAM_B1 ** ADAM_STEP)
    v_hat = v / (1.0 - ADAM_B2 ** ADAM_STEP)
    delta = -ADAM_LR * (m_hat / (jnp.sqrt(v_hat) + ADAM_EPS) + ADAM_WD * w)
    return delta, m, v


def token_shift_fwd(p, mu, *, tb, name):
    S, W = p.shape
    hb = tb // 8

    def body(p_ref, halo_ref, mu_ref, o_ref):
        i = pl.program_id(0)
        x = p_ref[...]
        before = halo_ref[7:8, :] * (i > 0).astype(F32)
        row = lax.broadcasted_iota(jnp.int32, (tb, W), 0)
        prev = jnp.where(row == 0, before, pltpu.roll(x, 1, 0))
        o_ref[...] = x + (prev - x) * mu_ref[...]

    blk = (2 * tb + 8) * W * 4
    return pl.pallas_call(
        body, name=name, grid=(S // tb,),
        in_specs=[pl.BlockSpec((tb, W), lambda i: (i, 0)),
                  pl.BlockSpec((8, W), lambda i: (jnp.maximum(i * hb - 1, 0), 0)),
                  pl.BlockSpec((1, W), lambda i: (0, 0))],
        out_specs=pl.BlockSpec((tb, W), lambda i: (i, 0)),
        out_shape=jax.ShapeDtypeStruct((S, W), F32),
        compiler_params=_cparams(("parallel",), blk),
    )(p, p, mu)


def token_shift_bwd(dxs, p, mu, *, tb, name):
    S, W = p.shape
    hb, nb = tb // 8, S // tb

    def body(d_ref, dnext_ref, p_ref, halo_ref, mu_ref, dp_ref, dmu_ref):
        i = pl.program_id(0)
        d, x, mu_v = d_ref[...], p_ref[...], mu_ref[...]
        row = lax.broadcasted_iota(jnp.int32, (tb, W), 0)
        before = halo_ref[7:8, :] * (i > 0).astype(F32)
        prev = jnp.where(row == 0, before, pltpu.roll(x, 1, 0))
        t = d * mu_v
        after = dnext_ref[0:1, :] * mu_v * (i < nb - 1).astype(F32)
        nxt = jnp.where(row == tb - 1, after, pltpu.roll(t, tb - 1, 0))
        dp_ref[...] = (d - t + nxt).astype(dp_ref.dtype)

        @pl.when(i == 0)
        def _():
            dmu_ref[...] = jnp.zeros_like(dmu_ref)

        dmu_ref[...] += jnp.sum(d * (prev - x), axis=0, keepdims=True)

    blk = (3 * tb + 16) * W * 4
    return pl.pallas_call(
        body, name=name, grid=(nb,),
        in_specs=[pl.BlockSpec((tb, W), lambda i: (i, 0)),
                  pl.BlockSpec((8, W), lambda i: (jnp.minimum((i + 1) * hb, S // 8 - 1), 0)),
                  pl.BlockSpec((tb, W), lambda i: (i, 0)),
                  pl.BlockSpec((8, W), lambda i: (jnp.maximum(i * hb - 1, 0), 0)),
                  pl.BlockSpec((1, W), lambda i: (0, 0))],
        out_specs=[pl.BlockSpec((tb, W), lambda i: (i, 0)), pl.BlockSpec((1, W), lambda i: (0, 0))],
        out_shape=[jax.ShapeDtypeStruct((S, W), BF16), jax.ShapeDtypeStruct((1, W), F32)],
        compiler_params=_cparams(("arbitrary",), blk),
    )(dxs, dxs, p, p, mu)


def _head_cols(h):
    return pl.ds(h * HEAD_DIM, HEAD_DIM)


def wkv_fwd(xs_rk, lw, k, a, b):
    S = lw.shape[0]
    C, nc, G, N = WKV_CHUNK, S // WKV_CHUNK, WKV_HEADS_PER_STEP, HEAD_DIM

    def body(r_ref, lw_ref, k_ref, v_ref, a_ref, b_ref, y_ref, st_ref, ti_ref, state):
        @pl.when(pl.program_id(1) == 0)
        def _():
            state[...] = jnp.zeros_like(state)

        heads = lambda ref: tuple(ref[:, _head_cols(h)] for h in range(G))
        s0 = tuple(state[h] for h in range(G))
        y, s1, t_inv = _wkv_chunk(s0, heads(r_ref), heads(lw_ref), heads(k_ref), heads(v_ref), heads(a_ref),
                                  heads(b_ref))
        for h in range(G):
            st_ref[h] = s0[h]
            ti_ref[h] = t_inv[h]
            y_ref[:, _head_cols(h)] = y[h]
            state[h] = s1[h]

    W = G * N
    seq = lambda j: pl.BlockSpec((C, W), functools.partial(lambda j, g, c: (c, j + g), j))
    per = D_MODEL // W
    per_chunk = pl.BlockSpec((None, G, N, N), lambda g, c: (c, g, 0, 0))
    return pl.pallas_call(
        body, name="wkv_fwd", grid=(RWKV_HEADS // G, nc),
        in_specs=[seq(0), seq(0), seq(0), seq(2 * per), seq(0), seq(0)],
        out_specs=[seq(0), per_chunk, per_chunk],
        out_shape=[jax.ShapeDtypeStruct((S, D_MODEL), F32)] + [jax.ShapeDtypeStruct((nc, RWKV_HEADS, N, N), F32)] * 2,
        scratch_shapes=[pltpu.VMEM((G, N, N), F32)],
        compiler_params=_cparams(("parallel", "arbitrary"), 8 * C * W * 4 + 3 * G * N * N * 4),
    )(xs_rk, lw, k, xs_rk, a, b)


def wkv_bwd(xs_rk, lw, k, a, b, states, t_invs, dy):
    S = lw.shape[0]
    C, nc, G, N = WKV_CHUNK, S // WKV_CHUNK, WKV_HEADS_PER_STEP, HEAD_DIM

    def body(r_ref, lw_ref, k_ref, v_ref, a_ref, b_ref, st_ref, ti_ref, dy_ref,
             dr_ref, dlw_ref, dk_ref, dv_ref, da_ref, db_ref, dstate):
        @pl.when(pl.program_id(1) == 0)
        def _():
            dstate[...] = jnp.zeros_like(dstate)

        heads = lambda ref: tuple(ref[:, _head_cols(h)] for h in range(G))
        t_inv = tuple(ti_ref[h] for h in range(G))
        chunk = lambda *args: _wkv_chunk(*args, t_inv)[:2]
        _, pull = jax.vjp(chunk, tuple(st_ref[h] for h in range(G)), heads(r_ref), heads(lw_ref),
                          heads(k_ref), heads(v_ref), heads(a_ref), heads(b_ref))
        ds0, *grads = pull((heads(dy_ref), tuple(dstate[h] for h in range(G))))
        for h in range(G):
            dstate[h] = ds0[h]
            for ref, grad in zip((dr_ref, dlw_ref, dk_ref, dv_ref, da_ref, db_ref), grads):
                ref[:, _head_cols(h)] = grad[h]

    W = G * N
    seq = lambda j: pl.BlockSpec((C, W), functools.partial(lambda j, g, c: (nc - 1 - c, j + g), j))
    per = D_MODEL // W
    st = pl.BlockSpec((None, G, N, N), lambda g, c: (nc - 1 - c, g, 0, 0))
    return pl.pallas_call(
        body, name="wkv_bwd", grid=(RWKV_HEADS // G, nc),
        in_specs=[seq(0), seq(0), seq(0), seq(2 * per), seq(0), seq(0), st, st, seq(0)],
        out_specs=[seq(0)] * 6, out_shape=[jax.ShapeDtypeStruct((S, D_MODEL), F32)] * 6,
        scratch_shapes=[pltpu.VMEM((G, N, N), F32)],
        compiler_params=_cparams(("parallel", "arbitrary"), 14 * C * W * 4 + 3 * G * N * N * 4),
    )(xs_rk, lw, k, xs_rk, a, b, states, t_invs, dy)


def _first_flag(i, per_seq):
    return (lax.rem(i, per_seq) == 0).astype(F32)


def _view(a):
    return a if isinstance(a, tuple) else (a, 0)


def _block_rows(half):
    return pl.ds(half * ATTN_BLK, ATTN_BLK)


def _block_heads(ref, half):
    return tuple(ref[_block_rows(half), _head_cols(h)] for h in range(ATTN_HPG))


def _pair_heads(ref):
    return _block_heads(ref, 0), _block_heads(ref, 1)


def attn_fwd(q, k, v, per_seq, name):
    (q, q_col), (k, k_col), (v, v_col) = _view(q), _view(k), _view(v)
    R, N = q.shape[0], GROUP_W
    n_pairs = R // (2 * ATTN_BLK)

    def body(q_ref, k_ref, kb_ref, v_ref, vb_ref, o_ref, lse_ref):
        pair = pl.program_id(0)
        first = (_first_flag(2 * pair, per_seq), _first_flag(2 * pair + 1, per_seq))
        o, lse = _attn_pair(_pair_heads(q_ref), _pair_heads(k_ref), _block_heads(kb_ref, 0), _pair_heads(v_ref),
                            _block_heads(vb_ref, 0), first)
        for half in range(2):
            for h in range(ATTN_HPG):
                o_ref[_block_rows(half), _head_cols(h)] = o[half][h]
                lse_ref[_block_rows(half), _head_cols(h)] = lse[half][h]

    cur = lambda col: pl.BlockSpec((2 * ATTN_BLK, N), lambda i: (i, col))
    prv = lambda col: pl.BlockSpec((ATTN_BLK, N), lambda i: (jnp.maximum(2 * i - 1, 0), col))
    return pl.pallas_call(
        body, name=name, grid=(n_pairs,), in_specs=[cur(q_col), cur(k_col), prv(k_col), cur(v_col), prv(v_col)],
        out_specs=[cur(0), cur(0)], out_shape=[jax.ShapeDtypeStruct((R, N), F32)] * 2,
        compiler_params=_cparams(("parallel",), 12 * ATTN_BLK * N * 4),
    )(q, k, k, v, v)


def attn_bwd(q, k, v, do, dlse, per_seq, name):
    views = [_view(a) for a in (q, k, v, do, dlse)]
    (q, q_col), (k, k_col), (v, v_col), (do, do_col), (dlse, dl_col) = views
    R, N = q.shape[0], GROUP_W
    n_pairs = R // (2 * ATTN_BLK)

    def body(q_ref, k_ref, kb_ref, v_ref, vb_ref, do_ref, dl_ref, dq_ref, dk_ref, dv_ref, carry_k, carry_v):
        step = pl.program_id(0)
        pair = n_pairs - 1 - step
        first = (_first_flag(2 * pair, per_seq), _first_flag(2 * pair + 1, per_seq))

        @pl.when(step == 0)
        def _():
            carry_k[...] = jnp.zeros_like(carry_k)
            carry_v[...] = jnp.zeros_like(carry_v)

        _, pull = jax.vjp(functools.partial(_attn_pair, first=first), _pair_heads(q_ref), _pair_heads(k_ref),
                          _block_heads(kb_ref, 0), _pair_heads(v_ref), _block_heads(vb_ref, 0))
        dq, dk, dk_before, dv, dv_before = pull((_pair_heads(do_ref), _pair_heads(dl_ref)))
        old_k, old_v = _block_heads(carry_k, 0), _block_heads(carry_v, 0)
        for h in range(ATTN_HPG):
            cols = _head_cols(h)
            for half in range(2):
                dq_ref[_block_rows(half), cols] = dq[half][h]
            dk_ref[_block_rows(0), cols] = dk[0][h]
            dv_ref[_block_rows(0), cols] = dv[0][h]
            dk_ref[_block_rows(1), cols] = dk[1][h] + old_k[h]
            dv_ref[_block_rows(1), cols] = dv[1][h] + old_v[h]
            carry_k[:, cols] = dk_before[h]
            carry_v[:, cols] = dv_before[h]

    cur = lambda col: pl.BlockSpec((2 * ATTN_BLK, N), lambda i: (n_pairs - 1 - i, col))
    prv = lambda col: pl.BlockSpec((ATTN_BLK, N), lambda i: (jnp.maximum(2 * (n_pairs - 1 - i) - 1, 0), col))
    return pl.pallas_call(
        body, name=name, grid=(n_pairs,),
        in_specs=[cur(q_col), cur(k_col), prv(k_col), cur(v_col), prv(v_col), cur(do_col), cur(dl_col)],
        out_specs=[cur(0)] * 3, out_shape=[jax.ShapeDtypeStruct((R, N), F32)] * 3,
        scratch_shapes=[pltpu.VMEM((ATTN_BLK, N), F32)] * 2,
        compiler_params=_cparams(("arbitrary",), 22 * ATTN_BLK * N * 4),
    )(q, k, k, v, v, do, dlse)


def by_residue(u, d):
    if d == 1:
        return u
    return u.reshape(u.shape[0] // d, d, GROUP_W).transpose(1, 0, 2).reshape(u.shape)


def by_position(u, d):
    if d == 1:
        return u
    return u.reshape(d, u.shape[0] // d, GROUP_W).transpose(1, 0, 2).reshape(u.shape)


def group_columns(t, col_block, d):
    if d == 1:
        return (t, col_block)
    return by_residue(t[:, GROUP_W * col_block:GROUP_W * (col_block + 1)], d)


def _ffn_fwd(x, norm, w_in, w_out, tag, token):
    h = rowmap(lambda x_b, g, tok: _rms(x_b, g) + tok[0:1, 0:1], [x], [norm, token], [(D_MODEL, BF16)], tb=512,
               name=tag + "_norm")[0]
    gu, act = ffn_in_act(h, w_in, tag + "_in")
    y = matmul(act, w_out, "nn", tag + "_out", add=x, scale=0.5)
    return y, (x, h, gu, act)


def _ffn_bwd(dy, saved, norm, w_in, w_out, tag):
    x, h, gu, act = saved
    dw_out = matmul(act, dy, "tn", tag + "_dwout", scale=0.5)
    dgu = ffn_dact_dgu(dy, w_out, gu, 0.5, tag + "_dgu")
    dh = matmul_cs(dgu, w_in, "nt", tag + "_dh")
    dw_in = matmul_cs(h, dgu, "tn", tag + "_dwin")

    def norm_bwd(x_b, dh_b, dy_b, g):
        dx, dg = jax.vjp(_rms, x_b, g)[1](dh_b)
        return dy_b + dx, dg

    dx, dnorm = rowmap(norm_bwd, [x, dh, dy], [norm], [(D_MODEL, F32)], [(1, D_MODEL)], tb=256,
                       name=tag + "_dnorm")
    return dx, dnorm, dw_in, dw_out


def layer_step(x, tgt, W, P, start_token, more_weights, on_mixer_grads):
    S = x.shape[0]
    x1, ffn1_saved = _ffn_fwd(x, P["ffn1_norm"], W["ffn1_w_in"], W["ffn1_w_out"], "ffn1", start_token)
    W = {**W, **more_weights("mixer", x1)}
    head_of = lambda n: jnp.arange(n)[:, None] // HEAD_DIM == jnp.arange(n // HEAD_DIM)[None, :]
    seg, seg_a = head_of(D_MODEL).astype(BF16), head_of(ATTN_WIDTH).astype(BF16)
    seg_t, seg_a_t = seg.T, seg_a.T
    tile_t = (jnp.arange(HEAD_DIM)[:, None] == jnp.arange(ATTN_WIDTH)[None, :] % HEAD_DIM).astype(BF16)
    qk_params = [P["attn_q_norm"], P["attn_k_norm"], seg_a, seg_a_t, tile_t]
    w_rkv, w_lora = W["w_in"][:, :RKV], W["w_in"][:, RKV:RKV + LORA]
    w_qkv = W["w_in"][:, RKV + LORA:RKV + LORA + 3 * ATTN_WIDTH]
    w_gate = W["w_in"][:, RKV + LORA + 3 * ATTN_WIDTH:]
    mu_rk, mu_lo = P["rwkv_mu"][:, :RKV], P["rwkv_mu"][:, RKV:]
    zeros = lambda n: jnp.zeros((n, D_MODEL), F32)
    w2p = jnp.concatenate([W["rwkv_w2"], zeros(LORA - LORA_W)], axis=0)
    a2p = jnp.concatenate([zeros(LORA_W), W["rwkv_a2"], zeros(LORA_G)], axis=0)
    g2p = jnp.concatenate([zeros(LORA_W + LORA_A), W["rwkv_g2"]], axis=0)
    pre_params = [P["rwkv_w0"], w2p, P["rwkv_a0"], a2p, g2p, P["rwkv_k_k"], P["rwkv_k_a"], seg, seg_t]
    post_params = [P["rwkv_r_k"], P["rwkv_ln_w"], P["rwkv_ln_b"], seg, seg_t]
    col = lambda arr, j: (arr, D_MODEL, j)

    h = rowmap(_rms, [x1], [P["mix_norm"]], [(D_MODEL, BF16)], tb=512, name="mix_norm")[0]
    p_rk = matmul(h, w_rkv, "nn", "proj_rkv")
    p_lo = matmul(h, w_lora, "nn", "proj_lora")
    p_qkv = matmul(h, w_qkv, "nn", "proj_qkv")
    p_gate = matmul(h, w_gate, "nn", "proj_gate")
    xs_rk = token_shift_fwd(p_rk, mu_rk, tb=256, name="shift_rk")
    xs_lo = token_shift_fwd(p_lo, mu_lo, tb=256, name="shift_lora")
    lw, k_mod, a_neg, b_kk, g = rowmap(
        _rwkv_pre, [xs_rk, xs_lo], pre_params, [(D_MODEL, F32)] * 5, tb=256, name="rwkv_pre")
    wkv, states, t_invs = wkv_fwd(xs_rk, lw, k_mod, a_neg, b_kk)
    post_rows = [wkv, col(xs_rk, 0), k_mod, col(xs_rk, 2), g]
    y_a = rowmap(_rwkv_post, post_rows, post_params, [(D_MODEL, BF16)], tb=256, name="rwkv_post")[0]

    qk_rows = [(p_qkv, ATTN_WIDTH, 0), (p_qkv, ATTN_WIDTH, 1)]
    qn, kn = rowmap(_qk_norm, qk_rows, qk_params, [(ATTN_WIDTH, F32)] * 2, tb=256, name="qk_norm")
    dil = [d for _, d in ATTN_PAIRS]
    groups = range(len(dil))
    per_seq = [S // d // ATTN_BLK for d in dil]
    v_first = 2 * ATTN_WIDTH // GROUP_W
    q_s = [group_columns(qn, g, dil[g]) for g in groups]
    k_s = [group_columns(kn, g, dil[g]) for g in groups]
    v_s = [group_columns(p_qkv, v_first + g, dil[g]) for g in groups]
    attn = [attn_fwd(q_s[g], k_s[g], v_s[g], per_seq[g], "attn_fwd_%d" % g) for g in groups]
    o_lse = [by_position(attn[g][j], dil[g]) for j in range(2) for g in groups]
    y_b = rowmap(_group_combine, o_lse, [], [(ATTN_WIDTH, BF16)], tb=512, name="attn_combine")[0]

    W = {**W, **more_weights("out", y_b)}
    pa = matmul(y_a, W["w_proj_rwkv"], "nn", "proj_a")
    pb = matmul(y_b, W["w_proj_attn"], "nn", "proj_b")
    merged = rowmap(_gate_merge, [p_gate, pa, pb], [P["b_gate"]], [(D_MODEL, BF16)], tb=256, name="merge")[0]
    x2 = matmul(merged, W["w_out"], "nn", "mix_out", add=x1)
    x3, ffn2_saved = _ffn_fwd(x2, P["ffn2_norm"], W["ffn2_w_in"], W["ffn2_w_out"], "ffn2",
                              jnp.zeros_like(start_token))

    def loss_head(y_b_, t_b):
        err = y_b_ - t_b
        return err * (1.0 / D_MODEL), (0.5 / D_MODEL) * jnp.sum(err * err, axis=0, keepdims=True)

    dx3, loss_cols = rowmap(loss_head, [x3, tgt], [], [(D_MODEL, F32)], [(1, D_MODEL)], tb=512, name="loss")

    gW, gP = {}, {}
    dx2, gP["ffn2_norm"], gW["ffn2_w_in"], gW["ffn2_w_out"] = _ffn_bwd(
        dx3, ffn2_saved, P["ffn2_norm"], W["ffn2_w_in"], W["ffn2_w_out"], "ffn2")

    dmerged = matmul(dx2, W["w_out"], "nt", "d_merged")
    gW["w_out"] = matmul(merged, dx2, "tn", "dw_out")

    def merge_bwd(pg, pa_b, pb_b, dm, bg):
        return jax.vjp(_gate_merge, pg, pa_b, pb_b, bg)[1](dm)

    dp_gate, dpa, dpb, gP["b_gate"] = rowmap(
        merge_bwd, [p_gate, pa, pb, dmerged], [P["b_gate"]],
        [(2 * D_MODEL, BF16), (D_MODEL, BF16), (D_MODEL, BF16)], [(1, 2 * D_MODEL)], tb=256, name="merge_bwd")
    dy_a = matmul(dpa, W["w_proj_rwkv"], "nt", "d_ya")
    gW["w_proj_rwkv"] = matmul(y_a, dpa, "tn", "dw_proj_a")
    dy_b = matmul(dpb, W["w_proj_attn"], "nt", "d_yb")
    gW["w_proj_attn"] = matmul(y_b, dpb, "tn", "dw_proj_b")

    def combine_bwd(*blocks):
        return jax.vjp(_group_combine, *blocks[:-1])[1](blocks[-1])

    d_o_lse = rowmap(combine_bwd, o_lse + [dy_b], [], [(GROUP_W, F32)] * 6, tb=256, name="attn_combine_bwd")
    d_attn = [attn_bwd(q_s[g], k_s[g], v_s[g], by_residue(d_o_lse[g], dil[g]), by_residue(d_o_lse[3 + g], dil[g]),
                       per_seq[g], "attn_bwd_%d" % g) for g in groups]

    def qk_norm_bwd(q_b, k_b, *rest):
        dqkv, (qg, kg, sg, sgt, tl) = rest[:9], rest[9:]
        f = lambda *a: _qk_norm(*a, sg, sgt, tl)
        dqn, dkn = jnp.concatenate(dqkv[0:3], axis=1), jnp.concatenate(dqkv[3:6], axis=1)
        dq, dk, dqg, dkg = jax.vjp(f, q_b, k_b, qg, kg)[1]((dqn, dkn))
        return jnp.concatenate([dq, dk, *dqkv[6:9]], axis=1), dqg, dkg

    dp_qkv, gP["attn_q_norm"], gP["attn_k_norm"] = rowmap(
        qk_norm_bwd, qk_rows + [by_position(d_attn[g][j], dil[g]) for j in range(3) for g in groups], qk_params,
        [(3 * ATTN_WIDTH, BF16)], [(1, HEAD_DIM)] * 2, tb=256, name="qk_norm_bwd")

    def post_bwd(wkv_b, r_b, k_b, v_b, g_b, d_b, r_k, ln_w, ln_b, sg, sgt):
        f = lambda *a: _rwkv_post(*a, sg, sgt)
        return jax.vjp(f, wkv_b, r_b, k_b, v_b, g_b, r_k, ln_w, ln_b)[1](d_b)

    dwkv, dr_p, dk_p, dv_p, dg, gP["rwkv_r_k"], gP["rwkv_ln_w"], gP["rwkv_ln_b"] = rowmap(
        post_bwd, post_rows + [dy_a], post_params, [(D_MODEL, F32)] * 5, [(1, D_MODEL)] * 3, tb=128,
        name="rwkv_post_bwd")
    dr_w, dlw, dk_w, dv_w, da_neg, db_kk = wkv_bwd(xs_rk, lw, k_mod, a_neg, b_kk, states, t_invs, dwkv)

    def pre_bwd(xrk_b, xlo_b, dlw_b, dkw_b, dkp_b, da_b, db_b, dg_b, drp_b, drw_b, dvp_b, dvw_b,
                w0, w2, a0, a2, g2, k_k, k_a, sg, sgt):
        f = lambda *a: _rwkv_pre(*a, sg, sgt)
        pull = jax.vjp(f, xrk_b, xlo_b, w0, w2, a0, a2, g2, k_k, k_a)[1]
        dxrk, dxlo, *dpar = pull((dlw_b, dkw_b + dkp_b, da_b, db_b, dg_b))
        direct = jnp.concatenate([drp_b + drw_b, jnp.zeros_like(drp_b), dvp_b + dvw_b], axis=1)
        return (dxrk + direct, dxlo, *dpar)

    pre_rows = [xs_rk, xs_lo, dlw, dk_w, dk_p, da_neg, db_kk, dg, dr_p, dr_w, dv_p, dv_w]
    dxs_rk, dxs_lo, gP["rwkv_w0"], dw2p, gP["rwkv_a0"], da2p, dg2p, gP["rwkv_k_k"], gP["rwkv_k_a"] = rowmap(
        pre_bwd, pre_rows, pre_params, [(RKV, F32), (LORA, F32)],
        [(1, D_MODEL), (LORA, D_MODEL), (1, D_MODEL), (LORA, D_MODEL), (LORA, D_MODEL), (1, D_MODEL), (1, D_MODEL)],
        tb=128, name="rwkv_pre_bwd")
    gW["rwkv_w2"] = dw2p[:LORA_W]
    gW["rwkv_a2"] = da2p[LORA_W:LORA_W + LORA_A]
    gW["rwkv_g2"] = dg2p[LORA_W + LORA_A:]
    dp_rk, dmu_rk = token_shift_bwd(dxs_rk, p_rk, mu_rk, tb=256, name="shift_rk_bwd")
    dp_lo, dmu_lo = token_shift_bwd(dxs_lo, p_lo, mu_lo, tb=256, name="shift_lora_bwd")
    gP["rwkv_mu"] = jnp.concatenate([dmu_rk, dmu_lo], axis=1)

    dh = matmul(dp_rk, w_rkv, "nt", "dh_rkv")
    dh = matmul(dp_lo, w_lora, "nt", "dh_lora", add=dh)
    dh = matmul(dp_qkv, w_qkv, "nt", "dh_qkv", add=dh)
    dh = matmul(dp_gate, w_gate, "nt", "dh_gate", add=dh)
    gW["w_in"] = jnp.concatenate([
        matmul(h, dp_rk, "tn", "dw_rkv"), matmul(h, dp_lo, "tn", "dw_lora"),
        matmul(h, dp_qkv, "tn", "dw_qkv"), matmul(h, dp_gate, "tn", "dw_gate")], axis=1)

    token = on_mixer_grads(gW)

    def norm_bwd(x_b, dh_b, dy_b, gn, tok):
        dx, dgn = jax.vjp(_rms, x_b, gn)[1](dh_b)
        return dy_b + dx + tok[0:1, 0:1], dgn

    dx1, gP["mix_norm"] = rowmap(norm_bwd, [x1, dh, dx2], [P["mix_norm"], token], [(D_MODEL, F32)],
                                 [(1, D_MODEL)], tb=256, name="mix_norm_bwd")
    dx, gP["ffn1_norm"], gW["ffn1_w_in"], gW["ffn1_w_out"] = _ffn_bwd(
        dx1, ffn1_saved, P["ffn1_norm"], W["ffn1_w_in"], W["ffn1_w_out"], "ffn1")
    return loss_cols, dx, gW, gP


N_SHARDS = 4
BIG = (("ffn1_w_in", (D_MODEL, 2 * D_FF), 1), ("ffn1_w_out", (D_FF, D_MODEL), 0),
       ("w_in", (D_MODEL, 7712), 1), ("rwkv_w2", (LORA_W, D_MODEL), 1), ("rwkv_a2", (LORA_A, D_MODEL), 1),
       ("rwkv_g2", (LORA_G, D_MODEL), 1), ("w_proj_rwkv", (D_MODEL, D_MODEL), 0),
       ("w_proj_attn", (ATTN_WIDTH, D_MODEL), 1), ("w_out", (D_MODEL, D_MODEL), 0),
       ("ffn2_w_in", (D_MODEL, 2 * D_FF), 1), ("ffn2_w_out", (D_FF, D_MODEL), 0))
SMALL = (("ffn1_norm", 1024), ("mix_norm", 1024), ("b_gate", 2048), ("rwkv_mu", 3360), ("rwkv_w0", 1024),
         ("rwkv_a0", 1024), ("rwkv_k_k", 1024), ("rwkv_k_a", 1024), ("rwkv_r_k", 1024), ("rwkv_ln_w", 1024),
         ("rwkv_ln_b", 1024), ("attn_q_norm", 64), ("attn_k_norm", 64), ("ffn2_norm", 1024))
WEIGHT_ORDER = ("ffn1_norm", "ffn1_w_in", "ffn1_w_out", "mix_norm", "w_in", "b_gate", "rwkv_mu", "rwkv_w0",
                "rwkv_w2", "rwkv_a0", "rwkv_a2", "rwkv_g2", "rwkv_k_k", "rwkv_k_a", "rwkv_r_k", "rwkv_ln_w",
                "rwkv_ln_b", "attn_q_norm", "attn_k_norm", "w_proj_rwkv", "w_proj_attn", "w_out", "ffn2_norm",
                "ffn2_w_in", "ffn2_w_out")


LORA_PARTS = ("rwkv_w2", "rwkv_a2", "rwkv_g2")
BLOCK_MAJOR = ("ffn1_w_in", "ffn2_w_in")
FIRST_FFN = ("ffn1_w_in", "ffn1_w_out")
MIXER_IN = ("w_in", "lora")
SMALL_USED = D_MODEL + sum(n for _, n in SMALL)
SMALL_W = -(-SMALL_USED // 128) * 128


def _travel():
    out = {}
    for name, shape, axis in BIG:
        if name == LORA_PARTS[0]:
            out["lora"] = ((LORA, D_MODEL), 1)
        elif name not in LORA_PARTS:
            out[name] = (shape, axis)
    return out


def local_blocks(vals):
    out = {n: vals[n] for n in _travel() if n != "lora"}
    out["lora"] = jnp.concatenate([vals[n] for n in LORA_PARTS], axis=0)
    return out


def split_lora(t):
    return {"rwkv_w2": t[:LORA_W], "rwkv_a2": t[LORA_W:LORA_W + LORA_A], "rwkv_g2": t[LORA_W + LORA_A:]}


def blocks_to_full(name, blocks):
    shape, axis = _travel()[name]
    if name in BLOCK_MAJOR:
        return blocks
    if axis == 0:
        return blocks.reshape(shape)
    return blocks.transpose(1, 0, 2).reshape(shape)


def full_to_blocks(name, full):
    shape, axis = _travel()[name]
    if name in BLOCK_MAJOR:
        return full
    if axis == 0:
        return full.reshape(N_SHARDS, shape[0] // N_SHARDS, shape[1])
    return full.reshape(shape[0], N_SHARDS, shape[1] // N_SHARDS).transpose(1, 0, 2)


def pack_small(vals, head):
    parts = [head] + [vals[name].reshape(1, n) for name, n in SMALL]
    parts.append(jnp.zeros((1, SMALL_W - SMALL_USED), F32))
    return jnp.concatenate(parts, axis=1)


def unpack_small(vec, shapes):
    out, off = {}, D_MODEL
    for name, n in SMALL:
        out[name] = vec[:, off:off + n].reshape(shapes[name])
        off += n
    return out


def _place():
    return lax.axis_index("x"), lax.axis_index("y"), lax.axis_index("c")


def _other_chips(x, y):
    return [(1 - x, y), (x, 1 - y), (1 - x, 1 - y)]


def _remote(src, dst, send_sem, recv_sem, device):
    return pltpu.make_async_remote_copy(src_ref=src, dst_ref=dst, send_sem=send_sem, recv_sem=recv_sem,
                                        device_id=device, device_id_type=MESH)


def _half(ref, who):
    hr = ref.shape[-2] // 2
    rows = pl.ds(pl.multiple_of(who * hr, 8), hr)
    return ref.at[rows] if len(ref.shape) == 2 else ref.at[:, rows]


HBM_REF = pl.BlockSpec(memory_space=pl.ANY)
COMM_PARAMS = dict(compiler_params=pltpu.CompilerParams(has_side_effects=True))


def gather_weights(blocks):
    n = len(blocks)

    def body(*refs):
        ins, outs = refs[:n], refs[n:2 * n]
        ici_send, ici_recv, d2d_send, d2d_recv = refs[2 * n:]
        x, y, c = _place()
        me, sibling, chips = 2 * x + y, (x, y, 1 - c), _other_chips(x, y)
        first = [_remote(_half(ins[t], c), _half(outs[t].at[me], c), ici_send.at[k, t], ici_recv.at[k, t],
                         (px, py, c)) for k, (px, py) in enumerate(chips) for t in range(n)]
        for cp in first:
            cp.start()
        passed = []
        for k, (px, py) in enumerate(chips):
            for t in range(n):
                landed = _half(outs[t].at[2 * px + py], c)
                _remote(landed, landed, ici_send.at[k, t], ici_recv.at[k, t], (px, py, c)).wait_recv()
                cp = _remote(landed, landed, d2d_send.at[k, t], d2d_recv.at[k, t], sibling)
                cp.start()
                passed.append(cp)
        for k, (px, py) in enumerate(chips):
            for t in range(n):
                other = _half(outs[t].at[2 * px + py], 1 - c)
                _remote(other, other, d2d_send.at[k, t], d2d_recv.at[k, t], sibling).wait_recv()
        for cp in first + passed:
            cp.wait_send()

    res = pl.pallas_call(
        body, name="gather_weights", in_specs=[HBM_REF] * n, out_specs=[HBM_REF] * n,
        out_shape=[jax.ShapeDtypeStruct((N_SHARDS,) + b.shape, b.dtype) for b in blocks],
        scratch_shapes=[pltpu.SemaphoreType.DMA((3, n))] * 4, **COMM_PARAMS)(*blocks)
    me = 2 * lax.axis_index("x") + lax.axis_index("y")
    return [lax.dynamic_update_slice(g, b[None], (me, 0, 0)) for g, b in zip(res, blocks)]


def _gather_copies(ins, outs, send_sem, recv_sem):
    x, y, c = _place()
    return [_remote(_half(ins[t], c), _half(outs[t].at[2 * x + y], c), send_sem(k, t), recv_sem(k, t), (px, py, c))
            for k, (px, py) in enumerate(_other_chips(x, y)) for t in range(len(ins))]


def gather_start(blocks, name):
    n = len(blocks)
    n_cp = 3 * n

    def body(*refs):
        ins, outs = refs[:n], refs[n:2 * n]
        sems, token = refs[2 * n:2 * n + 2 * n_cp], refs[-1]
        for cp in _gather_copies(ins, outs, lambda k, t: sems[k * n + t], lambda k, t: sems[n_cp + k * n + t]):
            cp.start()
        token[...] = jnp.zeros_like(token)

    hbm = lambda a: pltpu.with_memory_space_constraint(a, pltpu.HBM)
    landing = [lax.empty((N_SHARDS,) + b.shape, b.dtype) for b in blocks]
    res = pl.pallas_call(
        body, name=name,
        out_shape=(*[pltpu.SemaphoreType.DMA(())] * (2 * n_cp),
                   *[pltpu.HBM(a.shape, a.dtype) for a in list(blocks) + landing], jax.ShapeDtypeStruct((8, 128), F32)),
        in_specs=[SPLIT_HBM] * (2 * n),
        out_specs=(*[SPLIT_SEM] * (2 * n_cp), *[SPLIT_HBM] * (2 * n), pl.BlockSpec(memory_space=pltpu.VMEM)),
        input_output_aliases={t: 2 * n_cp + t for t in range(2 * n)}, **SPLIT_PARAMS,
    )(*[hbm(a) for a in list(blocks) + landing])
    return (n, res[:-1]), res[-1]


def gather_wait(handles, after, name):
    n, held = handles
    n_cp = 3 * n
    sems, thru = held[:2 * n_cp], held[2 * n_cp:]

    def body(*refs):
        ins, outs = refs[:n], refs[n:2 * n]
        sem_refs = refs[2 * n:2 * n + 2 * n_cp]
        for cp in _gather_copies(ins, outs, lambda k, t: sem_refs[k * n + t], lambda k, t: sem_refs[n_cp + k * n + t]):
            cp.wait_send()
            cp.wait_recv()

    res = pl.pallas_call(
        body, name=name, out_shape=tuple(pltpu.HBM(a.shape, a.dtype) for a in thru),
        in_specs=[SPLIT_HBM] * (2 * n) + [SPLIT_SEM] * (2 * n_cp) + [pl.BlockSpec(memory_space=pl.ANY)],
        out_specs=tuple([SPLIT_HBM] * (2 * n)), input_output_aliases={t: t for t in range(2 * n)}, **SPLIT_PARAMS,
    )(*thru, *sems, after)
    return list(res[n:])


def pass_halves(gathered, blocks, name):
    n = len(gathered)

    def body(*refs):
        outs = refs[n:2 * n]
        send_sems, recv_sems = refs[2 * n:]
        x, y, c = _place()
        slots = [2 * px + py for px, py in _other_chips(x, y)]
        give = [_remote(_half(outs[t].at[s], c), _half(outs[t].at[s], c), send_sems.at[k, t], recv_sems.at[k, t],
                        (x, y, 1 - c)) for k, s in enumerate(slots) for t in range(n)]
        for cp in give:
            cp.start()
        for k, s in enumerate(slots):
            for t in range(n):
                other = _half(outs[t].at[s], 1 - c)
                _remote(other, other, send_sems.at[k, t], recv_sems.at[k, t], (x, y, 1 - c)).wait_recv()
        for cp in give:
            cp.wait_send()

    res = pl.pallas_call(
        body, name=name, in_specs=[HBM_REF] * n, out_specs=[HBM_REF] * n,
        out_shape=[jax.ShapeDtypeStruct(g.shape, g.dtype) for g in gathered],
        input_output_aliases={t: t for t in range(n)},
        scratch_shapes=[pltpu.SemaphoreType.DMA((3, n))] * 2, **COMM_PARAMS)(*gathered)
    me = 2 * lax.axis_index("x") + lax.axis_index("y")
    return [lax.dynamic_update_slice(g, b[None], (me, 0, 0)) for g, b in zip(res, blocks)]


def swap_halves(grads):
    n = len(grads)

    def body(*refs):
        ins, got = refs[:n], refs[n:2 * n]
        send_sems, recv_sems = refs[2 * n:]
        x, y, c = _place()
        give = [_remote(_half(ins[t], 1 - c), got[t], send_sems.at[t], recv_sems.at[t], (x, y, 1 - c))
                for t in range(n)]
        for cp in give:
            cp.start()
        for cp in give:
            cp.wait_recv()
        for cp in give:
            cp.wait_send()

    return pl.pallas_call(
        body, name="swap_halves", in_specs=[HBM_REF] * n, out_specs=[HBM_REF] * n,
        out_shape=[jax.ShapeDtypeStruct((g.shape[0], g.shape[1] // 2, g.shape[2]), g.dtype) for g in grads],
        scratch_shapes=[pltpu.SemaphoreType.DMA((n,))] * 2, **COMM_PARAMS)(*grads)


def join_halves(blocks):
    n = len(blocks)

    def body(*refs):
        outs = refs[n:2 * n]
        send_sems, recv_sems = refs[2 * n:]
        x, y, c = _place()
        give = [_remote(_half(outs[t], c), _half(outs[t], c), send_sems.at[t], recv_sems.at[t], (x, y, 1 - c))
                for t in range(n)]
        for cp in give:
            cp.start()
        for t in range(n):
            arriving = _half(outs[t], 1 - c)
            _remote(arriving, arriving, send_sems.at[t], recv_sems.at[t], (x, y, 1 - c)).wait_recv()
        for cp in give:
            cp.wait_send()

    return pl.pallas_call(
        body, name="join_halves", in_specs=[HBM_REF] * n, out_specs=[HBM_REF] * n,
        out_shape=[jax.ShapeDtypeStruct(b.shape, b.dtype) for b in blocks],
        input_output_aliases={t: t for t in range(n)},
        scratch_shapes=[pltpu.SemaphoreType.DMA((n,))] * 2, **COMM_PARAMS)(*blocks)


SPLIT_HBM = pl.BlockSpec(memory_space=pltpu.HBM)
SPLIT_SEM = pl.BlockSpec(memory_space=pltpu.SEMAPHORE)
SPLIT_PARAMS = dict(compiler_params=pltpu.CompilerParams(has_side_effects=pltpu.SideEffectType.DATAFLOW_SIDE_EFFECTING))


def _scatter_copies(parts, landed, send_sem, recv_sem):
    x, y, c = _place()
    return [_remote(parts[t].at[2 * px + py], landed[t].at[k], send_sem(k, t), recv_sem(k, t), (px, py, c))
            for k, (px, py) in enumerate(_other_chips(x, y)) for t in range(len(parts))]


def scatter_start(partials, name):
    n = len(partials)
    n_cp = 3 * n

    def body(*refs):
        parts, landed = refs[:n], refs[n:2 * n]
        sems, token = refs[2 * n:2 * n + 2 * n_cp], refs[-1]
        for cp in _scatter_copies(parts, landed, lambda k, t: sems[k * n + t], lambda k, t: sems[n_cp + k * n + t]):
            cp.start()
        token[...] = jnp.zeros_like(token)

    hbm = lambda a: pltpu.with_memory_space_constraint(a, pltpu.HBM)
    landing = [lax.empty((3,) + p.shape[1:], p.dtype) for p in partials]
    res = pl.pallas_call(
        body, name=name,
        out_shape=(*[pltpu.SemaphoreType.DMA(())] * (2 * n_cp),
                   *[pltpu.HBM(a.shape, a.dtype) for a in partials + landing], jax.ShapeDtypeStruct((8, 128), F32)),
        in_specs=[SPLIT_HBM] * (2 * n),
        out_specs=(*[SPLIT_SEM] * (2 * n_cp), *[SPLIT_HBM] * (2 * n), pl.BlockSpec(memory_space=pltpu.VMEM)),
        input_output_aliases={t: 2 * n_cp + t for t in range(2 * n)}, **SPLIT_PARAMS,
    )(*[hbm(a) for a in partials + landing])
    return (n, res[:-1]), res[-1]


def scatter_wait(handles, after, name):
    n, held = handles
    n_cp = 3 * n
    sems, thru = held[:2 * n_cp], held[2 * n_cp:]

    def body(*refs):
        parts, landed = refs[:n], refs[n:2 * n]
        sem_refs = refs[2 * n:2 * n + 2 * n_cp]
        for cp in _scatter_copies(parts, landed, lambda k, t: sem_refs[k * n + t],
                                  lambda k, t: sem_refs[n_cp + k * n + t]):
            cp.wait_send()
            cp.wait_recv()

    res = pl.pallas_call(
        body, name=name, out_shape=tuple(pltpu.HBM(a.shape, a.dtype) for a in thru),
        in_specs=[SPLIT_HBM] * (2 * n) + [SPLIT_SEM] * (2 * n_cp) + [pl.BlockSpec(memory_space=pl.ANY)],
        out_specs=tuple([SPLIT_HBM] * (2 * n)), input_output_aliases={t: t for t in range(2 * n)}, **SPLIT_PARAMS,
    )(*thru, *sems, after)
    return list(res[n:])


def chip_sums(grads):
    names = list(grads)
    got = swap_halves([grads[n] for n in names])
    partials = []
    for name, theirs in zip(names, got):
        n_slot, hr, width = theirs.shape
        tb = _row_block(hr, width, 6)
        per_half = hr // tb
        mine = lambda i, s, per_half=per_half: (i // per_half) * 2 * per_half + s[0] * per_half + i % per_half
        p = placed_map(
            jnp.add,
            [(grads[name].reshape(2 * n_slot * hr, width), mine), (theirs.reshape(n_slot * hr, width), lambda i, s: i)],
            (n_slot * hr, width, BF16, lambda i, s: i), n_blocks=n_slot * per_half, tb=tb, name="chip_sum_" + name)
        partials.append(p.reshape(theirs.shape))
    return got, partials


def owner_sums(grads, got, landed):
    names = list(grads)
    blocks = []
    for name, theirs, arrived in zip(names, got, landed):
        n_slot, hr, width = theirs.shape
        tb = _row_block(hr, width, 6)
        per_half = hr // tb
        views = [(grads[name].reshape(2 * n_slot * hr, width),
                  lambda i, s, per_half=per_half: s[1] * 2 * per_half + s[0] * per_half + i),
                 (theirs.reshape(n_slot * hr, width), lambda i, s, per_half=per_half: s[1] * per_half + i)]
        views += [(arrived.reshape(3 * hr, width), functools.partial(lambda k, per_half, i, s: k * per_half + i,
                                                                     k, per_half)) for k in range(3)]
        f = lambda a, b, l0, l1, l2: (((a + b) + l0.astype(F32)) + l1.astype(F32)) + l2.astype(F32)
        blocks.append(placed_map(
            f, views,(2 * hr, width, F32, lambda i, s, per_half=per_half: s[0] * per_half + i),
            n_blocks=per_half, tb=tb, name="owner_sum_" + name))
    return dict(zip(names, join_halves(blocks)))


def adamw_block(name, w, g, m, v):
    rows, width = w.shape
    return rowmap(_adamw, [w, g, m, v], [], [(width, F32)] * 3, tb=_row_block(rows, width, 7),
                  name="adamw_" + name)


def reduce_small(vec, w, m, v):
    n_dev = 8

    def body(vec_ref, w_ref, m_ref, v_ref, loss_ref, g_ref, d_ref, m2_ref, v2_ref, slots, send_sems, recv_sems):
        x, y, c = _place()
        me = 4 * x + 2 * y + c
        slots[me] = vec_ref[...]
        flips = [(fx, fy, fc) for fx in (0, 1) for fy in (0, 1) for fc in (0, 1)][1:]
        peers = [(1 - x if fx else x, 1 - y if fy else y, 1 - c if fc else c) for fx, fy, fc in flips]
        sends = [pltpu.make_async_remote_copy(
            src_ref=vec_ref, dst_ref=slots.at[me], send_sem=send_sems.at[j], recv_sem=recv_sems.at[j],
            device_id=peer, device_id_type=MESH) for j, peer in enumerate(peers)]
        for cp in sends:
            cp.start()
        for j, (px, py, pc) in enumerate(peers):
            pltpu.make_async_remote_copy(
                src_ref=vec_ref, dst_ref=slots.at[4 * px + 2 * py + pc], send_sem=send_sems.at[j],
                recv_sem=recv_sems.at[j], device_id=(px, py, pc), device_id_type=MESH).wait_recv()
        for cp in sends:
            cp.wait_send()
        g = slots[0]
        for d in range(1, n_dev):
            g = g + slots[d]
        loss_ref[...] = jnp.sum(g[:, :D_MODEL], axis=1, keepdims=True)
        delta, m2, v2 = _adamw(w_ref[...], g, m_ref[...], v_ref[...])
        g_ref[...], d_ref[...], m2_ref[...], v2_ref[...] = g, delta, m2, v2

    vm = pl.BlockSpec(memory_space=pltpu.VMEM)
    vec_t = jax.ShapeDtypeStruct(vec.shape, F32)
    return pl.pallas_call(
        body, name="reduce_small", in_specs=[vm] * 4, out_specs=[vm] * 5,
        out_shape=[jax.ShapeDtypeStruct((1, 1), F32)] + [vec_t] * 4,
        scratch_shapes=[pltpu.VMEM((n_dev,) + vec.shape, F32), pltpu.SemaphoreType.DMA((n_dev - 1,)),
                        pltpu.SemaphoreType.DMA((n_dev - 1,))],
        compiler_params=pltpu.CompilerParams(has_side_effects=True),
    )(vec, w, m, v)


def kernel(x, ffn1_norm, ffn1_w_in, ffn1_w_out, mix_norm, w_in, b_gate, rwkv_mu, rwkv_w0, rwkv_w2, rwkv_a0, rwkv_a2, rwkv_g2, rwkv_k_k, rwkv_k_a, rwkv_r_k, rwkv_ln_w, rwkv_ln_b, attn_q_norm, attn_k_norm, w_proj_rwkv, w_proj_attn, w_out, ffn2_norm, ffn2_w_in, ffn2_w_out, loss_target, m_ffn1_norm, m_ffn1_w_in, m_ffn1_w_out, m_mix_norm, m_w_in, m_b_gate, m_rwkv_mu, m_rwkv_w0, m_rwkv_w2, m_rwkv_a0, m_rwkv_a2, m_rwkv_g2, m_rwkv_k_k, m_rwkv_k_a, m_rwkv_r_k, m_rwkv_ln_w, m_rwkv_ln_b, m_attn_q_norm, m_attn_k_norm, m_w_proj_rwkv, m_w_proj_attn, m_w_out, m_ffn2_norm, m_ffn2_w_in, m_ffn2_w_out, v_ffn1_norm, v_ffn1_w_in, v_ffn1_w_out, v_mix_norm, v_w_in, v_b_gate, v_rwkv_mu, v_rwkv_w0, v_rwkv_w2, v_rwkv_a0, v_rwkv_a2, v_rwkv_g2, v_rwkv_k_k, v_rwkv_k_a, v_rwkv_r_k, v_rwkv_ln_w, v_rwkv_ln_b, v_attn_q_norm, v_attn_k_norm, v_w_proj_rwkv, v_w_proj_attn, v_w_out, v_ffn2_norm, v_ffn2_w_in, v_ffn2_w_out):
    given = dict(locals())
    weights = {n: given[n] for n in WEIGHT_ORDER}
    mom_m = {n: given["m_" + n] for n in WEIGHT_ORDER}
    mom_v = {n: given["v_" + n] for n in WEIGHT_ORDER}
    big = [name for name, _, _ in BIG]
    shapes = {n: weights[n].shape for n in WEIGHT_ORDER}
    blocks_of = lambda d: local_blocks({n: d[n][0] for n in big})
    w_blk, m_blk, v_blk = blocks_of(weights), blocks_of(mom_m), blocks_of(mom_v)
    names = list(w_blk)

    early = [n for n in names if n not in FIRST_FFN]
    bf16_block = lambda n: w_blk[n].astype(BF16)
    W = {n: blocks_to_full(n, g) for n, g in zip(FIRST_FFN, gather_weights([bf16_block(n) for n in FIRST_FFN]))}
    stages = {"mixer": [n for n in early if n in MIXER_IN], "out": [n for n in early if n not in MIXER_IN]}
    stage_blocks = {s: [bf16_block(n) for n in stages[s]] for s in stages}
    started = {s: gather_start(stage_blocks[s], "gather_start_" + s) for s in ("mixer", "out")}
    start_token = started["mixer"][1] + started["out"][1]

    def more_weights(stage, after):
        landed = gather_wait(started[stage][0], after, "gather_wait_" + stage)
        got = pass_halves(landed, stage_blocks[stage], "pass_halves_" + stage)
        more = {n: blocks_to_full(n, g) for n, g in zip(stages[stage], got)}
        if "lora" in more:
            more.update(split_lora(more.pop("lora")))
        return more

    P = {n: weights[n].reshape(1, -1) for n, _ in SMALL}

    sent = {}

    def send_early(gw):
        lora = jnp.concatenate([gw[n] for n in LORA_PARTS], axis=0)
        sent["grads"] = {n: full_to_blocks(n, lora if n == "lora" else gw[n]) for n in early}
        sent["got"], partials = chip_sums(sent["grads"])
        sent["handles"], token = scatter_start(partials, "scatter_start")
        return token

    loss_cols, dx, gW, gP = layer_step(x[0], loss_target[0], W, P, start_token, more_weights, send_early)
    landed = scatter_wait(sent["handles"], gP["ffn1_norm"], "scatter_wait")

    late = {n: full_to_blocks(n, gW[n]) for n in FIRST_FFN}
    late_got, late_partials = chip_sums(late)
    late_handles, late_token = scatter_start(late_partials, "scatter_start_ffn1")
    landed[-1] = landed[-1] + late_token[0, 0].astype(landed[-1].dtype)
    out_g, out_d, out_m, out_v = {}, {}, {}, {}

    def apply(g_blk):
        for n in g_blk:
            res = (g_blk[n], *adamw_block(n, w_blk[n], g_blk[n], m_blk[n], v_blk[n]))
            for dst, t in zip((out_g, out_d, out_m, out_v), res):
                for part, val in (split_lora(t) if n == "lora" else {n: t}).items():
                    dst[part] = val.reshape(shapes[part])

    apply(owner_sums(sent["grads"], sent["got"], landed))
    apply(owner_sums(late, late_got, scatter_wait(late_handles, list(out_d.values())[-1], "scatter_wait_ffn1")))

    zero_head = jnp.zeros((1, D_MODEL), F32)
    vec = pack_small(gP, loss_cols)
    loss, g_s, d_s, m_s, v_s = reduce_small(
        vec, pack_small({n: weights[n] for n, _ in SMALL}, zero_head),
        pack_small({n: mom_m[n] for n, _ in SMALL}, zero_head),
        pack_small({n: mom_v[n] for n, _ in SMALL}, zero_head))
    for dst, src in ((out_g, g_s), (out_d, d_s), (out_m, m_s), (out_v, v_s)):
        dst.update(unpack_small(src, shapes))

    return (loss[0, 0], dx[None], *[out_g[n] for n in WEIGHT_ORDER], *[out_d[n] for n in WEIGHT_ORDER],
            *[out_m[n] for n in WEIGHT_ORDER], *[out_v[n] for n in WEIGHT_ORDER])
```

```python
import functools

import jax
import jax.numpy as jnp
from jax import lax
from jax.experimental import pallas as pl
from jax.experimental.pallas import tpu as pltpu

F32 = jnp.float32
BF16 = jnp.bfloat16
MESH = pl.DeviceIdType.MESH

D_MODEL = 1024
HEAD_DIM = 64
RWKV_HEADS = 16
LORA_W, LORA_A, LORA_G = 64, 64, 160
LORA = LORA_W + LORA_A + LORA_G
RKV = 3 * D_MODEL
ATTN_PAIRS = ((128, 1), (512, 4), (2048, 16))
ATTN_BLK = 128
ATTN_HPG = 4
ATTN_WIDTH = 768
GROUP_W = ATTN_HPG * HEAD_DIM
D_FF = 2816
GN_EPS = 64e-5
RMS_EPS = 1e-6
NEG_INF = -1e30
WKV_CHUNK = 64
WKV_HEADS_PER_STEP = 16

ADAM_LR, ADAM_B1, ADAM_B2, ADAM_EPS, ADAM_WD, ADAM_STEP = 0.001, 0.9, 0.999, 1e-08, 0.01, 10

V7X_VMEM_BYTES = 64 << 20
VMEM_TEMP_ALLOWANCE = 20 << 20
VMEM_LEFT_FREE = 6 << 20


def _cparams(sem, block_bytes):
    limit = min(2 * block_bytes + VMEM_TEMP_ALLOWANCE, V7X_VMEM_BYTES - VMEM_LEFT_FREE)
    return pltpu.CompilerParams(dimension_semantics=sem, vmem_limit_bytes=int(limit))


def _nbytes(shape, dtype):
    n = 1
    for s in shape:
        n *= s
    return n * jnp.dtype(dtype).itemsize


def _split_bf16(a):
    hi = a.astype(BF16)
    return hi, (a - hi.astype(F32)).astype(BF16)


def _make_dots():
    def raw(a, b, ca, cb):
        return lax.dot_general(a.astype(BF16), b.astype(BF16), (((ca,), (cb,)), ((), ())),
                               preferred_element_type=F32)

    @jax.custom_vjp
    def nn(a, b):
        return raw(a, b, 1, 0)

    @jax.custom_vjp
    def nt(a, b):
        return raw(a, b, 1, 1)

    @jax.custom_vjp
    def tn(a, b):
        return raw(a, b, 0, 0)

    nn.defvjp(lambda a, b: (raw(a, b, 1, 0), (a, b)),
              lambda res, g: (raw(g, res[1], 1, 1), raw(res[0], g, 0, 0)))
    nt.defvjp(lambda a, b: (raw(a, b, 1, 1), (a, b)),
              lambda res, g: (raw(g, res[1], 1, 0), raw(g, res[0], 0, 0)))
    tn.defvjp(lambda a, b: (raw(a, b, 0, 0), (a, b)),
              lambda res, g: (raw(res[1], g, 1, 1), raw(res[0], g, 1, 0)))
    return nn, nt, tn


def _exact_rhs_dot(x, ones, cx, co):
    hi, lo = _split_bf16(x)
    dims = (((cx,), (co,)), ((), ()))
    return (lax.dot_general(hi, ones, dims, preferred_element_type=F32)
            + lax.dot_general(lo, ones, dims, preferred_element_type=F32))


@jax.custom_vjp
def SEG(x, ones):
    return _exact_rhs_dot(x, ones, 1, 0)


SEG.defvjp(lambda x, ones: (_exact_rhs_dot(x, ones, 1, 0), ones),
           lambda ones, g: (_exact_rhs_dot(g, ones, 1, 1), jnp.zeros_like(ones)))

NN, NT, TN = _make_dots()


MM_TILE_M, MM_TILE_N, MM_TILE_K = 1408, 1408, 1536


def _pick(n, cap):
    best = None
    for t in range(128, min(n, cap) + 1, 128):
        if n % t == 0:
            best = t
    return best or n


def matmul(a, b, mode, name, *, add=None, scale=1.0, out_dtype=F32):
    if mode == "nn":
        (M, K), (K2, N) = a.shape, b.shape
    elif mode == "nt":
        (M, K), (N, K2) = a.shape, b.shape
    else:
        (K, M), (K2, N) = a.shape, b.shape
    assert K == K2, (name, a.shape, b.shape)
    tm, tn, tk = _pick(M, MM_TILE_M), _pick(N, MM_TILE_N), _pick(K, MM_TILE_K)
    nk = K // tk
    ca, cb = {"nn": (1, 0), "nt": (1, 1), "tn": (0, 0)}[mode]

    def body(*refs):
        if add is None:
            a_ref, b_ref, o_ref, acc_ref = refs
        else:
            a_ref, b_ref, add_ref, o_ref, acc_ref = refs
        k = pl.program_id(2)

        @pl.when(k == 0)
        def _():
            acc_ref[...] = jnp.zeros_like(acc_ref)

        acc_ref[...] += lax.dot_general(a_ref[...].astype(BF16), b_ref[...].astype(BF16),
                                        (((ca,), (cb,)), ((), ())), preferred_element_type=F32)

        @pl.when(k == nk - 1)
        def _():
            r = acc_ref[...] * scale
            if add is not None:
                r = add_ref[...] + r
            o_ref[...] = r.astype(o_ref.dtype)

    a_spec = (pl.BlockSpec((tk, tm), lambda i, j, k: (k, i)) if mode == "tn"
              else pl.BlockSpec((tm, tk), lambda i, j, k: (i, k)))
    b_spec = (pl.BlockSpec((tn, tk), lambda i, j, k: (j, k)) if mode == "nt"
              else pl.BlockSpec((tk, tn), lambda i, j, k: (k, j)))
    in_specs, args = [a_spec, b_spec], [a, b]
    blk = tm * tk * a.dtype.itemsize + tk * tn * b.dtype.itemsize + tm * tn * 8
    if add is not None:
        in_specs.append(pl.BlockSpec((tm, tn), lambda i, j, k: (i, j)))
        args.append(add)
        blk += tm * tn * 4
    return pl.pallas_call(
        body, name=name, grid=(M // tm, N // tn, nk),
        in_specs=in_specs, out_specs=pl.BlockSpec((tm, tn), lambda i, j, k: (i, j)),
        out_shape=jax.ShapeDtypeStruct((M, N), out_dtype),
        scratch_shapes=[pltpu.VMEM((tm, tn), F32)],
        compiler_params=_cparams(("parallel", "parallel", "arbitrary"), blk),
    )(*args)


def matmul_cs(a, w, mode, name, *, scale=1.0, out_dtype=F32):
    n_blk = N_SHARDS
    if mode == "tn":
        (K, R), Cs = a.shape, w.shape[2] // 2
        tm, tk = _pick(R, MM_TILE_M), _pick(K, 1024)
        grid = (R // tm, n_blk, K // tk)
        a_spec = pl.BlockSpec((tk, tm), lambda i, j, k: (k, i))
        w_spec = pl.BlockSpec((None, tk, Cs), lambda i, j, k: (j // 2, k, j % 2))
        o_spec = pl.BlockSpec((None, tm, Cs), lambda i, j, k: (j, i, 0))
        out_shape, acc_shape, dims = (n_blk, R, Cs), (tm, Cs), (0, 0)
        blk = tk * tm * a.dtype.itemsize + tk * Cs * w.dtype.itemsize + tm * Cs * 8
    else:
        M, (_, R, Cs) = a.shape[1], w.shape
        tm, tn = _pick(M, MM_TILE_M), _pick(R, MM_TILE_N)
        grid = (M // tm, R // tn, n_blk)
        a_spec = pl.BlockSpec((None, tm, Cs), lambda i, j, k: (k // 2, i, k % 2))
        w_spec = pl.BlockSpec((None, tn, Cs), lambda i, j, k: (k, j, 0))
        o_spec = pl.BlockSpec((tm, tn), lambda i, j, k: (i, j))
        out_shape, acc_shape, dims = (M, R), (tm, tn), (1, 1)
        blk = tm * Cs * a.dtype.itemsize + tn * Cs * w.dtype.itemsize + tm * tn * 8
    nk = grid[2]

    def body(a_ref, w_ref, o_ref, acc_ref):
        k = pl.program_id(2)

        @pl.when(k == 0)
        def _():
            acc_ref[...] = jnp.zeros_like(acc_ref)

        acc_ref[...] += lax.dot_general(a_ref[...].astype(BF16), w_ref[...].astype(BF16),
                                        (((dims[0],), (dims[1],)), ((), ())), preferred_element_type=F32)

        @pl.when(k == nk - 1)
        def _():
            o_ref[...] = (acc_ref[...] * scale).astype(o_ref.dtype)

    return pl.pallas_call(
        body, name=name, grid=grid, in_specs=[a_spec, w_spec], out_specs=o_spec,
        out_shape=jax.ShapeDtypeStruct(out_shape, out_dtype), scratch_shapes=[pltpu.VMEM(acc_shape, F32)],
        compiler_params=_cparams(("parallel", "parallel", "arbitrary"), blk),
    )(a, w)


FFN_TILE_M = 512


def _swiglu(gate, up):
    return gate * jax.nn.sigmoid(gate) * up


def ffn_in_act(h, w, name):
    (M, R), Cs, half = h.shape, w.shape[2], N_SHARDS // 2
    tm, tk = _pick(M, FFN_TILE_M), _pick(R, 1024)
    nk = R // tk

    def body(h_ref, wg_ref, wu_ref, gu_ref, act_ref, acc_ref):
        k = pl.program_id(2)

        @pl.when(k == 0)
        def _():
            acc_ref[...] = jnp.zeros_like(acc_ref)

        hb = h_ref[...].astype(BF16)
        for part, w_ref in enumerate((wg_ref, wu_ref)):
            acc_ref[part] += jnp.dot(hb, w_ref[...].astype(BF16), preferred_element_type=F32)

        @pl.when(k == nk - 1)
        def _():
            gu_ref[...] = acc_ref[...]
            act_ref[...] = _swiglu(acc_ref[0], acc_ref[1]).astype(act_ref.dtype)

    w_spec = lambda off: pl.BlockSpec((None, tk, Cs), functools.partial(lambda off, j, i, k: (j + off, k, 0), off))
    blk = tm * tk * h.dtype.itemsize + 2 * tk * Cs * w.dtype.itemsize + tm * Cs * (16 + 2)
    return pl.pallas_call(
        body, name=name, grid=(half, M // tm, nk),
        in_specs=[pl.BlockSpec((tm, tk), lambda j, i, k: (i, k)), w_spec(0), w_spec(half)],
        out_specs=[pl.BlockSpec((2, tm, Cs), lambda j, i, k: (0, i, j)), pl.BlockSpec((tm, Cs), lambda j, i, k: (i, j))],
        out_shape=[jax.ShapeDtypeStruct((2, M, half * Cs), F32), jax.ShapeDtypeStruct((M, half * Cs), BF16)],
        scratch_shapes=[pltpu.VMEM((2, tm, Cs), F32)],
        compiler_params=_cparams(("parallel", "parallel", "arbitrary"), blk),
    )(h, w, w)


def ffn_dact_dgu(dy, w_out, gu, scale, name):
    (M, D), F = dy.shape, w_out.shape[0]
    tm, tn = _pick(M, FFN_TILE_M), F // 2

    def body(dy_ref, w_ref, gu_ref, dgu_ref):
        dact = scale * lax.dot_general(dy_ref[...].astype(BF16), w_ref[...].astype(BF16),
                                       (((1,), (1,)), ((), ())), preferred_element_type=F32)
        dgate, dup = jax.vjp(_swiglu, gu_ref[0], gu_ref[1])[1](dact)
        dgu_ref[0] = dgate.astype(dgu_ref.dtype)
        dgu_ref[1] = dup.astype(dgu_ref.dtype)

    pair = pl.BlockSpec((2, tm, tn), lambda j, i: (0, i, j))
    blk = tm * D * dy.dtype.itemsize + tn * D * w_out.dtype.itemsize + 2 * tm * tn * (4 + 2)
    return pl.pallas_call(
        body, name=name, grid=(F // tn, M // tm),
        in_specs=[pl.BlockSpec((tm, D), lambda j, i: (i, 0)), pl.BlockSpec((tn, D), lambda j, i: (j, 0)), pair],
        out_specs=pair, out_shape=jax.ShapeDtypeStruct((2, M, F), BF16),
        compiler_params=_cparams(("parallel", "parallel"), blk),
    )(dy, w_out, gu)


def _row_block(n, width, n_arrays):
    cap = (V7X_VMEM_BYTES // 4) // (2 * 4 * width * n_arrays)
    best = None
    for t in range(16, min(n, cap) + 1, 16):
        if n % t == 0:
            best = t
    return best or n


def placed_map(f, ins, out, *, n_blocks, tb, name):
    def body(*refs):
        refs[-1][...] = f(*[r[...] for r in refs[:-1]]).astype(refs[-1].dtype)

    def spec(fn):
        def index(i):
            x, y, c = _place()
            return fn(i, (c, 2 * x + y)), 0
        return pl.BlockSpec((tb, width), index)

    o_rows, width, o_dtype, o_fn = out
    blk = (sum(a.dtype.itemsize for a, _ in ins) + jnp.dtype(o_dtype).itemsize) * tb * width
    return pl.pallas_call(
        body, name=name, grid=(n_blocks,), in_specs=[spec(fn) for _, fn in ins], out_specs=spec(o_fn),
        out_shape=jax.ShapeDtypeStruct((o_rows, width), o_dtype),
        compiler_params=_cparams(("parallel",), blk),
    )(*[a for a, _ in ins])


def rowmap(f, rows, params, outs, accs=(), *, tb, name):
    rows = [r if isinstance(r, tuple) else (r, r.shape[1], 0) for r in rows]
    S = rows[0][0].shape[0]
    assert S % tb == 0, (name, S, tb)
    n_in, n_out = len(rows) + len(params), len(outs)

    def body(*refs):
        res = f(*[r[...] for r in refs[:n_in]])
        res = res if isinstance(res, (tuple, list)) else (res,)
        o_refs, a_refs = refs[n_in:n_in + n_out], refs[n_in + n_out:]
        for ref, val in zip(o_refs, res[:n_out]):
            ref[...] = val.astype(ref.dtype)
        if a_refs:
            @pl.when(pl.program_id(0) == 0)
            def _():
                for ref in a_refs:
                    ref[...] = jnp.zeros_like(ref)

            for ref, val in zip(a_refs, res[n_out:]):
                ref[...] += val.astype(F32)

    in_specs = [pl.BlockSpec((tb, w), functools.partial(lambda cb, i: (i, cb), cb)) for _, w, cb in rows]
    in_specs += [pl.BlockSpec(p.shape, lambda i: (0, 0)) for p in params]
    out_specs = [pl.BlockSpec((tb, w), lambda i: (i, 0)) for w, _ in outs]
    out_specs += [pl.BlockSpec(tuple(s), lambda i: (0, 0)) for s in accs]
    out_shape = [jax.ShapeDtypeStruct((S, w), dt) for w, dt in outs]
    out_shape += [jax.ShapeDtypeStruct(tuple(s), F32) for s in accs]
    blk = sum(tb * w * a.dtype.itemsize for a, w, _ in rows) + sum(_nbytes(p.shape, p.dtype) for p in params)
    blk += sum(_nbytes((tb, w), dt) for w, dt in outs) + sum(_nbytes(s, F32) for s in accs)
    res = pl.pallas_call(
        body, name=name, grid=(S // tb,), in_specs=in_specs, out_specs=out_specs, out_shape=out_shape,
        compiler_params=_cparams(("arbitrary",) if accs else ("parallel",), blk),
    )(*[r[0] for r in rows], *[pltpu.with_memory_space_constraint(p, pltpu.HBM) for p in params])
    return res


def _rms(x, g):
    return x * lax.rsqrt(jnp.mean(x * x, axis=-1, keepdims=True) + RMS_EPS) * g


def _softplus(z):
    return jnp.maximum(z, 0.0) + jnp.log(1.0 + jnp.exp(-jnp.abs(z)))


def _rwkv_pre(xrk, xlo, w0, w2p, a0, a2p, g2p, k_k, k_a, seg, seg_t):
    k = xrk[:, D_MODEL:2 * D_MODEL]
    w = -_softplus(-(w0 + NN(jnp.tanh(xlo), w2p))) - 0.5
    log_decay = -jnp.exp(w)
    a = jax.nn.sigmoid(a0 + NN(xlo, a2p))
    g = NN(jax.nn.sigmoid(xlo), g2p)
    kk = k * k_k
    norm = jnp.maximum(jnp.sqrt(SEG(kk * kk, seg)), 1e-12)
    kk = kk * SEG(1.0 / norm, seg_t)
    k_mod = k * (1.0 + (a - 1.0) * k_a)
    return log_decay, k_mod, -kk, kk * a, g


def _rwkv_post(wkv, r, k_mod, v, g, r_k, ln_w, ln_b, seg, seg_t):
    inv_n = 1.0 / HEAD_DIM
    mean = SEG(wkv, seg) * inv_n
    cen = wkv - SEG(mean, seg_t)
    var = SEG(cen * cen, seg) * inv_n
    y = cen * SEG(lax.rsqrt(var + GN_EPS), seg_t) * ln_w + ln_b
    bonus = SEG(SEG(r * k_mod * r_k, seg), seg_t) * v
    return (y + bonus) * g


def _qk_norm(q, k, q_gain, k_gain, seg, seg_t, tile_t):
    def norm(x, gain):
        mean_sq = SEG(x * x, seg) * (1.0 / HEAD_DIM)
        return x * SEG(lax.rsqrt(mean_sq + RMS_EPS), seg_t) * SEG(gain, tile_t)

    return norm(q, q_gain) * (HEAD_DIM ** -0.5), norm(k, k_gain)


def _gate_merge(pgate, pa, pb, b_gate):
    sg = jax.nn.sigmoid(pgate + b_gate)
    return sg[:, :D_MODEL] * pa + sg[:, D_MODEL:] * pb


def _group_combine(o0, o1, o2, l0, l1, l2):
    m = jnp.maximum(jnp.maximum(l0, l1), l2)
    es = [jnp.exp(l - m) for l in (l0, l1, l2)]
    den = es[0] + es[1] + es[2]
    return jnp.concatenate([o * (e / den) for o, e in zip((o0, o1, o2), es)], axis=1)


def _each(f, *xs):
    return tuple(f(*args) for args in zip(*xs))


def _attn_block(q, kc, kp, vc, vp, first):
    qi = lax.broadcasted_iota(jnp.int32, (ATTN_BLK, ATTN_BLK), 0)
    kj = lax.broadcasted_iota(jnp.int32, (ATTN_BLK, ATTN_BLK), 1)
    own = kj <= qi
    s_c = _each(lambda a, b: jnp.where(own, NT(a, b), NEG_INF), q, kc)
    s_p = _each(lambda a, b, f: jnp.where((kj >= qi) & (f < 0.5), NT(a, b), NEG_INF), q, kp, first)
    row_max = lambda s: jnp.max(s, axis=-1, keepdims=True)
    row_sum = lambda s: jnp.sum(s, axis=-1, keepdims=True)
    m = _each(lambda c_, p_: jnp.maximum(row_max(c_), row_max(p_)), s_c, s_p)
    e_c, e_p = _each(lambda s, m_: jnp.exp(s - m_), s_c, m), _each(lambda s, m_: jnp.exp(s - m_), s_p, m)
    den = _each(lambda c_, p_: row_sum(c_) + row_sum(p_), e_c, e_p)
    inv = _each(lambda d_: 1.0 / d_, den)
    o = _each(lambda ec, ep, i_, vc_, vp_: (NN(ec, vc_) + NN(ep, vp_)) * i_, e_c, e_p, inv, vc, vp)
    lse = _each(lambda m_, d_: jnp.broadcast_to(m_ + jnp.log(d_), (ATTN_BLK, HEAD_DIM)), m, den)
    return o, lse


def _attn_pair(q, k, k_before, v, v_before, first):
    n = len(q[0])
    o, lse = _attn_block(q[0] + q[1], k[0] + k[1], k_before + k[0], v[0] + v[1], v_before + v[0],
                         (first[0],) * n + (first[1],) * n)
    return (o[:n], o[n:]), (lse[:n], lse[n:])


TRI_SEED = 8


def _tri_inverse(n):
    c = n[0].shape[0]
    row = lax.broadcasted_iota(jnp.int32, (c, c), 0)
    col = lax.broadcasted_iota(jnp.int32, (c, c), 1)
    same_block = lambda size: (row >> (size.bit_length() - 1)) == (col >> (size.bit_length() - 1))
    seed = same_block(TRI_SEED)
    p = _each(lambda m: jnp.where(seed, m, 0.0), n)
    t, span = _each(lambda m: (row == col).astype(F32) + m, p), 2
    while span < TRI_SEED:
        p = _each(NN, p, p)
        t = _each(lambda t_, p_: t_ + NN(t_, p_), t, p)
        span *= 2
    size = TRI_SEED
    while size < c:
        joins = same_block(2 * size) & jnp.logical_not(same_block(size))
        t = _each(lambda t_, m: t_ + NN(NN(t_, jnp.where(joins, m, 0.0)), t_), t, n)
        size *= 2
    return t


@jax.custom_vjp
def _tri_solve(n, rhs, t):
    return _each(NN, t, rhs)


def _tri_solve_fwd(n, rhs, t):
    x = _each(NN, t, rhs)
    return x, (t, x)


def _tri_solve_bwd(res, dx):
    t, x = res
    drhs = _each(TN, t, dx)
    return _each(NT, drhs, x), drhs, _each(jnp.zeros_like, t)


_tri_solve.defvjp(_tri_solve_fwd, _tri_solve_bwd)


def _lower_ones(c):
    row = lax.broadcasted_iota(jnp.int32, (c, c), 0)
    col = lax.broadcasted_iota(jnp.int32, (c, c), 1)
    return (row >= col).astype(BF16)


def _ones_dot(ones, x, contract):
    hi, lo = _split_bf16(x)
    dims = (((contract,), (0,)), ((), ()))
    return (lax.dot_general(ones, hi, dims, preferred_element_type=F32)
            + lax.dot_general(ones, lo, dims, preferred_element_type=F32))


@jax.custom_vjp
def _cumsum_rows(x):
    return _ones_dot(_lower_ones(x.shape[0]), x, 1)


_cumsum_rows.defvjp(lambda x: (_ones_dot(_lower_ones(x.shape[0]), x, 1), None),
                    lambda _, g: (_ones_dot(_lower_ones(g.shape[0]), g, 0),))


def _wkv_chunk(s0, r, lw, k, v, a, b, t_inv=None):
    c = r[0].shape[0]
    row = lax.broadcasted_iota(jnp.int32, (c, c), 0)
    col = lax.broadcasted_iota(jnp.int32, (c, c), 1)
    strict, incl = row > col, row >= col
    cat = lambda p, q: jnp.concatenate([p, q], axis=0)
    cum = _each(_cumsum_rows, lw)
    e_neg = _each(lambda c_: jnp.exp(-c_), cum)
    ar = _each(lambda a_, r_, c_, l_: cat(a_ * jnp.exp(c_ - l_), r_ * jnp.exp(c_)), a, r, cum, lw)
    b_t, k_t = _each(jnp.multiply, b, e_neg), _each(jnp.multiply, k, e_neg)
    p_b, p_k, p_s = _each(NT, ar, b_t), _each(NT, ar, k_t), _each(NT, ar, s0)
    n_ab = _each(lambda p: jnp.where(strict, p[:c], 0.0), p_b)
    m_rb = _each(lambda p: jnp.where(incl, p[c:], 0.0), p_b)
    n_ak = _each(lambda p: jnp.where(strict, p[:c], 0.0), p_k)
    m_rk = _each(lambda p: jnp.where(incl, p[c:], 0.0), p_k)
    if t_inv is None:
        t_inv = _tri_inverse(n_ab)
    u = _tri_solve(n_ab, _each(lambda p, n_, v_: p[:c] + NN(n_, v_), p_s, n_ak, v), t_inv)
    y = _each(lambda p, mb, u_, mk, v_: p[c:] + NN(mb, u_) + NN(mk, v_), p_s, m_rb, u, m_rk, v)
    g_end = _each(lambda l_: jnp.exp(jnp.sum(l_, axis=0, keepdims=True)), lw)
    s1 = _each(lambda s_, g_, u_, v_, b_, k_: s_ * g_ + TN(cat(u_, v_), cat(b_, k_) * g_),
               s0, g_end, u, v, b_t, k_t)
    return y, s1, t_inv


def _adamw(w, g, m, v):
    m = ADAM_B1 * m + (1.0 - ADAM_B1) * g
    v = ADAM_B2 * v + (1.0 - ADAM_B2) * jnp.square(g)
    m_hat = m / (1.0 - ADAM_B1 ** ADAM_STEP)
    v_hat = v / (1.0 - ADAM_B2 ** ADAM_STEP)
    delta = -ADAM_LR * (m_hat / (jnp.sqrt(v_hat) + ADAM_EPS) + ADAM_WD * w)
    return delta, m, v


def token_shift_fwd(p, mu, *, tb, name):
    S, W = p.shape
    hb = tb // 8

    def body(p_ref, halo_ref, mu_ref, o_ref):
        i = pl.program_id(0)
        x = p_ref[...]
        before = halo_ref[7:8, :] * (i > 0).astype(F32)
        row = lax.broadcasted_iota(jnp.int32, (tb, W), 0)
        prev = jnp.where(row == 0, before, pltpu.roll(x, 1, 0))
        o_ref[...] = x + (prev - x) * mu_ref[...]

    blk = (2 * tb + 8) * W * 4
    return pl.pallas_call(
        body, name=name, grid=(S // tb,),
        in_specs=[pl.BlockSpec((tb, W), lambda i: (i, 0)),
                  pl.BlockSpec((8, W), lambda i: (jnp.maximum(i * hb - 1, 0), 0)),
                  pl.BlockSpec((1, W), lambda i: (0, 0))],
        out_specs=pl.BlockSpec((tb, W), lambda i: (i, 0)),
        out_shape=jax.ShapeDtypeStruct((S, W), F32),
        compiler_params=_cparams(("parallel",), blk),
    )(p, p, mu)


def token_shift_bwd(dxs, p, mu, *, tb, name):
    S, W = p.shape
    hb, nb = tb // 8, S // tb

    def body(d_ref, dnext_ref, p_ref, halo_ref, mu_ref, dp_ref, dmu_ref):
        i = pl.program_id(0)
        d, x, mu_v = d_ref[...], p_ref[...], mu_ref[...]
        row = lax.broadcasted_iota(jnp.int32, (tb, W), 0)
        before = halo_ref[7:8, :] * (i > 0).astype(F32)
        prev = jnp.where(row == 0, before, pltpu.roll(x, 1, 0))
        t = d * mu_v
        after = dnext_ref[0:1, :] * mu_v * (i < nb - 1).astype(F32)
        nxt = jnp.where(row == tb - 1, after, pltpu.roll(t, tb - 1, 0))
        dp_ref[...] = (d - t + nxt).astype(dp_ref.dtype)

        @pl.when(i == 0)
        def _():
            dmu_ref[...] = jnp.zeros_like(dmu_ref)

        dmu_ref[...] += jnp.sum(d * (prev - x), axis=0, keepdims=True)

    blk = (3 * tb + 16) * W * 4
    return pl.pallas_call(
        body, name=name, grid=(nb,),
        in_specs=[pl.BlockSpec((tb, W), lambda i: (i, 0)),
                  pl.BlockSpec((8, W), lambda i: (jnp.minimum((i + 1) * hb, S // 8 - 1), 0)),
                  pl.BlockSpec((tb, W), lambda i: (i, 0)),
                  pl.BlockSpec((8, W), lambda i: (jnp.maximum(i * hb - 1, 0), 0)),
                  pl.BlockSpec((1, W), lambda i: (0, 0))],
        out_specs=[pl.BlockSpec((tb, W), lambda i: (i, 0)), pl.BlockSpec((1, W), lambda i: (0, 0))],
        out_shape=[jax.ShapeDtypeStruct((S, W), BF16), jax.ShapeDtypeStruct((1, W), F32)],
        compiler_params=_cparams(("arbitrary",), blk),
    )(dxs, dxs, p, p, mu)


def _head_cols(h):
    return pl.ds(h * HEAD_DIM, HEAD_DIM)


def wkv_fwd(xs_rk, lw, k, a, b):
    S = lw.shape[0]
    C, nc, G, N = WKV_CHUNK, S // WKV_CHUNK, WKV_HEADS_PER_STEP, HEAD_DIM

    def body(r_ref, lw_ref, k_ref, v_ref, a_ref, b_ref, y_ref, st_ref, ti_ref, state):
        @pl.when(pl.program_id(1) == 0)
        def _():
            state[...] = jnp.zeros_like(state)

        heads = lambda ref: tuple(ref[:, _head_cols(h)] for h in range(G))
        s0 = tuple(state[h] for h in range(G))
        y, s1, t_inv = _wkv_chunk(s0, heads(r_ref), heads(lw_ref), heads(k_ref), heads(v_ref), heads(a_ref),
                                  heads(b_ref))
        for h in range(G):
            st_ref[h] = s0[h]
            ti_ref[h] = t_inv[h]
            y_ref[:, _head_cols(h)] = y[h]
            state[h] = s1[h]

    W = G * N
    seq = lambda j: pl.BlockSpec((C, W), functools.partial(lambda j, g, c: (c, j + g), j))
    per = D_MODEL // W
    per_chunk = pl.BlockSpec((None, G, N, N), lambda g, c: (c, g, 0, 0))
    return pl.pallas_call(
        body, name="wkv_fwd", grid=(RWKV_HEADS // G, nc),
        in_specs=[seq(0), seq(0), seq(0), seq(2 * per), seq(0), seq(0)],
        out_specs=[seq(0), per_chunk, per_chunk],
        out_shape=[jax.ShapeDtypeStruct((S, D_MODEL), F32)] + [jax.ShapeDtypeStruct((nc, RWKV_HEADS, N, N), F32)] * 2,
        scratch_shapes=[pltpu.VMEM((G, N, N), F32)],
        compiler_params=_cparams(("parallel", "arbitrary"), 8 * C * W * 4 + 3 * G * N * N * 4),
    )(xs_rk, lw, k, xs_rk, a, b)


def wkv_bwd(xs_rk, lw, k, a, b, states, t_invs, dy):
    S = lw.shape[0]
    C, nc, G, N = WKV_CHUNK, S // WKV_CHUNK, WKV_HEADS_PER_STEP, HEAD_DIM

    def body(r_ref, lw_ref, k_ref, v_ref, a_ref, b_ref, st_ref, ti_ref, dy_ref,
             dr_ref, dlw_ref, dk_ref, dv_ref, da_ref, db_ref, dstate):
        @pl.when(pl.program_id(1) == 0)
        def _():
            dstate[...] = jnp.zeros_like(dstate)

        heads = lambda ref: tuple(ref[:, _head_cols(h)] for h in range(G))
        t_inv = tuple(ti_ref[h] for h in range(G))
        chunk = lambda *args: _wkv_chunk(*args, t_inv)[:2]
        _, pull = jax.vjp(chunk, tuple(st_ref[h] for h in range(G)), heads(r_ref), heads(lw_ref),
                          heads(k_ref), heads(v_ref), heads(a_ref), heads(b_ref))
        ds0, *grads = pull((heads(dy_ref), tuple(dstate[h] for h in range(G))))
        for h in range(G):
            dstate[h] = ds0[h]
            for ref, grad in zip((dr_ref, dlw_ref, dk_ref, dv_ref, da_ref, db_ref), grads):
                ref[:, _head_cols(h)] = grad[h]

    W = G * N
    seq = lambda j: pl.BlockSpec((C, W), functools.partial(lambda j, g, c: (nc - 1 - c, j + g), j))
    per = D_MODEL // W
    st = pl.BlockSpec((None, G, N, N), lambda g, c: (nc - 1 - c, g, 0, 0))
    return pl.pallas_call(
        body, name="wkv_bwd", grid=(RWKV_HEADS // G, nc),
        in_specs=[seq(0), seq(0), seq(0), seq(2 * per), seq(0), seq(0), st, st, seq(0)],
        out_specs=[seq(0)] * 6, out_shape=[jax.ShapeDtypeStruct((S, D_MODEL), F32)] * 6,
        scratch_shapes=[pltpu.VMEM((G, N, N), F32)],
        compiler_params=_cparams(("parallel", "arbitrary"), 14 * C * W * 4 + 3 * G * N * N * 4),
    )(xs_rk, lw, k, xs_rk, a, b, states, t_invs, dy)


def _first_flag(i, per_seq):
    return (lax.rem(i, per_seq) == 0).astype(F32)


def _view(a):
    return a if isinstance(a, tuple) else (a, 0)


def _block_rows(half):
    return pl.ds(half * ATTN_BLK, ATTN_BLK)


def _block_heads(ref, half):
    return tuple(ref[_block_rows(half), _head_cols(h)] for h in range(ATTN_HPG))


def _pair_heads(ref):
    return _block_heads(ref, 0), _block_heads(ref, 1)


def attn_fwd(q, k, v, per_seq, name):
    (q, q_col), (k, k_col), (v, v_col) = _view(q), _view(k), _view(v)
    R, N = q.shape[0], GROUP_W
    n_pairs = R // (2 * ATTN_BLK)

    def body(q_ref, k_ref, kb_ref, v_ref, vb_ref, o_ref, lse_ref):
        pair = pl.program_id(0)
        first = (_first_flag(2 * pair, per_seq), _first_flag(2 * pair + 1, per_seq))
        o, lse = _attn_pair(_pair_heads(q_ref), _pair_heads(k_ref), _block_heads(kb_ref, 0), _pair_heads(v_ref),
                            _block_heads(vb_ref, 0), first)
        for half in range(2):
            for h in range(ATTN_HPG):
                o_ref[_block_rows(half), _head_cols(h)] = o[half][h]
                lse_ref[_block_rows(half), _head_cols(h)] = lse[half][h]

    cur = lambda col: pl.BlockSpec((2 * ATTN_BLK, N), lambda i: (i, col))
    prv = lambda col: pl.BlockSpec((ATTN_BLK, N), lambda i: (jnp.maximum(2 * i - 1, 0), col))
    return pl.pallas_call(
        body, name=name, grid=(n_pairs,), in_specs=[cur(q_col), cur(k_col), prv(k_col), cur(v_col), prv(v_col)],
        out_specs=[cur(0), cur(0)], out_shape=[jax.ShapeDtypeStruct((R, N), F32)] * 2,
        compiler_params=_cparams(("parallel",), 12 * ATTN_BLK * N * 4),
    )(q, k, k, v, v)


def attn_bwd(q, k, v, do, dlse, per_seq, name):
    views = [_view(a) for a in (q, k, v, do, dlse)]
    (q, q_col), (k, k_col), (v, v_col), (do, do_col), (dlse, dl_col) = views
    R, N = q.shape[0], GROUP_W
    n_pairs = R // (2 * ATTN_BLK)

    def body(q_ref, k_ref, kb_ref, v_ref, vb_ref, do_ref, dl_ref, dq_ref, dk_ref, dv_ref, carry_k, carry_v):
        step = pl.program_id(0)
        pair = n_pairs - 1 - step
        first = (_first_flag(2 * pair, per_seq), _first_flag(2 * pair + 1, per_seq))

        @pl.when(step == 0)
        def _():
            carry_k[...] = jnp.zeros_like(carry_k)
            carry_v[...] = jnp.zeros_like(carry_v)

        _, pull = jax.vjp(functools.partial(_attn_pair, first=first), _pair_heads(q_ref), _pair_heads(k_ref),
                          _block_heads(kb_ref, 0), _pair_heads(v_ref), _block_heads(vb_ref, 0))
        dq, dk, dk_before, dv, dv_before = pull((_pair_heads(do_ref), _pair_heads(dl_ref)))
        old_k, old_v = _block_heads(carry_k, 0), _block_heads(carry_v, 0)
        for h in range(ATTN_HPG):
            cols = _head_cols(h)
            for half in range(2):
                dq_ref[_block_rows(half), cols] = dq[half][h]
            dk_ref[_block_rows(0), cols] = dk[0][h]
            dv_ref[_block_rows(0), cols] = dv[0][h]
            dk_ref[_block_rows(1), cols] = dk[1][h] + old_k[h]
            dv_ref[_block_rows(1), cols] = dv[1][h] + old_v[h]
            carry_k[:, cols] = dk_before[h]
            carry_v[:, cols] = dv_before[h]

    cur = lambda col: pl.BlockSpec((2 * ATTN_BLK, N), lambda i: (n_pairs - 1 - i, col))
    prv = lambda col: pl.BlockSpec((ATTN_BLK, N), lambda i: (jnp.maximum(2 * (n_pairs - 1 - i) - 1, 0), col))
    return pl.pallas_call(
        body, name=name, grid=(n_pairs,),
        in_specs=[cur(q_col), cur(k_col), prv(k_col), cur(v_col), prv(v_col), cur(do_col), cur(dl_col)],
        out_specs=[cur(0)] * 3, out_shape=[jax.ShapeDtypeStruct((R, N), F32)] * 3,
        scratch_shapes=[pltpu.VMEM((ATTN_BLK, N), F32)] * 2,
        compiler_params=_cparams(("arbitrary",), 22 * ATTN_BLK * N * 4),
    )(q, k, k, v, v, do, dlse)


def by_residue(u, d):
    if d == 1:
        return u
    return u.reshape(u.shape[0] // d, d, GROUP_W).transpose(1, 0, 2).reshape(u.shape)


def by_position(u, d):
    if d == 1:
        return u
    return u.reshape(d, u.shape[0] // d, GROUP_W).transpose(1, 0, 2).reshape(u.shape)


def group_columns(t, col_block, d):
    if d == 1:
        return (t, col_block)
    return by_residue(t[:, GROUP_W * col_block:GROUP_W * (col_block + 1)], d)


def _ffn_fwd(x, norm, w_in, w_out, tag, token):
    h = rowmap(lambda x_b, g, tok: _rms(x_b, g) + tok[0:1, 0:1], [x], [norm, token], [(D_MODEL, BF16)], tb=512,
               name=tag + "_norm")[0]
    gu, act = ffn_in_act(h, w_in, tag + "_in")
    y = matmul(act, w_out, "nn", tag + "_out", add=x, scale=0.5)
    return y, (x, h, gu, act)


def _ffn_bwd(dy, saved, norm, w_in, w_out, tag):
    x, h, gu, act = saved
    dw_out = matmul(act, dy, "tn", tag + "_dwout", scale=0.5)
    dgu = ffn_dact_dgu(dy, w_out, gu, 0.5, tag + "_dgu")
    dh = matmul_cs(dgu, w_in, "nt", tag + "_dh")
    dw_in = matmul_cs(h, dgu, "tn", tag + "_dwin")

    def norm_bwd(x_b, dh_b, dy_b, g):
        dx, dg = jax.vjp(_rms, x_b, g)[1](dh_b)
        return dy_b + dx, dg

    dx, dnorm = rowmap(norm_bwd, [x, dh, dy], [norm], [(D_MODEL, F32)], [(1, D_MODEL)], tb=256,
                       name=tag + "_dnorm")
    return dx, dnorm, dw_in, dw_out


def layer_step(x, tgt, W, P, start_token, more_weights, on_mixer_grads):
    S = x.shape[0]
    x1, ffn1_saved = _ffn_fwd(x, P["ffn1_norm"], W["ffn1_w_in"], W["ffn1_w_out"], "ffn1", start_token)
    W = {**W, **more_weights("mixer", x1)}
    head_of = lambda n: jnp.arange(n)[:, None] // HEAD_DIM == jnp.arange(n // HEAD_DIM)[None, :]
    seg, seg_a = head_of(D_MODEL).astype(BF16), head_of(ATTN_WIDTH).astype(BF16)
    seg_t, seg_a_t = seg.T, seg_a.T
    tile_t = (jnp.arange(HEAD_DIM)[:, None] == jnp.arange(ATTN_WIDTH)[None, :] % HEAD_DIM).astype(BF16)
    qk_params = [P["attn_q_norm"], P["attn_k_norm"], seg_a, seg_a_t, tile_t]
    w_rkv, w_lora = W["w_in"][:, :RKV], W["w_in"][:, RKV:RKV + LORA]
    w_qkv = W["w_in"][:, RKV + LORA:RKV + LORA + 3 * ATTN_WIDTH]
    w_gate = W["w_in"][:, RKV + LORA + 3 * ATTN_WIDTH:]
    mu_rk, mu_lo = P["rwkv_mu"][:, :RKV], P["rwkv_mu"][:, RKV:]
    zeros = lambda n: jnp.zeros((n, D_MODEL), F32)
    w2p = jnp.concatenate([W["rwkv_w2"], zeros(LORA - LORA_W)], axis=0)
    a2p = jnp.concatenate([zeros(LORA_W), W["rwkv_a2"], zeros(LORA_G)], axis=0)
    g2p = jnp.concatenate([zeros(LORA_W + LORA_A), W["rwkv_g2"]], axis=0)
    pre_params = [P["rwkv_w0"], w2p, P["rwkv_a0"], a2p, g2p, P["rwkv_k_k"], P["rwkv_k_a"], seg, seg_t]
    post_params = [P["rwkv_r_k"], P["rwkv_ln_w"], P["rwkv_ln_b"], seg, seg_t]
    col = lambda arr, j: (arr, D_MODEL, j)

    h = rowmap(_rms, [x1], [P["mix_norm"]], [(D_MODEL, BF16)], tb=512, name="mix_norm")[0]
    p_rk = matmul(h, w_rkv, "nn", "proj_rkv")
    p_lo = matmul(h, w_lora, "nn", "proj_lora")
    p_qkv = matmul(h, w_qkv, "nn", "proj_qkv")
    p_gate = matmul(h, w_gate, "nn", "proj_gate")
    xs_rk = token_shift_fwd(p_rk, mu_rk, tb=256, name="shift_rk")
    xs_lo = token_shift_fwd(p_lo, mu_lo, tb=256, name="shift_lora")
    lw, k_mod, a_neg, b_kk, g = rowmap(
        _rwkv_pre, [xs_rk, xs_lo], pre_params, [(D_MODEL, F32)] * 5, tb=256, name="rwkv_pre")
    wkv, states, t_invs = wkv_fwd(xs_rk, lw, k_mod, a_neg, b_kk)
    post_rows = [wkv, col(xs_rk, 0), k_mod, col(xs_rk, 2), g]
    y_a = rowmap(_rwkv_post, post_rows, post_params, [(D_MODEL, BF16)], tb=256, name="rwkv_post")[0]

    qk_rows = [(p_qkv, ATTN_WIDTH, 0), (p_qkv, ATTN_WIDTH, 1)]
    qn, kn = rowmap(_qk_norm, qk_rows, qk_params, [(ATTN_WIDTH, F32)] * 2, tb=256, name="qk_norm")
    dil = [d for _, d in ATTN_PAIRS]
    groups = range(len(dil))
    per_seq = [S // d // ATTN_BLK for d in dil]
    v_first = 2 * ATTN_WIDTH // GROUP_W
    q_s = [group_columns(qn, g, dil[g]) for g in groups]
    k_s = [group_columns(kn, g, dil[g]) for g in groups]
    v_s = [group_columns(p_qkv, v_first + g, dil[g]) for g in groups]
    attn = [attn_fwd(q_s[g], k_s[g], v_s[g], per_seq[g], "attn_fwd_%d" % g) for g in groups]
    o_lse = [by_position(attn[g][j], dil[g]) for j in range(2) for g in groups]
    y_b = rowmap(_group_combine, o_lse, [], [(ATTN_WIDTH, BF16)], tb=512, name="attn_combine")[0]

    W = {**W, **more_weights("out", y_b)}
    pa = matmul(y_a, W["w_proj_rwkv"], "nn", "proj_a")
    pb = matmul(y_b, W["w_proj_attn"], "nn", "proj_b")
    merged = rowmap(_gate_merge, [p_gate, pa, pb], [P["b_gate"]], [(D_MODEL, BF16)], tb=256, name="merge")[0]
    x2 = matmul(merged, W["w_out"], "nn", "mix_out", add=x1)
    x3, ffn2_saved = _ffn_fwd(x2, P["ffn2_norm"], W["ffn2_w_in"], W["ffn2_w_out"], "ffn2",
                              jnp.zeros_like(start_token))

    def loss_head(y_b_, t_b):
        err = y_b_ - t_b
        return err * (1.0 / D_MODEL), (0.5 / D_MODEL) * jnp.sum(err * err, axis=0, keepdims=True)

    dx3, loss_cols = rowmap(loss_head, [x3, tgt], [], [(D_MODEL, F32)], [(1, D_MODEL)], tb=512, name="loss")

    gW, gP = {}, {}
    dx2, gP["ffn2_norm"], gW["ffn2_w_in"], gW["ffn2_w_out"] = _ffn_bwd(
        dx3, ffn2_saved, P["ffn2_norm"], W["ffn2_w_in"], W["ffn2_w_out"], "ffn2")

    dmerged = matmul(dx2, W["w_out"], "nt", "d_merged")
    gW["w_out"] = matmul(merged, dx2, "tn", "dw_out")

    def merge_bwd(pg, pa_b, pb_b, dm, bg):
        return jax.vjp(_gate_merge, pg, pa_b, pb_b, bg)[1](dm)

    dp_gate, dpa, dpb, gP["b_gate"] = rowmap(
        merge_bwd, [p_gate, pa, pb, dmerged], [P["b_gate"]],
        [(2 * D_MODEL, BF16), (D_MODEL, BF16), (D_MODEL, BF16)], [(1, 2 * D_MODEL)], tb=256, name="merge_bwd")
    dy_a = matmul(dpa, W["w_proj_rwkv"], "nt", "d_ya")
    gW["w_proj_rwkv"] = matmul(y_a, dpa, "tn", "dw_proj_a")
    dy_b = matmul(dpb, W["w_proj_attn"], "nt", "d_yb")
    gW["w_proj_attn"] = matmul(y_b, dpb, "tn", "dw_proj_b")

    def combine_bwd(*blocks):
        return jax.vjp(_group_combine, *blocks[:-1])[1](blocks[-1])

    d_o_lse = rowmap(combine_bwd, o_lse + [dy_b], [], [(GROUP_W, F32)] * 6, tb=256, name="attn_combine_bwd")
    d_attn = [attn_bwd(q_s[g], k_s[g], v_s[g], by_residue(d_o_lse[g], dil[g]), by_residue(d_o_lse[3 + g], dil[g]),
                       per_seq[g], "attn_bwd_%d" % g) for g in groups]

    def qk_norm_bwd(q_b, k_b, *rest):
        dqkv, (qg, kg, sg, sgt, tl) = rest[:9], rest[9:]
        f = lambda *a: _qk_norm(*a, sg, sgt, tl)
        dqn, dkn = jnp.concatenate(dqkv[0:3], axis=1), jnp.concatenate(dqkv[3:6], axis=1)
        dq, dk, dqg, dkg = jax.vjp(f, q_b, k_b, qg, kg)[1]((dqn, dkn))
        return jnp.concatenate([dq, dk, *dqkv[6:9]], axis=1), dqg, dkg

    dp_qkv, gP["attn_q_norm"], gP["attn_k_norm"] = rowmap(
        qk_norm_bwd, qk_rows + [by_position(d_attn[g][j], dil[g]) for j in range(3) for g in groups], qk_params,
        [(3 * ATTN_WIDTH, BF16)], [(1, HEAD_DIM)] * 2, tb=256, name="qk_norm_bwd")

    def post_bwd(wkv_b, r_b, k_b, v_b, g_b, d_b, r_k, ln_w, ln_b, sg, sgt):
        f = lambda *a: _rwkv_post(*a, sg, sgt)
        return jax.vjp(f, wkv_b, r_b, k_b, v_b, g_b, r_k, ln_w, ln_b)[1](d_b)

    dwkv, dr_p, dk_p, dv_p, dg, gP["rwkv_r_k"], gP["rwkv_ln_w"], gP["rwkv_ln_b"] = rowmap(
        post_bwd, post_rows + [dy_a], post_params, [(D_MODEL, F32)] * 5, [(1, D_MODEL)] * 3, tb=128,
        name="rwkv_post_bwd")
    dr_w, dlw, dk_w, dv_w, da_neg, db_kk = wkv_bwd(xs_rk, lw, k_mod, a_neg, b_kk, states, t_invs, dwkv)

    def pre_bwd(xrk_b, xlo_b, dlw_b, dkw_b, dkp_b, da_b, db_b, dg_b, drp_b, drw_b, dvp_b, dvw_b,
                w0, w2, a0, a2, g2, k_k, k_a, sg, sgt):
        f = lambda *a: _rwkv_pre(*a, sg, sgt)
        pull = jax.vjp(f, xrk_b, xlo_b, w0, w2, a0, a2, g2, k_k, k_a)[1]
        dxrk, dxlo, *dpar = pull((dlw_b, dkw_b + dkp_b, da_b, db_b, dg_b))
        direct = jnp.concatenate([drp_b + drw_b, jnp.zeros_like(drp_b), dvp_b + dvw_b], axis=1)
        return (dxrk + direct, dxlo, *dpar)

    pre_rows = [xs_rk, xs_lo, dlw, dk_w, dk_p, da_neg, db_kk, dg, dr_p, dr_w, dv_p, dv_w]
    dxs_rk, dxs_lo, gP["rwkv_w0"], dw2p, gP["rwkv_a0"], da2p, dg2p, gP["rwkv_k_k"], gP["rwkv_k_a"] = rowmap(
        pre_bwd, pre_rows, pre_params, [(RKV, F32), (LORA, F32)],
        [(1, D_MODEL), (LORA, D_MODEL), (1, D_MODEL), (LORA, D_MODEL), (LORA, D_MODEL), (1, D_MODEL), (1, D_MODEL)],
        tb=128, name="rwkv_pre_bwd")
    gW["rwkv_w2"] = dw2p[:LORA_W]
    gW["rwkv_a2"] = da2p[LORA_W:LORA_W + LORA_A]
    gW["rwkv_g2"] = dg2p[LORA_W + LORA_A:]
    dp_rk, dmu_rk = token_shift_bwd(dxs_rk, p_rk, mu_rk, tb=256, name="shift_rk_bwd")
    dp_lo, dmu_lo = token_shift_bwd(dxs_lo, p_lo, mu_lo, tb=256, name="shift_lora_bwd")
    gP["rwkv_mu"] = jnp.concatenate([dmu_rk, dmu_lo], axis=1)

    dh = matmul(dp_rk, w_rkv, "nt", "dh_rkv")
    dh = matmul(dp_lo, w_lora, "nt", "dh_lora", add=dh)
    dh = matmul(dp_qkv, w_qkv, "nt", "dh_qkv", add=dh)
    dh = matmul(dp_gate, w_gate, "nt", "dh_gate", add=dh)
    gW["w_in"] = jnp.concatenate([
        matmul(h, dp_rk, "tn", "dw_rkv"), matmul(h, dp_lo, "tn", "dw_lora"),
        matmul(h, dp_qkv, "tn", "dw_qkv"), matmul(h, dp_gate, "tn", "dw_gate")], axis=1)

    token = on_mixer_grads(gW)

    def norm_bwd(x_b, dh_b, dy_b, gn, tok):
        dx, dgn = jax.vjp(_rms, x_b, gn)[1](dh_b)
        return dy_b + dx + tok[0:1, 0:1], dgn

    dx1, gP["mix_norm"] = rowmap(norm_bwd, [x1, dh, dx2], [P["mix_norm"], token], [(D_MODEL, F32)],
                                 [(1, D_MODEL)], tb=256, name="mix_norm_bwd")
    dx, gP["ffn1_norm"], gW["ffn1_w_in"], gW["ffn1_w_out"] = _ffn_bwd(
        dx1, ffn1_saved, P["ffn1_norm"], W["ffn1_w_in"], W["ffn1_w_out"], "ffn1")
    return loss_cols, dx, gW, gP


N_SHARDS = 4
BIG = (("ffn1_w_in", (D_MODEL, 2 * D_FF), 1), ("ffn1_w_out", (D_FF, D_MODEL), 0),
       ("w_in", (D_MODEL, 7712), 1), ("rwkv_w2", (LORA_W, D_MODEL), 1), ("rwkv_a2", (LORA_A, D_MODEL), 1),
       ("rwkv_g2", (LORA_G, D_MODEL), 1), ("w_proj_rwkv", (D_MODEL, D_MODEL), 0),
       ("w_proj_attn", (ATTN_WIDTH, D_MODEL), 1), ("w_out", (D_MODEL, D_MODEL), 0),
       ("ffn2_w_in", (D_MODEL, 2 * D_FF), 1), ("ffn2_w_out", (D_FF, D_MODEL), 0))
SMALL = (("ffn1_norm", 1024), ("mix_norm", 1024), ("b_gate", 2048), ("rwkv_mu", 3360), ("rwkv_w0", 1024),
         ("rwkv_a0", 1024), ("rwkv_k_k", 1024), ("rwkv_k_a", 1024), ("rwkv_r_k", 1024), ("rwkv_ln_w", 1024),
         ("rwkv_ln_b", 1024), ("attn_q_norm", 64), ("attn_k_norm", 64), ("ffn2_norm", 1024))
WEIGHT_ORDER = ("ffn1_norm", "ffn1_w_in", "ffn1_w_out", "mix_norm", "w_in", "b_gate", "rwkv_mu", "rwkv_w0",
                "rwkv_w2", "rwkv_a0", "rwkv_a2", "rwkv_g2", "rwkv_k_k", "rwkv_k_a", "rwkv_r_k", "rwkv_ln_w",
                "rwkv_ln_b", "attn_q_norm", "attn_k_norm", "w_proj_rwkv", "w_proj_attn", "w_out", "ffn2_norm",
                "ffn2_w_in", "ffn2_w_out")


LORA_PARTS = ("rwkv_w2", "rwkv_a2", "rwkv_g2")
BLOCK_MAJOR = ("ffn1_w_in", "ffn2_w_in")
FIRST_FFN = ("ffn1_w_in", "ffn1_w_out")
MIXER_IN = ("w_in", "lora")
SMALL_USED = D_MODEL + sum(n for _, n in SMALL)
SMALL_W = -(-SMALL_USED // 128) * 128


def _travel():
    out = {}
    for name, shape, axis in BIG:
        if name == LORA_PARTS[0]:
            out["lora"] = ((LORA, D_MODEL), 1)
        elif name not in LORA_PARTS:
            out[name] = (shape, axis)
    return out


def local_blocks(vals):
    out = {n: vals[n] for n in _travel() if n != "lora"}
    out["lora"] = jnp.concatenate([vals[n] for n in LORA_PARTS], axis=0)
    return out


def split_lora(t):
    return {"rwkv_w2": t[:LORA_W], "rwkv_a2": t[LORA_W:LORA_W + LORA_A], "rwkv_g2": t[LORA_W + LORA_A:]}


def blocks_to_full(name, blocks):
    shape, axis = _travel()[name]
    if name in BLOCK_MAJOR:
        return blocks
    if axis == 0:
        return blocks.reshape(shape)
    return blocks.transpose(1, 0, 2).reshape(shape)


def full_to_blocks(name, full):
    shape, axis = _travel()[name]
    if name in BLOCK_MAJOR:
        return full
    if axis == 0:
        return full.reshape(N_SHARDS, shape[0] // N_SHARDS, shape[1])
    return full.reshape(shape[0], N_SHARDS, shape[1] // N_SHARDS).transpose(1, 0, 2)


def pack_small(vals, head):
    parts = [head] + [vals[name].reshape(1, n) for name, n in SMALL]
    parts.append(jnp.zeros((1, SMALL_W - SMALL_USED), F32))
    return jnp.concatenate(parts, axis=1)


def unpack_small(vec, shapes):
    out, off = {}, D_MODEL
    for name, n in SMALL:
        out[name] = vec[:, off:off + n].reshape(shapes[name])
        off += n
    return out


def _place():
    return lax.axis_index("x"), lax.axis_index("y"), lax.axis_index("c")


def _other_chips(x, y):
    return [(1 - x, y), (x, 1 - y), (1 - x, 1 - y)]


def _remote(src, dst, send_sem, recv_sem, device):
    return pltpu.make_async_remote_copy(src_ref=src, dst_ref=dst, send_sem=send_sem, recv_sem=recv_sem,
                                        device_id=device, device_id_type=MESH)


def _half(ref, who):
    hr = ref.shape[-2] // 2
    rows = pl.ds(pl.multiple_of(who * hr, 8), hr)
    return ref.at[rows] if len(ref.shape) == 2 else ref.at[:, rows]


HBM_REF = pl.BlockSpec(memory_space=pl.ANY)
COMM_PARAMS = dict(compiler_params=pltpu.CompilerParams(has_side_effects=True))


def gather_weights(blocks):
    n = len(blocks)

    def body(*refs):
        ins, outs = refs[:n], refs[n:2 * n]
        ici_send, ici_recv, d2d_send, d2d_recv = refs[2 * n:]
        x, y, c = _place()
        me, sibling, chips = 2 * x + y, (x, y, 1 - c), _other_chips(x, y)
        first = [_remote(_half(ins[t], c), _half(outs[t].at[me], c), ici_send.at[k, t], ici_recv.at[k, t],
                         (px, py, c)) for k, (px, py) in enumerate(chips) for t in range(n)]
        for cp in first:
            cp.start()
        passed = []
        for k, (px, py) in enumerate(chips):
            for t in range(n):
                landed = _half(outs[t].at[2 * px + py], c)
                _remote(landed, landed, ici_send.at[k, t], ici_recv.at[k, t], (px, py, c)).wait_recv()
                cp = _remote(landed, landed, d2d_send.at[k, t], d2d_recv.at[k, t], sibling)
                cp.start()
                passed.append(cp)
        for k, (px, py) in enumerate(chips):
            for t in range(n):
                other = _half(outs[t].at[2 * px + py], 1 - c)
                _remote(other, other, d2d_send.at[k, t], d2d_recv.at[k, t], sibling).wait_recv()
        for cp in first + passed:
            cp.wait_send()

    res = pl.pallas_call(
        body, name="gather_weights", in_specs=[HBM_REF] * n, out_specs=[HBM_REF] * n,
        out_shape=[jax.ShapeDtypeStruct((N_SHARDS,) + b.shape, b.dtype) for b in blocks],
        scratch_shapes=[pltpu.SemaphoreType.DMA((3, n))] * 4, **COMM_PARAMS)(*blocks)
    me = 2 * lax.axis_index("x") + lax.axis_index("y")
    return [lax.dynamic_update_slice(g, b[None], (me, 0, 0)) for g, b in zip(res, blocks)]


def _gather_copies(ins, outs, send_sem, recv_sem):
    x, y, c = _place()
    return [_remote(_half(ins[t], c), _half(outs[t].at[2 * x + y], c), send_sem(k, t), recv_sem(k, t), (px, py, c))
            for k, (px, py) in enumerate(_other_chips(x, y)) for t in range(len(ins))]


def gather_start(blocks, name):
    n = len(blocks)
    n_cp = 3 * n

    def body(*refs):
        ins, outs = refs[:n], refs[n:2 * n]
        sems, token = refs[2 * n:2 * n + 2 * n_cp], refs[-1]
        for cp in _gather_copies(ins, outs, lambda k, t: sems[k * n + t], lambda k, t: sems[n_cp + k * n + t]):
            cp.start()
        token[...] = jnp.zeros_like(token)

    hbm = lambda a: pltpu.with_memory_space_constraint(a, pltpu.HBM)
    landing = [lax.empty((N_SHARDS,) + b.shape, b.dtype) for b in blocks]
    res = pl.pallas_call(
        body, name=name,
        out_shape=(*[pltpu.SemaphoreType.DMA(())] * (2 * n_cp),
                   *[pltpu.HBM(a.shape, a.dtype) for a in list(blocks) + landing], jax.ShapeDtypeStruct((8, 128), F32)),
        in_specs=[SPLIT_HBM] * (2 * n),
        out_specs=(*[SPLIT_SEM] * (2 * n_cp), *[SPLIT_HBM] * (2 * n), pl.BlockSpec(memory_space=pltpu.VMEM)),
        input_output_aliases={t: 2 * n_cp + t for t in range(2 * n)}, **SPLIT_PARAMS,
    )(*[hbm(a) for a in list(blocks) + landing])
    return (n, res[:-1]), res[-1]


def gather_wait(handles, after, name):
    n, held = handles
    n_cp = 3 * n
    sems, thru = held[:2 * n_cp], held[2 * n_cp:]

    def body(*refs):
        ins, outs = refs[:n], refs[n:2 * n]
        sem_refs = refs[2 * n:2 * n + 2 * n_cp]
        for cp in _gather_copies(ins, outs, lambda k, t: sem_refs[k * n + t], lambda k, t: sem_refs[n_cp + k * n + t]):
            cp.wait_send()
            cp.wait_recv()

    res = pl.pallas_call(
        body, name=name, out_shape=tuple(pltpu.HBM(a.shape, a.dtype) for a in thru),
        in_specs=[SPLIT_HBM] * (2 * n) + [SPLIT_SEM] * (2 * n_cp) + [pl.BlockSpec(memory_space=pl.ANY)],
        out_specs=tuple([SPLIT_HBM] * (2 * n)), input_output_aliases={t: t for t in range(2 * n)}, **SPLIT_PARAMS,
    )(*thru, *sems, after)
    return list(res[n:])


def pass_halves(gathered, blocks, name):
    n = len(gathered)

    def body(*refs):
        outs = refs[n:2 * n]
        send_sems, recv_sems = refs[2 * n:]
        x, y, c = _place()
        slots = [2 * px + py for px, py in _other_chips(x, y)]
        give = [_remote(_half(outs[t].at[s], c), _half(outs[t].at[s], c), send_sems.at[k, t], recv_sems.at[k, t],
                        (x, y, 1 - c)) for k, s in enumerate(slots) for t in range(n)]
        for cp in give:
            cp.start()
        for k, s in enumerate(slots):
            for t in range(n):
                other = _half(outs[t].at[s], 1 - c)
                _remote(other, other, send_sems.at[k, t], recv_sems.at[k, t], (x, y, 1 - c)).wait_recv()
        for cp in give:
            cp.wait_send()

    res = pl.pallas_call(
        body, name=name, in_specs=[HBM_REF] * n, out_specs=[HBM_REF] * n,
        out_shape=[jax.ShapeDtypeStruct(g.shape, g.dtype) for g in gathered],
        input_output_aliases={t: t for t in range(n)},
        scratch_shapes=[pltpu.SemaphoreType.DMA((3, n))] * 2, **COMM_PARAMS)(*gathered)
    me = 2 * lax.axis_index("x") + lax.axis_index("y")
    return [lax.dynamic_update_slice(g, b[None], (me, 0, 0)) for g, b in zip(res, blocks)]


def swap_halves(grads):
    n = len(grads)

    def body(*refs):
        ins, got = refs[:n], refs[n:2 * n]
        send_sems, recv_sems = refs[2 * n:]
        x, y, c = _place()
        give = [_remote(_half(ins[t], 1 - c), got[t], send_sems.at[t], recv_sems.at[t], (x, y, 1 - c))
                for t in range(n)]
        for cp in give:
            cp.start()
        for cp in give:
            cp.wait_recv()
        for cp in give:
            cp.wait_send()

    return pl.pallas_call(
        body, name="swap_halves", in_specs=[HBM_REF] * n, out_specs=[HBM_REF] * n,
        out_shape=[jax.ShapeDtypeStruct((g.shape[0], g.shape[1] // 2, g.shape[2]), g.dtype) for g in grads],
        scratch_shapes=[pltpu.SemaphoreType.DMA((n,))] * 2, **COMM_PARAMS)(*grads)


def join_halves(blocks):
    n = len(blocks)

    def body(*refs):
        outs = refs[n:2 * n]
        send_sems, recv_sems = refs[2 * n:]
        x, y, c = _place()
        give = [_remote(_half(outs[t], c), _half(outs[t], c), send_sems.at[t], recv_sems.at[t], (x, y, 1 - c))
                for t in range(n)]
        for cp in give:
            cp.start()
        for t in range(n):
            arriving = _half(outs[t], 1 - c)
            _remote(arriving, arriving, send_sems.at[t], recv_sems.at[t], (x, y, 1 - c)).wait_recv()
        for cp in give:
            cp.wait_send()

    return pl.pallas_call(
        body, name="join_halves", in_specs=[HBM_REF] * n, out_specs=[HBM_REF] * n,
        out_shape=[jax.ShapeDtypeStruct(b.shape, b.dtype) for b in blocks],
        input_output_aliases={t: t for t in range(n)},
        scratch_shapes=[pltpu.SemaphoreType.DMA((n,))] * 2, **COMM_PARAMS)(*blocks)


SPLIT_HBM = pl.BlockSpec(memory_space=pltpu.HBM)
SPLIT_SEM = pl.BlockSpec(memory_space=pltpu.SEMAPHORE)
SPLIT_PARAMS = dict(compiler_params=pltpu.CompilerParams(has_side_effects=pltpu.SideEffectType.DATAFLOW_SIDE_EFFECTING))


def _scatter_copies(parts, landed, send_sem, recv_sem):
    x, y, c = _place()
    return [_remote(parts[t].at[2 * px + py], landed[t].at[k], send_sem(k, t), recv_sem(k, t), (px, py, c))
            for k, (px, py) in enumerate(_other_chips(x, y)) for t in range(len(parts))]


def scatter_start(partials, name):
    n = len(partials)
    n_cp = 3 * n

    def body(*refs):
        parts, landed = refs[:n], refs[n:2 * n]
        sems, token = refs[2 * n:2 * n + 2 * n_cp], refs[-1]
        for cp in _scatter_copies(parts, landed, lambda k, t: sems[k * n + t], lambda k, t: sems[n_cp + k * n + t]):
            cp.start()
        token[...] = jnp.zeros_like(token)

    hbm = lambda a: pltpu.with_memory_space_constraint(a, pltpu.HBM)
    landing = [lax.empty((3,) + p.shape[1:], p.dtype) for p in partials]
    res = pl.pallas_call(
        body, name=name,
        out_shape=(*[pltpu.SemaphoreType.DMA(())] * (2 * n_cp),
                   *[pltpu.HBM(a.shape, a.dtype) for a in partials + landing], jax.ShapeDtypeStruct((8, 128), F32)),
        in_specs=[SPLIT_HBM] * (2 * n),
        out_specs=(*[SPLIT_SEM] * (2 * n_cp), *[SPLIT_HBM] * (2 * n), pl.BlockSpec(memory_space=pltpu.VMEM)),
        input_output_aliases={t: 2 * n_cp + t for t in range(2 * n)}, **SPLIT_PARAMS,
    )(*[hbm(a) for a in partials + landing])
    return (n, res[:-1]), res[-1]


def scatter_wait(handles, after, name):
    n, held = handles
    n_cp = 3 * n
    sems, thru = held[:2 * n_cp], held[2 * n_cp:]

    def body(*refs):
        parts, landed = refs[:n], refs[n:2 * n]
        sem_refs = refs[2 * n:2 * n + 2 * n_cp]
        for cp in _scatter_copies(parts, landed, lambda k, t: sem_refs[k * n + t],
                                  lambda k, t: sem_refs[n_cp + k * n + t]):
            cp.wait_send()
            cp.wait_recv()

    res = pl.pallas_call(
        body, name=name, out_shape=tuple(pltpu.HBM(a.shape, a.dtype) for a in thru),
        in_specs=[SPLIT_HBM] * (2 * n) + [SPLIT_SEM] * (2 * n_cp) + [pl.BlockSpec(memory_space=pl.ANY)],
        out_specs=tuple([SPLIT_HBM] * (2 * n)), input_output_aliases={t: t for t in range(2 * n)}, **SPLIT_PARAMS,
    )(*thru, *sems, after)
    return list(res[n:])


def chip_sums(grads):
    names = list(grads)
    got = swap_halves([grads[n] for n in names])
    partials = []
    for name, theirs in zip(names, got):
        n_slot, hr, width = theirs.shape
        tb = _row_block(hr, width, 6)
        per_half = hr // tb
        mine = lambda i, s, per_half=per_half: (i // per_half) * 2 * per_half + s[0] * per_half + i % per_half
        p = placed_map(
            jnp.add,
            [(grads[name].reshape(2 * n_slot * hr, width), mine), (theirs.reshape(n_slot * hr, width), lambda i, s: i)],
            (n_slot * hr, width, BF16, lambda i, s: i), n_blocks=n_slot * per_half, tb=tb, name="chip_sum_" + name)
        partials.append(p.reshape(theirs.shape))
    return got, partials


def owner_sums(grads, got, landed):
    names = list(grads)
    blocks = []
    for name, theirs, arrived in zip(names, got, landed):
        n_slot, hr, width = theirs.shape
        tb = _row_block(hr, width, 6)
        per_half = hr // tb
        views = [(grads[name].reshape(2 * n_slot * hr, width),
                  lambda i, s, per_half=per_half: s[1] * 2 * per_half + s[0] * per_half + i),
                 (theirs.reshape(n_slot * hr, width), lambda i, s, per_half=per_half: s[1] * per_half + i)]
        views += [(arrived.reshape(3 * hr, width), functools.partial(lambda k, per_half, i, s: k * per_half + i,
                                                                     k, per_half)) for k in range(3)]
        f = lambda a, b, l0, l1, l2: (((a + b) + l0.astype(F32)) + l1.astype(F32)) + l2.astype(F32)
        blocks.append(placed_map(
            f, views,(2 * hr, width, F32, lambda i, s, per_half=per_half: s[0] * per_half + i),
            n_blocks=per_half, tb=tb, name="owner_sum_" + name))
    return dict(zip(names, join_halves(blocks)))


def adamw_block(name, w, g, m, v):
    rows, width = w.shape
    return rowmap(_adamw, [w, g, m, v], [], [(width, F32)] * 3, tb=_row_block(rows, width, 7),
                  name="adamw_" + name)


def reduce_small(vec, w, m, v):
    n_dev = 8

    def body(vec_ref, w_ref, m_ref, v_ref, loss_ref, g_ref, d_ref, m2_ref, v2_ref, slots, send_sems, recv_sems):
        x, y, c = _place()
        me = 4 * x + 2 * y + c
        slots[me] = vec_ref[...]
        flips = [(fx, fy, fc) for fx in (0, 1) for fy in (0, 1) for fc in (0, 1)][1:]
        peers = [(1 - x if fx else x, 1 - y if fy else y, 1 - c if fc else c) for fx, fy, fc in flips]
        sends = [pltpu.make_async_remote_copy(
            src_ref=vec_ref, dst_ref=slots.at[me], send_sem=send_sems.at[j], recv_sem=recv_sems.at[j],
            device_id=peer, device_id_type=MESH) for j, peer in enumerate(peers)]
        for cp in sends:
            cp.start()
        for j, (px, py, pc) in enumerate(peers):
            pltpu.make_async_remote_copy(
                src_ref=vec_ref, dst_ref=slots.at[4 * px + 2 * py + pc], send_sem=send_sems.at[j],
                recv_sem=recv_sems.at[j], device_id=(px, py, pc), device_id_type=MESH).wait_recv()
        for cp in sends:
            cp.wait_send()
        g = slots[0]
        for d in range(1, n_dev):
            g = g + slots[d]
        loss_ref[...] = jnp.sum(g[:, :D_MODEL], axis=1, keepdims=True)
        delta, m2, v2 = _adamw(w_ref[...], g, m_ref[...], v_ref[...])
        g_ref[...], d_ref[...], m2_ref[...], v2_ref[...] = g, delta, m2, v2

    vm = pl.BlockSpec(memory_space=pltpu.VMEM)
    vec_t = jax.ShapeDtypeStruct(vec.shape, F32)
    return pl.pallas_call(
        body, name="reduce_small", in_specs=[vm] * 4, out_specs=[vm] * 5,
        out_shape=[jax.ShapeDtypeStruct((1, 1), F32)] + [vec_t] * 4,
        scratch_shapes=[pltpu.VMEM((n_dev,) + vec.shape, F32), pltpu.SemaphoreType.DMA((n_dev - 1,)),
                        pltpu.SemaphoreType.DMA((n_dev - 1,))],
        compiler_params=pltpu.CompilerParams(has_side_effects=True),
    )(vec, w, m, v)


def kernel(x, ffn1_norm, ffn1_w_in, ffn1_w_out, mix_norm, w_in, b_gate, rwkv_mu, rwkv_w0, rwkv_w2, rwkv_a0, rwkv_a2, rwkv_g2, rwkv_k_k, rwkv_k_a, rwkv_r_k, rwkv_ln_w, rwkv_ln_b, attn_q_norm, attn_k_norm, w_proj_rwkv, w_proj_attn, w_out, ffn2_norm, ffn2_w_in, ffn2_w_out, loss_target, m_ffn1_norm, m_ffn1_w_in, m_ffn1_w_out, m_mix_norm, m_w_in, m_b_gate, m_rwkv_mu, m_rwkv_w0, m_rwkv_w2, m_rwkv_a0, m_rwkv_a2, m_rwkv_g2, m_rwkv_k_k, m_rwkv_k_a, m_rwkv_r_k, m_rwkv_ln_w, m_rwkv_ln_b, m_attn_q_norm, m_attn_k_norm, m_w_proj_rwkv, m_w_proj_attn, m_w_out, m_ffn2_norm, m_ffn2_w_in, m_ffn2_w_out, v_ffn1_norm, v_ffn1_w_in, v_ffn1_w_out, v_mix_norm, v_w_in, v_b_gate, v_rwkv_mu, v_rwkv_w0, v_rwkv_w2, v_rwkv_a0, v_rwkv_a2, v_rwkv_g2, v_rwkv_k_k, v_rwkv_k_a, v_rwkv_r_k, v_rwkv_ln_w, v_rwkv_ln_b, v_attn_q_norm, v_attn_k_norm, v_w_proj_rwkv, v_w_proj_attn, v_w_out, v_ffn2_norm, v_ffn2_w_in, v_ffn2_w_out):
    given = dict(locals())
    weights = {n: given[n] for n in WEIGHT_ORDER}
    mom_m = {n: given["m_" + n] for n in WEIGHT_ORDER}
    mom_v = {n: given["v_" + n] for n in WEIGHT_ORDER}
    big = [name for name, _, _ in BIG]
    shapes = {n: weights[n].shape for n in WEIGHT_ORDER}
    blocks_of = lambda d: local_blocks({n: d[n][0] for n in big})
    w_blk, m_blk, v_blk = blocks_of(weights), blocks_of(mom_m), blocks_of(mom_v)
    names = list(w_blk)

    early = [n for n in names if n not in FIRST_FFN]
    bf16_block = lambda n: w_blk[n].astype(BF16)
    W = {n: blocks_to_full(n, g) for n, g in zip(FIRST_FFN, gather_weights([bf16_block(n) for n in FIRST_FFN]))}
    stages = {"mixer": [n for n in early if n in MIXER_IN], "out": [n for n in early if n not in MIXER_IN]}
    stage_blocks = {s: [bf16_block(n) for n in stages[s]] for s in stages}
    started = {s: gather_start(stage_blocks[s], "gather_start_" + s) for s in ("mixer", "out")}
    start_token = started["mixer"][1] + started["out"][1]

    def more_weights(stage, after):
        landed = gather_wait(started[stage][0], after, "gather_wait_" + stage)
        got = pass_halves(landed, stage_blocks[stage], "pass_halves_" + stage)
        more = {n: blocks_to_full(n, g) for n, g in zip(stages[stage], got)}
        if "lora" in more:
            more.update(split_lora(more.pop("lora")))
        return more

    P = {n: weights[n].reshape(1, -1) for n, _ in SMALL}

    sent = {}

    def send_early(gw):
        lora = jnp.concatenate([gw[n] for n in LORA_PARTS], axis=0)
        sent["grads"] = {n: full_to_blocks(n, lora if n == "lora" else gw[n]) for n in early}
        sent["got"], partials = chip_sums(sent["grads"])
        sent["handles"], token = scatter_start(partials, "scatter_start")
        return token

    loss_cols, dx, gW, gP = layer_step(x[0], loss_target[0], W, P, start_token, more_weights, send_early)
    landed = scatter_wait(sent["handles"], gP["ffn1_norm"], "scatter_wait")

    late = {n: full_to_blocks(n, gW[n]) for n in FIRST_FFN}
    late_got, late_partials = chip_sums(late)
    late_handles, late_token = scatter_start(late_partials, "scatter_start_ffn1")
    landed[-1] = landed[-1] + late_token[0, 0].astype(landed[-1].dtype)
    out_g, out_d, out_m, out_v = {}, {}, {}, {}

    def apply(g_blk):
        for n in g_blk:
            res = (g_blk[n], *adamw_block(n, w_blk[n], g_blk[n], m_blk[n], v_blk[n]))
            for dst, t in zip((out_g, out_d, out_m, out_v), res):
                for part, val in (split_lora(t) if n == "lora" else {n: t}).items():
                    dst[part] = val.reshape(shapes[part])

    apply(owner_sums(sent["grads"], sent["got"], landed))
    apply(owner_sums(late, late_got, scatter_wait(late_handles, list(out_d.values())[-1], "scatter_wait_ffn1")))

    zero_head = jnp.zeros((1, D_MODEL), F32)
    vec = pack_small(gP, loss_cols)
    loss, g_s, d_s, m_s, v_s = reduce_small(
        vec, pack_small({n: weights[n] for n, _ in SMALL}, zero_head),
        pack_small({n: mom_m[n] for n, _ in SMALL}, zero_head),
        pack_small({n: mom_v[n] for n, _ in SMALL}, zero_head))
    for dst, src in ((out_g, g_s), (out_d, d_s), (out_m, m_s), (out_v, v_s)):
        dst.update(unpack_small(src, shapes))

    return (loss[0, 0], dx[None], *[out_g[n] for n in WEIGHT_ORDER], *[out_d[n] for n in WEIGHT_ORDER],
            *[out_m[n] for n in WEIGHT_ORDER], *[out_v[n] for n in WEIGHT_ORDER])
```

```python
import functools

import jax
import jax.numpy as jnp
from jax import lax
from jax.experimental import pallas as pl
from jax.experimental.pallas import tpu as pltpu

F32 = jnp.float32
BF16 = jnp.bfloat16
MESH = pl.DeviceIdType.MESH

D_MODEL = 1024
HEAD_DIM = 64
RWKV_HEADS = 16
LORA_W, LORA_A, LORA_G = 64, 64, 160
LORA = LORA_W + LORA_A + LORA_G
RKV = 3 * D_MODEL
ATTN_PAIRS = ((128, 1), (512, 4), (2048, 16))
ATTN_BLK = 128
ATTN_HPG = 4
ATTN_WIDTH = 768
GROUP_W = ATTN_HPG * HEAD_DIM
D_FF = 2816
GN_EPS = 64e-5
RMS_EPS = 1e-6
NEG_INF = -1e30
WKV_CHUNK = 64
WKV_HEADS_PER_STEP = 16

ADAM_LR, ADAM_B1, ADAM_B2, ADAM_EPS, ADAM_WD, ADAM_STEP = 0.001, 0.9, 0.999, 1e-08, 0.01, 10

V7X_VMEM_BYTES = 64 << 20
VMEM_TEMP_ALLOWANCE = 20 << 20
VMEM_LEFT_FREE = 6 << 20


def _cparams(sem, block_bytes):
    limit = min(2 * block_bytes + VMEM_TEMP_ALLOWANCE, V7X_VMEM_BYTES - VMEM_LEFT_FREE)
    return pltpu.CompilerParams(dimension_semantics=sem, vmem_limit_bytes=int(limit))


def _nbytes(shape, dtype):
    n = 1
    for s in shape:
        n *= s
    return n * jnp.dtype(dtype).itemsize


def _split_bf16(a):
    hi = a.astype(BF16)
    return hi, (a - hi.astype(F32)).astype(BF16)


def _make_dots():
    def raw(a, b, ca, cb):
        return lax.dot_general(a.astype(BF16), b.astype(BF16), (((ca,), (cb,)), ((), ())),
                               preferred_element_type=F32)

    @jax.custom_vjp
    def nn(a, b):
        return raw(a, b, 1, 0)

    @jax.custom_vjp
    def nt(a, b):
        return raw(a, b, 1, 1)

    @jax.custom_vjp
    def tn(a, b):
        return raw(a, b, 0, 0)

    nn.defvjp(lambda a, b: (raw(a, b, 1, 0), (a, b)),
              lambda res, g: (raw(g, res[1], 1, 1), raw(res[0], g, 0, 0)))
    nt.defvjp(lambda a, b: (raw(a, b, 1, 1), (a, b)),
              lambda res, g: (raw(g, res[1], 1, 0), raw(g, res[0], 0, 0)))
    tn.defvjp(lambda a, b: (raw(a, b, 0, 0), (a, b)),
              lambda res, g: (raw(res[1], g, 1, 1), raw(res[0], g, 1, 0)))
    return nn, nt, tn


def _exact_rhs_dot(x, ones, cx, co):
    hi, lo = _split_bf16(x)
    dims = (((cx,), (co,)), ((), ()))
    return (lax.dot_general(hi, ones, dims, preferred_element_type=F32)
            + lax.dot_general(lo, ones, dims, preferred_element_type=F32))


@jax.custom_vjp
def SEG(x, ones):
    return _exact_rhs_dot(x, ones, 1, 0)


SEG.defvjp(lambda x, ones: (_exact_rhs_dot(x, ones, 1, 0), ones),
           lambda ones, g: (_exact_rhs_dot(g, ones, 1, 1), jnp.zeros_like(ones)))

NN, NT, TN = _make_dots()


MM_TILE_M, MM_TILE_N, MM_TILE_K = 1408, 1408, 1536


def _pick(n, cap):
    best = None
    for t in range(128, min(n, cap) + 1, 128):
        if n % t == 0:
            best = t
    return best or n


def matmul(a, b, mode, name, *, add=None, scale=1.0, out_dtype=F32):
    if mode == "nn":
        (M, K), (K2, N) = a.shape, b.shape
    elif mode == "nt":
        (M, K), (N, K2) = a.shape, b.shape
    else:
        (K, M), (K2, N) = a.shape, b.shape
    assert K == K2, (name, a.shape, b.shape)
    tm, tn, tk = _pick(M, MM_TILE_M), _pick(N, MM_TILE_N), _pick(K, MM_TILE_K)
    nk = K // tk
    ca, cb = {"nn": (1, 0), "nt": (1, 1), "tn": (0, 0)}[mode]

    def body(*refs):
        if add is None:
            a_ref, b_ref, o_ref, acc_ref = refs
        else:
            a_ref, b_ref, add_ref, o_ref, acc_ref = refs
        k = pl.program_id(2)

        @pl.when(k == 0)
        def _():
            acc_ref[...] = jnp.zeros_like(acc_ref)

        acc_ref[...] += lax.dot_general(a_ref[...].astype(BF16), b_ref[...].astype(BF16),
                                        (((ca,), (cb,)), ((), ())), preferred_element_type=F32)

        @pl.when(k == nk - 1)
        def _():
            r = acc_ref[...] * scale
            if add is not None:
                r = add_ref[...] + r
            o_ref[...] = r.astype(o_ref.dtype)

    a_spec = (pl.BlockSpec((tk, tm), lambda i, j, k: (k, i)) if mode == "tn"
              else pl.BlockSpec((tm, tk), lambda i, j, k: (i, k)))
    b_spec = (pl.BlockSpec((tn, tk), lambda i, j, k: (j, k)) if mode == "nt"
              else pl.BlockSpec((tk, tn), lambda i, j, k: (k, j)))
    in_specs, args = [a_spec, b_spec], [a, b]
    blk = tm * tk * a.dtype.itemsize + tk * tn * b.dtype.itemsize + tm * tn * 8
    if add is not None:
        in_specs.append(pl.BlockSpec((tm, tn), lambda i, j, k: (i, j)))
        args.append(add)
        blk += tm * tn * 4
    return pl.pallas_call(
        body, name=name, grid=(M // tm, N // tn, nk),
        in_specs=in_specs, out_specs=pl.BlockSpec((tm, tn), lambda i, j, k: (i, j)),
        out_shape=jax.ShapeDtypeStruct((M, N), out_dtype),
        scratch_shapes=[pltpu.VMEM((tm, tn), F32)],
        compiler_params=_cparams(("parallel", "parallel", "arbitrary"), blk),
    )(*args)


def matmul_cs(a, w, mode, name, token):
    n_blk = N_SHARDS
    if mode == "tn":
        (K, R), Cs = a.shape, w.shape[2] // 2
        tm, tk = _pick(R, MM_TILE_M), _pick(K, 1024)
        grid = (R // tm, n_blk, K // tk)
        a_spec = pl.BlockSpec((tk, tm), lambda i, j, k: (k, i))
        w_spec = pl.BlockSpec((None, tk, Cs), lambda i, j, k: (j // 2, k, j % 2))
        o_spec = pl.BlockSpec((None, tm, Cs), lambda i, j, k: (j, i, 0))
        out_shape, acc_shape, dims = (n_blk, R, Cs), (tm, Cs), (0, 0)
        blk = tk * tm * a.dtype.itemsize + tk * Cs * w.dtype.itemsize + tm * Cs * 8
    else:
        M, (_, R, Cs) = a.shape[1], w.shape
        tm, tn = _pick(M, MM_TILE_M), _pick(R, MM_TILE_N)
        grid = (M // tm, R // tn, n_blk)
        a_spec = pl.BlockSpec((None, tm, Cs), lambda i, j, k: (k // 2, i, k % 2))
        w_spec = pl.BlockSpec((None, tn, Cs), lambda i, j, k: (k, j, 0))
        o_spec = pl.BlockSpec((tm, tn), lambda i, j, k: (i, j))
        out_shape, acc_shape, dims = (M, R), (tm, tn), (1, 1)
        blk = tm * Cs * a.dtype.itemsize + tn * Cs * w.dtype.itemsize + tm * tn * 8
    nk = grid[2]

    def body(a_ref, w_ref, tok_ref, o_ref, acc_ref):
        k = pl.program_id(2)

        @pl.when(k == 0)
        def _():
            acc_ref[...] = jnp.zeros_like(acc_ref)

        acc_ref[...] += lax.dot_general(a_ref[...].astype(BF16), w_ref[...].astype(BF16),
                                        (((dims[0],), (dims[1],)), ((), ())), preferred_element_type=F32)

        @pl.when(k == nk - 1)
        def _():
            o_ref[...] = acc_ref[...] + tok_ref[0:1, 0:1]

    return pl.pallas_call(
        body, name=name, grid=grid, in_specs=[a_spec, w_spec, pl.BlockSpec(token.shape, lambda i, j, k: (0, 0))],
        out_specs=o_spec, out_shape=jax.ShapeDtypeStruct(out_shape, F32), scratch_shapes=[pltpu.VMEM(acc_shape, F32)],
        compiler_params=_cparams(("parallel", "parallel", "arbitrary"), blk),
    )(a, w, token)


FFN_TILE_M = 512


def _swiglu(gate, up):
    return gate * jax.nn.sigmoid(gate) * up


def ffn_in_act(h, w, name):
    (M, R), Cs, half = h.shape, w.shape[2], N_SHARDS // 2
    tm, tk = _pick(M, FFN_TILE_M), _pick(R, 1024)
    nk = R // tk

    def body(h_ref, wg_ref, wu_ref, gu_ref, act_ref, acc_ref):
        k = pl.program_id(2)

        @pl.when(k == 0)
        def _():
            acc_ref[...] = jnp.zeros_like(acc_ref)

        hb = h_ref[...].astype(BF16)
        for part, w_ref in enumerate((wg_ref, wu_ref)):
            acc_ref[part] += jnp.dot(hb, w_ref[...].astype(BF16), preferred_element_type=F32)

        @pl.when(k == nk - 1)
        def _():
            gu_ref[...] = acc_ref[...]
            act_ref[...] = _swiglu(acc_ref[0], acc_ref[1]).astype(act_ref.dtype)

    w_spec = lambda off: pl.BlockSpec((None, tk, Cs), functools.partial(lambda off, j, i, k: (j + off, k, 0), off))
    blk = tm * tk * h.dtype.itemsize + 2 * tk * Cs * w.dtype.itemsize + tm * Cs * (16 + 2)
    return pl.pallas_call(
        body, name=name, grid=(half, M // tm, nk),
        in_specs=[pl.BlockSpec((tm, tk), lambda j, i, k: (i, k)), w_spec(0), w_spec(half)],
        out_specs=[pl.BlockSpec((2, tm, Cs), lambda j, i, k: (0, i, j)), pl.BlockSpec((tm, Cs), lambda j, i, k: (i, j))],
        out_shape=[jax.ShapeDtypeStruct((2, M, half * Cs), F32), jax.ShapeDtypeStruct((M, half * Cs), BF16)],
        scratch_shapes=[pltpu.VMEM((2, tm, Cs), F32)],
        compiler_params=_cparams(("parallel", "parallel", "arbitrary"), blk),
    )(h, w, w)


def ffn_dact_dgu(dy, w_out, gu, scale, name):
    (M, D), F = dy.shape, w_out.shape[0]
    tm, tn = _pick(M, FFN_TILE_M), F // 2

    def body(dy_ref, w_ref, gu_ref, dgu_ref):
        dact = scale * lax.dot_general(dy_ref[...].astype(BF16), w_ref[...].astype(BF16),
                                       (((1,), (1,)), ((), ())), preferred_element_type=F32)
        dgate, dup = jax.vjp(_swiglu, gu_ref[0], gu_ref[1])[1](dact)
        dgu_ref[0] = dgate.astype(dgu_ref.dtype)
        dgu_ref[1] = dup.astype(dgu_ref.dtype)

    pair = pl.BlockSpec((2, tm, tn), lambda j, i: (0, i, j))
    blk = tm * D * dy.dtype.itemsize + tn * D * w_out.dtype.itemsize + 2 * tm * tn * (4 + 2)
    return pl.pallas_call(
        body, name=name, grid=(F // tn, M // tm),
        in_specs=[pl.BlockSpec((tm, D), lambda j, i: (i, 0)), pl.BlockSpec((tn, D), lambda j, i: (j, 0)), pair],
        out_specs=pair, out_shape=jax.ShapeDtypeStruct((2, M, F), BF16),
        compiler_params=_cparams(("parallel", "parallel"), blk),
    )(dy, w_out, gu)


def _row_block(n, width, n_arrays):
    cap = (V7X_VMEM_BYTES // 4) // (2 * 4 * width * n_arrays)
    best = None
    for t in range(16, min(n, cap) + 1, 16):
        if n % t == 0:
            best = t
    return best or n


def placed_map(f, ins, out, *, n_blocks, tb, name):
    def body(*refs):
        refs[-1][...] = f(*[r[...] for r in refs[:-1]]).astype(refs[-1].dtype)

    def spec(fn):
        def index(i):
            x, y, c = _place()
            return fn(i, (c, 2 * x + y)), 0
        return pl.BlockSpec((tb, width), index)

    o_rows, width, o_dtype, o_fn = out
    blk = (sum(a.dtype.itemsize for a, _ in ins) + jnp.dtype(o_dtype).itemsize) * tb * width
    return pl.pallas_call(
        body, name=name, grid=(n_blocks,), in_specs=[spec(fn) for _, fn in ins], out_specs=spec(o_fn),
        out_shape=jax.ShapeDtypeStruct((o_rows, width), o_dtype),
        compiler_params=_cparams(("parallel",), blk),
    )(*[a for a, _ in ins])


def rowmap(f, rows, params, outs, accs=(), *, tb, name):
    rows = [r if isinstance(r, tuple) else (r, r.shape[1], 0) for r in rows]
    S = rows[0][0].shape[0]
    assert S % tb == 0, (name, S, tb)
    n_in, n_out = len(rows) + len(params), len(outs)

    def body(*refs):
        res = f(*[r[...] for r in refs[:n_in]])
        res = res if isinstance(res, (tuple, list)) else (res,)
        o_refs, a_refs = refs[n_in:n_in + n_out], refs[n_in + n_out:]
        for ref, val in zip(o_refs, res[:n_out]):
            ref[...] = val.astype(ref.dtype)
        if a_refs:
            @pl.when(pl.program_id(0) == 0)
            def _():
                for ref in a_refs:
                    ref[...] = jnp.zeros_like(ref)

            for ref, val in zip(a_refs, res[n_out:]):
                ref[...] += val.astype(F32)

    in_specs = [pl.BlockSpec((tb, w), functools.partial(lambda cb, i: (i, cb), cb)) for _, w, cb in rows]
    in_specs += [pl.BlockSpec(p.shape, lambda i: (0, 0)) for p in params]
    out_specs = [pl.BlockSpec((tb, w), lambda i: (i, 0)) for w, _ in outs]
    out_specs += [pl.BlockSpec(tuple(s), lambda i: (0, 0)) for s in accs]
    out_shape = [jax.ShapeDtypeStruct((S, w), dt) for w, dt in outs]
    out_shape += [jax.ShapeDtypeStruct(tuple(s), F32) for s in accs]
    blk = sum(tb * w * a.dtype.itemsize for a, w, _ in rows) + sum(_nbytes(p.shape, p.dtype) for p in params)
    blk += sum(_nbytes((tb, w), dt) for w, dt in outs) + sum(_nbytes(s, F32) for s in accs)
    res = pl.pallas_call(
        body, name=name, grid=(S // tb,), in_specs=in_specs, out_specs=out_specs, out_shape=out_shape,
        compiler_params=_cparams(("arbitrary",) if accs else ("parallel",), blk),
    )(*[r[0] for r in rows], *[pltpu.with_memory_space_constraint(p, pltpu.HBM) for p in params])
    return res


def _rms(x, g):
    return x * lax.rsqrt(jnp.mean(x * x, axis=-1, keepdims=True) + RMS_EPS) * g


def _softplus(z):
    return jnp.maximum(z, 0.0) + jnp.log(1.0 + jnp.exp(-jnp.abs(z)))


def _rwkv_pre(xrk, xlo, w0, w2p, a0, a2p, g2p, k_k, k_a, seg, seg_t):
    k = xrk[:, D_MODEL:2 * D_MODEL]
    w = -_softplus(-(w0 + NN(jnp.tanh(xlo), w2p))) - 0.5
    log_decay = -jnp.exp(w)
    a = jax.nn.sigmoid(a0 + NN(xlo, a2p))
    g = NN(jax.nn.sigmoid(xlo), g2p)
    kk = k * k_k
    norm = jnp.maximum(jnp.sqrt(SEG(kk * kk, seg)), 1e-12)
    kk = kk * SEG(1.0 / norm, seg_t)
    k_mod = k * (1.0 + (a - 1.0) * k_a)
    return log_decay, k_mod, -kk, kk * a, g


def _rwkv_post(wkv, r, k_mod, v, g, r_k, ln_w, ln_b, seg, seg_t):
    inv_n = 1.0 / HEAD_DIM
    mean = SEG(wkv, seg) * inv_n
    cen = wkv - SEG(mean, seg_t)
    var = SEG(cen * cen, seg) * inv_n
    y = cen * SEG(lax.rsqrt(var + GN_EPS), seg_t) * ln_w + ln_b
    bonus = SEG(SEG(r * k_mod * r_k, seg), seg_t) * v
    return (y + bonus) * g


def _qk_norm(q, k, q_gain, k_gain, seg, seg_t, tile_t):
    def norm(x, gain):
        mean_sq = SEG(x * x, seg) * (1.0 / HEAD_DIM)
        return x * SEG(lax.rsqrt(mean_sq + RMS_EPS), seg_t) * SEG(gain, tile_t)

    return norm(q, q_gain) * (HEAD_DIM ** -0.5), norm(k, k_gain)


def _gate_merge(pgate, pa, pb, b_gate):
    sg = jax.nn.sigmoid(pgate + b_gate)
    return sg[:, :D_MODEL] * pa + sg[:, D_MODEL:] * pb


def _group_combine(o0, o1, o2, l0, l1, l2):
    m = jnp.maximum(jnp.maximum(l0, l1), l2)
    es = [jnp.exp(l - m) for l in (l0, l1, l2)]
    den = es[0] + es[1] + es[2]
    return jnp.concatenate([o * (e / den) for o, e in zip((o0, o1, o2), es)], axis=1)


def _each(f, *xs):
    return tuple(f(*args) for args in zip(*xs))


def _attn_block(q, kc, kp, vc, vp, first):
    qi = lax.broadcasted_iota(jnp.int32, (ATTN_BLK, ATTN_BLK), 0)
    kj = lax.broadcasted_iota(jnp.int32, (ATTN_BLK, ATTN_BLK), 1)
    own = kj <= qi
    s_c = _each(lambda a, b: jnp.where(own, NT(a, b), NEG_INF), q, kc)
    s_p = _each(lambda a, b, f: jnp.where((kj >= qi) & (f < 0.5), NT(a, b), NEG_INF), q, kp, first)
    row_max = lambda s: jnp.max(s, axis=-1, keepdims=True)
    row_sum = lambda s: jnp.sum(s, axis=-1, keepdims=True)
    m = _each(lambda c_, p_: jnp.maximum(row_max(c_), row_max(p_)), s_c, s_p)
    e_c, e_p = _each(lambda s, m_: jnp.exp(s - m_), s_c, m), _each(lambda s, m_: jnp.exp(s - m_), s_p, m)
    den = _each(lambda c_, p_: row_sum(c_) + row_sum(p_), e_c, e_p)
    inv = _each(lambda d_: 1.0 / d_, den)
    o = _each(lambda ec, ep, i_, vc_, vp_: (NN(ec, vc_) + NN(ep, vp_)) * i_, e_c, e_p, inv, vc, vp)
    lse = _each(lambda m_, d_: jnp.broadcast_to(m_ + jnp.log(d_), (ATTN_BLK, HEAD_DIM)), m, den)
    return o, lse


def _attn_pair(q, k, k_before, v, v_before, first):
    n = len(q[0])
    o, lse = _attn_block(q[0] + q[1], k[0] + k[1], k_before + k[0], v[0] + v[1], v_before + v[0],
                         (first[0],) * n + (first[1],) * n)
    return (o[:n], o[n:]), (lse[:n], lse[n:])


TRI_SEED = 8


def _tri_inverse(n):
    c = n[0].shape[0]
    row = lax.broadcasted_iota(jnp.int32, (c, c), 0)
    col = lax.broadcasted_iota(jnp.int32, (c, c), 1)
    same_block = lambda size: (row >> (size.bit_length() - 1)) == (col >> (size.bit_length() - 1))
    seed = same_block(TRI_SEED)
    p = _each(lambda m: jnp.where(seed, m, 0.0), n)
    t, span = _each(lambda m: (row == col).astype(F32) + m, p), 2
    while span < TRI_SEED:
        p = _each(NN, p, p)
        t = _each(lambda t_, p_: t_ + NN(t_, p_), t, p)
        span *= 2
    size = TRI_SEED
    while size < c:
        joins = same_block(2 * size) & jnp.logical_not(same_block(size))
        t = _each(lambda t_, m: t_ + NN(NN(t_, jnp.where(joins, m, 0.0)), t_), t, n)
        size *= 2
    return t


@jax.custom_vjp
def _tri_solve(n, rhs, t):
    return _each(NN, t, rhs)


def _tri_solve_fwd(n, rhs, t):
    x = _each(NN, t, rhs)
    return x, (t, x)


def _tri_solve_bwd(res, dx):
    t, x = res
    drhs = _each(TN, t, dx)
    return _each(NT, drhs, x), drhs, _each(jnp.zeros_like, t)


_tri_solve.defvjp(_tri_solve_fwd, _tri_solve_bwd)


def _lower_ones(c):
    row = lax.broadcasted_iota(jnp.int32, (c, c), 0)
    col = lax.broadcasted_iota(jnp.int32, (c, c), 1)
    return (row >= col).astype(BF16)


def _ones_dot(ones, x, contract):
    hi, lo = _split_bf16(x)
    dims = (((contract,), (0,)), ((), ()))
    return (lax.dot_general(ones, hi, dims, preferred_element_type=F32)
            + lax.dot_general(ones, lo, dims, preferred_element_type=F32))


@jax.custom_vjp
def _cumsum_rows(x):
    return _ones_dot(_lower_ones(x.shape[0]), x, 1)


_cumsum_rows.defvjp(lambda x: (_ones_dot(_lower_ones(x.shape[0]), x, 1), None),
                    lambda _, g: (_ones_dot(_lower_ones(g.shape[0]), g, 0),))


def _wkv_chunk(s0, r, lw, k, v, a, b, t_inv=None):
    c = r[0].shape[0]
    row = lax.broadcasted_iota(jnp.int32, (c, c), 0)
    col = lax.broadcasted_iota(jnp.int32, (c, c), 1)
    strict, incl = row > col, row >= col
    cat = lambda p, q: jnp.concatenate([p, q], axis=0)
    cum = _each(_cumsum_rows, lw)
    e_neg = _each(lambda c_: jnp.exp(-c_), cum)
    ar = _each(lambda a_, r_, c_, l_: cat(a_ * jnp.exp(c_ - l_), r_ * jnp.exp(c_)), a, r, cum, lw)
    b_t, k_t = _each(jnp.multiply, b, e_neg), _each(jnp.multiply, k, e_neg)
    p_b, p_k, p_s = _each(NT, ar, b_t), _each(NT, ar, k_t), _each(NT, ar, s0)
    n_ab = _each(lambda p: jnp.where(strict, p[:c], 0.0), p_b)
    m_rb = _each(lambda p: jnp.where(incl, p[c:], 0.0), p_b)
    n_ak = _each(lambda p: jnp.where(strict, p[:c], 0.0), p_k)
    m_rk = _each(lambda p: jnp.where(incl, p[c:], 0.0), p_k)
    if t_inv is None:
        t_inv = _tri_inverse(n_ab)
    u = _tri_solve(n_ab, _each(lambda p, n_, v_: p[:c] + NN(n_, v_), p_s, n_ak, v), t_inv)
    y = _each(lambda p, mb, u_, mk, v_: p[c:] + NN(mb, u_) + NN(mk, v_), p_s, m_rb, u, m_rk, v)
    g_end = _each(lambda l_: jnp.exp(jnp.sum(l_, axis=0, keepdims=True)), lw)
    s1 = _each(lambda s_, g_, u_, v_, b_, k_: s_ * g_ + TN(cat(u_, v_), cat(b_, k_) * g_),
               s0, g_end, u, v, b_t, k_t)
    return y, s1, t_inv


def _adamw(w, g, m, v):
    m = ADAM_B1 * m + (1.0 - ADAM_B1) * g
    v = ADAM_B2 * v + (1.0 - ADAM_B2) * jnp.square(g)
    m_hat = m / (1.0 - ADAM_B1 ** ADAM_STEP)
    v_hat = v / (1.0 - ADAM_B2 ** ADAM_STEP)
    delta = -ADAM_LR * (m_hat / (jnp.sqrt(v_hat) + ADAM_EPS) + ADAM_WD * w)
    return delta, m, v


def token_shift_fwd(p, mu, *, tb, name):
    S, W = p.shape
    hb = tb // 8

    def body(p_ref, halo_ref, mu_ref, o_ref):
        i = pl.program_id(0)
        x = p_ref[...]
        before = halo_ref[7:8, :] * (i > 0).astype(F32)
        row = lax.broadcasted_iota(jnp.int32, (tb, W), 0)
        prev = jnp.where(row == 0, before, pltpu.roll(x, 1, 0))
        o_ref[...] = x + (prev - x) * mu_ref[...]

    blk = (2 * tb + 8) * W * 4
    return pl.pallas_call(
        body, name=name, grid=(S // tb,),
        in_specs=[pl.BlockSpec((tb, W), lambda i: (i, 0)),
                  pl.BlockSpec((8, W), lambda i: (jnp.maximum(i * hb - 1, 0), 0)),
                  pl.BlockSpec((1, W), lambda i: (0, 0))],
        out_specs=pl.BlockSpec((tb, W), lambda i: (i, 0)),
        out_shape=jax.ShapeDtypeStruct((S, W), F32),
        compiler_params=_cparams(("parallel",), blk),
    )(p, p, mu)


def token_shift_bwd(dxs, p, mu, *, tb, name):
    S, W = p.shape
    hb, nb = tb // 8, S // tb

    def body(d_ref, dnext_ref, p_ref, halo_ref, mu_ref, dp_ref, dmu_ref):
        i = pl.program_id(0)
        d, x, mu_v = d_ref[...], p_ref[...], mu_ref[...]
        row = lax.broadcasted_iota(jnp.int32, (tb, W), 0)
        before = halo_ref[7:8, :] * (i > 0).astype(F32)
        prev = jnp.where(row == 0, before, pltpu.roll(x, 1, 0))
        t = d * mu_v
        after = dnext_ref[0:1, :] * mu_v * (i < nb - 1).astype(F32)
        nxt = jnp.where(row == tb - 1, after, pltpu.roll(t, tb - 1, 0))
        dp_ref[...] = (d - t + nxt).astype(dp_ref.dtype)

        @pl.when(i == 0)
        def _():
            dmu_ref[...] = jnp.zeros_like(dmu_ref)

        dmu_ref[...] += jnp.sum(d * (prev - x), axis=0, keepdims=True)

    blk = (3 * tb + 16) * W * 4
    return pl.pallas_call(
        body, name=name, grid=(nb,),
        in_specs=[pl.BlockSpec((tb, W), lambda i: (i, 0)),
                  pl.BlockSpec((8, W), lambda i: (jnp.minimum((i + 1) * hb, S // 8 - 1), 0)),
                  pl.BlockSpec((tb, W), lambda i: (i, 0)),
                  pl.BlockSpec((8, W), lambda i: (jnp.maximum(i * hb - 1, 0), 0)),
                  pl.BlockSpec((1, W), lambda i: (0, 0))],
        out_specs=[pl.BlockSpec((tb, W), lambda i: (i, 0)), pl.BlockSpec((1, W), lambda i: (0, 0))],
        out_shape=[jax.ShapeDtypeStruct((S, W), BF16), jax.ShapeDtypeStruct((1, W), F32)],
        compiler_params=_cparams(("arbitrary",), blk),
    )(dxs, dxs, p, p, mu)


def _head_cols(h):
    return pl.ds(h * HEAD_DIM, HEAD_DIM)


def wkv_fwd(xs_rk, lw, k, a, b):
    S = lw.shape[0]
    C, nc, G, N = WKV_CHUNK, S // WKV_CHUNK, WKV_HEADS_PER_STEP, HEAD_DIM

    def body(r_ref, lw_ref, k_ref, v_ref, a_ref, b_ref, y_ref, st_ref, ti_ref, state):
        @pl.when(pl.program_id(1) == 0)
        def _():
            state[...] = jnp.zeros_like(state)

        heads = lambda ref: tuple(ref[:, _head_cols(h)] for h in range(G))
        s0 = tuple(state[h] for h in range(G))
        y, s1, t_inv = _wkv_chunk(s0, heads(r_ref), heads(lw_ref), heads(k_ref), heads(v_ref), heads(a_ref),
                                  heads(b_ref))
        for h in range(G):
            st_ref[h] = s0[h]
            ti_ref[h] = t_inv[h]
            y_ref[:, _head_cols(h)] = y[h]
            state[h] = s1[h]

    W = G * N
    seq = lambda j: pl.BlockSpec((C, W), functools.partial(lambda j, g, c: (c, j + g), j))
    per = D_MODEL // W
    per_chunk = pl.BlockSpec((None, G, N, N), lambda g, c: (c, g, 0, 0))
    return pl.pallas_call(
        body, name="wkv_fwd", grid=(RWKV_HEADS // G, nc),
        in_specs=[seq(0), seq(0), seq(0), seq(2 * per), seq(0), seq(0)],
        out_specs=[seq(0), per_chunk, per_chunk],
        out_shape=[jax.ShapeDtypeStruct((S, D_MODEL), F32)] + [jax.ShapeDtypeStruct((nc, RWKV_HEADS, N, N), F32)] * 2,
        scratch_shapes=[pltpu.VMEM((G, N, N), F32)],
        compiler_params=_cparams(("parallel", "arbitrary"), 8 * C * W * 4 + 3 * G * N * N * 4),
    )(xs_rk, lw, k, xs_rk, a, b)


def wkv_bwd(xs_rk, lw, k, a, b, states, t_invs, dy):
    S = lw.shape[0]
    C, nc, G, N = WKV_CHUNK, S // WKV_CHUNK, WKV_HEADS_PER_STEP, HEAD_DIM

    def body(r_ref, lw_ref, k_ref, v_ref, a_ref, b_ref, st_ref, ti_ref, dy_ref,
             dr_ref, dlw_ref, dk_ref, dv_ref, da_ref, db_ref, dstate):
        @pl.when(pl.program_id(1) == 0)
        def _():
            dstate[...] = jnp.zeros_like(dstate)

        heads = lambda ref: tuple(ref[:, _head_cols(h)] for h in range(G))
        t_inv = tuple(ti_ref[h] for h in range(G))
        chunk = lambda *args: _wkv_chunk(*args, t_inv)[:2]
        _, pull = jax.vjp(chunk, tuple(st_ref[h] for h in range(G)), heads(r_ref), heads(lw_ref),
                          heads(k_ref), heads(v_ref), heads(a_ref), heads(b_ref))
        ds0, *grads = pull((heads(dy_ref), tuple(dstate[h] for h in range(G))))
        for h in range(G):
            dstate[h] = ds0[h]
            for ref, grad in zip((dr_ref, dlw_ref, dk_ref, dv_ref, da_ref, db_ref), grads):
                ref[:, _head_cols(h)] = grad[h]

    W = G * N
    seq = lambda j: pl.BlockSpec((C, W), functools.partial(lambda j, g, c: (nc - 1 - c, j + g), j))
    per = D_MODEL // W
    st = pl.BlockSpec((None, G, N, N), lambda g, c: (nc - 1 - c, g, 0, 0))
    return pl.pallas_call(
        body, name="wkv_bwd", grid=(RWKV_HEADS // G, nc),
        in_specs=[seq(0), seq(0), seq(0), seq(2 * per), seq(0), seq(0), st, st, seq(0)],
        out_specs=[seq(0)] * 6, out_shape=[jax.ShapeDtypeStruct((S, D_MODEL), F32)] * 6,
        scratch_shapes=[pltpu.VMEM((G, N, N), F32)],
        compiler_params=_cparams(("parallel", "arbitrary"), 14 * C * W * 4 + 3 * G * N * N * 4),
    )(xs_rk, lw, k, xs_rk, a, b, states, t_invs, dy)


def _first_flag(i, per_seq):
    return (lax.rem(i, per_seq) == 0).astype(F32)


def _view(a):
    return a if isinstance(a, tuple) else (a, 0)


def _block_rows(half):
    return pl.ds(half * ATTN_BLK, ATTN_BLK)


def _block_heads(ref, half):
    return tuple(ref[_block_rows(half), _head_cols(h)] for h in range(ATTN_HPG))


def _pair_heads(ref):
    return _block_heads(ref, 0), _block_heads(ref, 1)


def attn_fwd(q, k, v, per_seq, name):
    (q, q_col), (k, k_col), (v, v_col) = _view(q), _view(k), _view(v)
    R, N = q.shape[0], GROUP_W
    n_pairs = R // (2 * ATTN_BLK)

    def body(q_ref, k_ref, kb_ref, v_ref, vb_ref, o_ref, lse_ref):
        pair = pl.program_id(0)
        first = (_first_flag(2 * pair, per_seq), _first_flag(2 * pair + 1, per_seq))
        o, lse = _attn_pair(_pair_heads(q_ref), _pair_heads(k_ref), _block_heads(kb_ref, 0), _pair_heads(v_ref),
                            _block_heads(vb_ref, 0), first)
        for half in range(2):
            for h in range(ATTN_HPG):
                o_ref[_block_rows(half), _head_cols(h)] = o[half][h]
                lse_ref[_block_rows(half), _head_cols(h)] = lse[half][h]

    cur = lambda col: pl.BlockSpec((2 * ATTN_BLK, N), lambda i: (i, col))
    prv = lambda col: pl.BlockSpec((ATTN_BLK, N), lambda i: (jnp.maximum(2 * i - 1, 0), col))
    return pl.pallas_call(
        body, name=name, grid=(n_pairs,), in_specs=[cur(q_col), cur(k_col), prv(k_col), cur(v_col), prv(v_col)],
        out_specs=[cur(0), cur(0)], out_shape=[jax.ShapeDtypeStruct((R, N), F32)] * 2,
        compiler_params=_cparams(("parallel",), 12 * ATTN_BLK * N * 4),
    )(q, k, k, v, v)


def attn_bwd(q, k, v, do, dlse, per_seq, name):
    views = [_view(a) for a in (q, k, v, do, dlse)]
    (q, q_col), (k, k_col), (v, v_col), (do, do_col), (dlse, dl_col) = views
    R, N = q.shape[0], GROUP_W
    n_pairs = R // (2 * ATTN_BLK)

    def body(q_ref, k_ref, kb_ref, v_ref, vb_ref, do_ref, dl_ref, dq_ref, dk_ref, dv_ref, carry_k, carry_v):
        step = pl.program_id(0)
        pair = n_pairs - 1 - step
        first = (_first_flag(2 * pair, per_seq), _first_flag(2 * pair + 1, per_seq))

        @pl.when(step == 0)
        def _():
            carry_k[...] = jnp.zeros_like(carry_k)
            carry_v[...] = jnp.zeros_like(carry_v)

        _, pull = jax.vjp(functools.partial(_attn_pair, first=first), _pair_heads(q_ref), _pair_heads(k_ref),
                          _block_heads(kb_ref, 0), _pair_heads(v_ref), _block_heads(vb_ref, 0))
        dq, dk, dk_before, dv, dv_before = pull((_pair_heads(do_ref), _pair_heads(dl_ref)))
        old_k, old_v = _block_heads(carry_k, 0), _block_heads(carry_v, 0)
        for h in range(ATTN_HPG):
            cols = _head_cols(h)
            for half in range(2):
                dq_ref[_block_rows(half), cols] = dq[half][h]
            dk_ref[_block_rows(0), cols] = dk[0][h]
            dv_ref[_block_rows(0), cols] = dv[0][h]
            dk_ref[_block_rows(1), cols] = dk[1][h] + old_k[h]
            dv_ref[_block_rows(1), cols] = dv[1][h] + old_v[h]
            carry_k[:, cols] = dk_before[h]
            carry_v[:, cols] = dv_before[h]

    cur = lambda col: pl.BlockSpec((2 * ATTN_BLK, N), lambda i: (n_pairs - 1 - i, col))
    prv = lambda col: pl.BlockSpec((ATTN_BLK, N), lambda i: (jnp.maximum(2 * (n_pairs - 1 - i) - 1, 0), col))
    return pl.pallas_call(
        body, name=name, grid=(n_pairs,),
        in_specs=[cur(q_col), cur(k_col), prv(k_col), cur(v_col), prv(v_col), cur(do_col), cur(dl_col)],
        out_specs=[cur(0)] * 3, out_shape=[jax.ShapeDtypeStruct((R, N), F32)] * 3,
        scratch_shapes=[pltpu.VMEM((ATTN_BLK, N), F32)] * 2,
        compiler_params=_cparams(("arbitrary",), 22 * ATTN_BLK * N * 4),
    )(q, k, k, v, v, do, dlse)


def by_residue(u, d):
    if d == 1:
        return u
    return u.reshape(u.shape[0] // d, d, GROUP_W).transpose(1, 0, 2).reshape(u.shape)


def by_position(u, d):
    if d == 1:
        return u
    return u.reshape(d, u.shape[0] // d, GROUP_W).transpose(1, 0, 2).reshape(u.shape)


def group_columns(t, col_block, d):
    if d == 1:
        return (t, col_block)
    return by_residue(t[:, GROUP_W * col_block:GROUP_W * (col_block + 1)], d)


def _ffn_fwd(x, norm, w_in, w_out, tag, token):
    h = rowmap(lambda x_b, g, tok: _rms(x_b, g) + tok[0:1, 0:1], [x], [norm, token], [(D_MODEL, BF16)], tb=512,
               name=tag + "_norm")[0]
    gu, act = ffn_in_act(h, w_in, tag + "_in")
    y = matmul(act, w_out, "nn", tag + "_out", add=x, scale=0.5)
    return y, (x, h, gu, act)


def _ffn_bwd(dy, saved, norm, w_in, w_out, tag, on_weight_grads):
    x, h, gu, act = saved
    no_token = jnp.zeros((8, 128), F32)
    dw_out = matmul(act, dy, "tn", tag + "_dwout", scale=0.5)
    dgu = ffn_dact_dgu(dy, w_out, gu, 0.5, tag + "_dgu")
    dw_in = matmul_cs(h, dgu, "tn", tag + "_dwin", no_token)
    dh = matmul_cs(dgu, w_in, "nt", tag + "_dh", on_weight_grads(dw_in, dw_out))

    def norm_bwd(x_b, dh_b, dy_b, g):
        dx, dg = jax.vjp(_rms, x_b, g)[1](dh_b)
        return dy_b + dx, dg

    dx, dnorm = rowmap(norm_bwd, [x, dh, dy], [norm], [(D_MODEL, F32)], [(1, D_MODEL)], tb=256,
                       name=tag + "_dnorm")
    return dx, dnorm, dw_in, dw_out


def layer_step(x, tgt, W, P, start_token, more_weights, on_mixer_grads, on_ffn1_grads):
    S = x.shape[0]
    x1, ffn1_saved = _ffn_fwd(x, P["ffn1_norm"], W["ffn1_w_in"], W["ffn1_w_out"], "ffn1", start_token)
    W = {**W, **more_weights("mixer", x1)}
    head_of = lambda n: jnp.arange(n)[:, None] // HEAD_DIM == jnp.arange(n // HEAD_DIM)[None, :]
    seg, seg_a = head_of(D_MODEL).astype(BF16), head_of(ATTN_WIDTH).astype(BF16)
    seg_t, seg_a_t = seg.T, seg_a.T
    tile_t = (jnp.arange(HEAD_DIM)[:, None] == jnp.arange(ATTN_WIDTH)[None, :] % HEAD_DIM).astype(BF16)
    qk_params = [P["attn_q_norm"], P["attn_k_norm"], seg_a, seg_a_t, tile_t]
    w_rkv, w_lora = W["w_in"][:, :RKV], W["w_in"][:, RKV:RKV + LORA]
    w_qkv = W["w_in"][:, RKV + LORA:RKV + LORA + 3 * ATTN_WIDTH]
    w_gate = W["w_in"][:, RKV + LORA + 3 * ATTN_WIDTH:]
    mu_rk, mu_lo = P["rwkv_mu"][:, :RKV], P["rwkv_mu"][:, RKV:]
    zeros = lambda n: jnp.zeros((n, D_MODEL), F32)
    w2p = jnp.concatenate([W["rwkv_w2"], zeros(LORA - LORA_W)], axis=0)
    a2p = jnp.concatenate([zeros(LORA_W), W["rwkv_a2"], zeros(LORA_G)], axis=0)
    g2p = jnp.concatenate([zeros(LORA_W + LORA_A), W["rwkv_g2"]], axis=0)
    pre_params = [P["rwkv_w0"], w2p, P["rwkv_a0"], a2p, g2p, P["rwkv_k_k"], P["rwkv_k_a"], seg, seg_t]
    post_params = [P["rwkv_r_k"], P["rwkv_ln_w"], P["rwkv_ln_b"], seg, seg_t]
    col = lambda arr, j: (arr, D_MODEL, j)

    h = rowmap(_rms, [x1], [P["mix_norm"]], [(D_MODEL, BF16)], tb=512, name="mix_norm")[0]
    p_rk = matmul(h, w_rkv, "nn", "proj_rkv")
    p_lo = matmul(h, w_lora, "nn", "proj_lora")
    p_qkv = matmul(h, w_qkv, "nn", "proj_qkv")
    p_gate = matmul(h, w_gate, "nn", "proj_gate")
    xs_rk = token_shift_fwd(p_rk, mu_rk, tb=256, name="shift_rk")
    xs_lo = token_shift_fwd(p_lo, mu_lo, tb=256, name="shift_lora")
    lw, k_mod, a_neg, b_kk, g = rowmap(
        _rwkv_pre, [xs_rk, xs_lo], pre_params, [(D_MODEL, F32)] * 5, tb=256, name="rwkv_pre")
    wkv, states, t_invs = wkv_fwd(xs_rk, lw, k_mod, a_neg, b_kk)
    post_rows = [wkv, col(xs_rk, 0), k_mod, col(xs_rk, 2), g]
    y_a = rowmap(_rwkv_post, post_rows, post_params, [(D_MODEL, BF16)], tb=256, name="rwkv_post")[0]

    qk_rows = [(p_qkv, ATTN_WIDTH, 0), (p_qkv, ATTN_WIDTH, 1)]
    qn, kn = rowmap(_qk_norm, qk_rows, qk_params, [(ATTN_WIDTH, F32)] * 2, tb=256, name="qk_norm")
    dil = [d for _, d in ATTN_PAIRS]
    groups = range(len(dil))
    per_seq = [S // d // ATTN_BLK for d in dil]
    v_first = 2 * ATTN_WIDTH // GROUP_W
    q_s = [group_columns(qn, g, dil[g]) for g in groups]
    k_s = [group_columns(kn, g, dil[g]) for g in groups]
    v_s = [group_columns(p_qkv, v_first + g, dil[g]) for g in groups]
    attn = [attn_fwd(q_s[g], k_s[g], v_s[g], per_seq[g], "attn_fwd_%d" % g) for g in groups]
    o_lse = [by_position(attn[g][j], dil[g]) for j in range(2) for g in groups]
    y_b = rowmap(_group_combine, o_lse, [], [(ATTN_WIDTH, BF16)], tb=512, name="attn_combine")[0]

    W = {**W, **more_weights("out", y_b)}
    pa = matmul(y_a, W["w_proj_rwkv"], "nn", "proj_a")
    pb = matmul(y_b, W["w_proj_attn"], "nn", "proj_b")
    merged = rowmap(_gate_merge, [p_gate, pa, pb], [P["b_gate"]], [(D_MODEL, BF16)], tb=256, name="merge")[0]
    x2 = matmul(merged, W["w_out"], "nn", "mix_out", add=x1)
    x3, ffn2_saved = _ffn_fwd(x2, P["ffn2_norm"], W["ffn2_w_in"], W["ffn2_w_out"], "ffn2",
                              jnp.zeros_like(start_token))

    def loss_head(y_b_, t_b):
        err = y_b_ - t_b
        return err * (1.0 / D_MODEL), (0.5 / D_MODEL) * jnp.sum(err * err, axis=0, keepdims=True)

    dx3, loss_cols = rowmap(loss_head, [x3, tgt], [], [(D_MODEL, F32)], [(1, D_MODEL)], tb=512, name="loss")

    gW, gP = {}, {}
    dx2, gP["ffn2_norm"], gW["ffn2_w_in"], gW["ffn2_w_out"] = _ffn_bwd(
        dx3, ffn2_saved, P["ffn2_norm"], W["ffn2_w_in"], W["ffn2_w_out"], "ffn2",
        lambda dw_in, dw_out: jnp.zeros_like(start_token))

    dmerged = matmul(dx2, W["w_out"], "nt", "d_merged")
    gW["w_out"] = matmul(merged, dx2, "tn", "dw_out")

    def merge_bwd(pg, pa_b, pb_b, dm, bg):
        return jax.vjp(_gate_merge, pg, pa_b, pb_b, bg)[1](dm)

    dp_gate, dpa, dpb, gP["b_gate"] = rowmap(
        merge_bwd, [p_gate, pa, pb, dmerged], [P["b_gate"]],
        [(2 * D_MODEL, BF16), (D_MODEL, BF16), (D_MODEL, BF16)], [(1, 2 * D_MODEL)], tb=256, name="merge_bwd")
    dy_a = matmul(dpa, W["w_proj_rwkv"], "nt", "d_ya")
    gW["w_proj_rwkv"] = matmul(y_a, dpa, "tn", "dw_proj_a")
    dy_b = matmul(dpb, W["w_proj_attn"], "nt", "d_yb")
    gW["w_proj_attn"] = matmul(y_b, dpb, "tn", "dw_proj_b")

    def combine_bwd(*blocks):
        return jax.vjp(_group_combine, *blocks[:-1])[1](blocks[-1])

    d_o_lse = rowmap(combine_bwd, o_lse + [dy_b], [], [(GROUP_W, F32)] * 6, tb=256, name="attn_combine_bwd")
    d_attn = [attn_bwd(q_s[g], k_s[g], v_s[g], by_residue(d_o_lse[g], dil[g]), by_residue(d_o_lse[3 + g], dil[g]),
                       per_seq[g], "attn_bwd_%d" % g) for g in groups]

    def qk_norm_bwd(q_b, k_b, *rest):
        dqkv, (qg, kg, sg, sgt, tl) = rest[:9], rest[9:]
        f = lambda *a: _qk_norm(*a, sg, sgt, tl)
        dqn, dkn = jnp.concatenate(dqkv[0:3], axis=1), jnp.concatenate(dqkv[3:6], axis=1)
        dq, dk, dqg, dkg = jax.vjp(f, q_b, k_b, qg, kg)[1]((dqn, dkn))
        return jnp.concatenate([dq, dk, *dqkv[6:9]], axis=1), dqg, dkg

    dp_qkv, gP["attn_q_norm"], gP["attn_k_norm"] = rowmap(
        qk_norm_bwd, qk_rows + [by_position(d_attn[g][j], dil[g]) for j in range(3) for g in groups], qk_params,
        [(3 * ATTN_WIDTH, BF16)], [(1, HEAD_DIM)] * 2, tb=256, name="qk_norm_bwd")

    def post_bwd(wkv_b, r_b, k_b, v_b, g_b, d_b, r_k, ln_w, ln_b, sg, sgt):
        f = lambda *a: _rwkv_post(*a, sg, sgt)
        return jax.vjp(f, wkv_b, r_b, k_b, v_b, g_b, r_k, ln_w, ln_b)[1](d_b)

    dwkv, dr_p, dk_p, dv_p, dg, gP["rwkv_r_k"], gP["rwkv_ln_w"], gP["rwkv_ln_b"] = rowmap(
        post_bwd, post_rows + [dy_a], post_params, [(D_MODEL, F32)] * 5, [(1, D_MODEL)] * 3, tb=128,
        name="rwkv_post_bwd")
    dr_w, dlw, dk_w, dv_w, da_neg, db_kk = wkv_bwd(xs_rk, lw, k_mod, a_neg, b_kk, states, t_invs, dwkv)

    def pre_bwd(xrk_b, xlo_b, dlw_b, dkw_b, dkp_b, da_b, db_b, dg_b, drp_b, drw_b, dvp_b, dvw_b,
                w0, w2, a0, a2, g2, k_k, k_a, sg, sgt):
        f = lambda *a: _rwkv_pre(*a, sg, sgt)
        pull = jax.vjp(f, xrk_b, xlo_b, w0, w2, a0, a2, g2, k_k, k_a)[1]
        dxrk, dxlo, *dpar = pull((dlw_b, dkw_b + dkp_b, da_b, db_b, dg_b))
        direct = jnp.concatenate([drp_b + drw_b, jnp.zeros_like(drp_b), dvp_b + dvw_b], axis=1)
        return (dxrk + direct, dxlo, *dpar)

    pre_rows = [xs_rk, xs_lo, dlw, dk_w, dk_p, da_neg, db_kk, dg, dr_p, dr_w, dv_p, dv_w]
    dxs_rk, dxs_lo, gP["rwkv_w0"], dw2p, gP["rwkv_a0"], da2p, dg2p, gP["rwkv_k_k"], gP["rwkv_k_a"] = rowmap(
        pre_bwd, pre_rows, pre_params, [(RKV, F32), (LORA, F32)],
        [(1, D_MODEL), (LORA, D_MODEL), (1, D_MODEL), (LORA, D_MODEL), (LORA, D_MODEL), (1, D_MODEL), (1, D_MODEL)],
        tb=128, name="rwkv_pre_bwd")
    gW["rwkv_w2"] = dw2p[:LORA_W]
    gW["rwkv_a2"] = da2p[LORA_W:LORA_W + LORA_A]
    gW["rwkv_g2"] = dg2p[LORA_W + LORA_A:]
    dp_rk, dmu_rk = token_shift_bwd(dxs_rk, p_rk, mu_rk, tb=256, name="shift_rk_bwd")
    dp_lo, dmu_lo = token_shift_bwd(dxs_lo, p_lo, mu_lo, tb=256, name="shift_lora_bwd")
    gP["rwkv_mu"] = jnp.concatenate([dmu_rk, dmu_lo], axis=1)

    dh = matmul(dp_rk, w_rkv, "nt", "dh_rkv")
    dh = matmul(dp_lo, w_lora, "nt", "dh_lora", add=dh)
    dh = matmul(dp_qkv, w_qkv, "nt", "dh_qkv", add=dh)
    dh = matmul(dp_gate, w_gate, "nt", "dh_gate", add=dh)
    gW["w_in"] = jnp.concatenate([
        matmul(h, dp_rk, "tn", "dw_rkv"), matmul(h, dp_lo, "tn", "dw_lora"),
        matmul(h, dp_qkv, "tn", "dw_qkv"), matmul(h, dp_gate, "tn", "dw_gate")], axis=1)

    token = on_mixer_grads(gW)

    def norm_bwd(x_b, dh_b, dy_b, gn, tok):
        dx, dgn = jax.vjp(_rms, x_b, gn)[1](dh_b)
        return dy_b + dx + tok[0:1, 0:1], dgn

    dx1, gP["mix_norm"] = rowmap(norm_bwd, [x1, dh, dx2], [P["mix_norm"], token], [(D_MODEL, F32)],
                                 [(1, D_MODEL)], tb=256, name="mix_norm_bwd")
    dx, gP["ffn1_norm"], gW["ffn1_w_in"], gW["ffn1_w_out"] = _ffn_bwd(
        dx1, ffn1_saved, P["ffn1_norm"], W["ffn1_w_in"], W["ffn1_w_out"], "ffn1", on_ffn1_grads)
    return loss_cols, dx, gW, gP


N_SHARDS = 4
BIG = (("ffn1_w_in", (D_MODEL, 2 * D_FF), 1), ("ffn1_w_out", (D_FF, D_MODEL), 0),
       ("w_in", (D_MODEL, 7712), 1), ("rwkv_w2", (LORA_W, D_MODEL), 1), ("rwkv_a2", (LORA_A, D_MODEL), 1),
       ("rwkv_g2", (LORA_G, D_MODEL), 1), ("w_proj_rwkv", (D_MODEL, D_MODEL), 0),
       ("w_proj_attn", (ATTN_WIDTH, D_MODEL), 1), ("w_out", (D_MODEL, D_MODEL), 0),
       ("ffn2_w_in", (D_MODEL, 2 * D_FF), 1), ("ffn2_w_out", (D_FF, D_MODEL), 0))
SMALL = (("ffn1_norm", 1024), ("mix_norm", 1024), ("b_gate", 2048), ("rwkv_mu", 3360), ("rwkv_w0", 1024),
         ("rwkv_a0", 1024), ("rwkv_k_k", 1024), ("rwkv_k_a", 1024), ("rwkv_r_k", 1024), ("rwkv_ln_w", 1024),
         ("rwkv_ln_b", 1024), ("attn_q_norm", 64), ("attn_k_norm", 64), ("ffn2_norm", 1024))
WEIGHT_ORDER = ("ffn1_norm", "ffn1_w_in", "ffn1_w_out", "mix_norm", "w_in", "b_gate", "rwkv_mu", "rwkv_w0",
                "rwkv_w2", "rwkv_a0", "rwkv_a2", "rwkv_g2", "rwkv_k_k", "rwkv_k_a", "rwkv_r_k", "rwkv_ln_w",
                "rwkv_ln_b", "attn_q_norm", "attn_k_norm", "w_proj_rwkv", "w_proj_attn", "w_out", "ffn2_norm",
                "ffn2_w_in", "ffn2_w_out")


LORA_PARTS = ("rwkv_w2", "rwkv_a2", "rwkv_g2")
BLOCK_MAJOR = ("ffn1_w_in", "ffn2_w_in")
FIRST_FFN = ("ffn1_w_in", "ffn1_w_out")
MIXER_IN = ("w_in", "lora")
SMALL_USED = D_MODEL + sum(n for _, n in SMALL)
SMALL_W = -(-SMALL_USED // 128) * 128


def _travel():
    out = {}
    for name, shape, axis in BIG:
        if name == LORA_PARTS[0]:
            out["lora"] = ((LORA, D_MODEL), 1)
        elif name not in LORA_PARTS:
            out[name] = (shape, axis)
    return out


def local_blocks(vals):
    out = {n: vals[n] for n in _travel() if n != "lora"}
    out["lora"] = jnp.concatenate([vals[n] for n in LORA_PARTS], axis=0)
    return out


def split_lora(t):
    return {"rwkv_w2": t[:LORA_W], "rwkv_a2": t[LORA_W:LORA_W + LORA_A], "rwkv_g2": t[LORA_W + LORA_A:]}


def blocks_to_full(name, blocks):
    shape, axis = _travel()[name]
    if name in BLOCK_MAJOR:
        return blocks
    if axis == 0:
        return blocks.reshape(shape)
    return blocks.transpose(1, 0, 2).reshape(shape)


def full_to_blocks(name, full):
    shape, axis = _travel()[name]
    if name in BLOCK_MAJOR:
        return full
    if axis == 0:
        return full.reshape(N_SHARDS, shape[0] // N_SHARDS, shape[1])
    return full.reshape(shape[0], N_SHARDS, shape[1] // N_SHARDS).transpose(1, 0, 2)


def pack_small(vals, head):
    parts = [head] + [vals[name].reshape(1, n) for name, n in SMALL]
    parts.append(jnp.zeros((1, SMALL_W - SMALL_USED), F32))
    return jnp.concatenate(parts, axis=1)


def unpack_small(vec, shapes):
    out, off = {}, D_MODEL
    for name, n in SMALL:
        out[name] = vec[:, off:off + n].reshape(shapes[name])
        off += n
    return out


def _place():
    return lax.axis_index("x"), lax.axis_index("y"), lax.axis_index("c")


def _other_chips(x, y):
    return [(1 - x, y), (x, 1 - y), (1 - x, 1 - y)]


def _remote(src, dst, send_sem, recv_sem, device):
    return pltpu.make_async_remote_copy(src_ref=src, dst_ref=dst, send_sem=send_sem, recv_sem=recv_sem,
                                        device_id=device, device_id_type=MESH)


def _half(ref, who):
    hr = ref.shape[-2] // 2
    rows = pl.ds(pl.multiple_of(who * hr, 8), hr)
    return ref.at[rows] if len(ref.shape) == 2 else ref.at[:, rows]


HBM_REF = pl.BlockSpec(memory_space=pl.ANY)
COMM_PARAMS = dict(compiler_params=pltpu.CompilerParams(has_side_effects=True))


def gather_weights(blocks):
    n = len(blocks)

    def body(*refs):
        ins, outs = refs[:n], refs[n:2 * n]
        ici_send, ici_recv, d2d_send, d2d_recv = refs[2 * n:]
        x, y, c = _place()
        me, sibling, chips = 2 * x + y, (x, y, 1 - c), _other_chips(x, y)
        first = [_remote(_half(ins[t], c), _half(outs[t].at[me], c), ici_send.at[k, t], ici_recv.at[k, t],
                         (px, py, c)) for k, (px, py) in enumerate(chips) for t in range(n)]
        for cp in first:
            cp.start()
        passed = []
        for k, (px, py) in enumerate(chips):
            for t in range(n):
                landed = _half(outs[t].at[2 * px + py], c)
                _remote(landed, landed, ici_send.at[k, t], ici_recv.at[k, t], (px, py, c)).wait_recv()
                cp = _remote(landed, landed, d2d_send.at[k, t], d2d_recv.at[k, t], sibling)
                cp.start()
                passed.append(cp)
        for k, (px, py) in enumerate(chips):
            for t in range(n):
                other = _half(outs[t].at[2 * px + py], 1 - c)
                _remote(other, other, d2d_send.at[k, t], d2d_recv.at[k, t], sibling).wait_recv()
        for cp in first + passed:
            cp.wait_send()

    res = pl.pallas_call(
        body, name="gather_weights", in_specs=[HBM_REF] * n, out_specs=[HBM_REF] * n,
        out_shape=[jax.ShapeDtypeStruct((N_SHARDS,) + b.shape, b.dtype) for b in blocks],
        scratch_shapes=[pltpu.SemaphoreType.DMA((3, n))] * 4, **COMM_PARAMS)(*blocks)
    me = 2 * lax.axis_index("x") + lax.axis_index("y")
    return [lax.dynamic_update_slice(g, b[None], (me, 0, 0)) for g, b in zip(res, blocks)]


def _gather_copies(ins, outs, send_sem, recv_sem):
    x, y, c = _place()
    return [_remote(_half(ins[t], c), _half(outs[t].at[2 * x + y], c), send_sem(k, t), recv_sem(k, t), (px, py, c))
            for k, (px, py) in enumerate(_other_chips(x, y)) for t in range(len(ins))]


def gather_start(blocks, name):
    n = len(blocks)
    n_cp = 3 * n

    def body(*refs):
        ins, outs = refs[:n], refs[n:2 * n]
        sems, token = refs[2 * n:2 * n + 2 * n_cp], refs[-1]
        for cp in _gather_copies(ins, outs, lambda k, t: sems[k * n + t], lambda k, t: sems[n_cp + k * n + t]):
            cp.start()
        token[...] = jnp.zeros_like(token)

    hbm = lambda a: pltpu.with_memory_space_constraint(a, pltpu.HBM)
    landing = [lax.empty((N_SHARDS,) + b.shape, b.dtype) for b in blocks]
    res = pl.pallas_call(
        body, name=name,
        out_shape=(*[pltpu.SemaphoreType.DMA(())] * (2 * n_cp),
                   *[pltpu.HBM(a.shape, a.dtype) for a in list(blocks) + landing], jax.ShapeDtypeStruct((8, 128), F32)),
        in_specs=[SPLIT_HBM] * (2 * n),
        out_specs=(*[SPLIT_SEM] * (2 * n_cp), *[SPLIT_HBM] * (2 * n), pl.BlockSpec(memory_space=pltpu.VMEM)),
        input_output_aliases={t: 2 * n_cp + t for t in range(2 * n)}, **SPLIT_PARAMS,
    )(*[hbm(a) for a in list(blocks) + landing])
    return (n, res[:-1]), res[-1]


def gather_wait(handles, after, name):
    n, held = handles
    n_cp = 3 * n
    sems, thru = held[:2 * n_cp], held[2 * n_cp:]

    def body(*refs):
        ins, outs = refs[:n], refs[n:2 * n]
        sem_refs = refs[2 * n:2 * n + 2 * n_cp]
        for cp in _gather_copies(ins, outs, lambda k, t: sem_refs[k * n + t], lambda k, t: sem_refs[n_cp + k * n + t]):
            cp.wait_send()
            cp.wait_recv()

    res = pl.pallas_call(
        body, name=name, out_shape=tuple(pltpu.HBM(a.shape, a.dtype) for a in thru),
        in_specs=[SPLIT_HBM] * (2 * n) + [SPLIT_SEM] * (2 * n_cp) + [pl.BlockSpec(memory_space=pl.ANY)],
        out_specs=tuple([SPLIT_HBM] * (2 * n)), input_output_aliases={t: t for t in range(2 * n)}, **SPLIT_PARAMS,
    )(*thru, *sems, after)
    return list(res[n:])


def pass_halves(gathered, blocks, name):
    n = len(gathered)

    def body(*refs):
        outs = refs[n:2 * n]
        send_sems, recv_sems = refs[2 * n:]
        x, y, c = _place()
        slots = [2 * px + py for px, py in _other_chips(x, y)]
        give = [_remote(_half(outs[t].at[s], c), _half(outs[t].at[s], c), send_sems.at[k, t], recv_sems.at[k, t],
                        (x, y, 1 - c)) for k, s in enumerate(slots) for t in range(n)]
        for cp in give:
            cp.start()
        for k, s in enumerate(slots):
            for t in range(n):
                other = _half(outs[t].at[s], 1 - c)
                _remote(other, other, send_sems.at[k, t], recv_sems.at[k, t], (x, y, 1 - c)).wait_recv()
        for cp in give:
            cp.wait_send()

    res = pl.pallas_call(
        body, name=name, in_specs=[HBM_REF] * n, out_specs=[HBM_REF] * n,
        out_shape=[jax.ShapeDtypeStruct(g.shape, g.dtype) for g in gathered],
        input_output_aliases={t: t for t in range(n)},
        scratch_shapes=[pltpu.SemaphoreType.DMA((3, n))] * 2, **COMM_PARAMS)(*gathered)
    me = 2 * lax.axis_index("x") + lax.axis_index("y")
    return [lax.dynamic_update_slice(g, b[None], (me, 0, 0)) for g, b in zip(res, blocks)]


def swap_halves(grads):
    n = len(grads)

    def body(*refs):
        ins, got = refs[:n], refs[n:2 * n]
        send_sems, recv_sems = refs[2 * n:]
        x, y, c = _place()
        give = [_remote(_half(ins[t], 1 - c), got[t], send_sems.at[t], recv_sems.at[t], (x, y, 1 - c))
                for t in range(n)]
        for cp in give:
            cp.start()
        for cp in give:
            cp.wait_recv()
        for cp in give:
            cp.wait_send()

    return pl.pallas_call(
        body, name="swap_halves", in_specs=[HBM_REF] * n, out_specs=[HBM_REF] * n,
        out_shape=[jax.ShapeDtypeStruct((g.shape[0], g.shape[1] // 2, g.shape[2]), g.dtype) for g in grads],
        scratch_shapes=[pltpu.SemaphoreType.DMA((n,))] * 2, **COMM_PARAMS)(*grads)


def join_halves(blocks):
    n = len(blocks)

    def body(*refs):
        outs = refs[n:2 * n]
        send_sems, recv_sems = refs[2 * n:]
        x, y, c = _place()
        give = [_remote(_half(outs[t], c), _half(outs[t], c), send_sems.at[t], recv_sems.at[t], (x, y, 1 - c))
                for t in range(n)]
        for cp in give:
            cp.start()
        for t in range(n):
            arriving = _half(outs[t], 1 - c)
            _remote(arriving, arriving, send_sems.at[t], recv_sems.at[t], (x, y, 1 - c)).wait_recv()
        for cp in give:
            cp.wait_send()

    return pl.pallas_call(
        body, name="join_halves", in_specs=[HBM_REF] * n, out_specs=[HBM_REF] * n,
        out_shape=[jax.ShapeDtypeStruct(b.shape, b.dtype) for b in blocks],
        input_output_aliases={t: t for t in range(n)},
        scratch_shapes=[pltpu.SemaphoreType.DMA((n,))] * 2, **COMM_PARAMS)(*blocks)


SPLIT_HBM = pl.BlockSpec(memory_space=pltpu.HBM)
SPLIT_SEM = pl.BlockSpec(memory_space=pltpu.SEMAPHORE)
SPLIT_PARAMS = dict(compiler_params=pltpu.CompilerParams(has_side_effects=pltpu.SideEffectType.DATAFLOW_SIDE_EFFECTING))


def _scatter_copies(parts, landed, send_sem, recv_sem):
    x, y, c = _place()
    return [_remote(parts[t].at[2 * px + py], landed[t].at[k], send_sem(k, t), recv_sem(k, t), (px, py, c))
            for k, (px, py) in enumerate(_other_chips(x, y)) for t in range(len(parts))]


def scatter_start(partials, name):
    n = len(partials)
    n_cp = 3 * n

    def body(*refs):
        parts, landed = refs[:n], refs[n:2 * n]
        sems, token = refs[2 * n:2 * n + 2 * n_cp], refs[-1]
        for cp in _scatter_copies(parts, landed, lambda k, t: sems[k * n + t], lambda k, t: sems[n_cp + k * n + t]):
            cp.start()
        token[...] = jnp.zeros_like(token)

    hbm = lambda a: pltpu.with_memory_space_constraint(a, pltpu.HBM)
    landing = [lax.empty((3,) + p.shape[1:], p.dtype) for p in partials]
    res = pl.pallas_call(
        body, name=name,
        out_shape=(*[pltpu.SemaphoreType.DMA(())] * (2 * n_cp),
                   *[pltpu.HBM(a.shape, a.dtype) for a in partials + landing], jax.ShapeDtypeStruct((8, 128), F32)),
        in_specs=[SPLIT_HBM] * (2 * n),
        out_specs=(*[SPLIT_SEM] * (2 * n_cp), *[SPLIT_HBM] * (2 * n), pl.BlockSpec(memory_space=pltpu.VMEM)),
        input_output_aliases={t: 2 * n_cp + t for t in range(2 * n)}, **SPLIT_PARAMS,
    )(*[hbm(a) for a in partials + landing])
    return (n, res[:-1]), res[-1]


def scatter_wait(handles, after, name):
    n, held = handles
    n_cp = 3 * n
    sems, thru = held[:2 * n_cp], held[2 * n_cp:]

    def body(*refs):
        parts, landed = refs[:n], refs[n:2 * n]
        sem_refs = refs[2 * n:2 * n + 2 * n_cp]
        for cp in _scatter_copies(parts, landed, lambda k, t: sem_refs[k * n + t],
                                  lambda k, t: sem_refs[n_cp + k * n + t]):
            cp.wait_send()
            cp.wait_recv()

    res = pl.pallas_call(
        body, name=name, out_shape=tuple(pltpu.HBM(a.shape, a.dtype) for a in thru),
        in_specs=[SPLIT_HBM] * (2 * n) + [SPLIT_SEM] * (2 * n_cp) + [pl.BlockSpec(memory_space=pl.ANY)],
        out_specs=tuple([SPLIT_HBM] * (2 * n)), input_output_aliases={t: t for t in range(2 * n)}, **SPLIT_PARAMS,
    )(*thru, *sems, after)
    return list(res[n:])


def chip_sums(grads):
    names = list(grads)
    got = swap_halves([grads[n] for n in names])
    partials = []
    for name, theirs in zip(names, got):
        n_slot, hr, width = theirs.shape
        tb = _row_block(hr, width, 6)
        per_half = hr // tb
        mine = lambda i, s, per_half=per_half: (i // per_half) * 2 * per_half + s[0] * per_half + i % per_half
        p = placed_map(
            jnp.add,
            [(grads[name].reshape(2 * n_slot * hr, width), mine), (theirs.reshape(n_slot * hr, width), lambda i, s: i)],
            (n_slot * hr, width, BF16, lambda i, s: i), n_blocks=n_slot * per_half, tb=tb, name="chip_sum_" + name)
        partials.append(p.reshape(theirs.shape))
    return got, partials


def owner_sums(grads, got, landed):
    names = list(grads)
    blocks = []
    for name, theirs, arrived in zip(names, got, landed):
        n_slot, hr, width = theirs.shape
        tb = _row_block(hr, width, 6)
        per_half = hr // tb
        views = [(grads[name].reshape(2 * n_slot * hr, width),
                  lambda i, s, per_half=per_half: s[1] * 2 * per_half + s[0] * per_half + i),
                 (theirs.reshape(n_slot * hr, width), lambda i, s, per_half=per_half: s[1] * per_half + i)]
        views += [(arrived.reshape(3 * hr, width), functools.partial(lambda k, per_half, i, s: k * per_half + i,
                                                                     k, per_half)) for k in range(3)]
        f = lambda a, b, l0, l1, l2: (((a + b) + l0.astype(F32)) + l1.astype(F32)) + l2.astype(F32)
        blocks.append(placed_map(
            f, views,(2 * hr, width, F32, lambda i, s, per_half=per_half: s[0] * per_half + i),
            n_blocks=per_half, tb=tb, name="owner_sum_" + name))
    return dict(zip(names, join_halves(blocks)))


def adamw_block(name, w, g, m, v):
    rows, width = w.shape
    return rowmap(_adamw, [w, g, m, v], [], [(width, F32)] * 3, tb=_row_block(rows, width, 7),
                  name="adamw_" + name)


def reduce_small(vec, w, m, v):
    n_dev = 8

    def body(vec_ref, w_ref, m_ref, v_ref, loss_ref, g_ref, d_ref, m2_ref, v2_ref, slots, send_sems, recv_sems):
        x, y, c = _place()
        me = 4 * x + 2 * y + c
        slots[me] = vec_ref[...]
        flips = [(fx, fy, fc) for fx in (0, 1) for fy in (0, 1) for fc in (0, 1)][1:]
        peers = [(1 - x if fx else x, 1 - y if fy else y, 1 - c if fc else c) for fx, fy, fc in flips]
        sends = [pltpu.make_async_remote_copy(
            src_ref=vec_ref, dst_ref=slots.at[me], send_sem=send_sems.at[j], recv_sem=recv_sems.at[j],
            device_id=peer, device_id_type=MESH) for j, peer in enumerate(peers)]
        for cp in sends:
            cp.start()
        for j, (px, py, pc) in enumerate(peers):
            pltpu.make_async_remote_copy(
                src_ref=vec_ref, dst_ref=slots.at[4 * px + 2 * py + pc], send_sem=send_sems.at[j],
                recv_sem=recv_sems.at[j], device_id=(px, py, pc), device_id_type=MESH).wait_recv()
        for cp in sends:
            cp.wait_send()
        g = slots[0]
        for d in range(1, n_dev):
            g = g + slots[d]
        loss_ref[...] = jnp.sum(g[:, :D_MODEL], axis=1, keepdims=True)
        delta, m2, v2 = _adamw(w_ref[...], g, m_ref[...], v_ref[...])
        g_ref[...], d_ref[...], m2_ref[...], v2_ref[...] = g, delta, m2, v2

    vm = pl.BlockSpec(memory_space=pltpu.VMEM)
    vec_t = jax.ShapeDtypeStruct(vec.shape, F32)
    return pl.pallas_call(
        body, name="reduce_small", in_specs=[vm] * 4, out_specs=[vm] * 5,
        out_shape=[jax.ShapeDtypeStruct((1, 1), F32)] + [vec_t] * 4,
        scratch_shapes=[pltpu.VMEM((n_dev,) + vec.shape, F32), pltpu.SemaphoreType.DMA((n_dev - 1,)),
                        pltpu.SemaphoreType.DMA((n_dev - 1,))],
        compiler_params=pltpu.CompilerParams(has_side_effects=True),
    )(vec, w, m, v)


def kernel(x, ffn1_norm, ffn1_w_in, ffn1_w_out, mix_norm, w_in, b_gate, rwkv_mu, rwkv_w0, rwkv_w2, rwkv_a0, rwkv_a2, rwkv_g2, rwkv_k_k, rwkv_k_a, rwkv_r_k, rwkv_ln_w, rwkv_ln_b, attn_q_norm, attn_k_norm, w_proj_rwkv, w_proj_attn, w_out, ffn2_norm, ffn2_w_in, ffn2_w_out, loss_target, m_ffn1_norm, m_ffn1_w_in, m_ffn1_w_out, m_mix_norm, m_w_in, m_b_gate, m_rwkv_mu, m_rwkv_w0, m_rwkv_w2, m_rwkv_a0, m_rwkv_a2, m_rwkv_g2, m_rwkv_k_k, m_rwkv_k_a, m_rwkv_r_k, m_rwkv_ln_w, m_rwkv_ln_b, m_attn_q_norm, m_attn_k_norm, m_w_proj_rwkv, m_w_proj_attn, m_w_out, m_ffn2_norm, m_ffn2_w_in, m_ffn2_w_out, v_ffn1_norm, v_ffn1_w_in, v_ffn1_w_out, v_mix_norm, v_w_in, v_b_gate, v_rwkv_mu, v_rwkv_w0, v_rwkv_w2, v_rwkv_a0, v_rwkv_a2, v_rwkv_g2, v_rwkv_k_k, v_rwkv_k_a, v_rwkv_r_k, v_rwkv_ln_w, v_rwkv_ln_b, v_attn_q_norm, v_attn_k_norm, v_w_proj_rwkv, v_w_proj_attn, v_w_out, v_ffn2_norm, v_ffn2_w_in, v_ffn2_w_out):
    given = dict(locals())
    weights = {n: given[n] for n in WEIGHT_ORDER}
    mom_m = {n: given["m_" + n] for n in WEIGHT_ORDER}
    mom_v = {n: given["v_" + n] for n in WEIGHT_ORDER}
    big = [name for name, _, _ in BIG]
    shapes = {n: weights[n].shape for n in WEIGHT_ORDER}
    blocks_of = lambda d: local_blocks({n: d[n][0] for n in big})
    w_blk, m_blk, v_blk = blocks_of(weights), blocks_of(mom_m), blocks_of(mom_v)
    names = list(w_blk)

    early = [n for n in names if n not in FIRST_FFN]
    bf16_block = lambda n: w_blk[n].astype(BF16)
    W = {n: blocks_to_full(n, g) for n, g in zip(FIRST_FFN, gather_weights([bf16_block(n) for n in FIRST_FFN]))}
    stages = {"mixer": [n for n in early if n in MIXER_IN], "out": [n for n in early if n not in MIXER_IN]}
    stage_blocks = {s: [bf16_block(n) for n in stages[s]] for s in stages}
    started = {s: gather_start(stage_blocks[s], "gather_start_" + s) for s in ("mixer", "out")}
    start_token = started["mixer"][1] + started["out"][1]

    def more_weights(stage, after):
        landed = gather_wait(started[stage][0], after, "gather_wait_" + stage)
        got = pass_halves(landed, stage_blocks[stage], "pass_halves_" + stage)
        more = {n: blocks_to_full(n, g) for n, g in zip(stages[stage], got)}
        if "lora" in more:
            more.update(split_lora(more.pop("lora")))
        return more

    P = {n: weights[n].reshape(1, -1) for n, _ in SMALL}

    sent = {}

    def send_early(gw):
        lora = jnp.concatenate([gw[n] for n in LORA_PARTS], axis=0)
        sent["grads"] = {n: full_to_blocks(n, lora if n == "lora" else gw[n]) for n in early}
        sent["got"], partials = chip_sums(sent["grads"])
        sent["handles"], token = scatter_start(partials, "scatter_start")
        return token

    def send_late(dw_in, dw_out):
        sent["late"] = {n: full_to_blocks(n, g) for n, g in zip(FIRST_FFN, (dw_in, dw_out))}
        sent["late_got"], partials = chip_sums(sent["late"])
        sent["late_handles"], token = scatter_start(partials, "scatter_start_ffn1")
        return token

    loss_cols, dx, gW, gP = layer_step(x[0], loss_target[0], W, P, start_token, more_weights, send_early, send_late)
    landed = scatter_wait(sent["handles"], gP["ffn1_norm"], "scatter_wait")
    out_g, out_d, out_m, out_v = {}, {}, {}, {}

    def apply(g_blk):
        for n in g_blk:
            res = (g_blk[n], *adamw_block(n, w_blk[n], g_blk[n], m_blk[n], v_blk[n]))
            for dst, t in zip((out_g, out_d, out_m, out_v), res):
                for part, val in (split_lora(t) if n == "lora" else {n: t}).items():
                    dst[part] = val.reshape(shapes[part])

    apply(owner_sums(sent["grads"], sent["got"], landed))
    late_landed = scatter_wait(sent["late_handles"], list(out_d.values())[-1], "scatter_wait_ffn1")
    apply(owner_sums(sent["late"], sent["late_got"], late_landed))

    zero_head = jnp.zeros((1, D_MODEL), F32)
    vec = pack_small(gP, loss_cols)
    loss, g_s, d_s, m_s, v_s = reduce_small(
        vec, pack_small({n: weights[n] for n, _ in SMALL}, zero_head),
        pack_small({n: mom_m[n] for n, _ in SMALL}, zero_head),
        pack_small({n: mom_v[n] for n, _ in SMALL}, zero_head))
    for dst, src in ((out_g, g_s), (out_d, d_s), (out_m, m_s), (out_v, v_s)):
        dst.update(unpack_small(src, shapes))

    return (loss[0, 0], dx[None], *[out_g[n] for n in WEIGHT_ORDER], *[out_d[n] for n in WEIGHT_ORDER],
            *[out_m[n] for n in WEIGHT_ORDER], *[out_v[n] for n in WEIGHT_ORDER])
```

```python
import functools

import jax
import jax.numpy as jnp
from jax import lax
from jax.experimental import pallas as pl
from jax.experimental.pallas import tpu as pltpu

F32 = jnp.float32
BF16 = jnp.bfloat16
MESH = pl.DeviceIdType.MESH

D_MODEL = 1024
HEAD_DIM = 64
RWKV_HEADS = 16
LORA_W, LORA_A, LORA_G = 64, 64, 160
LORA = LORA_W + LORA_A + LORA_G
RKV = 3 * D_MODEL
ATTN_PAIRS = ((128, 1), (512, 4), (2048, 16))
ATTN_BLK = 128
ATTN_HPG = 4
ATTN_WIDTH = 768
GROUP_W = ATTN_HPG * HEAD_DIM
D_FF = 2816
GN_EPS = 64e-5
RMS_EPS = 1e-6
NEG_INF = -1e30
WKV_CHUNK = 64
WKV_HEADS_PER_STEP = 16

ADAM_LR, ADAM_B1, ADAM_B2, ADAM_EPS, ADAM_WD, ADAM_STEP = 0.001, 0.9, 0.999, 1e-08, 0.01, 10

V7X_VMEM_BYTES = 64 << 20
VMEM_TEMP_ALLOWANCE = 20 << 20
VMEM_LEFT_FREE = 6 << 20


def _cparams(sem, block_bytes):
    limit = min(2 * block_bytes + VMEM_TEMP_ALLOWANCE, V7X_VMEM_BYTES - VMEM_LEFT_FREE)
    return pltpu.CompilerParams(dimension_semantics=sem, vmem_limit_bytes=int(limit))


def _nbytes(shape, dtype):
    n = 1
    for s in shape:
        n *= s
    return n * jnp.dtype(dtype).itemsize


def _split_bf16(a):
    hi = a.astype(BF16)
    return hi, (a - hi.astype(F32)).astype(BF16)


def _make_dots():
    def raw(a, b, ca, cb):
        return lax.dot_general(a.astype(BF16), b.astype(BF16), (((ca,), (cb,)), ((), ())),
                               preferred_element_type=F32)

    @jax.custom_vjp
    def nn(a, b):
        return raw(a, b, 1, 0)

    @jax.custom_vjp
    def nt(a, b):
        return raw(a, b, 1, 1)

    @jax.custom_vjp
    def tn(a, b):
        return raw(a, b, 0, 0)

    nn.defvjp(lambda a, b: (raw(a, b, 1, 0), (a, b)),
              lambda res, g: (raw(g, res[1], 1, 1), raw(res[0], g, 0, 0)))
    nt.defvjp(lambda a, b: (raw(a, b, 1, 1), (a, b)),
              lambda res, g: (raw(g, res[1], 1, 0), raw(g, res[0], 0, 0)))
    tn.defvjp(lambda a, b: (raw(a, b, 0, 0), (a, b)),
              lambda res, g: (raw(res[1], g, 1, 1), raw(res[0], g, 1, 0)))
    return nn, nt, tn


def _exact_rhs_dot(x, ones, cx, co):
    hi, lo = _split_bf16(x)
    dims = (((cx,), (co,)), ((), ()))
    return (lax.dot_general(hi, ones, dims, preferred_element_type=F32)
            + lax.dot_general(lo, ones, dims, preferred_element_type=F32))


@jax.custom_vjp
def SEG(x, ones):
    return _exact_rhs_dot(x, ones, 1, 0)


SEG.defvjp(lambda x, ones: (_exact_rhs_dot(x, ones, 1, 0), ones),
           lambda ones, g: (_exact_rhs_dot(g, ones, 1, 1), jnp.zeros_like(ones)))

NN, NT, TN = _make_dots()


MM_TILE_M, MM_TILE_N, MM_TILE_K = 1408, 1408, 1536


def _pick(n, cap):
    best = None
    for t in range(128, min(n, cap) + 1, 128):
        if n % t == 0:
            best = t
    return best or n


def matmul(a, b, mode, name, *, add=None, scale=1.0, out_dtype=F32):
    if mode == "nn":
        (M, K), (K2, N) = a.shape, b.shape
    elif mode == "nt":
        (M, K), (N, K2) = a.shape, b.shape
    else:
        (K, M), (K2, N) = a.shape, b.shape
    assert K == K2, (name, a.shape, b.shape)
    tm, tn, tk = _pick(M, MM_TILE_M), _pick(N, MM_TILE_N), _pick(K, MM_TILE_K)
    nk = K // tk
    ca, cb = {"nn": (1, 0), "nt": (1, 1), "tn": (0, 0)}[mode]

    def body(*refs):
        if add is None:
            a_ref, b_ref, o_ref, acc_ref = refs
        else:
            a_ref, b_ref, add_ref, o_ref, acc_ref = refs
        k = pl.program_id(2)

        @pl.when(k == 0)
        def _():
            acc_ref[...] = jnp.zeros_like(acc_ref)

        acc_ref[...] += lax.dot_general(a_ref[...].astype(BF16), b_ref[...].astype(BF16),
                                        (((ca,), (cb,)), ((), ())), preferred_element_type=F32)

        @pl.when(k == nk - 1)
        def _():
            r = acc_ref[...] * scale
            if add is not None:
                r = add_ref[...] + r
            o_ref[...] = r.astype(o_ref.dtype)

    a_spec = (pl.BlockSpec((tk, tm), lambda i, j, k: (k, i)) if mode == "tn"
              else pl.BlockSpec((tm, tk), lambda i, j, k: (i, k)))
    b_spec = (pl.BlockSpec((tn, tk), lambda i, j, k: (j, k)) if mode == "nt"
              else pl.BlockSpec((tk, tn), lambda i, j, k: (k, j)))
    in_specs, args = [a_spec, b_spec], [a, b]
    blk = tm * tk * a.dtype.itemsize + tk * tn * b.dtype.itemsize + tm * tn * 8
    if add is not None:
        in_specs.append(pl.BlockSpec((tm, tn), lambda i, j, k: (i, j)))
        args.append(add)
        blk += tm * tn * 4
    return pl.pallas_call(
        body, name=name, grid=(M // tm, N // tn, nk),
        in_specs=in_specs, out_specs=pl.BlockSpec((tm, tn), lambda i, j, k: (i, j)),
        out_shape=jax.ShapeDtypeStruct((M, N), out_dtype),
        scratch_shapes=[pltpu.VMEM((tm, tn), F32)],
        compiler_params=_cparams(("parallel", "parallel", "arbitrary"), blk),
    )(*args)


def matmul_cs(a, w, mode, name, token):
    n_blk = N_SHARDS
    if mode == "tn":
        (K, R), Cs = a.shape, w.shape[2] // 2
        tm, tk = _pick(R, MM_TILE_M), _pick(K, 1024)
        grid = (R // tm, n_blk, K // tk)
        a_spec = pl.BlockSpec((tk, tm), lambda i, j, k: (k, i))
        w_spec = pl.BlockSpec((None, tk, Cs), lambda i, j, k: (j // 2, k, j % 2))
        o_spec = pl.BlockSpec((None, tm, Cs), lambda i, j, k: (j, i, 0))
        out_shape, acc_shape, dims = (n_blk, R, Cs), (tm, Cs), (0, 0)
        blk = tk * tm * a.dtype.itemsize + tk * Cs * w.dtype.itemsize + tm * Cs * 8
    else:
        M, (_, R, Cs) = a.shape[1], w.shape
        tm, tn = _pick(M, MM_TILE_M), _pick(R, MM_TILE_N)
        grid = (M // tm, R // tn, n_blk)
        a_spec = pl.BlockSpec((None, tm, Cs), lambda i, j, k: (k // 2, i, k % 2))
        w_spec = pl.BlockSpec((None, tn, Cs), lambda i, j, k: (k, j, 0))
        o_spec = pl.BlockSpec((tm, tn), lambda i, j, k: (i, j))
        out_shape, acc_shape, dims = (M, R), (tm, tn), (1, 1)
        blk = tm * Cs * a.dtype.itemsize + tn * Cs * w.dtype.itemsize + tm * tn * 8
    nk = grid[2]

    def body(a_ref, w_ref, tok_ref, o_ref, acc_ref):
        k = pl.program_id(2)

        @pl.when(k == 0)
        def _():
            acc_ref[...] = jnp.zeros_like(acc_ref)

        acc_ref[...] += lax.dot_general(a_ref[...].astype(BF16), w_ref[...].astype(BF16),
                                        (((dims[0],), (dims[1],)), ((), ())), preferred_element_type=F32)

        @pl.when(k == nk - 1)
        def _():
            o_ref[...] = acc_ref[...] + tok_ref[0:1, 0:1]

    return pl.pallas_call(
        body, name=name, grid=grid, in_specs=[a_spec, w_spec, pl.BlockSpec(token.shape, lambda i, j, k: (0, 0))],
        out_specs=o_spec, out_shape=jax.ShapeDtypeStruct(out_shape, F32), scratch_shapes=[pltpu.VMEM(acc_shape, F32)],
        compiler_params=_cparams(("parallel", "parallel", "arbitrary"), blk),
    )(a, w, token)


FFN_TILE_M = 512


def _swiglu(gate, up):
    return gate * jax.nn.sigmoid(gate) * up


def ffn_in_act(h, w, name):
    (M, R), Cs, half = h.shape, w.shape[2], N_SHARDS // 2
    tm, tk = _pick(M, FFN_TILE_M), _pick(R, 1024)
    nk = R // tk

    def body(h_ref, wg_ref, wu_ref, gu_ref, act_ref, acc_ref):
        k = pl.program_id(2)

        @pl.when(k == 0)
        def _():
            acc_ref[...] = jnp.zeros_like(acc_ref)

        hb = h_ref[...].astype(BF16)
        for part, w_ref in enumerate((wg_ref, wu_ref)):
            acc_ref[part] += jnp.dot(hb, w_ref[...].astype(BF16), preferred_element_type=F32)

        @pl.when(k == nk - 1)
        def _():
            gu_ref[...] = acc_ref[...]
            act_ref[...] = _swiglu(acc_ref[0], acc_ref[1]).astype(act_ref.dtype)

    w_spec = lambda off: pl.BlockSpec((None, tk, Cs), functools.partial(lambda off, j, i, k: (j + off, k, 0), off))
    blk = tm * tk * h.dtype.itemsize + 2 * tk * Cs * w.dtype.itemsize + tm * Cs * (16 + 2)
    return pl.pallas_call(
        body, name=name, grid=(half, M // tm, nk),
        in_specs=[pl.BlockSpec((tm, tk), lambda j, i, k: (i, k)), w_spec(0), w_spec(half)],
        out_specs=[pl.BlockSpec((2, tm, Cs), lambda j, i, k: (0, i, j)), pl.BlockSpec((tm, Cs), lambda j, i, k: (i, j))],
        out_shape=[jax.ShapeDtypeStruct((2, M, half * Cs), F32), jax.ShapeDtypeStruct((M, half * Cs), BF16)],
        scratch_shapes=[pltpu.VMEM((2, tm, Cs), F32)],
        compiler_params=_cparams(("parallel", "parallel", "arbitrary"), blk),
    )(h, w, w)


def ffn_dact_dgu(dy, w_out, gu, scale, name):
    (M, D), F = dy.shape, w_out.shape[0]
    tm, tn = _pick(M, FFN_TILE_M), F // 2

    def body(dy_ref, w_ref, gu_ref, dgu_ref):
        dact = scale * lax.dot_general(dy_ref[...].astype(BF16), w_ref[...].astype(BF16),
                                       (((1,), (1,)), ((), ())), preferred_element_type=F32)
        dgate, dup = jax.vjp(_swiglu, gu_ref[0], gu_ref[1])[1](dact)
        dgu_ref[0] = dgate.astype(dgu_ref.dtype)
        dgu_ref[1] = dup.astype(dgu_ref.dtype)

    pair = pl.BlockSpec((2, tm, tn), lambda j, i: (0, i, j))
    blk = tm * D * dy.dtype.itemsize + tn * D * w_out.dtype.itemsize + 2 * tm * tn * (4 + 2)
    return pl.pallas_call(
        body, name=name, grid=(F // tn, M // tm),
        in_specs=[pl.BlockSpec((tm, D), lambda j, i: (i, 0)), pl.BlockSpec((tn, D), lambda j, i: (j, 0)), pair],
        out_specs=pair, out_shape=jax.ShapeDtypeStruct((2, M, F), BF16),
        compiler_params=_cparams(("parallel", "parallel"), blk),
    )(dy, w_out, gu)


def _row_block(n, width, n_arrays):
    cap = (V7X_VMEM_BYTES // 4) // (2 * 4 * width * n_arrays)
    best = None
    for t in range(16, min(n, cap) + 1, 16):
        if n % t == 0:
            best = t
    return best or n


def placed_map(f, ins, out, *, n_blocks, tb, name):
    def body(*refs):
        refs[-1][...] = f(*[r[...] for r in refs[:-1]]).astype(refs[-1].dtype)

    def spec(fn):
        def index(i):
            x, y, c = _place()
            return fn(i, (c, 2 * x + y)), 0
        return pl.BlockSpec((tb, width), index)

    o_rows, width, o_dtype, o_fn = out
    blk = (sum(a.dtype.itemsize for a, _ in ins) + jnp.dtype(o_dtype).itemsize) * tb * width
    return pl.pallas_call(
        body, name=name, grid=(n_blocks,), in_specs=[spec(fn) for _, fn in ins], out_specs=spec(o_fn),
        out_shape=jax.ShapeDtypeStruct((o_rows, width), o_dtype),
        compiler_params=_cparams(("parallel",), blk),
    )(*[a for a, _ in ins])


def rowmap(f, rows, params, outs, accs=(), *, tb, name):
    rows = [r if isinstance(r, tuple) else (r, r.shape[1], 0) for r in rows]
    S = rows[0][0].shape[0]
    assert S % tb == 0, (name, S, tb)
    n_in, n_out = len(rows) + len(params), len(outs)

    def body(*refs):
        res = f(*[r[...] for r in refs[:n_in]])
        res = res if isinstance(res, (tuple, list)) else (res,)
        o_refs, a_refs = refs[n_in:n_in + n_out], refs[n_in + n_out:]
        for ref, val in zip(o_refs, res[:n_out]):
            ref[...] = val.astype(ref.dtype)
        if a_refs:
            @pl.when(pl.program_id(0) == 0)
            def _():
                for ref in a_refs:
                    ref[...] = jnp.zeros_like(ref)

            for ref, val in zip(a_refs, res[n_out:]):
                ref[...] += val.astype(F32)

    in_specs = [pl.BlockSpec((tb, w), functools.partial(lambda cb, i: (i, cb), cb)) for _, w, cb in rows]
    in_specs += [pl.BlockSpec(p.shape, lambda i: (0, 0)) for p in params]
    out_specs = [pl.BlockSpec((tb, w), lambda i: (i, 0)) for w, _ in outs]
    out_specs += [pl.BlockSpec(tuple(s), lambda i: (0, 0)) for s in accs]
    out_shape = [jax.ShapeDtypeStruct((S, w), dt) for w, dt in outs]
    out_shape += [jax.ShapeDtypeStruct(tuple(s), F32) for s in accs]
    blk = sum(tb * w * a.dtype.itemsize for a, w, _ in rows) + sum(_nbytes(p.shape, p.dtype) for p in params)
    blk += sum(_nbytes((tb, w), dt) for w, dt in outs) + sum(_nbytes(s, F32) for s in accs)
    res = pl.pallas_call(
        body, name=name, grid=(S // tb,), in_specs=in_specs, out_specs=out_specs, out_shape=out_shape,
        compiler_params=_cparams(("arbitrary",) if accs else ("parallel",), blk),
    )(*[r[0] for r in rows], *[pltpu.with_memory_space_constraint(p, pltpu.HBM) for p in params])
    return res


def _rms(x, g):
    return x * lax.rsqrt(jnp.mean(x * x, axis=-1, keepdims=True) + RMS_EPS) * g


def _softplus(z):
    return jnp.maximum(z, 0.0) + jnp.log(1.0 + jnp.exp(-jnp.abs(z)))


def _rwkv_pre(xrk, xlo, w0, w2p, a0, a2p, g2p, k_k, k_a, seg, seg_t):
    k = xrk[:, D_MODEL:2 * D_MODEL]
    w = -_softplus(-(w0 + NN(jnp.tanh(xlo), w2p))) - 0.5
    log_decay = -jnp.exp(w)
    a = jax.nn.sigmoid(a0 + NN(xlo, a2p))
    g = NN(jax.nn.sigmoid(xlo), g2p)
    kk = k * k_k
    norm = jnp.maximum(jnp.sqrt(SEG(kk * kk, seg)), 1e-12)
    kk = kk * SEG(1.0 / norm, seg_t)
    k_mod = k * (1.0 + (a - 1.0) * k_a)
    return log_decay, k_mod, -kk, kk * a, g


def _rwkv_post(wkv, r, k_mod, v, g, r_k, ln_w, ln_b, seg, seg_t):
    inv_n = 1.0 / HEAD_DIM
    mean = SEG(wkv, seg) * inv_n
    cen = wkv - SEG(mean, seg_t)
    var = SEG(cen * cen, seg) * inv_n
    y = cen * SEG(lax.rsqrt(var + GN_EPS), seg_t) * ln_w + ln_b
    bonus = SEG(SEG(r * k_mod * r_k, seg), seg_t) * v
    return (y + bonus) * g


def _qk_norm(q, k, q_gain, k_gain, seg, seg_t, tile_t):
    def norm(x, gain):
        mean_sq = SEG(x * x, seg) * (1.0 / HEAD_DIM)
        return x * SEG(lax.rsqrt(mean_sq + RMS_EPS), seg_t) * SEG(gain, tile_t)

    return norm(q, q_gain) * (HEAD_DIM ** -0.5), norm(k, k_gain)


def _gate_merge(pgate, pa, pb, b_gate):
    sg = jax.nn.sigmoid(pgate + b_gate)
    return sg[:, :D_MODEL] * pa + sg[:, D_MODEL:] * pb


def _group_combine(o0, o1, o2, l0, l1, l2):
    m = jnp.maximum(jnp.maximum(l0, l1), l2)
    es = [jnp.exp(l - m) for l in (l0, l1, l2)]
    den = es[0] + es[1] + es[2]
    return jnp.concatenate([o * (e / den) for o, e in zip((o0, o1, o2), es)], axis=1)


def _each(f, *xs):
    return tuple(f(*args) for args in zip(*xs))


def _attn_block(q, kc, kp, vc, vp, first):
    qi = lax.broadcasted_iota(jnp.int32, (ATTN_BLK, ATTN_BLK), 0)
    kj = lax.broadcasted_iota(jnp.int32, (ATTN_BLK, ATTN_BLK), 1)
    own = kj <= qi
    s_c = _each(lambda a, b: jnp.where(own, NT(a, b), NEG_INF), q, kc)
    s_p = _each(lambda a, b, f: jnp.where((kj >= qi) & (f < 0.5), NT(a, b), NEG_INF), q, kp, first)
    row_max = lambda s: jnp.max(s, axis=-1, keepdims=True)
    row_sum = lambda s: jnp.sum(s, axis=-1, keepdims=True)
    m = _each(lambda c_, p_: jnp.maximum(row_max(c_), row_max(p_)), s_c, s_p)
    e_c, e_p = _each(lambda s, m_: jnp.exp(s - m_), s_c, m), _each(lambda s, m_: jnp.exp(s - m_), s_p, m)
    den = _each(lambda c_, p_: row_sum(c_) + row_sum(p_), e_c, e_p)
    inv = _each(lambda d_: 1.0 / d_, den)
    o = _each(lambda ec, ep, i_, vc_, vp_: (NN(ec, vc_) + NN(ep, vp_)) * i_, e_c, e_p, inv, vc, vp)
    lse = _each(lambda m_, d_: jnp.broadcast_to(m_ + jnp.log(d_), (ATTN_BLK, HEAD_DIM)), m, den)
    return o, lse


def _attn_pair(q, k, k_before, v, v_before, first):
    n = len(q[0])
    o, lse = _attn_block(q[0] + q[1], k[0] + k[1], k_before + k[0], v[0] + v[1], v_before + v[0],
                         (first[0],) * n + (first[1],) * n)
    return (o[:n], o[n:]), (lse[:n], lse[n:])


TRI_SEED = 8


def _tri_inverse(n):
    c = n[0].shape[0]
    row = lax.broadcasted_iota(jnp.int32, (c, c), 0)
    col = lax.broadcasted_iota(jnp.int32, (c, c), 1)
    same_block = lambda size: (row >> (size.bit_length() - 1)) == (col >> (size.bit_length() - 1))
    seed = same_block(TRI_SEED)
    p = _each(lambda m: jnp.where(seed, m, 0.0), n)
    t, span = _each(lambda m: (row == col).astype(F32) + m, p), 2
    while span < TRI_SEED:
        p = _each(NN, p, p)
        t = _each(lambda t_, p_: t_ + NN(t_, p_), t, p)
        span *= 2
    size = TRI_SEED
    while size < c:
        joins = same_block(2 * size) & jnp.logical_not(same_block(size))
        t = _each(lambda t_, m: t_ + NN(NN(t_, jnp.where(joins, m, 0.0)), t_), t, n)
        size *= 2
    return t


@jax.custom_vjp
def _tri_solve(n, rhs, t):
    return _each(NN, t, rhs)


def _tri_solve_fwd(n, rhs, t):
    x = _each(NN, t, rhs)
    return x, (t, x)


def _tri_solve_bwd(res, dx):
    t, x = res
    drhs = _each(TN, t, dx)
    return _each(NT, drhs, x), drhs, _each(jnp.zeros_like, t)


_tri_solve.defvjp(_tri_solve_fwd, _tri_solve_bwd)


def _lower_ones(c):
    row = lax.broadcasted_iota(jnp.int32, (c, c), 0)
    col = lax.broadcasted_iota(jnp.int32, (c, c), 1)
    return (row >= col).astype(BF16)


def _ones_dot(ones, x, contract):
    hi, lo = _split_bf16(x)
    dims = (((contract,), (0,)), ((), ()))
    return (lax.dot_general(ones, hi, dims, preferred_element_type=F32)
            + lax.dot_general(ones, lo, dims, preferred_element_type=F32))


@jax.custom_vjp
def _cumsum_rows(x):
    return _ones_dot(_lower_ones(x.shape[0]), x, 1)


_cumsum_rows.defvjp(lambda x: (_ones_dot(_lower_ones(x.shape[0]), x, 1), None),
                    lambda _, g: (_ones_dot(_lower_ones(g.shape[0]), g, 0),))


def _wkv_chunk(s0, r, lw, k, v, a, b, t_inv=None):
    c = r[0].shape[0]
    row = lax.broadcasted_iota(jnp.int32, (c, c), 0)
    col = lax.broadcasted_iota(jnp.int32, (c, c), 1)
    strict, incl = row > col, row >= col
    cat = lambda p, q: jnp.concatenate([p, q], axis=0)
    cum = _each(_cumsum_rows, lw)
    e_neg = _each(lambda c_: jnp.exp(-c_), cum)
    ar = _each(lambda a_, r_, c_, l_: cat(a_ * jnp.exp(c_ - l_), r_ * jnp.exp(c_)), a, r, cum, lw)
    b_t, k_t = _each(jnp.multiply, b, e_neg), _each(jnp.multiply, k, e_neg)
    p_b, p_k, p_s = _each(NT, ar, b_t), _each(NT, ar, k_t), _each(NT, ar, s0)
    n_ab = _each(lambda p: jnp.where(strict, p[:c], 0.0), p_b)
    m_rb = _each(lambda p: jnp.where(incl, p[c:], 0.0), p_b)
    n_ak = _each(lambda p: jnp.where(strict, p[:c], 0.0), p_k)
    m_rk = _each(lambda p: jnp.where(incl, p[c:], 0.0), p_k)
    if t_inv is None:
        t_inv = _tri_inverse(n_ab)
    u = _tri_solve(n_ab, _each(lambda p, n_, v_: p[:c] + NN(n_, v_), p_s, n_ak, v), t_inv)
    y = _each(lambda p, mb, u_, mk, v_: p[c:] + NN(mb, u_) + NN(mk, v_), p_s, m_rb, u, m_rk, v)
    g_end = _each(lambda l_: jnp.exp(jnp.sum(l_, axis=0, keepdims=True)), lw)
    s1 = _each(lambda s_, g_, u_, v_, b_, k_: s_ * g_ + TN(cat(u_, v_), cat(b_, k_) * g_),
               s0, g_end, u, v, b_t, k_t)
    return y, s1, t_inv


def _adamw(w, g, m, v):
    m = ADAM_B1 * m + (1.0 - ADAM_B1) * g
    v = ADAM_B2 * v + (1.0 - ADAM_B2) * jnp.square(g)
    m_hat = m / (1.0 - ADAM_B1 ** ADAM_STEP)
    v_hat = v / (1.0 - ADAM_B2 ** ADAM_STEP)
    delta = -ADAM_LR * (m_hat / (jnp.sqrt(v_hat) + ADAM_EPS) + ADAM_WD * w)
    return delta, m, v


def token_shift_fwd(p, mu, *, tb, name):
    S, W = p.shape
    hb = tb // 8

    def body(p_ref, halo_ref, mu_ref, o_ref):
        i = pl.program_id(0)
        x = p_ref[...]
        before = halo_ref[7:8, :] * (i > 0).astype(F32)
        row = lax.broadcasted_iota(jnp.int32, (tb, W), 0)
        prev = jnp.where(row == 0, before, pltpu.roll(x, 1, 0))
        o_ref[...] = x + (prev - x) * mu_ref[...]

    blk = (2 * tb + 8) * W * 4
    return pl.pallas_call(
        body, name=name, grid=(S // tb,),
        in_specs=[pl.BlockSpec((tb, W), lambda i: (i, 0)),
                  pl.BlockSpec((8, W), lambda i: (jnp.maximum(i * hb - 1, 0), 0)),
                  pl.BlockSpec((1, W), lambda i: (0, 0))],
        out_specs=pl.BlockSpec((tb, W), lambda i: (i, 0)),
        out_shape=jax.ShapeDtypeStruct((S, W), F32),
        compiler_params=_cparams(("parallel",), blk),
    )(p, p, mu)


def token_shift_bwd(dxs, p, mu, *, tb, name):
    S, W = p.shape
    hb, nb = tb // 8, S // tb

    def body(d_ref, dnext_ref, p_ref, halo_ref, mu_ref, dp_ref, dmu_ref):
        i = pl.program_id(0)
        d, x, mu_v = d_ref[...], p_ref[...], mu_ref[...]
        row = lax.broadcasted_iota(jnp.int32, (tb, W), 0)
        before = halo_ref[7:8, :] * (i > 0).astype(F32)
        prev = jnp.where(row == 0, before, pltpu.roll(x, 1, 0))
        t = d * mu_v
        after = dnext_ref[0:1, :] * mu_v * (i < nb - 1).astype(F32)
        nxt = jnp.where(row == tb - 1, after, pltpu.roll(t, tb - 1, 0))
        dp_ref[...] = (d - t + nxt).astype(dp_ref.dtype)

        @pl.when(i == 0)
        def _():
            dmu_ref[...] = jnp.zeros_like(dmu_ref)

        dmu_ref[...] += jnp.sum(d * (prev - x), axis=0, keepdims=True)

    blk = (3 * tb + 16) * W * 4
    return pl.pallas_call(
        body, name=name, grid=(nb,),
        in_specs=[pl.BlockSpec((tb, W), lambda i: (i, 0)),
                  pl.BlockSpec((8, W), lambda i: (jnp.minimum((i + 1) * hb, S // 8 - 1), 0)),
                  pl.BlockSpec((tb, W), lambda i: (i, 0)),
                  pl.BlockSpec((8, W), lambda i: (jnp.maximum(i * hb - 1, 0), 0)),
                  pl.BlockSpec((1, W), lambda i: (0, 0))],
        out_specs=[pl.BlockSpec((tb, W), lambda i: (i, 0)), pl.BlockSpec((1, W), lambda i: (0, 0))],
        out_shape=[jax.ShapeDtypeStruct((S, W), BF16), jax.ShapeDtypeStruct((1, W), F32)],
        compiler_params=_cparams(("arbitrary",), blk),
    )(dxs, dxs, p, p, mu)


def _head_cols(h):
    return pl.ds(h * HEAD_DIM, HEAD_DIM)


def wkv_fwd(xs_rk, lw, k, a, b):
    S = lw.shape[0]
    C, nc, G, N = WKV_CHUNK, S // WKV_CHUNK, WKV_HEADS_PER_STEP, HEAD_DIM

    def body(r_ref, lw_ref, k_ref, v_ref, a_ref, b_ref, y_ref, st_ref, ti_ref, state):
        @pl.when(pl.program_id(1) == 0)
        def _():
            state[...] = jnp.zeros_like(state)

        heads = lambda ref: tuple(ref[:, _head_cols(h)] for h in range(G))
        s0 = tuple(state[h] for h in range(G))
        y, s1, t_inv = _wkv_chunk(s0, heads(r_ref), heads(lw_ref), heads(k_ref), heads(v_ref), heads(a_ref),
                                  heads(b_ref))
        for h in range(G):
            st_ref[h] = s0[h]
            ti_ref[h] = t_inv[h]
            y_ref[:, _head_cols(h)] = y[h]
            state[h] = s1[h]

    W = G * N
    seq = lambda j: pl.BlockSpec((C, W), functools.partial(lambda j, g, c: (c, j + g), j))
    per = D_MODEL // W
    per_chunk = pl.BlockSpec((None, G, N, N), lambda g, c: (c, g, 0, 0))
    return pl.pallas_call(
        body, name="wkv_fwd", grid=(RWKV_HEADS // G, nc),
        in_specs=[seq(0), seq(0), seq(0), seq(2 * per), seq(0), seq(0)],
        out_specs=[seq(0), per_chunk, per_chunk],
        out_shape=[jax.ShapeDtypeStruct((S, D_MODEL), F32)] + [jax.ShapeDtypeStruct((nc, RWKV_HEADS, N, N), F32)] * 2,
        scratch_shapes=[pltpu.VMEM((G, N, N), F32)],
        compiler_params=_cparams(("parallel", "arbitrary"), 8 * C * W * 4 + 3 * G * N * N * 4),
    )(xs_rk, lw, k, xs_rk, a, b)


def wkv_bwd(xs_rk, lw, k, a, b, states, t_invs, dy):
    S = lw.shape[0]
    C, nc, G, N = WKV_CHUNK, S // WKV_CHUNK, WKV_HEADS_PER_STEP, HEAD_DIM

    def body(r_ref, lw_ref, k_ref, v_ref, a_ref, b_ref, st_ref, ti_ref, dy_ref,
             dr_ref, dlw_ref, dk_ref, dv_ref, da_ref, db_ref, dstate):
        @pl.when(pl.program_id(1) == 0)
        def _():
            dstate[...] = jnp.zeros_like(dstate)

        heads = lambda ref: tuple(ref[:, _head_cols(h)] for h in range(G))
        t_inv = tuple(ti_ref[h] for h in range(G))
        chunk = lambda *args: _wkv_chunk(*args, t_inv)[:2]
        _, pull = jax.vjp(chunk, tuple(st_ref[h] for h in range(G)), heads(r_ref), heads(lw_ref),
                          heads(k_ref), heads(v_ref), heads(a_ref), heads(b_ref))
        ds0, *grads = pull((heads(dy_ref), tuple(dstate[h] for h in range(G))))
        for h in range(G):
            dstate[h] = ds0[h]
            for ref, grad in zip((dr_ref, dlw_ref, dk_ref, dv_ref, da_ref, db_ref), grads):
                ref[:, _head_cols(h)] = grad[h]

    W = G * N
    seq = lambda j: pl.BlockSpec((C, W), functools.partial(lambda j, g, c: (nc - 1 - c, j + g), j))
    per = D_MODEL // W
    st = pl.BlockSpec((None, G, N, N), lambda g, c: (nc - 1 - c, g, 0, 0))
    return pl.pallas_call(
        body, name="wkv_bwd", grid=(RWKV_HEADS // G, nc),
        in_specs=[seq(0), seq(0), seq(0), seq(2 * per), seq(0), seq(0), st, st, seq(0)],
        out_specs=[seq(0)] * 6, out_shape=[jax.ShapeDtypeStruct((S, D_MODEL), F32)] * 6,
        scratch_shapes=[pltpu.VMEM((G, N, N), F32)],
        compiler_params=_cparams(("parallel", "arbitrary"), 14 * C * W * 4 + 3 * G * N * N * 4),
    )(xs_rk, lw, k, xs_rk, a, b, states, t_invs, dy)


def _first_flag(i, per_seq):
    return (lax.rem(i, per_seq) == 0).astype(F32)


def _view(a):
    return a if isinstance(a, tuple) else (a, 0)


def _block_rows(half):
    return pl.ds(half * ATTN_BLK, ATTN_BLK)


def _block_heads(ref, half):
    return tuple(ref[_block_rows(half), _head_cols(h)] for h in range(ATTN_HPG))


def _pair_heads(ref):
    return _block_heads(ref, 0), _block_heads(ref, 1)


def attn_fwd(q, k, v, per_seq, name):
    (q, q_col), (k, k_col), (v, v_col) = _view(q), _view(k), _view(v)
    R, N = q.shape[0], GROUP_W
    n_pairs = R // (2 * ATTN_BLK)

    def body(q_ref, k_ref, kb_ref, v_ref, vb_ref, o_ref, lse_ref):
        pair = pl.program_id(0)
        first = (_first_flag(2 * pair, per_seq), _first_flag(2 * pair + 1, per_seq))
        o, lse = _attn_pair(_pair_heads(q_ref), _pair_heads(k_ref), _block_heads(kb_ref, 0), _pair_heads(v_ref),
                            _block_heads(vb_ref, 0), first)
        for half in range(2):
            for h in range(ATTN_HPG):
                o_ref[_block_rows(half), _head_cols(h)] = o[half][h]
                lse_ref[_block_rows(half), _head_cols(h)] = lse[half][h]

    cur = lambda col: pl.BlockSpec((2 * ATTN_BLK, N), lambda i: (i, col))
    prv = lambda col: pl.BlockSpec((ATTN_BLK, N), lambda i: (jnp.maximum(2 * i - 1, 0), col))
    return pl.pallas_call(
        body, name=name, grid=(n_pairs,), in_specs=[cur(q_col), cur(k_col), prv(k_col), cur(v_col), prv(v_col)],
        out_specs=[cur(0), cur(0)], out_shape=[jax.ShapeDtypeStruct((R, N), F32)] * 2,
        compiler_params=_cparams(("parallel",), 12 * ATTN_BLK * N * 4),
    )(q, k, k, v, v)


def attn_bwd(q, k, v, do, dlse, per_seq, name):
    views = [_view(a) for a in (q, k, v, do, dlse)]
    (q, q_col), (k, k_col), (v, v_col), (do, do_col), (dlse, dl_col) = views
    R, N = q.shape[0], GROUP_W
    n_pairs = R // (2 * ATTN_BLK)

    def body(q_ref, k_ref, kb_ref, v_ref, vb_ref, do_ref, dl_ref, dq_ref, dk_ref, dv_ref, carry_k, carry_v):
        step = pl.program_id(0)
        pair = n_pairs - 1 - step
        first = (_first_flag(2 * pair, per_seq), _first_flag(2 * pair + 1, per_seq))

        @pl.when(step == 0)
        def _():
            carry_k[...] = jnp.zeros_like(carry_k)
            carry_v[...] = jnp.zeros_like(carry_v)

        _, pull = jax.vjp(functools.partial(_attn_pair, first=first), _pair_heads(q_ref), _pair_heads(k_ref),
                          _block_heads(kb_ref, 0), _pair_heads(v_ref), _block_heads(vb_ref, 0))
        dq, dk, dk_before, dv, dv_before = pull((_pair_heads(do_ref), _pair_heads(dl_ref)))
        old_k, old_v = _block_heads(carry_k, 0), _block_heads(carry_v, 0)
        for h in range(ATTN_HPG):
            cols = _head_cols(h)
            for half in range(2):
                dq_ref[_block_rows(half), cols] = dq[half][h]
            dk_ref[_block_rows(0), cols] = dk[0][h]
            dv_ref[_block_rows(0), cols] = dv[0][h]
            dk_ref[_block_rows(1), cols] = dk[1][h] + old_k[h]
            dv_ref[_block_rows(1), cols] = dv[1][h] + old_v[h]
            carry_k[:, cols] = dk_before[h]
            carry_v[:, cols] = dv_before[h]

    cur = lambda col: pl.BlockSpec((2 * ATTN_BLK, N), lambda i: (n_pairs - 1 - i, col))
    prv = lambda col: pl.BlockSpec((ATTN_BLK, N), lambda i: (jnp.maximum(2 * (n_pairs - 1 - i) - 1, 0), col))
    return pl.pallas_call(
        body, name=name, grid=(n_pairs,),
        in_specs=[cur(q_col), cur(k_col), prv(k_col), cur(v_col), prv(v_col), cur(do_col), cur(dl_col)],
        out_specs=[cur(0)] * 3, out_shape=[jax.ShapeDtypeStruct((R, N), F32)] * 3,
        scratch_shapes=[pltpu.VMEM((ATTN_BLK, N), F32)] * 2,
        compiler_params=_cparams(("arbitrary",), 22 * ATTN_BLK * N * 4),
    )(q, k, k, v, v, do, dlse)


def by_residue(u, d):
    if d == 1:
        return u
    return u.reshape(u.shape[0] // d, d, GROUP_W).transpose(1, 0, 2).reshape(u.shape)


def by_position(u, d):
    if d == 1:
        return u
    return u.reshape(d, u.shape[0] // d, GROUP_W).transpose(1, 0, 2).reshape(u.shape)


def group_columns(t, col_block, d):
    if d == 1:
        return (t, col_block)
    return by_residue(t[:, GROUP_W * col_block:GROUP_W * (col_block + 1)], d)


def _ffn_fwd(x, norm, w_in, w_out, tag, token):
    h = rowmap(lambda x_b, g, tok: _rms(x_b, g) + tok[0:1, 0:1], [x], [norm, token], [(D_MODEL, BF16)], tb=512,
               name=tag + "_norm")[0]
    gu, act = ffn_in_act(h, w_in, tag + "_in")
    y = matmul(act, w_out, "nn", tag + "_out", add=x, scale=0.5)
    return y, (x, h, gu, act)


def _ffn_bwd(dy, saved, norm, w_in, w_out, tag, on_weight_grads):
    x, h, gu, act = saved
    no_token = jnp.zeros((8, 128), F32)
    dw_out = matmul(act, dy, "tn", tag + "_dwout", scale=0.5)
    dgu = ffn_dact_dgu(dy, w_out, gu, 0.5, tag + "_dgu")
    dw_in = matmul_cs(h, dgu, "tn", tag + "_dwin", no_token)
    dh = matmul_cs(dgu, w_in, "nt", tag + "_dh", on_weight_grads(dw_in, dw_out))

    def norm_bwd(x_b, dh_b, dy_b, g):
        dx, dg = jax.vjp(_rms, x_b, g)[1](dh_b)
        return dy_b + dx, dg

    dx, dnorm = rowmap(norm_bwd, [x, dh, dy], [norm], [(D_MODEL, F32)], [(1, D_MODEL)], tb=256,
                       name=tag + "_dnorm")
    return dx, dnorm, dw_in, dw_out


def layer_step(x, tgt, W, P, start_token, more_weights, on_out_grads, on_mixer_grads, on_ffn1_grads):
    S = x.shape[0]
    x1, ffn1_saved = _ffn_fwd(x, P["ffn1_norm"], W["ffn1_w_in"], W["ffn1_w_out"], "ffn1", start_token)
    W = {**W, **more_weights("mixer", x1)}
    head_of = lambda n: jnp.arange(n)[:, None] // HEAD_DIM == jnp.arange(n // HEAD_DIM)[None, :]
    seg, seg_a = head_of(D_MODEL).astype(BF16), head_of(ATTN_WIDTH).astype(BF16)
    seg_t, seg_a_t = seg.T, seg_a.T
    tile_t = (jnp.arange(HEAD_DIM)[:, None] == jnp.arange(ATTN_WIDTH)[None, :] % HEAD_DIM).astype(BF16)
    qk_params = [P["attn_q_norm"], P["attn_k_norm"], seg_a, seg_a_t, tile_t]
    w_rkv, w_lora = W["w_in"][:, :RKV], W["w_in"][:, RKV:RKV + LORA]
    w_qkv = W["w_in"][:, RKV + LORA:RKV + LORA + 3 * ATTN_WIDTH]
    w_gate = W["w_in"][:, RKV + LORA + 3 * ATTN_WIDTH:]
    mu_rk, mu_lo = P["rwkv_mu"][:, :RKV], P["rwkv_mu"][:, RKV:]
    zeros = lambda n: jnp.zeros((n, D_MODEL), F32)
    w2p = jnp.concatenate([W["rwkv_w2"], zeros(LORA - LORA_W)], axis=0)
    a2p = jnp.concatenate([zeros(LORA_W), W["rwkv_a2"], zeros(LORA_G)], axis=0)
    g2p = jnp.concatenate([zeros(LORA_W + LORA_A), W["rwkv_g2"]], axis=0)
    pre_params = [P["rwkv_w0"], w2p, P["rwkv_a0"], a2p, g2p, P["rwkv_k_k"], P["rwkv_k_a"], seg, seg_t]
    post_params = [P["rwkv_r_k"], P["rwkv_ln_w"], P["rwkv_ln_b"], seg, seg_t]
    col = lambda arr, j: (arr, D_MODEL, j)

    h = rowmap(_rms, [x1], [P["mix_norm"]], [(D_MODEL, BF16)], tb=512, name="mix_norm")[0]
    p_rk = matmul(h, w_rkv, "nn", "proj_rkv")
    p_lo = matmul(h, w_lora, "nn", "proj_lora")
    p_qkv = matmul(h, w_qkv, "nn", "proj_qkv")
    p_gate = matmul(h, w_gate, "nn", "proj_gate")
    xs_rk = token_shift_fwd(p_rk, mu_rk, tb=256, name="shift_rk")
    xs_lo = token_shift_fwd(p_lo, mu_lo, tb=256, name="shift_lora")
    lw, k_mod, a_neg, b_kk, g = rowmap(
        _rwkv_pre, [xs_rk, xs_lo], pre_params, [(D_MODEL, F32)] * 5, tb=256, name="rwkv_pre")
    wkv, states, t_invs = wkv_fwd(xs_rk, lw, k_mod, a_neg, b_kk)
    post_rows = [wkv, col(xs_rk, 0), k_mod, col(xs_rk, 2), g]
    y_a = rowmap(_rwkv_post, post_rows, post_params, [(D_MODEL, BF16)], tb=256, name="rwkv_post")[0]

    qk_rows = [(p_qkv, ATTN_WIDTH, 0), (p_qkv, ATTN_WIDTH, 1)]
    qn, kn = rowmap(_qk_norm, qk_rows, qk_params, [(ATTN_WIDTH, F32)] * 2, tb=256, name="qk_norm")
    dil = [d for _, d in ATTN_PAIRS]
    groups = range(len(dil))
    per_seq = [S // d // ATTN_BLK for d in dil]
    v_first = 2 * ATTN_WIDTH // GROUP_W
    q_s = [group_columns(qn, g, dil[g]) for g in groups]
    k_s = [group_columns(kn, g, dil[g]) for g in groups]
    v_s = [group_columns(p_qkv, v_first + g, dil[g]) for g in groups]
    attn = [attn_fwd(q_s[g], k_s[g], v_s[g], per_seq[g], "attn_fwd_%d" % g) for g in groups]
    o_lse = [by_position(attn[g][j], dil[g]) for j in range(2) for g in groups]
    y_b = rowmap(_group_combine, o_lse, [], [(ATTN_WIDTH, BF16)], tb=512, name="attn_combine")[0]

    W = {**W, **more_weights("out", y_b)}
    pa = matmul(y_a, W["w_proj_rwkv"], "nn", "proj_a")
    pb = matmul(y_b, W["w_proj_attn"], "nn", "proj_b")
    merged = rowmap(_gate_merge, [p_gate, pa, pb], [P["b_gate"]], [(D_MODEL, BF16)], tb=256, name="merge")[0]
    x2 = matmul(merged, W["w_out"], "nn", "mix_out", add=x1)
    x3, ffn2_saved = _ffn_fwd(x2, P["ffn2_norm"], W["ffn2_w_in"], W["ffn2_w_out"], "ffn2",
                              jnp.zeros_like(start_token))

    def loss_head(y_b_, t_b):
        err = y_b_ - t_b
        return err * (1.0 / D_MODEL), (0.5 / D_MODEL) * jnp.sum(err * err, axis=0, keepdims=True)

    dx3, loss_cols = rowmap(loss_head, [x3, tgt], [], [(D_MODEL, F32)], [(1, D_MODEL)], tb=512, name="loss")

    gW, gP = {}, {}
    dx2, gP["ffn2_norm"], gW["ffn2_w_in"], gW["ffn2_w_out"] = _ffn_bwd(
        dx3, ffn2_saved, P["ffn2_norm"], W["ffn2_w_in"], W["ffn2_w_out"], "ffn2",
        lambda dw_in, dw_out: jnp.zeros_like(start_token))

    dmerged = matmul(dx2, W["w_out"], "nt", "d_merged")
    gW["w_out"] = matmul(merged, dx2, "tn", "dw_out")

    def merge_bwd(pg, pa_b, pb_b, dm, bg):
        return jax.vjp(_gate_merge, pg, pa_b, pb_b, bg)[1](dm)

    dp_gate, dpa, dpb, gP["b_gate"] = rowmap(
        merge_bwd, [p_gate, pa, pb, dmerged], [P["b_gate"]],
        [(2 * D_MODEL, BF16), (D_MODEL, BF16), (D_MODEL, BF16)], [(1, 2 * D_MODEL)], tb=256, name="merge_bwd")
    dy_a = matmul(dpa, W["w_proj_rwkv"], "nt", "d_ya")
    gW["w_proj_rwkv"] = matmul(y_a, dpa, "tn", "dw_proj_a")
    dy_b = matmul(dpb, W["w_proj_attn"], "nt", "d_yb")
    gW["w_proj_attn"] = matmul(y_b, dpb, "tn", "dw_proj_b")

    out_token = on_out_grads(gW)

    def combine_bwd(*blocks):
        grads = jax.vjp(_group_combine, *blocks[:-2])[1](blocks[-2])
        return (grads[0] + blocks[-1][0:1, 0:1], *grads[1:])

    d_o_lse = rowmap(combine_bwd, o_lse + [dy_b], [out_token], [(GROUP_W, F32)] * 6, tb=256,
                     name="attn_combine_bwd")
    d_attn = [attn_bwd(q_s[g], k_s[g], v_s[g], by_residue(d_o_lse[g], dil[g]), by_residue(d_o_lse[3 + g], dil[g]),
                       per_seq[g], "attn_bwd_%d" % g) for g in groups]

    def qk_norm_bwd(q_b, k_b, *rest):
        dqkv, (qg, kg, sg, sgt, tl) = rest[:9], rest[9:]
        f = lambda *a: _qk_norm(*a, sg, sgt, tl)
        dqn, dkn = jnp.concatenate(dqkv[0:3], axis=1), jnp.concatenate(dqkv[3:6], axis=1)
        dq, dk, dqg, dkg = jax.vjp(f, q_b, k_b, qg, kg)[1]((dqn, dkn))
        return jnp.concatenate([dq, dk, *dqkv[6:9]], axis=1), dqg, dkg

    dp_qkv, gP["attn_q_norm"], gP["attn_k_norm"] = rowmap(
        qk_norm_bwd, qk_rows + [by_position(d_attn[g][j], dil[g]) for j in range(3) for g in groups], qk_params,
        [(3 * ATTN_WIDTH, BF16)], [(1, HEAD_DIM)] * 2, tb=256, name="qk_norm_bwd")

    def post_bwd(wkv_b, r_b, k_b, v_b, g_b, d_b, r_k, ln_w, ln_b, sg, sgt):
        f = lambda *a: _rwkv_post(*a, sg, sgt)
        return jax.vjp(f, wkv_b, r_b, k_b, v_b, g_b, r_k, ln_w, ln_b)[1](d_b)

    dwkv, dr_p, dk_p, dv_p, dg, gP["rwkv_r_k"], gP["rwkv_ln_w"], gP["rwkv_ln_b"] = rowmap(
        post_bwd, post_rows + [dy_a], post_params, [(D_MODEL, F32)] * 5, [(1, D_MODEL)] * 3, tb=128,
        name="rwkv_post_bwd")
    dr_w, dlw, dk_w, dv_w, da_neg, db_kk = wkv_bwd(xs_rk, lw, k_mod, a_neg, b_kk, states, t_invs, dwkv)

    def pre_bwd(xrk_b, xlo_b, dlw_b, dkw_b, dkp_b, da_b, db_b, dg_b, drp_b, drw_b, dvp_b, dvw_b,
                w0, w2, a0, a2, g2, k_k, k_a, sg, sgt):
        f = lambda *a: _rwkv_pre(*a, sg, sgt)
        pull = jax.vjp(f, xrk_b, xlo_b, w0, w2, a0, a2, g2, k_k, k_a)[1]
        dxrk, dxlo, *dpar = pull((dlw_b, dkw_b + dkp_b, da_b, db_b, dg_b))
        direct = jnp.concatenate([drp_b + drw_b, jnp.zeros_like(drp_b), dvp_b + dvw_b], axis=1)
        return (dxrk + direct, dxlo, *dpar)

    pre_rows = [xs_rk, xs_lo, dlw, dk_w, dk_p, da_neg, db_kk, dg, dr_p, dr_w, dv_p, dv_w]
    dxs_rk, dxs_lo, gP["rwkv_w0"], dw2p, gP["rwkv_a0"], da2p, dg2p, gP["rwkv_k_k"], gP["rwkv_k_a"] = rowmap(
        pre_bwd, pre_rows, pre_params, [(RKV, F32), (LORA, F32)],
        [(1, D_MODEL), (LORA, D_MODEL), (1, D_MODEL), (LORA, D_MODEL), (LORA, D_MODEL), (1, D_MODEL), (1, D_MODEL)],
        tb=128, name="rwkv_pre_bwd")
    gW["rwkv_w2"] = dw2p[:LORA_W]
    gW["rwkv_a2"] = da2p[LORA_W:LORA_W + LORA_A]
    gW["rwkv_g2"] = dg2p[LORA_W + LORA_A:]
    dp_rk, dmu_rk = token_shift_bwd(dxs_rk, p_rk, mu_rk, tb=256, name="shift_rk_bwd")
    dp_lo, dmu_lo = token_shift_bwd(dxs_lo, p_lo, mu_lo, tb=256, name="shift_lora_bwd")
    gP["rwkv_mu"] = jnp.concatenate([dmu_rk, dmu_lo], axis=1)

    dh = matmul(dp_rk, w_rkv, "nt", "dh_rkv")
    dh = matmul(dp_lo, w_lora, "nt", "dh_lora", add=dh)
    dh = matmul(dp_qkv, w_qkv, "nt", "dh_qkv", add=dh)
    dh = matmul(dp_gate, w_gate, "nt", "dh_gate", add=dh)
    gW["w_in"] = jnp.concatenate([
        matmul(h, dp_rk, "tn", "dw_rkv"), matmul(h, dp_lo, "tn", "dw_lora"),
        matmul(h, dp_qkv, "tn", "dw_qkv"), matmul(h, dp_gate, "tn", "dw_gate")], axis=1)

    token = on_mixer_grads(gW)

    def norm_bwd(x_b, dh_b, dy_b, gn, tok):
        dx, dgn = jax.vjp(_rms, x_b, gn)[1](dh_b)
        return dy_b + dx + tok[0:1, 0:1], dgn

    dx1, gP["mix_norm"] = rowmap(norm_bwd, [x1, dh, dx2], [P["mix_norm"], token], [(D_MODEL, F32)],
                                 [(1, D_MODEL)], tb=256, name="mix_norm_bwd")
    dx, gP["ffn1_norm"], gW["ffn1_w_in"], gW["ffn1_w_out"] = _ffn_bwd(
        dx1, ffn1_saved, P["ffn1_norm"], W["ffn1_w_in"], W["ffn1_w_out"], "ffn1", on_ffn1_grads)
    return loss_cols, dx, gW, gP


N_SHARDS = 4
BIG = (("ffn1_w_in", (D_MODEL, 2 * D_FF), 1), ("ffn1_w_out", (D_FF, D_MODEL), 0),
       ("w_in", (D_MODEL, 7712), 1), ("rwkv_w2", (LORA_W, D_MODEL), 1), ("rwkv_a2", (LORA_A, D_MODEL), 1),
       ("rwkv_g2", (LORA_G, D_MODEL), 1), ("w_proj_rwkv", (D_MODEL, D_MODEL), 0),
       ("w_proj_attn", (ATTN_WIDTH, D_MODEL), 1), ("w_out", (D_MODEL, D_MODEL), 0),
       ("ffn2_w_in", (D_MODEL, 2 * D_FF), 1), ("ffn2_w_out", (D_FF, D_MODEL), 0))
SMALL = (("ffn1_norm", 1024), ("mix_norm", 1024), ("b_gate", 2048), ("rwkv_mu", 3360), ("rwkv_w0", 1024),
         ("rwkv_a0", 1024), ("rwkv_k_k", 1024), ("rwkv_k_a", 1024), ("rwkv_r_k", 1024), ("rwkv_ln_w", 1024),
         ("rwkv_ln_b", 1024), ("attn_q_norm", 64), ("attn_k_norm", 64), ("ffn2_norm", 1024))
WEIGHT_ORDER = ("ffn1_norm", "ffn1_w_in", "ffn1_w_out", "mix_norm", "w_in", "b_gate", "rwkv_mu", "rwkv_w0",
                "rwkv_w2", "rwkv_a0", "rwkv_a2", "rwkv_g2", "rwkv_k_k", "rwkv_k_a", "rwkv_r_k", "rwkv_ln_w",
                "rwkv_ln_b", "attn_q_norm", "attn_k_norm", "w_proj_rwkv", "w_proj_attn", "w_out", "ffn2_norm",
                "ffn2_w_in", "ffn2_w_out")


LORA_PARTS = ("rwkv_w2", "rwkv_a2", "rwkv_g2")
BLOCK_MAJOR = ("ffn1_w_in", "ffn2_w_in")
FIRST_FFN = ("ffn1_w_in", "ffn1_w_out")
MIXER_IN = ("w_in", "lora")
SMALL_USED = D_MODEL + sum(n for _, n in SMALL)
SMALL_W = -(-SMALL_USED // 128) * 128


def _travel():
    out = {}
    for name, shape, axis in BIG:
        if name == LORA_PARTS[0]:
            out["lora"] = ((LORA, D_MODEL), 1)
        elif name not in LORA_PARTS:
            out[name] = (shape, axis)
    return out


def local_blocks(vals):
    out = {n: vals[n] for n in _travel() if n != "lora"}
    out["lora"] = jnp.concatenate([vals[n] for n in LORA_PARTS], axis=0)
    return out


def split_lora(t):
    return {"rwkv_w2": t[:LORA_W], "rwkv_a2": t[LORA_W:LORA_W + LORA_A], "rwkv_g2": t[LORA_W + LORA_A:]}


def blocks_to_full(name, blocks):
    shape, axis = _travel()[name]
    if name in BLOCK_MAJOR:
        return blocks
    if axis == 0:
        return blocks.reshape(shape)
    return blocks.transpose(1, 0, 2).reshape(shape)


def full_to_blocks(name, full):
    shape, axis = _travel()[name]
    if name in BLOCK_MAJOR:
        return full
    if axis == 0:
        return full.reshape(N_SHARDS, shape[0] // N_SHARDS, shape[1])
    return full.reshape(shape[0], N_SHARDS, shape[1] // N_SHARDS).transpose(1, 0, 2)


def pack_small(vals, head):
    parts = [head] + [vals[name].reshape(1, n) for name, n in SMALL]
    parts.append(jnp.zeros((1, SMALL_W - SMALL_USED), F32))
    return jnp.concatenate(parts, axis=1)


def unpack_small(vec, shapes):
    out, off = {}, D_MODEL
    for name, n in SMALL:
        out[name] = vec[:, off:off + n].reshape(shapes[name])
        off += n
    return out


def _place():
    return lax.axis_index("x"), lax.axis_index("y"), lax.axis_index("c")


def _other_chips(x, y):
    return [(1 - x, y), (x, 1 - y), (1 - x, 1 - y)]


def _remote(src, dst, send_sem, recv_sem, device):
    return pltpu.make_async_remote_copy(src_ref=src, dst_ref=dst, send_sem=send_sem, recv_sem=recv_sem,
                                        device_id=device, device_id_type=MESH)


def _half(ref, who):
    hr = ref.shape[-2] // 2
    rows = pl.ds(pl.multiple_of(who * hr, 8), hr)
    return ref.at[rows] if len(ref.shape) == 2 else ref.at[:, rows]


HBM_REF = pl.BlockSpec(memory_space=pl.ANY)
COMM_PARAMS = dict(compiler_params=pltpu.CompilerParams(has_side_effects=True))


def gather_weights(blocks):
    n = len(blocks)

    def body(*refs):
        ins, outs = refs[:n], refs[n:2 * n]
        ici_send, ici_recv, d2d_send, d2d_recv = refs[2 * n:]
        x, y, c = _place()
        me, sibling, chips = 2 * x + y, (x, y, 1 - c), _other_chips(x, y)
        first = [_remote(_half(ins[t], c), _half(outs[t].at[me], c), ici_send.at[k, t], ici_recv.at[k, t],
                         (px, py, c)) for k, (px, py) in enumerate(chips) for t in range(n)]
        for cp in first:
            cp.start()
        passed = []
        for k, (px, py) in enumerate(chips):
            for t in range(n):
                landed = _half(outs[t].at[2 * px + py], c)
                _remote(landed, landed, ici_send.at[k, t], ici_recv.at[k, t], (px, py, c)).wait_recv()
                cp = _remote(landed, landed, d2d_send.at[k, t], d2d_recv.at[k, t], sibling)
                cp.start()
                passed.append(cp)
        for k, (px, py) in enumerate(chips):
            for t in range(n):
                other = _half(outs[t].at[2 * px + py], 1 - c)
                _remote(other, other, d2d_send.at[k, t], d2d_recv.at[k, t], sibling).wait_recv()
        for cp in first + passed:
            cp.wait_send()

    res = pl.pallas_call(
        body, name="gather_weights", in_specs=[HBM_REF] * n, out_specs=[HBM_REF] * n,
        out_shape=[jax.ShapeDtypeStruct((N_SHARDS,) + b.shape, b.dtype) for b in blocks],
        scratch_shapes=[pltpu.SemaphoreType.DMA((3, n))] * 4, **COMM_PARAMS)(*blocks)
    me = 2 * lax.axis_index("x") + lax.axis_index("y")
    return [lax.dynamic_update_slice(g, b[None], (me, 0, 0)) for g, b in zip(res, blocks)]


def _gather_copies(ins, outs, send_sem, recv_sem):
    x, y, c = _place()
    return [_remote(_half(ins[t], c), _half(outs[t].at[2 * x + y], c), send_sem(k, t), recv_sem(k, t), (px, py, c))
            for k, (px, py) in enumerate(_other_chips(x, y)) for t in range(len(ins))]


def split_start(copies, per_tensor, sources, landing_shapes, name):
    n = len(sources)
    n_cp = per_tensor * n

    def body(*refs):
        srcs, dsts = refs[:n], refs[n:2 * n]
        sems, token = refs[2 * n:2 * n + 2 * n_cp], refs[-1]
        for cp in copies(srcs, dsts, lambda k, t: sems[k * n + t], lambda k, t: sems[n_cp + k * n + t]):
            cp.start()
        token[...] = jnp.zeros_like(token)

    hbm = lambda a: pltpu.with_memory_space_constraint(a, pltpu.HBM)
    buffers = list(sources) + [lax.empty(shape, s.dtype) for shape, s in zip(landing_shapes, sources)]
    res = pl.pallas_call(
        body, name=name,
        out_shape=(*[pltpu.SemaphoreType.DMA(())] * (2 * n_cp),
                   *[pltpu.HBM(a.shape, a.dtype) for a in buffers], jax.ShapeDtypeStruct((8, 128), F32)),
        in_specs=[SPLIT_HBM] * (2 * n),
        out_specs=(*[SPLIT_SEM] * (2 * n_cp), *[SPLIT_HBM] * (2 * n), pl.BlockSpec(memory_space=pltpu.VMEM)),
        input_output_aliases={t: 2 * n_cp + t for t in range(2 * n)}, **SPLIT_PARAMS,
    )(*[hbm(a) for a in buffers])
    return (copies, per_tensor, n, res[:-1]), res[-1]


def split_wait(handles, after, name):
    copies, per_tensor, n, held = handles
    n_cp = per_tensor * n
    sems, thru = held[:2 * n_cp], held[2 * n_cp:]

    def body(*refs):
        srcs, dsts = refs[:n], refs[n:2 * n]
        sem_refs = refs[2 * n:2 * n + 2 * n_cp]
        for cp in copies(srcs, dsts, lambda k, t: sem_refs[k * n + t], lambda k, t: sem_refs[n_cp + k * n + t]):
            cp.wait_send()
            cp.wait_recv()

    res = pl.pallas_call(
        body, name=name, out_shape=tuple(pltpu.HBM(a.shape, a.dtype) for a in thru),
        in_specs=[SPLIT_HBM] * (2 * n) + [SPLIT_SEM] * (2 * n_cp) + [pl.BlockSpec(memory_space=pl.ANY)],
        out_specs=tuple([SPLIT_HBM] * (2 * n)), input_output_aliases={t: t for t in range(2 * n)}, **SPLIT_PARAMS,
    )(*thru, *sems, after)
    return list(res[n:])


def gather_start(blocks, name):
    return split_start(_gather_copies, 3, blocks, [(N_SHARDS,) + b.shape for b in blocks], name)


def pass_halves(gathered, blocks, name):
    n = len(gathered)

    def body(*refs):
        outs = refs[n:2 * n]
        send_sems, recv_sems = refs[2 * n:]
        x, y, c = _place()
        slots = [2 * px + py for px, py in _other_chips(x, y)]
        give = [_remote(_half(outs[t].at[s], c), _half(outs[t].at[s], c), send_sems.at[k, t], recv_sems.at[k, t],
                        (x, y, 1 - c)) for k, s in enumerate(slots) for t in range(n)]
        for cp in give:
            cp.start()
        for k, s in enumerate(slots):
            for t in range(n):
                other = _half(outs[t].at[s], 1 - c)
                _remote(other, other, send_sems.at[k, t], recv_sems.at[k, t], (x, y, 1 - c)).wait_recv()
        for cp in give:
            cp.wait_send()

    res = pl.pallas_call(
        body, name=name, in_specs=[HBM_REF] * n, out_specs=[HBM_REF] * n,
        out_shape=[jax.ShapeDtypeStruct(g.shape, g.dtype) for g in gathered],
        input_output_aliases={t: t for t in range(n)},
        scratch_shapes=[pltpu.SemaphoreType.DMA((3, n))] * 2, **COMM_PARAMS)(*gathered)
    me = 2 * lax.axis_index("x") + lax.axis_index("y")
    return [lax.dynamic_update_slice(g, b[None], (me, 0, 0)) for g, b in zip(res, blocks)]


def swap_halves(grads):
    n = len(grads)

    def body(*refs):
        ins, got = refs[:n], refs[n:2 * n]
        send_sems, recv_sems = refs[2 * n:]
        x, y, c = _place()
        give = [_remote(_half(ins[t], 1 - c), got[t], send_sems.at[t], recv_sems.at[t], (x, y, 1 - c))
                for t in range(n)]
        for cp in give:
            cp.start()
        for cp in give:
            cp.wait_recv()
        for cp in give:
            cp.wait_send()

    return pl.pallas_call(
        body, name="swap_halves", in_specs=[HBM_REF] * n, out_specs=[HBM_REF] * n,
        out_shape=[jax.ShapeDtypeStruct((g.shape[0], g.shape[1] // 2, g.shape[2]), g.dtype) for g in grads],
        scratch_shapes=[pltpu.SemaphoreType.DMA((n,))] * 2, **COMM_PARAMS)(*grads)


def join_halves(blocks):
    n = len(blocks)

    def body(*refs):
        outs = refs[n:2 * n]
        send_sems, recv_sems = refs[2 * n:]
        x, y, c = _place()
        give = [_remote(_half(outs[t], c), _half(outs[t], c), send_sems.at[t], recv_sems.at[t], (x, y, 1 - c))
                for t in range(n)]
        for cp in give:
            cp.start()
        for t in range(n):
            arriving = _half(outs[t], 1 - c)
            _remote(arriving, arriving, send_sems.at[t], recv_sems.at[t], (x, y, 1 - c)).wait_recv()
        for cp in give:
            cp.wait_send()

    return pl.pallas_call(
        body, name="join_halves", in_specs=[HBM_REF] * n, out_specs=[HBM_REF] * n,
        out_shape=[jax.ShapeDtypeStruct(b.shape, b.dtype) for b in blocks],
        input_output_aliases={t: t for t in range(n)},
        scratch_shapes=[pltpu.SemaphoreType.DMA((n,))] * 2, **COMM_PARAMS)(*blocks)


SPLIT_HBM = pl.BlockSpec(memory_space=pltpu.HBM)
SPLIT_SEM = pl.BlockSpec(memory_space=pltpu.SEMAPHORE)
SPLIT_PARAMS = dict(compiler_params=pltpu.CompilerParams(has_side_effects=pltpu.SideEffectType.DATAFLOW_SIDE_EFFECTING))


def _scatter_copies(parts, landed, send_sem, recv_sem):
    x, y, c = _place()
    return [_remote(parts[t].at[2 * px + py], landed[t].at[k], send_sem(k, t), recv_sem(k, t), (px, py, c))
            for k, (px, py) in enumerate(_other_chips(x, y)) for t in range(len(parts))]


def scatter_start(partials, name):
    return split_start(_scatter_copies, 3, partials, [(3,) + p.shape[1:] for p in partials], name)


def _swap_copies(grads, got, send_sem, recv_sem):
    x, y, c = _place()
    return [_remote(_half(grads[t], 1 - c), got[t], send_sem(0, t), recv_sem(0, t), (x, y, 1 - c))
            for t in range(len(grads))]


def swap_start(grads, name):
    return split_start(_swap_copies, 1, grads, [(g.shape[0], g.shape[1] // 2, g.shape[2]) for g in grads], name)


def chip_sums(grads, got):
    names = list(grads)
    partials = []
    for name, theirs in zip(names, got):
        n_slot, hr, width = theirs.shape
        tb = _row_block(hr, width, 6)
        per_half = hr // tb
        mine = lambda i, s, per_half=per_half: (i // per_half) * 2 * per_half + s[0] * per_half + i % per_half
        p = placed_map(
            jnp.add,
            [(grads[name].reshape(2 * n_slot * hr, width), mine), (theirs.reshape(n_slot * hr, width), lambda i, s: i)],
            (n_slot * hr, width, BF16, lambda i, s: i), n_blocks=n_slot * per_half, tb=tb, name="chip_sum_" + name)
        partials.append(p.reshape(theirs.shape))
    return partials


def owner_sums(grads, got, landed):
    names = list(grads)
    blocks = []
    for name, theirs, arrived in zip(names, got, landed):
        n_slot, hr, width = theirs.shape
        tb = _row_block(hr, width, 6)
        per_half = hr // tb
        views = [(grads[name].reshape(2 * n_slot * hr, width),
                  lambda i, s, per_half=per_half: s[1] * 2 * per_half + s[0] * per_half + i),
                 (theirs.reshape(n_slot * hr, width), lambda i, s, per_half=per_half: s[1] * per_half + i)]
        views += [(arrived.reshape(3 * hr, width), functools.partial(lambda k, per_half, i, s: k * per_half + i,
                                                                     k, per_half)) for k in range(3)]
        f = lambda a, b, l0, l1, l2: (((a + b) + l0.astype(F32)) + l1.astype(F32)) + l2.astype(F32)
        blocks.append(placed_map(
            f, views,(2 * hr, width, F32, lambda i, s, per_half=per_half: s[0] * per_half + i),
            n_blocks=per_half, tb=tb, name="owner_sum_" + name))
    return dict(zip(names, join_halves(blocks)))


def adamw_block(name, w, g, m, v):
    rows, width = w.shape
    return rowmap(_adamw, [w, g, m, v], [], [(width, F32)] * 3, tb=_row_block(rows, width, 7),
                  name="adamw_" + name)


def reduce_small(vec, w, m, v):
    n_dev = 8

    def body(vec_ref, w_ref, m_ref, v_ref, loss_ref, g_ref, d_ref, m2_ref, v2_ref, slots, send_sems, recv_sems):
        x, y, c = _place()
        me = 4 * x + 2 * y + c
        slots[me] = vec_ref[...]
        flips = [(fx, fy, fc) for fx in (0, 1) for fy in (0, 1) for fc in (0, 1)][1:]
        peers = [(1 - x if fx else x, 1 - y if fy else y, 1 - c if fc else c) for fx, fy, fc in flips]
        sends = [pltpu.make_async_remote_copy(
            src_ref=vec_ref, dst_ref=slots.at[me], send_sem=send_sems.at[j], recv_sem=recv_sems.at[j],
            device_id=peer, device_id_type=MESH) for j, peer in enumerate(peers)]
        for cp in sends:
            cp.start()
        for j, (px, py, pc) in enumerate(peers):
            pltpu.make_async_remote_copy(
                src_ref=vec_ref, dst_ref=slots.at[4 * px + 2 * py + pc], send_sem=send_sems.at[j],
                recv_sem=recv_sems.at[j], device_id=(px, py, pc), device_id_type=MESH).wait_recv()
        for cp in sends:
            cp.wait_send()
        g = slots[0]
        for d in range(1, n_dev):
            g = g + slots[d]
        loss_ref[...] = jnp.sum(g[:, :D_MODEL], axis=1, keepdims=True)
        delta, m2, v2 = _adamw(w_ref[...], g, m_ref[...], v_ref[...])
        g_ref[...], d_ref[...], m2_ref[...], v2_ref[...] = g, delta, m2, v2

    vm = pl.BlockSpec(memory_space=pltpu.VMEM)
    vec_t = jax.ShapeDtypeStruct(vec.shape, F32)
    return pl.pallas_call(
        body, name="reduce_small", in_specs=[vm] * 4, out_specs=[vm] * 5,
        out_shape=[jax.ShapeDtypeStruct((1, 1), F32)] + [vec_t] * 4,
        scratch_shapes=[pltpu.VMEM((n_dev,) + vec.shape, F32), pltpu.SemaphoreType.DMA((n_dev - 1,)),
                        pltpu.SemaphoreType.DMA((n_dev - 1,))],
        compiler_params=pltpu.CompilerParams(has_side_effects=True),
    )(vec, w, m, v)


def kernel(x, ffn1_norm, ffn1_w_in, ffn1_w_out, mix_norm, w_in, b_gate, rwkv_mu, rwkv_w0, rwkv_w2, rwkv_a0, rwkv_a2, rwkv_g2, rwkv_k_k, rwkv_k_a, rwkv_r_k, rwkv_ln_w, rwkv_ln_b, attn_q_norm, attn_k_norm, w_proj_rwkv, w_proj_attn, w_out, ffn2_norm, ffn2_w_in, ffn2_w_out, loss_target, m_ffn1_norm, m_ffn1_w_in, m_ffn1_w_out, m_mix_norm, m_w_in, m_b_gate, m_rwkv_mu, m_rwkv_w0, m_rwkv_w2, m_rwkv_a0, m_rwkv_a2, m_rwkv_g2, m_rwkv_k_k, m_rwkv_k_a, m_rwkv_r_k, m_rwkv_ln_w, m_rwkv_ln_b, m_attn_q_norm, m_attn_k_norm, m_w_proj_rwkv, m_w_proj_attn, m_w_out, m_ffn2_norm, m_ffn2_w_in, m_ffn2_w_out, v_ffn1_norm, v_ffn1_w_in, v_ffn1_w_out, v_mix_norm, v_w_in, v_b_gate, v_rwkv_mu, v_rwkv_w0, v_rwkv_w2, v_rwkv_a0, v_rwkv_a2, v_rwkv_g2, v_rwkv_k_k, v_rwkv_k_a, v_rwkv_r_k, v_rwkv_ln_w, v_rwkv_ln_b, v_attn_q_norm, v_attn_k_norm, v_w_proj_rwkv, v_w_proj_attn, v_w_out, v_ffn2_norm, v_ffn2_w_in, v_ffn2_w_out):
    given = dict(locals())
    weights = {n: given[n] for n in WEIGHT_ORDER}
    mom_m = {n: given["m_" + n] for n in WEIGHT_ORDER}
    mom_v = {n: given["v_" + n] for n in WEIGHT_ORDER}
    big = [name for name, _, _ in BIG]
    shapes = {n: weights[n].shape for n in WEIGHT_ORDER}
    blocks_of = lambda d: local_blocks({n: d[n][0] for n in big})
    w_blk, m_blk, v_blk = blocks_of(weights), blocks_of(mom_m), blocks_of(mom_v)
    names = list(w_blk)

    early = [n for n in names if n not in FIRST_FFN]
    bf16_block = lambda n: w_blk[n].astype(BF16)
    W = {n: blocks_to_full(n, g) for n, g in zip(FIRST_FFN, gather_weights([bf16_block(n) for n in FIRST_FFN]))}
    stages = {"mixer": [n for n in early if n in MIXER_IN], "out": [n for n in early if n not in MIXER_IN]}
    stage_blocks = {s: [bf16_block(n) for n in stages[s]] for s in stages}
    started = {s: gather_start(stage_blocks[s], "gather_start_" + s) for s in ("mixer", "out")}
    start_token = started["mixer"][1] + started["out"][1]

    def more_weights(stage, after):
        landed = split_wait(started[stage][0], after, "gather_wait_" + stage)
        got = pass_halves(landed, stage_blocks[stage], "pass_halves_" + stage)
        more = {n: blocks_to_full(n, g) for n, g in zip(stages[stage], got)}
        if "lora" in more:
            more.update(split_lora(more.pop("lora")))
        return more

    P = {n: weights[n].reshape(1, -1) for n, _ in SMALL}

    sent = {}

    def swap_out(gw):
        sent["grads"] = {n: full_to_blocks(n, gw[n]) for n in stages["out"]}
        sent["out_swap"], token = swap_start(list(sent["grads"].values()), "swap_start_out")
        return token

    def send_early(gw):
        lora = jnp.concatenate([gw[n] for n in LORA_PARTS], axis=0)
        mixer = {n: full_to_blocks(n, lora if n == "lora" else gw[n]) for n in stages["mixer"]}
        sent["got"] = split_wait(sent["out_swap"], gw["w_in"], "swap_wait_out") + list(swap_halves(list(mixer.values())))
        sent["grads"].update(mixer)
        sent["handles"], token = scatter_start(chip_sums(sent["grads"], sent["got"]), "scatter_start")
        return token

    def send_late(dw_in, dw_out):
        sent["late"] = {n: full_to_blocks(n, g) for n, g in zip(FIRST_FFN, (dw_in, dw_out))}
        sent["late_got"] = swap_halves(list(sent["late"].values()))
        sent["late_handles"], token = scatter_start(chip_sums(sent["late"], sent["late_got"]), "scatter_start_ffn1")
        return token

    loss_cols, dx, gW, gP = layer_step(x[0], loss_target[0], W, P, start_token, more_weights, swap_out, send_early,
                                       send_late)
    landed = split_wait(sent["handles"], gP["ffn1_norm"], "scatter_wait")
    out_g, out_d, out_m, out_v = {}, {}, {}, {}

    def apply(g_blk):
        for n in g_blk:
            res = (g_blk[n], *adamw_block(n, w_blk[n], g_blk[n], m_blk[n], v_blk[n]))
            for dst, t in zip((out_g, out_d, out_m, out_v), res):
                for part, val in (split_lora(t) if n == "lora" else {n: t}).items():
                    dst[part] = val.reshape(shapes[part])

    apply(owner_sums(sent["grads"], sent["got"], landed))
    late_landed = split_wait(sent["late_handles"], list(out_d.values())[-1], "scatter_wait_ffn1")
    apply(owner_sums(sent["late"], sent["late_got"], late_landed))

    zero_head = jnp.zeros((1, D_MODEL), F32)
    vec = pack_small(gP, loss_cols)
    loss, g_s, d_s, m_s, v_s = reduce_small(
        vec, pack_small({n: weights[n] for n, _ in SMALL}, zero_head),
        pack_small({n: mom_m[n] for n, _ in SMALL}, zero_head),
        pack_small({n: mom_v[n] for n, _ in SMALL}, zero_head))
    for dst, src in ((out_g, g_s), (out_d, d_s), (out_m, m_s), (out_v, v_s)):
        dst.update(unpack_small(src, shapes))

    return (loss[0, 0], dx[None], *[out_g[n] for n in WEIGHT_ORDER], *[out_d[n] for n in WEIGHT_ORDER],
            *[out_m[n] for n in WEIGHT_ORDER], *[out_v[n] for n in WEIGHT_ORDER])
```

```python
import functools

import jax
import jax.numpy as jnp
from jax import lax
from jax.experimental import pallas as pl
from jax.experimental.pallas import tpu as pltpu

F32 = jnp.float32
BF16 = jnp.bfloat16
MESH = pl.DeviceIdType.MESH

D_MODEL = 1024
HEAD_DIM = 64
RWKV_HEADS = 16
LORA_W, LORA_A, LORA_G = 64, 64, 160
LORA = LORA_W + LORA_A + LORA_G
RKV = 3 * D_MODEL
ATTN_PAIRS = ((128, 1), (512, 4), (2048, 16))
ATTN_BLK = 128
ATTN_HPG = 4
ATTN_RUN = 2
ATTN_WIDTH = 768
GROUP_W = ATTN_HPG * HEAD_DIM
D_FF = 2816
GN_EPS = 64e-5
RMS_EPS = 1e-6
NEG_INF = -1e30
WKV_CHUNK = 64
WKV_HEADS_PER_STEP = 16

ADAM_LR, ADAM_B1, ADAM_B2, ADAM_EPS, ADAM_WD, ADAM_STEP = 0.001, 0.9, 0.999, 1e-08, 0.01, 10

V7X_VMEM_BYTES = 64 << 20
VMEM_TEMP_ALLOWANCE = 20 << 20
VMEM_LEFT_FREE = 6 << 20


def _cparams(sem, block_bytes):
    limit = min(2 * block_bytes + VMEM_TEMP_ALLOWANCE, V7X_VMEM_BYTES - VMEM_LEFT_FREE)
    return pltpu.CompilerParams(dimension_semantics=sem, vmem_limit_bytes=int(limit))


def _nbytes(shape, dtype):
    n = 1
    for s in shape:
        n *= s
    return n * jnp.dtype(dtype).itemsize


def _split_bf16(a):
    hi = a.astype(BF16)
    return hi, (a - hi.astype(F32)).astype(BF16)


def _make_dots():
    def raw(a, b, ca, cb):
        return lax.dot_general(a.astype(BF16), b.astype(BF16), (((ca,), (cb,)), ((), ())),
                               preferred_element_type=F32)

    @jax.custom_vjp
    def nn(a, b):
        return raw(a, b, 1, 0)

    @jax.custom_vjp
    def nt(a, b):
        return raw(a, b, 1, 1)

    @jax.custom_vjp
    def tn(a, b):
        return raw(a, b, 0, 0)

    nn.defvjp(lambda a, b: (raw(a, b, 1, 0), (a, b)),
              lambda res, g: (raw(g, res[1], 1, 1), raw(res[0], g, 0, 0)))
    nt.defvjp(lambda a, b: (raw(a, b, 1, 1), (a, b)),
              lambda res, g: (raw(g, res[1], 1, 0), raw(g, res[0], 0, 0)))
    tn.defvjp(lambda a, b: (raw(a, b, 0, 0), (a, b)),
              lambda res, g: (raw(res[1], g, 1, 1), raw(res[0], g, 1, 0)))
    return nn, nt, tn


def _exact_rhs_dot(x, ones, cx, co):
    hi, lo = _split_bf16(x)
    dims = (((cx,), (co,)), ((), ()))
    return (lax.dot_general(hi, ones, dims, preferred_element_type=F32)
            + lax.dot_general(lo, ones, dims, preferred_element_type=F32))


@jax.custom_vjp
def SEG(x, ones):
    return _exact_rhs_dot(x, ones, 1, 0)


SEG.defvjp(lambda x, ones: (_exact_rhs_dot(x, ones, 1, 0), ones),
           lambda ones, g: (_exact_rhs_dot(g, ones, 1, 1), jnp.zeros_like(ones)))

NN, NT, TN = _make_dots()


MM_TILE_M, MM_TILE_N, MM_TILE_K = 1408, 1408, 1536


def _pick(n, cap):
    best = None
    for t in range(128, min(n, cap) + 1, 128):
        if n % t == 0:
            best = t
    return best or n


def matmul(a, b, mode, name, *, add=None, scale=1.0):
    if mode == "nn":
        (M, K), (K2, N) = a.shape, b.shape
    elif mode == "nt":
        (M, K), (N, K2) = a.shape, b.shape
    else:
        (K, M), (K2, N) = a.shape, b.shape
    assert K == K2, (name, a.shape, b.shape)
    tm, tn, tk = _pick(M, MM_TILE_M), _pick(N, MM_TILE_N), _pick(K, MM_TILE_K)
    nk = K // tk
    ca, cb = {"nn": (1, 0), "nt": (1, 1), "tn": (0, 0)}[mode]

    def body(*refs):
        if add is None:
            a_ref, b_ref, o_ref, acc_ref = refs
        else:
            a_ref, b_ref, add_ref, o_ref, acc_ref = refs
        k = pl.program_id(2)

        @pl.when(k == 0)
        def _():
            acc_ref[...] = jnp.zeros_like(acc_ref)

        acc_ref[...] += lax.dot_general(a_ref[...].astype(BF16), b_ref[...].astype(BF16),
                                        (((ca,), (cb,)), ((), ())), preferred_element_type=F32)

        @pl.when(k == nk - 1)
        def _():
            r = acc_ref[...] * scale
            if add is not None:
                r = add_ref[...] + r
            o_ref[...] = r.astype(o_ref.dtype)

    a_spec = (pl.BlockSpec((tk, tm), lambda i, j, k: (k, i)) if mode == "tn"
              else pl.BlockSpec((tm, tk), lambda i, j, k: (i, k)))
    b_spec = (pl.BlockSpec((tn, tk), lambda i, j, k: (j, k)) if mode == "nt"
              else pl.BlockSpec((tk, tn), lambda i, j, k: (k, j)))
    in_specs, args = [a_spec, b_spec], [a, b]
    blk = tm * tk * a.dtype.itemsize + tk * tn * b.dtype.itemsize + tm * tn * 8
    if add is not None:
        in_specs.append(pl.BlockSpec((tm, tn), lambda i, j, k: (i, j)))
        args.append(add)
        blk += tm * tn * 4
    return pl.pallas_call(
        body, name=name, grid=(M // tm, N // tn, nk),
        in_specs=in_specs, out_specs=pl.BlockSpec((tm, tn), lambda i, j, k: (i, j)),
        out_shape=jax.ShapeDtypeStruct((M, N), F32),
        scratch_shapes=[pltpu.VMEM((tm, tn), F32)],
        compiler_params=_cparams(("parallel", "parallel", "arbitrary"), blk),
    )(*args)


def matmul_cs(a, w, mode, name, token):
    n_blk = N_SHARDS
    if mode == "tn":
        (K, R), Cs = a.shape, w.shape[2] // 2
        tm, tk = _pick(R, MM_TILE_M), _pick(K, 1024)
        grid = (R // tm, n_blk, K // tk)
        a_spec = pl.BlockSpec((tk, tm), lambda i, j, k: (k, i))
        w_spec = pl.BlockSpec((None, tk, Cs), lambda i, j, k: (j // 2, k, j % 2))
        o_spec = pl.BlockSpec((None, tm, Cs), lambda i, j, k: (j, i, 0))
        out_shape, acc_shape, dims = (n_blk, R, Cs), (tm, Cs), (0, 0)
        blk = tk * tm * a.dtype.itemsize + tk * Cs * w.dtype.itemsize + tm * Cs * 8
    else:
        M, (_, R, Cs) = a.shape[1], w.shape
        tm, tn = _pick(M, MM_TILE_M), _pick(R, MM_TILE_N)
        grid = (M // tm, R // tn, n_blk)
        a_spec = pl.BlockSpec((None, tm, Cs), lambda i, j, k: (k // 2, i, k % 2))
        w_spec = pl.BlockSpec((None, tn, Cs), lambda i, j, k: (k, j, 0))
        o_spec = pl.BlockSpec((tm, tn), lambda i, j, k: (i, j))
        out_shape, acc_shape, dims = (M, R), (tm, tn), (1, 1)
        blk = tm * Cs * a.dtype.itemsize + tn * Cs * w.dtype.itemsize + tm * tn * 8
    nk = grid[2]

    def body(a_ref, w_ref, tok_ref, o_ref, acc_ref):
        k = pl.program_id(2)

        @pl.when(k == 0)
        def _():
            acc_ref[...] = jnp.zeros_like(acc_ref)

        acc_ref[...] += lax.dot_general(a_ref[...].astype(BF16), w_ref[...].astype(BF16),
                                        (((dims[0],), (dims[1],)), ((), ())), preferred_element_type=F32)

        @pl.when(k == nk - 1)
        def _():
            o_ref[...] = acc_ref[...] + tok_ref[0:1, 0:1]

    return pl.pallas_call(
        body, name=name, grid=grid, in_specs=[a_spec, w_spec, pl.BlockSpec(token.shape, lambda i, j, k: (0, 0))],
        out_specs=o_spec, out_shape=jax.ShapeDtypeStruct(out_shape, F32), scratch_shapes=[pltpu.VMEM(acc_shape, F32)],
        compiler_params=_cparams(("parallel", "parallel", "arbitrary"), blk),
    )(a, w, token)


FFN_TILE_M = 512


def _swiglu(gate, up):
    return gate * jax.nn.sigmoid(gate) * up


def ffn_in_act(h, w, name):
    (M, R), Cs, half = h.shape, w.shape[2], N_SHARDS // 2
    tm, tk = _pick(M, FFN_TILE_M), _pick(R, 1024)
    nk = R // tk

    def body(h_ref, wg_ref, wu_ref, gu_ref, act_ref, acc_ref):
        k = pl.program_id(2)

        @pl.when(k == 0)
        def _():
            acc_ref[...] = jnp.zeros_like(acc_ref)

        hb = h_ref[...].astype(BF16)
        for part, w_ref in enumerate((wg_ref, wu_ref)):
            acc_ref[part] += jnp.dot(hb, w_ref[...].astype(BF16), preferred_element_type=F32)

        @pl.when(k == nk - 1)
        def _():
            gu_ref[...] = acc_ref[...]
            act_ref[...] = _swiglu(acc_ref[0], acc_ref[1]).astype(act_ref.dtype)

    w_spec = lambda off: pl.BlockSpec((None, tk, Cs), functools.partial(lambda off, j, i, k: (j + off, k, 0), off))
    blk = tm * tk * h.dtype.itemsize + 2 * tk * Cs * w.dtype.itemsize + tm * Cs * (16 + 2)
    return pl.pallas_call(
        body, name=name, grid=(half, M // tm, nk),
        in_specs=[pl.BlockSpec((tm, tk), lambda j, i, k: (i, k)), w_spec(0), w_spec(half)],
        out_specs=[pl.BlockSpec((2, tm, Cs), lambda j, i, k: (0, i, j)), pl.BlockSpec((tm, Cs), lambda j, i, k: (i, j))],
        out_shape=[jax.ShapeDtypeStruct((2, M, half * Cs), F32), jax.ShapeDtypeStruct((M, half * Cs), BF16)],
        scratch_shapes=[pltpu.VMEM((2, tm, Cs), F32)],
        compiler_params=_cparams(("parallel", "parallel", "arbitrary"), blk),
    )(h, w, w)


def ffn_dact_dgu(dy, w_out, gu, scale, name):
    (M, D), F = dy.shape, w_out.shape[0]
    tm, tn = _pick(M, FFN_TILE_M), F // 2

    def body(dy_ref, w_ref, gu_ref, dgu_ref):
        dact = scale * lax.dot_general(dy_ref[...].astype(BF16), w_ref[...].astype(BF16),
                                       (((1,), (1,)), ((), ())), preferred_element_type=F32)
        dgate, dup = jax.vjp(_swiglu, gu_ref[0], gu_ref[1])[1](dact)
        dgu_ref[0] = dgate.astype(dgu_ref.dtype)
        dgu_ref[1] = dup.astype(dgu_ref.dtype)

    pair = pl.BlockSpec((2, tm, tn), lambda j, i: (0, i, j))
    blk = tm * D * dy.dtype.itemsize + tn * D * w_out.dtype.itemsize + 2 * tm * tn * (4 + 2)
    return pl.pallas_call(
        body, name=name, grid=(F // tn, M // tm),
        in_specs=[pl.BlockSpec((tm, D), lambda j, i: (i, 0)), pl.BlockSpec((tn, D), lambda j, i: (j, 0)), pair],
        out_specs=pair, out_shape=jax.ShapeDtypeStruct((2, M, F), BF16),
        compiler_params=_cparams(("parallel", "parallel"), blk),
    )(dy, w_out, gu)


def _row_block(n, width, n_arrays):
    cap = (V7X_VMEM_BYTES // 4) // (2 * 4 * width * n_arrays)
    best = None
    for t in range(16, min(n, cap) + 1, 16):
        if n % t == 0:
            best = t
    return best or n


def placed_map(f, ins, out, *, n_blocks, tb, name):
    def body(*refs):
        refs[-1][...] = f(*[r[...] for r in refs[:-1]]).astype(refs[-1].dtype)

    def spec(fn):
        def index(i):
            x, y, c = _place()
            return fn(i, (c, 2 * x + y)), 0
        return pl.BlockSpec((tb, width), index)

    o_rows, width, o_dtype, o_fn = out
    blk = (sum(a.dtype.itemsize for a, _ in ins) + jnp.dtype(o_dtype).itemsize) * tb * width
    return pl.pallas_call(
        body, name=name, grid=(n_blocks,), in_specs=[spec(fn) for _, fn in ins], out_specs=spec(o_fn),
        out_shape=jax.ShapeDtypeStruct((o_rows, width), o_dtype),
        compiler_params=_cparams(("parallel",), blk),
    )(*[a for a, _ in ins])


def rowmap(f, rows, params, outs, accs=(), *, tb, name):
    rows = [r if isinstance(r, tuple) else (r, r.shape[1], 0) for r in rows]
    S = rows[0][0].shape[0]
    assert S % tb == 0, (name, S, tb)
    n_in, n_out = len(rows) + len(params), len(outs)

    def body(*refs):
        res = f(*[r[...] for r in refs[:n_in]])
        res = res if isinstance(res, (tuple, list)) else (res,)
        o_refs, a_refs = refs[n_in:n_in + n_out], refs[n_in + n_out:]
        for ref, val in zip(o_refs, res[:n_out]):
            ref[...] = val.astype(ref.dtype)
        if a_refs:
            @pl.when(pl.program_id(0) == 0)
            def _():
                for ref in a_refs:
                    ref[...] = jnp.zeros_like(ref)

            for ref, val in zip(a_refs, res[n_out:]):
                ref[...] += val.astype(F32)

    in_specs = [pl.BlockSpec((tb, w), functools.partial(lambda cb, i: (i, cb), cb)) for _, w, cb in rows]
    in_specs += [pl.BlockSpec(p.shape, lambda i: (0, 0)) for p in params]
    out_specs = [pl.BlockSpec((tb, w), lambda i: (i, 0)) for w, _ in outs]
    out_specs += [pl.BlockSpec(tuple(s), lambda i: (0, 0)) for s in accs]
    out_shape = [jax.ShapeDtypeStruct((S, w), dt) for w, dt in outs]
    out_shape += [jax.ShapeDtypeStruct(tuple(s), F32) for s in accs]
    blk = sum(tb * w * a.dtype.itemsize for a, w, _ in rows) + sum(_nbytes(p.shape, p.dtype) for p in params)
    blk += sum(_nbytes((tb, w), dt) for w, dt in outs) + sum(_nbytes(s, F32) for s in accs)
    res = pl.pallas_call(
        body, name=name, grid=(S // tb,), in_specs=in_specs, out_specs=out_specs, out_shape=out_shape,
        compiler_params=_cparams(("arbitrary",) if accs else ("parallel",), blk),
    )(*[r[0] for r in rows], *[pltpu.with_memory_space_constraint(p, pltpu.HBM) for p in params])
    return res


def _rms(x, g):
    return x * lax.rsqrt(jnp.mean(x * x, axis=-1, keepdims=True) + RMS_EPS) * g


def _softplus(z):
    return jnp.maximum(z, 0.0) + jnp.log(1.0 + jnp.exp(-jnp.abs(z)))


def _rwkv_pre(xrk, xlo, w0, w2p, a0, a2p, g2p, k_k, k_a, seg, seg_t):
    k = xrk[:, D_MODEL:2 * D_MODEL]
    w = -_softplus(-(w0 + NN(jnp.tanh(xlo), w2p))) - 0.5
    log_decay = -jnp.exp(w)
    a = jax.nn.sigmoid(a0 + NN(xlo, a2p))
    g = NN(jax.nn.sigmoid(xlo), g2p)
    kk = k * k_k
    norm = jnp.maximum(jnp.sqrt(SEG(kk * kk, seg)), 1e-12)
    kk = kk * SEG(1.0 / norm, seg_t)
    k_mod = k * (1.0 + (a - 1.0) * k_a)
    return log_decay, k_mod, -kk, kk * a, g


def _rwkv_post(wkv, r, k_mod, v, g, r_k, ln_w, ln_b, seg, seg_t):
    inv_n = 1.0 / HEAD_DIM
    mean = SEG(wkv, seg) * inv_n
    cen = wkv - SEG(mean, seg_t)
    var = SEG(cen * cen, seg) * inv_n
    y = cen * SEG(lax.rsqrt(var + GN_EPS), seg_t) * ln_w + ln_b
    bonus = SEG(SEG(r * k_mod * r_k, seg), seg_t) * v
    return (y + bonus) * g


def _qk_norm(q, k, q_gain, k_gain, seg, seg_t, tile_t):
    def norm(x, gain):
        mean_sq = SEG(x * x, seg) * (1.0 / HEAD_DIM)
        return x * SEG(lax.rsqrt(mean_sq + RMS_EPS), seg_t) * SEG(gain, tile_t)

    return norm(q, q_gain) * (HEAD_DIM ** -0.5), norm(k, k_gain)


def _gate_merge(pgate, pa, pb, b_gate):
    sg = jax.nn.sigmoid(pgate + b_gate)
    return sg[:, :D_MODEL] * pa + sg[:, D_MODEL:] * pb


def _group_combine(o0, o1, o2, l0, l1, l2):
    m = jnp.maximum(jnp.maximum(l0, l1), l2)
    es = [jnp.exp(l - m) for l in (l0, l1, l2)]
    den = es[0] + es[1] + es[2]
    return jnp.concatenate([o * (e / den) for o, e in zip((o0, o1, o2), es)], axis=1)


def _each(f, *xs):
    return tuple(f(*args) for args in zip(*xs))


def _attn_block(q, kc, kp, vc, vp, first):
    qi = lax.broadcasted_iota(jnp.int32, (ATTN_BLK, ATTN_BLK), 0)
    kj = lax.broadcasted_iota(jnp.int32, (ATTN_BLK, ATTN_BLK), 1)
    own = kj <= qi
    s_c = _each(lambda a, b: jnp.where(own, NT(a, b), NEG_INF), q, kc)
    s_p = _each(lambda a, b, f: jnp.where((kj >= qi) & (f < 0.5), NT(a, b), NEG_INF), q, kp, first)
    row_max = lambda s: jnp.max(s, axis=-1, keepdims=True)
    row_sum = lambda s: jnp.sum(s, axis=-1, keepdims=True)
    m = _each(lambda c_, p_: jnp.maximum(row_max(c_), row_max(p_)), s_c, s_p)
    e_c, e_p = _each(lambda s, m_: jnp.exp(s - m_), s_c, m), _each(lambda s, m_: jnp.exp(s - m_), s_p, m)
    den = _each(lambda c_, p_: row_sum(c_) + row_sum(p_), e_c, e_p)
    inv = _each(lambda d_: 1.0 / d_, den)
    o = _each(lambda ec, ep, i_, vc_, vp_: (NN(ec, vc_) + NN(ep, vp_)) * i_, e_c, e_p, inv, vc, vp)
    lse = _each(lambda m_, d_: jnp.broadcast_to(m_ + jnp.log(d_), (ATTN_BLK, HEAD_DIM)), m, den)
    return o, lse


def _attn_run(q, k, k_before, v, v_before, first):
    n, flat = len(q[0]), lambda blocks: sum(blocks, ())
    o, lse = _attn_block(flat(q), flat(k), k_before + flat(k[:-1]), flat(v), v_before + flat(v[:-1]),
                         flat(tuple((f,) * n for f in first)))
    split = lambda t: tuple(t[i * n:(i + 1) * n] for i in range(len(q)))
    return split(o), split(lse)


TRI_SEED = 8


def _tri_inverse(n):
    c = n[0].shape[0]
    row = lax.broadcasted_iota(jnp.int32, (c, c), 0)
    col = lax.broadcasted_iota(jnp.int32, (c, c), 1)
    same_block = lambda size: (row >> (size.bit_length() - 1)) == (col >> (size.bit_length() - 1))
    seed = same_block(TRI_SEED)
    p = _each(lambda m: jnp.where(seed, m, 0.0), n)
    t, span = _each(lambda m: (row == col).astype(F32) + m, p), 2
    while span < TRI_SEED:
        p = _each(NN, p, p)
        t = _each(lambda t_, p_: t_ + NN(t_, p_), t, p)
        span *= 2
    size = TRI_SEED
    while size < c:
        joins = same_block(2 * size) & jnp.logical_not(same_block(size))
        t = _each(lambda t_, m: t_ + NN(NN(t_, jnp.where(joins, m, 0.0)), t_), t, n)
        size *= 2
    return t


@jax.custom_vjp
def _tri_solve(n, rhs, t):
    return _each(NN, t, rhs)


def _tri_solve_fwd(n, rhs, t):
    x = _each(NN, t, rhs)
    return x, (t, x)


def _tri_solve_bwd(res, dx):
    t, x = res
    drhs = _each(TN, t, dx)
    return _each(NT, drhs, x), drhs, _each(jnp.zeros_like, t)


_tri_solve.defvjp(_tri_solve_fwd, _tri_solve_bwd)


def _lower_ones(c):
    row = lax.broadcasted_iota(jnp.int32, (c, c), 0)
    col = lax.broadcasted_iota(jnp.int32, (c, c), 1)
    return (row >= col).astype(BF16)


def _ones_dot(ones, x, contract):
    hi, lo = _split_bf16(x)
    dims = (((contract,), (0,)), ((), ()))
    return (lax.dot_general(ones, hi, dims, preferred_element_type=F32)
            + lax.dot_general(ones, lo, dims, preferred_element_type=F32))


@jax.custom_vjp
def _cumsum_rows(x):
    return _ones_dot(_lower_ones(x.shape[0]), x, 1)


_cumsum_rows.defvjp(lambda x: (_ones_dot(_lower_ones(x.shape[0]), x, 1), None),
                    lambda _, g: (_ones_dot(_lower_ones(g.shape[0]), g, 0),))


def _wkv_chunk(s0, r, lw, k, v, a, b, t_inv=None):
    c = r[0].shape[0]
    row = lax.broadcasted_iota(jnp.int32, (c, c), 0)
    col = lax.broadcasted_iota(jnp.int32, (c, c), 1)
    strict, incl = row > col, row >= col
    cat = lambda p, q: jnp.concatenate([p, q], axis=0)
    cum = _each(_cumsum_rows, lw)
    e_neg = _each(lambda c_: jnp.exp(-c_), cum)
    ar = _each(lambda a_, r_, c_, l_: cat(a_ * jnp.exp(c_ - l_), r_ * jnp.exp(c_)), a, r, cum, lw)
    b_t, k_t = _each(jnp.multiply, b, e_neg), _each(jnp.multiply, k, e_neg)
    p_b, p_k, p_s = _each(NT, ar, b_t), _each(NT, ar, k_t), _each(NT, ar, s0)
    n_ab = _each(lambda p: jnp.where(strict, p[:c], 0.0), p_b)
    m_rb = _each(lambda p: jnp.where(incl, p[c:], 0.0), p_b)
    n_ak = _each(lambda p: jnp.where(strict, p[:c], 0.0), p_k)
    m_rk = _each(lambda p: jnp.where(incl, p[c:], 0.0), p_k)
    if t_inv is None:
        t_inv = _tri_inverse(n_ab)
    u = _tri_solve(n_ab, _each(lambda p, n_, v_: p[:c] + NN(n_, v_), p_s, n_ak, v), t_inv)
    y = _each(lambda p, mb, u_, mk, v_: p[c:] + NN(mb, u_) + NN(mk, v_), p_s, m_rb, u, m_rk, v)
    g_end = _each(lambda l_: jnp.exp(jnp.sum(l_, axis=0, keepdims=True)), lw)
    s1 = _each(lambda s_, g_, u_, v_, b_, k_: s_ * g_ + TN(cat(u_, v_), cat(b_, k_) * g_),
               s0, g_end, u, v, b_t, k_t)
    return y, s1, t_inv


def _adamw(w, g, m, v):
    m = ADAM_B1 * m + (1.0 - ADAM_B1) * g
    v = ADAM_B2 * v + (1.0 - ADAM_B2) * jnp.square(g)
    m_hat = m / (1.0 - ADAM_B1 ** ADAM_STEP)
    v_hat = v / (1.0 - ADAM_B2 ** ADAM_STEP)
    delta = -ADAM_LR * (m_hat / (jnp.sqrt(v_hat) + ADAM_EPS) + ADAM_WD * w)
    return delta, m, v


def token_shift_fwd(p, mu, *, tb, name):
    S, W = p.shape
    hb = tb // 8

    def body(p_ref, halo_ref, mu_ref, o_ref):
        i = pl.program_id(0)
        x = p_ref[...]
        before = halo_ref[7:8, :] * (i > 0).astype(F32)
        row = lax.broadcasted_iota(jnp.int32, (tb, W), 0)
        prev = jnp.where(row == 0, before, pltpu.roll(x, 1, 0))
        o_ref[...] = x + (prev - x) * mu_ref[...]

    blk = (2 * tb + 8) * W * 4
    return pl.pallas_call(
        body, name=name, grid=(S // tb,),
        in_specs=[pl.BlockSpec((tb, W), lambda i: (i, 0)),
                  pl.BlockSpec((8, W), lambda i: (jnp.maximum(i * hb - 1, 0), 0)),
                  pl.BlockSpec((1, W), lambda i: (0, 0))],
        out_specs=pl.BlockSpec((tb, W), lambda i: (i, 0)),
        out_shape=jax.ShapeDtypeStruct((S, W), F32),
        compiler_params=_cparams(("parallel",), blk),
    )(p, p, mu)


def token_shift_bwd(dxs, p, mu, *, tb, name):
    S, W = p.shape
    hb, nb = tb // 8, S // tb

    def body(d_ref, dnext_ref, p_ref, halo_ref, mu_ref, dp_ref, dmu_ref):
        i = pl.program_id(0)
        d, x, mu_v = d_ref[...], p_ref[...], mu_ref[...]
        row = lax.broadcasted_iota(jnp.int32, (tb, W), 0)
        before = halo_ref[7:8, :] * (i > 0).astype(F32)
        prev = jnp.where(row == 0, before, pltpu.roll(x, 1, 0))
        t = d * mu_v
        after = dnext_ref[0:1, :] * mu_v * (i < nb - 1).astype(F32)
        nxt = jnp.where(row == tb - 1, after, pltpu.roll(t, tb - 1, 0))
        dp_ref[...] = (d - t + nxt).astype(dp_ref.dtype)

        @pl.when(i == 0)
        def _():
            dmu_ref[...] = jnp.zeros_like(dmu_ref)

        dmu_ref[...] += jnp.sum(d * (prev - x), axis=0, keepdims=True)

    blk = (3 * tb + 16) * W * 4
    return pl.pallas_call(
        body, name=name, grid=(nb,),
        in_specs=[pl.BlockSpec((tb, W), lambda i: (i, 0)),
                  pl.BlockSpec((8, W), lambda i: (jnp.minimum((i + 1) * hb, S // 8 - 1), 0)),
                  pl.BlockSpec((tb, W), lambda i: (i, 0)),
                  pl.BlockSpec((8, W), lambda i: (jnp.maximum(i * hb - 1, 0), 0)),
                  pl.BlockSpec((1, W), lambda i: (0, 0))],
        out_specs=[pl.BlockSpec((tb, W), lambda i: (i, 0)), pl.BlockSpec((1, W), lambda i: (0, 0))],
        out_shape=[jax.ShapeDtypeStruct((S, W), BF16), jax.ShapeDtypeStruct((1, W), F32)],
        compiler_params=_cparams(("arbitrary",), blk),
    )(dxs, dxs, p, p, mu)


def _head_cols(h):
    return pl.ds(h * HEAD_DIM, HEAD_DIM)


def wkv_fwd(xs_rk, lw, k, a, b):
    S = lw.shape[0]
    C, nc, G, N = WKV_CHUNK, S // WKV_CHUNK, WKV_HEADS_PER_STEP, HEAD_DIM

    def body(r_ref, lw_ref, k_ref, v_ref, a_ref, b_ref, y_ref, st_ref, ti_ref, state):
        @pl.when(pl.program_id(1) == 0)
        def _():
            state[...] = jnp.zeros_like(state)

        heads = lambda ref: tuple(ref[:, _head_cols(h)] for h in range(G))
        s0 = tuple(state[h] for h in range(G))
        y, s1, t_inv = _wkv_chunk(s0, heads(r_ref), heads(lw_ref), heads(k_ref), heads(v_ref), heads(a_ref),
                                  heads(b_ref))
        for h in range(G):
            st_ref[h] = s0[h]
            ti_ref[h] = t_inv[h]
            y_ref[:, _head_cols(h)] = y[h]
            state[h] = s1[h]

    W = G * N
    seq = lambda j: pl.BlockSpec((C, W), functools.partial(lambda j, g, c: (c, j + g), j))
    per = D_MODEL // W
    per_chunk = pl.BlockSpec((None, G, N, N), lambda g, c: (c, g, 0, 0))
    return pl.pallas_call(
        body, name="wkv_fwd", grid=(RWKV_HEADS // G, nc),
        in_specs=[seq(0), seq(0), seq(0), seq(2 * per), seq(0), seq(0)],
        out_specs=[seq(0), per_chunk, per_chunk],
        out_shape=[jax.ShapeDtypeStruct((S, D_MODEL), F32)] + [jax.ShapeDtypeStruct((nc, RWKV_HEADS, N, N), F32)] * 2,
        scratch_shapes=[pltpu.VMEM((G, N, N), F32)],
        compiler_params=_cparams(("parallel", "arbitrary"), 8 * C * W * 4 + 3 * G * N * N * 4),
    )(xs_rk, lw, k, xs_rk, a, b)


def wkv_bwd(xs_rk, lw, k, a, b, states, t_invs, dy):
    S = lw.shape[0]
    C, nc, G, N = WKV_CHUNK, S // WKV_CHUNK, WKV_HEADS_PER_STEP, HEAD_DIM

    def body(r_ref, lw_ref, k_ref, v_ref, a_ref, b_ref, st_ref, ti_ref, dy_ref,
             dr_ref, dlw_ref, dk_ref, dv_ref, da_ref, db_ref, dstate):
        @pl.when(pl.program_id(1) == 0)
        def _():
            dstate[...] = jnp.zeros_like(dstate)

        heads = lambda ref: tuple(ref[:, _head_cols(h)] for h in range(G))
        t_inv = tuple(ti_ref[h] for h in range(G))
        chunk = lambda *args: _wkv_chunk(*args, t_inv)[:2]
        _, pull = jax.vjp(chunk, tuple(st_ref[h] for h in range(G)), heads(r_ref), heads(lw_ref),
                          heads(k_ref), heads(v_ref), heads(a_ref), heads(b_ref))
        ds0, *grads = pull((heads(dy_ref), tuple(dstate[h] for h in range(G))))
        for h in range(G):
            dstate[h] = ds0[h]
            for ref, grad in zip((dr_ref, dlw_ref, dk_ref, dv_ref, da_ref, db_ref), grads):
                ref[:, _head_cols(h)] = grad[h]

    W = G * N
    seq = lambda j: pl.BlockSpec((C, W), functools.partial(lambda j, g, c: (nc - 1 - c, j + g), j))
    per = D_MODEL // W
    st = pl.BlockSpec((None, G, N, N), lambda g, c: (nc - 1 - c, g, 0, 0))
    return pl.pallas_call(
        body, name="wkv_bwd", grid=(RWKV_HEADS // G, nc),
        in_specs=[seq(0), seq(0), seq(0), seq(2 * per), seq(0), seq(0), st, st, seq(0)],
        out_specs=[seq(0)] * 6, out_shape=[jax.ShapeDtypeStruct((S, D_MODEL), F32)] * 6,
        scratch_shapes=[pltpu.VMEM((G, N, N), F32)],
        compiler_params=_cparams(("parallel", "arbitrary"), 14 * C * W * 4 + 3 * G * N * N * 4),
    )(xs_rk, lw, k, xs_rk, a, b, states, t_invs, dy)


def _first_flag(i, per_seq):
    return (lax.rem(i, per_seq) == 0).astype(F32)


def _view(a):
    return a if isinstance(a, tuple) else (a, 0)


def _block_rows(half):
    return pl.ds(half * ATTN_BLK, ATTN_BLK)


def _block_heads(ref, half):
    return tuple(ref[_block_rows(half), _head_cols(h)] for h in range(ATTN_HPG))


def _run_heads(ref):
    return tuple(_block_heads(ref, b) for b in range(ATTN_RUN))


def attn_fwd(q, k, v, per_seq, name):
    (q, q_col), (k, k_col), (v, v_col) = _view(q), _view(k), _view(v)
    R, N = q.shape[0], GROUP_W
    n_runs = R // (ATTN_RUN * ATTN_BLK)

    def body(q_ref, k_ref, kb_ref, v_ref, vb_ref, o_ref, lse_ref):
        run = pl.program_id(0)
        first = tuple(_first_flag(ATTN_RUN * run + b, per_seq) for b in range(ATTN_RUN))
        o, lse = _attn_run(_run_heads(q_ref), _run_heads(k_ref), _block_heads(kb_ref, 0), _run_heads(v_ref),
                           _block_heads(vb_ref, 0), first)
        for b in range(ATTN_RUN):
            for h in range(ATTN_HPG):
                o_ref[_block_rows(b), _head_cols(h)] = o[b][h]
                lse_ref[_block_rows(b), _head_cols(h)] = lse[b][h]

    cur = lambda col: pl.BlockSpec((ATTN_RUN * ATTN_BLK, N), lambda i: (i, col))
    prv = lambda col: pl.BlockSpec((ATTN_BLK, N), lambda i: (jnp.maximum(ATTN_RUN * i - 1, 0), col))
    return pl.pallas_call(
        body, name=name, grid=(n_runs,), in_specs=[cur(q_col), cur(k_col), prv(k_col), cur(v_col), prv(v_col)],
        out_specs=[cur(0), cur(0)], out_shape=[jax.ShapeDtypeStruct((R, N), F32)] * 2,
        compiler_params=_cparams(("parallel",), (5 * ATTN_RUN + 2) * ATTN_BLK * N * 4),
    )(q, k, k, v, v)


def attn_bwd(q, k, v, do, dlse, per_seq, name):
    views = [_view(a) for a in (q, k, v, do, dlse)]
    (q, q_col), (k, k_col), (v, v_col), (do, do_col), (dlse, dl_col) = views
    R, N = q.shape[0], GROUP_W
    n_runs = R // (ATTN_RUN * ATTN_BLK)

    def body(q_ref, k_ref, kb_ref, v_ref, vb_ref, do_ref, dl_ref, dq_ref, dk_ref, dv_ref, carry_k, carry_v):
        step = pl.program_id(0)
        run = n_runs - 1 - step
        first = tuple(_first_flag(ATTN_RUN * run + b, per_seq) for b in range(ATTN_RUN))

        @pl.when(step == 0)
        def _():
            carry_k[...] = jnp.zeros_like(carry_k)
            carry_v[...] = jnp.zeros_like(carry_v)

        _, pull = jax.vjp(functools.partial(_attn_run, first=first), _run_heads(q_ref), _run_heads(k_ref),
                          _block_heads(kb_ref, 0), _run_heads(v_ref), _block_heads(vb_ref, 0))
        dq, dk, dk_before, dv, dv_before = pull((_run_heads(do_ref), _run_heads(dl_ref)))
        old_k, old_v = _block_heads(carry_k, 0), _block_heads(carry_v, 0)
        last = ATTN_RUN - 1
        for h in range(ATTN_HPG):
            cols = _head_cols(h)
            for b in range(ATTN_RUN):
                dq_ref[_block_rows(b), cols] = dq[b][h]
                dk_ref[_block_rows(b), cols] = dk[b][h] + old_k[h] if b == last else dk[b][h]
                dv_ref[_block_rows(b), cols] = dv[b][h] + old_v[h] if b == last else dv[b][h]
            carry_k[:, cols] = dk_before[h]
            carry_v[:, cols] = dv_before[h]

    cur = lambda col: pl.BlockSpec((ATTN_RUN * ATTN_BLK, N), lambda i: (n_runs - 1 - i, col))
    prv = lambda col: pl.BlockSpec((ATTN_BLK, N), lambda i: (jnp.maximum(ATTN_RUN * (n_runs - 1 - i) - 1, 0), col))
    return pl.pallas_call(
        body, name=name, grid=(n_runs,),
        in_specs=[cur(q_col), cur(k_col), prv(k_col), cur(v_col), prv(v_col), cur(do_col), cur(dl_col)],
        out_specs=[cur(0)] * 3, out_shape=[jax.ShapeDtypeStruct((R, N), F32)] * 3,
        scratch_shapes=[pltpu.VMEM((ATTN_BLK, N), F32)] * 2,
        compiler_params=_cparams(("arbitrary",), (8 * ATTN_RUN + 4) * ATTN_BLK * N * 4),
    )(q, k, k, v, v, do, dlse)


def by_residue(u, d):
    if d == 1:
        return u
    return u.reshape(u.shape[0] // d, d, GROUP_W).transpose(1, 0, 2).reshape(u.shape)


def by_position(u, d):
    if d == 1:
        return u
    return u.reshape(d, u.shape[0] // d, GROUP_W).transpose(1, 0, 2).reshape(u.shape)


def group_columns(t, col_block, d):
    if d == 1:
        return (t, col_block)
    return by_residue(t[:, GROUP_W * col_block:GROUP_W * (col_block + 1)], d)


def _ffn_fwd(x, norm, w_in, w_out, tag, token):
    h = rowmap(lambda x_b, g, tok: _rms(x_b, g) + tok[0:1, 0:1], [x], [norm, token], [(D_MODEL, BF16)], tb=512,
               name=tag + "_norm")[0]
    gu, act = ffn_in_act(h, w_in, tag + "_in")
    y = matmul(act, w_out, "nn", tag + "_out", add=x, scale=0.5)
    return y, (x, h, gu, act)


def _ffn_bwd(dy, saved, norm, w_in, w_out, tag, on_weight_grads):
    x, h, gu, act = saved
    no_token = jnp.zeros((8, 128), F32)
    dw_out = matmul(act, dy, "tn", tag + "_dwout", scale=0.5)
    dgu = ffn_dact_dgu(dy, w_out, gu, 0.5, tag + "_dgu")
    dw_in = matmul_cs(h, dgu, "tn", tag + "_dwin", no_token)
    dh = matmul_cs(dgu, w_in, "nt", tag + "_dh", on_weight_grads(dw_in, dw_out))

    def norm_bwd(x_b, dh_b, dy_b, g):
        dx, dg = jax.vjp(_rms, x_b, g)[1](dh_b)
        return dy_b + dx, dg

    dx, dnorm = rowmap(norm_bwd, [x, dh, dy], [norm], [(D_MODEL, F32)], [(1, D_MODEL)], tb=256,
                       name=tag + "_dnorm")
    return dx, dnorm, dw_in, dw_out


def layer_step(x, tgt, W, P, start_token, more_weights, on_mixer_grads, on_ffn1_grads):
    S = x.shape[0]
    x1, ffn1_saved = _ffn_fwd(x, P["ffn1_norm"], W["ffn1_w_in"], W["ffn1_w_out"], "ffn1", start_token)
    W = {**W, **more_weights("mixer", x1)}
    head_of = lambda n: jnp.arange(n)[:, None] // HEAD_DIM == jnp.arange(n // HEAD_DIM)[None, :]
    seg, seg_a = head_of(D_MODEL).astype(BF16), head_of(ATTN_WIDTH).astype(BF16)
    seg_t, seg_a_t = seg.T, seg_a.T
    tile_t = (jnp.arange(HEAD_DIM)[:, None] == jnp.arange(ATTN_WIDTH)[None, :] % HEAD_DIM).astype(BF16)
    qk_params = [P["attn_q_norm"], P["attn_k_norm"], seg_a, seg_a_t, tile_t]
    w_rkv, w_lora = W["w_in"][:, :RKV], W["w_in"][:, RKV:RKV + LORA]
    w_qkv = W["w_in"][:, RKV + LORA:RKV + LORA + 3 * ATTN_WIDTH]
    w_gate = W["w_in"][:, RKV + LORA + 3 * ATTN_WIDTH:]
    mu_rk, mu_lo = P["rwkv_mu"][:, :RKV], P["rwkv_mu"][:, RKV:]
    zeros = lambda n: jnp.zeros((n, D_MODEL), F32)
    w2p = jnp.concatenate([W["rwkv_w2"], zeros(LORA - LORA_W)], axis=0)
    a2p = jnp.concatenate([zeros(LORA_W), W["rwkv_a2"], zeros(LORA_G)], axis=0)
    g2p = jnp.concatenate([zeros(LORA_W + LORA_A), W["rwkv_g2"]], axis=0)
    pre_params = [P["rwkv_w0"], w2p, P["rwkv_a0"], a2p, g2p, P["rwkv_k_k"], P["rwkv_k_a"], seg, seg_t]
    post_params = [P["rwkv_r_k"], P["rwkv_ln_w"], P["rwkv_ln_b"], seg, seg_t]
    col = lambda arr, j: (arr, D_MODEL, j)

    h = rowmap(_rms, [x1], [P["mix_norm"]], [(D_MODEL, BF16)], tb=512, name="mix_norm")[0]
    p_rk = matmul(h, w_rkv, "nn", "proj_rkv")
    p_lo = matmul(h, w_lora, "nn", "proj_lora")
    p_qkv = matmul(h, w_qkv, "nn", "proj_qkv")
    p_gate = matmul(h, w_gate, "nn", "proj_gate")
    xs_rk = token_shift_fwd(p_rk, mu_rk, tb=256, name="shift_rk")
    xs_lo = token_shift_fwd(p_lo, mu_lo, tb=256, name="shift_lora")
    lw, k_mod, a_neg, b_kk, g = rowmap(
        _rwkv_pre, [xs_rk, xs_lo], pre_params, [(D_MODEL, F32)] * 5, tb=256, name="rwkv_pre")
    wkv, states, t_invs = wkv_fwd(xs_rk, lw, k_mod, a_neg, b_kk)
    post_rows = [wkv, col(xs_rk, 0), k_mod, col(xs_rk, 2), g]
    y_a = rowmap(_rwkv_post, post_rows, post_params, [(D_MODEL, BF16)], tb=256, name="rwkv_post")[0]

    qk_rows = [(p_qkv, ATTN_WIDTH, 0), (p_qkv, ATTN_WIDTH, 1)]
    qn, kn = rowmap(_qk_norm, qk_rows, qk_params, [(ATTN_WIDTH, F32)] * 2, tb=256, name="qk_norm")
    dil = [d for _, d in ATTN_PAIRS]
    groups = range(len(dil))
    per_seq = [S // d // ATTN_BLK for d in dil]
    v_first = 2 * ATTN_WIDTH // GROUP_W
    q_s = [group_columns(qn, g, dil[g]) for g in groups]
    k_s = [group_columns(kn, g, dil[g]) for g in groups]
    v_s = [group_columns(p_qkv, v_first + g, dil[g]) for g in groups]
    attn = [attn_fwd(q_s[g], k_s[g], v_s[g], per_seq[g], "attn_fwd_%d" % g) for g in groups]
    o_lse = [by_position(attn[g][j], dil[g]) for j in range(2) for g in groups]
    y_b = rowmap(_group_combine, o_lse, [], [(ATTN_WIDTH, BF16)], tb=512, name="attn_combine")[0]

    W = {**W, **more_weights("out", y_b)}
    pa = matmul(y_a, W["w_proj_rwkv"], "nn", "proj_a")
    pb = matmul(y_b, W["w_proj_attn"], "nn", "proj_b")
    merged = rowmap(_gate_merge, [p_gate, pa, pb], [P["b_gate"]], [(D_MODEL, BF16)], tb=256, name="merge")[0]
    x2 = matmul(merged, W["w_out"], "nn", "mix_out", add=x1)
    x3, ffn2_saved = _ffn_fwd(x2, P["ffn2_norm"], W["ffn2_w_in"], W["ffn2_w_out"], "ffn2",
                              jnp.zeros_like(start_token))

    def loss_head(y_b_, t_b):
        err = y_b_ - t_b
        return err * (1.0 / D_MODEL), (0.5 / D_MODEL) * jnp.sum(err * err, axis=0, keepdims=True)

    dx3, loss_cols = rowmap(loss_head, [x3, tgt], [], [(D_MODEL, F32)], [(1, D_MODEL)], tb=512, name="loss")

    gW, gP = {}, {}
    dx2, gP["ffn2_norm"], gW["ffn2_w_in"], gW["ffn2_w_out"] = _ffn_bwd(
        dx3, ffn2_saved, P["ffn2_norm"], W["ffn2_w_in"], W["ffn2_w_out"], "ffn2",
        lambda dw_in, dw_out: jnp.zeros_like(start_token))

    dmerged = matmul(dx2, W["w_out"], "nt", "d_merged")
    gW["w_out"] = matmul(merged, dx2, "tn", "dw_out")

    def merge_bwd(pg, pa_b, pb_b, dm, bg):
        return jax.vjp(_gate_merge, pg, pa_b, pb_b, bg)[1](dm)

    dp_gate, dpa, dpb, gP["b_gate"] = rowmap(
        merge_bwd, [p_gate, pa, pb, dmerged], [P["b_gate"]],
        [(2 * D_MODEL, BF16), (D_MODEL, BF16), (D_MODEL, BF16)], [(1, 2 * D_MODEL)], tb=256, name="merge_bwd")
    dy_a = matmul(dpa, W["w_proj_rwkv"], "nt", "d_ya")
    gW["w_proj_rwkv"] = matmul(y_a, dpa, "tn", "dw_proj_a")
    dy_b = matmul(dpb, W["w_proj_attn"], "nt", "d_yb")
    gW["w_proj_attn"] = matmul(y_b, dpb, "tn", "dw_proj_b")

    def combine_bwd(*blocks):
        return jax.vjp(_group_combine, *blocks[:-1])[1](blocks[-1])

    d_o_lse = rowmap(combine_bwd, o_lse + [dy_b], [], [(GROUP_W, F32)] * 6, tb=256, name="attn_combine_bwd")
    d_attn = [attn_bwd(q_s[g], k_s[g], v_s[g], by_residue(d_o_lse[g], dil[g]), by_residue(d_o_lse[3 + g], dil[g]),
                       per_seq[g], "attn_bwd_%d" % g) for g in groups]

    def qk_norm_bwd(q_b, k_b, *rest):
        dqkv, (qg, kg, sg, sgt, tl) = rest[:9], rest[9:]
        f = lambda *a: _qk_norm(*a, sg, sgt, tl)
        dqn, dkn = jnp.concatenate(dqkv[0:3], axis=1), jnp.concatenate(dqkv[3:6], axis=1)
        dq, dk, dqg, dkg = jax.vjp(f, q_b, k_b, qg, kg)[1]((dqn, dkn))
        return jnp.concatenate([dq, dk, *dqkv[6:9]], axis=1), dqg, dkg

    dp_qkv, gP["attn_q_norm"], gP["attn_k_norm"] = rowmap(
        qk_norm_bwd, qk_rows + [by_position(d_attn[g][j], dil[g]) for j in range(3) for g in groups], qk_params,
        [(3 * ATTN_WIDTH, BF16)], [(1, HEAD_DIM)] * 2, tb=256, name="qk_norm_bwd")

    def post_bwd(wkv_b, r_b, k_b, v_b, g_b, d_b, r_k, ln_w, ln_b, sg, sgt):
        f = lambda *a: _rwkv_post(*a, sg, sgt)
        return jax.vjp(f, wkv_b, r_b, k_b, v_b, g_b, r_k, ln_w, ln_b)[1](d_b)

    dwkv, dr_p, dk_p, dv_p, dg, gP["rwkv_r_k"], gP["rwkv_ln_w"], gP["rwkv_ln_b"] = rowmap(
        post_bwd, post_rows + [dy_a], post_params, [(D_MODEL, F32)] * 5, [(1, D_MODEL)] * 3, tb=128,
        name="rwkv_post_bwd")
    dr_w, dlw, dk_w, dv_w, da_neg, db_kk = wkv_bwd(xs_rk, lw, k_mod, a_neg, b_kk, states, t_invs, dwkv)

    def pre_bwd(xrk_b, xlo_b, dlw_b, dkw_b, dkp_b, da_b, db_b, dg_b, drp_b, drw_b, dvp_b, dvw_b,
                w0, w2, a0, a2, g2, k_k, k_a, sg, sgt):
        f = lambda *a: _rwkv_pre(*a, sg, sgt)
        pull = jax.vjp(f, xrk_b, xlo_b, w0, w2, a0, a2, g2, k_k, k_a)[1]
        dxrk, dxlo, *dpar = pull((dlw_b, dkw_b + dkp_b, da_b, db_b, dg_b))
        direct = jnp.concatenate([drp_b + drw_b, jnp.zeros_like(drp_b), dvp_b + dvw_b], axis=1)
        return (dxrk + direct, dxlo, *dpar)

    pre_rows = [xs_rk, xs_lo, dlw, dk_w, dk_p, da_neg, db_kk, dg, dr_p, dr_w, dv_p, dv_w]
    dxs_rk, dxs_lo, gP["rwkv_w0"], dw2p, gP["rwkv_a0"], da2p, dg2p, gP["rwkv_k_k"], gP["rwkv_k_a"] = rowmap(
        pre_bwd, pre_rows, pre_params, [(RKV, F32), (LORA, F32)],
        [(1, D_MODEL), (LORA, D_MODEL), (1, D_MODEL), (LORA, D_MODEL), (LORA, D_MODEL), (1, D_MODEL), (1, D_MODEL)],
        tb=128, name="rwkv_pre_bwd")
    gW["rwkv_w2"] = dw2p[:LORA_W]
    gW["rwkv_a2"] = da2p[LORA_W:LORA_W + LORA_A]
    gW["rwkv_g2"] = dg2p[LORA_W + LORA_A:]
    dp_rk, dmu_rk = token_shift_bwd(dxs_rk, p_rk, mu_rk, tb=256, name="shift_rk_bwd")
    dp_lo, dmu_lo = token_shift_bwd(dxs_lo, p_lo, mu_lo, tb=256, name="shift_lora_bwd")
    gP["rwkv_mu"] = jnp.concatenate([dmu_rk, dmu_lo], axis=1)

    dh = matmul(dp_rk, w_rkv, "nt", "dh_rkv")
    dh = matmul(dp_lo, w_lora, "nt", "dh_lora", add=dh)
    dh = matmul(dp_qkv, w_qkv, "nt", "dh_qkv", add=dh)
    dh = matmul(dp_gate, w_gate, "nt", "dh_gate", add=dh)
    gW["w_in"] = jnp.concatenate([
        matmul(h, dp_rk, "tn", "dw_rkv"), matmul(h, dp_lo, "tn", "dw_lora"),
        matmul(h, dp_qkv, "tn", "dw_qkv"), matmul(h, dp_gate, "tn", "dw_gate")], axis=1)

    token = on_mixer_grads(gW)

    def norm_bwd(x_b, dh_b, dy_b, gn, tok):
        dx, dgn = jax.vjp(_rms, x_b, gn)[1](dh_b)
        return dy_b + dx + tok[0:1, 0:1], dgn

    dx1, gP["mix_norm"] = rowmap(norm_bwd, [x1, dh, dx2], [P["mix_norm"], token], [(D_MODEL, F32)],
                                 [(1, D_MODEL)], tb=256, name="mix_norm_bwd")
    dx, gP["ffn1_norm"], gW["ffn1_w_in"], gW["ffn1_w_out"] = _ffn_bwd(
        dx1, ffn1_saved, P["ffn1_norm"], W["ffn1_w_in"], W["ffn1_w_out"], "ffn1", on_ffn1_grads)
    return loss_cols, dx, gW, gP


N_SHARDS = 4
OTHER_CHIPS = N_SHARDS - 1
BIG = (("ffn1_w_in", (D_MODEL, 2 * D_FF), 1), ("ffn1_w_out", (D_FF, D_MODEL), 0),
       ("w_in", (D_MODEL, 7712), 1), ("rwkv_w2", (LORA_W, D_MODEL), 1), ("rwkv_a2", (LORA_A, D_MODEL), 1),
       ("rwkv_g2", (LORA_G, D_MODEL), 1), ("w_proj_rwkv", (D_MODEL, D_MODEL), 0),
       ("w_proj_attn", (ATTN_WIDTH, D_MODEL), 1), ("w_out", (D_MODEL, D_MODEL), 0),
       ("ffn2_w_in", (D_MODEL, 2 * D_FF), 1), ("ffn2_w_out", (D_FF, D_MODEL), 0))
SMALL = (("ffn1_norm", 1024), ("mix_norm", 1024), ("b_gate", 2048), ("rwkv_mu", 3360), ("rwkv_w0", 1024),
         ("rwkv_a0", 1024), ("rwkv_k_k", 1024), ("rwkv_k_a", 1024), ("rwkv_r_k", 1024), ("rwkv_ln_w", 1024),
         ("rwkv_ln_b", 1024), ("attn_q_norm", 64), ("attn_k_norm", 64), ("ffn2_norm", 1024))
WEIGHT_ORDER = ("ffn1_norm", "ffn1_w_in", "ffn1_w_out", "mix_norm", "w_in", "b_gate", "rwkv_mu", "rwkv_w0",
                "rwkv_w2", "rwkv_a0", "rwkv_a2", "rwkv_g2", "rwkv_k_k", "rwkv_k_a", "rwkv_r_k", "rwkv_ln_w",
                "rwkv_ln_b", "attn_q_norm", "attn_k_norm", "w_proj_rwkv", "w_proj_attn", "w_out", "ffn2_norm",
                "ffn2_w_in", "ffn2_w_out")


LORA_PARTS = ("rwkv_w2", "rwkv_a2", "rwkv_g2")
BLOCK_MAJOR = ("ffn1_w_in", "ffn2_w_in")
FIRST_FFN = ("ffn1_w_in", "ffn1_w_out")
MIXER_IN = ("w_in", "lora")
SMALL_USED = D_MODEL + sum(n for _, n in SMALL)
SMALL_W = -(-SMALL_USED // 128) * 128


def _travel():
    out = {}
    for name, shape, axis in BIG:
        if name == LORA_PARTS[0]:
            out["lora"] = ((LORA, D_MODEL), 1)
        elif name not in LORA_PARTS:
            out[name] = (shape, axis)
    return out


def local_blocks(vals):
    out = {n: vals[n] for n in _travel() if n != "lora"}
    out["lora"] = jnp.concatenate([vals[n] for n in LORA_PARTS], axis=0)
    return out


def split_lora(t):
    return {"rwkv_w2": t[:LORA_W], "rwkv_a2": t[LORA_W:LORA_W + LORA_A], "rwkv_g2": t[LORA_W + LORA_A:]}


def blocks_to_full(name, blocks):
    shape, axis = _travel()[name]
    if name in BLOCK_MAJOR:
        return blocks
    if axis == 0:
        return blocks.reshape(shape)
    return blocks.transpose(1, 0, 2).reshape(shape)


def full_to_blocks(name, full):
    shape, axis = _travel()[name]
    if name in BLOCK_MAJOR:
        return full
    if axis == 0:
        return full.reshape(N_SHARDS, shape[0] // N_SHARDS, shape[1])
    return full.reshape(shape[0], N_SHARDS, shape[1] // N_SHARDS).transpose(1, 0, 2)


def pack_small(vals, head):
    parts = [head] + [vals[name].reshape(1, n) for name, n in SMALL]
    parts.append(jnp.zeros((1, SMALL_W - SMALL_USED), F32))
    return jnp.concatenate(parts, axis=1)


def unpack_small(vec, shapes):
    out, off = {}, D_MODEL
    for name, n in SMALL:
        out[name] = vec[:, off:off + n].reshape(shapes[name])
        off += n
    return out


def _place():
    return lax.axis_index("x"), lax.axis_index("y"), lax.axis_index("c")


def _other_chips(x, y):
    return [(1 - x, y), (x, 1 - y), (1 - x, 1 - y)]


def _remote(src, dst, send_sem, recv_sem, device):
    return pltpu.make_async_remote_copy(src_ref=src, dst_ref=dst, send_sem=send_sem, recv_sem=recv_sem,
                                        device_id=device, device_id_type=MESH)


def _half(ref, who):
    hr = ref.shape[-2] // 2
    rows = pl.ds(pl.multiple_of(who * hr, 8), hr)
    return ref.at[rows] if len(ref.shape) == 2 else ref.at[:, rows]


HBM_REF = pl.BlockSpec(memory_space=pl.ANY)
COMM_PARAMS = dict(compiler_params=pltpu.CompilerParams(has_side_effects=True))


def gather_weights(blocks):
    n = len(blocks)

    def body(*refs):
        ins, outs = refs[:n], refs[n:2 * n]
        ici_send, ici_recv, d2d_send, d2d_recv = refs[2 * n:]
        x, y, c = _place()
        me, sibling, chips = 2 * x + y, (x, y, 1 - c), _other_chips(x, y)
        first = [_remote(_half(ins[t], c), _half(outs[t].at[me], c), ici_send.at[k, t], ici_recv.at[k, t],
                         (px, py, c)) for k, (px, py) in enumerate(chips) for t in range(n)]
        for cp in first:
            cp.start()
        passed = []
        for k, (px, py) in enumerate(chips):
            for t in range(n):
                landed = _half(outs[t].at[2 * px + py], c)
                _remote(landed, landed, ici_send.at[k, t], ici_recv.at[k, t], (px, py, c)).wait_recv()
                cp = _remote(landed, landed, d2d_send.at[k, t], d2d_recv.at[k, t], sibling)
                cp.start()
                passed.append(cp)
        for k, (px, py) in enumerate(chips):
            for t in range(n):
                other = _half(outs[t].at[2 * px + py], 1 - c)
                _remote(other, other, d2d_send.at[k, t], d2d_recv.at[k, t], sibling).wait_recv()
        for cp in first + passed:
            cp.wait_send()

    res = pl.pallas_call(
        body, name="gather_weights", in_specs=[HBM_REF] * n, out_specs=[HBM_REF] * n,
        out_shape=[jax.ShapeDtypeStruct((N_SHARDS,) + b.shape, b.dtype) for b in blocks],
        scratch_shapes=[pltpu.SemaphoreType.DMA((3, n))] * 4, **COMM_PARAMS)(*blocks)
    me = 2 * lax.axis_index("x") + lax.axis_index("y")
    return [lax.dynamic_update_slice(g, b[None], (me, 0, 0)) for g, b in zip(res, blocks)]


def _gather_copies(ins, outs, send_sem, recv_sem):
    x, y, c = _place()
    return [_remote(_half(ins[t], c), _half(outs[t].at[2 * x + y], c), send_sem(k, t), recv_sem(k, t), (px, py, c))
            for k, (px, py) in enumerate(_other_chips(x, y)) for t in range(len(ins))]


def split_start(copies, sources, landing_shapes, name):
    n = len(sources)
    n_cp = OTHER_CHIPS * n

    def body(*refs):
        srcs, dsts = refs[:n], refs[n:2 * n]
        sems, token = refs[2 * n:2 * n + 2 * n_cp], refs[-1]
        for cp in copies(srcs, dsts, lambda k, t: sems[k * n + t], lambda k, t: sems[n_cp + k * n + t]):
            cp.start()
        token[...] = jnp.zeros_like(token)

    hbm = lambda a: pltpu.with_memory_space_constraint(a, pltpu.HBM)
    buffers = list(sources) + [lax.empty(shape, s.dtype) for shape, s in zip(landing_shapes, sources)]
    res = pl.pallas_call(
        body, name=name,
        out_shape=(*[pltpu.SemaphoreType.DMA(())] * (2 * n_cp),
                   *[pltpu.HBM(a.shape, a.dtype) for a in buffers], jax.ShapeDtypeStruct((8, 128), F32)),
        in_specs=[SPLIT_HBM] * (2 * n),
        out_specs=(*[SPLIT_SEM] * (2 * n_cp), *[SPLIT_HBM] * (2 * n), pl.BlockSpec(memory_space=pltpu.VMEM)),
        input_output_aliases={t: 2 * n_cp + t for t in range(2 * n)}, **SPLIT_PARAMS,
    )(*[hbm(a) for a in buffers])
    return (copies, n, res[:-1]), res[-1]


def split_wait(handles, after, name):
    copies, n, held = handles
    n_cp = OTHER_CHIPS * n
    sems, thru = held[:2 * n_cp], held[2 * n_cp:]

    def body(*refs):
        srcs, dsts = refs[:n], refs[n:2 * n]
        sem_refs = refs[2 * n:2 * n + 2 * n_cp]
        for cp in copies(srcs, dsts, lambda k, t: sem_refs[k * n + t], lambda k, t: sem_refs[n_cp + k * n + t]):
            cp.wait_send()
            cp.wait_recv()

    res = pl.pallas_call(
        body, name=name, out_shape=tuple(pltpu.HBM(a.shape, a.dtype) for a in thru),
        in_specs=[SPLIT_HBM] * (2 * n) + [SPLIT_SEM] * (2 * n_cp) + [pl.BlockSpec(memory_space=pl.ANY)],
        out_specs=tuple([SPLIT_HBM] * (2 * n)), input_output_aliases={t: t for t in range(2 * n)}, **SPLIT_PARAMS,
    )(*thru, *sems, after)
    return list(res[n:])


def gather_start(blocks, name):
    return split_start(_gather_copies, blocks, [(N_SHARDS,) + b.shape for b in blocks], name)


def pass_halves(gathered, blocks, name):
    n = len(gathered)

    def body(*refs):
        outs = refs[n:2 * n]
        send_sems, recv_sems = refs[2 * n:]
        x, y, c = _place()
        slots = [2 * px + py for px, py in _other_chips(x, y)]
        give = [_remote(_half(outs[t].at[s], c), _half(outs[t].at[s], c), send_sems.at[k, t], recv_sems.at[k, t],
                        (x, y, 1 - c)) for k, s in enumerate(slots) for t in range(n)]
        for cp in give:
            cp.start()
        for k, s in enumerate(slots):
            for t in range(n):
                other = _half(outs[t].at[s], 1 - c)
                _remote(other, other, send_sems.at[k, t], recv_sems.at[k, t], (x, y, 1 - c)).wait_recv()
        for cp in give:
            cp.wait_send()

    res = pl.pallas_call(
        body, name=name, in_specs=[HBM_REF] * n, out_specs=[HBM_REF] * n,
        out_shape=[jax.ShapeDtypeStruct(g.shape, g.dtype) for g in gathered],
        input_output_aliases={t: t for t in range(n)},
        scratch_shapes=[pltpu.SemaphoreType.DMA((3, n))] * 2, **COMM_PARAMS)(*gathered)
    me = 2 * lax.axis_index("x") + lax.axis_index("y")
    return [lax.dynamic_update_slice(g, b[None], (me, 0, 0)) for g, b in zip(res, blocks)]


def swap_halves(grads):
    n = len(grads)

    def body(*refs):
        ins, got = refs[:n], refs[n:2 * n]
        send_sems, recv_sems = refs[2 * n:]
        x, y, c = _place()
        give = [_remote(_half(ins[t], 1 - c), got[t], send_sems.at[t], recv_sems.at[t], (x, y, 1 - c))
                for t in range(n)]
        for cp in give:
            cp.start()
        for cp in give:
            cp.wait_recv()
        for cp in give:
            cp.wait_send()

    return pl.pallas_call(
        body, name="swap_halves", in_specs=[HBM_REF] * n, out_specs=[HBM_REF] * n,
        out_shape=[jax.ShapeDtypeStruct((g.shape[0], g.shape[1] // 2, g.shape[2]), g.dtype) for g in grads],
        scratch_shapes=[pltpu.SemaphoreType.DMA((n,))] * 2, **COMM_PARAMS)(*grads)


def join_halves(blocks):
    n = len(blocks)

    def body(*refs):
        outs = refs[n:2 * n]
        send_sems, recv_sems = refs[2 * n:]
        x, y, c = _place()
        give = [_remote(_half(outs[t], c), _half(outs[t], c), send_sems.at[t], recv_sems.at[t], (x, y, 1 - c))
                for t in range(n)]
        for cp in give:
            cp.start()
        for t in range(n):
            arriving = _half(outs[t], 1 - c)
            _remote(arriving, arriving, send_sems.at[t], recv_sems.at[t], (x, y, 1 - c)).wait_recv()
        for cp in give:
            cp.wait_send()

    return pl.pallas_call(
        body, name="join_halves", in_specs=[HBM_REF] * n, out_specs=[HBM_REF] * n,
        out_shape=[jax.ShapeDtypeStruct(b.shape, b.dtype) for b in blocks],
        input_output_aliases={t: t for t in range(n)},
        scratch_shapes=[pltpu.SemaphoreType.DMA((n,))] * 2, **COMM_PARAMS)(*blocks)


SPLIT_HBM = pl.BlockSpec(memory_space=pltpu.HBM)
SPLIT_SEM = pl.BlockSpec(memory_space=pltpu.SEMAPHORE)
SPLIT_PARAMS = dict(compiler_params=pltpu.CompilerParams(has_side_effects=pltpu.SideEffectType.DATAFLOW_SIDE_EFFECTING))


def _scatter_copies(parts, landed, send_sem, recv_sem):
    x, y, c = _place()
    return [_remote(parts[t].at[2 * px + py], landed[t].at[k], send_sem(k, t), recv_sem(k, t), (px, py, c))
            for k, (px, py) in enumerate(_other_chips(x, y)) for t in range(len(parts))]


def scatter_start(partials, name):
    return split_start(_scatter_copies, partials, [(OTHER_CHIPS,) + p.shape[1:] for p in partials], name)


def chip_sums(grads, got):
    names = list(grads)
    partials = []
    for name, theirs in zip(names, got):
        n_slot, hr, width = theirs.shape
        tb = _row_block(hr, width, 6)
        per_half = hr // tb
        mine = lambda i, s, per_half=per_half: (i // per_half) * 2 * per_half + s[0] * per_half + i % per_half
        p = placed_map(
            jnp.add,
            [(grads[name].reshape(2 * n_slot * hr, width), mine), (theirs.reshape(n_slot * hr, width), lambda i, s: i)],
            (n_slot * hr, width, BF16, lambda i, s: i), n_blocks=n_slot * per_half, tb=tb, name="chip_sum_" + name)
        partials.append(p.reshape(theirs.shape))
    return partials


def owner_sums(grads, got, landed):
    names = list(grads)
    blocks = []
    for name, theirs, arrived in zip(names, got, landed):
        n_slot, hr, width = theirs.shape
        tb = _row_block(hr, width, 6)
        per_half = hr // tb
        views = [(grads[name].reshape(2 * n_slot * hr, width),
                  lambda i, s, per_half=per_half: s[1] * 2 * per_half + s[0] * per_half + i),
                 (theirs.reshape(n_slot * hr, width), lambda i, s, per_half=per_half: s[1] * per_half + i)]
        views += [(arrived.reshape(3 * hr, width), functools.partial(lambda k, per_half, i, s: k * per_half + i,
                                                                     k, per_half)) for k in range(3)]
        f = lambda a, b, l0, l1, l2: (((a + b) + l0.astype(F32)) + l1.astype(F32)) + l2.astype(F32)
        blocks.append(placed_map(
            f, views,(2 * hr, width, F32, lambda i, s, per_half=per_half: s[0] * per_half + i),
            n_blocks=per_half, tb=tb, name="owner_sum_" + name))
    return dict(zip(names, join_halves(blocks)))


def adamw_block(name, w, g, m, v):
    rows, width = w.shape
    return rowmap(_adamw, [w, g, m, v], [], [(width, F32)] * 3, tb=_row_block(rows, width, 7),
                  name="adamw_" + name)


def reduce_small(vec, w, m, v):
    n_dev = 8

    def body(vec_ref, w_ref, m_ref, v_ref, loss_ref, g_ref, d_ref, m2_ref, v2_ref, slots, send_sems, recv_sems):
        x, y, c = _place()
        me = 4 * x + 2 * y + c
        slots[me] = vec_ref[...]
        flips = [(fx, fy, fc) for fx in (0, 1) for fy in (0, 1) for fc in (0, 1)][1:]
        peers = [(1 - x if fx else x, 1 - y if fy else y, 1 - c if fc else c) for fx, fy, fc in flips]
        sends = [pltpu.make_async_remote_copy(
            src_ref=vec_ref, dst_ref=slots.at[me], send_sem=send_sems.at[j], recv_sem=recv_sems.at[j],
            device_id=peer, device_id_type=MESH) for j, peer in enumerate(peers)]
        for cp in sends:
            cp.start()
        for j, (px, py, pc) in enumerate(peers):
            pltpu.make_async_remote_copy(
                src_ref=vec_ref, dst_ref=slots.at[4 * px + 2 * py + pc], send_sem=send_sems.at[j],
                recv_sem=recv_sems.at[j], device_id=(px, py, pc), device_id_type=MESH).wait_recv()
        for cp in sends:
            cp.wait_send()
        g = slots[0]
        for d in range(1, n_dev):
            g = g + slots[d]
        loss_ref[...] = jnp.sum(g[:, :D_MODEL], axis=1, keepdims=True)
        delta, m2, v2 = _adamw(w_ref[...], g, m_ref[...], v_ref[...])
        g_ref[...], d_ref[...], m2_ref[...], v2_ref[...] = g, delta, m2, v2

    vm = pl.BlockSpec(memory_space=pltpu.VMEM)
    vec_t = jax.ShapeDtypeStruct(vec.shape, F32)
    return pl.pallas_call(
        body, name="reduce_small", in_specs=[vm] * 4, out_specs=[vm] * 5,
        out_shape=[jax.ShapeDtypeStruct((1, 1), F32)] + [vec_t] * 4,
        scratch_shapes=[pltpu.VMEM((n_dev,) + vec.shape, F32), pltpu.SemaphoreType.DMA((n_dev - 1,)),
                        pltpu.SemaphoreType.DMA((n_dev - 1,))],
        compiler_params=pltpu.CompilerParams(has_side_effects=True),
    )(vec, w, m, v)


def kernel(x, ffn1_norm, ffn1_w_in, ffn1_w_out, mix_norm, w_in, b_gate, rwkv_mu, rwkv_w0, rwkv_w2, rwkv_a0, rwkv_a2, rwkv_g2, rwkv_k_k, rwkv_k_a, rwkv_r_k, rwkv_ln_w, rwkv_ln_b, attn_q_norm, attn_k_norm, w_proj_rwkv, w_proj_attn, w_out, ffn2_norm, ffn2_w_in, ffn2_w_out, loss_target, m_ffn1_norm, m_ffn1_w_in, m_ffn1_w_out, m_mix_norm, m_w_in, m_b_gate, m_rwkv_mu, m_rwkv_w0, m_rwkv_w2, m_rwkv_a0, m_rwkv_a2, m_rwkv_g2, m_rwkv_k_k, m_rwkv_k_a, m_rwkv_r_k, m_rwkv_ln_w, m_rwkv_ln_b, m_attn_q_norm, m_attn_k_norm, m_w_proj_rwkv, m_w_proj_attn, m_w_out, m_ffn2_norm, m_ffn2_w_in, m_ffn2_w_out, v_ffn1_norm, v_ffn1_w_in, v_ffn1_w_out, v_mix_norm, v_w_in, v_b_gate, v_rwkv_mu, v_rwkv_w0, v_rwkv_w2, v_rwkv_a0, v_rwkv_a2, v_rwkv_g2, v_rwkv_k_k, v_rwkv_k_a, v_rwkv_r_k, v_rwkv_ln_w, v_rwkv_ln_b, v_attn_q_norm, v_attn_k_norm, v_w_proj_rwkv, v_w_proj_attn, v_w_out, v_ffn2_norm, v_ffn2_w_in, v_ffn2_w_out):
    given = dict(locals())
    weights = {n: given[n] for n in WEIGHT_ORDER}
    mom_m = {n: given["m_" + n] for n in WEIGHT_ORDER}
    mom_v = {n: given["v_" + n] for n in WEIGHT_ORDER}
    big = [name for name, _, _ in BIG]
    shapes = {n: weights[n].shape for n in WEIGHT_ORDER}
    blocks_of = lambda d: local_blocks({n: d[n][0] for n in big})
    w_blk, m_blk, v_blk = blocks_of(weights), blocks_of(mom_m), blocks_of(mom_v)
    names = list(w_blk)

    early = [n for n in names if n not in FIRST_FFN]
    bf16_block = lambda n: w_blk[n].astype(BF16)
    W = {n: blocks_to_full(n, g) for n, g in zip(FIRST_FFN, gather_weights([bf16_block(n) for n in FIRST_FFN]))}
    stages = {"mixer": [n for n in early if n in MIXER_IN], "out": [n for n in early if n not in MIXER_IN]}
    stage_blocks = {s: [bf16_block(n) for n in stages[s]] for s in stages}
    started = {s: gather_start(stage_blocks[s], "gather_start_" + s) for s in ("mixer", "out")}
    start_token = started["mixer"][1] + started["out"][1]

    def more_weights(stage, after):
        landed = split_wait(started[stage][0], after, "gather_wait_" + stage)
        got = pass_halves(landed, stage_blocks[stage], "pass_halves_" + stage)
        more = {n: blocks_to_full(n, g) for n, g in zip(stages[stage], got)}
        if "lora" in more:
            more.update(split_lora(more.pop("lora")))
        return more

    P = {n: weights[n].reshape(1, -1) for n, _ in SMALL}

    sent = {}

    def send_early(gw):
        lora = jnp.concatenate([gw[n] for n in LORA_PARTS], axis=0)
        sent["grads"] = {n: full_to_blocks(n, lora if n == "lora" else gw[n]) for n in early}
        sent["got"] = swap_halves(list(sent["grads"].values()))
        sent["handles"], token = scatter_start(chip_sums(sent["grads"], sent["got"]), "scatter_start")
        return token

    def send_late(dw_in, dw_out):
        sent["late"] = {n: full_to_blocks(n, g) for n, g in zip(FIRST_FFN, (dw_in, dw_out))}
        sent["late_got"] = swap_halves(list(sent["late"].values()))
        sent["late_handles"], token = scatter_start(chip_sums(sent["late"], sent["late_got"]), "scatter_start_ffn1")
        return token

    loss_cols, dx, gW, gP = layer_step(x[0], loss_target[0], W, P, start_token, more_weights, send_early, send_late)
    landed = split_wait(sent["handles"], gP["ffn1_norm"], "scatter_wait")
    out_g, out_d, out_m, out_v = {}, {}, {}, {}

    def apply(g_blk):
        for n in g_blk:
            res = (g_blk[n], *adamw_block(n, w_blk[n], g_blk[n], m_blk[n], v_blk[n]))
            for dst, t in zip((out_g, out_d, out_m, out_v), res):
                for part, val in (split_lora(t) if n == "lora" else {n: t}).items():
                    dst[part] = val.reshape(shapes[part])

    apply(owner_sums(sent["grads"], sent["got"], landed))
    late_landed = split_wait(sent["late_handles"], list(out_d.values())[-1], "scatter_wait_ffn1")
    apply(owner_sums(sent["late"], sent["late_got"], late_landed))

    zero_head = jnp.zeros((1, D_MODEL), F32)
    vec = pack_small(gP, loss_cols)
    loss, g_s, d_s, m_s, v_s = reduce_small(
        vec, pack_small({n: weights[n] for n, _ in SMALL}, zero_head),
        pack_small({n: mom_m[n] for n, _ in SMALL}, zero_head),
        pack_small({n: mom_v[n] for n, _ in SMALL}, zero_head))
    for dst, src in ((out_g, g_s), (out_d, d_s), (out_m, m_s), (out_v, v_s)):
        dst.update(unpack_small(src, shapes))

    return (loss[0, 0], dx[None], *[out_g[n] for n in WEIGHT_ORDER], *[out_d[n] for n in WEIGHT_ORDER],
            *[out_m[n] for n in WEIGHT_ORDER], *[out_v[n] for n in WEIGHT_ORDER])
```

```python
import functools

import jax
import jax.numpy as jnp
from jax import lax
from jax.experimental import pallas as pl
from jax.experimental.pallas import tpu as pltpu

F32 = jnp.float32
BF16 = jnp.bfloat16
MESH = pl.DeviceIdType.MESH

D_MODEL = 1024
HEAD_DIM = 64
RWKV_HEADS = 16
LORA_W, LORA_A, LORA_G = 64, 64, 160
LORA = LORA_W + LORA_A + LORA_G
RKV = 3 * D_MODEL
ATTN_PAIRS = ((128, 1), (512, 4), (2048, 16))
ATTN_BLK = 128
ATTN_HPG = 4
ATTN_RUN = 2
ATTN_WIDTH = 768
GROUP_W = ATTN_HPG * HEAD_DIM
D_FF = 2816
GN_EPS = 64e-5
RMS_EPS = 1e-6
NEG_INF = -1e30
WKV_CHUNK = 64
WKV_HEADS_PER_STEP = 16

ADAM_LR, ADAM_B1, ADAM_B2, ADAM_EPS, ADAM_WD, ADAM_STEP = 0.001, 0.9, 0.999, 1e-08, 0.01, 10

V7X_VMEM_BYTES = 64 << 20
VMEM_TEMP_ALLOWANCE = 20 << 20
VMEM_LEFT_FREE = 6 << 20


def _cparams(sem, block_bytes):
    limit = min(2 * block_bytes + VMEM_TEMP_ALLOWANCE, V7X_VMEM_BYTES - VMEM_LEFT_FREE)
    return pltpu.CompilerParams(dimension_semantics=sem, vmem_limit_bytes=int(limit))


def _nbytes(shape, dtype):
    n = 1
    for s in shape:
        n *= s
    return n * jnp.dtype(dtype).itemsize


def _split_bf16(a):
    hi = a.astype(BF16)
    return hi, (a - hi.astype(F32)).astype(BF16)


def _make_dots():
    def raw(a, b, ca, cb):
        return lax.dot_general(a.astype(BF16), b.astype(BF16), (((ca,), (cb,)), ((), ())),
                               preferred_element_type=F32)

    @jax.custom_vjp
    def nn(a, b):
        return raw(a, b, 1, 0)

    @jax.custom_vjp
    def nt(a, b):
        return raw(a, b, 1, 1)

    @jax.custom_vjp
    def tn(a, b):
        return raw(a, b, 0, 0)

    nn.defvjp(lambda a, b: (raw(a, b, 1, 0), (a, b)),
              lambda res, g: (raw(g, res[1], 1, 1), raw(res[0], g, 0, 0)))
    nt.defvjp(lambda a, b: (raw(a, b, 1, 1), (a, b)),
              lambda res, g: (raw(g, res[1], 1, 0), raw(g, res[0], 0, 0)))
    tn.defvjp(lambda a, b: (raw(a, b, 0, 0), (a, b)),
              lambda res, g: (raw(res[1], g, 1, 1), raw(res[0], g, 1, 0)))
    return nn, nt, tn


def _exact_rhs_dot(x, ones, cx, co):
    hi, lo = _split_bf16(x)
    dims = (((cx,), (co,)), ((), ()))
    return (lax.dot_general(hi, ones, dims, preferred_element_type=F32)
            + lax.dot_general(lo, ones, dims, preferred_element_type=F32))


@jax.custom_vjp
def SEG(x, ones):
    return _exact_rhs_dot(x, ones, 1, 0)


SEG.defvjp(lambda x, ones: (_exact_rhs_dot(x, ones, 1, 0), ones),
           lambda ones, g: (_exact_rhs_dot(g, ones, 1, 1), jnp.zeros_like(ones)))

NN, NT, TN = _make_dots()


MM_TILE_M, MM_TILE_N, MM_TILE_K = 1408, 1408, 1536


def _pick(n, cap):
    best = None
    for t in range(128, min(n, cap) + 1, 128):
        if n % t == 0:
            best = t
    return best or n


def matmul(a, b, mode, name, *, add=None, scale=1.0):
    if mode == "nn":
        (M, K), (K2, N) = a.shape, b.shape
    elif mode == "nt":
        (M, K), (N, K2) = a.shape, b.shape
    else:
        (K, M), (K2, N) = a.shape, b.shape
    assert K == K2, (name, a.shape, b.shape)
    tm, tn, tk = _pick(M, MM_TILE_M), _pick(N, MM_TILE_N), _pick(K, MM_TILE_K)
    nk = K // tk
    ca, cb = {"nn": (1, 0), "nt": (1, 1), "tn": (0, 0)}[mode]

    def body(*refs):
        if add is None:
            a_ref, b_ref, o_ref, acc_ref = refs
        else:
            a_ref, b_ref, add_ref, o_ref, acc_ref = refs
        k = pl.program_id(2)

        @pl.when(k == 0)
        def _():
            acc_ref[...] = jnp.zeros_like(acc_ref)

        acc_ref[...] += lax.dot_general(a_ref[...].astype(BF16), b_ref[...].astype(BF16),
                                        (((ca,), (cb,)), ((), ())), preferred_element_type=F32)

        @pl.when(k == nk - 1)
        def _():
            r = acc_ref[...] * scale
            if add is not None:
                r = add_ref[...] + r
            o_ref[...] = r.astype(o_ref.dtype)

    a_spec = (pl.BlockSpec((tk, tm), lambda i, j, k: (k, i)) if mode == "tn"
              else pl.BlockSpec((tm, tk), lambda i, j, k: (i, k)))
    b_spec = (pl.BlockSpec((tn, tk), lambda i, j, k: (j, k)) if mode == "nt"
              else pl.BlockSpec((tk, tn), lambda i, j, k: (k, j)))
    in_specs, args = [a_spec, b_spec], [a, b]
    blk = tm * tk * a.dtype.itemsize + tk * tn * b.dtype.itemsize + tm * tn * 8
    if add is not None:
        in_specs.append(pl.BlockSpec((tm, tn), lambda i, j, k: (i, j)))
        args.append(add)
        blk += tm * tn * 4
    return pl.pallas_call(
        body, name=name, grid=(M // tm, N // tn, nk),
        in_specs=in_specs, out_specs=pl.BlockSpec((tm, tn), lambda i, j, k: (i, j)),
        out_shape=jax.ShapeDtypeStruct((M, N), F32),
        scratch_shapes=[pltpu.VMEM((tm, tn), F32)],
        compiler_params=_cparams(("parallel", "parallel", "arbitrary"), blk),
    )(*args)


def matmul_cs(a, w, mode, name, token):
    n_blk = N_SHARDS
    if mode == "tn":
        (K, R), Cs = a.shape, w.shape[2] // 2
        tm, tk = _pick(R, MM_TILE_M), _pick(K, 1024)
        grid = (R // tm, n_blk, K // tk)
        a_spec = pl.BlockSpec((tk, tm), lambda i, j, k: (k, i))
        w_spec = pl.BlockSpec((None, tk, Cs), lambda i, j, k: (j // 2, k, j % 2))
        o_spec = pl.BlockSpec((None, tm, Cs), lambda i, j, k: (j, i, 0))
        out_shape, acc_shape, dims = (n_blk, R, Cs), (tm, Cs), (0, 0)
        blk = tk * tm * a.dtype.itemsize + tk * Cs * w.dtype.itemsize + tm * Cs * 8
    else:
        M, (_, R, Cs) = a.shape[1], w.shape
        tm, tn = _pick(M, MM_TILE_M), _pick(R, MM_TILE_N)
        grid = (M // tm, R // tn, n_blk)
        a_spec = pl.BlockSpec((None, tm, Cs), lambda i, j, k: (k // 2, i, k % 2))
        w_spec = pl.BlockSpec((None, tn, Cs), lambda i, j, k: (k, j, 0))
        o_spec = pl.BlockSpec((tm, tn), lambda i, j, k: (i, j))
        out_shape, acc_shape, dims = (M, R), (tm, tn), (1, 1)
        blk = tm * Cs * a.dtype.itemsize + tn * Cs * w.dtype.itemsize + tm * tn * 8
    nk = grid[2]

    def body(a_ref, w_ref, tok_ref, o_ref, acc_ref):
        k = pl.program_id(2)

        @pl.when(k == 0)
        def _():
            acc_ref[...] = jnp.zeros_like(acc_ref)

        acc_ref[...] += lax.dot_general(a_ref[...].astype(BF16), w_ref[...].astype(BF16),
                                        (((dims[0],), (dims[1],)), ((), ())), preferred_element_type=F32)

        @pl.when(k == nk - 1)
        def _():
            o_ref[...] = acc_ref[...] + tok_ref[0:1, 0:1]

    return pl.pallas_call(
        body, name=name, grid=grid, in_specs=[a_spec, w_spec, pl.BlockSpec(token.shape, lambda i, j, k: (0, 0))],
        out_specs=o_spec, out_shape=jax.ShapeDtypeStruct(out_shape, F32), scratch_shapes=[pltpu.VMEM(acc_shape, F32)],
        compiler_params=_cparams(("parallel", "parallel", "arbitrary"), blk),
    )(a, w, token)


FFN_TILE_M = 512


def _swiglu(gate, up):
    return gate * jax.nn.sigmoid(gate) * up


def ffn_in_act(h, w, name):
    (M, R), Cs, half = h.shape, w.shape[2], N_SHARDS // 2
    tm, tk = _pick(M, FFN_TILE_M), _pick(R, 1024)
    nk = R // tk

    def body(h_ref, wg_ref, wu_ref, gu_ref, act_ref, acc_ref):
        k = pl.program_id(2)

        @pl.when(k == 0)
        def _():
            acc_ref[...] = jnp.zeros_like(acc_ref)

        hb = h_ref[...].astype(BF16)
        for part, w_ref in enumerate((wg_ref, wu_ref)):
            acc_ref[part] += jnp.dot(hb, w_ref[...].astype(BF16), preferred_element_type=F32)

        @pl.when(k == nk - 1)
        def _():
            gu_ref[...] = acc_ref[...]
            act_ref[...] = _swiglu(acc_ref[0], acc_ref[1]).astype(act_ref.dtype)

    w_spec = lambda off: pl.BlockSpec((None, tk, Cs), functools.partial(lambda off, j, i, k: (j + off, k, 0), off))
    blk = tm * tk * h.dtype.itemsize + 2 * tk * Cs * w.dtype.itemsize + tm * Cs * (16 + 2)
    return pl.pallas_call(
        body, name=name, grid=(half, M // tm, nk),
        in_specs=[pl.BlockSpec((tm, tk), lambda j, i, k: (i, k)), w_spec(0), w_spec(half)],
        out_specs=[pl.BlockSpec((2, tm, Cs), lambda j, i, k: (0, i, j)), pl.BlockSpec((tm, Cs), lambda j, i, k: (i, j))],
        out_shape=[jax.ShapeDtypeStruct((2, M, half * Cs), F32), jax.ShapeDtypeStruct((M, half * Cs), BF16)],
        scratch_shapes=[pltpu.VMEM((2, tm, Cs), F32)],
        compiler_params=_cparams(("parallel", "parallel", "arbitrary"), blk),
    )(h, w, w)


def ffn_dact_dgu(dy, w_out, gu, scale, name):
    (M, D), F = dy.shape, w_out.shape[0]
    tm, tn = _pick(M, FFN_TILE_M), F // 2

    def body(dy_ref, w_ref, gu_ref, dgu_ref):
        dact = scale * lax.dot_general(dy_ref[...].astype(BF16), w_ref[...].astype(BF16),
                                       (((1,), (1,)), ((), ())), preferred_element_type=F32)
        dgate, dup = jax.vjp(_swiglu, gu_ref[0], gu_ref[1])[1](dact)
        dgu_ref[0] = dgate.astype(dgu_ref.dtype)
        dgu_ref[1] = dup.astype(dgu_ref.dtype)

    pair = pl.BlockSpec((2, tm, tn), lambda j, i: (0, i, j))
    blk = tm * D * dy.dtype.itemsize + tn * D * w_out.dtype.itemsize + 2 * tm * tn * (4 + 2)
    return pl.pallas_call(
        body, name=name, grid=(F // tn, M // tm),
        in_specs=[pl.BlockSpec((tm, D), lambda j, i: (i, 0)), pl.BlockSpec((tn, D), lambda j, i: (j, 0)), pair],
        out_specs=pair, out_shape=jax.ShapeDtypeStruct((2, M, F), BF16),
        compiler_params=_cparams(("parallel", "parallel"), blk),
    )(dy, w_out, gu)


def _row_block(n, width, n_arrays):
    cap = (V7X_VMEM_BYTES // 4) // (2 * 4 * width * n_arrays)
    best = None
    for t in range(16, min(n, cap) + 1, 16):
        if n % t == 0:
            best = t
    return best or n


def placed_map(f, ins, out, *, n_blocks, tb, name):
    def body(*refs):
        refs[-1][...] = f(*[r[...] for r in refs[:-1]]).astype(refs[-1].dtype)

    def spec(fn):
        def index(i):
            x, y, c = _place()
            return fn(i, (c, 2 * x + y)), 0
        return pl.BlockSpec((tb, width), index)

    o_rows, width, o_dtype, o_fn = out
    blk = (sum(a.dtype.itemsize for a, _ in ins) + jnp.dtype(o_dtype).itemsize) * tb * width
    return pl.pallas_call(
        body, name=name, grid=(n_blocks,), in_specs=[spec(fn) for _, fn in ins], out_specs=spec(o_fn),
        out_shape=jax.ShapeDtypeStruct((o_rows, width), o_dtype),
        compiler_params=_cparams(("parallel",), blk),
    )(*[a for a, _ in ins])


def rowmap(f, rows, params, outs, accs=(), *, tb, name):
    rows = [r if isinstance(r, tuple) else (r, r.shape[1], 0) for r in rows]
    S = rows[0][0].shape[0]
    assert S % tb == 0, (name, S, tb)
    n_in, n_out = len(rows) + len(params), len(outs)

    def body(*refs):
        res = f(*[r[...] for r in refs[:n_in]])
        res = res if isinstance(res, (tuple, list)) else (res,)
        o_refs, a_refs = refs[n_in:n_in + n_out], refs[n_in + n_out:]
        for ref, val in zip(o_refs, res[:n_out]):
            ref[...] = val.astype(ref.dtype)
        if a_refs:
            @pl.when(pl.program_id(0) == 0)
            def _():
                for ref in a_refs:
                    ref[...] = jnp.zeros_like(ref)

            for ref, val in zip(a_refs, res[n_out:]):
                ref[...] += val.astype(F32)

    in_specs = [pl.BlockSpec((tb, w), functools.partial(lambda cb, i: (i, cb), cb)) for _, w, cb in rows]
    in_specs += [pl.BlockSpec(p.shape, lambda i: (0, 0)) for p in params]
    out_specs = [pl.BlockSpec((tb, w), lambda i: (i, 0)) for w, _ in outs]
    out_specs += [pl.BlockSpec(tuple(s), lambda i: (0, 0)) for s in accs]
    out_shape = [jax.ShapeDtypeStruct((S, w), dt) for w, dt in outs]
    out_shape += [jax.ShapeDtypeStruct(tuple(s), F32) for s in accs]
    blk = sum(tb * w * a.dtype.itemsize for a, w, _ in rows) + sum(_nbytes(p.shape, p.dtype) for p in params)
    blk += sum(_nbytes((tb, w), dt) for w, dt in outs) + sum(_nbytes(s, F32) for s in accs)
    res = pl.pallas_call(
        body, name=name, grid=(S // tb,), in_specs=in_specs, out_specs=out_specs, out_shape=out_shape,
        compiler_params=_cparams(("arbitrary",) if accs else ("parallel",), blk),
    )(*[r[0] for r in rows], *[pltpu.with_memory_space_constraint(p, pltpu.HBM) for p in params])
    return res


def _rms(x, g):
    return x * lax.rsqrt(jnp.mean(x * x, axis=-1, keepdims=True) + RMS_EPS) * g


def _softplus(z):
    return jnp.maximum(z, 0.0) + jnp.log(1.0 + jnp.exp(-jnp.abs(z)))


def _rwkv_pre(xrk, xlo, w0, w2p, a0, a2p, g2p, k_k, k_a, seg, seg_t):
    k = xrk[:, D_MODEL:2 * D_MODEL]
    w = -_softplus(-(w0 + NN(jnp.tanh(xlo), w2p))) - 0.5
    log_decay = -jnp.exp(w)
    a = jax.nn.sigmoid(a0 + NN(xlo, a2p))
    g = NN(jax.nn.sigmoid(xlo), g2p)
    kk = k * k_k
    norm = jnp.maximum(jnp.sqrt(SEG(kk * kk, seg)), 1e-12)
    kk = kk * SEG(1.0 / norm, seg_t)
    k_mod = k * (1.0 + (a - 1.0) * k_a)
    return log_decay, k_mod, -kk, kk * a, g


def _rwkv_post(wkv, r, k_mod, v, g, r_k, ln_w, ln_b, seg, seg_t):
    inv_n = 1.0 / HEAD_DIM
    mean = SEG(wkv, seg) * inv_n
    cen = wkv - SEG(mean, seg_t)
    var = SEG(cen * cen, seg) * inv_n
    y = cen * SEG(lax.rsqrt(var + GN_EPS), seg_t) * ln_w + ln_b
    bonus = SEG(SEG(r * k_mod * r_k, seg), seg_t) * v
    return (y + bonus) * g


def _qk_norm(q, k, q_gain, k_gain, seg, seg_t, tile_t):
    def norm(x, gain):
        mean_sq = SEG(x * x, seg) * (1.0 / HEAD_DIM)
        return x * SEG(lax.rsqrt(mean_sq + RMS_EPS), seg_t) * SEG(gain, tile_t)

    return norm(q, q_gain) * (HEAD_DIM ** -0.5), norm(k, k_gain)


def _gate_merge(pgate, pa, pb, b_gate):
    sg = jax.nn.sigmoid(pgate + b_gate)
    return sg[:, :D_MODEL] * pa + sg[:, D_MODEL:] * pb


def _group_combine(o0, o1, o2, l0, l1, l2):
    m = jnp.maximum(jnp.maximum(l0, l1), l2)
    es = [jnp.exp(l - m) for l in (l0, l1, l2)]
    den = es[0] + es[1] + es[2]
    return jnp.concatenate([o * (e / den) for o, e in zip((o0, o1, o2), es)], axis=1)


def _each(f, *xs):
    return tuple(f(*args) for args in zip(*xs))


def _attn_block(q, kc, kp, vc, vp, first):
    qi = lax.broadcasted_iota(jnp.int32, (ATTN_BLK, ATTN_BLK), 0)
    kj = lax.broadcasted_iota(jnp.int32, (ATTN_BLK, ATTN_BLK), 1)
    own = kj <= qi
    s_c = _each(lambda a, b: jnp.where(own, NT(a, b), NEG_INF), q, kc)
    s_p = _each(lambda a, b, f: jnp.where((kj >= qi) & (f < 0.5), NT(a, b), NEG_INF), q, kp, first)
    row_max = lambda s: jnp.max(s, axis=-1, keepdims=True)
    row_sum = lambda s: jnp.sum(s, axis=-1, keepdims=True)
    m = _each(lambda c_, p_: jnp.maximum(row_max(c_), row_max(p_)), s_c, s_p)
    e_c, e_p = _each(lambda s, m_: jnp.exp(s - m_), s_c, m), _each(lambda s, m_: jnp.exp(s - m_), s_p, m)
    den = _each(lambda c_, p_: row_sum(c_) + row_sum(p_), e_c, e_p)
    inv = _each(lambda d_: 1.0 / d_, den)
    o = _each(lambda ec, ep, i_, vc_, vp_: (NN(ec, vc_) + NN(ep, vp_)) * i_, e_c, e_p, inv, vc, vp)
    lse = _each(lambda m_, d_: jnp.broadcast_to(m_ + jnp.log(d_), (ATTN_BLK, HEAD_DIM)), m, den)
    return o, lse


def _attn_run(q, k, k_before, v, v_before, first):
    n, flat = len(q[0]), lambda blocks: sum(blocks, ())
    o, lse = _attn_block(flat(q), flat(k), k_before + flat(k[:-1]), flat(v), v_before + flat(v[:-1]),
                         flat(tuple((f,) * n for f in first)))
    split = lambda t: tuple(t[i * n:(i + 1) * n] for i in range(len(q)))
    return split(o), split(lse)


TRI_SEED = 8


def _tri_inverse(n):
    c = n[0].shape[0]
    row = lax.broadcasted_iota(jnp.int32, (c, c), 0)
    col = lax.broadcasted_iota(jnp.int32, (c, c), 1)
    same_block = lambda size: (row >> (size.bit_length() - 1)) == (col >> (size.bit_length() - 1))
    seed = same_block(TRI_SEED)
    p = _each(lambda m: jnp.where(seed, m, 0.0), n)
    t, span = _each(lambda m: (row == col).astype(F32) + m, p), 2
    while span < TRI_SEED:
        p = _each(NN, p, p)
        t = _each(lambda t_, p_: t_ + NN(t_, p_), t, p)
        span *= 2
    size = TRI_SEED
    while size < c:
        joins = same_block(2 * size) & jnp.logical_not(same_block(size))
        t = _each(lambda t_, m: t_ + NN(NN(t_, jnp.where(joins, m, 0.0)), t_), t, n)
        size *= 2
    return t


@jax.custom_vjp
def _tri_solve(n, rhs, t):
    return _each(NN, t, rhs)


def _tri_solve_fwd(n, rhs, t):
    x = _each(NN, t, rhs)
    return x, (t, x)


def _tri_solve_bwd(res, dx):
    t, x = res
    drhs = _each(TN, t, dx)
    return _each(NT, drhs, x), drhs, _each(jnp.zeros_like, t)


_tri_solve.defvjp(_tri_solve_fwd, _tri_solve_bwd)


def _lower_ones(c):
    row = lax.broadcasted_iota(jnp.int32, (c, c), 0)
    col = lax.broadcasted_iota(jnp.int32, (c, c), 1)
    return (row >= col).astype(BF16)


def _ones_dot(ones, x, contract):
    hi, lo = _split_bf16(x)
    dims = (((contract,), (0,)), ((), ()))
    return (lax.dot_general(ones, hi, dims, preferred_element_type=F32)
            + lax.dot_general(ones, lo, dims, preferred_element_type=F32))


@jax.custom_vjp
def _cumsum_rows(x):
    return _ones_dot(_lower_ones(x.shape[0]), x, 1)


_cumsum_rows.defvjp(lambda x: (_ones_dot(_lower_ones(x.shape[0]), x, 1), None),
                    lambda _, g: (_ones_dot(_lower_ones(g.shape[0]), g, 0),))


def _wkv_chunk(s0, r, lw, k, v, a, b, t_inv=None):
    c = r[0].shape[0]
    row = lax.broadcasted_iota(jnp.int32, (c, c), 0)
    col = lax.broadcasted_iota(jnp.int32, (c, c), 1)
    strict, incl = row > col, row >= col
    cat = lambda p, q: jnp.concatenate([p, q], axis=0)
    cum = _each(_cumsum_rows, lw)
    e_neg = _each(lambda c_: jnp.exp(-c_), cum)
    ar = _each(lambda a_, r_, c_, l_: cat(a_ * jnp.exp(c_ - l_), r_ * jnp.exp(c_)), a, r, cum, lw)
    b_t, k_t = _each(jnp.multiply, b, e_neg), _each(jnp.multiply, k, e_neg)
    p_b, p_k, p_s = _each(NT, ar, b_t), _each(NT, ar, k_t), _each(NT, ar, s0)
    n_ab = _each(lambda p: jnp.where(strict, p[:c], 0.0), p_b)
    m_rb = _each(lambda p: jnp.where(incl, p[c:], 0.0), p_b)
    n_ak = _each(lambda p: jnp.where(strict, p[:c], 0.0), p_k)
    m_rk = _each(lambda p: jnp.where(incl, p[c:], 0.0), p_k)
    if t_inv is None:
        t_inv = _tri_inverse(n_ab)
    u = _tri_solve(n_ab, _each(lambda p, n_, v_: p[:c] + NN(n_, v_), p_s, n_ak, v), t_inv)
    y = _each(lambda p, mb, u_, mk, v_: p[c:] + NN(mb, u_) + NN(mk, v_), p_s, m_rb, u, m_rk, v)
    g_end = _each(lambda l_: jnp.exp(jnp.sum(l_, axis=0, keepdims=True)), lw)
    s1 = _each(lambda s_, g_, u_, v_, b_, k_: s_ * g_ + TN(cat(u_, v_), cat(b_, k_) * g_),
               s0, g_end, u, v, b_t, k_t)
    return y, s1, t_inv


def _adamw(w, g, m, v):
    m = ADAM_B1 * m + (1.0 - ADAM_B1) * g
    v = ADAM_B2 * v + (1.0 - ADAM_B2) * jnp.square(g)
    m_hat = m / (1.0 - ADAM_B1 ** ADAM_STEP)
    v_hat = v / (1.0 - ADAM_B2 ** ADAM_STEP)
    delta = -ADAM_LR * (m_hat / (jnp.sqrt(v_hat) + ADAM_EPS) + ADAM_WD * w)
    return delta, m, v


def token_shift_fwd(p, mu, *, tb, name):
    S, W = p.shape
    hb = tb // 8

    def body(p_ref, halo_ref, mu_ref, o_ref):
        i = pl.program_id(0)
        x = p_ref[...]
        before = halo_ref[7:8, :] * (i > 0).astype(F32)
        row = lax.broadcasted_iota(jnp.int32, (tb, W), 0)
        prev = jnp.where(row == 0, before, pltpu.roll(x, 1, 0))
        o_ref[...] = x + (prev - x) * mu_ref[...]

    blk = (2 * tb + 8) * W * 4
    return pl.pallas_call(
        body, name=name, grid=(S // tb,),
        in_specs=[pl.BlockSpec((tb, W), lambda i: (i, 0)),
                  pl.BlockSpec((8, W), lambda i: (jnp.maximum(i * hb - 1, 0), 0)),
                  pl.BlockSpec((1, W), lambda i: (0, 0))],
        out_specs=pl.BlockSpec((tb, W), lambda i: (i, 0)),
        out_shape=jax.ShapeDtypeStruct((S, W), F32),
        compiler_params=_cparams(("parallel",), blk),
    )(p, p, mu)


def token_shift_bwd(dxs, p, mu, *, tb, name):
    S, W = p.shape
    hb, nb = tb // 8, S // tb

    def body(d_ref, dnext_ref, p_ref, halo_ref, mu_ref, dp_ref, dmu_ref):
        i = pl.program_id(0)
        d, x, mu_v = d_ref[...], p_ref[...], mu_ref[...]
        row = lax.broadcasted_iota(jnp.int32, (tb, W), 0)
        before = halo_ref[7:8, :] * (i > 0).astype(F32)
        prev = jnp.where(row == 0, before, pltpu.roll(x, 1, 0))
        t = d * mu_v
        after = dnext_ref[0:1, :] * mu_v * (i < nb - 1).astype(F32)
        nxt = jnp.where(row == tb - 1, after, pltpu.roll(t, tb - 1, 0))
        dp_ref[...] = (d - t + nxt).astype(dp_ref.dtype)

        @pl.when(i == 0)
        def _():
            dmu_ref[...] = jnp.zeros_like(dmu_ref)

        dmu_ref[...] += jnp.sum(d * (prev - x), axis=0, keepdims=True)

    blk = (3 * tb + 16) * W * 4
    return pl.pallas_call(
        body, name=name, grid=(nb,),
        in_specs=[pl.BlockSpec((tb, W), lambda i: (i, 0)),
                  pl.BlockSpec((8, W), lambda i: (jnp.minimum((i + 1) * hb, S // 8 - 1), 0)),
                  pl.BlockSpec((tb, W), lambda i: (i, 0)),
                  pl.BlockSpec((8, W), lambda i: (jnp.maximum(i * hb - 1, 0), 0)),
                  pl.BlockSpec((1, W), lambda i: (0, 0))],
        out_specs=[pl.BlockSpec((tb, W), lambda i: (i, 0)), pl.BlockSpec((1, W), lambda i: (0, 0))],
        out_shape=[jax.ShapeDtypeStruct((S, W), BF16), jax.ShapeDtypeStruct((1, W), F32)],
        compiler_params=_cparams(("arbitrary",), blk),
    )(dxs, dxs, p, p, mu)


def _head_cols(h):
    return pl.ds(h * HEAD_DIM, HEAD_DIM)


def wkv_fwd(xs_rk, lw, k, a, b):
    S = lw.shape[0]
    C, nc, G, N = WKV_CHUNK, S // WKV_CHUNK, WKV_HEADS_PER_STEP, HEAD_DIM

    def body(r_ref, lw_ref, k_ref, v_ref, a_ref, b_ref, y_ref, st_ref, ti_ref, state):
        @pl.when(pl.program_id(1) == 0)
        def _():
            state[...] = jnp.zeros_like(state)

        heads = lambda ref: tuple(ref[:, _head_cols(h)] for h in range(G))
        s0 = tuple(state[h] for h in range(G))
        y, s1, t_inv = _wkv_chunk(s0, heads(r_ref), heads(lw_ref), heads(k_ref), heads(v_ref), heads(a_ref),
                                  heads(b_ref))
        for h in range(G):
            st_ref[h] = s0[h]
            ti_ref[h] = t_inv[h]
            y_ref[:, _head_cols(h)] = y[h]
            state[h] = s1[h]

    W = G * N
    seq = lambda j: pl.BlockSpec((C, W), functools.partial(lambda j, g, c: (c, j + g), j))
    per = D_MODEL // W
    per_chunk = pl.BlockSpec((None, G, N, N), lambda g, c: (c, g, 0, 0))
    return pl.pallas_call(
        body, name="wkv_fwd", grid=(RWKV_HEADS // G, nc),
        in_specs=[seq(0), seq(0), seq(0), seq(2 * per), seq(0), seq(0)],
        out_specs=[seq(0), per_chunk, per_chunk],
        out_shape=[jax.ShapeDtypeStruct((S, D_MODEL), F32)] + [jax.ShapeDtypeStruct((nc, RWKV_HEADS, N, N), F32)] * 2,
        scratch_shapes=[pltpu.VMEM((G, N, N), F32)],
        compiler_params=_cparams(("parallel", "arbitrary"), 8 * C * W * 4 + 3 * G * N * N * 4),
    )(xs_rk, lw, k, xs_rk, a, b)


def wkv_bwd(xs_rk, lw, k, a, b, states, t_invs, dy):
    S = lw.shape[0]
    C, nc, G, N = WKV_CHUNK, S // WKV_CHUNK, WKV_HEADS_PER_STEP, HEAD_DIM

    def body(r_ref, lw_ref, k_ref, v_ref, a_ref, b_ref, st_ref, ti_ref, dy_ref,
             dr_ref, dlw_ref, dk_ref, dv_ref, da_ref, db_ref, dstate):
        @pl.when(pl.program_id(1) == 0)
        def _():
            dstate[...] = jnp.zeros_like(dstate)

        heads = lambda ref: tuple(ref[:, _head_cols(h)] for h in range(G))
        t_inv = tuple(ti_ref[h] for h in range(G))
        chunk = lambda *args: _wkv_chunk(*args, t_inv)[:2]
        _, pull = jax.vjp(chunk, tuple(st_ref[h] for h in range(G)), heads(r_ref), heads(lw_ref),
                          heads(k_ref), heads(v_ref), heads(a_ref), heads(b_ref))
        ds0, *grads = pull((heads(dy_ref), tuple(dstate[h] for h in range(G))))
        for h in range(G):
            dstate[h] = ds0[h]
            for ref, grad in zip((dr_ref, dlw_ref, dk_ref, dv_ref, da_ref, db_ref), grads):
                ref[:, _head_cols(h)] = grad[h]

    W = G * N
    seq = lambda j: pl.BlockSpec((C, W), functools.partial(lambda j, g, c: (nc - 1 - c, j + g), j))
    per = D_MODEL // W
    st = pl.BlockSpec((None, G, N, N), lambda g, c: (nc - 1 - c, g, 0, 0))
    return pl.pallas_call(
        body, name="wkv_bwd", grid=(RWKV_HEADS // G, nc),
        in_specs=[seq(0), seq(0), seq(0), seq(2 * per), seq(0), seq(0), st, st, seq(0)],
        out_specs=[seq(0)] * 6, out_shape=[jax.ShapeDtypeStruct((S, D_MODEL), F32)] * 6,
        scratch_shapes=[pltpu.VMEM((G, N, N), F32)],
        compiler_params=_cparams(("parallel", "arbitrary"), 14 * C * W * 4 + 3 * G * N * N * 4),
    )(xs_rk, lw, k, xs_rk, a, b, states, t_invs, dy)


def _first_flag(i, per_seq):
    return (lax.rem(i, per_seq) == 0).astype(F32)


def _view(a):
    return a if isinstance(a, tuple) else (a, 0)


def _block_rows(half):
    return pl.ds(half * ATTN_BLK, ATTN_BLK)


def _block_heads(ref, half):
    return tuple(ref[_block_rows(half), _head_cols(h)] for h in range(ATTN_HPG))


def _run_heads(ref):
    return tuple(_block_heads(ref, b) for b in range(ATTN_RUN))


def attn_fwd(q, k, v, per_seq, name):
    (q, q_col), (k, k_col), (v, v_col) = _view(q), _view(k), _view(v)
    R, N = q.shape[0], GROUP_W
    n_runs = R // (ATTN_RUN * ATTN_BLK)

    def body(q_ref, k_ref, kb_ref, v_ref, vb_ref, o_ref, lse_ref):
        run = pl.program_id(0)
        first = tuple(_first_flag(ATTN_RUN * run + b, per_seq) for b in range(ATTN_RUN))
        o, lse = _attn_run(_run_heads(q_ref), _run_heads(k_ref), _block_heads(kb_ref, 0), _run_heads(v_ref),
                           _block_heads(vb_ref, 0), first)
        for b in range(ATTN_RUN):
            for h in range(ATTN_HPG):
                o_ref[_block_rows(b), _head_cols(h)] = o[b][h]
                lse_ref[_block_rows(b), _head_cols(h)] = lse[b][h]

    cur = lambda col: pl.BlockSpec((ATTN_RUN * ATTN_BLK, N), lambda i: (i, col))
    prv = lambda col: pl.BlockSpec((ATTN_BLK, N), lambda i: (jnp.maximum(ATTN_RUN * i - 1, 0), col))
    return pl.pallas_call(
        body, name=name, grid=(n_runs,), in_specs=[cur(q_col), cur(k_col), prv(k_col), cur(v_col), prv(v_col)],
        out_specs=[cur(0), cur(0)], out_shape=[jax.ShapeDtypeStruct((R, N), F32)] * 2,
        compiler_params=_cparams(("parallel",), (5 * ATTN_RUN + 2) * ATTN_BLK * N * 4),
    )(q, k, k, v, v)


def attn_bwd(q, k, v, do, dlse, per_seq, name):
    views = [_view(a) for a in (q, k, v, do, dlse)]
    (q, q_col), (k, k_col), (v, v_col), (do, do_col), (dlse, dl_col) = views
    R, N = q.shape[0], GROUP_W
    n_runs = R // (ATTN_RUN * ATTN_BLK)

    def body(q_ref, k_ref, kb_ref, v_ref, vb_ref, do_ref, dl_ref, dq_ref, dk_ref, dv_ref, carry_k, carry_v):
        step = pl.program_id(0)
        run = n_runs - 1 - step
        first = tuple(_first_flag(ATTN_RUN * run + b, per_seq) for b in range(ATTN_RUN))

        @pl.when(step == 0)
        def _():
            carry_k[...] = jnp.zeros_like(carry_k)
            carry_v[...] = jnp.zeros_like(carry_v)

        _, pull = jax.vjp(functools.partial(_attn_run, first=first), _run_heads(q_ref), _run_heads(k_ref),
                          _block_heads(kb_ref, 0), _run_heads(v_ref), _block_heads(vb_ref, 0))
        dq, dk, dk_before, dv, dv_before = pull((_run_heads(do_ref), _run_heads(dl_ref)))
        old_k, old_v = _block_heads(carry_k, 0), _block_heads(carry_v, 0)
        last = ATTN_RUN - 1
        for h in range(ATTN_HPG):
            cols = _head_cols(h)
            for b in range(ATTN_RUN):
                dq_ref[_block_rows(b), cols] = dq[b][h]
                dk_ref[_block_rows(b), cols] = dk[b][h] + old_k[h] if b == last else dk[b][h]
                dv_ref[_block_rows(b), cols] = dv[b][h] + old_v[h] if b == last else dv[b][h]
            carry_k[:, cols] = dk_before[h]
            carry_v[:, cols] = dv_before[h]

    cur = lambda col: pl.BlockSpec((ATTN_RUN * ATTN_BLK, N), lambda i: (n_runs - 1 - i, col))
    prv = lambda col: pl.BlockSpec((ATTN_BLK, N), lambda i: (jnp.maximum(ATTN_RUN * (n_runs - 1 - i) - 1, 0), col))
    return pl.pallas_call(
        body, name=name, grid=(n_runs,),
        in_specs=[cur(q_col), cur(k_col), prv(k_col), cur(v_col), prv(v_col), cur(do_col), cur(dl_col)],
        out_specs=[cur(0)] * 3, out_shape=[jax.ShapeDtypeStruct((R, N), F32)] * 3,
        scratch_shapes=[pltpu.VMEM((ATTN_BLK, N), F32)] * 2,
        compiler_params=_cparams(("arbitrary",), (9 * ATTN_RUN + 4) * ATTN_BLK * N * 4),
    )(q, k, k, v, v, do, dlse)


def by_residue(u, d):
    if d == 1:
        return u
    return u.reshape(u.shape[0] // d, d, GROUP_W).transpose(1, 0, 2).reshape(u.shape)


def by_position(u, d):
    if d == 1:
        return u
    return u.reshape(d, u.shape[0] // d, GROUP_W).transpose(1, 0, 2).reshape(u.shape)


def group_columns(t, col_block, d):
    if d == 1:
        return (t, col_block)
    return by_residue(t[:, GROUP_W * col_block:GROUP_W * (col_block + 1)], d)


def _ffn_fwd(x, norm, w_in, w_out, tag, token):
    h = rowmap(lambda x_b, g, tok: _rms(x_b, g) + tok[0:1, 0:1], [x], [norm, token], [(D_MODEL, BF16)], tb=512,
               name=tag + "_norm")[0]
    gu, act = ffn_in_act(h, w_in, tag + "_in")
    y = matmul(act, w_out, "nn", tag + "_out", add=x, scale=0.5)
    return y, (x, h, gu, act)


def _ffn_bwd(dy, saved, norm, w_in, w_out, tag, on_weight_grads):
    x, h, gu, act = saved
    no_token = jnp.zeros((8, 128), F32)
    dw_out = matmul(act, dy, "tn", tag + "_dwout", scale=0.5)
    dgu = ffn_dact_dgu(dy, w_out, gu, 0.5, tag + "_dgu")
    dw_in = matmul_cs(h, dgu, "tn", tag + "_dwin", no_token)
    dh = matmul_cs(dgu, w_in, "nt", tag + "_dh", on_weight_grads(dw_in, dw_out))

    def norm_bwd(x_b, dh_b, dy_b, g):
        dx, dg = jax.vjp(_rms, x_b, g)[1](dh_b)
        return dy_b + dx, dg

    dx, dnorm = rowmap(norm_bwd, [x, dh, dy], [norm], [(D_MODEL, F32)], [(1, D_MODEL)], tb=256,
                       name=tag + "_dnorm")
    return dx, dnorm, dw_in, dw_out


def layer_step(x, tgt, W, P, start_token, more_weights, on_mixer_grads, on_ffn1_grads):
    S = x.shape[0]
    x1, ffn1_saved = _ffn_fwd(x, P["ffn1_norm"], W["ffn1_w_in"], W["ffn1_w_out"], "ffn1", start_token)
    W = {**W, **more_weights("mixer", x1)}
    head_of = lambda n: jnp.arange(n)[:, None] // HEAD_DIM == jnp.arange(n // HEAD_DIM)[None, :]
    seg, seg_a = head_of(D_MODEL).astype(BF16), head_of(ATTN_WIDTH).astype(BF16)
    seg_t, seg_a_t = seg.T, seg_a.T
    tile_t = (jnp.arange(HEAD_DIM)[:, None] == jnp.arange(ATTN_WIDTH)[None, :] % HEAD_DIM).astype(BF16)
    qk_params = [P["attn_q_norm"], P["attn_k_norm"], seg_a, seg_a_t, tile_t]
    w_rkv, w_lora = W["w_in"][:, :RKV], W["w_in"][:, RKV:RKV + LORA]
    w_qkv = W["w_in"][:, RKV + LORA:RKV + LORA + 3 * ATTN_WIDTH]
    w_gate = W["w_in"][:, RKV + LORA + 3 * ATTN_WIDTH:]
    mu_rk, mu_lo = P["rwkv_mu"][:, :RKV], P["rwkv_mu"][:, RKV:]
    zeros = lambda n: jnp.zeros((n, D_MODEL), F32)
    w2p = jnp.concatenate([W["rwkv_w2"], zeros(LORA - LORA_W)], axis=0)
    a2p = jnp.concatenate([zeros(LORA_W), W["rwkv_a2"], zeros(LORA_G)], axis=0)
    g2p = jnp.concatenate([zeros(LORA_W + LORA_A), W["rwkv_g2"]], axis=0)
    pre_params = [P["rwkv_w0"], w2p, P["rwkv_a0"], a2p, g2p, P["rwkv_k_k"], P["rwkv_k_a"], seg, seg_t]
    post_params = [P["rwkv_r_k"], P["rwkv_ln_w"], P["rwkv_ln_b"], seg, seg_t]
    col = lambda arr, j: (arr, D_MODEL, j)

    h = rowmap(_rms, [x1], [P["mix_norm"]], [(D_MODEL, BF16)], tb=512, name="mix_norm")[0]
    p_rk = matmul(h, w_rkv, "nn", "proj_rkv")
    p_lo = matmul(h, w_lora, "nn", "proj_lora")
    p_qkv = matmul(h, w_qkv, "nn", "proj_qkv")
    p_gate = matmul(h, w_gate, "nn", "proj_gate")
    xs_rk = token_shift_fwd(p_rk, mu_rk, tb=256, name="shift_rk")
    xs_lo = token_shift_fwd(p_lo, mu_lo, tb=256, name="shift_lora")
    lw, k_mod, a_neg, b_kk, g = rowmap(
        _rwkv_pre, [xs_rk, xs_lo], pre_params, [(D_MODEL, F32)] * 5, tb=256, name="rwkv_pre")
    wkv, states, t_invs = wkv_fwd(xs_rk, lw, k_mod, a_neg, b_kk)
    post_rows = [wkv, col(xs_rk, 0), k_mod, col(xs_rk, 2), g]
    y_a = rowmap(_rwkv_post, post_rows, post_params, [(D_MODEL, BF16)], tb=256, name="rwkv_post")[0]

    qk_rows = [(p_qkv, ATTN_WIDTH, 0), (p_qkv, ATTN_WIDTH, 1)]
    qn, kn = rowmap(_qk_norm, qk_rows, qk_params, [(ATTN_WIDTH, F32)] * 2, tb=256, name="qk_norm")
    dil = [d for _, d in ATTN_PAIRS]
    groups = range(len(dil))
    per_seq = [S // d // ATTN_BLK for d in dil]
    v_first = 2 * ATTN_WIDTH // GROUP_W
    q_s = [group_columns(qn, g, dil[g]) for g in groups]
    k_s = [group_columns(kn, g, dil[g]) for g in groups]
    v_s = [group_columns(p_qkv, v_first + g, dil[g]) for g in groups]
    attn = [attn_fwd(q_s[g], k_s[g], v_s[g], per_seq[g], "attn_fwd_%d" % g) for g in groups]
    o_lse = [by_position(attn[g][j], dil[g]) for j in range(2) for g in groups]
    y_b = rowmap(_group_combine, o_lse, [], [(ATTN_WIDTH, BF16)], tb=512, name="attn_combine")[0]

    W = {**W, **more_weights("out", y_b)}
    pa = matmul(y_a, W["w_proj_rwkv"], "nn", "proj_a")
    pb = matmul(y_b, W["w_proj_attn"], "nn", "proj_b")
    merged = rowmap(_gate_merge, [p_gate, pa, pb], [P["b_gate"]], [(D_MODEL, BF16)], tb=256, name="merge")[0]
    x2 = matmul(merged, W["w_out"], "nn", "mix_out", add=x1)
    x3, ffn2_saved = _ffn_fwd(x2, P["ffn2_norm"], W["ffn2_w_in"], W["ffn2_w_out"], "ffn2",
                              jnp.zeros_like(start_token))

    def loss_head(y_b_, t_b):
        err = y_b_ - t_b
        return err * (1.0 / D_MODEL), (0.5 / D_MODEL) * jnp.sum(err * err, axis=0, keepdims=True)

    dx3, loss_cols = rowmap(loss_head, [x3, tgt], [], [(D_MODEL, F32)], [(1, D_MODEL)], tb=512, name="loss")

    gW, gP = {}, {}
    dx2, gP["ffn2_norm"], gW["ffn2_w_in"], gW["ffn2_w_out"] = _ffn_bwd(
        dx3, ffn2_saved, P["ffn2_norm"], W["ffn2_w_in"], W["ffn2_w_out"], "ffn2",
        lambda dw_in, dw_out: jnp.zeros_like(start_token))

    dmerged = matmul(dx2, W["w_out"], "nt", "d_merged")
    gW["w_out"] = matmul(merged, dx2, "tn", "dw_out")

    def merge_bwd(pg, pa_b, pb_b, dm, bg):
        return jax.vjp(_gate_merge, pg, pa_b, pb_b, bg)[1](dm)

    dp_gate, dpa, dpb, gP["b_gate"] = rowmap(
        merge_bwd, [p_gate, pa, pb, dmerged], [P["b_gate"]],
        [(2 * D_MODEL, BF16), (D_MODEL, BF16), (D_MODEL, BF16)], [(1, 2 * D_MODEL)], tb=256, name="merge_bwd")
    dy_a = matmul(dpa, W["w_proj_rwkv"], "nt", "d_ya")
    gW["w_proj_rwkv"] = matmul(y_a, dpa, "tn", "dw_proj_a")
    dy_b = matmul(dpb, W["w_proj_attn"], "nt", "d_yb")
    gW["w_proj_attn"] = matmul(y_b, dpb, "tn", "dw_proj_b")

    def combine_bwd(*blocks):
        return jax.vjp(_group_combine, *blocks[:-1])[1](blocks[-1])

    d_o_lse = rowmap(combine_bwd, o_lse + [dy_b], [], [(GROUP_W, F32)] * 6, tb=256, name="attn_combine_bwd")
    d_attn = [attn_bwd(q_s[g], k_s[g], v_s[g], by_residue(d_o_lse[g], dil[g]), by_residue(d_o_lse[3 + g], dil[g]),
                       per_seq[g], "attn_bwd_%d" % g) for g in groups]

    def qk_norm_bwd(q_b, k_b, *rest):
        dqkv, (qg, kg, sg, sgt, tl) = rest[:9], rest[9:]
        f = lambda *a: _qk_norm(*a, sg, sgt, tl)
        dqn, dkn = jnp.concatenate(dqkv[0:3], axis=1), jnp.concatenate(dqkv[3:6], axis=1)
        dq, dk, dqg, dkg = jax.vjp(f, q_b, k_b, qg, kg)[1]((dqn, dkn))
        return jnp.concatenate([dq, dk, *dqkv[6:9]], axis=1), dqg, dkg

    dp_qkv, gP["attn_q_norm"], gP["attn_k_norm"] = rowmap(
        qk_norm_bwd, qk_rows + [by_position(d_attn[g][j], dil[g]) for j in range(3) for g in groups], qk_params,
        [(3 * ATTN_WIDTH, BF16)], [(1, HEAD_DIM)] * 2, tb=256, name="qk_norm_bwd")

    def post_bwd(wkv_b, r_b, k_b, v_b, g_b, d_b, r_k, ln_w, ln_b, sg, sgt):
        f = lambda *a: _rwkv_post(*a, sg, sgt)
        return jax.vjp(f, wkv_b, r_b, k_b, v_b, g_b, r_k, ln_w, ln_b)[1](d_b)

    dwkv, dr_p, dk_p, dv_p, dg, gP["rwkv_r_k"], gP["rwkv_ln_w"], gP["rwkv_ln_b"] = rowmap(
        post_bwd, post_rows + [dy_a], post_params, [(D_MODEL, F32)] * 5, [(1, D_MODEL)] * 3, tb=128,
        name="rwkv_post_bwd")
    dr_w, dlw, dk_w, dv_w, da_neg, db_kk = wkv_bwd(xs_rk, lw, k_mod, a_neg, b_kk, states, t_invs, dwkv)

    def pre_bwd(xrk_b, xlo_b, dlw_b, dkw_b, dkp_b, da_b, db_b, dg_b, drp_b, drw_b, dvp_b, dvw_b,
                w0, w2, a0, a2, g2, k_k, k_a, sg, sgt):
        f = lambda *a: _rwkv_pre(*a, sg, sgt)
        pull = jax.vjp(f, xrk_b, xlo_b, w0, w2, a0, a2, g2, k_k, k_a)[1]
        dxrk, dxlo, *dpar = pull((dlw_b, dkw_b + dkp_b, da_b, db_b, dg_b))
        direct = jnp.concatenate([drp_b + drw_b, jnp.zeros_like(drp_b), dvp_b + dvw_b], axis=1)
        return (dxrk + direct, dxlo, *dpar)

    pre_rows = [xs_rk, xs_lo, dlw, dk_w, dk_p, da_neg, db_kk, dg, dr_p, dr_w, dv_p, dv_w]
    dxs_rk, dxs_lo, gP["rwkv_w0"], dw2p, gP["rwkv_a0"], da2p, dg2p, gP["rwkv_k_k"], gP["rwkv_k_a"] = rowmap(
        pre_bwd, pre_rows, pre_params, [(RKV, F32), (LORA, F32)],
        [(1, D_MODEL), (LORA, D_MODEL), (1, D_MODEL), (LORA, D_MODEL), (LORA, D_MODEL), (1, D_MODEL), (1, D_MODEL)],
        tb=128, name="rwkv_pre_bwd")
    gW["rwkv_w2"] = dw2p[:LORA_W]
    gW["rwkv_a2"] = da2p[LORA_W:LORA_W + LORA_A]
    gW["rwkv_g2"] = dg2p[LORA_W + LORA_A:]
    dp_rk, dmu_rk = token_shift_bwd(dxs_rk, p_rk, mu_rk, tb=256, name="shift_rk_bwd")
    dp_lo, dmu_lo = token_shift_bwd(dxs_lo, p_lo, mu_lo, tb=256, name="shift_lora_bwd")
    gP["rwkv_mu"] = jnp.concatenate([dmu_rk, dmu_lo], axis=1)

    dh = matmul(dp_rk, w_rkv, "nt", "dh_rkv")
    dh = matmul(dp_lo, w_lora, "nt", "dh_lora", add=dh)
    dh = matmul(dp_qkv, w_qkv, "nt", "dh_qkv", add=dh)
    dh = matmul(dp_gate, w_gate, "nt", "dh_gate", add=dh)
    gW["w_in"] = jnp.concatenate([
        matmul(h, dp_rk, "tn", "dw_rkv"), matmul(h, dp_lo, "tn", "dw_lora"),
        matmul(h, dp_qkv, "tn", "dw_qkv"), matmul(h, dp_gate, "tn", "dw_gate")], axis=1)

    token = on_mixer_grads(gW)

    def norm_bwd(x_b, dh_b, dy_b, gn, tok):
        dx, dgn = jax.vjp(_rms, x_b, gn)[1](dh_b)
        return dy_b + dx + tok[0:1, 0:1], dgn

    dx1, gP["mix_norm"] = rowmap(norm_bwd, [x1, dh, dx2], [P["mix_norm"], token], [(D_MODEL, F32)],
                                 [(1, D_MODEL)], tb=256, name="mix_norm_bwd")
    dx, gP["ffn1_norm"], gW["ffn1_w_in"], gW["ffn1_w_out"] = _ffn_bwd(
        dx1, ffn1_saved, P["ffn1_norm"], W["ffn1_w_in"], W["ffn1_w_out"], "ffn1", on_ffn1_grads)
    return loss_cols, dx, gW, gP


N_SHARDS = 4
OTHER_CHIPS = N_SHARDS - 1
BIG = (("ffn1_w_in", (D_MODEL, 2 * D_FF), 1), ("ffn1_w_out", (D_FF, D_MODEL), 0),
       ("w_in", (D_MODEL, 7712), 1), ("rwkv_w2", (LORA_W, D_MODEL), 1), ("rwkv_a2", (LORA_A, D_MODEL), 1),
       ("rwkv_g2", (LORA_G, D_MODEL), 1), ("w_proj_rwkv", (D_MODEL, D_MODEL), 0),
       ("w_proj_attn", (ATTN_WIDTH, D_MODEL), 1), ("w_out", (D_MODEL, D_MODEL), 0),
       ("ffn2_w_in", (D_MODEL, 2 * D_FF), 1), ("ffn2_w_out", (D_FF, D_MODEL), 0))
SMALL = (("ffn1_norm", 1024), ("mix_norm", 1024), ("b_gate", 2048), ("rwkv_mu", 3360), ("rwkv_w0", 1024),
         ("rwkv_a0", 1024), ("rwkv_k_k", 1024), ("rwkv_k_a", 1024), ("rwkv_r_k", 1024), ("rwkv_ln_w", 1024),
         ("rwkv_ln_b", 1024), ("attn_q_norm", 64), ("attn_k_norm", 64), ("ffn2_norm", 1024))
WEIGHT_ORDER = ("ffn1_norm", "ffn1_w_in", "ffn1_w_out", "mix_norm", "w_in", "b_gate", "rwkv_mu", "rwkv_w0",
                "rwkv_w2", "rwkv_a0", "rwkv_a2", "rwkv_g2", "rwkv_k_k", "rwkv_k_a", "rwkv_r_k", "rwkv_ln_w",
                "rwkv_ln_b", "attn_q_norm", "attn_k_norm", "w_proj_rwkv", "w_proj_attn", "w_out", "ffn2_norm",
                "ffn2_w_in", "ffn2_w_out")


LORA_PARTS = ("rwkv_w2", "rwkv_a2", "rwkv_g2")
BLOCK_MAJOR = ("ffn1_w_in", "ffn2_w_in")
FIRST_FFN = ("ffn1_w_in", "ffn1_w_out")
MIXER_IN = ("w_in", "lora")
SMALL_USED = D_MODEL + sum(n for _, n in SMALL)
SMALL_W = -(-SMALL_USED // 128) * 128


def _travel():
    out = {}
    for name, shape, axis in BIG:
        if name == LORA_PARTS[0]:
            out["lora"] = ((LORA, D_MODEL), 1)
        elif name not in LORA_PARTS:
            out[name] = (shape, axis)
    return out


def local_blocks(vals):
    out = {n: vals[n] for n in _travel() if n != "lora"}
    out["lora"] = jnp.concatenate([vals[n] for n in LORA_PARTS], axis=0)
    return out


def split_lora(t):
    return {"rwkv_w2": t[:LORA_W], "rwkv_a2": t[LORA_W:LORA_W + LORA_A], "rwkv_g2": t[LORA_W + LORA_A:]}


def blocks_to_full(name, blocks):
    shape, axis = _travel()[name]
    if name in BLOCK_MAJOR:
        return blocks
    if axis == 0:
        return blocks.reshape(shape)
    return blocks.transpose(1, 0, 2).reshape(shape)


def full_to_blocks(name, full):
    shape, axis = _travel()[name]
    if name in BLOCK_MAJOR:
        return full
    if axis == 0:
        return full.reshape(N_SHARDS, shape[0] // N_SHARDS, shape[1])
    return full.reshape(shape[0], N_SHARDS, shape[1] // N_SHARDS).transpose(1, 0, 2)


def pack_small(vals, head):
    parts = [head] + [vals[name].reshape(1, n) for name, n in SMALL]
    parts.append(jnp.zeros((1, SMALL_W - SMALL_USED), F32))
    return jnp.concatenate(parts, axis=1)


def unpack_small(vec, shapes):
    out, off = {}, D_MODEL
    for name, n in SMALL:
        out[name] = vec[:, off:off + n].reshape(shapes[name])
        off += n
    return out


def _place():
    return lax.axis_index("x"), lax.axis_index("y"), lax.axis_index("c")


def _other_chips(x, y):
    return [(1 - x, y), (x, 1 - y), (1 - x, 1 - y)]


def _remote(src, dst, send_sem, recv_sem, device):
    return pltpu.make_async_remote_copy(src_ref=src, dst_ref=dst, send_sem=send_sem, recv_sem=recv_sem,
                                        device_id=device, device_id_type=MESH)


def _half(ref, who):
    hr = ref.shape[-2] // 2
    rows = pl.ds(pl.multiple_of(who * hr, 8), hr)
    return ref.at[rows] if len(ref.shape) == 2 else ref.at[:, rows]


HBM_REF = pl.BlockSpec(memory_space=pl.ANY)
COMM_PARAMS = dict(compiler_params=pltpu.CompilerParams(has_side_effects=True))


def gather_weights(blocks):
    n = len(blocks)

    def body(*refs):
        ins, outs = refs[:n], refs[n:2 * n]
        ici_send, ici_recv, d2d_send, d2d_recv = refs[2 * n:]
        x, y, c = _place()
        me, sibling, chips = 2 * x + y, (x, y, 1 - c), _other_chips(x, y)
        first = [_remote(_half(ins[t], c), _half(outs[t].at[me], c), ici_send.at[k, t], ici_recv.at[k, t],
                         (px, py, c)) for k, (px, py) in enumerate(chips) for t in range(n)]
        for cp in first:
            cp.start()
        passed = []
        for k, (px, py) in enumerate(chips):
            for t in range(n):
                landed = _half(outs[t].at[2 * px + py], c)
                _remote(landed, landed, ici_send.at[k, t], ici_recv.at[k, t], (px, py, c)).wait_recv()
                cp = _remote(landed, landed, d2d_send.at[k, t], d2d_recv.at[k, t], sibling)
                cp.start()
                passed.append(cp)
        for k, (px, py) in enumerate(chips):
            for t in range(n):
                other = _half(outs[t].at[2 * px + py], 1 - c)
                _remote(other, other, d2d_send.at[k, t], d2d_recv.at[k, t], sibling).wait_recv()
        for cp in first + passed:
            cp.wait_send()

    res = pl.pallas_call(
        body, name="gather_weights", in_specs=[HBM_REF] * n, out_specs=[HBM_REF] * n,
        out_shape=[jax.ShapeDtypeStruct((N_SHARDS,) + b.shape, b.dtype) for b in blocks],
        scratch_shapes=[pltpu.SemaphoreType.DMA((3, n))] * 4, **COMM_PARAMS)(*blocks)
    me = 2 * lax.axis_index("x") + lax.axis_index("y")
    return [lax.dynamic_update_slice(g, b[None], (me, 0, 0)) for g, b in zip(res, blocks)]


def _gather_copies(ins, outs, send_sem, recv_sem):
    x, y, c = _place()
    return [_remote(_half(ins[t], c), _half(outs[t].at[2 * x + y], c), send_sem(k, t), recv_sem(k, t), (px, py, c))
            for k, (px, py) in enumerate(_other_chips(x, y)) for t in range(len(ins))]


def split_start(copies, sources, landing_shapes, name):
    n = len(sources)
    n_cp = OTHER_CHIPS * n

    def body(*refs):
        srcs, dsts = refs[:n], refs[n:2 * n]
        sems, token = refs[2 * n:2 * n + 2 * n_cp], refs[-1]
        for cp in copies(srcs, dsts, lambda k, t: sems[k * n + t], lambda k, t: sems[n_cp + k * n + t]):
            cp.start()
        token[...] = jnp.zeros_like(token)

    hbm = lambda a: pltpu.with_memory_space_constraint(a, pltpu.HBM)
    buffers = list(sources) + [lax.empty(shape, s.dtype) for shape, s in zip(landing_shapes, sources)]
    res = pl.pallas_call(
        body, name=name,
        out_shape=(*[pltpu.SemaphoreType.DMA(())] * (2 * n_cp),
                   *[pltpu.HBM(a.shape, a.dtype) for a in buffers], jax.ShapeDtypeStruct((8, 128), F32)),
        in_specs=[SPLIT_HBM] * (2 * n),
        out_specs=(*[SPLIT_SEM] * (2 * n_cp), *[SPLIT_HBM] * (2 * n), pl.BlockSpec(memory_space=pltpu.VMEM)),
        input_output_aliases={t: 2 * n_cp + t for t in range(2 * n)}, **SPLIT_PARAMS,
    )(*[hbm(a) for a in buffers])
    return (copies, n, res[:-1]), res[-1]


def split_wait(handles, after, name):
    copies, n, held = handles
    n_cp = OTHER_CHIPS * n
    sems, thru = held[:2 * n_cp], held[2 * n_cp:]

    def body(*refs):
        srcs, dsts = refs[:n], refs[n:2 * n]
        sem_refs = refs[2 * n:2 * n + 2 * n_cp]
        for cp in copies(srcs, dsts, lambda k, t: sem_refs[k * n + t], lambda k, t: sem_refs[n_cp + k * n + t]):
            cp.wait_send()
            cp.wait_recv()

    res = pl.pallas_call(
        body, name=name, out_shape=tuple(pltpu.HBM(a.shape, a.dtype) for a in thru),
        in_specs=[SPLIT_HBM] * (2 * n) + [SPLIT_SEM] * (2 * n_cp) + [pl.BlockSpec(memory_space=pl.ANY)],
        out_specs=tuple([SPLIT_HBM] * (2 * n)), input_output_aliases={t: t for t in range(2 * n)}, **SPLIT_PARAMS,
    )(*thru, *sems, after)
    return list(res[n:])


def gather_start(blocks, name):
    return split_start(_gather_copies, blocks, [(N_SHARDS,) + b.shape for b in blocks], name)


def pass_halves(gathered, blocks, name):
    n = len(gathered)

    def body(*refs):
        outs = refs[n:2 * n]
        send_sems, recv_sems = refs[2 * n:]
        x, y, c = _place()
        slots = [2 * px + py for px, py in _other_chips(x, y)]
        give = [_remote(_half(outs[t].at[s], c), _half(outs[t].at[s], c), send_sems.at[k, t], recv_sems.at[k, t],
                        (x, y, 1 - c)) for k, s in enumerate(slots) for t in range(n)]
        for cp in give:
            cp.start()
        for k, s in enumerate(slots):
            for t in range(n):
                other = _half(outs[t].at[s], 1 - c)
                _remote(other, other, send_sems.at[k, t], recv_sems.at[k, t], (x, y, 1 - c)).wait_recv()
        for cp in give:
            cp.wait_send()

    res = pl.pallas_call(
        body, name=name, in_specs=[HBM_REF] * n, out_specs=[HBM_REF] * n,
        out_shape=[jax.ShapeDtypeStruct(g.shape, g.dtype) for g in gathered],
        input_output_aliases={t: t for t in range(n)},
        scratch_shapes=[pltpu.SemaphoreType.DMA((3, n))] * 2, **COMM_PARAMS)(*gathered)
    me = 2 * lax.axis_index("x") + lax.axis_index("y")
    return [lax.dynamic_update_slice(g, b[None], (me, 0, 0)) for g, b in zip(res, blocks)]


def swap_halves(grads):
    n = len(grads)

    def body(*refs):
        ins, got = refs[:n], refs[n:2 * n]
        send_sems, recv_sems = refs[2 * n:]
        x, y, c = _place()
        give = [_remote(_half(ins[t], 1 - c), got[t], send_sems.at[t], recv_sems.at[t], (x, y, 1 - c))
                for t in range(n)]
        for cp in give:
            cp.start()
        for cp in give:
            cp.wait_recv()
        for cp in give:
            cp.wait_send()

    return pl.pallas_call(
        body, name="swap_halves", in_specs=[HBM_REF] * n, out_specs=[HBM_REF] * n,
        out_shape=[jax.ShapeDtypeStruct((g.shape[0], g.shape[1] // 2, g.shape[2]), g.dtype) for g in grads],
        scratch_shapes=[pltpu.SemaphoreType.DMA((n,))] * 2, **COMM_PARAMS)(*grads)


def join_halves(blocks):
    n = len(blocks)

    def body(*refs):
        outs = refs[n:2 * n]
        send_sems, recv_sems = refs[2 * n:]
        x, y, c = _place()
        give = [_remote(_half(outs[t], c), _half(outs[t], c), send_sems.at[t], recv_sems.at[t], (x, y, 1 - c))
                for t in range(n)]
        for cp in give:
            cp.start()
        for t in range(n):
            arriving = _half(outs[t], 1 - c)
            _remote(arriving, arriving, send_sems.at[t], recv_sems.at[t], (x, y, 1 - c)).wait_recv()
        for cp in give:
            cp.wait_send()

    return pl.pallas_call(
        body, name="join_halves", in_specs=[HBM_REF] * n, out_specs=[HBM_REF] * n,
        out_shape=[jax.ShapeDtypeStruct(b.shape, b.dtype) for b in blocks],
        input_output_aliases={t: t for t in range(n)},
        scratch_shapes=[pltpu.SemaphoreType.DMA((n,))] * 2, **COMM_PARAMS)(*blocks)


SPLIT_HBM = pl.BlockSpec(memory_space=pltpu.HBM)
SPLIT_SEM = pl.BlockSpec(memory_space=pltpu.SEMAPHORE)
SPLIT_PARAMS = dict(compiler_params=pltpu.CompilerParams(has_side_effects=pltpu.SideEffectType.DATAFLOW_SIDE_EFFECTING))


def _scatter_copies(parts, landed, send_sem, recv_sem):
    x, y, c = _place()
    return [_remote(parts[t].at[2 * px + py], landed[t].at[k], send_sem(k, t), recv_sem(k, t), (px, py, c))
            for k, (px, py) in enumerate(_other_chips(x, y)) for t in range(len(parts))]


def scatter_start(partials, name):
    return split_start(_scatter_copies, partials, [(OTHER_CHIPS,) + p.shape[1:] for p in partials], name)


def chip_sums(grads, got):
    names = list(grads)
    partials = []
    for name, theirs in zip(names, got):
        n_slot, hr, width = theirs.shape
        tb = _row_block(hr, width, 6)
        per_half = hr // tb
        mine = lambda i, s, per_half=per_half: (i // per_half) * 2 * per_half + s[0] * per_half + i % per_half
        p = placed_map(
            jnp.add,
            [(grads[name].reshape(2 * n_slot * hr, width), mine), (theirs.reshape(n_slot * hr, width), lambda i, s: i)],
            (n_slot * hr, width, BF16, lambda i, s: i), n_blocks=n_slot * per_half, tb=tb, name="chip_sum_" + name)
        partials.append(p.reshape(theirs.shape))
    return partials


def owner_sums(grads, got, landed):
    names = list(grads)
    blocks = []
    for name, theirs, arrived in zip(names, got, landed):
        n_slot, hr, width = theirs.shape
        tb = _row_block(hr, width, 6)
        per_half = hr // tb
        views = [(grads[name].reshape(2 * n_slot * hr, width),
                  lambda i, s, per_half=per_half: s[1] * 2 * per_half + s[0] * per_half + i),
                 (theirs.reshape(n_slot * hr, width), lambda i, s, per_half=per_half: s[1] * per_half + i)]
        views += [(arrived.reshape(3 * hr, width), functools.partial(lambda k, per_half, i, s: k * per_half + i,
                                                                     k, per_half)) for k in range(3)]
        f = lambda a, b, l0, l1, l2: (((a + b) + l0.astype(F32)) + l1.astype(F32)) + l2.astype(F32)
        blocks.append(placed_map(
            f, views,(2 * hr, width, F32, lambda i, s, per_half=per_half: s[0] * per_half + i),
            n_blocks=per_half, tb=tb, name="owner_sum_" + name))
    return dict(zip(names, join_halves(blocks)))


def adamw_block(name, w, g, m, v):
    rows, width = w.shape
    return rowmap(_adamw, [w, g, m, v], [], [(width, F32)] * 3, tb=_row_block(rows, width, 7),
                  name="adamw_" + name)


def reduce_small(vec, w, m, v):
    n_dev = 8

    def body(vec_ref, w_ref, m_ref, v_ref, loss_ref, g_ref, d_ref, m2_ref, v2_ref, slots, send_sems, recv_sems):
        x, y, c = _place()
        me = 4 * x + 2 * y + c
        slots[me] = vec_ref[...]
        flips = [(fx, fy, fc) for fx in (0, 1) for fy in (0, 1) for fc in (0, 1)][1:]
        peers = [(1 - x if fx else x, 1 - y if fy else y, 1 - c if fc else c) for fx, fy, fc in flips]
        sends = [pltpu.make_async_remote_copy(
            src_ref=vec_ref, dst_ref=slots.at[me], send_sem=send_sems.at[j], recv_sem=recv_sems.at[j],
            device_id=peer, device_id_type=MESH) for j, peer in enumerate(peers)]
        for cp in sends:
            cp.start()
        for j, (px, py, pc) in enumerate(peers):
            pltpu.make_async_remote_copy(
                src_ref=vec_ref, dst_ref=slots.at[4 * px + 2 * py + pc], send_sem=send_sems.at[j],
                recv_sem=recv_sems.at[j], device_id=(px, py, pc), device_id_type=MESH).wait_recv()
        for cp in sends:
            cp.wait_send()
        g = slots[0]
        for d in range(1, n_dev):
            g = g + slots[d]
        loss_ref[...] = jnp.sum(g[:, :D_MODEL], axis=1, keepdims=True)
        delta, m2, v2 = _adamw(w_ref[...], g, m_ref[...], v_ref[...])
        g_ref[...], d_ref[...], m2_ref[...], v2_ref[...] = g, delta, m2, v2

    vm = pl.BlockSpec(memory_space=pltpu.VMEM)
    vec_t = jax.ShapeDtypeStruct(vec.shape, F32)
    return pl.pallas_call(
        body, name="reduce_small", in_specs=[vm] * 4, out_specs=[vm] * 5,
        out_shape=[jax.ShapeDtypeStruct((1, 1), F32)] + [vec_t] * 4,
        scratch_shapes=[pltpu.VMEM((n_dev,) + vec.shape, F32), pltpu.SemaphoreType.DMA((n_dev - 1,)),
                        pltpu.SemaphoreType.DMA((n_dev - 1,))],
        compiler_params=pltpu.CompilerParams(has_side_effects=True),
    )(vec, w, m, v)


def kernel(x, ffn1_norm, ffn1_w_in, ffn1_w_out, mix_norm, w_in, b_gate, rwkv_mu, rwkv_w0, rwkv_w2, rwkv_a0, rwkv_a2, rwkv_g2, rwkv_k_k, rwkv_k_a, rwkv_r_k, rwkv_ln_w, rwkv_ln_b, attn_q_norm, attn_k_norm, w_proj_rwkv, w_proj_attn, w_out, ffn2_norm, ffn2_w_in, ffn2_w_out, loss_target, m_ffn1_norm, m_ffn1_w_in, m_ffn1_w_out, m_mix_norm, m_w_in, m_b_gate, m_rwkv_mu, m_rwkv_w0, m_rwkv_w2, m_rwkv_a0, m_rwkv_a2, m_rwkv_g2, m_rwkv_k_k, m_rwkv_k_a, m_rwkv_r_k, m_rwkv_ln_w, m_rwkv_ln_b, m_attn_q_norm, m_attn_k_norm, m_w_proj_rwkv, m_w_proj_attn, m_w_out, m_ffn2_norm, m_ffn2_w_in, m_ffn2_w_out, v_ffn1_norm, v_ffn1_w_in, v_ffn1_w_out, v_mix_norm, v_w_in, v_b_gate, v_rwkv_mu, v_rwkv_w0, v_rwkv_w2, v_rwkv_a0, v_rwkv_a2, v_rwkv_g2, v_rwkv_k_k, v_rwkv_k_a, v_rwkv_r_k, v_rwkv_ln_w, v_rwkv_ln_b, v_attn_q_norm, v_attn_k_norm, v_w_proj_rwkv, v_w_proj_attn, v_w_out, v_ffn2_norm, v_ffn2_w_in, v_ffn2_w_out):
    given = dict(locals())
    weights = {n: given[n] for n in WEIGHT_ORDER}
    mom_m = {n: given["m_" + n] for n in WEIGHT_ORDER}
    mom_v = {n: given["v_" + n] for n in WEIGHT_ORDER}
    big = [name for name, _, _ in BIG]
    shapes = {n: weights[n].shape for n in WEIGHT_ORDER}
    blocks_of = lambda d: local_blocks({n: d[n][0] for n in big})
    w_blk, m_blk, v_blk = blocks_of(weights), blocks_of(mom_m), blocks_of(mom_v)
    names = list(w_blk)

    early = [n for n in names if n not in FIRST_FFN]
    bf16_block = lambda n: w_blk[n].astype(BF16)
    W = {n: blocks_to_full(n, g) for n, g in zip(FIRST_FFN, gather_weights([bf16_block(n) for n in FIRST_FFN]))}
    stages = {"mixer": [n for n in early if n in MIXER_IN], "out": [n for n in early if n not in MIXER_IN]}
    stage_blocks = {s: [bf16_block(n) for n in stages[s]] for s in stages}
    started = {s: gather_start(stage_blocks[s], "gather_start_" + s) for s in ("mixer", "out")}
    start_token = started["mixer"][1] + started["out"][1]

    def more_weights(stage, after):
        landed = split_wait(started[stage][0], after, "gather_wait_" + stage)
        got = pass_halves(landed, stage_blocks[stage], "pass_halves_" + stage)
        more = {n: blocks_to_full(n, g) for n, g in zip(stages[stage], got)}
        if "lora" in more:
            more.update(split_lora(more.pop("lora")))
        return more

    P = {n: weights[n].reshape(1, -1) for n, _ in SMALL}

    sent = {}

    def send_early(gw):
        lora = jnp.concatenate([gw[n] for n in LORA_PARTS], axis=0)
        sent["grads"] = {n: full_to_blocks(n, lora if n == "lora" else gw[n]) for n in early}
        sent["got"] = swap_halves(list(sent["grads"].values()))
        sent["handles"], token = scatter_start(chip_sums(sent["grads"], sent["got"]), "scatter_start")
        return token

    def send_late(dw_in, dw_out):
        sent["late"] = {n: full_to_blocks(n, g) for n, g in zip(FIRST_FFN, (dw_in, dw_out))}
        sent["late_got"] = swap_halves(list(sent["late"].values()))
        sent["late_handles"], token = scatter_start(chip_sums(sent["late"], sent["late_got"]), "scatter_start_ffn1")
        return token

    loss_cols, dx, gW, gP = layer_step(x[0], loss_target[0], W, P, start_token, more_weights, send_early, send_late)
    landed = split_wait(sent["handles"], gP["ffn1_norm"], "scatter_wait")
    out_g, out_d, out_m, out_v = {}, {}, {}, {}

    def apply(g_blk):
        for n in g_blk:
            res = (g_blk[n], *adamw_block(n, w_blk[n], g_blk[n], m_blk[n], v_blk[n]))
            for dst, t in zip((out_g, out_d, out_m, out_v), res):
                for part, val in (split_lora(t) if n == "lora" else {n: t}).items():
                    dst[part] = val.reshape(shapes[part])

    apply(owner_sums(sent["grads"], sent["got"], landed))
    late_landed = split_wait(sent["late_handles"], list(out_d.values())[-1], "scatter_wait_ffn1")
    apply(owner_sums(sent["late"], sent["late_got"], late_landed))

    zero_head = jnp.zeros((1, D_MODEL), F32)
    vec = pack_small(gP, loss_cols)
    loss, g_s, d_s, m_s, v_s = reduce_small(
        vec, pack_small({n: weights[n] for n, _ in SMALL}, zero_head),
        pack_small({n: mom_m[n] for n, _ in SMALL}, zero_head),
        pack_small({n: mom_v[n] for n, _ in SMALL}, zero_head))
    for dst, src in ((out_g, g_s), (out_d, d_s), (out_m, m_s), (out_v, v_s)):
        dst.update(unpack_small(src, shapes))

    return (loss[0, 0], dx[None], *[out_g[n] for n in WEIGHT_ORDER], *[out_d[n] for n in WEIGHT_ORDER],
            *[out_m[n] for n in WEIGHT_ORDER], *[out_v[n] for n in WEIGHT_ORDER])
```

```python
import functools

import jax
import jax.numpy as jnp
from jax import lax
from jax.experimental import pallas as pl
from jax.experimental.pallas import tpu as pltpu

F32 = jnp.float32
BF16 = jnp.bfloat16
MESH = pl.DeviceIdType.MESH

D_MODEL = 1024
HEAD_DIM = 64
RWKV_HEADS = 16
LORA_W, LORA_A, LORA_G = 64, 64, 160
LORA = LORA_W + LORA_A + LORA_G
RKV = 3 * D_MODEL
ATTN_PAIRS = ((128, 1), (512, 4), (2048, 16))
ATTN_BLK = 128
ATTN_HPG = 4
ATTN_WIDTH = 768
GROUP_W = ATTN_HPG * HEAD_DIM
D_FF = 2816
GN_EPS = 64e-5
RMS_EPS = 1e-6
NEG_INF = -1e30
WKV_CHUNK = 64
WKV_HEADS_PER_STEP = 16
WKV_HEADS_PER_RUN = 8

ADAM_LR, ADAM_B1, ADAM_B2, ADAM_EPS, ADAM_WD, ADAM_STEP = 0.001, 0.9, 0.999, 1e-08, 0.01, 10

V7X_VMEM_BYTES = 64 << 20
VMEM_TEMP_ALLOWANCE = 20 << 20
VMEM_LEFT_FREE = 6 << 20


def _cparams(sem, block_bytes):
    limit = min(2 * block_bytes + VMEM_TEMP_ALLOWANCE, V7X_VMEM_BYTES - VMEM_LEFT_FREE)
    return pltpu.CompilerParams(dimension_semantics=sem, vmem_limit_bytes=int(limit))


def _nbytes(shape, dtype):
    n = 1
    for s in shape:
        n *= s
    return n * jnp.dtype(dtype).itemsize


def _split_bf16(a):
    hi = a.astype(BF16)
    return hi, (a - hi.astype(F32)).astype(BF16)


def _make_dots():
    def raw(a, b, ca, cb):
        return lax.dot_general(a.astype(BF16), b.astype(BF16), (((ca,), (cb,)), ((), ())),
                               preferred_element_type=F32)

    @jax.custom_vjp
    def nn(a, b):
        return raw(a, b, 1, 0)

    @jax.custom_vjp
    def nt(a, b):
        return raw(a, b, 1, 1)

    @jax.custom_vjp
    def tn(a, b):
        return raw(a, b, 0, 0)

    nn.defvjp(lambda a, b: (raw(a, b, 1, 0), (a, b)),
              lambda res, g: (raw(g, res[1], 1, 1), raw(res[0], g, 0, 0)))
    nt.defvjp(lambda a, b: (raw(a, b, 1, 1), (a, b)),
              lambda res, g: (raw(g, res[1], 1, 0), raw(g, res[0], 0, 0)))
    tn.defvjp(lambda a, b: (raw(a, b, 0, 0), (a, b)),
              lambda res, g: (raw(res[1], g, 1, 1), raw(res[0], g, 1, 0)))
    return nn, nt, tn


def _exact_rhs_dot(x, ones, cx, co):
    hi, lo = _split_bf16(x)
    dims = (((cx,), (co,)), ((), ()))
    return (lax.dot_general(hi, ones, dims, preferred_element_type=F32)
            + lax.dot_general(lo, ones, dims, preferred_element_type=F32))


@jax.custom_vjp
def SEG(x, ones):
    return _exact_rhs_dot(x, ones, 1, 0)


SEG.defvjp(lambda x, ones: (_exact_rhs_dot(x, ones, 1, 0), ones),
           lambda ones, g: (_exact_rhs_dot(g, ones, 1, 1), jnp.zeros_like(ones)))

NN, NT, TN = _make_dots()


MM_TILE_M, MM_TILE_N, MM_TILE_K = 1408, 1408, 1536


def _pick(n, cap):
    best = None
    for t in range(128, min(n, cap) + 1, 128):
        if n % t == 0:
            best = t
    return best or n


def matmul(a, b, mode, name, *, add=None, scale=1.0):
    if mode == "nn":
        (M, K), (K2, N) = a.shape, b.shape
    elif mode == "nt":
        (M, K), (N, K2) = a.shape, b.shape
    else:
        (K, M), (K2, N) = a.shape, b.shape
    assert K == K2, (name, a.shape, b.shape)
    tm, tn, tk = _pick(M, MM_TILE_M), _pick(N, MM_TILE_N), _pick(K, MM_TILE_K)
    nk = K // tk
    ca, cb = {"nn": (1, 0), "nt": (1, 1), "tn": (0, 0)}[mode]

    def body(*refs):
        if add is None:
            a_ref, b_ref, o_ref, acc_ref = refs
        else:
            a_ref, b_ref, add_ref, o_ref, acc_ref = refs
        k = pl.program_id(2)

        @pl.when(k == 0)
        def _():
            acc_ref[...] = jnp.zeros_like(acc_ref)

        acc_ref[...] += lax.dot_general(a_ref[...].astype(BF16), b_ref[...].astype(BF16),
                                        (((ca,), (cb,)), ((), ())), preferred_element_type=F32)

        @pl.when(k == nk - 1)
        def _():
            r = acc_ref[...] * scale
            if add is not None:
                r = add_ref[...] + r
            o_ref[...] = r.astype(o_ref.dtype)

    a_spec = (pl.BlockSpec((tk, tm), lambda i, j, k: (k, i)) if mode == "tn"
              else pl.BlockSpec((tm, tk), lambda i, j, k: (i, k)))
    b_spec = (pl.BlockSpec((tn, tk), lambda i, j, k: (j, k)) if mode == "nt"
              else pl.BlockSpec((tk, tn), lambda i, j, k: (k, j)))
    in_specs, args = [a_spec, b_spec], [a, b]
    blk = tm * tk * a.dtype.itemsize + tk * tn * b.dtype.itemsize + tm * tn * 8
    if add is not None:
        in_specs.append(pl.BlockSpec((tm, tn), lambda i, j, k: (i, j)))
        args.append(add)
        blk += tm * tn * 4
    return pl.pallas_call(
        body, name=name, grid=(M // tm, N // tn, nk),
        in_specs=in_specs, out_specs=pl.BlockSpec((tm, tn), lambda i, j, k: (i, j)),
        out_shape=jax.ShapeDtypeStruct((M, N), F32),
        scratch_shapes=[pltpu.VMEM((tm, tn), F32)],
        compiler_params=_cparams(("parallel", "parallel", "arbitrary"), blk),
    )(*args)


def matmul_cs(a, w, mode, name, token):
    n_blk = N_SHARDS
    if mode == "tn":
        (K, R), Cs = a.shape, w.shape[2] // 2
        tm, tk = _pick(R, MM_TILE_M), _pick(K, 1024)
        grid = (R // tm, n_blk, K // tk)
        a_spec = pl.BlockSpec((tk, tm), lambda i, j, k: (k, i))
        w_spec = pl.BlockSpec((None, tk, Cs), lambda i, j, k: (j // 2, k, j % 2))
        o_spec = pl.BlockSpec((None, tm, Cs), lambda i, j, k: (j, i, 0))
        out_shape, acc_shape, dims = (n_blk, R, Cs), (tm, Cs), (0, 0)
        blk = tk * tm * a.dtype.itemsize + tk * Cs * w.dtype.itemsize + tm * Cs * 8
    else:
        M, (_, R, Cs) = a.shape[1], w.shape
        tm, tn = _pick(M, MM_TILE_M), _pick(R, MM_TILE_N)
        grid = (M // tm, R // tn, n_blk)
        a_spec = pl.BlockSpec((None, tm, Cs), lambda i, j, k: (k // 2, i, k % 2))
        w_spec = pl.BlockSpec((None, tn, Cs), lambda i, j, k: (k, j, 0))
        o_spec = pl.BlockSpec((tm, tn), lambda i, j, k: (i, j))
        out_shape, acc_shape, dims = (M, R), (tm, tn), (1, 1)
        blk = tm * Cs * a.dtype.itemsize + tn * Cs * w.dtype.itemsize + tm * tn * 8
    nk = grid[2]

    def body(a_ref, w_ref, tok_ref, o_ref, acc_ref):
        k = pl.program_id(2)

        @pl.when(k == 0)
        def _():
            acc_ref[...] = jnp.zeros_like(acc_ref)

        acc_ref[...] += lax.dot_general(a_ref[...].astype(BF16), w_ref[...].astype(BF16),
                                        (((dims[0],), (dims[1],)), ((), ())), preferred_element_type=F32)

        @pl.when(k == nk - 1)
        def _():
            o_ref[...] = acc_ref[...] + tok_ref[0:1, 0:1]

    return pl.pallas_call(
        body, name=name, grid=grid, in_specs=[a_spec, w_spec, pl.BlockSpec(token.shape, lambda i, j, k: (0, 0))],
        out_specs=o_spec, out_shape=jax.ShapeDtypeStruct(out_shape, F32), scratch_shapes=[pltpu.VMEM(acc_shape, F32)],
        compiler_params=_cparams(("parallel", "parallel", "arbitrary"), blk),
    )(a, w, token)


FFN_TILE_M = 512


def _swiglu(gate, up):
    return gate * jax.nn.sigmoid(gate) * up


def ffn_in_act(h, w, name):
    (M, R), Cs, half = h.shape, w.shape[2], N_SHARDS // 2
    tm, tk = _pick(M, FFN_TILE_M), _pick(R, 1024)
    nk = R // tk

    def body(h_ref, wg_ref, wu_ref, gu_ref, act_ref, acc_ref):
        k = pl.program_id(2)

        @pl.when(k == 0)
        def _():
            acc_ref[...] = jnp.zeros_like(acc_ref)

        hb = h_ref[...].astype(BF16)
        for part, w_ref in enumerate((wg_ref, wu_ref)):
            acc_ref[part] += jnp.dot(hb, w_ref[...].astype(BF16), preferred_element_type=F32)

        @pl.when(k == nk - 1)
        def _():
            gu_ref[...] = acc_ref[...]
            act_ref[...] = _swiglu(acc_ref[0], acc_ref[1]).astype(act_ref.dtype)

    w_spec = lambda off: pl.BlockSpec((None, tk, Cs), functools.partial(lambda off, j, i, k: (j + off, k, 0), off))
    blk = tm * tk * h.dtype.itemsize + 2 * tk * Cs * w.dtype.itemsize + tm * Cs * (16 + 2)
    return pl.pallas_call(
        body, name=name, grid=(half, M // tm, nk),
        in_specs=[pl.BlockSpec((tm, tk), lambda j, i, k: (i, k)), w_spec(0), w_spec(half)],
        out_specs=[pl.BlockSpec((2, tm, Cs), lambda j, i, k: (0, i, j)), pl.BlockSpec((tm, Cs), lambda j, i, k: (i, j))],
        out_shape=[jax.ShapeDtypeStruct((2, M, half * Cs), F32), jax.ShapeDtypeStruct((M, half * Cs), BF16)],
        scratch_shapes=[pltpu.VMEM((2, tm, Cs), F32)],
        compiler_params=_cparams(("parallel", "parallel", "arbitrary"), blk),
    )(h, w, w)


def ffn_dact_dgu(dy, w_out, gu, scale, name):
    (M, D), F = dy.shape, w_out.shape[0]
    tm, tn = _pick(M, FFN_TILE_M), F // 2

    def body(dy_ref, w_ref, gu_ref, dgu_ref):
        dact = scale * lax.dot_general(dy_ref[...].astype(BF16), w_ref[...].astype(BF16),
                                       (((1,), (1,)), ((), ())), preferred_element_type=F32)
        dgate, dup = jax.vjp(_swiglu, gu_ref[0], gu_ref[1])[1](dact)
        dgu_ref[0] = dgate.astype(dgu_ref.dtype)
        dgu_ref[1] = dup.astype(dgu_ref.dtype)

    pair = pl.BlockSpec((2, tm, tn), lambda j, i: (0, i, j))
    blk = tm * D * dy.dtype.itemsize + tn * D * w_out.dtype.itemsize + 2 * tm * tn * (4 + 2)
    return pl.pallas_call(
        body, name=name, grid=(F // tn, M // tm),
        in_specs=[pl.BlockSpec((tm, D), lambda j, i: (i, 0)), pl.BlockSpec((tn, D), lambda j, i: (j, 0)), pair],
        out_specs=pair, out_shape=jax.ShapeDtypeStruct((2, M, F), BF16),
        compiler_params=_cparams(("parallel", "parallel"), blk),
    )(dy, w_out, gu)


def _row_block(n, width, n_arrays):
    cap = (V7X_VMEM_BYTES // 4) // (2 * 4 * width * n_arrays)
    best = None
    for t in range(16, min(n, cap) + 1, 16):
        if n % t == 0:
            best = t
    return best or n


def placed_map(f, ins, out, *, n_blocks, tb, name):
    def body(*refs):
        refs[-1][...] = f(*[r[...] for r in refs[:-1]]).astype(refs[-1].dtype)

    def spec(fn):
        def index(i):
            x, y, c = _place()
            return fn(i, (c, 2 * x + y)), 0
        return pl.BlockSpec((tb, width), index)

    o_rows, width, o_dtype, o_fn = out
    blk = (sum(a.dtype.itemsize for a, _ in ins) + jnp.dtype(o_dtype).itemsize) * tb * width
    return pl.pallas_call(
        body, name=name, grid=(n_blocks,), in_specs=[spec(fn) for _, fn in ins], out_specs=spec(o_fn),
        out_shape=jax.ShapeDtypeStruct((o_rows, width), o_dtype),
        compiler_params=_cparams(("parallel",), blk),
    )(*[a for a, _ in ins])


def rowmap(f, rows, params, outs, accs=(), *, tb, name):
    rows = [r if isinstance(r, tuple) else (r, r.shape[1], 0) for r in rows]
    S = rows[0][0].shape[0]
    assert S % tb == 0, (name, S, tb)
    n_in, n_out = len(rows) + len(params), len(outs)

    def body(*refs):
        res = f(*[r[...] for r in refs[:n_in]])
        res = res if isinstance(res, (tuple, list)) else (res,)
        o_refs, a_refs = refs[n_in:n_in + n_out], refs[n_in + n_out:]
        for ref, val in zip(o_refs, res[:n_out]):
            ref[...] = val.astype(ref.dtype)
        if a_refs:
            @pl.when(pl.program_id(0) == 0)
            def _():
                for ref in a_refs:
                    ref[...] = jnp.zeros_like(ref)

            for ref, val in zip(a_refs, res[n_out:]):
                ref[...] += val.astype(F32)

    in_specs = [pl.BlockSpec((tb, w), functools.partial(lambda cb, i: (i, cb), cb)) for _, w, cb in rows]
    in_specs += [pl.BlockSpec(p.shape, lambda i: (0, 0)) for p in params]
    out_specs = [pl.BlockSpec((tb, w), lambda i: (i, 0)) for w, _ in outs]
    out_specs += [pl.BlockSpec(tuple(s), lambda i: (0, 0)) for s in accs]
    out_shape = [jax.ShapeDtypeStruct((S, w), dt) for w, dt in outs]
    out_shape += [jax.ShapeDtypeStruct(tuple(s), F32) for s in accs]
    blk = sum(tb * w * a.dtype.itemsize for a, w, _ in rows) + sum(_nbytes(p.shape, p.dtype) for p in params)
    blk += sum(_nbytes((tb, w), dt) for w, dt in outs) + sum(_nbytes(s, F32) for s in accs)
    res = pl.pallas_call(
        body, name=name, grid=(S // tb,), in_specs=in_specs, out_specs=out_specs, out_shape=out_shape,
        compiler_params=_cparams(("arbitrary",) if accs else ("parallel",), blk),
    )(*[r[0] for r in rows], *[pltpu.with_memory_space_constraint(p, pltpu.HBM) for p in params])
    return res


def _rms(x, g):
    return x * lax.rsqrt(jnp.mean(x * x, axis=-1, keepdims=True) + RMS_EPS) * g


def _softplus(z):
    return jnp.maximum(z, 0.0) + jnp.log(1.0 + jnp.exp(-jnp.abs(z)))


def _rwkv_pre(xrk, xlo, w0, w2p, a0, a2p, g2p, k_k, k_a, seg, seg_t):
    k = xrk[:, D_MODEL:2 * D_MODEL]
    w = -_softplus(-(w0 + NN(jnp.tanh(xlo), w2p))) - 0.5
    log_decay = -jnp.exp(w)
    a = jax.nn.sigmoid(a0 + NN(xlo, a2p))
    g = NN(jax.nn.sigmoid(xlo), g2p)
    kk = k * k_k
    norm = jnp.maximum(jnp.sqrt(SEG(kk * kk, seg)), 1e-12)
    kk = kk * SEG(1.0 / norm, seg_t)
    k_mod = k * (1.0 + (a - 1.0) * k_a)
    return log_decay, k_mod, -kk, kk * a, g


def _rwkv_post(wkv, r, k_mod, v, g, r_k, ln_w, ln_b, seg, seg_t):
    inv_n = 1.0 / HEAD_DIM
    mean = SEG(wkv, seg) * inv_n
    cen = wkv - SEG(mean, seg_t)
    var = SEG(cen * cen, seg) * inv_n
    y = cen * SEG(lax.rsqrt(var + GN_EPS), seg_t) * ln_w + ln_b
    bonus = SEG(SEG(r * k_mod * r_k, seg), seg_t) * v
    return (y + bonus) * g


def _qk_norm(q, k, q_gain, k_gain, seg, seg_t, tile_t):
    def norm(x, gain):
        mean_sq = SEG(x * x, seg) * (1.0 / HEAD_DIM)
        return x * SEG(lax.rsqrt(mean_sq + RMS_EPS), seg_t) * SEG(gain, tile_t)

    return norm(q, q_gain) * (HEAD_DIM ** -0.5), norm(k, k_gain)


def _gate_merge(pgate, pa, pb, b_gate):
    sg = jax.nn.sigmoid(pgate + b_gate)
    return sg[:, :D_MODEL] * pa + sg[:, D_MODEL:] * pb


def _group_combine(o0, o1, o2, l0, l1, l2):
    m = jnp.maximum(jnp.maximum(l0, l1), l2)
    es = [jnp.exp(l - m) for l in (l0, l1, l2)]
    den = es[0] + es[1] + es[2]
    return jnp.concatenate([o * (e / den) for o, e in zip((o0, o1, o2), es)], axis=1)


def _each(f, *xs):
    return tuple(f(*args) for args in zip(*xs))


def _attn_block(q, kc, kp, vc, vp, first):
    qi = lax.broadcasted_iota(jnp.int32, (ATTN_BLK, ATTN_BLK), 0)
    kj = lax.broadcasted_iota(jnp.int32, (ATTN_BLK, ATTN_BLK), 1)
    own = kj <= qi
    s_c = _each(lambda a, b: jnp.where(own, NT(a, b), NEG_INF), q, kc)
    s_p = _each(lambda a, b, f: jnp.where((kj >= qi) & (f < 0.5), NT(a, b), NEG_INF), q, kp, first)
    row_max = lambda s: jnp.max(s, axis=-1, keepdims=True)
    row_sum = lambda s: jnp.sum(s, axis=-1, keepdims=True)
    m = _each(lambda c_, p_: jnp.maximum(row_max(c_), row_max(p_)), s_c, s_p)
    e_c, e_p = _each(lambda s, m_: jnp.exp(s - m_), s_c, m), _each(lambda s, m_: jnp.exp(s - m_), s_p, m)
    den = _each(lambda c_, p_: row_sum(c_) + row_sum(p_), e_c, e_p)
    inv = _each(lambda d_: 1.0 / d_, den)
    o = _each(lambda ec, ep, i_, vc_, vp_: (NN(ec, vc_) + NN(ep, vp_)) * i_, e_c, e_p, inv, vc, vp)
    lse = _each(lambda m_, d_: jnp.broadcast_to(m_ + jnp.log(d_), (ATTN_BLK, HEAD_DIM)), m, den)
    return o, lse


def _attn_pair(q, k, k_before, v, v_before, first):
    n = len(q[0])
    o, lse = _attn_block(q[0] + q[1], k[0] + k[1], k_before + k[0], v[0] + v[1], v_before + v[0],
                         (first[0],) * n + (first[1],) * n)
    return (o[:n], o[n:]), (lse[:n], lse[n:])


TRI_SEED = 8


def _tri_inverse(n):
    c = n[0].shape[0]
    row = lax.broadcasted_iota(jnp.int32, (c, c), 0)
    col = lax.broadcasted_iota(jnp.int32, (c, c), 1)
    same_block = lambda size: (row >> (size.bit_length() - 1)) == (col >> (size.bit_length() - 1))
    seed = same_block(TRI_SEED)
    p = _each(lambda m: jnp.where(seed, m, 0.0), n)
    t, span = _each(lambda m: (row == col).astype(F32) + m, p), 2
    while span < TRI_SEED:
        p = _each(NN, p, p)
        t = _each(lambda t_, p_: t_ + NN(t_, p_), t, p)
        span *= 2
    size = TRI_SEED
    while size < c:
        joins = same_block(2 * size) & jnp.logical_not(same_block(size))
        t = _each(lambda t_, m: t_ + NN(NN(t_, jnp.where(joins, m, 0.0)), t_), t, n)
        size *= 2
    return t


@jax.custom_vjp
def _tri_solve(n, rhs, t):
    return _each(NN, t, rhs)


def _tri_solve_fwd(n, rhs, t):
    x = _each(NN, t, rhs)
    return x, (t, x)


def _tri_solve_bwd(res, dx):
    t, x = res
    drhs = _each(TN, t, dx)
    return _each(NT, drhs, x), drhs, _each(jnp.zeros_like, t)


_tri_solve.defvjp(_tri_solve_fwd, _tri_solve_bwd)


def _lower_ones(c):
    row = lax.broadcasted_iota(jnp.int32, (c, c), 0)
    col = lax.broadcasted_iota(jnp.int32, (c, c), 1)
    return (row >= col).astype(BF16)


def _ones_dot(ones, x, contract):
    hi, lo = _split_bf16(x)
    dims = (((contract,), (0,)), ((), ()))
    return (lax.dot_general(ones, hi, dims, preferred_element_type=F32)
            + lax.dot_general(ones, lo, dims, preferred_element_type=F32))


@jax.custom_vjp
def _cumsum_rows(x):
    return _ones_dot(_lower_ones(x.shape[0]), x, 1)


_cumsum_rows.defvjp(lambda x: (_ones_dot(_lower_ones(x.shape[0]), x, 1), None),
                    lambda _, g: (_ones_dot(_lower_ones(g.shape[0]), g, 0),))


def _wkv_chunk(s0, r, lw, k, v, a, b, t_inv=None):
    c = r[0].shape[0]
    row = lax.broadcasted_iota(jnp.int32, (c, c), 0)
    col = lax.broadcasted_iota(jnp.int32, (c, c), 1)
    strict, incl = row > col, row >= col
    cat = lambda p, q: jnp.concatenate([p, q], axis=0)
    cum = _each(_cumsum_rows, lw)
    e_neg = _each(lambda c_: jnp.exp(-c_), cum)
    ar = _each(lambda a_, r_, c_, l_: cat(a_ * jnp.exp(c_ - l_), r_ * jnp.exp(c_)), a, r, cum, lw)
    b_t, k_t = _each(jnp.multiply, b, e_neg), _each(jnp.multiply, k, e_neg)
    p_b, p_k, p_s = _each(NT, ar, b_t), _each(NT, ar, k_t), _each(NT, ar, s0)
    n_ab = _each(lambda p: jnp.where(strict, p[:c], 0.0), p_b)
    m_rb = _each(lambda p: jnp.where(incl, p[c:], 0.0), p_b)
    n_ak = _each(lambda p: jnp.where(strict, p[:c], 0.0), p_k)
    m_rk = _each(lambda p: jnp.where(incl, p[c:], 0.0), p_k)
    if t_inv is None:
        t_inv = _tri_inverse(n_ab)
    u = _tri_solve(n_ab, _each(lambda p, n_, v_: p[:c] + NN(n_, v_), p_s, n_ak, v), t_inv)
    y = _each(lambda p, mb, u_, mk, v_: p[c:] + NN(mb, u_) + NN(mk, v_), p_s, m_rb, u, m_rk, v)
    g_end = _each(lambda l_: jnp.exp(jnp.sum(l_, axis=0, keepdims=True)), lw)
    s1 = _each(lambda s_, g_, u_, v_, b_, k_: s_ * g_ + TN(cat(u_, v_), cat(b_, k_) * g_),
               s0, g_end, u, v, b_t, k_t)
    return y, s1, t_inv


def _adamw(w, g, m, v):
    m = ADAM_B1 * m + (1.0 - ADAM_B1) * g
    v = ADAM_B2 * v + (1.0 - ADAM_B2) * jnp.square(g)
    m_hat = m / (1.0 - ADAM_B1 ** ADAM_STEP)
    v_hat = v / (1.0 - ADAM_B2 ** ADAM_STEP)
    delta = -ADAM_LR * (m_hat / (jnp.sqrt(v_hat) + ADAM_EPS) + ADAM_WD * w)
    return delta, m, v


def token_shift_fwd(p, mu, *, tb, name):
    S, W = p.shape
    hb = tb // 8

    def body(p_ref, halo_ref, mu_ref, o_ref):
        i = pl.program_id(0)
        x = p_ref[...]
        before = halo_ref[7:8, :] * (i > 0).astype(F32)
        row = lax.broadcasted_iota(jnp.int32, (tb, W), 0)
        prev = jnp.where(row == 0, before, pltpu.roll(x, 1, 0))
        o_ref[...] = x + (prev - x) * mu_ref[...]

    blk = (2 * tb + 8) * W * 4
    return pl.pallas_call(
        body, name=name, grid=(S // tb,),
        in_specs=[pl.BlockSpec((tb, W), lambda i: (i, 0)),
                  pl.BlockSpec((8, W), lambda i: (jnp.maximum(i * hb - 1, 0), 0)),
                  pl.BlockSpec((1, W), lambda i: (0, 0))],
        out_specs=pl.BlockSpec((tb, W), lambda i: (i, 0)),
        out_shape=jax.ShapeDtypeStruct((S, W), F32),
        compiler_params=_cparams(("parallel",), blk),
    )(p, p, mu)


def token_shift_bwd(dxs, p, mu, *, tb, name):
    S, W = p.shape
    hb, nb = tb // 8, S // tb

    def body(d_ref, dnext_ref, p_ref, halo_ref, mu_ref, dp_ref, dmu_ref):
        i = pl.program_id(0)
        d, x, mu_v = d_ref[...], p_ref[...], mu_ref[...]
        row = lax.broadcasted_iota(jnp.int32, (tb, W), 0)
        before = halo_ref[7:8, :] * (i > 0).astype(F32)
        prev = jnp.where(row == 0, before, pltpu.roll(x, 1, 0))
        t = d * mu_v
        after = dnext_ref[0:1, :] * mu_v * (i < nb - 1).astype(F32)
        nxt = jnp.where(row == tb - 1, after, pltpu.roll(t, tb - 1, 0))
        dp_ref[...] = (d - t + nxt).astype(dp_ref.dtype)

        @pl.when(i == 0)
        def _():
            dmu_ref[...] = jnp.zeros_like(dmu_ref)

        dmu_ref[...] += jnp.sum(d * (prev - x), axis=0, keepdims=True)

    blk = (3 * tb + 16) * W * 4
    return pl.pallas_call(
        body, name=name, grid=(nb,),
        in_specs=[pl.BlockSpec((tb, W), lambda i: (i, 0)),
                  pl.BlockSpec((8, W), lambda i: (jnp.minimum((i + 1) * hb, S // 8 - 1), 0)),
                  pl.BlockSpec((tb, W), lambda i: (i, 0)),
                  pl.BlockSpec((8, W), lambda i: (jnp.maximum(i * hb - 1, 0), 0)),
                  pl.BlockSpec((1, W), lambda i: (0, 0))],
        out_specs=[pl.BlockSpec((tb, W), lambda i: (i, 0)), pl.BlockSpec((1, W), lambda i: (0, 0))],
        out_shape=[jax.ShapeDtypeStruct((S, W), BF16), jax.ShapeDtypeStruct((1, W), F32)],
        compiler_params=_cparams(("arbitrary",), blk),
    )(dxs, dxs, p, p, mu)


def _head_cols(h):
    return pl.ds(h * HEAD_DIM, HEAD_DIM)


def wkv_fwd(xs_rk, lw, k, a, b):
    S = lw.shape[0]
    C, nc, G, N = WKV_CHUNK, S // WKV_CHUNK, WKV_HEADS_PER_STEP, HEAD_DIM

    def body(r_ref, lw_ref, k_ref, v_ref, a_ref, b_ref, y_ref, st_ref, ti_ref, state):
        @pl.when(pl.program_id(1) == 0)
        def _():
            state[...] = jnp.zeros_like(state)

        for base in range(0, G, WKV_HEADS_PER_RUN):
            run = range(base, base + WKV_HEADS_PER_RUN)
            heads = lambda ref: tuple(ref[:, _head_cols(h)] for h in run)
            s0 = tuple(state[h] for h in run)
            y, s1, t_inv = _wkv_chunk(s0, heads(r_ref), heads(lw_ref), heads(k_ref), heads(v_ref), heads(a_ref),
                                      heads(b_ref))
            for i, h in enumerate(run):
                st_ref[h] = s0[i]
                ti_ref[h] = t_inv[i]
                y_ref[:, _head_cols(h)] = y[i]
                state[h] = s1[i]

    W = G * N
    seq = lambda j: pl.BlockSpec((C, W), functools.partial(lambda j, g, c: (c, j + g), j))
    per = D_MODEL // W
    per_chunk = pl.BlockSpec((None, G, N, N), lambda g, c: (c, g, 0, 0))
    return pl.pallas_call(
        body, name="wkv_fwd", grid=(RWKV_HEADS // G, nc),
        in_specs=[seq(0), seq(0), seq(0), seq(2 * per), seq(0), seq(0)],
        out_specs=[seq(0), per_chunk, per_chunk],
        out_shape=[jax.ShapeDtypeStruct((S, D_MODEL), F32)] + [jax.ShapeDtypeStruct((nc, RWKV_HEADS, N, N), F32)] * 2,
        scratch_shapes=[pltpu.VMEM((G, N, N), F32)],
        compiler_params=_cparams(("parallel", "arbitrary"), 8 * C * W * 4 + 3 * G * N * N * 4),
    )(xs_rk, lw, k, xs_rk, a, b)


def wkv_bwd(xs_rk, lw, k, a, b, states, t_invs, dy):
    S = lw.shape[0]
    C, nc, G, N = WKV_CHUNK, S // WKV_CHUNK, WKV_HEADS_PER_STEP, HEAD_DIM

    def body(r_ref, lw_ref, k_ref, v_ref, a_ref, b_ref, st_ref, ti_ref, dy_ref,
             dr_ref, dlw_ref, dk_ref, dv_ref, da_ref, db_ref, dstate):
        @pl.when(pl.program_id(1) == 0)
        def _():
            dstate[...] = jnp.zeros_like(dstate)

        for base in range(0, G, WKV_HEADS_PER_RUN):
            run = range(base, base + WKV_HEADS_PER_RUN)
            heads = lambda ref: tuple(ref[:, _head_cols(h)] for h in run)
            t_inv = tuple(ti_ref[h] for h in run)
            chunk = lambda *args: _wkv_chunk(*args, t_inv)[:2]
            _, pull = jax.vjp(chunk, tuple(st_ref[h] for h in run), heads(r_ref), heads(lw_ref),
                              heads(k_ref), heads(v_ref), heads(a_ref), heads(b_ref))
            ds0, *grads = pull((heads(dy_ref), tuple(dstate[h] for h in run)))
            for i, h in enumerate(run):
                dstate[h] = ds0[i]
                for ref, grad in zip((dr_ref, dlw_ref, dk_ref, dv_ref, da_ref, db_ref), grads):
                    ref[:, _head_cols(h)] = grad[i]

    W = G * N
    seq = lambda j: pl.BlockSpec((C, W), functools.partial(lambda j, g, c: (nc - 1 - c, j + g), j))
    per = D_MODEL // W
    st = pl.BlockSpec((None, G, N, N), lambda g, c: (nc - 1 - c, g, 0, 0))
    return pl.pallas_call(
        body, name="wkv_bwd", grid=(RWKV_HEADS // G, nc),
        in_specs=[seq(0), seq(0), seq(0), seq(2 * per), seq(0), seq(0), st, st, seq(0)],
        out_specs=[seq(0)] * 6, out_shape=[jax.ShapeDtypeStruct((S, D_MODEL), F32)] * 6,
        scratch_shapes=[pltpu.VMEM((G, N, N), F32)],
        compiler_params=_cparams(("parallel", "arbitrary"), 14 * C * W * 4 + 3 * G * N * N * 4),
    )(xs_rk, lw, k, xs_rk, a, b, states, t_invs, dy)


def _first_flag(i, per_seq):
    return (lax.rem(i, per_seq) == 0).astype(F32)


def _view(a):
    return a if isinstance(a, tuple) else (a, 0)


def _block_rows(half):
    return pl.ds(half * ATTN_BLK, ATTN_BLK)


def _block_heads(ref, half):
    return tuple(ref[_block_rows(half), _head_cols(h)] for h in range(ATTN_HPG))


def _pair_heads(ref):
    return _block_heads(ref, 0), _block_heads(ref, 1)


def attn_fwd(q, k, v, per_seq, name):
    (q, q_col), (k, k_col), (v, v_col) = _view(q), _view(k), _view(v)
    R, N = q.shape[0], GROUP_W
    n_pairs = R // (2 * ATTN_BLK)

    def body(q_ref, k_ref, kb_ref, v_ref, vb_ref, o_ref, lse_ref):
        pair = pl.program_id(0)
        first = (_first_flag(2 * pair, per_seq), _first_flag(2 * pair + 1, per_seq))
        o, lse = _attn_pair(_pair_heads(q_ref), _pair_heads(k_ref), _block_heads(kb_ref, 0), _pair_heads(v_ref),
                            _block_heads(vb_ref, 0), first)
        for half in range(2):
            for h in range(ATTN_HPG):
                o_ref[_block_rows(half), _head_cols(h)] = o[half][h]
                lse_ref[_block_rows(half), _head_cols(h)] = lse[half][h]

    cur = lambda col: pl.BlockSpec((2 * ATTN_BLK, N), lambda i: (i, col))
    prv = lambda col: pl.BlockSpec((ATTN_BLK, N), lambda i: (jnp.maximum(2 * i - 1, 0), col))
    return pl.pallas_call(
        body, name=name, grid=(n_pairs,), in_specs=[cur(q_col), cur(k_col), prv(k_col), cur(v_col), prv(v_col)],
        out_specs=[cur(0), cur(0)], out_shape=[jax.ShapeDtypeStruct((R, N), F32)] * 2,
        compiler_params=_cparams(("parallel",), 12 * ATTN_BLK * N * 4),
    )(q, k, k, v, v)


def attn_bwd(q, k, v, do, dlse, per_seq, name):
    views = [_view(a) for a in (q, k, v, do, dlse)]
    (q, q_col), (k, k_col), (v, v_col), (do, do_col), (dlse, dl_col) = views
    R, N = q.shape[0], GROUP_W
    n_pairs = R // (2 * ATTN_BLK)

    def body(q_ref, k_ref, kb_ref, v_ref, vb_ref, do_ref, dl_ref, dq_ref, dk_ref, dv_ref, carry_k, carry_v):
        step = pl.program_id(0)
        pair = n_pairs - 1 - step
        first = (_first_flag(2 * pair, per_seq), _first_flag(2 * pair + 1, per_seq))

        @pl.when(step == 0)
        def _():
            carry_k[...] = jnp.zeros_like(carry_k)
            carry_v[...] = jnp.zeros_like(carry_v)

        _, pull = jax.vjp(functools.partial(_attn_pair, first=first), _pair_heads(q_ref), _pair_heads(k_ref),
                          _block_heads(kb_ref, 0), _pair_heads(v_ref), _block_heads(vb_ref, 0))
        dq, dk, dk_before, dv, dv_before = pull((_pair_heads(do_ref), _pair_heads(dl_ref)))
        old_k, old_v = _block_heads(carry_k, 0), _block_heads(carry_v, 0)
        for h in range(ATTN_HPG):
            cols = _head_cols(h)
            for half in range(2):
                dq_ref[_block_rows(half), cols] = dq[half][h]
            dk_ref[_block_rows(0), cols] = dk[0][h]
            dv_ref[_block_rows(0), cols] = dv[0][h]
            dk_ref[_block_rows(1), cols] = dk[1][h] + old_k[h]
            dv_ref[_block_rows(1), cols] = dv[1][h] + old_v[h]
            carry_k[:, cols] = dk_before[h]
            carry_v[:, cols] = dv_before[h]

    cur = lambda col: pl.BlockSpec((2 * ATTN_BLK, N), lambda i: (n_pairs - 1 - i, col))
    prv = lambda col: pl.BlockSpec((ATTN_BLK, N), lambda i: (jnp.maximum(2 * (n_pairs - 1 - i) - 1, 0), col))
    return pl.pallas_call(
        body, name=name, grid=(n_pairs,),
        in_specs=[cur(q_col), cur(k_col), prv(k_col), cur(v_col), prv(v_col), cur(do_col), cur(dl_col)],
        out_specs=[cur(0)] * 3, out_shape=[jax.ShapeDtypeStruct((R, N), F32)] * 3,
        scratch_shapes=[pltpu.VMEM((ATTN_BLK, N), F32)] * 2,
        compiler_params=_cparams(("arbitrary",), 22 * ATTN_BLK * N * 4),
    )(q, k, k, v, v, do, dlse)


def by_residue(u, d):
    if d == 1:
        return u
    return u.reshape(u.shape[0] // d, d, GROUP_W).transpose(1, 0, 2).reshape(u.shape)


def by_position(u, d):
    if d == 1:
        return u
    return u.reshape(d, u.shape[0] // d, GROUP_W).transpose(1, 0, 2).reshape(u.shape)


def group_columns(t, col_block, d):
    if d == 1:
        return (t, col_block)
    return by_residue(t[:, GROUP_W * col_block:GROUP_W * (col_block + 1)], d)


def _ffn_fwd(x, norm, w_in, w_out, tag, token):
    h = rowmap(lambda x_b, g, tok: _rms(x_b, g) + tok[0:1, 0:1], [x], [norm, token], [(D_MODEL, BF16)], tb=512,
               name=tag + "_norm")[0]
    gu, act = ffn_in_act(h, w_in, tag + "_in")
    y = matmul(act, w_out, "nn", tag + "_out", add=x, scale=0.5)
    return y, (x, h, gu, act)


def _ffn_bwd(dy, saved, norm, w_in, w_out, tag, on_weight_grads):
    x, h, gu, act = saved
    no_token = jnp.zeros((8, 128), F32)
    dw_out = matmul(act, dy, "tn", tag + "_dwout", scale=0.5)
    dgu = ffn_dact_dgu(dy, w_out, gu, 0.5, tag + "_dgu")
    dw_in = matmul_cs(h, dgu, "tn", tag + "_dwin", no_token)
    dh = matmul_cs(dgu, w_in, "nt", tag + "_dh", on_weight_grads(dw_in, dw_out))

    def norm_bwd(x_b, dh_b, dy_b, g):
        dx, dg = jax.vjp(_rms, x_b, g)[1](dh_b)
        return dy_b + dx, dg

    dx, dnorm = rowmap(norm_bwd, [x, dh, dy], [norm], [(D_MODEL, F32)], [(1, D_MODEL)], tb=256,
                       name=tag + "_dnorm")
    return dx, dnorm, dw_in, dw_out


def layer_step(x, tgt, W, P, start_token, more_weights, on_mixer_grads, on_ffn1_grads):
    S = x.shape[0]
    x1, ffn1_saved = _ffn_fwd(x, P["ffn1_norm"], W["ffn1_w_in"], W["ffn1_w_out"], "ffn1", start_token)
    W = {**W, **more_weights("mixer", x1)}
    head_of = lambda n: jnp.arange(n)[:, None] // HEAD_DIM == jnp.arange(n // HEAD_DIM)[None, :]
    seg, seg_a = head_of(D_MODEL).astype(BF16), head_of(ATTN_WIDTH).astype(BF16)
    seg_t, seg_a_t = seg.T, seg_a.T
    tile_t = (jnp.arange(HEAD_DIM)[:, None] == jnp.arange(ATTN_WIDTH)[None, :] % HEAD_DIM).astype(BF16)
    qk_params = [P["attn_q_norm"], P["attn_k_norm"], seg_a, seg_a_t, tile_t]
    w_rkv, w_lora = W["w_in"][:, :RKV], W["w_in"][:, RKV:RKV + LORA]
    w_qkv = W["w_in"][:, RKV + LORA:RKV + LORA + 3 * ATTN_WIDTH]
    w_gate = W["w_in"][:, RKV + LORA + 3 * ATTN_WIDTH:]
    mu_rk, mu_lo = P["rwkv_mu"][:, :RKV], P["rwkv_mu"][:, RKV:]
    zeros = lambda n: jnp.zeros((n, D_MODEL), F32)
    w2p = jnp.concatenate([W["rwkv_w2"], zeros(LORA - LORA_W)], axis=0)
    a2p = jnp.concatenate([zeros(LORA_W), W["rwkv_a2"], zeros(LORA_G)], axis=0)
    g2p = jnp.concatenate([zeros(LORA_W + LORA_A), W["rwkv_g2"]], axis=0)
    pre_params = [P["rwkv_w0"], w2p, P["rwkv_a0"], a2p, g2p, P["rwkv_k_k"], P["rwkv_k_a"], seg, seg_t]
    post_params = [P["rwkv_r_k"], P["rwkv_ln_w"], P["rwkv_ln_b"], seg, seg_t]
    col = lambda arr, j: (arr, D_MODEL, j)

    h = rowmap(_rms, [x1], [P["mix_norm"]], [(D_MODEL, BF16)], tb=512, name="mix_norm")[0]
    p_rk = matmul(h, w_rkv, "nn", "proj_rkv")
    p_lo = matmul(h, w_lora, "nn", "proj_lora")
    p_qkv = matmul(h, w_qkv, "nn", "proj_qkv")
    p_gate = matmul(h, w_gate, "nn", "proj_gate")
    xs_rk = token_shift_fwd(p_rk, mu_rk, tb=256, name="shift_rk")
    xs_lo = token_shift_fwd(p_lo, mu_lo, tb=256, name="shift_lora")
    lw, k_mod, a_neg, b_kk, g = rowmap(
        _rwkv_pre, [xs_rk, xs_lo], pre_params, [(D_MODEL, F32)] * 5, tb=256, name="rwkv_pre")
    wkv, states, t_invs = wkv_fwd(xs_rk, lw, k_mod, a_neg, b_kk)
    post_rows = [wkv, col(xs_rk, 0), k_mod, col(xs_rk, 2), g]
    y_a = rowmap(_rwkv_post, post_rows, post_params, [(D_MODEL, BF16)], tb=256, name="rwkv_post")[0]

    qk_rows = [(p_qkv, ATTN_WIDTH, 0), (p_qkv, ATTN_WIDTH, 1)]
    qn, kn = rowmap(_qk_norm, qk_rows, qk_params, [(ATTN_WIDTH, F32)] * 2, tb=256, name="qk_norm")
    dil = [d for _, d in ATTN_PAIRS]
    groups = range(len(dil))
    per_seq = [S // d // ATTN_BLK for d in dil]
    v_first = 2 * ATTN_WIDTH // GROUP_W
    q_s = [group_columns(qn, g, dil[g]) for g in groups]
    k_s = [group_columns(kn, g, dil[g]) for g in groups]
    v_s = [group_columns(p_qkv, v_first + g, dil[g]) for g in groups]
    attn = [attn_fwd(q_s[g], k_s[g], v_s[g], per_seq[g], "attn_fwd_%d" % g) for g in groups]
    o_lse = [by_position(attn[g][j], dil[g]) for j in range(2) for g in groups]
    y_b = rowmap(_group_combine, o_lse, [], [(ATTN_WIDTH, BF16)], tb=512, name="attn_combine")[0]

    W = {**W, **more_weights("out", y_b)}
    pa = matmul(y_a, W["w_proj_rwkv"], "nn", "proj_a")
    pb = matmul(y_b, W["w_proj_attn"], "nn", "proj_b")
    merged = rowmap(_gate_merge, [p_gate, pa, pb], [P["b_gate"]], [(D_MODEL, BF16)], tb=256, name="merge")[0]
    x2 = matmul(merged, W["w_out"], "nn", "mix_out", add=x1)
    x3, ffn2_saved = _ffn_fwd(x2, P["ffn2_norm"], W["ffn2_w_in"], W["ffn2_w_out"], "ffn2",
                              jnp.zeros_like(start_token))

    def loss_head(y_b_, t_b):
        err = y_b_ - t_b
        return err * (1.0 / D_MODEL), (0.5 / D_MODEL) * jnp.sum(err * err, axis=0, keepdims=True)

    dx3, loss_cols = rowmap(loss_head, [x3, tgt], [], [(D_MODEL, F32)], [(1, D_MODEL)], tb=512, name="loss")

    gW, gP = {}, {}
    dx2, gP["ffn2_norm"], gW["ffn2_w_in"], gW["ffn2_w_out"] = _ffn_bwd(
        dx3, ffn2_saved, P["ffn2_norm"], W["ffn2_w_in"], W["ffn2_w_out"], "ffn2",
        lambda dw_in, dw_out: jnp.zeros_like(start_token))

    dmerged = matmul(dx2, W["w_out"], "nt", "d_merged")
    gW["w_out"] = matmul(merged, dx2, "tn", "dw_out")

    def merge_bwd(pg, pa_b, pb_b, dm, bg):
        return jax.vjp(_gate_merge, pg, pa_b, pb_b, bg)[1](dm)

    dp_gate, dpa, dpb, gP["b_gate"] = rowmap(
        merge_bwd, [p_gate, pa, pb, dmerged], [P["b_gate"]],
        [(2 * D_MODEL, BF16), (D_MODEL, BF16), (D_MODEL, BF16)], [(1, 2 * D_MODEL)], tb=256, name="merge_bwd")
    dy_a = matmul(dpa, W["w_proj_rwkv"], "nt", "d_ya")
    gW["w_proj_rwkv"] = matmul(y_a, dpa, "tn", "dw_proj_a")
    dy_b = matmul(dpb, W["w_proj_attn"], "nt", "d_yb")
    gW["w_proj_attn"] = matmul(y_b, dpb, "tn", "dw_proj_b")

    def combine_bwd(*blocks):
        return jax.vjp(_group_combine, *blocks[:-1])[1](blocks[-1])

    d_o_lse = rowmap(combine_bwd, o_lse + [dy_b], [], [(GROUP_W, F32)] * 6, tb=256, name="attn_combine_bwd")
    d_attn = [attn_bwd(q_s[g], k_s[g], v_s[g], by_residue(d_o_lse[g], dil[g]), by_residue(d_o_lse[3 + g], dil[g]),
                       per_seq[g], "attn_bwd_%d" % g) for g in groups]

    def qk_norm_bwd(q_b, k_b, *rest):
        dqkv, (qg, kg, sg, sgt, tl) = rest[:9], rest[9:]
        f = lambda *a: _qk_norm(*a, sg, sgt, tl)
        dqn, dkn = jnp.concatenate(dqkv[0:3], axis=1), jnp.concatenate(dqkv[3:6], axis=1)
        dq, dk, dqg, dkg = jax.vjp(f, q_b, k_b, qg, kg)[1]((dqn, dkn))
        return jnp.concatenate([dq, dk, *dqkv[6:9]], axis=1), dqg, dkg

    dp_qkv, gP["attn_q_norm"], gP["attn_k_norm"] = rowmap(
        qk_norm_bwd, qk_rows + [by_position(d_attn[g][j], dil[g]) for j in range(3) for g in groups], qk_params,
        [(3 * ATTN_WIDTH, BF16)], [(1, HEAD_DIM)] * 2, tb=256, name="qk_norm_bwd")

    def post_bwd(wkv_b, r_b, k_b, v_b, g_b, d_b, r_k, ln_w, ln_b, sg, sgt):
        f = lambda *a: _rwkv_post(*a, sg, sgt)
        return jax.vjp(f, wkv_b, r_b, k_b, v_b, g_b, r_k, ln_w, ln_b)[1](d_b)

    dwkv, dr_p, dk_p, dv_p, dg, gP["rwkv_r_k"], gP["rwkv_ln_w"], gP["rwkv_ln_b"] = rowmap(
        post_bwd, post_rows + [dy_a], post_params, [(D_MODEL, F32)] * 5, [(1, D_MODEL)] * 3, tb=128,
        name="rwkv_post_bwd")
    dr_w, dlw, dk_w, dv_w, da_neg, db_kk = wkv_bwd(xs_rk, lw, k_mod, a_neg, b_kk, states, t_invs, dwkv)

    def pre_bwd(xrk_b, xlo_b, dlw_b, dkw_b, dkp_b, da_b, db_b, dg_b, drp_b, drw_b, dvp_b, dvw_b,
                w0, w2, a0, a2, g2, k_k, k_a, sg, sgt):
        f = lambda *a: _rwkv_pre(*a, sg, sgt)
        pull = jax.vjp(f, xrk_b, xlo_b, w0, w2, a0, a2, g2, k_k, k_a)[1]
        dxrk, dxlo, *dpar = pull((dlw_b, dkw_b + dkp_b, da_b, db_b, dg_b))
        direct = jnp.concatenate([drp_b + drw_b, jnp.zeros_like(drp_b), dvp_b + dvw_b], axis=1)
        return (dxrk + direct, dxlo, *dpar)

    pre_rows = [xs_rk, xs_lo, dlw, dk_w, dk_p, da_neg, db_kk, dg, dr_p, dr_w, dv_p, dv_w]
    dxs_rk, dxs_lo, gP["rwkv_w0"], dw2p, gP["rwkv_a0"], da2p, dg2p, gP["rwkv_k_k"], gP["rwkv_k_a"] = rowmap(
        pre_bwd, pre_rows, pre_params, [(RKV, F32), (LORA, F32)],
        [(1, D_MODEL), (LORA, D_MODEL), (1, D_MODEL), (LORA, D_MODEL), (LORA, D_MODEL), (1, D_MODEL), (1, D_MODEL)],
        tb=128, name="rwkv_pre_bwd")
    gW["rwkv_w2"] = dw2p[:LORA_W]
    gW["rwkv_a2"] = da2p[LORA_W:LORA_W + LORA_A]
    gW["rwkv_g2"] = dg2p[LORA_W + LORA_A:]
    dp_rk, dmu_rk = token_shift_bwd(dxs_rk, p_rk, mu_rk, tb=256, name="shift_rk_bwd")
    dp_lo, dmu_lo = token_shift_bwd(dxs_lo, p_lo, mu_lo, tb=256, name="shift_lora_bwd")
    gP["rwkv_mu"] = jnp.concatenate([dmu_rk, dmu_lo], axis=1)

    dh = matmul(dp_rk, w_rkv, "nt", "dh_rkv")
    dh = matmul(dp_lo, w_lora, "nt", "dh_lora", add=dh)
    dh = matmul(dp_qkv, w_qkv, "nt", "dh_qkv", add=dh)
    dh = matmul(dp_gate, w_gate, "nt", "dh_gate", add=dh)
    gW["w_in"] = jnp.concatenate([
        matmul(h, dp_rk, "tn", "dw_rkv"), matmul(h, dp_lo, "tn", "dw_lora"),
        matmul(h, dp_qkv, "tn", "dw_qkv"), matmul(h, dp_gate, "tn", "dw_gate")], axis=1)

    token = on_mixer_grads(gW)

    def norm_bwd(x_b, dh_b, dy_b, gn, tok):
        dx, dgn = jax.vjp(_rms, x_b, gn)[1](dh_b)
        return dy_b + dx + tok[0:1, 0:1], dgn

    dx1, gP["mix_norm"] = rowmap(norm_bwd, [x1, dh, dx2], [P["mix_norm"], token], [(D_MODEL, F32)],
                                 [(1, D_MODEL)], tb=256, name="mix_norm_bwd")
    dx, gP["ffn1_norm"], gW["ffn1_w_in"], gW["ffn1_w_out"] = _ffn_bwd(
        dx1, ffn1_saved, P["ffn1_norm"], W["ffn1_w_in"], W["ffn1_w_out"], "ffn1", on_ffn1_grads)
    return loss_cols, dx, gW, gP


N_SHARDS = 4
OTHER_CHIPS = N_SHARDS - 1
BIG = (("ffn1_w_in", (D_MODEL, 2 * D_FF), 1), ("ffn1_w_out", (D_FF, D_MODEL), 0),
       ("w_in", (D_MODEL, 7712), 1), ("rwkv_w2", (LORA_W, D_MODEL), 1), ("rwkv_a2", (LORA_A, D_MODEL), 1),
       ("rwkv_g2", (LORA_G, D_MODEL), 1), ("w_proj_rwkv", (D_MODEL, D_MODEL), 0),
       ("w_proj_attn", (ATTN_WIDTH, D_MODEL), 1), ("w_out", (D_MODEL, D_MODEL), 0),
       ("ffn2_w_in", (D_MODEL, 2 * D_FF), 1), ("ffn2_w_out", (D_FF, D_MODEL), 0))
SMALL = (("ffn1_norm", 1024), ("mix_norm", 1024), ("b_gate", 2048), ("rwkv_mu", 3360), ("rwkv_w0", 1024),
         ("rwkv_a0", 1024), ("rwkv_k_k", 1024), ("rwkv_k_a", 1024), ("rwkv_r_k", 1024), ("rwkv_ln_w", 1024),
         ("rwkv_ln_b", 1024), ("attn_q_norm", 64), ("attn_k_norm", 64), ("ffn2_norm", 1024))
WEIGHT_ORDER = ("ffn1_norm", "ffn1_w_in", "ffn1_w_out", "mix_norm", "w_in", "b_gate", "rwkv_mu", "rwkv_w0",
                "rwkv_w2", "rwkv_a0", "rwkv_a2", "rwkv_g2", "rwkv_k_k", "rwkv_k_a", "rwkv_r_k", "rwkv_ln_w",
                "rwkv_ln_b", "attn_q_norm", "attn_k_norm", "w_proj_rwkv", "w_proj_attn", "w_out", "ffn2_norm",
                "ffn2_w_in", "ffn2_w_out")


LORA_PARTS = ("rwkv_w2", "rwkv_a2", "rwkv_g2")
BLOCK_MAJOR = ("ffn1_w_in", "ffn2_w_in")
FIRST_FFN = ("ffn1_w_in", "ffn1_w_out")
MIXER_IN = ("w_in", "lora")
SMALL_USED = D_MODEL + sum(n for _, n in SMALL)
SMALL_W = -(-SMALL_USED // 128) * 128


def _travel():
    out = {}
    for name, shape, axis in BIG:
        if name == LORA_PARTS[0]:
            out["lora"] = ((LORA, D_MODEL), 1)
        elif name not in LORA_PARTS:
            out[name] = (shape, axis)
    return out


def local_blocks(vals):
    out = {n: vals[n] for n in _travel() if n != "lora"}
    out["lora"] = jnp.concatenate([vals[n] for n in LORA_PARTS], axis=0)
    return out


def split_lora(t):
    return {"rwkv_w2": t[:LORA_W], "rwkv_a2": t[LORA_W:LORA_W + LORA_A], "rwkv_g2": t[LORA_W + LORA_A:]}


def blocks_to_full(name, blocks):
    shape, axis = _travel()[name]
    if name in BLOCK_MAJOR:
        return blocks
    if axis == 0:
        return blocks.reshape(shape)
    return blocks.transpose(1, 0, 2).reshape(shape)


def full_to_blocks(name, full):
    shape, axis = _travel()[name]
    if name in BLOCK_MAJOR:
        return full
    if axis == 0:
        return full.reshape(N_SHARDS, shape[0] // N_SHARDS, shape[1])
    return full.reshape(shape[0], N_SHARDS, shape[1] // N_SHARDS).transpose(1, 0, 2)


def pack_small(vals, head):
    parts = [head] + [vals[name].reshape(1, n) for name, n in SMALL]
    parts.append(jnp.zeros((1, SMALL_W - SMALL_USED), F32))
    return jnp.concatenate(parts, axis=1)


def unpack_small(vec, shapes):
    out, off = {}, D_MODEL
    for name, n in SMALL:
        out[name] = vec[:, off:off + n].reshape(shapes[name])
        off += n
    return out


def _place():
    return lax.axis_index("x"), lax.axis_index("y"), lax.axis_index("c")


def _other_chips(x, y):
    return [(1 - x, y), (x, 1 - y), (1 - x, 1 - y)]


def _remote(src, dst, send_sem, recv_sem, device):
    return pltpu.make_async_remote_copy(src_ref=src, dst_ref=dst, send_sem=send_sem, recv_sem=recv_sem,
                                        device_id=device, device_id_type=MESH)


def _half(ref, who):
    hr = ref.shape[-2] // 2
    rows = pl.ds(pl.multiple_of(who * hr, 8), hr)
    return ref.at[rows] if len(ref.shape) == 2 else ref.at[:, rows]


HBM_REF = pl.BlockSpec(memory_space=pl.ANY)
COMM_PARAMS = dict(compiler_params=pltpu.CompilerParams(has_side_effects=True))


def gather_weights(blocks):
    n = len(blocks)

    def body(*refs):
        ins, outs = refs[:n], refs[n:2 * n]
        ici_send, ici_recv, d2d_send, d2d_recv = refs[2 * n:]
        x, y, c = _place()
        me, sibling, chips = 2 * x + y, (x, y, 1 - c), _other_chips(x, y)
        first = [_remote(_half(ins[t], c), _half(outs[t].at[me], c), ici_send.at[k, t], ici_recv.at[k, t],
                         (px, py, c)) for k, (px, py) in enumerate(chips) for t in range(n)]
        for cp in first:
            cp.start()
        passed = []
        for k, (px, py) in enumerate(chips):
            for t in range(n):
                landed = _half(outs[t].at[2 * px + py], c)
                _remote(landed, landed, ici_send.at[k, t], ici_recv.at[k, t], (px, py, c)).wait_recv()
                cp = _remote(landed, landed, d2d_send.at[k, t], d2d_recv.at[k, t], sibling)
                cp.start()
                passed.append(cp)
        for k, (px, py) in enumerate(chips):
            for t in range(n):
                other = _half(outs[t].at[2 * px + py], 1 - c)
                _remote(other, other, d2d_send.at[k, t], d2d_recv.at[k, t], sibling).wait_recv()
        for cp in first + passed:
            cp.wait_send()

    res = pl.pallas_call(
        body, name="gather_weights", in_specs=[HBM_REF] * n, out_specs=[HBM_REF] * n,
        out_shape=[jax.ShapeDtypeStruct((N_SHARDS,) + b.shape, b.dtype) for b in blocks],
        scratch_shapes=[pltpu.SemaphoreType.DMA((3, n))] * 4, **COMM_PARAMS)(*blocks)
    me = 2 * lax.axis_index("x") + lax.axis_index("y")
    return [lax.dynamic_update_slice(g, b[None], (me, 0, 0)) for g, b in zip(res, blocks)]


def _gather_copies(ins, outs, send_sem, recv_sem):
    x, y, c = _place()
    return [_remote(_half(ins[t], c), _half(outs[t].at[2 * x + y], c), send_sem(k, t), recv_sem(k, t), (px, py, c))
            for k, (px, py) in enumerate(_other_chips(x, y)) for t in range(len(ins))]


def split_start(copies, sources, landing_shapes, name):
    n = len(sources)
    n_cp = OTHER_CHIPS * n

    def body(*refs):
        srcs, dsts = refs[:n], refs[n:2 * n]
        sems, token = refs[2 * n:2 * n + 2 * n_cp], refs[-1]
        for cp in copies(srcs, dsts, lambda k, t: sems[k * n + t], lambda k, t: sems[n_cp + k * n + t]):
            cp.start()
        token[...] = jnp.zeros_like(token)

    hbm = lambda a: pltpu.with_memory_space_constraint(a, pltpu.HBM)
    buffers = list(sources) + [lax.empty(shape, s.dtype) for shape, s in zip(landing_shapes, sources)]
    res = pl.pallas_call(
        body, name=name,
        out_shape=(*[pltpu.SemaphoreType.DMA(())] * (2 * n_cp),
                   *[pltpu.HBM(a.shape, a.dtype) for a in buffers], jax.ShapeDtypeStruct((8, 128), F32)),
        in_specs=[SPLIT_HBM] * (2 * n),
        out_specs=(*[SPLIT_SEM] * (2 * n_cp), *[SPLIT_HBM] * (2 * n), pl.BlockSpec(memory_space=pltpu.VMEM)),
        input_output_aliases={t: 2 * n_cp + t for t in range(2 * n)}, **SPLIT_PARAMS,
    )(*[hbm(a) for a in buffers])
    return (copies, n, res[:-1]), res[-1]


def split_wait(handles, after, name):
    copies, n, held = handles
    n_cp = OTHER_CHIPS * n
    sems, thru = held[:2 * n_cp], held[2 * n_cp:]

    def body(*refs):
        srcs, dsts = refs[:n], refs[n:2 * n]
        sem_refs = refs[2 * n:2 * n + 2 * n_cp]
        for cp in copies(srcs, dsts, lambda k, t: sem_refs[k * n + t], lambda k, t: sem_refs[n_cp + k * n + t]):
            cp.wait_send()
            cp.wait_recv()

    res = pl.pallas_call(
        body, name=name, out_shape=tuple(pltpu.HBM(a.shape, a.dtype) for a in thru),
        in_specs=[SPLIT_HBM] * (2 * n) + [SPLIT_SEM] * (2 * n_cp) + [pl.BlockSpec(memory_space=pl.ANY)],
        out_specs=tuple([SPLIT_HBM] * (2 * n)), input_output_aliases={t: t for t in range(2 * n)}, **SPLIT_PARAMS,
    )(*thru, *sems, after)
    return list(res[n:])


def gather_start(blocks, name):
    return split_start(_gather_copies, blocks, [(N_SHARDS,) + b.shape for b in blocks], name)


def pass_halves(gathered, blocks, name):
    n = len(gathered)

    def body(*refs):
        outs = refs[n:2 * n]
        send_sems, recv_sems = refs[2 * n:]
        x, y, c = _place()
        slots = [2 * px + py for px, py in _other_chips(x, y)]
        give = [_remote(_half(outs[t].at[s], c), _half(outs[t].at[s], c), send_sems.at[k, t], recv_sems.at[k, t],
                        (x, y, 1 - c)) for k, s in enumerate(slots) for t in range(n)]
        for cp in give:
            cp.start()
        for k, s in enumerate(slots):
            for t in range(n):
                other = _half(outs[t].at[s], 1 - c)
                _remote(other, other, send_sems.at[k, t], recv_sems.at[k, t], (x, y, 1 - c)).wait_recv()
        for cp in give:
            cp.wait_send()

    res = pl.pallas_call(
        body, name=name, in_specs=[HBM_REF] * n, out_specs=[HBM_REF] * n,
        out_shape=[jax.ShapeDtypeStruct(g.shape, g.dtype) for g in gathered],
        input_output_aliases={t: t for t in range(n)},
        scratch_shapes=[pltpu.SemaphoreType.DMA((3, n))] * 2, **COMM_PARAMS)(*gathered)
    me = 2 * lax.axis_index("x") + lax.axis_index("y")
    return [lax.dynamic_update_slice(g, b[None], (me, 0, 0)) for g, b in zip(res, blocks)]


def swap_halves(grads):
    n = len(grads)

    def body(*refs):
        ins, got = refs[:n], refs[n:2 * n]
        send_sems, recv_sems = refs[2 * n:]
        x, y, c = _place()
        give = [_remote(_half(ins[t], 1 - c), got[t], send_sems.at[t], recv_sems.at[t], (x, y, 1 - c))
                for t in range(n)]
        for cp in give:
            cp.start()
        for cp in give:
            cp.wait_recv()
        for cp in give:
            cp.wait_send()

    return pl.pallas_call(
        body, name="swap_halves", in_specs=[HBM_REF] * n, out_specs=[HBM_REF] * n,
        out_shape=[jax.ShapeDtypeStruct((g.shape[0], g.shape[1] // 2, g.shape[2]), g.dtype) for g in grads],
        scratch_shapes=[pltpu.SemaphoreType.DMA((n,))] * 2, **COMM_PARAMS)(*grads)


def join_halves(blocks):
    n = len(blocks)

    def body(*refs):
        outs = refs[n:2 * n]
        send_sems, recv_sems = refs[2 * n:]
        x, y, c = _place()
        give = [_remote(_half(outs[t], c), _half(outs[t], c), send_sems.at[t], recv_sems.at[t], (x, y, 1 - c))
                for t in range(n)]
        for cp in give:
            cp.start()
        for t in range(n):
            arriving = _half(outs[t], 1 - c)
            _remote(arriving, arriving, send_sems.at[t], recv_sems.at[t], (x, y, 1 - c)).wait_recv()
        for cp in give:
            cp.wait_send()

    return pl.pallas_call(
        body, name="join_halves", in_specs=[HBM_REF] * n, out_specs=[HBM_REF] * n,
        out_shape=[jax.ShapeDtypeStruct(b.shape, b.dtype) for b in blocks],
        input_output_aliases={t: t for t in range(n)},
        scratch_shapes=[pltpu.SemaphoreType.DMA((n,))] * 2, **COMM_PARAMS)(*blocks)


SPLIT_HBM = pl.BlockSpec(memory_space=pltpu.HBM)
SPLIT_SEM = pl.BlockSpec(memory_space=pltpu.SEMAPHORE)
SPLIT_PARAMS = dict(compiler_params=pltpu.CompilerParams(has_side_effects=pltpu.SideEffectType.DATAFLOW_SIDE_EFFECTING))


def _scatter_copies(parts, landed, send_sem, recv_sem):
    x, y, c = _place()
    return [_remote(parts[t].at[2 * px + py], landed[t].at[k], send_sem(k, t), recv_sem(k, t), (px, py, c))
            for k, (px, py) in enumerate(_other_chips(x, y)) for t in range(len(parts))]


def scatter_start(partials, name):
    return split_start(_scatter_copies, partials, [(OTHER_CHIPS,) + p.shape[1:] for p in partials], name)


def chip_sums(grads, got):
    names = list(grads)
    partials = []
    for name, theirs in zip(names, got):
        n_slot, hr, width = theirs.shape
        tb = _row_block(hr, width, 6)
        per_half = hr // tb
        mine = lambda i, s, per_half=per_half: (i // per_half) * 2 * per_half + s[0] * per_half + i % per_half
        p = placed_map(
            jnp.add,
            [(grads[name].reshape(2 * n_slot * hr, width), mine), (theirs.reshape(n_slot * hr, width), lambda i, s: i)],
            (n_slot * hr, width, BF16, lambda i, s: i), n_blocks=n_slot * per_half, tb=tb, name="chip_sum_" + name)
        partials.append(p.reshape(theirs.shape))
    return partials


def owner_sums(grads, got, landed):
    names = list(grads)
    blocks = []
    for name, theirs, arrived in zip(names, got, landed):
        n_slot, hr, width = theirs.shape
        tb = _row_block(hr, width, 6)
        per_half = hr // tb
        views = [(grads[name].reshape(2 * n_slot * hr, width),
                  lambda i, s, per_half=per_half: s[1] * 2 * per_half + s[0] * per_half + i),
                 (theirs.reshape(n_slot * hr, width), lambda i, s, per_half=per_half: s[1] * per_half + i)]
        views += [(arrived.reshape(3 * hr, width), functools.partial(lambda k, per_half, i, s: k * per_half + i,
                                                                     k, per_half)) for k in range(3)]
        f = lambda a, b, l0, l1, l2: (((a + b) + l0.astype(F32)) + l1.astype(F32)) + l2.astype(F32)
        blocks.append(placed_map(
            f, views,(2 * hr, width, F32, lambda i, s, per_half=per_half: s[0] * per_half + i),
            n_blocks=per_half, tb=tb, name="owner_sum_" + name))
    return dict(zip(names, join_halves(blocks)))


def adamw_block(name, w, g, m, v):
    rows, width = w.shape
    return rowmap(_adamw, [w, g, m, v], [], [(width, F32)] * 3, tb=_row_block(rows, width, 7),
                  name="adamw_" + name)


def reduce_small(vec, w, m, v):
    n_dev = 8

    def body(vec_ref, w_ref, m_ref, v_ref, loss_ref, g_ref, d_ref, m2_ref, v2_ref, slots, send_sems, recv_sems):
        x, y, c = _place()
        me = 4 * x + 2 * y + c
        slots[me] = vec_ref[...]
        flips = [(fx, fy, fc) for fx in (0, 1) for fy in (0, 1) for fc in (0, 1)][1:]
        peers = [(1 - x if fx else x, 1 - y if fy else y, 1 - c if fc else c) for fx, fy, fc in flips]
        sends = [pltpu.make_async_remote_copy(
            src_ref=vec_ref, dst_ref=slots.at[me], send_sem=send_sems.at[j], recv_sem=recv_sems.at[j],
            device_id=peer, device_id_type=MESH) for j, peer in enumerate(peers)]
        for cp in sends:
            cp.start()
        for j, (px, py, pc) in enumerate(peers):
            pltpu.make_async_remote_copy(
                src_ref=vec_ref, dst_ref=slots.at[4 * px + 2 * py + pc], send_sem=send_sems.at[j],
                recv_sem=recv_sems.at[j], device_id=(px, py, pc), device_id_type=MESH).wait_recv()
        for cp in sends:
            cp.wait_send()
        g = slots[0]
        for d in range(1, n_dev):
            g = g + slots[d]
        loss_ref[...] = jnp.sum(g[:, :D_MODEL], axis=1, keepdims=True)
        delta, m2, v2 = _adamw(w_ref[...], g, m_ref[...], v_ref[...])
        g_ref[...], d_ref[...], m2_ref[...], v2_ref[...] = g, delta, m2, v2

    vm = pl.BlockSpec(memory_space=pltpu.VMEM)
    vec_t = jax.ShapeDtypeStruct(vec.shape, F32)
    return pl.pallas_call(
        body, name="reduce_small", in_specs=[vm] * 4, out_specs=[vm] * 5,
        out_shape=[jax.ShapeDtypeStruct((1, 1), F32)] + [vec_t] * 4,
        scratch_shapes=[pltpu.VMEM((n_dev,) + vec.shape, F32), pltpu.SemaphoreType.DMA((n_dev - 1,)),
                        pltpu.SemaphoreType.DMA((n_dev - 1,))],
        compiler_params=pltpu.CompilerParams(has_side_effects=True),
    )(vec, w, m, v)


def kernel(x, ffn1_norm, ffn1_w_in, ffn1_w_out, mix_norm, w_in, b_gate, rwkv_mu, rwkv_w0, rwkv_w2, rwkv_a0, rwkv_a2, rwkv_g2, rwkv_k_k, rwkv_k_a, rwkv_r_k, rwkv_ln_w, rwkv_ln_b, attn_q_norm, attn_k_norm, w_proj_rwkv, w_proj_attn, w_out, ffn2_norm, ffn2_w_in, ffn2_w_out, loss_target, m_ffn1_norm, m_ffn1_w_in, m_ffn1_w_out, m_mix_norm, m_w_in, m_b_gate, m_rwkv_mu, m_rwkv_w0, m_rwkv_w2, m_rwkv_a0, m_rwkv_a2, m_rwkv_g2, m_rwkv_k_k, m_rwkv_k_a, m_rwkv_r_k, m_rwkv_ln_w, m_rwkv_ln_b, m_attn_q_norm, m_attn_k_norm, m_w_proj_rwkv, m_w_proj_attn, m_w_out, m_ffn2_norm, m_ffn2_w_in, m_ffn2_w_out, v_ffn1_norm, v_ffn1_w_in, v_ffn1_w_out, v_mix_norm, v_w_in, v_b_gate, v_rwkv_mu, v_rwkv_w0, v_rwkv_w2, v_rwkv_a0, v_rwkv_a2, v_rwkv_g2, v_rwkv_k_k, v_rwkv_k_a, v_rwkv_r_k, v_rwkv_ln_w, v_rwkv_ln_b, v_attn_q_norm, v_attn_k_norm, v_w_proj_rwkv, v_w_proj_attn, v_w_out, v_ffn2_norm, v_ffn2_w_in, v_ffn2_w_out):
    given = dict(locals())
    weights = {n: given[n] for n in WEIGHT_ORDER}
    mom_m = {n: given["m_" + n] for n in WEIGHT_ORDER}
    mom_v = {n: given["v_" + n] for n in WEIGHT_ORDER}
    big = [name for name, _, _ in BIG]
    shapes = {n: weights[n].shape for n in WEIGHT_ORDER}
    blocks_of = lambda d: local_blocks({n: d[n][0] for n in big})
    w_blk, m_blk, v_blk = blocks_of(weights), blocks_of(mom_m), blocks_of(mom_v)
    names = list(w_blk)

    early = [n for n in names if n not in FIRST_FFN]
    bf16_block = lambda n: w_blk[n].astype(BF16)
    W = {n: blocks_to_full(n, g) for n, g in zip(FIRST_FFN, gather_weights([bf16_block(n) for n in FIRST_FFN]))}
    stages = {"mixer": [n for n in early if n in MIXER_IN], "out": [n for n in early if n not in MIXER_IN]}
    stage_blocks = {s: [bf16_block(n) for n in stages[s]] for s in stages}
    started = {s: gather_start(stage_blocks[s], "gather_start_" + s) for s in ("mixer", "out")}
    start_token = started["mixer"][1] + started["out"][1]

    def more_weights(stage, after):
        landed = split_wait(started[stage][0], after, "gather_wait_" + stage)
        got = pass_halves(landed, stage_blocks[stage], "pass_halves_" + stage)
        more = {n: blocks_to_full(n, g) for n, g in zip(stages[stage], got)}
        if "lora" in more:
            more.update(split_lora(more.pop("lora")))
        return more

    P = {n: weights[n].reshape(1, -1) for n, _ in SMALL}

    sent = {}

    def send_early(gw):
        lora = jnp.concatenate([gw[n] for n in LORA_PARTS], axis=0)
        sent["grads"] = {n: full_to_blocks(n, lora if n == "lora" else gw[n]) for n in early}
        sent["got"] = swap_halves(list(sent["grads"].values()))
        sent["handles"], token = scatter_start(chip_sums(sent["grads"], sent["got"]), "scatter_start")
        return token

    def send_late(dw_in, dw_out):
        sent["late"] = {n: full_to_blocks(n, g) for n, g in zip(FIRST_FFN, (dw_in, dw_out))}
        sent["late_got"] = swap_halves(list(sent["late"].values()))
        sent["late_handles"], token = scatter_start(chip_sums(sent["late"], sent["late_got"]), "scatter_start_ffn1")
        return token

    loss_cols, dx, gW, gP = layer_step(x[0], loss_target[0], W, P, start_token, more_weights, send_early, send_late)
    landed = split_wait(sent["handles"], gP["ffn1_norm"], "scatter_wait")
    out_g, out_d, out_m, out_v = {}, {}, {}, {}

    def apply(g_blk):
        for n in g_blk:
            res = (g_blk[n], *adamw_block(n, w_blk[n], g_blk[n], m_blk[n], v_blk[n]))
            for dst, t in zip((out_g, out_d, out_m, out_v), res):
                for part, val in (split_lora(t) if n == "lora" else {n: t}).items():
                    dst[part] = val.reshape(shapes[part])

    apply(owner_sums(sent["grads"], sent["got"], landed))
    late_landed = split_wait(sent["late_handles"], list(out_d.values())[-1], "scatter_wait_ffn1")
    apply(owner_sums(sent["late"], sent["late_got"], late_landed))

    zero_head = jnp.zeros((1, D_MODEL), F32)
    vec = pack_small(gP, loss_cols)
    loss, g_s, d_s, m_s, v_s = reduce_small(
        vec, pack_small({n: weights[n] for n, _ in SMALL}, zero_head),
        pack_small({n: mom_m[n] for n, _ in SMALL}, zero_head),
        pack_small({n: mom_v[n] for n, _ in SMALL}, zero_head))
    for dst, src in ((out_g, g_s), (out_d, d_s), (out_m, m_s), (out_v, v_s)):
        dst.update(unpack_small(src, shapes))

    return (loss[0, 0], dx[None], *[out_g[n] for n in WEIGHT_ORDER], *[out_d[n] for n in WEIGHT_ORDER],
            *[out_m[n] for n in WEIGHT_ORDER], *[out_v[n] for n in WEIGHT_ORDER])
```

```python
import functools

import jax
import jax.numpy as jnp
from jax import lax
from jax.experimental import pallas as pl
from jax.experimental.pallas import tpu as pltpu

F32 = jnp.float32
BF16 = jnp.bfloat16
MESH = pl.DeviceIdType.MESH

D_MODEL = 1024
HEAD_DIM = 64
RWKV_HEADS = 16
LORA_W, LORA_A, LORA_G = 64, 64, 160
LORA = LORA_W + LORA_A + LORA_G
RKV = 3 * D_MODEL
ATTN_PAIRS = ((128, 1), (512, 4), (2048, 16))
ATTN_BLK = 128
ATTN_HPG = 4
ATTN_WIDTH = 768
GROUP_W = ATTN_HPG * HEAD_DIM
D_FF = 2816
GN_EPS = 64e-5
RMS_EPS = 1e-6
NEG_INF = -1e30
WKV_CHUNK = 128
WKV_HEADS_PER_STEP = 16
WKV_HEADS_PER_RUN = 16

ADAM_LR, ADAM_B1, ADAM_B2, ADAM_EPS, ADAM_WD, ADAM_STEP = 0.001, 0.9, 0.999, 1e-08, 0.01, 10

V7X_VMEM_BYTES = 64 << 20
VMEM_TEMP_ALLOWANCE = 20 << 20
VMEM_LEFT_FREE = 6 << 20


def _cparams(sem, block_bytes):
    limit = min(2 * block_bytes + VMEM_TEMP_ALLOWANCE, V7X_VMEM_BYTES - VMEM_LEFT_FREE)
    return pltpu.CompilerParams(dimension_semantics=sem, vmem_limit_bytes=int(limit))


def _nbytes(shape, dtype):
    n = 1
    for s in shape:
        n *= s
    return n * jnp.dtype(dtype).itemsize


def _split_bf16(a):
    hi = a.astype(BF16)
    return hi, (a - hi.astype(F32)).astype(BF16)


def _make_dots():
    def raw(a, b, ca, cb):
        return lax.dot_general(a.astype(BF16), b.astype(BF16), (((ca,), (cb,)), ((), ())),
                               preferred_element_type=F32)

    @jax.custom_vjp
    def nn(a, b):
        return raw(a, b, 1, 0)

    @jax.custom_vjp
    def nt(a, b):
        return raw(a, b, 1, 1)

    @jax.custom_vjp
    def tn(a, b):
        return raw(a, b, 0, 0)

    nn.defvjp(lambda a, b: (raw(a, b, 1, 0), (a, b)),
              lambda res, g: (raw(g, res[1], 1, 1), raw(res[0], g, 0, 0)))
    nt.defvjp(lambda a, b: (raw(a, b, 1, 1), (a, b)),
              lambda res, g: (raw(g, res[1], 1, 0), raw(g, res[0], 0, 0)))
    tn.defvjp(lambda a, b: (raw(a, b, 0, 0), (a, b)),
              lambda res, g: (raw(res[1], g, 1, 1), raw(res[0], g, 1, 0)))
    return nn, nt, tn


def _exact_rhs_dot(x, ones, cx, co):
    hi, lo = _split_bf16(x)
    dims = (((cx,), (co,)), ((), ()))
    return (lax.dot_general(hi, ones, dims, preferred_element_type=F32)
            + lax.dot_general(lo, ones, dims, preferred_element_type=F32))


@jax.custom_vjp
def SEG(x, ones):
    return _exact_rhs_dot(x, ones, 1, 0)


SEG.defvjp(lambda x, ones: (_exact_rhs_dot(x, ones, 1, 0), ones),
           lambda ones, g: (_exact_rhs_dot(g, ones, 1, 1), jnp.zeros_like(ones)))

NN, NT, TN = _make_dots()


MM_TILE_M, MM_TILE_N, MM_TILE_K = 1408, 1408, 1536


def _pick(n, cap):
    best = None
    for t in range(128, min(n, cap) + 1, 128):
        if n % t == 0:
            best = t
    return best or n


def matmul(a, b, mode, name, *, add=None, scale=1.0):
    if mode == "nn":
        (M, K), (K2, N) = a.shape, b.shape
    elif mode == "nt":
        (M, K), (N, K2) = a.shape, b.shape
    else:
        (K, M), (K2, N) = a.shape, b.shape
    assert K == K2, (name, a.shape, b.shape)
    tm, tn, tk = _pick(M, MM_TILE_M), _pick(N, MM_TILE_N), _pick(K, MM_TILE_K)
    nk = K // tk
    ca, cb = {"nn": (1, 0), "nt": (1, 1), "tn": (0, 0)}[mode]

    def body(*refs):
        if add is None:
            a_ref, b_ref, o_ref, acc_ref = refs
        else:
            a_ref, b_ref, add_ref, o_ref, acc_ref = refs
        k = pl.program_id(2)

        @pl.when(k == 0)
        def _():
            acc_ref[...] = jnp.zeros_like(acc_ref)

        acc_ref[...] += lax.dot_general(a_ref[...].astype(BF16), b_ref[...].astype(BF16),
                                        (((ca,), (cb,)), ((), ())), preferred_element_type=F32)

        @pl.when(k == nk - 1)
        def _():
            r = acc_ref[...] * scale
            if add is not None:
                r = add_ref[...] + r
            o_ref[...] = r.astype(o_ref.dtype)

    a_spec = (pl.BlockSpec((tk, tm), lambda i, j, k: (k, i)) if mode == "tn"
              else pl.BlockSpec((tm, tk), lambda i, j, k: (i, k)))
    b_spec = (pl.BlockSpec((tn, tk), lambda i, j, k: (j, k)) if mode == "nt"
              else pl.BlockSpec((tk, tn), lambda i, j, k: (k, j)))
    in_specs, args = [a_spec, b_spec], [a, b]
    blk = tm * tk * a.dtype.itemsize + tk * tn * b.dtype.itemsize + tm * tn * 8
    if add is not None:
        in_specs.append(pl.BlockSpec((tm, tn), lambda i, j, k: (i, j)))
        args.append(add)
        blk += tm * tn * 4
    return pl.pallas_call(
        body, name=name, grid=(M // tm, N // tn, nk),
        in_specs=in_specs, out_specs=pl.BlockSpec((tm, tn), lambda i, j, k: (i, j)),
        out_shape=jax.ShapeDtypeStruct((M, N), F32),
        scratch_shapes=[pltpu.VMEM((tm, tn), F32)],
        compiler_params=_cparams(("parallel", "parallel", "arbitrary"), blk),
    )(*args)


def matmul_cs(a, w, mode, name, token):
    n_blk = N_SHARDS
    if mode == "tn":
        (K, R), Cs = a.shape, w.shape[2] // 2
        tm, tk = _pick(R, MM_TILE_M), _pick(K, 1024)
        grid = (R // tm, n_blk, K // tk)
        a_spec = pl.BlockSpec((tk, tm), lambda i, j, k: (k, i))
        w_spec = pl.BlockSpec((None, tk, Cs), lambda i, j, k: (j // 2, k, j % 2))
        o_spec = pl.BlockSpec((None, tm, Cs), lambda i, j, k: (j, i, 0))
        out_shape, acc_shape, dims = (n_blk, R, Cs), (tm, Cs), (0, 0)
        blk = tk * tm * a.dtype.itemsize + tk * Cs * w.dtype.itemsize + tm * Cs * 8
    else:
        M, (_, R, Cs) = a.shape[1], w.shape
        tm, tn = _pick(M, MM_TILE_M), _pick(R, MM_TILE_N)
        grid = (M // tm, R // tn, n_blk)
        a_spec = pl.BlockSpec((None, tm, Cs), lambda i, j, k: (k // 2, i, k % 2))
        w_spec = pl.BlockSpec((None, tn, Cs), lambda i, j, k: (k, j, 0))
        o_spec = pl.BlockSpec((tm, tn), lambda i, j, k: (i, j))
        out_shape, acc_shape, dims = (M, R), (tm, tn), (1, 1)
        blk = tm * Cs * a.dtype.itemsize + tn * Cs * w.dtype.itemsize + tm * tn * 8
    nk = grid[2]

    def body(a_ref, w_ref, tok_ref, o_ref, acc_ref):
        k = pl.program_id(2)

        @pl.when(k == 0)
        def _():
            acc_ref[...] = jnp.zeros_like(acc_ref)

        acc_ref[...] += lax.dot_general(a_ref[...].astype(BF16), w_ref[...].astype(BF16),
                                        (((dims[0],), (dims[1],)), ((), ())), preferred_element_type=F32)

        @pl.when(k == nk - 1)
        def _():
            o_ref[...] = acc_ref[...] + tok_ref[0:1, 0:1]

    return pl.pallas_call(
        body, name=name, grid=grid, in_specs=[a_spec, w_spec, pl.BlockSpec(token.shape, lambda i, j, k: (0, 0))],
        out_specs=o_spec, out_shape=jax.ShapeDtypeStruct(out_shape, F32), scratch_shapes=[pltpu.VMEM(acc_shape, F32)],
        compiler_params=_cparams(("parallel", "parallel", "arbitrary"), blk),
    )(a, w, token)


FFN_TILE_M = 512


def _swiglu(gate, up):
    return gate * jax.nn.sigmoid(gate) * up


def ffn_in_act(h, w, name):
    (M, R), Cs, half = h.shape, w.shape[2], N_SHARDS // 2
    tm, tk = _pick(M, FFN_TILE_M), _pick(R, 1024)
    nk = R // tk

    def body(h_ref, wg_ref, wu_ref, gu_ref, act_ref, acc_ref):
        k = pl.program_id(2)

        @pl.when(k == 0)
        def _():
            acc_ref[...] = jnp.zeros_like(acc_ref)

        hb = h_ref[...].astype(BF16)
        for part, w_ref in enumerate((wg_ref, wu_ref)):
            acc_ref[part] += jnp.dot(hb, w_ref[...].astype(BF16), preferred_element_type=F32)

        @pl.when(k == nk - 1)
        def _():
            gu_ref[...] = acc_ref[...]
            act_ref[...] = _swiglu(acc_ref[0], acc_ref[1]).astype(act_ref.dtype)

    w_spec = lambda off: pl.BlockSpec((None, tk, Cs), functools.partial(lambda off, j, i, k: (j + off, k, 0), off))
    blk = tm * tk * h.dtype.itemsize + 2 * tk * Cs * w.dtype.itemsize + tm * Cs * (16 + 2)
    return pl.pallas_call(
        body, name=name, grid=(half, M // tm, nk),
        in_specs=[pl.BlockSpec((tm, tk), lambda j, i, k: (i, k)), w_spec(0), w_spec(half)],
        out_specs=[pl.BlockSpec((2, tm, Cs), lambda j, i, k: (0, i, j)), pl.BlockSpec((tm, Cs), lambda j, i, k: (i, j))],
        out_shape=[jax.ShapeDtypeStruct((2, M, half * Cs), F32), jax.ShapeDtypeStruct((M, half * Cs), BF16)],
        scratch_shapes=[pltpu.VMEM((2, tm, Cs), F32)],
        compiler_params=_cparams(("parallel", "parallel", "arbitrary"), blk),
    )(h, w, w)


def ffn_dact_dgu(dy, w_out, gu, scale, name):
    (M, D), F = dy.shape, w_out.shape[0]
    tm, tn = _pick(M, FFN_TILE_M), F // 2

    def body(dy_ref, w_ref, gu_ref, dgu_ref):
        dact = scale * lax.dot_general(dy_ref[...].astype(BF16), w_ref[...].astype(BF16),
                                       (((1,), (1,)), ((), ())), preferred_element_type=F32)
        dgate, dup = jax.vjp(_swiglu, gu_ref[0], gu_ref[1])[1](dact)
        dgu_ref[0] = dgate.astype(dgu_ref.dtype)
        dgu_ref[1] = dup.astype(dgu_ref.dtype)

    pair = pl.BlockSpec((2, tm, tn), lambda j, i: (0, i, j))
    blk = tm * D * dy.dtype.itemsize + tn * D * w_out.dtype.itemsize + 2 * tm * tn * (4 + 2)
    return pl.pallas_call(
        body, name=name, grid=(F // tn, M // tm),
        in_specs=[pl.BlockSpec((tm, D), lambda j, i: (i, 0)), pl.BlockSpec((tn, D), lambda j, i: (j, 0)), pair],
        out_specs=pair, out_shape=jax.ShapeDtypeStruct((2, M, F), BF16),
        compiler_params=_cparams(("parallel", "parallel"), blk),
    )(dy, w_out, gu)


def _row_block(n, width, n_arrays):
    cap = (V7X_VMEM_BYTES // 4) // (2 * 4 * width * n_arrays)
    best = None
    for t in range(16, min(n, cap) + 1, 16):
        if n % t == 0:
            best = t
    return best or n


def placed_map(f, ins, out, *, n_blocks, tb, name):
    def body(*refs):
        refs[-1][...] = f(*[r[...] for r in refs[:-1]]).astype(refs[-1].dtype)

    def spec(fn):
        def index(i):
            x, y, c = _place()
            return fn(i, (c, 2 * x + y)), 0
        return pl.BlockSpec((tb, width), index)

    o_rows, width, o_dtype, o_fn = out
    blk = (sum(a.dtype.itemsize for a, _ in ins) + jnp.dtype(o_dtype).itemsize) * tb * width
    return pl.pallas_call(
        body, name=name, grid=(n_blocks,), in_specs=[spec(fn) for _, fn in ins], out_specs=spec(o_fn),
        out_shape=jax.ShapeDtypeStruct((o_rows, width), o_dtype),
        compiler_params=_cparams(("parallel",), blk),
    )(*[a for a, _ in ins])


def rowmap(f, rows, params, outs, accs=(), *, tb, name):
    rows = [r if isinstance(r, tuple) else (r, r.shape[1], 0) for r in rows]
    S = rows[0][0].shape[0]
    assert S % tb == 0, (name, S, tb)
    n_in, n_out = len(rows) + len(params), len(outs)

    def body(*refs):
        res = f(*[r[...] for r in refs[:n_in]])
        res = res if isinstance(res, (tuple, list)) else (res,)
        o_refs, a_refs = refs[n_in:n_in + n_out], refs[n_in + n_out:]
        for ref, val in zip(o_refs, res[:n_out]):
            ref[...] = val.astype(ref.dtype)
        if a_refs:
            @pl.when(pl.program_id(0) == 0)
            def _():
                for ref in a_refs:
                    ref[...] = jnp.zeros_like(ref)

            for ref, val in zip(a_refs, res[n_out:]):
                ref[...] += val.astype(F32)

    in_specs = [pl.BlockSpec((tb, w), functools.partial(lambda cb, i: (i, cb), cb)) for _, w, cb in rows]
    in_specs += [pl.BlockSpec(p.shape, lambda i: (0, 0)) for p in params]
    out_specs = [pl.BlockSpec((tb, w), lambda i: (i, 0)) for w, _ in outs]
    out_specs += [pl.BlockSpec(tuple(s), lambda i: (0, 0)) for s in accs]
    out_shape = [jax.ShapeDtypeStruct((S, w), dt) for w, dt in outs]
    out_shape += [jax.ShapeDtypeStruct(tuple(s), F32) for s in accs]
    blk = sum(tb * w * a.dtype.itemsize for a, w, _ in rows) + sum(_nbytes(p.shape, p.dtype) for p in params)
    blk += sum(_nbytes((tb, w), dt) for w, dt in outs) + sum(_nbytes(s, F32) for s in accs)
    res = pl.pallas_call(
        body, name=name, grid=(S // tb,), in_specs=in_specs, out_specs=out_specs, out_shape=out_shape,
        compiler_params=_cparams(("arbitrary",) if accs else ("parallel",), blk),
    )(*[r[0] for r in rows], *[pltpu.with_memory_space_constraint(p, pltpu.HBM) for p in params])
    return res


def _rms(x, g):
    return x * lax.rsqrt(jnp.mean(x * x, axis=-1, keepdims=True) + RMS_EPS) * g


def _softplus(z):
    return jnp.maximum(z, 0.0) + jnp.log(1.0 + jnp.exp(-jnp.abs(z)))


def _rwkv_pre(xrk, xlo, w0, w2p, a0, a2p, g2p, k_k, k_a, seg, seg_t):
    k = xrk[:, D_MODEL:2 * D_MODEL]
    w = -_softplus(-(w0 + NN(jnp.tanh(xlo), w2p))) - 0.5
    log_decay = -jnp.exp(w)
    a = jax.nn.sigmoid(a0 + NN(xlo, a2p))
    g = NN(jax.nn.sigmoid(xlo), g2p)
    kk = k * k_k
    norm = jnp.maximum(jnp.sqrt(SEG(kk * kk, seg)), 1e-12)
    kk = kk * SEG(1.0 / norm, seg_t)
    k_mod = k * (1.0 + (a - 1.0) * k_a)
    return log_decay, k_mod, -kk, kk * a, g


def _rwkv_post(wkv, r, k_mod, v, g, r_k, ln_w, ln_b, seg, seg_t):
    inv_n = 1.0 / HEAD_DIM
    mean = SEG(wkv, seg) * inv_n
    cen = wkv - SEG(mean, seg_t)
    var = SEG(cen * cen, seg) * inv_n
    y = cen * SEG(lax.rsqrt(var + GN_EPS), seg_t) * ln_w + ln_b
    bonus = SEG(SEG(r * k_mod * r_k, seg), seg_t) * v
    return (y + bonus) * g


def _qk_norm(q, k, q_gain, k_gain, seg, seg_t, tile_t):
    def norm(x, gain):
        mean_sq = SEG(x * x, seg) * (1.0 / HEAD_DIM)
        return x * SEG(lax.rsqrt(mean_sq + RMS_EPS), seg_t) * SEG(gain, tile_t)

    return norm(q, q_gain) * (HEAD_DIM ** -0.5), norm(k, k_gain)


def _gate_merge(pgate, pa, pb, b_gate):
    sg = jax.nn.sigmoid(pgate + b_gate)
    return sg[:, :D_MODEL] * pa + sg[:, D_MODEL:] * pb


def _group_combine(o0, o1, o2, l0, l1, l2):
    m = jnp.maximum(jnp.maximum(l0, l1), l2)
    es = [jnp.exp(l - m) for l in (l0, l1, l2)]
    den = es[0] + es[1] + es[2]
    return jnp.concatenate([o * (e / den) for o, e in zip((o0, o1, o2), es)], axis=1)


def _each(f, *xs):
    return tuple(f(*args) for args in zip(*xs))


def _attn_block(q, kc, kp, vc, vp, first):
    qi = lax.broadcasted_iota(jnp.int32, (ATTN_BLK, ATTN_BLK), 0)
    kj = lax.broadcasted_iota(jnp.int32, (ATTN_BLK, ATTN_BLK), 1)
    own = kj <= qi
    s_c = _each(lambda a, b: jnp.where(own, NT(a, b), NEG_INF), q, kc)
    s_p = _each(lambda a, b, f: jnp.where((kj >= qi) & (f < 0.5), NT(a, b), NEG_INF), q, kp, first)
    row_max = lambda s: jnp.max(s, axis=-1, keepdims=True)
    row_sum = lambda s: jnp.sum(s, axis=-1, keepdims=True)
    m = _each(lambda c_, p_: jnp.maximum(row_max(c_), row_max(p_)), s_c, s_p)
    e_c, e_p = _each(lambda s, m_: jnp.exp(s - m_), s_c, m), _each(lambda s, m_: jnp.exp(s - m_), s_p, m)
    den = _each(lambda c_, p_: row_sum(c_) + row_sum(p_), e_c, e_p)
    inv = _each(lambda d_: 1.0 / d_, den)
    o = _each(lambda ec, ep, i_, vc_, vp_: (NN(ec, vc_) + NN(ep, vp_)) * i_, e_c, e_p, inv, vc, vp)
    lse = _each(lambda m_, d_: jnp.broadcast_to(m_ + jnp.log(d_), (ATTN_BLK, HEAD_DIM)), m, den)
    return o, lse


def _attn_pair(q, k, k_before, v, v_before, first):
    n = len(q[0])
    o, lse = _attn_block(q[0] + q[1], k[0] + k[1], k_before + k[0], v[0] + v[1], v_before + v[0],
                         (first[0],) * n + (first[1],) * n)
    return (o[:n], o[n:]), (lse[:n], lse[n:])


TRI_SEED = 8


def _tri_inverse(n):
    c = n[0].shape[0]
    row = lax.broadcasted_iota(jnp.int32, (c, c), 0)
    col = lax.broadcasted_iota(jnp.int32, (c, c), 1)
    same_block = lambda size: (row >> (size.bit_length() - 1)) == (col >> (size.bit_length() - 1))
    seed = same_block(TRI_SEED)
    p = _each(lambda m: jnp.where(seed, m, 0.0), n)
    t, span = _each(lambda m: (row == col).astype(F32) + m, p), 2
    while span < TRI_SEED:
        p = _each(NN, p, p)
        t = _each(lambda t_, p_: t_ + NN(t_, p_), t, p)
        span *= 2
    size = TRI_SEED
    while size < c:
        joins = same_block(2 * size) & jnp.logical_not(same_block(size))
        t = _each(lambda t_, m: t_ + NN(NN(t_, jnp.where(joins, m, 0.0)), t_), t, n)
        size *= 2
    return t


@jax.custom_vjp
def _tri_solve(n, rhs, t):
    return _each(NN, t, rhs)


def _tri_solve_fwd(n, rhs, t):
    x = _each(NN, t, rhs)
    return x, (t, x)


def _tri_solve_bwd(res, dx):
    t, x = res
    drhs = _each(TN, t, dx)
    return _each(NT, drhs, x), drhs, _each(jnp.zeros_like, t)


_tri_solve.defvjp(_tri_solve_fwd, _tri_solve_bwd)


def _lower_ones(c):
    row = lax.broadcasted_iota(jnp.int32, (c, c), 0)
    col = lax.broadcasted_iota(jnp.int32, (c, c), 1)
    return (row >= col).astype(BF16)


def _ones_dot(ones, x, contract):
    hi, lo = _split_bf16(x)
    dims = (((contract,), (0,)), ((), ()))
    return (lax.dot_general(ones, hi, dims, preferred_element_type=F32)
            + lax.dot_general(ones, lo, dims, preferred_element_type=F32))


@jax.custom_vjp
def _cumsum_rows(x):
    return _ones_dot(_lower_ones(x.shape[0]), x, 1)


_cumsum_rows.defvjp(lambda x: (_ones_dot(_lower_ones(x.shape[0]), x, 1), None),
                    lambda _, g: (_ones_dot(_lower_ones(g.shape[0]), g, 0),))


def _wkv_chunk(s0, r, lw, k, v, a, b, t_inv=None):
    c = r[0].shape[0]
    row = lax.broadcasted_iota(jnp.int32, (c, c), 0)
    col = lax.broadcasted_iota(jnp.int32, (c, c), 1)
    strict, incl = row > col, row >= col
    cat = lambda p, q: jnp.concatenate([p, q], axis=0)
    cum = _each(_cumsum_rows, lw)
    e_neg = _each(lambda c_: jnp.exp(-c_), cum)
    ar = _each(lambda a_, r_, c_, l_: cat(a_ * jnp.exp(c_ - l_), r_ * jnp.exp(c_)), a, r, cum, lw)
    b_t, k_t = _each(jnp.multiply, b, e_neg), _each(jnp.multiply, k, e_neg)
    p_b, p_k, p_s = _each(NT, ar, b_t), _each(NT, ar, k_t), _each(NT, ar, s0)
    n_ab = _each(lambda p: jnp.where(strict, p[:c], 0.0), p_b)
    m_rb = _each(lambda p: jnp.where(incl, p[c:], 0.0), p_b)
    n_ak = _each(lambda p: jnp.where(strict, p[:c], 0.0), p_k)
    m_rk = _each(lambda p: jnp.where(incl, p[c:], 0.0), p_k)
    if t_inv is None:
        t_inv = _tri_inverse(n_ab)
    u = _tri_solve(n_ab, _each(lambda p, n_, v_: p[:c] + NN(n_, v_), p_s, n_ak, v), t_inv)
    y = _each(lambda p, mb, u_, mk, v_: p[c:] + NN(mb, u_) + NN(mk, v_), p_s, m_rb, u, m_rk, v)
    g_end = _each(lambda l_: jnp.exp(jnp.sum(l_, axis=0, keepdims=True)), lw)
    s1 = _each(lambda s_, g_, u_, v_, b_, k_: s_ * g_ + TN(cat(u_, v_), cat(b_, k_) * g_),
               s0, g_end, u, v, b_t, k_t)
    return y, s1, t_inv


def _adamw(w, g, m, v):
    m = ADAM_B1 * m + (1.0 - ADAM_B1) * g
    v = ADAM_B2 * v + (1.0 - ADAM_B2) * jnp.square(g)
    m_hat = m / (1.0 - ADAM_B1 ** ADAM_STEP)
    v_hat = v / (1.0 - ADAM_B2 ** ADAM_STEP)
    delta = -ADAM_LR * (m_hat / (jnp.sqrt(v_hat) + ADAM_EPS) + ADAM_WD * w)
    return delta, m, v


def token_shift_fwd(p, mu, *, tb, name):
    S, W = p.shape
    hb = tb // 8

    def body(p_ref, halo_ref, mu_ref, o_ref):
        i = pl.program_id(0)
        x = p_ref[...]
        before = halo_ref[7:8, :] * (i > 0).astype(F32)
        row = lax.broadcasted_iota(jnp.int32, (tb, W), 0)
        prev = jnp.where(row == 0, before, pltpu.roll(x, 1, 0))
        o_ref[...] = x + (prev - x) * mu_ref[...]

    blk = (2 * tb + 8) * W * 4
    return pl.pallas_call(
        body, name=name, grid=(S // tb,),
        in_specs=[pl.BlockSpec((tb, W), lambda i: (i, 0)),
                  pl.BlockSpec((8, W), lambda i: (jnp.maximum(i * hb - 1, 0), 0)),
                  pl.BlockSpec((1, W), lambda i: (0, 0))],
        out_specs=pl.BlockSpec((tb, W), lambda i: (i, 0)),
        out_shape=jax.ShapeDtypeStruct((S, W), F32),
        compiler_params=_cparams(("parallel",), blk),
    )(p, p, mu)


def token_shift_bwd(dxs, p, mu, *, tb, name):
    S, W = p.shape
    hb, nb = tb // 8, S // tb

    def body(d_ref, dnext_ref, p_ref, halo_ref, mu_ref, dp_ref, dmu_ref):
        i = pl.program_id(0)
        d, x, mu_v = d_ref[...], p_ref[...], mu_ref[...]
        row = lax.broadcasted_iota(jnp.int32, (tb, W), 0)
        before = halo_ref[7:8, :] * (i > 0).astype(F32)
        prev = jnp.where(row == 0, before, pltpu.roll(x, 1, 0))
        t = d * mu_v
        after = dnext_ref[0:1, :] * mu_v * (i < nb - 1).astype(F32)
        nxt = jnp.where(row == tb - 1, after, pltpu.roll(t, tb - 1, 0))
        dp_ref[...] = (d - t + nxt).astype(dp_ref.dtype)

        @pl.when(i == 0)
        def _():
            dmu_ref[...] = jnp.zeros_like(dmu_ref)

        dmu_ref[...] += jnp.sum(d * (prev - x), axis=0, keepdims=True)

    blk = (3 * tb + 16) * W * 4
    return pl.pallas_call(
        body, name=name, grid=(nb,),
        in_specs=[pl.BlockSpec((tb, W), lambda i: (i, 0)),
                  pl.BlockSpec((8, W), lambda i: (jnp.minimum((i + 1) * hb, S // 8 - 1), 0)),
                  pl.BlockSpec((tb, W), lambda i: (i, 0)),
                  pl.BlockSpec((8, W), lambda i: (jnp.maximum(i * hb - 1, 0), 0)),
                  pl.BlockSpec((1, W), lambda i: (0, 0))],
        out_specs=[pl.BlockSpec((tb, W), lambda i: (i, 0)), pl.BlockSpec((1, W), lambda i: (0, 0))],
        out_shape=[jax.ShapeDtypeStruct((S, W), BF16), jax.ShapeDtypeStruct((1, W), F32)],
        compiler_params=_cparams(("arbitrary",), blk),
    )(dxs, dxs, p, p, mu)


def _head_cols(h):
    return pl.ds(h * HEAD_DIM, HEAD_DIM)


def wkv_fwd(xs_rk, lw, k, a, b):
    S = lw.shape[0]
    C, nc, G, N = WKV_CHUNK, S // WKV_CHUNK, WKV_HEADS_PER_STEP, HEAD_DIM

    def body(r_ref, lw_ref, k_ref, v_ref, a_ref, b_ref, y_ref, st_ref, ti_ref, state):
        @pl.when(pl.program_id(1) == 0)
        def _():
            state[...] = jnp.zeros_like(state)

        for base in range(0, G, WKV_HEADS_PER_RUN):
            run = range(base, base + WKV_HEADS_PER_RUN)
            heads = lambda ref: tuple(ref[:, _head_cols(h)] for h in run)
            s0 = tuple(state[h] for h in run)
            y, s1, t_inv = _wkv_chunk(s0, heads(r_ref), heads(lw_ref), heads(k_ref), heads(v_ref), heads(a_ref),
                                      heads(b_ref))
            for i, h in enumerate(run):
                st_ref[h] = s0[i]
                ti_ref[h] = t_inv[i]
                y_ref[:, _head_cols(h)] = y[i]
                state[h] = s1[i]

    W = G * N
    seq = lambda j: pl.BlockSpec((C, W), functools.partial(lambda j, g, c: (c, j + g), j))
    per = D_MODEL // W
    per_chunk = lambda n: pl.BlockSpec((None, G, n, n), lambda g, c: (c, g, 0, 0))
    return pl.pallas_call(
        body, name="wkv_fwd", grid=(RWKV_HEADS // G, nc),
        in_specs=[seq(0), seq(0), seq(0), seq(2 * per), seq(0), seq(0)],
        out_specs=[seq(0), per_chunk(N), per_chunk(C)],
        out_shape=[jax.ShapeDtypeStruct((S, D_MODEL), F32), jax.ShapeDtypeStruct((nc, RWKV_HEADS, N, N), F32),
                   jax.ShapeDtypeStruct((nc, RWKV_HEADS, C, C), F32)],
        scratch_shapes=[pltpu.VMEM((G, N, N), F32)],
        compiler_params=_cparams(("parallel", "arbitrary"), 8 * C * W * 4 + 2 * G * N * N * 4 + G * C * C * 4),
    )(xs_rk, lw, k, xs_rk, a, b)


def wkv_bwd(xs_rk, lw, k, a, b, states, t_invs, dy):
    S = lw.shape[0]
    C, nc, G, N = WKV_CHUNK, S // WKV_CHUNK, WKV_HEADS_PER_STEP, HEAD_DIM

    def body(r_ref, lw_ref, k_ref, v_ref, a_ref, b_ref, st_ref, ti_ref, dy_ref,
             dr_ref, dlw_ref, dk_ref, dv_ref, da_ref, db_ref, dstate):
        @pl.when(pl.program_id(1) == 0)
        def _():
            dstate[...] = jnp.zeros_like(dstate)

        for base in range(0, G, WKV_HEADS_PER_RUN):
            run = range(base, base + WKV_HEADS_PER_RUN)
            heads = lambda ref: tuple(ref[:, _head_cols(h)] for h in run)
            t_inv = tuple(ti_ref[h] for h in run)
            chunk = lambda *args: _wkv_chunk(*args, t_inv)[:2]
            _, pull = jax.vjp(chunk, tuple(st_ref[h] for h in run), heads(r_ref), heads(lw_ref),
                              heads(k_ref), heads(v_ref), heads(a_ref), heads(b_ref))
            ds0, *grads = pull((heads(dy_ref), tuple(dstate[h] for h in run)))
            for i, h in enumerate(run):
                dstate[h] = ds0[i]
                for ref, grad in zip((dr_ref, dlw_ref, dk_ref, dv_ref, da_ref, db_ref), grads):
                    ref[:, _head_cols(h)] = grad[i]

    W = G * N
    seq = lambda j: pl.BlockSpec((C, W), functools.partial(lambda j, g, c: (nc - 1 - c, j + g), j))
    per = D_MODEL // W
    st = lambda n: pl.BlockSpec((None, G, n, n), lambda g, c: (nc - 1 - c, g, 0, 0))
    return pl.pallas_call(
        body, name="wkv_bwd", grid=(RWKV_HEADS // G, nc),
        in_specs=[seq(0), seq(0), seq(0), seq(2 * per), seq(0), seq(0), st(N), st(C), seq(0)],
        out_specs=[seq(0)] * 6, out_shape=[jax.ShapeDtypeStruct((S, D_MODEL), F32)] * 6,
        scratch_shapes=[pltpu.VMEM((G, N, N), F32)],
        compiler_params=_cparams(("parallel", "arbitrary"), 14 * C * W * 4 + 2 * G * N * N * 4 + G * C * C * 4),
    )(xs_rk, lw, k, xs_rk, a, b, states, t_invs, dy)


def _first_flag(i, per_seq):
    return (lax.rem(i, per_seq) == 0).astype(F32)


def _view(a):
    return a if isinstance(a, tuple) else (a, 0)


def _block_rows(half):
    return pl.ds(half * ATTN_BLK, ATTN_BLK)


def _block_heads(ref, half):
    return tuple(ref[_block_rows(half), _head_cols(h)] for h in range(ATTN_HPG))


def _pair_heads(ref):
    return _block_heads(ref, 0), _block_heads(ref, 1)


def attn_fwd(q, k, v, per_seq, name):
    (q, q_col), (k, k_col), (v, v_col) = _view(q), _view(k), _view(v)
    R, N = q.shape[0], GROUP_W
    n_pairs = R // (2 * ATTN_BLK)

    def body(q_ref, k_ref, kb_ref, v_ref, vb_ref, o_ref, lse_ref):
        pair = pl.program_id(0)
        first = (_first_flag(2 * pair, per_seq), _first_flag(2 * pair + 1, per_seq))
        o, lse = _attn_pair(_pair_heads(q_ref), _pair_heads(k_ref), _block_heads(kb_ref, 0), _pair_heads(v_ref),
                            _block_heads(vb_ref, 0), first)
        for half in range(2):
            for h in range(ATTN_HPG):
                o_ref[_block_rows(half), _head_cols(h)] = o[half][h]
                lse_ref[_block_rows(half), _head_cols(h)] = lse[half][h]

    cur = lambda col: pl.BlockSpec((2 * ATTN_BLK, N), lambda i: (i, col))
    prv = lambda col: pl.BlockSpec((ATTN_BLK, N), lambda i: (jnp.maximum(2 * i - 1, 0), col))
    return pl.pallas_call(
        body, name=name, grid=(n_pairs,), in_specs=[cur(q_col), cur(k_col), prv(k_col), cur(v_col), prv(v_col)],
        out_specs=[cur(0), cur(0)], out_shape=[jax.ShapeDtypeStruct((R, N), F32)] * 2,
        compiler_params=_cparams(("parallel",), 12 * ATTN_BLK * N * 4),
    )(q, k, k, v, v)


def attn_bwd(q, k, v, do, dlse, per_seq, name):
    views = [_view(a) for a in (q, k, v, do, dlse)]
    (q, q_col), (k, k_col), (v, v_col), (do, do_col), (dlse, dl_col) = views
    R, N = q.shape[0], GROUP_W
    n_pairs = R // (2 * ATTN_BLK)

    def body(q_ref, k_ref, kb_ref, v_ref, vb_ref, do_ref, dl_ref, dq_ref, dk_ref, dv_ref, carry_k, carry_v):
        step = pl.program_id(0)
        pair = n_pairs - 1 - step
        first = (_first_flag(2 * pair, per_seq), _first_flag(2 * pair + 1, per_seq))

        @pl.when(step == 0)
        def _():
            carry_k[...] = jnp.zeros_like(carry_k)
            carry_v[...] = jnp.zeros_like(carry_v)

        _, pull = jax.vjp(functools.partial(_attn_pair, first=first), _pair_heads(q_ref), _pair_heads(k_ref),
                          _block_heads(kb_ref, 0), _pair_heads(v_ref), _block_heads(vb_ref, 0))
        dq, dk, dk_before, dv, dv_before = pull((_pair_heads(do_ref), _pair_heads(dl_ref)))
        old_k, old_v = _block_heads(carry_k, 0), _block_heads(carry_v, 0)
        for h in range(ATTN_HPG):
            cols = _head_cols(h)
            for half in range(2):
                dq_ref[_block_rows(half), cols] = dq[half][h]
            dk_ref[_block_rows(0), cols] = dk[0][h]
            dv_ref[_block_rows(0), cols] = dv[0][h]
            dk_ref[_block_rows(1), cols] = dk[1][h] + old_k[h]
            dv_ref[_block_rows(1), cols] = dv[1][h] + old_v[h]
            carry_k[:, cols] = dk_before[h]
            carry_v[:, cols] = dv_before[h]

    cur = lambda col: pl.BlockSpec((2 * ATTN_BLK, N), lambda i: (n_pairs - 1 - i, col))
    prv = lambda col: pl.BlockSpec((ATTN_BLK, N), lambda i: (jnp.maximum(2 * (n_pairs - 1 - i) - 1, 0), col))
    return pl.pallas_call(
        body, name=name, grid=(n_pairs,),
        in_specs=[cur(q_col), cur(k_col), prv(k_col), cur(v_col), prv(v_col), cur(do_col), cur(dl_col)],
        out_specs=[cur(0)] * 3, out_shape=[jax.ShapeDtypeStruct((R, N), F32)] * 3,
        scratch_shapes=[pltpu.VMEM((ATTN_BLK, N), F32)] * 2,
        compiler_params=_cparams(("arbitrary",), 22 * ATTN_BLK * N * 4),
    )(q, k, k, v, v, do, dlse)


def by_residue(u, d):
    if d == 1:
        return u
    return u.reshape(u.shape[0] // d, d, GROUP_W).transpose(1, 0, 2).reshape(u.shape)


def by_position(u, d):
    if d == 1:
        return u
    return u.reshape(d, u.shape[0] // d, GROUP_W).transpose(1, 0, 2).reshape(u.shape)


def group_columns(t, col_block, d):
    if d == 1:
        return (t, col_block)
    return by_residue(t[:, GROUP_W * col_block:GROUP_W * (col_block + 1)], d)


def _ffn_fwd(x, norm, w_in, w_out, tag, token):
    h = rowmap(lambda x_b, g, tok: _rms(x_b, g) + tok[0:1, 0:1], [x], [norm, token], [(D_MODEL, BF16)], tb=512,
               name=tag + "_norm")[0]
    gu, act = ffn_in_act(h, w_in, tag + "_in")
    y = matmul(act, w_out, "nn", tag + "_out", add=x, scale=0.5)
    return y, (x, h, gu, act)


def _ffn_bwd(dy, saved, norm, w_in, w_out, tag, on_weight_grads):
    x, h, gu, act = saved
    no_token = jnp.zeros((8, 128), F32)
    dw_out = matmul(act, dy, "tn", tag + "_dwout", scale=0.5)
    dgu = ffn_dact_dgu(dy, w_out, gu, 0.5, tag + "_dgu")
    dw_in = matmul_cs(h, dgu, "tn", tag + "_dwin", no_token)
    dh = matmul_cs(dgu, w_in, "nt", tag + "_dh", on_weight_grads(dw_in, dw_out))

    def norm_bwd(x_b, dh_b, dy_b, g):
        dx, dg = jax.vjp(_rms, x_b, g)[1](dh_b)
        return dy_b + dx, dg

    dx, dnorm = rowmap(norm_bwd, [x, dh, dy], [norm], [(D_MODEL, F32)], [(1, D_MODEL)], tb=256,
                       name=tag + "_dnorm")
    return dx, dnorm, dw_in, dw_out


def layer_step(x, tgt, W, P, start_token, more_weights, on_mixer_grads, on_ffn1_grads):
    S = x.shape[0]
    x1, ffn1_saved = _ffn_fwd(x, P["ffn1_norm"], W["ffn1_w_in"], W["ffn1_w_out"], "ffn1", start_token)
    W = {**W, **more_weights("mixer", x1)}
    head_of = lambda n: jnp.arange(n)[:, None] // HEAD_DIM == jnp.arange(n // HEAD_DIM)[None, :]
    seg, seg_a = head_of(D_MODEL).astype(BF16), head_of(ATTN_WIDTH).astype(BF16)
    seg_t, seg_a_t = seg.T, seg_a.T
    tile_t = (jnp.arange(HEAD_DIM)[:, None] == jnp.arange(ATTN_WIDTH)[None, :] % HEAD_DIM).astype(BF16)
    qk_params = [P["attn_q_norm"], P["attn_k_norm"], seg_a, seg_a_t, tile_t]
    w_rkv, w_lora = W["w_in"][:, :RKV], W["w_in"][:, RKV:RKV + LORA]
    w_qkv = W["w_in"][:, RKV + LORA:RKV + LORA + 3 * ATTN_WIDTH]
    w_gate = W["w_in"][:, RKV + LORA + 3 * ATTN_WIDTH:]
    mu_rk, mu_lo = P["rwkv_mu"][:, :RKV], P["rwkv_mu"][:, RKV:]
    zeros = lambda n: jnp.zeros((n, D_MODEL), F32)
    w2p = jnp.concatenate([W["rwkv_w2"], zeros(LORA - LORA_W)], axis=0)
    a2p = jnp.concatenate([zeros(LORA_W), W["rwkv_a2"], zeros(LORA_G)], axis=0)
    g2p = jnp.concatenate([zeros(LORA_W + LORA_A), W["rwkv_g2"]], axis=0)
    pre_params = [P["rwkv_w0"], w2p, P["rwkv_a0"], a2p, g2p, P["rwkv_k_k"], P["rwkv_k_a"], seg, seg_t]
    post_params = [P["rwkv_r_k"], P["rwkv_ln_w"], P["rwkv_ln_b"], seg, seg_t]
    col = lambda arr, j: (arr, D_MODEL, j)

    h = rowmap(_rms, [x1], [P["mix_norm"]], [(D_MODEL, BF16)], tb=512, name="mix_norm")[0]
    p_rk = matmul(h, w_rkv, "nn", "proj_rkv")
    p_lo = matmul(h, w_lora, "nn", "proj_lora")
    p_qkv = matmul(h, w_qkv, "nn", "proj_qkv")
    p_gate = matmul(h, w_gate, "nn", "proj_gate")
    xs_rk = token_shift_fwd(p_rk, mu_rk, tb=256, name="shift_rk")
    xs_lo = token_shift_fwd(p_lo, mu_lo, tb=256, name="shift_lora")
    lw, k_mod, a_neg, b_kk, g = rowmap(
        _rwkv_pre, [xs_rk, xs_lo], pre_params, [(D_MODEL, F32)] * 5, tb=256, name="rwkv_pre")
    wkv, states, t_invs = wkv_fwd(xs_rk, lw, k_mod, a_neg, b_kk)
    post_rows = [wkv, col(xs_rk, 0), k_mod, col(xs_rk, 2), g]
    y_a = rowmap(_rwkv_post, post_rows, post_params, [(D_MODEL, BF16)], tb=256, name="rwkv_post")[0]

    qk_rows = [(p_qkv, ATTN_WIDTH, 0), (p_qkv, ATTN_WIDTH, 1)]
    qn, kn = rowmap(_qk_norm, qk_rows, qk_params, [(ATTN_WIDTH, F32)] * 2, tb=256, name="qk_norm")
    dil = [d for _, d in ATTN_PAIRS]
    groups = range(len(dil))
    per_seq = [S // d // ATTN_BLK for d in dil]
    v_first = 2 * ATTN_WIDTH // GROUP_W
    q_s = [group_columns(qn, g, dil[g]) for g in groups]
    k_s = [group_columns(kn, g, dil[g]) for g in groups]
    v_s = [group_columns(p_qkv, v_first + g, dil[g]) for g in groups]
    attn = [attn_fwd(q_s[g], k_s[g], v_s[g], per_seq[g], "attn_fwd_%d" % g) for g in groups]
    o_lse = [by_position(attn[g][j], dil[g]) for j in range(2) for g in groups]
    y_b = rowmap(_group_combine, o_lse, [], [(ATTN_WIDTH, BF16)], tb=512, name="attn_combine")[0]

    W = {**W, **more_weights("out", y_b)}
    pa = matmul(y_a, W["w_proj_rwkv"], "nn", "proj_a")
    pb = matmul(y_b, W["w_proj_attn"], "nn", "proj_b")
    merged = rowmap(_gate_merge, [p_gate, pa, pb], [P["b_gate"]], [(D_MODEL, BF16)], tb=256, name="merge")[0]
    x2 = matmul(merged, W["w_out"], "nn", "mix_out", add=x1)
    x3, ffn2_saved = _ffn_fwd(x2, P["ffn2_norm"], W["ffn2_w_in"], W["ffn2_w_out"], "ffn2",
                              jnp.zeros_like(start_token))

    def loss_head(y_b_, t_b):
        err = y_b_ - t_b
        return err * (1.0 / D_MODEL), (0.5 / D_MODEL) * jnp.sum(err * err, axis=0, keepdims=True)

    dx3, loss_cols = rowmap(loss_head, [x3, tgt], [], [(D_MODEL, F32)], [(1, D_MODEL)], tb=512, name="loss")

    gW, gP = {}, {}
    dx2, gP["ffn2_norm"], gW["ffn2_w_in"], gW["ffn2_w_out"] = _ffn_bwd(
        dx3, ffn2_saved, P["ffn2_norm"], W["ffn2_w_in"], W["ffn2_w_out"], "ffn2",
        lambda dw_in, dw_out: jnp.zeros_like(start_token))

    dmerged = matmul(dx2, W["w_out"], "nt", "d_merged")
    gW["w_out"] = matmul(merged, dx2, "tn", "dw_out")

    def merge_bwd(pg, pa_b, pb_b, dm, bg):
        return jax.vjp(_gate_merge, pg, pa_b, pb_b, bg)[1](dm)

    dp_gate, dpa, dpb, gP["b_gate"] = rowmap(
        merge_bwd, [p_gate, pa, pb, dmerged], [P["b_gate"]],
        [(2 * D_MODEL, BF16), (D_MODEL, BF16), (D_MODEL, BF16)], [(1, 2 * D_MODEL)], tb=256, name="merge_bwd")
    dy_a = matmul(dpa, W["w_proj_rwkv"], "nt", "d_ya")
    gW["w_proj_rwkv"] = matmul(y_a, dpa, "tn", "dw_proj_a")
    dy_b = matmul(dpb, W["w_proj_attn"], "nt", "d_yb")
    gW["w_proj_attn"] = matmul(y_b, dpb, "tn", "dw_proj_b")

    def combine_bwd(*blocks):
        return jax.vjp(_group_combine, *blocks[:-1])[1](blocks[-1])

    d_o_lse = rowmap(combine_bwd, o_lse + [dy_b], [], [(GROUP_W, F32)] * 6, tb=256, name="attn_combine_bwd")
    d_attn = [attn_bwd(q_s[g], k_s[g], v_s[g], by_residue(d_o_lse[g], dil[g]), by_residue(d_o_lse[3 + g], dil[g]),
                       per_seq[g], "attn_bwd_%d" % g) for g in groups]

    def qk_norm_bwd(q_b, k_b, *rest):
        dqkv, (qg, kg, sg, sgt, tl) = rest[:9], rest[9:]
        f = lambda *a: _qk_norm(*a, sg, sgt, tl)
        dqn, dkn = jnp.concatenate(dqkv[0:3], axis=1), jnp.concatenate(dqkv[3:6], axis=1)
        dq, dk, dqg, dkg = jax.vjp(f, q_b, k_b, qg, kg)[1]((dqn, dkn))
        return jnp.concatenate([dq, dk, *dqkv[6:9]], axis=1), dqg, dkg

    dp_qkv, gP["attn_q_norm"], gP["attn_k_norm"] = rowmap(
        qk_norm_bwd, qk_rows + [by_position(d_attn[g][j], dil[g]) for j in range(3) for g in groups], qk_params,
        [(3 * ATTN_WIDTH, BF16)], [(1, HEAD_DIM)] * 2, tb=256, name="qk_norm_bwd")

    def post_bwd(wkv_b, r_b, k_b, v_b, g_b, d_b, r_k, ln_w, ln_b, sg, sgt):
        f = lambda *a: _rwkv_post(*a, sg, sgt)
        return jax.vjp(f, wkv_b, r_b, k_b, v_b, g_b, r_k, ln_w, ln_b)[1](d_b)

    dwkv, dr_p, dk_p, dv_p, dg, gP["rwkv_r_k"], gP["rwkv_ln_w"], gP["rwkv_ln_b"] = rowmap(
        post_bwd, post_rows + [dy_a], post_params, [(D_MODEL, F32)] * 5, [(1, D_MODEL)] * 3, tb=128,
        name="rwkv_post_bwd")
    dr_w, dlw, dk_w, dv_w, da_neg, db_kk = wkv_bwd(xs_rk, lw, k_mod, a_neg, b_kk, states, t_invs, dwkv)

    def pre_bwd(xrk_b, xlo_b, dlw_b, dkw_b, dkp_b, da_b, db_b, dg_b, drp_b, drw_b, dvp_b, dvw_b,
                w0, w2, a0, a2, g2, k_k, k_a, sg, sgt):
        f = lambda *a: _rwkv_pre(*a, sg, sgt)
        pull = jax.vjp(f, xrk_b, xlo_b, w0, w2, a0, a2, g2, k_k, k_a)[1]
        dxrk, dxlo, *dpar = pull((dlw_b, dkw_b + dkp_b, da_b, db_b, dg_b))
        direct = jnp.concatenate([drp_b + drw_b, jnp.zeros_like(drp_b), dvp_b + dvw_b], axis=1)
        return (dxrk + direct, dxlo, *dpar)

    pre_rows = [xs_rk, xs_lo, dlw, dk_w, dk_p, da_neg, db_kk, dg, dr_p, dr_w, dv_p, dv_w]
    dxs_rk, dxs_lo, gP["rwkv_w0"], dw2p, gP["rwkv_a0"], da2p, dg2p, gP["rwkv_k_k"], gP["rwkv_k_a"] = rowmap(
        pre_bwd, pre_rows, pre_params, [(RKV, F32), (LORA, F32)],
        [(1, D_MODEL), (LORA, D_MODEL), (1, D_MODEL), (LORA, D_MODEL), (LORA, D_MODEL), (1, D_MODEL), (1, D_MODEL)],
        tb=128, name="rwkv_pre_bwd")
    gW["rwkv_w2"] = dw2p[:LORA_W]
    gW["rwkv_a2"] = da2p[LORA_W:LORA_W + LORA_A]
    gW["rwkv_g2"] = dg2p[LORA_W + LORA_A:]
    dp_rk, dmu_rk = token_shift_bwd(dxs_rk, p_rk, mu_rk, tb=256, name="shift_rk_bwd")
    dp_lo, dmu_lo = token_shift_bwd(dxs_lo, p_lo, mu_lo, tb=256, name="shift_lora_bwd")
    gP["rwkv_mu"] = jnp.concatenate([dmu_rk, dmu_lo], axis=1)

    dh = matmul(dp_rk, w_rkv, "nt", "dh_rkv")
    dh = matmul(dp_lo, w_lora, "nt", "dh_lora", add=dh)
    dh = matmul(dp_qkv, w_qkv, "nt", "dh_qkv", add=dh)
    dh = matmul(dp_gate, w_gate, "nt", "dh_gate", add=dh)
    gW["w_in"] = jnp.concatenate([
        matmul(h, dp_rk, "tn", "dw_rkv"), matmul(h, dp_lo, "tn", "dw_lora"),
        matmul(h, dp_qkv, "tn", "dw_qkv"), matmul(h, dp_gate, "tn", "dw_gate")], axis=1)

    token = on_mixer_grads(gW)

    def norm_bwd(x_b, dh_b, dy_b, gn, tok):
        dx, dgn = jax.vjp(_rms, x_b, gn)[1](dh_b)
        return dy_b + dx + tok[0:1, 0:1], dgn

    dx1, gP["mix_norm"] = rowmap(norm_bwd, [x1, dh, dx2], [P["mix_norm"], token], [(D_MODEL, F32)],
                                 [(1, D_MODEL)], tb=256, name="mix_norm_bwd")
    dx, gP["ffn1_norm"], gW["ffn1_w_in"], gW["ffn1_w_out"] = _ffn_bwd(
        dx1, ffn1_saved, P["ffn1_norm"], W["ffn1_w_in"], W["ffn1_w_out"], "ffn1", on_ffn1_grads)
    return loss_cols, dx, gW, gP


N_SHARDS = 4
OTHER_CHIPS = N_SHARDS - 1
BIG = (("ffn1_w_in", (D_MODEL, 2 * D_FF), 1), ("ffn1_w_out", (D_FF, D_MODEL), 0),
       ("w_in", (D_MODEL, 7712), 1), ("rwkv_w2", (LORA_W, D_MODEL), 1), ("rwkv_a2", (LORA_A, D_MODEL), 1),
       ("rwkv_g2", (LORA_G, D_MODEL), 1), ("w_proj_rwkv", (D_MODEL, D_MODEL), 0),
       ("w_proj_attn", (ATTN_WIDTH, D_MODEL), 1), ("w_out", (D_MODEL, D_MODEL), 0),
       ("ffn2_w_in", (D_MODEL, 2 * D_FF), 1), ("ffn2_w_out", (D_FF, D_MODEL), 0))
SMALL = (("ffn1_norm", 1024), ("mix_norm", 1024), ("b_gate", 2048), ("rwkv_mu", 3360), ("rwkv_w0", 1024),
         ("rwkv_a0", 1024), ("rwkv_k_k", 1024), ("rwkv_k_a", 1024), ("rwkv_r_k", 1024), ("rwkv_ln_w", 1024),
         ("rwkv_ln_b", 1024), ("attn_q_norm", 64), ("attn_k_norm", 64), ("ffn2_norm", 1024))
WEIGHT_ORDER = ("ffn1_norm", "ffn1_w_in", "ffn1_w_out", "mix_norm", "w_in", "b_gate", "rwkv_mu", "rwkv_w0",
                "rwkv_w2", "rwkv_a0", "rwkv_a2", "rwkv_g2", "rwkv_k_k", "rwkv_k_a", "rwkv_r_k", "rwkv_ln_w",
                "rwkv_ln_b", "attn_q_norm", "attn_k_norm", "w_proj_rwkv", "w_proj_attn", "w_out", "ffn2_norm",
                "ffn2_w_in", "ffn2_w_out")


LORA_PARTS = ("rwkv_w2", "rwkv_a2", "rwkv_g2")
BLOCK_MAJOR = ("ffn1_w_in", "ffn2_w_in")
FIRST_FFN = ("ffn1_w_in", "ffn1_w_out")
MIXER_IN = ("w_in", "lora")
SMALL_USED = D_MODEL + sum(n for _, n in SMALL)
SMALL_W = -(-SMALL_USED // 128) * 128


def _travel():
    out = {}
    for name, shape, axis in BIG:
        if name == LORA_PARTS[0]:
            out["lora"] = ((LORA, D_MODEL), 1)
        elif name not in LORA_PARTS:
            out[name] = (shape, axis)
    return out


def local_blocks(vals):
    out = {n: vals[n] for n in _travel() if n != "lora"}
    out["lora"] = jnp.concatenate([vals[n] for n in LORA_PARTS], axis=0)
    return out


def split_lora(t):
    return {"rwkv_w2": t[:LORA_W], "rwkv_a2": t[LORA_W:LORA_W + LORA_A], "rwkv_g2": t[LORA_W + LORA_A:]}


def blocks_to_full(name, blocks):
    shape, axis = _travel()[name]
    if name in BLOCK_MAJOR:
        return blocks
    if axis == 0:
        return blocks.reshape(shape)
    return blocks.transpose(1, 0, 2).reshape(shape)


def full_to_blocks(name, full):
    shape, axis = _travel()[name]
    if name in BLOCK_MAJOR:
        return full
    if axis == 0:
        return full.reshape(N_SHARDS, shape[0] // N_SHARDS, shape[1])
    return full.reshape(shape[0], N_SHARDS, shape[1] // N_SHARDS).transpose(1, 0, 2)


def pack_small(vals, head):
    parts = [head] + [vals[name].reshape(1, n) for name, n in SMALL]
    parts.append(jnp.zeros((1, SMALL_W - SMALL_USED), F32))
    return jnp.concatenate(parts, axis=1)


def unpack_small(vec, shapes):
    out, off = {}, D_MODEL
    for name, n in SMALL:
        out[name] = vec[:, off:off + n].reshape(shapes[name])
        off += n
    return out


def _place():
    return lax.axis_index("x"), lax.axis_index("y"), lax.axis_index("c")


def _other_chips(x, y):
    return [(1 - x, y), (x, 1 - y), (1 - x, 1 - y)]


def _remote(src, dst, send_sem, recv_sem, device):
    return pltpu.make_async_remote_copy(src_ref=src, dst_ref=dst, send_sem=send_sem, recv_sem=recv_sem,
                                        device_id=device, device_id_type=MESH)


def _half(ref, who):
    hr = ref.shape[-2] // 2
    rows = pl.ds(pl.multiple_of(who * hr, 8), hr)
    return ref.at[rows] if len(ref.shape) == 2 else ref.at[:, rows]


HBM_REF = pl.BlockSpec(memory_space=pl.ANY)
COMM_PARAMS = dict(compiler_params=pltpu.CompilerParams(has_side_effects=True))


def gather_weights(blocks):
    n = len(blocks)

    def body(*refs):
        ins, outs = refs[:n], refs[n:2 * n]
        ici_send, ici_recv, d2d_send, d2d_recv = refs[2 * n:]
        x, y, c = _place()
        me, sibling, chips = 2 * x + y, (x, y, 1 - c), _other_chips(x, y)
        first = [_remote(_half(ins[t], c), _half(outs[t].at[me], c), ici_send.at[k, t], ici_recv.at[k, t],
                         (px, py, c)) for k, (px, py) in enumerate(chips) for t in range(n)]
        for cp in first:
            cp.start()
        passed = []
        for k, (px, py) in enumerate(chips):
            for t in range(n):
                landed = _half(outs[t].at[2 * px + py], c)
                _remote(landed, landed, ici_send.at[k, t], ici_recv.at[k, t], (px, py, c)).wait_recv()
                cp = _remote(landed, landed, d2d_send.at[k, t], d2d_recv.at[k, t], sibling)
                cp.start()
                passed.append(cp)
        for k, (px, py) in enumerate(chips):
            for t in range(n):
                other = _half(outs[t].at[2 * px + py], 1 - c)
                _remote(other, other, d2d_send.at[k, t], d2d_recv.at[k, t], sibling).wait_recv()
        for cp in first + passed:
            cp.wait_send()

    res = pl.pallas_call(
        body, name="gather_weights", in_specs=[HBM_REF] * n, out_specs=[HBM_REF] * n,
        out_shape=[jax.ShapeDtypeStruct((N_SHARDS,) + b.shape, b.dtype) for b in blocks],
        scratch_shapes=[pltpu.SemaphoreType.DMA((3, n))] * 4, **COMM_PARAMS)(*blocks)
    me = 2 * lax.axis_index("x") + lax.axis_index("y")
    return [lax.dynamic_update_slice(g, b[None], (me, 0, 0)) for g, b in zip(res, blocks)]


def _gather_copies(ins, outs, send_sem, recv_sem):
    x, y, c = _place()
    return [_remote(_half(ins[t], c), _half(outs[t].at[2 * x + y], c), send_sem(k, t), recv_sem(k, t), (px, py, c))
            for k, (px, py) in enumerate(_other_chips(x, y)) for t in range(len(ins))]


def split_start(copies, sources, landing_shapes, name):
    n = len(sources)
    n_cp = OTHER_CHIPS * n

    def body(*refs):
        srcs, dsts = refs[:n], refs[n:2 * n]
        sems, token = refs[2 * n:2 * n + 2 * n_cp], refs[-1]
        for cp in copies(srcs, dsts, lambda k, t: sems[k * n + t], lambda k, t: sems[n_cp + k * n + t]):
            cp.start()
        token[...] = jnp.zeros_like(token)

    hbm = lambda a: pltpu.with_memory_space_constraint(a, pltpu.HBM)
    buffers = list(sources) + [lax.empty(shape, s.dtype) for shape, s in zip(landing_shapes, sources)]
    res = pl.pallas_call(
        body, name=name,
        out_shape=(*[pltpu.SemaphoreType.DMA(())] * (2 * n_cp),
                   *[pltpu.HBM(a.shape, a.dtype) for a in buffers], jax.ShapeDtypeStruct((8, 128), F32)),
        in_specs=[SPLIT_HBM] * (2 * n),
        out_specs=(*[SPLIT_SEM] * (2 * n_cp), *[SPLIT_HBM] * (2 * n), pl.BlockSpec(memory_space=pltpu.VMEM)),
        input_output_aliases={t: 2 * n_cp + t for t in range(2 * n)}, **SPLIT_PARAMS,
    )(*[hbm(a) for a in buffers])
    return (copies, n, res[:-1]), res[-1]


def split_wait(handles, after, name):
    copies, n, held = handles
    n_cp = OTHER_CHIPS * n
    sems, thru = held[:2 * n_cp], held[2 * n_cp:]

    def body(*refs):
        srcs, dsts = refs[:n], refs[n:2 * n]
        sem_refs = refs[2 * n:2 * n + 2 * n_cp]
        for cp in copies(srcs, dsts, lambda k, t: sem_refs[k * n + t], lambda k, t: sem_refs[n_cp + k * n + t]):
            cp.wait_send()
            cp.wait_recv()

    res = pl.pallas_call(
        body, name=name, out_shape=tuple(pltpu.HBM(a.shape, a.dtype) for a in thru),
        in_specs=[SPLIT_HBM] * (2 * n) + [SPLIT_SEM] * (2 * n_cp) + [pl.BlockSpec(memory_space=pl.ANY)],
        out_specs=tuple([SPLIT_HBM] * (2 * n)), input_output_aliases={t: t for t in range(2 * n)}, **SPLIT_PARAMS,
    )(*thru, *sems, after)
    return list(res[n:])


def gather_start(blocks, name):
    return split_start(_gather_copies, blocks, [(N_SHARDS,) + b.shape for b in blocks], name)


def pass_halves(gathered, blocks, name):
    n = len(gathered)

    def body(*refs):
        outs = refs[n:2 * n]
        send_sems, recv_sems = refs[2 * n:]
        x, y, c = _place()
        slots = [2 * px + py for px, py in _other_chips(x, y)]
        give = [_remote(_half(outs[t].at[s], c), _half(outs[t].at[s], c), send_sems.at[k, t], recv_sems.at[k, t],
                        (x, y, 1 - c)) for k, s in enumerate(slots) for t in range(n)]
        for cp in give:
            cp.start()
        for k, s in enumerate(slots):
            for t in range(n):
                other = _half(outs[t].at[s], 1 - c)
                _remote(other, other, send_sems.at[k, t], recv_sems.at[k, t], (x, y, 1 - c)).wait_recv()
        for cp in give:
            cp.wait_send()

    res = pl.pallas_call(
        body, name=name, in_specs=[HBM_REF] * n, out_specs=[HBM_REF] * n,
        out_shape=[jax.ShapeDtypeStruct(g.shape, g.dtype) for g in gathered],
        input_output_aliases={t: t for t in range(n)},
        scratch_shapes=[pltpu.SemaphoreType.DMA((3, n))] * 2, **COMM_PARAMS)(*gathered)
    me = 2 * lax.axis_index("x") + lax.axis_index("y")
    return [lax.dynamic_update_slice(g, b[None], (me, 0, 0)) for g, b in zip(res, blocks)]


def swap_halves(grads):
    n = len(grads)

    def body(*refs):
        ins, got = refs[:n], refs[n:2 * n]
        send_sems, recv_sems = refs[2 * n:]
        x, y, c = _place()
        give = [_remote(_half(ins[t], 1 - c), got[t], send_sems.at[t], recv_sems.at[t], (x, y, 1 - c))
                for t in range(n)]
        for cp in give:
            cp.start()
        for cp in give:
            cp.wait_recv()
        for cp in give:
            cp.wait_send()

    return pl.pallas_call(
        body, name="swap_halves", in_specs=[HBM_REF] * n, out_specs=[HBM_REF] * n,
        out_shape=[jax.ShapeDtypeStruct((g.shape[0], g.shape[1] // 2, g.shape[2]), g.dtype) for g in grads],
        scratch_shapes=[pltpu.SemaphoreType.DMA((n,))] * 2, **COMM_PARAMS)(*grads)


def join_halves(blocks):
    n = len(blocks)

    def body(*refs):
        outs = refs[n:2 * n]
        send_sems, recv_sems = refs[2 * n:]
        x, y, c = _place()
        give = [_remote(_half(outs[t], c), _half(outs[t], c), send_sems.at[t], recv_sems.at[t], (x, y, 1 - c))
                for t in range(n)]
        for cp in give:
            cp.start()
        for t in range(n):
            arriving = _half(outs[t], 1 - c)
            _remote(arriving, arriving, send_sems.at[t], recv_sems.at[t], (x, y, 1 - c)).wait_recv()
        for cp in give:
            cp.wait_send()

    return pl.pallas_call(
        body, name="join_halves", in_specs=[HBM_REF] * n, out_specs=[HBM_REF] * n,
        out_shape=[jax.ShapeDtypeStruct(b.shape, b.dtype) for b in blocks],
        input_output_aliases={t: t for t in range(n)},
        scratch_shapes=[pltpu.SemaphoreType.DMA((n,))] * 2, **COMM_PARAMS)(*blocks)


SPLIT_HBM = pl.BlockSpec(memory_space=pltpu.HBM)
SPLIT_SEM = pl.BlockSpec(memory_space=pltpu.SEMAPHORE)
SPLIT_PARAMS = dict(compiler_params=pltpu.CompilerParams(has_side_effects=pltpu.SideEffectType.DATAFLOW_SIDE_EFFECTING))


def _scatter_copies(parts, landed, send_sem, recv_sem):
    x, y, c = _place()
    return [_remote(parts[t].at[2 * px + py], landed[t].at[k], send_sem(k, t), recv_sem(k, t), (px, py, c))
            for k, (px, py) in enumerate(_other_chips(x, y)) for t in range(len(parts))]


def scatter_start(partials, name):
    return split_start(_scatter_copies, partials, [(OTHER_CHIPS,) + p.shape[1:] for p in partials], name)


def chip_sums(grads, got):
    names = list(grads)
    partials = []
    for name, theirs in zip(names, got):
        n_slot, hr, width = theirs.shape
        tb = _row_block(hr, width, 6)
        per_half = hr // tb
        mine = lambda i, s, per_half=per_half: (i // per_half) * 2 * per_half + s[0] * per_half + i % per_half
        p = placed_map(
            jnp.add,
            [(grads[name].reshape(2 * n_slot * hr, width), mine), (theirs.reshape(n_slot * hr, width), lambda i, s: i)],
            (n_slot * hr, width, BF16, lambda i, s: i), n_blocks=n_slot * per_half, tb=tb, name="chip_sum_" + name)
        partials.append(p.reshape(theirs.shape))
    return partials


def owner_sums(grads, got, landed):
    names = list(grads)
    blocks = []
    for name, theirs, arrived in zip(names, got, landed):
        n_slot, hr, width = theirs.shape
        tb = _row_block(hr, width, 6)
        per_half = hr // tb
        views = [(grads[name].reshape(2 * n_slot * hr, width),
                  lambda i, s, per_half=per_half: s[1] * 2 * per_half + s[0] * per_half + i),
                 (theirs.reshape(n_slot * hr, width), lambda i, s, per_half=per_half: s[1] * per_half + i)]
        views += [(arrived.reshape(3 * hr, width), functools.partial(lambda k, per_half, i, s: k * per_half + i,
                                                                     k, per_half)) for k in range(3)]
        f = lambda a, b, l0, l1, l2: (((a + b) + l0.astype(F32)) + l1.astype(F32)) + l2.astype(F32)
        blocks.append(placed_map(
            f, views,(2 * hr, width, F32, lambda i, s, per_half=per_half: s[0] * per_half + i),
            n_blocks=per_half, tb=tb, name="owner_sum_" + name))
    return dict(zip(names, join_halves(blocks)))


def adamw_block(name, w, g, m, v):
    rows, width = w.shape
    return rowmap(_adamw, [w, g, m, v], [], [(width, F32)] * 3, tb=_row_block(rows, width, 7),
                  name="adamw_" + name)


def reduce_small(vec, w, m, v):
    n_dev = 8

    def body(vec_ref, w_ref, m_ref, v_ref, loss_ref, g_ref, d_ref, m2_ref, v2_ref, slots, send_sems, recv_sems):
        x, y, c = _place()
        me = 4 * x + 2 * y + c
        slots[me] = vec_ref[...]
        flips = [(fx, fy, fc) for fx in (0, 1) for fy in (0, 1) for fc in (0, 1)][1:]
        peers = [(1 - x if fx else x, 1 - y if fy else y, 1 - c if fc else c) for fx, fy, fc in flips]
        sends = [pltpu.make_async_remote_copy(
            src_ref=vec_ref, dst_ref=slots.at[me], send_sem=send_sems.at[j], recv_sem=recv_sems.at[j],
            device_id=peer, device_id_type=MESH) for j, peer in enumerate(peers)]
        for cp in sends:
            cp.start()
        for j, (px, py, pc) in enumerate(peers):
            pltpu.make_async_remote_copy(
                src_ref=vec_ref, dst_ref=slots.at[4 * px + 2 * py + pc], send_sem=send_sems.at[j],
                recv_sem=recv_sems.at[j], device_id=(px, py, pc), device_id_type=MESH).wait_recv()
        for cp in sends:
            cp.wait_send()
        g = slots[0]
        for d in range(1, n_dev):
            g = g + slots[d]
        loss_ref[...] = jnp.sum(g[:, :D_MODEL], axis=1, keepdims=True)
        delta, m2, v2 = _adamw(w_ref[...], g, m_ref[...], v_ref[...])
        g_ref[...], d_ref[...], m2_ref[...], v2_ref[...] = g, delta, m2, v2

    vm = pl.BlockSpec(memory_space=pltpu.VMEM)
    vec_t = jax.ShapeDtypeStruct(vec.shape, F32)
    return pl.pallas_call(
        body, name="reduce_small", in_specs=[vm] * 4, out_specs=[vm] * 5,
        out_shape=[jax.ShapeDtypeStruct((1, 1), F32)] + [vec_t] * 4,
        scratch_shapes=[pltpu.VMEM((n_dev,) + vec.shape, F32), pltpu.SemaphoreType.DMA((n_dev - 1,)),
                        pltpu.SemaphoreType.DMA((n_dev - 1,))],
        compiler_params=pltpu.CompilerParams(has_side_effects=True),
    )(vec, w, m, v)


def kernel(x, ffn1_norm, ffn1_w_in, ffn1_w_out, mix_norm, w_in, b_gate, rwkv_mu, rwkv_w0, rwkv_w2, rwkv_a0, rwkv_a2, rwkv_g2, rwkv_k_k, rwkv_k_a, rwkv_r_k, rwkv_ln_w, rwkv_ln_b, attn_q_norm, attn_k_norm, w_proj_rwkv, w_proj_attn, w_out, ffn2_norm, ffn2_w_in, ffn2_w_out, loss_target, m_ffn1_norm, m_ffn1_w_in, m_ffn1_w_out, m_mix_norm, m_w_in, m_b_gate, m_rwkv_mu, m_rwkv_w0, m_rwkv_w2, m_rwkv_a0, m_rwkv_a2, m_rwkv_g2, m_rwkv_k_k, m_rwkv_k_a, m_rwkv_r_k, m_rwkv_ln_w, m_rwkv_ln_b, m_attn_q_norm, m_attn_k_norm, m_w_proj_rwkv, m_w_proj_attn, m_w_out, m_ffn2_norm, m_ffn2_w_in, m_ffn2_w_out, v_ffn1_norm, v_ffn1_w_in, v_ffn1_w_out, v_mix_norm, v_w_in, v_b_gate, v_rwkv_mu, v_rwkv_w0, v_rwkv_w2, v_rwkv_a0, v_rwkv_a2, v_rwkv_g2, v_rwkv_k_k, v_rwkv_k_a, v_rwkv_r_k, v_rwkv_ln_w, v_rwkv_ln_b, v_attn_q_norm, v_attn_k_norm, v_w_proj_rwkv, v_w_proj_attn, v_w_out, v_ffn2_norm, v_ffn2_w_in, v_ffn2_w_out):
    given = dict(locals())
    weights = {n: given[n] for n in WEIGHT_ORDER}
    mom_m = {n: given["m_" + n] for n in WEIGHT_ORDER}
    mom_v = {n: given["v_" + n] for n in WEIGHT_ORDER}
    big = [name for name, _, _ in BIG]
    shapes = {n: weights[n].shape for n in WEIGHT_ORDER}
    blocks_of = lambda d: local_blocks({n: d[n][0] for n in big})
    w_blk, m_blk, v_blk = blocks_of(weights), blocks_of(mom_m), blocks_of(mom_v)
    names = list(w_blk)

    early = [n for n in names if n not in FIRST_FFN]
    bf16_block = lambda n: w_blk[n].astype(BF16)
    W = {n: blocks_to_full(n, g) for n, g in zip(FIRST_FFN, gather_weights([bf16_block(n) for n in FIRST_FFN]))}
    stages = {"mixer": [n for n in early if n in MIXER_IN], "out": [n for n in early if n not in MIXER_IN]}
    stage_blocks = {s: [bf16_block(n) for n in stages[s]] for s in stages}
    started = {s: gather_start(stage_blocks[s], "gather_start_" + s) for s in ("mixer", "out")}
    start_token = started["mixer"][1] + started["out"][1]

    def more_weights(stage, after):
        landed = split_wait(started[stage][0], after, "gather_wait_" + stage)
        got = pass_halves(landed, stage_blocks[stage], "pass_halves_" + stage)
        more = {n: blocks_to_full(n, g) for n, g in zip(stages[stage], got)}
        if "lora" in more:
            more.update(split_lora(more.pop("lora")))
        return more

    P = {n: weights[n].reshape(1, -1) for n, _ in SMALL}

    sent = {}

    def send_early(gw):
        lora = jnp.concatenate([gw[n] for n in LORA_PARTS], axis=0)
        sent["grads"] = {n: full_to_blocks(n, lora if n == "lora" else gw[n]) for n in early}
        sent["got"] = swap_halves(list(sent["grads"].values()))
        sent["handles"], token = scatter_start(chip_sums(sent["grads"], sent["got"]), "scatter_start")
        return token

    def send_late(dw_in, dw_out):
        sent["late"] = {n: full_to_blocks(n, g) for n, g in zip(FIRST_FFN, (dw_in, dw_out))}
        sent["late_got"] = swap_halves(list(sent["late"].values()))
        sent["late_handles"], token = scatter_start(chip_sums(sent["late"], sent["late_got"]), "scatter_start_ffn1")
        return token

    loss_cols, dx, gW, gP = layer_step(x[0], loss_target[0], W, P, start_token, more_weights, send_early, send_late)
    landed = split_wait(sent["handles"], gP["ffn1_norm"], "scatter_wait")
    out_g, out_d, out_m, out_v = {}, {}, {}, {}

    def apply(g_blk):
        for n in g_blk:
            res = (g_blk[n], *adamw_block(n, w_blk[n], g_blk[n], m_blk[n], v_blk[n]))
            for dst, t in zip((out_g, out_d, out_m, out_v), res):
                for part, val in (split_lora(t) if n == "lora" else {n: t}).items():
                    dst[part] = val.reshape(shapes[part])

    apply(owner_sums(sent["grads"], sent["got"], landed))
    late_landed = split_wait(sent["late_handles"], list(out_d.values())[-1], "scatter_wait_ffn1")
    apply(owner_sums(sent["late"], sent["late_got"], late_landed))

    zero_head = jnp.zeros((1, D_MODEL), F32)
    vec = pack_small(gP, loss_cols)
    loss, g_s, d_s, m_s, v_s = reduce_small(
        vec, pack_small({n: weights[n] for n, _ in SMALL}, zero_head),
        pack_small({n: mom_m[n] for n, _ in SMALL}, zero_head),
        pack_small({n: mom_v[n] for n, _ in SMALL}, zero_head))
    for dst, src in ((out_g, g_s), (out_d, d_s), (out_m, m_s), (out_v, v_s)):
        dst.update(unpack_small(src, shapes))

    return (loss[0, 0], dx[None], *[out_g[n] for n in WEIGHT_ORDER], *[out_d[n] for n in WEIGHT_ORDER],
            *[out_m[n] for n in WEIGHT_ORDER], *[out_v[n] for n in WEIGHT_ORDER])
```

```python
import functools

import jax
import jax.numpy as jnp
from jax import lax
from jax.experimental import pallas as pl
from jax.experimental.pallas import tpu as pltpu

F32 = jnp.float32
BF16 = jnp.bfloat16
MESH = pl.DeviceIdType.MESH

D_MODEL = 1024
HEAD_DIM = 64
RWKV_HEADS = 16
LORA_W, LORA_A, LORA_G = 64, 64, 160
LORA = LORA_W + LORA_A + LORA_G
RKV = 3 * D_MODEL
ATTN_PAIRS = ((128, 1), (512, 4), (2048, 16))
ATTN_BLK = 128
ATTN_HPG = 4
ATTN_WIDTH = 768
GROUP_W = ATTN_HPG * HEAD_DIM
D_FF = 2816
GN_EPS = 64e-5
RMS_EPS = 1e-6
NEG_INF = -1e30
WKV_CHUNK = 128
WKV_HEADS_PER_STEP = 16
WKV_HEADS_PER_RUN = 8

ADAM_LR, ADAM_B1, ADAM_B2, ADAM_EPS, ADAM_WD, ADAM_STEP = 0.001, 0.9, 0.999, 1e-08, 0.01, 10

V7X_VMEM_BYTES = 64 << 20
VMEM_TEMP_ALLOWANCE = 20 << 20
VMEM_LEFT_FREE = 6 << 20


def _cparams(sem, block_bytes):
    limit = min(2 * block_bytes + VMEM_TEMP_ALLOWANCE, V7X_VMEM_BYTES - VMEM_LEFT_FREE)
    return pltpu.CompilerParams(dimension_semantics=sem, vmem_limit_bytes=int(limit))


def _nbytes(shape, dtype):
    n = 1
    for s in shape:
        n *= s
    return n * jnp.dtype(dtype).itemsize


def _split_bf16(a):
    hi = a.astype(BF16)
    return hi, (a - hi.astype(F32)).astype(BF16)


def _make_dots():
    def raw(a, b, ca, cb):
        return lax.dot_general(a.astype(BF16), b.astype(BF16), (((ca,), (cb,)), ((), ())),
                               preferred_element_type=F32)

    @jax.custom_vjp
    def nn(a, b):
        return raw(a, b, 1, 0)

    @jax.custom_vjp
    def nt(a, b):
        return raw(a, b, 1, 1)

    @jax.custom_vjp
    def tn(a, b):
        return raw(a, b, 0, 0)

    nn.defvjp(lambda a, b: (raw(a, b, 1, 0), (a, b)),
              lambda res, g: (raw(g, res[1], 1, 1), raw(res[0], g, 0, 0)))
    nt.defvjp(lambda a, b: (raw(a, b, 1, 1), (a, b)),
              lambda res, g: (raw(g, res[1], 1, 0), raw(g, res[0], 0, 0)))
    tn.defvjp(lambda a, b: (raw(a, b, 0, 0), (a, b)),
              lambda res, g: (raw(res[1], g, 1, 1), raw(res[0], g, 1, 0)))
    return nn, nt, tn


def _exact_rhs_dot(x, ones, cx, co):
    hi, lo = _split_bf16(x)
    dims = (((cx,), (co,)), ((), ()))
    return (lax.dot_general(hi, ones, dims, preferred_element_type=F32)
            + lax.dot_general(lo, ones, dims, preferred_element_type=F32))


@jax.custom_vjp
def SEG(x, ones):
    return _exact_rhs_dot(x, ones, 1, 0)


SEG.defvjp(lambda x, ones: (_exact_rhs_dot(x, ones, 1, 0), ones),
           lambda ones, g: (_exact_rhs_dot(g, ones, 1, 1), jnp.zeros_like(ones)))

NN, NT, TN = _make_dots()


MM_TILE_M, MM_TILE_N, MM_TILE_K = 1408, 1408, 1536


def _pick(n, cap):
    best = None
    for t in range(128, min(n, cap) + 1, 128):
        if n % t == 0:
            best = t
    return best or n


def matmul(a, b, mode, name, *, add=None, scale=1.0):
    if mode == "nn":
        (M, K), (K2, N) = a.shape, b.shape
    elif mode == "nt":
        (M, K), (N, K2) = a.shape, b.shape
    else:
        (K, M), (K2, N) = a.shape, b.shape
    assert K == K2, (name, a.shape, b.shape)
    tm, tn, tk = _pick(M, MM_TILE_M), _pick(N, MM_TILE_N), _pick(K, MM_TILE_K)
    nk = K // tk
    ca, cb = {"nn": (1, 0), "nt": (1, 1), "tn": (0, 0)}[mode]

    def body(*refs):
        if add is None:
            a_ref, b_ref, o_ref, acc_ref = refs
        else:
            a_ref, b_ref, add_ref, o_ref, acc_ref = refs
        k = pl.program_id(2)

        @pl.when(k == 0)
        def _():
            acc_ref[...] = jnp.zeros_like(acc_ref)

        acc_ref[...] += lax.dot_general(a_ref[...].astype(BF16), b_ref[...].astype(BF16),
                                        (((ca,), (cb,)), ((), ())), preferred_element_type=F32)

        @pl.when(k == nk - 1)
        def _():
            r = acc_ref[...] * scale
            if add is not None:
                r = add_ref[...] + r
            o_ref[...] = r.astype(o_ref.dtype)

    a_spec = (pl.BlockSpec((tk, tm), lambda i, j, k: (k, i)) if mode == "tn"
              else pl.BlockSpec((tm, tk), lambda i, j, k: (i, k)))
    b_spec = (pl.BlockSpec((tn, tk), lambda i, j, k: (j, k)) if mode == "nt"
              else pl.BlockSpec((tk, tn), lambda i, j, k: (k, j)))
    in_specs, args = [a_spec, b_spec], [a, b]
    blk = tm * tk * a.dtype.itemsize + tk * tn * b.dtype.itemsize + tm * tn * 8
    if add is not None:
        in_specs.append(pl.BlockSpec((tm, tn), lambda i, j, k: (i, j)))
        args.append(add)
        blk += tm * tn * 4
    return pl.pallas_call(
        body, name=name, grid=(M // tm, N // tn, nk),
        in_specs=in_specs, out_specs=pl.BlockSpec((tm, tn), lambda i, j, k: (i, j)),
        out_shape=jax.ShapeDtypeStruct((M, N), F32),
        scratch_shapes=[pltpu.VMEM((tm, tn), F32)],
        compiler_params=_cparams(("parallel", "parallel", "arbitrary"), blk),
    )(*args)


def matmul_cs(a, w, mode, name, token):
    n_blk = N_SHARDS
    if mode == "tn":
        (K, R), Cs = a.shape, w.shape[2] // 2
        tm, tk = _pick(R, MM_TILE_M), _pick(K, 1024)
        grid = (R // tm, n_blk, K // tk)
        a_spec = pl.BlockSpec((tk, tm), lambda i, j, k: (k, i))
        w_spec = pl.BlockSpec((None, tk, Cs), lambda i, j, k: (j // 2, k, j % 2))
        o_spec = pl.BlockSpec((None, tm, Cs), lambda i, j, k: (j, i, 0))
        out_shape, acc_shape, dims = (n_blk, R, Cs), (tm, Cs), (0, 0)
        blk = tk * tm * a.dtype.itemsize + tk * Cs * w.dtype.itemsize + tm * Cs * 8
    else:
        M, (_, R, Cs) = a.shape[1], w.shape
        tm, tn = _pick(M, MM_TILE_M), _pick(R, MM_TILE_N)
        grid = (M // tm, R // tn, n_blk)
        a_spec = pl.BlockSpec((None, tm, Cs), lambda i, j, k: (k // 2, i, k % 2))
        w_spec = pl.BlockSpec((None, tn, Cs), lambda i, j, k: (k, j, 0))
        o_spec = pl.BlockSpec((tm, tn), lambda i, j, k: (i, j))
        out_shape, acc_shape, dims = (M, R), (tm, tn), (1, 1)
        blk = tm * Cs * a.dtype.itemsize + tn * Cs * w.dtype.itemsize + tm * tn * 8
    nk = grid[2]

    def body(a_ref, w_ref, tok_ref, o_ref, acc_ref):
        k = pl.program_id(2)

        @pl.when(k == 0)
        def _():
            acc_ref[...] = jnp.zeros_like(acc_ref)

        acc_ref[...] += lax.dot_general(a_ref[...].astype(BF16), w_ref[...].astype(BF16),
                                        (((dims[0],), (dims[1],)), ((), ())), preferred_element_type=F32)

        @pl.when(k == nk - 1)
        def _():
            o_ref[...] = acc_ref[...] + tok_ref[0:1, 0:1]

    return pl.pallas_call(
        body, name=name, grid=grid, in_specs=[a_spec, w_spec, pl.BlockSpec(token.shape, lambda i, j, k: (0, 0))],
        out_specs=o_spec, out_shape=jax.ShapeDtypeStruct(out_shape, F32), scratch_shapes=[pltpu.VMEM(acc_shape, F32)],
        compiler_params=_cparams(("parallel", "parallel", "arbitrary"), blk),
    )(a, w, token)


FFN_TILE_M = 512


def _swiglu(gate, up):
    return gate * jax.nn.sigmoid(gate) * up


def ffn_in_act(h, w, name):
    (M, R), Cs, half = h.shape, w.shape[2], N_SHARDS // 2
    tm, tk = _pick(M, FFN_TILE_M), _pick(R, 1024)
    nk = R // tk

    def body(h_ref, wg_ref, wu_ref, gu_ref, act_ref, acc_ref):
        k = pl.program_id(2)

        @pl.when(k == 0)
        def _():
            acc_ref[...] = jnp.zeros_like(acc_ref)

        hb = h_ref[...].astype(BF16)
        for part, w_ref in enumerate((wg_ref, wu_ref)):
            acc_ref[part] += jnp.dot(hb, w_ref[...].astype(BF16), preferred_element_type=F32)

        @pl.when(k == nk - 1)
        def _():
            gu_ref[...] = acc_ref[...]
            act_ref[...] = _swiglu(acc_ref[0], acc_ref[1]).astype(act_ref.dtype)

    w_spec = lambda off: pl.BlockSpec((None, tk, Cs), functools.partial(lambda off, j, i, k: (j + off, k, 0), off))
    blk = tm * tk * h.dtype.itemsize + 2 * tk * Cs * w.dtype.itemsize + tm * Cs * (16 + 2)
    return pl.pallas_call(
        body, name=name, grid=(half, M // tm, nk),
        in_specs=[pl.BlockSpec((tm, tk), lambda j, i, k: (i, k)), w_spec(0), w_spec(half)],
        out_specs=[pl.BlockSpec((2, tm, Cs), lambda j, i, k: (0, i, j)), pl.BlockSpec((tm, Cs), lambda j, i, k: (i, j))],
        out_shape=[jax.ShapeDtypeStruct((2, M, half * Cs), F32), jax.ShapeDtypeStruct((M, half * Cs), BF16)],
        scratch_shapes=[pltpu.VMEM((2, tm, Cs), F32)],
        compiler_params=_cparams(("parallel", "parallel", "arbitrary"), blk),
    )(h, w, w)


def ffn_dact_dgu(dy, w_out, gu, scale, name):
    (M, D), F = dy.shape, w_out.shape[0]
    tm, tn = _pick(M, FFN_TILE_M), F // 2

    def body(dy_ref, w_ref, gu_ref, dgu_ref):
        dact = scale * lax.dot_general(dy_ref[...].astype(BF16), w_ref[...].astype(BF16),
                                       (((1,), (1,)), ((), ())), preferred_element_type=F32)
        dgate, dup = jax.vjp(_swiglu, gu_ref[0], gu_ref[1])[1](dact)
        dgu_ref[0] = dgate.astype(dgu_ref.dtype)
        dgu_ref[1] = dup.astype(dgu_ref.dtype)

    pair = pl.BlockSpec((2, tm, tn), lambda j, i: (0, i, j))
    blk = tm * D * dy.dtype.itemsize + tn * D * w_out.dtype.itemsize + 2 * tm * tn * (4 + 2)
    return pl.pallas_call(
        body, name=name, grid=(F // tn, M // tm),
        in_specs=[pl.BlockSpec((tm, D), lambda j, i: (i, 0)), pl.BlockSpec((tn, D), lambda j, i: (j, 0)), pair],
        out_specs=pair, out_shape=jax.ShapeDtypeStruct((2, M, F), BF16),
        compiler_params=_cparams(("parallel", "parallel"), blk),
    )(dy, w_out, gu)


def _row_block(n, width, n_arrays):
    cap = (V7X_VMEM_BYTES // 4) // (2 * 4 * width * n_arrays)
    best = None
    for t in range(16, min(n, cap) + 1, 16):
        if n % t == 0:
            best = t
    return best or n


def placed_map(f, ins, out, *, n_blocks, tb, name):
    def body(*refs):
        refs[-1][...] = f(*[r[...] for r in refs[:-1]]).astype(refs[-1].dtype)

    def spec(fn):
        def index(i):
            x, y, c = _place()
            return fn(i, (c, 2 * x + y)), 0
        return pl.BlockSpec((tb, width), index)

    o_rows, width, o_dtype, o_fn = out
    blk = (sum(a.dtype.itemsize for a, _ in ins) + jnp.dtype(o_dtype).itemsize) * tb * width
    return pl.pallas_call(
        body, name=name, grid=(n_blocks,), in_specs=[spec(fn) for _, fn in ins], out_specs=spec(o_fn),
        out_shape=jax.ShapeDtypeStruct((o_rows, width), o_dtype),
        compiler_params=_cparams(("parallel",), blk),
    )(*[a for a, _ in ins])


def rowmap(f, rows, params, outs, accs=(), *, tb, name):
    rows = [r if isinstance(r, tuple) else (r, r.shape[1], 0) for r in rows]
    S = rows[0][0].shape[0]
    assert S % tb == 0, (name, S, tb)
    n_in, n_out = len(rows) + len(params), len(outs)

    def body(*refs):
        res = f(*[r[...] for r in refs[:n_in]])
        res = res if isinstance(res, (tuple, list)) else (res,)
        o_refs, a_refs = refs[n_in:n_in + n_out], refs[n_in + n_out:]
        for ref, val in zip(o_refs, res[:n_out]):
            ref[...] = val.astype(ref.dtype)
        if a_refs:
            @pl.when(pl.program_id(0) == 0)
            def _():
                for ref in a_refs:
                    ref[...] = jnp.zeros_like(ref)

            for ref, val in zip(a_refs, res[n_out:]):
                ref[...] += val.astype(F32)

    in_specs = [pl.BlockSpec((tb, w), functools.partial(lambda cb, i: (i, cb), cb)) for _, w, cb in rows]
    in_specs += [pl.BlockSpec(p.shape, lambda i: (0, 0)) for p in params]
    out_specs = [pl.BlockSpec((tb, w), lambda i: (i, 0)) for w, _ in outs]
    out_specs += [pl.BlockSpec(tuple(s), lambda i: (0, 0)) for s in accs]
    out_shape = [jax.ShapeDtypeStruct((S, w), dt) for w, dt in outs]
    out_shape += [jax.ShapeDtypeStruct(tuple(s), F32) for s in accs]
    blk = sum(tb * w * a.dtype.itemsize for a, w, _ in rows) + sum(_nbytes(p.shape, p.dtype) for p in params)
    blk += sum(_nbytes((tb, w), dt) for w, dt in outs) + sum(_nbytes(s, F32) for s in accs)
    res = pl.pallas_call(
        body, name=name, grid=(S // tb,), in_specs=in_specs, out_specs=out_specs, out_shape=out_shape,
        compiler_params=_cparams(("arbitrary",) if accs else ("parallel",), blk),
    )(*[r[0] for r in rows], *[pltpu.with_memory_space_constraint(p, pltpu.HBM) for p in params])
    return res


def _rms(x, g):
    return x * lax.rsqrt(jnp.mean(x * x, axis=-1, keepdims=True) + RMS_EPS) * g


def _softplus(z):
    return jnp.maximum(z, 0.0) + jnp.log(1.0 + jnp.exp(-jnp.abs(z)))


def _rwkv_pre(xrk, xlo, w0, w2p, a0, a2p, g2p, k_k, k_a, seg, seg_t):
    k = xrk[:, D_MODEL:2 * D_MODEL]
    w = -_softplus(-(w0 + NN(jnp.tanh(xlo), w2p))) - 0.5
    log_decay = -jnp.exp(w)
    a = jax.nn.sigmoid(a0 + NN(xlo, a2p))
    g = NN(jax.nn.sigmoid(xlo), g2p)
    kk = k * k_k
    norm = jnp.maximum(jnp.sqrt(SEG(kk * kk, seg)), 1e-12)
    kk = kk * SEG(1.0 / norm, seg_t)
    k_mod = k * (1.0 + (a - 1.0) * k_a)
    return log_decay, k_mod, -kk, kk * a, g


def _rwkv_post(wkv, r, k_mod, v, g, r_k, ln_w, ln_b, seg, seg_t):
    inv_n = 1.0 / HEAD_DIM
    mean = SEG(wkv, seg) * inv_n
    cen = wkv - SEG(mean, seg_t)
    var = SEG(cen * cen, seg) * inv_n
    y = cen * SEG(lax.rsqrt(var + GN_EPS), seg_t) * ln_w + ln_b
    bonus = SEG(SEG(r * k_mod * r_k, seg), seg_t) * v
    return (y + bonus) * g


def _qk_norm(q, k, q_gain, k_gain, seg, seg_t, tile_t):
    def norm(x, gain):
        mean_sq = SEG(x * x, seg) * (1.0 / HEAD_DIM)
        return x * SEG(lax.rsqrt(mean_sq + RMS_EPS), seg_t) * SEG(gain, tile_t)

    return norm(q, q_gain) * (HEAD_DIM ** -0.5), norm(k, k_gain)


def _gate_merge(pgate, pa, pb, b_gate):
    sg = jax.nn.sigmoid(pgate + b_gate)
    return sg[:, :D_MODEL] * pa + sg[:, D_MODEL:] * pb


def _group_combine(o0, o1, o2, l0, l1, l2):
    m = jnp.maximum(jnp.maximum(l0, l1), l2)
    es = [jnp.exp(l - m) for l in (l0, l1, l2)]
    den = es[0] + es[1] + es[2]
    return jnp.concatenate([o * (e / den) for o, e in zip((o0, o1, o2), es)], axis=1)


def _each(f, *xs):
    return tuple(f(*args) for args in zip(*xs))


def _attn_block(q, kc, kp, vc, vp, first):
    qi = lax.broadcasted_iota(jnp.int32, (ATTN_BLK, ATTN_BLK), 0)
    kj = lax.broadcasted_iota(jnp.int32, (ATTN_BLK, ATTN_BLK), 1)
    own = kj <= qi
    s_c = _each(lambda a, b: jnp.where(own, NT(a, b), NEG_INF), q, kc)
    s_p = _each(lambda a, b, f: jnp.where((kj >= qi) & (f < 0.5), NT(a, b), NEG_INF), q, kp, first)
    row_max = lambda s: jnp.max(s, axis=-1, keepdims=True)
    row_sum = lambda s: jnp.sum(s, axis=-1, keepdims=True)
    m = _each(lambda c_, p_: jnp.maximum(row_max(c_), row_max(p_)), s_c, s_p)
    e_c, e_p = _each(lambda s, m_: jnp.exp(s - m_), s_c, m), _each(lambda s, m_: jnp.exp(s - m_), s_p, m)
    den = _each(lambda c_, p_: row_sum(c_) + row_sum(p_), e_c, e_p)
    inv = _each(lambda d_: 1.0 / d_, den)
    o = _each(lambda ec, ep, i_, vc_, vp_: (NN(ec, vc_) + NN(ep, vp_)) * i_, e_c, e_p, inv, vc, vp)
    lse = _each(lambda m_, d_: jnp.broadcast_to(m_ + jnp.log(d_), (ATTN_BLK, HEAD_DIM)), m, den)
    return o, lse


def _attn_pair(q, k, k_before, v, v_before, first):
    n = len(q[0])
    o, lse = _attn_block(q[0] + q[1], k[0] + k[1], k_before + k[0], v[0] + v[1], v_before + v[0],
                         (first[0],) * n + (first[1],) * n)
    return (o[:n], o[n:]), (lse[:n], lse[n:])


TRI_SEED = 8


def _tri_inverse(n):
    c = n[0].shape[0]
    row = lax.broadcasted_iota(jnp.int32, (c, c), 0)
    col = lax.broadcasted_iota(jnp.int32, (c, c), 1)
    same_block = lambda size: (row >> (size.bit_length() - 1)) == (col >> (size.bit_length() - 1))
    seed = same_block(TRI_SEED)
    p = _each(lambda m: jnp.where(seed, m, 0.0), n)
    t, span = _each(lambda m: (row == col).astype(F32) + m, p), 2
    while span < TRI_SEED:
        p = _each(NN, p, p)
        t = _each(lambda t_, p_: t_ + NN(t_, p_), t, p)
        span *= 2
    size = TRI_SEED
    while size < c:
        joins = same_block(2 * size) & jnp.logical_not(same_block(size))
        t = _each(lambda t_, m: t_ + NN(NN(t_, jnp.where(joins, m, 0.0)), t_), t, n)
        size *= 2
    return t


@jax.custom_vjp
def _tri_solve(n, rhs, t):
    return _each(NN, t, rhs)


def _tri_solve_fwd(n, rhs, t):
    x = _each(NN, t, rhs)
    return x, (t, x)


def _tri_solve_bwd(res, dx):
    t, x = res
    drhs = _each(TN, t, dx)
    return _each(NT, drhs, x), drhs, _each(jnp.zeros_like, t)


_tri_solve.defvjp(_tri_solve_fwd, _tri_solve_bwd)


def _lower_ones(c):
    row = lax.broadcasted_iota(jnp.int32, (c, c), 0)
    col = lax.broadcasted_iota(jnp.int32, (c, c), 1)
    return (row >= col).astype(BF16)


def _ones_dot(ones, x, contract):
    hi, lo = _split_bf16(x)
    dims = (((contract,), (0,)), ((), ()))
    return (lax.dot_general(ones, hi, dims, preferred_element_type=F32)
            + lax.dot_general(ones, lo, dims, preferred_element_type=F32))


@jax.custom_vjp
def _cumsum_rows(x):
    return _ones_dot(_lower_ones(x.shape[0]), x, 1)


_cumsum_rows.defvjp(lambda x: (_ones_dot(_lower_ones(x.shape[0]), x, 1), None),
                    lambda _, g: (_ones_dot(_lower_ones(g.shape[0]), g, 0),))


def _wkv_chunk(s0, r, lw, k, v, a, b, t_inv=None):
    c = r[0].shape[0]
    row = lax.broadcasted_iota(jnp.int32, (c, c), 0)
    col = lax.broadcasted_iota(jnp.int32, (c, c), 1)
    strict, incl = row > col, row >= col
    cat = lambda p, q: jnp.concatenate([p, q], axis=0)
    cum = _each(_cumsum_rows, lw)
    e_neg = _each(lambda c_: jnp.exp(-c_), cum)
    ar = _each(lambda a_, r_, c_, l_: cat(a_ * jnp.exp(c_ - l_), r_ * jnp.exp(c_)), a, r, cum, lw)
    b_t, k_t = _each(jnp.multiply, b, e_neg), _each(jnp.multiply, k, e_neg)
    p_b, p_k, p_s = _each(NT, ar, b_t), _each(NT, ar, k_t), _each(NT, ar, s0)
    n_ab = _each(lambda p: jnp.where(strict, p[:c], 0.0), p_b)
    m_rb = _each(lambda p: jnp.where(incl, p[c:], 0.0), p_b)
    n_ak = _each(lambda p: jnp.where(strict, p[:c], 0.0), p_k)
    m_rk = _each(lambda p: jnp.where(incl, p[c:], 0.0), p_k)
    if t_inv is None:
        t_inv = _tri_inverse(n_ab)
    u = _tri_solve(n_ab, _each(lambda p, n_, v_: p[:c] + NN(n_, v_), p_s, n_ak, v), t_inv)
    y = _each(lambda p, mb, u_, mk, v_: p[c:] + NN(mb, u_) + NN(mk, v_), p_s, m_rb, u, m_rk, v)
    g_end = _each(lambda l_: jnp.exp(jnp.sum(l_, axis=0, keepdims=True)), lw)
    s1 = _each(lambda s_, g_, u_, v_, b_, k_: s_ * g_ + TN(cat(u_, v_), cat(b_, k_) * g_),
               s0, g_end, u, v, b_t, k_t)
    return y, s1, t_inv


def _adamw(w, g, m, v):
    m = ADAM_B1 * m + (1.0 - ADAM_B1) * g
    v = ADAM_B2 * v + (1.0 - ADAM_B2) * jnp.square(g)
    m_hat = m / (1.0 - ADAM_B1 ** ADAM_STEP)
    v_hat = v / (1.0 - ADAM_B2 ** ADAM_STEP)
    delta = -ADAM_LR * (m_hat / (jnp.sqrt(v_hat) + ADAM_EPS) + ADAM_WD * w)
    return delta, m, v


def token_shift_fwd(p, mu, *, tb, name):
    S, W = p.shape
    hb = tb // 8

    def body(p_ref, halo_ref, mu_ref, o_ref):
        i = pl.program_id(0)
        x = p_ref[...]
        before = halo_ref[7:8, :] * (i > 0).astype(F32)
        row = lax.broadcasted_iota(jnp.int32, (tb, W), 0)
        prev = jnp.where(row == 0, before, pltpu.roll(x, 1, 0))
        o_ref[...] = x + (prev - x) * mu_ref[...]

    blk = (2 * tb + 8) * W * 4
    return pl.pallas_call(
        body, name=name, grid=(S // tb,),
        in_specs=[pl.BlockSpec((tb, W), lambda i: (i, 0)),
                  pl.BlockSpec((8, W), lambda i: (jnp.maximum(i * hb - 1, 0), 0)),
                  pl.BlockSpec((1, W), lambda i: (0, 0))],
        out_specs=pl.BlockSpec((tb, W), lambda i: (i, 0)),
        out_shape=jax.ShapeDtypeStruct((S, W), F32),
        compiler_params=_cparams(("parallel",), blk),
    )(p, p, mu)


def token_shift_bwd(dxs, p, mu, *, tb, name):
    S, W = p.shape
    hb, nb = tb // 8, S // tb

    def body(d_ref, dnext_ref, p_ref, halo_ref, mu_ref, dp_ref, dmu_ref):
        i = pl.program_id(0)
        d, x, mu_v = d_ref[...], p_ref[...], mu_ref[...]
        row = lax.broadcasted_iota(jnp.int32, (tb, W), 0)
        before = halo_ref[7:8, :] * (i > 0).astype(F32)
        prev = jnp.where(row == 0, before, pltpu.roll(x, 1, 0))
        t = d * mu_v
        after = dnext_ref[0:1, :] * mu_v * (i < nb - 1).astype(F32)
        nxt = jnp.where(row == tb - 1, after, pltpu.roll(t, tb - 1, 0))
        dp_ref[...] = (d - t + nxt).astype(dp_ref.dtype)

        @pl.when(i == 0)
        def _():
            dmu_ref[...] = jnp.zeros_like(dmu_ref)

        dmu_ref[...] += jnp.sum(d * (prev - x), axis=0, keepdims=True)

    blk = (3 * tb + 16) * W * 4
    return pl.pallas_call(
        body, name=name, grid=(nb,),
        in_specs=[pl.BlockSpec((tb, W), lambda i: (i, 0)),
                  pl.BlockSpec((8, W), lambda i: (jnp.minimum((i + 1) * hb, S // 8 - 1), 0)),
                  pl.BlockSpec((tb, W), lambda i: (i, 0)),
                  pl.BlockSpec((8, W), lambda i: (jnp.maximum(i * hb - 1, 0), 0)),
                  pl.BlockSpec((1, W), lambda i: (0, 0))],
        out_specs=[pl.BlockSpec((tb, W), lambda i: (i, 0)), pl.BlockSpec((1, W), lambda i: (0, 0))],
        out_shape=[jax.ShapeDtypeStruct((S, W), BF16), jax.ShapeDtypeStruct((1, W), F32)],
        compiler_params=_cparams(("arbitrary",), blk),
    )(dxs, dxs, p, p, mu)


def _head_cols(h):
    return pl.ds(h * HEAD_DIM, HEAD_DIM)


def wkv_fwd(xs_rk, lw, k, a, b):
    S = lw.shape[0]
    C, nc, G, N = WKV_CHUNK, S // WKV_CHUNK, WKV_HEADS_PER_STEP, HEAD_DIM

    def body(r_ref, lw_ref, k_ref, v_ref, a_ref, b_ref, y_ref, st_ref, ti_ref, state):
        @pl.when(pl.program_id(1) == 0)
        def _():
            state[...] = jnp.zeros_like(state)

        for base in range(0, G, WKV_HEADS_PER_RUN):
            run = range(base, base + WKV_HEADS_PER_RUN)
            heads = lambda ref: tuple(ref[:, _head_cols(h)] for h in run)
            s0 = tuple(state[h] for h in run)
            y, s1, t_inv = _wkv_chunk(s0, heads(r_ref), heads(lw_ref), heads(k_ref), heads(v_ref), heads(a_ref),
                                      heads(b_ref))
            for i, h in enumerate(run):
                st_ref[h] = s0[i]
                ti_ref[h] = t_inv[i]
                y_ref[:, _head_cols(h)] = y[i]
                state[h] = s1[i]

    W = G * N
    seq = lambda j: pl.BlockSpec((C, W), functools.partial(lambda j, g, c: (c, j + g), j))
    per = D_MODEL // W
    per_chunk = lambda n: pl.BlockSpec((None, G, n, n), lambda g, c: (c, g, 0, 0))
    return pl.pallas_call(
        body, name="wkv_fwd", grid=(RWKV_HEADS // G, nc),
        in_specs=[seq(0), seq(0), seq(0), seq(2 * per), seq(0), seq(0)],
        out_specs=[seq(0), per_chunk(N), per_chunk(C)],
        out_shape=[jax.ShapeDtypeStruct((S, D_MODEL), F32), jax.ShapeDtypeStruct((nc, RWKV_HEADS, N, N), F32),
                   jax.ShapeDtypeStruct((nc, RWKV_HEADS, C, C), F32)],
        scratch_shapes=[pltpu.VMEM((G, N, N), F32)],
        compiler_params=_cparams(("parallel", "arbitrary"), 8 * C * W * 4 + 2 * G * N * N * 4 + G * C * C * 4),
    )(xs_rk, lw, k, xs_rk, a, b)


def wkv_bwd(xs_rk, lw, k, a, b, states, t_invs, dy):
    S = lw.shape[0]
    C, nc, G, N = WKV_CHUNK, S // WKV_CHUNK, WKV_HEADS_PER_STEP, HEAD_DIM

    def body(r_ref, lw_ref, k_ref, v_ref, a_ref, b_ref, st_ref, ti_ref, dy_ref,
             dr_ref, dlw_ref, dk_ref, dv_ref, da_ref, db_ref, dstate):
        @pl.when(pl.program_id(1) == 0)
        def _():
            dstate[...] = jnp.zeros_like(dstate)

        for base in range(0, G, WKV_HEADS_PER_RUN):
            run = range(base, base + WKV_HEADS_PER_RUN)
            heads = lambda ref: tuple(ref[:, _head_cols(h)] for h in run)
            t_inv = tuple(ti_ref[h] for h in run)
            chunk = lambda *args: _wkv_chunk(*args, t_inv)[:2]
            _, pull = jax.vjp(chunk, tuple(st_ref[h] for h in run), heads(r_ref), heads(lw_ref),
                              heads(k_ref), heads(v_ref), heads(a_ref), heads(b_ref))
            ds0, *grads = pull((heads(dy_ref), tuple(dstate[h] for h in run)))
            for i, h in enumerate(run):
                dstate[h] = ds0[i]
                for ref, grad in zip((dr_ref, dlw_ref, dk_ref, dv_ref, da_ref, db_ref), grads):
                    ref[:, _head_cols(h)] = grad[i]

    W = G * N
    seq = lambda j: pl.BlockSpec((C, W), functools.partial(lambda j, g, c: (nc - 1 - c, j + g), j))
    per = D_MODEL // W
    st = lambda n: pl.BlockSpec((None, G, n, n), lambda g, c: (nc - 1 - c, g, 0, 0))
    return pl.pallas_call(
        body, name="wkv_bwd", grid=(RWKV_HEADS // G, nc),
        in_specs=[seq(0), seq(0), seq(0), seq(2 * per), seq(0), seq(0), st(N), st(C), seq(0)],
        out_specs=[seq(0)] * 6, out_shape=[jax.ShapeDtypeStruct((S, D_MODEL), F32)] * 6,
        scratch_shapes=[pltpu.VMEM((G, N, N), F32)],
        compiler_params=_cparams(("parallel", "arbitrary"), 14 * C * W * 4 + 2 * G * N * N * 4 + G * C * C * 4),
    )(xs_rk, lw, k, xs_rk, a, b, states, t_invs, dy)


def _first_flag(i, per_seq):
    return (lax.rem(i, per_seq) == 0).astype(F32)


def _view(a):
    return a if isinstance(a, tuple) else (a, 0)


def _block_rows(half):
    return pl.ds(half * ATTN_BLK, ATTN_BLK)


def _block_heads(ref, half):
    return tuple(ref[_block_rows(half), _head_cols(h)] for h in range(ATTN_HPG))


def _pair_heads(ref):
    return _block_heads(ref, 0), _block_heads(ref, 1)


def attn_fwd(q, k, v, per_seq, name):
    (q, q_col), (k, k_col), (v, v_col) = _view(q), _view(k), _view(v)
    R, N = q.shape[0], GROUP_W
    n_pairs = R // (2 * ATTN_BLK)

    def body(q_ref, k_ref, kb_ref, v_ref, vb_ref, o_ref, lse_ref):
        pair = pl.program_id(0)
        first = (_first_flag(2 * pair, per_seq), _first_flag(2 * pair + 1, per_seq))
        o, lse = _attn_pair(_pair_heads(q_ref), _pair_heads(k_ref), _block_heads(kb_ref, 0), _pair_heads(v_ref),
                            _block_heads(vb_ref, 0), first)
        for half in range(2):
            for h in range(ATTN_HPG):
                o_ref[_block_rows(half), _head_cols(h)] = o[half][h]
                lse_ref[_block_rows(half), _head_cols(h)] = lse[half][h]

    cur = lambda col: pl.BlockSpec((2 * ATTN_BLK, N), lambda i: (i, col))
    prv = lambda col: pl.BlockSpec((ATTN_BLK, N), lambda i: (jnp.maximum(2 * i - 1, 0), col))
    return pl.pallas_call(
        body, name=name, grid=(n_pairs,), in_specs=[cur(q_col), cur(k_col), prv(k_col), cur(v_col), prv(v_col)],
        out_specs=[cur(0), cur(0)], out_shape=[jax.ShapeDtypeStruct((R, N), F32)] * 2,
        compiler_params=_cparams(("parallel",), 12 * ATTN_BLK * N * 4),
    )(q, k, k, v, v)


def attn_bwd(q, k, v, do, dlse, per_seq, name):
    views = [_view(a) for a in (q, k, v, do, dlse)]
    (q, q_col), (k, k_col), (v, v_col), (do, do_col), (dlse, dl_col) = views
    R, N = q.shape[0], GROUP_W
    n_pairs = R // (2 * ATTN_BLK)

    def body(q_ref, k_ref, kb_ref, v_ref, vb_ref, do_ref, dl_ref, dq_ref, dk_ref, dv_ref, carry_k, carry_v):
        step = pl.program_id(0)
        pair = n_pairs - 1 - step
        first = (_first_flag(2 * pair, per_seq), _first_flag(2 * pair + 1, per_seq))

        @pl.when(step == 0)
        def _():
            carry_k[...] = jnp.zeros_like(carry_k)
            carry_v[...] = jnp.zeros_like(carry_v)

        _, pull = jax.vjp(functools.partial(_attn_pair, first=first), _pair_heads(q_ref), _pair_heads(k_ref),
                          _block_heads(kb_ref, 0), _pair_heads(v_ref), _block_heads(vb_ref, 0))
        dq, dk, dk_before, dv, dv_before = pull((_pair_heads(do_ref), _pair_heads(dl_ref)))
        old_k, old_v = _block_heads(carry_k, 0), _block_heads(carry_v, 0)
        for h in range(ATTN_HPG):
            cols = _head_cols(h)
            for half in range(2):
                dq_ref[_block_rows(half), cols] = dq[half][h]
            dk_ref[_block_rows(0), cols] = dk[0][h]
            dv_ref[_block_rows(0), cols] = dv[0][h]
            dk_ref[_block_rows(1), cols] = dk[1][h] + old_k[h]
            dv_ref[_block_rows(1), cols] = dv[1][h] + old_v[h]
            carry_k[:, cols] = dk_before[h]
            carry_v[:, cols] = dv_before[h]

    cur = lambda col: pl.BlockSpec((2 * ATTN_BLK, N), lambda i: (n_pairs - 1 - i, col))
    prv = lambda col: pl.BlockSpec((ATTN_BLK, N), lambda i: (jnp.maximum(2 * (n_pairs - 1 - i) - 1, 0), col))
    return pl.pallas_call(
        body, name=name, grid=(n_pairs,),
        in_specs=[cur(q_col), cur(k_col), prv(k_col), cur(v_col), prv(v_col), cur(do_col), cur(dl_col)],
        out_specs=[cur(0)] * 3, out_shape=[jax.ShapeDtypeStruct((R, N), F32)] * 3,
        scratch_shapes=[pltpu.VMEM((ATTN_BLK, N), F32)] * 2,
        compiler_params=_cparams(("arbitrary",), 22 * ATTN_BLK * N * 4),
    )(q, k, k, v, v, do, dlse)


def by_residue(u, d):
    if d == 1:
        return u
    return u.reshape(u.shape[0] // d, d, GROUP_W).transpose(1, 0, 2).reshape(u.shape)


def by_position(u, d):
    if d == 1:
        return u
    return u.reshape(d, u.shape[0] // d, GROUP_W).transpose(1, 0, 2).reshape(u.shape)


def group_columns(t, col_block, d):
    if d == 1:
        return (t, col_block)
    return by_residue(t[:, GROUP_W * col_block:GROUP_W * (col_block + 1)], d)


def _ffn_fwd(x, norm, w_in, w_out, tag, token):
    h = rowmap(lambda x_b, g, tok: _rms(x_b, g) + tok[0:1, 0:1], [x], [norm, token], [(D_MODEL, BF16)], tb=512,
               name=tag + "_norm")[0]
    gu, act = ffn_in_act(h, w_in, tag + "_in")
    y = matmul(act, w_out, "nn", tag + "_out", add=x, scale=0.5)
    return y, (x, h, gu, act)


def _ffn_bwd(dy, saved, norm, w_in, w_out, tag, on_weight_grads):
    x, h, gu, act = saved
    no_token = jnp.zeros((8, 128), F32)
    dw_out = matmul(act, dy, "tn", tag + "_dwout", scale=0.5)
    dgu = ffn_dact_dgu(dy, w_out, gu, 0.5, tag + "_dgu")
    dw_in = matmul_cs(h, dgu, "tn", tag + "_dwin", no_token)
    dh = matmul_cs(dgu, w_in, "nt", tag + "_dh", on_weight_grads(dw_in, dw_out))

    def norm_bwd(x_b, dh_b, dy_b, g):
        dx, dg = jax.vjp(_rms, x_b, g)[1](dh_b)
        return dy_b + dx, dg

    dx, dnorm = rowmap(norm_bwd, [x, dh, dy], [norm], [(D_MODEL, F32)], [(1, D_MODEL)], tb=256,
                       name=tag + "_dnorm")
    return dx, dnorm, dw_in, dw_out


def layer_step(x, tgt, W, P, start_token, more_weights, on_mixer_grads, on_ffn1_grads):
    S = x.shape[0]
    x1, ffn1_saved = _ffn_fwd(x, P["ffn1_norm"], W["ffn1_w_in"], W["ffn1_w_out"], "ffn1", start_token)
    W = {**W, **more_weights("mixer", x1)}
    head_of = lambda n: jnp.arange(n)[:, None] // HEAD_DIM == jnp.arange(n // HEAD_DIM)[None, :]
    seg, seg_a = head_of(D_MODEL).astype(BF16), head_of(ATTN_WIDTH).astype(BF16)
    seg_t, seg_a_t = seg.T, seg_a.T
    tile_t = (jnp.arange(HEAD_DIM)[:, None] == jnp.arange(ATTN_WIDTH)[None, :] % HEAD_DIM).astype(BF16)
    qk_params = [P["attn_q_norm"], P["attn_k_norm"], seg_a, seg_a_t, tile_t]
    w_rkv, w_lora = W["w_in"][:, :RKV], W["w_in"][:, RKV:RKV + LORA]
    w_qkv = W["w_in"][:, RKV + LORA:RKV + LORA + 3 * ATTN_WIDTH]
    w_gate = W["w_in"][:, RKV + LORA + 3 * ATTN_WIDTH:]
    mu_rk, mu_lo = P["rwkv_mu"][:, :RKV], P["rwkv_mu"][:, RKV:]
    zeros = lambda n: jnp.zeros((n, D_MODEL), F32)
    w2p = jnp.concatenate([W["rwkv_w2"], zeros(LORA - LORA_W)], axis=0)
    a2p = jnp.concatenate([zeros(LORA_W), W["rwkv_a2"], zeros(LORA_G)], axis=0)
    g2p = jnp.concatenate([zeros(LORA_W + LORA_A), W["rwkv_g2"]], axis=0)
    pre_params = [P["rwkv_w0"], w2p, P["rwkv_a0"], a2p, g2p, P["rwkv_k_k"], P["rwkv_k_a"], seg, seg_t]
    post_params = [P["rwkv_r_k"], P["rwkv_ln_w"], P["rwkv_ln_b"], seg, seg_t]
    col = lambda arr, j: (arr, D_MODEL, j)

    h = rowmap(_rms, [x1], [P["mix_norm"]], [(D_MODEL, BF16)], tb=512, name="mix_norm")[0]
    p_rk = matmul(h, w_rkv, "nn", "proj_rkv")
    p_lo = matmul(h, w_lora, "nn", "proj_lora")
    p_qkv = matmul(h, w_qkv, "nn", "proj_qkv")
    p_gate = matmul(h, w_gate, "nn", "proj_gate")
    xs_rk = token_shift_fwd(p_rk, mu_rk, tb=256, name="shift_rk")
    xs_lo = token_shift_fwd(p_lo, mu_lo, tb=256, name="shift_lora")
    lw, k_mod, a_neg, b_kk, g = rowmap(
        _rwkv_pre, [xs_rk, xs_lo], pre_params, [(D_MODEL, F32)] * 5, tb=256, name="rwkv_pre")
    wkv, states, t_invs = wkv_fwd(xs_rk, lw, k_mod, a_neg, b_kk)
    post_rows = [wkv, col(xs_rk, 0), k_mod, col(xs_rk, 2), g]
    y_a = rowmap(_rwkv_post, post_rows, post_params, [(D_MODEL, BF16)], tb=256, name="rwkv_post")[0]

    qk_rows = [(p_qkv, ATTN_WIDTH, 0), (p_qkv, ATTN_WIDTH, 1)]
    qn, kn = rowmap(_qk_norm, qk_rows, qk_params, [(ATTN_WIDTH, F32)] * 2, tb=256, name="qk_norm")
    dil = [d for _, d in ATTN_PAIRS]
    groups = range(len(dil))
    per_seq = [S // d // ATTN_BLK for d in dil]
    v_first = 2 * ATTN_WIDTH // GROUP_W
    q_s = [group_columns(qn, g, dil[g]) for g in groups]
    k_s = [group_columns(kn, g, dil[g]) for g in groups]
    v_s = [group_columns(p_qkv, v_first + g, dil[g]) for g in groups]
    attn = [attn_fwd(q_s[g], k_s[g], v_s[g], per_seq[g], "attn_fwd_%d" % g) for g in groups]
    o_lse = [by_position(attn[g][j], dil[g]) for j in range(2) for g in groups]
    y_b = rowmap(_group_combine, o_lse, [], [(ATTN_WIDTH, BF16)], tb=512, name="attn_combine")[0]

    W = {**W, **more_weights("out", y_b)}
    pa = matmul(y_a, W["w_proj_rwkv"], "nn", "proj_a")
    pb = matmul(y_b, W["w_proj_attn"], "nn", "proj_b")
    merged = rowmap(_gate_merge, [p_gate, pa, pb], [P["b_gate"]], [(D_MODEL, BF16)], tb=256, name="merge")[0]
    x2 = matmul(merged, W["w_out"], "nn", "mix_out", add=x1)
    x3, ffn2_saved = _ffn_fwd(x2, P["ffn2_norm"], W["ffn2_w_in"], W["ffn2_w_out"], "ffn2",
                              jnp.zeros_like(start_token))

    def loss_head(y_b_, t_b):
        err = y_b_ - t_b
        return err * (1.0 / D_MODEL), (0.5 / D_MODEL) * jnp.sum(err * err, axis=0, keepdims=True)

    dx3, loss_cols = rowmap(loss_head, [x3, tgt], [], [(D_MODEL, F32)], [(1, D_MODEL)], tb=512, name="loss")

    gW, gP = {}, {}
    dx2, gP["ffn2_norm"], gW["ffn2_w_in"], gW["ffn2_w_out"] = _ffn_bwd(
        dx3, ffn2_saved, P["ffn2_norm"], W["ffn2_w_in"], W["ffn2_w_out"], "ffn2",
        lambda dw_in, dw_out: jnp.zeros_like(start_token))

    dmerged = matmul(dx2, W["w_out"], "nt", "d_merged")
    gW["w_out"] = matmul(merged, dx2, "tn", "dw_out")

    def merge_bwd(pg, pa_b, pb_b, dm, bg):
        return jax.vjp(_gate_merge, pg, pa_b, pb_b, bg)[1](dm)

    dp_gate, dpa, dpb, gP["b_gate"] = rowmap(
        merge_bwd, [p_gate, pa, pb, dmerged], [P["b_gate"]],
        [(2 * D_MODEL, BF16), (D_MODEL, BF16), (D_MODEL, BF16)], [(1, 2 * D_MODEL)], tb=256, name="merge_bwd")
    dy_a = matmul(dpa, W["w_proj_rwkv"], "nt", "d_ya")
    gW["w_proj_rwkv"] = matmul(y_a, dpa, "tn", "dw_proj_a")
    dy_b = matmul(dpb, W["w_proj_attn"], "nt", "d_yb")
    gW["w_proj_attn"] = matmul(y_b, dpb, "tn", "dw_proj_b")

    def combine_bwd(*blocks):
        return jax.vjp(_group_combine, *blocks[:-1])[1](blocks[-1])

    d_o_lse = rowmap(combine_bwd, o_lse + [dy_b], [], [(GROUP_W, F32)] * 6, tb=256, name="attn_combine_bwd")
    d_attn = [attn_bwd(q_s[g], k_s[g], v_s[g], by_residue(d_o_lse[g], dil[g]), by_residue(d_o_lse[3 + g], dil[g]),
                       per_seq[g], "attn_bwd_%d" % g) for g in groups]

    def qk_norm_bwd(q_b, k_b, *rest):
        dqkv, (qg, kg, sg, sgt, tl) = rest[:9], rest[9:]
        f = lambda *a: _qk_norm(*a, sg, sgt, tl)
        dqn, dkn = jnp.concatenate(dqkv[0:3], axis=1), jnp.concatenate(dqkv[3:6], axis=1)
        dq, dk, dqg, dkg = jax.vjp(f, q_b, k_b, qg, kg)[1]((dqn, dkn))
        return jnp.concatenate([dq, dk, *dqkv[6:9]], axis=1), dqg, dkg

    dp_qkv, gP["attn_q_norm"], gP["attn_k_norm"] = rowmap(
        qk_norm_bwd, qk_rows + [by_position(d_attn[g][j], dil[g]) for j in range(3) for g in groups], qk_params,
        [(3 * ATTN_WIDTH, BF16)], [(1, HEAD_DIM)] * 2, tb=256, name="qk_norm_bwd")

    def post_bwd(wkv_b, r_b, k_b, v_b, g_b, d_b, r_k, ln_w, ln_b, sg, sgt):
        f = lambda *a: _rwkv_post(*a, sg, sgt)
        return jax.vjp(f, wkv_b, r_b, k_b, v_b, g_b, r_k, ln_w, ln_b)[1](d_b)

    dwkv, dr_p, dk_p, dv_p, dg, gP["rwkv_r_k"], gP["rwkv_ln_w"], gP["rwkv_ln_b"] = rowmap(
        post_bwd, post_rows + [dy_a], post_params, [(D_MODEL, F32)] * 5, [(1, D_MODEL)] * 3, tb=128,
        name="rwkv_post_bwd")
    dr_w, dlw, dk_w, dv_w, da_neg, db_kk = wkv_bwd(xs_rk, lw, k_mod, a_neg, b_kk, states, t_invs, dwkv)

    def pre_bwd(xrk_b, xlo_b, dlw_b, dkw_b, dkp_b, da_b, db_b, dg_b, drp_b, drw_b, dvp_b, dvw_b,
                w0, w2, a0, a2, g2, k_k, k_a, sg, sgt):
        f = lambda *a: _rwkv_pre(*a, sg, sgt)
        pull = jax.vjp(f, xrk_b, xlo_b, w0, w2, a0, a2, g2, k_k, k_a)[1]
        dxrk, dxlo, *dpar = pull((dlw_b, dkw_b + dkp_b, da_b, db_b, dg_b))
        direct = jnp.concatenate([drp_b + drw_b, jnp.zeros_like(drp_b), dvp_b + dvw_b], axis=1)
        return (dxrk + direct, dxlo, *dpar)

    pre_rows = [xs_rk, xs_lo, dlw, dk_w, dk_p, da_neg, db_kk, dg, dr_p, dr_w, dv_p, dv_w]
    dxs_rk, dxs_lo, gP["rwkv_w0"], dw2p, gP["rwkv_a0"], da2p, dg2p, gP["rwkv_k_k"], gP["rwkv_k_a"] = rowmap(
        pre_bwd, pre_rows, pre_params, [(RKV, F32), (LORA, F32)],
        [(1, D_MODEL), (LORA, D_MODEL), (1, D_MODEL), (LORA, D_MODEL), (LORA, D_MODEL), (1, D_MODEL), (1, D_MODEL)],
        tb=128, name="rwkv_pre_bwd")
    gW["rwkv_w2"] = dw2p[:LORA_W]
    gW["rwkv_a2"] = da2p[LORA_W:LORA_W + LORA_A]
    gW["rwkv_g2"] = dg2p[LORA_W + LORA_A:]
    dp_rk, dmu_rk = token_shift_bwd(dxs_rk, p_rk, mu_rk, tb=256, name="shift_rk_bwd")
    dp_lo, dmu_lo = token_shift_bwd(dxs_lo, p_lo, mu_lo, tb=256, name="shift_lora_bwd")
    gP["rwkv_mu"] = jnp.concatenate([dmu_rk, dmu_lo], axis=1)

    dh = matmul(dp_rk, w_rkv, "nt", "dh_rkv")
    dh = matmul(dp_lo, w_lora, "nt", "dh_lora", add=dh)
    dh = matmul(dp_qkv, w_qkv, "nt", "dh_qkv", add=dh)
    dh = matmul(dp_gate, w_gate, "nt", "dh_gate", add=dh)
    gW["w_in"] = jnp.concatenate([
        matmul(h, dp_rk, "tn", "dw_rkv"), matmul(h, dp_lo, "tn", "dw_lora"),
        matmul(h, dp_qkv, "tn", "dw_qkv"), matmul(h, dp_gate, "tn", "dw_gate")], axis=1)

    token = on_mixer_grads(gW)

    def norm_bwd(x_b, dh_b, dy_b, gn, tok):
        dx, dgn = jax.vjp(_rms, x_b, gn)[1](dh_b)
        return dy_b + dx + tok[0:1, 0:1], dgn

    dx1, gP["mix_norm"] = rowmap(norm_bwd, [x1, dh, dx2], [P["mix_norm"], token], [(D_MODEL, F32)],
                                 [(1, D_MODEL)], tb=256, name="mix_norm_bwd")
    dx, gP["ffn1_norm"], gW["ffn1_w_in"], gW["ffn1_w_out"] = _ffn_bwd(
        dx1, ffn1_saved, P["ffn1_norm"], W["ffn1_w_in"], W["ffn1_w_out"], "ffn1", on_ffn1_grads)
    return loss_cols, dx, gW, gP


N_SHARDS = 4
OTHER_CHIPS = N_SHARDS - 1
BIG = (("ffn1_w_in", (D_MODEL, 2 * D_FF), 1), ("ffn1_w_out", (D_FF, D_MODEL), 0),
       ("w_in", (D_MODEL, 7712), 1), ("rwkv_w2", (LORA_W, D_MODEL), 1), ("rwkv_a2", (LORA_A, D_MODEL), 1),
       ("rwkv_g2", (LORA_G, D_MODEL), 1), ("w_proj_rwkv", (D_MODEL, D_MODEL), 0),
       ("w_proj_attn", (ATTN_WIDTH, D_MODEL), 1), ("w_out", (D_MODEL, D_MODEL), 0),
       ("ffn2_w_in", (D_MODEL, 2 * D_FF), 1), ("ffn2_w_out", (D_FF, D_MODEL), 0))
SMALL = (("ffn1_norm", 1024), ("mix_norm", 1024), ("b_gate", 2048), ("rwkv_mu", 3360), ("rwkv_w0", 1024),
         ("rwkv_a0", 1024), ("rwkv_k_k", 1024), ("rwkv_k_a", 1024), ("rwkv_r_k", 1024), ("rwkv_ln_w", 1024),
         ("rwkv_ln_b", 1024), ("attn_q_norm", 64), ("attn_k_norm", 64), ("ffn2_norm", 1024))
WEIGHT_ORDER = ("ffn1_norm", "ffn1_w_in", "ffn1_w_out", "mix_norm", "w_in", "b_gate", "rwkv_mu", "rwkv_w0",
                "rwkv_w2", "rwkv_a0", "rwkv_a2", "rwkv_g2", "rwkv_k_k", "rwkv_k_a", "rwkv_r_k", "rwkv_ln_w",
                "rwkv_ln_b", "attn_q_norm", "attn_k_norm", "w_proj_rwkv", "w_proj_attn", "w_out", "ffn2_norm",
                "ffn2_w_in", "ffn2_w_out")


LORA_PARTS = ("rwkv_w2", "rwkv_a2", "rwkv_g2")
BLOCK_MAJOR = ("ffn1_w_in", "ffn2_w_in")
FIRST_FFN = ("ffn1_w_in", "ffn1_w_out")
MIXER_IN = ("w_in", "lora")
SMALL_USED = D_MODEL + sum(n for _, n in SMALL)
SMALL_W = -(-SMALL_USED // 128) * 128


def _travel():
    out = {}
    for name, shape, axis in BIG:
        if name == LORA_PARTS[0]:
            out["lora"] = ((LORA, D_MODEL), 1)
        elif name not in LORA_PARTS:
            out[name] = (shape, axis)
    return out


def local_blocks(vals):
    out = {n: vals[n] for n in _travel() if n != "lora"}
    out["lora"] = jnp.concatenate([vals[n] for n in LORA_PARTS], axis=0)
    return out


def split_lora(t):
    return {"rwkv_w2": t[:LORA_W], "rwkv_a2": t[LORA_W:LORA_W + LORA_A], "rwkv_g2": t[LORA_W + LORA_A:]}


def blocks_to_full(name, blocks):
    shape, axis = _travel()[name]
    if name in BLOCK_MAJOR:
        return blocks
    if axis == 0:
        return blocks.reshape(shape)
    return blocks.transpose(1, 0, 2).reshape(shape)


def full_to_blocks(name, full):
    shape, axis = _travel()[name]
    if name in BLOCK_MAJOR:
        return full
    if axis == 0:
        return full.reshape(N_SHARDS, shape[0] // N_SHARDS, shape[1])
    return full.reshape(shape[0], N_SHARDS, shape[1] // N_SHARDS).transpose(1, 0, 2)


def pack_small(vals, head):
    parts = [head] + [vals[name].reshape(1, n) for name, n in SMALL]
    parts.append(jnp.zeros((1, SMALL_W - SMALL_USED), F32))
    return jnp.concatenate(parts, axis=1)


def unpack_small(vec, shapes):
    out, off = {}, D_MODEL
    for name, n in SMALL:
        out[name] = vec[:, off:off + n].reshape(shapes[name])
        off += n
    return out


def _place():
    return lax.axis_index("x"), lax.axis_index("y"), lax.axis_index("c")


def _other_chips(x, y):
    return [(1 - x, y), (x, 1 - y), (1 - x, 1 - y)]


def _remote(src, dst, send_sem, recv_sem, device):
    return pltpu.make_async_remote_copy(src_ref=src, dst_ref=dst, send_sem=send_sem, recv_sem=recv_sem,
                                        device_id=device, device_id_type=MESH)


def _half(ref, who):
    hr = ref.shape[-2] // 2
    rows = pl.ds(pl.multiple_of(who * hr, 8), hr)
    return ref.at[rows] if len(ref.shape) == 2 else ref.at[:, rows]


HBM_REF = pl.BlockSpec(memory_space=pl.ANY)
COMM_PARAMS = dict(compiler_params=pltpu.CompilerParams(has_side_effects=True))


def gather_weights(blocks):
    n = len(blocks)

    def body(*refs):
        ins, outs = refs[:n], refs[n:2 * n]
        ici_send, ici_recv, d2d_send, d2d_recv = refs[2 * n:]
        x, y, c = _place()
        me, sibling, chips = 2 * x + y, (x, y, 1 - c), _other_chips(x, y)
        first = [_remote(_half(ins[t], c), _half(outs[t].at[me], c), ici_send.at[k, t], ici_recv.at[k, t],
                         (px, py, c)) for k, (px, py) in enumerate(chips) for t in range(n)]
        for cp in first:
            cp.start()
        passed = []
        for k, (px, py) in enumerate(chips):
            for t in range(n):
                landed = _half(outs[t].at[2 * px + py], c)
                _remote(landed, landed, ici_send.at[k, t], ici_recv.at[k, t], (px, py, c)).wait_recv()
                cp = _remote(landed, landed, d2d_send.at[k, t], d2d_recv.at[k, t], sibling)
                cp.start()
                passed.append(cp)
        for k, (px, py) in enumerate(chips):
            for t in range(n):
                other = _half(outs[t].at[2 * px + py], 1 - c)
                _remote(other, other, d2d_send.at[k, t], d2d_recv.at[k, t], sibling).wait_recv()
        for cp in first + passed:
            cp.wait_send()

    res = pl.pallas_call(
        body, name="gather_weights", in_specs=[HBM_REF] * n, out_specs=[HBM_REF] * n,
        out_shape=[jax.ShapeDtypeStruct((N_SHARDS,) + b.shape, b.dtype) for b in blocks],
        scratch_shapes=[pltpu.SemaphoreType.DMA((3, n))] * 4, **COMM_PARAMS)(*blocks)
    me = 2 * lax.axis_index("x") + lax.axis_index("y")
    return [lax.dynamic_update_slice(g, b[None], (me, 0, 0)) for g, b in zip(res, blocks)]


def _gather_copies(ins, outs, send_sem, recv_sem):
    x, y, c = _place()
    return [_remote(_half(ins[t], c), _half(outs[t].at[2 * x + y], c), send_sem(k, t), recv_sem(k, t), (px, py, c))
            for k, (px, py) in enumerate(_other_chips(x, y)) for t in range(len(ins))]


def split_start(copies, sources, landing_shapes, name):
    n = len(sources)
    n_cp = OTHER_CHIPS * n

    def body(*refs):
        srcs, dsts = refs[:n], refs[n:2 * n]
        sems, token = refs[2 * n:2 * n + 2 * n_cp], refs[-1]
        for cp in copies(srcs, dsts, lambda k, t: sems[k * n + t], lambda k, t: sems[n_cp + k * n + t]):
            cp.start()
        token[...] = jnp.zeros_like(token)

    hbm = lambda a: pltpu.with_memory_space_constraint(a, pltpu.HBM)
    buffers = list(sources) + [lax.empty(shape, s.dtype) for shape, s in zip(landing_shapes, sources)]
    res = pl.pallas_call(
        body, name=name,
        out_shape=(*[pltpu.SemaphoreType.DMA(())] * (2 * n_cp),
                   *[pltpu.HBM(a.shape, a.dtype) for a in buffers], jax.ShapeDtypeStruct((8, 128), F32)),
        in_specs=[SPLIT_HBM] * (2 * n),
        out_specs=(*[SPLIT_SEM] * (2 * n_cp), *[SPLIT_HBM] * (2 * n), pl.BlockSpec(memory_space=pltpu.VMEM)),
        input_output_aliases={t: 2 * n_cp + t for t in range(2 * n)}, **SPLIT_PARAMS,
    )(*[hbm(a) for a in buffers])
    return (copies, n, res[:-1]), res[-1]


def split_wait(handles, after, name):
    copies, n, held = handles
    n_cp = OTHER_CHIPS * n
    sems, thru = held[:2 * n_cp], held[2 * n_cp:]

    def body(*refs):
        srcs, dsts = refs[:n], refs[n:2 * n]
        sem_refs = refs[2 * n:2 * n + 2 * n_cp]
        for cp in copies(srcs, dsts, lambda k, t: sem_refs[k * n + t], lambda k, t: sem_refs[n_cp + k * n + t]):
            cp.wait_send()
            cp.wait_recv()

    res = pl.pallas_call(
        body, name=name, out_shape=tuple(pltpu.HBM(a.shape, a.dtype) for a in thru),
        in_specs=[SPLIT_HBM] * (2 * n) + [SPLIT_SEM] * (2 * n_cp) + [pl.BlockSpec(memory_space=pl.ANY)],
        out_specs=tuple([SPLIT_HBM] * (2 * n)), input_output_aliases={t: t for t in range(2 * n)}, **SPLIT_PARAMS,
    )(*thru, *sems, after)
    return list(res[n:])


def gather_start(blocks, name):
    return split_start(_gather_copies, blocks, [(N_SHARDS,) + b.shape for b in blocks], name)


def pass_halves(gathered, blocks, name):
    n = len(gathered)

    def body(*refs):
        outs = refs[n:2 * n]
        send_sems, recv_sems = refs[2 * n:]
        x, y, c = _place()
        slots = [2 * px + py for px, py in _other_chips(x, y)]
        give = [_remote(_half(outs[t].at[s], c), _half(outs[t].at[s], c), send_sems.at[k, t], recv_sems.at[k, t],
                        (x, y, 1 - c)) for k, s in enumerate(slots) for t in range(n)]
        for cp in give:
            cp.start()
        for k, s in enumerate(slots):
            for t in range(n):
                other = _half(outs[t].at[s], 1 - c)
                _remote(other, other, send_sems.at[k, t], recv_sems.at[k, t], (x, y, 1 - c)).wait_recv()
        for cp in give:
            cp.wait_send()

    res = pl.pallas_call(
        body, name=name, in_specs=[HBM_REF] * n, out_specs=[HBM_REF] * n,
        out_shape=[jax.ShapeDtypeStruct(g.shape, g.dtype) for g in gathered],
        input_output_aliases={t: t for t in range(n)},
        scratch_shapes=[pltpu.SemaphoreType.DMA((3, n))] * 2, **COMM_PARAMS)(*gathered)
    me = 2 * lax.axis_index("x") + lax.axis_index("y")
    return [lax.dynamic_update_slice(g, b[None], (me, 0, 0)) for g, b in zip(res, blocks)]


def swap_halves(grads):
    n = len(grads)

    def body(*refs):
        ins, got = refs[:n], refs[n:2 * n]
        send_sems, recv_sems = refs[2 * n:]
        x, y, c = _place()
        give = [_remote(_half(ins[t], 1 - c), got[t], send_sems.at[t], recv_sems.at[t], (x, y, 1 - c))
                for t in range(n)]
        for cp in give:
            cp.start()
        for cp in give:
            cp.wait_recv()
        for cp in give:
            cp.wait_send()

    return pl.pallas_call(
        body, name="swap_halves", in_specs=[HBM_REF] * n, out_specs=[HBM_REF] * n,
        out_shape=[jax.ShapeDtypeStruct((g.shape[0], g.shape[1] // 2, g.shape[2]), g.dtype) for g in grads],
        scratch_shapes=[pltpu.SemaphoreType.DMA((n,))] * 2, **COMM_PARAMS)(*grads)


def join_halves(blocks):
    n = len(blocks)

    def body(*refs):
        outs = refs[n:2 * n]
        send_sems, recv_sems = refs[2 * n:]
        x, y, c = _place()
        give = [_remote(_half(outs[t], c), _half(outs[t], c), send_sems.at[t], recv_sems.at[t], (x, y, 1 - c))
                for t in range(n)]
        for cp in give:
            cp.start()
        for t in range(n):
            arriving = _half(outs[t], 1 - c)
            _remote(arriving, arriving, send_sems.at[t], recv_sems.at[t], (x, y, 1 - c)).wait_recv()
        for cp in give:
            cp.wait_send()

    return pl.pallas_call(
        body, name="join_halves", in_specs=[HBM_REF] * n, out_specs=[HBM_REF] * n,
        out_shape=[jax.ShapeDtypeStruct(b.shape, b.dtype) for b in blocks],
        input_output_aliases={t: t for t in range(n)},
        scratch_shapes=[pltpu.SemaphoreType.DMA((n,))] * 2, **COMM_PARAMS)(*blocks)


SPLIT_HBM = pl.BlockSpec(memory_space=pltpu.HBM)
SPLIT_SEM = pl.BlockSpec(memory_space=pltpu.SEMAPHORE)
SPLIT_PARAMS = dict(compiler_params=pltpu.CompilerParams(has_side_effects=pltpu.SideEffectType.DATAFLOW_SIDE_EFFECTING))


def _scatter_copies(parts, landed, send_sem, recv_sem):
    x, y, c = _place()
    return [_remote(parts[t].at[2 * px + py], landed[t].at[k], send_sem(k, t), recv_sem(k, t), (px, py, c))
            for k, (px, py) in enumerate(_other_chips(x, y)) for t in range(len(parts))]


def scatter_start(partials, name):
    return split_start(_scatter_copies, partials, [(OTHER_CHIPS,) + p.shape[1:] for p in partials], name)


def chip_sums(grads, got):
    names = list(grads)
    partials = []
    for name, theirs in zip(names, got):
        n_slot, hr, width = theirs.shape
        tb = _row_block(hr, width, 6)
        per_half = hr // tb
        mine = lambda i, s, per_half=per_half: (i // per_half) * 2 * per_half + s[0] * per_half + i % per_half
        p = placed_map(
            jnp.add,
            [(grads[name].reshape(2 * n_slot * hr, width), mine), (theirs.reshape(n_slot * hr, width), lambda i, s: i)],
            (n_slot * hr, width, BF16, lambda i, s: i), n_blocks=n_slot * per_half, tb=tb, name="chip_sum_" + name)
        partials.append(p.reshape(theirs.shape))
    return partials


def owner_sums(grads, got, landed):
    names = list(grads)
    blocks = []
    for name, theirs, arrived in zip(names, got, landed):
        n_slot, hr, width = theirs.shape
        tb = _row_block(hr, width, 6)
        per_half = hr // tb
        views = [(grads[name].reshape(2 * n_slot * hr, width),
                  lambda i, s, per_half=per_half: s[1] * 2 * per_half + s[0] * per_half + i),
                 (theirs.reshape(n_slot * hr, width), lambda i, s, per_half=per_half: s[1] * per_half + i)]
        views += [(arrived.reshape(3 * hr, width), functools.partial(lambda k, per_half, i, s: k * per_half + i,
                                                                     k, per_half)) for k in range(3)]
        f = lambda a, b, l0, l1, l2: (((a + b) + l0.astype(F32)) + l1.astype(F32)) + l2.astype(F32)
        blocks.append(placed_map(
            f, views,(2 * hr, width, F32, lambda i, s, per_half=per_half: s[0] * per_half + i),
            n_blocks=per_half, tb=tb, name="owner_sum_" + name))
    return dict(zip(names, join_halves(blocks)))


def adamw_block(name, w, g, m, v):
    rows, width = w.shape
    return rowmap(_adamw, [w, g, m, v], [], [(width, F32)] * 3, tb=_row_block(rows, width, 7),
                  name="adamw_" + name)


def reduce_small(vec, w, m, v):
    n_dev = 8

    def body(vec_ref, w_ref, m_ref, v_ref, loss_ref, g_ref, d_ref, m2_ref, v2_ref, slots, send_sems, recv_sems):
        x, y, c = _place()
        me = 4 * x + 2 * y + c
        slots[me] = vec_ref[...]
        flips = [(fx, fy, fc) for fx in (0, 1) for fy in (0, 1) for fc in (0, 1)][1:]
        peers = [(1 - x if fx else x, 1 - y if fy else y, 1 - c if fc else c) for fx, fy, fc in flips]
        sends = [pltpu.make_async_remote_copy(
            src_ref=vec_ref, dst_ref=slots.at[me], send_sem=send_sems.at[j], recv_sem=recv_sems.at[j],
            device_id=peer, device_id_type=MESH) for j, peer in enumerate(peers)]
        for cp in sends:
            cp.start()
        for j, (px, py, pc) in enumerate(peers):
            pltpu.make_async_remote_copy(
                src_ref=vec_ref, dst_ref=slots.at[4 * px + 2 * py + pc], send_sem=send_sems.at[j],
                recv_sem=recv_sems.at[j], device_id=(px, py, pc), device_id_type=MESH).wait_recv()
        for cp in sends:
            cp.wait_send()
        g = slots[0]
        for d in range(1, n_dev):
            g = g + slots[d]
        loss_ref[...] = jnp.sum(g[:, :D_MODEL], axis=1, keepdims=True)
        delta, m2, v2 = _adamw(w_ref[...], g, m_ref[...], v_ref[...])
        g_ref[...], d_ref[...], m2_ref[...], v2_ref[...] = g, delta, m2, v2

    vm = pl.BlockSpec(memory_space=pltpu.VMEM)
    vec_t = jax.ShapeDtypeStruct(vec.shape, F32)
    return pl.pallas_call(
        body, name="reduce_small", in_specs=[vm] * 4, out_specs=[vm] * 5,
        out_shape=[jax.ShapeDtypeStruct((1, 1), F32)] + [vec_t] * 4,
        scratch_shapes=[pltpu.VMEM((n_dev,) + vec.shape, F32), pltpu.SemaphoreType.DMA((n_dev - 1,)),
                        pltpu.SemaphoreType.DMA((n_dev - 1,))],
        compiler_params=pltpu.CompilerParams(has_side_effects=True),
    )(vec, w, m, v)


def kernel(x, ffn1_norm, ffn1_w_in, ffn1_w_out, mix_norm, w_in, b_gate, rwkv_mu, rwkv_w0, rwkv_w2, rwkv_a0, rwkv_a2, rwkv_g2, rwkv_k_k, rwkv_k_a, rwkv_r_k, rwkv_ln_w, rwkv_ln_b, attn_q_norm, attn_k_norm, w_proj_rwkv, w_proj_attn, w_out, ffn2_norm, ffn2_w_in, ffn2_w_out, loss_target, m_ffn1_norm, m_ffn1_w_in, m_ffn1_w_out, m_mix_norm, m_w_in, m_b_gate, m_rwkv_mu, m_rwkv_w0, m_rwkv_w2, m_rwkv_a0, m_rwkv_a2, m_rwkv_g2, m_rwkv_k_k, m_rwkv_k_a, m_rwkv_r_k, m_rwkv_ln_w, m_rwkv_ln_b, m_attn_q_norm, m_attn_k_norm, m_w_proj_rwkv, m_w_proj_attn, m_w_out, m_ffn2_norm, m_ffn2_w_in, m_ffn2_w_out, v_ffn1_norm, v_ffn1_w_in, v_ffn1_w_out, v_mix_norm, v_w_in, v_b_gate, v_rwkv_mu, v_rwkv_w0, v_rwkv_w2, v_rwkv_a0, v_rwkv_a2, v_rwkv_g2, v_rwkv_k_k, v_rwkv_k_a, v_rwkv_r_k, v_rwkv_ln_w, v_rwkv_ln_b, v_attn_q_norm, v_attn_k_norm, v_w_proj_rwkv, v_w_proj_attn, v_w_out, v_ffn2_norm, v_ffn2_w_in, v_ffn2_w_out):
    given = dict(locals())
    weights = {n: given[n] for n in WEIGHT_ORDER}
    mom_m = {n: given["m_" + n] for n in WEIGHT_ORDER}
    mom_v = {n: given["v_" + n] for n in WEIGHT_ORDER}
    big = [name for name, _, _ in BIG]
    shapes = {n: weights[n].shape for n in WEIGHT_ORDER}
    blocks_of = lambda d: local_blocks({n: d[n][0] for n in big})
    w_blk, m_blk, v_blk = blocks_of(weights), blocks_of(mom_m), blocks_of(mom_v)
    names = list(w_blk)

    early = [n for n in names if n not in FIRST_FFN]
    bf16_block = lambda n: w_blk[n].astype(BF16)
    W = {n: blocks_to_full(n, g) for n, g in zip(FIRST_FFN, gather_weights([bf16_block(n) for n in FIRST_FFN]))}
    stages = {"mixer": [n for n in early if n in MIXER_IN], "out": [n for n in early if n not in MIXER_IN]}
    stage_blocks = {s: [bf16_block(n) for n in stages[s]] for s in stages}
    started = {s: gather_start(stage_blocks[s], "gather_start_" + s) for s in ("mixer", "out")}
    start_token = started["mixer"][1] + started["out"][1]

    def more_weights(stage, after):
        landed = split_wait(started[stage][0], after, "gather_wait_" + stage)
        got = pass_halves(landed, stage_blocks[stage], "pass_halves_" + stage)
        more = {n: blocks_to_full(n, g) for n, g in zip(stages[stage], got)}
        if "lora" in more:
            more.update(split_lora(more.pop("lora")))
        return more

    P = {n: weights[n].reshape(1, -1) for n, _ in SMALL}

    sent = {}

    def send_early(gw):
        lora = jnp.concatenate([gw[n] for n in LORA_PARTS], axis=0)
        sent["grads"] = {n: full_to_blocks(n, lora if n == "lora" else gw[n]) for n in early}
        sent["got"] = swap_halves(list(sent["grads"].values()))
        sent["handles"], token = scatter_start(chip_sums(sent["grads"], sent["got"]), "scatter_start")
        return token

    def send_late(dw_in, dw_out):
        sent["late"] = {n: full_to_blocks(n, g) for n, g in zip(FIRST_FFN, (dw_in, dw_out))}
        sent["late_got"] = swap_halves(list(sent["late"].values()))
        sent["late_handles"], token = scatter_start(chip_sums(sent["late"], sent["late_got"]), "scatter_start_ffn1")
        return token

    loss_cols, dx, gW, gP = layer_step(x[0], loss_target[0], W, P, start_token, more_weights, send_early, send_late)
    landed = split_wait(sent["handles"], gP["ffn1_norm"], "scatter_wait")
    out_g, out_d, out_m, out_v = {}, {}, {}, {}

    def apply(g_blk):
        for n in g_blk:
            res = (g_blk[n], *adamw_block(n, w_blk[n], g_blk[n], m_blk[n], v_blk[n]))
            for dst, t in zip((out_g, out_d, out_m, out_v), res):
                for part, val in (split_lora(t) if n == "lora" else {n: t}).items():
                    dst[part] = val.reshape(shapes[part])

    apply(owner_sums(sent["grads"], sent["got"], landed))
    late_landed = split_wait(sent["late_handles"], list(out_d.values())[-1], "scatter_wait_ffn1")
    apply(owner_sums(sent["late"], sent["late_got"], late_landed))

    zero_head = jnp.zeros((1, D_MODEL), F32)
    vec = pack_small(gP, loss_cols)
    loss, g_s, d_s, m_s, v_s = reduce_small(
        vec, pack_small({n: weights[n] for n, _ in SMALL}, zero_head),
        pack_small({n: mom_m[n] for n, _ in SMALL}, zero_head),
        pack_small({n: mom_v[n] for n, _ in SMALL}, zero_head))
    for dst, src in ((out_g, g_s), (out_d, d_s), (out_m, m_s), (out_v, v_s)):
        dst.update(unpack_small(src, shapes))

    return (loss[0, 0], dx[None], *[out_g[n] for n in WEIGHT_ORDER], *[out_d[n] for n in WEIGHT_ORDER],
            *[out_m[n] for n in WEIGHT_ORDER], *[out_v[n] for n in WEIGHT_ORDER])
```

```python
import functools

import jax
import jax.numpy as jnp
from jax import lax
from jax.experimental import pallas as pl
from jax.experimental.pallas import tpu as pltpu

F32 = jnp.float32
BF16 = jnp.bfloat16
MESH = pl.DeviceIdType.MESH

D_MODEL = 1024
HEAD_DIM = 64
RWKV_HEADS = 16
LORA_W, LORA_A, LORA_G = 64, 64, 160
LORA = LORA_W + LORA_A + LORA_G
RKV = 3 * D_MODEL
ATTN_PAIRS = ((128, 1), (512, 4), (2048, 16))
ATTN_BLK = 128
ATTN_HPG = 4
ATTN_WIDTH = 768
GROUP_W = ATTN_HPG * HEAD_DIM
D_FF = 2816
GN_EPS = 64e-5
RMS_EPS = 1e-6
NEG_INF = -1e30
WKV_CHUNK = 128
WKV_HEADS_PER_STEP = 16
WKV_HEADS_PER_RUN = 16

ADAM_LR, ADAM_B1, ADAM_B2, ADAM_EPS, ADAM_WD, ADAM_STEP = 0.001, 0.9, 0.999, 1e-08, 0.01, 10

V7X_VMEM_BYTES = 64 << 20
VMEM_TEMP_ALLOWANCE = 20 << 20
VMEM_LEFT_FREE = 6 << 20


def _cparams(sem, block_bytes):
    limit = min(2 * block_bytes + VMEM_TEMP_ALLOWANCE, V7X_VMEM_BYTES - VMEM_LEFT_FREE)
    return pltpu.CompilerParams(dimension_semantics=sem, vmem_limit_bytes=int(limit))


def _nbytes(shape, dtype):
    n = 1
    for s in shape:
        n *= s
    return n * jnp.dtype(dtype).itemsize


def _split_bf16(a):
    hi = a.astype(BF16)
    return hi, (a - hi.astype(F32)).astype(BF16)


def _make_dots():
    def raw(a, b, ca, cb):
        return lax.dot_general(a.astype(BF16), b.astype(BF16), (((ca,), (cb,)), ((), ())),
                               preferred_element_type=F32)

    @jax.custom_vjp
    def nn(a, b):
        return raw(a, b, 1, 0)

    @jax.custom_vjp
    def nt(a, b):
        return raw(a, b, 1, 1)

    @jax.custom_vjp
    def tn(a, b):
        return raw(a, b, 0, 0)

    nn.defvjp(lambda a, b: (raw(a, b, 1, 0), (a, b)),
              lambda res, g: (raw(g, res[1], 1, 1), raw(res[0], g, 0, 0)))
    nt.defvjp(lambda a, b: (raw(a, b, 1, 1), (a, b)),
              lambda res, g: (raw(g, res[1], 1, 0), raw(g, res[0], 0, 0)))
    tn.defvjp(lambda a, b: (raw(a, b, 0, 0), (a, b)),
              lambda res, g: (raw(res[1], g, 1, 1), raw(res[0], g, 1, 0)))
    return nn, nt, tn


def _exact_rhs_dot(x, ones, cx, co):
    hi, lo = _split_bf16(x)
    dims = (((cx,), (co,)), ((), ()))
    return (lax.dot_general(hi, ones, dims, preferred_element_type=F32)
            + lax.dot_general(lo, ones, dims, preferred_element_type=F32))


@jax.custom_vjp
def SEG(x, ones):
    return _exact_rhs_dot(x, ones, 1, 0)


SEG.defvjp(lambda x, ones: (_exact_rhs_dot(x, ones, 1, 0), ones),
           lambda ones, g: (_exact_rhs_dot(g, ones, 1, 1), jnp.zeros_like(ones)))

NN, NT, TN = _make_dots()


MM_TILE_M, MM_TILE_N, MM_TILE_K = 1408, 1408, 1536


def _pick(n, cap):
    best = None
    for t in range(128, min(n, cap) + 1, 128):
        if n % t == 0:
            best = t
    return best or n


def matmul(a, b, mode, name, *, add=None, scale=1.0):
    if mode == "nn":
        (M, K), (K2, N) = a.shape, b.shape
    elif mode == "nt":
        (M, K), (N, K2) = a.shape, b.shape
    else:
        (K, M), (K2, N) = a.shape, b.shape
    assert K == K2, (name, a.shape, b.shape)
    tm, tn, tk = _pick(M, MM_TILE_M), _pick(N, MM_TILE_N), _pick(K, MM_TILE_K)
    nk = K // tk
    ca, cb = {"nn": (1, 0), "nt": (1, 1), "tn": (0, 0)}[mode]

    def body(*refs):
        if add is None:
            a_ref, b_ref, o_ref, acc_ref = refs
        else:
            a_ref, b_ref, add_ref, o_ref, acc_ref = refs
        k = pl.program_id(2)

        @pl.when(k == 0)
        def _():
            acc_ref[...] = jnp.zeros_like(acc_ref)

        acc_ref[...] += lax.dot_general(a_ref[...].astype(BF16), b_ref[...].astype(BF16),
                                        (((ca,), (cb,)), ((), ())), preferred_element_type=F32)

        @pl.when(k == nk - 1)
        def _():
            r = acc_ref[...] * scale
            if add is not None:
                r = add_ref[...] + r
            o_ref[...] = r.astype(o_ref.dtype)

    a_spec = (pl.BlockSpec((tk, tm), lambda i, j, k: (k, i)) if mode == "tn"
              else pl.BlockSpec((tm, tk), lambda i, j, k: (i, k)))
    b_spec = (pl.BlockSpec((tn, tk), lambda i, j, k: (j, k)) if mode == "nt"
              else pl.BlockSpec((tk, tn), lambda i, j, k: (k, j)))
    in_specs, args = [a_spec, b_spec], [a, b]
    blk = tm * tk * a.dtype.itemsize + tk * tn * b.dtype.itemsize + tm * tn * 8
    if add is not None:
        in_specs.append(pl.BlockSpec((tm, tn), lambda i, j, k: (i, j)))
        args.append(add)
        blk += tm * tn * 4
    return pl.pallas_call(
        body, name=name, grid=(M // tm, N // tn, nk),
        in_specs=in_specs, out_specs=pl.BlockSpec((tm, tn), lambda i, j, k: (i, j)),
        out_shape=jax.ShapeDtypeStruct((M, N), F32),
        scratch_shapes=[pltpu.VMEM((tm, tn), F32)],
        compiler_params=_cparams(("parallel", "parallel", "arbitrary"), blk),
    )(*args)


def matmul_cs(a, w, mode, name, token):
    n_blk = N_SHARDS
    if mode == "tn":
        (K, R), Cs = a.shape, w.shape[2] // 2
        tm, tk = _pick(R, MM_TILE_M), _pick(K, 1024)
        grid = (R // tm, n_blk, K // tk)
        a_spec = pl.BlockSpec((tk, tm), lambda i, j, k: (k, i))
        w_spec = pl.BlockSpec((None, tk, Cs), lambda i, j, k: (j // 2, k, j % 2))
        o_spec = pl.BlockSpec((None, tm, Cs), lambda i, j, k: (j, i, 0))
        out_shape, acc_shape, dims = (n_blk, R, Cs), (tm, Cs), (0, 0)
        blk = tk * tm * a.dtype.itemsize + tk * Cs * w.dtype.itemsize + tm * Cs * 8
    else:
        M, (_, R, Cs) = a.shape[1], w.shape
        tm, tn = _pick(M, MM_TILE_M), _pick(R, MM_TILE_N)
        grid = (M // tm, R // tn, n_blk)
        a_spec = pl.BlockSpec((None, tm, Cs), lambda i, j, k: (k // 2, i, k % 2))
        w_spec = pl.BlockSpec((None, tn, Cs), lambda i, j, k: (k, j, 0))
        o_spec = pl.BlockSpec((tm, tn), lambda i, j, k: (i, j))
        out_shape, acc_shape, dims = (M, R), (tm, tn), (1, 1)
        blk = tm * Cs * a.dtype.itemsize + tn * Cs * w.dtype.itemsize + tm * tn * 8
    nk = grid[2]

    def body(a_ref, w_ref, tok_ref, o_ref, acc_ref):
        k = pl.program_id(2)

        @pl.when(k == 0)
        def _():
            acc_ref[...] = jnp.zeros_like(acc_ref)

        acc_ref[...] += lax.dot_general(a_ref[...].astype(BF16), w_ref[...].astype(BF16),
                                        (((dims[0],), (dims[1],)), ((), ())), preferred_element_type=F32)

        @pl.when(k == nk - 1)
        def _():
            o_ref[...] = acc_ref[...] + tok_ref[0:1, 0:1]

    return pl.pallas_call(
        body, name=name, grid=grid, in_specs=[a_spec, w_spec, pl.BlockSpec(token.shape, lambda i, j, k: (0, 0))],
        out_specs=o_spec, out_shape=jax.ShapeDtypeStruct(out_shape, F32), scratch_shapes=[pltpu.VMEM(acc_shape, F32)],
        compiler_params=_cparams(("parallel", "parallel", "arbitrary"), blk),
    )(a, w, token)


FFN_TILE_M = 512


def _swiglu(gate, up):
    return gate * jax.nn.sigmoid(gate) * up


def ffn_in_act(h, w, name):
    (M, R), Cs, half = h.shape, w.shape[2], N_SHARDS // 2
    tm, tk = _pick(M, FFN_TILE_M), _pick(R, 1024)
    nk = R // tk

    def body(h_ref, wg_ref, wu_ref, gu_ref, act_ref, acc_ref):
        k = pl.program_id(2)

        @pl.when(k == 0)
        def _():
            acc_ref[...] = jnp.zeros_like(acc_ref)

        hb = h_ref[...].astype(BF16)
        for part, w_ref in enumerate((wg_ref, wu_ref)):
            acc_ref[part] += jnp.dot(hb, w_ref[...].astype(BF16), preferred_element_type=F32)

        @pl.when(k == nk - 1)
        def _():
            gu_ref[...] = acc_ref[...]
            act_ref[...] = _swiglu(acc_ref[0], acc_ref[1]).astype(act_ref.dtype)

    w_spec = lambda off: pl.BlockSpec((None, tk, Cs), functools.partial(lambda off, j, i, k: (j + off, k, 0), off))
    blk = tm * tk * h.dtype.itemsize + 2 * tk * Cs * w.dtype.itemsize + tm * Cs * (16 + 2)
    return pl.pallas_call(
        body, name=name, grid=(half, M // tm, nk),
        in_specs=[pl.BlockSpec((tm, tk), lambda j, i, k: (i, k)), w_spec(0), w_spec(half)],
        out_specs=[pl.BlockSpec((2, tm, Cs), lambda j, i, k: (0, i, j)), pl.BlockSpec((tm, Cs), lambda j, i, k: (i, j))],
        out_shape=[jax.ShapeDtypeStruct((2, M, half * Cs), F32), jax.ShapeDtypeStruct((M, half * Cs), BF16)],
        scratch_shapes=[pltpu.VMEM((2, tm, Cs), F32)],
        compiler_params=_cparams(("parallel", "parallel", "arbitrary"), blk),
    )(h, w, w)


def ffn_dact_dgu(dy, w_out, gu, scale, name):
    (M, D), F = dy.shape, w_out.shape[0]
    tm, tn = _pick(M, FFN_TILE_M), F // 2

    def body(dy_ref, w_ref, gu_ref, dgu_ref):
        dact = scale * lax.dot_general(dy_ref[...].astype(BF16), w_ref[...].astype(BF16),
                                       (((1,), (1,)), ((), ())), preferred_element_type=F32)
        dgate, dup = jax.vjp(_swiglu, gu_ref[0], gu_ref[1])[1](dact)
        dgu_ref[0] = dgate.astype(dgu_ref.dtype)
        dgu_ref[1] = dup.astype(dgu_ref.dtype)

    pair = pl.BlockSpec((2, tm, tn), lambda j, i: (0, i, j))
    blk = tm * D * dy.dtype.itemsize + tn * D * w_out.dtype.itemsize + 2 * tm * tn * (4 + 2)
    return pl.pallas_call(
        body, name=name, grid=(F // tn, M // tm),
        in_specs=[pl.BlockSpec((tm, D), lambda j, i: (i, 0)), pl.BlockSpec((tn, D), lambda j, i: (j, 0)), pair],
        out_specs=pair, out_shape=jax.ShapeDtypeStruct((2, M, F), BF16),
        compiler_params=_cparams(("parallel", "parallel"), blk),
    )(dy, w_out, gu)


def _row_block(n, width, n_arrays):
    cap = (V7X_VMEM_BYTES // 4) // (2 * 4 * width * n_arrays)
    best = None
    for t in range(16, min(n, cap) + 1, 16):
        if n % t == 0:
            best = t
    return best or n


def placed_map(f, ins, out, *, n_blocks, tb, name):
    def body(*refs):
        refs[-1][...] = f(*[r[...] for r in refs[:-1]]).astype(refs[-1].dtype)

    def spec(fn):
        def index(i):
            x, y, c = _place()
            return fn(i, (c, 2 * x + y)), 0
        return pl.BlockSpec((tb, width), index)

    o_rows, width, o_dtype, o_fn = out
    blk = (sum(a.dtype.itemsize for a, _ in ins) + jnp.dtype(o_dtype).itemsize) * tb * width
    return pl.pallas_call(
        body, name=name, grid=(n_blocks,), in_specs=[spec(fn) for _, fn in ins], out_specs=spec(o_fn),
        out_shape=jax.ShapeDtypeStruct((o_rows, width), o_dtype),
        compiler_params=_cparams(("parallel",), blk),
    )(*[a for a, _ in ins])


def rowmap(f, rows, params, outs, accs=(), *, tb, name):
    rows = [r if isinstance(r, tuple) else (r, r.shape[1], 0) for r in rows]
    S = rows[0][0].shape[0]
    assert S % tb == 0, (name, S, tb)
    n_in, n_out = len(rows) + len(params), len(outs)

    def body(*refs):
        res = f(*[r[...] for r in refs[:n_in]])
        res = res if isinstance(res, (tuple, list)) else (res,)
        o_refs, a_refs = refs[n_in:n_in + n_out], refs[n_in + n_out:]
        for ref, val in zip(o_refs, res[:n_out]):
            ref[...] = val.astype(ref.dtype)
        if a_refs:
            @pl.when(pl.program_id(0) == 0)
            def _():
                for ref in a_refs:
                    ref[...] = jnp.zeros_like(ref)

            for ref, val in zip(a_refs, res[n_out:]):
                ref[...] += val.astype(F32)

    in_specs = [pl.BlockSpec((tb, w), functools.partial(lambda cb, i: (i, cb), cb)) for _, w, cb in rows]
    in_specs += [pl.BlockSpec(p.shape, lambda i: (0, 0)) for p in params]
    out_specs = [pl.BlockSpec((tb, w), lambda i: (i, 0)) for w, _ in outs]
    out_specs += [pl.BlockSpec(tuple(s), lambda i: (0, 0)) for s in accs]
    out_shape = [jax.ShapeDtypeStruct((S, w), dt) for w, dt in outs]
    out_shape += [jax.ShapeDtypeStruct(tuple(s), F32) for s in accs]
    blk = sum(tb * w * a.dtype.itemsize for a, w, _ in rows) + sum(_nbytes(p.shape, p.dtype) for p in params)
    blk += sum(_nbytes((tb, w), dt) for w, dt in outs) + sum(_nbytes(s, F32) for s in accs)
    res = pl.pallas_call(
        body, name=name, grid=(S // tb,), in_specs=in_specs, out_specs=out_specs, out_shape=out_shape,
        compiler_params=_cparams(("arbitrary",) if accs else ("parallel",), blk),
    )(*[r[0] for r in rows], *[pltpu.with_memory_space_constraint(p, pltpu.HBM) for p in params])
    return res


def _rms(x, g):
    return x * lax.rsqrt(jnp.mean(x * x, axis=-1, keepdims=True) + RMS_EPS) * g


def _softplus(z):
    return jnp.maximum(z, 0.0) + jnp.log(1.0 + jnp.exp(-jnp.abs(z)))


def _rwkv_pre(xrk, xlo, w0, w2p, a0, a2p, g2p, k_k, k_a, seg, seg_t):
    k = xrk[:, D_MODEL:2 * D_MODEL]
    w = -_softplus(-(w0 + NN(jnp.tanh(xlo), w2p))) - 0.5
    log_decay = -jnp.exp(w)
    a = jax.nn.sigmoid(a0 + NN(xlo, a2p))
    g = NN(jax.nn.sigmoid(xlo), g2p)
    kk = k * k_k
    norm = jnp.maximum(jnp.sqrt(SEG(kk * kk, seg)), 1e-12)
    kk = kk * SEG(1.0 / norm, seg_t)
    k_mod = k * (1.0 + (a - 1.0) * k_a)
    return log_decay, k_mod, -kk, kk * a, g


def _rwkv_post(wkv, r, k_mod, v, g, r_k, ln_w, ln_b, seg, seg_t):
    inv_n = 1.0 / HEAD_DIM
    mean = SEG(wkv, seg) * inv_n
    cen = wkv - SEG(mean, seg_t)
    var = SEG(cen * cen, seg) * inv_n
    y = cen * SEG(lax.rsqrt(var + GN_EPS), seg_t) * ln_w + ln_b
    bonus = SEG(SEG(r * k_mod * r_k, seg), seg_t) * v
    return (y + bonus) * g


def _qk_norm(q, k, q_gain, k_gain, seg, seg_t, tile_t):
    def norm(x, gain):
        mean_sq = SEG(x * x, seg) * (1.0 / HEAD_DIM)
        return x * SEG(lax.rsqrt(mean_sq + RMS_EPS), seg_t) * SEG(gain, tile_t)

    return norm(q, q_gain) * (HEAD_DIM ** -0.5), norm(k, k_gain)


def _gate_merge(pgate, pa, pb, b_gate):
    sg = jax.nn.sigmoid(pgate + b_gate)
    return sg[:, :D_MODEL] * pa + sg[:, D_MODEL:] * pb


def _group_combine(o0, o1, o2, l0, l1, l2):
    m = jnp.maximum(jnp.maximum(l0, l1), l2)
    es = [jnp.exp(l - m) for l in (l0, l1, l2)]
    den = es[0] + es[1] + es[2]
    return jnp.concatenate([o * (e / den) for o, e in zip((o0, o1, o2), es)], axis=1)


def _each(f, *xs):
    return tuple(f(*args) for args in zip(*xs))


def _attn_block(q, kc, kp, vc, vp, first):
    qi = lax.broadcasted_iota(jnp.int32, (ATTN_BLK, ATTN_BLK), 0)
    kj = lax.broadcasted_iota(jnp.int32, (ATTN_BLK, ATTN_BLK), 1)
    own = kj <= qi
    s_c = _each(lambda a, b: jnp.where(own, NT(a, b), NEG_INF), q, kc)
    s_p = _each(lambda a, b, f: jnp.where((kj >= qi) & (f < 0.5), NT(a, b), NEG_INF), q, kp, first)
    row_max = lambda s: jnp.max(s, axis=-1, keepdims=True)
    row_sum = lambda s: jnp.sum(s, axis=-1, keepdims=True)
    m = _each(lambda c_, p_: jnp.maximum(row_max(c_), row_max(p_)), s_c, s_p)
    e_c, e_p = _each(lambda s, m_: jnp.exp(s - m_), s_c, m), _each(lambda s, m_: jnp.exp(s - m_), s_p, m)
    den = _each(lambda c_, p_: row_sum(c_) + row_sum(p_), e_c, e_p)
    inv = _each(lambda d_: 1.0 / d_, den)
    o = _each(lambda ec, ep, i_, vc_, vp_: (NN(ec, vc_) + NN(ep, vp_)) * i_, e_c, e_p, inv, vc, vp)
    lse = _each(lambda m_, d_: jnp.broadcast_to(m_ + jnp.log(d_), (ATTN_BLK, HEAD_DIM)), m, den)
    return o, lse


def _attn_pair(q, k, k_before, v, v_before, first):
    n = len(q[0])
    o, lse = _attn_block(q[0] + q[1], k[0] + k[1], k_before + k[0], v[0] + v[1], v_before + v[0],
                         (first[0],) * n + (first[1],) * n)
    return (o[:n], o[n:]), (lse[:n], lse[n:])


TRI_SEED = 16


def _tri_inverse(n):
    c = n[0].shape[0]
    row = lax.broadcasted_iota(jnp.int32, (c, c), 0)
    col = lax.broadcasted_iota(jnp.int32, (c, c), 1)
    same_block = lambda size: (row >> (size.bit_length() - 1)) == (col >> (size.bit_length() - 1))
    seed = same_block(TRI_SEED)
    p = _each(lambda m: jnp.where(seed, m, 0.0), n)
    t, span = _each(lambda m: (row == col).astype(F32) + m, p), 2
    while span < TRI_SEED:
        p = _each(NN, p, p)
        t = _each(lambda t_, p_: t_ + NN(t_, p_), t, p)
        span *= 2
    size = TRI_SEED
    while size < c:
        joins = same_block(2 * size) & jnp.logical_not(same_block(size))
        t = _each(lambda t_, m: t_ + NN(NN(t_, jnp.where(joins, m, 0.0)), t_), t, n)
        size *= 2
    return t


@jax.custom_vjp
def _tri_solve(n, rhs, t):
    return _each(NN, t, rhs)


def _tri_solve_fwd(n, rhs, t):
    x = _each(NN, t, rhs)
    return x, (t, x)


def _tri_solve_bwd(res, dx):
    t, x = res
    drhs = _each(TN, t, dx)
    return _each(NT, drhs, x), drhs, _each(jnp.zeros_like, t)


_tri_solve.defvjp(_tri_solve_fwd, _tri_solve_bwd)


def _lower_ones(c):
    row = lax.broadcasted_iota(jnp.int32, (c, c), 0)
    col = lax.broadcasted_iota(jnp.int32, (c, c), 1)
    return (row >= col).astype(BF16)


def _ones_dot(ones, x, contract):
    hi, lo = _split_bf16(x)
    dims = (((contract,), (0,)), ((), ()))
    return (lax.dot_general(ones, hi, dims, preferred_element_type=F32)
            + lax.dot_general(ones, lo, dims, preferred_element_type=F32))


@jax.custom_vjp
def _cumsum_rows(x):
    return _ones_dot(_lower_ones(x.shape[0]), x, 1)


_cumsum_rows.defvjp(lambda x: (_ones_dot(_lower_ones(x.shape[0]), x, 1), None),
                    lambda _, g: (_ones_dot(_lower_ones(g.shape[0]), g, 0),))


def _wkv_chunk(s0, r, lw, k, v, a, b, t_inv=None):
    c = r[0].shape[0]
    row = lax.broadcasted_iota(jnp.int32, (c, c), 0)
    col = lax.broadcasted_iota(jnp.int32, (c, c), 1)
    strict, incl = row > col, row >= col
    cat = lambda p, q: jnp.concatenate([p, q], axis=0)
    cum = _each(_cumsum_rows, lw)
    e_neg = _each(lambda c_: jnp.exp(-c_), cum)
    ar = _each(lambda a_, r_, c_, l_: cat(a_ * jnp.exp(c_ - l_), r_ * jnp.exp(c_)), a, r, cum, lw)
    b_t, k_t = _each(jnp.multiply, b, e_neg), _each(jnp.multiply, k, e_neg)
    p_b, p_k, p_s = _each(NT, ar, b_t), _each(NT, ar, k_t), _each(NT, ar, s0)
    n_ab = _each(lambda p: jnp.where(strict, p[:c], 0.0), p_b)
    m_rb = _each(lambda p: jnp.where(incl, p[c:], 0.0), p_b)
    n_ak = _each(lambda p: jnp.where(strict, p[:c], 0.0), p_k)
    m_rk = _each(lambda p: jnp.where(incl, p[c:], 0.0), p_k)
    if t_inv is None:
        t_inv = _tri_inverse(n_ab)
    u = _tri_solve(n_ab, _each(lambda p, n_, v_: p[:c] + NN(n_, v_), p_s, n_ak, v), t_inv)
    y = _each(lambda p, mb, u_, mk, v_: p[c:] + NN(mb, u_) + NN(mk, v_), p_s, m_rb, u, m_rk, v)
    g_end = _each(lambda l_: jnp.exp(jnp.sum(l_, axis=0, keepdims=True)), lw)
    s1 = _each(lambda s_, g_, u_, v_, b_, k_: s_ * g_ + TN(cat(u_, v_), cat(b_, k_) * g_),
               s0, g_end, u, v, b_t, k_t)
    return y, s1, t_inv


def _adamw(w, g, m, v):
    m = ADAM_B1 * m + (1.0 - ADAM_B1) * g
    v = ADAM_B2 * v + (1.0 - ADAM_B2) * jnp.square(g)
    m_hat = m / (1.0 - ADAM_B1 ** ADAM_STEP)
    v_hat = v / (1.0 - ADAM_B2 ** ADAM_STEP)
    delta = -ADAM_LR * (m_hat / (jnp.sqrt(v_hat) + ADAM_EPS) + ADAM_WD * w)
    return delta, m, v


def token_shift_fwd(p, mu, *, tb, name):
    S, W = p.shape
    hb = tb // 8

    def body(p_ref, halo_ref, mu_ref, o_ref):
        i = pl.program_id(0)
        x = p_ref[...]
        before = halo_ref[7:8, :] * (i > 0).astype(F32)
        row = lax.broadcasted_iota(jnp.int32, (tb, W), 0)
        prev = jnp.where(row == 0, before, pltpu.roll(x, 1, 0))
        o_ref[...] = x + (prev - x) * mu_ref[...]

    blk = (2 * tb + 8) * W * 4
    return pl.pallas_call(
        body, name=name, grid=(S // tb,),
        in_specs=[pl.BlockSpec((tb, W), lambda i: (i, 0)),
                  pl.BlockSpec((8, W), lambda i: (jnp.maximum(i * hb - 1, 0), 0)),
                  pl.BlockSpec((1, W), lambda i: (0, 0))],
        out_specs=pl.BlockSpec((tb, W), lambda i: (i, 0)),
        out_shape=jax.ShapeDtypeStruct((S, W), F32),
        compiler_params=_cparams(("parallel",), blk),
    )(p, p, mu)


def token_shift_bwd(dxs, p, mu, *, tb, name):
    S, W = p.shape
    hb, nb = tb // 8, S // tb

    def body(d_ref, dnext_ref, p_ref, halo_ref, mu_ref, dp_ref, dmu_ref):
        i = pl.program_id(0)
        d, x, mu_v = d_ref[...], p_ref[...], mu_ref[...]
        row = lax.broadcasted_iota(jnp.int32, (tb, W), 0)
        before = halo_ref[7:8, :] * (i > 0).astype(F32)
        prev = jnp.where(row == 0, before, pltpu.roll(x, 1, 0))
        t = d * mu_v
        after = dnext_ref[0:1, :] * mu_v * (i < nb - 1).astype(F32)
        nxt = jnp.where(row == tb - 1, after, pltpu.roll(t, tb - 1, 0))
        dp_ref[...] = (d - t + nxt).astype(dp_ref.dtype)

        @pl.when(i == 0)
        def _():
            dmu_ref[...] = jnp.zeros_like(dmu_ref)

        dmu_ref[...] += jnp.sum(d * (prev - x), axis=0, keepdims=True)

    blk = (3 * tb + 16) * W * 4
    return pl.pallas_call(
        body, name=name, grid=(nb,),
        in_specs=[pl.BlockSpec((tb, W), lambda i: (i, 0)),
                  pl.BlockSpec((8, W), lambda i: (jnp.minimum((i + 1) * hb, S // 8 - 1), 0)),
                  pl.BlockSpec((tb, W), lambda i: (i, 0)),
                  pl.BlockSpec((8, W), lambda i: (jnp.maximum(i * hb - 1, 0), 0)),
                  pl.BlockSpec((1, W), lambda i: (0, 0))],
        out_specs=[pl.BlockSpec((tb, W), lambda i: (i, 0)), pl.BlockSpec((1, W), lambda i: (0, 0))],
        out_shape=[jax.ShapeDtypeStruct((S, W), BF16), jax.ShapeDtypeStruct((1, W), F32)],
        compiler_params=_cparams(("arbitrary",), blk),
    )(dxs, dxs, p, p, mu)


def _head_cols(h):
    return pl.ds(h * HEAD_DIM, HEAD_DIM)


def wkv_fwd(xs_rk, lw, k, a, b):
    S = lw.shape[0]
    C, nc, G, N = WKV_CHUNK, S // WKV_CHUNK, WKV_HEADS_PER_STEP, HEAD_DIM

    def body(r_ref, lw_ref, k_ref, v_ref, a_ref, b_ref, y_ref, st_ref, ti_ref, state):
        @pl.when(pl.program_id(1) == 0)
        def _():
            state[...] = jnp.zeros_like(state)

        for base in range(0, G, WKV_HEADS_PER_RUN):
            run = range(base, base + WKV_HEADS_PER_RUN)
            heads = lambda ref: tuple(ref[:, _head_cols(h)] for h in run)
            s0 = tuple(state[h] for h in run)
            y, s1, t_inv = _wkv_chunk(s0, heads(r_ref), heads(lw_ref), heads(k_ref), heads(v_ref), heads(a_ref),
                                      heads(b_ref))
            for i, h in enumerate(run):
                st_ref[h] = s0[i]
                ti_ref[h] = t_inv[i]
                y_ref[:, _head_cols(h)] = y[i]
                state[h] = s1[i]

    W = G * N
    seq = lambda j: pl.BlockSpec((C, W), functools.partial(lambda j, g, c: (c, j + g), j))
    per = D_MODEL // W
    per_chunk = lambda n: pl.BlockSpec((None, G, n, n), lambda g, c: (c, g, 0, 0))
    return pl.pallas_call(
        body, name="wkv_fwd", grid=(RWKV_HEADS // G, nc),
        in_specs=[seq(0), seq(0), seq(0), seq(2 * per), seq(0), seq(0)],
        out_specs=[seq(0), per_chunk(N), per_chunk(C)],
        out_shape=[jax.ShapeDtypeStruct((S, D_MODEL), F32), jax.ShapeDtypeStruct((nc, RWKV_HEADS, N, N), F32),
                   jax.ShapeDtypeStruct((nc, RWKV_HEADS, C, C), F32)],
        scratch_shapes=[pltpu.VMEM((G, N, N), F32)],
        compiler_params=_cparams(("parallel", "arbitrary"), 8 * C * W * 4 + 2 * G * N * N * 4 + G * C * C * 4),
    )(xs_rk, lw, k, xs_rk, a, b)


def wkv_bwd(xs_rk, lw, k, a, b, states, t_invs, dy):
    S = lw.shape[0]
    C, nc, G, N = WKV_CHUNK, S // WKV_CHUNK, WKV_HEADS_PER_STEP, HEAD_DIM

    def body(r_ref, lw_ref, k_ref, v_ref, a_ref, b_ref, st_ref, ti_ref, dy_ref,
             dr_ref, dlw_ref, dk_ref, dv_ref, da_ref, db_ref, dstate):
        @pl.when(pl.program_id(1) == 0)
        def _():
            dstate[...] = jnp.zeros_like(dstate)

        for base in range(0, G, WKV_HEADS_PER_RUN):
            run = range(base, base + WKV_HEADS_PER_RUN)
            heads = lambda ref: tuple(ref[:, _head_cols(h)] for h in run)
            t_inv = tuple(ti_ref[h] for h in run)
            chunk = lambda *args: _wkv_chunk(*args, t_inv)[:2]
            _, pull = jax.vjp(chunk, tuple(st_ref[h] for h in run), heads(r_ref), heads(lw_ref),
                              heads(k_ref), heads(v_ref), heads(a_ref), heads(b_ref))
            ds0, *grads = pull((heads(dy_ref), tuple(dstate[h] for h in run)))
            for i, h in enumerate(run):
                dstate[h] = ds0[i]
                for ref, grad in zip((dr_ref, dlw_ref, dk_ref, dv_ref, da_ref, db_ref), grads):
                    ref[:, _head_cols(h)] = grad[i]

    W = G * N
    seq = lambda j: pl.BlockSpec((C, W), functools.partial(lambda j, g, c: (nc - 1 - c, j + g), j))
    per = D_MODEL // W
    st = lambda n: pl.BlockSpec((None, G, n, n), lambda g, c: (nc - 1 - c, g, 0, 0))
    return pl.pallas_call(
        body, name="wkv_bwd", grid=(RWKV_HEADS // G, nc),
        in_specs=[seq(0), seq(0), seq(0), seq(2 * per), seq(0), seq(0), st(N), st(C), seq(0)],
        out_specs=[seq(0)] * 6, out_shape=[jax.ShapeDtypeStruct((S, D_MODEL), F32)] * 6,
        scratch_shapes=[pltpu.VMEM((G, N, N), F32)],
        compiler_params=_cparams(("parallel", "arbitrary"), 14 * C * W * 4 + 2 * G * N * N * 4 + G * C * C * 4),
    )(xs_rk, lw, k, xs_rk, a, b, states, t_invs, dy)


def _first_flag(i, per_seq):
    return (lax.rem(i, per_seq) == 0).astype(F32)


def _view(a):
    return a if isinstance(a, tuple) else (a, 0)


def _block_rows(half):
    return pl.ds(half * ATTN_BLK, ATTN_BLK)


def _block_heads(ref, half):
    return tuple(ref[_block_rows(half), _head_cols(h)] for h in range(ATTN_HPG))


def _pair_heads(ref):
    return _block_heads(ref, 0), _block_heads(ref, 1)


def attn_fwd(q, k, v, per_seq, name):
    (q, q_col), (k, k_col), (v, v_col) = _view(q), _view(k), _view(v)
    R, N = q.shape[0], GROUP_W
    n_pairs = R // (2 * ATTN_BLK)

    def body(q_ref, k_ref, kb_ref, v_ref, vb_ref, o_ref, lse_ref):
        pair = pl.program_id(0)
        first = (_first_flag(2 * pair, per_seq), _first_flag(2 * pair + 1, per_seq))
        o, lse = _attn_pair(_pair_heads(q_ref), _pair_heads(k_ref), _block_heads(kb_ref, 0), _pair_heads(v_ref),
                            _block_heads(vb_ref, 0), first)
        for half in range(2):
            for h in range(ATTN_HPG):
                o_ref[_block_rows(half), _head_cols(h)] = o[half][h]
                lse_ref[_block_rows(half), _head_cols(h)] = lse[half][h]

    cur = lambda col: pl.BlockSpec((2 * ATTN_BLK, N), lambda i: (i, col))
    prv = lambda col: pl.BlockSpec((ATTN_BLK, N), lambda i: (jnp.maximum(2 * i - 1, 0), col))
    return pl.pallas_call(
        body, name=name, grid=(n_pairs,), in_specs=[cur(q_col), cur(k_col), prv(k_col), cur(v_col), prv(v_col)],
        out_specs=[cur(0), cur(0)], out_shape=[jax.ShapeDtypeStruct((R, N), F32)] * 2,
        compiler_params=_cparams(("parallel",), 12 * ATTN_BLK * N * 4),
    )(q, k, k, v, v)


def attn_bwd(q, k, v, do, dlse, per_seq, name):
    views = [_view(a) for a in (q, k, v, do, dlse)]
    (q, q_col), (k, k_col), (v, v_col), (do, do_col), (dlse, dl_col) = views
    R, N = q.shape[0], GROUP_W
    n_pairs = R // (2 * ATTN_BLK)

    def body(q_ref, k_ref, kb_ref, v_ref, vb_ref, do_ref, dl_ref, dq_ref, dk_ref, dv_ref, carry_k, carry_v):
        step = pl.program_id(0)
        pair = n_pairs - 1 - step
        first = (_first_flag(2 * pair, per_seq), _first_flag(2 * pair + 1, per_seq))

        @pl.when(step == 0)
        def _():
            carry_k[...] = jnp.zeros_like(carry_k)
            carry_v[...] = jnp.zeros_like(carry_v)

        _, pull = jax.vjp(functools.partial(_attn_pair, first=first), _pair_heads(q_ref), _pair_heads(k_ref),
                          _block_heads(kb_ref, 0), _pair_heads(v_ref), _block_heads(vb_ref, 0))
        dq, dk, dk_before, dv, dv_before = pull((_pair_heads(do_ref), _pair_heads(dl_ref)))
        old_k, old_v = _block_heads(carry_k, 0), _block_heads(carry_v, 0)
        for h in range(ATTN_HPG):
            cols = _head_cols(h)
            for half in range(2):
                dq_ref[_block_rows(half), cols] = dq[half][h]
            dk_ref[_block_rows(0), cols] = dk[0][h]
            dv_ref[_block_rows(0), cols] = dv[0][h]
            dk_ref[_block_rows(1), cols] = dk[1][h] + old_k[h]
            dv_ref[_block_rows(1), cols] = dv[1][h] + old_v[h]
            carry_k[:, cols] = dk_before[h]
            carry_v[:, cols] = dv_before[h]

    cur = lambda col: pl.BlockSpec((2 * ATTN_BLK, N), lambda i: (n_pairs - 1 - i, col))
    prv = lambda col: pl.BlockSpec((ATTN_BLK, N), lambda i: (jnp.maximum(2 * (n_pairs - 1 - i) - 1, 0), col))
    return pl.pallas_call(
        body, name=name, grid=(n_pairs,),
        in_specs=[cur(q_col), cur(k_col), prv(k_col), cur(v_col), prv(v_col), cur(do_col), cur(dl_col)],
        out_specs=[cur(0)] * 3, out_shape=[jax.ShapeDtypeStruct((R, N), F32)] * 3,
        scratch_shapes=[pltpu.VMEM((ATTN_BLK, N), F32)] * 2,
        compiler_params=_cparams(("arbitrary",), 22 * ATTN_BLK * N * 4),
    )(q, k, k, v, v, do, dlse)


def by_residue(u, d):
    if d == 1:
        return u
    return u.reshape(u.shape[0] // d, d, GROUP_W).transpose(1, 0, 2).reshape(u.shape)


def by_position(u, d):
    if d == 1:
        return u
    return u.reshape(d, u.shape[0] // d, GROUP_W).transpose(1, 0, 2).reshape(u.shape)


def group_columns(t, col_block, d):
    if d == 1:
        return (t, col_block)
    return by_residue(t[:, GROUP_W * col_block:GROUP_W * (col_block + 1)], d)


def _ffn_fwd(x, norm, w_in, w_out, tag, token):
    h = rowmap(lambda x_b, g, tok: _rms(x_b, g) + tok[0:1, 0:1], [x], [norm, token], [(D_MODEL, BF16)], tb=512,
               name=tag + "_norm")[0]
    gu, act = ffn_in_act(h, w_in, tag + "_in")
    y = matmul(act, w_out, "nn", tag + "_out", add=x, scale=0.5)
    return y, (x, h, gu, act)


def _ffn_bwd(dy, saved, norm, w_in, w_out, tag, on_weight_grads):
    x, h, gu, act = saved
    no_token = jnp.zeros((8, 128), F32)
    dw_out = matmul(act, dy, "tn", tag + "_dwout", scale=0.5)
    dgu = ffn_dact_dgu(dy, w_out, gu, 0.5, tag + "_dgu")
    dw_in = matmul_cs(h, dgu, "tn", tag + "_dwin", no_token)
    dh = matmul_cs(dgu, w_in, "nt", tag + "_dh", on_weight_grads(dw_in, dw_out))

    def norm_bwd(x_b, dh_b, dy_b, g):
        dx, dg = jax.vjp(_rms, x_b, g)[1](dh_b)
        return dy_b + dx, dg

    dx, dnorm = rowmap(norm_bwd, [x, dh, dy], [norm], [(D_MODEL, F32)], [(1, D_MODEL)], tb=256,
                       name=tag + "_dnorm")
    return dx, dnorm, dw_in, dw_out


def layer_step(x, tgt, W, P, start_token, more_weights, on_mixer_grads, on_ffn1_grads):
    S = x.shape[0]
    x1, ffn1_saved = _ffn_fwd(x, P["ffn1_norm"], W["ffn1_w_in"], W["ffn1_w_out"], "ffn1", start_token)
    W = {**W, **more_weights("mixer", x1)}
    head_of = lambda n: jnp.arange(n)[:, None] // HEAD_DIM == jnp.arange(n // HEAD_DIM)[None, :]
    seg, seg_a = head_of(D_MODEL).astype(BF16), head_of(ATTN_WIDTH).astype(BF16)
    seg_t, seg_a_t = seg.T, seg_a.T
    tile_t = (jnp.arange(HEAD_DIM)[:, None] == jnp.arange(ATTN_WIDTH)[None, :] % HEAD_DIM).astype(BF16)
    qk_params = [P["attn_q_norm"], P["attn_k_norm"], seg_a, seg_a_t, tile_t]
    w_rkv, w_lora = W["w_in"][:, :RKV], W["w_in"][:, RKV:RKV + LORA]
    w_qkv = W["w_in"][:, RKV + LORA:RKV + LORA + 3 * ATTN_WIDTH]
    w_gate = W["w_in"][:, RKV + LORA + 3 * ATTN_WIDTH:]
    mu_rk, mu_lo = P["rwkv_mu"][:, :RKV], P["rwkv_mu"][:, RKV:]
    zeros = lambda n: jnp.zeros((n, D_MODEL), F32)
    w2p = jnp.concatenate([W["rwkv_w2"], zeros(LORA - LORA_W)], axis=0)
    a2p = jnp.concatenate([zeros(LORA_W), W["rwkv_a2"], zeros(LORA_G)], axis=0)
    g2p = jnp.concatenate([zeros(LORA_W + LORA_A), W["rwkv_g2"]], axis=0)
    pre_params = [P["rwkv_w0"], w2p, P["rwkv_a0"], a2p, g2p, P["rwkv_k_k"], P["rwkv_k_a"], seg, seg_t]
    post_params = [P["rwkv_r_k"], P["rwkv_ln_w"], P["rwkv_ln_b"], seg, seg_t]
    col = lambda arr, j: (arr, D_MODEL, j)

    h = rowmap(_rms, [x1], [P["mix_norm"]], [(D_MODEL, BF16)], tb=512, name="mix_norm")[0]
    p_rk = matmul(h, w_rkv, "nn", "proj_rkv")
    p_lo = matmul(h, w_lora, "nn", "proj_lora")
    p_qkv = matmul(h, w_qkv, "nn", "proj_qkv")
    p_gate = matmul(h, w_gate, "nn", "proj_gate")
    xs_rk = token_shift_fwd(p_rk, mu_rk, tb=256, name="shift_rk")
    xs_lo = token_shift_fwd(p_lo, mu_lo, tb=256, name="shift_lora")
    lw, k_mod, a_neg, b_kk, g = rowmap(
        _rwkv_pre, [xs_rk, xs_lo], pre_params, [(D_MODEL, F32)] * 5, tb=256, name="rwkv_pre")
    wkv, states, t_invs = wkv_fwd(xs_rk, lw, k_mod, a_neg, b_kk)
    post_rows = [wkv, col(xs_rk, 0), k_mod, col(xs_rk, 2), g]
    y_a = rowmap(_rwkv_post, post_rows, post_params, [(D_MODEL, BF16)], tb=256, name="rwkv_post")[0]

    qk_rows = [(p_qkv, ATTN_WIDTH, 0), (p_qkv, ATTN_WIDTH, 1)]
    qn, kn = rowmap(_qk_norm, qk_rows, qk_params, [(ATTN_WIDTH, F32)] * 2, tb=256, name="qk_norm")
    dil = [d for _, d in ATTN_PAIRS]
    groups = range(len(dil))
    per_seq = [S // d // ATTN_BLK for d in dil]
    v_first = 2 * ATTN_WIDTH // GROUP_W
    q_s = [group_columns(qn, g, dil[g]) for g in groups]
    k_s = [group_columns(kn, g, dil[g]) for g in groups]
    v_s = [group_columns(p_qkv, v_first + g, dil[g]) for g in groups]
    attn = [attn_fwd(q_s[g], k_s[g], v_s[g], per_seq[g], "attn_fwd_%d" % g) for g in groups]
    o_lse = [by_position(attn[g][j], dil[g]) for j in range(2) for g in groups]
    y_b = rowmap(_group_combine, o_lse, [], [(ATTN_WIDTH, BF16)], tb=512, name="attn_combine")[0]

    W = {**W, **more_weights("out", y_b)}
    pa = matmul(y_a, W["w_proj_rwkv"], "nn", "proj_a")
    pb = matmul(y_b, W["w_proj_attn"], "nn", "proj_b")
    merged = rowmap(_gate_merge, [p_gate, pa, pb], [P["b_gate"]], [(D_MODEL, BF16)], tb=256, name="merge")[0]
    x2 = matmul(merged, W["w_out"], "nn", "mix_out", add=x1)
    x3, ffn2_saved = _ffn_fwd(x2, P["ffn2_norm"], W["ffn2_w_in"], W["ffn2_w_out"], "ffn2",
                              jnp.zeros_like(start_token))

    def loss_head(y_b_, t_b):
        err = y_b_ - t_b
        return err * (1.0 / D_MODEL), (0.5 / D_MODEL) * jnp.sum(err * err, axis=0, keepdims=True)

    dx3, loss_cols = rowmap(loss_head, [x3, tgt], [], [(D_MODEL, F32)], [(1, D_MODEL)], tb=512, name="loss")

    gW, gP = {}, {}
    dx2, gP["ffn2_norm"], gW["ffn2_w_in"], gW["ffn2_w_out"] = _ffn_bwd(
        dx3, ffn2_saved, P["ffn2_norm"], W["ffn2_w_in"], W["ffn2_w_out"], "ffn2",
        lambda dw_in, dw_out: jnp.zeros_like(start_token))

    dmerged = matmul(dx2, W["w_out"], "nt", "d_merged")
    gW["w_out"] = matmul(merged, dx2, "tn", "dw_out")

    def merge_bwd(pg, pa_b, pb_b, dm, bg):
        return jax.vjp(_gate_merge, pg, pa_b, pb_b, bg)[1](dm)

    dp_gate, dpa, dpb, gP["b_gate"] = rowmap(
        merge_bwd, [p_gate, pa, pb, dmerged], [P["b_gate"]],
        [(2 * D_MODEL, BF16), (D_MODEL, BF16), (D_MODEL, BF16)], [(1, 2 * D_MODEL)], tb=256, name="merge_bwd")
    dy_a = matmul(dpa, W["w_proj_rwkv"], "nt", "d_ya")
    gW["w_proj_rwkv"] = matmul(y_a, dpa, "tn", "dw_proj_a")
    dy_b = matmul(dpb, W["w_proj_attn"], "nt", "d_yb")
    gW["w_proj_attn"] = matmul(y_b, dpb, "tn", "dw_proj_b")

    def combine_bwd(*blocks):
        return jax.vjp(_group_combine, *blocks[:-1])[1](blocks[-1])

    d_o_lse = rowmap(combine_bwd, o_lse + [dy_b], [], [(GROUP_W, F32)] * 6, tb=256, name="attn_combine_bwd")
    d_attn = [attn_bwd(q_s[g], k_s[g], v_s[g], by_residue(d_o_lse[g], dil[g]), by_residue(d_o_lse[3 + g], dil[g]),
                       per_seq[g], "attn_bwd_%d" % g) for g in groups]

    def qk_norm_bwd(q_b, k_b, *rest):
        dqkv, (qg, kg, sg, sgt, tl) = rest[:9], rest[9:]
        f = lambda *a: _qk_norm(*a, sg, sgt, tl)
        dqn, dkn = jnp.concatenate(dqkv[0:3], axis=1), jnp.concatenate(dqkv[3:6], axis=1)
        dq, dk, dqg, dkg = jax.vjp(f, q_b, k_b, qg, kg)[1]((dqn, dkn))
        return jnp.concatenate([dq, dk, *dqkv[6:9]], axis=1), dqg, dkg

    dp_qkv, gP["attn_q_norm"], gP["attn_k_norm"] = rowmap(
        qk_norm_bwd, qk_rows + [by_position(d_attn[g][j], dil[g]) for j in range(3) for g in groups], qk_params,
        [(3 * ATTN_WIDTH, BF16)], [(1, HEAD_DIM)] * 2, tb=256, name="qk_norm_bwd")

    def post_bwd(wkv_b, r_b, k_b, v_b, g_b, d_b, r_k, ln_w, ln_b, sg, sgt):
        f = lambda *a: _rwkv_post(*a, sg, sgt)
        return jax.vjp(f, wkv_b, r_b, k_b, v_b, g_b, r_k, ln_w, ln_b)[1](d_b)

    dwkv, dr_p, dk_p, dv_p, dg, gP["rwkv_r_k"], gP["rwkv_ln_w"], gP["rwkv_ln_b"] = rowmap(
        post_bwd, post_rows + [dy_a], post_params, [(D_MODEL, F32)] * 5, [(1, D_MODEL)] * 3, tb=128,
        name="rwkv_post_bwd")
    dr_w, dlw, dk_w, dv_w, da_neg, db_kk = wkv_bwd(xs_rk, lw, k_mod, a_neg, b_kk, states, t_invs, dwkv)

    def pre_bwd(xrk_b, xlo_b, dlw_b, dkw_b, dkp_b, da_b, db_b, dg_b, drp_b, drw_b, dvp_b, dvw_b,
                w0, w2, a0, a2, g2, k_k, k_a, sg, sgt):
        f = lambda *a: _rwkv_pre(*a, sg, sgt)
        pull = jax.vjp(f, xrk_b, xlo_b, w0, w2, a0, a2, g2, k_k, k_a)[1]
        dxrk, dxlo, *dpar = pull((dlw_b, dkw_b + dkp_b, da_b, db_b, dg_b))
        direct = jnp.concatenate([drp_b + drw_b, jnp.zeros_like(drp_b), dvp_b + dvw_b], axis=1)
        return (dxrk + direct, dxlo, *dpar)

    pre_rows = [xs_rk, xs_lo, dlw, dk_w, dk_p, da_neg, db_kk, dg, dr_p, dr_w, dv_p, dv_w]
    dxs_rk, dxs_lo, gP["rwkv_w0"], dw2p, gP["rwkv_a0"], da2p, dg2p, gP["rwkv_k_k"], gP["rwkv_k_a"] = rowmap(
        pre_bwd, pre_rows, pre_params, [(RKV, F32), (LORA, F32)],
        [(1, D_MODEL), (LORA, D_MODEL), (1, D_MODEL), (LORA, D_MODEL), (LORA, D_MODEL), (1, D_MODEL), (1, D_MODEL)],
        tb=128, name="rwkv_pre_bwd")
    gW["rwkv_w2"] = dw2p[:LORA_W]
    gW["rwkv_a2"] = da2p[LORA_W:LORA_W + LORA_A]
    gW["rwkv_g2"] = dg2p[LORA_W + LORA_A:]
    dp_rk, dmu_rk = token_shift_bwd(dxs_rk, p_rk, mu_rk, tb=256, name="shift_rk_bwd")
    dp_lo, dmu_lo = token_shift_bwd(dxs_lo, p_lo, mu_lo, tb=256, name="shift_lora_bwd")
    gP["rwkv_mu"] = jnp.concatenate([dmu_rk, dmu_lo], axis=1)

    dh = matmul(dp_rk, w_rkv, "nt", "dh_rkv")
    dh = matmul(dp_lo, w_lora, "nt", "dh_lora", add=dh)
    dh = matmul(dp_qkv, w_qkv, "nt", "dh_qkv", add=dh)
    dh = matmul(dp_gate, w_gate, "nt", "dh_gate", add=dh)
    gW["w_in"] = jnp.concatenate([
        matmul(h, dp_rk, "tn", "dw_rkv"), matmul(h, dp_lo, "tn", "dw_lora"),
        matmul(h, dp_qkv, "tn", "dw_qkv"), matmul(h, dp_gate, "tn", "dw_gate")], axis=1)

    token = on_mixer_grads(gW)

    def norm_bwd(x_b, dh_b, dy_b, gn, tok):
        dx, dgn = jax.vjp(_rms, x_b, gn)[1](dh_b)
        return dy_b + dx + tok[0:1, 0:1], dgn

    dx1, gP["mix_norm"] = rowmap(norm_bwd, [x1, dh, dx2], [P["mix_norm"], token], [(D_MODEL, F32)],
                                 [(1, D_MODEL)], tb=256, name="mix_norm_bwd")
    dx, gP["ffn1_norm"], gW["ffn1_w_in"], gW["ffn1_w_out"] = _ffn_bwd(
        dx1, ffn1_saved, P["ffn1_norm"], W["ffn1_w_in"], W["ffn1_w_out"], "ffn1", on_ffn1_grads)
    return loss_cols, dx, gW, gP


N_SHARDS = 4
OTHER_CHIPS = N_SHARDS - 1
BIG = (("ffn1_w_in", (D_MODEL, 2 * D_FF), 1), ("ffn1_w_out", (D_FF, D_MODEL), 0),
       ("w_in", (D_MODEL, 7712), 1), ("rwkv_w2", (LORA_W, D_MODEL), 1), ("rwkv_a2", (LORA_A, D_MODEL), 1),
       ("rwkv_g2", (LORA_G, D_MODEL), 1), ("w_proj_rwkv", (D_MODEL, D_MODEL), 0),
       ("w_proj_attn", (ATTN_WIDTH, D_MODEL), 1), ("w_out", (D_MODEL, D_MODEL), 0),
       ("ffn2_w_in", (D_MODEL, 2 * D_FF), 1), ("ffn2_w_out", (D_FF, D_MODEL), 0))
SMALL = (("ffn1_norm", 1024), ("mix_norm", 1024), ("b_gate", 2048), ("rwkv_mu", 3360), ("rwkv_w0", 1024),
         ("rwkv_a0", 1024), ("rwkv_k_k", 1024), ("rwkv_k_a", 1024), ("rwkv_r_k", 1024), ("rwkv_ln_w", 1024),
         ("rwkv_ln_b", 1024), ("attn_q_norm", 64), ("attn_k_norm", 64), ("ffn2_norm", 1024))
WEIGHT_ORDER = ("ffn1_norm", "ffn1_w_in", "ffn1_w_out", "mix_norm", "w_in", "b_gate", "rwkv_mu", "rwkv_w0",
                "rwkv_w2", "rwkv_a0", "rwkv_a2", "rwkv_g2", "rwkv_k_k", "rwkv_k_a", "rwkv_r_k", "rwkv_ln_w",
                "rwkv_ln_b", "attn_q_norm", "attn_k_norm", "w_proj_rwkv", "w_proj_attn", "w_out", "ffn2_norm",
                "ffn2_w_in", "ffn2_w_out")


LORA_PARTS = ("rwkv_w2", "rwkv_a2", "rwkv_g2")
BLOCK_MAJOR = ("ffn1_w_in", "ffn2_w_in")
FIRST_FFN = ("ffn1_w_in", "ffn1_w_out")
MIXER_IN = ("w_in", "lora")
SMALL_USED = D_MODEL + sum(n for _, n in SMALL)
SMALL_W = -(-SMALL_USED // 128) * 128


def _travel():
    out = {}
    for name, shape, axis in BIG:
        if name == LORA_PARTS[0]:
            out["lora"] = ((LORA, D_MODEL), 1)
        elif name not in LORA_PARTS:
            out[name] = (shape, axis)
    return out


def local_blocks(vals):
    out = {n: vals[n] for n in _travel() if n != "lora"}
    out["lora"] = jnp.concatenate([vals[n] for n in LORA_PARTS], axis=0)
    return out


def split_lora(t):
    return {"rwkv_w2": t[:LORA_W], "rwkv_a2": t[LORA_W:LORA_W + LORA_A], "rwkv_g2": t[LORA_W + LORA_A:]}


def blocks_to_full(name, blocks):
    shape, axis = _travel()[name]
    if name in BLOCK_MAJOR:
        return blocks
    if axis == 0:
        return blocks.reshape(shape)
    return blocks.transpose(1, 0, 2).reshape(shape)


def full_to_blocks(name, full):
    shape, axis = _travel()[name]
    if name in BLOCK_MAJOR:
        return full
    if axis == 0:
        return full.reshape(N_SHARDS, shape[0] // N_SHARDS, shape[1])
    return full.reshape(shape[0], N_SHARDS, shape[1] // N_SHARDS).transpose(1, 0, 2)


def pack_small(vals, head):
    parts = [head] + [vals[name].reshape(1, n) for name, n in SMALL]
    parts.append(jnp.zeros((1, SMALL_W - SMALL_USED), F32))
    return jnp.concatenate(parts, axis=1)


def unpack_small(vec, shapes):
    out, off = {}, D_MODEL
    for name, n in SMALL:
        out[name] = vec[:, off:off + n].reshape(shapes[name])
        off += n
    return out


def _place():
    return lax.axis_index("x"), lax.axis_index("y"), lax.axis_index("c")


def _other_chips(x, y):
    return [(1 - x, y), (x, 1 - y), (1 - x, 1 - y)]


def _remote(src, dst, send_sem, recv_sem, device):
    return pltpu.make_async_remote_copy(src_ref=src, dst_ref=dst, send_sem=send_sem, recv_sem=recv_sem,
                                        device_id=device, device_id_type=MESH)


def _half(ref, who):
    hr = ref.shape[-2] // 2
    rows = pl.ds(pl.multiple_of(who * hr, 8), hr)
    return ref.at[rows] if len(ref.shape) == 2 else ref.at[:, rows]


HBM_REF = pl.BlockSpec(memory_space=pl.ANY)
COMM_PARAMS = dict(compiler_params=pltpu.CompilerParams(has_side_effects=True))


def gather_weights(blocks):
    n = len(blocks)

    def body(*refs):
        ins, outs = refs[:n], refs[n:2 * n]
        ici_send, ici_recv, d2d_send, d2d_recv = refs[2 * n:]
        x, y, c = _place()
        me, sibling, chips = 2 * x + y, (x, y, 1 - c), _other_chips(x, y)
        first = [_remote(_half(ins[t], c), _half(outs[t].at[me], c), ici_send.at[k, t], ici_recv.at[k, t],
                         (px, py, c)) for k, (px, py) in enumerate(chips) for t in range(n)]
        for cp in first:
            cp.start()
        passed = []
        for k, (px, py) in enumerate(chips):
            for t in range(n):
                landed = _half(outs[t].at[2 * px + py], c)
                _remote(landed, landed, ici_send.at[k, t], ici_recv.at[k, t], (px, py, c)).wait_recv()
                cp = _remote(landed, landed, d2d_send.at[k, t], d2d_recv.at[k, t], sibling)
                cp.start()
                passed.append(cp)
        for k, (px, py) in enumerate(chips):
            for t in range(n):
                other = _half(outs[t].at[2 * px + py], 1 - c)
                _remote(other, other, d2d_send.at[k, t], d2d_recv.at[k, t], sibling).wait_recv()
        for cp in first + passed:
            cp.wait_send()

    res = pl.pallas_call(
        body, name="gather_weights", in_specs=[HBM_REF] * n, out_specs=[HBM_REF] * n,
        out_shape=[jax.ShapeDtypeStruct((N_SHARDS,) + b.shape, b.dtype) for b in blocks],
        scratch_shapes=[pltpu.SemaphoreType.DMA((3, n))] * 4, **COMM_PARAMS)(*blocks)
    me = 2 * lax.axis_index("x") + lax.axis_index("y")
    return [lax.dynamic_update_slice(g, b[None], (me, 0, 0)) for g, b in zip(res, blocks)]


def _gather_copies(ins, outs, send_sem, recv_sem):
    x, y, c = _place()
    return [_remote(_half(ins[t], c), _half(outs[t].at[2 * x + y], c), send_sem(k, t), recv_sem(k, t), (px, py, c))
            for k, (px, py) in enumerate(_other_chips(x, y)) for t in range(len(ins))]


def split_start(copies, sources, landing_shapes, name):
    n = len(sources)
    n_cp = OTHER_CHIPS * n

    def body(*refs):
        srcs, dsts = refs[:n], refs[n:2 * n]
        sems, token = refs[2 * n:2 * n + 2 * n_cp], refs[-1]
        for cp in copies(srcs, dsts, lambda k, t: sems[k * n + t], lambda k, t: sems[n_cp + k * n + t]):
            cp.start()
        token[...] = jnp.zeros_like(token)

    hbm = lambda a: pltpu.with_memory_space_constraint(a, pltpu.HBM)
    buffers = list(sources) + [lax.empty(shape, s.dtype) for shape, s in zip(landing_shapes, sources)]
    res = pl.pallas_call(
        body, name=name,
        out_shape=(*[pltpu.SemaphoreType.DMA(())] * (2 * n_cp),
                   *[pltpu.HBM(a.shape, a.dtype) for a in buffers], jax.ShapeDtypeStruct((8, 128), F32)),
        in_specs=[SPLIT_HBM] * (2 * n),
        out_specs=(*[SPLIT_SEM] * (2 * n_cp), *[SPLIT_HBM] * (2 * n), pl.BlockSpec(memory_space=pltpu.VMEM)),
        input_output_aliases={t: 2 * n_cp + t for t in range(2 * n)}, **SPLIT_PARAMS,
    )(*[hbm(a) for a in buffers])
    return (copies, n, res[:-1]), res[-1]


def split_wait(handles, after, name):
    copies, n, held = handles
    n_cp = OTHER_CHIPS * n
    sems, thru = held[:2 * n_cp], held[2 * n_cp:]

    def body(*refs):
        srcs, dsts = refs[:n], refs[n:2 * n]
        sem_refs = refs[2 * n:2 * n + 2 * n_cp]
        for cp in copies(srcs, dsts, lambda k, t: sem_refs[k * n + t], lambda k, t: sem_refs[n_cp + k * n + t]):
            cp.wait_send()
            cp.wait_recv()

    res = pl.pallas_call(
        body, name=name, out_shape=tuple(pltpu.HBM(a.shape, a.dtype) for a in thru),
        in_specs=[SPLIT_HBM] * (2 * n) + [SPLIT_SEM] * (2 * n_cp) + [pl.BlockSpec(memory_space=pl.ANY)],
        out_specs=tuple([SPLIT_HBM] * (2 * n)), input_output_aliases={t: t for t in range(2 * n)}, **SPLIT_PARAMS,
    )(*thru, *sems, after)
    return list(res[n:])


def gather_start(blocks, name):
    return split_start(_gather_copies, blocks, [(N_SHARDS,) + b.shape for b in blocks], name)


def pass_halves(gathered, blocks, name):
    n = len(gathered)

    def body(*refs):
        outs = refs[n:2 * n]
        send_sems, recv_sems = refs[2 * n:]
        x, y, c = _place()
        slots = [2 * px + py for px, py in _other_chips(x, y)]
        give = [_remote(_half(outs[t].at[s], c), _half(outs[t].at[s], c), send_sems.at[k, t], recv_sems.at[k, t],
                        (x, y, 1 - c)) for k, s in enumerate(slots) for t in range(n)]
        for cp in give:
            cp.start()
        for k, s in enumerate(slots):
            for t in range(n):
                other = _half(outs[t].at[s], 1 - c)
                _remote(other, other, send_sems.at[k, t], recv_sems.at[k, t], (x, y, 1 - c)).wait_recv()
        for cp in give:
            cp.wait_send()

    res = pl.pallas_call(
        body, name=name, in_specs=[HBM_REF] * n, out_specs=[HBM_REF] * n,
        out_shape=[jax.ShapeDtypeStruct(g.shape, g.dtype) for g in gathered],
        input_output_aliases={t: t for t in range(n)},
        scratch_shapes=[pltpu.SemaphoreType.DMA((3, n))] * 2, **COMM_PARAMS)(*gathered)
    me = 2 * lax.axis_index("x") + lax.axis_index("y")
    return [lax.dynamic_update_slice(g, b[None], (me, 0, 0)) for g, b in zip(res, blocks)]


def swap_halves(grads):
    n = len(grads)

    def body(*refs):
        ins, got = refs[:n], refs[n:2 * n]
        send_sems, recv_sems = refs[2 * n:]
        x, y, c = _place()
        give = [_remote(_half(ins[t], 1 - c), got[t], send_sems.at[t], recv_sems.at[t], (x, y, 1 - c))
                for t in range(n)]
        for cp in give:
            cp.start()
        for cp in give:
            cp.wait_recv()
        for cp in give:
            cp.wait_send()

    return pl.pallas_call(
        body, name="swap_halves", in_specs=[HBM_REF] * n, out_specs=[HBM_REF] * n,
        out_shape=[jax.ShapeDtypeStruct((g.shape[0], g.shape[1] // 2, g.shape[2]), g.dtype) for g in grads],
        scratch_shapes=[pltpu.SemaphoreType.DMA((n,))] * 2, **COMM_PARAMS)(*grads)


def join_halves(blocks):
    n = len(blocks)

    def body(*refs):
        outs = refs[n:2 * n]
        send_sems, recv_sems = refs[2 * n:]
        x, y, c = _place()
        give = [_remote(_half(outs[t], c), _half(outs[t], c), send_sems.at[t], recv_sems.at[t], (x, y, 1 - c))
                for t in range(n)]
        for cp in give:
            cp.start()
        for t in range(n):
            arriving = _half(outs[t], 1 - c)
            _remote(arriving, arriving, send_sems.at[t], recv_sems.at[t], (x, y, 1 - c)).wait_recv()
        for cp in give:
            cp.wait_send()

    return pl.pallas_call(
        body, name="join_halves", in_specs=[HBM_REF] * n, out_specs=[HBM_REF] * n,
        out_shape=[jax.ShapeDtypeStruct(b.shape, b.dtype) for b in blocks],
        input_output_aliases={t: t for t in range(n)},
        scratch_shapes=[pltpu.SemaphoreType.DMA((n,))] * 2, **COMM_PARAMS)(*blocks)


SPLIT_HBM = pl.BlockSpec(memory_space=pltpu.HBM)
SPLIT_SEM = pl.BlockSpec(memory_space=pltpu.SEMAPHORE)
SPLIT_PARAMS = dict(compiler_params=pltpu.CompilerParams(has_side_effects=pltpu.SideEffectType.DATAFLOW_SIDE_EFFECTING))


def _scatter_copies(parts, landed, send_sem, recv_sem):
    x, y, c = _place()
    return [_remote(parts[t].at[2 * px + py], landed[t].at[k], send_sem(k, t), recv_sem(k, t), (px, py, c))
            for k, (px, py) in enumerate(_other_chips(x, y)) for t in range(len(parts))]


def scatter_start(partials, name):
    return split_start(_scatter_copies, partials, [(OTHER_CHIPS,) + p.shape[1:] for p in partials], name)


def chip_sums(grads, got):
    names = list(grads)
    partials = []
    for name, theirs in zip(names, got):
        n_slot, hr, width = theirs.shape
        tb = _row_block(hr, width, 6)
        per_half = hr // tb
        mine = lambda i, s, per_half=per_half: (i // per_half) * 2 * per_half + s[0] * per_half + i % per_half
        p = placed_map(
            jnp.add,
            [(grads[name].reshape(2 * n_slot * hr, width), mine), (theirs.reshape(n_slot * hr, width), lambda i, s: i)],
            (n_slot * hr, width, BF16, lambda i, s: i), n_blocks=n_slot * per_half, tb=tb, name="chip_sum_" + name)
        partials.append(p.reshape(theirs.shape))
    return partials


def owner_sums(grads, got, landed):
    names = list(grads)
    blocks = []
    for name, theirs, arrived in zip(names, got, landed):
        n_slot, hr, width = theirs.shape
        tb = _row_block(hr, width, 6)
        per_half = hr // tb
        views = [(grads[name].reshape(2 * n_slot * hr, width),
                  lambda i, s, per_half=per_half: s[1] * 2 * per_half + s[0] * per_half + i),
                 (theirs.reshape(n_slot * hr, width), lambda i, s, per_half=per_half: s[1] * per_half + i)]
        views += [(arrived.reshape(3 * hr, width), functools.partial(lambda k, per_half, i, s: k * per_half + i,
                                                                     k, per_half)) for k in range(3)]
        f = lambda a, b, l0, l1, l2: (((a + b) + l0.astype(F32)) + l1.astype(F32)) + l2.astype(F32)
        blocks.append(placed_map(
            f, views,(2 * hr, width, F32, lambda i, s, per_half=per_half: s[0] * per_half + i),
            n_blocks=per_half, tb=tb, name="owner_sum_" + name))
    return dict(zip(names, join_halves(blocks)))


def adamw_block(name, w, g, m, v):
    rows, width = w.shape
    return rowmap(_adamw, [w, g, m, v], [], [(width, F32)] * 3, tb=_row_block(rows, width, 7),
                  name="adamw_" + name)


def reduce_small(vec, w, m, v):
    n_dev = 8

    def body(vec_ref, w_ref, m_ref, v_ref, loss_ref, g_ref, d_ref, m2_ref, v2_ref, slots, send_sems, recv_sems):
        x, y, c = _place()
        me = 4 * x + 2 * y + c
        slots[me] = vec_ref[...]
        flips = [(fx, fy, fc) for fx in (0, 1) for fy in (0, 1) for fc in (0, 1)][1:]
        peers = [(1 - x if fx else x, 1 - y if fy else y, 1 - c if fc else c) for fx, fy, fc in flips]
        sends = [pltpu.make_async_remote_copy(
            src_ref=vec_ref, dst_ref=slots.at[me], send_sem=send_sems.at[j], recv_sem=recv_sems.at[j],
            device_id=peer, device_id_type=MESH) for j, peer in enumerate(peers)]
        for cp in sends:
            cp.start()
        for j, (px, py, pc) in enumerate(peers):
            pltpu.make_async_remote_copy(
                src_ref=vec_ref, dst_ref=slots.at[4 * px + 2 * py + pc], send_sem=send_sems.at[j],
                recv_sem=recv_sems.at[j], device_id=(px, py, pc), device_id_type=MESH).wait_recv()
        for cp in sends:
            cp.wait_send()
        g = slots[0]
        for d in range(1, n_dev):
            g = g + slots[d]
        loss_ref[...] = jnp.sum(g[:, :D_MODEL], axis=1, keepdims=True)
        delta, m2, v2 = _adamw(w_ref[...], g, m_ref[...], v_ref[...])
        g_ref[...], d_ref[...], m2_ref[...], v2_ref[...] = g, delta, m2, v2

    vm = pl.BlockSpec(memory_space=pltpu.VMEM)
    vec_t = jax.ShapeDtypeStruct(vec.shape, F32)
    return pl.pallas_call(
        body, name="reduce_small", in_specs=[vm] * 4, out_specs=[vm] * 5,
        out_shape=[jax.ShapeDtypeStruct((1, 1), F32)] + [vec_t] * 4,
        scratch_shapes=[pltpu.VMEM((n_dev,) + vec.shape, F32), pltpu.SemaphoreType.DMA((n_dev - 1,)),
                        pltpu.SemaphoreType.DMA((n_dev - 1,))],
        compiler_params=pltpu.CompilerParams(has_side_effects=True),
    )(vec, w, m, v)


def kernel(x, ffn1_norm, ffn1_w_in, ffn1_w_out, mix_norm, w_in, b_gate, rwkv_mu, rwkv_w0, rwkv_w2, rwkv_a0, rwkv_a2, rwkv_g2, rwkv_k_k, rwkv_k_a, rwkv_r_k, rwkv_ln_w, rwkv_ln_b, attn_q_norm, attn_k_norm, w_proj_rwkv, w_proj_attn, w_out, ffn2_norm, ffn2_w_in, ffn2_w_out, loss_target, m_ffn1_norm, m_ffn1_w_in, m_ffn1_w_out, m_mix_norm, m_w_in, m_b_gate, m_rwkv_mu, m_rwkv_w0, m_rwkv_w2, m_rwkv_a0, m_rwkv_a2, m_rwkv_g2, m_rwkv_k_k, m_rwkv_k_a, m_rwkv_r_k, m_rwkv_ln_w, m_rwkv_ln_b, m_attn_q_norm, m_attn_k_norm, m_w_proj_rwkv, m_w_proj_attn, m_w_out, m_ffn2_norm, m_ffn2_w_in, m_ffn2_w_out, v_ffn1_norm, v_ffn1_w_in, v_ffn1_w_out, v_mix_norm, v_w_in, v_b_gate, v_rwkv_mu, v_rwkv_w0, v_rwkv_w2, v_rwkv_a0, v_rwkv_a2, v_rwkv_g2, v_rwkv_k_k, v_rwkv_k_a, v_rwkv_r_k, v_rwkv_ln_w, v_rwkv_ln_b, v_attn_q_norm, v_attn_k_norm, v_w_proj_rwkv, v_w_proj_attn, v_w_out, v_ffn2_norm, v_ffn2_w_in, v_ffn2_w_out):
    given = dict(locals())
    weights = {n: given[n] for n in WEIGHT_ORDER}
    mom_m = {n: given["m_" + n] for n in WEIGHT_ORDER}
    mom_v = {n: given["v_" + n] for n in WEIGHT_ORDER}
    big = [name for name, _, _ in BIG]
    shapes = {n: weights[n].shape for n in WEIGHT_ORDER}
    blocks_of = lambda d: local_blocks({n: d[n][0] for n in big})
    w_blk, m_blk, v_blk = blocks_of(weights), blocks_of(mom_m), blocks_of(mom_v)
    names = list(w_blk)

    early = [n for n in names if n not in FIRST_FFN]
    bf16_block = lambda n: w_blk[n].astype(BF16)
    W = {n: blocks_to_full(n, g) for n, g in zip(FIRST_FFN, gather_weights([bf16_block(n) for n in FIRST_FFN]))}
    stages = {"mixer": [n for n in early if n in MIXER_IN], "out": [n for n in early if n not in MIXER_IN]}
    stage_blocks = {s: [bf16_block(n) for n in stages[s]] for s in stages}
    started = {s: gather_start(stage_blocks[s], "gather_start_" + s) for s in ("mixer", "out")}
    start_token = started["mixer"][1] + started["out"][1]

    def more_weights(stage, after):
        landed = split_wait(started[stage][0], after, "gather_wait_" + stage)
        got = pass_halves(landed, stage_blocks[stage], "pass_halves_" + stage)
        more = {n: blocks_to_full(n, g) for n, g in zip(stages[stage], got)}
        if "lora" in more:
            more.update(split_lora(more.pop("lora")))
        return more

    P = {n: weights[n].reshape(1, -1) for n, _ in SMALL}

    sent = {}

    def send_early(gw):
        lora = jnp.concatenate([gw[n] for n in LORA_PARTS], axis=0)
        sent["grads"] = {n: full_to_blocks(n, lora if n == "lora" else gw[n]) for n in early}
        sent["got"] = swap_halves(list(sent["grads"].values()))
        sent["handles"], token = scatter_start(chip_sums(sent["grads"], sent["got"]), "scatter_start")
        return token

    def send_late(dw_in, dw_out):
        sent["late"] = {n: full_to_blocks(n, g) for n, g in zip(FIRST_FFN, (dw_in, dw_out))}
        sent["late_got"] = swap_halves(list(sent["late"].values()))
        sent["late_handles"], token = scatter_start(chip_sums(sent["late"], sent["late_got"]), "scatter_start_ffn1")
        return token

    loss_cols, dx, gW, gP = layer_step(x[0], loss_target[0], W, P, start_token, more_weights, send_early, send_late)
    landed = split_wait(sent["handles"], gP["ffn1_norm"], "scatter_wait")
    out_g, out_d, out_m, out_v = {}, {}, {}, {}

    def apply(g_blk):
        for n in g_blk:
            res = (g_blk[n], *adamw_block(n, w_blk[n], g_blk[n], m_blk[n], v_blk[n]))
            for dst, t in zip((out_g, out_d, out_m, out_v), res):
                for part, val in (split_lora(t) if n == "lora" else {n: t}).items():
                    dst[part] = val.reshape(shapes[part])

    apply(owner_sums(sent["grads"], sent["got"], landed))
    late_landed = split_wait(sent["late_handles"], list(out_d.values())[-1], "scatter_wait_ffn1")
    apply(owner_sums(sent["late"], sent["late_got"], late_landed))

    zero_head = jnp.zeros((1, D_MODEL), F32)
    vec = pack_small(gP, loss_cols)
    loss, g_s, d_s, m_s, v_s = reduce_small(
        vec, pack_small({n: weights[n] for n, _ in SMALL}, zero_head),
        pack_small({n: mom_m[n] for n, _ in SMALL}, zero_head),
        pack_small({n: mom_v[n] for n, _ in SMALL}, zero_head))
    for dst, src in ((out_g, g_s), (out_d, d_s), (out_m, m_s), (out_v, v_s)):
        dst.update(unpack_small(src, shapes))

    return (loss[0, 0], dx[None], *[out_g[n] for n in WEIGHT_ORDER], *[out_d[n] for n in WEIGHT_ORDER],
            *[out_m[n] for n in WEIGHT_ORDER], *[out_v[n] for n in WEIGHT_ORDER])
```

```python
import functools

import jax
import jax.numpy as jnp
from jax import lax
from jax.experimental import pallas as pl
from jax.experimental.pallas import tpu as pltpu

F32 = jnp.float32
BF16 = jnp.bfloat16
MESH = pl.DeviceIdType.MESH

D_MODEL = 1024
HEAD_DIM = 64
RWKV_HEADS = 16
LORA_W, LORA_A, LORA_G = 64, 64, 160
LORA = LORA_W + LORA_A + LORA_G
RKV = 3 * D_MODEL
ATTN_PAIRS = ((128, 1), (512, 4), (2048, 16))
ATTN_BLK = 128
ATTN_HPG = 4
ATTN_WIDTH = 768
GROUP_W = ATTN_HPG * HEAD_DIM
D_FF = 2816
GN_EPS = 64e-5
RMS_EPS = 1e-6
NEG_INF = -1e30
WKV_CHUNK = 128
WKV_HEADS_PER_STEP = 16
WKV_HEADS_PER_RUN = 16

ADAM_LR, ADAM_B1, ADAM_B2, ADAM_EPS, ADAM_WD, ADAM_STEP = 0.001, 0.9, 0.999, 1e-08, 0.01, 10

V7X_VMEM_BYTES = 64 << 20
VMEM_TEMP_ALLOWANCE = 20 << 20
VMEM_LEFT_FREE = 6 << 20


def _cparams(sem, block_bytes):
    limit = min(2 * block_bytes + VMEM_TEMP_ALLOWANCE, V7X_VMEM_BYTES - VMEM_LEFT_FREE)
    return pltpu.CompilerParams(dimension_semantics=sem, vmem_limit_bytes=int(limit))


def _nbytes(shape, dtype):
    n = 1
    for s in shape:
        n *= s
    return n * jnp.dtype(dtype).itemsize


def _split_bf16(a):
    hi = a.astype(BF16)
    return hi, (a - hi.astype(F32)).astype(BF16)


def _make_dots():
    def raw(a, b, ca, cb):
        return lax.dot_general(a.astype(BF16), b.astype(BF16), (((ca,), (cb,)), ((), ())),
                               preferred_element_type=F32)

    @jax.custom_vjp
    def nn(a, b):
        return raw(a, b, 1, 0)

    @jax.custom_vjp
    def nt(a, b):
        return raw(a, b, 1, 1)

    @jax.custom_vjp
    def tn(a, b):
        return raw(a, b, 0, 0)

    nn.defvjp(lambda a, b: (raw(a, b, 1, 0), (a, b)),
              lambda res, g: (raw(g, res[1], 1, 1), raw(res[0], g, 0, 0)))
    nt.defvjp(lambda a, b: (raw(a, b, 1, 1), (a, b)),
              lambda res, g: (raw(g, res[1], 1, 0), raw(g, res[0], 0, 0)))
    tn.defvjp(lambda a, b: (raw(a, b, 0, 0), (a, b)),
              lambda res, g: (raw(res[1], g, 1, 1), raw(res[0], g, 1, 0)))
    return nn, nt, tn


def _exact_rhs_dot(x, ones, cx, co):
    hi, lo = _split_bf16(x)
    dims = (((cx,), (co,)), ((), ()))
    return (lax.dot_general(hi, ones, dims, preferred_element_type=F32)
            + lax.dot_general(lo, ones, dims, preferred_element_type=F32))


@jax.custom_vjp
def SEG(x, ones):
    return _exact_rhs_dot(x, ones, 1, 0)


SEG.defvjp(lambda x, ones: (_exact_rhs_dot(x, ones, 1, 0), ones),
           lambda ones, g: (_exact_rhs_dot(g, ones, 1, 1), jnp.zeros_like(ones)))

NN, NT, TN = _make_dots()


MM_TILE_M, MM_TILE_N, MM_TILE_K = 1408, 1408, 1536


def _pick(n, cap):
    best = None
    for t in range(128, min(n, cap) + 1, 128):
        if n % t == 0:
            best = t
    return best or n


def matmul(a, b, mode, name, *, add=None, scale=1.0):
    if mode == "nn":
        (M, K), (K2, N) = a.shape, b.shape
    elif mode == "nt":
        (M, K), (N, K2) = a.shape, b.shape
    else:
        (K, M), (K2, N) = a.shape, b.shape
    assert K == K2, (name, a.shape, b.shape)
    tm, tn, tk = _pick(M, MM_TILE_M), _pick(N, MM_TILE_N), _pick(K, MM_TILE_K)
    nk = K // tk
    ca, cb = {"nn": (1, 0), "nt": (1, 1), "tn": (0, 0)}[mode]

    def body(*refs):
        if add is None:
            a_ref, b_ref, o_ref, acc_ref = refs
        else:
            a_ref, b_ref, add_ref, o_ref, acc_ref = refs
        k = pl.program_id(2)

        @pl.when(k == 0)
        def _():
            acc_ref[...] = jnp.zeros_like(acc_ref)

        acc_ref[...] += lax.dot_general(a_ref[...].astype(BF16), b_ref[...].astype(BF16),
                                        (((ca,), (cb,)), ((), ())), preferred_element_type=F32)

        @pl.when(k == nk - 1)
        def _():
            r = acc_ref[...] * scale
            if add is not None:
                r = add_ref[...] + r
            o_ref[...] = r.astype(o_ref.dtype)

    a_spec = (pl.BlockSpec((tk, tm), lambda i, j, k: (k, i)) if mode == "tn"
              else pl.BlockSpec((tm, tk), lambda i, j, k: (i, k)))
    b_spec = (pl.BlockSpec((tn, tk), lambda i, j, k: (j, k)) if mode == "nt"
              else pl.BlockSpec((tk, tn), lambda i, j, k: (k, j)))
    in_specs, args = [a_spec, b_spec], [a, b]
    blk = tm * tk * a.dtype.itemsize + tk * tn * b.dtype.itemsize + tm * tn * 8
    if add is not None:
        in_specs.append(pl.BlockSpec((tm, tn), lambda i, j, k: (i, j)))
        args.append(add)
        blk += tm * tn * 4
    return pl.pallas_call(
        body, name=name, grid=(M // tm, N // tn, nk),
        in_specs=in_specs, out_specs=pl.BlockSpec((tm, tn), lambda i, j, k: (i, j)),
        out_shape=jax.ShapeDtypeStruct((M, N), F32),
        scratch_shapes=[pltpu.VMEM((tm, tn), F32)],
        compiler_params=_cparams(("parallel", "parallel", "arbitrary"), blk),
    )(*args)


def matmul_cs(a, w, mode, name, token):
    n_blk = N_SHARDS
    if mode == "tn":
        (K, R), Cs = a.shape, w.shape[2] // 2
        tm, tk = _pick(R, MM_TILE_M), _pick(K, 1024)
        grid = (R // tm, n_blk, K // tk)
        a_spec = pl.BlockSpec((tk, tm), lambda i, j, k: (k, i))
        w_spec = pl.BlockSpec((None, tk, Cs), lambda i, j, k: (j // 2, k, j % 2))
        o_spec = pl.BlockSpec((None, tm, Cs), lambda i, j, k: (j, i, 0))
        out_shape, acc_shape, dims = (n_blk, R, Cs), (tm, Cs), (0, 0)
        blk = tk * tm * a.dtype.itemsize + tk * Cs * w.dtype.itemsize + tm * Cs * 8
    else:
        M, (_, R, Cs) = a.shape[1], w.shape
        tm, tn = _pick(M, MM_TILE_M), _pick(R, MM_TILE_N)
        grid = (M // tm, R // tn, n_blk)
        a_spec = pl.BlockSpec((None, tm, Cs), lambda i, j, k: (k // 2, i, k % 2))
        w_spec = pl.BlockSpec((None, tn, Cs), lambda i, j, k: (k, j, 0))
        o_spec = pl.BlockSpec((tm, tn), lambda i, j, k: (i, j))
        out_shape, acc_shape, dims = (M, R), (tm, tn), (1, 1)
        blk = tm * Cs * a.dtype.itemsize + tn * Cs * w.dtype.itemsize + tm * tn * 8
    nk = grid[2]

    def body(a_ref, w_ref, tok_ref, o_ref, acc_ref):
        k = pl.program_id(2)

        @pl.when(k == 0)
        def _():
            acc_ref[...] = jnp.zeros_like(acc_ref)

        acc_ref[...] += lax.dot_general(a_ref[...].astype(BF16), w_ref[...].astype(BF16),
                                        (((dims[0],), (dims[1],)), ((), ())), preferred_element_type=F32)

        @pl.when(k == nk - 1)
        def _():
            o_ref[...] = acc_ref[...] + tok_ref[0:1, 0:1]

    return pl.pallas_call(
        body, name=name, grid=grid, in_specs=[a_spec, w_spec, pl.BlockSpec(token.shape, lambda i, j, k: (0, 0))],
        out_specs=o_spec, out_shape=jax.ShapeDtypeStruct(out_shape, F32), scratch_shapes=[pltpu.VMEM(acc_shape, F32)],
        compiler_params=_cparams(("parallel", "parallel", "arbitrary"), blk),
    )(a, w, token)


FFN_TILE_M = 512


def _swiglu(gate, up):
    return gate * jax.nn.sigmoid(gate) * up


def ffn_in_act(h, w, name):
    (M, R), Cs, half = h.shape, w.shape[2], N_SHARDS // 2
    tm, tk = _pick(M, FFN_TILE_M), _pick(R, 1024)
    nk = R // tk

    def body(h_ref, wg_ref, wu_ref, gu_ref, act_ref, acc_ref):
        k = pl.program_id(2)

        @pl.when(k == 0)
        def _():
            acc_ref[...] = jnp.zeros_like(acc_ref)

        hb = h_ref[...].astype(BF16)
        for part, w_ref in enumerate((wg_ref, wu_ref)):
            acc_ref[part] += jnp.dot(hb, w_ref[...].astype(BF16), preferred_element_type=F32)

        @pl.when(k == nk - 1)
        def _():
            gu_ref[...] = acc_ref[...]
            act_ref[...] = _swiglu(acc_ref[0], acc_ref[1]).astype(act_ref.dtype)

    w_spec = lambda off: pl.BlockSpec((None, tk, Cs), functools.partial(lambda off, j, i, k: (j + off, k, 0), off))
    blk = tm * tk * h.dtype.itemsize + 2 * tk * Cs * w.dtype.itemsize + tm * Cs * (16 + 2)
    return pl.pallas_call(
        body, name=name, grid=(half, M // tm, nk),
        in_specs=[pl.BlockSpec((tm, tk), lambda j, i, k: (i, k)), w_spec(0), w_spec(half)],
        out_specs=[pl.BlockSpec((2, tm, Cs), lambda j, i, k: (0, i, j)), pl.BlockSpec((tm, Cs), lambda j, i, k: (i, j))],
        out_shape=[jax.ShapeDtypeStruct((2, M, half * Cs), F32), jax.ShapeDtypeStruct((M, half * Cs), BF16)],
        scratch_shapes=[pltpu.VMEM((2, tm, Cs), F32)],
        compiler_params=_cparams(("parallel", "parallel", "arbitrary"), blk),
    )(h, w, w)


def ffn_dact_dgu(dy, w_out, gu, scale, name):
    (M, D), F = dy.shape, w_out.shape[0]
    tm, tn = _pick(M, FFN_TILE_M), F // 2

    def body(dy_ref, w_ref, gu_ref, dgu_ref):
        dact = scale * lax.dot_general(dy_ref[...].astype(BF16), w_ref[...].astype(BF16),
                                       (((1,), (1,)), ((), ())), preferred_element_type=F32)
        dgate, dup = jax.vjp(_swiglu, gu_ref[0], gu_ref[1])[1](dact)
        dgu_ref[0] = dgate.astype(dgu_ref.dtype)
        dgu_ref[1] = dup.astype(dgu_ref.dtype)

    pair = pl.BlockSpec((2, tm, tn), lambda j, i: (0, i, j))
    blk = tm * D * dy.dtype.itemsize + tn * D * w_out.dtype.itemsize + 2 * tm * tn * (4 + 2)
    return pl.pallas_call(
        body, name=name, grid=(F // tn, M // tm),
        in_specs=[pl.BlockSpec((tm, D), lambda j, i: (i, 0)), pl.BlockSpec((tn, D), lambda j, i: (j, 0)), pair],
        out_specs=pair, out_shape=jax.ShapeDtypeStruct((2, M, F), BF16),
        compiler_params=_cparams(("parallel", "parallel"), blk),
    )(dy, w_out, gu)


def _row_block(n, width, n_arrays):
    cap = (V7X_VMEM_BYTES // 4) // (2 * 4 * width * n_arrays)
    best = None
    for t in range(16, min(n, cap) + 1, 16):
        if n % t == 0:
            best = t
    return best or n


def placed_map(f, ins, out, *, n_blocks, tb, name):
    def body(*refs):
        refs[-1][...] = f(*[r[...] for r in refs[:-1]]).astype(refs[-1].dtype)

    def spec(fn):
        def index(i):
            x, y, c = _place()
            return fn(i, (c, 2 * x + y)), 0
        return pl.BlockSpec((tb, width), index)

    o_rows, width, o_dtype, o_fn = out
    blk = (sum(a.dtype.itemsize for a, _ in ins) + jnp.dtype(o_dtype).itemsize) * tb * width
    return pl.pallas_call(
        body, name=name, grid=(n_blocks,), in_specs=[spec(fn) for _, fn in ins], out_specs=spec(o_fn),
        out_shape=jax.ShapeDtypeStruct((o_rows, width), o_dtype),
        compiler_params=_cparams(("parallel",), blk),
    )(*[a for a, _ in ins])


def rowmap(f, rows, params, outs, accs=(), *, tb, name):
    rows = [r if isinstance(r, tuple) else (r, r.shape[1], 0) for r in rows]
    S = rows[0][0].shape[0]
    assert S % tb == 0, (name, S, tb)
    n_in, n_out = len(rows) + len(params), len(outs)

    def body(*refs):
        res = f(*[r[...] for r in refs[:n_in]])
        res = res if isinstance(res, (tuple, list)) else (res,)
        o_refs, a_refs = refs[n_in:n_in + n_out], refs[n_in + n_out:]
        for ref, val in zip(o_refs, res[:n_out]):
            ref[...] = val.astype(ref.dtype)
        if a_refs:
            @pl.when(pl.program_id(0) == 0)
            def _():
                for ref in a_refs:
                    ref[...] = jnp.zeros_like(ref)

            for ref, val in zip(a_refs, res[n_out:]):
                ref[...] += val.astype(F32)

    in_specs = [pl.BlockSpec((tb, w), functools.partial(lambda cb, i: (i, cb), cb)) for _, w, cb in rows]
    in_specs += [pl.BlockSpec(p.shape, lambda i: (0, 0)) for p in params]
    out_specs = [pl.BlockSpec((tb, w), lambda i: (i, 0)) for w, _ in outs]
    out_specs += [pl.BlockSpec(tuple(s), lambda i: (0, 0)) for s in accs]
    out_shape = [jax.ShapeDtypeStruct((S, w), dt) for w, dt in outs]
    out_shape += [jax.ShapeDtypeStruct(tuple(s), F32) for s in accs]
    blk = sum(tb * w * a.dtype.itemsize for a, w, _ in rows) + sum(_nbytes(p.shape, p.dtype) for p in params)
    blk += sum(_nbytes((tb, w), dt) for w, dt in outs) + sum(_nbytes(s, F32) for s in accs)
    res = pl.pallas_call(
        body, name=name, grid=(S // tb,), in_specs=in_specs, out_specs=out_specs, out_shape=out_shape,
        compiler_params=_cparams(("arbitrary",) if accs else ("parallel",), blk),
    )(*[r[0] for r in rows], *[pltpu.with_memory_space_constraint(p, pltpu.HBM) for p in params])
    return res


def _rms(x, g):
    return x * lax.rsqrt(jnp.mean(x * x, axis=-1, keepdims=True) + RMS_EPS) * g


def _softplus(z):
    return jnp.maximum(z, 0.0) + jnp.log(1.0 + jnp.exp(-jnp.abs(z)))


def _rwkv_pre(xrk, xlo, w0, w2p, a0, a2p, g2p, k_k, k_a, seg, seg_t):
    k = xrk[:, D_MODEL:2 * D_MODEL]
    w = -_softplus(-(w0 + NN(jnp.tanh(xlo), w2p))) - 0.5
    log_decay = -jnp.exp(w)
    a = jax.nn.sigmoid(a0 + NN(xlo, a2p))
    g = NN(jax.nn.sigmoid(xlo), g2p)
    kk = k * k_k
    norm = jnp.maximum(jnp.sqrt(SEG(kk * kk, seg)), 1e-12)
    kk = kk * SEG(1.0 / norm, seg_t)
    k_mod = k * (1.0 + (a - 1.0) * k_a)
    return log_decay, k_mod, -kk, kk * a, g


def _rwkv_post(wkv, r, k_mod, v, g, r_k, ln_w, ln_b, seg, seg_t):
    inv_n = 1.0 / HEAD_DIM
    mean = SEG(wkv, seg) * inv_n
    cen = wkv - SEG(mean, seg_t)
    var = SEG(cen * cen, seg) * inv_n
    y = cen * SEG(lax.rsqrt(var + GN_EPS), seg_t) * ln_w + ln_b
    bonus = SEG(SEG(r * k_mod * r_k, seg), seg_t) * v
    return (y + bonus) * g


def _qk_norm(q, k, q_gain, k_gain, seg, seg_t, tile_t):
    def norm(x, gain):
        mean_sq = SEG(x * x, seg) * (1.0 / HEAD_DIM)
        return x * SEG(lax.rsqrt(mean_sq + RMS_EPS), seg_t) * SEG(gain, tile_t)

    return norm(q, q_gain) * (HEAD_DIM ** -0.5), norm(k, k_gain)


def _gate_merge(pgate, pa, pb, b_gate):
    sg = jax.nn.sigmoid(pgate + b_gate)
    return sg[:, :D_MODEL] * pa + sg[:, D_MODEL:] * pb


def _group_combine(o0, o1, o2, l0, l1, l2):
    m = jnp.maximum(jnp.maximum(l0, l1), l2)
    es = [jnp.exp(l - m) for l in (l0, l1, l2)]
    den = es[0] + es[1] + es[2]
    return jnp.concatenate([o * (e / den) for o, e in zip((o0, o1, o2), es)], axis=1)


def _each(f, *xs):
    return tuple(f(*args) for args in zip(*xs))


def _attn_block(q, kc, kp, vc, vp, first):
    qi = lax.broadcasted_iota(jnp.int32, (ATTN_BLK, ATTN_BLK), 0)
    kj = lax.broadcasted_iota(jnp.int32, (ATTN_BLK, ATTN_BLK), 1)
    own = kj <= qi
    s_c = _each(lambda a, b: jnp.where(own, NT(a, b), NEG_INF), q, kc)
    s_p = _each(lambda a, b, f: jnp.where((kj >= qi) & (f < 0.5), NT(a, b), NEG_INF), q, kp, first)
    row_max = lambda s: jnp.max(s, axis=-1, keepdims=True)
    row_sum = lambda s: jnp.sum(s, axis=-1, keepdims=True)
    m = _each(lambda c_, p_: jnp.maximum(row_max(c_), row_max(p_)), s_c, s_p)
    e_c, e_p = _each(lambda s, m_: jnp.exp(s - m_), s_c, m), _each(lambda s, m_: jnp.exp(s - m_), s_p, m)
    den = _each(lambda c_, p_: row_sum(c_) + row_sum(p_), e_c, e_p)
    inv = _each(lambda d_: 1.0 / d_, den)
    o = _each(lambda ec, ep, i_, vc_, vp_: (NN(ec, vc_) + NN(ep, vp_)) * i_, e_c, e_p, inv, vc, vp)
    lse = _each(lambda m_, d_: jnp.broadcast_to(m_ + jnp.log(d_), (ATTN_BLK, HEAD_DIM)), m, den)
    return o, lse


def _attn_pair(q, k, k_before, v, v_before, first):
    n = len(q[0])
    o, lse = _attn_block(q[0] + q[1], k[0] + k[1], k_before + k[0], v[0] + v[1], v_before + v[0],
                         (first[0],) * n + (first[1],) * n)
    return (o[:n], o[n:]), (lse[:n], lse[n:])


TRI_SEED = 8


def _tri_inverse(n):
    c = n[0].shape[0]
    row = lax.broadcasted_iota(jnp.int32, (c, c), 0)
    col = lax.broadcasted_iota(jnp.int32, (c, c), 1)
    same_block = lambda size: (row >> (size.bit_length() - 1)) == (col >> (size.bit_length() - 1))
    seed = same_block(TRI_SEED)
    p = _each(lambda m: jnp.where(seed, m, 0.0), n)
    t, span = _each(lambda m: (row == col).astype(F32) + m, p), 2
    while span < TRI_SEED:
        p = _each(NN, p, p)
        t = _each(lambda t_, p_: t_ + NN(t_, p_), t, p)
        span *= 2
    size = TRI_SEED
    while size < c:
        joins = same_block(2 * size) & jnp.logical_not(same_block(size))
        t = _each(lambda t_, m: t_ + NN(NN(t_, jnp.where(joins, m, 0.0)), t_), t, n)
        size *= 2
    return t


@jax.custom_vjp
def _tri_solve(n, rhs, t):
    return _each(NN, t, rhs)


def _tri_solve_fwd(n, rhs, t):
    x = _each(NN, t, rhs)
    return x, (t, x)


def _tri_solve_bwd(res, dx):
    t, x = res
    drhs = _each(TN, t, dx)
    return _each(NT, drhs, x), drhs, _each(jnp.zeros_like, t)


_tri_solve.defvjp(_tri_solve_fwd, _tri_solve_bwd)


def _lower_ones(c):
    row = lax.broadcasted_iota(jnp.int32, (c, c), 0)
    col = lax.broadcasted_iota(jnp.int32, (c, c), 1)
    return (row >= col).astype(BF16)


def _ones_dot(ones, x, contract):
    hi, lo = _split_bf16(x)
    dims = (((contract,), (0,)), ((), ()))
    return (lax.dot_general(ones, hi, dims, preferred_element_type=F32)
            + lax.dot_general(ones, lo, dims, preferred_element_type=F32))


@jax.custom_vjp
def _cumsum_rows(x):
    return _ones_dot(_lower_ones(x.shape[0]), x, 1)


_cumsum_rows.defvjp(lambda x: (_ones_dot(_lower_ones(x.shape[0]), x, 1), None),
                    lambda _, g: (_ones_dot(_lower_ones(g.shape[0]), g, 0),))


def _wkv_chunk(s0, r, lw, k, v, a, b, t_inv=None):
    c = r[0].shape[0]
    row = lax.broadcasted_iota(jnp.int32, (c, c), 0)
    col = lax.broadcasted_iota(jnp.int32, (c, c), 1)
    strict, incl = row > col, row >= col
    cat = lambda p, q: jnp.concatenate([p, q], axis=0)
    cum = _each(_cumsum_rows, lw)
    e_neg = _each(lambda c_: jnp.exp(-c_), cum)
    ar = _each(lambda a_, r_, c_, l_: cat(a_ * jnp.exp(c_ - l_), r_ * jnp.exp(c_)), a, r, cum, lw)
    b_t, k_t = _each(jnp.multiply, b, e_neg), _each(jnp.multiply, k, e_neg)
    p_b, p_k, p_s = _each(NT, ar, b_t), _each(NT, ar, k_t), _each(NT, ar, s0)
    n_ab = _each(lambda p: jnp.where(strict, p[:c], 0.0), p_b)
    m_rb = _each(lambda p: jnp.where(incl, p[c:], 0.0), p_b)
    n_ak = _each(lambda p: jnp.where(strict, p[:c], 0.0), p_k)
    m_rk = _each(lambda p: jnp.where(incl, p[c:], 0.0), p_k)
    if t_inv is None:
        t_inv = _tri_inverse(n_ab)
    u = _tri_solve(n_ab, _each(lambda p, n_, v_: p[:c] + NN(n_, v_), p_s, n_ak, v), t_inv)
    y = _each(lambda p, mb, u_, mk, v_: p[c:] + NN(mb, u_) + NN(mk, v_), p_s, m_rb, u, m_rk, v)
    g_end = _each(lambda l_: jnp.exp(jnp.sum(l_, axis=0, keepdims=True)), lw)
    s1 = _each(lambda s_, g_, u_, v_, b_, k_: s_ * g_ + TN(cat(u_, v_), cat(b_, k_) * g_),
               s0, g_end, u, v, b_t, k_t)
    return y, s1, t_inv


def _adamw(w, g, m, v):
    m = ADAM_B1 * m + (1.0 - ADAM_B1) * g
    v = ADAM_B2 * v + (1.0 - ADAM_B2) * jnp.square(g)
    m_hat = m / (1.0 - ADAM_B1 ** ADAM_STEP)
    v_hat = v / (1.0 - ADAM_B2 ** ADAM_STEP)
    delta = -ADAM_LR * (m_hat / (jnp.sqrt(v_hat) + ADAM_EPS) + ADAM_WD * w)
    return delta, m, v


def token_shift_fwd(p, mu, *, tb, name):
    S, W = p.shape
    hb = tb // 8

    def body(p_ref, halo_ref, mu_ref, o_ref):
        i = pl.program_id(0)
        x = p_ref[...]
        before = halo_ref[7:8, :] * (i > 0).astype(F32)
        row = lax.broadcasted_iota(jnp.int32, (tb, W), 0)
        prev = jnp.where(row == 0, before, pltpu.roll(x, 1, 0))
        o_ref[...] = x + (prev - x) * mu_ref[...]

    blk = (2 * tb + 8) * W * 4
    return pl.pallas_call(
        body, name=name, grid=(S // tb,),
        in_specs=[pl.BlockSpec((tb, W), lambda i: (i, 0)),
                  pl.BlockSpec((8, W), lambda i: (jnp.maximum(i * hb - 1, 0), 0)),
                  pl.BlockSpec((1, W), lambda i: (0, 0))],
        out_specs=pl.BlockSpec((tb, W), lambda i: (i, 0)),
        out_shape=jax.ShapeDtypeStruct((S, W), F32),
        compiler_params=_cparams(("parallel",), blk),
    )(p, p, mu)


def token_shift_bwd(dxs, p, mu, *, tb, name):
    S, W = p.shape
    hb, nb = tb // 8, S // tb

    def body(d_ref, dnext_ref, p_ref, halo_ref, mu_ref, dp_ref, dmu_ref):
        i = pl.program_id(0)
        d, x, mu_v = d_ref[...], p_ref[...], mu_ref[...]
        row = lax.broadcasted_iota(jnp.int32, (tb, W), 0)
        before = halo_ref[7:8, :] * (i > 0).astype(F32)
        prev = jnp.where(row == 0, before, pltpu.roll(x, 1, 0))
        t = d * mu_v
        after = dnext_ref[0:1, :] * mu_v * (i < nb - 1).astype(F32)
        nxt = jnp.where(row == tb - 1, after, pltpu.roll(t, tb - 1, 0))
        dp_ref[...] = (d - t + nxt).astype(dp_ref.dtype)

        @pl.when(i == 0)
        def _():
            dmu_ref[...] = jnp.zeros_like(dmu_ref)

        dmu_ref[...] += jnp.sum(d * (prev - x), axis=0, keepdims=True)

    blk = (3 * tb + 16) * W * 4
    return pl.pallas_call(
        body, name=name, grid=(nb,),
        in_specs=[pl.BlockSpec((tb, W), lambda i: (i, 0)),
                  pl.BlockSpec((8, W), lambda i: (jnp.minimum((i + 1) * hb, S // 8 - 1), 0)),
                  pl.BlockSpec((tb, W), lambda i: (i, 0)),
                  pl.BlockSpec((8, W), lambda i: (jnp.maximum(i * hb - 1, 0), 0)),
                  pl.BlockSpec((1, W), lambda i: (0, 0))],
        out_specs=[pl.BlockSpec((tb, W), lambda i: (i, 0)), pl.BlockSpec((1, W), lambda i: (0, 0))],
        out_shape=[jax.ShapeDtypeStruct((S, W), BF16), jax.ShapeDtypeStruct((1, W), F32)],
        compiler_params=_cparams(("arbitrary",), blk),
    )(dxs, dxs, p, p, mu)


def _head_cols(h):
    return pl.ds(h * HEAD_DIM, HEAD_DIM)


def wkv_fwd(xs_rk, lw, k, a, b):
    S = lw.shape[0]
    C, nc, G, N = WKV_CHUNK, S // WKV_CHUNK, WKV_HEADS_PER_STEP, HEAD_DIM

    def body(r_ref, lw_ref, k_ref, v_ref, a_ref, b_ref, y_ref, st_ref, ti_ref, state):
        @pl.when(pl.program_id(1) == 0)
        def _():
            state[...] = jnp.zeros_like(state)

        for base in range(0, G, WKV_HEADS_PER_RUN):
            run = range(base, base + WKV_HEADS_PER_RUN)
            heads = lambda ref: tuple(ref[:, _head_cols(h)] for h in run)
            s0 = tuple(state[h] for h in run)
            y, s1, t_inv = _wkv_chunk(s0, heads(r_ref), heads(lw_ref), heads(k_ref), heads(v_ref), heads(a_ref),
                                      heads(b_ref))
            for i, h in enumerate(run):
                st_ref[h] = s0[i]
                ti_ref[h] = t_inv[i].astype(BF16)
                y_ref[:, _head_cols(h)] = y[i]
                state[h] = s1[i]

    W = G * N
    seq = lambda j: pl.BlockSpec((C, W), functools.partial(lambda j, g, c: (c, j + g), j))
    per = D_MODEL // W
    per_chunk = lambda n: pl.BlockSpec((None, G, n, n), lambda g, c: (c, g, 0, 0))
    return pl.pallas_call(
        body, name="wkv_fwd", grid=(RWKV_HEADS // G, nc),
        in_specs=[seq(0), seq(0), seq(0), seq(2 * per), seq(0), seq(0)],
        out_specs=[seq(0), per_chunk(N), per_chunk(C)],
        out_shape=[jax.ShapeDtypeStruct((S, D_MODEL), F32), jax.ShapeDtypeStruct((nc, RWKV_HEADS, N, N), F32),
                   jax.ShapeDtypeStruct((nc, RWKV_HEADS, C, C), BF16)],
        scratch_shapes=[pltpu.VMEM((G, N, N), F32)],
        compiler_params=_cparams(("parallel", "arbitrary"), 8 * C * W * 4 + 2 * G * N * N * 4 + G * C * C * 4),
    )(xs_rk, lw, k, xs_rk, a, b)


def wkv_bwd(xs_rk, lw, k, a, b, states, t_invs, dy):
    S = lw.shape[0]
    C, nc, G, N = WKV_CHUNK, S // WKV_CHUNK, WKV_HEADS_PER_STEP, HEAD_DIM

    def body(r_ref, lw_ref, k_ref, v_ref, a_ref, b_ref, st_ref, ti_ref, dy_ref,
             dr_ref, dlw_ref, dk_ref, dv_ref, da_ref, db_ref, dstate):
        @pl.when(pl.program_id(1) == 0)
        def _():
            dstate[...] = jnp.zeros_like(dstate)

        for base in range(0, G, WKV_HEADS_PER_RUN):
            run = range(base, base + WKV_HEADS_PER_RUN)
            heads = lambda ref: tuple(ref[:, _head_cols(h)] for h in run)
            t_inv = tuple(ti_ref[h] for h in run)
            chunk = lambda *args: _wkv_chunk(*args, t_inv)[:2]
            _, pull = jax.vjp(chunk, tuple(st_ref[h] for h in run), heads(r_ref), heads(lw_ref),
                              heads(k_ref), heads(v_ref), heads(a_ref), heads(b_ref))
            ds0, *grads = pull((heads(dy_ref), tuple(dstate[h] for h in run)))
            for i, h in enumerate(run):
                dstate[h] = ds0[i]
                for ref, grad in zip((dr_ref, dlw_ref, dk_ref, dv_ref, da_ref, db_ref), grads):
                    ref[:, _head_cols(h)] = grad[i]

    W = G * N
    seq = lambda j: pl.BlockSpec((C, W), functools.partial(lambda j, g, c: (nc - 1 - c, j + g), j))
    per = D_MODEL // W
    st = lambda n: pl.BlockSpec((None, G, n, n), lambda g, c: (nc - 1 - c, g, 0, 0))
    return pl.pallas_call(
        body, name="wkv_bwd", grid=(RWKV_HEADS // G, nc),
        in_specs=[seq(0), seq(0), seq(0), seq(2 * per), seq(0), seq(0), st(N), st(C), seq(0)],
        out_specs=[seq(0)] * 6, out_shape=[jax.ShapeDtypeStruct((S, D_MODEL), F32)] * 6,
        scratch_shapes=[pltpu.VMEM((G, N, N), F32)],
        compiler_params=_cparams(("parallel", "arbitrary"), 14 * C * W * 4 + 2 * G * N * N * 4 + G * C * C * 4),
    )(xs_rk, lw, k, xs_rk, a, b, states, t_invs, dy)


def _first_flag(i, per_seq):
    return (lax.rem(i, per_seq) == 0).astype(F32)


def _view(a):
    return a if isinstance(a, tuple) else (a, 0)


def _block_rows(half):
    return pl.ds(half * ATTN_BLK, ATTN_BLK)


def _block_heads(ref, half):
    return tuple(ref[_block_rows(half), _head_cols(h)] for h in range(ATTN_HPG))


def _pair_heads(ref):
    return _block_heads(ref, 0), _block_heads(ref, 1)


def attn_fwd(q, k, v, per_seq, name):
    (q, q_col), (k, k_col), (v, v_col) = _view(q), _view(k), _view(v)
    R, N = q.shape[0], GROUP_W
    n_pairs = R // (2 * ATTN_BLK)

    def body(q_ref, k_ref, kb_ref, v_ref, vb_ref, o_ref, lse_ref):
        pair = pl.program_id(0)
        first = (_first_flag(2 * pair, per_seq), _first_flag(2 * pair + 1, per_seq))
        o, lse = _attn_pair(_pair_heads(q_ref), _pair_heads(k_ref), _block_heads(kb_ref, 0), _pair_heads(v_ref),
                            _block_heads(vb_ref, 0), first)
        for half in range(2):
            for h in range(ATTN_HPG):
                o_ref[_block_rows(half), _head_cols(h)] = o[half][h]
                lse_ref[_block_rows(half), _head_cols(h)] = lse[half][h]

    cur = lambda col: pl.BlockSpec((2 * ATTN_BLK, N), lambda i: (i, col))
    prv = lambda col: pl.BlockSpec((ATTN_BLK, N), lambda i: (jnp.maximum(2 * i - 1, 0), col))
    return pl.pallas_call(
        body, name=name, grid=(n_pairs,), in_specs=[cur(q_col), cur(k_col), prv(k_col), cur(v_col), prv(v_col)],
        out_specs=[cur(0), cur(0)], out_shape=[jax.ShapeDtypeStruct((R, N), F32)] * 2,
        compiler_params=_cparams(("parallel",), 12 * ATTN_BLK * N * 4),
    )(q, k, k, v, v)


def attn_bwd(q, k, v, do, dlse, per_seq, name):
    views = [_view(a) for a in (q, k, v, do, dlse)]
    (q, q_col), (k, k_col), (v, v_col), (do, do_col), (dlse, dl_col) = views
    R, N = q.shape[0], GROUP_W
    n_pairs = R // (2 * ATTN_BLK)

    def body(q_ref, k_ref, kb_ref, v_ref, vb_ref, do_ref, dl_ref, dq_ref, dk_ref, dv_ref, carry_k, carry_v):
        step = pl.program_id(0)
        pair = n_pairs - 1 - step
        first = (_first_flag(2 * pair, per_seq), _first_flag(2 * pair + 1, per_seq))

        @pl.when(step == 0)
        def _():
            carry_k[...] = jnp.zeros_like(carry_k)
            carry_v[...] = jnp.zeros_like(carry_v)

        _, pull = jax.vjp(functools.partial(_attn_pair, first=first), _pair_heads(q_ref), _pair_heads(k_ref),
                          _block_heads(kb_ref, 0), _pair_heads(v_ref), _block_heads(vb_ref, 0))
        dq, dk, dk_before, dv, dv_before = pull((_pair_heads(do_ref), _pair_heads(dl_ref)))
        old_k, old_v = _block_heads(carry_k, 0), _block_heads(carry_v, 0)
        for h in range(ATTN_HPG):
            cols = _head_cols(h)
            for half in range(2):
                dq_ref[_block_rows(half), cols] = dq[half][h]
            dk_ref[_block_rows(0), cols] = dk[0][h]
            dv_ref[_block_rows(0), cols] = dv[0][h]
            dk_ref[_block_rows(1), cols] = dk[1][h] + old_k[h]
            dv_ref[_block_rows(1), cols] = dv[1][h] + old_v[h]
            carry_k[:, cols] = dk_before[h]
            carry_v[:, cols] = dv_before[h]

    cur = lambda col: pl.BlockSpec((2 * ATTN_BLK, N), lambda i: (n_pairs - 1 - i, col))
    prv = lambda col: pl.BlockSpec((ATTN_BLK, N), lambda i: (jnp.maximum(2 * (n_pairs - 1 - i) - 1, 0), col))
    return pl.pallas_call(
        body, name=name, grid=(n_pairs,),
        in_specs=[cur(q_col), cur(k_col), prv(k_col), cur(v_col), prv(v_col), cur(do_col), cur(dl_col)],
        out_specs=[cur(0)] * 3, out_shape=[jax.ShapeDtypeStruct((R, N), F32)] * 3,
        scratch_shapes=[pltpu.VMEM((ATTN_BLK, N), F32)] * 2,
        compiler_params=_cparams(("arbitrary",), 22 * ATTN_BLK * N * 4),
    )(q, k, k, v, v, do, dlse)


def by_residue(u, d):
    if d == 1:
        return u
    return u.reshape(u.shape[0] // d, d, GROUP_W).transpose(1, 0, 2).reshape(u.shape)


def by_position(u, d):
    if d == 1:
        return u
    return u.reshape(d, u.shape[0] // d, GROUP_W).transpose(1, 0, 2).reshape(u.shape)


def group_columns(t, col_block, d):
    if d == 1:
        return (t, col_block)
    return by_residue(t[:, GROUP_W * col_block:GROUP_W * (col_block + 1)], d)


def _ffn_fwd(x, norm, w_in, w_out, tag, token):
    h = rowmap(lambda x_b, g, tok: _rms(x_b, g) + tok[0:1, 0:1], [x], [norm, token], [(D_MODEL, BF16)], tb=512,
               name=tag + "_norm")[0]
    gu, act = ffn_in_act(h, w_in, tag + "_in")
    y = matmul(act, w_out, "nn", tag + "_out", add=x, scale=0.5)
    return y, (x, h, gu, act)


def _ffn_bwd(dy, saved, norm, w_in, w_out, tag, on_weight_grads):
    x, h, gu, act = saved
    no_token = jnp.zeros((8, 128), F32)
    dw_out = matmul(act, dy, "tn", tag + "_dwout", scale=0.5)
    dgu = ffn_dact_dgu(dy, w_out, gu, 0.5, tag + "_dgu")
    dw_in = matmul_cs(h, dgu, "tn", tag + "_dwin", no_token)
    dh = matmul_cs(dgu, w_in, "nt", tag + "_dh", on_weight_grads(dw_in, dw_out))

    def norm_bwd(x_b, dh_b, dy_b, g):
        dx, dg = jax.vjp(_rms, x_b, g)[1](dh_b)
        return dy_b + dx, dg

    dx, dnorm = rowmap(norm_bwd, [x, dh, dy], [norm], [(D_MODEL, F32)], [(1, D_MODEL)], tb=256,
                       name=tag + "_dnorm")
    return dx, dnorm, dw_in, dw_out


def layer_step(x, tgt, W, P, start_token, more_weights, on_mixer_grads, on_ffn1_grads):
    S = x.shape[0]
    x1, ffn1_saved = _ffn_fwd(x, P["ffn1_norm"], W["ffn1_w_in"], W["ffn1_w_out"], "ffn1", start_token)
    W = {**W, **more_weights("mixer", x1)}
    head_of = lambda n: jnp.arange(n)[:, None] // HEAD_DIM == jnp.arange(n // HEAD_DIM)[None, :]
    seg, seg_a = head_of(D_MODEL).astype(BF16), head_of(ATTN_WIDTH).astype(BF16)
    seg_t, seg_a_t = seg.T, seg_a.T
    tile_t = (jnp.arange(HEAD_DIM)[:, None] == jnp.arange(ATTN_WIDTH)[None, :] % HEAD_DIM).astype(BF16)
    qk_params = [P["attn_q_norm"], P["attn_k_norm"], seg_a, seg_a_t, tile_t]
    w_rkv, w_lora = W["w_in"][:, :RKV], W["w_in"][:, RKV:RKV + LORA]
    w_qkv = W["w_in"][:, RKV + LORA:RKV + LORA + 3 * ATTN_WIDTH]
    w_gate = W["w_in"][:, RKV + LORA + 3 * ATTN_WIDTH:]
    mu_rk, mu_lo = P["rwkv_mu"][:, :RKV], P["rwkv_mu"][:, RKV:]
    zeros = lambda n: jnp.zeros((n, D_MODEL), F32)
    w2p = jnp.concatenate([W["rwkv_w2"], zeros(LORA - LORA_W)], axis=0)
    a2p = jnp.concatenate([zeros(LORA_W), W["rwkv_a2"], zeros(LORA_G)], axis=0)
    g2p = jnp.concatenate([zeros(LORA_W + LORA_A), W["rwkv_g2"]], axis=0)
    pre_params = [P["rwkv_w0"], w2p, P["rwkv_a0"], a2p, g2p, P["rwkv_k_k"], P["rwkv_k_a"], seg, seg_t]
    post_params = [P["rwkv_r_k"], P["rwkv_ln_w"], P["rwkv_ln_b"], seg, seg_t]
    col = lambda arr, j: (arr, D_MODEL, j)

    h = rowmap(_rms, [x1], [P["mix_norm"]], [(D_MODEL, BF16)], tb=512, name="mix_norm")[0]
    p_rk = matmul(h, w_rkv, "nn", "proj_rkv")
    p_lo = matmul(h, w_lora, "nn", "proj_lora")
    p_qkv = matmul(h, w_qkv, "nn", "proj_qkv")
    p_gate = matmul(h, w_gate, "nn", "proj_gate")
    xs_rk = token_shift_fwd(p_rk, mu_rk, tb=256, name="shift_rk")
    xs_lo = token_shift_fwd(p_lo, mu_lo, tb=256, name="shift_lora")
    lw, k_mod, a_neg, b_kk, g = rowmap(
        _rwkv_pre, [xs_rk, xs_lo], pre_params, [(D_MODEL, F32)] * 5, tb=256, name="rwkv_pre")
    wkv, states, t_invs = wkv_fwd(xs_rk, lw, k_mod, a_neg, b_kk)
    post_rows = [wkv, col(xs_rk, 0), k_mod, col(xs_rk, 2), g]
    y_a = rowmap(_rwkv_post, post_rows, post_params, [(D_MODEL, BF16)], tb=256, name="rwkv_post")[0]

    qk_rows = [(p_qkv, ATTN_WIDTH, 0), (p_qkv, ATTN_WIDTH, 1)]
    qn, kn = rowmap(_qk_norm, qk_rows, qk_params, [(ATTN_WIDTH, F32)] * 2, tb=256, name="qk_norm")
    dil = [d for _, d in ATTN_PAIRS]
    groups = range(len(dil))
    per_seq = [S // d // ATTN_BLK for d in dil]
    v_first = 2 * ATTN_WIDTH // GROUP_W
    q_s = [group_columns(qn, g, dil[g]) for g in groups]
    k_s = [group_columns(kn, g, dil[g]) for g in groups]
    v_s = [group_columns(p_qkv, v_first + g, dil[g]) for g in groups]
    attn = [attn_fwd(q_s[g], k_s[g], v_s[g], per_seq[g], "attn_fwd_%d" % g) for g in groups]
    o_lse = [by_position(attn[g][j], dil[g]) for j in range(2) for g in groups]
    y_b = rowmap(_group_combine, o_lse, [], [(ATTN_WIDTH, BF16)], tb=512, name="attn_combine")[0]

    W = {**W, **more_weights("out", y_b)}
    pa = matmul(y_a, W["w_proj_rwkv"], "nn", "proj_a")
    pb = matmul(y_b, W["w_proj_attn"], "nn", "proj_b")
    merged = rowmap(_gate_merge, [p_gate, pa, pb], [P["b_gate"]], [(D_MODEL, BF16)], tb=256, name="merge")[0]
    x2 = matmul(merged, W["w_out"], "nn", "mix_out", add=x1)
    x3, ffn2_saved = _ffn_fwd(x2, P["ffn2_norm"], W["ffn2_w_in"], W["ffn2_w_out"], "ffn2",
                              jnp.zeros_like(start_token))

    def loss_head(y_b_, t_b):
        err = y_b_ - t_b
        return err * (1.0 / D_MODEL), (0.5 / D_MODEL) * jnp.sum(err * err, axis=0, keepdims=True)

    dx3, loss_cols = rowmap(loss_head, [x3, tgt], [], [(D_MODEL, F32)], [(1, D_MODEL)], tb=512, name="loss")

    gW, gP = {}, {}
    dx2, gP["ffn2_norm"], gW["ffn2_w_in"], gW["ffn2_w_out"] = _ffn_bwd(
        dx3, ffn2_saved, P["ffn2_norm"], W["ffn2_w_in"], W["ffn2_w_out"], "ffn2",
        lambda dw_in, dw_out: jnp.zeros_like(start_token))

    dmerged = matmul(dx2, W["w_out"], "nt", "d_merged")
    gW["w_out"] = matmul(merged, dx2, "tn", "dw_out")

    def merge_bwd(pg, pa_b, pb_b, dm, bg):
        return jax.vjp(_gate_merge, pg, pa_b, pb_b, bg)[1](dm)

    dp_gate, dpa, dpb, gP["b_gate"] = rowmap(
        merge_bwd, [p_gate, pa, pb, dmerged], [P["b_gate"]],
        [(2 * D_MODEL, BF16), (D_MODEL, BF16), (D_MODEL, BF16)], [(1, 2 * D_MODEL)], tb=256, name="merge_bwd")
    dy_a = matmul(dpa, W["w_proj_rwkv"], "nt", "d_ya")
    gW["w_proj_rwkv"] = matmul(y_a, dpa, "tn", "dw_proj_a")
    dy_b = matmul(dpb, W["w_proj_attn"], "nt", "d_yb")
    gW["w_proj_attn"] = matmul(y_b, dpb, "tn", "dw_proj_b")

    def combine_bwd(*blocks):
        return jax.vjp(_group_combine, *blocks[:-1])[1](blocks[-1])

    d_o_lse = rowmap(combine_bwd, o_lse + [dy_b], [], [(GROUP_W, F32)] * 6, tb=256, name="attn_combine_bwd")
    d_attn = [attn_bwd(q_s[g], k_s[g], v_s[g], by_residue(d_o_lse[g], dil[g]), by_residue(d_o_lse[3 + g], dil[g]),
                       per_seq[g], "attn_bwd_%d" % g) for g in groups]

    def qk_norm_bwd(q_b, k_b, *rest):
        dqkv, (qg, kg, sg, sgt, tl) = rest[:9], rest[9:]
        f = lambda *a: _qk_norm(*a, sg, sgt, tl)
        dqn, dkn = jnp.concatenate(dqkv[0:3], axis=1), jnp.concatenate(dqkv[3:6], axis=1)
        dq, dk, dqg, dkg = jax.vjp(f, q_b, k_b, qg, kg)[1]((dqn, dkn))
        return jnp.concatenate([dq, dk, *dqkv[6:9]], axis=1), dqg, dkg

    dp_qkv, gP["attn_q_norm"], gP["attn_k_norm"] = rowmap(
        qk_norm_bwd, qk_rows + [by_position(d_attn[g][j], dil[g]) for j in range(3) for g in groups], qk_params,
        [(3 * ATTN_WIDTH, BF16)], [(1, HEAD_DIM)] * 2, tb=256, name="qk_norm_bwd")

    def post_bwd(wkv_b, r_b, k_b, v_b, g_b, d_b, r_k, ln_w, ln_b, sg, sgt):
        f = lambda *a: _rwkv_post(*a, sg, sgt)
        return jax.vjp(f, wkv_b, r_b, k_b, v_b, g_b, r_k, ln_w, ln_b)[1](d_b)

    dwkv, dr_p, dk_p, dv_p, dg, gP["rwkv_r_k"], gP["rwkv_ln_w"], gP["rwkv_ln_b"] = rowmap(
        post_bwd, post_rows + [dy_a], post_params, [(D_MODEL, F32)] * 5, [(1, D_MODEL)] * 3, tb=128,
        name="rwkv_post_bwd")
    dr_w, dlw, dk_w, dv_w, da_neg, db_kk = wkv_bwd(xs_rk, lw, k_mod, a_neg, b_kk, states, t_invs, dwkv)

    def pre_bwd(xrk_b, xlo_b, dlw_b, dkw_b, dkp_b, da_b, db_b, dg_b, drp_b, drw_b, dvp_b, dvw_b,
                w0, w2, a0, a2, g2, k_k, k_a, sg, sgt):
        f = lambda *a: _rwkv_pre(*a, sg, sgt)
        pull = jax.vjp(f, xrk_b, xlo_b, w0, w2, a0, a2, g2, k_k, k_a)[1]
        dxrk, dxlo, *dpar = pull((dlw_b, dkw_b + dkp_b, da_b, db_b, dg_b))
        direct = jnp.concatenate([drp_b + drw_b, jnp.zeros_like(drp_b), dvp_b + dvw_b], axis=1)
        return (dxrk + direct, dxlo, *dpar)

    pre_rows = [xs_rk, xs_lo, dlw, dk_w, dk_p, da_neg, db_kk, dg, dr_p, dr_w, dv_p, dv_w]
    dxs_rk, dxs_lo, gP["rwkv_w0"], dw2p, gP["rwkv_a0"], da2p, dg2p, gP["rwkv_k_k"], gP["rwkv_k_a"] = rowmap(
        pre_bwd, pre_rows, pre_params, [(RKV, F32), (LORA, F32)],
        [(1, D_MODEL), (LORA, D_MODEL), (1, D_MODEL), (LORA, D_MODEL), (LORA, D_MODEL), (1, D_MODEL), (1, D_MODEL)],
        tb=128, name="rwkv_pre_bwd")
    gW["rwkv_w2"] = dw2p[:LORA_W]
    gW["rwkv_a2"] = da2p[LORA_W:LORA_W + LORA_A]
    gW["rwkv_g2"] = dg2p[LORA_W + LORA_A:]
    dp_rk, dmu_rk = token_shift_bwd(dxs_rk, p_rk, mu_rk, tb=256, name="shift_rk_bwd")
    dp_lo, dmu_lo = token_shift_bwd(dxs_lo, p_lo, mu_lo, tb=256, name="shift_lora_bwd")
    gP["rwkv_mu"] = jnp.concatenate([dmu_rk, dmu_lo], axis=1)

    dh = matmul(dp_rk, w_rkv, "nt", "dh_rkv")
    dh = matmul(dp_lo, w_lora, "nt", "dh_lora", add=dh)
    dh = matmul(dp_qkv, w_qkv, "nt", "dh_qkv", add=dh)
    dh = matmul(dp_gate, w_gate, "nt", "dh_gate", add=dh)
    gW["w_in"] = jnp.concatenate([
        matmul(h, dp_rk, "tn", "dw_rkv"), matmul(h, dp_lo, "tn", "dw_lora"),
        matmul(h, dp_qkv, "tn", "dw_qkv"), matmul(h, dp_gate, "tn", "dw_gate")], axis=1)

    token = on_mixer_grads(gW)

    def norm_bwd(x_b, dh_b, dy_b, gn, tok):
        dx, dgn = jax.vjp(_rms, x_b, gn)[1](dh_b)
        return dy_b + dx + tok[0:1, 0:1], dgn

    dx1, gP["mix_norm"] = rowmap(norm_bwd, [x1, dh, dx2], [P["mix_norm"], token], [(D_MODEL, F32)],
                                 [(1, D_MODEL)], tb=256, name="mix_norm_bwd")
    dx, gP["ffn1_norm"], gW["ffn1_w_in"], gW["ffn1_w_out"] = _ffn_bwd(
        dx1, ffn1_saved, P["ffn1_norm"], W["ffn1_w_in"], W["ffn1_w_out"], "ffn1", on_ffn1_grads)
    return loss_cols, dx, gW, gP


N_SHARDS = 4
OTHER_CHIPS = N_SHARDS - 1
BIG = (("ffn1_w_in", (D_MODEL, 2 * D_FF), 1), ("ffn1_w_out", (D_FF, D_MODEL), 0),
       ("w_in", (D_MODEL, 7712), 1), ("rwkv_w2", (LORA_W, D_MODEL), 1), ("rwkv_a2", (LORA_A, D_MODEL), 1),
       ("rwkv_g2", (LORA_G, D_MODEL), 1), ("w_proj_rwkv", (D_MODEL, D_MODEL), 0),
       ("w_proj_attn", (ATTN_WIDTH, D_MODEL), 1), ("w_out", (D_MODEL, D_MODEL), 0),
       ("ffn2_w_in", (D_MODEL, 2 * D_FF), 1), ("ffn2_w_out", (D_FF, D_MODEL), 0))
SMALL = (("ffn1_norm", 1024), ("mix_norm", 1024), ("b_gate", 2048), ("rwkv_mu", 3360), ("rwkv_w0", 1024),
         ("rwkv_a0", 1024), ("rwkv_k_k", 1024), ("rwkv_k_a", 1024), ("rwkv_r_k", 1024), ("rwkv_ln_w", 1024),
         ("rwkv_ln_b", 1024), ("attn_q_norm", 64), ("attn_k_norm", 64), ("ffn2_norm", 1024))
WEIGHT_ORDER = ("ffn1_norm", "ffn1_w_in", "ffn1_w_out", "mix_norm", "w_in", "b_gate", "rwkv_mu", "rwkv_w0",
                "rwkv_w2", "rwkv_a0", "rwkv_a2", "rwkv_g2", "rwkv_k_k", "rwkv_k_a", "rwkv_r_k", "rwkv_ln_w",
                "rwkv_ln_b", "attn_q_norm", "attn_k_norm", "w_proj_rwkv", "w_proj_attn", "w_out", "ffn2_norm",
                "ffn2_w_in", "ffn2_w_out")


LORA_PARTS = ("rwkv_w2", "rwkv_a2", "rwkv_g2")
BLOCK_MAJOR = ("ffn1_w_in", "ffn2_w_in")
FIRST_FFN = ("ffn1_w_in", "ffn1_w_out")
MIXER_IN = ("w_in", "lora")
SMALL_USED = D_MODEL + sum(n for _, n in SMALL)
SMALL_W = -(-SMALL_USED // 128) * 128


def _travel():
    out = {}
    for name, shape, axis in BIG:
        if name == LORA_PARTS[0]:
            out["lora"] = ((LORA, D_MODEL), 1)
        elif name not in LORA_PARTS:
            out[name] = (shape, axis)
    return out


def local_blocks(vals):
    out = {n: vals[n] for n in _travel() if n != "lora"}
    out["lora"] = jnp.concatenate([vals[n] for n in LORA_PARTS], axis=0)
    return out


def split_lora(t):
    return {"rwkv_w2": t[:LORA_W], "rwkv_a2": t[LORA_W:LORA_W + LORA_A], "rwkv_g2": t[LORA_W + LORA_A:]}


def blocks_to_full(name, blocks):
    shape, axis = _travel()[name]
    if name in BLOCK_MAJOR:
        return blocks
    if axis == 0:
        return blocks.reshape(shape)
    return blocks.transpose(1, 0, 2).reshape(shape)


def full_to_blocks(name, full):
    shape, axis = _travel()[name]
    if name in BLOCK_MAJOR:
        return full
    if axis == 0:
        return full.reshape(N_SHARDS, shape[0] // N_SHARDS, shape[1])
    return full.reshape(shape[0], N_SHARDS, shape[1] // N_SHARDS).transpose(1, 0, 2)


def pack_small(vals, head):
    parts = [head] + [vals[name].reshape(1, n) for name, n in SMALL]
    parts.append(jnp.zeros((1, SMALL_W - SMALL_USED), F32))
    return jnp.concatenate(parts, axis=1)


def unpack_small(vec, shapes):
    out, off = {}, D_MODEL
    for name, n in SMALL:
        out[name] = vec[:, off:off + n].reshape(shapes[name])
        off += n
    return out


def _place():
    return lax.axis_index("x"), lax.axis_index("y"), lax.axis_index("c")


def _other_chips(x, y):
    return [(1 - x, y), (x, 1 - y), (1 - x, 1 - y)]


def _remote(src, dst, send_sem, recv_sem, device):
    return pltpu.make_async_remote_copy(src_ref=src, dst_ref=dst, send_sem=send_sem, recv_sem=recv_sem,
                                        device_id=device, device_id_type=MESH)


def _half(ref, who):
    hr = ref.shape[-2] // 2
    rows = pl.ds(pl.multiple_of(who * hr, 8), hr)
    return ref.at[rows] if len(ref.shape) == 2 else ref.at[:, rows]


HBM_REF = pl.BlockSpec(memory_space=pl.ANY)
COMM_PARAMS = dict(compiler_params=pltpu.CompilerParams(has_side_effects=True))


def gather_weights(blocks):
    n = len(blocks)

    def body(*refs):
        ins, outs = refs[:n], refs[n:2 * n]
        ici_send, ici_recv, d2d_send, d2d_recv = refs[2 * n:]
        x, y, c = _place()
        me, sibling, chips = 2 * x + y, (x, y, 1 - c), _other_chips(x, y)
        first = [_remote(_half(ins[t], c), _half(outs[t].at[me], c), ici_send.at[k, t], ici_recv.at[k, t],
                         (px, py, c)) for k, (px, py) in enumerate(chips) for t in range(n)]
        for cp in first:
            cp.start()
        passed = []
        for k, (px, py) in enumerate(chips):
            for t in range(n):
                landed = _half(outs[t].at[2 * px + py], c)
                _remote(landed, landed, ici_send.at[k, t], ici_recv.at[k, t], (px, py, c)).wait_recv()
                cp = _remote(landed, landed, d2d_send.at[k, t], d2d_recv.at[k, t], sibling)
                cp.start()
                passed.append(cp)
        for k, (px, py) in enumerate(chips):
            for t in range(n):
                other = _half(outs[t].at[2 * px + py], 1 - c)
                _remote(other, other, d2d_send.at[k, t], d2d_recv.at[k, t], sibling).wait_recv()
        for cp in first + passed:
            cp.wait_send()

    res = pl.pallas_call(
        body, name="gather_weights", in_specs=[HBM_REF] * n, out_specs=[HBM_REF] * n,
        out_shape=[jax.ShapeDtypeStruct((N_SHARDS,) + b.shape, b.dtype) for b in blocks],
        scratch_shapes=[pltpu.SemaphoreType.DMA((3, n))] * 4, **COMM_PARAMS)(*blocks)
    me = 2 * lax.axis_index("x") + lax.axis_index("y")
    return [lax.dynamic_update_slice(g, b[None], (me, 0, 0)) for g, b in zip(res, blocks)]


def _gather_copies(ins, outs, send_sem, recv_sem):
    x, y, c = _place()
    return [_remote(_half(ins[t], c), _half(outs[t].at[2 * x + y], c), send_sem(k, t), recv_sem(k, t), (px, py, c))
            for k, (px, py) in enumerate(_other_chips(x, y)) for t in range(len(ins))]


def split_start(copies, sources, landing_shapes, name):
    n = len(sources)
    n_cp = OTHER_CHIPS * n

    def body(*refs):
        srcs, dsts = refs[:n], refs[n:2 * n]
        sems, token = refs[2 * n:2 * n + 2 * n_cp], refs[-1]
        for cp in copies(srcs, dsts, lambda k, t: sems[k * n + t], lambda k, t: sems[n_cp + k * n + t]):
            cp.start()
        token[...] = jnp.zeros_like(token)

    hbm = lambda a: pltpu.with_memory_space_constraint(a, pltpu.HBM)
    buffers = list(sources) + [lax.empty(shape, s.dtype) for shape, s in zip(landing_shapes, sources)]
    res = pl.pallas_call(
        body, name=name,
        out_shape=(*[pltpu.SemaphoreType.DMA(())] * (2 * n_cp),
                   *[pltpu.HBM(a.shape, a.dtype) for a in buffers], jax.ShapeDtypeStruct((8, 128), F32)),
        in_specs=[SPLIT_HBM] * (2 * n),
        out_specs=(*[SPLIT_SEM] * (2 * n_cp), *[SPLIT_HBM] * (2 * n), pl.BlockSpec(memory_space=pltpu.VMEM)),
        input_output_aliases={t: 2 * n_cp + t for t in range(2 * n)}, **SPLIT_PARAMS,
    )(*[hbm(a) for a in buffers])
    return (copies, n, res[:-1]), res[-1]


def split_wait(handles, after, name):
    copies, n, held = handles
    n_cp = OTHER_CHIPS * n
    sems, thru = held[:2 * n_cp], held[2 * n_cp:]

    def body(*refs):
        srcs, dsts = refs[:n], refs[n:2 * n]
        sem_refs = refs[2 * n:2 * n + 2 * n_cp]
        for cp in copies(srcs, dsts, lambda k, t: sem_refs[k * n + t], lambda k, t: sem_refs[n_cp + k * n + t]):
            cp.wait_send()
            cp.wait_recv()

    res = pl.pallas_call(
        body, name=name, out_shape=tuple(pltpu.HBM(a.shape, a.dtype) for a in thru),
        in_specs=[SPLIT_HBM] * (2 * n) + [SPLIT_SEM] * (2 * n_cp) + [pl.BlockSpec(memory_space=pl.ANY)],
        out_specs=tuple([SPLIT_HBM] * (2 * n)), input_output_aliases={t: t for t in range(2 * n)}, **SPLIT_PARAMS,
    )(*thru, *sems, after)
    return list(res[n:])


def gather_start(blocks, name):
    return split_start(_gather_copies, blocks, [(N_SHARDS,) + b.shape for b in blocks], name)


def pass_halves(gathered, blocks, name):
    n = len(gathered)

    def body(*refs):
        outs = refs[n:2 * n]
        send_sems, recv_sems = refs[2 * n:]
        x, y, c = _place()
        slots = [2 * px + py for px, py in _other_chips(x, y)]
        give = [_remote(_half(outs[t].at[s], c), _half(outs[t].at[s], c), send_sems.at[k, t], recv_sems.at[k, t],
                        (x, y, 1 - c)) for k, s in enumerate(slots) for t in range(n)]
        for cp in give:
            cp.start()
        for k, s in enumerate(slots):
            for t in range(n):
                other = _half(outs[t].at[s], 1 - c)
                _remote(other, other, send_sems.at[k, t], recv_sems.at[k, t], (x, y, 1 - c)).wait_recv()
        for cp in give:
            cp.wait_send()

    res = pl.pallas_call(
        body, name=name, in_specs=[HBM_REF] * n, out_specs=[HBM_REF] * n,
        out_shape=[jax.ShapeDtypeStruct(g.shape, g.dtype) for g in gathered],
        input_output_aliases={t: t for t in range(n)},
        scratch_shapes=[pltpu.SemaphoreType.DMA((3, n))] * 2, **COMM_PARAMS)(*gathered)
    me = 2 * lax.axis_index("x") + lax.axis_index("y")
    return [lax.dynamic_update_slice(g, b[None], (me, 0, 0)) for g, b in zip(res, blocks)]


def swap_halves(grads):
    n = len(grads)

    def body(*refs):
        ins, got = refs[:n], refs[n:2 * n]
        send_sems, recv_sems = refs[2 * n:]
        x, y, c = _place()
        give = [_remote(_half(ins[t], 1 - c), got[t], send_sems.at[t], recv_sems.at[t], (x, y, 1 - c))
                for t in range(n)]
        for cp in give:
            cp.start()
        for cp in give:
            cp.wait_recv()
        for cp in give:
            cp.wait_send()

    return pl.pallas_call(
        body, name="swap_halves", in_specs=[HBM_REF] * n, out_specs=[HBM_REF] * n,
        out_shape=[jax.ShapeDtypeStruct((g.shape[0], g.shape[1] // 2, g.shape[2]), g.dtype) for g in grads],
        scratch_shapes=[pltpu.SemaphoreType.DMA((n,))] * 2, **COMM_PARAMS)(*grads)


def join_halves(blocks):
    n = len(blocks)

    def body(*refs):
        outs = refs[n:2 * n]
        send_sems, recv_sems = refs[2 * n:]
        x, y, c = _place()
        give = [_remote(_half(outs[t], c), _half(outs[t], c), send_sems.at[t], recv_sems.at[t], (x, y, 1 - c))
                for t in range(n)]
        for cp in give:
            cp.start()
        for t in range(n):
            arriving = _half(outs[t], 1 - c)
            _remote(arriving, arriving, send_sems.at[t], recv_sems.at[t], (x, y, 1 - c)).wait_recv()
        for cp in give:
            cp.wait_send()

    return pl.pallas_call(
        body, name="join_halves", in_specs=[HBM_REF] * n, out_specs=[HBM_REF] * n,
        out_shape=[jax.ShapeDtypeStruct(b.shape, b.dtype) for b in blocks],
        input_output_aliases={t: t for t in range(n)},
        scratch_shapes=[pltpu.SemaphoreType.DMA((n,))] * 2, **COMM_PARAMS)(*blocks)


SPLIT_HBM = pl.BlockSpec(memory_space=pltpu.HBM)
SPLIT_SEM = pl.BlockSpec(memory_space=pltpu.SEMAPHORE)
SPLIT_PARAMS = dict(compiler_params=pltpu.CompilerParams(has_side_effects=pltpu.SideEffectType.DATAFLOW_SIDE_EFFECTING))


def _scatter_copies(parts, landed, send_sem, recv_sem):
    x, y, c = _place()
    return [_remote(parts[t].at[2 * px + py], landed[t].at[k], send_sem(k, t), recv_sem(k, t), (px, py, c))
            for k, (px, py) in enumerate(_other_chips(x, y)) for t in range(len(parts))]


def scatter_start(partials, name):
    return split_start(_scatter_copies, partials, [(OTHER_CHIPS,) + p.shape[1:] for p in partials], name)


def chip_sums(grads, got):
    names = list(grads)
    partials = []
    for name, theirs in zip(names, got):
        n_slot, hr, width = theirs.shape
        tb = _row_block(hr, width, 6)
        per_half = hr // tb
        mine = lambda i, s, per_half=per_half: (i // per_half) * 2 * per_half + s[0] * per_half + i % per_half
        p = placed_map(
            jnp.add,
            [(grads[name].reshape(2 * n_slot * hr, width), mine), (theirs.reshape(n_slot * hr, width), lambda i, s: i)],
            (n_slot * hr, width, BF16, lambda i, s: i), n_blocks=n_slot * per_half, tb=tb, name="chip_sum_" + name)
        partials.append(p.reshape(theirs.shape))
    return partials


def owner_sums(grads, got, landed):
    names = list(grads)
    blocks = []
    for name, theirs, arrived in zip(names, got, landed):
        n_slot, hr, width = theirs.shape
        tb = _row_block(hr, width, 6)
        per_half = hr // tb
        views = [(grads[name].reshape(2 * n_slot * hr, width),
                  lambda i, s, per_half=per_half: s[1] * 2 * per_half + s[0] * per_half + i),
                 (theirs.reshape(n_slot * hr, width), lambda i, s, per_half=per_half: s[1] * per_half + i)]
        views += [(arrived.reshape(3 * hr, width), functools.partial(lambda k, per_half, i, s: k * per_half + i,
                                                                     k, per_half)) for k in range(3)]
        f = lambda a, b, l0, l1, l2: (((a + b) + l0.astype(F32)) + l1.astype(F32)) + l2.astype(F32)
        blocks.append(placed_map(
            f, views,(2 * hr, width, F32, lambda i, s, per_half=per_half: s[0] * per_half + i),
            n_blocks=per_half, tb=tb, name="owner_sum_" + name))
    return dict(zip(names, join_halves(blocks)))


def adamw_block(name, w, g, m, v):
    rows, width = w.shape
    return rowmap(_adamw, [w, g, m, v], [], [(width, F32)] * 3, tb=_row_block(rows, width, 7),
                  name="adamw_" + name)


def reduce_small(vec, w, m, v):
    n_dev = 8

    def body(vec_ref, w_ref, m_ref, v_ref, loss_ref, g_ref, d_ref, m2_ref, v2_ref, slots, send_sems, recv_sems):
        x, y, c = _place()
        me = 4 * x + 2 * y + c
        slots[me] = vec_ref[...]
        flips = [(fx, fy, fc) for fx in (0, 1) for fy in (0, 1) for fc in (0, 1)][1:]
        peers = [(1 - x if fx else x, 1 - y if fy else y, 1 - c if fc else c) for fx, fy, fc in flips]
        sends = [pltpu.make_async_remote_copy(
            src_ref=vec_ref, dst_ref=slots.at[me], send_sem=send_sems.at[j], recv_sem=recv_sems.at[j],
            device_id=peer, device_id_type=MESH) for j, peer in enumerate(peers)]
        for cp in sends:
            cp.start()
        for j, (px, py, pc) in enumerate(peers):
            pltpu.make_async_remote_copy(
                src_ref=vec_ref, dst_ref=slots.at[4 * px + 2 * py + pc], send_sem=send_sems.at[j],
                recv_sem=recv_sems.at[j], device_id=(px, py, pc), device_id_type=MESH).wait_recv()
        for cp in sends:
            cp.wait_send()
        g = slots[0]
        for d in range(1, n_dev):
            g = g + slots[d]
        loss_ref[...] = jnp.sum(g[:, :D_MODEL], axis=1, keepdims=True)
        delta, m2, v2 = _adamw(w_ref[...], g, m_ref[...], v_ref[...])
        g_ref[...], d_ref[...], m2_ref[...], v2_ref[...] = g, delta, m2, v2

    vm = pl.BlockSpec(memory_space=pltpu.VMEM)
    vec_t = jax.ShapeDtypeStruct(vec.shape, F32)
    return pl.pallas_call(
        body, name="reduce_small", in_specs=[vm] * 4, out_specs=[vm] * 5,
        out_shape=[jax.ShapeDtypeStruct((1, 1), F32)] + [vec_t] * 4,
        scratch_shapes=[pltpu.VMEM((n_dev,) + vec.shape, F32), pltpu.SemaphoreType.DMA((n_dev - 1,)),
                        pltpu.SemaphoreType.DMA((n_dev - 1,))],
        compiler_params=pltpu.CompilerParams(has_side_effects=True),
    )(vec, w, m, v)


def kernel(x, ffn1_norm, ffn1_w_in, ffn1_w_out, mix_norm, w_in, b_gate, rwkv_mu, rwkv_w0, rwkv_w2, rwkv_a0, rwkv_a2, rwkv_g2, rwkv_k_k, rwkv_k_a, rwkv_r_k, rwkv_ln_w, rwkv_ln_b, attn_q_norm, attn_k_norm, w_proj_rwkv, w_proj_attn, w_out, ffn2_norm, ffn2_w_in, ffn2_w_out, loss_target, m_ffn1_norm, m_ffn1_w_in, m_ffn1_w_out, m_mix_norm, m_w_in, m_b_gate, m_rwkv_mu, m_rwkv_w0, m_rwkv_w2, m_rwkv_a0, m_rwkv_a2, m_rwkv_g2, m_rwkv_k_k, m_rwkv_k_a, m_rwkv_r_k, m_rwkv_ln_w, m_rwkv_ln_b, m_attn_q_norm, m_attn_k_norm, m_w_proj_rwkv, m_w_proj_attn, m_w_out, m_ffn2_norm, m_ffn2_w_in, m_ffn2_w_out, v_ffn1_norm, v_ffn1_w_in, v_ffn1_w_out, v_mix_norm, v_w_in, v_b_gate, v_rwkv_mu, v_rwkv_w0, v_rwkv_w2, v_rwkv_a0, v_rwkv_a2, v_rwkv_g2, v_rwkv_k_k, v_rwkv_k_a, v_rwkv_r_k, v_rwkv_ln_w, v_rwkv_ln_b, v_attn_q_norm, v_attn_k_norm, v_w_proj_rwkv, v_w_proj_attn, v_w_out, v_ffn2_norm, v_ffn2_w_in, v_ffn2_w_out):
    given = dict(locals())
    weights = {n: given[n] for n in WEIGHT_ORDER}
    mom_m = {n: given["m_" + n] for n in WEIGHT_ORDER}
    mom_v = {n: given["v_" + n] for n in WEIGHT_ORDER}
    big = [name for name, _, _ in BIG]
    shapes = {n: weights[n].shape for n in WEIGHT_ORDER}
    blocks_of = lambda d: local_blocks({n: d[n][0] for n in big})
    w_blk, m_blk, v_blk = blocks_of(weights), blocks_of(mom_m), blocks_of(mom_v)
    names = list(w_blk)

    early = [n for n in names if n not in FIRST_FFN]
    bf16_block = lambda n: w_blk[n].astype(BF16)
    W = {n: blocks_to_full(n, g) for n, g in zip(FIRST_FFN, gather_weights([bf16_block(n) for n in FIRST_FFN]))}
    stages = {"mixer": [n for n in early if n in MIXER_IN], "out": [n for n in early if n not in MIXER_IN]}
    stage_blocks = {s: [bf16_block(n) for n in stages[s]] for s in stages}
    started = {s: gather_start(stage_blocks[s], "gather_start_" + s) for s in ("mixer", "out")}
    start_token = started["mixer"][1] + started["out"][1]

    def more_weights(stage, after):
        landed = split_wait(started[stage][0], after, "gather_wait_" + stage)
        got = pass_halves(landed, stage_blocks[stage], "pass_halves_" + stage)
        more = {n: blocks_to_full(n, g) for n, g in zip(stages[stage], got)}
        if "lora" in more:
            more.update(split_lora(more.pop("lora")))
        return more

    P = {n: weights[n].reshape(1, -1) for n, _ in SMALL}

    sent = {}

    def send_early(gw):
        lora = jnp.concatenate([gw[n] for n in LORA_PARTS], axis=0)
        sent["grads"] = {n: full_to_blocks(n, lora if n == "lora" else gw[n]) for n in early}
        sent["got"] = swap_halves(list(sent["grads"].values()))
        sent["handles"], token = scatter_start(chip_sums(sent["grads"], sent["got"]), "scatter_start")
        return token

    def send_late(dw_in, dw_out):
        sent["late"] = {n: full_to_blocks(n, g) for n, g in zip(FIRST_FFN, (dw_in, dw_out))}
        sent["late_got"] = swap_halves(list(sent["late"].values()))
        sent["late_handles"], token = scatter_start(chip_sums(sent["late"], sent["late_got"]), "scatter_start_ffn1")
        return token

    loss_cols, dx, gW, gP = layer_step(x[0], loss_target[0], W, P, start_token, more_weights, send_early, send_late)
    landed = split_wait(sent["handles"], gP["ffn1_norm"], "scatter_wait")
    out_g, out_d, out_m, out_v = {}, {}, {}, {}

    def apply(g_blk):
        for n in g_blk:
            res = (g_blk[n], *adamw_block(n, w_blk[n], g_blk[n], m_blk[n], v_blk[n]))
            for dst, t in zip((out_g, out_d, out_m, out_v), res):
                for part, val in (split_lora(t) if n == "lora" else {n: t}).items():
                    dst[part] = val.reshape(shapes[part])

    apply(owner_sums(sent["grads"], sent["got"], landed))
    late_landed = split_wait(sent["late_handles"], list(out_d.values())[-1], "scatter_wait_ffn1")
    apply(owner_sums(sent["late"], sent["late_got"], late_landed))

    zero_head = jnp.zeros((1, D_MODEL), F32)
    vec = pack_small(gP, loss_cols)
    loss, g_s, d_s, m_s, v_s = reduce_small(
        vec, pack_small({n: weights[n] for n, _ in SMALL}, zero_head),
        pack_small({n: mom_m[n] for n, _ in SMALL}, zero_head),
        pack_small({n: mom_v[n] for n, _ in SMALL}, zero_head))
    for dst, src in ((out_g, g_s), (out_d, d_s), (out_m, m_s), (out_v, v_s)):
        dst.update(unpack_small(src, shapes))

    return (loss[0, 0], dx[None], *[out_g[n] for n in WEIGHT_ORDER], *[out_d[n] for n in WEIGHT_ORDER],
            *[out_m[n] for n in WEIGHT_ORDER], *[out_v[n] for n in WEIGHT_ORDER])
```
